```python
import math
import jax
import jax.numpy as jnp
from jax import lax
import numpy as np


D_MODEL = 2048
BATCH = 8
SEQ = 8192
DEPTH = 1

ATTN_HEADS = 16
ATTN_HEAD_DIM = D_MODEL // ATTN_HEADS
ATTN_WIDTH = ATTN_HEADS * ATTN_HEAD_DIM
DILATED_PATTERNS = ((128, 1), (512, 4), (2048, 16))
ATTN_BLOCK = 128

SSM_EXPAND = 2
SSM_INNER = SSM_EXPAND * D_MODEL
SSM_HEAD_DIM = 64
SSM_HEADS = SSM_INNER // SSM_HEAD_DIM
SSM_GROUPS = 8
SSM_STATE = 128
SSM_CONV = 4
SSM_CHUNK = 128
SSM_CONV_DIM = SSM_INNER + 2 * SSM_GROUPS * SSM_STATE

RMS_EPS = 1e-6
IN_SIZES = (ATTN_WIDTH, ATTN_WIDTH, ATTN_WIDTH, ATTN_WIDTH,
            SSM_INNER, SSM_CONV_DIM, SSM_HEADS, D_MODEL, D_MODEL)
N_IN = 4 * ATTN_WIDTH + SSM_INNER + SSM_CONV_DIM + SSM_HEADS + 2 * D_MODEL

kernel_name = 'hybrid_dilated_attn_ssd_block'


def rms_norm(x, w):
    xf = x.astype(jnp.float32)
    y = xf * lax.rsqrt(jnp.mean(xf * xf, axis=-1, keepdims=True) + RMS_EPS)
    return (y * w.astype(jnp.float32)).astype(x.dtype)


def alibi_slopes(n_heads):
    return jnp.asarray([2.0 ** (-8.0 * (h + 1) / n_heads) for h in range(n_heads)], jnp.float32)


def dilated_window_attention(q, k, v, window, dilation, slopes):
    b, s, h, e = q.shape
    sub_len = s // dilation
    span = window // dilation
    blk = ATTN_BLOCK
    nb = -(-sub_len // blk)
    padded = nb * blk

    def to_sub(t):
        t = t.reshape(b, sub_len, dilation, h, e).transpose(0, 2, 3, 1, 4)
        t = jnp.pad(t, ((0, 0), (0, 0), (0, 0), (0, padded - sub_len), (0, 0)))
        return t.reshape(b, dilation, h, nb, blk, e)

    def with_prev(t):
        prev = jnp.pad(t[:, :, :, :-1], ((0, 0), (0, 0), (0, 0), (1, 0), (0, 0), (0, 0)))
        return jnp.concatenate([prev, t], axis=4)

    qb = to_sub(q)
    kc = with_prev(to_sub(k))
    vc = with_prev(to_sub(v))
    scores = jnp.einsum('brhiqe,brhike->brhiqk', qb, kc).astype(jnp.float32) * (e ** -0.5)

    qi = jnp.arange(blk)[:, None]
    ki = jnp.arange(2 * blk)[None, :]
    dist = qi - ki + blk
    key_idx = jnp.arange(nb)[:, None, None] * blk - blk + ki
    valid = (dist >= 0) & (dist <= span) & (key_idx >= 0)
    alibi = -slopes[:, None, None, None] * (dist * dilation).astype(jnp.float32)
    scores = jnp.where(valid, scores + alibi, -jnp.inf)
    lse = jax.nn.logsumexp(scores, axis=-1)
    p = jnp.exp(scores - lse[..., None])
    o = jnp.einsum('brhiqk,brhike->brhiqe', p.astype(v.dtype), vc)

    o = o.reshape(b, dilation, h, padded, e)[:, :, :, :sub_len]
    o = o.transpose(0, 3, 1, 2, 4).reshape(b, s, h, e)
    lse = lse.reshape(b, dilation, h, padded)[:, :, :, :sub_len]
    lse = lse.transpose(0, 3, 1, 2).reshape(b, s, h)
    return o, lse


def causal_depthwise_conv(x, w, bias):
    c = x.shape[-1]
    y = lax.conv_general_dilated(x, w[:, None, :].astype(x.dtype), window_strides=(1,),
                                 padding=[(SSM_CONV - 1, 0)],
                                 dimension_numbers=('NWC', 'WIO', 'NWC'),
                                 feature_group_count=c)
    return y + bias


def ssd_chunked_scan(xs, dt, a, bm, cm):
    b, s, g, j, p = xs.shape
    n = bm.shape[-1]
    L = SSM_CHUNK
    nc = s // L
    xdt = xs.astype(jnp.float32) * dt[..., None]
    da = dt * a

    def chunks(t):
        return jnp.moveaxis(t.reshape(b, nc, L, *t.shape[2:]), 1, 0)

    causal = jnp.tril(jnp.ones((L, L), dtype=bool))

    def step(state, inp):
        xc, dac, bc, cc = inp
        acum = jnp.cumsum(dac, axis=1)
        acum_t = jnp.moveaxis(acum, 1, -1)
        seg = acum_t[..., :, None] - acum_t[..., None, :]
        decay = jnp.exp(jnp.where(causal, seg, -jnp.inf))
        cb = jnp.einsum('blgn,bsgn->bgls', cc, bc)
        y_diag = jnp.einsum('bgls,bgjls,bsgjp->blgjp', cb, decay, xc)
        y_off = jnp.einsum('blgn,bgjpn,blgj->blgjp', cc, state, jnp.exp(acum))
        last = acum[:, -1]
        w_s = jnp.exp(last[:, None] - acum)
        new_state = state * jnp.exp(last)[..., None, None] + \
            jnp.einsum('blgj,blgjp,blgn->bgjpn', w_s, xc, bc)
        return new_state, y_diag + y_off

    state0 = jnp.zeros((b, g, j, p, n), jnp.float32)
    _, ys = lax.scan(step, state0, (chunks(xdt), chunks(da),
                                     chunks(bm.astype(jnp.float32)), chunks(cm.astype(jnp.float32))))
    return jnp.moveaxis(ys, 0, 1).reshape(b, s, g, j, p)


def hybrid_layer(x, norm_w, w_in, conv_w, conv_b, dt_bias, a_log, d_skip, ssm_norm_w,
                 w_attn_branch, w_ssm_branch, w_out):
    b, s, _ = x.shape
    hpg = SSM_HEADS // SSM_GROUPS
    hn = rms_norm(x, norm_w)
    proj = hn @ w_in
    split_points = []
    acc = 0
    for size in IN_SIZES[:-1]:
        acc += size
        split_points.append(acc)
    q, k, v, z_a, z_s, xbc, dt_raw, g_a, g_s = jnp.split(proj, split_points, axis=-1)

    q = q.reshape(b, s, ATTN_HEADS, ATTN_HEAD_DIM)
    k = k.reshape(b, s, ATTN_HEADS, ATTN_HEAD_DIM)
    v = v.reshape(b, s, ATTN_HEADS, ATTN_HEAD_DIM)
    slopes = alibi_slopes(ATTN_HEADS)
    outs = []
    lses = []
    for window, dilation in DILATED_PATTERNS:
        o, l = dilated_window_attention(q, k, v, window, dilation, slopes)
        outs.append(o)
        lses.append(l)
    wts = jax.nn.softmax(jnp.stack(lses), axis=0)
    o_a = jnp.einsum('pbsh,pbshe->bshe', wts.astype(q.dtype), jnp.stack(outs))
    o_a = o_a.reshape(b, s, ATTN_WIDTH) * jax.nn.silu(z_a)

    xbc = jax.nn.silu(causal_depthwise_conv(xbc, conv_w, conv_b))
    xs, bm, cm = jnp.split(xbc, [SSM_INNER, SSM_INNER + SSM_GROUPS * SSM_STATE], axis=-1)
    xs = xs.reshape(b, s, SSM_GROUPS, hpg, SSM_HEAD_DIM)
    bm = bm.reshape(b, s, SSM_GROUPS, SSM_STATE)
    cm = cm.reshape(b, s, SSM_GROUPS, SSM_STATE)
    dt = jax.nn.softplus(dt_raw.astype(jnp.float32) + dt_bias.astype(jnp.float32))
    dt = dt.reshape(b, s, SSM_GROUPS, hpg)
    a = -jnp.exp(a_log.astype(jnp.float32)).reshape(SSM_GROUPS, hpg)
    y = ssd_chunked_scan(xs, dt, a, bm, cm)
    y = y + d_skip.astype(jnp.float32).reshape(SSM_GROUPS, hpg)[..., None] * xs.astype(jnp.float32)
    y = y.reshape(b, s, SSM_INNER).astype(x.dtype)
    y = rms_norm(y * jax.nn.silu(z_s), ssm_norm_w)

    merged = jax.nn.sigmoid(g_a) * (o_a @ w_attn_branch) + jax.nn.sigmoid(g_s) * (y @ w_ssm_branch)
    return x + merged @ w_out


def _fwd_setup_inputs(seed: int = 0) -> dict:
    key = jax.random.key(seed)
    ks = jax.random.split(key, 16)
    f32 = jnp.float32

    def dense(k, fan_in, fan_out):
        return jax.random.normal(k, (DEPTH, fan_in, fan_out), f32) * fan_in ** -0.5

    def gain(k, n):
        return 1.0 + 0.02 * jax.random.normal(k, (DEPTH, n), f32)

    x = jax.random.normal(ks[0], (BATCH, SEQ, D_MODEL), f32)
    norm_w = gain(ks[1], D_MODEL)
    w_in = dense(ks[2], D_MODEL, N_IN)
    conv_w = jax.random.normal(ks[3], (DEPTH, SSM_CONV, SSM_CONV_DIM), f32) * SSM_CONV ** -0.5
    conv_b = 0.01 * jax.random.normal(ks[4], (DEPTH, SSM_CONV_DIM), f32)
    u = jax.random.uniform(ks[5], (DEPTH, SSM_HEADS), f32)
    dt0 = jnp.exp(u * (math.log(0.1) - math.log(0.001)) + math.log(0.001))
    dt_bias = dt0 + jnp.log(-jnp.expm1(-dt0))
    a_log = jnp.log(jax.random.uniform(ks[6], (DEPTH, SSM_HEADS), f32, 1.0, 16.0))
    d_skip = gain(ks[7], SSM_HEADS)
    ssm_norm_w = gain(ks[8], SSM_INNER)
    w_attn_branch = dense(ks[9], ATTN_WIDTH, D_MODEL)
    w_ssm_branch = dense(ks[10], SSM_INNER, D_MODEL)
    w_out = dense(ks[11], D_MODEL, D_MODEL)
    final_norm_w = 1.0 + 0.02 * jax.random.normal(ks[12], (D_MODEL,), f32)
    return {'x': x, 'norm_w': norm_w, 'w_in': w_in, 'conv_w': conv_w, 'conv_b': conv_b,
            'dt_bias': dt_bias, 'a_log': a_log, 'd_skip': d_skip, 'ssm_norm_w': ssm_norm_w,
            'w_attn_branch': w_attn_branch, 'w_ssm_branch': w_ssm_branch, 'w_out': w_out,
            'final_norm_w': final_norm_w}


def _fwd_reference(x, norm_w, w_in, conv_w, conv_b, dt_bias, a_log, d_skip, ssm_norm_w,
              w_attn_branch, w_ssm_branch, w_out, final_norm_w):
    for layer in range(DEPTH):
        x = hybrid_layer(x, norm_w[layer], w_in[layer], conv_w[layer], conv_b[layer],
                         dt_bias[layer], a_log[layer], d_skip[layer], ssm_norm_w[layer],
                         w_attn_branch[layer], w_ssm_branch[layer], w_out[layer])
    return rms_norm(x, final_norm_w)


import jax as _jax
import jax.numpy as _jnp

TWIN_FORMAT = 'train_step'
FWD_PARAMS = ['x', 'norm_w', 'w_in', 'conv_w', 'conv_b', 'dt_bias', 'a_log', 'd_skip', 'ssm_norm_w', 'w_attn_branch', 'w_ssm_branch', 'w_out', 'final_norm_w']
TWIN_WEIGHTS = ['norm_w', 'w_in', 'conv_w', 'conv_b', 'dt_bias', 'a_log', 'd_skip', 'ssm_norm_w', 'w_attn_branch', 'w_ssm_branch', 'w_out', 'final_norm_w']
TWIN_DIFF_INPUT = 'x'
TWIN_INPUTS = ['x', 'norm_w', 'w_in', 'conv_w', 'conv_b', 'dt_bias', 'a_log', 'd_skip', 'ssm_norm_w', 'w_attn_branch', 'w_ssm_branch', 'w_out', 'final_norm_w', 'loss_target', 'm_norm_w', 'm_w_in', 'm_conv_w', 'm_conv_b', 'm_dt_bias', 'm_a_log', 'm_d_skip', 'm_ssm_norm_w', 'm_w_attn_branch', 'm_w_ssm_branch', 'm_w_out', 'm_final_norm_w', 'v_norm_w', 'v_w_in', 'v_conv_w', 'v_conv_b', 'v_dt_bias', 'v_a_log', 'v_d_skip', 'v_ssm_norm_w', 'v_w_attn_branch', 'v_w_ssm_branch', 'v_w_out', 'v_final_norm_w']
TWIN_OUTPUTS = ['loss', 'grad_x', 'grad_norm_w', 'grad_w_in', 'grad_conv_w', 'grad_conv_b', 'grad_dt_bias', 'grad_a_log', 'grad_d_skip', 'grad_ssm_norm_w', 'grad_w_attn_branch', 'grad_w_ssm_branch', 'grad_w_out', 'grad_final_norm_w', 'delta_norm_w', 'delta_w_in', 'delta_conv_w', 'delta_conv_b', 'delta_dt_bias', 'delta_a_log', 'delta_d_skip', 'delta_ssm_norm_w', 'delta_w_attn_branch', 'delta_w_ssm_branch', 'delta_w_out', 'delta_final_norm_w', 'new_m_norm_w', 'new_m_w_in', 'new_m_conv_w', 'new_m_conv_b', 'new_m_dt_bias', 'new_m_a_log', 'new_m_d_skip', 'new_m_ssm_norm_w', 'new_m_w_attn_branch', 'new_m_w_ssm_branch', 'new_m_w_out', 'new_m_final_norm_w', 'new_v_norm_w', 'new_v_w_in', 'new_v_conv_w', 'new_v_conv_b', 'new_v_dt_bias', 'new_v_a_log', 'new_v_d_skip', 'new_v_ssm_norm_w', 'new_v_w_attn_branch', 'new_v_w_ssm_branch', 'new_v_w_out', 'new_v_final_norm_w']
TWIN_LEAF_KINDS = {'loss': 'loss', 'grad_x': 'grad_x', 'grad_norm_w': 'grad_w', 'grad_w_in': 'grad_w', 'grad_conv_w': 'grad_w', 'grad_conv_b': 'grad_w', 'grad_dt_bias': 'grad_w', 'grad_a_log': 'grad_w', 'grad_d_skip': 'grad_w', 'grad_ssm_norm_w': 'grad_w', 'grad_w_attn_branch': 'grad_w', 'grad_w_ssm_branch': 'grad_w', 'grad_w_out': 'grad_w', 'grad_final_norm_w': 'grad_w', 'delta_norm_w': 'delta_w', 'delta_w_in': 'delta_w', 'delta_conv_w': 'delta_w', 'delta_conv_b': 'delta_w', 'delta_dt_bias': 'delta_w', 'delta_a_log': 'delta_w', 'delta_d_skip': 'delta_w', 'delta_ssm_norm_w': 'delta_w', 'delta_w_attn_branch': 'delta_w', 'delta_w_ssm_branch': 'delta_w', 'delta_w_out': 'delta_w', 'delta_final_norm_w': 'delta_w', 'new_m_norm_w': 'new_m', 'new_m_w_in': 'new_m', 'new_m_conv_w': 'new_m', 'new_m_conv_b': 'new_m', 'new_m_dt_bias': 'new_m', 'new_m_a_log': 'new_m', 'new_m_d_skip': 'new_m', 'new_m_ssm_norm_w': 'new_m', 'new_m_w_attn_branch': 'new_m', 'new_m_w_ssm_branch': 'new_m', 'new_m_w_out': 'new_m', 'new_m_final_norm_w': 'new_m', 'new_v_norm_w': 'new_v', 'new_v_w_in': 'new_v', 'new_v_conv_w': 'new_v', 'new_v_conv_b': 'new_v', 'new_v_dt_bias': 'new_v', 'new_v_a_log': 'new_v', 'new_v_d_skip': 'new_v', 'new_v_ssm_norm_w': 'new_v', 'new_v_w_attn_branch': 'new_v', 'new_v_w_ssm_branch': 'new_v', 'new_v_w_out': 'new_v', 'new_v_final_norm_w': 'new_v'}


def _forward(args):
    return _fwd_reference(*[args[k] for k in FWD_PARAMS])


def _output_shape():
    def fwd():
        inp = _fwd_setup_inputs(0)
        return _fwd_reference(*[inp[k] for k in FWD_PARAMS])
    out = _jax.eval_shape(fwd)
    return out.shape, out.dtype

N_MICROBATCH = 1
ADAM_LR = 0.001
ADAM_B1 = 0.9
ADAM_B2 = 0.999
ADAM_EPS = 1e-08
ADAM_WD = 0.01
ADAM_STEP = 10
PER_EXAMPLE_BATCH_AXIS = {'x': 0, 'loss_target': 0}
SHARED_INPUTS = []
_WEIGHT_DTYPES = {'norm_w': _jnp.float32, 'w_in': _jnp.float32, 'conv_w': _jnp.float32, 'conv_b': _jnp.float32, 'dt_bias': _jnp.float32, 'a_log': _jnp.float32, 'd_skip': _jnp.float32, 'ssm_norm_w': _jnp.float32, 'w_attn_branch': _jnp.float32, 'w_ssm_branch': _jnp.float32, 'w_out': _jnp.float32, 'final_norm_w': _jnp.float32}
MOMENT_SCALE = {'norm_w': 9.861103e-02, 'w_in': 2.941098e-02, 'conv_w': 3.766198e-02, 'conv_b': 5.025355e-02, 'dt_bias': 1.188415e-01, 'a_log': 1.524285e-01, 'd_skip': 2.210832e-01, 'ssm_norm_w': 4.274261e-02, 'w_attn_branch': 1.446352e-02, 'w_ssm_branch': 5.970021e-02, 'w_out': 6.119059e-02, 'final_norm_w': 3.200352e+01}


def _to_microbatches(a, axis):
    t = _jnp.moveaxis(a, axis, 0)
    t = t.reshape((N_MICROBATCH, t.shape[0] // N_MICROBATCH) + t.shape[1:])
    return _jnp.moveaxis(t, 1, axis + 1)


def setup_inputs(seed: int = 0) -> dict:
    inp = _fwd_setup_inputs(seed)
    key = _jax.random.fold_in(_jax.random.key(seed), 7919)
    shape, _ = _output_shape()
    out = dict(inp)
    out["loss_target"] = _jax.random.normal(_jax.random.fold_in(key, 0), shape, _jnp.float32)
    for i, name in enumerate(TWIN_WEIGHTS):
        w = inp[name].astype(_jnp.float32)
        if MOMENT_SCALE is None:
            s = _jnp.sqrt(_jnp.mean(_jnp.square(w)) + 1e-30)
        else:
            s = MOMENT_SCALE[name]
        km, kv = _jax.random.split(_jax.random.fold_in(key, i + 1))
        out[name] = w
        out["m_" + name] = s * _jax.random.normal(km, w.shape, _jnp.float32)
        out["v_" + name] = (s * s) * _jax.random.uniform(kv, w.shape, _jnp.float32, 0.5, 1.5)
    if N_MICROBATCH > 1:
        for name, axis in PER_EXAMPLE_BATCH_AXIS.items():
            out[name] = _to_microbatches(out[name], axis)
    return {'x': out['x'], 'norm_w': out['norm_w'], 'w_in': out['w_in'], 'conv_w': out['conv_w'], 'conv_b': out['conv_b'], 'dt_bias': out['dt_bias'], 'a_log': out['a_log'], 'd_skip': out['d_skip'], 'ssm_norm_w': out['ssm_norm_w'], 'w_attn_branch': out['w_attn_branch'], 'w_ssm_branch': out['w_ssm_branch'], 'w_out': out['w_out'], 'final_norm_w': out['final_norm_w'], 'loss_target': out['loss_target'], 'm_norm_w': out['m_norm_w'], 'm_w_in': out['m_w_in'], 'm_conv_w': out['m_conv_w'], 'm_conv_b': out['m_conv_b'], 'm_dt_bias': out['m_dt_bias'], 'm_a_log': out['m_a_log'], 'm_d_skip': out['m_d_skip'], 'm_ssm_norm_w': out['m_ssm_norm_w'], 'm_w_attn_branch': out['m_w_attn_branch'], 'm_w_ssm_branch': out['m_w_ssm_branch'], 'm_w_out': out['m_w_out'], 'm_final_norm_w': out['m_final_norm_w'], 'v_norm_w': out['v_norm_w'], 'v_w_in': out['v_w_in'], 'v_conv_w': out['v_conv_w'], 'v_conv_b': out['v_conv_b'], 'v_dt_bias': out['v_dt_bias'], 'v_a_log': out['v_a_log'], 'v_d_skip': out['v_d_skip'], 'v_ssm_norm_w': out['v_ssm_norm_w'], 'v_w_attn_branch': out['v_w_attn_branch'], 'v_w_ssm_branch': out['v_w_ssm_branch'], 'v_w_out': out['v_w_out'], 'v_final_norm_w': out['v_final_norm_w']}


def _loss(weights, diff, rest, loss_target):
    with _jax.named_scope("forward"):
        args = {**rest, TWIN_DIFF_INPUT: diff, **{k: w.astype(_WEIGHT_DTYPES[k]) for k, w in weights.items()}}
        y = _forward(args)
    with _jax.named_scope("loss_head"):
        err = _jnp.square(y.astype(_jnp.float32) - loss_target)
        return 0.5 * _jnp.sum(_jnp.mean(err, axis=-1)) if err.ndim else 0.5 * err


def _adamw(w, g, m, v):
    m = ADAM_B1 * m + (1.0 - ADAM_B1) * g
    v = ADAM_B2 * v + (1.0 - ADAM_B2) * _jnp.square(g)
    m_hat = m / (1.0 - ADAM_B1 ** ADAM_STEP)
    v_hat = v / (1.0 - ADAM_B2 ** ADAM_STEP)
    delta = -ADAM_LR * (m_hat / (_jnp.sqrt(v_hat) + ADAM_EPS) + ADAM_WD * w)
    return delta, m, v


def reference(x, norm_w, w_in, conv_w, conv_b, dt_bias, a_log, d_skip, ssm_norm_w, w_attn_branch, w_ssm_branch, w_out, final_norm_w, loss_target, m_norm_w, m_w_in, m_conv_w, m_conv_b, m_dt_bias, m_a_log, m_d_skip, m_ssm_norm_w, m_w_attn_branch, m_w_ssm_branch, m_w_out, m_final_norm_w, v_norm_w, v_w_in, v_conv_w, v_conv_b, v_dt_bias, v_a_log, v_d_skip, v_ssm_norm_w, v_w_attn_branch, v_w_ssm_branch, v_w_out, v_final_norm_w):
    given = dict(x=x, norm_w=norm_w, w_in=w_in, conv_w=conv_w, conv_b=conv_b, dt_bias=dt_bias, a_log=a_log, d_skip=d_skip, ssm_norm_w=ssm_norm_w, w_attn_branch=w_attn_branch, w_ssm_branch=w_ssm_branch, w_out=w_out, final_norm_w=final_norm_w, loss_target=loss_target, m_norm_w=m_norm_w, m_w_in=m_w_in, m_conv_w=m_conv_w, m_conv_b=m_conv_b, m_dt_bias=m_dt_bias, m_a_log=m_a_log, m_d_skip=m_d_skip, m_ssm_norm_w=m_ssm_norm_w, m_w_attn_branch=m_w_attn_branch, m_w_ssm_branch=m_w_ssm_branch, m_w_out=m_w_out, m_final_norm_w=m_final_norm_w, v_norm_w=v_norm_w, v_w_in=v_w_in, v_conv_w=v_conv_w, v_conv_b=v_conv_b, v_dt_bias=v_dt_bias, v_a_log=v_a_log, v_d_skip=v_d_skip, v_ssm_norm_w=v_ssm_norm_w, v_w_attn_branch=v_w_attn_branch, v_w_ssm_branch=v_w_ssm_branch, v_w_out=v_w_out, v_final_norm_w=v_final_norm_w)
    weights = {n: given[n] for n in TWIN_WEIGHTS}
    shared = {n: given[n] for n in SHARED_INPUTS}
    per_example = {n: given[n] for n in ['x']}
    grad_fn = _jax.value_and_grad(_loss, argnums=(0, 1))

    def one_microbatch(ex, loss_target):
        ex = dict(ex)
        diff = ex.pop(TWIN_DIFF_INPUT)
        return grad_fn(weights, diff, {**shared, **ex}, loss_target)

    if N_MICROBATCH == 1:
        loss, (grad_w, grad_x) = one_microbatch(per_example, given["loss_target"])
    else:
        def body(carry, xs):
            loss_sum, grad_sum = carry
            l_k, (gw_k, gx_k) = one_microbatch(xs[0], xs[1])
            with _jax.named_scope("update"):
                return (loss_sum + l_k, _jax.tree.map(_jnp.add, grad_sum, gw_k)), gx_k

        init = (_jnp.zeros((), _jnp.float32), _jax.tree.map(_jnp.zeros_like, weights))
        (loss, grad_w), grad_x = _jax.lax.scan(body, init, (per_example, given["loss_target"]))
    with _jax.named_scope("update"):
        delta_w, new_m, new_v = {}, {}, {}
        for n in TWIN_WEIGHTS:
            delta_w[n], new_m[n], new_v[n] = _adamw(weights[n], grad_w[n], given["m_" + n], given["v_" + n])
    return (loss, grad_x, *[grad_w[n] for n in TWIN_WEIGHTS], *[delta_w[n] for n in TWIN_WEIGHTS],
            *[new_m[n] for n in TWIN_WEIGHTS], *[new_v[n] for n in TWIN_WEIGHTS])
```

```python
import functools
import math

import jax
import jax.numpy as jnp
from jax import lax
from jax.experimental import pallas as pl
from jax.experimental.pallas import tpu as pltpu

F32 = jnp.float32
BF16 = jnp.bfloat16
SDS = jax.ShapeDtypeStruct

RMS_EPS = 1e-6
LANES = 128
CHUNK = 128
SSM_HEAD_DIM = 64
SSM_GROUPS = 8
SSM_STATE = 128
CONV_K = 4
ATTN_HEAD_DIM = 128
DILATED_PATTERNS = ((128, 1), (512, 4), (2048, 16))
ATTN_WINDOW = max(w for w, _ in DILATED_PATTERNS)
NEG = -1e30
VMEM_LIMIT = 56 * 1024 * 1024
ADAM_LR, ADAM_B1, ADAM_B2, ADAM_EPS, ADAM_WD, ADAM_STEP = 0.001, 0.9, 0.999, 1e-08, 0.01, 10
MESH = pl.DeviceIdType.MESH
N_CHIPS = 4
N_DEV = 8


class _Cfg:
    def __init__(self, s, d):
        self.S, self.D = s, d
        self.H = d // ATTN_HEAD_DIM
        self.SI = 2 * d
        self.NH = self.SI // SSM_HEAD_DIM
        self.HPG = self.NH // SSM_GROUPS
        self.GW = self.HPG * SSM_HEAD_DIM
        self.BC = SSM_GROUPS * SSM_STATE
        self.CD = self.SI + 2 * self.BC
        self.OQ, self.OK, self.OV, self.OZA = 0, d, 2 * d, 3 * d
        self.OZS = 4 * d
        self.OXBC = self.OZS + self.SI
        self.OGA = self.OXBC + self.CD
        self.OGS = self.OGA + d
        self.NM = self.OGS + d
        self.N_IN = self.NM + self.NH
        assert self.GW % LANES == 0 and self.NH <= LANES and s % 512 == 0 and d % 512 == 0


def _params(sem=None):
    return pltpu.CompilerParams(dimension_semantics=sem, vmem_limit_bytes=VMEM_LIMIT)


def _sigmoid(x):
    return 1.0 / (1.0 + jnp.exp(-x))


def _softplus(x):
    u = jnp.exp(-jnp.abs(x))
    l1p = jnp.where(u < 1e-3, u * (1.0 - u * (0.5 - u * (1.0 / 3.0))), jnp.log(1.0 + u))
    return jnp.maximum(x, 0.0) + l1p


def _nt(a, b):
    return lax.dot_general(a, b, (((1,), (1,)), ((), ())), preferred_element_type=F32)


def _tn(a, b):
    return lax.dot_general(a, b, (((0,), (0,)), ((), ())), preferred_element_type=F32)


def _nn(a, b):
    return jnp.dot(a, b, preferred_element_type=F32)


def _tile(n, target):
    if n <= target:
        return n
    best = None
    for t in range(LANES, target + 1, LANES):
        if n % t == 0:
            best = t
    assert best is not None, (n, target)
    return best


def _mm(a, b, dims, out_dtype, name, tm=1024, tn=2048, tk=512, init=None):
    if dims == "nn":
        (m, k), (k2, n) = a.shape, b.shape
    elif dims == "nt":
        (m, k), (n, k2) = a.shape, b.shape
    else:
        (k, m), (k2, n) = a.shape, b.shape
    assert k == k2
    tm, tn, tk = _tile(m, tm), _tile(n, tn), _tile(k, tk)
    nk = k // tk
    if dims == "tn":
        a_spec = pl.BlockSpec((tk, tm), lambda i, j, kk: (kk, i))
    else:
        a_spec = pl.BlockSpec((tm, tk), lambda i, j, kk: (i, kk))
    if dims == "nt":
        b_spec = pl.BlockSpec((tn, tk), lambda i, j, kk: (j, kk))
    else:
        b_spec = pl.BlockSpec((tk, tn), lambda i, j, kk: (kk, j))
    o_spec = pl.BlockSpec((tm, tn), lambda i, j, kk: (i, j))
    op = {"nn": _nn, "nt": _nt, "tn": _tn}[dims]
    has_init = init is not None

    def body(*refs):
        if has_init:
            a_ref, b_ref, i_ref, o_ref, acc = refs
        else:
            a_ref, b_ref, o_ref, acc = refs
        kk = pl.program_id(2)

        @pl.when(kk == 0)
        def _():
            acc[...] = i_ref[...].astype(F32) if has_init else jnp.zeros_like(acc)

        acc[...] += op(a_ref[...], b_ref[...])

        @pl.when(kk == nk - 1)
        def _():
            o_ref[...] = acc[...].astype(out_dtype)

    in_specs = [a_spec, b_spec] + ([o_spec] if has_init else [])
    args = (a, b) + ((init,) if has_init else ())
    return pl.pallas_call(
        body, out_shape=SDS((m, n), out_dtype), grid=(m // tm, n // tn, nk),
        in_specs=in_specs, out_specs=o_spec, scratch_shapes=[pltpu.VMEM((tm, tn), F32)],
        compiler_params=_params(("parallel", "parallel", "arbitrary")), name=name)(*args)


def _rmsnorm_fwd(x, w):
    s, d = x.shape
    tr = 256

    def body(x_ref, w_ref, o_ref):
        xv = x_ref[...]
        r = lax.rsqrt(jnp.mean(xv * xv, axis=-1, keepdims=True) + RMS_EPS)
        o_ref[...] = (xv * r * w_ref[...]).astype(BF16)

    return pl.pallas_call(
        body, out_shape=SDS((s, d), BF16), grid=(s // tr,),
        in_specs=[pl.BlockSpec((tr, d), lambda i: (i, 0)), pl.BlockSpec((1, d), lambda i: (0, 0))],
        out_specs=pl.BlockSpec((tr, d), lambda i: (i, 0)),
        compiler_params=_params(("parallel",)), name="rmsnorm_fwd")(x, w)


def _rmsnorm_bwd(x, w, dhn, dout):
    s, d = x.shape
    tr = 256

    def body(x_ref, w_ref, dh_ref, do_ref, gx_ref, gw_ref):
        xv = x_ref[...]
        r = lax.rsqrt(jnp.mean(xv * xv, axis=-1, keepdims=True) + RMS_EPS)
        nrm = xv * r
        dh = dh_ref[...]
        gy = dh * w_ref[...]
        gx_ref[...] = do_ref[...] + r * (gy - nrm * jnp.mean(gy * nrm, axis=-1, keepdims=True))

        @pl.when(pl.program_id(0) == 0)
        def _():
            gw_ref[...] = jnp.zeros_like(gw_ref)

        gw_ref[...] += jnp.sum(dh * nrm, axis=0, keepdims=True)

    blk = pl.BlockSpec((tr, d), lambda i: (i, 0))
    row = pl.BlockSpec((1, d), lambda i: (0, 0))
    return pl.pallas_call(
        body, out_shape=(SDS((s, d), F32), SDS((1, d), F32)), grid=(s // tr,),
        in_specs=[blk, row, blk, blk], out_specs=(blk, row),
        compiler_params=_params(("arbitrary",)), name="rmsnorm_bwd")(x, w, dhn, dout)


def _attn_tables(tq):
    w = ATTN_WINDOW + tq
    i = jnp.arange(tq, dtype=jnp.int32)[:, None]
    j = jnp.arange(w, dtype=jnp.int32)[None, :]
    delta = i + ATTN_WINDOW - j
    n = jnp.zeros((tq, w), F32)
    for window, dil in DILATED_PATTERNS:
        n = n + ((delta >= 0) & (delta <= window) & (delta % dil == 0)).astype(F32)
    logn = jnp.where(n > 0, jnp.log(jnp.maximum(n, 1.0)), NEG)
    return logn, jnp.maximum(delta, 0).astype(F32)


def _slopes(h):
    s = jnp.asarray([2.0 ** (-8.0 * (i + 1) / h) for i in range(h)], F32)
    return jnp.broadcast_to(s[:, None, None], (h, 1, LANES))


def _attn_scores(q_ref, k_ref, logn_ref, dist_ref, slope_ref, start, w):
    tq = q_ref.shape[0]
    kw = k_ref[pl.ds(start, w), :]
    s = _nt(q_ref[...], kw) * (ATTN_HEAD_DIM ** -0.5)
    slope = slope_ref[0:1, 0:1]
    s = s + (logn_ref[...] - slope * dist_ref[...])
    col = lax.broadcasted_iota(jnp.int32, (tq, w), 1)
    s = jnp.where(col >= ATTN_WINDOW - start, s, NEG)
    m = jnp.max(s, axis=1, keepdims=True)
    p = jnp.exp(s - m)
    l = jnp.sum(p, axis=1, keepdims=True)
    return p, l


def _attn_fwd(cfg, proj, kpad, vpad, logn, dist, slopes):
    s, h = cfg.S, cfg.H
    tq = logn.shape[0]
    w = ATTN_WINDOW + tq
    sp = s + ATTN_WINDOW

    def body(q_ref, z_ref, k_ref, v_ref, logn_ref, dist_ref, slope_ref, o_ref, og_ref):
        start = pl.multiple_of(pl.program_id(1) * tq, tq)
        p, l = _attn_scores(q_ref, k_ref, logn_ref, dist_ref, slope_ref, start, w)
        o = _nn(p.astype(BF16), v_ref[pl.ds(start, w), :]) / l
        z = z_ref[...].astype(F32)
        o_ref[...] = o.astype(BF16)
        og_ref[...] = (o * (z * _sigmoid(z))).astype(BF16)

    qb = cfg.OQ // LANES
    zb = cfg.OZA // LANES
    blk = pl.BlockSpec((tq, LANES), lambda hh, i: (i, hh))
    return pl.pallas_call(
        body, out_shape=(SDS((s, cfg.D), BF16), SDS((s, cfg.D), BF16)), grid=(h, s // tq),
        in_specs=[pl.BlockSpec((tq, LANES), lambda hh, i: (i, qb + hh)),
                  pl.BlockSpec((tq, LANES), lambda hh, i: (i, zb + hh)),
                  pl.BlockSpec((sp, LANES), lambda hh, i: (0, hh)),
                  pl.BlockSpec((sp, LANES), lambda hh, i: (0, hh)),
                  pl.BlockSpec((tq, w), lambda hh, i: (0, 0)),
                  pl.BlockSpec((tq, w), lambda hh, i: (0, 0)),
                  pl.BlockSpec((None, 1, LANES), lambda hh, i: (hh, 0, 0))],
        out_specs=(blk, blk),
        compiler_params=_params(("parallel", "parallel")), name="attn_fwd")(proj, proj, kpad, vpad, logn, dist, slopes)


def _attn_bwd(cfg, proj, kpad, vpad, o_a, doag, logn, dist, slopes):
    s, h = cfg.S, cfg.H
    tq = logn.shape[0]
    w = ATTN_WINDOW + tq
    sp = s + ATTN_WINDOW
    scale = ATTN_HEAD_DIM ** -0.5

    def body(q_ref, z_ref, k_ref, v_ref, o_ref, dg_ref, logn_ref, dist_ref, slope_ref,
             dq_ref, dz_ref, dk_ref, dv_ref):
        i = pl.program_id(1)
        start = pl.multiple_of(i * tq, tq)

        @pl.when(i == 0)
        def _():
            dk_ref[...] = jnp.zeros_like(dk_ref)
            dv_ref[...] = jnp.zeros_like(dv_ref)

        p, l = _attn_scores(q_ref, k_ref, logn_ref, dist_ref, slope_ref, start, w)
        p = p / l
        z = z_ref[...].astype(F32)
        sg = _sigmoid(z)
        o = o_ref[...].astype(F32)
        dg = dg_ref[...].astype(F32)
        do = dg * (z * sg)
        dz_ref[...] = (dg * o * (sg * (1.0 + z * (1.0 - sg)))).astype(BF16)
        delta = jnp.sum(do * o, axis=1, keepdims=True)
        dob = do.astype(BF16)
        dp = _nt(dob, v_ref[pl.ds(start, w), :])
        ds = (p * (dp - delta) * scale).astype(BF16)
        dq_ref[...] = _nn(ds, k_ref[pl.ds(start, w), :]).astype(BF16)
        dk_ref[pl.ds(start, w), :] += _tn(ds, q_ref[...])
        dv_ref[pl.ds(start, w), :] += _tn(p.astype(BF16), dob)

    qb = cfg.OQ // LANES
    zb = cfg.OZA // LANES
    blk = pl.BlockSpec((tq, LANES), lambda hh, i: (i, hh))
    full = pl.BlockSpec((sp, LANES), lambda hh, i: (0, hh))
    tab = pl.BlockSpec((tq, w), lambda hh, i: (0, 0))
    return pl.pallas_call(
        body,
        out_shape=(SDS((s, cfg.D), BF16), SDS((s, cfg.D), BF16), SDS((sp, cfg.D), F32), SDS((sp, cfg.D), F32)),
        grid=(h, s // tq),
        in_specs=[pl.BlockSpec((tq, LANES), lambda hh, i: (i, qb + hh)),
                  pl.BlockSpec((tq, LANES), lambda hh, i: (i, zb + hh)),
                  full, full, blk, blk, tab, tab,
                  pl.BlockSpec((None, 1, LANES), lambda hh, i: (hh, 0, 0))],
        out_specs=(blk, blk, full, full),
        compiler_params=_params(("parallel", "arbitrary")), name="attn_bwd")(
            proj, proj, kpad, vpad, o_a, doag, logn, dist, slopes)


CONV_HALO = 16
CONV_TR = 512
CONV_CW = 512


def _conv_fwd(cfg, proj, conv_w, conv_b):
    s, cd = cfg.S, cfg.CD
    tr, cw, hl = CONV_TR, CONV_CW, CONV_HALO
    cb0 = cfg.OXBC // cw

    def body(x_ref, h_ref, w_ref, b_ref, o_ref, scr):
        i = pl.program_id(0)
        scr[pl.ds(0, hl), :] = jnp.where(i > 0, h_ref[...].astype(F32), 0.0)
        scr[pl.ds(hl, tr), :] = x_ref[...].astype(F32)
        pre = b_ref[...] + jnp.zeros((tr, cw), F32)
        for k in range(CONV_K):
            pre = pre + w_ref[k:k + 1, :] * scr[pl.ds(hl - (CONV_K - 1) + k, tr), :]
        o_ref[...] = (pre * _sigmoid(pre)).astype(BF16)

    return pl.pallas_call(
        body, out_shape=SDS((s, cd), BF16), grid=(s // tr, cd // cw),
        in_specs=[pl.BlockSpec((tr, cw), lambda i, j: (i, cb0 + j)),
                  pl.BlockSpec((hl, cw), lambda i, j: (jnp.maximum(i * (tr // hl) - 1, 0), cb0 + j)),
                  pl.BlockSpec((CONV_K, cw), lambda i, j: (0, j)),
                  pl.BlockSpec((1, cw), lambda i, j: (0, j))],
        out_specs=pl.BlockSpec((tr, cw), lambda i, j: (i, j)),
        scratch_shapes=[pltpu.VMEM((tr + hl, cw), F32)],
        compiler_params=_params(("parallel", "parallel")), name="conv_fwd")(proj, proj, conv_w, conv_b)


def _conv_bwd(cfg, proj, dact, conv_w, conv_b, dproj):
    s, cd = cfg.S, cfg.CD
    tr, cw, hl = CONV_TR, CONV_CW, CONV_HALO
    cb0 = cfg.OXBC // cw
    nr = s // tr
    last_h = s // hl - 1

    def body(x_ref, hp_ref, hn_ref, d_ref, dn_ref, w_ref, b_ref, dp_in, dx_ref, gw_ref, gb_ref, xs, ds):
        del dp_in
        i = pl.program_id(1)
        xs[pl.ds(0, hl), :] = jnp.where(i > 0, hp_ref[...].astype(F32), 0.0)
        xs[pl.ds(hl, tr), :] = x_ref[...].astype(F32)
        xs[pl.ds(hl + tr, hl), :] = hn_ref[...].astype(F32)
        pre = b_ref[...] + jnp.zeros((tr + hl, cw), F32)
        for k in range(CONV_K):
            pre = pre + w_ref[k:k + 1, :] * xs[pl.ds(hl - (CONV_K - 1) + k, tr + hl), :]
        sg = _sigmoid(pre)
        dsilu = sg * (1.0 + pre * (1.0 - sg))
        ds[pl.ds(0, tr), :] = d_ref[...].astype(F32) * dsilu[0:tr]
        ds[pl.ds(tr, hl), :] = jnp.where(i < nr - 1, dn_ref[...].astype(F32), 0.0) * dsilu[tr:tr + hl]
        dx = jnp.zeros((tr, cw), F32)
        for k in range(CONV_K):
            dx = dx + w_ref[k:k + 1, :] * ds[pl.ds(CONV_K - 1 - k, tr), :]
        dx_ref[...] = dx.astype(BF16)

        @pl.when(i == 0)
        def _():
            gw_ref[...] = jnp.zeros_like(gw_ref)
            gb_ref[...] = jnp.zeros_like(gb_ref)

        dcur = ds[pl.ds(0, tr), :]
        gb_ref[...] += jnp.sum(dcur, axis=0, keepdims=True)
        for k in range(CONV_K):
            gw_ref[k:k + 1, :] += jnp.sum(dcur * xs[pl.ds(hl - (CONV_K - 1) + k, tr), :], axis=0, keepdims=True)

    return pl.pallas_call(
        body, out_shape=(SDS(dproj.shape, BF16), SDS((CONV_K, cd), F32), SDS((1, cd), F32)), grid=(cd // cw, nr),
        in_specs=[pl.BlockSpec((tr, cw), lambda j, i: (i, cb0 + j)),
                  pl.BlockSpec((hl, cw), lambda j, i: (jnp.maximum(i * (tr // hl) - 1, 0), cb0 + j)),
                  pl.BlockSpec((hl, cw), lambda j, i: (jnp.minimum((i + 1) * (tr // hl), last_h), cb0 + j)),
                  pl.BlockSpec((tr, cw), lambda j, i: (i, j)),
                  pl.BlockSpec((hl, cw), lambda j, i: (jnp.minimum((i + 1) * (tr // hl), last_h), j)),
                  pl.BlockSpec((CONV_K, cw), lambda j, i: (0, j)),
                  pl.BlockSpec((1, cw), lambda j, i: (0, j)),
                  pl.BlockSpec(memory_space=pl.ANY)],
        out_specs=(pl.BlockSpec((tr, cw), lambda j, i: (i, cb0 + j)),
                   pl.BlockSpec((CONV_K, cw), lambda j, i: (0, j)),
                   pl.BlockSpec((1, cw), lambda j, i: (0, j))),
        scratch_shapes=[pltpu.VMEM((tr + 2 * hl, cw), F32), pltpu.VMEM((tr + hl, cw), F32)],
        input_output_aliases={7: 0},
        compiler_params=_params(("parallel", "arbitrary")), name="conv_bwd")(
            proj, proj, proj, dact, dact, conv_w, conv_b, dproj)


def _expand(v, e, terms):
    out, rem = None, v
    for _ in range(terms):
        hi = rem.astype(BF16)
        t = _nn(hi, e)
        out = t if out is None else out + t
        rem = rem - hi.astype(F32)
    return out


def _segsum(v, e, terms):
    out, rem = None, v
    for _ in range(terms):
        hi = rem.astype(BF16)
        t = _nt(hi, e)
        out = t if out is None else out + t
        rem = rem - hi.astype(F32)
    return out


def _expand_row(row, e, terms):
    return _expand(jnp.broadcast_to(row, (8, LANES)), e, terms)[0:1]


def _segsum_row(row, e, terms):
    return _segsum(jnp.broadcast_to(row, (8, row.shape[1])), e, terms)[0:1]


def _expansion_matrix(cfg):
    hh = jnp.arange(LANES, dtype=jnp.int32)[:, None]
    cc = jnp.arange(cfg.SI, dtype=jnp.int32)[None, :]
    return (cc // SSM_HEAD_DIM == hh).astype(BF16)


def _tri(lower):
    r = lax.broadcasted_iota(jnp.int32, (CHUNK, CHUNK), 0)
    c = lax.broadcasted_iota(jnp.int32, (CHUNK, CHUNK), 1)
    return (c <= r) if lower else (c >= r)


def _ssd_prep(dtr_ref, db_ref, al_ref, e):
    dtr = dtr_ref[...] + db_ref[...]
    dt = _softplus(dtr)
    a = -jnp.exp(al_ref[...])
    acum = jnp.dot(_tri(True).astype(F32), dt * a, precision=lax.Precision.HIGHEST, preferred_element_type=F32)
    return dtr, dt, a, _expand(dt, e, 2), _expand(acum, e, 3)


def _ssd_fwd(cfg, xact, dt_raw, proj, dt_bias, a_log, d_skip, norm_w, e):
    s, si, cd, gw, bc = cfg.S, cfg.SI, cfg.CD, cfg.GW, cfg.BC
    nc = s // CHUNK
    zb = cfg.OZS // si
    tiles = gw // LANES

    def body(xa_ref, dtr_ref, z_ref, db_ref, al_ref, dsk_ref, nw_ref, e_ref, y_ref, y2_ref, st_ref,
             state, ybuf, x_s, xw_s, ae_s, ea_s, lam_s):
        @pl.when(pl.program_id(0) == 0)
        def _():
            state[...] = jnp.zeros_like(state)

        st_ref[...] = state[...]
        ev = e_ref[...]
        _, _, _, dt_e, a_e = _ssd_prep(dtr_ref, db_ref, al_ref, ev)
        xs = xa_ref[:, 0:si].astype(F32)
        x = xs * dt_e
        lam_e = a_e[CHUNK - 1:CHUNK, :]
        x_s[...] = x.astype(BF16)
        xw_s[...] = (x * jnp.exp(lam_e - a_e)).astype(BF16)
        ae_s[...] = a_e
        ea_s[...] = jnp.exp(a_e)
        ybuf[...] = _expand_row(dsk_ref[...], ev, 3) * xs
        lam_s[...] = jnp.broadcast_to(jnp.exp(lam_e), (8, si))
        tril = _tri(True)
        lane = lax.broadcasted_iota(jnp.int32, (CHUNK, LANES), 1)

        def group(g, carry):
            co = pl.multiple_of(g * gw, LANES)
            bg = xa_ref[:, pl.ds(pl.multiple_of(si + g * SSM_STATE, LANES), SSM_STATE)]
            cg = xa_ref[:, pl.ds(pl.multiple_of(si + bc + g * SSM_STATE, LANES), SSM_STATE)]
            cbm = _nt(cg, bg)
            st = state[:, pl.ds(co, gw)]
            yoff = _nn(cg, st.astype(BF16)) * ea_s[:, pl.ds(co, gw)]
            for k in range(tiles):
                tc = pl.multiple_of(co + k * LANES, LANES)
                at = ae_s[:, pl.ds(tc, LANES)]
                att = at.T
                xt = x_s[:, pl.ds(tc, LANES)]
                acc = yoff[:, k * LANES:(k + 1) * LANES]
                for half in range(2):
                    lo = half * SSM_HEAD_DIM
                    seg = at[:, lo:lo + 1] - att[lo:lo + 1, :]
                    dec = jnp.exp(jnp.where(tril, seg, NEG))
                    xh = jnp.where((lane >= lo) & (lane < lo + SSM_HEAD_DIM), xt, jnp.zeros_like(xt))
                    acc = acc + _nn((cbm * dec).astype(BF16), xh)
                ybuf[:, pl.ds(tc, LANES)] += acc
            state[:, pl.ds(co, gw)] = st * lam_s[0:1, pl.ds(co, gw)] + _tn(bg, xw_s[:, pl.ds(co, gw)])
            return carry

        lax.fori_loop(0, SSM_GROUPS, group, 0)
        y = ybuf[...]
        y_ref[...] = y.astype(BF16)
        z = z_ref[...].astype(F32)
        u = y * (z * _sigmoid(z))
        r = lax.rsqrt(jnp.mean(u * u, axis=-1, keepdims=True) + RMS_EPS)
        y2_ref[...] = (u * r * nw_ref[...]).astype(BF16)

    row = lambda n: pl.BlockSpec((1, n), lambda c: (0, 0))
    return pl.pallas_call(
        body,
        out_shape=(SDS((s, si), BF16), SDS((s, si), BF16), SDS((nc, SSM_STATE, si), F32)),
        grid=(nc,),
        in_specs=[pl.BlockSpec((CHUNK, cd), lambda c: (c, 0)),
                  pl.BlockSpec((CHUNK, LANES), lambda c: (c, 0)),
                  pl.BlockSpec((CHUNK, si), lambda c: (c, zb)),
                  row(LANES), row(LANES), row(LANES), row(si),
                  pl.BlockSpec((LANES, si), lambda c: (0, 0))],
        out_specs=(pl.BlockSpec((CHUNK, si), lambda c: (c, 0)),
                   pl.BlockSpec((CHUNK, si), lambda c: (c, 0)),
                   pl.BlockSpec((None, SSM_STATE, si), lambda c: (c, 0, 0))),
        scratch_shapes=[pltpu.VMEM((SSM_STATE, si), F32), pltpu.VMEM((CHUNK, si), F32),
                        pltpu.VMEM((CHUNK, si), BF16), pltpu.VMEM((CHUNK, si), BF16),
                        pltpu.VMEM((CHUNK, si), F32), pltpu.VMEM((CHUNK, si), F32),
                        pltpu.VMEM((8, si), F32)],
        compiler_params=_params(("arbitrary",)), name="ssd_fwd")(
            xact, dt_raw, proj, dt_bias, a_log, d_skip, norm_w, e)


def _ssd_bwd(cfg, xact, dt_raw, proj, y, dy2, states, dt_bias, a_log, d_skip, norm_w, e, dproj):
    s, si, cd, gw, bc, hpg = cfg.S, cfg.SI, cfg.CD, cfg.GW, cfg.BC, cfg.HPG
    nc = s // CHUNK
    zb = cfg.OZS // si
    tiles = gw // LANES

    def body(xa_ref, dtr_ref, z_ref, y_ref, d2_ref, st_ref, db_ref, al_ref, dsk_ref, nw_ref, e_ref, dp_in,
             dz_ref, dxa_ref, ddt_ref, gnw_ref, gdb_ref, gal_ref, gds_ref,
             dh, dhn, xs_s, x_s, w_s, ae_s, ea_s, g_s, dx_s, dae_s, r_s, lam_s, dle_s):
        del dp_in

        @pl.when(pl.program_id(0) == 0)
        def _():
            dh[...] = jnp.zeros_like(dh)
            gnw_ref[...] = jnp.zeros_like(gnw_ref)
            gdb_ref[...] = jnp.zeros_like(gdb_ref)
            gal_ref[...] = jnp.zeros_like(gal_ref)
            gds_ref[...] = jnp.zeros_like(gds_ref)

        ev = e_ref[...]
        yv = y_ref[...].astype(F32)
        z = z_ref[...].astype(F32)
        sg = _sigmoid(z)
        sz = z * sg
        u = yv * sz
        r = lax.rsqrt(jnp.mean(u * u, axis=-1, keepdims=True) + RMS_EPS)
        nrm = u * r
        d2 = d2_ref[...].astype(F32)
        gnw_ref[...] += jnp.sum(d2 * nrm, axis=0, keepdims=True)
        gn = d2 * nw_ref[...]
        du = r * (gn - nrm * jnp.mean(gn * nrm, axis=-1, keepdims=True))
        gv = du * sz
        dz_ref[...] = (du * yv * (sg * (1.0 + z * (1.0 - sg)))).astype(BF16)
        g_s[...] = gv

        dtr, dt, a, dt_e, a_e = _ssd_prep(dtr_ref, db_ref, al_ref, ev)
        xs = xa_ref[:, 0:si].astype(F32)
        x = xs * dt_e
        lam_e = a_e[CHUNK - 1:CHUNK, :]
        xs_s[...] = xs
        x_s[...] = x
        w_s[...] = jnp.exp(lam_e - a_e)
        ae_s[...] = a_e
        ea_s[...] = jnp.exp(a_e)
        lam_s[...] = jnp.broadcast_to(jnp.exp(lam_e), (8, si))
        gds_ref[...] += _segsum_row(jnp.sum(gv * xs, axis=0, keepdims=True), ev, 2)
        r_s[...] = jnp.zeros_like(r_s)
        tril = _tri(True)
        lane = lax.broadcasted_iota(jnp.int32, (CHUNK, LANES), 1)
        sub = lax.broadcasted_iota(jnp.int32, (CHUNK, LANES), 0)

        def group(g, carry):
            co = pl.multiple_of(g * gw, LANES)
            bo = pl.multiple_of(si + g * SSM_STATE, LANES)
            cof = pl.multiple_of(si + bc + g * SSM_STATE, LANES)
            cols = pl.ds(co, gw)
            bg = xa_ref[:, pl.ds(bo, SSM_STATE)]
            cg = xa_ref[:, pl.ds(cof, SSM_STATE)]
            cbm = _nt(cg, bg)
            st = st_ref[:, cols]
            stb = st.astype(BF16)
            dho = dh[:, cols]
            dhob = dho.astype(BF16)
            ea = ea_s[:, cols]
            gg = g_s[:, cols]
            xg = x_s[:, cols]
            wg = w_s[:, cols]
            explam = lam_s[0:1, cols]
            yoff = _nn(cg, stb) * ea
            ga = (gg * ea).astype(BF16)
            dc = _nt(ga, stb)
            dhn[:, cols] = dho * explam + _tn(cg, ga)
            bdh = _nn(bg, dhob)
            db = _nt((xg * wg).astype(BF16), dhob)
            t = xg * bdh * wg
            dle_s[0:1, cols] = jnp.sum(t, axis=0, keepdims=True) + explam * jnp.sum(dho * st, axis=0, keepdims=True)
            dae_base = gg * yoff - t
            dxw = wg * bdh
            dcb = jnp.zeros((CHUNK, CHUNK), F32)
            for k in range(tiles):
                tc = pl.multiple_of(co + k * LANES, LANES)
                ksl = slice(k * LANES, (k + 1) * LANES)
                at = ae_s[:, pl.ds(tc, LANES)]
                att = at.T
                xt = xg[:, ksl].astype(BF16)
                gt = gg[:, ksl].astype(BF16)
                dxt = dxw[:, ksl]
                place = jnp.zeros((CHUNK, LANES), F32)
                for half in range(2):
                    lo = half * SSM_HEAD_DIM
                    seg = at[:, lo:lo + 1] - att[lo:lo + 1, :]
                    dec = jnp.exp(jnp.where(tril, seg, NEG))
                    mh = cbm * dec
                    gh = jnp.where((lane >= lo) & (lane < lo + SSM_HEAD_DIM), gt, jnp.zeros_like(gt))
                    dm = _nt(gh, xt)
                    dxt = dxt + _tn(mh.astype(BF16), gh)
                    dcb = dcb + dm * dec
                    dseg = dm * mh
                    place = place + jnp.where(lane == lo, jnp.sum(dseg, axis=1, keepdims=True), 0.0)
                    hidx = g * hpg + 2 * k + half
                    r_s[...] += jnp.where(sub == hidx, jnp.sum(dseg, axis=0, keepdims=True), 0.0)
                dx_s[:, pl.ds(tc, LANES)] = dxt
                dae_s[:, pl.ds(tc, LANES)] = dae_base[:, ksl] + place
            dcbb = dcb.astype(BF16)
            dxa_ref[:, pl.ds(bo, SSM_STATE)] = (db + _tn(dcbb, cg)).astype(BF16)
            dxa_ref[:, pl.ds(cof, SSM_STATE)] = (dc + _nn(dcbb, bg)).astype(BF16)
            return carry

        lax.fori_loop(0, SSM_GROUPS, group, 0)
        dlam = _segsum_row(dle_s[0:1, :], ev, 2)
        da_ = _segsum(dae_s[...], ev, 2) - r_s[...].T
        da_ = da_ + jnp.where(sub == CHUNK - 1, dlam, 0.0)
        dda = jnp.dot(_tri(False).astype(F32), da_, precision=lax.Precision.HIGHEST, preferred_element_type=F32)
        dxv = dx_s[...]
        xs = xs_s[...]
        ddt = dda * a + _segsum(dxv * xs, ev, 2)
        gal_ref[...] += jnp.sum(dda * dt, axis=0, keepdims=True) * a
        ddtr = ddt * _sigmoid(dtr)
        gdb_ref[...] += jnp.sum(ddtr, axis=0, keepdims=True)
        ddt_ref[...] = ddtr
        dxa_ref[:, 0:si] = (dxv * dt_e + g_s[...] * _expand_row(dsk_ref[...], ev, 3)).astype(BF16)
        dh[...] = dhn[...]

    rev = lambda c: nc - 1 - c
    row = lambda n: pl.BlockSpec((1, n), lambda c: (0, 0))
    big = lambda: pltpu.VMEM((CHUNK, si), F32)
    return pl.pallas_call(
        body,
        out_shape=(SDS(dproj.shape, BF16), SDS((s, cd), BF16), SDS((s, LANES), F32),
                   SDS((1, si), F32), SDS((1, LANES), F32), SDS((1, LANES), F32), SDS((1, LANES), F32)),
        grid=(nc,),
        in_specs=[pl.BlockSpec((CHUNK, cd), lambda c: (rev(c), 0)),
                  pl.BlockSpec((CHUNK, LANES), lambda c: (rev(c), 0)),
                  pl.BlockSpec((CHUNK, si), lambda c: (rev(c), zb)),
                  pl.BlockSpec((CHUNK, si), lambda c: (rev(c), 0)),
                  pl.BlockSpec((CHUNK, si), lambda c: (rev(c), 0)),
                  pl.BlockSpec((None, SSM_STATE, si), lambda c: (rev(c), 0, 0)),
                  row(LANES), row(LANES), row(LANES), row(si),
                  pl.BlockSpec((LANES, si), lambda c: (0, 0)),
                  pl.BlockSpec(memory_space=pl.ANY)],
        out_specs=(pl.BlockSpec((CHUNK, si), lambda c: (rev(c), zb)),
                   pl.BlockSpec((CHUNK, cd), lambda c: (rev(c), 0)),
                   pl.BlockSpec((CHUNK, LANES), lambda c: (rev(c), 0)),
                   row(si), row(LANES), row(LANES), row(LANES)),
        scratch_shapes=[pltpu.VMEM((SSM_STATE, si), F32), pltpu.VMEM((SSM_STATE, si), F32),
                        big(), big(), big(), big(), big(), big(), big(), big(),
                        pltpu.VMEM((CHUNK, LANES), F32), pltpu.VMEM((8, si), F32), pltpu.VMEM((8, si), F32)],
        input_output_aliases={11: 0},
        compiler_params=_params(("arbitrary",)), name="ssd_bwd")(
            xact, dt_raw, proj, y, dy2, states, dt_bias, a_log, d_skip, norm_w, e, dproj)


MERGE_TR = 512
MERGE_CW = 512


def _merge_fwd(cfg, proj, a_br, s_br):
    s, d = cfg.S, cfg.D
    tr, cw = MERGE_TR, MERGE_CW
    ga0, gs0 = cfg.OGA // cw, cfg.OGS // cw

    def body(ga_ref, gs_ref, a_ref, s_ref, o_ref):
        o_ref[...] = (_sigmoid(ga_ref[...].astype(F32)) * a_ref[...].astype(F32)
                      + _sigmoid(gs_ref[...].astype(F32)) * s_ref[...].astype(F32)).astype(BF16)

    blk = pl.BlockSpec((tr, cw), lambda i, j: (i, j))
    return pl.pallas_call(
        body, out_shape=SDS((s, d), BF16), grid=(s // tr, d // cw),
        in_specs=[pl.BlockSpec((tr, cw), lambda i, j: (i, ga0 + j)),
                  pl.BlockSpec((tr, cw), lambda i, j: (i, gs0 + j)), blk, blk],
        out_specs=blk, compiler_params=_params(("parallel", "parallel")), name="merge_fwd")(proj, proj, a_br, s_br)


def _merge_bwd(cfg, proj, branch, dmerged, gate_off, dproj, name):
    s, d = cfg.S, cfg.D
    tr, cw = MERGE_TR, MERGE_CW
    g0 = gate_off // cw
    fresh = dproj is None

    def body(*refs):
        g_ref, b_ref, dm_ref = refs[:3]
        dg_ref, db_ref = refs[-2:]
        dm = dm_ref[...].astype(F32)
        sg = _sigmoid(g_ref[...].astype(F32))
        db_ref[...] = (dm * sg).astype(BF16)
        dg_ref[...] = (dm * b_ref[...].astype(F32) * sg * (1.0 - sg)).astype(BF16)

    blk = pl.BlockSpec((tr, cw), lambda i, j: (i, j))
    gate = pl.BlockSpec((tr, cw), lambda i, j: (i, g0 + j))
    return pl.pallas_call(
        body, out_shape=(SDS((s, cfg.NM), BF16), SDS((s, d), BF16)), grid=(s // tr, d // cw),
        in_specs=[gate, blk, blk] + ([] if fresh else [HBM_SPEC]),
        out_specs=(gate, blk),
        input_output_aliases={} if fresh else {3: 0},
        compiler_params=_params(("parallel", "parallel")), name=name)(
            *((proj, branch, dmerged) + (() if fresh else (dproj,))))


def _outproj_loss(merged, w_out, x, target, fnw):
    s, d = x.shape
    tr = 256

    def body(m_ref, w_ref, x_ref, t_ref, fw_ref, dof_ref, dob_ref, loss_ref, g_ref):
        out = x_ref[...] + _nn(m_ref[...], w_ref[...])
        r = lax.rsqrt(jnp.mean(out * out, axis=-1, keepdims=True) + RMS_EPS)
        nrm = out * r
        fw = fw_ref[...]
        err = nrm * fw - t_ref[...]
        dy = err * (1.0 / d)
        gy = dy * fw
        dout = r * (gy - nrm * jnp.mean(gy * nrm, axis=-1, keepdims=True))
        dof_ref[...] = dout
        dob_ref[...] = dout.astype(BF16)

        @pl.when(pl.program_id(0) == 0)
        def _():
            loss_ref[...] = jnp.zeros_like(loss_ref)
            g_ref[...] = jnp.zeros_like(g_ref)

        loss_ref[...] += jnp.sum(jnp.sum(err * err, axis=1, keepdims=True), axis=0, keepdims=True) * (0.5 / d)
        g_ref[...] += jnp.sum(dy * nrm, axis=0, keepdims=True)

    blk = pl.BlockSpec((tr, d), lambda i: (i, 0))
    return pl.pallas_call(
        body, out_shape=(SDS((s, d), F32), SDS((s, d), BF16), SDS((1, LANES), F32), SDS((1, d), F32)), grid=(s // tr,),
        in_specs=[blk, pl.BlockSpec((d, d), lambda i: (0, 0)), blk, blk, pl.BlockSpec((1, d), lambda i: (0, 0))],
        out_specs=(blk, blk, pl.BlockSpec((1, LANES), lambda i: (0, 0)), pl.BlockSpec((1, d), lambda i: (0, 0))),
        compiler_params=_params(("arbitrary",)), name="outproj_loss")(merged, w_out, x, target, fnw)


ELEMWISE_BLOCK_BYTES = 1 << 20


def _row_block(rows, cols, itemsize=4):
    best = None
    for tr in range(8, rows + 1, 8):
        if rows % tr == 0 and tr * cols * itemsize <= ELEMWISE_BLOCK_BYTES:
            best = tr
    return best if best is not None else rows


def _adamw(w, g, m, v, name):
    rows, cols = w.shape
    tr = _row_block(rows, cols)

    def body(w_ref, g_ref, m_ref, v_ref, d_ref, nm_ref, nv_ref):
        gv = g_ref[...]
        nm = ADAM_B1 * m_ref[...] + (1.0 - ADAM_B1) * gv
        nv = ADAM_B2 * v_ref[...] + (1.0 - ADAM_B2) * jnp.square(gv)
        m_hat = nm / (1.0 - ADAM_B1 ** ADAM_STEP)
        v_hat = nv / (1.0 - ADAM_B2 ** ADAM_STEP)
        d_ref[...] = -ADAM_LR * (m_hat / (jnp.sqrt(v_hat) + ADAM_EPS) + ADAM_WD * w_ref[...])
        nm_ref[...] = nm
        nv_ref[...] = nv

    blk = pl.BlockSpec((tr, cols), lambda i: (i, 0))
    out = SDS((rows, cols), F32)
    return pl.pallas_call(
        body, out_shape=(out, out, out), grid=(rows // tr,), in_specs=[blk] * 4, out_specs=(blk,) * 3,
        compiler_params=_params(("parallel",)), name=name)(w, g, m, v)


HBM_SPEC = pl.BlockSpec(memory_space=pl.ANY)


def _position():
    return lax.axis_index("x"), lax.axis_index("y"), lax.axis_index("c")


def _gather_chips(shards):
    n = len(shards)

    def body(*refs):
        ins, outs = refs[:n], refs[n:2 * n]
        send_sems, recv_sems, loc_sems = refs[2 * n:]
        x, y, c = _position()
        me = 2 * x + y
        peers = [(1 - x, y), (x, 1 - y), (1 - x, 1 - y)]

        def remote(t, p, chip):
            px, py = peers[p]
            return pltpu.make_async_remote_copy(
                src_ref=ins[t], dst_ref=outs[t].at[chip], send_sem=send_sems.at[3 * t + p],
                recv_sem=recv_sems.at[3 * t + p], device_id=(px, py, c), device_id_type=MESH)

        local = [pltpu.make_async_copy(ins[t], outs[t].at[me], loc_sems.at[t]) for t in range(n)]
        sends = [remote(t, p, me) for t in range(n) for p in range(3)]
        for cp in local + sends:
            cp.start()
        for t in range(n):
            for p, (px, py) in enumerate(peers):
                remote(t, p, 2 * px + py).wait_recv()
        for cp in sends:
            cp.wait_send()
        for cp in local:
            cp.wait()

    return pl.pallas_call(
        body, out_shape=[SDS((N_CHIPS,) + a.shape, a.dtype) for a in shards],
        in_specs=[HBM_SPEC] * n, out_specs=[HBM_SPEC] * n,
        scratch_shapes=[pltpu.SemaphoreType.DMA((3 * n,)), pltpu.SemaphoreType.DMA((3 * n,)),
                        pltpu.SemaphoreType.DMA((n,))],
        compiler_params=pltpu.CompilerParams(has_side_effects=True), name="gather_weights")(*shards)


def _exchange_halves(grads):
    n = len(grads)

    def body(*refs):
        ins, outs = refs[:n], refs[n:2 * n]
        send_sems, recv_sems = refs[2 * n:]
        x, y, c = _position()
        cps = []
        for t in range(n):
            r2 = ins[t].shape[1] // 2
            cps.append(pltpu.make_async_remote_copy(
                src_ref=ins[t].at[:, pl.ds((1 - c) * r2, r2), :], dst_ref=outs[t],
                send_sem=send_sems.at[t], recv_sem=recv_sems.at[t], device_id=(x, y, 1 - c), device_id_type=MESH))
        for cp in cps:
            cp.start()
        for cp in cps:
            cp.wait()

    return pl.pallas_call(
        body, out_shape=[SDS((a.shape[0], a.shape[1] // 2, a.shape[2]), a.dtype) for a in grads],
        in_specs=[HBM_SPEC] * n, out_specs=[HBM_SPEC] * n,
        scratch_shapes=[pltpu.SemaphoreType.DMA((n,)), pltpu.SemaphoreType.DMA((n,))],
        compiler_params=pltpu.CompilerParams(has_side_effects=True), name="reduce_sibling")(*grads)


def _scatter_chips(parts):
    n = len(parts)

    def body(*refs):
        ins, outs = refs[:n], refs[n:2 * n]
        send_sems, recv_sems, loc_sems = refs[2 * n:]
        x, y, c = _position()
        me = 2 * x + y
        peers = [(1 - x, y), (x, 1 - y), (1 - x, 1 - y)]

        def remote(t, p, src_slab, dst_slab):
            px, py = peers[p]
            return pltpu.make_async_remote_copy(
                src_ref=ins[t].at[src_slab], dst_ref=outs[t].at[dst_slab], send_sem=send_sems.at[3 * t + p],
                recv_sem=recv_sems.at[3 * t + p], device_id=(px, py, c), device_id_type=MESH)

        local = [pltpu.make_async_copy(ins[t].at[me], outs[t].at[me], loc_sems.at[t]) for t in range(n)]
        sends = [remote(t, p, 2 * peers[p][0] + peers[p][1], me) for t in range(n) for p in range(3)]
        for cp in local + sends:
            cp.start()
        for t in range(n):
            for p, (px, py) in enumerate(peers):
                remote(t, p, me, 2 * px + py).wait_recv()
        for cp in sends:
            cp.wait_send()
        for cp in local:
            cp.wait()

    return pl.pallas_call(
        body, out_shape=[SDS(a.shape, a.dtype) for a in parts],
        in_specs=[HBM_SPEC] * n, out_specs=[HBM_SPEC] * n,
        scratch_shapes=[pltpu.SemaphoreType.DMA((3 * n,)), pltpu.SemaphoreType.DMA((3 * n,)),
                        pltpu.SemaphoreType.DMA((n,))],
        compiler_params=pltpu.CompilerParams(has_side_effects=True), name="reduce_chips")(*parts)


def _share_halves(halves):
    n = len(halves)

    def body(*refs):
        ins, outs = refs[:n], refs[n:2 * n]
        send_sems, recv_sems, loc_sems = refs[2 * n:]
        x, y, c = _position()
        local = [pltpu.make_async_copy(ins[t], outs[t].at[c], loc_sems.at[t]) for t in range(n)]
        sends = [pltpu.make_async_remote_copy(
            src_ref=ins[t], dst_ref=outs[t].at[c], send_sem=send_sems.at[t], recv_sem=recv_sems.at[t],
            device_id=(x, y, 1 - c), device_id_type=MESH) for t in range(n)]
        for cp in local + sends:
            cp.start()
        for t in range(n):
            pltpu.make_async_remote_copy(
                src_ref=ins[t], dst_ref=outs[t].at[1 - c], send_sem=send_sems.at[t], recv_sem=recv_sems.at[t],
                device_id=(x, y, 1 - c), device_id_type=MESH).wait_recv()
        for cp in sends:
            cp.wait_send()
        for cp in local:
            cp.wait()

    return pl.pallas_call(
        body, out_shape=[SDS((2,) + a.shape, a.dtype) for a in halves],
        in_specs=[HBM_SPEC] * n, out_specs=[HBM_SPEC] * n,
        scratch_shapes=[pltpu.SemaphoreType.DMA((n,)), pltpu.SemaphoreType.DMA((n,)), pltpu.SemaphoreType.DMA((n,))],
        compiler_params=pltpu.CompilerParams(has_side_effects=True), name="share_sibling")(*halves)


def _add_sibling(grad, recv, core):
    nch, r2, cols = recv.shape
    tr = _row_block(r2, cols)
    nb = r2 // tr

    def body(c_ref, g_ref, r_ref, o_ref):
        del c_ref
        o_ref[...] = g_ref[...] + r_ref[...]

    return pl.pallas_call(
        body, out_shape=SDS(recv.shape, F32),
        grid_spec=pltpu.PrefetchScalarGridSpec(
            num_scalar_prefetch=1, grid=(nch, nb),
            in_specs=[pl.BlockSpec((None, tr, cols), lambda j, i, c_ref: (j, c_ref[0] * nb + i, 0)),
                      pl.BlockSpec((None, tr, cols), lambda j, i, c_ref: (j, i, 0))],
            out_specs=pl.BlockSpec((None, tr, cols), lambda j, i, c_ref: (j, i, 0))),
        compiler_params=_params(("parallel", "parallel")), name="add_sibling")(core, grad, recv)


def _add_chips(parts):
    nch, r2, cols = parts.shape
    tr = _row_block(r2, cols)

    def body(p_ref, o_ref):
        acc = p_ref[0]
        for j in range(1, nch):
            acc = acc + p_ref[j]
        o_ref[...] = acc

    return pl.pallas_call(
        body, out_shape=SDS((r2, cols), F32), grid=(r2 // tr,),
        in_specs=[pl.BlockSpec((nch, tr, cols), lambda i: (0, i, 0))],
        out_specs=pl.BlockSpec((tr, cols), lambda i: (i, 0)),
        compiler_params=_params(("parallel",)), name="add_chips")(parts)


def _allreduce_small(pack):
    rows = pack.shape[0]

    def body(p_ref, o_ref, buf, send_sems, recv_sems):
        x, y, c = _position()
        me = 4 * x + 2 * y + c
        buf[me] = p_ref[...]

        def copy(dst_dev, slot):
            return pltpu.make_async_remote_copy(
                src_ref=p_ref, dst_ref=buf.at[slot], send_sem=send_sems.at[dst_dev], recv_sem=recv_sems.at[slot],
                device_id=(dst_dev // 4, (dst_dev // 2) % 2, dst_dev % 2), device_id_type=MESH)

        for dev in range(N_DEV):
            @pl.when(dev != me)
            def _():
                copy(dev, me).start()
        for dev in range(N_DEV):
            @pl.when(dev != me)
            def _():
                copy(dev, dev).wait_recv()
        for dev in range(N_DEV):
            @pl.when(dev != me)
            def _():
                copy(dev, me).wait_send()
        acc = buf[0]
        for dev in range(1, N_DEV):
            acc = acc + buf[dev]
        o_ref[...] = acc

    return pl.pallas_call(
        body, out_shape=SDS(pack.shape, F32),
        in_specs=[pl.BlockSpec(memory_space=pltpu.VMEM)], out_specs=pl.BlockSpec(memory_space=pltpu.VMEM),
        scratch_shapes=[pltpu.VMEM((N_DEV, rows, LANES), F32), pltpu.SemaphoreType.DMA((N_DEV,)),
                        pltpu.SemaphoreType.DMA((N_DEV,))],
        compiler_params=pltpu.CompilerParams(has_side_effects=True), name="allreduce_small")(pack)


ATTN_TQ = 256


def _local_step(cfg, x, target, w):
    d = cfg.D
    win = ATTN_WINDOW
    hn = _rmsnorm_fwd(x, w["norm_w"])
    proj = _mm(hn, w["w_main"], "nn", BF16, "proj_main")
    dt_raw = _mm(hn, w["w_dt"], "nn", F32, "proj_dt")
    logn, dist = _attn_tables(ATTN_TQ)
    slopes = _slopes(cfg.H)
    kpad = jnp.pad(proj[:, cfg.OK:cfg.OK + d], ((win, 0), (0, 0)))
    vpad = jnp.pad(proj[:, cfg.OV:cfg.OV + d], ((win, 0), (0, 0)))
    o_a, oag = _attn_fwd(cfg, proj, kpad, vpad, logn, dist, slopes)
    xact = _conv_fwd(cfg, proj, w["conv_w"], w["conv_b"])
    e = _expansion_matrix(cfg)
    y, y2, states = _ssd_fwd(cfg, xact, dt_raw, proj, w["dt_bias"], w["a_log"], w["d_skip"], w["ssm_norm_w"], e)
    a_br = _mm(oag, w["w_attn"], "nn", BF16, "branch_attn")
    s_br = _mm(y2, w["w_ssm"], "nn", BF16, "branch_ssm")
    merged = _merge_fwd(cfg, proj, a_br, s_br)
    dout_f, dout_b, loss_row, g_fnw = _outproj_loss(merged, w["w_out"], x, target, w["final_norm_w"])

    dmerged = _mm(dout_b, w["w_out"], "nt", BF16, "d_merged")
    g_w_out = _mm(merged, dout_b, "tn", F32, "g_w_out")
    dproj, da_br = _merge_bwd(cfg, proj, a_br, dmerged, cfg.OGA, None, "merge_bwd_attn")
    dproj, ds_br = _merge_bwd(cfg, proj, s_br, dmerged, cfg.OGS, dproj, "merge_bwd_ssm")
    doag = _mm(da_br, w["w_attn"], "nt", BF16, "d_oag")
    g_w_attn = _mm(oag, da_br, "tn", F32, "g_w_attn")
    dy2 = _mm(ds_br, w["w_ssm"], "nt", BF16, "d_y2")
    g_w_ssm = _mm(y2, ds_br, "tn", F32, "g_w_ssm")
    dproj, dxact, ddt, g_snw, g_dtb, g_alog, g_dsk = _ssd_bwd(
        cfg, xact, dt_raw, proj, y, dy2, states, w["dt_bias"], w["a_log"], w["d_skip"], w["ssm_norm_w"], e, dproj)
    dproj, g_cw, g_cb = _conv_bwd(cfg, proj, dxact, w["conv_w"], w["conv_b"], dproj)
    dq, dza, dkp, dvp = _attn_bwd(cfg, proj, kpad, vpad, o_a, doag, logn, dist, slopes)
    datt = jnp.concatenate([dq, dkp[win:].astype(BF16), dvp[win:].astype(BF16), dza], axis=1)
    dproj = lax.dynamic_update_slice(dproj, datt, (0, 0))
    ddt_b = ddt.astype(BF16)
    dhn = _mm(dproj, w["w_main"], "nt", F32, "d_hn", init=_mm(ddt_b, w["w_dt"], "nt", F32, "d_hn_dt"))
    g_w_main = _mm(hn, dproj, "tn", F32, "g_w_main")
    g_w_dt = _mm(hn, ddt_b, "tn", F32, "g_w_dt")
    grad_x, g_nw = _rmsnorm_bwd(x, w["norm_w"], dhn, dout_f)
    grads = dict(norm_w=g_nw, w_main=g_w_main, w_dt=g_w_dt, conv_w=g_cw, conv_b=g_cb, dt_bias=g_dtb, a_log=g_alog,
                 d_skip=g_dsk, ssm_norm_w=g_snw, w_attn=g_w_attn, w_ssm=g_w_ssm, w_out=g_w_out, final_norm_w=g_fnw)
    return loss_row, grad_x, grads


def _pad_lanes(v):
    return jnp.pad(v, ((0, 0), (0, LANES - v.shape[1])))


def _full_weights(cfg, norm_w, w_in, conv_w, conv_b, dt_bias, a_log, d_skip, ssm_norm_w, w_attn, w_ssm, w_out, fnw):
    w_main = jnp.concatenate([w_in[:, :cfg.OGA], w_in[:, cfg.OGA + cfg.NH:]], axis=1).astype(BF16)
    w_dt = _pad_lanes(w_in[:, cfg.OGA:cfg.OGA + cfg.NH]).astype(BF16)
    return dict(norm_w=norm_w, w_main=w_main, w_dt=w_dt, conv_w=conv_w, conv_b=conv_b, dt_bias=_pad_lanes(dt_bias),
                a_log=_pad_lanes(a_log), d_skip=_pad_lanes(d_skip), ssm_norm_w=ssm_norm_w, w_attn=w_attn.astype(BF16),
                w_ssm=w_ssm.astype(BF16), w_out=w_out.astype(BF16), final_norm_w=fnw)


def _grad_w_in(cfg, grads):
    return jnp.concatenate([grads["w_main"][:, :cfg.OGA], grads["w_dt"][:, :cfg.NH], grads["w_main"][:, cfg.OGA:]], axis=1)


def kernel(x, norm_w, w_in, conv_w, conv_b, dt_bias, a_log, d_skip, ssm_norm_w, w_attn_branch, w_ssm_branch, w_out, final_norm_w, loss_target, m_norm_w, m_w_in, m_conv_w, m_conv_b, m_dt_bias, m_a_log, m_d_skip, m_ssm_norm_w, m_w_attn_branch, m_w_ssm_branch, m_w_out, m_final_norm_w, v_norm_w, v_w_in, v_conv_w, v_conv_b, v_dt_bias, v_a_log, v_d_skip, v_ssm_norm_w, v_w_attn_branch, v_w_ssm_branch, v_w_out, v_final_norm_w):
    cfg = _Cfg(x.shape[1], x.shape[2])
    d, si, cd, nh = cfg.D, cfg.SI, cfg.CD, cfg.NH
    chip = 2 * lax.axis_index("x") + lax.axis_index("y")
    core = lax.axis_index("c").astype(jnp.int32).reshape(1)

    a_in, a_attn, a_ssm, a_out, a_cw = _gather_chips(
        [w_in[0].astype(BF16), w_attn_branch[0].astype(BF16), w_ssm_branch[0].astype(BF16), w_out[0].astype(BF16),
         conv_w[0]])
    w_in_full = a_in.transpose(1, 0, 2).reshape(d, cfg.N_IN)
    conv_w_full = a_cw.transpose(1, 0, 2).reshape(CONV_K, cd)
    w = _full_weights(cfg, norm_w, w_in_full, conv_w_full, conv_b, dt_bias, a_log, d_skip, ssm_norm_w,
                      a_attn.reshape(d, d), a_ssm.reshape(si, d), a_out.reshape(d, d), final_norm_w.reshape(1, d))

    loss_row, grad_x, grads = _local_step(cfg, x[0], loss_target[0], w)

    by_chip = [_grad_w_in(cfg, grads).reshape(d, N_CHIPS, cfg.N_IN // N_CHIPS).transpose(1, 0, 2),
               grads["w_attn"].reshape(N_CHIPS, d // N_CHIPS, d),
               grads["w_ssm"].reshape(N_CHIPS, si // N_CHIPS, d),
               grads["w_out"].reshape(N_CHIPS, d // N_CHIPS, d)]
    from_sibling = _exchange_halves(by_chip)
    chip_sums = [_add_sibling(g, r, core) for g, r in zip(by_chip, from_sibling)]
    from_chips = _scatter_chips(chip_sums)
    halves = [_add_chips(p) for p in from_chips]
    g_in, g_attn, g_ssm, g_out = [h.reshape(2 * h.shape[1], h.shape[2]) for h in _share_halves(halves)]

    small = [loss_row, grads["norm_w"], grads["conv_b"], grads["dt_bias"], grads["a_log"], grads["d_skip"],
             grads["ssm_norm_w"], grads["final_norm_w"], grads["conv_w"].reshape(1, CONV_K * cd)]
    sizes = [a.shape[1] for a in small]
    total = sum(sizes)
    rows = -(-total // (8 * LANES)) * 8
    flat = jnp.pad(jnp.concatenate(small, axis=1), ((0, 0), (0, rows * LANES - total)))
    red = _allreduce_small(flat.reshape(rows, LANES)).reshape(1, rows * LANES)
    offs = [sum(sizes[:i]) for i in range(len(sizes))]
    loss_r, g_nw, g_cb, g_dtb, g_alog, g_dsk, g_snw, g_fnw, g_cw_flat = [
        red[:, o:o + n] for o, n in zip(offs, sizes)]
    loss = loss_r[0, 0]
    g_dtb, g_alog, g_dsk = g_dtb[:, :nh], g_alog[:, :nh], g_dsk[:, :nh]
    cshard = cd // N_CHIPS
    g_cw = lax.dynamic_slice_in_dim(g_cw_flat.reshape(CONV_K, cd), chip * cshard, cshard, axis=1)

    upd = {}
    for name, wv, gv, mv, vv in [("w_in", w_in[0], g_in, m_w_in[0], v_w_in[0]),
                                 ("w_attn", w_attn_branch[0], g_attn, m_w_attn_branch[0], v_w_attn_branch[0]),
                                 ("w_ssm", w_ssm_branch[0], g_ssm, m_w_ssm_branch[0], v_w_ssm_branch[0]),
                                 ("w_out", w_out[0], g_out, m_w_out[0], v_w_out[0])]:
        upd[name] = _adamw(wv, gv, mv, vv, "adamw_" + name)
    names = ["norm_w", "conv_w", "conv_b", "dt_bias", "a_log", "d_skip", "ssm_norm_w", "final_norm_w"]
    ws = [norm_w, conv_w[0].reshape(1, -1), conv_b, dt_bias, a_log, d_skip, ssm_norm_w, final_norm_w.reshape(1, d)]
    gs = [g_nw, g_cw.reshape(1, -1), g_cb, g_dtb, g_alog, g_dsk, g_snw, g_fnw]
    ms = [m_norm_w, m_conv_w[0].reshape(1, -1), m_conv_b, m_dt_bias, m_a_log, m_d_skip, m_ssm_norm_w,
          m_final_norm_w.reshape(1, d)]
    vs = [v_norm_w, v_conv_w[0].reshape(1, -1), v_conv_b, v_dt_bias, v_a_log, v_d_skip, v_ssm_norm_w,
          v_final_norm_w.reshape(1, d)]
    ssz = [a.shape[1] for a in ws]
    stot = sum(ssz)
    srows = -(-stot // (8 * LANES)) * 8

    def pack(parts):
        return jnp.pad(jnp.concatenate(parts, axis=1), ((0, 0), (0, srows * LANES - stot))).reshape(srows, LANES)

    packed = _adamw(pack(ws), pack(gs), pack(ms), pack(vs), "adamw_small")
    soffs = [sum(ssz[:i]) for i in range(len(ssz))]
    for k, nm in enumerate(names):
        upd[nm] = tuple(p.reshape(1, srows * LANES)[:, soffs[k]:soffs[k] + ssz[k]] for p in packed)

    shapes = dict(norm_w=norm_w.shape, w_in=w_in.shape, conv_w=conv_w.shape, conv_b=conv_b.shape, dt_bias=dt_bias.shape,
                  a_log=a_log.shape, d_skip=d_skip.shape, ssm_norm_w=ssm_norm_w.shape, w_attn=w_attn_branch.shape,
                  w_ssm=w_ssm_branch.shape, w_out=w_out.shape, final_norm_w=final_norm_w.shape)
    order = ["norm_w", "w_in", "conv_w", "conv_b", "dt_bias", "a_log", "d_skip", "ssm_norm_w", "w_attn", "w_ssm",
             "w_out", "final_norm_w"]
    gradv = dict(norm_w=g_nw, w_in=g_in, conv_w=g_cw, conv_b=g_cb, dt_bias=g_dtb, a_log=g_alog, d_skip=g_dsk,
                 ssm_norm_w=g_snw, w_attn=g_attn, w_ssm=g_ssm, w_out=g_out, final_norm_w=g_fnw)
    outs = [loss, grad_x[None]]
    outs += [gradv[n].reshape(shapes[n]) for n in order]
    for k in range(3):
        outs += [upd[n][k].reshape(shapes[n]) for n in order]
    return tuple(outs)
```

```python
import functools
import math

import jax
import jax.numpy as jnp
from jax import lax
from jax.experimental import pallas as pl
from jax.experimental.pallas import tpu as pltpu

F32 = jnp.float32
BF16 = jnp.bfloat16
SDS = jax.ShapeDtypeStruct

RMS_EPS = 1e-6
LANES = 128
CHUNK = 128
SSM_HEAD_DIM = 64
SSM_GROUPS = 8
SSM_STATE = 128
CONV_K = 4
ATTN_HEAD_DIM = 128
DILATED_PATTERNS = ((128, 1), (512, 4), (2048, 16))
ATTN_WINDOW = max(w for w, _ in DILATED_PATTERNS)
NEG = -1e30
VMEM_LIMIT = 56 * 1024 * 1024
ADAM_LR, ADAM_B1, ADAM_B2, ADAM_EPS, ADAM_WD, ADAM_STEP = 0.001, 0.9, 0.999, 1e-08, 0.01, 10
MESH = pl.DeviceIdType.MESH
N_CHIPS = 4
N_DEV = 8


class _Cfg:
    def __init__(self, s, d):
        self.S, self.D = s, d
        self.H = d // ATTN_HEAD_DIM
        self.SI = 2 * d
        self.NH = self.SI // SSM_HEAD_DIM
        self.HPG = self.NH // SSM_GROUPS
        self.GW = self.HPG * SSM_HEAD_DIM
        self.BC = SSM_GROUPS * SSM_STATE
        self.CD = self.SI + 2 * self.BC
        self.OQ, self.OK, self.OV, self.OZA = 0, d, 2 * d, 3 * d
        self.OZS = 4 * d
        self.OXBC = self.OZS + self.SI
        self.OGA = self.OXBC + self.CD
        self.OGS = self.OGA + d
        self.NM = self.OGS + d
        self.N_IN = self.NM + self.NH
        assert self.GW % LANES == 0 and self.NH <= LANES and s % 512 == 0 and d % 512 == 0


def _params(sem=None):
    return pltpu.CompilerParams(dimension_semantics=sem, vmem_limit_bytes=VMEM_LIMIT)


def _sigmoid(x):
    return 1.0 / (1.0 + jnp.exp(-x))


def _softplus(x):
    u = jnp.exp(-jnp.abs(x))
    l1p = jnp.where(u < 1e-3, u * (1.0 - u * (0.5 - u * (1.0 / 3.0))), jnp.log(1.0 + u))
    return jnp.maximum(x, 0.0) + l1p


def _nt(a, b):
    return lax.dot_general(a, b, (((1,), (1,)), ((), ())), preferred_element_type=F32)


def _tn(a, b):
    return lax.dot_general(a, b, (((0,), (0,)), ((), ())), preferred_element_type=F32)


def _nn(a, b):
    return jnp.dot(a, b, preferred_element_type=F32)


def _tile(n, target):
    if n <= target:
        return n
    best = None
    for t in range(LANES, target + 1, LANES):
        if n % t == 0:
            best = t
    assert best is not None, (n, target)
    return best


def _mm(a, b, dims, out_dtype, name, tm=1024, tn=2048, tk=512, init=None):
    if dims == "nn":
        (m, k), (k2, n) = a.shape, b.shape
    elif dims == "nt":
        (m, k), (n, k2) = a.shape, b.shape
    else:
        (k, m), (k2, n) = a.shape, b.shape
    assert k == k2
    tm, tn, tk = _tile(m, tm), _tile(n, tn), _tile(k, tk)
    nk = k // tk
    if dims == "tn":
        a_spec = pl.BlockSpec((tk, tm), lambda i, j, kk: (kk, i))
    else:
        a_spec = pl.BlockSpec((tm, tk), lambda i, j, kk: (i, kk))
    if dims == "nt":
        b_spec = pl.BlockSpec((tn, tk), lambda i, j, kk: (j, kk))
    else:
        b_spec = pl.BlockSpec((tk, tn), lambda i, j, kk: (kk, j))
    o_spec = pl.BlockSpec((tm, tn), lambda i, j, kk: (i, j))
    op = {"nn": _nn, "nt": _nt, "tn": _tn}[dims]
    has_init = init is not None

    def body(*refs):
        if has_init:
            a_ref, b_ref, i_ref, o_ref, acc = refs
        else:
            a_ref, b_ref, o_ref, acc = refs
        kk = pl.program_id(2)

        @pl.when(kk == 0)
        def _():
            acc[...] = i_ref[...].astype(F32) if has_init else jnp.zeros_like(acc)

        acc[...] += op(a_ref[...], b_ref[...])

        @pl.when(kk == nk - 1)
        def _():
            o_ref[...] = acc[...].astype(out_dtype)

    in_specs = [a_spec, b_spec] + ([o_spec] if has_init else [])
    args = (a, b) + ((init,) if has_init else ())
    return pl.pallas_call(
        body, out_shape=SDS((m, n), out_dtype), grid=(m // tm, n // tn, nk),
        in_specs=in_specs, out_specs=o_spec, scratch_shapes=[pltpu.VMEM((tm, tn), F32)],
        compiler_params=_params(("parallel", "parallel", "arbitrary")), name=name)(*args)


def _rmsnorm_fwd(x, w):
    s, d = x.shape
    tr = 256

    def body(x_ref, w_ref, o_ref):
        xv = x_ref[...]
        r = lax.rsqrt(jnp.mean(xv * xv, axis=-1, keepdims=True) + RMS_EPS)
        o_ref[...] = (xv * r * w_ref[...]).astype(BF16)

    return pl.pallas_call(
        body, out_shape=SDS((s, d), BF16), grid=(s // tr,),
        in_specs=[pl.BlockSpec((tr, d), lambda i: (i, 0)), pl.BlockSpec((1, d), lambda i: (0, 0))],
        out_specs=pl.BlockSpec((tr, d), lambda i: (i, 0)),
        compiler_params=_params(("parallel",)), name="rmsnorm_fwd")(x, w)


def _rmsnorm_bwd(x, w, dhn, dout):
    s, d = x.shape
    tr = 256

    def body(x_ref, w_ref, dh_ref, do_ref, gx_ref, gw_ref):
        xv = x_ref[...]
        r = lax.rsqrt(jnp.mean(xv * xv, axis=-1, keepdims=True) + RMS_EPS)
        nrm = xv * r
        dh = dh_ref[...]
        gy = dh * w_ref[...]
        gx_ref[...] = do_ref[...] + r * (gy - nrm * jnp.mean(gy * nrm, axis=-1, keepdims=True))

        @pl.when(pl.program_id(0) == 0)
        def _():
            gw_ref[...] = jnp.zeros_like(gw_ref)

        gw_ref[...] += jnp.sum(dh * nrm, axis=0, keepdims=True)

    blk = pl.BlockSpec((tr, d), lambda i: (i, 0))
    row = pl.BlockSpec((1, d), lambda i: (0, 0))
    return pl.pallas_call(
        body, out_shape=(SDS((s, d), F32), SDS((1, d), F32)), grid=(s // tr,),
        in_specs=[blk, row, blk, blk], out_specs=(blk, row),
        compiler_params=_params(("arbitrary",)), name="rmsnorm_bwd")(x, w, dhn, dout)


def _attn_tables(tq):
    w = ATTN_WINDOW + tq
    i = jnp.arange(tq, dtype=jnp.int32)[:, None]
    j = jnp.arange(w, dtype=jnp.int32)[None, :]
    delta = i + ATTN_WINDOW - j
    n = jnp.zeros((tq, w), F32)
    for window, dil in DILATED_PATTERNS:
        n = n + ((delta >= 0) & (delta <= window) & (delta % dil == 0)).astype(F32)
    logn = jnp.where(n > 0, jnp.log(jnp.maximum(n, 1.0)), NEG)
    return logn, jnp.maximum(delta, 0).astype(F32)


def _slopes(h):
    s = jnp.asarray([2.0 ** (-8.0 * (i + 1) / h) for i in range(h)], F32)
    return jnp.broadcast_to(s[:, None, None], (h, 1, LANES))


def _attn_scores(q_ref, k_ref, logn_ref, dist_ref, slope_ref, start, w):
    tq = q_ref.shape[0]
    kw = k_ref[pl.ds(start, w), :]
    s = _nt(q_ref[...], kw) * (ATTN_HEAD_DIM ** -0.5)
    slope = slope_ref[0:1, 0:1]
    s = s + (logn_ref[...] - slope * dist_ref[...])
    col = lax.broadcasted_iota(jnp.int32, (tq, w), 1)
    s = jnp.where(col >= ATTN_WINDOW - start, s, NEG)
    m = jnp.max(s, axis=1, keepdims=True)
    p = jnp.exp(s - m)
    l = jnp.sum(p, axis=1, keepdims=True)
    return p, l


def _attn_fwd(cfg, proj, kpad, vpad, logn, dist, slopes):
    s, h = cfg.S, cfg.H
    tq = logn.shape[0]
    w = ATTN_WINDOW + tq
    sp = s + ATTN_WINDOW

    def body(q_ref, z_ref, k_ref, v_ref, logn_ref, dist_ref, slope_ref, o_ref, og_ref):
        start = pl.multiple_of(pl.program_id(1) * tq, tq)
        p, l = _attn_scores(q_ref, k_ref, logn_ref, dist_ref, slope_ref, start, w)
        o = _nn(p.astype(BF16), v_ref[pl.ds(start, w), :]) / l
        z = z_ref[...].astype(F32)
        o_ref[...] = o.astype(BF16)
        og_ref[...] = (o * (z * _sigmoid(z))).astype(BF16)

    qb = cfg.OQ // LANES
    zb = cfg.OZA // LANES
    blk = pl.BlockSpec((tq, LANES), lambda hh, i: (i, hh))
    return pl.pallas_call(
        body, out_shape=(SDS((s, cfg.D), BF16), SDS((s, cfg.D), BF16)), grid=(h, s // tq),
        in_specs=[pl.BlockSpec((tq, LANES), lambda hh, i: (i, qb + hh)),
                  pl.BlockSpec((tq, LANES), lambda hh, i: (i, zb + hh)),
                  pl.BlockSpec((sp, LANES), lambda hh, i: (0, hh)),
                  pl.BlockSpec((sp, LANES), lambda hh, i: (0, hh)),
                  pl.BlockSpec((tq, w), lambda hh, i: (0, 0)),
                  pl.BlockSpec((tq, w), lambda hh, i: (0, 0)),
                  pl.BlockSpec((None, 1, LANES), lambda hh, i: (hh, 0, 0))],
        out_specs=(blk, blk),
        compiler_params=_params(("parallel", "parallel")), name="attn_fwd")(proj, proj, kpad, vpad, logn, dist, slopes)


def _attn_bwd(cfg, proj, kpad, vpad, o_a, doag, logn, dist, slopes):
    s, h = cfg.S, cfg.H
    tq = logn.shape[0]
    w = ATTN_WINDOW + tq
    sp = s + ATTN_WINDOW
    scale = ATTN_HEAD_DIM ** -0.5

    def body(q_ref, z_ref, k_ref, v_ref, o_ref, dg_ref, logn_ref, dist_ref, slope_ref,
             dq_ref, dz_ref, dk_ref, dv_ref):
        i = pl.program_id(1)
        start = pl.multiple_of(i * tq, tq)

        @pl.when(i == 0)
        def _():
            dk_ref[...] = jnp.zeros_like(dk_ref)
            dv_ref[...] = jnp.zeros_like(dv_ref)

        p, l = _attn_scores(q_ref, k_ref, logn_ref, dist_ref, slope_ref, start, w)
        p = p / l
        z = z_ref[...].astype(F32)
        sg = _sigmoid(z)
        o = o_ref[...].astype(F32)
        dg = dg_ref[...].astype(F32)
        do = dg * (z * sg)
        dz_ref[...] = (dg * o * (sg * (1.0 + z * (1.0 - sg)))).astype(BF16)
        delta = jnp.sum(do * o, axis=1, keepdims=True)
        dob = do.astype(BF16)
        dp = _nt(dob, v_ref[pl.ds(start, w), :])
        ds = (p * (dp - delta) * scale).astype(BF16)
        dq_ref[...] = _nn(ds, k_ref[pl.ds(start, w), :]).astype(BF16)
        dk_ref[pl.ds(start, w), :] += _tn(ds, q_ref[...])
        dv_ref[pl.ds(start, w), :] += _tn(p.astype(BF16), dob)

    qb = cfg.OQ // LANES
    zb = cfg.OZA // LANES
    blk = pl.BlockSpec((tq, LANES), lambda hh, i: (i, hh))
    full = pl.BlockSpec((sp, LANES), lambda hh, i: (0, hh))
    tab = pl.BlockSpec((tq, w), lambda hh, i: (0, 0))
    return pl.pallas_call(
        body,
        out_shape=(SDS((s, cfg.D), BF16), SDS((s, cfg.D), BF16), SDS((sp, cfg.D), F32), SDS((sp, cfg.D), F32)),
        grid=(h, s // tq),
        in_specs=[pl.BlockSpec((tq, LANES), lambda hh, i: (i, qb + hh)),
                  pl.BlockSpec((tq, LANES), lambda hh, i: (i, zb + hh)),
                  full, full, blk, blk, tab, tab,
                  pl.BlockSpec((None, 1, LANES), lambda hh, i: (hh, 0, 0))],
        out_specs=(blk, blk, full, full),
        compiler_params=_params(("parallel", "arbitrary")), name="attn_bwd")(
            proj, proj, kpad, vpad, o_a, doag, logn, dist, slopes)


CONV_HALO = 16
CONV_TR = 512
CONV_CW = 512


def _conv_fwd(cfg, proj, conv_w, conv_b):
    s, cd = cfg.S, cfg.CD
    tr, cw, hl = CONV_TR, CONV_CW, CONV_HALO
    cb0 = cfg.OXBC // cw

    def body(x_ref, h_ref, w_ref, b_ref, o_ref, scr):
        i = pl.program_id(0)
        scr[pl.ds(0, hl), :] = jnp.where(i > 0, h_ref[...].astype(F32), 0.0)
        scr[pl.ds(hl, tr), :] = x_ref[...].astype(F32)
        pre = b_ref[...] + jnp.zeros((tr, cw), F32)
        for k in range(CONV_K):
            pre = pre + w_ref[k:k + 1, :] * scr[pl.ds(hl - (CONV_K - 1) + k, tr), :]
        o_ref[...] = (pre * _sigmoid(pre)).astype(BF16)

    return pl.pallas_call(
        body, out_shape=SDS((s, cd), BF16), grid=(s // tr, cd // cw),
        in_specs=[pl.BlockSpec((tr, cw), lambda i, j: (i, cb0 + j)),
                  pl.BlockSpec((hl, cw), lambda i, j: (jnp.maximum(i * (tr // hl) - 1, 0), cb0 + j)),
                  pl.BlockSpec((CONV_K, cw), lambda i, j: (0, j)),
                  pl.BlockSpec((1, cw), lambda i, j: (0, j))],
        out_specs=pl.BlockSpec((tr, cw), lambda i, j: (i, j)),
        scratch_shapes=[pltpu.VMEM((tr + hl, cw), F32)],
        compiler_params=_params(("parallel", "parallel")), name="conv_fwd")(proj, proj, conv_w, conv_b)


def _conv_bwd(cfg, proj, dact, conv_w, conv_b, dproj):
    s, cd = cfg.S, cfg.CD
    tr, cw, hl = CONV_TR, CONV_CW, CONV_HALO
    cb0 = cfg.OXBC // cw
    nr = s // tr
    last_h = s // hl - 1

    def body(x_ref, hp_ref, hn_ref, d_ref, dn_ref, w_ref, b_ref, dp_in, dx_ref, gw_ref, gb_ref, xs, ds):
        del dp_in
        i = pl.program_id(1)
        xs[pl.ds(0, hl), :] = jnp.where(i > 0, hp_ref[...].astype(F32), 0.0)
        xs[pl.ds(hl, tr), :] = x_ref[...].astype(F32)
        xs[pl.ds(hl + tr, hl), :] = hn_ref[...].astype(F32)
        pre = b_ref[...] + jnp.zeros((tr + hl, cw), F32)
        for k in range(CONV_K):
            pre = pre + w_ref[k:k + 1, :] * xs[pl.ds(hl - (CONV_K - 1) + k, tr + hl), :]
        sg = _sigmoid(pre)
        dsilu = sg * (1.0 + pre * (1.0 - sg))
        ds[pl.ds(0, tr), :] = d_ref[...].astype(F32) * dsilu[0:tr]
        ds[pl.ds(tr, hl), :] = jnp.where(i < nr - 1, dn_ref[...].astype(F32), 0.0) * dsilu[tr:tr + hl]
        dx = jnp.zeros((tr, cw), F32)
        for k in range(CONV_K):
            dx = dx + w_ref[k:k + 1, :] * ds[pl.ds(CONV_K - 1 - k, tr), :]
        dx_ref[...] = dx.astype(BF16)

        @pl.when(i == 0)
        def _():
            gw_ref[...] = jnp.zeros_like(gw_ref)
            gb_ref[...] = jnp.zeros_like(gb_ref)

        dcur = ds[pl.ds(0, tr), :]
        gb_ref[...] += jnp.sum(dcur, axis=0, keepdims=True)
        for k in range(CONV_K):
            gw_ref[k:k + 1, :] += jnp.sum(dcur * xs[pl.ds(hl - (CONV_K - 1) + k, tr), :], axis=0, keepdims=True)

    return pl.pallas_call(
        body, out_shape=(SDS(dproj.shape, BF16), SDS((CONV_K, cd), F32), SDS((1, cd), F32)), grid=(cd // cw, nr),
        in_specs=[pl.BlockSpec((tr, cw), lambda j, i: (i, cb0 + j)),
                  pl.BlockSpec((hl, cw), lambda j, i: (jnp.maximum(i * (tr // hl) - 1, 0), cb0 + j)),
                  pl.BlockSpec((hl, cw), lambda j, i: (jnp.minimum((i + 1) * (tr // hl), last_h), cb0 + j)),
                  pl.BlockSpec((tr, cw), lambda j, i: (i, j)),
                  pl.BlockSpec((hl, cw), lambda j, i: (jnp.minimum((i + 1) * (tr // hl), last_h), j)),
                  pl.BlockSpec((CONV_K, cw), lambda j, i: (0, j)),
                  pl.BlockSpec((1, cw), lambda j, i: (0, j)),
                  pl.BlockSpec(memory_space=pl.ANY)],
        out_specs=(pl.BlockSpec((tr, cw), lambda j, i: (i, cb0 + j)),
                   pl.BlockSpec((CONV_K, cw), lambda j, i: (0, j)),
                   pl.BlockSpec((1, cw), lambda j, i: (0, j))),
        scratch_shapes=[pltpu.VMEM((tr + 2 * hl, cw), F32), pltpu.VMEM((tr + hl, cw), F32)],
        input_output_aliases={7: 0},
        compiler_params=_params(("parallel", "arbitrary")), name="conv_bwd")(
            proj, proj, proj, dact, dact, conv_w, conv_b, dproj)


def _expand(v, e, terms):
    out, rem = None, v
    for _ in range(terms):
        hi = rem.astype(BF16)
        t = _nn(hi, e)
        out = t if out is None else out + t
        rem = rem - hi.astype(F32)
    return out


def _segsum(v, e, terms):
    out, rem = None, v
    for _ in range(terms):
        hi = rem.astype(BF16)
        t = _nt(hi, e)
        out = t if out is None else out + t
        rem = rem - hi.astype(F32)
    return out


def _expand_row(row, e, terms):
    return _expand(jnp.broadcast_to(row, (8, LANES)), e, terms)[0:1]


def _segsum_row(row, e, terms):
    return _segsum(jnp.broadcast_to(row, (8, row.shape[1])), e, terms)[0:1]


def _expansion_matrix(cfg):
    hh = jnp.arange(LANES, dtype=jnp.int32)[:, None]
    cc = jnp.arange(cfg.SI, dtype=jnp.int32)[None, :]
    return (cc // SSM_HEAD_DIM == hh).astype(BF16)


def _tri(lower):
    r = lax.broadcasted_iota(jnp.int32, (CHUNK, CHUNK), 0)
    c = lax.broadcasted_iota(jnp.int32, (CHUNK, CHUNK), 1)
    return (c <= r) if lower else (c >= r)


def _ssd_prep(dtr_ref, db_ref, al_ref, e):
    dtr = dtr_ref[...] + db_ref[...]
    dt = _softplus(dtr)
    a = -jnp.exp(al_ref[...])
    acum = jnp.dot(_tri(True).astype(F32), dt * a, precision=lax.Precision.HIGHEST, preferred_element_type=F32)
    return dtr, dt, a, _expand(dt, e, 2), _expand(acum, e, 3)


def _ssd_fwd(cfg, xact, dt_raw, proj, dt_bias, a_log, d_skip, norm_w, e):
    s, si, cd, gw, bc = cfg.S, cfg.SI, cfg.CD, cfg.GW, cfg.BC
    nc = s // CHUNK
    zb = cfg.OZS // si
    tiles = gw // LANES

    def body(xa_ref, dtr_ref, z_ref, db_ref, al_ref, dsk_ref, nw_ref, e_ref, y_ref, y2_ref, st_ref,
             state, ybuf, x_s, xw_s, ae_s, ea_s, lam_s):
        @pl.when(pl.program_id(0) == 0)
        def _():
            state[...] = jnp.zeros_like(state)

        st_ref[...] = state[...]
        ev = e_ref[...]
        _, _, _, dt_e, a_e = _ssd_prep(dtr_ref, db_ref, al_ref, ev)
        xs = xa_ref[:, 0:si].astype(F32)
        x = xs * dt_e
        lam_e = a_e[CHUNK - 1:CHUNK, :]
        x_s[...] = x.astype(BF16)
        xw_s[...] = (x * jnp.exp(lam_e - a_e)).astype(BF16)
        ae_s[...] = a_e
        ea_s[...] = jnp.exp(a_e)
        ybuf[...] = _expand_row(dsk_ref[...], ev, 3) * xs
        lam_s[...] = jnp.broadcast_to(jnp.exp(lam_e), (8, si))
        tril = _tri(True)
        lane = lax.broadcasted_iota(jnp.int32, (CHUNK, LANES), 1)

        def group(g, carry):
            co = pl.multiple_of(g * gw, LANES)
            bg = xa_ref[:, pl.ds(pl.multiple_of(si + g * SSM_STATE, LANES), SSM_STATE)]
            cg = xa_ref[:, pl.ds(pl.multiple_of(si + bc + g * SSM_STATE, LANES), SSM_STATE)]
            cbm = _nt(cg, bg)
            st = state[:, pl.ds(co, gw)]
            yoff = _nn(cg, st.astype(BF16)) * ea_s[:, pl.ds(co, gw)]
            for k in range(tiles):
                tc = pl.multiple_of(co + k * LANES, LANES)
                at = ae_s[:, pl.ds(tc, LANES)]
                att = at.T
                xt = x_s[:, pl.ds(tc, LANES)]
                acc = yoff[:, k * LANES:(k + 1) * LANES]
                for half in range(2):
                    lo = half * SSM_HEAD_DIM
                    seg = at[:, lo:lo + 1] - att[lo:lo + 1, :]
                    dec = jnp.exp(jnp.where(tril, seg, NEG))
                    xh = jnp.where((lane >= lo) & (lane < lo + SSM_HEAD_DIM), xt, jnp.zeros_like(xt))
                    acc = acc + _nn((cbm * dec).astype(BF16), xh)
                ybuf[:, pl.ds(tc, LANES)] += acc
            state[:, pl.ds(co, gw)] = st * lam_s[0:1, pl.ds(co, gw)] + _tn(bg, xw_s[:, pl.ds(co, gw)])
            return carry

        lax.fori_loop(0, SSM_GROUPS, group, 0)
        y = ybuf[...]
        y_ref[...] = y.astype(BF16)
        z = z_ref[...].astype(F32)
        u = y * (z * _sigmoid(z))
        r = lax.rsqrt(jnp.mean(u * u, axis=-1, keepdims=True) + RMS_EPS)
        y2_ref[...] = (u * r * nw_ref[...]).astype(BF16)

    row = lambda n: pl.BlockSpec((1, n), lambda c: (0, 0))
    return pl.pallas_call(
        body,
        out_shape=(SDS((s, si), BF16), SDS((s, si), BF16), SDS((nc, SSM_STATE, si), F32)),
        grid=(nc,),
        in_specs=[pl.BlockSpec((CHUNK, cd), lambda c: (c, 0)),
                  pl.BlockSpec((CHUNK, LANES), lambda c: (c, 0)),
                  pl.BlockSpec((CHUNK, si), lambda c: (c, zb)),
                  row(LANES), row(LANES), row(LANES), row(si),
                  pl.BlockSpec((LANES, si), lambda c: (0, 0))],
        out_specs=(pl.BlockSpec((CHUNK, si), lambda c: (c, 0)),
                   pl.BlockSpec((CHUNK, si), lambda c: (c, 0)),
                   pl.BlockSpec((None, SSM_STATE, si), lambda c: (c, 0, 0))),
        scratch_shapes=[pltpu.VMEM((SSM_STATE, si), F32), pltpu.VMEM((CHUNK, si), F32),
                        pltpu.VMEM((CHUNK, si), BF16), pltpu.VMEM((CHUNK, si), BF16),
                        pltpu.VMEM((CHUNK, si), F32), pltpu.VMEM((CHUNK, si), F32),
                        pltpu.VMEM((8, si), F32)],
        compiler_params=_params(("arbitrary",)), name="ssd_fwd")(
            xact, dt_raw, proj, dt_bias, a_log, d_skip, norm_w, e)


def _ssd_bwd(cfg, xact, dt_raw, proj, y, dy2, states, dt_bias, a_log, d_skip, norm_w, e, dproj):
    s, si, cd, gw, bc, hpg = cfg.S, cfg.SI, cfg.CD, cfg.GW, cfg.BC, cfg.HPG
    nc = s // CHUNK
    zb = cfg.OZS // si
    tiles = gw // LANES

    def body(xa_ref, dtr_ref, z_ref, y_ref, d2_ref, st_ref, db_ref, al_ref, dsk_ref, nw_ref, e_ref, dp_in,
             dz_ref, dxa_ref, ddt_ref, gnw_ref, gdb_ref, gal_ref, gds_ref,
             dh, dhn, xs_s, x_s, w_s, ae_s, ea_s, g_s, dx_s, dae_s, r_s, lam_s, dle_s):
        del dp_in

        @pl.when(pl.program_id(0) == 0)
        def _():
            dh[...] = jnp.zeros_like(dh)
            gnw_ref[...] = jnp.zeros_like(gnw_ref)
            gdb_ref[...] = jnp.zeros_like(gdb_ref)
            gal_ref[...] = jnp.zeros_like(gal_ref)
            gds_ref[...] = jnp.zeros_like(gds_ref)

        ev = e_ref[...]
        yv = y_ref[...].astype(F32)
        z = z_ref[...].astype(F32)
        sg = _sigmoid(z)
        sz = z * sg
        u = yv * sz
        r = lax.rsqrt(jnp.mean(u * u, axis=-1, keepdims=True) + RMS_EPS)
        nrm = u * r
        d2 = d2_ref[...].astype(F32)
        gnw_ref[...] += jnp.sum(d2 * nrm, axis=0, keepdims=True)
        gn = d2 * nw_ref[...]
        du = r * (gn - nrm * jnp.mean(gn * nrm, axis=-1, keepdims=True))
        gv = du * sz
        dz_ref[...] = (du * yv * (sg * (1.0 + z * (1.0 - sg)))).astype(BF16)
        g_s[...] = gv

        dtr, dt, a, dt_e, a_e = _ssd_prep(dtr_ref, db_ref, al_ref, ev)
        xs = xa_ref[:, 0:si].astype(F32)
        x = xs * dt_e
        lam_e = a_e[CHUNK - 1:CHUNK, :]
        xs_s[...] = xs
        x_s[...] = x
        w_s[...] = jnp.exp(lam_e - a_e)
        ae_s[...] = a_e
        ea_s[...] = jnp.exp(a_e)
        lam_s[...] = jnp.broadcast_to(jnp.exp(lam_e), (8, si))
        gds_ref[...] += _segsum_row(jnp.sum(gv * xs, axis=0, keepdims=True), ev, 2)
        r_s[...] = jnp.zeros_like(r_s)
        tril = _tri(True)
        lane = lax.broadcasted_iota(jnp.int32, (CHUNK, LANES), 1)
        sub = lax.broadcasted_iota(jnp.int32, (CHUNK, LANES), 0)

        def group(g, carry):
            co = pl.multiple_of(g * gw, LANES)
            bo = pl.multiple_of(si + g * SSM_STATE, LANES)
            cof = pl.multiple_of(si + bc + g * SSM_STATE, LANES)
            cols = pl.ds(co, gw)
            bg = xa_ref[:, pl.ds(bo, SSM_STATE)]
            cg = xa_ref[:, pl.ds(cof, SSM_STATE)]
            cbm = _nt(cg, bg)
            st = st_ref[:, cols]
            stb = st.astype(BF16)
            dho = dh[:, cols]
            dhob = dho.astype(BF16)
            ea = ea_s[:, cols]
            gg = g_s[:, cols]
            xg = x_s[:, cols]
            wg = w_s[:, cols]
            explam = lam_s[0:1, cols]
            yoff = _nn(cg, stb) * ea
            ga = (gg * ea).astype(BF16)
            dc = _nt(ga, stb)
            dhn[:, cols] = dho * explam + _tn(cg, ga)
            bdh = _nn(bg, dhob)
            db = _nt((xg * wg).astype(BF16), dhob)
            t = xg * bdh * wg
            dle_s[0:1, cols] = jnp.sum(t, axis=0, keepdims=True) + explam * jnp.sum(dho * st, axis=0, keepdims=True)
            dae_base = gg * yoff - t
            dxw = wg * bdh
            dcb = jnp.zeros((CHUNK, CHUNK), F32)
            for k in range(tiles):
                tc = pl.multiple_of(co + k * LANES, LANES)
                ksl = slice(k * LANES, (k + 1) * LANES)
                at = ae_s[:, pl.ds(tc, LANES)]
                att = at.T
                xt = xg[:, ksl].astype(BF16)
                gt = gg[:, ksl].astype(BF16)
                dxt = dxw[:, ksl]
                place = jnp.zeros((CHUNK, LANES), F32)
                for half in range(2):
                    lo = half * SSM_HEAD_DIM
                    seg = at[:, lo:lo + 1] - att[lo:lo + 1, :]
                    dec = jnp.exp(jnp.where(tril, seg, NEG))
                    mh = cbm * dec
                    gh = jnp.where((lane >= lo) & (lane < lo + SSM_HEAD_DIM), gt, jnp.zeros_like(gt))
                    dm = _nt(gh, xt)
                    dxt = dxt + _tn(mh.astype(BF16), gh)
                    dcb = dcb + dm * dec
                    dseg = dm * mh
                    place = place + jnp.where(lane == lo, jnp.sum(dseg, axis=1, keepdims=True), 0.0)
                    hidx = g * hpg + 2 * k + half
                    r_s[...] += jnp.where(sub == hidx, jnp.sum(dseg, axis=0, keepdims=True), 0.0)
                dx_s[:, pl.ds(tc, LANES)] = dxt
                dae_s[:, pl.ds(tc, LANES)] = dae_base[:, ksl] + place
            dcbb = dcb.astype(BF16)
            dxa_ref[:, pl.ds(bo, SSM_STATE)] = (db + _tn(dcbb, cg)).astype(BF16)
            dxa_ref[:, pl.ds(cof, SSM_STATE)] = (dc + _nn(dcbb, bg)).astype(BF16)
            return carry

        lax.fori_loop(0, SSM_GROUPS, group, 0)
        dlam = _segsum_row(dle_s[0:1, :], ev, 2)
        da_ = _segsum(dae_s[...], ev, 2) - r_s[...].T
        da_ = da_ + jnp.where(sub == CHUNK - 1, dlam, 0.0)
        dda = jnp.dot(_tri(False).astype(F32), da_, precision=lax.Precision.HIGHEST, preferred_element_type=F32)
        dxv = dx_s[...]
        xs = xs_s[...]
        ddt = dda * a + _segsum(dxv * xs, ev, 2)
        gal_ref[...] += jnp.sum(dda * dt, axis=0, keepdims=True) * a
        ddtr = ddt * _sigmoid(dtr)
        gdb_ref[...] += jnp.sum(ddtr, axis=0, keepdims=True)
        ddt_ref[...] = ddtr
        dxa_ref[:, 0:si] = (dxv * dt_e + g_s[...] * _expand_row(dsk_ref[...], ev, 3)).astype(BF16)
        dh[...] = dhn[...]

    rev = lambda c: nc - 1 - c
    row = lambda n: pl.BlockSpec((1, n), lambda c: (0, 0))
    big = lambda: pltpu.VMEM((CHUNK, si), F32)
    return pl.pallas_call(
        body,
        out_shape=(SDS(dproj.shape, BF16), SDS((s, cd), BF16), SDS((s, LANES), F32),
                   SDS((1, si), F32), SDS((1, LANES), F32), SDS((1, LANES), F32), SDS((1, LANES), F32)),
        grid=(nc,),
        in_specs=[pl.BlockSpec((CHUNK, cd), lambda c: (rev(c), 0)),
                  pl.BlockSpec((CHUNK, LANES), lambda c: (rev(c), 0)),
                  pl.BlockSpec((CHUNK, si), lambda c: (rev(c), zb)),
                  pl.BlockSpec((CHUNK, si), lambda c: (rev(c), 0)),
                  pl.BlockSpec((CHUNK, si), lambda c: (rev(c), 0)),
                  pl.BlockSpec((None, SSM_STATE, si), lambda c: (rev(c), 0, 0)),
                  row(LANES), row(LANES), row(LANES), row(si),
                  pl.BlockSpec((LANES, si), lambda c: (0, 0)),
                  pl.BlockSpec(memory_space=pl.ANY)],
        out_specs=(pl.BlockSpec((CHUNK, si), lambda c: (rev(c), zb)),
                   pl.BlockSpec((CHUNK, cd), lambda c: (rev(c), 0)),
                   pl.BlockSpec((CHUNK, LANES), lambda c: (rev(c), 0)),
                   row(si), row(LANES), row(LANES), row(LANES)),
        scratch_shapes=[pltpu.VMEM((SSM_STATE, si), F32), pltpu.VMEM((SSM_STATE, si), F32),
                        big(), big(), big(), big(), big(), big(), big(), big(),
                        pltpu.VMEM((CHUNK, LANES), F32), pltpu.VMEM((8, si), F32), pltpu.VMEM((8, si), F32)],
        input_output_aliases={11: 0},
        compiler_params=_params(("arbitrary",)), name="ssd_bwd")(
            xact, dt_raw, proj, y, dy2, states, dt_bias, a_log, d_skip, norm_w, e, dproj)


MERGE_TR = 512
MERGE_CW = 512


def _merge_fwd(cfg, proj, a_br, s_br):
    s, d = cfg.S, cfg.D
    tr, cw = MERGE_TR, MERGE_CW
    ga0, gs0 = cfg.OGA // cw, cfg.OGS // cw

    def body(ga_ref, gs_ref, a_ref, s_ref, o_ref):
        o_ref[...] = (_sigmoid(ga_ref[...].astype(F32)) * a_ref[...].astype(F32)
                      + _sigmoid(gs_ref[...].astype(F32)) * s_ref[...].astype(F32)).astype(BF16)

    blk = pl.BlockSpec((tr, cw), lambda i, j: (i, j))
    return pl.pallas_call(
        body, out_shape=SDS((s, d), BF16), grid=(s // tr, d // cw),
        in_specs=[pl.BlockSpec((tr, cw), lambda i, j: (i, ga0 + j)),
                  pl.BlockSpec((tr, cw), lambda i, j: (i, gs0 + j)), blk, blk],
        out_specs=blk, compiler_params=_params(("parallel", "parallel")), name="merge_fwd")(proj, proj, a_br, s_br)


def _merge_bwd(cfg, proj, branch, dmerged, gate_off, dproj, name):
    s, d = cfg.S, cfg.D
    tr, cw = MERGE_TR, MERGE_CW
    g0 = gate_off // cw
    fresh = dproj is None

    def body(*refs):
        g_ref, b_ref, dm_ref = refs[:3]
        dg_ref, db_ref = refs[-2:]
        dm = dm_ref[...].astype(F32)
        sg = _sigmoid(g_ref[...].astype(F32))
        db_ref[...] = (dm * sg).astype(BF16)
        dg_ref[...] = (dm * b_ref[...].astype(F32) * sg * (1.0 - sg)).astype(BF16)

    blk = pl.BlockSpec((tr, cw), lambda i, j: (i, j))
    gate = pl.BlockSpec((tr, cw), lambda i, j: (i, g0 + j))
    return pl.pallas_call(
        body, out_shape=(SDS((s, cfg.NM), BF16), SDS((s, d), BF16)), grid=(s // tr, d // cw),
        in_specs=[gate, blk, blk] + ([] if fresh else [HBM_SPEC]),
        out_specs=(gate, blk),
        input_output_aliases={} if fresh else {3: 0},
        compiler_params=_params(("parallel", "parallel")), name=name)(
            *((proj, branch, dmerged) + (() if fresh else (dproj,))))


def _outproj_loss(merged, w_out, x, target, fnw):
    s, d = x.shape
    tr = 256

    def body(m_ref, w_ref, x_ref, t_ref, fw_ref, dof_ref, dob_ref, loss_ref, g_ref):
        out = x_ref[...] + _nn(m_ref[...], w_ref[...])
        r = lax.rsqrt(jnp.mean(out * out, axis=-1, keepdims=True) + RMS_EPS)
        nrm = out * r
        fw = fw_ref[...]
        err = nrm * fw - t_ref[...]
        dy = err * (1.0 / d)
        gy = dy * fw
        dout = r * (gy - nrm * jnp.mean(gy * nrm, axis=-1, keepdims=True))
        dof_ref[...] = dout
        dob_ref[...] = dout.astype(BF16)

        @pl.when(pl.program_id(0) == 0)
        def _():
            loss_ref[...] = jnp.zeros_like(loss_ref)
            g_ref[...] = jnp.zeros_like(g_ref)

        loss_ref[...] += jnp.sum(jnp.sum(err * err, axis=1, keepdims=True), axis=0, keepdims=True) * (0.5 / d)
        g_ref[...] += jnp.sum(dy * nrm, axis=0, keepdims=True)

    blk = pl.BlockSpec((tr, d), lambda i: (i, 0))
    return pl.pallas_call(
        body, out_shape=(SDS((s, d), F32), SDS((s, d), BF16), SDS((1, LANES), F32), SDS((1, d), F32)), grid=(s // tr,),
        in_specs=[blk, pl.BlockSpec((d, d), lambda i: (0, 0)), blk, blk, pl.BlockSpec((1, d), lambda i: (0, 0))],
        out_specs=(blk, blk, pl.BlockSpec((1, LANES), lambda i: (0, 0)), pl.BlockSpec((1, d), lambda i: (0, 0))),
        compiler_params=_params(("arbitrary",)), name="outproj_loss")(merged, w_out, x, target, fnw)


ELEMWISE_BLOCK_BYTES = 1 << 20


def _row_block(rows, cols, itemsize=4):
    best = None
    for tr in range(16, rows + 1, 16):
        if rows % tr == 0 and tr * cols * itemsize <= ELEMWISE_BLOCK_BYTES:
            best = tr
    return best if best is not None else rows


def _adamw(w, g, m, v, name):
    rows, cols = w.shape
    tr = _row_block(rows, cols)

    def body(w_ref, g_ref, m_ref, v_ref, d_ref, nm_ref, nv_ref):
        gv = g_ref[...]
        nm = ADAM_B1 * m_ref[...] + (1.0 - ADAM_B1) * gv
        nv = ADAM_B2 * v_ref[...] + (1.0 - ADAM_B2) * jnp.square(gv)
        m_hat = nm / (1.0 - ADAM_B1 ** ADAM_STEP)
        v_hat = nv / (1.0 - ADAM_B2 ** ADAM_STEP)
        d_ref[...] = -ADAM_LR * (m_hat / (jnp.sqrt(v_hat) + ADAM_EPS) + ADAM_WD * w_ref[...])
        nm_ref[...] = nm
        nv_ref[...] = nv

    blk = pl.BlockSpec((tr, cols), lambda i: (i, 0))
    out = SDS((rows, cols), F32)
    return pl.pallas_call(
        body, out_shape=(out, out, out), grid=(rows // tr,), in_specs=[blk] * 4, out_specs=(blk,) * 3,
        compiler_params=_params(("parallel",)), name=name)(w, g, m, v)


HBM_SPEC = pl.BlockSpec(memory_space=pl.ANY)


def _position():
    return lax.axis_index("x"), lax.axis_index("y"), lax.axis_index("c")


def _gather_chips(shards):
    n = len(shards)

    def body(*refs):
        ins, outs = refs[:n], refs[n:2 * n]
        send_sems, recv_sems, fsend_sems, frecv_sems = refs[2 * n:]
        x, y, c = _position()
        me = 2 * x + y
        peers = [(1 - x, y), (x, 1 - y), (1 - x, 1 - y)]

        def over_ici(t, p, chip):
            px, py = peers[p]
            r2 = ins[t].shape[0] // 2
            return pltpu.make_async_remote_copy(
                src_ref=ins[t].at[pl.ds(c * r2, r2), :], dst_ref=outs[t].at[chip, c], send_sem=send_sems.at[3 * t + p],
                recv_sem=recv_sems.at[3 * t + p], device_id=(px, py, c), device_id_type=MESH)

        def to_sibling(t, p, half):
            px, py = peers[p]
            slab = outs[t].at[2 * px + py, half]
            return pltpu.make_async_remote_copy(
                src_ref=slab, dst_ref=slab, send_sem=fsend_sems.at[3 * t + p], recv_sem=frecv_sems.at[3 * t + p],
                device_id=(x, y, 1 - c), device_id_type=MESH)

        sends = [over_ici(t, p, me) for t in range(n) for p in range(3)]
        for cp in sends:
            cp.start()
        passed = []
        for t in range(n):
            for p, (px, py) in enumerate(peers):
                over_ici(t, p, 2 * px + py).wait_recv()
                passed.append(to_sibling(t, p, c))
                passed[-1].start()
        for t in range(n):
            for p in range(3):
                to_sibling(t, p, 1 - c).wait_recv()
        for cp in sends + passed:
            cp.wait_send()

    return pl.pallas_call(
        body, out_shape=[SDS((N_CHIPS, 2, a.shape[0] // 2, a.shape[1]), a.dtype) for a in shards],
        in_specs=[HBM_SPEC] * n, out_specs=[HBM_SPEC] * n,
        scratch_shapes=[pltpu.SemaphoreType.DMA((3 * n,))] * 4,
        compiler_params=pltpu.CompilerParams(has_side_effects=True), name="gather_weights")(*shards)


def _with_own(gathered, own, chip):
    full = gathered.reshape((N_CHIPS,) + own.shape)
    return lax.dynamic_update_index_in_dim(full, own, chip, 0)


def _exchange_halves(grads):
    n = len(grads)

    def body(*refs):
        ins, outs = refs[:n], refs[n:2 * n]
        send_sems, recv_sems = refs[2 * n:]
        x, y, c = _position()
        cps = []
        for t in range(n):
            r2 = ins[t].shape[1] // 2
            cps.append(pltpu.make_async_remote_copy(
                src_ref=ins[t].at[:, pl.ds((1 - c) * r2, r2), :], dst_ref=outs[t],
                send_sem=send_sems.at[t], recv_sem=recv_sems.at[t], device_id=(x, y, 1 - c), device_id_type=MESH))
        for cp in cps:
            cp.start()
        for cp in cps:
            cp.wait()

    return pl.pallas_call(
        body, out_shape=[SDS((a.shape[0], a.shape[1] // 2, a.shape[2]), a.dtype) for a in grads],
        in_specs=[HBM_SPEC] * n, out_specs=[HBM_SPEC] * n,
        scratch_shapes=[pltpu.SemaphoreType.DMA((n,)), pltpu.SemaphoreType.DMA((n,))],
        compiler_params=pltpu.CompilerParams(has_side_effects=True), name="reduce_sibling")(*grads)


def _scatter_chips(parts):
    n = len(parts)

    def body(*refs):
        ins, outs = refs[:n], refs[n:2 * n]
        send_sems, recv_sems = refs[2 * n:]
        x, y, c = _position()
        me = 2 * x + y
        peers = [(1 - x, y), (x, 1 - y), (1 - x, 1 - y)]

        def remote(t, p, src_slab, dst_slab):
            px, py = peers[p]
            return pltpu.make_async_remote_copy(
                src_ref=ins[t].at[src_slab], dst_ref=outs[t].at[dst_slab], send_sem=send_sems.at[3 * t + p],
                recv_sem=recv_sems.at[3 * t + p], device_id=(px, py, c), device_id_type=MESH)

        sends = [remote(t, p, 2 * peers[p][0] + peers[p][1], me) for t in range(n) for p in range(3)]
        for cp in sends:
            cp.start()
        for t in range(n):
            for p, (px, py) in enumerate(peers):
                remote(t, p, me, 2 * px + py).wait_recv()
        for cp in sends:
            cp.wait_send()

    return pl.pallas_call(
        body, out_shape=[SDS(a.shape, a.dtype) for a in parts],
        in_specs=[HBM_SPEC] * n, out_specs=[HBM_SPEC] * n,
        scratch_shapes=[pltpu.SemaphoreType.DMA((3 * n,)), pltpu.SemaphoreType.DMA((3 * n,))],
        compiler_params=pltpu.CompilerParams(has_side_effects=True), name="reduce_chips")(*parts)


def _share_halves(halves):
    n = len(halves)

    def body(*refs):
        ins, outs = refs[:n], refs[n:2 * n]
        send_sems, recv_sems = refs[2 * n:]
        x, y, c = _position()

        def copy(t, slab):
            return pltpu.make_async_remote_copy(
                src_ref=ins[t].at[slab], dst_ref=outs[t].at[slab], send_sem=send_sems.at[t], recv_sem=recv_sems.at[t],
                device_id=(x, y, 1 - c), device_id_type=MESH)

        for t in range(n):
            copy(t, c).start()
        for t in range(n):
            copy(t, 1 - c).wait_recv()
        for t in range(n):
            copy(t, c).wait_send()

    return pl.pallas_call(
        body, out_shape=[SDS(a.shape, a.dtype) for a in halves],
        in_specs=[HBM_SPEC] * n, out_specs=[HBM_SPEC] * n,
        scratch_shapes=[pltpu.SemaphoreType.DMA((n,)), pltpu.SemaphoreType.DMA((n,))],
        input_output_aliases={t: t for t in range(n)},
        compiler_params=pltpu.CompilerParams(has_side_effects=True), name="share_sibling")(*halves)


def _add_sibling(grad, recv, core):
    nch, r2, cols = recv.shape
    tr = _row_block(r2, cols)
    nb = r2 // tr

    def body(c_ref, g_ref, r_ref, o_ref):
        del c_ref
        o_ref[...] = (g_ref[...] + r_ref[...]).astype(BF16)

    return pl.pallas_call(
        body, out_shape=SDS(recv.shape, BF16),
        grid_spec=pltpu.PrefetchScalarGridSpec(
            num_scalar_prefetch=1, grid=(nch, nb),
            in_specs=[pl.BlockSpec((None, tr, cols), lambda j, i, c_ref: (j, c_ref[0] * nb + i, 0)),
                      pl.BlockSpec((None, tr, cols), lambda j, i, c_ref: (j, i, 0))],
            out_specs=pl.BlockSpec((None, tr, cols), lambda j, i, c_ref: (j, i, 0))),
        compiler_params=_params(("parallel", "parallel")), name="add_sibling")(core, grad, recv)


def _add_chips(own, recv, chip_core):
    nch, r2, cols = recv.shape
    tr = _row_block(r2, cols)

    def body(cc_ref, own_ref, *refs):
        p_refs, o_ref = refs[:nch], refs[nch]
        me = cc_ref[0]
        acc = None
        for j in range(nch):
            term = jnp.where(me == j, own_ref[...], p_refs[j][...]).astype(F32)
            acc = term if acc is None else acc + term
        o_ref[...] = acc

    def slab(j):
        return pl.BlockSpec((None, tr, cols), lambda i, cc: (jnp.where(cc[0] == j, (j + 1) % nch, j), i, 0))

    return pl.pallas_call(
        body, out_shape=SDS((2, r2, cols), F32),
        grid_spec=pltpu.PrefetchScalarGridSpec(
            num_scalar_prefetch=1, grid=(r2 // tr,),
            in_specs=[pl.BlockSpec((None, tr, cols), lambda i, cc: (cc[0], i, 0))] + [slab(j) for j in range(nch)],
            out_specs=pl.BlockSpec((None, tr, cols), lambda i, cc: (cc[1], i, 0))),
        compiler_params=_params(("parallel",)), name="add_chips")(chip_core, own, *([recv] * nch))


def _allreduce_small(pack):
    rows = pack.shape[0]

    def body(p_ref, o_ref, buf, send_sems, recv_sems):
        x, y, c = _position()
        me = 4 * x + 2 * y + c
        buf[me] = p_ref[...]

        def copy(dst_dev, slot):
            return pltpu.make_async_remote_copy(
                src_ref=p_ref, dst_ref=buf.at[slot], send_sem=send_sems.at[dst_dev], recv_sem=recv_sems.at[slot],
                device_id=(dst_dev // 4, (dst_dev // 2) % 2, dst_dev % 2), device_id_type=MESH)

        for dev in range(N_DEV):
            @pl.when(dev != me)
            def _():
                copy(dev, me).start()
        for dev in range(N_DEV):
            @pl.when(dev != me)
            def _():
                copy(dev, dev).wait_recv()
        for dev in range(N_DEV):
            @pl.when(dev != me)
            def _():
                copy(dev, me).wait_send()
        acc = buf[0]
        for dev in range(1, N_DEV):
            acc = acc + buf[dev]
        o_ref[...] = acc

    return pl.pallas_call(
        body, out_shape=SDS(pack.shape, F32),
        in_specs=[pl.BlockSpec(memory_space=pltpu.VMEM)], out_specs=pl.BlockSpec(memory_space=pltpu.VMEM),
        scratch_shapes=[pltpu.VMEM((N_DEV, rows, LANES), F32), pltpu.SemaphoreType.DMA((N_DEV,)),
                        pltpu.SemaphoreType.DMA((N_DEV,))],
        compiler_params=pltpu.CompilerParams(has_side_effects=True), name="allreduce_small")(pack)


ATTN_TQ = 256


def _local_step(cfg, x, target, w):
    d = cfg.D
    win = ATTN_WINDOW
    hn = _rmsnorm_fwd(x, w["norm_w"])
    proj = _mm(hn, w["w_main"], "nn", BF16, "proj_main")
    dt_raw = _mm(hn, w["w_dt"], "nn", F32, "proj_dt")
    logn, dist = _attn_tables(ATTN_TQ)
    slopes = _slopes(cfg.H)
    kpad = jnp.pad(proj[:, cfg.OK:cfg.OK + d], ((win, 0), (0, 0)))
    vpad = jnp.pad(proj[:, cfg.OV:cfg.OV + d], ((win, 0), (0, 0)))
    o_a, oag = _attn_fwd(cfg, proj, kpad, vpad, logn, dist, slopes)
    xact = _conv_fwd(cfg, proj, w["conv_w"], w["conv_b"])
    e = _expansion_matrix(cfg)
    y, y2, states = _ssd_fwd(cfg, xact, dt_raw, proj, w["dt_bias"], w["a_log"], w["d_skip"], w["ssm_norm_w"], e)
    a_br = _mm(oag, w["w_attn"], "nn", BF16, "branch_attn")
    s_br = _mm(y2, w["w_ssm"], "nn", BF16, "branch_ssm")
    merged = _merge_fwd(cfg, proj, a_br, s_br)
    dout_f, dout_b, loss_row, g_fnw = _outproj_loss(merged, w["w_out"], x, target, w["final_norm_w"])

    dmerged = _mm(dout_b, w["w_out"], "nt", BF16, "d_merged")
    g_w_out = _mm(merged, dout_b, "tn", F32, "g_w_out")
    dproj, da_br = _merge_bwd(cfg, proj, a_br, dmerged, cfg.OGA, None, "merge_bwd_attn")
    dproj, ds_br = _merge_bwd(cfg, proj, s_br, dmerged, cfg.OGS, dproj, "merge_bwd_ssm")
    doag = _mm(da_br, w["w_attn"], "nt", BF16, "d_oag")
    g_w_attn = _mm(oag, da_br, "tn", F32, "g_w_attn")
    dy2 = _mm(ds_br, w["w_ssm"], "nt", BF16, "d_y2")
    g_w_ssm = _mm(y2, ds_br, "tn", F32, "g_w_ssm")
    dproj, dxact, ddt, g_snw, g_dtb, g_alog, g_dsk = _ssd_bwd(
        cfg, xact, dt_raw, proj, y, dy2, states, w["dt_bias"], w["a_log"], w["d_skip"], w["ssm_norm_w"], e, dproj)
    dproj, g_cw, g_cb = _conv_bwd(cfg, proj, dxact, w["conv_w"], w["conv_b"], dproj)
    dq, dza, dkp, dvp = _attn_bwd(cfg, proj, kpad, vpad, o_a, doag, logn, dist, slopes)
    datt = jnp.concatenate([dq, dkp[win:].astype(BF16), dvp[win:].astype(BF16), dza], axis=1)
    dproj = lax.dynamic_update_slice(dproj, datt, (0, 0))
    ddt_b = ddt.astype(BF16)
    dhn = _mm(dproj, w["w_main"], "nt", F32, "d_hn", init=_mm(ddt_b, w["w_dt"], "nt", F32, "d_hn_dt"))
    g_w_main = _mm(hn, dproj, "tn", F32, "g_w_main")
    g_w_dt = _mm(hn, ddt_b, "tn", F32, "g_w_dt")
    grad_x, g_nw = _rmsnorm_bwd(x, w["norm_w"], dhn, dout_f)
    grads = dict(norm_w=g_nw, w_main=g_w_main, w_dt=g_w_dt, conv_w=g_cw, conv_b=g_cb, dt_bias=g_dtb, a_log=g_alog,
                 d_skip=g_dsk, ssm_norm_w=g_snw, w_attn=g_w_attn, w_ssm=g_w_ssm, w_out=g_w_out, final_norm_w=g_fnw)
    return loss_row, grad_x, grads


def _pad_lanes(v):
    return jnp.pad(v, ((0, 0), (0, LANES - v.shape[1])))


def _full_weights(cfg, norm_w, w_in, conv_w, conv_b, dt_bias, a_log, d_skip, ssm_norm_w, w_attn, w_ssm, w_out, fnw):
    w_main = jnp.concatenate([w_in[:, :cfg.OGA], w_in[:, cfg.OGA + cfg.NH:]], axis=1).astype(BF16)
    w_dt = _pad_lanes(w_in[:, cfg.OGA:cfg.OGA + cfg.NH]).astype(BF16)
    return dict(norm_w=norm_w, w_main=w_main, w_dt=w_dt, conv_w=conv_w, conv_b=conv_b, dt_bias=_pad_lanes(dt_bias),
                a_log=_pad_lanes(a_log), d_skip=_pad_lanes(d_skip), ssm_norm_w=ssm_norm_w, w_attn=w_attn.astype(BF16),
                w_ssm=w_ssm.astype(BF16), w_out=w_out.astype(BF16), final_norm_w=fnw)


def _grad_w_in(cfg, grads):
    return jnp.concatenate([grads["w_main"][:, :cfg.OGA], grads["w_dt"][:, :cfg.NH], grads["w_main"][:, cfg.OGA:]], axis=1)


def kernel(x, norm_w, w_in, conv_w, conv_b, dt_bias, a_log, d_skip, ssm_norm_w, w_attn_branch, w_ssm_branch, w_out, final_norm_w, loss_target, m_norm_w, m_w_in, m_conv_w, m_conv_b, m_dt_bias, m_a_log, m_d_skip, m_ssm_norm_w, m_w_attn_branch, m_w_ssm_branch, m_w_out, m_final_norm_w, v_norm_w, v_w_in, v_conv_w, v_conv_b, v_dt_bias, v_a_log, v_d_skip, v_ssm_norm_w, v_w_attn_branch, v_w_ssm_branch, v_w_out, v_final_norm_w):
    cfg = _Cfg(x.shape[1], x.shape[2])
    d, si, cd, nh = cfg.D, cfg.SI, cfg.CD, cfg.NH
    chip = 2 * lax.axis_index("x") + lax.axis_index("y")
    core = lax.axis_index("c").astype(jnp.int32).reshape(1)
    chip_core = jnp.concatenate([chip.astype(jnp.int32).reshape(1), core])

    own = [w_in[0].astype(BF16), w_attn_branch[0].astype(BF16), w_ssm_branch[0].astype(BF16), w_out[0].astype(BF16),
           conv_w[0].reshape(4 * CONV_K, -1)]
    a_in, a_attn, a_ssm, a_out, a_cw = [_with_own(g, o, chip) for g, o in zip(_gather_chips(own), own)]
    w_in_full = a_in.transpose(1, 0, 2).reshape(d, cfg.N_IN)
    conv_w_full = a_cw.reshape(N_CHIPS, CONV_K, cd // N_CHIPS).transpose(1, 0, 2).reshape(CONV_K, cd)
    w = _full_weights(cfg, norm_w, w_in_full, conv_w_full, conv_b, dt_bias, a_log, d_skip, ssm_norm_w,
                      a_attn.reshape(d, d), a_ssm.reshape(si, d), a_out.reshape(d, d), final_norm_w.reshape(1, d))

    loss_row, grad_x, grads = _local_step(cfg, x[0], loss_target[0], w)

    by_chip = [_grad_w_in(cfg, grads).reshape(d, N_CHIPS, cfg.N_IN // N_CHIPS).transpose(1, 0, 2),
               grads["w_attn"].reshape(N_CHIPS, d // N_CHIPS, d),
               grads["w_ssm"].reshape(N_CHIPS, si // N_CHIPS, d),
               grads["w_out"].reshape(N_CHIPS, d // N_CHIPS, d)]
    from_sibling = _exchange_halves(by_chip)
    chip_sums = [_add_sibling(g, r, core) for g, r in zip(by_chip, from_sibling)]
    from_chips = _scatter_chips(chip_sums)
    halves = [_add_chips(o, p, chip_core) for o, p in zip(chip_sums, from_chips)]
    g_in, g_attn, g_ssm, g_out = [h.reshape(2 * h.shape[1], h.shape[2]) for h in _share_halves(halves)]

    small = [loss_row, grads["norm_w"], grads["conv_b"], grads["dt_bias"], grads["a_log"], grads["d_skip"],
             grads["ssm_norm_w"], grads["final_norm_w"], grads["conv_w"].reshape(1, CONV_K * cd)]
    sizes = [a.shape[1] for a in small]
    total = sum(sizes)
    rows = -(-total // (8 * LANES)) * 8
    flat = jnp.pad(jnp.concatenate(small, axis=1), ((0, 0), (0, rows * LANES - total)))
    red = _allreduce_small(flat.reshape(rows, LANES)).reshape(1, rows * LANES)
    offs = [sum(sizes[:i]) for i in range(len(sizes))]
    loss_r, g_nw, g_cb, g_dtb, g_alog, g_dsk, g_snw, g_fnw, g_cw_flat = [
        red[:, o:o + n] for o, n in zip(offs, sizes)]
    loss = loss_r[0, 0]
    g_dtb, g_alog, g_dsk = g_dtb[:, :nh], g_alog[:, :nh], g_dsk[:, :nh]
    cshard = cd // N_CHIPS
    g_cw = lax.dynamic_slice_in_dim(g_cw_flat.reshape(CONV_K, cd), chip * cshard, cshard, axis=1)

    upd = {}
    for name, wv, gv, mv, vv in [("w_in", w_in[0], g_in, m_w_in[0], v_w_in[0]),
                                 ("w_attn", w_attn_branch[0], g_attn, m_w_attn_branch[0], v_w_attn_branch[0]),
                                 ("w_ssm", w_ssm_branch[0], g_ssm, m_w_ssm_branch[0], v_w_ssm_branch[0]),
                                 ("w_out", w_out[0], g_out, m_w_out[0], v_w_out[0])]:
        upd[name] = _adamw(wv, gv, mv, vv, "adamw_" + name)
    names = ["norm_w", "conv_w", "conv_b", "dt_bias", "a_log", "d_skip", "ssm_norm_w", "final_norm_w"]
    ws = [norm_w, conv_w[0].reshape(1, -1), conv_b, dt_bias, a_log, d_skip, ssm_norm_w, final_norm_w.reshape(1, d)]
    gs = [g_nw, g_cw.reshape(1, -1), g_cb, g_dtb, g_alog, g_dsk, g_snw, g_fnw]
    ms = [m_norm_w, m_conv_w[0].reshape(1, -1), m_conv_b, m_dt_bias, m_a_log, m_d_skip, m_ssm_norm_w,
          m_final_norm_w.reshape(1, d)]
    vs = [v_norm_w, v_conv_w[0].reshape(1, -1), v_conv_b, v_dt_bias, v_a_log, v_d_skip, v_ssm_norm_w,
          v_final_norm_w.reshape(1, d)]
    ssz = [a.shape[1] for a in ws]
    stot = sum(ssz)
    srows = -(-stot // (8 * LANES)) * 8

    def pack(parts):
        return jnp.pad(jnp.concatenate(parts, axis=1), ((0, 0), (0, srows * LANES - stot))).reshape(srows, LANES)

    packed = _adamw(pack(ws), pack(gs), pack(ms), pack(vs), "adamw_small")
    soffs = [sum(ssz[:i]) for i in range(len(ssz))]
    for k, nm in enumerate(names):
        upd[nm] = tuple(p.reshape(1, srows * LANES)[:, soffs[k]:soffs[k] + ssz[k]] for p in packed)

    shapes = dict(norm_w=norm_w.shape, w_in=w_in.shape, conv_w=conv_w.shape, conv_b=conv_b.shape, dt_bias=dt_bias.shape,
                  a_log=a_log.shape, d_skip=d_skip.shape, ssm_norm_w=ssm_norm_w.shape, w_attn=w_attn_branch.shape,
                  w_ssm=w_ssm_branch.shape, w_out=w_out.shape, final_norm_w=final_norm_w.shape)
    order = ["norm_w", "w_in", "conv_w", "conv_b", "dt_bias", "a_log", "d_skip", "ssm_norm_w", "w_attn", "w_ssm",
             "w_out", "final_norm_w"]
    gradv = dict(norm_w=g_nw, w_in=g_in, conv_w=g_cw, conv_b=g_cb, dt_bias=g_dtb, a_log=g_alog, d_skip=g_dsk,
                 ssm_norm_w=g_snw, w_attn=g_attn, w_ssm=g_ssm, w_out=g_out, final_norm_w=g_fnw)
    outs = [loss, grad_x[None]]
    outs += [gradv[n].reshape(shapes[n]) for n in order]
    for k in range(3):
        outs += [upd[n][k].reshape(shapes[n]) for n in order]
    return tuple(outs)
```

```python
import functools
import math

import jax
import jax.numpy as jnp
from jax import lax
from jax.experimental import pallas as pl
from jax.experimental.pallas import tpu as pltpu

F32 = jnp.float32
BF16 = jnp.bfloat16
SDS = jax.ShapeDtypeStruct

RMS_EPS = 1e-6
LANES = 128
CHUNK = 128
SSM_HEAD_DIM = 64
SSM_GROUPS = 8
SSM_STATE = 128
CONV_K = 4
ATTN_HEAD_DIM = 128
DILATED_PATTERNS = ((128, 1), (512, 4), (2048, 16))
ATTN_WINDOW = max(w for w, _ in DILATED_PATTERNS)
NEG = -1e30
VMEM_LIMIT = 56 * 1024 * 1024
ADAM_LR, ADAM_B1, ADAM_B2, ADAM_EPS, ADAM_WD, ADAM_STEP = 0.001, 0.9, 0.999, 1e-08, 0.01, 10
MESH = pl.DeviceIdType.MESH
N_CHIPS = 4
N_DEV = 8


class _Cfg:
    def __init__(self, s, d):
        self.S, self.D = s, d
        self.H = d // ATTN_HEAD_DIM
        self.SI = 2 * d
        self.NH = self.SI // SSM_HEAD_DIM
        self.HPG = self.NH // SSM_GROUPS
        self.GW = self.HPG * SSM_HEAD_DIM
        self.BC = SSM_GROUPS * SSM_STATE
        self.CD = self.SI + 2 * self.BC
        self.OQ, self.OK, self.OV, self.OZA = 0, d, 2 * d, 3 * d
        self.OZS = 4 * d
        self.OXBC = self.OZS + self.SI
        self.OGA = self.OXBC + self.CD
        self.OGS = self.OGA + d
        self.NM = self.OGS + d
        self.N_IN = self.NM + self.NH
        assert self.GW % LANES == 0 and self.NH <= LANES and s % 512 == 0 and d % 512 == 0


def _params(sem=None):
    return pltpu.CompilerParams(dimension_semantics=sem, vmem_limit_bytes=VMEM_LIMIT)


def _sigmoid(x):
    return 1.0 / (1.0 + jnp.exp(-x))


def _softplus(x):
    u = jnp.exp(-jnp.abs(x))
    l1p = jnp.where(u < 1e-3, u * (1.0 - u * (0.5 - u * (1.0 / 3.0))), jnp.log(1.0 + u))
    return jnp.maximum(x, 0.0) + l1p


def _nt(a, b):
    return lax.dot_general(a, b, (((1,), (1,)), ((), ())), preferred_element_type=F32)


def _tn(a, b):
    return lax.dot_general(a, b, (((0,), (0,)), ((), ())), preferred_element_type=F32)


def _nn(a, b):
    return jnp.dot(a, b, preferred_element_type=F32)


def _tile(n, target):
    if n <= target:
        return n
    best = None
    for t in range(LANES, target + 1, LANES):
        if n % t == 0:
            best = t
    assert best is not None, (n, target)
    return best


def _mm(a, b, dims, out_dtype, name, tm=1024, tn=2048, tk=512, init=None):
    if dims == "nn":
        (m, k), (k2, n) = a.shape, b.shape
    elif dims == "nt":
        (m, k), (n, k2) = a.shape, b.shape
    else:
        (k, m), (k2, n) = a.shape, b.shape
    assert k == k2
    tm, tn, tk = _tile(m, tm), _tile(n, tn), _tile(k, tk)
    nk = k // tk
    if dims == "tn":
        a_spec = pl.BlockSpec((tk, tm), lambda i, j, kk: (kk, i))
    else:
        a_spec = pl.BlockSpec((tm, tk), lambda i, j, kk: (i, kk))
    if dims == "nt":
        b_spec = pl.BlockSpec((tn, tk), lambda i, j, kk: (j, kk))
    else:
        b_spec = pl.BlockSpec((tk, tn), lambda i, j, kk: (kk, j))
    o_spec = pl.BlockSpec((tm, tn), lambda i, j, kk: (i, j))
    op = {"nn": _nn, "nt": _nt, "tn": _tn}[dims]
    has_init = init is not None

    def body(*refs):
        if has_init:
            a_ref, b_ref, i_ref, o_ref, acc = refs
        else:
            a_ref, b_ref, o_ref, acc = refs
        kk = pl.program_id(2)

        @pl.when(kk == 0)
        def _():
            acc[...] = i_ref[...].astype(F32) if has_init else jnp.zeros_like(acc)

        acc[...] += op(a_ref[...], b_ref[...])

        @pl.when(kk == nk - 1)
        def _():
            o_ref[...] = acc[...].astype(out_dtype)

    in_specs = [a_spec, b_spec] + ([o_spec] if has_init else [])
    args = (a, b) + ((init,) if has_init else ())
    return pl.pallas_call(
        body, out_shape=SDS((m, n), out_dtype), grid=(m // tm, n // tn, nk),
        in_specs=in_specs, out_specs=o_spec, scratch_shapes=[pltpu.VMEM((tm, tn), F32)],
        compiler_params=_params(("parallel", "parallel", "arbitrary")), name=name)(*args)


def _rmsnorm_fwd(x, w):
    s, d = x.shape
    tr = 256

    def body(x_ref, w_ref, o_ref):
        xv = x_ref[...]
        r = lax.rsqrt(jnp.mean(xv * xv, axis=-1, keepdims=True) + RMS_EPS)
        o_ref[...] = (xv * r * w_ref[...]).astype(BF16)

    return pl.pallas_call(
        body, out_shape=SDS((s, d), BF16), grid=(s // tr,),
        in_specs=[pl.BlockSpec((tr, d), lambda i: (i, 0)), pl.BlockSpec((1, d), lambda i: (0, 0))],
        out_specs=pl.BlockSpec((tr, d), lambda i: (i, 0)),
        compiler_params=_params(("parallel",)), name="rmsnorm_fwd")(x, w)


def _rmsnorm_bwd(x, w, dhn, dout):
    s, d = x.shape
    tr = 256

    def body(x_ref, w_ref, dh_ref, do_ref, gx_ref, gw_ref):
        xv = x_ref[...]
        r = lax.rsqrt(jnp.mean(xv * xv, axis=-1, keepdims=True) + RMS_EPS)
        nrm = xv * r
        dh = dh_ref[...]
        gy = dh * w_ref[...]
        gx_ref[...] = do_ref[...] + r * (gy - nrm * jnp.mean(gy * nrm, axis=-1, keepdims=True))

        @pl.when(pl.program_id(0) == 0)
        def _():
            gw_ref[...] = jnp.zeros_like(gw_ref)

        gw_ref[...] += jnp.sum(dh * nrm, axis=0, keepdims=True)

    blk = pl.BlockSpec((tr, d), lambda i: (i, 0))
    row = pl.BlockSpec((1, d), lambda i: (0, 0))
    return pl.pallas_call(
        body, out_shape=(SDS((s, d), F32), SDS((1, d), F32)), grid=(s // tr,),
        in_specs=[blk, row, blk, blk], out_specs=(blk, row),
        compiler_params=_params(("arbitrary",)), name="rmsnorm_bwd")(x, w, dhn, dout)


DEINT = DILATED_PATTERNS[-1][1]
DEINT_ROWS = DEINT * LANES


class _Pass:
    def __init__(self, tq, patterns, unit, seg_len):
        self.tq, self.patterns, self.unit, self.seg_len = tq, patterns, unit, seg_len
        self.win = max(w for w, _ in patterns) // unit
        self.w = self.win + tq
        assert self.win % tq == 0


def _attn_tables(ps):
    i = jnp.arange(ps.tq, dtype=jnp.int32)[:, None]
    j = jnp.arange(ps.w, dtype=jnp.int32)[None, :]
    delta = (i + ps.win - j) * ps.unit
    n = jnp.zeros((ps.tq, ps.w), F32)
    for window, dil in ps.patterns:
        n = n + ((delta >= 0) & (delta <= window) & (delta % dil == 0)).astype(F32)
    logn = jnp.where(n > 0, jnp.log(jnp.maximum(n, 1.0)), NEG)
    return logn, jnp.maximum(delta, 0).astype(F32)


def _slopes(h):
    s = jnp.asarray([2.0 ** (-8.0 * (i + 1) / h) for i in range(h)], F32)
    return jnp.broadcast_to(s[:, None, None], (h, 1, LANES))


def _window_starts(ps, start):
    nprev = ps.win // ps.tq
    return [pl.multiple_of(jnp.maximum(start - (nprev - b) * ps.tq, 0), ps.tq) for b in range(nprev)] + [start]


def _window(ref, ps, starts):
    return jnp.concatenate([ref[pl.ds(st, ps.tq), :] for st in starts], axis=0)


def _attn_scores(ps, q_ref, kw, logn_ref, dist_ref, slope_ref, start):
    s = _nt(q_ref[...], kw) * (ATTN_HEAD_DIM ** -0.5)
    s = s + (logn_ref[...] - slope_ref[0:1, 0:1] * dist_ref[...])
    col = lax.broadcasted_iota(jnp.int32, (ps.tq, ps.w), 1)
    return jnp.where(col >= ps.win - lax.rem(start, ps.seg_len), s, NEG)


def _attn_fwd(cfg, ps, qkv, cols, tables, slopes, name):
    s, h = cfg.S, cfg.H
    tq = ps.tq
    logn, dist = tables

    def body(q_ref, k_ref, v_ref, logn_ref, dist_ref, slope_ref, o_ref, lse_ref):
        start = pl.multiple_of(pl.program_id(1) * tq, tq)
        starts = _window_starts(ps, start)
        sc = _attn_scores(ps, q_ref, _window(k_ref, ps, starts), logn_ref, dist_ref, slope_ref, start)
        m = jnp.max(sc, axis=1, keepdims=True)
        p = jnp.exp(sc - m)
        l = jnp.sum(p, axis=1, keepdims=True)
        o_ref[...] = (_nn(p.astype(BF16), _window(v_ref, ps, starts)) / l).astype(BF16)
        lse_ref[...] = jnp.broadcast_to(m + jnp.log(l), (tq, LANES))

    qb, kb, vb = [c // LANES for c in cols]
    blk = pl.BlockSpec((tq, LANES), lambda hh, i: (i, hh))
    tab = pl.BlockSpec((tq, ps.w), lambda hh, i: (0, 0))
    return pl.pallas_call(
        body, out_shape=(SDS((s, cfg.D), BF16), SDS((s, h * LANES), F32)), grid=(h, s // tq),
        in_specs=[pl.BlockSpec((tq, LANES), lambda hh, i: (i, qb + hh)),
                  pl.BlockSpec((s, LANES), lambda hh, i: (0, kb + hh)),
                  pl.BlockSpec((s, LANES), lambda hh, i: (0, vb + hh)),
                  tab, tab, pl.BlockSpec((None, 1, LANES), lambda hh, i: (hh, 0, 0))],
        out_specs=(blk, blk),
        compiler_params=_params(("parallel", "parallel")), name=name)(qkv, qkv, qkv, logn, dist, slopes)


def _attn_bwd(cfg, ps, qkv, cols, do, lse, delta, tables, slopes, name):
    s, h = cfg.S, cfg.H
    tq = ps.tq
    logn, dist = tables
    scale = ATTN_HEAD_DIM ** -0.5

    def body(q_ref, k_ref, v_ref, do_ref, lse_ref, dl_ref, logn_ref, dist_ref, slope_ref, dq_ref, dk_ref, dv_ref):
        hh = pl.program_id(0)
        i = pl.program_id(1)
        start = pl.multiple_of(i * tq, tq)

        @pl.when(i == 0)
        def _():
            dk_ref[...] = jnp.zeros_like(dk_ref)
            dv_ref[...] = jnp.zeros_like(dv_ref)

        starts = _window_starts(ps, start)
        kw = _window(k_ref, ps, starts)
        vw = _window(v_ref, ps, starts)
        sc = _attn_scores(ps, q_ref, kw, logn_ref, dist_ref, slope_ref, start)
        mine = lax.broadcasted_iota(jnp.int32, (tq, LANES), 1) == hh
        lse_h = jnp.sum(jnp.where(mine, lse_ref[...], 0.0), axis=1, keepdims=True)
        delta_h = jnp.sum(jnp.where(mine, dl_ref[...], 0.0), axis=1, keepdims=True)
        p = jnp.exp(sc - lse_h)
        dob = do_ref[...]
        ds = (p * (_nt(dob, vw) - delta_h) * scale).astype(BF16)
        dq_ref[...] = _nn(ds, kw).astype(BF16)
        dkw = _tn(ds, q_ref[...])
        dvw = _tn(p.astype(BF16), dob)
        for b, st in enumerate(starts):
            dk_ref[pl.ds(st, tq), :] += dkw[b * tq:(b + 1) * tq]
            dv_ref[pl.ds(st, tq), :] += dvw[b * tq:(b + 1) * tq]

    qb, kb, vb = [c // LANES for c in cols]
    blk = pl.BlockSpec((tq, LANES), lambda hh, i: (i, hh))
    stat = pl.BlockSpec((tq, LANES), lambda hh, i: (i, 0))
    full = pl.BlockSpec((s, LANES), lambda hh, i: (0, hh))
    tab = pl.BlockSpec((tq, ps.w), lambda hh, i: (0, 0))
    return pl.pallas_call(
        body, out_shape=(SDS((s, cfg.D), BF16), SDS((s, cfg.D), F32), SDS((s, cfg.D), F32)), grid=(h, s // tq),
        in_specs=[pl.BlockSpec((tq, LANES), lambda hh, i: (i, qb + hh)),
                  pl.BlockSpec((s, LANES), lambda hh, i: (0, kb + hh)),
                  pl.BlockSpec((s, LANES), lambda hh, i: (0, vb + hh)),
                  blk, stat, stat, tab, tab, pl.BlockSpec((None, 1, LANES), lambda hh, i: (hh, 0, 0))],
        out_specs=(blk, full, full),
        compiler_params=_params(("parallel", "arbitrary")), name=name)(
            qkv, qkv, qkv, do, lse, delta, logn, dist, slopes)


def _by_residue(a):
    return a.reshape(DEINT, a.shape[0] // DEINT, a.shape[1])


def _deint_spec(colblock):
    return pl.BlockSpec((DEINT, LANES, LANES), lambda b, j: (0, b, colblock(j)))


def _deint_rows(scr, out_ref, dtype):
    for r in range(DEINT):
        out_ref[r] = scr[pl.ds(r, LANES, stride=DEINT), :].astype(dtype)


def _int_rows(in_ref, scr):
    for r in range(DEINT):
        scr[pl.ds(r, LANES, stride=DEINT), :] = in_ref[r].astype(F32)


def _deinterleave(x, col0, ncols, name):
    s = x.shape[0]
    c0 = col0 // LANES

    def body(x_ref, o_ref, scr):
        scr[...] = x_ref[...].astype(F32)
        _deint_rows(scr, o_ref, x.dtype)

    out = pl.pallas_call(
        body, out_shape=SDS((DEINT, s // DEINT, ncols), x.dtype), grid=(s // DEINT_ROWS, ncols // LANES),
        in_specs=[pl.BlockSpec((DEINT_ROWS, LANES), lambda b, j: (b, c0 + j))],
        out_specs=_deint_spec(lambda j: j),
        scratch_shapes=[pltpu.VMEM((DEINT_ROWS, LANES), F32)],
        compiler_params=_params(("parallel", "parallel")), name=name)(x)
    return out.reshape(s, ncols)


def _attn_merge(cfg, proj, o_1, lse_1, o_2, lse_2):
    s, h = cfg.S, cfg.H
    zb = cfg.OZA // LANES
    rows = DEINT_ROWS

    def body(o1_ref, l1_ref, o2_ref, l2_ref, z_ref, o_ref, og_ref, lse_ref, so, sl):
        hh = pl.program_id(1)
        _int_rows(o2_ref, so)
        _int_rows(l2_ref, sl)
        l1, l2 = l1_ref[...], sl[...]
        mx = jnp.maximum(l1, l2)
        w1, w2 = jnp.exp(l1 - mx), jnp.exp(l2 - mx)
        den = w1 + w2
        o = (w1 * o1_ref[...].astype(F32) + w2 * so[...]) / den
        z = z_ref[...].astype(F32)
        o_ref[...] = o.astype(BF16)
        og_ref[...] = (o * (z * _sigmoid(z))).astype(BF16)

        @pl.when(hh == 0)
        def _():
            lse_ref[...] = jnp.zeros_like(lse_ref)

        lane = lax.broadcasted_iota(jnp.int32, (rows, LANES), 1)
        lse_ref[...] = jnp.where(lane == hh, mx + jnp.log(den), lse_ref[...])

    blk = pl.BlockSpec((rows, LANES), lambda b, j: (b, j))
    return pl.pallas_call(
        body, out_shape=(SDS((s, cfg.D), BF16), SDS((s, cfg.D), BF16), SDS((s, LANES), F32)),
        grid=(s // rows, h),
        in_specs=[blk, blk, _deint_spec(lambda j: j), _deint_spec(lambda j: j),
                  pl.BlockSpec((rows, LANES), lambda b, j: (b, zb + j))],
        out_specs=(blk, blk, pl.BlockSpec((rows, LANES), lambda b, j: (b, 0))),
        scratch_shapes=[pltpu.VMEM((rows, LANES), F32), pltpu.VMEM((rows, LANES), F32)],
        compiler_params=_params(("parallel", "arbitrary")), name="attn_merge")(
            o_1, lse_1, _by_residue(o_2), _by_residue(lse_2), proj)


def _attn_bwd_prep(cfg, proj, o_a, doag, lse, dproj):
    s, h = cfg.S, cfg.H
    zb = cfg.OZA // LANES
    rows = DEINT_ROWS

    def body(o_ref, dg_ref, z_ref, lse_ref, dp_in, dz_ref, do_ref, do2_ref, dl_ref, dl2_ref, lse2_ref, scr):
        del dp_in
        hh = pl.program_id(1)
        z = z_ref[...].astype(F32)
        sg = _sigmoid(z)
        o = o_ref[...].astype(F32)
        dg = dg_ref[...].astype(F32)
        do = dg * (z * sg)
        dz_ref[...] = (dg * o * (sg * (1.0 + z * (1.0 - sg)))).astype(BF16)
        do_ref[...] = do.astype(BF16)
        scr[...] = do
        _deint_rows(scr, do2_ref, BF16)

        @pl.when(hh == 0)
        def _():
            dl_ref[...] = jnp.zeros_like(dl_ref)

        lane = lax.broadcasted_iota(jnp.int32, (rows, LANES), 1)
        dl_ref[...] = jnp.where(lane == hh, jnp.sum(do * o, axis=1, keepdims=True), dl_ref[...])

        @pl.when(hh == h - 1)
        def _():
            scr[...] = dl_ref[...]
            _deint_rows(scr, dl2_ref, F32)
            scr[...] = lse_ref[...]
            _deint_rows(scr, lse2_ref, F32)

    blk = pl.BlockSpec((rows, LANES), lambda b, j: (b, j))
    stat = pl.BlockSpec((rows, LANES), lambda b, j: (b, 0))
    stat2 = _deint_spec(lambda j: 0)
    outs = pl.pallas_call(
        body,
        out_shape=(SDS(dproj.shape, BF16), SDS((s, cfg.D), BF16), SDS((DEINT, s // DEINT, cfg.D), BF16),
                   SDS((s, LANES), F32), SDS((DEINT, s // DEINT, LANES), F32), SDS((DEINT, s // DEINT, LANES), F32)),
        grid=(s // rows, h),
        in_specs=[blk, blk, pl.BlockSpec((rows, LANES), lambda b, j: (b, zb + j)), stat, HBM_SPEC],
        out_specs=(pl.BlockSpec((rows, LANES), lambda b, j: (b, zb + j)), blk, _deint_spec(lambda j: j),
                   stat, stat2, stat2),
        scratch_shapes=[pltpu.VMEM((rows, LANES), F32)],
        input_output_aliases={4: 0},
        compiler_params=_params(("parallel", "arbitrary")), name="attn_bwd_prep")(o_a, doag, proj, lse, dproj)
    dproj, do, do2, dl, dl2, lse2 = outs
    return dproj, do, do2.reshape(s, cfg.D), dl, dl2.reshape(s, LANES), lse2.reshape(s, LANES)


def _attn_grad_sum(cfg, g_1, g_2, col0, dproj, name):
    s, h = cfg.S, cfg.H
    c0 = col0 // LANES
    rows = DEINT_ROWS

    def body(g1_ref, g2_ref, dp_in, o_ref, scr):
        del dp_in
        _int_rows(g2_ref, scr)
        o_ref[...] = (g1_ref[...].astype(F32) + scr[...]).astype(BF16)

    return pl.pallas_call(
        body, out_shape=SDS(dproj.shape, BF16), grid=(s // rows, h),
        in_specs=[pl.BlockSpec((rows, LANES), lambda b, j: (b, j)), _deint_spec(lambda j: j), HBM_SPEC],
        out_specs=pl.BlockSpec((rows, LANES), lambda b, j: (b, c0 + j)),
        scratch_shapes=[pltpu.VMEM((rows, LANES), F32)],
        input_output_aliases={2: 0},
        compiler_params=_params(("parallel", "parallel")), name=name)(g_1, _by_residue(g_2), dproj)


CONV_HALO = 16
CONV_TR = 512
CONV_CW = 512


def _conv_fwd(cfg, proj, conv_w, conv_b):
    s, cd = cfg.S, cfg.CD
    tr, cw, hl = CONV_TR, CONV_CW, CONV_HALO
    cb0 = cfg.OXBC // cw

    def body(x_ref, h_ref, w_ref, b_ref, o_ref, scr):
        i = pl.program_id(0)
        scr[pl.ds(0, hl), :] = jnp.where(i > 0, h_ref[...].astype(F32), 0.0)
        scr[pl.ds(hl, tr), :] = x_ref[...].astype(F32)
        pre = b_ref[...] + jnp.zeros((tr, cw), F32)
        for k in range(CONV_K):
            pre = pre + w_ref[k:k + 1, :] * scr[pl.ds(hl - (CONV_K - 1) + k, tr), :]
        o_ref[...] = (pre * _sigmoid(pre)).astype(BF16)

    return pl.pallas_call(
        body, out_shape=SDS((s, cd), BF16), grid=(s // tr, cd // cw),
        in_specs=[pl.BlockSpec((tr, cw), lambda i, j: (i, cb0 + j)),
                  pl.BlockSpec((hl, cw), lambda i, j: (jnp.maximum(i * (tr // hl) - 1, 0), cb0 + j)),
                  pl.BlockSpec((CONV_K, cw), lambda i, j: (0, j)),
                  pl.BlockSpec((1, cw), lambda i, j: (0, j))],
        out_specs=pl.BlockSpec((tr, cw), lambda i, j: (i, j)),
        scratch_shapes=[pltpu.VMEM((tr + hl, cw), F32)],
        compiler_params=_params(("parallel", "parallel")), name="conv_fwd")(proj, proj, conv_w, conv_b)


def _conv_bwd(cfg, proj, dact, conv_w, conv_b, dproj):
    s, cd = cfg.S, cfg.CD
    tr, cw, hl = CONV_TR, CONV_CW, CONV_HALO
    cb0 = cfg.OXBC // cw
    nr = s // tr
    last_h = s // hl - 1

    def body(x_ref, hp_ref, hn_ref, d_ref, dn_ref, w_ref, b_ref, dp_in, dx_ref, gw_ref, gb_ref, xs, ds):
        del dp_in
        i = pl.program_id(1)
        xs[pl.ds(0, hl), :] = jnp.where(i > 0, hp_ref[...].astype(F32), 0.0)
        xs[pl.ds(hl, tr), :] = x_ref[...].astype(F32)
        xs[pl.ds(hl + tr, hl), :] = hn_ref[...].astype(F32)
        pre = b_ref[...] + jnp.zeros((tr + hl, cw), F32)
        for k in range(CONV_K):
            pre = pre + w_ref[k:k + 1, :] * xs[pl.ds(hl - (CONV_K - 1) + k, tr + hl), :]
        sg = _sigmoid(pre)
        dsilu = sg * (1.0 + pre * (1.0 - sg))
        ds[pl.ds(0, tr), :] = d_ref[...].astype(F32) * dsilu[0:tr]
        ds[pl.ds(tr, hl), :] = jnp.where(i < nr - 1, dn_ref[...].astype(F32), 0.0) * dsilu[tr:tr + hl]
        dx = jnp.zeros((tr, cw), F32)
        for k in range(CONV_K):
            dx = dx + w_ref[k:k + 1, :] * ds[pl.ds(CONV_K - 1 - k, tr), :]
        dx_ref[...] = dx.astype(BF16)

        @pl.when(i == 0)
        def _():
            gw_ref[...] = jnp.zeros_like(gw_ref)
            gb_ref[...] = jnp.zeros_like(gb_ref)

        dcur = ds[pl.ds(0, tr), :]
        gb_ref[...] += jnp.sum(dcur, axis=0, keepdims=True)
        for k in range(CONV_K):
            gw_ref[k:k + 1, :] += jnp.sum(dcur * xs[pl.ds(hl - (CONV_K - 1) + k, tr), :], axis=0, keepdims=True)

    return pl.pallas_call(
        body, out_shape=(SDS(dproj.shape, BF16), SDS((CONV_K, cd), F32), SDS((1, cd), F32)), grid=(cd // cw, nr),
        in_specs=[pl.BlockSpec((tr, cw), lambda j, i: (i, cb0 + j)),
                  pl.BlockSpec((hl, cw), lambda j, i: (jnp.maximum(i * (tr // hl) - 1, 0), cb0 + j)),
                  pl.BlockSpec((hl, cw), lambda j, i: (jnp.minimum((i + 1) * (tr // hl), last_h), cb0 + j)),
                  pl.BlockSpec((tr, cw), lambda j, i: (i, j)),
                  pl.BlockSpec((hl, cw), lambda j, i: (jnp.minimum((i + 1) * (tr // hl), last_h), j)),
                  pl.BlockSpec((CONV_K, cw), lambda j, i: (0, j)),
                  pl.BlockSpec((1, cw), lambda j, i: (0, j)),
                  pl.BlockSpec(memory_space=pl.ANY)],
        out_specs=(pl.BlockSpec((tr, cw), lambda j, i: (i, cb0 + j)),
                   pl.BlockSpec((CONV_K, cw), lambda j, i: (0, j)),
                   pl.BlockSpec((1, cw), lambda j, i: (0, j))),
        scratch_shapes=[pltpu.VMEM((tr + 2 * hl, cw), F32), pltpu.VMEM((tr + hl, cw), F32)],
        input_output_aliases={7: 0},
        compiler_params=_params(("parallel", "arbitrary")), name="conv_bwd")(
            proj, proj, proj, dact, dact, conv_w, conv_b, dproj)


def _expand(v, e, terms):
    out, rem = None, v
    for _ in range(terms):
        hi = rem.astype(BF16)
        t = _nn(hi, e)
        out = t if out is None else out + t
        rem = rem - hi.astype(F32)
    return out


def _segsum(v, e, terms):
    out, rem = None, v
    for _ in range(terms):
        hi = rem.astype(BF16)
        t = _nt(hi, e)
        out = t if out is None else out + t
        rem = rem - hi.astype(F32)
    return out


def _expand_row(row, e, terms):
    return _expand(jnp.broadcast_to(row, (8, LANES)), e, terms)[0:1]


def _segsum_row(row, e, terms):
    return _segsum(jnp.broadcast_to(row, (8, row.shape[1])), e, terms)[0:1]


def _expansion_matrix(cfg):
    hh = jnp.arange(LANES, dtype=jnp.int32)[:, None]
    cc = jnp.arange(cfg.SI, dtype=jnp.int32)[None, :]
    return (cc // SSM_HEAD_DIM == hh).astype(BF16)


def _tri(lower):
    r = lax.broadcasted_iota(jnp.int32, (CHUNK, CHUNK), 0)
    c = lax.broadcasted_iota(jnp.int32, (CHUNK, CHUNK), 1)
    return (c <= r) if lower else (c >= r)


def _ssd_prep(dtr_ref, db_ref, al_ref, e):
    dtr = dtr_ref[...] + db_ref[...]
    dt = _softplus(dtr)
    a = -jnp.exp(al_ref[...])
    acum = jnp.dot(_tri(True).astype(F32), dt * a, precision=lax.Precision.HIGHEST, preferred_element_type=F32)
    return dtr, dt, a, _expand(dt, e, 2), _expand(acum, e, 3)


def _ssd_fwd(cfg, xact, dt_raw, proj, dt_bias, a_log, d_skip, norm_w, e):
    s, si, cd, gw, bc = cfg.S, cfg.SI, cfg.CD, cfg.GW, cfg.BC
    nc = s // CHUNK
    zb = cfg.OZS // si
    tiles = gw // LANES

    def body(xa_ref, dtr_ref, z_ref, db_ref, al_ref, dsk_ref, nw_ref, e_ref, y_ref, y2_ref, st_ref,
             state, ybuf, x_s, xw_s, ae_s, ea_s, lam_s):
        @pl.when(pl.program_id(0) == 0)
        def _():
            state[...] = jnp.zeros_like(state)

        st_ref[...] = state[...]
        ev = e_ref[...]
        _, _, _, dt_e, a_e = _ssd_prep(dtr_ref, db_ref, al_ref, ev)
        xs = xa_ref[:, 0:si].astype(F32)
        x = xs * dt_e
        lam_e = a_e[CHUNK - 1:CHUNK, :]
        x_s[...] = x.astype(BF16)
        xw_s[...] = (x * jnp.exp(lam_e - a_e)).astype(BF16)
        ae_s[...] = a_e
        ea_s[...] = jnp.exp(a_e)
        ybuf[...] = _expand_row(dsk_ref[...], ev, 3) * xs
        lam_s[...] = jnp.broadcast_to(jnp.exp(lam_e), (8, si))
        tril = _tri(True)
        lane = lax.broadcasted_iota(jnp.int32, (CHUNK, LANES), 1)

        def group(g, carry):
            co = pl.multiple_of(g * gw, LANES)
            bg = xa_ref[:, pl.ds(pl.multiple_of(si + g * SSM_STATE, LANES), SSM_STATE)]
            cg = xa_ref[:, pl.ds(pl.multiple_of(si + bc + g * SSM_STATE, LANES), SSM_STATE)]
            cbm = _nt(cg, bg)
            st = state[:, pl.ds(co, gw)]
            yoff = _nn(cg, st.astype(BF16)) * ea_s[:, pl.ds(co, gw)]
            for k in range(tiles):
                tc = pl.multiple_of(co + k * LANES, LANES)
                at = ae_s[:, pl.ds(tc, LANES)]
                att = at.T
                xt = x_s[:, pl.ds(tc, LANES)]
                acc = yoff[:, k * LANES:(k + 1) * LANES]
                for half in range(2):
                    lo = half * SSM_HEAD_DIM
                    seg = at[:, lo:lo + 1] - att[lo:lo + 1, :]
                    dec = jnp.exp(jnp.where(tril, seg, NEG))
                    xh = jnp.where((lane >= lo) & (lane < lo + SSM_HEAD_DIM), xt, jnp.zeros_like(xt))
                    acc = acc + _nn((cbm * dec).astype(BF16), xh)
                ybuf[:, pl.ds(tc, LANES)] += acc
            state[:, pl.ds(co, gw)] = st * lam_s[0:1, pl.ds(co, gw)] + _tn(bg, xw_s[:, pl.ds(co, gw)])
            return carry

        lax.fori_loop(0, SSM_GROUPS, group, 0)
        y = ybuf[...]
        y_ref[...] = y.astype(BF16)
        z = z_ref[...].astype(F32)
        u = y * (z * _sigmoid(z))
        r = lax.rsqrt(jnp.mean(u * u, axis=-1, keepdims=True) + RMS_EPS)
        y2_ref[...] = (u * r * nw_ref[...]).astype(BF16)

    row = lambda n: pl.BlockSpec((1, n), lambda c: (0, 0))
    return pl.pallas_call(
        body,
        out_shape=(SDS((s, si), BF16), SDS((s, si), BF16), SDS((nc, SSM_STATE, si), F32)),
        grid=(nc,),
        in_specs=[pl.BlockSpec((CHUNK, cd), lambda c: (c, 0)),
                  pl.BlockSpec((CHUNK, LANES), lambda c: (c, 0)),
                  pl.BlockSpec((CHUNK, si), lambda c: (c, zb)),
                  row(LANES), row(LANES), row(LANES), row(si),
                  pl.BlockSpec((LANES, si), lambda c: (0, 0))],
        out_specs=(pl.BlockSpec((CHUNK, si), lambda c: (c, 0)),
                   pl.BlockSpec((CHUNK, si), lambda c: (c, 0)),
                   pl.BlockSpec((None, SSM_STATE, si), lambda c: (c, 0, 0))),
        scratch_shapes=[pltpu.VMEM((SSM_STATE, si), F32), pltpu.VMEM((CHUNK, si), F32),
                        pltpu.VMEM((CHUNK, si), BF16), pltpu.VMEM((CHUNK, si), BF16),
                        pltpu.VMEM((CHUNK, si), F32), pltpu.VMEM((CHUNK, si), F32),
                        pltpu.VMEM((8, si), F32)],
        compiler_params=_params(("arbitrary",)), name="ssd_fwd")(
            xact, dt_raw, proj, dt_bias, a_log, d_skip, norm_w, e)


def _ssd_bwd(cfg, xact, dt_raw, proj, y, dy2, states, dt_bias, a_log, d_skip, norm_w, e, dproj):
    s, si, cd, gw, bc, hpg = cfg.S, cfg.SI, cfg.CD, cfg.GW, cfg.BC, cfg.HPG
    nc = s // CHUNK
    zb = cfg.OZS // si
    tiles = gw // LANES

    def body(xa_ref, dtr_ref, z_ref, y_ref, d2_ref, st_ref, db_ref, al_ref, dsk_ref, nw_ref, e_ref, dp_in,
             dz_ref, dxa_ref, ddt_ref, gnw_ref, gdb_ref, gal_ref, gds_ref,
             dh, dhn, xs_s, x_s, w_s, ae_s, ea_s, g_s, dx_s, dae_s, r_s, lam_s, dle_s):
        del dp_in

        @pl.when(pl.program_id(0) == 0)
        def _():
            dh[...] = jnp.zeros_like(dh)
            gnw_ref[...] = jnp.zeros_like(gnw_ref)
            gdb_ref[...] = jnp.zeros_like(gdb_ref)
            gal_ref[...] = jnp.zeros_like(gal_ref)
            gds_ref[...] = jnp.zeros_like(gds_ref)

        ev = e_ref[...]
        yv = y_ref[...].astype(F32)
        z = z_ref[...].astype(F32)
        sg = _sigmoid(z)
        sz = z * sg
        u = yv * sz
        r = lax.rsqrt(jnp.mean(u * u, axis=-1, keepdims=True) + RMS_EPS)
        nrm = u * r
        d2 = d2_ref[...].astype(F32)
        gnw_ref[...] += jnp.sum(d2 * nrm, axis=0, keepdims=True)
        gn = d2 * nw_ref[...]
        du = r * (gn - nrm * jnp.mean(gn * nrm, axis=-1, keepdims=True))
        gv = du * sz
        dz_ref[...] = (du * yv * (sg * (1.0 + z * (1.0 - sg)))).astype(BF16)
        g_s[...] = gv

        dtr, dt, a, dt_e, a_e = _ssd_prep(dtr_ref, db_ref, al_ref, ev)
        xs = xa_ref[:, 0:si].astype(F32)
        x = xs * dt_e
        lam_e = a_e[CHUNK - 1:CHUNK, :]
        xs_s[...] = xs
        x_s[...] = x
        w_s[...] = jnp.exp(lam_e - a_e)
        ae_s[...] = a_e
        ea_s[...] = jnp.exp(a_e)
        lam_s[...] = jnp.broadcast_to(jnp.exp(lam_e), (8, si))
        gds_ref[...] += _segsum_row(jnp.sum(gv * xs, axis=0, keepdims=True), ev, 2)
        r_s[...] = jnp.zeros_like(r_s)
        tril = _tri(True)
        lane = lax.broadcasted_iota(jnp.int32, (CHUNK, LANES), 1)
        sub = lax.broadcasted_iota(jnp.int32, (CHUNK, LANES), 0)

        def group(g, carry):
            co = pl.multiple_of(g * gw, LANES)
            bo = pl.multiple_of(si + g * SSM_STATE, LANES)
            cof = pl.multiple_of(si + bc + g * SSM_STATE, LANES)
            cols = pl.ds(co, gw)
            bg = xa_ref[:, pl.ds(bo, SSM_STATE)]
            cg = xa_ref[:, pl.ds(cof, SSM_STATE)]
            cbm = _nt(cg, bg)
            st = st_ref[:, cols]
            stb = st.astype(BF16)
            dho = dh[:, cols]
            dhob = dho.astype(BF16)
            ea = ea_s[:, cols]
            gg = g_s[:, cols]
            xg = x_s[:, cols]
            wg = w_s[:, cols]
            explam = lam_s[0:1, cols]
            yoff = _nn(cg, stb) * ea
            ga = (gg * ea).astype(BF16)
            dc = _nt(ga, stb)
            dhn[:, cols] = dho * explam + _tn(cg, ga)
            bdh = _nn(bg, dhob)
            db = _nt((xg * wg).astype(BF16), dhob)
            t = xg * bdh * wg
            dle_s[0:1, cols] = jnp.sum(t, axis=0, keepdims=True) + explam * jnp.sum(dho * st, axis=0, keepdims=True)
            dae_base = gg * yoff - t
            dxw = wg * bdh
            dcb = jnp.zeros((CHUNK, CHUNK), F32)
            for k in range(tiles):
                tc = pl.multiple_of(co + k * LANES, LANES)
                ksl = slice(k * LANES, (k + 1) * LANES)
                at = ae_s[:, pl.ds(tc, LANES)]
                att = at.T
                xt = xg[:, ksl].astype(BF16)
                gt = gg[:, ksl].astype(BF16)
                dxt = dxw[:, ksl]
                place = jnp.zeros((CHUNK, LANES), F32)
                for half in range(2):
                    lo = half * SSM_HEAD_DIM
                    seg = at[:, lo:lo + 1] - att[lo:lo + 1, :]
                    dec = jnp.exp(jnp.where(tril, seg, NEG))
                    mh = cbm * dec
                    gh = jnp.where((lane >= lo) & (lane < lo + SSM_HEAD_DIM), gt, jnp.zeros_like(gt))
                    dm = _nt(gh, xt)
                    dxt = dxt + _tn(mh.astype(BF16), gh)
                    dcb = dcb + dm * dec
                    dseg = dm * mh
                    place = place + jnp.where(lane == lo, jnp.sum(dseg, axis=1, keepdims=True), 0.0)
                    hidx = g * hpg + 2 * k + half
                    r_s[...] += jnp.where(sub == hidx, jnp.sum(dseg, axis=0, keepdims=True), 0.0)
                dx_s[:, pl.ds(tc, LANES)] = dxt
                dae_s[:, pl.ds(tc, LANES)] = dae_base[:, ksl] + place
            dcbb = dcb.astype(BF16)
            dxa_ref[:, pl.ds(bo, SSM_STATE)] = (db + _tn(dcbb, cg)).astype(BF16)
            dxa_ref[:, pl.ds(cof, SSM_STATE)] = (dc + _nn(dcbb, bg)).astype(BF16)
            return carry

        lax.fori_loop(0, SSM_GROUPS, group, 0)
        dlam = _segsum_row(dle_s[0:1, :], ev, 2)
        da_ = _segsum(dae_s[...], ev, 2) - r_s[...].T
        da_ = da_ + jnp.where(sub == CHUNK - 1, dlam, 0.0)
        dda = jnp.dot(_tri(False).astype(F32), da_, precision=lax.Precision.HIGHEST, preferred_element_type=F32)
        dxv = dx_s[...]
        xs = xs_s[...]
        ddt = dda * a + _segsum(dxv * xs, ev, 2)
        gal_ref[...] += jnp.sum(dda * dt, axis=0, keepdims=True) * a
        ddtr = ddt * _sigmoid(dtr)
        gdb_ref[...] += jnp.sum(ddtr, axis=0, keepdims=True)
        ddt_ref[...] = ddtr
        dxa_ref[:, 0:si] = (dxv * dt_e + g_s[...] * _expand_row(dsk_ref[...], ev, 3)).astype(BF16)
        dh[...] = dhn[...]

    rev = lambda c: nc - 1 - c
    row = lambda n: pl.BlockSpec((1, n), lambda c: (0, 0))
    big = lambda: pltpu.VMEM((CHUNK, si), F32)
    return pl.pallas_call(
        body,
        out_shape=(SDS(dproj.shape, BF16), SDS((s, cd), BF16), SDS((s, LANES), F32),
                   SDS((1, si), F32), SDS((1, LANES), F32), SDS((1, LANES), F32), SDS((1, LANES), F32)),
        grid=(nc,),
        in_specs=[pl.BlockSpec((CHUNK, cd), lambda c: (rev(c), 0)),
                  pl.BlockSpec((CHUNK, LANES), lambda c: (rev(c), 0)),
                  pl.BlockSpec((CHUNK, si), lambda c: (rev(c), zb)),
                  pl.BlockSpec((CHUNK, si), lambda c: (rev(c), 0)),
                  pl.BlockSpec((CHUNK, si), lambda c: (rev(c), 0)),
                  pl.BlockSpec((None, SSM_STATE, si), lambda c: (rev(c), 0, 0)),
                  row(LANES), row(LANES), row(LANES), row(si),
                  pl.BlockSpec((LANES, si), lambda c: (0, 0)),
                  pl.BlockSpec(memory_space=pl.ANY)],
        out_specs=(pl.BlockSpec((CHUNK, si), lambda c: (rev(c), zb)),
                   pl.BlockSpec((CHUNK, cd), lambda c: (rev(c), 0)),
                   pl.BlockSpec((CHUNK, LANES), lambda c: (rev(c), 0)),
                   row(si), row(LANES), row(LANES), row(LANES)),
        scratch_shapes=[pltpu.VMEM((SSM_STATE, si), F32), pltpu.VMEM((SSM_STATE, si), F32),
                        big(), big(), big(), big(), big(), big(), big(), big(),
                        pltpu.VMEM((CHUNK, LANES), F32), pltpu.VMEM((8, si), F32), pltpu.VMEM((8, si), F32)],
        input_output_aliases={11: 0},
        compiler_params=_params(("arbitrary",)), name="ssd_bwd")(
            xact, dt_raw, proj, y, dy2, states, dt_bias, a_log, d_skip, norm_w, e, dproj)


MERGE_TR = 512
MERGE_CW = 512


def _merge_fwd(cfg, proj, a_br, s_br):
    s, d = cfg.S, cfg.D
    tr, cw = MERGE_TR, MERGE_CW
    ga0, gs0 = cfg.OGA // cw, cfg.OGS // cw

    def body(ga_ref, gs_ref, a_ref, s_ref, o_ref):
        o_ref[...] = (_sigmoid(ga_ref[...].astype(F32)) * a_ref[...].astype(F32)
                      + _sigmoid(gs_ref[...].astype(F32)) * s_ref[...].astype(F32)).astype(BF16)

    blk = pl.BlockSpec((tr, cw), lambda i, j: (i, j))
    return pl.pallas_call(
        body, out_shape=SDS((s, d), BF16), grid=(s // tr, d // cw),
        in_specs=[pl.BlockSpec((tr, cw), lambda i, j: (i, ga0 + j)),
                  pl.BlockSpec((tr, cw), lambda i, j: (i, gs0 + j)), blk, blk],
        out_specs=blk, compiler_params=_params(("parallel", "parallel")), name="merge_fwd")(proj, proj, a_br, s_br)


def _merge_bwd(cfg, proj, branch, dmerged, gate_off, dproj, name):
    s, d = cfg.S, cfg.D
    tr, cw = MERGE_TR, MERGE_CW
    g0 = gate_off // cw
    fresh = dproj is None

    def body(*refs):
        g_ref, b_ref, dm_ref = refs[:3]
        dg_ref, db_ref = refs[-2:]
        dm = dm_ref[...].astype(F32)
        sg = _sigmoid(g_ref[...].astype(F32))
        db_ref[...] = (dm * sg).astype(BF16)
        dg_ref[...] = (dm * b_ref[...].astype(F32) * sg * (1.0 - sg)).astype(BF16)

    blk = pl.BlockSpec((tr, cw), lambda i, j: (i, j))
    gate = pl.BlockSpec((tr, cw), lambda i, j: (i, g0 + j))
    return pl.pallas_call(
        body, out_shape=(SDS((s, cfg.NM), BF16), SDS((s, d), BF16)), grid=(s // tr, d // cw),
        in_specs=[gate, blk, blk] + ([] if fresh else [HBM_SPEC]),
        out_specs=(gate, blk),
        input_output_aliases={} if fresh else {3: 0},
        compiler_params=_params(("parallel", "parallel")), name=name)(
            *((proj, branch, dmerged) + (() if fresh else (dproj,))))


def _outproj_loss(merged, w_out, x, target, fnw):
    s, d = x.shape
    tr = 256

    def body(m_ref, w_ref, x_ref, t_ref, fw_ref, dof_ref, dob_ref, loss_ref, g_ref):
        out = x_ref[...] + _nn(m_ref[...], w_ref[...])
        r = lax.rsqrt(jnp.mean(out * out, axis=-1, keepdims=True) + RMS_EPS)
        nrm = out * r
        fw = fw_ref[...]
        err = nrm * fw - t_ref[...]
        dy = err * (1.0 / d)
        gy = dy * fw
        dout = r * (gy - nrm * jnp.mean(gy * nrm, axis=-1, keepdims=True))
        dof_ref[...] = dout
        dob_ref[...] = dout.astype(BF16)

        @pl.when(pl.program_id(0) == 0)
        def _():
            loss_ref[...] = jnp.zeros_like(loss_ref)
            g_ref[...] = jnp.zeros_like(g_ref)

        loss_ref[...] += jnp.sum(jnp.sum(err * err, axis=1, keepdims=True), axis=0, keepdims=True) * (0.5 / d)
        g_ref[...] += jnp.sum(dy * nrm, axis=0, keepdims=True)

    blk = pl.BlockSpec((tr, d), lambda i: (i, 0))
    return pl.pallas_call(
        body, out_shape=(SDS((s, d), F32), SDS((s, d), BF16), SDS((1, LANES), F32), SDS((1, d), F32)), grid=(s // tr,),
        in_specs=[blk, pl.BlockSpec((d, d), lambda i: (0, 0)), blk, blk, pl.BlockSpec((1, d), lambda i: (0, 0))],
        out_specs=(blk, blk, pl.BlockSpec((1, LANES), lambda i: (0, 0)), pl.BlockSpec((1, d), lambda i: (0, 0))),
        compiler_params=_params(("arbitrary",)), name="outproj_loss")(merged, w_out, x, target, fnw)


ELEMWISE_BLOCK_BYTES = 1 << 20


def _row_block(rows, cols, itemsize=4):
    best = None
    for tr in range(16, rows + 1, 16):
        if rows % tr == 0 and tr * cols * itemsize <= ELEMWISE_BLOCK_BYTES:
            best = tr
    return best if best is not None else rows


def _adamw(w, g, m, v, name):
    rows, cols = w.shape
    tr = _row_block(rows, cols)

    def body(w_ref, g_ref, m_ref, v_ref, d_ref, nm_ref, nv_ref):
        gv = g_ref[...]
        nm = ADAM_B1 * m_ref[...] + (1.0 - ADAM_B1) * gv
        nv = ADAM_B2 * v_ref[...] + (1.0 - ADAM_B2) * jnp.square(gv)
        m_hat = nm / (1.0 - ADAM_B1 ** ADAM_STEP)
        v_hat = nv / (1.0 - ADAM_B2 ** ADAM_STEP)
        d_ref[...] = -ADAM_LR * (m_hat / (jnp.sqrt(v_hat) + ADAM_EPS) + ADAM_WD * w_ref[...])
        nm_ref[...] = nm
        nv_ref[...] = nv

    blk = pl.BlockSpec((tr, cols), lambda i: (i, 0))
    out = SDS((rows, cols), F32)
    return pl.pallas_call(
        body, out_shape=(out, out, out), grid=(rows // tr,), in_specs=[blk] * 4, out_specs=(blk,) * 3,
        compiler_params=_params(("parallel",)), name=name)(w, g, m, v)


HBM_SPEC = pl.BlockSpec(memory_space=pl.ANY)


def _position():
    return lax.axis_index("x"), lax.axis_index("y"), lax.axis_index("c")


def _gather_chips(shards):
    n = len(shards)

    def body(*refs):
        ins, outs = refs[:n], refs[n:2 * n]
        send_sems, recv_sems, fsend_sems, frecv_sems = refs[2 * n:]
        x, y, c = _position()
        me = 2 * x + y
        peers = [(1 - x, y), (x, 1 - y), (1 - x, 1 - y)]

        def over_ici(t, p, chip):
            px, py = peers[p]
            r2 = ins[t].shape[0] // 2
            return pltpu.make_async_remote_copy(
                src_ref=ins[t].at[pl.ds(c * r2, r2), :], dst_ref=outs[t].at[chip, c], send_sem=send_sems.at[3 * t + p],
                recv_sem=recv_sems.at[3 * t + p], device_id=(px, py, c), device_id_type=MESH)

        def to_sibling(t, p, half):
            px, py = peers[p]
            slab = outs[t].at[2 * px + py, half]
            return pltpu.make_async_remote_copy(
                src_ref=slab, dst_ref=slab, send_sem=fsend_sems.at[3 * t + p], recv_sem=frecv_sems.at[3 * t + p],
                device_id=(x, y, 1 - c), device_id_type=MESH)

        sends = [over_ici(t, p, me) for t in range(n) for p in range(3)]
        for cp in sends:
            cp.start()
        passed = []
        for t in range(n):
            for p, (px, py) in enumerate(peers):
                over_ici(t, p, 2 * px + py).wait_recv()
                passed.append(to_sibling(t, p, c))
                passed[-1].start()
        for t in range(n):
            for p in range(3):
                to_sibling(t, p, 1 - c).wait_recv()
        for cp in sends + passed:
            cp.wait_send()

    return pl.pallas_call(
        body, out_shape=[SDS((N_CHIPS, 2, a.shape[0] // 2, a.shape[1]), a.dtype) for a in shards],
        in_specs=[HBM_SPEC] * n, out_specs=[HBM_SPEC] * n,
        scratch_shapes=[pltpu.SemaphoreType.DMA((3 * n,))] * 4,
        compiler_params=pltpu.CompilerParams(has_side_effects=True), name="gather_weights")(*shards)


def _with_own(gathered, own, chip):
    full = gathered.reshape((N_CHIPS,) + own.shape)
    return lax.dynamic_update_index_in_dim(full, own, chip, 0)


def _exchange_halves(grads):
    n = len(grads)

    def body(*refs):
        ins, outs = refs[:n], refs[n:2 * n]
        send_sems, recv_sems = refs[2 * n:]
        x, y, c = _position()
        cps = []
        for t in range(n):
            r2 = ins[t].shape[1] // 2
            cps.append(pltpu.make_async_remote_copy(
                src_ref=ins[t].at[:, pl.ds((1 - c) * r2, r2), :], dst_ref=outs[t],
                send_sem=send_sems.at[t], recv_sem=recv_sems.at[t], device_id=(x, y, 1 - c), device_id_type=MESH))
        for cp in cps:
            cp.start()
        for cp in cps:
            cp.wait()

    return pl.pallas_call(
        body, out_shape=[SDS((a.shape[0], a.shape[1] // 2, a.shape[2]), a.dtype) for a in grads],
        in_specs=[HBM_SPEC] * n, out_specs=[HBM_SPEC] * n,
        scratch_shapes=[pltpu.SemaphoreType.DMA((n,)), pltpu.SemaphoreType.DMA((n,))],
        compiler_params=pltpu.CompilerParams(has_side_effects=True), name="reduce_sibling")(*grads)


def _scatter_chips(parts):
    n = len(parts)

    def body(*refs):
        ins, outs = refs[:n], refs[n:2 * n]
        send_sems, recv_sems = refs[2 * n:]
        x, y, c = _position()
        me = 2 * x + y
        peers = [(1 - x, y), (x, 1 - y), (1 - x, 1 - y)]

        def remote(t, p, src_slab, dst_slab):
            px, py = peers[p]
            return pltpu.make_async_remote_copy(
                src_ref=ins[t].at[src_slab], dst_ref=outs[t].at[dst_slab], send_sem=send_sems.at[3 * t + p],
                recv_sem=recv_sems.at[3 * t + p], device_id=(px, py, c), device_id_type=MESH)

        sends = [remote(t, p, 2 * peers[p][0] + peers[p][1], me) for t in range(n) for p in range(3)]
        for cp in sends:
            cp.start()
        for t in range(n):
            for p, (px, py) in enumerate(peers):
                remote(t, p, me, 2 * px + py).wait_recv()
        for cp in sends:
            cp.wait_send()

    return pl.pallas_call(
        body, out_shape=[SDS(a.shape, a.dtype) for a in parts],
        in_specs=[HBM_SPEC] * n, out_specs=[HBM_SPEC] * n,
        scratch_shapes=[pltpu.SemaphoreType.DMA((3 * n,)), pltpu.SemaphoreType.DMA((3 * n,))],
        compiler_params=pltpu.CompilerParams(has_side_effects=True), name="reduce_chips")(*parts)


def _share_halves(halves):
    n = len(halves)

    def body(*refs):
        ins, outs = refs[:n], refs[n:2 * n]
        send_sems, recv_sems = refs[2 * n:]
        x, y, c = _position()

        def copy(t, slab):
            return pltpu.make_async_remote_copy(
                src_ref=ins[t].at[slab], dst_ref=outs[t].at[slab], send_sem=send_sems.at[t], recv_sem=recv_sems.at[t],
                device_id=(x, y, 1 - c), device_id_type=MESH)

        for t in range(n):
            copy(t, c).start()
        for t in range(n):
            copy(t, 1 - c).wait_recv()
        for t in range(n):
            copy(t, c).wait_send()

    return pl.pallas_call(
        body, out_shape=[SDS(a.shape, a.dtype) for a in halves],
        in_specs=[HBM_SPEC] * n, out_specs=[HBM_SPEC] * n,
        scratch_shapes=[pltpu.SemaphoreType.DMA((n,)), pltpu.SemaphoreType.DMA((n,))],
        input_output_aliases={t: t for t in range(n)},
        compiler_params=pltpu.CompilerParams(has_side_effects=True), name="share_sibling")(*halves)


def _add_sibling(grad, recv, core):
    nch, r2, cols = recv.shape
    tr = _row_block(r2, cols)
    nb = r2 // tr

    def body(c_ref, g_ref, r_ref, o_ref):
        del c_ref
        o_ref[...] = (g_ref[...] + r_ref[...]).astype(BF16)

    return pl.pallas_call(
        body, out_shape=SDS(recv.shape, BF16),
        grid_spec=pltpu.PrefetchScalarGridSpec(
            num_scalar_prefetch=1, grid=(nch, nb),
            in_specs=[pl.BlockSpec((None, tr, cols), lambda j, i, c_ref: (j, c_ref[0] * nb + i, 0)),
                      pl.BlockSpec((None, tr, cols), lambda j, i, c_ref: (j, i, 0))],
            out_specs=pl.BlockSpec((None, tr, cols), lambda j, i, c_ref: (j, i, 0))),
        compiler_params=_params(("parallel", "parallel")), name="add_sibling")(core, grad, recv)


def _add_chips(own, recv, chip_core):
    nch, r2, cols = recv.shape
    tr = _row_block(r2, cols)

    def body(cc_ref, own_ref, *refs):
        p_refs, o_ref = refs[:nch], refs[nch]
        me = cc_ref[0]
        acc = None
        for j in range(nch):
            term = jnp.where(me == j, own_ref[...], p_refs[j][...]).astype(F32)
            acc = term if acc is None else acc + term
        o_ref[...] = acc

    def slab(j):
        return pl.BlockSpec((None, tr, cols), lambda i, cc: (cc[2 + j], i, 0))

    return pl.pallas_call(
        body, out_shape=SDS((2, r2, cols), F32),
        grid_spec=pltpu.PrefetchScalarGridSpec(
            num_scalar_prefetch=1, grid=(r2 // tr,),
            in_specs=[pl.BlockSpec((None, tr, cols), lambda i, cc: (cc[0], i, 0))] + [slab(j) for j in range(nch)],
            out_specs=pl.BlockSpec((None, tr, cols), lambda i, cc: (cc[1], i, 0))),
        compiler_params=_params(("parallel",)), name="add_chips")(chip_core, own, *([recv] * nch))


def _allreduce_small(pack):
    rows = pack.shape[0]

    def body(p_ref, o_ref, buf, send_sems, recv_sems):
        x, y, c = _position()
        me = 4 * x + 2 * y + c
        buf[me] = p_ref[...]

        def copy(dst_dev, slot):
            return pltpu.make_async_remote_copy(
                src_ref=p_ref, dst_ref=buf.at[slot], send_sem=send_sems.at[dst_dev], recv_sem=recv_sems.at[slot],
                device_id=(dst_dev // 4, (dst_dev // 2) % 2, dst_dev % 2), device_id_type=MESH)

        for dev in range(N_DEV):
            @pl.when(dev != me)
            def _():
                copy(dev, me).start()
        for dev in range(N_DEV):
            @pl.when(dev != me)
            def _():
                copy(dev, dev).wait_recv()
        for dev in range(N_DEV):
            @pl.when(dev != me)
            def _():
                copy(dev, me).wait_send()
        acc = buf[0]
        for dev in range(1, N_DEV):
            acc = acc + buf[dev]
        o_ref[...] = acc

    return pl.pallas_call(
        body, out_shape=SDS(pack.shape, F32),
        in_specs=[pl.BlockSpec(memory_space=pltpu.VMEM)], out_specs=pl.BlockSpec(memory_space=pltpu.VMEM),
        scratch_shapes=[pltpu.VMEM((N_DEV, rows, LANES), F32), pltpu.SemaphoreType.DMA((N_DEV,)),
                        pltpu.SemaphoreType.DMA((N_DEV,))],
        compiler_params=pltpu.CompilerParams(has_side_effects=True), name="allreduce_small")(pack)


ATTN_TQ = 256


def _local_step(cfg, x, target, w):
    d = cfg.D
    win = ATTN_WINDOW
    hn = _rmsnorm_fwd(x, w["norm_w"])
    proj = _mm(hn, w["w_main"], "nn", BF16, "proj_main")
    dt_raw = _mm(hn, w["w_dt"], "nn", F32, "proj_dt")
    slopes = _slopes(cfg.H)
    near = _Pass(ATTN_TQ, DILATED_PATTERNS[:-1], 1, cfg.S)
    far = _Pass(LANES, DILATED_PATTERNS[-1:], DEINT, cfg.S // DEINT)
    tab_near, tab_far = _attn_tables(near), _attn_tables(far)
    cols_near, cols_far = (cfg.OQ, cfg.OK, cfg.OV), (0, d, 2 * d)
    qkv_far = _deinterleave(proj, 0, 3 * d, "attn_deinterleave")
    o_1, lse_1 = _attn_fwd(cfg, near, proj, cols_near, tab_near, slopes, "attn_fwd_near")
    o_2, lse_2 = _attn_fwd(cfg, far, qkv_far, cols_far, tab_far, slopes, "attn_fwd_far")
    o_a, oag, lse = _attn_merge(cfg, proj, o_1, lse_1, o_2, lse_2)
    xact = _conv_fwd(cfg, proj, w["conv_w"], w["conv_b"])
    e = _expansion_matrix(cfg)
    y, y2, states = _ssd_fwd(cfg, xact, dt_raw, proj, w["dt_bias"], w["a_log"], w["d_skip"], w["ssm_norm_w"], e)
    a_br = _mm(oag, w["w_attn"], "nn", BF16, "branch_attn")
    s_br = _mm(y2, w["w_ssm"], "nn", BF16, "branch_ssm")
    merged = _merge_fwd(cfg, proj, a_br, s_br)
    dout_f, dout_b, loss_row, g_fnw = _outproj_loss(merged, w["w_out"], x, target, w["final_norm_w"])

    dmerged = _mm(dout_b, w["w_out"], "nt", BF16, "d_merged")
    g_w_out = _mm(merged, dout_b, "tn", F32, "g_w_out")
    dproj, da_br = _merge_bwd(cfg, proj, a_br, dmerged, cfg.OGA, None, "merge_bwd_attn")
    dproj, ds_br = _merge_bwd(cfg, proj, s_br, dmerged, cfg.OGS, dproj, "merge_bwd_ssm")
    doag = _mm(da_br, w["w_attn"], "nt", BF16, "d_oag")
    g_w_attn = _mm(oag, da_br, "tn", F32, "g_w_attn")
    dy2 = _mm(ds_br, w["w_ssm"], "nt", BF16, "d_y2")
    g_w_ssm = _mm(y2, ds_br, "tn", F32, "g_w_ssm")
    dproj, dxact, ddt, g_snw, g_dtb, g_alog, g_dsk = _ssd_bwd(
        cfg, xact, dt_raw, proj, y, dy2, states, w["dt_bias"], w["a_log"], w["d_skip"], w["ssm_norm_w"], e, dproj)
    dproj, g_cw, g_cb = _conv_bwd(cfg, proj, dxact, w["conv_w"], w["conv_b"], dproj)
    dproj, do, do_far, dl, dl_far, lse_far = _attn_bwd_prep(cfg, proj, o_a, doag, lse, dproj)
    g_near = _attn_bwd(cfg, near, proj, cols_near, do, lse, dl, tab_near, slopes, "attn_bwd_near")
    g_far = _attn_bwd(cfg, far, qkv_far, cols_far, do_far, lse_far, dl_far, tab_far, slopes, "attn_bwd_far")
    for g_1, g_2, col0, nm in zip(g_near, g_far, cols_near, ("attn_dq", "attn_dk", "attn_dv")):
        dproj = _attn_grad_sum(cfg, g_1, g_2, col0, dproj, nm)
    ddt_b = ddt.astype(BF16)
    dhn = _mm(dproj, w["w_main"], "nt", F32, "d_hn", init=_mm(ddt_b, w["w_dt"], "nt", F32, "d_hn_dt"))
    g_w_main = _mm(hn, dproj, "tn", F32, "g_w_main")
    g_w_dt = _mm(hn, ddt_b, "tn", F32, "g_w_dt")
    grad_x, g_nw = _rmsnorm_bwd(x, w["norm_w"], dhn, dout_f)
    grads = dict(norm_w=g_nw, w_main=g_w_main, w_dt=g_w_dt, conv_w=g_cw, conv_b=g_cb, dt_bias=g_dtb, a_log=g_alog,
                 d_skip=g_dsk, ssm_norm_w=g_snw, w_attn=g_w_attn, w_ssm=g_w_ssm, w_out=g_w_out, final_norm_w=g_fnw)
    return loss_row, grad_x, grads


def _pad_lanes(v):
    return jnp.pad(v, ((0, 0), (0, LANES - v.shape[1])))


def _full_weights(cfg, norm_w, w_in, conv_w, conv_b, dt_bias, a_log, d_skip, ssm_norm_w, w_attn, w_ssm, w_out, fnw):
    w_main = jnp.concatenate([w_in[:, :cfg.OGA], w_in[:, cfg.OGA + cfg.NH:]], axis=1).astype(BF16)
    w_dt = _pad_lanes(w_in[:, cfg.OGA:cfg.OGA + cfg.NH]).astype(BF16)
    return dict(norm_w=norm_w, w_main=w_main, w_dt=w_dt, conv_w=conv_w, conv_b=conv_b, dt_bias=_pad_lanes(dt_bias),
                a_log=_pad_lanes(a_log), d_skip=_pad_lanes(d_skip), ssm_norm_w=ssm_norm_w, w_attn=w_attn.astype(BF16),
                w_ssm=w_ssm.astype(BF16), w_out=w_out.astype(BF16), final_norm_w=fnw)


def _grad_w_in(cfg, grads):
    return jnp.concatenate([grads["w_main"][:, :cfg.OGA], grads["w_dt"][:, :cfg.NH], grads["w_main"][:, cfg.OGA:]], axis=1)


def kernel(x, norm_w, w_in, conv_w, conv_b, dt_bias, a_log, d_skip, ssm_norm_w, w_attn_branch, w_ssm_branch, w_out, final_norm_w, loss_target, m_norm_w, m_w_in, m_conv_w, m_conv_b, m_dt_bias, m_a_log, m_d_skip, m_ssm_norm_w, m_w_attn_branch, m_w_ssm_branch, m_w_out, m_final_norm_w, v_norm_w, v_w_in, v_conv_w, v_conv_b, v_dt_bias, v_a_log, v_d_skip, v_ssm_norm_w, v_w_attn_branch, v_w_ssm_branch, v_w_out, v_final_norm_w):
    cfg = _Cfg(x.shape[1], x.shape[2])
    d, si, cd, nh = cfg.D, cfg.SI, cfg.CD, cfg.NH
    chip = 2 * lax.axis_index("x") + lax.axis_index("y")
    core = lax.axis_index("c").astype(jnp.int32).reshape(1)
    slabs = jnp.arange(N_CHIPS, dtype=jnp.int32)
    chip_core = jnp.concatenate([chip.astype(jnp.int32).reshape(1), core,
                                 jnp.where(slabs == chip, (slabs + 1) % N_CHIPS, slabs)])

    own = [w_in[0].astype(BF16), w_attn_branch[0].astype(BF16), w_ssm_branch[0].astype(BF16), w_out[0].astype(BF16),
           conv_w[0].reshape(4 * CONV_K, -1)]
    a_in, a_attn, a_ssm, a_out, a_cw = [_with_own(g, o, chip) for g, o in zip(_gather_chips(own), own)]
    w_in_full = a_in.transpose(1, 0, 2).reshape(d, cfg.N_IN)
    conv_w_full = a_cw.reshape(N_CHIPS, CONV_K, cd // N_CHIPS).transpose(1, 0, 2).reshape(CONV_K, cd)
    w = _full_weights(cfg, norm_w, w_in_full, conv_w_full, conv_b, dt_bias, a_log, d_skip, ssm_norm_w,
                      a_attn.reshape(d, d), a_ssm.reshape(si, d), a_out.reshape(d, d), final_norm_w.reshape(1, d))

    loss_row, grad_x, grads = _local_step(cfg, x[0], loss_target[0], w)

    by_chip = [_grad_w_in(cfg, grads).reshape(d, N_CHIPS, cfg.N_IN // N_CHIPS).transpose(1, 0, 2),
               grads["w_attn"].reshape(N_CHIPS, d // N_CHIPS, d),
               grads["w_ssm"].reshape(N_CHIPS, si // N_CHIPS, d),
               grads["w_out"].reshape(N_CHIPS, d // N_CHIPS, d)]
    from_sibling = _exchange_halves(by_chip)
    chip_sums = [_add_sibling(g, r, core) for g, r in zip(by_chip, from_sibling)]
    from_chips = _scatter_chips(chip_sums)
    halves = [_add_chips(o, p, chip_core) for o, p in zip(chip_sums, from_chips)]
    g_in, g_attn, g_ssm, g_out = [h.reshape(2 * h.shape[1], h.shape[2]) for h in _share_halves(halves)]

    small = [loss_row, grads["norm_w"], grads["conv_b"], grads["dt_bias"], grads["a_log"], grads["d_skip"],
             grads["ssm_norm_w"], grads["final_norm_w"], grads["conv_w"].reshape(1, CONV_K * cd)]
    sizes = [a.shape[1] for a in small]
    total = sum(sizes)
    rows = -(-total // (8 * LANES)) * 8
    flat = jnp.pad(jnp.concatenate(small, axis=1), ((0, 0), (0, rows * LANES - total)))
    red = _allreduce_small(flat.reshape(rows, LANES)).reshape(1, rows * LANES)
    offs = [sum(sizes[:i]) for i in range(len(sizes))]
    loss_r, g_nw, g_cb, g_dtb, g_alog, g_dsk, g_snw, g_fnw, g_cw_flat = [
        red[:, o:o + n] for o, n in zip(offs, sizes)]
    loss = loss_r[0, 0]
    g_dtb, g_alog, g_dsk = g_dtb[:, :nh], g_alog[:, :nh], g_dsk[:, :nh]
    cshard = cd // N_CHIPS
    g_cw = lax.dynamic_slice_in_dim(g_cw_flat.reshape(CONV_K, cd), chip * cshard, cshard, axis=1)

    upd = {}
    for name, wv, gv, mv, vv in [("w_in", w_in[0], g_in, m_w_in[0], v_w_in[0]),
                                 ("w_attn", w_attn_branch[0], g_attn, m_w_attn_branch[0], v_w_attn_branch[0]),
                                 ("w_ssm", w_ssm_branch[0], g_ssm, m_w_ssm_branch[0], v_w_ssm_branch[0]),
                                 ("w_out", w_out[0], g_out, m_w_out[0], v_w_out[0])]:
        upd[name] = _adamw(wv, gv, mv, vv, "adamw_" + name)
    names = ["norm_w", "conv_w", "conv_b", "dt_bias", "a_log", "d_skip", "ssm_norm_w", "final_norm_w"]
    ws = [norm_w, conv_w[0].reshape(1, -1), conv_b, dt_bias, a_log, d_skip, ssm_norm_w, final_norm_w.reshape(1, d)]
    gs = [g_nw, g_cw.reshape(1, -1), g_cb, g_dtb, g_alog, g_dsk, g_snw, g_fnw]
    ms = [m_norm_w, m_conv_w[0].reshape(1, -1), m_conv_b, m_dt_bias, m_a_log, m_d_skip, m_ssm_norm_w,
          m_final_norm_w.reshape(1, d)]
    vs = [v_norm_w, v_conv_w[0].reshape(1, -1), v_conv_b, v_dt_bias, v_a_log, v_d_skip, v_ssm_norm_w,
          v_final_norm_w.reshape(1, d)]
    ssz = [a.shape[1] for a in ws]
    stot = sum(ssz)
    srows = -(-stot // (8 * LANES)) * 8

    def pack(parts):
        return jnp.pad(jnp.concatenate(parts, axis=1), ((0, 0), (0, srows * LANES - stot))).reshape(srows, LANES)

    packed = _adamw(pack(ws), pack(gs), pack(ms), pack(vs), "adamw_small")
    soffs = [sum(ssz[:i]) for i in range(len(ssz))]
    for k, nm in enumerate(names):
        upd[nm] = tuple(p.reshape(1, srows * LANES)[:, soffs[k]:soffs[k] + ssz[k]] for p in packed)

    shapes = dict(norm_w=norm_w.shape, w_in=w_in.shape, conv_w=conv_w.shape, conv_b=conv_b.shape, dt_bias=dt_bias.shape,
                  a_log=a_log.shape, d_skip=d_skip.shape, ssm_norm_w=ssm_norm_w.shape, w_attn=w_attn_branch.shape,
                  w_ssm=w_ssm_branch.shape, w_out=w_out.shape, final_norm_w=final_norm_w.shape)
    order = ["norm_w", "w_in", "conv_w", "conv_b", "dt_bias", "a_log", "d_skip", "ssm_norm_w", "w_attn", "w_ssm",
             "w_out", "final_norm_w"]
    gradv = dict(norm_w=g_nw, w_in=g_in, conv_w=g_cw, conv_b=g_cb, dt_bias=g_dtb, a_log=g_alog, d_skip=g_dsk,
                 ssm_norm_w=g_snw, w_attn=g_attn, w_ssm=g_ssm, w_out=g_out, final_norm_w=g_fnw)
    outs = [loss, grad_x[None]]
    outs += [gradv[n].reshape(shapes[n]) for n in order]
    for k in range(3):
        outs += [upd[n][k].reshape(shapes[n]) for n in order]
    return tuple(outs)
```

```python
import functools
import math

import jax
import jax.numpy as jnp
from jax import lax
from jax.experimental import pallas as pl
from jax.experimental.pallas import tpu as pltpu

F32 = jnp.float32
BF16 = jnp.bfloat16
SDS = jax.ShapeDtypeStruct

RMS_EPS = 1e-6
LANES = 128
CHUNK = 128
SSM_HEAD_DIM = 64
SSM_GROUPS = 8
SSM_STATE = 128
CONV_K = 4
ATTN_HEAD_DIM = 128
DILATED_PATTERNS = ((128, 1), (512, 4), (2048, 16))
ATTN_WINDOW = max(w for w, _ in DILATED_PATTERNS)
NEG = -1e30
VMEM_LIMIT = 56 * 1024 * 1024
ADAM_LR, ADAM_B1, ADAM_B2, ADAM_EPS, ADAM_WD, ADAM_STEP = 0.001, 0.9, 0.999, 1e-08, 0.01, 10
MESH = pl.DeviceIdType.MESH
N_CHIPS = 4
N_DEV = 8


class _Cfg:
    def __init__(self, s, d):
        self.S, self.D = s, d
        self.H = d // ATTN_HEAD_DIM
        self.SI = 2 * d
        self.NH = self.SI // SSM_HEAD_DIM
        self.HPG = self.NH // SSM_GROUPS
        self.GW = self.HPG * SSM_HEAD_DIM
        self.BC = SSM_GROUPS * SSM_STATE
        self.CD = self.SI + 2 * self.BC
        self.OQ, self.OK, self.OV, self.OZA = 0, d, 2 * d, 3 * d
        self.OZS = 4 * d
        self.OXBC = self.OZS + self.SI
        self.OGA = self.OXBC + self.CD
        self.OGS = self.OGA + d
        self.NM = self.OGS + d
        self.N_IN = self.NM + self.NH
        assert self.GW % LANES == 0 and self.NH <= LANES and s % 512 == 0 and d % 512 == 0


def _params(sem=None):
    return pltpu.CompilerParams(dimension_semantics=sem, vmem_limit_bytes=VMEM_LIMIT)


def _sigmoid(x):
    return 1.0 / (1.0 + jnp.exp(-x))


def _softplus(x):
    u = jnp.exp(-jnp.abs(x))
    l1p = jnp.where(u < 1e-3, u * (1.0 - u * (0.5 - u * (1.0 / 3.0))), jnp.log(1.0 + u))
    return jnp.maximum(x, 0.0) + l1p


def _nt(a, b):
    return lax.dot_general(a, b, (((1,), (1,)), ((), ())), preferred_element_type=F32)


def _tn(a, b):
    return lax.dot_general(a, b, (((0,), (0,)), ((), ())), preferred_element_type=F32)


def _nn(a, b):
    return jnp.dot(a, b, preferred_element_type=F32)


def _tile(n, target):
    if n <= target:
        return n
    best = None
    for t in range(LANES, target + 1, LANES):
        if n % t == 0:
            best = t
    assert best is not None, (n, target)
    return best


def _mm(a, b, dims, out_dtype, name, tm=1024, tn=2048, tk=512, init=None):
    if dims == "nn":
        (m, k), (k2, n) = a.shape, b.shape
    elif dims == "nt":
        (m, k), (n, k2) = a.shape, b.shape
    else:
        (k, m), (k2, n) = a.shape, b.shape
    assert k == k2
    tm, tn, tk = _tile(m, tm), _tile(n, tn), _tile(k, tk)
    nk = k // tk
    if dims == "tn":
        a_spec = pl.BlockSpec((tk, tm), lambda i, j, kk: (kk, i))
    else:
        a_spec = pl.BlockSpec((tm, tk), lambda i, j, kk: (i, kk))
    if dims == "nt":
        b_spec = pl.BlockSpec((tn, tk), lambda i, j, kk: (j, kk))
    else:
        b_spec = pl.BlockSpec((tk, tn), lambda i, j, kk: (kk, j))
    o_spec = pl.BlockSpec((tm, tn), lambda i, j, kk: (i, j))
    op = {"nn": _nn, "nt": _nt, "tn": _tn}[dims]
    has_init = init is not None

    def body(*refs):
        if has_init:
            a_ref, b_ref, i_ref, o_ref, acc = refs
        else:
            a_ref, b_ref, o_ref, acc = refs
        kk = pl.program_id(2)

        @pl.when(kk == 0)
        def _():
            acc[...] = i_ref[...].astype(F32) if has_init else jnp.zeros_like(acc)

        acc[...] += op(a_ref[...], b_ref[...])

        @pl.when(kk == nk - 1)
        def _():
            o_ref[...] = acc[...].astype(out_dtype)

    in_specs = [a_spec, b_spec] + ([o_spec] if has_init else [])
    args = (a, b) + ((init,) if has_init else ())
    return pl.pallas_call(
        body, out_shape=SDS((m, n), out_dtype), grid=(m // tm, n // tn, nk),
        in_specs=in_specs, out_specs=o_spec, scratch_shapes=[pltpu.VMEM((tm, tn), F32)],
        compiler_params=_params(("parallel", "parallel", "arbitrary")), name=name)(*args)


def _rmsnorm_fwd(x, w):
    s, d = x.shape
    tr = 256

    def body(x_ref, w_ref, o_ref):
        xv = x_ref[...]
        r = lax.rsqrt(jnp.mean(xv * xv, axis=-1, keepdims=True) + RMS_EPS)
        o_ref[...] = (xv * r * w_ref[...]).astype(BF16)

    return pl.pallas_call(
        body, out_shape=SDS((s, d), BF16), grid=(s // tr,),
        in_specs=[pl.BlockSpec((tr, d), lambda i: (i, 0)), pl.BlockSpec((1, d), lambda i: (0, 0))],
        out_specs=pl.BlockSpec((tr, d), lambda i: (i, 0)),
        compiler_params=_params(("parallel",)), name="rmsnorm_fwd")(x, w)


def _rmsnorm_bwd(x, w, dhn, dout):
    s, d = x.shape
    tr = 256

    def body(x_ref, w_ref, dh_ref, do_ref, gx_ref, gw_ref):
        xv = x_ref[...]
        r = lax.rsqrt(jnp.mean(xv * xv, axis=-1, keepdims=True) + RMS_EPS)
        nrm = xv * r
        dh = dh_ref[...]
        gy = dh * w_ref[...]
        gx_ref[...] = do_ref[...] + r * (gy - nrm * jnp.mean(gy * nrm, axis=-1, keepdims=True))

        @pl.when(pl.program_id(0) == 0)
        def _():
            gw_ref[...] = jnp.zeros_like(gw_ref)

        gw_ref[...] += jnp.sum(dh * nrm, axis=0, keepdims=True)

    blk = pl.BlockSpec((tr, d), lambda i: (i, 0))
    row = pl.BlockSpec((1, d), lambda i: (0, 0))
    return pl.pallas_call(
        body, out_shape=(SDS((s, d), F32), SDS((1, d), F32)), grid=(s // tr,),
        in_specs=[blk, row, blk, blk], out_specs=(blk, row),
        compiler_params=_params(("arbitrary",)), name="rmsnorm_bwd")(x, w, dhn, dout)


DEINT = DILATED_PATTERNS[-1][1]
DEINT_ROWS = DEINT * LANES


class _Pass:
    def __init__(self, tq, patterns, unit, seg_len):
        self.tq, self.patterns, self.unit, self.seg_len = tq, patterns, unit, seg_len
        self.win = max(w for w, _ in patterns) // unit
        self.w = self.win + tq
        assert self.win % tq == 0


def _attn_tables(ps):
    i = jnp.arange(ps.tq, dtype=jnp.int32)[:, None]
    j = jnp.arange(ps.w, dtype=jnp.int32)[None, :]
    delta = (i + ps.win - j) * ps.unit
    n = jnp.zeros((ps.tq, ps.w), F32)
    for window, dil in ps.patterns:
        n = n + ((delta >= 0) & (delta <= window) & (delta % dil == 0)).astype(F32)
    logn = jnp.where(n > 0, jnp.log(jnp.maximum(n, 1.0)), NEG)
    return logn, jnp.maximum(delta, 0).astype(F32)


def _slopes(h):
    s = jnp.asarray([2.0 ** (-8.0 * (i + 1) / h) for i in range(h)], F32)
    return jnp.broadcast_to(s[:, None, None], (h, 1, LANES))


def _masked_logn(ps, logn_ref, start):
    col = lax.broadcasted_iota(jnp.int32, (ps.tq, ps.w), 1)
    return jnp.where(col >= ps.win - lax.rem(start, ps.seg_len), logn_ref[...], NEG)


def _head_cols(hh):
    return pl.ds(pl.multiple_of(hh * ATTN_HEAD_DIM, LANES), ATTN_HEAD_DIM)


def _head_window(refs, cs):
    return jnp.concatenate([r[:, cs] for r in refs], axis=0)


def _head_scores(q_ref, kw, cs, base, dist_ref, slope_ref, hh):
    return _nt(q_ref[:, cs], kw) * (ATTN_HEAD_DIM ** -0.5) + (base - slope_ref[hh][0:1, 0:1] * dist_ref[...])


def _lane_of(stat, hh):
    lane = lax.broadcasted_iota(jnp.int32, stat.shape, 1)
    return jnp.sum(jnp.where(lane == hh, stat, 0.0), axis=1, keepdims=True)


def _window_specs(ps, d, col, nb):
    nprev = ps.win // ps.tq
    return [pl.BlockSpec((ps.tq, d), lambda i, b=b: (jnp.maximum(jnp.minimum(i, nb - 1) - (nprev - b), 0), col))
            for b in range(nprev + 1)]


def _attn_fwd(cfg, ps, qkv, cols, tables, slopes, name):
    s, h, d = cfg.S, cfg.H, cfg.D
    tq, nw = ps.tq, ps.win // ps.tq + 1
    nb = s // tq
    logn, dist = tables
    qc, kc, vc = [c // d for c in cols]

    def body(*refs):
        q_ref, k_refs, v_refs = refs[0], refs[1:1 + nw], refs[1 + nw:1 + 2 * nw]
        logn_ref, dist_ref, slope_ref, o_ref, lse_ref = refs[1 + 2 * nw:]
        base = _masked_logn(ps, logn_ref, pl.program_id(0) * tq)
        lane = lax.broadcasted_iota(jnp.int32, (tq, LANES), 1)

        def head(hh, lse):
            cs = _head_cols(hh)
            sc = _head_scores(q_ref, _head_window(k_refs, cs), cs, base, dist_ref, slope_ref, hh)
            m = jnp.max(sc, axis=1, keepdims=True)
            p = jnp.exp(sc - m)
            l = jnp.sum(p, axis=1, keepdims=True)
            o_ref[:, cs] = (_nn(p.astype(BF16), _head_window(v_refs, cs)) / l).astype(BF16)
            return jnp.where(lane == hh, m + jnp.log(l), lse)

        lse_ref[...] = lax.fori_loop(0, h, head, jnp.zeros((tq, LANES), F32))

    tab = pl.BlockSpec((tq, ps.w), lambda i: (0, 0))
    return pl.pallas_call(
        body, out_shape=(SDS((s, d), BF16), SDS((s, LANES), F32)), grid=(nb,),
        in_specs=[pl.BlockSpec((tq, d), lambda i: (i, qc))] + _window_specs(ps, d, kc, nb) + _window_specs(ps, d, vc, nb)
        + [tab, tab, pl.BlockSpec((h, 1, LANES), lambda i: (0, 0, 0))],
        out_specs=(pl.BlockSpec((tq, d), lambda i: (i, 0)), pl.BlockSpec((tq, LANES), lambda i: (i, 0))),
        compiler_params=_params(("parallel",)), name=name)(*([qkv] * (1 + 2 * nw)), logn, dist, slopes)


def _attn_bwd(cfg, ps, qkv, cols, do, lse, delta, tables, slopes, name):
    s, h, d = cfg.S, cfg.H, cfg.D
    tq, nprev = ps.tq, ps.win // ps.tq
    nw = nprev + 1
    nb = s // tq
    logn, dist = tables
    qc, kc, vc = [c // d for c in cols]
    scale = ATTN_HEAD_DIM ** -0.5

    def body(*refs):
        q_ref, k_refs, v_refs = refs[0], refs[1:1 + nw], refs[1 + nw:1 + 2 * nw]
        do_ref, lse_ref, dl_ref, logn_ref, dist_ref, slope_ref, dq_ref, dk_ref, dv_ref, ck, cv = refs[1 + 2 * nw:]
        i = pl.program_id(0)
        slot = lambda b: lax.rem(i + b, nprev)

        @pl.when(i == 0)
        def _():
            ck[...] = jnp.zeros_like(ck)
            cv[...] = jnp.zeros_like(cv)

        @pl.when(i < nb)
        def _():
            base = _masked_logn(ps, logn_ref, i * tq)
            lse_all, dl_all = lse_ref[...], dl_ref[...]

            def head(hh, carry):
                cs = _head_cols(hh)
                kw, vw = _head_window(k_refs, cs), _head_window(v_refs, cs)
                sc = _head_scores(q_ref, kw, cs, base, dist_ref, slope_ref, hh)
                p = jnp.exp(sc - _lane_of(lse_all, hh))
                dob = do_ref[:, cs]
                ds = (p * (_nt(dob, vw) - _lane_of(dl_all, hh)) * scale).astype(BF16)
                dq_ref[:, cs] = _nn(ds, kw).astype(BF16)
                dkw = _tn(ds, q_ref[:, cs])
                dvw = _tn(p.astype(BF16), dob)
                dk_ref[:, cs] = ck[slot(0), :, cs] + dkw[0:tq]
                dv_ref[:, cs] = cv[slot(0), :, cs] + dvw[0:tq]
                for b in range(1, nprev):
                    ck[slot(b), :, cs] += dkw[b * tq:(b + 1) * tq]
                    cv[slot(b), :, cs] += dvw[b * tq:(b + 1) * tq]
                ck[slot(0), :, cs] = dkw[nprev * tq:]
                cv[slot(0), :, cs] = dvw[nprev * tq:]
                return carry

            lax.fori_loop(0, h, head, 0)

        @pl.when(i >= nb)
        def _():
            dk_ref[...] = ck[slot(0)]
            dv_ref[...] = cv[slot(0)]

    here = lambda i: jnp.minimum(i, nb - 1)
    blk = pl.BlockSpec((tq, d), lambda i: (here(i), 0))
    stat = pl.BlockSpec((tq, LANES), lambda i: (here(i), 0))
    late = pl.BlockSpec((tq, d), lambda i: (jnp.maximum(i - nprev, 0), 0))
    tab = pl.BlockSpec((tq, ps.w), lambda i: (0, 0))
    return pl.pallas_call(
        body, out_shape=(SDS((s, d), BF16), SDS((s, d), F32), SDS((s, d), F32)), grid=(nb + nprev,),
        in_specs=[pl.BlockSpec((tq, d), lambda i: (here(i), qc))] + _window_specs(ps, d, kc, nb)
        + _window_specs(ps, d, vc, nb) + [blk, stat, stat, tab, tab, pl.BlockSpec((h, 1, LANES), lambda i: (0, 0, 0))],
        out_specs=(blk, late, late),
        scratch_shapes=[pltpu.VMEM((nprev, tq, d), F32), pltpu.VMEM((nprev, tq, d), F32)],
        compiler_params=_params(("arbitrary",)), name=name)(
            *([qkv] * (1 + 2 * nw)), do, lse, delta, logn, dist, slopes)


def _by_residue(a):
    return a.reshape(DEINT, a.shape[0] // DEINT, a.shape[1])


def _deint_spec(colblock):
    return pl.BlockSpec((DEINT, LANES, LANES), lambda b, j: (0, b, colblock(j)))


def _deint_rows(scr, out_ref, dtype):
    for r in range(DEINT):
        out_ref[r] = scr[pl.ds(r, LANES, stride=DEINT), :].astype(dtype)


def _int_rows(in_ref, scr):
    for r in range(DEINT):
        scr[pl.ds(r, LANES, stride=DEINT), :] = in_ref[r].astype(F32)


def _deinterleave(x, col0, ncols, name):
    s = x.shape[0]
    c0 = col0 // LANES

    def body(x_ref, o_ref, scr):
        scr[...] = x_ref[...].astype(F32)
        _deint_rows(scr, o_ref, x.dtype)

    out = pl.pallas_call(
        body, out_shape=SDS((DEINT, s // DEINT, ncols), x.dtype), grid=(s // DEINT_ROWS, ncols // LANES),
        in_specs=[pl.BlockSpec((DEINT_ROWS, LANES), lambda b, j: (b, c0 + j))],
        out_specs=_deint_spec(lambda j: j),
        scratch_shapes=[pltpu.VMEM((DEINT_ROWS, LANES), F32)],
        compiler_params=_params(("parallel", "parallel")), name=name)(x)
    return out.reshape(s, ncols)


def _attn_merge(cfg, proj, o_1, lse_1, o_2, lse_2):
    s, h = cfg.S, cfg.H
    zb = cfg.OZA // LANES
    rows = DEINT_ROWS

    def body(o1_ref, l1_ref, o2_ref, l2_ref, z_ref, o_ref, og_ref, lse_ref, so, sl):
        hh = pl.program_id(1)
        _int_rows(o2_ref, so)

        @pl.when(hh == 0)
        def _():
            _int_rows(l2_ref, sl)

        l1, l2 = _lane_of(l1_ref[...], hh), _lane_of(sl[...], hh)
        mx = jnp.maximum(l1, l2)
        w1, w2 = jnp.exp(l1 - mx), jnp.exp(l2 - mx)
        den = w1 + w2
        o = (w1 * o1_ref[...].astype(F32) + w2 * so[...]) / den
        z = z_ref[...].astype(F32)
        o_ref[...] = o.astype(BF16)
        og_ref[...] = (o * (z * _sigmoid(z))).astype(BF16)

        @pl.when(hh == 0)
        def _():
            lse_ref[...] = jnp.zeros_like(lse_ref)

        lane = lax.broadcasted_iota(jnp.int32, (rows, LANES), 1)
        lse_ref[...] = jnp.where(lane == hh, mx + jnp.log(den), lse_ref[...])

    blk = pl.BlockSpec((rows, LANES), lambda b, j: (b, j))
    return pl.pallas_call(
        body, out_shape=(SDS((s, cfg.D), BF16), SDS((s, cfg.D), BF16), SDS((s, LANES), F32)),
        grid=(s // rows, h),
        in_specs=[blk, pl.BlockSpec((rows, LANES), lambda b, j: (b, 0)), _deint_spec(lambda j: j),
                  _deint_spec(lambda j: 0), pl.BlockSpec((rows, LANES), lambda b, j: (b, zb + j))],
        out_specs=(blk, blk, pl.BlockSpec((rows, LANES), lambda b, j: (b, 0))),
        scratch_shapes=[pltpu.VMEM((rows, LANES), F32), pltpu.VMEM((rows, LANES), F32)],
        compiler_params=_params(("parallel", "arbitrary")), name="attn_merge")(
            o_1, lse_1, _by_residue(o_2), _by_residue(lse_2), proj)


def _attn_bwd_prep(cfg, proj, o_a, doag, lse, dproj):
    s, h = cfg.S, cfg.H
    zb = cfg.OZA // LANES
    rows = DEINT_ROWS

    def body(o_ref, dg_ref, z_ref, lse_ref, dp_in, dz_ref, do_ref, do2_ref, dl_ref, dl2_ref, lse2_ref, scr):
        del dp_in
        hh = pl.program_id(1)
        z = z_ref[...].astype(F32)
        sg = _sigmoid(z)
        o = o_ref[...].astype(F32)
        dg = dg_ref[...].astype(F32)
        do = dg * (z * sg)
        dz_ref[...] = (dg * o * (sg * (1.0 + z * (1.0 - sg)))).astype(BF16)
        do_ref[...] = do.astype(BF16)
        scr[...] = do
        _deint_rows(scr, do2_ref, BF16)

        @pl.when(hh == 0)
        def _():
            dl_ref[...] = jnp.zeros_like(dl_ref)

        lane = lax.broadcasted_iota(jnp.int32, (rows, LANES), 1)
        dl_ref[...] = jnp.where(lane == hh, jnp.sum(do * o, axis=1, keepdims=True), dl_ref[...])

        @pl.when(hh == h - 1)
        def _():
            scr[...] = dl_ref[...]
            _deint_rows(scr, dl2_ref, F32)
            scr[...] = lse_ref[...]
            _deint_rows(scr, lse2_ref, F32)

    blk = pl.BlockSpec((rows, LANES), lambda b, j: (b, j))
    stat = pl.BlockSpec((rows, LANES), lambda b, j: (b, 0))
    stat2 = _deint_spec(lambda j: 0)
    outs = pl.pallas_call(
        body,
        out_shape=(SDS(dproj.shape, BF16), SDS((s, cfg.D), BF16), SDS((DEINT, s // DEINT, cfg.D), BF16),
                   SDS((s, LANES), F32), SDS((DEINT, s // DEINT, LANES), F32), SDS((DEINT, s // DEINT, LANES), F32)),
        grid=(s // rows, h),
        in_specs=[blk, blk, pl.BlockSpec((rows, LANES), lambda b, j: (b, zb + j)), stat, HBM_SPEC],
        out_specs=(pl.BlockSpec((rows, LANES), lambda b, j: (b, zb + j)), blk, _deint_spec(lambda j: j),
                   stat, stat2, stat2),
        scratch_shapes=[pltpu.VMEM((rows, LANES), F32)],
        input_output_aliases={4: 0},
        compiler_params=_params(("parallel", "arbitrary")), name="attn_bwd_prep")(o_a, doag, proj, lse, dproj)
    dproj, do, do2, dl, dl2, lse2 = outs
    return dproj, do, do2.reshape(s, cfg.D), dl, dl2.reshape(s, LANES), lse2.reshape(s, LANES)


def _attn_grad_sum(cfg, g_1, g_2, col0, dproj, name):
    s, h = cfg.S, cfg.H
    c0 = col0 // LANES
    rows = DEINT_ROWS

    def body(g1_ref, g2_ref, dp_in, o_ref, scr):
        del dp_in
        _int_rows(g2_ref, scr)
        o_ref[...] = (g1_ref[...].astype(F32) + scr[...]).astype(BF16)

    return pl.pallas_call(
        body, out_shape=SDS(dproj.shape, BF16), grid=(s // rows, h),
        in_specs=[pl.BlockSpec((rows, LANES), lambda b, j: (b, j)), _deint_spec(lambda j: j), HBM_SPEC],
        out_specs=pl.BlockSpec((rows, LANES), lambda b, j: (b, c0 + j)),
        scratch_shapes=[pltpu.VMEM((rows, LANES), F32)],
        input_output_aliases={2: 0},
        compiler_params=_params(("parallel", "parallel")), name=name)(g_1, _by_residue(g_2), dproj)


CONV_HALO = 16
CONV_TR = 512
CONV_CW = 512


def _conv_fwd(cfg, proj, conv_w, conv_b):
    s, cd = cfg.S, cfg.CD
    tr, cw, hl = CONV_TR, CONV_CW, CONV_HALO
    cb0 = cfg.OXBC // cw

    def body(x_ref, h_ref, w_ref, b_ref, o_ref, scr):
        i = pl.program_id(0)
        scr[pl.ds(0, hl), :] = jnp.where(i > 0, h_ref[...].astype(F32), 0.0)
        scr[pl.ds(hl, tr), :] = x_ref[...].astype(F32)
        pre = b_ref[...] + jnp.zeros((tr, cw), F32)
        for k in range(CONV_K):
            pre = pre + w_ref[k:k + 1, :] * scr[pl.ds(hl - (CONV_K - 1) + k, tr), :]
        o_ref[...] = (pre * _sigmoid(pre)).astype(BF16)

    return pl.pallas_call(
        body, out_shape=SDS((s, cd), BF16), grid=(s // tr, cd // cw),
        in_specs=[pl.BlockSpec((tr, cw), lambda i, j: (i, cb0 + j)),
                  pl.BlockSpec((hl, cw), lambda i, j: (jnp.maximum(i * (tr // hl) - 1, 0), cb0 + j)),
                  pl.BlockSpec((CONV_K, cw), lambda i, j: (0, j)),
                  pl.BlockSpec((1, cw), lambda i, j: (0, j))],
        out_specs=pl.BlockSpec((tr, cw), lambda i, j: (i, j)),
        scratch_shapes=[pltpu.VMEM((tr + hl, cw), F32)],
        compiler_params=_params(("parallel", "parallel")), name="conv_fwd")(proj, proj, conv_w, conv_b)


def _conv_bwd(cfg, proj, dact, conv_w, conv_b, dproj):
    s, cd = cfg.S, cfg.CD
    tr, cw, hl = CONV_TR, CONV_CW, CONV_HALO
    cb0 = cfg.OXBC // cw
    nr = s // tr
    last_h = s // hl - 1

    def body(x_ref, hp_ref, hn_ref, d_ref, dn_ref, w_ref, b_ref, dp_in, dx_ref, gw_ref, gb_ref, xs, ds):
        del dp_in
        i = pl.program_id(1)
        xs[pl.ds(0, hl), :] = jnp.where(i > 0, hp_ref[...].astype(F32), 0.0)
        xs[pl.ds(hl, tr), :] = x_ref[...].astype(F32)
        xs[pl.ds(hl + tr, hl), :] = hn_ref[...].astype(F32)
        pre = b_ref[...] + jnp.zeros((tr + hl, cw), F32)
        for k in range(CONV_K):
            pre = pre + w_ref[k:k + 1, :] * xs[pl.ds(hl - (CONV_K - 1) + k, tr + hl), :]
        sg = _sigmoid(pre)
        dsilu = sg * (1.0 + pre * (1.0 - sg))
        ds[pl.ds(0, tr), :] = d_ref[...].astype(F32) * dsilu[0:tr]
        ds[pl.ds(tr, hl), :] = jnp.where(i < nr - 1, dn_ref[...].astype(F32), 0.0) * dsilu[tr:tr + hl]
        dx = jnp.zeros((tr, cw), F32)
        for k in range(CONV_K):
            dx = dx + w_ref[k:k + 1, :] * ds[pl.ds(CONV_K - 1 - k, tr), :]
        dx_ref[...] = dx.astype(BF16)

        @pl.when(i == 0)
        def _():
            gw_ref[...] = jnp.zeros_like(gw_ref)
            gb_ref[...] = jnp.zeros_like(gb_ref)

        dcur = ds[pl.ds(0, tr), :]
        gb_ref[...] += jnp.sum(dcur, axis=0, keepdims=True)
        for k in range(CONV_K):
            gw_ref[k:k + 1, :] += jnp.sum(dcur * xs[pl.ds(hl - (CONV_K - 1) + k, tr), :], axis=0, keepdims=True)

    return pl.pallas_call(
        body, out_shape=(SDS(dproj.shape, BF16), SDS((CONV_K, cd), F32), SDS((1, cd), F32)), grid=(cd // cw, nr),
        in_specs=[pl.BlockSpec((tr, cw), lambda j, i: (i, cb0 + j)),
                  pl.BlockSpec((hl, cw), lambda j, i: (jnp.maximum(i * (tr // hl) - 1, 0), cb0 + j)),
                  pl.BlockSpec((hl, cw), lambda j, i: (jnp.minimum((i + 1) * (tr // hl), last_h), cb0 + j)),
                  pl.BlockSpec((tr, cw), lambda j, i: (i, j)),
                  pl.BlockSpec((hl, cw), lambda j, i: (jnp.minimum((i + 1) * (tr // hl), last_h), j)),
                  pl.BlockSpec((CONV_K, cw), lambda j, i: (0, j)),
                  pl.BlockSpec((1, cw), lambda j, i: (0, j)),
                  pl.BlockSpec(memory_space=pl.ANY)],
        out_specs=(pl.BlockSpec((tr, cw), lambda j, i: (i, cb0 + j)),
                   pl.BlockSpec((CONV_K, cw), lambda j, i: (0, j)),
                   pl.BlockSpec((1, cw), lambda j, i: (0, j))),
        scratch_shapes=[pltpu.VMEM((tr + 2 * hl, cw), F32), pltpu.VMEM((tr + hl, cw), F32)],
        input_output_aliases={7: 0},
        compiler_params=_params(("parallel", "arbitrary")), name="conv_bwd")(
            proj, proj, proj, dact, dact, conv_w, conv_b, dproj)


def _expand(v, e, terms):
    out, rem = None, v
    for _ in range(terms):
        hi = rem.astype(BF16)
        t = _nn(hi, e)
        out = t if out is None else out + t
        rem = rem - hi.astype(F32)
    return out


def _segsum(v, e, terms):
    out, rem = None, v
    for _ in range(terms):
        hi = rem.astype(BF16)
        t = _nt(hi, e)
        out = t if out is None else out + t
        rem = rem - hi.astype(F32)
    return out


def _expand_row(row, e, terms):
    return _expand(jnp.broadcast_to(row, (8, LANES)), e, terms)[0:1]


def _segsum_row(row, e, terms):
    return _segsum(jnp.broadcast_to(row, (8, row.shape[1])), e, terms)[0:1]


def _expansion_matrix(cfg):
    hh = jnp.arange(LANES, dtype=jnp.int32)[:, None]
    cc = jnp.arange(cfg.SI, dtype=jnp.int32)[None, :]
    return (cc // SSM_HEAD_DIM == hh).astype(BF16)


def _tri(lower):
    r = lax.broadcasted_iota(jnp.int32, (CHUNK, CHUNK), 0)
    c = lax.broadcasted_iota(jnp.int32, (CHUNK, CHUNK), 1)
    return (c <= r) if lower else (c >= r)


def _ssd_prep(dtr_ref, db_ref, al_ref, e):
    dtr = dtr_ref[...] + db_ref[...]
    dt = _softplus(dtr)
    a = -jnp.exp(al_ref[...])
    acum = jnp.dot(_tri(True).astype(F32), dt * a, precision=lax.Precision.HIGHEST, preferred_element_type=F32)
    return dtr, dt, a, _expand(dt, e, 2), _expand(acum, e, 3)


def _ssd_fwd(cfg, xact, dt_raw, proj, dt_bias, a_log, d_skip, norm_w, e):
    s, si, cd, gw, bc = cfg.S, cfg.SI, cfg.CD, cfg.GW, cfg.BC
    nc = s // CHUNK
    zb = cfg.OZS // si
    tiles = gw // LANES

    def body(xa_ref, dtr_ref, z_ref, db_ref, al_ref, dsk_ref, nw_ref, e_ref, y_ref, y2_ref, st_ref,
             state, ybuf, x_s, xw_s, ae_s, ea_s, lam_s):
        @pl.when(pl.program_id(0) == 0)
        def _():
            state[...] = jnp.zeros_like(state)

        st_ref[...] = state[...]
        ev = e_ref[...]
        _, _, _, dt_e, a_e = _ssd_prep(dtr_ref, db_ref, al_ref, ev)
        xs = xa_ref[:, 0:si].astype(F32)
        x = xs * dt_e
        lam_e = a_e[CHUNK - 1:CHUNK, :]
        x_s[...] = x.astype(BF16)
        xw_s[...] = (x * jnp.exp(lam_e - a_e)).astype(BF16)
        ae_s[...] = a_e
        ea_s[...] = jnp.exp(a_e)
        ybuf[...] = _expand_row(dsk_ref[...], ev, 3) * xs
        lam_s[...] = jnp.broadcast_to(jnp.exp(lam_e), (8, si))
        tril = _tri(True)
        lane = lax.broadcasted_iota(jnp.int32, (CHUNK, LANES), 1)

        def group(g, carry):
            co = pl.multiple_of(g * gw, LANES)
            bg = xa_ref[:, pl.ds(pl.multiple_of(si + g * SSM_STATE, LANES), SSM_STATE)]
            cg = xa_ref[:, pl.ds(pl.multiple_of(si + bc + g * SSM_STATE, LANES), SSM_STATE)]
            cbm = _nt(cg, bg)
            st = state[:, pl.ds(co, gw)]
            yoff = _nn(cg, st.astype(BF16)) * ea_s[:, pl.ds(co, gw)]
            for k in range(tiles):
                tc = pl.multiple_of(co + k * LANES, LANES)
                at = ae_s[:, pl.ds(tc, LANES)]
                att = at.T
                xt = x_s[:, pl.ds(tc, LANES)]
                acc = yoff[:, k * LANES:(k + 1) * LANES]
                for half in range(2):
                    lo = half * SSM_HEAD_DIM
                    seg = at[:, lo:lo + 1] - att[lo:lo + 1, :]
                    dec = jnp.exp(jnp.where(tril, seg, NEG))
                    xh = jnp.where((lane >= lo) & (lane < lo + SSM_HEAD_DIM), xt, jnp.zeros_like(xt))
                    acc = acc + _nn((cbm * dec).astype(BF16), xh)
                ybuf[:, pl.ds(tc, LANES)] += acc
            state[:, pl.ds(co, gw)] = st * lam_s[0:1, pl.ds(co, gw)] + _tn(bg, xw_s[:, pl.ds(co, gw)])
            return carry

        lax.fori_loop(0, SSM_GROUPS, group, 0)
        y = ybuf[...]
        y_ref[...] = y.astype(BF16)
        z = z_ref[...].astype(F32)
        u = y * (z * _sigmoid(z))
        r = lax.rsqrt(jnp.mean(u * u, axis=-1, keepdims=True) + RMS_EPS)
        y2_ref[...] = (u * r * nw_ref[...]).astype(BF16)

    row = lambda n: pl.BlockSpec((1, n), lambda c: (0, 0))
    return pl.pallas_call(
        body,
        out_shape=(SDS((s, si), BF16), SDS((s, si), BF16), SDS((nc, SSM_STATE, si), F32)),
        grid=(nc,),
        in_specs=[pl.BlockSpec((CHUNK, cd), lambda c: (c, 0)),
                  pl.BlockSpec((CHUNK, LANES), lambda c: (c, 0)),
                  pl.BlockSpec((CHUNK, si), lambda c: (c, zb)),
                  row(LANES), row(LANES), row(LANES), row(si),
                  pl.BlockSpec((LANES, si), lambda c: (0, 0))],
        out_specs=(pl.BlockSpec((CHUNK, si), lambda c: (c, 0)),
                   pl.BlockSpec((CHUNK, si), lambda c: (c, 0)),
                   pl.BlockSpec((None, SSM_STATE, si), lambda c: (c, 0, 0))),
        scratch_shapes=[pltpu.VMEM((SSM_STATE, si), F32), pltpu.VMEM((CHUNK, si), F32),
                        pltpu.VMEM((CHUNK, si), BF16), pltpu.VMEM((CHUNK, si), BF16),
                        pltpu.VMEM((CHUNK, si), F32), pltpu.VMEM((CHUNK, si), F32),
                        pltpu.VMEM((8, si), F32)],
        compiler_params=_params(("arbitrary",)), name="ssd_fwd")(
            xact, dt_raw, proj, dt_bias, a_log, d_skip, norm_w, e)


def _ssd_bwd(cfg, xact, dt_raw, proj, y, dy2, states, dt_bias, a_log, d_skip, norm_w, e, dproj):
    s, si, cd, gw, bc, hpg = cfg.S, cfg.SI, cfg.CD, cfg.GW, cfg.BC, cfg.HPG
    nc = s // CHUNK
    zb = cfg.OZS // si
    tiles = gw // LANES

    def body(xa_ref, dtr_ref, z_ref, y_ref, d2_ref, st_ref, db_ref, al_ref, dsk_ref, nw_ref, e_ref, dp_in,
             dz_ref, dxa_ref, ddt_ref, gnw_ref, gdb_ref, gal_ref, gds_ref,
             dh, dhn, xs_s, x_s, w_s, ae_s, ea_s, g_s, dx_s, dae_s, r_s, lam_s, dle_s):
        del dp_in

        @pl.when(pl.program_id(0) == 0)
        def _():
            dh[...] = jnp.zeros_like(dh)
            gnw_ref[...] = jnp.zeros_like(gnw_ref)
            gdb_ref[...] = jnp.zeros_like(gdb_ref)
            gal_ref[...] = jnp.zeros_like(gal_ref)
            gds_ref[...] = jnp.zeros_like(gds_ref)

        ev = e_ref[...]
        yv = y_ref[...].astype(F32)
        z = z_ref[...].astype(F32)
        sg = _sigmoid(z)
        sz = z * sg
        u = yv * sz
        r = lax.rsqrt(jnp.mean(u * u, axis=-1, keepdims=True) + RMS_EPS)
        nrm = u * r
        d2 = d2_ref[...].astype(F32)
        gnw_ref[...] += jnp.sum(d2 * nrm, axis=0, keepdims=True)
        gn = d2 * nw_ref[...]
        du = r * (gn - nrm * jnp.mean(gn * nrm, axis=-1, keepdims=True))
        gv = du * sz
        dz_ref[...] = (du * yv * (sg * (1.0 + z * (1.0 - sg)))).astype(BF16)
        g_s[...] = gv

        dtr, dt, a, dt_e, a_e = _ssd_prep(dtr_ref, db_ref, al_ref, ev)
        xs = xa_ref[:, 0:si].astype(F32)
        x = xs * dt_e
        lam_e = a_e[CHUNK - 1:CHUNK, :]
        xs_s[...] = xs
        x_s[...] = x
        w_s[...] = jnp.exp(lam_e - a_e)
        ae_s[...] = a_e
        ea_s[...] = jnp.exp(a_e)
        lam_s[...] = jnp.broadcast_to(jnp.exp(lam_e), (8, si))
        gds_ref[...] += _segsum_row(jnp.sum(gv * xs, axis=0, keepdims=True), ev, 2)
        r_s[...] = jnp.zeros_like(r_s)
        tril = _tri(True)
        lane = lax.broadcasted_iota(jnp.int32, (CHUNK, LANES), 1)
        sub = lax.broadcasted_iota(jnp.int32, (CHUNK, LANES), 0)

        def group(g, carry):
            co = pl.multiple_of(g * gw, LANES)
            bo = pl.multiple_of(si + g * SSM_STATE, LANES)
            cof = pl.multiple_of(si + bc + g * SSM_STATE, LANES)
            cols = pl.ds(co, gw)
            bg = xa_ref[:, pl.ds(bo, SSM_STATE)]
            cg = xa_ref[:, pl.ds(cof, SSM_STATE)]
            cbm = _nt(cg, bg)
            st = st_ref[:, cols]
            stb = st.astype(BF16)
            dho = dh[:, cols]
            dhob = dho.astype(BF16)
            ea = ea_s[:, cols]
            gg = g_s[:, cols]
            xg = x_s[:, cols]
            wg = w_s[:, cols]
            explam = lam_s[0:1, cols]
            yoff = _nn(cg, stb) * ea
            ga = (gg * ea).astype(BF16)
            dc = _nt(ga, stb)
            dhn[:, cols] = dho * explam + _tn(cg, ga)
            bdh = _nn(bg, dhob)
            db = _nt((xg * wg).astype(BF16), dhob)
            t = xg * bdh * wg
            dle_s[0:1, cols] = jnp.sum(t, axis=0, keepdims=True) + explam * jnp.sum(dho * st, axis=0, keepdims=True)
            dae_base = gg * yoff - t
            dxw = wg * bdh
            dcb = jnp.zeros((CHUNK, CHUNK), F32)
            for k in range(tiles):
                tc = pl.multiple_of(co + k * LANES, LANES)
                ksl = slice(k * LANES, (k + 1) * LANES)
                at = ae_s[:, pl.ds(tc, LANES)]
                att = at.T
                xt = xg[:, ksl].astype(BF16)
                gt = gg[:, ksl].astype(BF16)
                dxt = dxw[:, ksl]
                place = jnp.zeros((CHUNK, LANES), F32)
                for half in range(2):
                    lo = half * SSM_HEAD_DIM
                    seg = at[:, lo:lo + 1] - att[lo:lo + 1, :]
                    dec = jnp.exp(jnp.where(tril, seg, NEG))
                    mh = cbm * dec
                    gh = jnp.where((lane >= lo) & (lane < lo + SSM_HEAD_DIM), gt, jnp.zeros_like(gt))
                    dm = _nt(gh, xt)
                    dxt = dxt + _tn(mh.astype(BF16), gh)
                    dcb = dcb + dm * dec
                    dseg = dm * mh
                    place = place + jnp.where(lane == lo, jnp.sum(dseg, axis=1, keepdims=True), 0.0)
                    hidx = g * hpg + 2 * k + half
                    r_s[...] += jnp.where(sub == hidx, jnp.sum(dseg, axis=0, keepdims=True), 0.0)
                dx_s[:, pl.ds(tc, LANES)] = dxt
                dae_s[:, pl.ds(tc, LANES)] = dae_base[:, ksl] + place
            dcbb = dcb.astype(BF16)
            dxa_ref[:, pl.ds(bo, SSM_STATE)] = (db + _tn(dcbb, cg)).astype(BF16)
            dxa_ref[:, pl.ds(cof, SSM_STATE)] = (dc + _nn(dcbb, bg)).astype(BF16)
            return carry

        lax.fori_loop(0, SSM_GROUPS, group, 0)
        dlam = _segsum_row(dle_s[0:1, :], ev, 2)
        da_ = _segsum(dae_s[...], ev, 2) - r_s[...].T
        da_ = da_ + jnp.where(sub == CHUNK - 1, dlam, 0.0)
        dda = jnp.dot(_tri(False).astype(F32), da_, precision=lax.Precision.HIGHEST, preferred_element_type=F32)
        dxv = dx_s[...]
        xs = xs_s[...]
        ddt = dda * a + _segsum(dxv * xs, ev, 2)
        gal_ref[...] += jnp.sum(dda * dt, axis=0, keepdims=True) * a
        ddtr = ddt * _sigmoid(dtr)
        gdb_ref[...] += jnp.sum(ddtr, axis=0, keepdims=True)
        ddt_ref[...] = ddtr
        dxa_ref[:, 0:si] = (dxv * dt_e + g_s[...] * _expand_row(dsk_ref[...], ev, 3)).astype(BF16)
        dh[...] = dhn[...]

    rev = lambda c: nc - 1 - c
    row = lambda n: pl.BlockSpec((1, n), lambda c: (0, 0))
    big = lambda: pltpu.VMEM((CHUNK, si), F32)
    return pl.pallas_call(
        body,
        out_shape=(SDS(dproj.shape, BF16), SDS((s, cd), BF16), SDS((s, LANES), F32),
                   SDS((1, si), F32), SDS((1, LANES), F32), SDS((1, LANES), F32), SDS((1, LANES), F32)),
        grid=(nc,),
        in_specs=[pl.BlockSpec((CHUNK, cd), lambda c: (rev(c), 0)),
                  pl.BlockSpec((CHUNK, LANES), lambda c: (rev(c), 0)),
                  pl.BlockSpec((CHUNK, si), lambda c: (rev(c), zb)),
                  pl.BlockSpec((CHUNK, si), lambda c: (rev(c), 0)),
                  pl.BlockSpec((CHUNK, si), lambda c: (rev(c), 0)),
                  pl.BlockSpec((None, SSM_STATE, si), lambda c: (rev(c), 0, 0)),
                  row(LANES), row(LANES), row(LANES), row(si),
                  pl.BlockSpec((LANES, si), lambda c: (0, 0)),
                  pl.BlockSpec(memory_space=pl.ANY)],
        out_specs=(pl.BlockSpec((CHUNK, si), lambda c: (rev(c), zb)),
                   pl.BlockSpec((CHUNK, cd), lambda c: (rev(c), 0)),
                   pl.BlockSpec((CHUNK, LANES), lambda c: (rev(c), 0)),
                   row(si), row(LANES), row(LANES), row(LANES)),
        scratch_shapes=[pltpu.VMEM((SSM_STATE, si), F32), pltpu.VMEM((SSM_STATE, si), F32),
                        big(), big(), big(), big(), big(), big(), big(), big(),
                        pltpu.VMEM((CHUNK, LANES), F32), pltpu.VMEM((8, si), F32), pltpu.VMEM((8, si), F32)],
        input_output_aliases={11: 0},
        compiler_params=_params(("arbitrary",)), name="ssd_bwd")(
            xact, dt_raw, proj, y, dy2, states, dt_bias, a_log, d_skip, norm_w, e, dproj)


MERGE_TR = 512
MERGE_CW = 512


def _merge_fwd(cfg, proj, a_br, s_br):
    s, d = cfg.S, cfg.D
    tr, cw = MERGE_TR, MERGE_CW
    ga0, gs0 = cfg.OGA // cw, cfg.OGS // cw

    def body(ga_ref, gs_ref, a_ref, s_ref, o_ref):
        o_ref[...] = (_sigmoid(ga_ref[...].astype(F32)) * a_ref[...].astype(F32)
                      + _sigmoid(gs_ref[...].astype(F32)) * s_ref[...].astype(F32)).astype(BF16)

    blk = pl.BlockSpec((tr, cw), lambda i, j: (i, j))
    return pl.pallas_call(
        body, out_shape=SDS((s, d), BF16), grid=(s // tr, d // cw),
        in_specs=[pl.BlockSpec((tr, cw), lambda i, j: (i, ga0 + j)),
                  pl.BlockSpec((tr, cw), lambda i, j: (i, gs0 + j)), blk, blk],
        out_specs=blk, compiler_params=_params(("parallel", "parallel")), name="merge_fwd")(proj, proj, a_br, s_br)


def _merge_bwd(cfg, proj, branch, dmerged, gate_off, dproj, name):
    s, d = cfg.S, cfg.D
    tr, cw = MERGE_TR, MERGE_CW
    g0 = gate_off // cw
    fresh = dproj is None

    def body(*refs):
        g_ref, b_ref, dm_ref = refs[:3]
        dg_ref, db_ref = refs[-2:]
        dm = dm_ref[...].astype(F32)
        sg = _sigmoid(g_ref[...].astype(F32))
        db_ref[...] = (dm * sg).astype(BF16)
        dg_ref[...] = (dm * b_ref[...].astype(F32) * sg * (1.0 - sg)).astype(BF16)

    blk = pl.BlockSpec((tr, cw), lambda i, j: (i, j))
    gate = pl.BlockSpec((tr, cw), lambda i, j: (i, g0 + j))
    return pl.pallas_call(
        body, out_shape=(SDS((s, cfg.NM), BF16), SDS((s, d), BF16)), grid=(s // tr, d // cw),
        in_specs=[gate, blk, blk] + ([] if fresh else [HBM_SPEC]),
        out_specs=(gate, blk),
        input_output_aliases={} if fresh else {3: 0},
        compiler_params=_params(("parallel", "parallel")), name=name)(
            *((proj, branch, dmerged) + (() if fresh else (dproj,))))


def _outproj_loss(merged, w_out, x, target, fnw):
    s, d = x.shape
    tr = 256

    def body(m_ref, w_ref, x_ref, t_ref, fw_ref, dof_ref, dob_ref, loss_ref, g_ref):
        out = x_ref[...] + _nn(m_ref[...], w_ref[...])
        r = lax.rsqrt(jnp.mean(out * out, axis=-1, keepdims=True) + RMS_EPS)
        nrm = out * r
        fw = fw_ref[...]
        err = nrm * fw - t_ref[...]
        dy = err * (1.0 / d)
        gy = dy * fw
        dout = r * (gy - nrm * jnp.mean(gy * nrm, axis=-1, keepdims=True))
        dof_ref[...] = dout
        dob_ref[...] = dout.astype(BF16)

        @pl.when(pl.program_id(0) == 0)
        def _():
            loss_ref[...] = jnp.zeros_like(loss_ref)
            g_ref[...] = jnp.zeros_like(g_ref)

        loss_ref[...] += jnp.sum(jnp.sum(err * err, axis=1, keepdims=True), axis=0, keepdims=True) * (0.5 / d)
        g_ref[...] += jnp.sum(dy * nrm, axis=0, keepdims=True)

    blk = pl.BlockSpec((tr, d), lambda i: (i, 0))
    return pl.pallas_call(
        body, out_shape=(SDS((s, d), F32), SDS((s, d), BF16), SDS((1, LANES), F32), SDS((1, d), F32)), grid=(s // tr,),
        in_specs=[blk, pl.BlockSpec((d, d), lambda i: (0, 0)), blk, blk, pl.BlockSpec((1, d), lambda i: (0, 0))],
        out_specs=(blk, blk, pl.BlockSpec((1, LANES), lambda i: (0, 0)), pl.BlockSpec((1, d), lambda i: (0, 0))),
        compiler_params=_params(("arbitrary",)), name="outproj_loss")(merged, w_out, x, target, fnw)


ELEMWISE_BLOCK_BYTES = 1 << 20


def _row_block(rows, cols, itemsize=4):
    best = None
    for tr in range(16, rows + 1, 16):
        if rows % tr == 0 and tr * cols * itemsize <= ELEMWISE_BLOCK_BYTES:
            best = tr
    return best if best is not None else rows


def _adamw(w, g, m, v, name):
    rows, cols = w.shape
    tr = _row_block(rows, cols)

    def body(w_ref, g_ref, m_ref, v_ref, d_ref, nm_ref, nv_ref):
        gv = g_ref[...]
        nm = ADAM_B1 * m_ref[...] + (1.0 - ADAM_B1) * gv
        nv = ADAM_B2 * v_ref[...] + (1.0 - ADAM_B2) * jnp.square(gv)
        m_hat = nm / (1.0 - ADAM_B1 ** ADAM_STEP)
        v_hat = nv / (1.0 - ADAM_B2 ** ADAM_STEP)
        d_ref[...] = -ADAM_LR * (m_hat / (jnp.sqrt(v_hat) + ADAM_EPS) + ADAM_WD * w_ref[...])
        nm_ref[...] = nm
        nv_ref[...] = nv

    blk = pl.BlockSpec((tr, cols), lambda i: (i, 0))
    out = SDS((rows, cols), F32)
    return pl.pallas_call(
        body, out_shape=(out, out, out), grid=(rows // tr,), in_specs=[blk] * 4, out_specs=(blk,) * 3,
        compiler_params=_params(("parallel",)), name=name)(w, g, m, v)


HBM_SPEC = pl.BlockSpec(memory_space=pl.ANY)


def _position():
    return lax.axis_index("x"), lax.axis_index("y"), lax.axis_index("c")


def _gather_chips(shards):
    n = len(shards)

    def body(*refs):
        ins, outs = refs[:n], refs[n:2 * n]
        send_sems, recv_sems, fsend_sems, frecv_sems = refs[2 * n:]
        x, y, c = _position()
        me = 2 * x + y
        peers = [(1 - x, y), (x, 1 - y), (1 - x, 1 - y)]

        def over_ici(t, p, chip):
            px, py = peers[p]
            r2 = ins[t].shape[0] // 2
            return pltpu.make_async_remote_copy(
                src_ref=ins[t].at[pl.ds(c * r2, r2), :], dst_ref=outs[t].at[chip, c], send_sem=send_sems.at[3 * t + p],
                recv_sem=recv_sems.at[3 * t + p], device_id=(px, py, c), device_id_type=MESH)

        def to_sibling(t, p, half):
            px, py = peers[p]
            slab = outs[t].at[2 * px + py, half]
            return pltpu.make_async_remote_copy(
                src_ref=slab, dst_ref=slab, send_sem=fsend_sems.at[3 * t + p], recv_sem=frecv_sems.at[3 * t + p],
                device_id=(x, y, 1 - c), device_id_type=MESH)

        sends = [over_ici(t, p, me) for t in range(n) for p in range(3)]
        for cp in sends:
            cp.start()
        passed = []
        for t in range(n):
            for p, (px, py) in enumerate(peers):
                over_ici(t, p, 2 * px + py).wait_recv()
                passed.append(to_sibling(t, p, c))
                passed[-1].start()
        for t in range(n):
            for p in range(3):
                to_sibling(t, p, 1 - c).wait_recv()
        for cp in sends + passed:
            cp.wait_send()

    return pl.pallas_call(
        body, out_shape=[SDS((N_CHIPS, 2, a.shape[0] // 2, a.shape[1]), a.dtype) for a in shards],
        in_specs=[HBM_SPEC] * n, out_specs=[HBM_SPEC] * n,
        scratch_shapes=[pltpu.SemaphoreType.DMA((3 * n,))] * 4,
        compiler_params=pltpu.CompilerParams(has_side_effects=True), name="gather_weights")(*shards)


def _with_own(gathered, own, chip):
    full = gathered.reshape((N_CHIPS,) + own.shape)
    return lax.dynamic_update_index_in_dim(full, own, chip, 0)


def _exchange_halves(grads):
    n = len(grads)

    def body(*refs):
        ins, outs = refs[:n], refs[n:2 * n]
        send_sems, recv_sems = refs[2 * n:]
        x, y, c = _position()
        cps = []
        for t in range(n):
            r2 = ins[t].shape[1] // 2
            cps.append(pltpu.make_async_remote_copy(
                src_ref=ins[t].at[:, pl.ds((1 - c) * r2, r2), :], dst_ref=outs[t],
                send_sem=send_sems.at[t], recv_sem=recv_sems.at[t], device_id=(x, y, 1 - c), device_id_type=MESH))
        for cp in cps:
            cp.start()
        for cp in cps:
            cp.wait()

    return pl.pallas_call(
        body, out_shape=[SDS((a.shape[0], a.shape[1] // 2, a.shape[2]), a.dtype) for a in grads],
        in_specs=[HBM_SPEC] * n, out_specs=[HBM_SPEC] * n,
        scratch_shapes=[pltpu.SemaphoreType.DMA((n,)), pltpu.SemaphoreType.DMA((n,))],
        compiler_params=pltpu.CompilerParams(has_side_effects=True), name="reduce_sibling")(*grads)


def _scatter_chips(parts):
    n = len(parts)

    def body(*refs):
        ins, outs = refs[:n], refs[n:2 * n]
        send_sems, recv_sems = refs[2 * n:]
        x, y, c = _position()
        me = 2 * x + y
        peers = [(1 - x, y), (x, 1 - y), (1 - x, 1 - y)]

        def remote(t, p, src_slab, dst_slab):
            px, py = peers[p]
            return pltpu.make_async_remote_copy(
                src_ref=ins[t].at[src_slab], dst_ref=outs[t].at[dst_slab], send_sem=send_sems.at[3 * t + p],
                recv_sem=recv_sems.at[3 * t + p], device_id=(px, py, c), device_id_type=MESH)

        sends = [remote(t, p, 2 * peers[p][0] + peers[p][1], me) for t in range(n) for p in range(3)]
        for cp in sends:
            cp.start()
        for t in range(n):
            for p, (px, py) in enumerate(peers):
                remote(t, p, me, 2 * px + py).wait_recv()
        for cp in sends:
            cp.wait_send()

    return pl.pallas_call(
        body, out_shape=[SDS(a.shape, a.dtype) for a in parts],
        in_specs=[HBM_SPEC] * n, out_specs=[HBM_SPEC] * n,
        scratch_shapes=[pltpu.SemaphoreType.DMA((3 * n,)), pltpu.SemaphoreType.DMA((3 * n,))],
        compiler_params=pltpu.CompilerParams(has_side_effects=True), name="reduce_chips")(*parts)


def _share_halves(halves):
    n = len(halves)

    def body(*refs):
        ins, outs = refs[:n], refs[n:2 * n]
        send_sems, recv_sems = refs[2 * n:]
        x, y, c = _position()

        def copy(t, slab):
            return pltpu.make_async_remote_copy(
                src_ref=ins[t].at[slab], dst_ref=outs[t].at[slab], send_sem=send_sems.at[t], recv_sem=recv_sems.at[t],
                device_id=(x, y, 1 - c), device_id_type=MESH)

        for t in range(n):
            copy(t, c).start()
        for t in range(n):
            copy(t, 1 - c).wait_recv()
        for t in range(n):
            copy(t, c).wait_send()

    return pl.pallas_call(
        body, out_shape=[SDS(a.shape, a.dtype) for a in halves],
        in_specs=[HBM_SPEC] * n, out_specs=[HBM_SPEC] * n,
        scratch_shapes=[pltpu.SemaphoreType.DMA((n,)), pltpu.SemaphoreType.DMA((n,))],
        input_output_aliases={t: t for t in range(n)},
        compiler_params=pltpu.CompilerParams(has_side_effects=True), name="share_sibling")(*halves)


def _add_sibling(grad, recv, core):
    nch, r2, cols = recv.shape
    tr = _row_block(r2, cols)
    nb = r2 // tr

    def body(c_ref, g_ref, r_ref, o_ref):
        del c_ref
        o_ref[...] = (g_ref[...] + r_ref[...]).astype(BF16)

    return pl.pallas_call(
        body, out_shape=SDS(recv.shape, BF16),
        grid_spec=pltpu.PrefetchScalarGridSpec(
            num_scalar_prefetch=1, grid=(nch, nb),
            in_specs=[pl.BlockSpec((None, tr, cols), lambda j, i, c_ref: (j, c_ref[0] * nb + i, 0)),
                      pl.BlockSpec((None, tr, cols), lambda j, i, c_ref: (j, i, 0))],
            out_specs=pl.BlockSpec((None, tr, cols), lambda j, i, c_ref: (j, i, 0))),
        compiler_params=_params(("parallel", "parallel")), name="add_sibling")(core, grad, recv)


def _add_chips(own, recv, chip_core):
    nch, r2, cols = recv.shape
    tr = _row_block(r2, cols)

    def body(cc_ref, own_ref, *refs):
        p_refs, o_ref = refs[:nch], refs[nch]
        me = cc_ref[0]
        acc = None
        for j in range(nch):
            term = jnp.where(me == j, own_ref[...], p_refs[j][...]).astype(F32)
            acc = term if acc is None else acc + term
        o_ref[...] = acc

    def slab(j):
        return pl.BlockSpec((None, tr, cols), lambda i, cc: (cc[2 + j], i, 0))

    return pl.pallas_call(
        body, out_shape=SDS((2, r2, cols), F32),
        grid_spec=pltpu.PrefetchScalarGridSpec(
            num_scalar_prefetch=1, grid=(r2 // tr,),
            in_specs=[pl.BlockSpec((None, tr, cols), lambda i, cc: (cc[0], i, 0))] + [slab(j) for j in range(nch)],
            out_specs=pl.BlockSpec((None, tr, cols), lambda i, cc: (cc[1], i, 0))),
        compiler_params=_params(("parallel",)), name="add_chips")(chip_core, own, *([recv] * nch))


def _allreduce_small(pack):
    rows = pack.shape[0]

    def body(p_ref, o_ref, buf, send_sems, recv_sems):
        x, y, c = _position()
        me = 4 * x + 2 * y + c
        buf[me] = p_ref[...]

        def copy(dst_dev, slot):
            return pltpu.make_async_remote_copy(
                src_ref=p_ref, dst_ref=buf.at[slot], send_sem=send_sems.at[dst_dev], recv_sem=recv_sems.at[slot],
                device_id=(dst_dev // 4, (dst_dev // 2) % 2, dst_dev % 2), device_id_type=MESH)

        for dev in range(N_DEV):
            @pl.when(dev != me)
            def _():
                copy(dev, me).start()
        for dev in range(N_DEV):
            @pl.when(dev != me)
            def _():
                copy(dev, dev).wait_recv()
        for dev in range(N_DEV):
            @pl.when(dev != me)
            def _():
                copy(dev, me).wait_send()
        acc = buf[0]
        for dev in range(1, N_DEV):
            acc = acc + buf[dev]
        o_ref[...] = acc

    return pl.pallas_call(
        body, out_shape=SDS(pack.shape, F32),
        in_specs=[pl.BlockSpec(memory_space=pltpu.VMEM)], out_specs=pl.BlockSpec(memory_space=pltpu.VMEM),
        scratch_shapes=[pltpu.VMEM((N_DEV, rows, LANES), F32), pltpu.SemaphoreType.DMA((N_DEV,)),
                        pltpu.SemaphoreType.DMA((N_DEV,))],
        compiler_params=pltpu.CompilerParams(has_side_effects=True), name="allreduce_small")(pack)


ATTN_TQ = 256


def _local_step(cfg, x, target, w):
    d = cfg.D
    win = ATTN_WINDOW
    hn = _rmsnorm_fwd(x, w["norm_w"])
    proj = _mm(hn, w["w_main"], "nn", BF16, "proj_main")
    dt_raw = _mm(hn, w["w_dt"], "nn", F32, "proj_dt")
    slopes = _slopes(cfg.H)
    near = _Pass(ATTN_TQ, DILATED_PATTERNS[:-1], 1, cfg.S)
    far = _Pass(LANES, DILATED_PATTERNS[-1:], DEINT, cfg.S // DEINT)
    tab_near, tab_far = _attn_tables(near), _attn_tables(far)
    cols_near, cols_far = (cfg.OQ, cfg.OK, cfg.OV), (0, d, 2 * d)
    qkv_far = _deinterleave(proj, 0, 3 * d, "attn_deinterleave")
    o_1, lse_1 = _attn_fwd(cfg, near, proj, cols_near, tab_near, slopes, "attn_fwd_near")
    o_2, lse_2 = _attn_fwd(cfg, far, qkv_far, cols_far, tab_far, slopes, "attn_fwd_far")
    o_a, oag, lse = _attn_merge(cfg, proj, o_1, lse_1, o_2, lse_2)
    xact = _conv_fwd(cfg, proj, w["conv_w"], w["conv_b"])
    e = _expansion_matrix(cfg)
    y, y2, states = _ssd_fwd(cfg, xact, dt_raw, proj, w["dt_bias"], w["a_log"], w["d_skip"], w["ssm_norm_w"], e)
    a_br = _mm(oag, w["w_attn"], "nn", BF16, "branch_attn")
    s_br = _mm(y2, w["w_ssm"], "nn", BF16, "branch_ssm")
    merged = _merge_fwd(cfg, proj, a_br, s_br)
    dout_f, dout_b, loss_row, g_fnw = _outproj_loss(merged, w["w_out"], x, target, w["final_norm_w"])

    dmerged = _mm(dout_b, w["w_out"], "nt", BF16, "d_merged")
    g_w_out = _mm(merged, dout_b, "tn", F32, "g_w_out")
    dproj, da_br = _merge_bwd(cfg, proj, a_br, dmerged, cfg.OGA, None, "merge_bwd_attn")
    dproj, ds_br = _merge_bwd(cfg, proj, s_br, dmerged, cfg.OGS, dproj, "merge_bwd_ssm")
    doag = _mm(da_br, w["w_attn"], "nt", BF16, "d_oag")
    g_w_attn = _mm(oag, da_br, "tn", F32, "g_w_attn")
    dy2 = _mm(ds_br, w["w_ssm"], "nt", BF16, "d_y2")
    g_w_ssm = _mm(y2, ds_br, "tn", F32, "g_w_ssm")
    dproj, dxact, ddt, g_snw, g_dtb, g_alog, g_dsk = _ssd_bwd(
        cfg, xact, dt_raw, proj, y, dy2, states, w["dt_bias"], w["a_log"], w["d_skip"], w["ssm_norm_w"], e, dproj)
    dproj, g_cw, g_cb = _conv_bwd(cfg, proj, dxact, w["conv_w"], w["conv_b"], dproj)
    dproj, do, do_far, dl, dl_far, lse_far = _attn_bwd_prep(cfg, proj, o_a, doag, lse, dproj)
    g_near = _attn_bwd(cfg, near, proj, cols_near, do, lse, dl, tab_near, slopes, "attn_bwd_near")
    g_far = _attn_bwd(cfg, far, qkv_far, cols_far, do_far, lse_far, dl_far, tab_far, slopes, "attn_bwd_far")
    for g_1, g_2, col0, nm in zip(g_near, g_far, cols_near, ("attn_dq", "attn_dk", "attn_dv")):
        dproj = _attn_grad_sum(cfg, g_1, g_2, col0, dproj, nm)
    ddt_b = ddt.astype(BF16)
    dhn = _mm(dproj, w["w_main"], "nt", F32, "d_hn", init=_mm(ddt_b, w["w_dt"], "nt", F32, "d_hn_dt"))
    g_w_main = _mm(hn, dproj, "tn", F32, "g_w_main")
    g_w_dt = _mm(hn, ddt_b, "tn", F32, "g_w_dt")
    grad_x, g_nw = _rmsnorm_bwd(x, w["norm_w"], dhn, dout_f)
    grads = dict(norm_w=g_nw, w_main=g_w_main, w_dt=g_w_dt, conv_w=g_cw, conv_b=g_cb, dt_bias=g_dtb, a_log=g_alog,
                 d_skip=g_dsk, ssm_norm_w=g_snw, w_attn=g_w_attn, w_ssm=g_w_ssm, w_out=g_w_out, final_norm_w=g_fnw)
    return loss_row, grad_x, grads


def _pad_lanes(v):
    return jnp.pad(v, ((0, 0), (0, LANES - v.shape[1])))


def _full_weights(cfg, norm_w, w_in, conv_w, conv_b, dt_bias, a_log, d_skip, ssm_norm_w, w_attn, w_ssm, w_out, fnw):
    w_main = jnp.concatenate([w_in[:, :cfg.OGA], w_in[:, cfg.OGA + cfg.NH:]], axis=1).astype(BF16)
    w_dt = _pad_lanes(w_in[:, cfg.OGA:cfg.OGA + cfg.NH]).astype(BF16)
    return dict(norm_w=norm_w, w_main=w_main, w_dt=w_dt, conv_w=conv_w, conv_b=conv_b, dt_bias=_pad_lanes(dt_bias),
                a_log=_pad_lanes(a_log), d_skip=_pad_lanes(d_skip), ssm_norm_w=ssm_norm_w, w_attn=w_attn.astype(BF16),
                w_ssm=w_ssm.astype(BF16), w_out=w_out.astype(BF16), final_norm_w=fnw)


def _grad_w_in(cfg, grads):
    return jnp.concatenate([grads["w_main"][:, :cfg.OGA], grads["w_dt"][:, :cfg.NH], grads["w_main"][:, cfg.OGA:]], axis=1)


def kernel(x, norm_w, w_in, conv_w, conv_b, dt_bias, a_log, d_skip, ssm_norm_w, w_attn_branch, w_ssm_branch, w_out, final_norm_w, loss_target, m_norm_w, m_w_in, m_conv_w, m_conv_b, m_dt_bias, m_a_log, m_d_skip, m_ssm_norm_w, m_w_attn_branch, m_w_ssm_branch, m_w_out, m_final_norm_w, v_norm_w, v_w_in, v_conv_w, v_conv_b, v_dt_bias, v_a_log, v_d_skip, v_ssm_norm_w, v_w_attn_branch, v_w_ssm_branch, v_w_out, v_final_norm_w):
    cfg = _Cfg(x.shape[1], x.shape[2])
    d, si, cd, nh = cfg.D, cfg.SI, cfg.CD, cfg.NH
    chip = 2 * lax.axis_index("x") + lax.axis_index("y")
    core = lax.axis_index("c").astype(jnp.int32).reshape(1)
    slabs = jnp.arange(N_CHIPS, dtype=jnp.int32)
    chip_core = jnp.concatenate([chip.astype(jnp.int32).reshape(1), core,
                                 jnp.where(slabs == chip, (slabs + 1) % N_CHIPS, slabs)])

    own = [w_in[0].astype(BF16), w_attn_branch[0].astype(BF16), w_ssm_branch[0].astype(BF16), w_out[0].astype(BF16),
           conv_w[0].reshape(4 * CONV_K, -1)]
    a_in, a_attn, a_ssm, a_out, a_cw = [_with_own(g, o, chip) for g, o in zip(_gather_chips(own), own)]
    w_in_full = a_in.transpose(1, 0, 2).reshape(d, cfg.N_IN)
    conv_w_full = a_cw.reshape(N_CHIPS, CONV_K, cd // N_CHIPS).transpose(1, 0, 2).reshape(CONV_K, cd)
    w = _full_weights(cfg, norm_w, w_in_full, conv_w_full, conv_b, dt_bias, a_log, d_skip, ssm_norm_w,
                      a_attn.reshape(d, d), a_ssm.reshape(si, d), a_out.reshape(d, d), final_norm_w.reshape(1, d))

    loss_row, grad_x, grads = _local_step(cfg, x[0], loss_target[0], w)

    by_chip = [_grad_w_in(cfg, grads).reshape(d, N_CHIPS, cfg.N_IN // N_CHIPS).transpose(1, 0, 2),
               grads["w_attn"].reshape(N_CHIPS, d // N_CHIPS, d),
               grads["w_ssm"].reshape(N_CHIPS, si // N_CHIPS, d),
               grads["w_out"].reshape(N_CHIPS, d // N_CHIPS, d)]
    from_sibling = _exchange_halves(by_chip)
    chip_sums = [_add_sibling(g, r, core) for g, r in zip(by_chip, from_sibling)]
    from_chips = _scatter_chips(chip_sums)
    halves = [_add_chips(o, p, chip_core) for o, p in zip(chip_sums, from_chips)]
    g_in, g_attn, g_ssm, g_out = [h.reshape(2 * h.shape[1], h.shape[2]) for h in _share_halves(halves)]

    small = [loss_row, grads["norm_w"], grads["conv_b"], grads["dt_bias"], grads["a_log"], grads["d_skip"],
             grads["ssm_norm_w"], grads["final_norm_w"], grads["conv_w"].reshape(1, CONV_K * cd)]
    sizes = [a.shape[1] for a in small]
    total = sum(sizes)
    rows = -(-total // (8 * LANES)) * 8
    flat = jnp.pad(jnp.concatenate(small, axis=1), ((0, 0), (0, rows * LANES - total)))
    red = _allreduce_small(flat.reshape(rows, LANES)).reshape(1, rows * LANES)
    offs = [sum(sizes[:i]) for i in range(len(sizes))]
    loss_r, g_nw, g_cb, g_dtb, g_alog, g_dsk, g_snw, g_fnw, g_cw_flat = [
        red[:, o:o + n] for o, n in zip(offs, sizes)]
    loss = loss_r[0, 0]
    g_dtb, g_alog, g_dsk = g_dtb[:, :nh], g_alog[:, :nh], g_dsk[:, :nh]
    cshard = cd // N_CHIPS
    g_cw = lax.dynamic_slice_in_dim(g_cw_flat.reshape(CONV_K, cd), chip * cshard, cshard, axis=1)

    upd = {}
    for name, wv, gv, mv, vv in [("w_in", w_in[0], g_in, m_w_in[0], v_w_in[0]),
                                 ("w_attn", w_attn_branch[0], g_attn, m_w_attn_branch[0], v_w_attn_branch[0]),
                                 ("w_ssm", w_ssm_branch[0], g_ssm, m_w_ssm_branch[0], v_w_ssm_branch[0]),
                                 ("w_out", w_out[0], g_out, m_w_out[0], v_w_out[0])]:
        upd[name] = _adamw(wv, gv, mv, vv, "adamw_" + name)
    names = ["norm_w", "conv_w", "conv_b", "dt_bias", "a_log", "d_skip", "ssm_norm_w", "final_norm_w"]
    ws = [norm_w, conv_w[0].reshape(1, -1), conv_b, dt_bias, a_log, d_skip, ssm_norm_w, final_norm_w.reshape(1, d)]
    gs = [g_nw, g_cw.reshape(1, -1), g_cb, g_dtb, g_alog, g_dsk, g_snw, g_fnw]
    ms = [m_norm_w, m_conv_w[0].reshape(1, -1), m_conv_b, m_dt_bias, m_a_log, m_d_skip, m_ssm_norm_w,
          m_final_norm_w.reshape(1, d)]
    vs = [v_norm_w, v_conv_w[0].reshape(1, -1), v_conv_b, v_dt_bias, v_a_log, v_d_skip, v_ssm_norm_w,
          v_final_norm_w.reshape(1, d)]
    ssz = [a.shape[1] for a in ws]
    stot = sum(ssz)
    srows = -(-stot // (8 * LANES)) * 8

    def pack(parts):
        return jnp.pad(jnp.concatenate(parts, axis=1), ((0, 0), (0, srows * LANES - stot))).reshape(srows, LANES)

    packed = _adamw(pack(ws), pack(gs), pack(ms), pack(vs), "adamw_small")
    soffs = [sum(ssz[:i]) for i in range(len(ssz))]
    for k, nm in enumerate(names):
        upd[nm] = tuple(p.reshape(1, srows * LANES)[:, soffs[k]:soffs[k] + ssz[k]] for p in packed)

    shapes = dict(norm_w=norm_w.shape, w_in=w_in.shape, conv_w=conv_w.shape, conv_b=conv_b.shape, dt_bias=dt_bias.shape,
                  a_log=a_log.shape, d_skip=d_skip.shape, ssm_norm_w=ssm_norm_w.shape, w_attn=w_attn_branch.shape,
                  w_ssm=w_ssm_branch.shape, w_out=w_out.shape, final_norm_w=final_norm_w.shape)
    order = ["norm_w", "w_in", "conv_w", "conv_b", "dt_bias", "a_log", "d_skip", "ssm_norm_w", "w_attn", "w_ssm",
             "w_out", "final_norm_w"]
    gradv = dict(norm_w=g_nw, w_in=g_in, conv_w=g_cw, conv_b=g_cb, dt_bias=g_dtb, a_log=g_alog, d_skip=g_dsk,
                 ssm_norm_w=g_snw, w_attn=g_attn, w_ssm=g_ssm, w_out=g_out, final_norm_w=g_fnw)
    outs = [loss, grad_x[None]]
    outs += [gradv[n].reshape(shapes[n]) for n in order]
    for k in range(3):
        outs += [upd[n][k].reshape(shapes[n]) for n in order]
    return tuple(outs)
```

```python
import functools
import math

import jax
import jax.numpy as jnp
from jax import lax
from jax.experimental import pallas as pl
from jax.experimental.pallas import tpu as pltpu

F32 = jnp.float32
BF16 = jnp.bfloat16
SDS = jax.ShapeDtypeStruct

RMS_EPS = 1e-6
LANES = 128
CHUNK = 128
SSM_HEAD_DIM = 64
SSM_GROUPS = 8
SSM_STATE = 128
CONV_K = 4
ATTN_HEAD_DIM = 128
DILATED_PATTERNS = ((128, 1), (512, 4), (2048, 16))
ATTN_WINDOW = max(w for w, _ in DILATED_PATTERNS)
NEG = -1e30
VMEM_LIMIT = 56 * 1024 * 1024
ADAM_LR, ADAM_B1, ADAM_B2, ADAM_EPS, ADAM_WD, ADAM_STEP = 0.001, 0.9, 0.999, 1e-08, 0.01, 10
MESH = pl.DeviceIdType.MESH
N_CHIPS = 4
N_DEV = 8


class _Cfg:
    def __init__(self, s, d):
        self.S, self.D = s, d
        self.H = d // ATTN_HEAD_DIM
        self.SI = 2 * d
        self.NH = self.SI // SSM_HEAD_DIM
        self.HPG = self.NH // SSM_GROUPS
        self.GW = self.HPG * SSM_HEAD_DIM
        self.BC = SSM_GROUPS * SSM_STATE
        self.CD = self.SI + 2 * self.BC
        self.OQ, self.OK, self.OV, self.OZA = 0, d, 2 * d, 3 * d
        self.OZS = 4 * d
        self.OXBC = self.OZS + self.SI
        self.OGA = self.OXBC + self.CD
        self.OGS = self.OGA + d
        self.NM = self.OGS + d
        self.N_IN = self.NM + self.NH
        assert self.GW % LANES == 0 and self.NH <= LANES and s % 512 == 0 and d % 512 == 0


def _params(sem=None):
    return pltpu.CompilerParams(dimension_semantics=sem, vmem_limit_bytes=VMEM_LIMIT)


def _sigmoid(x):
    return 1.0 / (1.0 + jnp.exp(-x))


def _softplus(x):
    u = jnp.exp(-jnp.abs(x))
    l1p = jnp.where(u < 1e-3, u * (1.0 - u * (0.5 - u * (1.0 / 3.0))), jnp.log(1.0 + u))
    return jnp.maximum(x, 0.0) + l1p


def _nt(a, b):
    return lax.dot_general(a, b, (((1,), (1,)), ((), ())), preferred_element_type=F32)


def _tn(a, b):
    return lax.dot_general(a, b, (((0,), (0,)), ((), ())), preferred_element_type=F32)


def _nn(a, b):
    return jnp.dot(a, b, preferred_element_type=F32)


def _tile(n, target):
    if n <= target:
        return n
    best = None
    for t in range(LANES, target + 1, LANES):
        if n % t == 0:
            best = t
    assert best is not None, (n, target)
    return best


def _mm(a, b, dims, out_dtype, name, tm=1024, tn=2048, tk=512, init=None, exchange=()):
    if dims == "nn":
        (m, k), (k2, n) = a.shape, b.shape
    elif dims == "nt":
        (m, k), (n, k2) = a.shape, b.shape
    else:
        (k, m), (k2, n) = a.shape, b.shape
    assert k == k2
    tm, tn, tk = _tile(m, tm), _tile(n, tn), _tile(k, tk)
    nk = k // tk
    if dims == "tn":
        a_spec = pl.BlockSpec((tk, tm), lambda i, j, kk: (kk, i))
    else:
        a_spec = pl.BlockSpec((tm, tk), lambda i, j, kk: (i, kk))
    if dims == "nt":
        b_spec = pl.BlockSpec((tn, tk), lambda i, j, kk: (j, kk))
    else:
        b_spec = pl.BlockSpec((tk, tn), lambda i, j, kk: (kk, j))
    o_spec = pl.BlockSpec((tm, tn), lambda i, j, kk: (i, j))
    op = {"nn": _nn, "nt": _nt, "tn": _tn}[dims]
    has_init = init is not None
    nx = len(exchange)
    ni, nj = m // tm, n // tn

    def body(*refs):
        a_ref, b_ref = refs[0], refs[1]
        i_ref = refs[2] if has_init else None
        x_in = refs[2 + has_init:2 + has_init + nx]
        o_ref = refs[2 + has_init + nx]
        x_out = refs[3 + has_init + nx:3 + has_init + 2 * nx]
        acc = refs[3 + has_init + 2 * nx]
        i, j, kk = pl.program_id(0), pl.program_id(1), pl.program_id(2)

        if nx:
            sends, lands = _scatter_copies(x_in, x_out, *refs[4 + has_init + 2 * nx:])

            @pl.when((i == 0) & (j == 0) & (kk == 0))
            def _():
                for cp in sends:
                    cp.start()

        @pl.when(kk == 0)
        def _():
            acc[...] = i_ref[...].astype(F32) if has_init else jnp.zeros_like(acc)

        acc[...] += op(a_ref[...], b_ref[...])

        @pl.when(kk == nk - 1)
        def _():
            o_ref[...] = acc[...].astype(out_dtype)

        if nx:
            @pl.when((i == ni - 1) & (j == nj - 1) & (kk == nk - 1))
            def _():
                for cp in lands:
                    cp.wait_recv()
                for cp in sends:
                    cp.wait_send()

    in_specs = [a_spec, b_spec] + ([o_spec] if has_init else []) + [HBM_SPEC] * nx
    args = (a, b) + ((init,) if has_init else ()) + tuple(exchange)
    sems = [pltpu.SemaphoreType.DMA((3 * nx,))] * 2 if nx else []
    outs = pl.pallas_call(
        body, out_shape=[SDS((m, n), out_dtype)] + [SDS(e.shape, e.dtype) for e in exchange], grid=(ni, nj, nk),
        in_specs=in_specs, out_specs=[o_spec] + [HBM_SPEC] * nx,
        scratch_shapes=[pltpu.VMEM((tm, tn), F32)] + sems,
        compiler_params=_params(("arbitrary",) * 3 if nx else ("parallel", "parallel", "arbitrary")), name=name)(*args)
    return (outs[0], outs[1:]) if nx else outs[0]


def _rmsnorm_fwd(x, w):
    s, d = x.shape
    tr = 256

    def body(x_ref, w_ref, o_ref):
        xv = x_ref[...]
        r = lax.rsqrt(jnp.mean(xv * xv, axis=-1, keepdims=True) + RMS_EPS)
        o_ref[...] = (xv * r * w_ref[...]).astype(BF16)

    return pl.pallas_call(
        body, out_shape=SDS((s, d), BF16), grid=(s // tr,),
        in_specs=[pl.BlockSpec((tr, d), lambda i: (i, 0)), pl.BlockSpec((1, d), lambda i: (0, 0))],
        out_specs=pl.BlockSpec((tr, d), lambda i: (i, 0)),
        compiler_params=_params(("parallel",)), name="rmsnorm_fwd")(x, w)


def _rmsnorm_bwd(x, w, dhn, dout):
    s, d = x.shape
    tr = 256

    def body(x_ref, w_ref, dh_ref, do_ref, gx_ref, gw_ref):
        xv = x_ref[...]
        r = lax.rsqrt(jnp.mean(xv * xv, axis=-1, keepdims=True) + RMS_EPS)
        nrm = xv * r
        dh = dh_ref[...]
        gy = dh * w_ref[...]
        gx_ref[...] = do_ref[...] + r * (gy - nrm * jnp.mean(gy * nrm, axis=-1, keepdims=True))

        @pl.when(pl.program_id(0) == 0)
        def _():
            gw_ref[...] = jnp.zeros_like(gw_ref)

        gw_ref[...] += jnp.sum(dh * nrm, axis=0, keepdims=True)

    blk = pl.BlockSpec((tr, d), lambda i: (i, 0))
    row = pl.BlockSpec((1, d), lambda i: (0, 0))
    return pl.pallas_call(
        body, out_shape=(SDS((s, d), F32), SDS((1, d), F32)), grid=(s // tr,),
        in_specs=[blk, row, blk, blk], out_specs=(blk, row),
        compiler_params=_params(("arbitrary",)), name="rmsnorm_bwd")(x, w, dhn, dout)


DEINT = DILATED_PATTERNS[-1][1]
DEINT_ROWS = DEINT * LANES


class _Pass:
    def __init__(self, tq, patterns, unit, seg_len):
        self.tq, self.patterns, self.unit, self.seg_len = tq, patterns, unit, seg_len
        self.win = max(w for w, _ in patterns) // unit
        self.w = self.win + tq
        assert self.win % tq == 0


def _attn_tables(ps):
    i = jnp.arange(ps.tq, dtype=jnp.int32)[:, None]
    j = jnp.arange(ps.w, dtype=jnp.int32)[None, :]
    delta = (i + ps.win - j) * ps.unit
    n = jnp.zeros((ps.tq, ps.w), F32)
    for window, dil in ps.patterns:
        n = n + ((delta >= 0) & (delta <= window) & (delta % dil == 0)).astype(F32)
    logn = jnp.where(n > 0, jnp.log(jnp.maximum(n, 1.0)), NEG)
    return logn, jnp.maximum(delta, 0).astype(F32)


def _slopes(h):
    s = jnp.asarray([2.0 ** (-8.0 * (i + 1) / h) for i in range(h)], F32)
    return jnp.broadcast_to(s[:, None, None], (h, 1, LANES))


def _masked_logn(ps, logn_ref, start):
    col = lax.broadcasted_iota(jnp.int32, (ps.tq, ps.w), 1)
    return jnp.where(col >= ps.win - lax.rem(start, ps.seg_len), logn_ref[...], NEG)


def _head_cols(hh):
    return slice(hh * ATTN_HEAD_DIM, (hh + 1) * ATTN_HEAD_DIM)


def _head_window(refs, cs):
    return jnp.concatenate([r[:, cs] for r in refs], axis=0)


def _head_scores(q_ref, kw, cs, base, dist_ref, slope_ref, hh):
    return _nt(q_ref[:, cs], kw) * (ATTN_HEAD_DIM ** -0.5) + (base - slope_ref[hh][0:1, 0:1] * dist_ref[...])


def _lane_of(stat, hh):
    lane = lax.broadcasted_iota(jnp.int32, stat.shape, 1)
    return jnp.sum(jnp.where(lane == hh, stat, 0.0), axis=1, keepdims=True)


def _window_specs(ps, d, col, nb):
    nprev = ps.win // ps.tq
    return [pl.BlockSpec((ps.tq, d), lambda i, b=b: (jnp.maximum(jnp.minimum(i, nb - 1) - (nprev - b), 0), col))
            for b in range(nprev + 1)]


def _attn_fwd(cfg, ps, qkv, cols, tables, slopes, name):
    s, h, d = cfg.S, cfg.H, cfg.D
    tq, nw = ps.tq, ps.win // ps.tq + 1
    nb = s // tq
    logn, dist = tables
    qc, kc, vc = [c // d for c in cols]

    def body(*refs):
        q_ref, k_refs, v_refs = refs[0], refs[1:1 + nw], refs[1 + nw:1 + 2 * nw]
        logn_ref, dist_ref, slope_ref, o_ref, lse_ref = refs[1 + 2 * nw:]
        base = _masked_logn(ps, logn_ref, pl.program_id(0) * tq)
        lane = lax.broadcasted_iota(jnp.int32, (tq, LANES), 1)

        lse = jnp.zeros((tq, LANES), F32)
        for hh in range(h):
            cs = _head_cols(hh)
            sc = _head_scores(q_ref, _head_window(k_refs, cs), cs, base, dist_ref, slope_ref, hh)
            m = jnp.max(sc, axis=1, keepdims=True)
            p = jnp.exp(sc - m)
            l = jnp.sum(p, axis=1, keepdims=True)
            o_ref[:, cs] = (_nn(p.astype(BF16), _head_window(v_refs, cs)) / l).astype(BF16)
            lse = jnp.where(lane == hh, m + jnp.log(l), lse)
        lse_ref[...] = lse

    tab = pl.BlockSpec((tq, ps.w), lambda i: (0, 0))
    return pl.pallas_call(
        body, out_shape=(SDS((s, d), BF16), SDS((s, LANES), F32)), grid=(nb,),
        in_specs=[pl.BlockSpec((tq, d), lambda i: (i, qc))] + _window_specs(ps, d, kc, nb) + _window_specs(ps, d, vc, nb)
        + [tab, tab, pl.BlockSpec((h, 1, LANES), lambda i: (0, 0, 0))],
        out_specs=(pl.BlockSpec((tq, d), lambda i: (i, 0)), pl.BlockSpec((tq, LANES), lambda i: (i, 0))),
        compiler_params=_params(("parallel",)), name=name)(*([qkv] * (1 + 2 * nw)), logn, dist, slopes)


def _attn_bwd(cfg, ps, qkv, cols, do, lse, delta, tables, slopes, name):
    s, h, d = cfg.S, cfg.H, cfg.D
    tq, nprev = ps.tq, ps.win // ps.tq
    nw = nprev + 1
    nb = s // tq
    logn, dist = tables
    qc, kc, vc = [c // d for c in cols]
    scale = ATTN_HEAD_DIM ** -0.5

    def body(*refs):
        q_ref, k_refs, v_refs = refs[0], refs[1:1 + nw], refs[1 + nw:1 + 2 * nw]
        do_ref, lse_ref, dl_ref, logn_ref, dist_ref, slope_ref, dq_ref, dk_ref, dv_ref, ck, cv = refs[1 + 2 * nw:]
        i = pl.program_id(0)
        slot = lambda b: lax.rem(i + b, nprev)

        @pl.when(i == 0)
        def _():
            ck[...] = jnp.zeros_like(ck)
            cv[...] = jnp.zeros_like(cv)

        @pl.when(i < nb)
        def _():
            base = _masked_logn(ps, logn_ref, i * tq)
            lse_all, dl_all = lse_ref[...], dl_ref[...]

            for hh in range(h):
                cs = _head_cols(hh)
                kw, vw = _head_window(k_refs, cs), _head_window(v_refs, cs)
                sc = _head_scores(q_ref, kw, cs, base, dist_ref, slope_ref, hh)
                p = jnp.exp(sc - lse_all[:, hh:hh + 1])
                dob = do_ref[:, cs]
                ds = (p * (_nt(dob, vw) - dl_all[:, hh:hh + 1]) * scale).astype(BF16)
                dq_ref[:, cs] = _nn(ds, kw).astype(BF16)
                dkw = _tn(ds, q_ref[:, cs])
                dvw = _tn(p.astype(BF16), dob)
                dk_ref[:, cs] = ck[slot(0), :, cs] + dkw[0:tq]
                dv_ref[:, cs] = cv[slot(0), :, cs] + dvw[0:tq]
                for b in range(1, nprev):
                    ck[slot(b), :, cs] += dkw[b * tq:(b + 1) * tq]
                    cv[slot(b), :, cs] += dvw[b * tq:(b + 1) * tq]
                ck[slot(0), :, cs] = dkw[nprev * tq:]
                cv[slot(0), :, cs] = dvw[nprev * tq:]

        @pl.when(i >= nb)
        def _():
            dk_ref[...] = ck[slot(0)]
            dv_ref[...] = cv[slot(0)]

    here = lambda i: jnp.minimum(i, nb - 1)
    blk = pl.BlockSpec((tq, d), lambda i: (here(i), 0))
    stat = pl.BlockSpec((tq, LANES), lambda i: (here(i), 0))
    late = pl.BlockSpec((tq, d), lambda i: (jnp.maximum(i - nprev, 0), 0))
    tab = pl.BlockSpec((tq, ps.w), lambda i: (0, 0))
    return pl.pallas_call(
        body, out_shape=(SDS((s, d), BF16), SDS((s, d), F32), SDS((s, d), F32)), grid=(nb + nprev,),
        in_specs=[pl.BlockSpec((tq, d), lambda i: (here(i), qc))] + _window_specs(ps, d, kc, nb)
        + _window_specs(ps, d, vc, nb) + [blk, stat, stat, tab, tab, pl.BlockSpec((h, 1, LANES), lambda i: (0, 0, 0))],
        out_specs=(blk, late, late),
        scratch_shapes=[pltpu.VMEM((nprev, tq, d), F32), pltpu.VMEM((nprev, tq, d), F32)],
        compiler_params=_params(("arbitrary",)), name=name)(
            *([qkv] * (1 + 2 * nw)), do, lse, delta, logn, dist, slopes)


def _by_residue(a):
    return a.reshape(DEINT, a.shape[0] // DEINT, a.shape[1])


def _deint_spec(colblock):
    return pl.BlockSpec((DEINT, LANES, LANES), lambda b, j: (0, b, colblock(j)))


def _deint_rows(scr, out_ref, dtype):
    for r in range(DEINT):
        out_ref[r] = scr[pl.ds(r, LANES, stride=DEINT), :].astype(dtype)


def _int_rows(in_ref, scr):
    for r in range(DEINT):
        scr[pl.ds(r, LANES, stride=DEINT), :] = in_ref[r].astype(F32)


def _deinterleave(x, col0, ncols, name):
    s = x.shape[0]
    c0 = col0 // LANES

    def body(x_ref, o_ref, scr):
        scr[...] = x_ref[...].astype(F32)
        _deint_rows(scr, o_ref, x.dtype)

    out = pl.pallas_call(
        body, out_shape=SDS((DEINT, s // DEINT, ncols), x.dtype), grid=(s // DEINT_ROWS, ncols // LANES),
        in_specs=[pl.BlockSpec((DEINT_ROWS, LANES), lambda b, j: (b, c0 + j))],
        out_specs=_deint_spec(lambda j: j),
        scratch_shapes=[pltpu.VMEM((DEINT_ROWS, LANES), F32)],
        compiler_params=_params(("parallel", "parallel")), name=name)(x)
    return out.reshape(s, ncols)


def _attn_merge(cfg, proj, o_1, lse_1, o_2, lse_2):
    s, h = cfg.S, cfg.H
    zb = cfg.OZA // LANES
    rows = DEINT_ROWS

    def body(o1_ref, l1_ref, o2_ref, l2_ref, z_ref, o_ref, og_ref, lse_ref, so, sl):
        hh = pl.program_id(1)
        _int_rows(o2_ref, so)

        @pl.when(hh == 0)
        def _():
            _int_rows(l2_ref, sl)

        l1, l2 = _lane_of(l1_ref[...], hh), _lane_of(sl[...], hh)
        mx = jnp.maximum(l1, l2)
        w1, w2 = jnp.exp(l1 - mx), jnp.exp(l2 - mx)
        den = w1 + w2
        o = (w1 * o1_ref[...].astype(F32) + w2 * so[...]) / den
        z = z_ref[...].astype(F32)
        o_ref[...] = o.astype(BF16)
        og_ref[...] = (o * (z * _sigmoid(z))).astype(BF16)

        @pl.when(hh == 0)
        def _():
            lse_ref[...] = jnp.zeros_like(lse_ref)

        lane = lax.broadcasted_iota(jnp.int32, (rows, LANES), 1)
        lse_ref[...] = jnp.where(lane == hh, mx + jnp.log(den), lse_ref[...])

    blk = pl.BlockSpec((rows, LANES), lambda b, j: (b, j))
    return pl.pallas_call(
        body, out_shape=(SDS((s, cfg.D), BF16), SDS((s, cfg.D), BF16), SDS((s, LANES), F32)),
        grid=(s // rows, h),
        in_specs=[blk, pl.BlockSpec((rows, LANES), lambda b, j: (b, 0)), _deint_spec(lambda j: j),
                  _deint_spec(lambda j: 0), pl.BlockSpec((rows, LANES), lambda b, j: (b, zb + j))],
        out_specs=(blk, blk, pl.BlockSpec((rows, LANES), lambda b, j: (b, 0))),
        scratch_shapes=[pltpu.VMEM((rows, LANES), F32), pltpu.VMEM((rows, LANES), F32)],
        compiler_params=_params(("parallel", "arbitrary")), name="attn_merge")(
            o_1, lse_1, _by_residue(o_2), _by_residue(lse_2), proj)


def _attn_bwd_prep(cfg, proj, o_a, doag, lse, dproj):
    s, h = cfg.S, cfg.H
    zb = cfg.OZA // LANES
    rows = DEINT_ROWS

    def body(o_ref, dg_ref, z_ref, lse_ref, dp_in, dz_ref, do_ref, do2_ref, dl_ref, dl2_ref, lse2_ref, scr):
        del dp_in
        hh = pl.program_id(1)
        z = z_ref[...].astype(F32)
        sg = _sigmoid(z)
        o = o_ref[...].astype(F32)
        dg = dg_ref[...].astype(F32)
        do = dg * (z * sg)
        dz_ref[...] = (dg * o * (sg * (1.0 + z * (1.0 - sg)))).astype(BF16)
        do_ref[...] = do.astype(BF16)
        scr[...] = do
        _deint_rows(scr, do2_ref, BF16)

        @pl.when(hh == 0)
        def _():
            dl_ref[...] = jnp.zeros_like(dl_ref)

        lane = lax.broadcasted_iota(jnp.int32, (rows, LANES), 1)
        dl_ref[...] = jnp.where(lane == hh, jnp.sum(do * o, axis=1, keepdims=True), dl_ref[...])

        @pl.when(hh == h - 1)
        def _():
            scr[...] = dl_ref[...]
            _deint_rows(scr, dl2_ref, F32)
            scr[...] = lse_ref[...]
            _deint_rows(scr, lse2_ref, F32)

    blk = pl.BlockSpec((rows, LANES), lambda b, j: (b, j))
    stat = pl.BlockSpec((rows, LANES), lambda b, j: (b, 0))
    stat2 = _deint_spec(lambda j: 0)
    outs = pl.pallas_call(
        body,
        out_shape=(SDS(dproj.shape, BF16), SDS((s, cfg.D), BF16), SDS((DEINT, s // DEINT, cfg.D), BF16),
                   SDS((s, LANES), F32), SDS((DEINT, s // DEINT, LANES), F32), SDS((DEINT, s // DEINT, LANES), F32)),
        grid=(s // rows, h),
        in_specs=[blk, blk, pl.BlockSpec((rows, LANES), lambda b, j: (b, zb + j)), stat, HBM_SPEC],
        out_specs=(pl.BlockSpec((rows, LANES), lambda b, j: (b, zb + j)), blk, _deint_spec(lambda j: j),
                   stat, stat2, stat2),
        scratch_shapes=[pltpu.VMEM((rows, LANES), F32)],
        input_output_aliases={4: 0},
        compiler_params=_params(("parallel", "arbitrary")), name="attn_bwd_prep")(o_a, doag, proj, lse, dproj)
    dproj, do, do2, dl, dl2, lse2 = outs
    return dproj, do, do2.reshape(s, cfg.D), dl, dl2.reshape(s, LANES), lse2.reshape(s, LANES)


def _attn_grad_sum(cfg, g_1, g_2, col0, dproj, name):
    s, h = cfg.S, cfg.H
    c0 = col0 // LANES
    rows = DEINT_ROWS

    def body(g1_ref, g2_ref, dp_in, o_ref, scr):
        del dp_in
        _int_rows(g2_ref, scr)
        o_ref[...] = (g1_ref[...].astype(F32) + scr[...]).astype(BF16)

    return pl.pallas_call(
        body, out_shape=SDS(dproj.shape, BF16), grid=(s // rows, h),
        in_specs=[pl.BlockSpec((rows, LANES), lambda b, j: (b, j)), _deint_spec(lambda j: j), HBM_SPEC],
        out_specs=pl.BlockSpec((rows, LANES), lambda b, j: (b, c0 + j)),
        scratch_shapes=[pltpu.VMEM((rows, LANES), F32)],
        input_output_aliases={2: 0},
        compiler_params=_params(("parallel", "parallel")), name=name)(g_1, _by_residue(g_2), dproj)


CONV_HALO = 16
CONV_TR = 512
CONV_CW = 512


def _conv_fwd(cfg, proj, conv_w, conv_b):
    s, cd = cfg.S, cfg.CD
    tr, cw, hl = CONV_TR, CONV_CW, CONV_HALO
    cb0 = cfg.OXBC // cw

    def body(x_ref, h_ref, w_ref, b_ref, o_ref, scr):
        i = pl.program_id(0)
        scr[pl.ds(0, hl), :] = jnp.where(i > 0, h_ref[...].astype(F32), 0.0)
        scr[pl.ds(hl, tr), :] = x_ref[...].astype(F32)
        pre = b_ref[...] + jnp.zeros((tr, cw), F32)
        for k in range(CONV_K):
            pre = pre + w_ref[k:k + 1, :] * scr[pl.ds(hl - (CONV_K - 1) + k, tr), :]
        o_ref[...] = (pre * _sigmoid(pre)).astype(BF16)

    return pl.pallas_call(
        body, out_shape=SDS((s, cd), BF16), grid=(s // tr, cd // cw),
        in_specs=[pl.BlockSpec((tr, cw), lambda i, j: (i, cb0 + j)),
                  pl.BlockSpec((hl, cw), lambda i, j: (jnp.maximum(i * (tr // hl) - 1, 0), cb0 + j)),
                  pl.BlockSpec((CONV_K, cw), lambda i, j: (0, j)),
                  pl.BlockSpec((1, cw), lambda i, j: (0, j))],
        out_specs=pl.BlockSpec((tr, cw), lambda i, j: (i, j)),
        scratch_shapes=[pltpu.VMEM((tr + hl, cw), F32)],
        compiler_params=_params(("parallel", "parallel")), name="conv_fwd")(proj, proj, conv_w, conv_b)


def _conv_bwd(cfg, proj, dact, conv_w, conv_b, dproj):
    s, cd = cfg.S, cfg.CD
    tr, cw, hl = CONV_TR, CONV_CW, CONV_HALO
    cb0 = cfg.OXBC // cw
    nr = s // tr
    last_h = s // hl - 1

    def body(x_ref, hp_ref, hn_ref, d_ref, dn_ref, w_ref, b_ref, dp_in, dx_ref, gw_ref, gb_ref, xs, ds):
        del dp_in
        i = pl.program_id(1)
        xs[pl.ds(0, hl), :] = jnp.where(i > 0, hp_ref[...].astype(F32), 0.0)
        xs[pl.ds(hl, tr), :] = x_ref[...].astype(F32)
        xs[pl.ds(hl + tr, hl), :] = hn_ref[...].astype(F32)
        shifted = [xs[pl.ds(hl - (CONV_K - 1) + k, tr + hl), :] for k in range(CONV_K)]
        pre = b_ref[...] + jnp.zeros((tr + hl, cw), F32)
        for k in range(CONV_K):
            pre = pre + w_ref[k:k + 1, :] * shifted[k]
        sg = _sigmoid(pre)
        dsilu = sg * (1.0 + pre * (1.0 - sg))
        ds[pl.ds(0, tr), :] = d_ref[...].astype(F32) * dsilu[0:tr]
        ds[pl.ds(tr, hl), :] = jnp.where(i < nr - 1, dn_ref[...].astype(F32), 0.0) * dsilu[tr:tr + hl]
        dx = jnp.zeros((tr, cw), F32)
        for k in range(CONV_K):
            dx = dx + w_ref[k:k + 1, :] * ds[pl.ds(CONV_K - 1 - k, tr), :]
        dx_ref[...] = dx.astype(BF16)

        @pl.when(i == 0)
        def _():
            gw_ref[...] = jnp.zeros_like(gw_ref)
            gb_ref[...] = jnp.zeros_like(gb_ref)

        dcur = ds[pl.ds(0, tr), :]
        gb_ref[...] += jnp.sum(dcur, axis=0, keepdims=True)
        for k in range(CONV_K):
            gw_ref[k:k + 1, :] += jnp.sum(dcur * shifted[k][0:tr], axis=0, keepdims=True)

    return pl.pallas_call(
        body, out_shape=(SDS(dproj.shape, BF16), SDS((CONV_K, cd), F32), SDS((1, cd), F32)), grid=(cd // cw, nr),
        in_specs=[pl.BlockSpec((tr, cw), lambda j, i: (i, cb0 + j)),
                  pl.BlockSpec((hl, cw), lambda j, i: (jnp.maximum(i * (tr // hl) - 1, 0), cb0 + j)),
                  pl.BlockSpec((hl, cw), lambda j, i: (jnp.minimum((i + 1) * (tr // hl), last_h), cb0 + j)),
                  pl.BlockSpec((tr, cw), lambda j, i: (i, j)),
                  pl.BlockSpec((hl, cw), lambda j, i: (jnp.minimum((i + 1) * (tr // hl), last_h), j)),
                  pl.BlockSpec((CONV_K, cw), lambda j, i: (0, j)),
                  pl.BlockSpec((1, cw), lambda j, i: (0, j)),
                  pl.BlockSpec(memory_space=pl.ANY)],
        out_specs=(pl.BlockSpec((tr, cw), lambda j, i: (i, cb0 + j)),
                   pl.BlockSpec((CONV_K, cw), lambda j, i: (0, j)),
                   pl.BlockSpec((1, cw), lambda j, i: (0, j))),
        scratch_shapes=[pltpu.VMEM((tr + 2 * hl, cw), F32), pltpu.VMEM((tr + hl, cw), F32)],
        input_output_aliases={7: 0},
        compiler_params=_params(("parallel", "arbitrary")), name="conv_bwd")(
            proj, proj, proj, dact, dact, conv_w, conv_b, dproj)


def _expand(v, e, terms):
    out, rem = None, v
    for _ in range(terms):
        hi = rem.astype(BF16)
        t = _nn(hi, e)
        out = t if out is None else out + t
        rem = rem - hi.astype(F32)
    return out


def _segsum(v, e, terms):
    out, rem = None, v
    for _ in range(terms):
        hi = rem.astype(BF16)
        t = _nt(hi, e)
        out = t if out is None else out + t
        rem = rem - hi.astype(F32)
    return out


def _expand_row(row, e, terms):
    return _expand(jnp.broadcast_to(row, (8, LANES)), e, terms)[0:1]


def _segsum_row(row, e, terms):
    return _segsum(jnp.broadcast_to(row, (8, row.shape[1])), e, terms)[0:1]


def _expansion_matrix(cfg):
    hh = jnp.arange(LANES, dtype=jnp.int32)[:, None]
    cc = jnp.arange(cfg.SI, dtype=jnp.int32)[None, :]
    return (cc // SSM_HEAD_DIM == hh).astype(BF16)


def _tri(lower):
    r = lax.broadcasted_iota(jnp.int32, (CHUNK, CHUNK), 0)
    c = lax.broadcasted_iota(jnp.int32, (CHUNK, CHUNK), 1)
    return (c <= r) if lower else (c >= r)


def _ssd_prep(dtr_ref, db_ref, al_ref, e):
    dtr = dtr_ref[...] + db_ref[...]
    dt = _softplus(dtr)
    a = -jnp.exp(al_ref[...])
    acum = jnp.dot(_tri(True).astype(F32), dt * a, precision=lax.Precision.HIGHEST, preferred_element_type=F32)
    return dtr, dt, a, _expand(dt, e, 2), _expand(acum, e, 3)


def _ssd_fwd(cfg, xact, dt_raw, proj, dt_bias, a_log, d_skip, norm_w, e):
    s, si, cd, gw, bc = cfg.S, cfg.SI, cfg.CD, cfg.GW, cfg.BC
    nc = s // CHUNK
    zb = cfg.OZS // si
    tiles = gw // LANES

    def body(xa_ref, dtr_ref, z_ref, db_ref, al_ref, dsk_ref, nw_ref, e_ref, y_ref, y2_ref, st_ref,
             state, ybuf, x_s, xw_s, ae_s, ea_s, lam_s):
        @pl.when(pl.program_id(0) == 0)
        def _():
            state[...] = jnp.zeros_like(state)

        st_ref[...] = state[...]
        ev = e_ref[...]
        _, _, _, dt_e, a_e = _ssd_prep(dtr_ref, db_ref, al_ref, ev)
        xs = xa_ref[:, 0:si].astype(F32)
        x = xs * dt_e
        lam_e = a_e[CHUNK - 1:CHUNK, :]
        x_s[...] = x.astype(BF16)
        xw_s[...] = (x * jnp.exp(lam_e - a_e)).astype(BF16)
        ae_s[...] = a_e
        ea_s[...] = jnp.exp(a_e)
        ybuf[...] = _expand_row(dsk_ref[...], ev, 3) * xs
        lam_s[...] = jnp.broadcast_to(jnp.exp(lam_e), (8, si))
        tril = _tri(True)
        lane = lax.broadcasted_iota(jnp.int32, (CHUNK, LANES), 1)

        def group(g, carry):
            co = pl.multiple_of(g * gw, LANES)
            bg = xa_ref[:, pl.ds(pl.multiple_of(si + g * SSM_STATE, LANES), SSM_STATE)]
            cg = xa_ref[:, pl.ds(pl.multiple_of(si + bc + g * SSM_STATE, LANES), SSM_STATE)]
            cbm = _nt(cg, bg)
            st = state[:, pl.ds(co, gw)]
            yoff = _nn(cg, st.astype(BF16)) * ea_s[:, pl.ds(co, gw)]
            for k in range(tiles):
                tc = pl.multiple_of(co + k * LANES, LANES)
                at = ae_s[:, pl.ds(tc, LANES)]
                att = at.T
                xt = x_s[:, pl.ds(tc, LANES)]
                acc = yoff[:, k * LANES:(k + 1) * LANES]
                for half in range(2):
                    lo = half * SSM_HEAD_DIM
                    seg = at[:, lo:lo + 1] - att[lo:lo + 1, :]
                    dec = jnp.exp(jnp.where(tril, seg, NEG))
                    xh = jnp.where((lane >= lo) & (lane < lo + SSM_HEAD_DIM), xt, jnp.zeros_like(xt))
                    acc = acc + _nn((cbm * dec).astype(BF16), xh)
                ybuf[:, pl.ds(tc, LANES)] += acc
            state[:, pl.ds(co, gw)] = st * lam_s[0:1, pl.ds(co, gw)] + _tn(bg, xw_s[:, pl.ds(co, gw)])
            return carry

        lax.fori_loop(0, SSM_GROUPS, group, 0)
        y = ybuf[...]
        y_ref[...] = y.astype(BF16)
        z = z_ref[...].astype(F32)
        u = y * (z * _sigmoid(z))
        r = lax.rsqrt(jnp.mean(u * u, axis=-1, keepdims=True) + RMS_EPS)
        y2_ref[...] = (u * r * nw_ref[...]).astype(BF16)

    row = lambda n: pl.BlockSpec((1, n), lambda c: (0, 0))
    return pl.pallas_call(
        body,
        out_shape=(SDS((s, si), BF16), SDS((s, si), BF16), SDS((nc, SSM_STATE, si), F32)),
        grid=(nc,),
        in_specs=[pl.BlockSpec((CHUNK, cd), lambda c: (c, 0)),
                  pl.BlockSpec((CHUNK, LANES), lambda c: (c, 0)),
                  pl.BlockSpec((CHUNK, si), lambda c: (c, zb)),
                  row(LANES), row(LANES), row(LANES), row(si),
                  pl.BlockSpec((LANES, si), lambda c: (0, 0))],
        out_specs=(pl.BlockSpec((CHUNK, si), lambda c: (c, 0)),
                   pl.BlockSpec((CHUNK, si), lambda c: (c, 0)),
                   pl.BlockSpec((None, SSM_STATE, si), lambda c: (c, 0, 0))),
        scratch_shapes=[pltpu.VMEM((SSM_STATE, si), F32), pltpu.VMEM((CHUNK, si), F32),
                        pltpu.VMEM((CHUNK, si), BF16), pltpu.VMEM((CHUNK, si), BF16),
                        pltpu.VMEM((CHUNK, si), F32), pltpu.VMEM((CHUNK, si), F32),
                        pltpu.VMEM((8, si), F32)],
        compiler_params=_params(("arbitrary",)), name="ssd_fwd")(
            xact, dt_raw, proj, dt_bias, a_log, d_skip, norm_w, e)


def _ssd_bwd(cfg, xact, dt_raw, proj, y, dy2, states, dt_bias, a_log, d_skip, norm_w, e, dproj):
    s, si, cd, gw, bc, hpg = cfg.S, cfg.SI, cfg.CD, cfg.GW, cfg.BC, cfg.HPG
    nc = s // CHUNK
    zb = cfg.OZS // si
    tiles = gw // LANES

    def body(xa_ref, dtr_ref, z_ref, y_ref, d2_ref, st_ref, db_ref, al_ref, dsk_ref, nw_ref, e_ref, dp_in,
             dz_ref, dxa_ref, ddt_ref, gnw_ref, gdb_ref, gal_ref, gds_ref,
             dh, dhn, xs_s, x_s, w_s, ae_s, ea_s, g_s, dx_s, dae_s, r_s, lam_s, dle_s):
        del dp_in

        @pl.when(pl.program_id(0) == 0)
        def _():
            dh[...] = jnp.zeros_like(dh)
            gnw_ref[...] = jnp.zeros_like(gnw_ref)
            gdb_ref[...] = jnp.zeros_like(gdb_ref)
            gal_ref[...] = jnp.zeros_like(gal_ref)
            gds_ref[...] = jnp.zeros_like(gds_ref)

        ev = e_ref[...]
        yv = y_ref[...].astype(F32)
        z = z_ref[...].astype(F32)
        sg = _sigmoid(z)
        sz = z * sg
        u = yv * sz
        r = lax.rsqrt(jnp.mean(u * u, axis=-1, keepdims=True) + RMS_EPS)
        nrm = u * r
        d2 = d2_ref[...].astype(F32)
        gnw_ref[...] += jnp.sum(d2 * nrm, axis=0, keepdims=True)
        gn = d2 * nw_ref[...]
        du = r * (gn - nrm * jnp.mean(gn * nrm, axis=-1, keepdims=True))
        gv = du * sz
        dz_ref[...] = (du * yv * (sg * (1.0 + z * (1.0 - sg)))).astype(BF16)
        g_s[...] = gv

        dtr, dt, a, dt_e, a_e = _ssd_prep(dtr_ref, db_ref, al_ref, ev)
        xs = xa_ref[:, 0:si].astype(F32)
        x = xs * dt_e
        lam_e = a_e[CHUNK - 1:CHUNK, :]
        xs_s[...] = xs
        x_s[...] = x
        w_s[...] = jnp.exp(lam_e - a_e)
        ae_s[...] = a_e
        ea_s[...] = jnp.exp(a_e)
        lam_s[...] = jnp.broadcast_to(jnp.exp(lam_e), (8, si))
        gds_ref[...] += _segsum_row(jnp.sum(gv * xs, axis=0, keepdims=True), ev, 2)
        r_s[...] = jnp.zeros_like(r_s)
        tril = _tri(True)
        lane = lax.broadcasted_iota(jnp.int32, (CHUNK, LANES), 1)
        sub = lax.broadcasted_iota(jnp.int32, (CHUNK, LANES), 0)

        def group(g, carry):
            co = pl.multiple_of(g * gw, LANES)
            bo = pl.multiple_of(si + g * SSM_STATE, LANES)
            cof = pl.multiple_of(si + bc + g * SSM_STATE, LANES)
            cols = pl.ds(co, gw)
            bg = xa_ref[:, pl.ds(bo, SSM_STATE)]
            cg = xa_ref[:, pl.ds(cof, SSM_STATE)]
            cbm = _nt(cg, bg)
            st = st_ref[:, cols]
            stb = st.astype(BF16)
            dho = dh[:, cols]
            dhob = dho.astype(BF16)
            ea = ea_s[:, cols]
            gg = g_s[:, cols]
            xg = x_s[:, cols]
            wg = w_s[:, cols]
            explam = lam_s[0:1, cols]
            yoff = _nn(cg, stb) * ea
            ga = (gg * ea).astype(BF16)
            dc = _nt(ga, stb)
            dhn[:, cols] = dho * explam + _tn(cg, ga)
            bdh = _nn(bg, dhob)
            db = _nt((xg * wg).astype(BF16), dhob)
            t = xg * bdh * wg
            dle_s[0:1, cols] = jnp.sum(t, axis=0, keepdims=True) + explam * jnp.sum(dho * st, axis=0, keepdims=True)
            dae_base = gg * yoff - t
            dxw = wg * bdh
            dcb = jnp.zeros((CHUNK, CHUNK), F32)
            for k in range(tiles):
                tc = pl.multiple_of(co + k * LANES, LANES)
                ksl = slice(k * LANES, (k + 1) * LANES)
                at = ae_s[:, pl.ds(tc, LANES)]
                att = at.T
                xt = xg[:, ksl].astype(BF16)
                gt = gg[:, ksl].astype(BF16)
                dxt = dxw[:, ksl]
                place = jnp.zeros((CHUNK, LANES), F32)
                for half in range(2):
                    lo = half * SSM_HEAD_DIM
                    seg = at[:, lo:lo + 1] - att[lo:lo + 1, :]
                    dec = jnp.exp(jnp.where(tril, seg, NEG))
                    mh = cbm * dec
                    gh = jnp.where((lane >= lo) & (lane < lo + SSM_HEAD_DIM), gt, jnp.zeros_like(gt))
                    dm = _nt(gh, xt)
                    dxt = dxt + _tn(mh.astype(BF16), gh)
                    dcb = dcb + dm * dec
                    dseg = dm * mh
                    place = place + jnp.where(lane == lo, jnp.sum(dseg, axis=1, keepdims=True), 0.0)
                    hidx = g * hpg + 2 * k + half
                    r_s[...] += jnp.where(sub == hidx, jnp.sum(dseg, axis=0, keepdims=True), 0.0)
                dx_s[:, pl.ds(tc, LANES)] = dxt
                dae_s[:, pl.ds(tc, LANES)] = dae_base[:, ksl] + place
            dcbb = dcb.astype(BF16)
            dxa_ref[:, pl.ds(bo, SSM_STATE)] = (db + _tn(dcbb, cg)).astype(BF16)
            dxa_ref[:, pl.ds(cof, SSM_STATE)] = (dc + _nn(dcbb, bg)).astype(BF16)
            return carry

        lax.fori_loop(0, SSM_GROUPS, group, 0)
        dlam = _segsum_row(dle_s[0:1, :], ev, 2)
        da_ = _segsum(dae_s[...], ev, 2) - r_s[...].T
        da_ = da_ + jnp.where(sub == CHUNK - 1, dlam, 0.0)
        dda = jnp.dot(_tri(False).astype(F32), da_, precision=lax.Precision.HIGHEST, preferred_element_type=F32)
        dxv = dx_s[...]
        xs = xs_s[...]
        ddt = dda * a + _segsum(dxv * xs, ev, 2)
        gal_ref[...] += jnp.sum(dda * dt, axis=0, keepdims=True) * a
        ddtr = ddt * _sigmoid(dtr)
        gdb_ref[...] += jnp.sum(ddtr, axis=0, keepdims=True)
        ddt_ref[...] = ddtr
        dxa_ref[:, 0:si] = (dxv * dt_e + g_s[...] * _expand_row(dsk_ref[...], ev, 3)).astype(BF16)
        dh[...] = dhn[...]

    rev = lambda c: nc - 1 - c
    row = lambda n: pl.BlockSpec((1, n), lambda c: (0, 0))
    big = lambda: pltpu.VMEM((CHUNK, si), F32)
    return pl.pallas_call(
        body,
        out_shape=(SDS(dproj.shape, BF16), SDS((s, cd), BF16), SDS((s, LANES), F32),
                   SDS((1, si), F32), SDS((1, LANES), F32), SDS((1, LANES), F32), SDS((1, LANES), F32)),
        grid=(nc,),
        in_specs=[pl.BlockSpec((CHUNK, cd), lambda c: (rev(c), 0)),
                  pl.BlockSpec((CHUNK, LANES), lambda c: (rev(c), 0)),
                  pl.BlockSpec((CHUNK, si), lambda c: (rev(c), zb)),
                  pl.BlockSpec((CHUNK, si), lambda c: (rev(c), 0)),
                  pl.BlockSpec((CHUNK, si), lambda c: (rev(c), 0)),
                  pl.BlockSpec((None, SSM_STATE, si), lambda c: (rev(c), 0, 0)),
                  row(LANES), row(LANES), row(LANES), row(si),
                  pl.BlockSpec((LANES, si), lambda c: (0, 0)),
                  pl.BlockSpec(memory_space=pl.ANY)],
        out_specs=(pl.BlockSpec((CHUNK, si), lambda c: (rev(c), zb)),
                   pl.BlockSpec((CHUNK, cd), lambda c: (rev(c), 0)),
                   pl.BlockSpec((CHUNK, LANES), lambda c: (rev(c), 0)),
                   row(si), row(LANES), row(LANES), row(LANES)),
        scratch_shapes=[pltpu.VMEM((SSM_STATE, si), F32), pltpu.VMEM((SSM_STATE, si), F32),
                        big(), big(), big(), big(), big(), big(), big(), big(),
                        pltpu.VMEM((CHUNK, LANES), F32), pltpu.VMEM((8, si), F32), pltpu.VMEM((8, si), F32)],
        input_output_aliases={11: 0},
        compiler_params=_params(("arbitrary",)), name="ssd_bwd")(
            xact, dt_raw, proj, y, dy2, states, dt_bias, a_log, d_skip, norm_w, e, dproj)


MERGE_TR = 512
MERGE_CW = 512


def _merge_fwd(cfg, proj, a_br, s_br):
    s, d = cfg.S, cfg.D
    tr, cw = MERGE_TR, MERGE_CW
    ga0, gs0 = cfg.OGA // cw, cfg.OGS // cw

    def body(ga_ref, gs_ref, a_ref, s_ref, o_ref):
        o_ref[...] = (_sigmoid(ga_ref[...].astype(F32)) * a_ref[...].astype(F32)
                      + _sigmoid(gs_ref[...].astype(F32)) * s_ref[...].astype(F32)).astype(BF16)

    blk = pl.BlockSpec((tr, cw), lambda i, j: (i, j))
    return pl.pallas_call(
        body, out_shape=SDS((s, d), BF16), grid=(s // tr, d // cw),
        in_specs=[pl.BlockSpec((tr, cw), lambda i, j: (i, ga0 + j)),
                  pl.BlockSpec((tr, cw), lambda i, j: (i, gs0 + j)), blk, blk],
        out_specs=blk, compiler_params=_params(("parallel", "parallel")), name="merge_fwd")(proj, proj, a_br, s_br)


def _merge_bwd(cfg, proj, branch, dmerged, gate_off, dproj, name):
    s, d = cfg.S, cfg.D
    tr, cw = MERGE_TR, MERGE_CW
    g0 = gate_off // cw
    fresh = dproj is None

    def body(*refs):
        g_ref, b_ref, dm_ref = refs[:3]
        dg_ref, db_ref = refs[-2:]
        dm = dm_ref[...].astype(F32)
        sg = _sigmoid(g_ref[...].astype(F32))
        db_ref[...] = (dm * sg).astype(BF16)
        dg_ref[...] = (dm * b_ref[...].astype(F32) * sg * (1.0 - sg)).astype(BF16)

    blk = pl.BlockSpec((tr, cw), lambda i, j: (i, j))
    gate = pl.BlockSpec((tr, cw), lambda i, j: (i, g0 + j))
    return pl.pallas_call(
        body, out_shape=(SDS((s, cfg.NM), BF16), SDS((s, d), BF16)), grid=(s // tr, d // cw),
        in_specs=[gate, blk, blk] + ([] if fresh else [HBM_SPEC]),
        out_specs=(gate, blk),
        input_output_aliases={} if fresh else {3: 0},
        compiler_params=_params(("parallel", "parallel")), name=name)(
            *((proj, branch, dmerged) + (() if fresh else (dproj,))))


def _outproj_loss(merged, w_out, x, target, fnw):
    s, d = x.shape
    tr = 256

    def body(m_ref, w_ref, x_ref, t_ref, fw_ref, dof_ref, dob_ref, loss_ref, g_ref):
        out = x_ref[...] + _nn(m_ref[...], w_ref[...])
        r = lax.rsqrt(jnp.mean(out * out, axis=-1, keepdims=True) + RMS_EPS)
        nrm = out * r
        fw = fw_ref[...]
        err = nrm * fw - t_ref[...]
        dy = err * (1.0 / d)
        gy = dy * fw
        dout = r * (gy - nrm * jnp.mean(gy * nrm, axis=-1, keepdims=True))
        dof_ref[...] = dout
        dob_ref[...] = dout.astype(BF16)

        @pl.when(pl.program_id(0) == 0)
        def _():
            loss_ref[...] = jnp.zeros_like(loss_ref)
            g_ref[...] = jnp.zeros_like(g_ref)

        loss_ref[...] += jnp.sum(jnp.sum(err * err, axis=1, keepdims=True), axis=0, keepdims=True) * (0.5 / d)
        g_ref[...] += jnp.sum(dy * nrm, axis=0, keepdims=True)

    blk = pl.BlockSpec((tr, d), lambda i: (i, 0))
    return pl.pallas_call(
        body, out_shape=(SDS((s, d), F32), SDS((s, d), BF16), SDS((1, LANES), F32), SDS((1, d), F32)), grid=(s // tr,),
        in_specs=[blk, pl.BlockSpec((d, d), lambda i: (0, 0)), blk, blk, pl.BlockSpec((1, d), lambda i: (0, 0))],
        out_specs=(blk, blk, pl.BlockSpec((1, LANES), lambda i: (0, 0)), pl.BlockSpec((1, d), lambda i: (0, 0))),
        compiler_params=_params(("arbitrary",)), name="outproj_loss")(merged, w_out, x, target, fnw)


ELEMWISE_BLOCK_BYTES = 1 << 20


def _row_block(rows, cols, itemsize=4):
    best = None
    for tr in range(16, rows + 1, 16):
        if rows % tr == 0 and tr * cols * itemsize <= ELEMWISE_BLOCK_BYTES:
            best = tr
    return best if best is not None else rows


def _adamw(w, g, m, v, name):
    rows, cols = w.shape
    tr = _row_block(rows, cols)

    def body(w_ref, g_ref, m_ref, v_ref, d_ref, nm_ref, nv_ref):
        gv = g_ref[...]
        nm = ADAM_B1 * m_ref[...] + (1.0 - ADAM_B1) * gv
        nv = ADAM_B2 * v_ref[...] + (1.0 - ADAM_B2) * jnp.square(gv)
        m_hat = nm / (1.0 - ADAM_B1 ** ADAM_STEP)
        v_hat = nv / (1.0 - ADAM_B2 ** ADAM_STEP)
        d_ref[...] = -ADAM_LR * (m_hat / (jnp.sqrt(v_hat) + ADAM_EPS) + ADAM_WD * w_ref[...])
        nm_ref[...] = nm
        nv_ref[...] = nv

    blk = pl.BlockSpec((tr, cols), lambda i: (i, 0))
    out = SDS((rows, cols), F32)
    return pl.pallas_call(
        body, out_shape=(out, out, out), grid=(rows // tr,), in_specs=[blk] * 4, out_specs=(blk,) * 3,
        compiler_params=_params(("parallel",)), name=name)(w, g, m, v)


HBM_SPEC = pl.BlockSpec(memory_space=pl.ANY)


def _position():
    return lax.axis_index("x"), lax.axis_index("y"), lax.axis_index("c")


def _gather_chips(shards):
    n = len(shards)

    def body(*refs):
        ins, outs = refs[:n], refs[n:2 * n]
        send_sems, recv_sems, fsend_sems, frecv_sems = refs[2 * n:]
        x, y, c = _position()
        me = 2 * x + y
        peers = [(1 - x, y), (x, 1 - y), (1 - x, 1 - y)]

        def over_ici(t, p, chip):
            px, py = peers[p]
            r2 = ins[t].shape[0] // 2
            return pltpu.make_async_remote_copy(
                src_ref=ins[t].at[pl.ds(c * r2, r2), :], dst_ref=outs[t].at[chip, c], send_sem=send_sems.at[3 * t + p],
                recv_sem=recv_sems.at[3 * t + p], device_id=(px, py, c), device_id_type=MESH)

        def to_sibling(t, p, half):
            px, py = peers[p]
            slab = outs[t].at[2 * px + py, half]
            return pltpu.make_async_remote_copy(
                src_ref=slab, dst_ref=slab, send_sem=fsend_sems.at[3 * t + p], recv_sem=frecv_sems.at[3 * t + p],
                device_id=(x, y, 1 - c), device_id_type=MESH)

        sends = [over_ici(t, p, me) for t in range(n) for p in range(3)]
        for cp in sends:
            cp.start()
        passed = []
        for t in range(n):
            for p, (px, py) in enumerate(peers):
                over_ici(t, p, 2 * px + py).wait_recv()
                passed.append(to_sibling(t, p, c))
                passed[-1].start()
        for t in range(n):
            for p in range(3):
                to_sibling(t, p, 1 - c).wait_recv()
        for cp in sends + passed:
            cp.wait_send()

    return pl.pallas_call(
        body, out_shape=[SDS((N_CHIPS, 2, a.shape[0] // 2, a.shape[1]), a.dtype) for a in shards],
        in_specs=[HBM_SPEC] * n, out_specs=[HBM_SPEC] * n,
        scratch_shapes=[pltpu.SemaphoreType.DMA((3 * n,))] * 4,
        compiler_params=pltpu.CompilerParams(has_side_effects=True), name="gather_weights")(*shards)


def _with_own(gathered, own, chip):
    full = gathered.reshape((N_CHIPS,) + own.shape)
    return lax.dynamic_update_index_in_dim(full, own, chip, 0)


def _exchange_halves(grads):
    n = len(grads)

    def body(*refs):
        ins, outs = refs[:n], refs[n:2 * n]
        send_sems, recv_sems = refs[2 * n:]
        x, y, c = _position()
        cps = []
        for t in range(n):
            r2 = ins[t].shape[1] // 2
            cps.append(pltpu.make_async_remote_copy(
                src_ref=ins[t].at[:, pl.ds((1 - c) * r2, r2), :], dst_ref=outs[t],
                send_sem=send_sems.at[t], recv_sem=recv_sems.at[t], device_id=(x, y, 1 - c), device_id_type=MESH))
        for cp in cps:
            cp.start()
        for cp in cps:
            cp.wait()

    return pl.pallas_call(
        body, out_shape=[SDS((a.shape[0], a.shape[1] // 2, a.shape[2]), a.dtype) for a in grads],
        in_specs=[HBM_SPEC] * n, out_specs=[HBM_SPEC] * n,
        scratch_shapes=[pltpu.SemaphoreType.DMA((n,)), pltpu.SemaphoreType.DMA((n,))],
        compiler_params=pltpu.CompilerParams(has_side_effects=True), name="reduce_sibling")(*grads)


def _scatter_copies(ins, outs, send_sems, recv_sems):
    x, y, c = _position()
    me = 2 * x + y
    peers = [(1 - x, y), (x, 1 - y), (1 - x, 1 - y)]

    def remote(t, p, src_slab, dst_slab):
        px, py = peers[p]
        return pltpu.make_async_remote_copy(
            src_ref=ins[t].at[src_slab], dst_ref=outs[t].at[dst_slab], send_sem=send_sems.at[3 * t + p],
            recv_sem=recv_sems.at[3 * t + p], device_id=(px, py, c), device_id_type=MESH)

    n = len(ins)
    sends = [remote(t, p, 2 * peers[p][0] + peers[p][1], me) for t in range(n) for p in range(3)]
    lands = [remote(t, p, me, 2 * peers[p][0] + peers[p][1]) for t in range(n) for p in range(3)]
    return sends, lands


def _share_halves(halves):
    n = len(halves)

    def body(*refs):
        ins, outs = refs[:n], refs[n:2 * n]
        send_sems, recv_sems = refs[2 * n:]
        x, y, c = _position()

        def copy(t, slab):
            return pltpu.make_async_remote_copy(
                src_ref=ins[t].at[slab], dst_ref=outs[t].at[slab], send_sem=send_sems.at[t], recv_sem=recv_sems.at[t],
                device_id=(x, y, 1 - c), device_id_type=MESH)

        for t in range(n):
            copy(t, c).start()
        for t in range(n):
            copy(t, 1 - c).wait_recv()
        for t in range(n):
            copy(t, c).wait_send()

    return pl.pallas_call(
        body, out_shape=[SDS(a.shape, a.dtype) for a in halves],
        in_specs=[HBM_SPEC] * n, out_specs=[HBM_SPEC] * n,
        scratch_shapes=[pltpu.SemaphoreType.DMA((n,)), pltpu.SemaphoreType.DMA((n,))],
        input_output_aliases={t: t for t in range(n)},
        compiler_params=pltpu.CompilerParams(has_side_effects=True), name="share_sibling")(*halves)


def _add_sibling(grad, recv, core):
    nch, r2, cols = recv.shape
    tr = _row_block(r2, cols)
    nb = r2 // tr

    def body(c_ref, g_ref, r_ref, o_ref):
        del c_ref
        o_ref[...] = (g_ref[...] + r_ref[...]).astype(BF16)

    return pl.pallas_call(
        body, out_shape=SDS(recv.shape, BF16),
        grid_spec=pltpu.PrefetchScalarGridSpec(
            num_scalar_prefetch=1, grid=(nch, nb),
            in_specs=[pl.BlockSpec((None, tr, cols), lambda j, i, c_ref: (j, c_ref[0] * nb + i, 0)),
                      pl.BlockSpec((None, tr, cols), lambda j, i, c_ref: (j, i, 0))],
            out_specs=pl.BlockSpec((None, tr, cols), lambda j, i, c_ref: (j, i, 0))),
        compiler_params=_params(("parallel", "parallel")), name="add_sibling")(core, grad, recv)


def _add_chips(own, recv, chip_core):
    nch, r2, cols = recv.shape
    tr = _row_block(r2, cols)

    def body(cc_ref, own_ref, *refs):
        p_refs, o_ref = refs[:nch], refs[nch]
        me = cc_ref[0]
        acc = None
        for j in range(nch):
            term = jnp.where(me == j, own_ref[...], p_refs[j][...]).astype(F32)
            acc = term if acc is None else acc + term
        o_ref[...] = acc

    def slab(j):
        return pl.BlockSpec((None, tr, cols), lambda i, cc: (cc[2 + j], i, 0))

    return pl.pallas_call(
        body, out_shape=SDS((2, r2, cols), F32),
        grid_spec=pltpu.PrefetchScalarGridSpec(
            num_scalar_prefetch=1, grid=(r2 // tr,),
            in_specs=[pl.BlockSpec((None, tr, cols), lambda i, cc: (cc[0], i, 0))] + [slab(j) for j in range(nch)],
            out_specs=pl.BlockSpec((None, tr, cols), lambda i, cc: (cc[1], i, 0))),
        compiler_params=_params(("parallel",)), name="add_chips")(chip_core, own, *([recv] * nch))


def _allreduce_small(pack):
    rows = pack.shape[0]

    def body(p_ref, o_ref, buf, send_sems, recv_sems):
        x, y, c = _position()
        me = 4 * x + 2 * y + c
        buf[me] = p_ref[...]

        def copy(dst_dev, slot):
            return pltpu.make_async_remote_copy(
                src_ref=p_ref, dst_ref=buf.at[slot], send_sem=send_sems.at[dst_dev], recv_sem=recv_sems.at[slot],
                device_id=(dst_dev // 4, (dst_dev // 2) % 2, dst_dev % 2), device_id_type=MESH)

        for dev in range(N_DEV):
            @pl.when(dev != me)
            def _():
                copy(dev, me).start()
        for dev in range(N_DEV):
            @pl.when(dev != me)
            def _():
                copy(dev, dev).wait_recv()
        for dev in range(N_DEV):
            @pl.when(dev != me)
            def _():
                copy(dev, me).wait_send()
        acc = buf[0]
        for dev in range(1, N_DEV):
            acc = acc + buf[dev]
        o_ref[...] = acc

    return pl.pallas_call(
        body, out_shape=SDS(pack.shape, F32),
        in_specs=[pl.BlockSpec(memory_space=pltpu.VMEM)], out_specs=pl.BlockSpec(memory_space=pltpu.VMEM),
        scratch_shapes=[pltpu.VMEM((N_DEV, rows, LANES), F32), pltpu.SemaphoreType.DMA((N_DEV,)),
                        pltpu.SemaphoreType.DMA((N_DEV,))],
        compiler_params=pltpu.CompilerParams(has_side_effects=True), name="allreduce_small")(pack)


ATTN_TQ = 256


def _local_step(cfg, x, target, w, to_chips=None):
    d = cfg.D
    win = ATTN_WINDOW
    hn = _rmsnorm_fwd(x, w["norm_w"])
    proj = _mm(hn, w["w_main"], "nn", BF16, "proj_main")
    dt_raw = _mm(hn, w["w_dt"], "nn", F32, "proj_dt")
    slopes = _slopes(cfg.H)
    near = _Pass(ATTN_TQ, DILATED_PATTERNS[:-1], 1, cfg.S)
    far = _Pass(LANES, DILATED_PATTERNS[-1:], DEINT, cfg.S // DEINT)
    tab_near, tab_far = _attn_tables(near), _attn_tables(far)
    cols_near, cols_far = (cfg.OQ, cfg.OK, cfg.OV), (0, d, 2 * d)
    qkv_far = _deinterleave(proj, 0, 3 * d, "attn_deinterleave")
    o_1, lse_1 = _attn_fwd(cfg, near, proj, cols_near, tab_near, slopes, "attn_fwd_near")
    o_2, lse_2 = _attn_fwd(cfg, far, qkv_far, cols_far, tab_far, slopes, "attn_fwd_far")
    o_a, oag, lse = _attn_merge(cfg, proj, o_1, lse_1, o_2, lse_2)
    xact = _conv_fwd(cfg, proj, w["conv_w"], w["conv_b"])
    e = _expansion_matrix(cfg)
    y, y2, states = _ssd_fwd(cfg, xact, dt_raw, proj, w["dt_bias"], w["a_log"], w["d_skip"], w["ssm_norm_w"], e)
    a_br = _mm(oag, w["w_attn"], "nn", BF16, "branch_attn")
    s_br = _mm(y2, w["w_ssm"], "nn", BF16, "branch_ssm")
    merged = _merge_fwd(cfg, proj, a_br, s_br)
    dout_f, dout_b, loss_row, g_fnw = _outproj_loss(merged, w["w_out"], x, target, w["final_norm_w"])

    dmerged = _mm(dout_b, w["w_out"], "nt", BF16, "d_merged")
    g_w_out = _mm(merged, dout_b, "tn", F32, "g_w_out")
    dproj, da_br = _merge_bwd(cfg, proj, a_br, dmerged, cfg.OGA, None, "merge_bwd_attn")
    dproj, ds_br = _merge_bwd(cfg, proj, s_br, dmerged, cfg.OGS, dproj, "merge_bwd_ssm")
    doag = _mm(da_br, w["w_attn"], "nt", BF16, "d_oag")
    g_w_attn = _mm(oag, da_br, "tn", F32, "g_w_attn")
    dy2 = _mm(ds_br, w["w_ssm"], "nt", BF16, "d_y2")
    g_w_ssm = _mm(y2, ds_br, "tn", F32, "g_w_ssm")
    dproj, dxact, ddt, g_snw, g_dtb, g_alog, g_dsk = _ssd_bwd(
        cfg, xact, dt_raw, proj, y, dy2, states, w["dt_bias"], w["a_log"], w["d_skip"], w["ssm_norm_w"], e, dproj)
    dproj, g_cw, g_cb = _conv_bwd(cfg, proj, dxact, w["conv_w"], w["conv_b"], dproj)
    dproj, do, do_far, dl, dl_far, lse_far = _attn_bwd_prep(cfg, proj, o_a, doag, lse, dproj)
    g_near = _attn_bwd(cfg, near, proj, cols_near, do, lse, dl, tab_near, slopes, "attn_bwd_near")
    g_far = _attn_bwd(cfg, far, qkv_far, cols_far, do_far, lse_far, dl_far, tab_far, slopes, "attn_bwd_far")
    for g_1, g_2, col0, nm in zip(g_near, g_far, cols_near, ("attn_dq", "attn_dk", "attn_dv")):
        dproj = _attn_grad_sum(cfg, g_1, g_2, col0, dproj, nm)
    ddt_b = ddt.astype(BF16)
    g_w_main = _mm(hn, dproj, "tn", F32, "g_w_main")
    g_w_dt = _mm(hn, ddt_b, "tn", F32, "g_w_dt")
    grads = dict(w_main=g_w_main, w_dt=g_w_dt, conv_w=g_cw, conv_b=g_cb, dt_bias=g_dtb, a_log=g_alog,
                 d_skip=g_dsk, ssm_norm_w=g_snw, w_attn=g_w_attn, w_ssm=g_w_ssm, w_out=g_w_out, final_norm_w=g_fnw)
    sent = to_chips(grads) if to_chips is not None else ()
    dhn = _mm(dproj, w["w_main"], "nt", F32, "d_hn", init=_mm(ddt_b, w["w_dt"], "nt", F32, "d_hn_dt"), exchange=sent)
    landed = ()
    if sent:
        dhn, landed = dhn
    grad_x, grads["norm_w"] = _rmsnorm_bwd(x, w["norm_w"], dhn, dout_f)
    return loss_row, grad_x, grads, sent, landed


def _pad_lanes(v):
    return jnp.pad(v, ((0, 0), (0, LANES - v.shape[1])))


def _full_weights(cfg, norm_w, w_in, conv_w, conv_b, dt_bias, a_log, d_skip, ssm_norm_w, w_attn, w_ssm, w_out, fnw):
    w_main = jnp.concatenate([w_in[:, :cfg.OGA], w_in[:, cfg.OGA + cfg.NH:]], axis=1).astype(BF16)
    w_dt = _pad_lanes(w_in[:, cfg.OGA:cfg.OGA + cfg.NH]).astype(BF16)
    return dict(norm_w=norm_w, w_main=w_main, w_dt=w_dt, conv_w=conv_w, conv_b=conv_b, dt_bias=_pad_lanes(dt_bias),
                a_log=_pad_lanes(a_log), d_skip=_pad_lanes(d_skip), ssm_norm_w=ssm_norm_w, w_attn=w_attn.astype(BF16),
                w_ssm=w_ssm.astype(BF16), w_out=w_out.astype(BF16), final_norm_w=fnw)


def _grad_w_in(cfg, grads):
    return jnp.concatenate([grads["w_main"][:, :cfg.OGA], grads["w_dt"][:, :cfg.NH], grads["w_main"][:, cfg.OGA:]], axis=1)


def kernel(x, norm_w, w_in, conv_w, conv_b, dt_bias, a_log, d_skip, ssm_norm_w, w_attn_branch, w_ssm_branch, w_out, final_norm_w, loss_target, m_norm_w, m_w_in, m_conv_w, m_conv_b, m_dt_bias, m_a_log, m_d_skip, m_ssm_norm_w, m_w_attn_branch, m_w_ssm_branch, m_w_out, m_final_norm_w, v_norm_w, v_w_in, v_conv_w, v_conv_b, v_dt_bias, v_a_log, v_d_skip, v_ssm_norm_w, v_w_attn_branch, v_w_ssm_branch, v_w_out, v_final_norm_w):
    cfg = _Cfg(x.shape[1], x.shape[2])
    d, si, cd, nh = cfg.D, cfg.SI, cfg.CD, cfg.NH
    chip = 2 * lax.axis_index("x") + lax.axis_index("y")
    core = lax.axis_index("c").astype(jnp.int32).reshape(1)
    slabs = jnp.arange(N_CHIPS, dtype=jnp.int32)
    chip_core = jnp.concatenate([chip.astype(jnp.int32).reshape(1), core,
                                 jnp.where(slabs == chip, (slabs + 1) % N_CHIPS, slabs)])

    own = [w_in[0].astype(BF16), w_attn_branch[0].astype(BF16), w_ssm_branch[0].astype(BF16), w_out[0].astype(BF16),
           conv_w[0].reshape(4 * CONV_K, -1)]
    a_in, a_attn, a_ssm, a_out, a_cw = [_with_own(g, o, chip) for g, o in zip(_gather_chips(own), own)]
    w_in_full = a_in.transpose(1, 0, 2).reshape(d, cfg.N_IN)
    conv_w_full = a_cw.reshape(N_CHIPS, CONV_K, cd // N_CHIPS).transpose(1, 0, 2).reshape(CONV_K, cd)
    w = _full_weights(cfg, norm_w, w_in_full, conv_w_full, conv_b, dt_bias, a_log, d_skip, ssm_norm_w,
                      a_attn.reshape(d, d), a_ssm.reshape(si, d), a_out.reshape(d, d), final_norm_w.reshape(1, d))

    def to_chips(grads):
        by_chip = [_grad_w_in(cfg, grads).reshape(d, N_CHIPS, cfg.N_IN // N_CHIPS).transpose(1, 0, 2),
                   grads["w_attn"].reshape(N_CHIPS, d // N_CHIPS, d),
                   grads["w_ssm"].reshape(N_CHIPS, si // N_CHIPS, d),
                   grads["w_out"].reshape(N_CHIPS, d // N_CHIPS, d)]
        from_sibling = _exchange_halves(by_chip)
        return [_add_sibling(g, r, core) for g, r in zip(by_chip, from_sibling)]

    loss_row, grad_x, grads, chip_sums, from_chips = _local_step(cfg, x[0], loss_target[0], w, to_chips)
    halves = [_add_chips(o, p, chip_core) for o, p in zip(chip_sums, from_chips)]
    g_in, g_attn, g_ssm, g_out = [h.reshape(2 * h.shape[1], h.shape[2]) for h in _share_halves(halves)]

    small = [loss_row, grads["norm_w"], grads["conv_b"], grads["dt_bias"], grads["a_log"], grads["d_skip"],
             grads["ssm_norm_w"], grads["final_norm_w"], grads["conv_w"].reshape(1, CONV_K * cd)]
    sizes = [a.shape[1] for a in small]
    total = sum(sizes)
    rows = -(-total // (8 * LANES)) * 8
    flat = jnp.pad(jnp.concatenate(small, axis=1), ((0, 0), (0, rows * LANES - total)))
    red = _allreduce_small(flat.reshape(rows, LANES)).reshape(1, rows * LANES)
    offs = [sum(sizes[:i]) for i in range(len(sizes))]
    loss_r, g_nw, g_cb, g_dtb, g_alog, g_dsk, g_snw, g_fnw, g_cw_flat = [
        red[:, o:o + n] for o, n in zip(offs, sizes)]
    loss = loss_r[0, 0]
    g_dtb, g_alog, g_dsk = g_dtb[:, :nh], g_alog[:, :nh], g_dsk[:, :nh]
    cshard = cd // N_CHIPS
    g_cw = lax.dynamic_slice_in_dim(g_cw_flat.reshape(CONV_K, cd), chip * cshard, cshard, axis=1)

    upd = {}
    for name, wv, gv, mv, vv in [("w_in", w_in[0], g_in, m_w_in[0], v_w_in[0]),
                                 ("w_attn", w_attn_branch[0], g_attn, m_w_attn_branch[0], v_w_attn_branch[0]),
                                 ("w_ssm", w_ssm_branch[0], g_ssm, m_w_ssm_branch[0], v_w_ssm_branch[0]),
                                 ("w_out", w_out[0], g_out, m_w_out[0], v_w_out[0])]:
        upd[name] = _adamw(wv, gv, mv, vv, "adamw_" + name)
    names = ["norm_w", "conv_w", "conv_b", "dt_bias", "a_log", "d_skip", "ssm_norm_w", "final_norm_w"]
    ws = [norm_w, conv_w[0].reshape(1, -1), conv_b, dt_bias, a_log, d_skip, ssm_norm_w, final_norm_w.reshape(1, d)]
    gs = [g_nw, g_cw.reshape(1, -1), g_cb, g_dtb, g_alog, g_dsk, g_snw, g_fnw]
    ms = [m_norm_w, m_conv_w[0].reshape(1, -1), m_conv_b, m_dt_bias, m_a_log, m_d_skip, m_ssm_norm_w,
          m_final_norm_w.reshape(1, d)]
    vs = [v_norm_w, v_conv_w[0].reshape(1, -1), v_conv_b, v_dt_bias, v_a_log, v_d_skip, v_ssm_norm_w,
          v_final_norm_w.reshape(1, d)]
    ssz = [a.shape[1] for a in ws]
    stot = sum(ssz)
    srows = -(-stot // (8 * LANES)) * 8

    def pack(parts):
        return jnp.pad(jnp.concatenate(parts, axis=1), ((0, 0), (0, srows * LANES - stot))).reshape(srows, LANES)

    packed = _adamw(pack(ws), pack(gs), pack(ms), pack(vs), "adamw_small")
    soffs = [sum(ssz[:i]) for i in range(len(ssz))]
    for k, nm in enumerate(names):
        upd[nm] = tuple(p.reshape(1, srows * LANES)[:, soffs[k]:soffs[k] + ssz[k]] for p in packed)

    shapes = dict(norm_w=norm_w.shape, w_in=w_in.shape, conv_w=conv_w.shape, conv_b=conv_b.shape, dt_bias=dt_bias.shape,
                  a_log=a_log.shape, d_skip=d_skip.shape, ssm_norm_w=ssm_norm_w.shape, w_attn=w_attn_branch.shape,
                  w_ssm=w_ssm_branch.shape, w_out=w_out.shape, final_norm_w=final_norm_w.shape)
    order = ["norm_w", "w_in", "conv_w", "conv_b", "dt_bias", "a_log", "d_skip", "ssm_norm_w", "w_attn", "w_ssm",
             "w_out", "final_norm_w"]
    gradv = dict(norm_w=g_nw, w_in=g_in, conv_w=g_cw, conv_b=g_cb, dt_bias=g_dtb, a_log=g_alog, d_skip=g_dsk,
                 ssm_norm_w=g_snw, w_attn=g_attn, w_ssm=g_ssm, w_out=g_out, final_norm_w=g_fnw)
    outs = [loss, grad_x[None]]
    outs += [gradv[n].reshape(shapes[n]) for n in order]
    for k in range(3):
        outs += [upd[n][k].reshape(shapes[n]) for n in order]
    return tuple(outs)
```

```python
import functools
import math

import jax
import jax.numpy as jnp
from jax import lax
from jax.experimental import pallas as pl
from jax.experimental.pallas import tpu as pltpu

F32 = jnp.float32
BF16 = jnp.bfloat16
SDS = jax.ShapeDtypeStruct

RMS_EPS = 1e-6
LANES = 128
CHUNK = 128
SSM_HEAD_DIM = 64
SSM_GROUPS = 8
SSM_STATE = 128
CONV_K = 4
ATTN_HEAD_DIM = 128
DILATED_PATTERNS = ((128, 1), (512, 4), (2048, 16))
ATTN_WINDOW = max(w for w, _ in DILATED_PATTERNS)
NEG = -1e30
VMEM_LIMIT = 56 * 1024 * 1024
ADAM_LR, ADAM_B1, ADAM_B2, ADAM_EPS, ADAM_WD, ADAM_STEP = 0.001, 0.9, 0.999, 1e-08, 0.01, 10
MESH = pl.DeviceIdType.MESH
N_CHIPS = 4
N_DEV = 8


class _Cfg:
    def __init__(self, s, d):
        self.S, self.D = s, d
        self.H = d // ATTN_HEAD_DIM
        self.SI = 2 * d
        self.NH = self.SI // SSM_HEAD_DIM
        self.HPG = self.NH // SSM_GROUPS
        self.GW = self.HPG * SSM_HEAD_DIM
        self.BC = SSM_GROUPS * SSM_STATE
        self.CD = self.SI + 2 * self.BC
        self.OQ, self.OK, self.OV, self.OZA = 0, d, 2 * d, 3 * d
        self.OZS = 4 * d
        self.OXBC = self.OZS + self.SI
        self.OGA = self.OXBC + self.CD
        self.OGS = self.OGA + d
        self.NM = self.OGS + d
        self.N_IN = self.NM + self.NH
        assert self.GW % LANES == 0 and self.NH <= LANES and s % 512 == 0 and d % 512 == 0


def _params(sem=None):
    return pltpu.CompilerParams(dimension_semantics=sem, vmem_limit_bytes=VMEM_LIMIT)


def _sigmoid(x):
    return 1.0 / (1.0 + jnp.exp(-x))


def _softplus(x):
    u = jnp.exp(-jnp.abs(x))
    l1p = jnp.where(u < 1e-3, u * (1.0 - u * (0.5 - u * (1.0 / 3.0))), jnp.log(1.0 + u))
    return jnp.maximum(x, 0.0) + l1p


def _nt(a, b):
    return lax.dot_general(a, b, (((1,), (1,)), ((), ())), preferred_element_type=F32)


def _tn(a, b):
    return lax.dot_general(a, b, (((0,), (0,)), ((), ())), preferred_element_type=F32)


def _nn(a, b):
    return jnp.dot(a, b, preferred_element_type=F32)


def _tile(n, target):
    if n <= target:
        return n
    best = None
    for t in range(LANES, target + 1, LANES):
        if n % t == 0:
            best = t
    assert best is not None, (n, target)
    return best


MM_TK = {"nn": 2048, "nt": 2048, "tn": 1024}


def _mm(a, b, dims, out_dtype, name, tm=1024, tn=2048, tk=None, init=None, exchange=()):
    tk = MM_TK[dims] if tk is None else tk
    if dims == "nn":
        (m, k), (k2, n) = a.shape, b.shape
    elif dims == "nt":
        (m, k), (n, k2) = a.shape, b.shape
    else:
        (k, m), (k2, n) = a.shape, b.shape
    assert k == k2
    tm, tn, tk = _tile(m, tm), _tile(n, tn), _tile(k, tk)
    nk = k // tk
    if dims == "tn":
        a_spec = pl.BlockSpec((tk, tm), lambda i, j, kk: (kk, i))
    else:
        a_spec = pl.BlockSpec((tm, tk), lambda i, j, kk: (i, kk))
    if dims == "nt":
        b_spec = pl.BlockSpec((tn, tk), lambda i, j, kk: (j, kk))
    else:
        b_spec = pl.BlockSpec((tk, tn), lambda i, j, kk: (kk, j))
    o_spec = pl.BlockSpec((tm, tn), lambda i, j, kk: (i, j))
    op = {"nn": _nn, "nt": _nt, "tn": _tn}[dims]
    has_init = init is not None
    nx = len(exchange)
    ni, nj = m // tm, n // tn

    def body(*refs):
        a_ref, b_ref = refs[0], refs[1]
        i_ref = refs[2] if has_init else None
        x_in = refs[2 + has_init:2 + has_init + nx]
        o_ref = refs[2 + has_init + nx]
        x_out = refs[3 + has_init + nx:3 + has_init + 2 * nx]
        acc = refs[3 + has_init + 2 * nx]
        i, j, kk = pl.program_id(0), pl.program_id(1), pl.program_id(2)

        if nx:
            sends, lands = _scatter_copies(x_in, x_out, *refs[4 + has_init + 2 * nx:])

            @pl.when((i == 0) & (j == 0) & (kk == 0))
            def _():
                for cp in sends:
                    cp.start()

        prod = op(a_ref[...], b_ref[...])
        with_init = (lambda p: p + i_ref[...].astype(F32)) if has_init else (lambda p: p)
        if nk == 1:
            o_ref[...] = with_init(prod).astype(out_dtype)
        else:
            @pl.when(kk == 0)
            def _():
                acc[...] = with_init(prod)

            @pl.when((kk > 0) & (kk < nk - 1))
            def _():
                acc[...] += prod

            @pl.when(kk == nk - 1)
            def _():
                o_ref[...] = (acc[...] + prod).astype(out_dtype)

        if nx:
            @pl.when((i == ni - 1) & (j == nj - 1) & (kk == nk - 1))
            def _():
                for cp in lands:
                    cp.wait_recv()
                for cp in sends:
                    cp.wait_send()

    in_specs = [a_spec, b_spec] + ([o_spec] if has_init else []) + [HBM_SPEC] * nx
    args = (a, b) + ((init,) if has_init else ()) + tuple(exchange)
    sems = [pltpu.SemaphoreType.DMA((3 * nx,))] * 2 if nx else []
    outs = pl.pallas_call(
        body, out_shape=[SDS((m, n), out_dtype)] + [SDS(e.shape, e.dtype) for e in exchange], grid=(ni, nj, nk),
        in_specs=in_specs, out_specs=[o_spec] + [HBM_SPEC] * nx,
        scratch_shapes=[pltpu.VMEM((tm, tn) if nk > 1 else (8, LANES), F32)] + sems,
        compiler_params=_params(("arbitrary",) * 3 if nx else ("parallel", "parallel", "arbitrary")), name=name)(*args)
    return (outs[0], outs[1:]) if nx else outs[0]


def _rmsnorm_fwd(x, w):
    s, d = x.shape
    tr = 256

    def body(x_ref, w_ref, o_ref):
        xv = x_ref[...]
        r = lax.rsqrt(jnp.mean(xv * xv, axis=-1, keepdims=True) + RMS_EPS)
        o_ref[...] = (xv * r * w_ref[...]).astype(BF16)

    return pl.pallas_call(
        body, out_shape=SDS((s, d), BF16), grid=(s // tr,),
        in_specs=[pl.BlockSpec((tr, d), lambda i: (i, 0)), pl.BlockSpec((1, d), lambda i: (0, 0))],
        out_specs=pl.BlockSpec((tr, d), lambda i: (i, 0)),
        compiler_params=_params(("parallel",)), name="rmsnorm_fwd")(x, w)


def _rmsnorm_bwd(x, w, dhn, dout):
    s, d = x.shape
    tr = 256

    def body(x_ref, w_ref, dh_ref, do_ref, gx_ref, gw_ref):
        xv = x_ref[...]
        r = lax.rsqrt(jnp.mean(xv * xv, axis=-1, keepdims=True) + RMS_EPS)
        nrm = xv * r
        dh = dh_ref[...]
        gy = dh * w_ref[...]
        gx_ref[...] = do_ref[...] + r * (gy - nrm * jnp.mean(gy * nrm, axis=-1, keepdims=True))

        @pl.when(pl.program_id(0) == 0)
        def _():
            gw_ref[...] = jnp.zeros_like(gw_ref)

        gw_ref[...] += jnp.sum(dh * nrm, axis=0, keepdims=True)

    blk = pl.BlockSpec((tr, d), lambda i: (i, 0))
    row = pl.BlockSpec((1, d), lambda i: (0, 0))
    return pl.pallas_call(
        body, out_shape=(SDS((s, d), F32), SDS((1, d), F32)), grid=(s // tr,),
        in_specs=[blk, row, blk, blk], out_specs=(blk, row),
        compiler_params=_params(("arbitrary",)), name="rmsnorm_bwd")(x, w, dhn, dout)


DEINT = DILATED_PATTERNS[-1][1]
DEINT_ROWS = DEINT * LANES


class _Pass:
    def __init__(self, tq, patterns, unit, seg_len):
        self.tq, self.patterns, self.unit, self.seg_len = tq, patterns, unit, seg_len
        self.win = max(w for w, _ in patterns) // unit
        self.w = self.win + tq
        assert self.win % tq == 0


def _attn_tables(ps):
    i = jnp.arange(ps.tq, dtype=jnp.int32)[:, None]
    j = jnp.arange(ps.w, dtype=jnp.int32)[None, :]
    delta = (i + ps.win - j) * ps.unit
    n = jnp.zeros((ps.tq, ps.w), F32)
    for window, dil in ps.patterns:
        n = n + ((delta >= 0) & (delta <= window) & (delta % dil == 0)).astype(F32)
    logn = jnp.where(n > 0, jnp.log(jnp.maximum(n, 1.0)), NEG)
    return logn, jnp.maximum(delta, 0).astype(F32)


def _slopes(h):
    s = jnp.asarray([2.0 ** (-8.0 * (i + 1) / h) for i in range(h)], F32)
    return jnp.broadcast_to(s[:, None, None], (h, 1, LANES))


def _masked_logn(ps, logn_ref, start):
    col = lax.broadcasted_iota(jnp.int32, (ps.tq, ps.w), 1)
    return jnp.where(col >= ps.win - lax.rem(start, ps.seg_len), logn_ref[...], NEG)


def _head_cols(hh):
    return slice(hh * ATTN_HEAD_DIM, (hh + 1) * ATTN_HEAD_DIM)


def _head_window(refs, cs):
    return jnp.concatenate([r[:, cs] for r in refs], axis=0)


def _head_scores(q_ref, kw, cs, base, dist_ref, slope_ref, hh):
    return _nt(q_ref[:, cs], kw) * (ATTN_HEAD_DIM ** -0.5) + (base - slope_ref[hh][0:1, 0:1] * dist_ref[...])


def _lane_of(stat, hh):
    lane = lax.broadcasted_iota(jnp.int32, stat.shape, 1)
    return jnp.sum(jnp.where(lane == hh, stat, 0.0), axis=1, keepdims=True)


def _window_specs(ps, d, col, nb):
    nprev = ps.win // ps.tq
    return [pl.BlockSpec((ps.tq, d), lambda i, b=b: (jnp.maximum(jnp.minimum(i, nb - 1) - (nprev - b), 0), col))
            for b in range(nprev + 1)]


def _attn_fwd(cfg, ps, qkv, cols, tables, slopes, name):
    s, h, d = cfg.S, cfg.H, cfg.D
    tq, nw = ps.tq, ps.win // ps.tq + 1
    nb = s // tq
    logn, dist = tables
    qc, kc, vc = [c // d for c in cols]

    def body(*refs):
        q_ref, k_refs, v_refs = refs[0], refs[1:1 + nw], refs[1 + nw:1 + 2 * nw]
        logn_ref, dist_ref, slope_ref, o_ref, lse_ref = refs[1 + 2 * nw:]
        base = _masked_logn(ps, logn_ref, pl.program_id(0) * tq)
        lane = lax.broadcasted_iota(jnp.int32, (tq, LANES), 1)

        lse = jnp.zeros((tq, LANES), F32)
        for hh in range(h):
            cs = _head_cols(hh)
            sc = _head_scores(q_ref, _head_window(k_refs, cs), cs, base, dist_ref, slope_ref, hh)
            m = jnp.max(sc, axis=1, keepdims=True)
            p = jnp.exp(sc - m)
            l = jnp.sum(p, axis=1, keepdims=True)
            o_ref[:, cs] = (_nn(p.astype(BF16), _head_window(v_refs, cs)) / l).astype(BF16)
            lse = jnp.where(lane == hh, m + jnp.log(l), lse)
        lse_ref[...] = lse

    tab = pl.BlockSpec((tq, ps.w), lambda i: (0, 0))
    return pl.pallas_call(
        body, out_shape=(SDS((s, d), BF16), SDS((s, LANES), F32)), grid=(nb,),
        in_specs=[pl.BlockSpec((tq, d), lambda i: (i, qc))] + _window_specs(ps, d, kc, nb) + _window_specs(ps, d, vc, nb)
        + [tab, tab, pl.BlockSpec((h, 1, LANES), lambda i: (0, 0, 0))],
        out_specs=(pl.BlockSpec((tq, d), lambda i: (i, 0)), pl.BlockSpec((tq, LANES), lambda i: (i, 0))),
        compiler_params=_params(("parallel",)), name=name)(*([qkv] * (1 + 2 * nw)), logn, dist, slopes)


def _attn_bwd(cfg, ps, qkv, cols, do, lse, delta, tables, slopes, name):
    s, h, d = cfg.S, cfg.H, cfg.D
    tq, nprev = ps.tq, ps.win // ps.tq
    nw = nprev + 1
    nb = s // tq
    logn, dist = tables
    qc, kc, vc = [c // d for c in cols]
    scale = ATTN_HEAD_DIM ** -0.5

    def body(*refs):
        q_ref, k_refs, v_refs = refs[0], refs[1:1 + nw], refs[1 + nw:1 + 2 * nw]
        do_ref, lse_ref, dl_ref, logn_ref, dist_ref, slope_ref, dq_ref, dk_ref, dv_ref, ck, cv = refs[1 + 2 * nw:]
        i = pl.program_id(0)
        slot = lambda b: lax.rem(i + b, nprev)

        @pl.when(i == 0)
        def _():
            ck[...] = jnp.zeros_like(ck)
            cv[...] = jnp.zeros_like(cv)

        @pl.when(i < nb)
        def _():
            base = _masked_logn(ps, logn_ref, i * tq)
            lse_all, dl_all = lse_ref[...], dl_ref[...]

            for hh in range(h):
                cs = _head_cols(hh)
                kw, vw = _head_window(k_refs, cs), _head_window(v_refs, cs)
                sc = _head_scores(q_ref, kw, cs, base, dist_ref, slope_ref, hh)
                p = jnp.exp(sc - lse_all[:, hh:hh + 1])
                dob = do_ref[:, cs]
                ds = (p * (_nt(dob, vw) - dl_all[:, hh:hh + 1]) * scale).astype(BF16)
                dq_ref[:, cs] = _nn(ds, kw).astype(BF16)
                dkw = _tn(ds, q_ref[:, cs])
                dvw = _tn(p.astype(BF16), dob)
                dk_ref[:, cs] = ck[slot(0), :, cs] + dkw[0:tq]
                dv_ref[:, cs] = cv[slot(0), :, cs] + dvw[0:tq]
                for b in range(1, nprev):
                    ck[slot(b), :, cs] += dkw[b * tq:(b + 1) * tq]
                    cv[slot(b), :, cs] += dvw[b * tq:(b + 1) * tq]
                ck[slot(0), :, cs] = dkw[nprev * tq:]
                cv[slot(0), :, cs] = dvw[nprev * tq:]

        @pl.when(i >= nb)
        def _():
            dk_ref[...] = ck[slot(0)]
            dv_ref[...] = cv[slot(0)]

    here = lambda i: jnp.minimum(i, nb - 1)
    blk = pl.BlockSpec((tq, d), lambda i: (here(i), 0))
    stat = pl.BlockSpec((tq, LANES), lambda i: (here(i), 0))
    late = pl.BlockSpec((tq, d), lambda i: (jnp.maximum(i - nprev, 0), 0))
    tab = pl.BlockSpec((tq, ps.w), lambda i: (0, 0))
    return pl.pallas_call(
        body, out_shape=(SDS((s, d), BF16), SDS((s, d), F32), SDS((s, d), F32)), grid=(nb + nprev,),
        in_specs=[pl.BlockSpec((tq, d), lambda i: (here(i), qc))] + _window_specs(ps, d, kc, nb)
        + _window_specs(ps, d, vc, nb) + [blk, stat, stat, tab, tab, pl.BlockSpec((h, 1, LANES), lambda i: (0, 0, 0))],
        out_specs=(blk, late, late),
        scratch_shapes=[pltpu.VMEM((nprev, tq, d), F32), pltpu.VMEM((nprev, tq, d), F32)],
        compiler_params=_params(("arbitrary",)), name=name)(
            *([qkv] * (1 + 2 * nw)), do, lse, delta, logn, dist, slopes)


def _by_residue(a):
    return a.reshape(DEINT, a.shape[0] // DEINT, a.shape[1])


def _deint_spec(colblock):
    return pl.BlockSpec((DEINT, LANES, LANES), lambda b, j: (0, b, colblock(j)))


def _deint_rows(scr, out_ref, dtype):
    for r in range(DEINT):
        out_ref[r] = scr[pl.ds(r, LANES, stride=DEINT), :].astype(dtype)


def _int_rows(in_ref, scr):
    for r in range(DEINT):
        scr[pl.ds(r, LANES, stride=DEINT), :] = in_ref[r].astype(F32)


def _deinterleave(x, col0, ncols, name):
    s = x.shape[0]
    c0 = col0 // LANES

    def body(x_ref, o_ref, scr):
        scr[...] = x_ref[...].astype(F32)
        _deint_rows(scr, o_ref, x.dtype)

    out = pl.pallas_call(
        body, out_shape=SDS((DEINT, s // DEINT, ncols), x.dtype), grid=(s // DEINT_ROWS, ncols // LANES),
        in_specs=[pl.BlockSpec((DEINT_ROWS, LANES), lambda b, j: (b, c0 + j))],
        out_specs=_deint_spec(lambda j: j),
        scratch_shapes=[pltpu.VMEM((DEINT_ROWS, LANES), F32)],
        compiler_params=_params(("parallel", "parallel")), name=name)(x)
    return out.reshape(s, ncols)


def _attn_merge(cfg, proj, o_1, lse_1, o_2, lse_2):
    s, h = cfg.S, cfg.H
    zb = cfg.OZA // LANES
    rows = DEINT_ROWS

    def body(o1_ref, l1_ref, o2_ref, l2_ref, z_ref, o_ref, og_ref, lse_ref, so, sl):
        hh = pl.program_id(1)
        _int_rows(o2_ref, so)

        @pl.when(hh == 0)
        def _():
            _int_rows(l2_ref, sl)

        l1, l2 = _lane_of(l1_ref[...], hh), _lane_of(sl[...], hh)
        mx = jnp.maximum(l1, l2)
        w1, w2 = jnp.exp(l1 - mx), jnp.exp(l2 - mx)
        den = w1 + w2
        o = (w1 * o1_ref[...].astype(F32) + w2 * so[...]) / den
        z = z_ref[...].astype(F32)
        o_ref[...] = o.astype(BF16)
        og_ref[...] = (o * (z * _sigmoid(z))).astype(BF16)

        @pl.when(hh == 0)
        def _():
            lse_ref[...] = jnp.zeros_like(lse_ref)

        lane = lax.broadcasted_iota(jnp.int32, (rows, LANES), 1)
        lse_ref[...] = jnp.where(lane == hh, mx + jnp.log(den), lse_ref[...])

    blk = pl.BlockSpec((rows, LANES), lambda b, j: (b, j))
    return pl.pallas_call(
        body, out_shape=(SDS((s, cfg.D), BF16), SDS((s, cfg.D), BF16), SDS((s, LANES), F32)),
        grid=(s // rows, h),
        in_specs=[blk, pl.BlockSpec((rows, LANES), lambda b, j: (b, 0)), _deint_spec(lambda j: j),
                  _deint_spec(lambda j: 0), pl.BlockSpec((rows, LANES), lambda b, j: (b, zb + j))],
        out_specs=(blk, blk, pl.BlockSpec((rows, LANES), lambda b, j: (b, 0))),
        scratch_shapes=[pltpu.VMEM((rows, LANES), F32), pltpu.VMEM((rows, LANES), F32)],
        compiler_params=_params(("parallel", "arbitrary")), name="attn_merge")(
            o_1, lse_1, _by_residue(o_2), _by_residue(lse_2), proj)


def _attn_bwd_prep(cfg, proj, o_a, doag, lse, dproj):
    s, h = cfg.S, cfg.H
    zb = cfg.OZA // LANES
    rows = DEINT_ROWS

    def body(o_ref, dg_ref, z_ref, lse_ref, dp_in, dz_ref, do_ref, do2_ref, dl_ref, dl2_ref, lse2_ref, scr):
        del dp_in
        hh = pl.program_id(1)
        z = z_ref[...].astype(F32)
        sg = _sigmoid(z)
        o = o_ref[...].astype(F32)
        dg = dg_ref[...].astype(F32)
        do = dg * (z * sg)
        dz_ref[...] = (dg * o * (sg * (1.0 + z * (1.0 - sg)))).astype(BF16)
        do_ref[...] = do.astype(BF16)
        scr[...] = do
        _deint_rows(scr, do2_ref, BF16)

        @pl.when(hh == 0)
        def _():
            dl_ref[...] = jnp.zeros_like(dl_ref)

        lane = lax.broadcasted_iota(jnp.int32, (rows, LANES), 1)
        dl_ref[...] = jnp.where(lane == hh, jnp.sum(do * o, axis=1, keepdims=True), dl_ref[...])

        @pl.when(hh == h - 1)
        def _():
            scr[...] = dl_ref[...]
            _deint_rows(scr, dl2_ref, F32)
            scr[...] = lse_ref[...]
            _deint_rows(scr, lse2_ref, F32)

    blk = pl.BlockSpec((rows, LANES), lambda b, j: (b, j))
    stat = pl.BlockSpec((rows, LANES), lambda b, j: (b, 0))
    stat2 = _deint_spec(lambda j: 0)
    outs = pl.pallas_call(
        body,
        out_shape=(SDS(dproj.shape, BF16), SDS((s, cfg.D), BF16), SDS((DEINT, s // DEINT, cfg.D), BF16),
                   SDS((s, LANES), F32), SDS((DEINT, s // DEINT, LANES), F32), SDS((DEINT, s // DEINT, LANES), F32)),
        grid=(s // rows, h),
        in_specs=[blk, blk, pl.BlockSpec((rows, LANES), lambda b, j: (b, zb + j)), stat, HBM_SPEC],
        out_specs=(pl.BlockSpec((rows, LANES), lambda b, j: (b, zb + j)), blk, _deint_spec(lambda j: j),
                   stat, stat2, stat2),
        scratch_shapes=[pltpu.VMEM((rows, LANES), F32)],
        input_output_aliases={4: 0},
        compiler_params=_params(("parallel", "arbitrary")), name="attn_bwd_prep")(o_a, doag, proj, lse, dproj)
    dproj, do, do2, dl, dl2, lse2 = outs
    return dproj, do, do2.reshape(s, cfg.D), dl, dl2.reshape(s, LANES), lse2.reshape(s, LANES)


def _attn_grad_sum(cfg, g_1, g_2, col0, dproj, name):
    s, h = cfg.S, cfg.H
    c0 = col0 // LANES
    rows = DEINT_ROWS

    def body(g1_ref, g2_ref, dp_in, o_ref, scr):
        del dp_in
        _int_rows(g2_ref, scr)
        o_ref[...] = (g1_ref[...].astype(F32) + scr[...]).astype(BF16)

    return pl.pallas_call(
        body, out_shape=SDS(dproj.shape, BF16), grid=(s // rows, h),
        in_specs=[pl.BlockSpec((rows, LANES), lambda b, j: (b, j)), _deint_spec(lambda j: j), HBM_SPEC],
        out_specs=pl.BlockSpec((rows, LANES), lambda b, j: (b, c0 + j)),
        scratch_shapes=[pltpu.VMEM((rows, LANES), F32)],
        input_output_aliases={2: 0},
        compiler_params=_params(("parallel", "parallel")), name=name)(g_1, _by_residue(g_2), dproj)


CONV_HALO = 16
CONV_TR = 512
CONV_CW = 512


def _conv_fwd(cfg, proj, conv_w, conv_b):
    s, cd = cfg.S, cfg.CD
    tr, cw, hl = CONV_TR, CONV_CW, CONV_HALO
    cb0 = cfg.OXBC // cw

    def body(x_ref, h_ref, w_ref, b_ref, o_ref, scr):
        i = pl.program_id(0)
        scr[pl.ds(0, hl), :] = jnp.where(i > 0, h_ref[...].astype(F32), 0.0)
        scr[pl.ds(hl, tr), :] = x_ref[...].astype(F32)
        pre = b_ref[...] + jnp.zeros((tr, cw), F32)
        for k in range(CONV_K):
            pre = pre + w_ref[k:k + 1, :] * scr[pl.ds(hl - (CONV_K - 1) + k, tr), :]
        o_ref[...] = (pre * _sigmoid(pre)).astype(BF16)

    return pl.pallas_call(
        body, out_shape=SDS((s, cd), BF16), grid=(s // tr, cd // cw),
        in_specs=[pl.BlockSpec((tr, cw), lambda i, j: (i, cb0 + j)),
                  pl.BlockSpec((hl, cw), lambda i, j: (jnp.maximum(i * (tr // hl) - 1, 0), cb0 + j)),
                  pl.BlockSpec((CONV_K, cw), lambda i, j: (0, j)),
                  pl.BlockSpec((1, cw), lambda i, j: (0, j))],
        out_specs=pl.BlockSpec((tr, cw), lambda i, j: (i, j)),
        scratch_shapes=[pltpu.VMEM((tr + hl, cw), F32)],
        compiler_params=_params(("parallel", "parallel")), name="conv_fwd")(proj, proj, conv_w, conv_b)


def _conv_bwd(cfg, proj, dact, conv_w, conv_b, dproj):
    s, cd = cfg.S, cfg.CD
    tr, cw, hl = CONV_TR, CONV_CW, CONV_HALO
    cb0 = cfg.OXBC // cw
    nr = s // tr
    last_h = s // hl - 1

    def body(x_ref, hp_ref, hn_ref, d_ref, dn_ref, w_ref, b_ref, dp_in, dx_ref, gw_ref, gb_ref, xs, ds):
        del dp_in
        i = pl.program_id(1)
        xs[pl.ds(0, hl), :] = jnp.where(i > 0, hp_ref[...].astype(F32), 0.0)
        xs[pl.ds(hl, tr), :] = x_ref[...].astype(F32)
        xs[pl.ds(hl + tr, hl), :] = hn_ref[...].astype(F32)
        shifted = [xs[pl.ds(hl - (CONV_K - 1) + k, tr + hl), :] for k in range(CONV_K)]
        pre = b_ref[...] + jnp.zeros((tr + hl, cw), F32)
        for k in range(CONV_K):
            pre = pre + w_ref[k:k + 1, :] * shifted[k]
        sg = _sigmoid(pre)
        dsilu = sg * (1.0 + pre * (1.0 - sg))
        ds[pl.ds(0, tr), :] = d_ref[...].astype(F32) * dsilu[0:tr]
        ds[pl.ds(tr, hl), :] = jnp.where(i < nr - 1, dn_ref[...].astype(F32), 0.0) * dsilu[tr:tr + hl]
        dx = jnp.zeros((tr, cw), F32)
        for k in range(CONV_K):
            dx = dx + w_ref[k:k + 1, :] * ds[pl.ds(CONV_K - 1 - k, tr), :]
        dx_ref[...] = dx.astype(BF16)

        @pl.when(i == 0)
        def _():
            gw_ref[...] = jnp.zeros_like(gw_ref)
            gb_ref[...] = jnp.zeros_like(gb_ref)

        dcur = ds[pl.ds(0, tr), :]
        gb_ref[...] += jnp.sum(dcur, axis=0, keepdims=True)
        for k in range(CONV_K):
            gw_ref[k:k + 1, :] += jnp.sum(dcur * shifted[k][0:tr], axis=0, keepdims=True)

    return pl.pallas_call(
        body, out_shape=(SDS(dproj.shape, BF16), SDS((CONV_K, cd), F32), SDS((1, cd), F32)), grid=(cd // cw, nr),
        in_specs=[pl.BlockSpec((tr, cw), lambda j, i: (i, cb0 + j)),
                  pl.BlockSpec((hl, cw), lambda j, i: (jnp.maximum(i * (tr // hl) - 1, 0), cb0 + j)),
                  pl.BlockSpec((hl, cw), lambda j, i: (jnp.minimum((i + 1) * (tr // hl), last_h), cb0 + j)),
                  pl.BlockSpec((tr, cw), lambda j, i: (i, j)),
                  pl.BlockSpec((hl, cw), lambda j, i: (jnp.minimum((i + 1) * (tr // hl), last_h), j)),
                  pl.BlockSpec((CONV_K, cw), lambda j, i: (0, j)),
                  pl.BlockSpec((1, cw), lambda j, i: (0, j)),
                  pl.BlockSpec(memory_space=pl.ANY)],
        out_specs=(pl.BlockSpec((tr, cw), lambda j, i: (i, cb0 + j)),
                   pl.BlockSpec((CONV_K, cw), lambda j, i: (0, j)),
                   pl.BlockSpec((1, cw), lambda j, i: (0, j))),
        scratch_shapes=[pltpu.VMEM((tr + 2 * hl, cw), F32), pltpu.VMEM((tr + hl, cw), F32)],
        input_output_aliases={7: 0},
        compiler_params=_params(("parallel", "arbitrary")), name="conv_bwd")(
            proj, proj, proj, dact, dact, conv_w, conv_b, dproj)


def _expand(v, e, terms):
    out, rem = None, v
    for _ in range(terms):
        hi = rem.astype(BF16)
        t = _nn(hi, e)
        out = t if out is None else out + t
        rem = rem - hi.astype(F32)
    return out


def _segsum(v, e, terms):
    out, rem = None, v
    for _ in range(terms):
        hi = rem.astype(BF16)
        t = _nt(hi, e)
        out = t if out is None else out + t
        rem = rem - hi.astype(F32)
    return out


def _expand_row(row, e, terms):
    return _expand(jnp.broadcast_to(row, (8, LANES)), e, terms)[0:1]


def _segsum_row(row, e, terms):
    return _segsum(jnp.broadcast_to(row, (8, row.shape[1])), e, terms)[0:1]


def _expansion_matrix(cfg):
    hh = jnp.arange(LANES, dtype=jnp.int32)[:, None]
    cc = jnp.arange(cfg.SI, dtype=jnp.int32)[None, :]
    return (cc // SSM_HEAD_DIM == hh).astype(BF16)


def _tri(lower):
    r = lax.broadcasted_iota(jnp.int32, (CHUNK, CHUNK), 0)
    c = lax.broadcasted_iota(jnp.int32, (CHUNK, CHUNK), 1)
    return (c <= r) if lower else (c >= r)


def _ssd_prep(dtr_ref, db_ref, al_ref, e):
    dtr = dtr_ref[...] + db_ref[...]
    dt = _softplus(dtr)
    a = -jnp.exp(al_ref[...])
    acum = jnp.dot(_tri(True).astype(F32), dt * a, precision=lax.Precision.HIGHEST, preferred_element_type=F32)
    return dtr, dt, a, _expand(dt, e, 2), _expand(acum, e, 3)


def _ssd_fwd(cfg, xact, dt_raw, proj, dt_bias, a_log, d_skip, norm_w, e):
    s, si, cd, gw, bc = cfg.S, cfg.SI, cfg.CD, cfg.GW, cfg.BC
    nc = s // CHUNK
    zb = cfg.OZS // si
    tiles = gw // LANES

    def body(xa_ref, dtr_ref, z_ref, db_ref, al_ref, dsk_ref, nw_ref, e_ref, y_ref, y2_ref, st_ref,
             state, ybuf, x_s, xw_s, ae_s, ea_s, lam_s):
        @pl.when(pl.program_id(0) == 0)
        def _():
            state[...] = jnp.zeros_like(state)

        st_ref[...] = state[...]
        ev = e_ref[...]
        _, _, _, dt_e, a_e = _ssd_prep(dtr_ref, db_ref, al_ref, ev)
        xs = xa_ref[:, 0:si].astype(F32)
        x = xs * dt_e
        lam_e = a_e[CHUNK - 1:CHUNK, :]
        x_s[...] = x.astype(BF16)
        xw_s[...] = (x * jnp.exp(lam_e - a_e)).astype(BF16)
        ae_s[...] = a_e
        ea_s[...] = jnp.exp(a_e)
        ybuf[...] = _expand_row(dsk_ref[...], ev, 3) * xs
        lam_s[...] = jnp.broadcast_to(jnp.exp(lam_e), (8, si))
        tril = _tri(True)
        lane = lax.broadcasted_iota(jnp.int32, (CHUNK, LANES), 1)

        def group(g, carry):
            co = pl.multiple_of(g * gw, LANES)
            bg = xa_ref[:, pl.ds(pl.multiple_of(si + g * SSM_STATE, LANES), SSM_STATE)]
            cg = xa_ref[:, pl.ds(pl.multiple_of(si + bc + g * SSM_STATE, LANES), SSM_STATE)]
            cbm = _nt(cg, bg)
            st = state[:, pl.ds(co, gw)]
            yoff = _nn(cg, st.astype(BF16)) * ea_s[:, pl.ds(co, gw)]
            for k in range(tiles):
                tc = pl.multiple_of(co + k * LANES, LANES)
                at = ae_s[:, pl.ds(tc, LANES)]
                att = at.T
                xt = x_s[:, pl.ds(tc, LANES)]
                acc = yoff[:, k * LANES:(k + 1) * LANES]
                for half in range(2):
                    lo = half * SSM_HEAD_DIM
                    seg = at[:, lo:lo + 1] - att[lo:lo + 1, :]
                    dec = jnp.exp(jnp.where(tril, seg, NEG))
                    xh = jnp.where((lane >= lo) & (lane < lo + SSM_HEAD_DIM), xt, jnp.zeros_like(xt))
                    acc = acc + _nn((cbm * dec).astype(BF16), xh)
                ybuf[:, pl.ds(tc, LANES)] += acc
            state[:, pl.ds(co, gw)] = st * lam_s[0:1, pl.ds(co, gw)] + _tn(bg, xw_s[:, pl.ds(co, gw)])
            return carry

        lax.fori_loop(0, SSM_GROUPS, group, 0)
        y = ybuf[...]
        y_ref[...] = y.astype(BF16)
        z = z_ref[...].astype(F32)
        u = y * (z * _sigmoid(z))
        r = lax.rsqrt(jnp.mean(u * u, axis=-1, keepdims=True) + RMS_EPS)
        y2_ref[...] = (u * r * nw_ref[...]).astype(BF16)

    row = lambda n: pl.BlockSpec((1, n), lambda c: (0, 0))
    return pl.pallas_call(
        body,
        out_shape=(SDS((s, si), BF16), SDS((s, si), BF16), SDS((nc, SSM_STATE, si), F32)),
        grid=(nc,),
        in_specs=[pl.BlockSpec((CHUNK, cd), lambda c: (c, 0)),
                  pl.BlockSpec((CHUNK, LANES), lambda c: (c, 0)),
                  pl.BlockSpec((CHUNK, si), lambda c: (c, zb)),
                  row(LANES), row(LANES), row(LANES), row(si),
                  pl.BlockSpec((LANES, si), lambda c: (0, 0))],
        out_specs=(pl.BlockSpec((CHUNK, si), lambda c: (c, 0)),
                   pl.BlockSpec((CHUNK, si), lambda c: (c, 0)),
                   pl.BlockSpec((None, SSM_STATE, si), lambda c: (c, 0, 0))),
        scratch_shapes=[pltpu.VMEM((SSM_STATE, si), F32), pltpu.VMEM((CHUNK, si), F32),
                        pltpu.VMEM((CHUNK, si), BF16), pltpu.VMEM((CHUNK, si), BF16),
                        pltpu.VMEM((CHUNK, si), F32), pltpu.VMEM((CHUNK, si), F32),
                        pltpu.VMEM((8, si), F32)],
        compiler_params=_params(("arbitrary",)), name="ssd_fwd")(
            xact, dt_raw, proj, dt_bias, a_log, d_skip, norm_w, e)


def _ssd_bwd(cfg, xact, dt_raw, proj, y, dy2, states, dt_bias, a_log, d_skip, norm_w, e, dproj):
    s, si, cd, gw, bc, hpg = cfg.S, cfg.SI, cfg.CD, cfg.GW, cfg.BC, cfg.HPG
    nc = s // CHUNK
    zb = cfg.OZS // si
    tiles = gw // LANES

    def body(xa_ref, dtr_ref, z_ref, y_ref, d2_ref, st_ref, db_ref, al_ref, dsk_ref, nw_ref, e_ref, dp_in,
             dz_ref, dxa_ref, ddt_ref, gnw_ref, gdb_ref, gal_ref, gds_ref,
             dh, dhn, xs_s, x_s, w_s, ae_s, ea_s, g_s, dx_s, dae_s, r_s, lam_s, dle_s):
        del dp_in

        @pl.when(pl.program_id(0) == 0)
        def _():
            dh[...] = jnp.zeros_like(dh)
            gnw_ref[...] = jnp.zeros_like(gnw_ref)
            gdb_ref[...] = jnp.zeros_like(gdb_ref)
            gal_ref[...] = jnp.zeros_like(gal_ref)
            gds_ref[...] = jnp.zeros_like(gds_ref)

        ev = e_ref[...]
        yv = y_ref[...].astype(F32)
        z = z_ref[...].astype(F32)
        sg = _sigmoid(z)
        sz = z * sg
        u = yv * sz
        r = lax.rsqrt(jnp.mean(u * u, axis=-1, keepdims=True) + RMS_EPS)
        nrm = u * r
        d2 = d2_ref[...].astype(F32)
        gnw_ref[...] += jnp.sum(d2 * nrm, axis=0, keepdims=True)
        gn = d2 * nw_ref[...]
        du = r * (gn - nrm * jnp.mean(gn * nrm, axis=-1, keepdims=True))
        gv = du * sz
        dz_ref[...] = (du * yv * (sg * (1.0 + z * (1.0 - sg)))).astype(BF16)
        g_s[...] = gv

        dtr, dt, a, dt_e, a_e = _ssd_prep(dtr_ref, db_ref, al_ref, ev)
        xs = xa_ref[:, 0:si].astype(F32)
        x = xs * dt_e
        lam_e = a_e[CHUNK - 1:CHUNK, :]
        xs_s[...] = xs
        x_s[...] = x
        w_s[...] = jnp.exp(lam_e - a_e)
        ae_s[...] = a_e
        ea_s[...] = jnp.exp(a_e)
        lam_s[...] = jnp.broadcast_to(jnp.exp(lam_e), (8, si))
        gds_ref[...] += _segsum_row(jnp.sum(gv * xs, axis=0, keepdims=True), ev, 2)
        r_s[...] = jnp.zeros_like(r_s)
        tril = _tri(True)
        lane = lax.broadcasted_iota(jnp.int32, (CHUNK, LANES), 1)
        sub = lax.broadcasted_iota(jnp.int32, (CHUNK, LANES), 0)

        def group(g, carry):
            co = pl.multiple_of(g * gw, LANES)
            bo = pl.multiple_of(si + g * SSM_STATE, LANES)
            cof = pl.multiple_of(si + bc + g * SSM_STATE, LANES)
            cols = pl.ds(co, gw)
            bg = xa_ref[:, pl.ds(bo, SSM_STATE)]
            cg = xa_ref[:, pl.ds(cof, SSM_STATE)]
            cbm = _nt(cg, bg)
            st = st_ref[:, cols]
            stb = st.astype(BF16)
            dho = dh[:, cols]
            dhob = dho.astype(BF16)
            ea = ea_s[:, cols]
            gg = g_s[:, cols]
            xg = x_s[:, cols]
            wg = w_s[:, cols]
            explam = lam_s[0:1, cols]
            yoff = _nn(cg, stb) * ea
            ga = (gg * ea).astype(BF16)
            dc = _nt(ga, stb)
            dhn[:, cols] = dho * explam + _tn(cg, ga)
            bdh = _nn(bg, dhob)
            db = _nt((xg * wg).astype(BF16), dhob)
            t = xg * bdh * wg
            dle_s[0:1, cols] = jnp.sum(t, axis=0, keepdims=True) + explam * jnp.sum(dho * st, axis=0, keepdims=True)
            dae_base = gg * yoff - t
            dxw = wg * bdh
            dcb = jnp.zeros((CHUNK, CHUNK), F32)
            for k in range(tiles):
                tc = pl.multiple_of(co + k * LANES, LANES)
                ksl = slice(k * LANES, (k + 1) * LANES)
                at = ae_s[:, pl.ds(tc, LANES)]
                att = at.T
                xt = xg[:, ksl].astype(BF16)
                gt = gg[:, ksl].astype(BF16)
                dxt = dxw[:, ksl]
                place = jnp.zeros((CHUNK, LANES), F32)
                for half in range(2):
                    lo = half * SSM_HEAD_DIM
                    seg = at[:, lo:lo + 1] - att[lo:lo + 1, :]
                    dec = jnp.exp(jnp.where(tril, seg, NEG))
                    mh = cbm * dec
                    gh = jnp.where((lane >= lo) & (lane < lo + SSM_HEAD_DIM), gt, jnp.zeros_like(gt))
                    dm = _nt(gh, xt)
                    dxt = dxt + _tn(mh.astype(BF16), gh)
                    dcb = dcb + dm * dec
                    dseg = dm * mh
                    place = place + jnp.where(lane == lo, jnp.sum(dseg, axis=1, keepdims=True), 0.0)
                    hidx = g * hpg + 2 * k + half
                    r_s[...] += jnp.where(sub == hidx, jnp.sum(dseg, axis=0, keepdims=True), 0.0)
                dx_s[:, pl.ds(tc, LANES)] = dxt
                dae_s[:, pl.ds(tc, LANES)] = dae_base[:, ksl] + place
            dcbb = dcb.astype(BF16)
            dxa_ref[:, pl.ds(bo, SSM_STATE)] = (db + _tn(dcbb, cg)).astype(BF16)
            dxa_ref[:, pl.ds(cof, SSM_STATE)] = (dc + _nn(dcbb, bg)).astype(BF16)
            return carry

        lax.fori_loop(0, SSM_GROUPS, group, 0)
        dlam = _segsum_row(dle_s[0:1, :], ev, 2)
        da_ = _segsum(dae_s[...], ev, 2) - r_s[...].T
        da_ = da_ + jnp.where(sub == CHUNK - 1, dlam, 0.0)
        dda = jnp.dot(_tri(False).astype(F32), da_, precision=lax.Precision.HIGHEST, preferred_element_type=F32)
        dxv = dx_s[...]
        xs = xs_s[...]
        ddt = dda * a + _segsum(dxv * xs, ev, 2)
        gal_ref[...] += jnp.sum(dda * dt, axis=0, keepdims=True) * a
        ddtr = ddt * _sigmoid(dtr)
        gdb_ref[...] += jnp.sum(ddtr, axis=0, keepdims=True)
        ddt_ref[...] = ddtr
        dxa_ref[:, 0:si] = (dxv * dt_e + g_s[...] * _expand_row(dsk_ref[...], ev, 3)).astype(BF16)
        dh[...] = dhn[...]

    rev = lambda c: nc - 1 - c
    row = lambda n: pl.BlockSpec((1, n), lambda c: (0, 0))
    big = lambda: pltpu.VMEM((CHUNK, si), F32)
    return pl.pallas_call(
        body,
        out_shape=(SDS(dproj.shape, BF16), SDS((s, cd), BF16), SDS((s, LANES), F32),
                   SDS((1, si), F32), SDS((1, LANES), F32), SDS((1, LANES), F32), SDS((1, LANES), F32)),
        grid=(nc,),
        in_specs=[pl.BlockSpec((CHUNK, cd), lambda c: (rev(c), 0)),
                  pl.BlockSpec((CHUNK, LANES), lambda c: (rev(c), 0)),
                  pl.BlockSpec((CHUNK, si), lambda c: (rev(c), zb)),
                  pl.BlockSpec((CHUNK, si), lambda c: (rev(c), 0)),
                  pl.BlockSpec((CHUNK, si), lambda c: (rev(c), 0)),
                  pl.BlockSpec((None, SSM_STATE, si), lambda c: (rev(c), 0, 0)),
                  row(LANES), row(LANES), row(LANES), row(si),
                  pl.BlockSpec((LANES, si), lambda c: (0, 0)),
                  pl.BlockSpec(memory_space=pl.ANY)],
        out_specs=(pl.BlockSpec((CHUNK, si), lambda c: (rev(c), zb)),
                   pl.BlockSpec((CHUNK, cd), lambda c: (rev(c), 0)),
                   pl.BlockSpec((CHUNK, LANES), lambda c: (rev(c), 0)),
                   row(si), row(LANES), row(LANES), row(LANES)),
        scratch_shapes=[pltpu.VMEM((SSM_STATE, si), F32), pltpu.VMEM((SSM_STATE, si), F32),
                        big(), big(), big(), big(), big(), big(), big(), big(),
                        pltpu.VMEM((CHUNK, LANES), F32), pltpu.VMEM((8, si), F32), pltpu.VMEM((8, si), F32)],
        input_output_aliases={11: 0},
        compiler_params=_params(("arbitrary",)), name="ssd_bwd")(
            xact, dt_raw, proj, y, dy2, states, dt_bias, a_log, d_skip, norm_w, e, dproj)


MERGE_TR = 512
MERGE_CW = 512


def _merge_fwd(cfg, proj, a_br, s_br):
    s, d = cfg.S, cfg.D
    tr, cw = MERGE_TR, MERGE_CW
    ga0, gs0 = cfg.OGA // cw, cfg.OGS // cw

    def body(ga_ref, gs_ref, a_ref, s_ref, o_ref):
        o_ref[...] = (_sigmoid(ga_ref[...].astype(F32)) * a_ref[...].astype(F32)
                      + _sigmoid(gs_ref[...].astype(F32)) * s_ref[...].astype(F32)).astype(BF16)

    blk = pl.BlockSpec((tr, cw), lambda i, j: (i, j))
    return pl.pallas_call(
        body, out_shape=SDS((s, d), BF16), grid=(s // tr, d // cw),
        in_specs=[pl.BlockSpec((tr, cw), lambda i, j: (i, ga0 + j)),
                  pl.BlockSpec((tr, cw), lambda i, j: (i, gs0 + j)), blk, blk],
        out_specs=blk, compiler_params=_params(("parallel", "parallel")), name="merge_fwd")(proj, proj, a_br, s_br)


def _merge_bwd(cfg, proj, branch, dmerged, gate_off, dproj, name):
    s, d = cfg.S, cfg.D
    tr, cw = MERGE_TR, MERGE_CW
    g0 = gate_off // cw
    fresh = dproj is None

    def body(*refs):
        g_ref, b_ref, dm_ref = refs[:3]
        dg_ref, db_ref = refs[-2:]
        dm = dm_ref[...].astype(F32)
        sg = _sigmoid(g_ref[...].astype(F32))
        db_ref[...] = (dm * sg).astype(BF16)
        dg_ref[...] = (dm * b_ref[...].astype(F32) * sg * (1.0 - sg)).astype(BF16)

    blk = pl.BlockSpec((tr, cw), lambda i, j: (i, j))
    gate = pl.BlockSpec((tr, cw), lambda i, j: (i, g0 + j))
    return pl.pallas_call(
        body, out_shape=(SDS((s, cfg.NM), BF16), SDS((s, d), BF16)), grid=(s // tr, d // cw),
        in_specs=[gate, blk, blk] + ([] if fresh else [HBM_SPEC]),
        out_specs=(gate, blk),
        input_output_aliases={} if fresh else {3: 0},
        compiler_params=_params(("parallel", "parallel")), name=name)(
            *((proj, branch, dmerged) + (() if fresh else (dproj,))))


def _outproj_loss(merged, w_out, x, target, fnw):
    s, d = x.shape
    tr = 256

    def body(m_ref, w_ref, x_ref, t_ref, fw_ref, dof_ref, dob_ref, loss_ref, g_ref):
        out = x_ref[...] + _nn(m_ref[...], w_ref[...])
        r = lax.rsqrt(jnp.mean(out * out, axis=-1, keepdims=True) + RMS_EPS)
        nrm = out * r
        fw = fw_ref[...]
        err = nrm * fw - t_ref[...]
        dy = err * (1.0 / d)
        gy = dy * fw
        dout = r * (gy - nrm * jnp.mean(gy * nrm, axis=-1, keepdims=True))
        dof_ref[...] = dout
        dob_ref[...] = dout.astype(BF16)

        @pl.when(pl.program_id(0) == 0)
        def _():
            loss_ref[...] = jnp.zeros_like(loss_ref)
            g_ref[...] = jnp.zeros_like(g_ref)

        loss_ref[...] += jnp.sum(jnp.sum(err * err, axis=1, keepdims=True), axis=0, keepdims=True) * (0.5 / d)
        g_ref[...] += jnp.sum(dy * nrm, axis=0, keepdims=True)

    blk = pl.BlockSpec((tr, d), lambda i: (i, 0))
    return pl.pallas_call(
        body, out_shape=(SDS((s, d), F32), SDS((s, d), BF16), SDS((1, LANES), F32), SDS((1, d), F32)), grid=(s // tr,),
        in_specs=[blk, pl.BlockSpec((d, d), lambda i: (0, 0)), blk, blk, pl.BlockSpec((1, d), lambda i: (0, 0))],
        out_specs=(blk, blk, pl.BlockSpec((1, LANES), lambda i: (0, 0)), pl.BlockSpec((1, d), lambda i: (0, 0))),
        compiler_params=_params(("arbitrary",)), name="outproj_loss")(merged, w_out, x, target, fnw)


ELEMWISE_BLOCK_BYTES = 1 << 20


def _row_block(rows, cols, itemsize=4):
    best = None
    for tr in range(16, rows + 1, 16):
        if rows % tr == 0 and tr * cols * itemsize <= ELEMWISE_BLOCK_BYTES:
            best = tr
    return best if best is not None else rows


def _adamw(w, g, m, v, name):
    rows, cols = w.shape
    tr = _row_block(rows, cols)

    def body(w_ref, g_ref, m_ref, v_ref, d_ref, nm_ref, nv_ref):
        gv = g_ref[...]
        nm = ADAM_B1 * m_ref[...] + (1.0 - ADAM_B1) * gv
        nv = ADAM_B2 * v_ref[...] + (1.0 - ADAM_B2) * jnp.square(gv)
        m_hat = nm / (1.0 - ADAM_B1 ** ADAM_STEP)
        v_hat = nv / (1.0 - ADAM_B2 ** ADAM_STEP)
        d_ref[...] = -ADAM_LR * (m_hat / (jnp.sqrt(v_hat) + ADAM_EPS) + ADAM_WD * w_ref[...])
        nm_ref[...] = nm
        nv_ref[...] = nv

    blk = pl.BlockSpec((tr, cols), lambda i: (i, 0))
    out = SDS((rows, cols), F32)
    return pl.pallas_call(
        body, out_shape=(out, out, out), grid=(rows // tr,), in_specs=[blk] * 4, out_specs=(blk,) * 3,
        compiler_params=_params(("parallel",)), name=name)(w, g, m, v)


HBM_SPEC = pl.BlockSpec(memory_space=pl.ANY)


def _position():
    return lax.axis_index("x"), lax.axis_index("y"), lax.axis_index("c")


def _gather_chips(shards):
    n = len(shards)

    def body(*refs):
        ins, outs = refs[:n], refs[n:2 * n]
        send_sems, recv_sems, fsend_sems, frecv_sems = refs[2 * n:]
        x, y, c = _position()
        me = 2 * x + y
        peers = [(1 - x, y), (x, 1 - y), (1 - x, 1 - y)]

        def over_ici(t, p, chip):
            px, py = peers[p]
            r2 = ins[t].shape[0] // 2
            return pltpu.make_async_remote_copy(
                src_ref=ins[t].at[pl.ds(c * r2, r2), :], dst_ref=outs[t].at[chip, c], send_sem=send_sems.at[3 * t + p],
                recv_sem=recv_sems.at[3 * t + p], device_id=(px, py, c), device_id_type=MESH)

        def to_sibling(t, p, half):
            px, py = peers[p]
            slab = outs[t].at[2 * px + py, half]
            return pltpu.make_async_remote_copy(
                src_ref=slab, dst_ref=slab, send_sem=fsend_sems.at[3 * t + p], recv_sem=frecv_sems.at[3 * t + p],
                device_id=(x, y, 1 - c), device_id_type=MESH)

        sends = [over_ici(t, p, me) for t in range(n) for p in range(3)]
        for cp in sends:
            cp.start()
        passed = []
        for t in range(n):
            for p, (px, py) in enumerate(peers):
                over_ici(t, p, 2 * px + py).wait_recv()
                passed.append(to_sibling(t, p, c))
                passed[-1].start()
        for t in range(n):
            for p in range(3):
                to_sibling(t, p, 1 - c).wait_recv()
        for cp in sends + passed:
            cp.wait_send()

    return pl.pallas_call(
        body, out_shape=[SDS((N_CHIPS, 2, a.shape[0] // 2, a.shape[1]), a.dtype) for a in shards],
        in_specs=[HBM_SPEC] * n, out_specs=[HBM_SPEC] * n,
        scratch_shapes=[pltpu.SemaphoreType.DMA((3 * n,))] * 4,
        compiler_params=pltpu.CompilerParams(has_side_effects=True), name="gather_weights")(*shards)


def _with_own(gathered, own, chip):
    full = gathered.reshape((N_CHIPS,) + own.shape)
    return lax.dynamic_update_index_in_dim(full, own, chip, 0)


def _exchange_halves(grads):
    n = len(grads)

    def body(*refs):
        ins, outs = refs[:n], refs[n:2 * n]
        send_sems, recv_sems = refs[2 * n:]
        x, y, c = _position()
        cps = []
        for t in range(n):
            r2 = ins[t].shape[1] // 2
            cps.append(pltpu.make_async_remote_copy(
                src_ref=ins[t].at[:, pl.ds((1 - c) * r2, r2), :], dst_ref=outs[t],
                send_sem=send_sems.at[t], recv_sem=recv_sems.at[t], device_id=(x, y, 1 - c), device_id_type=MESH))
        for cp in cps:
            cp.start()
        for cp in cps:
            cp.wait()

    return pl.pallas_call(
        body, out_shape=[SDS((a.shape[0], a.shape[1] // 2, a.shape[2]), a.dtype) for a in grads],
        in_specs=[HBM_SPEC] * n, out_specs=[HBM_SPEC] * n,
        scratch_shapes=[pltpu.SemaphoreType.DMA((n,)), pltpu.SemaphoreType.DMA((n,))],
        compiler_params=pltpu.CompilerParams(has_side_effects=True), name="reduce_sibling")(*grads)


def _scatter_copies(ins, outs, send_sems, recv_sems):
    x, y, c = _position()
    me = 2 * x + y
    peers = [(1 - x, y), (x, 1 - y), (1 - x, 1 - y)]

    def remote(t, p, src_slab, dst_slab):
        px, py = peers[p]
        return pltpu.make_async_remote_copy(
            src_ref=ins[t].at[src_slab], dst_ref=outs[t].at[dst_slab], send_sem=send_sems.at[3 * t + p],
            recv_sem=recv_sems.at[3 * t + p], device_id=(px, py, c), device_id_type=MESH)

    n = len(ins)
    sends = [remote(t, p, 2 * peers[p][0] + peers[p][1], me) for t in range(n) for p in range(3)]
    lands = [remote(t, p, me, 2 * peers[p][0] + peers[p][1]) for t in range(n) for p in range(3)]
    return sends, lands


def _share_halves(halves):
    n = len(halves)

    def body(*refs):
        ins, outs = refs[:n], refs[n:2 * n]
        send_sems, recv_sems = refs[2 * n:]
        x, y, c = _position()

        def copy(t, slab):
            return pltpu.make_async_remote_copy(
                src_ref=ins[t].at[slab], dst_ref=outs[t].at[slab], send_sem=send_sems.at[t], recv_sem=recv_sems.at[t],
                device_id=(x, y, 1 - c), device_id_type=MESH)

        for t in range(n):
            copy(t, c).start()
        for t in range(n):
            copy(t, 1 - c).wait_recv()
        for t in range(n):
            copy(t, c).wait_send()

    return pl.pallas_call(
        body, out_shape=[SDS(a.shape, a.dtype) for a in halves],
        in_specs=[HBM_SPEC] * n, out_specs=[HBM_SPEC] * n,
        scratch_shapes=[pltpu.SemaphoreType.DMA((n,)), pltpu.SemaphoreType.DMA((n,))],
        input_output_aliases={t: t for t in range(n)},
        compiler_params=pltpu.CompilerParams(has_side_effects=True), name="share_sibling")(*halves)


def _add_sibling(grad, recv, core):
    nch, r2, cols = recv.shape
    tr = _row_block(r2, cols)
    nb = r2 // tr

    def body(c_ref, g_ref, r_ref, o_ref):
        del c_ref
        o_ref[...] = (g_ref[...].astype(F32) + r_ref[...].astype(F32)).astype(BF16)

    return pl.pallas_call(
        body, out_shape=SDS(recv.shape, BF16),
        grid_spec=pltpu.PrefetchScalarGridSpec(
            num_scalar_prefetch=1, grid=(nch, nb),
            in_specs=[pl.BlockSpec((None, tr, cols), lambda j, i, c_ref: (j, c_ref[0] * nb + i, 0)),
                      pl.BlockSpec((None, tr, cols), lambda j, i, c_ref: (j, i, 0))],
            out_specs=pl.BlockSpec((None, tr, cols), lambda j, i, c_ref: (j, i, 0))),
        compiler_params=_params(("parallel", "parallel")), name="add_sibling")(core, grad, recv)


def _add_chips(own, recv, chip_core):
    nch, r2, cols = recv.shape
    tr = _row_block(r2, cols)

    def body(cc_ref, own_ref, *refs):
        p_refs, o_ref = refs[:nch], refs[nch]
        me = cc_ref[0]
        acc = None
        for j in range(nch):
            term = jnp.where(me == j, own_ref[...], p_refs[j][...]).astype(F32)
            acc = term if acc is None else acc + term
        o_ref[...] = acc

    def slab(j):
        return pl.BlockSpec((None, tr, cols), lambda i, cc: (cc[2 + j], i, 0))

    return pl.pallas_call(
        body, out_shape=SDS((2, r2, cols), F32),
        grid_spec=pltpu.PrefetchScalarGridSpec(
            num_scalar_prefetch=1, grid=(r2 // tr,),
            in_specs=[pl.BlockSpec((None, tr, cols), lambda i, cc: (cc[0], i, 0))] + [slab(j) for j in range(nch)],
            out_specs=pl.BlockSpec((None, tr, cols), lambda i, cc: (cc[1], i, 0))),
        compiler_params=_params(("parallel",)), name="add_chips")(chip_core, own, *([recv] * nch))


def _allreduce_small(pack):
    rows = pack.shape[0]

    def body(p_ref, o_ref, buf, send_sems, recv_sems):
        x, y, c = _position()
        me = 4 * x + 2 * y + c
        buf[me] = p_ref[...]

        def copy(dst_dev, slot):
            return pltpu.make_async_remote_copy(
                src_ref=p_ref, dst_ref=buf.at[slot], send_sem=send_sems.at[dst_dev], recv_sem=recv_sems.at[slot],
                device_id=(dst_dev // 4, (dst_dev // 2) % 2, dst_dev % 2), device_id_type=MESH)

        for dev in range(N_DEV):
            @pl.when(dev != me)
            def _():
                copy(dev, me).start()
        for dev in range(N_DEV):
            @pl.when(dev != me)
            def _():
                copy(dev, dev).wait_recv()
        for dev in range(N_DEV):
            @pl.when(dev != me)
            def _():
                copy(dev, me).wait_send()
        acc = buf[0]
        for dev in range(1, N_DEV):
            acc = acc + buf[dev]
        o_ref[...] = acc

    return pl.pallas_call(
        body, out_shape=SDS(pack.shape, F32),
        in_specs=[pl.BlockSpec(memory_space=pltpu.VMEM)], out_specs=pl.BlockSpec(memory_space=pltpu.VMEM),
        scratch_shapes=[pltpu.VMEM((N_DEV, rows, LANES), F32), pltpu.SemaphoreType.DMA((N_DEV,)),
                        pltpu.SemaphoreType.DMA((N_DEV,))],
        compiler_params=pltpu.CompilerParams(has_side_effects=True), name="allreduce_small")(pack)


ATTN_TQ = 256


def _local_step(cfg, x, target, w, to_chips=None):
    d = cfg.D
    win = ATTN_WINDOW
    hn = _rmsnorm_fwd(x, w["norm_w"])
    proj = _mm(hn, w["w_main"], "nn", BF16, "proj_main")
    dt_raw = _mm(hn, w["w_dt"], "nn", F32, "proj_dt")
    slopes = _slopes(cfg.H)
    near = _Pass(ATTN_TQ, DILATED_PATTERNS[:-1], 1, cfg.S)
    far = _Pass(LANES, DILATED_PATTERNS[-1:], DEINT, cfg.S // DEINT)
    tab_near, tab_far = _attn_tables(near), _attn_tables(far)
    cols_near, cols_far = (cfg.OQ, cfg.OK, cfg.OV), (0, d, 2 * d)
    qkv_far = _deinterleave(proj, 0, 3 * d, "attn_deinterleave")
    o_1, lse_1 = _attn_fwd(cfg, near, proj, cols_near, tab_near, slopes, "attn_fwd_near")
    o_2, lse_2 = _attn_fwd(cfg, far, qkv_far, cols_far, tab_far, slopes, "attn_fwd_far")
    o_a, oag, lse = _attn_merge(cfg, proj, o_1, lse_1, o_2, lse_2)
    xact = _conv_fwd(cfg, proj, w["conv_w"], w["conv_b"])
    e = _expansion_matrix(cfg)
    y, y2, states = _ssd_fwd(cfg, xact, dt_raw, proj, w["dt_bias"], w["a_log"], w["d_skip"], w["ssm_norm_w"], e)
    a_br = _mm(oag, w["w_attn"], "nn", BF16, "branch_attn")
    s_br = _mm(y2, w["w_ssm"], "nn", BF16, "branch_ssm")
    merged = _merge_fwd(cfg, proj, a_br, s_br)
    dout_f, dout_b, loss_row, g_fnw = _outproj_loss(merged, w["w_out"], x, target, w["final_norm_w"])

    dmerged = _mm(dout_b, w["w_out"], "nt", BF16, "d_merged")
    g_w_out = _mm(merged, dout_b, "tn", BF16, "g_w_out")
    dproj, da_br = _merge_bwd(cfg, proj, a_br, dmerged, cfg.OGA, None, "merge_bwd_attn")
    dproj, ds_br = _merge_bwd(cfg, proj, s_br, dmerged, cfg.OGS, dproj, "merge_bwd_ssm")
    doag = _mm(da_br, w["w_attn"], "nt", BF16, "d_oag")
    g_w_attn = _mm(oag, da_br, "tn", BF16, "g_w_attn")
    dy2 = _mm(ds_br, w["w_ssm"], "nt", BF16, "d_y2")
    g_w_ssm = _mm(y2, ds_br, "tn", BF16, "g_w_ssm")
    dproj, dxact, ddt, g_snw, g_dtb, g_alog, g_dsk = _ssd_bwd(
        cfg, xact, dt_raw, proj, y, dy2, states, w["dt_bias"], w["a_log"], w["d_skip"], w["ssm_norm_w"], e, dproj)
    dproj, g_cw, g_cb = _conv_bwd(cfg, proj, dxact, w["conv_w"], w["conv_b"], dproj)
    dproj, do, do_far, dl, dl_far, lse_far = _attn_bwd_prep(cfg, proj, o_a, doag, lse, dproj)
    g_near = _attn_bwd(cfg, near, proj, cols_near, do, lse, dl, tab_near, slopes, "attn_bwd_near")
    g_far = _attn_bwd(cfg, far, qkv_far, cols_far, do_far, lse_far, dl_far, tab_far, slopes, "attn_bwd_far")
    for g_1, g_2, col0, nm in zip(g_near, g_far, cols_near, ("attn_dq", "attn_dk", "attn_dv")):
        dproj = _attn_grad_sum(cfg, g_1, g_2, col0, dproj, nm)
    ddt_b = ddt.astype(BF16)
    g_w_main = _mm(hn, dproj, "tn", BF16, "g_w_main")
    g_w_dt = _mm(hn, ddt_b, "tn", BF16, "g_w_dt")
    grads = dict(w_main=g_w_main, w_dt=g_w_dt, conv_w=g_cw, conv_b=g_cb, dt_bias=g_dtb, a_log=g_alog,
                 d_skip=g_dsk, ssm_norm_w=g_snw, w_attn=g_w_attn, w_ssm=g_w_ssm, w_out=g_w_out, final_norm_w=g_fnw)
    sent = to_chips(grads) if to_chips is not None else ()
    dhn = _mm(dproj, w["w_main"], "nt", F32, "d_hn", tk=512, init=_mm(ddt_b, w["w_dt"], "nt", F32, "d_hn_dt"),
              exchange=sent)
    landed = ()
    if sent:
        dhn, landed = dhn
    grad_x, grads["norm_w"] = _rmsnorm_bwd(x, w["norm_w"], dhn, dout_f)
    return loss_row, grad_x, grads, sent, landed


def _pad_lanes(v):
    return jnp.pad(v, ((0, 0), (0, LANES - v.shape[1])))


def _cut(lo, hi, a, b):
    a, b = max(lo, a), min(hi, b)
    return (a, b) if a < b else None


def _main_from_shards(cfg, shards):
    per = cfg.N_IN // len(shards)
    main, dt = [], []
    for j, sh in enumerate(shards):
        lo, hi = j * per, (j + 1) * per
        for dst, rng in ((main, (0, cfg.OGA)), (dt, (cfg.OGA, cfg.OGA + cfg.NH)), (main, (cfg.OGA + cfg.NH, cfg.N_IN))):
            c = _cut(lo, hi, *rng)
            if c is not None:
                dst.append(sh[:, c[0] - lo:c[1] - lo])
    return jnp.concatenate(main, axis=1), _pad_lanes(jnp.concatenate(dt, axis=1))


def _shards_from_main(cfg, g_main, g_dt, n):
    per = cfg.N_IN // n
    out = []
    for j in range(n):
        lo, hi = j * per, (j + 1) * per
        parts = []
        for src, off, rng in ((g_main, 0, (0, cfg.OGA)), (g_dt, cfg.OGA, (cfg.OGA, cfg.OGA + cfg.NH)),
                              (g_main, cfg.NH, (cfg.OGA + cfg.NH, cfg.N_IN))):
            c = _cut(lo, hi, *rng)
            if c is not None:
                parts.append(src[:, c[0] - off:c[1] - off])
        out.append(jnp.concatenate(parts, axis=1) if len(parts) > 1 else parts[0])
    return out


def _full_weights(cfg, norm_w, w_in_shards, conv_w, conv_b, dt_bias, a_log, d_skip, ssm_norm_w, w_attn, w_ssm, w_out, fnw):
    w_main, w_dt = _main_from_shards(cfg, w_in_shards)
    return dict(norm_w=norm_w, w_main=w_main.astype(BF16), w_dt=w_dt.astype(BF16), conv_w=conv_w, conv_b=conv_b,
                dt_bias=_pad_lanes(dt_bias), a_log=_pad_lanes(a_log), d_skip=_pad_lanes(d_skip), ssm_norm_w=ssm_norm_w,
                w_attn=w_attn.astype(BF16), w_ssm=w_ssm.astype(BF16), w_out=w_out.astype(BF16), final_norm_w=fnw)


def _grad_w_in(cfg, grads):
    return _shards_from_main(cfg, grads["w_main"], grads["w_dt"], 1)[0]


def kernel(x, norm_w, w_in, conv_w, conv_b, dt_bias, a_log, d_skip, ssm_norm_w, w_attn_branch, w_ssm_branch, w_out, final_norm_w, loss_target, m_norm_w, m_w_in, m_conv_w, m_conv_b, m_dt_bias, m_a_log, m_d_skip, m_ssm_norm_w, m_w_attn_branch, m_w_ssm_branch, m_w_out, m_final_norm_w, v_norm_w, v_w_in, v_conv_w, v_conv_b, v_dt_bias, v_a_log, v_d_skip, v_ssm_norm_w, v_w_attn_branch, v_w_ssm_branch, v_w_out, v_final_norm_w):
    cfg = _Cfg(x.shape[1], x.shape[2])
    d, si, cd, nh = cfg.D, cfg.SI, cfg.CD, cfg.NH
    chip = 2 * lax.axis_index("x") + lax.axis_index("y")
    core = lax.axis_index("c").astype(jnp.int32).reshape(1)
    slabs = jnp.arange(N_CHIPS, dtype=jnp.int32)
    chip_core = jnp.concatenate([chip.astype(jnp.int32).reshape(1), core,
                                 jnp.where(slabs == chip, (slabs + 1) % N_CHIPS, slabs)])

    own = [w_in[0].astype(BF16), w_attn_branch[0].astype(BF16), w_ssm_branch[0].astype(BF16), w_out[0].astype(BF16),
           conv_w[0].reshape(4 * CONV_K, -1)]
    a_in, a_attn, a_ssm, a_out, a_cw = [_with_own(g, o, chip) for g, o in zip(_gather_chips(own), own)]
    conv_w_full = a_cw.reshape(N_CHIPS, CONV_K, cd // N_CHIPS).transpose(1, 0, 2).reshape(CONV_K, cd)
    w = _full_weights(cfg, norm_w, [a_in[j] for j in range(N_CHIPS)], conv_w_full, conv_b, dt_bias, a_log, d_skip,
                      ssm_norm_w, a_attn.reshape(d, d), a_ssm.reshape(si, d), a_out.reshape(d, d),
                      final_norm_w.reshape(1, d))

    def to_chips(grads):
        by_chip = [jnp.stack(_shards_from_main(cfg, grads["w_main"], grads["w_dt"], N_CHIPS)),
                   grads["w_attn"].reshape(N_CHIPS, d // N_CHIPS, d),
                   grads["w_ssm"].reshape(N_CHIPS, si // N_CHIPS, d),
                   grads["w_out"].reshape(N_CHIPS, d // N_CHIPS, d)]
        from_sibling = _exchange_halves(by_chip)
        return [_add_sibling(g, r, core) for g, r in zip(by_chip, from_sibling)]

    loss_row, grad_x, grads, chip_sums, from_chips = _local_step(cfg, x[0], loss_target[0], w, to_chips)
    halves = [_add_chips(o, p, chip_core) for o, p in zip(chip_sums, from_chips)]
    g_in, g_attn, g_ssm, g_out = [h.reshape(2 * h.shape[1], h.shape[2]) for h in _share_halves(halves)]

    small = [loss_row, grads["norm_w"], grads["conv_b"], grads["dt_bias"], grads["a_log"], grads["d_skip"],
             grads["ssm_norm_w"], grads["final_norm_w"], grads["conv_w"].reshape(1, CONV_K * cd)]
    sizes = [a.shape[1] for a in small]
    total = sum(sizes)
    rows = -(-total // (8 * LANES)) * 8
    flat = jnp.pad(jnp.concatenate(small, axis=1), ((0, 0), (0, rows * LANES - total)))
    red = _allreduce_small(flat.reshape(rows, LANES)).reshape(1, rows * LANES)
    offs = [sum(sizes[:i]) for i in range(len(sizes))]
    loss_r, g_nw, g_cb, g_dtb, g_alog, g_dsk, g_snw, g_fnw, g_cw_flat = [
        red[:, o:o + n] for o, n in zip(offs, sizes)]
    loss = loss_r[0, 0]
    g_dtb, g_alog, g_dsk = g_dtb[:, :nh], g_alog[:, :nh], g_dsk[:, :nh]
    cshard = cd // N_CHIPS
    g_cw = lax.dynamic_slice_in_dim(g_cw_flat.reshape(CONV_K, cd), chip * cshard, cshard, axis=1)

    upd = {}
    for name, wv, gv, mv, vv in [("w_in", w_in[0], g_in, m_w_in[0], v_w_in[0]),
                                 ("w_attn", w_attn_branch[0], g_attn, m_w_attn_branch[0], v_w_attn_branch[0]),
                                 ("w_ssm", w_ssm_branch[0], g_ssm, m_w_ssm_branch[0], v_w_ssm_branch[0]),
                                 ("w_out", w_out[0], g_out, m_w_out[0], v_w_out[0])]:
        upd[name] = _adamw(wv, gv, mv, vv, "adamw_" + name)
    names = ["norm_w", "conv_w", "conv_b", "dt_bias", "a_log", "d_skip", "ssm_norm_w", "final_norm_w"]
    ws = [norm_w, conv_w[0].reshape(1, -1), conv_b, dt_bias, a_log, d_skip, ssm_norm_w, final_norm_w.reshape(1, d)]
    gs = [g_nw, g_cw.reshape(1, -1), g_cb, g_dtb, g_alog, g_dsk, g_snw, g_fnw]
    ms = [m_norm_w, m_conv_w[0].reshape(1, -1), m_conv_b, m_dt_bias, m_a_log, m_d_skip, m_ssm_norm_w,
          m_final_norm_w.reshape(1, d)]
    vs = [v_norm_w, v_conv_w[0].reshape(1, -1), v_conv_b, v_dt_bias, v_a_log, v_d_skip, v_ssm_norm_w,
          v_final_norm_w.reshape(1, d)]
    ssz = [a.shape[1] for a in ws]
    stot = sum(ssz)
    srows = -(-stot // (8 * LANES)) * 8

    def pack(parts):
        return jnp.pad(jnp.concatenate(parts, axis=1), ((0, 0), (0, srows * LANES - stot))).reshape(srows, LANES)

    packed = _adamw(pack(ws), pack(gs), pack(ms), pack(vs), "adamw_small")
    soffs = [sum(ssz[:i]) for i in range(len(ssz))]
    for k, nm in enumerate(names):
        upd[nm] = tuple(p.reshape(1, srows * LANES)[:, soffs[k]:soffs[k] + ssz[k]] for p in packed)

    shapes = dict(norm_w=norm_w.shape, w_in=w_in.shape, conv_w=conv_w.shape, conv_b=conv_b.shape, dt_bias=dt_bias.shape,
                  a_log=a_log.shape, d_skip=d_skip.shape, ssm_norm_w=ssm_norm_w.shape, w_attn=w_attn_branch.shape,
                  w_ssm=w_ssm_branch.shape, w_out=w_out.shape, final_norm_w=final_norm_w.shape)
    order = ["norm_w", "w_in", "conv_w", "conv_b", "dt_bias", "a_log", "d_skip", "ssm_norm_w", "w_attn", "w_ssm",
             "w_out", "final_norm_w"]
    gradv = dict(norm_w=g_nw, w_in=g_in, conv_w=g_cw, conv_b=g_cb, dt_bias=g_dtb, a_log=g_alog, d_skip=g_dsk,
                 ssm_norm_w=g_snw, w_attn=g_attn, w_ssm=g_ssm, w_out=g_out, final_norm_w=g_fnw)
    outs = [loss, grad_x[None]]
    outs += [gradv[n].reshape(shapes[n]) for n in order]
    for k in range(3):
        outs += [upd[n][k].reshape(shapes[n]) for n in order]
    return tuple(outs)
```

```python
import functools
import math

import jax
import jax.numpy as jnp
from jax import lax
from jax.experimental import pallas as pl
from jax.experimental.pallas import tpu as pltpu

F32 = jnp.float32
BF16 = jnp.bfloat16
SDS = jax.ShapeDtypeStruct

RMS_EPS = 1e-6
LANES = 128
CHUNK = 128
SSM_HEAD_DIM = 64
SSM_GROUPS = 8
SSM_STATE = 128
CONV_K = 4
ATTN_HEAD_DIM = 128
DILATED_PATTERNS = ((128, 1), (512, 4), (2048, 16))
ATTN_WINDOW = max(w for w, _ in DILATED_PATTERNS)
NEG = -1e30
VMEM_LIMIT = 56 * 1024 * 1024
ADAM_LR, ADAM_B1, ADAM_B2, ADAM_EPS, ADAM_WD, ADAM_STEP = 0.001, 0.9, 0.999, 1e-08, 0.01, 10
MESH = pl.DeviceIdType.MESH
N_CHIPS = 4
N_DEV = 8


class _Cfg:
    def __init__(self, s, d):
        self.S, self.D = s, d
        self.H = d // ATTN_HEAD_DIM
        self.SI = 2 * d
        self.NH = self.SI // SSM_HEAD_DIM
        self.HPG = self.NH // SSM_GROUPS
        self.GW = self.HPG * SSM_HEAD_DIM
        self.BC = SSM_GROUPS * SSM_STATE
        self.CD = self.SI + 2 * self.BC
        self.OQ, self.OK, self.OV, self.OZA = 0, d, 2 * d, 3 * d
        self.OZS = 4 * d
        self.OXBC = self.OZS + self.SI
        self.OGA = self.OXBC + self.CD
        self.OGS = self.OGA + d
        self.NM = self.OGS + d
        self.N_IN = self.NM + self.NH
        assert self.GW % LANES == 0 and self.NH <= LANES and s % 512 == 0 and d % 512 == 0


def _params(sem=None):
    return pltpu.CompilerParams(dimension_semantics=sem, vmem_limit_bytes=VMEM_LIMIT)


def _sigmoid(x):
    return 1.0 / (1.0 + jnp.exp(-x))


def _softplus(x):
    u = jnp.exp(-jnp.abs(x))
    l1p = jnp.where(u < 1e-3, u * (1.0 - u * (0.5 - u * (1.0 / 3.0))), jnp.log(1.0 + u))
    return jnp.maximum(x, 0.0) + l1p


def _nt(a, b):
    return lax.dot_general(a, b, (((1,), (1,)), ((), ())), preferred_element_type=F32)


def _tn(a, b):
    return lax.dot_general(a, b, (((0,), (0,)), ((), ())), preferred_element_type=F32)


def _nn(a, b):
    return jnp.dot(a, b, preferred_element_type=F32)


def _tile(n, target):
    if n <= target:
        return n
    best = None
    for t in range(LANES, target + 1, LANES):
        if n % t == 0:
            best = t
    assert best is not None, (n, target)
    return best


MM_TK = {"nn": 2048, "nt": 2048, "tn": 1024}


def _mm(a, b, dims, out_dtype, name, tm=1024, tn=2048, tk=None, init=None, carry=None):
    tk = MM_TK[dims] if tk is None else tk
    if dims == "nn":
        (m, k), (k2, n) = a.shape, b.shape
    elif dims == "nt":
        (m, k), (n, k2) = a.shape, b.shape
    else:
        (k, m), (k2, n) = a.shape, b.shape
    assert k == k2
    tm, tn, tk = _tile(m, tm), _tile(n, tn), _tile(k, tk)
    nk = k // tk
    if dims == "tn":
        a_spec = pl.BlockSpec((tk, tm), lambda i, j, kk: (kk, i))
    else:
        a_spec = pl.BlockSpec((tm, tk), lambda i, j, kk: (i, kk))
    if dims == "nt":
        b_spec = pl.BlockSpec((tn, tk), lambda i, j, kk: (j, kk))
    else:
        b_spec = pl.BlockSpec((tk, tn), lambda i, j, kk: (kk, j))
    o_spec = pl.BlockSpec((tm, tn), lambda i, j, kk: (i, j))
    op = {"nn": _nn, "nt": _nt, "tn": _tn}[dims]
    has_init = init is not None
    nx = len(carry.arrays) if carry is not None else 0
    ni, nj = m // tm, n // tn

    def body(*refs):
        a_ref, b_ref = refs[0], refs[1]
        i_ref = refs[2] if has_init else None
        x_in = refs[2 + has_init:2 + has_init + nx]
        o_ref = refs[2 + has_init + nx]
        x_out = refs[3 + has_init + nx:3 + has_init + 2 * nx]
        acc = refs[3 + has_init + 2 * nx]
        x_sems = refs[4 + has_init + 2 * nx:]
        i, j, kk = pl.program_id(0), pl.program_id(1), pl.program_id(2)

        if nx:
            @pl.when((i == 0) & (j == 0) & (kk == 0))
            def _():
                carry.start(x_in, x_out, x_sems)

        prod = lambda: op(a_ref[...], b_ref[...])
        with_init = (lambda p: p + i_ref[...].astype(F32)) if has_init else (lambda p: p)
        if nk == 1:
            o_ref[...] = with_init(prod()).astype(out_dtype)
        else:
            @pl.when(kk == 0)
            def _():
                acc[...] = with_init(prod())

            @pl.when((kk > 0) & (kk < nk - 1))
            def _():
                acc[...] += prod()

            @pl.when(kk == nk - 1)
            def _():
                o_ref[...] = (acc[...] + prod()).astype(out_dtype)

        if nx:
            @pl.when((i == ni - 1) & (j == nj - 1) & (kk == nk - 1))
            def _():
                carry.finish(x_in, x_out, x_sems)

    in_specs = [a_spec, b_spec] + ([o_spec] if has_init else []) + [HBM_SPEC] * nx
    args = (a, b) + ((init,) if has_init else ()) + (tuple(carry.arrays) if nx else ())
    sems = carry.sem_shapes() if nx else []
    outs = pl.pallas_call(
        body, out_shape=[SDS((m, n), out_dtype)] + (carry.out_shapes if nx else []), grid=(ni, nj, nk),
        in_specs=in_specs, out_specs=[o_spec] + [HBM_SPEC] * nx,
        scratch_shapes=[pltpu.VMEM((tm, tn) if nk > 1 else (8, LANES), F32)] + sems,
        compiler_params=_params(("arbitrary",) * 3 if nx else ("parallel", "parallel", "arbitrary")), name=name)(*args)
    return (outs[0], outs[1:]) if nx else outs[0]


def _rmsnorm_fwd(x, w):
    s, d = x.shape
    tr = 256

    def body(x_ref, w_ref, o_ref):
        xv = x_ref[...]
        r = lax.rsqrt(jnp.mean(xv * xv, axis=-1, keepdims=True) + RMS_EPS)
        o_ref[...] = (xv * r * w_ref[...]).astype(BF16)

    return pl.pallas_call(
        body, out_shape=SDS((s, d), BF16), grid=(s // tr,),
        in_specs=[pl.BlockSpec((tr, d), lambda i: (i, 0)), pl.BlockSpec((1, d), lambda i: (0, 0))],
        out_specs=pl.BlockSpec((tr, d), lambda i: (i, 0)),
        compiler_params=_params(("parallel",)), name="rmsnorm_fwd")(x, w)


def _rmsnorm_bwd(x, w, dhn_a, dhn_b, dout):
    s, d = x.shape
    tr = 256

    def body(x_ref, w_ref, dh_ref, dh2_ref, do_ref, gx_ref, gw_ref):
        xv = x_ref[...]
        r = lax.rsqrt(jnp.mean(xv * xv, axis=-1, keepdims=True) + RMS_EPS)
        nrm = xv * r
        dh = dh_ref[...] + dh2_ref[...]
        gy = dh * w_ref[...]
        gx_ref[...] = do_ref[...] + r * (gy - nrm * jnp.mean(gy * nrm, axis=-1, keepdims=True))

        @pl.when(pl.program_id(0) == 0)
        def _():
            gw_ref[...] = jnp.zeros_like(gw_ref)

        gw_ref[...] += jnp.sum(dh * nrm, axis=0, keepdims=True)

    blk = pl.BlockSpec((tr, d), lambda i: (i, 0))
    row = pl.BlockSpec((1, d), lambda i: (0, 0))
    return pl.pallas_call(
        body, out_shape=(SDS((s, d), F32), SDS((1, d), F32)), grid=(s // tr,),
        in_specs=[blk, row, blk, blk, blk], out_specs=(blk, row),
        compiler_params=_params(("arbitrary",)), name="rmsnorm_bwd")(x, w, dhn_a, dhn_b, dout)


DEINT = DILATED_PATTERNS[-1][1]
DEINT_ROWS = DEINT * LANES


class _Pass:
    def __init__(self, tq, patterns, unit, seg_len):
        self.tq, self.patterns, self.unit, self.seg_len = tq, patterns, unit, seg_len
        self.win = max(w for w, _ in patterns) // unit
        self.w = self.win + tq
        assert self.win % tq == 0


def _attn_tables(ps):
    i = jnp.arange(ps.tq, dtype=jnp.int32)[:, None]
    j = jnp.arange(ps.w, dtype=jnp.int32)[None, :]
    delta = (i + ps.win - j) * ps.unit
    n = jnp.zeros((ps.tq, ps.w), F32)
    for window, dil in ps.patterns:
        n = n + ((delta >= 0) & (delta <= window) & (delta % dil == 0)).astype(F32)
    logn = jnp.where(n > 0, jnp.log(jnp.maximum(n, 1.0)), NEG)
    return logn, jnp.maximum(delta, 0).astype(F32)


def _slopes(h):
    s = jnp.asarray([2.0 ** (-8.0 * (i + 1) / h) for i in range(h)], F32)
    return jnp.broadcast_to(s[:, None, None], (h, 1, LANES))


def _masked_logn(ps, logn_ref, start):
    col = lax.broadcasted_iota(jnp.int32, (ps.tq, ps.w), 1)
    return jnp.where(col >= ps.win - lax.rem(start, ps.seg_len), logn_ref[...], NEG)


def _head_cols(hh):
    return slice(hh * ATTN_HEAD_DIM, (hh + 1) * ATTN_HEAD_DIM)


def _head_window(refs, cs):
    return jnp.concatenate([r[:, cs] for r in refs], axis=0)


def _head_scores(q_ref, kw, cs, base, dist_ref, slope_ref, hh):
    return _nt(q_ref[:, cs], kw) * (ATTN_HEAD_DIM ** -0.5) + (base - slope_ref[hh][0:1, 0:1] * dist_ref[...])


def _lane_of(stat, hh):
    lane = lax.broadcasted_iota(jnp.int32, stat.shape, 1)
    return jnp.sum(jnp.where(lane == hh, stat, 0.0), axis=1, keepdims=True)


def _window_specs(ps, d, col, nb):
    nprev = ps.win // ps.tq
    return [pl.BlockSpec((ps.tq, d), lambda i, b=b: (jnp.maximum(jnp.minimum(i, nb - 1) - (nprev - b), 0), col))
            for b in range(nprev + 1)]


def _attn_fwd(cfg, ps, qkv, cols, tables, slopes, name):
    s, h, d = cfg.S, cfg.H, cfg.D
    tq, nw = ps.tq, ps.win // ps.tq + 1
    nb = s // tq
    logn, dist = tables
    qc, kc, vc = [c // d for c in cols]

    def body(*refs):
        q_ref, k_refs, v_refs = refs[0], refs[1:1 + nw], refs[1 + nw:1 + 2 * nw]
        logn_ref, dist_ref, slope_ref, o_ref, lse_ref = refs[1 + 2 * nw:]
        base = _masked_logn(ps, logn_ref, pl.program_id(0) * tq)
        lane = lax.broadcasted_iota(jnp.int32, (tq, LANES), 1)

        lse = jnp.zeros((tq, LANES), F32)
        for hh in range(h):
            cs = _head_cols(hh)
            sc = _head_scores(q_ref, _head_window(k_refs, cs), cs, base, dist_ref, slope_ref, hh)
            m = jnp.max(sc, axis=1, keepdims=True)
            p = jnp.exp(sc - m)
            l = jnp.sum(p, axis=1, keepdims=True)
            o_ref[:, cs] = (_nn(p.astype(BF16), _head_window(v_refs, cs)) / l).astype(BF16)
            lse = jnp.where(lane == hh, m + jnp.log(l), lse)
        lse_ref[...] = lse

    tab = pl.BlockSpec((tq, ps.w), lambda i: (0, 0))
    return pl.pallas_call(
        body, out_shape=(SDS((s, d), BF16), SDS((s, LANES), F32)), grid=(nb,),
        in_specs=[pl.BlockSpec((tq, d), lambda i: (i, qc))] + _window_specs(ps, d, kc, nb) + _window_specs(ps, d, vc, nb)
        + [tab, tab, pl.BlockSpec((h, 1, LANES), lambda i: (0, 0, 0))],
        out_specs=(pl.BlockSpec((tq, d), lambda i: (i, 0)), pl.BlockSpec((tq, LANES), lambda i: (i, 0))),
        compiler_params=_params(("parallel",)), name=name)(*([qkv] * (1 + 2 * nw)), logn, dist, slopes)


def _attn_bwd(cfg, ps, qkv, cols, do, lse, delta, tables, slopes, name):
    s, h, d = cfg.S, cfg.H, cfg.D
    tq, nprev = ps.tq, ps.win // ps.tq
    nw = nprev + 1
    nb = s // tq
    logn, dist = tables
    qc, kc, vc = [c // d for c in cols]
    scale = ATTN_HEAD_DIM ** -0.5

    def body(*refs):
        q_ref, k_refs, v_refs = refs[0], refs[1:1 + nw], refs[1 + nw:1 + 2 * nw]
        do_ref, lse_ref, dl_ref, logn_ref, dist_ref, slope_ref, dq_ref, dk_ref, dv_ref, ck, cv = refs[1 + 2 * nw:]
        i = pl.program_id(0)
        slot = lambda b: lax.rem(i + b, nprev)

        @pl.when(i == 0)
        def _():
            ck[...] = jnp.zeros_like(ck)
            cv[...] = jnp.zeros_like(cv)

        @pl.when(i < nb)
        def _():
            base = _masked_logn(ps, logn_ref, i * tq)
            lse_all, dl_all = lse_ref[...], dl_ref[...]

            for hh in range(h):
                cs = _head_cols(hh)
                kw, vw = _head_window(k_refs, cs), _head_window(v_refs, cs)
                sc = _head_scores(q_ref, kw, cs, base, dist_ref, slope_ref, hh)
                p = jnp.exp(sc - lse_all[:, hh:hh + 1])
                dob = do_ref[:, cs]
                ds = (p * (_nt(dob, vw) - dl_all[:, hh:hh + 1]) * scale).astype(BF16)
                dq_ref[:, cs] = _nn(ds, kw).astype(BF16)
                dkw = _tn(ds, q_ref[:, cs])
                dvw = _tn(p.astype(BF16), dob)
                dk_ref[:, cs] = ck[slot(0), :, cs] + dkw[0:tq]
                dv_ref[:, cs] = cv[slot(0), :, cs] + dvw[0:tq]
                for b in range(1, nprev):
                    ck[slot(b), :, cs] += dkw[b * tq:(b + 1) * tq]
                    cv[slot(b), :, cs] += dvw[b * tq:(b + 1) * tq]
                ck[slot(0), :, cs] = dkw[nprev * tq:]
                cv[slot(0), :, cs] = dvw[nprev * tq:]

        @pl.when(i >= nb)
        def _():
            dk_ref[...] = ck[slot(0)]
            dv_ref[...] = cv[slot(0)]

    here = lambda i: jnp.minimum(i, nb - 1)
    blk = pl.BlockSpec((tq, d), lambda i: (here(i), 0))
    stat = pl.BlockSpec((tq, LANES), lambda i: (here(i), 0))
    late = pl.BlockSpec((tq, d), lambda i: (jnp.maximum(i - nprev, 0), 0))
    tab = pl.BlockSpec((tq, ps.w), lambda i: (0, 0))
    return pl.pallas_call(
        body, out_shape=(SDS((s, d), BF16), SDS((s, d), F32), SDS((s, d), F32)), grid=(nb + nprev,),
        in_specs=[pl.BlockSpec((tq, d), lambda i: (here(i), qc))] + _window_specs(ps, d, kc, nb)
        + _window_specs(ps, d, vc, nb) + [blk, stat, stat, tab, tab, pl.BlockSpec((h, 1, LANES), lambda i: (0, 0, 0))],
        out_specs=(blk, late, late),
        scratch_shapes=[pltpu.VMEM((nprev, tq, d), F32), pltpu.VMEM((nprev, tq, d), F32)],
        compiler_params=_params(("arbitrary",)), name=name)(
            *([qkv] * (1 + 2 * nw)), do, lse, delta, logn, dist, slopes)


def _by_residue(a):
    return a.reshape(DEINT, a.shape[0] // DEINT, a.shape[1])


def _deint_spec(colblock):
    return pl.BlockSpec((DEINT, LANES, LANES), lambda b, j: (0, b, colblock(j)))


def _deint_rows(scr, out_ref, dtype):
    for r in range(DEINT):
        out_ref[r] = scr[pl.ds(r, LANES, stride=DEINT), :].astype(dtype)


def _int_rows(in_ref, scr):
    for r in range(DEINT):
        scr[pl.ds(r, LANES, stride=DEINT), :] = in_ref[r].astype(F32)


WIDE = 4 * LANES


def _wide_spec():
    return pl.BlockSpec((DEINT, LANES, WIDE), lambda b, j: (0, b, j))


def _deinterleave(x, col0, ncols, name):
    s = x.shape[0]
    c0 = col0 // WIDE

    def body(x_ref, o_ref, scr):
        for t in range(WIDE // LANES):
            cs = slice(t * LANES, (t + 1) * LANES)
            scr[t] = x_ref[:, cs].astype(F32)
            for r in range(DEINT):
                o_ref[r, :, cs] = scr.at[t][pl.ds(r, LANES, stride=DEINT), :].astype(x.dtype)

    out = pl.pallas_call(
        body, out_shape=SDS((DEINT, s // DEINT, ncols), x.dtype), grid=(s // DEINT_ROWS, ncols // WIDE),
        in_specs=[pl.BlockSpec((DEINT_ROWS, WIDE), lambda b, j: (b, c0 + j))],
        out_specs=_wide_spec(),
        scratch_shapes=[pltpu.VMEM((WIDE // LANES, DEINT_ROWS, LANES), F32)],
        compiler_params=_params(("parallel", "parallel")), name=name)(x)
    return out.reshape(s, ncols)


def _attn_merge(cfg, proj, o_1, lse_1, o_2, lse_2):
    s, h = cfg.S, cfg.H
    zb = cfg.OZA // LANES
    rows = DEINT_ROWS

    def body(o1_ref, l1_ref, o2_ref, l2_ref, z_ref, o_ref, og_ref, lse_ref, so, sl):
        hh = pl.program_id(1)
        _int_rows(o2_ref, so)

        @pl.when(hh == 0)
        def _():
            _int_rows(l2_ref, sl)

        l1, l2 = _lane_of(l1_ref[...], hh), _lane_of(sl[...], hh)
        mx = jnp.maximum(l1, l2)
        w1, w2 = jnp.exp(l1 - mx), jnp.exp(l2 - mx)
        den = w1 + w2
        o = (w1 * o1_ref[...].astype(F32) + w2 * so[...]) / den
        z = z_ref[...].astype(F32)
        o_ref[...] = o.astype(BF16)
        og_ref[...] = (o * (z * _sigmoid(z))).astype(BF16)

        @pl.when(hh == 0)
        def _():
            lse_ref[...] = jnp.zeros_like(lse_ref)

        lane = lax.broadcasted_iota(jnp.int32, (rows, LANES), 1)
        lse_ref[...] = jnp.where(lane == hh, mx + jnp.log(den), lse_ref[...])

    blk = pl.BlockSpec((rows, LANES), lambda b, j: (b, j))
    return pl.pallas_call(
        body, out_shape=(SDS((s, cfg.D), BF16), SDS((s, cfg.D), BF16), SDS((s, LANES), F32)),
        grid=(s // rows, h),
        in_specs=[blk, pl.BlockSpec((rows, LANES), lambda b, j: (b, 0)), _deint_spec(lambda j: j),
                  _deint_spec(lambda j: 0), pl.BlockSpec((rows, LANES), lambda b, j: (b, zb + j))],
        out_specs=(blk, blk, pl.BlockSpec((rows, LANES), lambda b, j: (b, 0))),
        scratch_shapes=[pltpu.VMEM((rows, LANES), F32), pltpu.VMEM((rows, LANES), F32)],
        compiler_params=_params(("parallel", "arbitrary")), name="attn_merge")(
            o_1, lse_1, _by_residue(o_2), _by_residue(lse_2), proj)


def _attn_bwd_prep(cfg, proj, o_a, doag, lse, dproj):
    s, h = cfg.S, cfg.H
    zb = cfg.OZA // LANES
    rows = DEINT_ROWS

    def body(o_ref, dg_ref, z_ref, lse_ref, dp_in, dz_ref, do_ref, do2_ref, dl_ref, dl2_ref, lse2_ref, scr):
        del dp_in
        hh = pl.program_id(1)
        z = z_ref[...].astype(F32)
        sg = _sigmoid(z)
        o = o_ref[...].astype(F32)
        dg = dg_ref[...].astype(F32)
        do = dg * (z * sg)
        dz_ref[...] = (dg * o * (sg * (1.0 + z * (1.0 - sg)))).astype(BF16)
        do_ref[...] = do.astype(BF16)
        scr[...] = do
        _deint_rows(scr, do2_ref, BF16)

        @pl.when(hh == 0)
        def _():
            dl_ref[...] = jnp.zeros_like(dl_ref)

        lane = lax.broadcasted_iota(jnp.int32, (rows, LANES), 1)
        dl_ref[...] = jnp.where(lane == hh, jnp.sum(do * o, axis=1, keepdims=True), dl_ref[...])

        @pl.when(hh == h - 1)
        def _():
            scr[...] = dl_ref[...]
            _deint_rows(scr, dl2_ref, F32)
            scr[...] = lse_ref[...]
            _deint_rows(scr, lse2_ref, F32)

    blk = pl.BlockSpec((rows, LANES), lambda b, j: (b, j))
    stat = pl.BlockSpec((rows, LANES), lambda b, j: (b, 0))
    stat2 = _deint_spec(lambda j: 0)
    outs = pl.pallas_call(
        body,
        out_shape=(SDS(dproj.shape, BF16), SDS((s, cfg.D), BF16), SDS((DEINT, s // DEINT, cfg.D), BF16),
                   SDS((s, LANES), F32), SDS((DEINT, s // DEINT, LANES), F32), SDS((DEINT, s // DEINT, LANES), F32)),
        grid=(s // rows, h),
        in_specs=[blk, blk, pl.BlockSpec((rows, LANES), lambda b, j: (b, zb + j)), stat, HBM_SPEC],
        out_specs=(pl.BlockSpec((rows, LANES), lambda b, j: (b, zb + j)), blk, _deint_spec(lambda j: j),
                   stat, stat2, stat2),
        scratch_shapes=[pltpu.VMEM((rows, LANES), F32)],
        input_output_aliases={4: 0},
        compiler_params=_params(("parallel", "arbitrary")), name="attn_bwd_prep")(o_a, doag, proj, lse, dproj)
    dproj, do, do2, dl, dl2, lse2 = outs
    return dproj, do, do2.reshape(s, cfg.D), dl, dl2.reshape(s, LANES), lse2.reshape(s, LANES)


def _attn_grad_sum(cfg, g_1, g_2, col0, dproj, name):
    s = cfg.S
    c0 = col0 // WIDE
    rows = DEINT_ROWS

    def body(g1_ref, g2_ref, dp_in, o_ref, scr):
        del dp_in
        for t in range(WIDE // LANES):
            cs = slice(t * LANES, (t + 1) * LANES)
            for r in range(DEINT):
                scr.at[t][pl.ds(r, LANES, stride=DEINT), :] = g2_ref[r, :, cs].astype(F32)
            o_ref[:, cs] = (g1_ref[:, cs].astype(F32) + scr[t]).astype(BF16)

    return pl.pallas_call(
        body, out_shape=SDS(dproj.shape, BF16), grid=(s // rows, cfg.D // WIDE),
        in_specs=[pl.BlockSpec((rows, WIDE), lambda b, j: (b, j)), _wide_spec(), HBM_SPEC],
        out_specs=pl.BlockSpec((rows, WIDE), lambda b, j: (b, c0 + j)),
        scratch_shapes=[pltpu.VMEM((WIDE // LANES, rows, LANES), F32)],
        input_output_aliases={2: 0},
        compiler_params=_params(("parallel", "parallel")), name=name)(g_1, _by_residue(g_2), dproj)


CONV_HALO = 16
CONV_TR = 512
CONV_CW = 512


def _conv_fwd(cfg, proj, conv_w, conv_b):
    s, cd = cfg.S, cfg.CD
    tr, cw, hl = CONV_TR, CONV_CW, CONV_HALO
    cb0 = cfg.OXBC // cw

    def body(x_ref, h_ref, w_ref, b_ref, o_ref, scr):
        i = pl.program_id(0)
        scr[pl.ds(0, hl), :] = jnp.where(i > 0, h_ref[...].astype(F32), 0.0)
        scr[pl.ds(hl, tr), :] = x_ref[...].astype(F32)
        pre = b_ref[...] + jnp.zeros((tr, cw), F32)
        for k in range(CONV_K):
            pre = pre + w_ref[k:k + 1, :] * scr[pl.ds(hl - (CONV_K - 1) + k, tr), :]
        o_ref[...] = (pre * _sigmoid(pre)).astype(BF16)

    return pl.pallas_call(
        body, out_shape=SDS((s, cd), BF16), grid=(s // tr, cd // cw),
        in_specs=[pl.BlockSpec((tr, cw), lambda i, j: (i, cb0 + j)),
                  pl.BlockSpec((hl, cw), lambda i, j: (jnp.maximum(i * (tr // hl) - 1, 0), cb0 + j)),
                  pl.BlockSpec((CONV_K, cw), lambda i, j: (0, j)),
                  pl.BlockSpec((1, cw), lambda i, j: (0, j))],
        out_specs=pl.BlockSpec((tr, cw), lambda i, j: (i, j)),
        scratch_shapes=[pltpu.VMEM((tr + hl, cw), F32)],
        compiler_params=_params(("parallel", "parallel")), name="conv_fwd")(proj, proj, conv_w, conv_b)


def _conv_bwd(cfg, proj, dact, conv_w, conv_b, dproj):
    s, cd = cfg.S, cfg.CD
    tr, cw, hl = CONV_TR, CONV_CW, CONV_HALO
    cb0 = cfg.OXBC // cw
    nr = s // tr
    last_h = s // hl - 1

    def body(x_ref, hp_ref, hn_ref, d_ref, dn_ref, w_ref, b_ref, dp_in, dx_ref, gw_ref, gb_ref, xs, ds):
        del dp_in
        i = pl.program_id(1)
        xs[pl.ds(0, hl), :] = jnp.where(i > 0, hp_ref[...].astype(F32), 0.0)
        xs[pl.ds(hl, tr), :] = x_ref[...].astype(F32)
        xs[pl.ds(hl + tr, hl), :] = hn_ref[...].astype(F32)
        shifted = [xs[pl.ds(hl - (CONV_K - 1) + k, tr + hl), :] for k in range(CONV_K)]
        pre = b_ref[...] + jnp.zeros((tr + hl, cw), F32)
        for k in range(CONV_K):
            pre = pre + w_ref[k:k + 1, :] * shifted[k]
        sg = _sigmoid(pre)
        dsilu = sg * (1.0 + pre * (1.0 - sg))
        ds[pl.ds(0, tr), :] = d_ref[...].astype(F32) * dsilu[0:tr]
        ds[pl.ds(tr, hl), :] = jnp.where(i < nr - 1, dn_ref[...].astype(F32), 0.0) * dsilu[tr:tr + hl]
        dx = jnp.zeros((tr, cw), F32)
        for k in range(CONV_K):
            dx = dx + w_ref[k:k + 1, :] * ds[pl.ds(CONV_K - 1 - k, tr), :]
        dx_ref[...] = dx.astype(BF16)

        @pl.when(i == 0)
        def _():
            gw_ref[...] = jnp.zeros_like(gw_ref)
            gb_ref[...] = jnp.zeros_like(gb_ref)

        dcur = ds[pl.ds(0, tr), :]
        gb_ref[...] += jnp.sum(dcur, axis=0, keepdims=True)
        for k in range(CONV_K):
            gw_ref[k:k + 1, :] += jnp.sum(dcur * shifted[k][0:tr], axis=0, keepdims=True)

    return pl.pallas_call(
        body, out_shape=(SDS(dproj.shape, BF16), SDS((CONV_K, cd), F32), SDS((1, cd), F32)), grid=(cd // cw, nr),
        in_specs=[pl.BlockSpec((tr, cw), lambda j, i: (i, cb0 + j)),
                  pl.BlockSpec((hl, cw), lambda j, i: (jnp.maximum(i * (tr // hl) - 1, 0), cb0 + j)),
                  pl.BlockSpec((hl, cw), lambda j, i: (jnp.minimum((i + 1) * (tr // hl), last_h), cb0 + j)),
                  pl.BlockSpec((tr, cw), lambda j, i: (i, j)),
                  pl.BlockSpec((hl, cw), lambda j, i: (jnp.minimum((i + 1) * (tr // hl), last_h), j)),
                  pl.BlockSpec((CONV_K, cw), lambda j, i: (0, j)),
                  pl.BlockSpec((1, cw), lambda j, i: (0, j)),
                  pl.BlockSpec(memory_space=pl.ANY)],
        out_specs=(pl.BlockSpec((tr, cw), lambda j, i: (i, cb0 + j)),
                   pl.BlockSpec((CONV_K, cw), lambda j, i: (0, j)),
                   pl.BlockSpec((1, cw), lambda j, i: (0, j))),
        scratch_shapes=[pltpu.VMEM((tr + 2 * hl, cw), F32), pltpu.VMEM((tr + hl, cw), F32)],
        input_output_aliases={7: 0},
        compiler_params=_params(("parallel", "arbitrary")), name="conv_bwd")(
            proj, proj, proj, dact, dact, conv_w, conv_b, dproj)


def _expand(v, e, terms):
    out, rem = None, v
    for _ in range(terms):
        hi = rem.astype(BF16)
        t = _nn(hi, e)
        out = t if out is None else out + t
        rem = rem - hi.astype(F32)
    return out


def _segsum(v, e, terms):
    out, rem = None, v
    for _ in range(terms):
        hi = rem.astype(BF16)
        t = _nt(hi, e)
        out = t if out is None else out + t
        rem = rem - hi.astype(F32)
    return out


def _expand_row(row, e, terms):
    return _expand(jnp.broadcast_to(row, (8, LANES)), e, terms)[0:1]


def _segsum_row(row, e, terms):
    return _segsum(jnp.broadcast_to(row, (8, row.shape[1])), e, terms)[0:1]


def _expansion_matrix(cfg):
    hh = jnp.arange(LANES, dtype=jnp.int32)[:, None]
    cc = jnp.arange(cfg.SI, dtype=jnp.int32)[None, :]
    return (cc // SSM_HEAD_DIM == hh).astype(BF16)


def _tri(lower):
    r = lax.broadcasted_iota(jnp.int32, (CHUNK, CHUNK), 0)
    c = lax.broadcasted_iota(jnp.int32, (CHUNK, CHUNK), 1)
    return (c <= r) if lower else (c >= r)


def _ssd_prep(dtr_ref, db_ref, al_ref, e):
    dtr = dtr_ref[...] + db_ref[...]
    dt = _softplus(dtr)
    a = -jnp.exp(al_ref[...])
    acum = jnp.dot(_tri(True).astype(F32), dt * a, precision=lax.Precision.HIGHEST, preferred_element_type=F32)
    return dtr, dt, a, _expand(dt, e, 2), _expand(acum, e, 3)


def _ssd_fwd(cfg, xact, dt_raw, proj, dt_bias, a_log, d_skip, norm_w, e):
    s, si, cd, gw, bc = cfg.S, cfg.SI, cfg.CD, cfg.GW, cfg.BC
    nc = s // CHUNK
    zb = cfg.OZS // si
    tiles = gw // LANES

    def body(xa_ref, dtr_ref, z_ref, db_ref, al_ref, dsk_ref, nw_ref, e_ref, y_ref, y2_ref, st_ref,
             state, ybuf, x_s, xw_s, ae_s, ea_s, lam_s):
        @pl.when(pl.program_id(0) == 0)
        def _():
            state[...] = jnp.zeros_like(state)

        st_ref[...] = state[...]
        ev = e_ref[...]
        _, _, _, dt_e, a_e = _ssd_prep(dtr_ref, db_ref, al_ref, ev)
        xs = xa_ref[:, 0:si].astype(F32)
        x = xs * dt_e
        lam_e = a_e[CHUNK - 1:CHUNK, :]
        x_s[...] = x.astype(BF16)
        xw_s[...] = (x * jnp.exp(lam_e - a_e)).astype(BF16)
        ae_s[...] = a_e
        ea_s[...] = jnp.exp(a_e)
        ybuf[...] = _expand_row(dsk_ref[...], ev, 3) * xs
        lam_s[...] = jnp.broadcast_to(jnp.exp(lam_e), (8, si))
        tril = _tri(True)
        lane = lax.broadcasted_iota(jnp.int32, (CHUNK, LANES), 1)

        def group(g, carry):
            co = pl.multiple_of(g * gw, LANES)
            bg = xa_ref[:, pl.ds(pl.multiple_of(si + g * SSM_STATE, LANES), SSM_STATE)]
            cg = xa_ref[:, pl.ds(pl.multiple_of(si + bc + g * SSM_STATE, LANES), SSM_STATE)]
            cbm = _nt(cg, bg)
            st = state[:, pl.ds(co, gw)]
            yoff = _nn(cg, st.astype(BF16)) * ea_s[:, pl.ds(co, gw)]
            for k in range(tiles):
                tc = pl.multiple_of(co + k * LANES, LANES)
                at = ae_s[:, pl.ds(tc, LANES)]
                att = at.T
                xt = x_s[:, pl.ds(tc, LANES)]
                acc = yoff[:, k * LANES:(k + 1) * LANES]
                for half in range(2):
                    lo = half * SSM_HEAD_DIM
                    seg = at[:, lo:lo + 1] - att[lo:lo + 1, :]
                    dec = jnp.exp(jnp.where(tril, seg, NEG))
                    xh = jnp.where((lane >= lo) & (lane < lo + SSM_HEAD_DIM), xt, jnp.zeros_like(xt))
                    acc = acc + _nn((cbm * dec).astype(BF16), xh)
                ybuf[:, pl.ds(tc, LANES)] += acc
            state[:, pl.ds(co, gw)] = st * lam_s[0:1, pl.ds(co, gw)] + _tn(bg, xw_s[:, pl.ds(co, gw)])
            return carry

        lax.fori_loop(0, SSM_GROUPS, group, 0)
        y = ybuf[...]
        y_ref[...] = y.astype(BF16)
        z = z_ref[...].astype(F32)
        u = y * (z * _sigmoid(z))
        r = lax.rsqrt(jnp.mean(u * u, axis=-1, keepdims=True) + RMS_EPS)
        y2_ref[...] = (u * r * nw_ref[...]).astype(BF16)

    row = lambda n: pl.BlockSpec((1, n), lambda c: (0, 0))
    return pl.pallas_call(
        body,
        out_shape=(SDS((s, si), BF16), SDS((s, si), BF16), SDS((nc, SSM_STATE, si), F32)),
        grid=(nc,),
        in_specs=[pl.BlockSpec((CHUNK, cd), lambda c: (c, 0)),
                  pl.BlockSpec((CHUNK, LANES), lambda c: (c, 0)),
                  pl.BlockSpec((CHUNK, si), lambda c: (c, zb)),
                  row(LANES), row(LANES), row(LANES), row(si),
                  pl.BlockSpec((LANES, si), lambda c: (0, 0))],
        out_specs=(pl.BlockSpec((CHUNK, si), lambda c: (c, 0)),
                   pl.BlockSpec((CHUNK, si), lambda c: (c, 0)),
                   pl.BlockSpec((None, SSM_STATE, si), lambda c: (c, 0, 0))),
        scratch_shapes=[pltpu.VMEM((SSM_STATE, si), F32), pltpu.VMEM((CHUNK, si), F32),
                        pltpu.VMEM((CHUNK, si), BF16), pltpu.VMEM((CHUNK, si), BF16),
                        pltpu.VMEM((CHUNK, si), F32), pltpu.VMEM((CHUNK, si), F32),
                        pltpu.VMEM((8, si), F32)],
        compiler_params=_params(("arbitrary",)), name="ssd_fwd")(
            xact, dt_raw, proj, dt_bias, a_log, d_skip, norm_w, e)


def _ssd_bwd(cfg, xact, dt_raw, proj, y, dy2, states, dt_bias, a_log, d_skip, norm_w, e, dproj):
    s, si, cd, gw, bc, hpg = cfg.S, cfg.SI, cfg.CD, cfg.GW, cfg.BC, cfg.HPG
    nc = s // CHUNK
    zb = cfg.OZS // si
    tiles = gw // LANES

    def body(xa_ref, dtr_ref, z_ref, y_ref, d2_ref, st_ref, db_ref, al_ref, dsk_ref, nw_ref, e_ref, dp_in,
             dz_ref, dxa_ref, ddt_ref, gnw_ref, gdb_ref, gal_ref, gds_ref,
             dh, dhn, xs_s, x_s, w_s, ae_s, ea_s, g_s, dx_s, dae_s, r_s, lam_s, dle_s):
        del dp_in

        @pl.when(pl.program_id(0) == 0)
        def _():
            dh[...] = jnp.zeros_like(dh)
            gnw_ref[...] = jnp.zeros_like(gnw_ref)
            gdb_ref[...] = jnp.zeros_like(gdb_ref)
            gal_ref[...] = jnp.zeros_like(gal_ref)
            gds_ref[...] = jnp.zeros_like(gds_ref)

        ev = e_ref[...]
        yv = y_ref[...].astype(F32)
        z = z_ref[...].astype(F32)
        sg = _sigmoid(z)
        sz = z * sg
        u = yv * sz
        r = lax.rsqrt(jnp.mean(u * u, axis=-1, keepdims=True) + RMS_EPS)
        nrm = u * r
        d2 = d2_ref[...].astype(F32)
        gnw_ref[...] += jnp.sum(d2 * nrm, axis=0, keepdims=True)
        gn = d2 * nw_ref[...]
        du = r * (gn - nrm * jnp.mean(gn * nrm, axis=-1, keepdims=True))
        gv = du * sz
        dz_ref[...] = (du * yv * (sg * (1.0 + z * (1.0 - sg)))).astype(BF16)
        g_s[...] = gv

        dtr, dt, a, dt_e, a_e = _ssd_prep(dtr_ref, db_ref, al_ref, ev)
        xs = xa_ref[:, 0:si].astype(F32)
        x = xs * dt_e
        lam_e = a_e[CHUNK - 1:CHUNK, :]
        xs_s[...] = xs
        x_s[...] = x
        w_s[...] = jnp.exp(lam_e - a_e)
        ae_s[...] = a_e
        ea_s[...] = jnp.exp(a_e)
        lam_s[...] = jnp.broadcast_to(jnp.exp(lam_e), (8, si))
        gds_ref[...] += _segsum_row(jnp.sum(gv * xs, axis=0, keepdims=True), ev, 2)
        r_s[...] = jnp.zeros_like(r_s)
        tril = _tri(True)
        lane = lax.broadcasted_iota(jnp.int32, (CHUNK, LANES), 1)
        sub = lax.broadcasted_iota(jnp.int32, (CHUNK, LANES), 0)

        def group(g, carry):
            co = pl.multiple_of(g * gw, LANES)
            bo = pl.multiple_of(si + g * SSM_STATE, LANES)
            cof = pl.multiple_of(si + bc + g * SSM_STATE, LANES)
            cols = pl.ds(co, gw)
            bg = xa_ref[:, pl.ds(bo, SSM_STATE)]
            cg = xa_ref[:, pl.ds(cof, SSM_STATE)]
            cbm = _nt(cg, bg)
            st = st_ref[:, cols]
            stb = st.astype(BF16)
            dho = dh[:, cols]
            dhob = dho.astype(BF16)
            ea = ea_s[:, cols]
            gg = g_s[:, cols]
            xg = x_s[:, cols]
            wg = w_s[:, cols]
            explam = lam_s[0:1, cols]
            yoff = _nn(cg, stb) * ea
            ga = (gg * ea).astype(BF16)
            dc = _nt(ga, stb)
            dhn[:, cols] = dho * explam + _tn(cg, ga)
            bdh = _nn(bg, dhob)
            db = _nt((xg * wg).astype(BF16), dhob)
            t = xg * bdh * wg
            dle_s[0:1, cols] = jnp.sum(t, axis=0, keepdims=True) + explam * jnp.sum(dho * st, axis=0, keepdims=True)
            dae_base = gg * yoff - t
            dxw = wg * bdh
            dcb = jnp.zeros((CHUNK, CHUNK), F32)
            for k in range(tiles):
                tc = pl.multiple_of(co + k * LANES, LANES)
                ksl = slice(k * LANES, (k + 1) * LANES)
                at = ae_s[:, pl.ds(tc, LANES)]
                att = at.T
                xt = xg[:, ksl].astype(BF16)
                gt = gg[:, ksl].astype(BF16)
                dxt = dxw[:, ksl]
                place = jnp.zeros((CHUNK, LANES), F32)
                for half in range(2):
                    lo = half * SSM_HEAD_DIM
                    seg = at[:, lo:lo + 1] - att[lo:lo + 1, :]
                    dec = jnp.exp(jnp.where(tril, seg, NEG))
                    mh = cbm * dec
                    gh = jnp.where((lane >= lo) & (lane < lo + SSM_HEAD_DIM), gt, jnp.zeros_like(gt))
                    dm = _nt(gh, xt)
                    dxt = dxt + _tn(mh.astype(BF16), gh)
                    dcb = dcb + dm * dec
                    dseg = dm * mh
                    place = place + jnp.where(lane == lo, jnp.sum(dseg, axis=1, keepdims=True), 0.0)
                    hidx = g * hpg + 2 * k + half
                    r_s[...] += jnp.where(sub == hidx, jnp.sum(dseg, axis=0, keepdims=True), 0.0)
                dx_s[:, pl.ds(tc, LANES)] = dxt
                dae_s[:, pl.ds(tc, LANES)] = dae_base[:, ksl] + place
            dcbb = dcb.astype(BF16)
            dxa_ref[:, pl.ds(bo, SSM_STATE)] = (db + _tn(dcbb, cg)).astype(BF16)
            dxa_ref[:, pl.ds(cof, SSM_STATE)] = (dc + _nn(dcbb, bg)).astype(BF16)
            return carry

        lax.fori_loop(0, SSM_GROUPS, group, 0)
        dlam = _segsum_row(dle_s[0:1, :], ev, 2)
        da_ = _segsum(dae_s[...], ev, 2) - r_s[...].T
        da_ = da_ + jnp.where(sub == CHUNK - 1, dlam, 0.0)
        dda = jnp.dot(_tri(False).astype(F32), da_, precision=lax.Precision.HIGHEST, preferred_element_type=F32)
        dxv = dx_s[...]
        xs = xs_s[...]
        ddt = dda * a + _segsum(dxv * xs, ev, 2)
        gal_ref[...] += jnp.sum(dda * dt, axis=0, keepdims=True) * a
        ddtr = ddt * _sigmoid(dtr)
        gdb_ref[...] += jnp.sum(ddtr, axis=0, keepdims=True)
        ddt_ref[...] = ddtr
        dxa_ref[:, 0:si] = (dxv * dt_e + g_s[...] * _expand_row(dsk_ref[...], ev, 3)).astype(BF16)
        dh[...] = dhn[...]

    rev = lambda c: nc - 1 - c
    row = lambda n: pl.BlockSpec((1, n), lambda c: (0, 0))
    big = lambda: pltpu.VMEM((CHUNK, si), F32)
    return pl.pallas_call(
        body,
        out_shape=(SDS(dproj.shape, BF16), SDS((s, cd), BF16), SDS((s, LANES), F32),
                   SDS((1, si), F32), SDS((1, LANES), F32), SDS((1, LANES), F32), SDS((1, LANES), F32)),
        grid=(nc,),
        in_specs=[pl.BlockSpec((CHUNK, cd), lambda c: (rev(c), 0)),
                  pl.BlockSpec((CHUNK, LANES), lambda c: (rev(c), 0)),
                  pl.BlockSpec((CHUNK, si), lambda c: (rev(c), zb)),
                  pl.BlockSpec((CHUNK, si), lambda c: (rev(c), 0)),
                  pl.BlockSpec((CHUNK, si), lambda c: (rev(c), 0)),
                  pl.BlockSpec((None, SSM_STATE, si), lambda c: (rev(c), 0, 0)),
                  row(LANES), row(LANES), row(LANES), row(si),
                  pl.BlockSpec((LANES, si), lambda c: (0, 0)),
                  pl.BlockSpec(memory_space=pl.ANY)],
        out_specs=(pl.BlockSpec((CHUNK, si), lambda c: (rev(c), zb)),
                   pl.BlockSpec((CHUNK, cd), lambda c: (rev(c), 0)),
                   pl.BlockSpec((CHUNK, LANES), lambda c: (rev(c), 0)),
                   row(si), row(LANES), row(LANES), row(LANES)),
        scratch_shapes=[pltpu.VMEM((SSM_STATE, si), F32), pltpu.VMEM((SSM_STATE, si), F32),
                        big(), big(), big(), big(), big(), big(), big(), big(),
                        pltpu.VMEM((CHUNK, LANES), F32), pltpu.VMEM((8, si), F32), pltpu.VMEM((8, si), F32)],
        input_output_aliases={11: 0},
        compiler_params=_params(("arbitrary",)), name="ssd_bwd")(
            xact, dt_raw, proj, y, dy2, states, dt_bias, a_log, d_skip, norm_w, e, dproj)


MERGE_TR = 512
MERGE_CW = 512


def _merge_fwd(cfg, proj, a_br, s_br):
    s, d = cfg.S, cfg.D
    tr, cw = MERGE_TR, MERGE_CW
    ga0, gs0 = cfg.OGA // cw, cfg.OGS // cw

    def body(ga_ref, gs_ref, a_ref, s_ref, o_ref):
        o_ref[...] = (_sigmoid(ga_ref[...].astype(F32)) * a_ref[...].astype(F32)
                      + _sigmoid(gs_ref[...].astype(F32)) * s_ref[...].astype(F32)).astype(BF16)

    blk = pl.BlockSpec((tr, cw), lambda i, j: (i, j))
    return pl.pallas_call(
        body, out_shape=SDS((s, d), BF16), grid=(s // tr, d // cw),
        in_specs=[pl.BlockSpec((tr, cw), lambda i, j: (i, ga0 + j)),
                  pl.BlockSpec((tr, cw), lambda i, j: (i, gs0 + j)), blk, blk],
        out_specs=blk, compiler_params=_params(("parallel", "parallel")), name="merge_fwd")(proj, proj, a_br, s_br)


def _merge_bwd(cfg, proj, branch, dmerged, gate_off, dproj, name):
    s, d = cfg.S, cfg.D
    tr, cw = MERGE_TR, MERGE_CW
    g0 = gate_off // cw
    fresh = dproj is None

    def body(*refs):
        g_ref, b_ref, dm_ref = refs[:3]
        dg_ref, db_ref = refs[-2:]
        dm = dm_ref[...].astype(F32)
        sg = _sigmoid(g_ref[...].astype(F32))
        db_ref[...] = (dm * sg).astype(BF16)
        dg_ref[...] = (dm * b_ref[...].astype(F32) * sg * (1.0 - sg)).astype(BF16)

    blk = pl.BlockSpec((tr, cw), lambda i, j: (i, j))
    gate = pl.BlockSpec((tr, cw), lambda i, j: (i, g0 + j))
    return pl.pallas_call(
        body, out_shape=(SDS((s, cfg.NM), BF16), SDS((s, d), BF16)), grid=(s // tr, d // cw),
        in_specs=[gate, blk, blk] + ([] if fresh else [HBM_SPEC]),
        out_specs=(gate, blk),
        input_output_aliases={} if fresh else {3: 0},
        compiler_params=_params(("parallel", "parallel")), name=name)(
            *((proj, branch, dmerged) + (() if fresh else (dproj,))))


def _outproj_loss(merged, w_out, x, target, fnw):
    s, d = x.shape
    tr = 256

    def body(m_ref, w_ref, x_ref, t_ref, fw_ref, dof_ref, dob_ref, loss_ref, g_ref):
        out = x_ref[...] + _nn(m_ref[...], w_ref[...])
        r = lax.rsqrt(jnp.mean(out * out, axis=-1, keepdims=True) + RMS_EPS)
        nrm = out * r
        fw = fw_ref[...]
        err = nrm * fw - t_ref[...]
        dy = err * (1.0 / d)
        gy = dy * fw
        dout = r * (gy - nrm * jnp.mean(gy * nrm, axis=-1, keepdims=True))
        dof_ref[...] = dout
        dob_ref[...] = dout.astype(BF16)

        @pl.when(pl.program_id(0) == 0)
        def _():
            loss_ref[...] = jnp.zeros_like(loss_ref)
            g_ref[...] = jnp.zeros_like(g_ref)

        loss_ref[...] += jnp.sum(jnp.sum(err * err, axis=1, keepdims=True), axis=0, keepdims=True) * (0.5 / d)
        g_ref[...] += jnp.sum(dy * nrm, axis=0, keepdims=True)

    blk = pl.BlockSpec((tr, d), lambda i: (i, 0))
    return pl.pallas_call(
        body, out_shape=(SDS((s, d), F32), SDS((s, d), BF16), SDS((1, LANES), F32), SDS((1, d), F32)), grid=(s // tr,),
        in_specs=[blk, pl.BlockSpec((d, d), lambda i: (0, 0)), blk, blk, pl.BlockSpec((1, d), lambda i: (0, 0))],
        out_specs=(blk, blk, pl.BlockSpec((1, LANES), lambda i: (0, 0)), pl.BlockSpec((1, d), lambda i: (0, 0))),
        compiler_params=_params(("arbitrary",)), name="outproj_loss")(merged, w_out, x, target, fnw)


ELEMWISE_BLOCK_BYTES = 1 << 20


def _row_block(rows, cols, itemsize=4):
    best = None
    for tr in range(16, rows + 1, 16):
        if rows % tr == 0 and tr * cols * itemsize <= ELEMWISE_BLOCK_BYTES:
            best = tr
    return best if best is not None else rows


def _adamw(w, g, m, v, name):
    rows, cols = w.shape
    tr = _row_block(rows, cols)

    def body(w_ref, g_ref, m_ref, v_ref, d_ref, nm_ref, nv_ref):
        gv = g_ref[...]
        nm = ADAM_B1 * m_ref[...] + (1.0 - ADAM_B1) * gv
        nv = ADAM_B2 * v_ref[...] + (1.0 - ADAM_B2) * jnp.square(gv)
        m_hat = nm / (1.0 - ADAM_B1 ** ADAM_STEP)
        v_hat = nv / (1.0 - ADAM_B2 ** ADAM_STEP)
        d_ref[...] = -ADAM_LR * (m_hat / (jnp.sqrt(v_hat) + ADAM_EPS) + ADAM_WD * w_ref[...])
        nm_ref[...] = nm
        nv_ref[...] = nv

    blk = pl.BlockSpec((tr, cols), lambda i: (i, 0))
    out = SDS((rows, cols), F32)
    return pl.pallas_call(
        body, out_shape=(out, out, out), grid=(rows // tr,), in_specs=[blk] * 4, out_specs=(blk,) * 3,
        compiler_params=_params(("parallel",)), name=name)(w, g, m, v)


HBM_SPEC = pl.BlockSpec(memory_space=pl.ANY)


def _position():
    return lax.axis_index("x"), lax.axis_index("y"), lax.axis_index("c")


class _Carry:
    def __init__(self, arrays, out_shapes, sems, start, finish):
        self.arrays, self.out_shapes, self.sems, self.start, self.finish = list(arrays), out_shapes, sems, start, finish

    def sem_shapes(self):
        return [pltpu.SemaphoreType.DMA((k,)) for k in self.sems]


def _run_carry(carry, name):
    n = len(carry.arrays)

    def body(*refs):
        carry.start(refs[:n], refs[n:2 * n], refs[2 * n:])
        carry.finish(refs[:n], refs[n:2 * n], refs[2 * n:])

    return pl.pallas_call(
        body, out_shape=carry.out_shapes, in_specs=[HBM_SPEC] * n, out_specs=[HBM_SPEC] * n,
        scratch_shapes=carry.sem_shapes(),
        compiler_params=pltpu.CompilerParams(has_side_effects=True), name=name)(*carry.arrays)


def _gather_carry(shards):
    n = len(shards)

    def copies(ins, outs, sems):
        send_sems, recv_sems, fsend_sems, frecv_sems = sems
        x, y, c = _position()
        me = 2 * x + y
        peers = [(1 - x, y), (x, 1 - y), (1 - x, 1 - y)]

        def over_ici(t, p, chip):
            px, py = peers[p]
            r2 = ins[t].shape[0] // 2
            return pltpu.make_async_remote_copy(
                src_ref=ins[t].at[pl.ds(c * r2, r2), :], dst_ref=outs[t].at[chip, c], send_sem=send_sems.at[3 * t + p],
                recv_sem=recv_sems.at[3 * t + p], device_id=(px, py, c), device_id_type=MESH)

        def to_sibling(t, p, half):
            px, py = peers[p]
            slab = outs[t].at[2 * px + py, half]
            return pltpu.make_async_remote_copy(
                src_ref=slab, dst_ref=slab, send_sem=fsend_sems.at[3 * t + p], recv_sem=frecv_sems.at[3 * t + p],
                device_id=(x, y, 1 - c), device_id_type=MESH)

        pairs = [(t, p) for t in range(n) for p in range(3)]
        sends = [over_ici(t, p, me) for t, p in pairs]
        lands = [over_ici(t, p, 2 * peers[p][0] + peers[p][1]) for t, p in pairs]
        passed = [to_sibling(t, p, c) for t, p in pairs]
        from_sibling = [to_sibling(t, p, 1 - c) for t, p in pairs]
        return sends, lands, passed, from_sibling

    def start(ins, outs, sems):
        for cp in copies(ins, outs, sems)[0]:
            cp.start()

    def finish(ins, outs, sems):
        sends, lands, passed, from_sibling = copies(ins, outs, sems)
        for land, fwd in zip(lands, passed):
            land.wait_recv()
            fwd.start()
        for cp in from_sibling:
            cp.wait_recv()
        for cp in sends + passed:
            cp.wait_send()

    return _Carry(shards, [SDS((N_CHIPS, 2, a.shape[0] // 2, a.shape[1]), a.dtype) for a in shards], [3 * n] * 4,
                  start, finish)


def _scatter_carry(parts):
    def start(ins, outs, sems):
        for cp in _scatter_copies(ins, outs, *sems)[0]:
            cp.start()

    def finish(ins, outs, sems):
        sends, lands = _scatter_copies(ins, outs, *sems)
        for cp in lands:
            cp.wait_recv()
        for cp in sends:
            cp.wait_send()

    return _Carry(parts, [SDS(a.shape, a.dtype) for a in parts], [3 * len(parts)] * 2, start, finish)


def _with_own(gathered, own, chip):
    full = gathered.reshape((N_CHIPS,) + own.shape)
    return lax.dynamic_update_index_in_dim(full, own, chip, 0)


def _exchange_halves(grads):
    n = len(grads)

    def body(*refs):
        ins, outs = refs[:n], refs[n:2 * n]
        send_sems, recv_sems = refs[2 * n:]
        x, y, c = _position()
        cps = []
        for t in range(n):
            r2 = ins[t].shape[1] // 2
            cps.append(pltpu.make_async_remote_copy(
                src_ref=ins[t].at[:, pl.ds((1 - c) * r2, r2), :], dst_ref=outs[t],
                send_sem=send_sems.at[t], recv_sem=recv_sems.at[t], device_id=(x, y, 1 - c), device_id_type=MESH))
        for cp in cps:
            cp.start()
        for cp in cps:
            cp.wait()

    return pl.pallas_call(
        body, out_shape=[SDS((a.shape[0], a.shape[1] // 2, a.shape[2]), a.dtype) for a in grads],
        in_specs=[HBM_SPEC] * n, out_specs=[HBM_SPEC] * n,
        scratch_shapes=[pltpu.SemaphoreType.DMA((n,)), pltpu.SemaphoreType.DMA((n,))],
        compiler_params=pltpu.CompilerParams(has_side_effects=True), name="reduce_sibling")(*grads)


def _scatter_copies(ins, outs, send_sems, recv_sems):
    x, y, c = _position()
    me = 2 * x + y
    peers = [(1 - x, y), (x, 1 - y), (1 - x, 1 - y)]

    def remote(t, p, src_slab, dst_slab):
        px, py = peers[p]
        return pltpu.make_async_remote_copy(
            src_ref=ins[t].at[src_slab], dst_ref=outs[t].at[dst_slab], send_sem=send_sems.at[3 * t + p],
            recv_sem=recv_sems.at[3 * t + p], device_id=(px, py, c), device_id_type=MESH)

    n = len(ins)
    sends = [remote(t, p, 2 * peers[p][0] + peers[p][1], me) for t in range(n) for p in range(3)]
    lands = [remote(t, p, me, 2 * peers[p][0] + peers[p][1]) for t in range(n) for p in range(3)]
    return sends, lands


def _share_halves(halves):
    n = len(halves)

    def body(*refs):
        ins, outs = refs[:n], refs[n:2 * n]
        send_sems, recv_sems = refs[2 * n:]
        x, y, c = _position()

        def copy(t, slab):
            return pltpu.make_async_remote_copy(
                src_ref=ins[t].at[slab], dst_ref=outs[t].at[slab], send_sem=send_sems.at[t], recv_sem=recv_sems.at[t],
                device_id=(x, y, 1 - c), device_id_type=MESH)

        for t in range(n):
            copy(t, c).start()
        for t in range(n):
            copy(t, 1 - c).wait_recv()
        for t in range(n):
            copy(t, c).wait_send()

    return pl.pallas_call(
        body, out_shape=[SDS(a.shape, a.dtype) for a in halves],
        in_specs=[HBM_SPEC] * n, out_specs=[HBM_SPEC] * n,
        scratch_shapes=[pltpu.SemaphoreType.DMA((n,)), pltpu.SemaphoreType.DMA((n,))],
        input_output_aliases={t: t for t in range(n)},
        compiler_params=pltpu.CompilerParams(has_side_effects=True), name="share_sibling")(*halves)


def _add_sibling(grad, recv, core):
    nch, r2, cols = recv.shape
    tr = _row_block(r2, cols)
    nb = r2 // tr

    def body(c_ref, g_ref, r_ref, o_ref):
        del c_ref
        o_ref[...] = (g_ref[...].astype(F32) + r_ref[...].astype(F32)).astype(BF16)

    return pl.pallas_call(
        body, out_shape=SDS(recv.shape, BF16),
        grid_spec=pltpu.PrefetchScalarGridSpec(
            num_scalar_prefetch=1, grid=(nch, nb),
            in_specs=[pl.BlockSpec((None, tr, cols), lambda j, i, c_ref: (j, c_ref[0] * nb + i, 0)),
                      pl.BlockSpec((None, tr, cols), lambda j, i, c_ref: (j, i, 0))],
            out_specs=pl.BlockSpec((None, tr, cols), lambda j, i, c_ref: (j, i, 0))),
        compiler_params=_params(("parallel", "parallel")), name="add_sibling")(core, grad, recv)


def _add_chips(own, recv, chip_core):
    nch, r2, cols = recv.shape
    tr = _row_block(r2, cols)

    def body(cc_ref, own_ref, *refs):
        p_refs, o_ref = refs[:nch], refs[nch]
        me = cc_ref[0]
        acc = None
        for j in range(nch):
            term = jnp.where(me == j, own_ref[...], p_refs[j][...]).astype(F32)
            acc = term if acc is None else acc + term
        o_ref[...] = acc

    def slab(j):
        return pl.BlockSpec((None, tr, cols), lambda i, cc: (cc[2 + j], i, 0))

    return pl.pallas_call(
        body, out_shape=SDS((2, r2, cols), F32),
        grid_spec=pltpu.PrefetchScalarGridSpec(
            num_scalar_prefetch=1, grid=(r2 // tr,),
            in_specs=[pl.BlockSpec((None, tr, cols), lambda i, cc: (cc[0], i, 0))] + [slab(j) for j in range(nch)],
            out_specs=pl.BlockSpec((None, tr, cols), lambda i, cc: (cc[1], i, 0))),
        compiler_params=_params(("parallel",)), name="add_chips")(chip_core, own, *([recv] * nch))


def _allreduce_small(pack):
    rows = pack.shape[0]

    def body(p_ref, o_ref, buf, send_sems, recv_sems):
        x, y, c = _position()
        me = 4 * x + 2 * y + c
        buf[me] = p_ref[...]

        def copy(dst_dev, slot):
            return pltpu.make_async_remote_copy(
                src_ref=p_ref, dst_ref=buf.at[slot], send_sem=send_sems.at[dst_dev], recv_sem=recv_sems.at[slot],
                device_id=(dst_dev // 4, (dst_dev // 2) % 2, dst_dev % 2), device_id_type=MESH)

        for dev in range(N_DEV):
            @pl.when(dev != me)
            def _():
                copy(dev, me).start()
        for dev in range(N_DEV):
            @pl.when(dev != me)
            def _():
                copy(dev, dev).wait_recv()
        for dev in range(N_DEV):
            @pl.when(dev != me)
            def _():
                copy(dev, me).wait_send()
        acc = buf[0]
        for dev in range(1, N_DEV):
            acc = acc + buf[dev]
        o_ref[...] = acc

    return pl.pallas_call(
        body, out_shape=SDS(pack.shape, F32),
        in_specs=[pl.BlockSpec(memory_space=pltpu.VMEM)], out_specs=pl.BlockSpec(memory_space=pltpu.VMEM),
        scratch_shapes=[pltpu.VMEM((N_DEV, rows, LANES), F32), pltpu.SemaphoreType.DMA((N_DEV,)),
                        pltpu.SemaphoreType.DMA((N_DEV,))],
        compiler_params=pltpu.CompilerParams(has_side_effects=True), name="allreduce_small")(pack)


ATTN_TQ = 256


def _local_step(cfg, x, target, w, to_chips=None, late=None):
    d = cfg.D
    hn = _rmsnorm_fwd(x, w["norm_w"])
    proj = _mm(hn, w["w_main"], "nn", BF16, "proj_main", carry=late[0] if late else None)
    if late:
        proj, arrived = proj
        w = {**w, **late[1](arrived)}
    dt_raw = _mm(hn, w["w_dt"], "nn", F32, "proj_dt")
    slopes = _slopes(cfg.H)
    near = _Pass(ATTN_TQ, DILATED_PATTERNS[:-1], 1, cfg.S)
    far = _Pass(LANES, DILATED_PATTERNS[-1:], DEINT, cfg.S // DEINT)
    tab_near, tab_far = _attn_tables(near), _attn_tables(far)
    cols_near, cols_far = (cfg.OQ, cfg.OK, cfg.OV), (0, d, 2 * d)
    qkv_far = _deinterleave(proj, 0, 3 * d, "attn_deinterleave")
    o_1, lse_1 = _attn_fwd(cfg, near, proj, cols_near, tab_near, slopes, "attn_fwd_near")
    o_2, lse_2 = _attn_fwd(cfg, far, qkv_far, cols_far, tab_far, slopes, "attn_fwd_far")
    o_a, oag, lse = _attn_merge(cfg, proj, o_1, lse_1, o_2, lse_2)
    xact = _conv_fwd(cfg, proj, w["conv_w"], w["conv_b"])
    e = _expansion_matrix(cfg)
    y, y2, states = _ssd_fwd(cfg, xact, dt_raw, proj, w["dt_bias"], w["a_log"], w["d_skip"], w["ssm_norm_w"], e)
    a_br = _mm(oag, w["w_attn"], "nn", BF16, "branch_attn")
    s_br = _mm(y2, w["w_ssm"], "nn", BF16, "branch_ssm")
    merged = _merge_fwd(cfg, proj, a_br, s_br)
    dout_f, dout_b, loss_row, g_fnw = _outproj_loss(merged, w["w_out"], x, target, w["final_norm_w"])

    dmerged = _mm(dout_b, w["w_out"], "nt", BF16, "d_merged")
    g_w_out = _mm(merged, dout_b, "tn", BF16, "g_w_out")
    dproj, da_br = _merge_bwd(cfg, proj, a_br, dmerged, cfg.OGA, None, "merge_bwd_attn")
    dproj, ds_br = _merge_bwd(cfg, proj, s_br, dmerged, cfg.OGS, dproj, "merge_bwd_ssm")
    doag = _mm(da_br, w["w_attn"], "nt", BF16, "d_oag")
    g_w_attn = _mm(oag, da_br, "tn", BF16, "g_w_attn")
    dy2 = _mm(ds_br, w["w_ssm"], "nt", BF16, "d_y2")
    g_w_ssm = _mm(y2, ds_br, "tn", BF16, "g_w_ssm")
    dproj, dxact, ddt, g_snw, g_dtb, g_alog, g_dsk = _ssd_bwd(
        cfg, xact, dt_raw, proj, y, dy2, states, w["dt_bias"], w["a_log"], w["d_skip"], w["ssm_norm_w"], e, dproj)
    dproj, g_cw, g_cb = _conv_bwd(cfg, proj, dxact, w["conv_w"], w["conv_b"], dproj)
    dproj, do, do_far, dl, dl_far, lse_far = _attn_bwd_prep(cfg, proj, o_a, doag, lse, dproj)
    g_near = _attn_bwd(cfg, near, proj, cols_near, do, lse, dl, tab_near, slopes, "attn_bwd_near")
    g_far = _attn_bwd(cfg, far, qkv_far, cols_far, do_far, lse_far, dl_far, tab_far, slopes, "attn_bwd_far")
    for g_1, g_2, col0, nm in zip(g_near, g_far, cols_near, ("attn_dq", "attn_dk", "attn_dv")):
        dproj = _attn_grad_sum(cfg, g_1, g_2, col0, dproj, nm)
    ddt_b = ddt.astype(BF16)
    g_w_main = _mm(hn, dproj, "tn", BF16, "g_w_main")
    g_w_dt = _mm(hn, ddt_b, "tn", BF16, "g_w_dt")
    grads = dict(w_main=g_w_main, w_dt=g_w_dt, conv_w=g_cw, conv_b=g_cb, dt_bias=g_dtb, a_log=g_alog,
                 d_skip=g_dsk, ssm_norm_w=g_snw, w_attn=g_w_attn, w_ssm=g_w_ssm, w_out=g_w_out, final_norm_w=g_fnw)
    sent = to_chips(grads) if to_chips is not None else ()
    dhn = _mm(dproj, w["w_main"], "nt", F32, "d_hn", tk=1024, carry=_scatter_carry(sent) if sent else None)
    landed = ()
    if sent:
        dhn, landed = dhn
    dhn_dt = _mm(ddt_b, w["w_dt"], "nt", F32, "d_hn_dt")
    grad_x, grads["norm_w"] = _rmsnorm_bwd(x, w["norm_w"], dhn, dhn_dt, dout_f)
    return loss_row, grad_x, grads, sent, landed


def _pad_lanes(v):
    return jnp.pad(v, ((0, 0), (0, LANES - v.shape[1])))


def _cut(lo, hi, a, b):
    a, b = max(lo, a), min(hi, b)
    return (a, b) if a < b else None


def _main_from_shards(cfg, shards):
    per = cfg.N_IN // len(shards)
    main, dt = [], []
    for j, sh in enumerate(shards):
        lo, hi = j * per, (j + 1) * per
        for dst, rng in ((main, (0, cfg.OGA)), (dt, (cfg.OGA, cfg.OGA + cfg.NH)), (main, (cfg.OGA + cfg.NH, cfg.N_IN))):
            c = _cut(lo, hi, *rng)
            if c is not None:
                dst.append(sh[:, c[0] - lo:c[1] - lo])
    return jnp.concatenate(main, axis=1), _pad_lanes(jnp.concatenate(dt, axis=1))


def _shards_from_main(cfg, g_main, g_dt, n):
    per = cfg.N_IN // n
    out = []
    for j in range(n):
        lo, hi = j * per, (j + 1) * per
        parts = []
        for src, off, rng in ((g_main, 0, (0, cfg.OGA)), (g_dt, cfg.OGA, (cfg.OGA, cfg.OGA + cfg.NH)),
                              (g_main, cfg.NH, (cfg.OGA + cfg.NH, cfg.N_IN))):
            c = _cut(lo, hi, *rng)
            if c is not None:
                parts.append(src[:, c[0] - off:c[1] - off])
        out.append(jnp.concatenate(parts, axis=1) if len(parts) > 1 else parts[0])
    return out


def _full_weights(cfg, norm_w, w_in_shards, conv_w, conv_b, dt_bias, a_log, d_skip, ssm_norm_w, w_attn, w_ssm, w_out, fnw):
    w_main, w_dt = _main_from_shards(cfg, w_in_shards)
    return dict(norm_w=norm_w, w_main=w_main.astype(BF16), w_dt=w_dt.astype(BF16), conv_w=conv_w, conv_b=conv_b,
                dt_bias=_pad_lanes(dt_bias), a_log=_pad_lanes(a_log), d_skip=_pad_lanes(d_skip), ssm_norm_w=ssm_norm_w,
                final_norm_w=fnw, **{k: v.astype(BF16) for k, v in (("w_attn", w_attn), ("w_ssm", w_ssm), ("w_out", w_out))
                                     if v is not None})


def _grad_w_in(cfg, grads):
    return _shards_from_main(cfg, grads["w_main"], grads["w_dt"], 1)[0]


def kernel(x, norm_w, w_in, conv_w, conv_b, dt_bias, a_log, d_skip, ssm_norm_w, w_attn_branch, w_ssm_branch, w_out, final_norm_w, loss_target, m_norm_w, m_w_in, m_conv_w, m_conv_b, m_dt_bias, m_a_log, m_d_skip, m_ssm_norm_w, m_w_attn_branch, m_w_ssm_branch, m_w_out, m_final_norm_w, v_norm_w, v_w_in, v_conv_w, v_conv_b, v_dt_bias, v_a_log, v_d_skip, v_ssm_norm_w, v_w_attn_branch, v_w_ssm_branch, v_w_out, v_final_norm_w):
    cfg = _Cfg(x.shape[1], x.shape[2])
    d, si, cd, nh = cfg.D, cfg.SI, cfg.CD, cfg.NH
    chip = 2 * lax.axis_index("x") + lax.axis_index("y")
    core = lax.axis_index("c").astype(jnp.int32).reshape(1)
    slabs = jnp.arange(N_CHIPS, dtype=jnp.int32)
    chip_core = jnp.concatenate([chip.astype(jnp.int32).reshape(1), core,
                                 jnp.where(slabs == chip, (slabs + 1) % N_CHIPS, slabs)])

    own = [w_in[0].astype(BF16), conv_w[0].reshape(4 * CONV_K, -1)]
    a_in, a_cw = [_with_own(g, o, chip) for g, o in zip(_run_carry(_gather_carry(own), "gather_weights"), own)]
    conv_w_full = a_cw.reshape(N_CHIPS, CONV_K, cd // N_CHIPS).transpose(1, 0, 2).reshape(CONV_K, cd)
    w = _full_weights(cfg, norm_w, [a_in[j] for j in range(N_CHIPS)], conv_w_full, conv_b, dt_bias, a_log, d_skip,
                      ssm_norm_w, None, None, None, final_norm_w.reshape(1, d))
    own_late = [w_attn_branch[0].astype(BF16), w_ssm_branch[0].astype(BF16), w_out[0].astype(BF16)]

    def late_weights(arrived):
        a_attn, a_ssm, a_out = [_with_own(g, o, chip) for g, o in zip(arrived, own_late)]
        return dict(w_attn=a_attn.reshape(d, d), w_ssm=a_ssm.reshape(si, d), w_out=a_out.reshape(d, d))

    def to_chips(grads):
        by_chip = [jnp.stack(_shards_from_main(cfg, grads["w_main"], grads["w_dt"], N_CHIPS)),
                   grads["w_attn"].reshape(N_CHIPS, d // N_CHIPS, d),
                   grads["w_ssm"].reshape(N_CHIPS, si // N_CHIPS, d),
                   grads["w_out"].reshape(N_CHIPS, d // N_CHIPS, d)]
        from_sibling = _exchange_halves(by_chip)
        return [_add_sibling(g, r, core) for g, r in zip(by_chip, from_sibling)]

    loss_row, grad_x, grads, chip_sums, from_chips = _local_step(
        cfg, x[0], loss_target[0], w, to_chips, (_gather_carry(own_late), late_weights))
    halves = [_add_chips(o, p, chip_core) for o, p in zip(chip_sums, from_chips)]
    g_in, g_attn, g_ssm, g_out = [h.reshape(2 * h.shape[1], h.shape[2]) for h in _share_halves(halves)]

    small = [loss_row, grads["norm_w"], grads["conv_b"], grads["dt_bias"], grads["a_log"], grads["d_skip"],
             grads["ssm_norm_w"], grads["final_norm_w"], grads["conv_w"].reshape(1, CONV_K * cd)]
    sizes = [a.shape[1] for a in small]
    total = sum(sizes)
    rows = -(-total // (8 * LANES)) * 8
    flat = jnp.pad(jnp.concatenate(small, axis=1), ((0, 0), (0, rows * LANES - total)))
    red = _allreduce_small(flat.reshape(rows, LANES)).reshape(1, rows * LANES)
    offs = [sum(sizes[:i]) for i in range(len(sizes))]
    loss_r, g_nw, g_cb, g_dtb, g_alog, g_dsk, g_snw, g_fnw, g_cw_flat = [
        red[:, o:o + n] for o, n in zip(offs, sizes)]
    loss = loss_r[0, 0]
    g_dtb, g_alog, g_dsk = g_dtb[:, :nh], g_alog[:, :nh], g_dsk[:, :nh]
    cshard = cd // N_CHIPS
    g_cw = lax.dynamic_slice_in_dim(g_cw_flat.reshape(CONV_K, cd), chip * cshard, cshard, axis=1)

    upd = {}
    for name, wv, gv, mv, vv in [("w_in", w_in[0], g_in, m_w_in[0], v_w_in[0]),
                                 ("w_attn", w_attn_branch[0], g_attn, m_w_attn_branch[0], v_w_attn_branch[0]),
                                 ("w_ssm", w_ssm_branch[0], g_ssm, m_w_ssm_branch[0], v_w_ssm_branch[0]),
                                 ("w_out", w_out[0], g_out, m_w_out[0], v_w_out[0])]:
        upd[name] = _adamw(wv, gv, mv, vv, "adamw_" + name)
    names = ["norm_w", "conv_w", "conv_b", "dt_bias", "a_log", "d_skip", "ssm_norm_w", "final_norm_w"]
    ws = [norm_w, conv_w[0].reshape(1, -1), conv_b, dt_bias, a_log, d_skip, ssm_norm_w, final_norm_w.reshape(1, d)]
    gs = [g_nw, g_cw.reshape(1, -1), g_cb, g_dtb, g_alog, g_dsk, g_snw, g_fnw]
    ms = [m_norm_w, m_conv_w[0].reshape(1, -1), m_conv_b, m_dt_bias, m_a_log, m_d_skip, m_ssm_norm_w,
          m_final_norm_w.reshape(1, d)]
    vs = [v_norm_w, v_conv_w[0].reshape(1, -1), v_conv_b, v_dt_bias, v_a_log, v_d_skip, v_ssm_norm_w,
          v_final_norm_w.reshape(1, d)]
    ssz = [a.shape[1] for a in ws]
    stot = sum(ssz)
    srows = -(-stot // (8 * LANES)) * 8

    def pack(parts):
        return jnp.pad(jnp.concatenate(parts, axis=1), ((0, 0), (0, srows * LANES - stot))).reshape(srows, LANES)

    packed = _adamw(pack(ws), pack(gs), pack(ms), pack(vs), "adamw_small")
    soffs = [sum(ssz[:i]) for i in range(len(ssz))]
    for k, nm in enumerate(names):
        upd[nm] = tuple(p.reshape(1, srows * LANES)[:, soffs[k]:soffs[k] + ssz[k]] for p in packed)

    shapes = dict(norm_w=norm_w.shape, w_in=w_in.shape, conv_w=conv_w.shape, conv_b=conv_b.shape, dt_bias=dt_bias.shape,
                  a_log=a_log.shape, d_skip=d_skip.shape, ssm_norm_w=ssm_norm_w.shape, w_attn=w_attn_branch.shape,
                  w_ssm=w_ssm_branch.shape, w_out=w_out.shape, final_norm_w=final_norm_w.shape)
    order = ["norm_w", "w_in", "conv_w", "conv_b", "dt_bias", "a_log", "d_skip", "ssm_norm_w", "w_attn", "w_ssm",
             "w_out", "final_norm_w"]
    gradv = dict(norm_w=g_nw, w_in=g_in, conv_w=g_cw, conv_b=g_cb, dt_bias=g_dtb, a_log=g_alog, d_skip=g_dsk,
                 ssm_norm_w=g_snw, w_attn=g_attn, w_ssm=g_ssm, w_out=g_out, final_norm_w=g_fnw)
    outs = [loss, grad_x[None]]
    outs += [gradv[n].reshape(shapes[n]) for n in order]
    for k in range(3):
        outs += [upd[n][k].reshape(shapes[n]) for n in order]
    return tuple(outs)
```

```python
import functools
import math

import jax
import jax.numpy as jnp
from jax import lax
from jax.experimental import pallas as pl
from jax.experimental.pallas import tpu as pltpu

F32 = jnp.float32
BF16 = jnp.bfloat16
SDS = jax.ShapeDtypeStruct

RMS_EPS = 1e-6
LANES = 128
CHUNK = 128
SSM_HEAD_DIM = 64
SSM_GROUPS = 8
SSM_STATE = 128
CONV_K = 4
ATTN_HEAD_DIM = 128
DILATED_PATTERNS = ((128, 1), (512, 4), (2048, 16))
ATTN_WINDOW = max(w for w, _ in DILATED_PATTERNS)
NEG = -1e30
VMEM_LIMIT = 56 * 1024 * 1024
ADAM_LR, ADAM_B1, ADAM_B2, ADAM_EPS, ADAM_WD, ADAM_STEP = 0.001, 0.9, 0.999, 1e-08, 0.01, 10
MESH = pl.DeviceIdType.MESH
N_CHIPS = 4
N_DEV = 8


class _Cfg:
    def __init__(self, s, d):
        self.S, self.D = s, d
        self.H = d // ATTN_HEAD_DIM
        self.SI = 2 * d
        self.NH = self.SI // SSM_HEAD_DIM
        self.HPG = self.NH // SSM_GROUPS
        self.GW = self.HPG * SSM_HEAD_DIM
        self.BC = SSM_GROUPS * SSM_STATE
        self.CD = self.SI + 2 * self.BC
        self.OQ, self.OK, self.OV, self.OZA = 0, d, 2 * d, 3 * d
        self.OZS = 4 * d
        self.OXBC = self.OZS + self.SI
        self.OGA = self.OXBC + self.CD
        self.OGS = self.OGA + d
        self.NM = self.OGS + d
        self.N_IN = self.NM + self.NH
        assert self.GW % LANES == 0 and self.NH <= LANES and s % 512 == 0 and d % 512 == 0


def _params(sem=None):
    return pltpu.CompilerParams(dimension_semantics=sem, vmem_limit_bytes=VMEM_LIMIT)


def _sigmoid(x):
    return 1.0 / (1.0 + jnp.exp(-x))


def _softplus(x):
    u = jnp.exp(-jnp.abs(x))
    l1p = jnp.where(u < 1e-3, u * (1.0 - u * (0.5 - u * (1.0 / 3.0))), jnp.log(1.0 + u))
    return jnp.maximum(x, 0.0) + l1p


def _nt(a, b):
    return lax.dot_general(a, b, (((1,), (1,)), ((), ())), preferred_element_type=F32)


def _tn(a, b):
    return lax.dot_general(a, b, (((0,), (0,)), ((), ())), preferred_element_type=F32)


def _nn(a, b):
    return jnp.dot(a, b, preferred_element_type=F32)


def _tile(n, target):
    if n <= target:
        return n
    best = None
    for t in range(LANES, target + 1, LANES):
        if n % t == 0:
            best = t
    assert best is not None, (n, target)
    return best


MM_TK = {"nn": 2048, "nt": 2048, "tn": 1024}


def _mm(a, b, dims, out_dtype, name, tm=1024, tn=2048, tk=None, init=None, carry=None):
    tk = MM_TK[dims] if tk is None else tk
    if dims == "nn":
        (m, k), (k2, n) = a.shape, b.shape
    elif dims == "nt":
        (m, k), (n, k2) = a.shape, b.shape
    else:
        (k, m), (k2, n) = a.shape, b.shape
    assert k == k2
    tm, tn, tk = _tile(m, tm), _tile(n, tn), _tile(k, tk)
    nk = k // tk
    if dims == "tn":
        a_spec = pl.BlockSpec((tk, tm), lambda i, j, kk: (kk, i))
    else:
        a_spec = pl.BlockSpec((tm, tk), lambda i, j, kk: (i, kk))
    if dims == "nt":
        b_spec = pl.BlockSpec((tn, tk), lambda i, j, kk: (j, kk))
    else:
        b_spec = pl.BlockSpec((tk, tn), lambda i, j, kk: (kk, j))
    o_spec = pl.BlockSpec((tm, tn), lambda i, j, kk: (i, j))
    op = {"nn": _nn, "nt": _nt, "tn": _tn}[dims]
    has_init = init is not None
    nx = len(carry.arrays) if carry is not None else 0
    ni, nj = m // tm, n // tn

    def body(*refs):
        a_ref, b_ref = refs[0], refs[1]
        i_ref = refs[2] if has_init else None
        x_in = refs[2 + has_init:2 + has_init + nx]
        o_ref = refs[2 + has_init + nx]
        x_out = refs[3 + has_init + nx:3 + has_init + 2 * nx]
        acc = refs[3 + has_init + 2 * nx]
        x_sems = refs[4 + has_init + 2 * nx:]
        i, j, kk = pl.program_id(0), pl.program_id(1), pl.program_id(2)

        if nx:
            @pl.when((i == 0) & (j == 0) & (kk == 0))
            def _():
                carry.start(x_in, x_out, x_sems)

        prod = lambda: op(a_ref[...], b_ref[...])
        with_init = (lambda p: p + i_ref[...].astype(F32)) if has_init else (lambda p: p)
        if nk == 1:
            o_ref[...] = with_init(prod()).astype(out_dtype)
        else:
            @pl.when(kk == 0)
            def _():
                acc[...] = with_init(prod())

            @pl.when((kk > 0) & (kk < nk - 1))
            def _():
                acc[...] += prod()

            @pl.when(kk == nk - 1)
            def _():
                o_ref[...] = (acc[...] + prod()).astype(out_dtype)

        if nx:
            @pl.when((i == ni - 1) & (j == nj - 1) & (kk == nk - 1))
            def _():
                carry.finish(x_in, x_out, x_sems)

    in_specs = [a_spec, b_spec] + ([o_spec] if has_init else []) + [HBM_SPEC] * nx
    args = (a, b) + ((init,) if has_init else ()) + (tuple(carry.arrays) if nx else ())
    sems = carry.sem_shapes() if nx else []
    outs = pl.pallas_call(
        body, out_shape=[SDS((m, n), out_dtype)] + (carry.out_shapes if nx else []), grid=(ni, nj, nk),
        in_specs=in_specs, out_specs=[o_spec] + [HBM_SPEC] * nx,
        scratch_shapes=[pltpu.VMEM((tm, tn) if nk > 1 else (8, LANES), F32)] + sems,
        compiler_params=_params(("arbitrary",) * 3 if nx else ("parallel", "parallel", "arbitrary")), name=name)(*args)
    return (outs[0], outs[1:]) if nx else outs[0]


def _rmsnorm_fwd(x, w):
    s, d = x.shape
    tr = 256

    def body(x_ref, w_ref, o_ref):
        xv = x_ref[...]
        r = lax.rsqrt(jnp.mean(xv * xv, axis=-1, keepdims=True) + RMS_EPS)
        o_ref[...] = (xv * r * w_ref[...]).astype(BF16)

    return pl.pallas_call(
        body, out_shape=SDS((s, d), BF16), grid=(s // tr,),
        in_specs=[pl.BlockSpec((tr, d), lambda i: (i, 0)), pl.BlockSpec((1, d), lambda i: (0, 0))],
        out_specs=pl.BlockSpec((tr, d), lambda i: (i, 0)),
        compiler_params=_params(("parallel",)), name="rmsnorm_fwd")(x, w)


def _rmsnorm_bwd(x, w, dhn_a, dhn_b, dout):
    s, d = x.shape
    tr = 256

    def body(x_ref, w_ref, dh_ref, dh2_ref, do_ref, gx_ref, gw_ref):
        xv = x_ref[...]
        r = lax.rsqrt(jnp.mean(xv * xv, axis=-1, keepdims=True) + RMS_EPS)
        nrm = xv * r
        dh = dh_ref[...] + dh2_ref[...]
        gy = dh * w_ref[...]
        gx_ref[...] = do_ref[...] + r * (gy - nrm * jnp.mean(gy * nrm, axis=-1, keepdims=True))

        @pl.when(pl.program_id(0) == 0)
        def _():
            gw_ref[...] = jnp.zeros_like(gw_ref)

        gw_ref[...] += jnp.sum(dh * nrm, axis=0, keepdims=True)

    blk = pl.BlockSpec((tr, d), lambda i: (i, 0))
    row = pl.BlockSpec((1, d), lambda i: (0, 0))
    return pl.pallas_call(
        body, out_shape=(SDS((s, d), F32), SDS((1, d), F32)), grid=(s // tr,),
        in_specs=[blk, row, blk, blk, blk], out_specs=(blk, row),
        compiler_params=_params(("arbitrary",)), name="rmsnorm_bwd")(x, w, dhn_a, dhn_b, dout)


DEINT = DILATED_PATTERNS[-1][1]
DEINT_ROWS = DEINT * LANES


class _Pass:
    def __init__(self, tq, patterns, unit, seg_len):
        self.tq, self.patterns, self.unit, self.seg_len = tq, patterns, unit, seg_len
        self.win = max(w for w, _ in patterns) // unit
        self.w = self.win + tq
        assert self.win % tq == 0


def _attn_tables(ps):
    i = jnp.arange(ps.tq, dtype=jnp.int32)[:, None]
    j = jnp.arange(ps.w, dtype=jnp.int32)[None, :]
    delta = (i + ps.win - j) * ps.unit
    n = jnp.zeros((ps.tq, ps.w), F32)
    for window, dil in ps.patterns:
        n = n + ((delta >= 0) & (delta <= window) & (delta % dil == 0)).astype(F32)
    logn = jnp.where(n > 0, jnp.log(jnp.maximum(n, 1.0)), NEG)
    return logn, jnp.maximum(delta, 0).astype(F32)


def _slopes(h):
    s = jnp.asarray([2.0 ** (-8.0 * (i + 1) / h) for i in range(h)], F32)
    return jnp.broadcast_to(s[:, None, None], (h, 1, LANES))


def _masked_logn(ps, logn_ref, start):
    col = lax.broadcasted_iota(jnp.int32, (ps.tq, ps.w), 1)
    return jnp.where(col >= ps.win - lax.rem(start, ps.seg_len), logn_ref[...], NEG)


def _head_cols(hh):
    return slice(hh * ATTN_HEAD_DIM, (hh + 1) * ATTN_HEAD_DIM)


def _head_window(refs, cs):
    return jnp.concatenate([r[:, cs] for r in refs], axis=0)


def _head_scores(q_ref, kw, cs, base, dist_ref, slope_ref, hh):
    return _nt(q_ref[:, cs], kw) * (ATTN_HEAD_DIM ** -0.5) + (base - slope_ref[hh][0:1, 0:1] * dist_ref[...])


def _lane_of(stat, hh):
    lane = lax.broadcasted_iota(jnp.int32, stat.shape, 1)
    return jnp.sum(jnp.where(lane == hh, stat, 0.0), axis=1, keepdims=True)


def _window_specs(ps, d, col, nb):
    nprev = ps.win // ps.tq
    return [pl.BlockSpec((ps.tq, d), lambda i, b=b: (jnp.maximum(jnp.minimum(i, nb - 1) - (nprev - b), 0), col))
            for b in range(nprev + 1)]


def _attn_fwd(cfg, ps, qkv, cols, tables, slopes, name):
    s, h, d = cfg.S, cfg.H, cfg.D
    tq, nw = ps.tq, ps.win // ps.tq + 1
    nb = s // tq
    logn, dist = tables
    qc, kc, vc = [c // d for c in cols]

    def body(*refs):
        q_ref, k_refs, v_refs = refs[0], refs[1:1 + nw], refs[1 + nw:1 + 2 * nw]
        logn_ref, dist_ref, slope_ref, o_ref, lse_ref = refs[1 + 2 * nw:]
        base = _masked_logn(ps, logn_ref, pl.program_id(0) * tq)
        lane = lax.broadcasted_iota(jnp.int32, (tq, LANES), 1)

        lse = jnp.zeros((tq, LANES), F32)
        for hh in range(h):
            cs = _head_cols(hh)
            sc = _head_scores(q_ref, _head_window(k_refs, cs), cs, base, dist_ref, slope_ref, hh)
            m = jnp.max(sc, axis=1, keepdims=True)
            p = jnp.exp(sc - m)
            l = jnp.sum(p, axis=1, keepdims=True)
            o_ref[:, cs] = (_nn(p.astype(BF16), _head_window(v_refs, cs)) / l).astype(BF16)
            lse = jnp.where(lane == hh, m + jnp.log(l), lse)
        lse_ref[...] = lse

    tab = pl.BlockSpec((tq, ps.w), lambda i: (0, 0))
    return pl.pallas_call(
        body, out_shape=(SDS((s, d), BF16), SDS((s, LANES), F32)), grid=(nb,),
        in_specs=[pl.BlockSpec((tq, d), lambda i: (i, qc))] + _window_specs(ps, d, kc, nb) + _window_specs(ps, d, vc, nb)
        + [tab, tab, pl.BlockSpec((h, 1, LANES), lambda i: (0, 0, 0))],
        out_specs=(pl.BlockSpec((tq, d), lambda i: (i, 0)), pl.BlockSpec((tq, LANES), lambda i: (i, 0))),
        compiler_params=_params(("parallel",)), name=name)(*([qkv] * (1 + 2 * nw)), logn, dist, slopes)


def _attn_bwd(cfg, ps, qkv, cols, do, lse, delta, tables, slopes, name):
    s, h, d = cfg.S, cfg.H, cfg.D
    tq, nprev = ps.tq, ps.win // ps.tq
    nw = nprev + 1
    nb = s // tq
    logn, dist = tables
    qc, kc, vc = [c // d for c in cols]
    scale = ATTN_HEAD_DIM ** -0.5

    def body(*refs):
        q_ref, k_refs, v_refs = refs[0], refs[1:1 + nw], refs[1 + nw:1 + 2 * nw]
        do_ref, lse_ref, dl_ref, logn_ref, dist_ref, slope_ref, dq_ref, dk_ref, dv_ref, ck, cv = refs[1 + 2 * nw:]
        i = pl.program_id(0)
        slot = lambda b: lax.rem(i + b, nprev)

        @pl.when(i == 0)
        def _():
            ck[...] = jnp.zeros_like(ck)
            cv[...] = jnp.zeros_like(cv)

        @pl.when(i < nb)
        def _():
            base = _masked_logn(ps, logn_ref, i * tq)
            lse_all, dl_all = lse_ref[...], dl_ref[...]

            for hh in range(h):
                cs = _head_cols(hh)
                kw, vw = _head_window(k_refs, cs), _head_window(v_refs, cs)
                sc = _head_scores(q_ref, kw, cs, base, dist_ref, slope_ref, hh)
                p = jnp.exp(sc - lse_all[:, hh:hh + 1])
                dob = do_ref[:, cs]
                ds = (p * (_nt(dob, vw) - dl_all[:, hh:hh + 1]) * scale).astype(BF16)
                dq_ref[:, cs] = _nn(ds, kw).astype(BF16)
                dkw = _tn(ds, q_ref[:, cs])
                dvw = _tn(p.astype(BF16), dob)
                dk_ref[:, cs] = ck[slot(0), :, cs] + dkw[0:tq]
                dv_ref[:, cs] = cv[slot(0), :, cs] + dvw[0:tq]
                for b in range(1, nprev):
                    ck[slot(b), :, cs] += dkw[b * tq:(b + 1) * tq]
                    cv[slot(b), :, cs] += dvw[b * tq:(b + 1) * tq]
                ck[slot(0), :, cs] = dkw[nprev * tq:]
                cv[slot(0), :, cs] = dvw[nprev * tq:]

        @pl.when(i >= nb)
        def _():
            dk_ref[...] = ck[slot(0)]
            dv_ref[...] = cv[slot(0)]

    here = lambda i: jnp.minimum(i, nb - 1)
    blk = pl.BlockSpec((tq, d), lambda i: (here(i), 0))
    stat = pl.BlockSpec((tq, LANES), lambda i: (here(i), 0))
    late = pl.BlockSpec((tq, d), lambda i: (jnp.maximum(i - nprev, 0), 0))
    tab = pl.BlockSpec((tq, ps.w), lambda i: (0, 0))
    return pl.pallas_call(
        body, out_shape=(SDS((s, d), BF16), SDS((s, d), F32), SDS((s, d), F32)), grid=(nb + nprev,),
        in_specs=[pl.BlockSpec((tq, d), lambda i: (here(i), qc))] + _window_specs(ps, d, kc, nb)
        + _window_specs(ps, d, vc, nb) + [blk, stat, stat, tab, tab, pl.BlockSpec((h, 1, LANES), lambda i: (0, 0, 0))],
        out_specs=(blk, late, late),
        scratch_shapes=[pltpu.VMEM((nprev, tq, d), F32), pltpu.VMEM((nprev, tq, d), F32)],
        compiler_params=_params(("arbitrary",)), name=name)(
            *([qkv] * (1 + 2 * nw)), do, lse, delta, logn, dist, slopes)


def _by_residue(a):
    return a.reshape(DEINT, a.shape[0] // DEINT, a.shape[1])


def _deint_spec(colblock):
    return pl.BlockSpec((DEINT, LANES, LANES), lambda b, j: (0, b, colblock(j)))


def _deint_rows(scr, out_ref, dtype):
    for r in range(DEINT):
        out_ref[r] = scr[pl.ds(r, LANES, stride=DEINT), :].astype(dtype)


def _int_rows(in_ref, scr):
    for r in range(DEINT):
        scr[pl.ds(r, LANES, stride=DEINT), :] = in_ref[r].astype(F32)


WIDE = 4 * LANES


def _wide_spec():
    return pl.BlockSpec((DEINT, LANES, WIDE), lambda b, j: (0, b, j))


def _deinterleave(x, col0, ncols, name):
    s = x.shape[0]
    c0 = col0 // WIDE

    def body(x_ref, o_ref, scr):
        for t in range(WIDE // LANES):
            cs = slice(t * LANES, (t + 1) * LANES)
            scr[t] = x_ref[:, cs].astype(F32)
            for r in range(DEINT):
                o_ref[r, :, cs] = scr.at[t][pl.ds(r, LANES, stride=DEINT), :].astype(x.dtype)

    out = pl.pallas_call(
        body, out_shape=SDS((DEINT, s // DEINT, ncols), x.dtype), grid=(s // DEINT_ROWS, ncols // WIDE),
        in_specs=[pl.BlockSpec((DEINT_ROWS, WIDE), lambda b, j: (b, c0 + j))],
        out_specs=_wide_spec(),
        scratch_shapes=[pltpu.VMEM((WIDE // LANES, DEINT_ROWS, LANES), F32)],
        compiler_params=_params(("parallel", "parallel")), name=name)(x)
    return out.reshape(s, ncols)


def _attn_merge(cfg, proj, o_1, lse_1, o_2, lse_2):
    s, h = cfg.S, cfg.H
    zb = cfg.OZA // LANES
    rows = DEINT_ROWS

    def body(o1_ref, l1_ref, o2_ref, l2_ref, z_ref, o_ref, og_ref, lse_ref, so, sl):
        hh = pl.program_id(1)
        _int_rows(o2_ref, so)

        @pl.when(hh == 0)
        def _():
            _int_rows(l2_ref, sl)

        l1, l2 = _lane_of(l1_ref[...], hh), _lane_of(sl[...], hh)
        mx = jnp.maximum(l1, l2)
        w1, w2 = jnp.exp(l1 - mx), jnp.exp(l2 - mx)
        den = w1 + w2
        o = (w1 * o1_ref[...].astype(F32) + w2 * so[...]) / den
        z = z_ref[...].astype(F32)
        o_ref[...] = o.astype(BF16)
        og_ref[...] = (o * (z * _sigmoid(z))).astype(BF16)

        @pl.when(hh == 0)
        def _():
            lse_ref[...] = jnp.zeros_like(lse_ref)

        lane = lax.broadcasted_iota(jnp.int32, (rows, LANES), 1)
        lse_ref[...] = jnp.where(lane == hh, mx + jnp.log(den), lse_ref[...])

    blk = pl.BlockSpec((rows, LANES), lambda b, j: (b, j))
    return pl.pallas_call(
        body, out_shape=(SDS((s, cfg.D), BF16), SDS((s, cfg.D), BF16), SDS((s, LANES), F32)),
        grid=(s // rows, h),
        in_specs=[blk, pl.BlockSpec((rows, LANES), lambda b, j: (b, 0)), _deint_spec(lambda j: j),
                  _deint_spec(lambda j: 0), pl.BlockSpec((rows, LANES), lambda b, j: (b, zb + j))],
        out_specs=(blk, blk, pl.BlockSpec((rows, LANES), lambda b, j: (b, 0))),
        scratch_shapes=[pltpu.VMEM((rows, LANES), F32), pltpu.VMEM((rows, LANES), F32)],
        compiler_params=_params(("parallel", "arbitrary")), name="attn_merge")(
            o_1, lse_1, _by_residue(o_2), _by_residue(lse_2), proj)


def _attn_bwd_prep(cfg, proj, o_a, doag, lse, dproj):
    s, h = cfg.S, cfg.H
    zb = cfg.OZA // LANES
    rows = DEINT_ROWS

    def body(o_ref, dg_ref, z_ref, lse_ref, dp_in, dz_ref, do_ref, do2_ref, dl_ref, dl2_ref, lse2_ref, scr):
        del dp_in
        hh = pl.program_id(1)
        z = z_ref[...].astype(F32)
        sg = _sigmoid(z)
        o = o_ref[...].astype(F32)
        dg = dg_ref[...].astype(F32)
        do = dg * (z * sg)
        dz_ref[...] = (dg * o * (sg * (1.0 + z * (1.0 - sg)))).astype(BF16)
        do_ref[...] = do.astype(BF16)
        scr[...] = do
        _deint_rows(scr, do2_ref, BF16)

        @pl.when(hh == 0)
        def _():
            dl_ref[...] = jnp.zeros_like(dl_ref)

        lane = lax.broadcasted_iota(jnp.int32, (rows, LANES), 1)
        dl_ref[...] = jnp.where(lane == hh, jnp.sum(do * o, axis=1, keepdims=True), dl_ref[...])

        @pl.when(hh == h - 1)
        def _():
            scr[...] = dl_ref[...]
            _deint_rows(scr, dl2_ref, F32)
            scr[...] = lse_ref[...]
            _deint_rows(scr, lse2_ref, F32)

    blk = pl.BlockSpec((rows, LANES), lambda b, j: (b, j))
    stat = pl.BlockSpec((rows, LANES), lambda b, j: (b, 0))
    stat2 = _deint_spec(lambda j: 0)
    outs = pl.pallas_call(
        body,
        out_shape=(SDS(dproj.shape, BF16), SDS((s, cfg.D), BF16), SDS((DEINT, s // DEINT, cfg.D), BF16),
                   SDS((s, LANES), F32), SDS((DEINT, s // DEINT, LANES), F32), SDS((DEINT, s // DEINT, LANES), F32)),
        grid=(s // rows, h),
        in_specs=[blk, blk, pl.BlockSpec((rows, LANES), lambda b, j: (b, zb + j)), stat, HBM_SPEC],
        out_specs=(pl.BlockSpec((rows, LANES), lambda b, j: (b, zb + j)), blk, _deint_spec(lambda j: j),
                   stat, stat2, stat2),
        scratch_shapes=[pltpu.VMEM((rows, LANES), F32)],
        input_output_aliases={4: 0},
        compiler_params=_params(("parallel", "arbitrary")), name="attn_bwd_prep")(o_a, doag, proj, lse, dproj)
    dproj, do, do2, dl, dl2, lse2 = outs
    return dproj, do, do2.reshape(s, cfg.D), dl, dl2.reshape(s, LANES), lse2.reshape(s, LANES)


def _attn_grad_sum(cfg, g_1, g_2, col0, dproj, name):
    s = cfg.S
    c0 = col0 // WIDE
    rows = DEINT_ROWS

    def body(g1_ref, g2_ref, dp_in, o_ref, scr):
        del dp_in
        for t in range(WIDE // LANES):
            cs = slice(t * LANES, (t + 1) * LANES)
            for r in range(DEINT):
                scr.at[t][pl.ds(r, LANES, stride=DEINT), :] = g2_ref[r, :, cs].astype(F32)
            o_ref[:, cs] = (g1_ref[:, cs].astype(F32) + scr[t]).astype(BF16)

    return pl.pallas_call(
        body, out_shape=SDS(dproj.shape, BF16), grid=(s // rows, cfg.D // WIDE),
        in_specs=[pl.BlockSpec((rows, WIDE), lambda b, j: (b, j)), _wide_spec(), HBM_SPEC],
        out_specs=pl.BlockSpec((rows, WIDE), lambda b, j: (b, c0 + j)),
        scratch_shapes=[pltpu.VMEM((WIDE // LANES, rows, LANES), F32)],
        input_output_aliases={2: 0},
        compiler_params=_params(("parallel", "parallel")), name=name)(g_1, _by_residue(g_2), dproj)


CONV_HALO = 16
CONV_TR = 512
CONV_CW = 512


def _rows_back(a, n):
    return a if n == 0 else pltpu.roll(a, n % a.shape[0], axis=0)


def _conv_fwd(cfg, proj, conv_w, conv_b):
    s, cd = cfg.S, cfg.CD
    tr, cw, hl = CONV_TR, CONV_CW, CONV_HALO
    cb0 = cfg.OXBC // cw

    def body(x_ref, h_ref, w_ref, b_ref, o_ref):
        i = pl.program_id(0)
        halo = jnp.where(i > 0, h_ref[...].astype(F32), 0.0)
        ext = jnp.concatenate([halo, x_ref[...].astype(F32)], axis=0)
        pre = b_ref[...] + jnp.zeros((tr, cw), F32)
        for k in range(CONV_K):
            pre = pre + w_ref[k:k + 1, :] * _rows_back(ext, CONV_K - 1 - k)[hl:]
        o_ref[...] = (pre * _sigmoid(pre)).astype(BF16)

    return pl.pallas_call(
        body, out_shape=SDS((s, cd), BF16), grid=(s // tr, cd // cw),
        in_specs=[pl.BlockSpec((tr, cw), lambda i, j: (i, cb0 + j)),
                  pl.BlockSpec((hl, cw), lambda i, j: (jnp.maximum(i * (tr // hl) - 1, 0), cb0 + j)),
                  pl.BlockSpec((CONV_K, cw), lambda i, j: (0, j)),
                  pl.BlockSpec((1, cw), lambda i, j: (0, j))],
        out_specs=pl.BlockSpec((tr, cw), lambda i, j: (i, j)),
        compiler_params=_params(("parallel", "parallel")), name="conv_fwd")(proj, proj, conv_w, conv_b)


def _conv_bwd(cfg, proj, dact, conv_w, conv_b, dproj):
    s, cd = cfg.S, cfg.CD
    tr, cw, hl = CONV_TR, CONV_CW, CONV_HALO
    cb0 = cfg.OXBC // cw
    nr = s // tr
    last_h = s // hl - 1

    def body(x_ref, hp_ref, hn_ref, d_ref, dn_ref, w_ref, b_ref, dp_in, dx_ref, gw_ref, gb_ref):
        del dp_in
        i = pl.program_id(1)
        ext = jnp.concatenate([jnp.where(i > 0, hp_ref[...].astype(F32), 0.0), x_ref[...].astype(F32),
                               hn_ref[...].astype(F32)], axis=0)
        shifted = [_rows_back(ext, CONV_K - 1 - k)[hl:] for k in range(CONV_K)]
        pre = b_ref[...] + jnp.zeros((tr + hl, cw), F32)
        for k in range(CONV_K):
            pre = pre + w_ref[k:k + 1, :] * shifted[k]
        sg = _sigmoid(pre)
        dact = jnp.concatenate([d_ref[...].astype(F32), jnp.where(i < nr - 1, dn_ref[...].astype(F32), 0.0)], axis=0)
        dpre = dact * (sg * (1.0 + pre * (1.0 - sg)))
        dx = jnp.zeros((tr, cw), F32)
        for k in range(CONV_K):
            dx = dx + w_ref[k:k + 1, :] * _rows_back(dpre, -(CONV_K - 1 - k))[0:tr]
        dx_ref[...] = dx.astype(BF16)

        @pl.when(i == 0)
        def _():
            gw_ref[...] = jnp.zeros_like(gw_ref)
            gb_ref[...] = jnp.zeros_like(gb_ref)

        dcur = dpre[0:tr]
        gb_ref[...] += jnp.sum(dcur, axis=0, keepdims=True)
        for k in range(CONV_K):
            gw_ref[k:k + 1, :] += jnp.sum(dcur * shifted[k][0:tr], axis=0, keepdims=True)

    return pl.pallas_call(
        body, out_shape=(SDS(dproj.shape, BF16), SDS((CONV_K, cd), F32), SDS((1, cd), F32)), grid=(cd // cw, nr),
        in_specs=[pl.BlockSpec((tr, cw), lambda j, i: (i, cb0 + j)),
                  pl.BlockSpec((hl, cw), lambda j, i: (jnp.maximum(i * (tr // hl) - 1, 0), cb0 + j)),
                  pl.BlockSpec((hl, cw), lambda j, i: (jnp.minimum((i + 1) * (tr // hl), last_h), cb0 + j)),
                  pl.BlockSpec((tr, cw), lambda j, i: (i, j)),
                  pl.BlockSpec((hl, cw), lambda j, i: (jnp.minimum((i + 1) * (tr // hl), last_h), j)),
                  pl.BlockSpec((CONV_K, cw), lambda j, i: (0, j)),
                  pl.BlockSpec((1, cw), lambda j, i: (0, j)),
                  pl.BlockSpec(memory_space=pl.ANY)],
        out_specs=(pl.BlockSpec((tr, cw), lambda j, i: (i, cb0 + j)),
                   pl.BlockSpec((CONV_K, cw), lambda j, i: (0, j)),
                   pl.BlockSpec((1, cw), lambda j, i: (0, j))),
        input_output_aliases={7: 0},
        compiler_params=_params(("parallel", "arbitrary")), name="conv_bwd")(
            proj, proj, proj, dact, dact, conv_w, conv_b, dproj)


def _expand(v, e, terms):
    out, rem = None, v
    for _ in range(terms):
        hi = rem.astype(BF16)
        t = _nn(hi, e)
        out = t if out is None else out + t
        rem = rem - hi.astype(F32)
    return out


def _segsum(v, e, terms):
    out, rem = None, v
    for _ in range(terms):
        hi = rem.astype(BF16)
        t = _nt(hi, e)
        out = t if out is None else out + t
        rem = rem - hi.astype(F32)
    return out


def _expand_row(row, e, terms):
    return _expand(jnp.broadcast_to(row, (8, LANES)), e, terms)[0:1]


def _segsum_row(row, e, terms):
    return _segsum(jnp.broadcast_to(row, (8, row.shape[1])), e, terms)[0:1]


def _expansion_matrix(cfg):
    hh = jnp.arange(LANES, dtype=jnp.int32)[:, None]
    cc = jnp.arange(cfg.SI, dtype=jnp.int32)[None, :]
    return (cc // SSM_HEAD_DIM == hh).astype(BF16)


def _tri(lower):
    r = lax.broadcasted_iota(jnp.int32, (CHUNK, CHUNK), 0)
    c = lax.broadcasted_iota(jnp.int32, (CHUNK, CHUNK), 1)
    return (c <= r) if lower else (c >= r)


def _ssd_prep(dtr_ref, db_ref, al_ref, e):
    dtr = dtr_ref[...] + db_ref[...]
    dt = _softplus(dtr)
    a = -jnp.exp(al_ref[...])
    acum = jnp.dot(_tri(True).astype(F32), dt * a, precision=lax.Precision.HIGHEST, preferred_element_type=F32)
    return dtr, dt, a, _expand(dt, e, 2), _expand(acum, e, 3)


def _ssd_fwd(cfg, xact, dt_raw, proj, dt_bias, a_log, d_skip, norm_w, e):
    s, si, cd, gw, bc = cfg.S, cfg.SI, cfg.CD, cfg.GW, cfg.BC
    nc = s // CHUNK
    zb = cfg.OZS // si
    tiles = gw // LANES

    def body(xa_ref, dtr_ref, z_ref, db_ref, al_ref, dsk_ref, nw_ref, e_ref, y_ref, y2_ref, st_ref,
             state, ybuf, x_s, xw_s, ae_s, ea_s, lam_s):
        @pl.when(pl.program_id(0) == 0)
        def _():
            state[...] = jnp.zeros_like(state)

        st_ref[...] = state[...]
        ev = e_ref[...]
        _, _, _, dt_e, a_e = _ssd_prep(dtr_ref, db_ref, al_ref, ev)
        xs = xa_ref[:, 0:si].astype(F32)
        x = xs * dt_e
        lam_e = a_e[CHUNK - 1:CHUNK, :]
        x_s[...] = x.astype(BF16)
        xw_s[...] = (x * jnp.exp(lam_e - a_e)).astype(BF16)
        ae_s[...] = a_e
        ea_s[...] = jnp.exp(a_e)
        ybuf[...] = _expand_row(dsk_ref[...], ev, 3) * xs
        lam_s[...] = jnp.broadcast_to(jnp.exp(lam_e), (8, si))
        tril = _tri(True)
        lane = lax.broadcasted_iota(jnp.int32, (CHUNK, LANES), 1)

        def group(g, carry):
            co = pl.multiple_of(g * gw, LANES)
            bg = xa_ref[:, pl.ds(pl.multiple_of(si + g * SSM_STATE, LANES), SSM_STATE)]
            cg = xa_ref[:, pl.ds(pl.multiple_of(si + bc + g * SSM_STATE, LANES), SSM_STATE)]
            cbm = _nt(cg, bg)
            st = state[:, pl.ds(co, gw)]
            yoff = _nn(cg, st.astype(BF16)) * ea_s[:, pl.ds(co, gw)]
            for k in range(tiles):
                tc = pl.multiple_of(co + k * LANES, LANES)
                at = ae_s[:, pl.ds(tc, LANES)]
                att = at.T
                xt = x_s[:, pl.ds(tc, LANES)]
                acc = yoff[:, k * LANES:(k + 1) * LANES]
                for half in range(2):
                    lo = half * SSM_HEAD_DIM
                    seg = at[:, lo:lo + 1] - att[lo:lo + 1, :]
                    dec = jnp.exp(jnp.where(tril, seg, NEG))
                    xh = jnp.where((lane >= lo) & (lane < lo + SSM_HEAD_DIM), xt, jnp.zeros_like(xt))
                    acc = acc + _nn((cbm * dec).astype(BF16), xh)
                ybuf[:, pl.ds(tc, LANES)] += acc
            state[:, pl.ds(co, gw)] = st * lam_s[0:1, pl.ds(co, gw)] + _tn(bg, xw_s[:, pl.ds(co, gw)])
            return carry

        lax.fori_loop(0, SSM_GROUPS, group, 0)
        y = ybuf[...]
        y_ref[...] = y.astype(BF16)
        z = z_ref[...].astype(F32)
        u = y * (z * _sigmoid(z))
        r = lax.rsqrt(jnp.mean(u * u, axis=-1, keepdims=True) + RMS_EPS)
        y2_ref[...] = (u * r * nw_ref[...]).astype(BF16)

    row = lambda n: pl.BlockSpec((1, n), lambda c: (0, 0))
    return pl.pallas_call(
        body,
        out_shape=(SDS((s, si), BF16), SDS((s, si), BF16), SDS((nc, SSM_STATE, si), F32)),
        grid=(nc,),
        in_specs=[pl.BlockSpec((CHUNK, cd), lambda c: (c, 0)),
                  pl.BlockSpec((CHUNK, LANES), lambda c: (c, 0)),
                  pl.BlockSpec((CHUNK, si), lambda c: (c, zb)),
                  row(LANES), row(LANES), row(LANES), row(si),
                  pl.BlockSpec((LANES, si), lambda c: (0, 0))],
        out_specs=(pl.BlockSpec((CHUNK, si), lambda c: (c, 0)),
                   pl.BlockSpec((CHUNK, si), lambda c: (c, 0)),
                   pl.BlockSpec((None, SSM_STATE, si), lambda c: (c, 0, 0))),
        scratch_shapes=[pltpu.VMEM((SSM_STATE, si), F32), pltpu.VMEM((CHUNK, si), F32),
                        pltpu.VMEM((CHUNK, si), BF16), pltpu.VMEM((CHUNK, si), BF16),
                        pltpu.VMEM((CHUNK, si), F32), pltpu.VMEM((CHUNK, si), F32),
                        pltpu.VMEM((8, si), F32)],
        compiler_params=_params(("arbitrary",)), name="ssd_fwd")(
            xact, dt_raw, proj, dt_bias, a_log, d_skip, norm_w, e)


def _ssd_bwd(cfg, xact, dt_raw, proj, y, dy2, states, dt_bias, a_log, d_skip, norm_w, e, dproj):
    s, si, cd, gw, bc, hpg = cfg.S, cfg.SI, cfg.CD, cfg.GW, cfg.BC, cfg.HPG
    nc = s // CHUNK
    zb = cfg.OZS // si
    tiles = gw // LANES

    def body(xa_ref, dtr_ref, z_ref, y_ref, d2_ref, st_ref, db_ref, al_ref, dsk_ref, nw_ref, e_ref, dp_in,
             dz_ref, dxa_ref, ddt_ref, gnw_ref, gdb_ref, gal_ref, gds_ref,
             dh, dhn, xs_s, x_s, w_s, ae_s, ea_s, g_s, dx_s, dae_s, r_s, lam_s, dle_s):
        del dp_in

        @pl.when(pl.program_id(0) == 0)
        def _():
            dh[...] = jnp.zeros_like(dh)
            gnw_ref[...] = jnp.zeros_like(gnw_ref)
            gdb_ref[...] = jnp.zeros_like(gdb_ref)
            gal_ref[...] = jnp.zeros_like(gal_ref)
            gds_ref[...] = jnp.zeros_like(gds_ref)

        ev = e_ref[...]
        yv = y_ref[...].astype(F32)
        z = z_ref[...].astype(F32)
        sg = _sigmoid(z)
        sz = z * sg
        u = yv * sz
        r = lax.rsqrt(jnp.mean(u * u, axis=-1, keepdims=True) + RMS_EPS)
        nrm = u * r
        d2 = d2_ref[...].astype(F32)
        gnw_ref[...] += jnp.sum(d2 * nrm, axis=0, keepdims=True)
        gn = d2 * nw_ref[...]
        du = r * (gn - nrm * jnp.mean(gn * nrm, axis=-1, keepdims=True))
        gv = du * sz
        dz_ref[...] = (du * yv * (sg * (1.0 + z * (1.0 - sg)))).astype(BF16)
        g_s[...] = gv

        dtr, dt, a, dt_e, a_e = _ssd_prep(dtr_ref, db_ref, al_ref, ev)
        xs = xa_ref[:, 0:si].astype(F32)
        x = xs * dt_e
        lam_e = a_e[CHUNK - 1:CHUNK, :]
        xs_s[...] = xs
        x_s[...] = x
        w_s[...] = jnp.exp(lam_e - a_e)
        ae_s[...] = a_e
        ea_s[...] = jnp.exp(a_e)
        lam_s[...] = jnp.broadcast_to(jnp.exp(lam_e), (8, si))
        gds_ref[...] += _segsum_row(jnp.sum(gv * xs, axis=0, keepdims=True), ev, 2)
        r_s[...] = jnp.zeros_like(r_s)
        tril = _tri(True)
        lane = lax.broadcasted_iota(jnp.int32, (CHUNK, LANES), 1)
        sub = lax.broadcasted_iota(jnp.int32, (CHUNK, LANES), 0)

        def group(g, carry):
            co = pl.multiple_of(g * gw, LANES)
            bo = pl.multiple_of(si + g * SSM_STATE, LANES)
            cof = pl.multiple_of(si + bc + g * SSM_STATE, LANES)
            cols = pl.ds(co, gw)
            bg = xa_ref[:, pl.ds(bo, SSM_STATE)]
            cg = xa_ref[:, pl.ds(cof, SSM_STATE)]
            cbm = _nt(cg, bg)
            st = st_ref[:, cols]
            stb = st.astype(BF16)
            dho = dh[:, cols]
            dhob = dho.astype(BF16)
            ea = ea_s[:, cols]
            gg = g_s[:, cols]
            xg = x_s[:, cols]
            wg = w_s[:, cols]
            explam = lam_s[0:1, cols]
            yoff = _nn(cg, stb) * ea
            ga = (gg * ea).astype(BF16)
            dc = _nt(ga, stb)
            dhn[:, cols] = dho * explam + _tn(cg, ga)
            bdh = _nn(bg, dhob)
            db = _nt((xg * wg).astype(BF16), dhob)
            t = xg * bdh * wg
            dle_s[0:1, cols] = jnp.sum(t, axis=0, keepdims=True) + explam * jnp.sum(dho * st, axis=0, keepdims=True)
            dae_base = gg * yoff - t
            dxw = wg * bdh
            dcb = jnp.zeros((CHUNK, CHUNK), F32)
            for k in range(tiles):
                tc = pl.multiple_of(co + k * LANES, LANES)
                ksl = slice(k * LANES, (k + 1) * LANES)
                at = ae_s[:, pl.ds(tc, LANES)]
                att = at.T
                xt = xg[:, ksl].astype(BF16)
                gt = gg[:, ksl].astype(BF16)
                dxt = dxw[:, ksl]
                place = jnp.zeros((CHUNK, LANES), F32)
                for half in range(2):
                    lo = half * SSM_HEAD_DIM
                    seg = at[:, lo:lo + 1] - att[lo:lo + 1, :]
                    dec = jnp.exp(jnp.where(tril, seg, NEG))
                    mh = cbm * dec
                    gh = jnp.where((lane >= lo) & (lane < lo + SSM_HEAD_DIM), gt, jnp.zeros_like(gt))
                    dm = _nt(gh, xt)
                    dxt = dxt + _tn(mh.astype(BF16), gh)
                    dcb = dcb + dm * dec
                    dseg = dm * mh
                    place = place + jnp.where(lane == lo, jnp.sum(dseg, axis=1, keepdims=True), 0.0)
                    hidx = g * hpg + 2 * k + half
                    r_s[...] += jnp.where(sub == hidx, jnp.sum(dseg, axis=0, keepdims=True), 0.0)
                dx_s[:, pl.ds(tc, LANES)] = dxt
                dae_s[:, pl.ds(tc, LANES)] = dae_base[:, ksl] + place
            dcbb = dcb.astype(BF16)
            dxa_ref[:, pl.ds(bo, SSM_STATE)] = (db + _tn(dcbb, cg)).astype(BF16)
            dxa_ref[:, pl.ds(cof, SSM_STATE)] = (dc + _nn(dcbb, bg)).astype(BF16)
            return carry

        lax.fori_loop(0, SSM_GROUPS, group, 0)
        dlam = _segsum_row(dle_s[0:1, :], ev, 2)
        da_ = _segsum(dae_s[...], ev, 2) - r_s[...].T
        da_ = da_ + jnp.where(sub == CHUNK - 1, dlam, 0.0)
        dda = jnp.dot(_tri(False).astype(F32), da_, precision=lax.Precision.HIGHEST, preferred_element_type=F32)
        dxv = dx_s[...]
        xs = xs_s[...]
        ddt = dda * a + _segsum(dxv * xs, ev, 2)
        gal_ref[...] += jnp.sum(dda * dt, axis=0, keepdims=True) * a
        ddtr = ddt * _sigmoid(dtr)
        gdb_ref[...] += jnp.sum(ddtr, axis=0, keepdims=True)
        ddt_ref[...] = ddtr
        dxa_ref[:, 0:si] = (dxv * dt_e + g_s[...] * _expand_row(dsk_ref[...], ev, 3)).astype(BF16)
        dh[...] = dhn[...]

    rev = lambda c: nc - 1 - c
    row = lambda n: pl.BlockSpec((1, n), lambda c: (0, 0))
    big = lambda: pltpu.VMEM((CHUNK, si), F32)
    return pl.pallas_call(
        body,
        out_shape=(SDS(dproj.shape, BF16), SDS((s, cd), BF16), SDS((s, LANES), F32),
                   SDS((1, si), F32), SDS((1, LANES), F32), SDS((1, LANES), F32), SDS((1, LANES), F32)),
        grid=(nc,),
        in_specs=[pl.BlockSpec((CHUNK, cd), lambda c: (rev(c), 0)),
                  pl.BlockSpec((CHUNK, LANES), lambda c: (rev(c), 0)),
                  pl.BlockSpec((CHUNK, si), lambda c: (rev(c), zb)),
                  pl.BlockSpec((CHUNK, si), lambda c: (rev(c), 0)),
                  pl.BlockSpec((CHUNK, si), lambda c: (rev(c), 0)),
                  pl.BlockSpec((None, SSM_STATE, si), lambda c: (rev(c), 0, 0)),
                  row(LANES), row(LANES), row(LANES), row(si),
                  pl.BlockSpec((LANES, si), lambda c: (0, 0)),
                  pl.BlockSpec(memory_space=pl.ANY)],
        out_specs=(pl.BlockSpec((CHUNK, si), lambda c: (rev(c), zb)),
                   pl.BlockSpec((CHUNK, cd), lambda c: (rev(c), 0)),
                   pl.BlockSpec((CHUNK, LANES), lambda c: (rev(c), 0)),
                   row(si), row(LANES), row(LANES), row(LANES)),
        scratch_shapes=[pltpu.VMEM((SSM_STATE, si), F32), pltpu.VMEM((SSM_STATE, si), F32),
                        big(), big(), big(), big(), big(), big(), big(), big(),
                        pltpu.VMEM((CHUNK, LANES), F32), pltpu.VMEM((8, si), F32), pltpu.VMEM((8, si), F32)],
        input_output_aliases={11: 0},
        compiler_params=_params(("arbitrary",)), name="ssd_bwd")(
            xact, dt_raw, proj, y, dy2, states, dt_bias, a_log, d_skip, norm_w, e, dproj)


MERGE_TR = 512
MERGE_CW = 512


def _merge_fwd(cfg, proj, a_br, s_br):
    s, d = cfg.S, cfg.D
    tr, cw = MERGE_TR, MERGE_CW
    ga0, gs0 = cfg.OGA // cw, cfg.OGS // cw

    def body(ga_ref, gs_ref, a_ref, s_ref, o_ref):
        o_ref[...] = (_sigmoid(ga_ref[...].astype(F32)) * a_ref[...].astype(F32)
                      + _sigmoid(gs_ref[...].astype(F32)) * s_ref[...].astype(F32)).astype(BF16)

    blk = pl.BlockSpec((tr, cw), lambda i, j: (i, j))
    return pl.pallas_call(
        body, out_shape=SDS((s, d), BF16), grid=(s // tr, d // cw),
        in_specs=[pl.BlockSpec((tr, cw), lambda i, j: (i, ga0 + j)),
                  pl.BlockSpec((tr, cw), lambda i, j: (i, gs0 + j)), blk, blk],
        out_specs=blk, compiler_params=_params(("parallel", "parallel")), name="merge_fwd")(proj, proj, a_br, s_br)


def _merge_bwd(cfg, proj, branch, dmerged, gate_off, dproj, name):
    s, d = cfg.S, cfg.D
    tr, cw = MERGE_TR, MERGE_CW
    g0 = gate_off // cw
    fresh = dproj is None

    def body(*refs):
        g_ref, b_ref, dm_ref = refs[:3]
        dg_ref, db_ref = refs[-2:]
        dm = dm_ref[...].astype(F32)
        sg = _sigmoid(g_ref[...].astype(F32))
        db_ref[...] = (dm * sg).astype(BF16)
        dg_ref[...] = (dm * b_ref[...].astype(F32) * sg * (1.0 - sg)).astype(BF16)

    blk = pl.BlockSpec((tr, cw), lambda i, j: (i, j))
    gate = pl.BlockSpec((tr, cw), lambda i, j: (i, g0 + j))
    return pl.pallas_call(
        body, out_shape=(SDS((s, cfg.NM), BF16), SDS((s, d), BF16)), grid=(s // tr, d // cw),
        in_specs=[gate, blk, blk] + ([] if fresh else [HBM_SPEC]),
        out_specs=(gate, blk),
        input_output_aliases={} if fresh else {3: 0},
        compiler_params=_params(("parallel", "parallel")), name=name)(
            *((proj, branch, dmerged) + (() if fresh else (dproj,))))


def _outproj_loss(merged, w_out, x, target, fnw):
    s, d = x.shape
    tr = 256

    def body(m_ref, w_ref, x_ref, t_ref, fw_ref, dof_ref, dob_ref, loss_ref, g_ref):
        out = x_ref[...] + _nn(m_ref[...], w_ref[...])
        r = lax.rsqrt(jnp.mean(out * out, axis=-1, keepdims=True) + RMS_EPS)
        nrm = out * r
        fw = fw_ref[...]
        err = nrm * fw - t_ref[...]
        dy = err * (1.0 / d)
        gy = dy * fw
        dout = r * (gy - nrm * jnp.mean(gy * nrm, axis=-1, keepdims=True))
        dof_ref[...] = dout
        dob_ref[...] = dout.astype(BF16)

        @pl.when(pl.program_id(0) == 0)
        def _():
            loss_ref[...] = jnp.zeros_like(loss_ref)
            g_ref[...] = jnp.zeros_like(g_ref)

        loss_ref[...] += jnp.sum(jnp.sum(err * err, axis=1, keepdims=True), axis=0, keepdims=True) * (0.5 / d)
        g_ref[...] += jnp.sum(dy * nrm, axis=0, keepdims=True)

    blk = pl.BlockSpec((tr, d), lambda i: (i, 0))
    return pl.pallas_call(
        body, out_shape=(SDS((s, d), F32), SDS((s, d), BF16), SDS((1, LANES), F32), SDS((1, d), F32)), grid=(s // tr,),
        in_specs=[blk, pl.BlockSpec((d, d), lambda i: (0, 0)), blk, blk, pl.BlockSpec((1, d), lambda i: (0, 0))],
        out_specs=(blk, blk, pl.BlockSpec((1, LANES), lambda i: (0, 0)), pl.BlockSpec((1, d), lambda i: (0, 0))),
        compiler_params=_params(("arbitrary",)), name="outproj_loss")(merged, w_out, x, target, fnw)


ELEMWISE_BLOCK_BYTES = 1 << 20


def _row_block(rows, cols, itemsize=4):
    best = None
    for tr in range(16, rows + 1, 16):
        if rows % tr == 0 and tr * cols * itemsize <= ELEMWISE_BLOCK_BYTES:
            best = tr
    return best if best is not None else rows


def _adamw(w, g, m, v, name):
    rows, cols = w.shape
    tr = _row_block(rows, cols)

    def body(w_ref, g_ref, m_ref, v_ref, d_ref, nm_ref, nv_ref):
        gv = g_ref[...]
        nm = ADAM_B1 * m_ref[...] + (1.0 - ADAM_B1) * gv
        nv = ADAM_B2 * v_ref[...] + (1.0 - ADAM_B2) * jnp.square(gv)
        m_hat = nm / (1.0 - ADAM_B1 ** ADAM_STEP)
        v_hat = nv / (1.0 - ADAM_B2 ** ADAM_STEP)
        d_ref[...] = -ADAM_LR * (m_hat / (jnp.sqrt(v_hat) + ADAM_EPS) + ADAM_WD * w_ref[...])
        nm_ref[...] = nm
        nv_ref[...] = nv

    blk = pl.BlockSpec((tr, cols), lambda i: (i, 0))
    out = SDS((rows, cols), F32)
    return pl.pallas_call(
        body, out_shape=(out, out, out), grid=(rows // tr,), in_specs=[blk] * 4, out_specs=(blk,) * 3,
        compiler_params=_params(("parallel",)), name=name)(w, g, m, v)


HBM_SPEC = pl.BlockSpec(memory_space=pl.ANY)


def _position():
    return lax.axis_index("x"), lax.axis_index("y"), lax.axis_index("c")


class _Carry:
    def __init__(self, arrays, out_shapes, sems, start, finish):
        self.arrays, self.out_shapes, self.sems, self.start, self.finish = list(arrays), out_shapes, sems, start, finish

    def sem_shapes(self):
        return [pltpu.SemaphoreType.DMA((k,)) for k in self.sems]


def _run_carry(carry, name):
    n = len(carry.arrays)

    def body(*refs):
        carry.start(refs[:n], refs[n:2 * n], refs[2 * n:])
        carry.finish(refs[:n], refs[n:2 * n], refs[2 * n:])

    return pl.pallas_call(
        body, out_shape=carry.out_shapes, in_specs=[HBM_SPEC] * n, out_specs=[HBM_SPEC] * n,
        scratch_shapes=carry.sem_shapes(),
        compiler_params=pltpu.CompilerParams(has_side_effects=True), name=name)(*carry.arrays)


def _gather_carry(shards):
    n = len(shards)

    def copies(ins, outs, sems):
        send_sems, recv_sems, fsend_sems, frecv_sems = sems
        x, y, c = _position()
        me = 2 * x + y
        peers = [(1 - x, y), (x, 1 - y), (1 - x, 1 - y)]

        def over_ici(t, p, chip):
            px, py = peers[p]
            r2 = ins[t].shape[0] // 2
            return pltpu.make_async_remote_copy(
                src_ref=ins[t].at[pl.ds(c * r2, r2), :], dst_ref=outs[t].at[chip, c], send_sem=send_sems.at[3 * t + p],
                recv_sem=recv_sems.at[3 * t + p], device_id=(px, py, c), device_id_type=MESH)

        def to_sibling(t, p, half):
            px, py = peers[p]
            slab = outs[t].at[2 * px + py, half]
            return pltpu.make_async_remote_copy(
                src_ref=slab, dst_ref=slab, send_sem=fsend_sems.at[3 * t + p], recv_sem=frecv_sems.at[3 * t + p],
                device_id=(x, y, 1 - c), device_id_type=MESH)

        pairs = [(t, p) for t in range(n) for p in range(3)]
        sends = [over_ici(t, p, me) for t, p in pairs]
        lands = [over_ici(t, p, 2 * peers[p][0] + peers[p][1]) for t, p in pairs]
        passed = [to_sibling(t, p, c) for t, p in pairs]
        from_sibling = [to_sibling(t, p, 1 - c) for t, p in pairs]
        return sends, lands, passed, from_sibling

    def start(ins, outs, sems):
        for cp in copies(ins, outs, sems)[0]:
            cp.start()

    def finish(ins, outs, sems):
        sends, lands, passed, from_sibling = copies(ins, outs, sems)
        for land, fwd in zip(lands, passed):
            land.wait_recv()
            fwd.start()
        for cp in from_sibling:
            cp.wait_recv()
        for cp in sends + passed:
            cp.wait_send()

    return _Carry(shards, [SDS((N_CHIPS, 2, a.shape[0] // 2, a.shape[1]), a.dtype) for a in shards], [3 * n] * 4,
                  start, finish)


def _scatter_carry(parts):
    def start(ins, outs, sems):
        for cp in _scatter_copies(ins, outs, *sems)[0]:
            cp.start()

    def finish(ins, outs, sems):
        sends, lands = _scatter_copies(ins, outs, *sems)
        for cp in lands:
            cp.wait_recv()
        for cp in sends:
            cp.wait_send()

    return _Carry(parts, [SDS(a.shape, a.dtype) for a in parts], [3 * len(parts)] * 2, start, finish)


def _with_own(gathered, own, chip):
    full = gathered.reshape((N_CHIPS,) + own.shape)
    return lax.dynamic_update_index_in_dim(full, own, chip, 0)


def _exchange_halves(grads):
    n = len(grads)

    def body(*refs):
        ins, outs = refs[:n], refs[n:2 * n]
        send_sems, recv_sems = refs[2 * n:]
        x, y, c = _position()
        cps = []
        for t in range(n):
            r2 = ins[t].shape[1] // 2
            cps.append(pltpu.make_async_remote_copy(
                src_ref=ins[t].at[:, pl.ds((1 - c) * r2, r2), :], dst_ref=outs[t],
                send_sem=send_sems.at[t], recv_sem=recv_sems.at[t], device_id=(x, y, 1 - c), device_id_type=MESH))
        for cp in cps:
            cp.start()
        for cp in cps:
            cp.wait()

    return pl.pallas_call(
        body, out_shape=[SDS((a.shape[0], a.shape[1] // 2, a.shape[2]), a.dtype) for a in grads],
        in_specs=[HBM_SPEC] * n, out_specs=[HBM_SPEC] * n,
        scratch_shapes=[pltpu.SemaphoreType.DMA((n,)), pltpu.SemaphoreType.DMA((n,))],
        compiler_params=pltpu.CompilerParams(has_side_effects=True), name="reduce_sibling")(*grads)


def _scatter_copies(ins, outs, send_sems, recv_sems):
    x, y, c = _position()
    me = 2 * x + y
    peers = [(1 - x, y), (x, 1 - y), (1 - x, 1 - y)]

    def remote(t, p, src_slab, dst_slab):
        px, py = peers[p]
        return pltpu.make_async_remote_copy(
            src_ref=ins[t].at[src_slab], dst_ref=outs[t].at[dst_slab], send_sem=send_sems.at[3 * t + p],
            recv_sem=recv_sems.at[3 * t + p], device_id=(px, py, c), device_id_type=MESH)

    n = len(ins)
    sends = [remote(t, p, 2 * peers[p][0] + peers[p][1], me) for t in range(n) for p in range(3)]
    lands = [remote(t, p, me, 2 * peers[p][0] + peers[p][1]) for t in range(n) for p in range(3)]
    return sends, lands


def _share_halves(halves):
    n = len(halves)

    def body(*refs):
        ins, outs = refs[:n], refs[n:2 * n]
        send_sems, recv_sems = refs[2 * n:]
        x, y, c = _position()

        def copy(t, slab):
            return pltpu.make_async_remote_copy(
                src_ref=ins[t].at[slab], dst_ref=outs[t].at[slab], send_sem=send_sems.at[t], recv_sem=recv_sems.at[t],
                device_id=(x, y, 1 - c), device_id_type=MESH)

        for t in range(n):
            copy(t, c).start()
        for t in range(n):
            copy(t, 1 - c).wait_recv()
        for t in range(n):
            copy(t, c).wait_send()

    return pl.pallas_call(
        body, out_shape=[SDS(a.shape, a.dtype) for a in halves],
        in_specs=[HBM_SPEC] * n, out_specs=[HBM_SPEC] * n,
        scratch_shapes=[pltpu.SemaphoreType.DMA((n,)), pltpu.SemaphoreType.DMA((n,))],
        input_output_aliases={t: t for t in range(n)},
        compiler_params=pltpu.CompilerParams(has_side_effects=True), name="share_sibling")(*halves)


def _add_sibling(grad, recv, core):
    nch, r2, cols = recv.shape
    tr = _row_block(r2, cols)
    nb = r2 // tr

    def body(c_ref, g_ref, r_ref, o_ref):
        del c_ref
        o_ref[...] = (g_ref[...].astype(F32) + r_ref[...].astype(F32)).astype(BF16)

    return pl.pallas_call(
        body, out_shape=SDS(recv.shape, BF16),
        grid_spec=pltpu.PrefetchScalarGridSpec(
            num_scalar_prefetch=1, grid=(nch, nb),
            in_specs=[pl.BlockSpec((None, tr, cols), lambda j, i, c_ref: (j, c_ref[0] * nb + i, 0)),
                      pl.BlockSpec((None, tr, cols), lambda j, i, c_ref: (j, i, 0))],
            out_specs=pl.BlockSpec((None, tr, cols), lambda j, i, c_ref: (j, i, 0))),
        compiler_params=_params(("parallel", "parallel")), name="add_sibling")(core, grad, recv)


def _add_chips(own, recv, chip_core):
    nch, r2, cols = recv.shape
    tr = _row_block(r2, cols)

    nsc = 2 + nch

    def body(*refs):
        me = refs[0][0]
        own_ref, p_refs, o_ref = refs[nsc], refs[nsc + 1:nsc + 1 + nch], refs[nsc + 1 + nch]
        acc = None
        for j in range(nch):
            term = jnp.where(me == j, own_ref[...], p_refs[j][...]).astype(F32)
            acc = term if acc is None else acc + term
        o_ref[...] = acc

    def slab(j):
        return pl.BlockSpec((None, tr, cols), lambda i, *sc: (sc[2 + j][0], i, 0))

    return pl.pallas_call(
        body, out_shape=SDS((2, r2, cols), F32),
        grid_spec=pltpu.PrefetchScalarGridSpec(
            num_scalar_prefetch=nsc, grid=(r2 // tr,),
            in_specs=[pl.BlockSpec((None, tr, cols), lambda i, *sc: (sc[0][0], i, 0))] + [slab(j) for j in range(nch)],
            out_specs=pl.BlockSpec((None, tr, cols), lambda i, *sc: (sc[1][0], i, 0))),
        compiler_params=_params(("parallel",)), name="add_chips")(*chip_core, own, *([recv] * nch))


def _allreduce_small(pack):
    rows = pack.shape[0]

    def body(p_ref, o_ref, buf, send_sems, recv_sems):
        x, y, c = _position()
        me = 4 * x + 2 * y + c
        buf[me] = p_ref[...]

        def copy(dst_dev, slot):
            return pltpu.make_async_remote_copy(
                src_ref=p_ref, dst_ref=buf.at[slot], send_sem=send_sems.at[dst_dev], recv_sem=recv_sems.at[slot],
                device_id=(dst_dev // 4, (dst_dev // 2) % 2, dst_dev % 2), device_id_type=MESH)

        for dev in range(N_DEV):
            @pl.when(dev != me)
            def _():
                copy(dev, me).start()
        for dev in range(N_DEV):
            @pl.when(dev != me)
            def _():
                copy(dev, dev).wait_recv()
        for dev in range(N_DEV):
            @pl.when(dev != me)
            def _():
                copy(dev, me).wait_send()
        acc = buf[0]
        for dev in range(1, N_DEV):
            acc = acc + buf[dev]
        o_ref[...] = acc

    return pl.pallas_call(
        body, out_shape=SDS(pack.shape, F32),
        in_specs=[pl.BlockSpec(memory_space=pltpu.VMEM)], out_specs=pl.BlockSpec(memory_space=pltpu.VMEM),
        scratch_shapes=[pltpu.VMEM((N_DEV, rows, LANES), F32), pltpu.SemaphoreType.DMA((N_DEV,)),
                        pltpu.SemaphoreType.DMA((N_DEV,))],
        compiler_params=pltpu.CompilerParams(has_side_effects=True), name="allreduce_small")(pack)


ATTN_TQ = 256


def _local_step(cfg, x, target, w, to_chips=None, late=None):
    d = cfg.D
    hn = _rmsnorm_fwd(x, w["norm_w"])
    proj = _mm(hn, w["w_main"], "nn", BF16, "proj_main", carry=late[0] if late else None)
    if late:
        proj, arrived = proj
        w = {**w, **late[1](arrived)}
    dt_raw = _mm(hn, w["w_dt"], "nn", F32, "proj_dt")
    slopes = _slopes(cfg.H)
    near = _Pass(ATTN_TQ, DILATED_PATTERNS[:-1], 1, cfg.S)
    far = _Pass(LANES, DILATED_PATTERNS[-1:], DEINT, cfg.S // DEINT)
    tab_near, tab_far = _attn_tables(near), _attn_tables(far)
    cols_near, cols_far = (cfg.OQ, cfg.OK, cfg.OV), (0, d, 2 * d)
    qkv_far = _deinterleave(proj, 0, 3 * d, "attn_deinterleave")
    o_1, lse_1 = _attn_fwd(cfg, near, proj, cols_near, tab_near, slopes, "attn_fwd_near")
    o_2, lse_2 = _attn_fwd(cfg, far, qkv_far, cols_far, tab_far, slopes, "attn_fwd_far")
    o_a, oag, lse = _attn_merge(cfg, proj, o_1, lse_1, o_2, lse_2)
    xact = _conv_fwd(cfg, proj, w["conv_w"], w["conv_b"])
    e = _expansion_matrix(cfg)
    y, y2, states = _ssd_fwd(cfg, xact, dt_raw, proj, w["dt_bias"], w["a_log"], w["d_skip"], w["ssm_norm_w"], e)
    a_br = _mm(oag, w["w_attn"], "nn", BF16, "branch_attn")
    s_br = _mm(y2, w["w_ssm"], "nn", BF16, "branch_ssm")
    merged = _merge_fwd(cfg, proj, a_br, s_br)
    dout_f, dout_b, loss_row, g_fnw = _outproj_loss(merged, w["w_out"], x, target, w["final_norm_w"])

    dmerged = _mm(dout_b, w["w_out"], "nt", BF16, "d_merged")
    g_w_out = _mm(merged, dout_b, "tn", BF16, "g_w_out")
    dproj, da_br = _merge_bwd(cfg, proj, a_br, dmerged, cfg.OGA, None, "merge_bwd_attn")
    dproj, ds_br = _merge_bwd(cfg, proj, s_br, dmerged, cfg.OGS, dproj, "merge_bwd_ssm")
    doag = _mm(da_br, w["w_attn"], "nt", BF16, "d_oag")
    g_w_attn = _mm(oag, da_br, "tn", BF16, "g_w_attn")
    dy2 = _mm(ds_br, w["w_ssm"], "nt", BF16, "d_y2")
    g_w_ssm = _mm(y2, ds_br, "tn", BF16, "g_w_ssm")
    dproj, dxact, ddt, g_snw, g_dtb, g_alog, g_dsk = _ssd_bwd(
        cfg, xact, dt_raw, proj, y, dy2, states, w["dt_bias"], w["a_log"], w["d_skip"], w["ssm_norm_w"], e, dproj)
    dproj, g_cw, g_cb = _conv_bwd(cfg, proj, dxact, w["conv_w"], w["conv_b"], dproj)
    dproj, do, do_far, dl, dl_far, lse_far = _attn_bwd_prep(cfg, proj, o_a, doag, lse, dproj)
    g_near = _attn_bwd(cfg, near, proj, cols_near, do, lse, dl, tab_near, slopes, "attn_bwd_near")
    g_far = _attn_bwd(cfg, far, qkv_far, cols_far, do_far, lse_far, dl_far, tab_far, slopes, "attn_bwd_far")
    for g_1, g_2, col0, nm in zip(g_near, g_far, cols_near, ("attn_dq", "attn_dk", "attn_dv")):
        dproj = _attn_grad_sum(cfg, g_1, g_2, col0, dproj, nm)
    ddt_b = ddt.astype(BF16)
    g_w_main = _mm(hn, dproj, "tn", BF16, "g_w_main")
    g_w_dt = _mm(hn, ddt_b, "tn", BF16, "g_w_dt")
    grads = dict(w_main=g_w_main, w_dt=g_w_dt, conv_w=g_cw, conv_b=g_cb, dt_bias=g_dtb, a_log=g_alog,
                 d_skip=g_dsk, ssm_norm_w=g_snw, w_attn=g_w_attn, w_ssm=g_w_ssm, w_out=g_w_out, final_norm_w=g_fnw)
    sent = to_chips(grads) if to_chips is not None else ()
    dhn = _mm(dproj, w["w_main"], "nt", F32, "d_hn", tk=1024, carry=_scatter_carry(sent) if sent else None)
    landed = ()
    if sent:
        dhn, landed = dhn
    dhn_dt = _mm(ddt_b, w["w_dt"], "nt", F32, "d_hn_dt")
    grad_x, grads["norm_w"] = _rmsnorm_bwd(x, w["norm_w"], dhn, dhn_dt, dout_f)
    return loss_row, grad_x, grads, sent, landed


def _pad_lanes(v):
    return jnp.pad(v, ((0, 0), (0, LANES - v.shape[1])))


def _cut(lo, hi, a, b):
    a, b = max(lo, a), min(hi, b)
    return (a, b) if a < b else None


def _main_from_shards(cfg, shards):
    per = cfg.N_IN // len(shards)
    main, dt = [], []
    for j, sh in enumerate(shards):
        lo, hi = j * per, (j + 1) * per
        for dst, rng in ((main, (0, cfg.OGA)), (dt, (cfg.OGA, cfg.OGA + cfg.NH)), (main, (cfg.OGA + cfg.NH, cfg.N_IN))):
            c = _cut(lo, hi, *rng)
            if c is not None:
                dst.append(sh[:, c[0] - lo:c[1] - lo])
    return jnp.concatenate(main, axis=1), _pad_lanes(jnp.concatenate(dt, axis=1))


def _shards_from_main(cfg, g_main, g_dt, n):
    per = cfg.N_IN // n
    out = []
    for j in range(n):
        lo, hi = j * per, (j + 1) * per
        parts = []
        for src, off, rng in ((g_main, 0, (0, cfg.OGA)), (g_dt, cfg.OGA, (cfg.OGA, cfg.OGA + cfg.NH)),
                              (g_main, cfg.NH, (cfg.OGA + cfg.NH, cfg.N_IN))):
            c = _cut(lo, hi, *rng)
            if c is not None:
                parts.append(src[:, c[0] - off:c[1] - off])
        out.append(jnp.concatenate(parts, axis=1) if len(parts) > 1 else parts[0])
    return out


def _full_weights(cfg, norm_w, w_in_shards, conv_w, conv_b, dt_bias, a_log, d_skip, ssm_norm_w, w_attn, w_ssm, w_out, fnw):
    w_main, w_dt = _main_from_shards(cfg, w_in_shards)
    return dict(norm_w=norm_w, w_main=w_main.astype(BF16), w_dt=w_dt.astype(BF16), conv_w=conv_w, conv_b=conv_b,
                dt_bias=_pad_lanes(dt_bias), a_log=_pad_lanes(a_log), d_skip=_pad_lanes(d_skip), ssm_norm_w=ssm_norm_w,
                final_norm_w=fnw, **{k: v.astype(BF16) for k, v in (("w_attn", w_attn), ("w_ssm", w_ssm), ("w_out", w_out))
                                     if v is not None})


def _grad_w_in(cfg, grads):
    return _shards_from_main(cfg, grads["w_main"], grads["w_dt"], 1)[0]


def kernel(x, norm_w, w_in, conv_w, conv_b, dt_bias, a_log, d_skip, ssm_norm_w, w_attn_branch, w_ssm_branch, w_out, final_norm_w, loss_target, m_norm_w, m_w_in, m_conv_w, m_conv_b, m_dt_bias, m_a_log, m_d_skip, m_ssm_norm_w, m_w_attn_branch, m_w_ssm_branch, m_w_out, m_final_norm_w, v_norm_w, v_w_in, v_conv_w, v_conv_b, v_dt_bias, v_a_log, v_d_skip, v_ssm_norm_w, v_w_attn_branch, v_w_ssm_branch, v_w_out, v_final_norm_w):
    cfg = _Cfg(x.shape[1], x.shape[2])
    d, si, cd, nh = cfg.D, cfg.SI, cfg.CD, cfg.NH
    chip = 2 * lax.axis_index("x") + lax.axis_index("y")
    core = lax.axis_index("c").astype(jnp.int32).reshape(1)
    chip = chip.astype(jnp.int32)
    chip_core = [chip.reshape(1), core] + [jnp.where(chip == j, (j + 1) % N_CHIPS, j).astype(jnp.int32).reshape(1)
                                           for j in range(N_CHIPS)]

    own = [w_in[0].astype(BF16), conv_w[0].reshape(4 * CONV_K, -1)]
    a_in, a_cw = [_with_own(g, o, chip) for g, o in zip(_run_carry(_gather_carry(own), "gather_weights"), own)]
    conv_w_full = a_cw.reshape(N_CHIPS, CONV_K, cd // N_CHIPS).transpose(1, 0, 2).reshape(CONV_K, cd)
    w = _full_weights(cfg, norm_w, [a_in[j] for j in range(N_CHIPS)], conv_w_full, conv_b, dt_bias, a_log, d_skip,
                      ssm_norm_w, None, None, None, final_norm_w.reshape(1, d))
    own_late = [w_attn_branch[0].astype(BF16), w_ssm_branch[0].astype(BF16), w_out[0].astype(BF16)]

    def late_weights(arrived):
        a_attn, a_ssm, a_out = [_with_own(g, o, chip) for g, o in zip(arrived, own_late)]
        return dict(w_attn=a_attn.reshape(d, d), w_ssm=a_ssm.reshape(si, d), w_out=a_out.reshape(d, d))

    def to_chips(grads):
        by_chip = [jnp.stack(_shards_from_main(cfg, grads["w_main"], grads["w_dt"], N_CHIPS)),
                   grads["w_attn"].reshape(N_CHIPS, d // N_CHIPS, d),
                   grads["w_ssm"].reshape(N_CHIPS, si // N_CHIPS, d),
                   grads["w_out"].reshape(N_CHIPS, d // N_CHIPS, d)]
        from_sibling = _exchange_halves(by_chip)
        return [_add_sibling(g, r, core) for g, r in zip(by_chip, from_sibling)]

    loss_row, grad_x, grads, chip_sums, from_chips = _local_step(
        cfg, x[0], loss_target[0], w, to_chips, (_gather_carry(own_late), late_weights))
    halves = [_add_chips(o, p, chip_core) for o, p in zip(chip_sums, from_chips)]
    g_in, g_attn, g_ssm, g_out = [h.reshape(2 * h.shape[1], h.shape[2]) for h in _share_halves(halves)]

    small = [loss_row, grads["norm_w"], grads["conv_b"], grads["dt_bias"], grads["a_log"], grads["d_skip"],
             grads["ssm_norm_w"], grads["final_norm_w"], grads["conv_w"].reshape(1, CONV_K * cd)]
    sizes = [a.shape[1] for a in small]
    total = sum(sizes)
    rows = -(-total // (8 * LANES)) * 8
    flat = jnp.pad(jnp.concatenate(small, axis=1), ((0, 0), (0, rows * LANES - total)))
    red = _allreduce_small(flat.reshape(rows, LANES)).reshape(1, rows * LANES)
    offs = [sum(sizes[:i]) for i in range(len(sizes))]
    loss_r, g_nw, g_cb, g_dtb, g_alog, g_dsk, g_snw, g_fnw, g_cw_flat = [
        red[:, o:o + n] for o, n in zip(offs, sizes)]
    loss = loss_r[0, 0]
    g_dtb, g_alog, g_dsk = g_dtb[:, :nh], g_alog[:, :nh], g_dsk[:, :nh]
    cshard = cd // N_CHIPS
    g_cw = lax.dynamic_slice_in_dim(g_cw_flat.reshape(CONV_K, cd), chip * cshard, cshard, axis=1)

    upd = {}
    for name, wv, gv, mv, vv in [("w_in", w_in[0], g_in, m_w_in[0], v_w_in[0]),
                                 ("w_attn", w_attn_branch[0], g_attn, m_w_attn_branch[0], v_w_attn_branch[0]),
                                 ("w_ssm", w_ssm_branch[0], g_ssm, m_w_ssm_branch[0], v_w_ssm_branch[0]),
                                 ("w_out", w_out[0], g_out, m_w_out[0], v_w_out[0])]:
        upd[name] = _adamw(wv, gv, mv, vv, "adamw_" + name)
    names = ["norm_w", "conv_w", "conv_b", "dt_bias", "a_log", "d_skip", "ssm_norm_w", "final_norm_w"]
    ws = [norm_w, conv_w[0].reshape(1, -1), conv_b, dt_bias, a_log, d_skip, ssm_norm_w, final_norm_w.reshape(1, d)]
    gs = [g_nw, g_cw.reshape(1, -1), g_cb, g_dtb, g_alog, g_dsk, g_snw, g_fnw]
    ms = [m_norm_w, m_conv_w[0].reshape(1, -1), m_conv_b, m_dt_bias, m_a_log, m_d_skip, m_ssm_norm_w,
          m_final_norm_w.reshape(1, d)]
    vs = [v_norm_w, v_conv_w[0].reshape(1, -1), v_conv_b, v_dt_bias, v_a_log, v_d_skip, v_ssm_norm_w,
          v_final_norm_w.reshape(1, d)]
    ssz = [a.shape[1] for a in ws]
    stot = sum(ssz)
    srows = -(-stot // (8 * LANES)) * 8

    def pack(parts):
        return jnp.pad(jnp.concatenate(parts, axis=1), ((0, 0), (0, srows * LANES - stot))).reshape(srows, LANES)

    packed = _adamw(pack(ws), pack(gs), pack(ms), pack(vs), "adamw_small")
    soffs = [sum(ssz[:i]) for i in range(len(ssz))]
    for k, nm in enumerate(names):
        upd[nm] = tuple(p.reshape(1, srows * LANES)[:, soffs[k]:soffs[k] + ssz[k]] for p in packed)

    shapes = dict(norm_w=norm_w.shape, w_in=w_in.shape, conv_w=conv_w.shape, conv_b=conv_b.shape, dt_bias=dt_bias.shape,
                  a_log=a_log.shape, d_skip=d_skip.shape, ssm_norm_w=ssm_norm_w.shape, w_attn=w_attn_branch.shape,
                  w_ssm=w_ssm_branch.shape, w_out=w_out.shape, final_norm_w=final_norm_w.shape)
    order = ["norm_w", "w_in", "conv_w", "conv_b", "dt_bias", "a_log", "d_skip", "ssm_norm_w", "w_attn", "w_ssm",
             "w_out", "final_norm_w"]
    gradv = dict(norm_w=g_nw, w_in=g_in, conv_w=g_cw, conv_b=g_cb, dt_bias=g_dtb, a_log=g_alog, d_skip=g_dsk,
                 ssm_norm_w=g_snw, w_attn=g_attn, w_ssm=g_ssm, w_out=g_out, final_norm_w=g_fnw)
    outs = [loss, grad_x[None]]
    outs += [gradv[n].reshape(shapes[n]) for n in order]
    for k in range(3):
        outs += [upd[n][k].reshape(shapes[n]) for n in order]
    return tuple(outs)
```

```python
import functools
import math

import jax
import jax.numpy as jnp
from jax import lax
from jax.experimental import pallas as pl
from jax.experimental.pallas import tpu as pltpu

F32 = jnp.float32
BF16 = jnp.bfloat16
SDS = jax.ShapeDtypeStruct

RMS_EPS = 1e-6
LANES = 128
CHUNK = 128
SSM_HEAD_DIM = 64
SSM_GROUPS = 8
SSM_STATE = 128
CONV_K = 4
ATTN_HEAD_DIM = 128
DILATED_PATTERNS = ((128, 1), (512, 4), (2048, 16))
ATTN_WINDOW = max(w for w, _ in DILATED_PATTERNS)
NEG = -1e30
VMEM_LIMIT = 56 * 1024 * 1024
ADAM_LR, ADAM_B1, ADAM_B2, ADAM_EPS, ADAM_WD, ADAM_STEP = 0.001, 0.9, 0.999, 1e-08, 0.01, 10
MESH = pl.DeviceIdType.MESH
N_CHIPS = 4
N_DEV = 8


class _Cfg:
    def __init__(self, s, d):
        self.S, self.D = s, d
        self.H = d // ATTN_HEAD_DIM
        self.SI = 2 * d
        self.NH = self.SI // SSM_HEAD_DIM
        self.HPG = self.NH // SSM_GROUPS
        self.GW = self.HPG * SSM_HEAD_DIM
        self.BC = SSM_GROUPS * SSM_STATE
        self.CD = self.SI + 2 * self.BC
        self.OQ, self.OK, self.OV, self.OZA = 0, d, 2 * d, 3 * d
        self.OZS = 4 * d
        self.OXBC = self.OZS + self.SI
        self.OGA = self.OXBC + self.CD
        self.OGS = self.OGA + d
        self.NM = self.OGS + d
        self.N_IN = self.NM + self.NH
        assert self.GW % LANES == 0 and self.NH <= LANES and s % 512 == 0 and d % 512 == 0


def _params(sem=None):
    return pltpu.CompilerParams(dimension_semantics=sem, vmem_limit_bytes=VMEM_LIMIT)


def _sigmoid(x):
    return 1.0 / (1.0 + jnp.exp(-x))


def _softplus(x):
    u = jnp.exp(-jnp.abs(x))
    l1p = jnp.where(u < 1e-3, u * (1.0 - u * (0.5 - u * (1.0 / 3.0))), jnp.log(1.0 + u))
    return jnp.maximum(x, 0.0) + l1p


def _nt(a, b):
    return lax.dot_general(a, b, (((1,), (1,)), ((), ())), preferred_element_type=F32)


def _tn(a, b):
    return lax.dot_general(a, b, (((0,), (0,)), ((), ())), preferred_element_type=F32)


def _nn(a, b):
    return jnp.dot(a, b, preferred_element_type=F32)


def _tile(n, target):
    if n <= target:
        return n
    best = None
    for t in range(LANES, target + 1, LANES):
        if n % t == 0:
            best = t
    assert best is not None, (n, target)
    return best


MM_TK = {"nn": 2048, "nt": 2048, "tn": 1024}


def _mm(a, b, dims, out_dtype, name, tm=1024, tn=2048, tk=None, init=None, carry=None):
    tk = MM_TK[dims] if tk is None else tk
    if dims == "nn":
        (m, k), (k2, n) = a.shape, b.shape
    elif dims == "nt":
        (m, k), (n, k2) = a.shape, b.shape
    else:
        (k, m), (k2, n) = a.shape, b.shape
    assert k == k2
    tm, tn, tk = _tile(m, tm), _tile(n, tn), _tile(k, tk)
    nk = k // tk
    if dims == "tn":
        a_spec = pl.BlockSpec((tk, tm), lambda i, j, kk: (kk, i))
    else:
        a_spec = pl.BlockSpec((tm, tk), lambda i, j, kk: (i, kk))
    if dims == "nt":
        b_spec = pl.BlockSpec((tn, tk), lambda i, j, kk: (j, kk))
    else:
        b_spec = pl.BlockSpec((tk, tn), lambda i, j, kk: (kk, j))
    o_spec = pl.BlockSpec((tm, tn), lambda i, j, kk: (i, j))
    op = {"nn": _nn, "nt": _nt, "tn": _tn}[dims]
    has_init = init is not None
    nx = len(carry.arrays) if carry is not None else 0
    ni, nj = m // tm, n // tn

    def body(*refs):
        a_ref, b_ref = refs[0], refs[1]
        i_ref = refs[2] if has_init else None
        x_in = refs[2 + has_init:2 + has_init + nx]
        o_ref = refs[2 + has_init + nx]
        x_out = refs[3 + has_init + nx:3 + has_init + 2 * nx]
        acc = refs[3 + has_init + 2 * nx]
        x_sems = refs[4 + has_init + 2 * nx:]
        i, j, kk = pl.program_id(0), pl.program_id(1), pl.program_id(2)

        if nx:
            @pl.when((i == 0) & (j == 0) & (kk == 0))
            def _():
                carry.start(x_in, x_out, x_sems)

        prod = lambda: op(a_ref[...], b_ref[...])
        with_init = (lambda p: p + i_ref[...].astype(F32)) if has_init else (lambda p: p)
        if nk == 1:
            o_ref[...] = with_init(prod()).astype(out_dtype)
        else:
            @pl.when(kk == 0)
            def _():
                acc[...] = with_init(prod())

            @pl.when((kk > 0) & (kk < nk - 1))
            def _():
                acc[...] += prod()

            @pl.when(kk == nk - 1)
            def _():
                o_ref[...] = (acc[...] + prod()).astype(out_dtype)

        if nx:
            @pl.when((i == ni - 1) & (j == nj - 1) & (kk == nk - 1))
            def _():
                carry.finish(x_in, x_out, x_sems)

    in_specs = [a_spec, b_spec] + ([o_spec] if has_init else []) + [HBM_SPEC] * nx
    args = (a, b) + ((init,) if has_init else ()) + (tuple(carry.arrays) if nx else ())
    sems = carry.sem_shapes() if nx else []
    outs = pl.pallas_call(
        body, out_shape=[SDS((m, n), out_dtype)] + (carry.out_shapes if nx else []), grid=(ni, nj, nk),
        in_specs=in_specs, out_specs=[o_spec] + [HBM_SPEC] * nx,
        scratch_shapes=[pltpu.VMEM((tm, tn) if nk > 1 else (8, LANES), F32)] + sems,
        compiler_params=_params(("arbitrary",) * 3 if nx else ("parallel", "parallel", "arbitrary")), name=name)(*args)
    return (outs[0], outs[1:]) if nx else outs[0]


def _rmsnorm_fwd(x, w):
    s, d = x.shape
    tr = 256

    def body(x_ref, w_ref, o_ref):
        xv = x_ref[...]
        r = lax.rsqrt(jnp.mean(xv * xv, axis=-1, keepdims=True) + RMS_EPS)
        o_ref[...] = (xv * r * w_ref[...]).astype(BF16)

    return pl.pallas_call(
        body, out_shape=SDS((s, d), BF16), grid=(s // tr,),
        in_specs=[pl.BlockSpec((tr, d), lambda i: (i, 0)), pl.BlockSpec((1, d), lambda i: (0, 0))],
        out_specs=pl.BlockSpec((tr, d), lambda i: (i, 0)),
        compiler_params=_params(("parallel",)), name="rmsnorm_fwd")(x, w)


def _rmsnorm_bwd(x, w, dhn_a, dhn_b, dout):
    s, d = x.shape
    tr = 256

    def body(x_ref, w_ref, dh_ref, dh2_ref, do_ref, gx_ref, gw_ref):
        xv = x_ref[...]
        r = lax.rsqrt(jnp.mean(xv * xv, axis=-1, keepdims=True) + RMS_EPS)
        nrm = xv * r
        dh = dh_ref[...] + dh2_ref[...]
        gy = dh * w_ref[...]
        gx_ref[...] = do_ref[...] + r * (gy - nrm * jnp.mean(gy * nrm, axis=-1, keepdims=True))

        @pl.when(pl.program_id(0) == 0)
        def _():
            gw_ref[...] = jnp.zeros_like(gw_ref)

        gw_ref[...] += jnp.sum(dh * nrm, axis=0, keepdims=True)

    blk = pl.BlockSpec((tr, d), lambda i: (i, 0))
    row = pl.BlockSpec((1, d), lambda i: (0, 0))
    return pl.pallas_call(
        body, out_shape=(SDS((s, d), F32), SDS((1, d), F32)), grid=(s // tr,),
        in_specs=[blk, row, blk, blk, blk], out_specs=(blk, row),
        compiler_params=_params(("arbitrary",)), name="rmsnorm_bwd")(x, w, dhn_a, dhn_b, dout)


DEINT = DILATED_PATTERNS[-1][1]
DEINT_ROWS = DEINT * LANES


class _Pass:
    def __init__(self, tq, patterns, unit, seg_len):
        self.tq, self.patterns, self.unit, self.seg_len = tq, patterns, unit, seg_len
        self.win = max(w for w, _ in patterns) // unit
        self.w = self.win + tq
        assert self.win % tq == 0


def _attn_tables(ps):
    i = jnp.arange(ps.tq, dtype=jnp.int32)[:, None]
    j = jnp.arange(ps.w, dtype=jnp.int32)[None, :]
    delta = (i + ps.win - j) * ps.unit
    n = jnp.zeros((ps.tq, ps.w), F32)
    for window, dil in ps.patterns:
        n = n + ((delta >= 0) & (delta <= window) & (delta % dil == 0)).astype(F32)
    logn = jnp.where(n > 0, jnp.log(jnp.maximum(n, 1.0)), NEG)
    return logn, jnp.maximum(delta, 0).astype(F32)


def _slopes(h):
    s = jnp.asarray([2.0 ** (-8.0 * (i + 1) / h) for i in range(h)], F32)
    return jnp.broadcast_to(s[:, None, None], (h, 1, LANES))


def _masked_logn(ps, logn_ref, start):
    col = lax.broadcasted_iota(jnp.int32, (ps.tq, ps.w), 1)
    return jnp.where(col >= ps.win - lax.rem(start, ps.seg_len), logn_ref[...], NEG)


def _head_cols(hh):
    return slice(hh * ATTN_HEAD_DIM, (hh + 1) * ATTN_HEAD_DIM)


def _head_window(refs, cs):
    return jnp.concatenate([r[:, cs] for r in refs], axis=0)


def _head_scores(q_ref, kw, cs, base, dist_ref, slope_ref, hh):
    return _nt(q_ref[:, cs], kw) * (ATTN_HEAD_DIM ** -0.5) + (base - slope_ref[hh][0:1, 0:1] * dist_ref[...])


def _lane_of(stat, hh):
    lane = lax.broadcasted_iota(jnp.int32, stat.shape, 1)
    return jnp.sum(jnp.where(lane == hh, stat, 0.0), axis=1, keepdims=True)


def _window_specs(ps, d, col, nb):
    nprev = ps.win // ps.tq
    return [pl.BlockSpec((ps.tq, d), lambda i, b=b: (jnp.maximum(jnp.minimum(i, nb - 1) - (nprev - b), 0), col))
            for b in range(nprev + 1)]


def _attn_fwd(cfg, ps, qkv, cols, tables, slopes, name):
    s, h, d = cfg.S, cfg.H, cfg.D
    tq, nw = ps.tq, ps.win // ps.tq + 1
    nb = s // tq
    logn, dist = tables
    qc, kc, vc = [c // d for c in cols]

    def body(*refs):
        q_ref, k_refs, v_refs = refs[0], refs[1:1 + nw], refs[1 + nw:1 + 2 * nw]
        logn_ref, dist_ref, slope_ref, o_ref, lse_ref = refs[1 + 2 * nw:]
        base = _masked_logn(ps, logn_ref, pl.program_id(0) * tq)
        lane = lax.broadcasted_iota(jnp.int32, (tq, LANES), 1)

        lse = jnp.zeros((tq, LANES), F32)
        for hh in range(h):
            cs = _head_cols(hh)
            sc = _head_scores(q_ref, _head_window(k_refs, cs), cs, base, dist_ref, slope_ref, hh)
            m = jnp.max(sc, axis=1, keepdims=True)
            p = jnp.exp(sc - m)
            l = jnp.sum(p, axis=1, keepdims=True)
            o_ref[:, cs] = (_nn(p.astype(BF16), _head_window(v_refs, cs)) / l).astype(BF16)
            lse = jnp.where(lane == hh, m + jnp.log(l), lse)
        lse_ref[...] = lse

    tab = pl.BlockSpec((tq, ps.w), lambda i: (0, 0))
    return pl.pallas_call(
        body, out_shape=(SDS((s, d), BF16), SDS((s, LANES), F32)), grid=(nb,),
        in_specs=[pl.BlockSpec((tq, d), lambda i: (i, qc))] + _window_specs(ps, d, kc, nb) + _window_specs(ps, d, vc, nb)
        + [tab, tab, pl.BlockSpec((h, 1, LANES), lambda i: (0, 0, 0))],
        out_specs=(pl.BlockSpec((tq, d), lambda i: (i, 0)), pl.BlockSpec((tq, LANES), lambda i: (i, 0))),
        compiler_params=_params(("parallel",)), name=name)(*([qkv] * (1 + 2 * nw)), logn, dist, slopes)


def _attn_bwd(cfg, ps, qkv, cols, do, lse, delta, tables, slopes, name):
    s, h, d = cfg.S, cfg.H, cfg.D
    tq, nprev = ps.tq, ps.win // ps.tq
    nw = nprev + 1
    nb = s // tq
    logn, dist = tables
    qc, kc, vc = [c // d for c in cols]
    scale = ATTN_HEAD_DIM ** -0.5

    def body(*refs):
        q_ref, k_refs, v_refs = refs[0], refs[1:1 + nw], refs[1 + nw:1 + 2 * nw]
        do_ref, lse_ref, dl_ref, logn_ref, dist_ref, slope_ref, dq_ref, dk_ref, dv_ref, ck, cv = refs[1 + 2 * nw:]
        i = pl.program_id(0)
        slot = lambda b: lax.rem(i + b, nprev)

        @pl.when(i == 0)
        def _():
            ck[...] = jnp.zeros_like(ck)
            cv[...] = jnp.zeros_like(cv)

        @pl.when(i < nb)
        def _():
            base = _masked_logn(ps, logn_ref, i * tq)
            lse_all, dl_all = lse_ref[...], dl_ref[...]

            for hh in range(h):
                cs = _head_cols(hh)
                kw, vw = _head_window(k_refs, cs), _head_window(v_refs, cs)
                sc = _head_scores(q_ref, kw, cs, base, dist_ref, slope_ref, hh)
                p = jnp.exp(sc - lse_all[:, hh:hh + 1])
                dob = do_ref[:, cs]
                ds = (p * (_nt(dob, vw) - dl_all[:, hh:hh + 1]) * scale).astype(BF16)
                dq_ref[:, cs] = _nn(ds, kw).astype(BF16)
                dkw = _tn(ds, q_ref[:, cs])
                dvw = _tn(p.astype(BF16), dob)
                dk_ref[:, cs] = ck[slot(0), :, cs] + dkw[0:tq]
                dv_ref[:, cs] = cv[slot(0), :, cs] + dvw[0:tq]
                for b in range(1, nprev):
                    ck[slot(b), :, cs] += dkw[b * tq:(b + 1) * tq]
                    cv[slot(b), :, cs] += dvw[b * tq:(b + 1) * tq]
                ck[slot(0), :, cs] = dkw[nprev * tq:]
                cv[slot(0), :, cs] = dvw[nprev * tq:]

        @pl.when(i >= nb)
        def _():
            dk_ref[...] = ck[slot(0)]
            dv_ref[...] = cv[slot(0)]

    here = lambda i: jnp.minimum(i, nb - 1)
    blk = pl.BlockSpec((tq, d), lambda i: (here(i), 0))
    stat = pl.BlockSpec((tq, LANES), lambda i: (here(i), 0))
    late = pl.BlockSpec((tq, d), lambda i: (jnp.maximum(i - nprev, 0), 0))
    tab = pl.BlockSpec((tq, ps.w), lambda i: (0, 0))
    return pl.pallas_call(
        body, out_shape=(SDS((s, d), BF16), SDS((s, d), F32), SDS((s, d), F32)), grid=(nb + nprev,),
        in_specs=[pl.BlockSpec((tq, d), lambda i: (here(i), qc))] + _window_specs(ps, d, kc, nb)
        + _window_specs(ps, d, vc, nb) + [blk, stat, stat, tab, tab, pl.BlockSpec((h, 1, LANES), lambda i: (0, 0, 0))],
        out_specs=(blk, late, late),
        scratch_shapes=[pltpu.VMEM((nprev, tq, d), F32), pltpu.VMEM((nprev, tq, d), F32)],
        compiler_params=_params(("arbitrary",)), name=name)(
            *([qkv] * (1 + 2 * nw)), do, lse, delta, logn, dist, slopes)


def _by_residue(a):
    return a.reshape(DEINT, a.shape[0] // DEINT, a.shape[1])


def _deint_spec(colblock):
    return pl.BlockSpec((DEINT, LANES, LANES), lambda b, j: (0, b, colblock(j)))


def _deint_rows(scr, out_ref, dtype):
    for r in range(DEINT):
        out_ref[r] = scr[pl.ds(r, LANES, stride=DEINT), :].astype(dtype)


def _int_rows(in_ref, scr):
    for r in range(DEINT):
        scr[pl.ds(r, LANES, stride=DEINT), :] = in_ref[r].astype(F32)


WIDE = 4 * LANES


def _wide_spec():
    return pl.BlockSpec((DEINT, LANES, WIDE), lambda b, j: (0, b, j))


def _deinterleave(x, col0, ncols, name):
    s = x.shape[0]
    c0 = col0 // WIDE

    def body(x_ref, o_ref, scr):
        for t in range(WIDE // LANES):
            cs = slice(t * LANES, (t + 1) * LANES)
            scr[t] = x_ref[:, cs].astype(F32)
            for r in range(DEINT):
                o_ref[r, :, cs] = scr.at[t][pl.ds(r, LANES, stride=DEINT), :].astype(x.dtype)

    out = pl.pallas_call(
        body, out_shape=SDS((DEINT, s // DEINT, ncols), x.dtype), grid=(s // DEINT_ROWS, ncols // WIDE),
        in_specs=[pl.BlockSpec((DEINT_ROWS, WIDE), lambda b, j: (b, c0 + j))],
        out_specs=_wide_spec(),
        scratch_shapes=[pltpu.VMEM((WIDE // LANES, DEINT_ROWS, LANES), F32)],
        compiler_params=_params(("parallel", "parallel")), name=name)(x)
    return out.reshape(s, ncols)


def _attn_merge(cfg, proj, o_1, lse_1, o_2, lse_2):
    s, h = cfg.S, cfg.H
    zb = cfg.OZA // LANES
    rows = DEINT_ROWS

    def body(o1_ref, l1_ref, o2_ref, l2_ref, z_ref, o_ref, og_ref, lse_ref, so, sl):
        hh = pl.program_id(1)
        _int_rows(o2_ref, so)

        @pl.when(hh == 0)
        def _():
            _int_rows(l2_ref, sl)

        l1, l2 = _lane_of(l1_ref[...], hh), _lane_of(sl[...], hh)
        mx = jnp.maximum(l1, l2)
        w1, w2 = jnp.exp(l1 - mx), jnp.exp(l2 - mx)
        den = w1 + w2
        o = (w1 * o1_ref[...].astype(F32) + w2 * so[...]) / den
        z = z_ref[...].astype(F32)
        o_ref[...] = o.astype(BF16)
        og_ref[...] = (o * (z * _sigmoid(z))).astype(BF16)

        @pl.when(hh == 0)
        def _():
            lse_ref[...] = jnp.zeros_like(lse_ref)

        lane = lax.broadcasted_iota(jnp.int32, (rows, LANES), 1)
        lse_ref[...] = jnp.where(lane == hh, mx + jnp.log(den), lse_ref[...])

    blk = pl.BlockSpec((rows, LANES), lambda b, j: (b, j))
    return pl.pallas_call(
        body, out_shape=(SDS((s, cfg.D), BF16), SDS((s, cfg.D), BF16), SDS((s, LANES), F32)),
        grid=(s // rows, h),
        in_specs=[blk, pl.BlockSpec((rows, LANES), lambda b, j: (b, 0)), _deint_spec(lambda j: j),
                  _deint_spec(lambda j: 0), pl.BlockSpec((rows, LANES), lambda b, j: (b, zb + j))],
        out_specs=(blk, blk, pl.BlockSpec((rows, LANES), lambda b, j: (b, 0))),
        scratch_shapes=[pltpu.VMEM((rows, LANES), F32), pltpu.VMEM((rows, LANES), F32)],
        compiler_params=_params(("parallel", "arbitrary")), name="attn_merge")(
            o_1, lse_1, _by_residue(o_2), _by_residue(lse_2), proj)


def _attn_bwd_prep(cfg, proj, o_a, doag, lse, dproj):
    s, h = cfg.S, cfg.H
    zb = cfg.OZA // LANES
    rows = DEINT_ROWS

    def body(o_ref, dg_ref, z_ref, lse_ref, dp_in, dz_ref, do_ref, do2_ref, dl_ref, dl2_ref, lse2_ref, scr):
        del dp_in
        hh = pl.program_id(1)
        z = z_ref[...].astype(F32)
        sg = _sigmoid(z)
        o = o_ref[...].astype(F32)
        dg = dg_ref[...].astype(F32)
        do = dg * (z * sg)
        dz_ref[...] = (dg * o * (sg * (1.0 + z * (1.0 - sg)))).astype(BF16)
        do_ref[...] = do.astype(BF16)
        scr[...] = do
        _deint_rows(scr, do2_ref, BF16)

        @pl.when(hh == 0)
        def _():
            dl_ref[...] = jnp.zeros_like(dl_ref)

        lane = lax.broadcasted_iota(jnp.int32, (rows, LANES), 1)
        dl_ref[...] = jnp.where(lane == hh, jnp.sum(do * o, axis=1, keepdims=True), dl_ref[...])

        @pl.when(hh == h - 1)
        def _():
            scr[...] = dl_ref[...]
            _deint_rows(scr, dl2_ref, F32)
            scr[...] = lse_ref[...]
            _deint_rows(scr, lse2_ref, F32)

    blk = pl.BlockSpec((rows, LANES), lambda b, j: (b, j))
    stat = pl.BlockSpec((rows, LANES), lambda b, j: (b, 0))
    stat2 = _deint_spec(lambda j: 0)
    outs = pl.pallas_call(
        body,
        out_shape=(SDS(dproj.shape, BF16), SDS((s, cfg.D), BF16), SDS((DEINT, s // DEINT, cfg.D), BF16),
                   SDS((s, LANES), F32), SDS((DEINT, s // DEINT, LANES), F32), SDS((DEINT, s // DEINT, LANES), F32)),
        grid=(s // rows, h),
        in_specs=[blk, blk, pl.BlockSpec((rows, LANES), lambda b, j: (b, zb + j)), stat, HBM_SPEC],
        out_specs=(pl.BlockSpec((rows, LANES), lambda b, j: (b, zb + j)), blk, _deint_spec(lambda j: j),
                   stat, stat2, stat2),
        scratch_shapes=[pltpu.VMEM((rows, LANES), F32)],
        input_output_aliases={4: 0},
        compiler_params=_params(("parallel", "arbitrary")), name="attn_bwd_prep")(o_a, doag, proj, lse, dproj)
    dproj, do, do2, dl, dl2, lse2 = outs
    return dproj, do, do2.reshape(s, cfg.D), dl, dl2.reshape(s, LANES), lse2.reshape(s, LANES)


def _attn_grad_sum(cfg, g_1, g_2, col0, dproj, name):
    s = cfg.S
    c0 = col0 // WIDE
    rows = DEINT_ROWS

    def body(g1_ref, g2_ref, dp_in, o_ref, scr):
        del dp_in
        for t in range(WIDE // LANES):
            cs = slice(t * LANES, (t + 1) * LANES)
            for r in range(DEINT):
                scr.at[t][pl.ds(r, LANES, stride=DEINT), :] = g2_ref[r, :, cs].astype(F32)
            o_ref[:, cs] = (g1_ref[:, cs].astype(F32) + scr[t]).astype(BF16)

    return pl.pallas_call(
        body, out_shape=SDS(dproj.shape, BF16), grid=(s // rows, cfg.D // WIDE),
        in_specs=[pl.BlockSpec((rows, WIDE), lambda b, j: (b, j)), _wide_spec(), HBM_SPEC],
        out_specs=pl.BlockSpec((rows, WIDE), lambda b, j: (b, c0 + j)),
        scratch_shapes=[pltpu.VMEM((WIDE // LANES, rows, LANES), F32)],
        input_output_aliases={2: 0},
        compiler_params=_params(("parallel", "parallel")), name=name)(g_1, _by_residue(g_2), dproj)


CONV_HALO = 16
CONV_TR = 512
CONV_CW = 1024


def _rows_back(a, n):
    return a if n == 0 else pltpu.roll(a, n % a.shape[0], axis=0)


def _conv_fwd(cfg, proj, conv_w, conv_b):
    s, cd = cfg.S, cfg.CD
    tr, cw, hl = CONV_TR, CONV_CW, CONV_HALO
    cb0 = cfg.OXBC // cw

    def body(x_ref, h_ref, w_ref, b_ref, o_ref):
        i = pl.program_id(0)
        halo = jnp.where(i > 0, h_ref[...].astype(F32), 0.0)
        ext = jnp.concatenate([halo, x_ref[...].astype(F32)], axis=0)
        pre = b_ref[...] + jnp.zeros((tr, cw), F32)
        for k in range(CONV_K):
            pre = pre + w_ref[k:k + 1, :] * _rows_back(ext, CONV_K - 1 - k)[hl:]
        o_ref[...] = (pre * _sigmoid(pre)).astype(BF16)

    return pl.pallas_call(
        body, out_shape=SDS((s, cd), BF16), grid=(s // tr, cd // cw),
        in_specs=[pl.BlockSpec((tr, cw), lambda i, j: (i, cb0 + j)),
                  pl.BlockSpec((hl, cw), lambda i, j: (jnp.maximum(i * (tr // hl) - 1, 0), cb0 + j)),
                  pl.BlockSpec((CONV_K, cw), lambda i, j: (0, j)),
                  pl.BlockSpec((1, cw), lambda i, j: (0, j))],
        out_specs=pl.BlockSpec((tr, cw), lambda i, j: (i, j)),
        compiler_params=_params(("parallel", "parallel")), name="conv_fwd")(proj, proj, conv_w, conv_b)


def _conv_bwd(cfg, proj, dact, conv_w, conv_b, dproj):
    s, cd = cfg.S, cfg.CD
    tr, cw, hl = CONV_TR, CONV_CW, CONV_HALO
    cb0 = cfg.OXBC // cw
    nr = s // tr
    last_h = s // hl - 1

    def body(x_ref, hp_ref, hn_ref, d_ref, dn_ref, w_ref, b_ref, dp_in, dx_ref, gw_ref, gb_ref):
        del dp_in
        i = pl.program_id(1)
        ext = jnp.concatenate([jnp.where(i > 0, hp_ref[...].astype(F32), 0.0), x_ref[...].astype(F32),
                               hn_ref[...].astype(F32)], axis=0)
        shifted = [_rows_back(ext, CONV_K - 1 - k)[hl:] for k in range(CONV_K)]
        pre = b_ref[...] + jnp.zeros((tr + hl, cw), F32)
        for k in range(CONV_K):
            pre = pre + w_ref[k:k + 1, :] * shifted[k]
        sg = _sigmoid(pre)
        dact = jnp.concatenate([d_ref[...].astype(F32), jnp.where(i < nr - 1, dn_ref[...].astype(F32), 0.0)], axis=0)
        dpre = dact * (sg * (1.0 + pre * (1.0 - sg)))
        dx = jnp.zeros((tr, cw), F32)
        for k in range(CONV_K):
            dx = dx + w_ref[k:k + 1, :] * _rows_back(dpre, -(CONV_K - 1 - k))[0:tr]
        dx_ref[...] = dx.astype(BF16)

        @pl.when(i == 0)
        def _():
            gw_ref[...] = jnp.zeros_like(gw_ref)
            gb_ref[...] = jnp.zeros_like(gb_ref)

        dcur = dpre[0:tr]
        gb_ref[...] += jnp.sum(dcur, axis=0, keepdims=True)
        for k in range(CONV_K):
            gw_ref[k:k + 1, :] += jnp.sum(dcur * shifted[k][0:tr], axis=0, keepdims=True)

    return pl.pallas_call(
        body, out_shape=(SDS(dproj.shape, BF16), SDS((CONV_K, cd), F32), SDS((1, cd), F32)), grid=(cd // cw, nr),
        in_specs=[pl.BlockSpec((tr, cw), lambda j, i: (i, cb0 + j)),
                  pl.BlockSpec((hl, cw), lambda j, i: (jnp.maximum(i * (tr // hl) - 1, 0), cb0 + j)),
                  pl.BlockSpec((hl, cw), lambda j, i: (jnp.minimum((i + 1) * (tr // hl), last_h), cb0 + j)),
                  pl.BlockSpec((tr, cw), lambda j, i: (i, j)),
                  pl.BlockSpec((hl, cw), lambda j, i: (jnp.minimum((i + 1) * (tr // hl), last_h), j)),
                  pl.BlockSpec((CONV_K, cw), lambda j, i: (0, j)),
                  pl.BlockSpec((1, cw), lambda j, i: (0, j)),
                  pl.BlockSpec(memory_space=pl.ANY)],
        out_specs=(pl.BlockSpec((tr, cw), lambda j, i: (i, cb0 + j)),
                   pl.BlockSpec((CONV_K, cw), lambda j, i: (0, j)),
                   pl.BlockSpec((1, cw), lambda j, i: (0, j))),
        input_output_aliases={7: 0},
        compiler_params=_params(("parallel", "arbitrary")), name="conv_bwd")(
            proj, proj, proj, dact, dact, conv_w, conv_b, dproj)


def _expand(v, e, terms):
    out, rem = None, v
    for _ in range(terms):
        hi = rem.astype(BF16)
        t = _nn(hi, e)
        out = t if out is None else out + t
        rem = rem - hi.astype(F32)
    return out


def _segsum(v, e, terms):
    out, rem = None, v
    for _ in range(terms):
        hi = rem.astype(BF16)
        t = _nt(hi, e)
        out = t if out is None else out + t
        rem = rem - hi.astype(F32)
    return out


def _expand_row(row, e, terms):
    return _expand(jnp.broadcast_to(row, (8, LANES)), e, terms)[0:1]


def _segsum_row(row, e, terms):
    return _segsum(jnp.broadcast_to(row, (8, row.shape[1])), e, terms)[0:1]


def _expansion_matrix(cfg):
    hh = jnp.arange(LANES, dtype=jnp.int32)[:, None]
    cc = jnp.arange(cfg.SI, dtype=jnp.int32)[None, :]
    return (cc // SSM_HEAD_DIM == hh).astype(BF16)


def _tri(lower):
    r = lax.broadcasted_iota(jnp.int32, (CHUNK, CHUNK), 0)
    c = lax.broadcasted_iota(jnp.int32, (CHUNK, CHUNK), 1)
    return (c <= r) if lower else (c >= r)


def _ssd_prep(dtr_ref, db_ref, al_ref, e):
    dtr = dtr_ref[...] + db_ref[...]
    dt = _softplus(dtr)
    a = -jnp.exp(al_ref[...])
    acum = jnp.dot(_tri(True).astype(F32), dt * a, precision=lax.Precision.HIGHEST, preferred_element_type=F32)
    return dtr, dt, a, _expand(dt, e, 2), _expand(acum, e, 3)


def _ssd_fwd(cfg, xact, dt_raw, proj, dt_bias, a_log, d_skip, norm_w, e):
    s, si, cd, gw, bc = cfg.S, cfg.SI, cfg.CD, cfg.GW, cfg.BC
    nc = s // CHUNK
    zb = cfg.OZS // si
    tiles = gw // LANES

    def body(xa_ref, dtr_ref, z_ref, db_ref, al_ref, dsk_ref, nw_ref, e_ref, y_ref, y2_ref, st_ref,
             state, ybuf, x_s, xw_s, ae_s, ea_s, lam_s):
        @pl.when(pl.program_id(0) == 0)
        def _():
            state[...] = jnp.zeros_like(state)

        st_ref[...] = state[...]
        ev = e_ref[...]
        _, _, _, dt_e, a_e = _ssd_prep(dtr_ref, db_ref, al_ref, ev)
        xs = xa_ref[:, 0:si].astype(F32)
        x = xs * dt_e
        lam_e = a_e[CHUNK - 1:CHUNK, :]
        x_s[...] = x.astype(BF16)
        xw_s[...] = (x * jnp.exp(lam_e - a_e)).astype(BF16)
        ae_s[...] = a_e
        ea_s[...] = jnp.exp(a_e)
        ybuf[...] = _expand_row(dsk_ref[...], ev, 3) * xs
        lam_s[...] = jnp.broadcast_to(jnp.exp(lam_e), (8, si))
        tril = _tri(True)
        lane = lax.broadcasted_iota(jnp.int32, (CHUNK, LANES), 1)

        def group(g, carry):
            co = pl.multiple_of(g * gw, LANES)
            bg = xa_ref[:, pl.ds(pl.multiple_of(si + g * SSM_STATE, LANES), SSM_STATE)]
            cg = xa_ref[:, pl.ds(pl.multiple_of(si + bc + g * SSM_STATE, LANES), SSM_STATE)]
            cbm = _nt(cg, bg)
            st = state[:, pl.ds(co, gw)]
            yoff = _nn(cg, st.astype(BF16)) * ea_s[:, pl.ds(co, gw)]
            for k in range(tiles):
                tc = pl.multiple_of(co + k * LANES, LANES)
                at = ae_s[:, pl.ds(tc, LANES)]
                att = at.T
                xt = x_s[:, pl.ds(tc, LANES)]
                acc = yoff[:, k * LANES:(k + 1) * LANES]
                for half in range(2):
                    lo = half * SSM_HEAD_DIM
                    seg = at[:, lo:lo + 1] - att[lo:lo + 1, :]
                    dec = jnp.exp(jnp.where(tril, seg, NEG))
                    xh = jnp.where((lane >= lo) & (lane < lo + SSM_HEAD_DIM), xt, jnp.zeros_like(xt))
                    acc = acc + _nn((cbm * dec).astype(BF16), xh)
                ybuf[:, pl.ds(tc, LANES)] += acc
            state[:, pl.ds(co, gw)] = st * lam_s[0:1, pl.ds(co, gw)] + _tn(bg, xw_s[:, pl.ds(co, gw)])
            return carry

        lax.fori_loop(0, SSM_GROUPS, group, 0)
        y = ybuf[...]
        y_ref[...] = y.astype(BF16)
        z = z_ref[...].astype(F32)
        u = y * (z * _sigmoid(z))
        r = lax.rsqrt(jnp.mean(u * u, axis=-1, keepdims=True) + RMS_EPS)
        y2_ref[...] = (u * r * nw_ref[...]).astype(BF16)

    row = lambda n: pl.BlockSpec((1, n), lambda c: (0, 0))
    return pl.pallas_call(
        body,
        out_shape=(SDS((s, si), BF16), SDS((s, si), BF16), SDS((nc, SSM_STATE, si), F32)),
        grid=(nc,),
        in_specs=[pl.BlockSpec((CHUNK, cd), lambda c: (c, 0)),
                  pl.BlockSpec((CHUNK, LANES), lambda c: (c, 0)),
                  pl.BlockSpec((CHUNK, si), lambda c: (c, zb)),
                  row(LANES), row(LANES), row(LANES), row(si),
                  pl.BlockSpec((LANES, si), lambda c: (0, 0))],
        out_specs=(pl.BlockSpec((CHUNK, si), lambda c: (c, 0)),
                   pl.BlockSpec((CHUNK, si), lambda c: (c, 0)),
                   pl.BlockSpec((None, SSM_STATE, si), lambda c: (c, 0, 0))),
        scratch_shapes=[pltpu.VMEM((SSM_STATE, si), F32), pltpu.VMEM((CHUNK, si), F32),
                        pltpu.VMEM((CHUNK, si), BF16), pltpu.VMEM((CHUNK, si), BF16),
                        pltpu.VMEM((CHUNK, si), F32), pltpu.VMEM((CHUNK, si), F32),
                        pltpu.VMEM((8, si), F32)],
        compiler_params=_params(("arbitrary",)), name="ssd_fwd")(
            xact, dt_raw, proj, dt_bias, a_log, d_skip, norm_w, e)


def _ssd_bwd(cfg, xact, dt_raw, proj, y, dy2, states, dt_bias, a_log, d_skip, norm_w, e, dproj):
    s, si, cd, gw, bc, hpg = cfg.S, cfg.SI, cfg.CD, cfg.GW, cfg.BC, cfg.HPG
    nc = s // CHUNK
    zb = cfg.OZS // si
    tiles = gw // LANES

    def body(xa_ref, dtr_ref, z_ref, y_ref, d2_ref, st_ref, db_ref, al_ref, dsk_ref, nw_ref, e_ref, dp_in,
             dz_ref, dxa_ref, ddt_ref, gnw_ref, gdb_ref, gal_ref, gds_ref,
             dh, dhn, xs_s, x_s, w_s, ae_s, ea_s, g_s, dx_s, dae_s, r_s, lam_s, dle_s):
        del dp_in

        @pl.when(pl.program_id(0) == 0)
        def _():
            dh[...] = jnp.zeros_like(dh)
            gnw_ref[...] = jnp.zeros_like(gnw_ref)
            gdb_ref[...] = jnp.zeros_like(gdb_ref)
            gal_ref[...] = jnp.zeros_like(gal_ref)
            gds_ref[...] = jnp.zeros_like(gds_ref)

        ev = e_ref[...]
        yv = y_ref[...].astype(F32)
        z = z_ref[...].astype(F32)
        sg = _sigmoid(z)
        sz = z * sg
        u = yv * sz
        r = lax.rsqrt(jnp.mean(u * u, axis=-1, keepdims=True) + RMS_EPS)
        nrm = u * r
        d2 = d2_ref[...].astype(F32)
        gnw_ref[...] += jnp.sum(d2 * nrm, axis=0, keepdims=True)
        gn = d2 * nw_ref[...]
        du = r * (gn - nrm * jnp.mean(gn * nrm, axis=-1, keepdims=True))
        gv = du * sz
        dz_ref[...] = (du * yv * (sg * (1.0 + z * (1.0 - sg)))).astype(BF16)
        g_s[...] = gv

        dtr, dt, a, dt_e, a_e = _ssd_prep(dtr_ref, db_ref, al_ref, ev)
        xs = xa_ref[:, 0:si].astype(F32)
        x = xs * dt_e
        lam_e = a_e[CHUNK - 1:CHUNK, :]
        xs_s[...] = xs
        x_s[...] = x
        w_s[...] = jnp.exp(lam_e - a_e)
        ae_s[...] = a_e
        ea_s[...] = jnp.exp(a_e)
        lam_s[...] = jnp.broadcast_to(jnp.exp(lam_e), (8, si))
        gds_ref[...] += _segsum_row(jnp.sum(gv * xs, axis=0, keepdims=True), ev, 2)
        r_s[...] = jnp.zeros_like(r_s)
        tril = _tri(True)
        lane = lax.broadcasted_iota(jnp.int32, (CHUNK, LANES), 1)
        sub = lax.broadcasted_iota(jnp.int32, (CHUNK, LANES), 0)

        def group(g, carry):
            co = pl.multiple_of(g * gw, LANES)
            bo = pl.multiple_of(si + g * SSM_STATE, LANES)
            cof = pl.multiple_of(si + bc + g * SSM_STATE, LANES)
            cols = pl.ds(co, gw)
            bg = xa_ref[:, pl.ds(bo, SSM_STATE)]
            cg = xa_ref[:, pl.ds(cof, SSM_STATE)]
            cbm = _nt(cg, bg)
            st = st_ref[:, cols]
            stb = st.astype(BF16)
            dho = dh[:, cols]
            dhob = dho.astype(BF16)
            ea = ea_s[:, cols]
            gg = g_s[:, cols]
            xg = x_s[:, cols]
            wg = w_s[:, cols]
            explam = lam_s[0:1, cols]
            yoff = _nn(cg, stb) * ea
            ga = (gg * ea).astype(BF16)
            dc = _nt(ga, stb)
            dhn[:, cols] = dho * explam + _tn(cg, ga)
            bdh = _nn(bg, dhob)
            db = _nt((xg * wg).astype(BF16), dhob)
            t = xg * bdh * wg
            dle_s[0:1, cols] = jnp.sum(t, axis=0, keepdims=True) + explam * jnp.sum(dho * st, axis=0, keepdims=True)
            dae_base = gg * yoff - t
            dxw = wg * bdh
            dcb = jnp.zeros((CHUNK, CHUNK), F32)
            for k in range(tiles):
                tc = pl.multiple_of(co + k * LANES, LANES)
                ksl = slice(k * LANES, (k + 1) * LANES)
                at = ae_s[:, pl.ds(tc, LANES)]
                att = at.T
                xt = xg[:, ksl].astype(BF16)
                gt = gg[:, ksl].astype(BF16)
                dxt = dxw[:, ksl]
                place = jnp.zeros((CHUNK, LANES), F32)
                for half in range(2):
                    lo = half * SSM_HEAD_DIM
                    seg = at[:, lo:lo + 1] - att[lo:lo + 1, :]
                    dec = jnp.exp(jnp.where(tril, seg, NEG))
                    mh = cbm * dec
                    gh = jnp.where((lane >= lo) & (lane < lo + SSM_HEAD_DIM), gt, jnp.zeros_like(gt))
                    dm = _nt(gh, xt)
                    dxt = dxt + _tn(mh.astype(BF16), gh)
                    dcb = dcb + dm * dec
                    dseg = dm * mh
                    place = place + jnp.where(lane == lo, jnp.sum(dseg, axis=1, keepdims=True), 0.0)
                    hidx = g * hpg + 2 * k + half
                    r_s[...] += jnp.where(sub == hidx, jnp.sum(dseg, axis=0, keepdims=True), 0.0)
                dx_s[:, pl.ds(tc, LANES)] = dxt
                dae_s[:, pl.ds(tc, LANES)] = dae_base[:, ksl] + place
            dcbb = dcb.astype(BF16)
            dxa_ref[:, pl.ds(bo, SSM_STATE)] = (db + _tn(dcbb, cg)).astype(BF16)
            dxa_ref[:, pl.ds(cof, SSM_STATE)] = (dc + _nn(dcbb, bg)).astype(BF16)
            return carry

        lax.fori_loop(0, SSM_GROUPS, group, 0)
        dlam = _segsum_row(dle_s[0:1, :], ev, 2)
        da_ = _segsum(dae_s[...], ev, 2) - r_s[...].T
        da_ = da_ + jnp.where(sub == CHUNK - 1, dlam, 0.0)
        dda = jnp.dot(_tri(False).astype(F32), da_, precision=lax.Precision.HIGHEST, preferred_element_type=F32)
        dxv = dx_s[...]
        xs = xs_s[...]
        ddt = dda * a + _segsum(dxv * xs, ev, 2)
        gal_ref[...] += jnp.sum(dda * dt, axis=0, keepdims=True) * a
        ddtr = ddt * _sigmoid(dtr)
        gdb_ref[...] += jnp.sum(ddtr, axis=0, keepdims=True)
        ddt_ref[...] = ddtr
        dxa_ref[:, 0:si] = (dxv * dt_e + g_s[...] * _expand_row(dsk_ref[...], ev, 3)).astype(BF16)
        dh[...] = dhn[...]

    rev = lambda c: nc - 1 - c
    row = lambda n: pl.BlockSpec((1, n), lambda c: (0, 0))
    big = lambda: pltpu.VMEM((CHUNK, si), F32)
    return pl.pallas_call(
        body,
        out_shape=(SDS(dproj.shape, BF16), SDS((s, cd), BF16), SDS((s, LANES), F32),
                   SDS((1, si), F32), SDS((1, LANES), F32), SDS((1, LANES), F32), SDS((1, LANES), F32)),
        grid=(nc,),
        in_specs=[pl.BlockSpec((CHUNK, cd), lambda c: (rev(c), 0)),
                  pl.BlockSpec((CHUNK, LANES), lambda c: (rev(c), 0)),
                  pl.BlockSpec((CHUNK, si), lambda c: (rev(c), zb)),
                  pl.BlockSpec((CHUNK, si), lambda c: (rev(c), 0)),
                  pl.BlockSpec((CHUNK, si), lambda c: (rev(c), 0)),
                  pl.BlockSpec((None, SSM_STATE, si), lambda c: (rev(c), 0, 0)),
                  row(LANES), row(LANES), row(LANES), row(si),
                  pl.BlockSpec((LANES, si), lambda c: (0, 0)),
                  pl.BlockSpec(memory_space=pl.ANY)],
        out_specs=(pl.BlockSpec((CHUNK, si), lambda c: (rev(c), zb)),
                   pl.BlockSpec((CHUNK, cd), lambda c: (rev(c), 0)),
                   pl.BlockSpec((CHUNK, LANES), lambda c: (rev(c), 0)),
                   row(si), row(LANES), row(LANES), row(LANES)),
        scratch_shapes=[pltpu.VMEM((SSM_STATE, si), F32), pltpu.VMEM((SSM_STATE, si), F32),
                        big(), big(), big(), big(), big(), big(), big(), big(),
                        pltpu.VMEM((CHUNK, LANES), F32), pltpu.VMEM((8, si), F32), pltpu.VMEM((8, si), F32)],
        input_output_aliases={11: 0},
        compiler_params=_params(("arbitrary",)), name="ssd_bwd")(
            xact, dt_raw, proj, y, dy2, states, dt_bias, a_log, d_skip, norm_w, e, dproj)


MERGE_TR = 512
MERGE_CW = 2048


def _merge_fwd(cfg, proj, a_br, s_br):
    s, d = cfg.S, cfg.D
    tr, cw = MERGE_TR, min(MERGE_CW, d)
    ga0, gs0 = cfg.OGA // cw, cfg.OGS // cw

    def body(ga_ref, gs_ref, a_ref, s_ref, o_ref):
        o_ref[...] = (_sigmoid(ga_ref[...].astype(F32)) * a_ref[...].astype(F32)
                      + _sigmoid(gs_ref[...].astype(F32)) * s_ref[...].astype(F32)).astype(BF16)

    blk = pl.BlockSpec((tr, cw), lambda i, j: (i, j))
    return pl.pallas_call(
        body, out_shape=SDS((s, d), BF16), grid=(s // tr, d // cw),
        in_specs=[pl.BlockSpec((tr, cw), lambda i, j: (i, ga0 + j)),
                  pl.BlockSpec((tr, cw), lambda i, j: (i, gs0 + j)), blk, blk],
        out_specs=blk, compiler_params=_params(("parallel", "parallel")), name="merge_fwd")(proj, proj, a_br, s_br)


def _merge_bwd(cfg, proj, branch, dmerged, gate_off, dproj, name):
    s, d = cfg.S, cfg.D
    tr, cw = MERGE_TR, min(MERGE_CW, d)
    g0 = gate_off // cw
    fresh = dproj is None

    def body(*refs):
        g_ref, b_ref, dm_ref = refs[:3]
        dg_ref, db_ref = refs[-2:]
        dm = dm_ref[...].astype(F32)
        sg = _sigmoid(g_ref[...].astype(F32))
        db_ref[...] = (dm * sg).astype(BF16)
        dg_ref[...] = (dm * b_ref[...].astype(F32) * sg * (1.0 - sg)).astype(BF16)

    blk = pl.BlockSpec((tr, cw), lambda i, j: (i, j))
    gate = pl.BlockSpec((tr, cw), lambda i, j: (i, g0 + j))
    return pl.pallas_call(
        body, out_shape=(SDS((s, cfg.NM), BF16), SDS((s, d), BF16)), grid=(s // tr, d // cw),
        in_specs=[gate, blk, blk] + ([] if fresh else [HBM_SPEC]),
        out_specs=(gate, blk),
        input_output_aliases={} if fresh else {3: 0},
        compiler_params=_params(("parallel", "parallel")), name=name)(
            *((proj, branch, dmerged) + (() if fresh else (dproj,))))


def _outproj_loss(merged, w_out, x, target, fnw):
    s, d = x.shape
    tr = 256

    def body(m_ref, w_ref, x_ref, t_ref, fw_ref, dof_ref, dob_ref, loss_ref, g_ref):
        out = x_ref[...] + _nn(m_ref[...], w_ref[...])
        r = lax.rsqrt(jnp.mean(out * out, axis=-1, keepdims=True) + RMS_EPS)
        nrm = out * r
        fw = fw_ref[...]
        err = nrm * fw - t_ref[...]
        dy = err * (1.0 / d)
        gy = dy * fw
        dout = r * (gy - nrm * jnp.mean(gy * nrm, axis=-1, keepdims=True))
        dof_ref[...] = dout
        dob_ref[...] = dout.astype(BF16)

        @pl.when(pl.program_id(0) == 0)
        def _():
            loss_ref[...] = jnp.zeros_like(loss_ref)
            g_ref[...] = jnp.zeros_like(g_ref)

        loss_ref[...] += jnp.sum(jnp.sum(err * err, axis=1, keepdims=True), axis=0, keepdims=True) * (0.5 / d)
        g_ref[...] += jnp.sum(dy * nrm, axis=0, keepdims=True)

    blk = pl.BlockSpec((tr, d), lambda i: (i, 0))
    return pl.pallas_call(
        body, out_shape=(SDS((s, d), F32), SDS((s, d), BF16), SDS((1, LANES), F32), SDS((1, d), F32)), grid=(s // tr,),
        in_specs=[blk, pl.BlockSpec((d, d), lambda i: (0, 0)), blk, blk, pl.BlockSpec((1, d), lambda i: (0, 0))],
        out_specs=(blk, blk, pl.BlockSpec((1, LANES), lambda i: (0, 0)), pl.BlockSpec((1, d), lambda i: (0, 0))),
        compiler_params=_params(("arbitrary",)), name="outproj_loss")(merged, w_out, x, target, fnw)


ELEMWISE_BLOCK_BYTES = 1 << 20


def _row_block(rows, cols, itemsize=4):
    best = None
    for tr in range(16, rows + 1, 16):
        if rows % tr == 0 and tr * cols * itemsize <= ELEMWISE_BLOCK_BYTES:
            best = tr
    return best if best is not None else rows


def _adamw(w, g, m, v, name):
    rows, cols = w.shape
    tr = _row_block(rows, cols)

    def body(w_ref, g_ref, m_ref, v_ref, d_ref, nm_ref, nv_ref):
        gv = g_ref[...]
        nm = ADAM_B1 * m_ref[...] + (1.0 - ADAM_B1) * gv
        nv = ADAM_B2 * v_ref[...] + (1.0 - ADAM_B2) * jnp.square(gv)
        m_hat = nm / (1.0 - ADAM_B1 ** ADAM_STEP)
        v_hat = nv / (1.0 - ADAM_B2 ** ADAM_STEP)
        d_ref[...] = -ADAM_LR * (m_hat / (jnp.sqrt(v_hat) + ADAM_EPS) + ADAM_WD * w_ref[...])
        nm_ref[...] = nm
        nv_ref[...] = nv

    blk = pl.BlockSpec((tr, cols), lambda i: (i, 0))
    out = SDS((rows, cols), F32)
    return pl.pallas_call(
        body, out_shape=(out, out, out), grid=(rows // tr,), in_specs=[blk] * 4, out_specs=(blk,) * 3,
        compiler_params=_params(("parallel",)), name=name)(w, g, m, v)


HBM_SPEC = pl.BlockSpec(memory_space=pl.ANY)


def _position():
    return lax.axis_index("x"), lax.axis_index("y"), lax.axis_index("c")


class _Carry:
    def __init__(self, arrays, out_shapes, sems, start, finish):
        self.arrays, self.out_shapes, self.sems, self.start, self.finish = list(arrays), out_shapes, sems, start, finish

    def sem_shapes(self):
        return [pltpu.SemaphoreType.DMA((k,)) for k in self.sems]


def _run_carry(carry, name):
    n = len(carry.arrays)

    def body(*refs):
        carry.start(refs[:n], refs[n:2 * n], refs[2 * n:])
        carry.finish(refs[:n], refs[n:2 * n], refs[2 * n:])

    return pl.pallas_call(
        body, out_shape=carry.out_shapes, in_specs=[HBM_SPEC] * n, out_specs=[HBM_SPEC] * n,
        scratch_shapes=carry.sem_shapes(),
        compiler_params=pltpu.CompilerParams(has_side_effects=True), name=name)(*carry.arrays)


def _gather_carry(shards):
    n = len(shards)

    def copies(ins, outs, sems):
        send_sems, recv_sems, fsend_sems, frecv_sems = sems
        x, y, c = _position()
        me = 2 * x + y
        peers = [(1 - x, y), (x, 1 - y), (1 - x, 1 - y)]

        def over_ici(t, p, chip):
            px, py = peers[p]
            r2 = ins[t].shape[0] // 2
            return pltpu.make_async_remote_copy(
                src_ref=ins[t].at[pl.ds(c * r2, r2), :], dst_ref=outs[t].at[chip, c], send_sem=send_sems.at[3 * t + p],
                recv_sem=recv_sems.at[3 * t + p], device_id=(px, py, c), device_id_type=MESH)

        def to_sibling(t, p, half):
            px, py = peers[p]
            slab = outs[t].at[2 * px + py, half]
            return pltpu.make_async_remote_copy(
                src_ref=slab, dst_ref=slab, send_sem=fsend_sems.at[3 * t + p], recv_sem=frecv_sems.at[3 * t + p],
                device_id=(x, y, 1 - c), device_id_type=MESH)

        pairs = [(t, p) for t in range(n) for p in range(3)]
        sends = [over_ici(t, p, me) for t, p in pairs]
        lands = [over_ici(t, p, 2 * peers[p][0] + peers[p][1]) for t, p in pairs]
        passed = [to_sibling(t, p, c) for t, p in pairs]
        from_sibling = [to_sibling(t, p, 1 - c) for t, p in pairs]
        return sends, lands, passed, from_sibling

    def start(ins, outs, sems):
        for cp in copies(ins, outs, sems)[0]:
            cp.start()

    def finish(ins, outs, sems):
        sends, lands, passed, from_sibling = copies(ins, outs, sems)
        for land, fwd in zip(lands, passed):
            land.wait_recv()
            fwd.start()
        for cp in from_sibling:
            cp.wait_recv()
        for cp in sends + passed:
            cp.wait_send()

    return _Carry(shards, [SDS((N_CHIPS, 2, a.shape[0] // 2, a.shape[1]), a.dtype) for a in shards], [3 * n] * 4,
                  start, finish)


def _scatter_carry(parts):
    def start(ins, outs, sems):
        for cp in _scatter_copies(ins, outs, *sems)[0]:
            cp.start()

    def finish(ins, outs, sems):
        sends, lands = _scatter_copies(ins, outs, *sems)
        for cp in lands:
            cp.wait_recv()
        for cp in sends:
            cp.wait_send()

    return _Carry(parts, [SDS(a.shape, a.dtype) for a in parts], [3 * len(parts)] * 2, start, finish)


def _with_own(gathered, own, chip):
    full = gathered.reshape((N_CHIPS,) + own.shape)
    return lax.dynamic_update_index_in_dim(full, own, chip, 0)


def _exchange_halves(grads):
    n = len(grads)
    slabs = [list(g) if isinstance(g, (list, tuple)) else [g] for g in grads]
    flat = [a for s in slabs for a in s]
    ncp = len(flat)

    def body(*refs):
        ins, outs = refs[:ncp], refs[ncp:ncp + n]
        send_sems, recv_sems = refs[ncp + n:]
        x, y, c = _position()
        cps, k = [], 0
        for t in range(n):
            for j in range(len(slabs[t])):
                if len(slabs[t]) == 1:
                    r2 = ins[k].shape[1] // 2
                    src, dst = ins[k].at[:, pl.ds((1 - c) * r2, r2), :], outs[t]
                else:
                    r2 = ins[k].shape[0] // 2
                    src, dst = ins[k].at[pl.ds((1 - c) * r2, r2), :], outs[t].at[j]
                cps.append(pltpu.make_async_remote_copy(
                    src_ref=src, dst_ref=dst, send_sem=send_sems.at[k], recv_sem=recv_sems.at[k],
                    device_id=(x, y, 1 - c), device_id_type=MESH))
                k += 1
        for cp in cps:
            cp.start()
        for cp in cps:
            cp.wait()

    def landing(s):
        a = s[0]
        return SDS((N_CHIPS, a.shape[-2] // 2, a.shape[-1]), a.dtype)

    return pl.pallas_call(
        body, out_shape=[landing(s) for s in slabs],
        in_specs=[HBM_SPEC] * ncp, out_specs=[HBM_SPEC] * n,
        scratch_shapes=[pltpu.SemaphoreType.DMA((ncp,)), pltpu.SemaphoreType.DMA((ncp,))],
        compiler_params=pltpu.CompilerParams(has_side_effects=True), name="reduce_sibling")(*flat)


def _scatter_copies(ins, outs, send_sems, recv_sems):
    x, y, c = _position()
    me = 2 * x + y
    peers = [(1 - x, y), (x, 1 - y), (1 - x, 1 - y)]

    def remote(t, p, src_slab, dst_slab):
        px, py = peers[p]
        return pltpu.make_async_remote_copy(
            src_ref=ins[t].at[src_slab], dst_ref=outs[t].at[dst_slab], send_sem=send_sems.at[3 * t + p],
            recv_sem=recv_sems.at[3 * t + p], device_id=(px, py, c), device_id_type=MESH)

    n = len(ins)
    sends = [remote(t, p, 2 * peers[p][0] + peers[p][1], me) for t in range(n) for p in range(3)]
    lands = [remote(t, p, me, 2 * peers[p][0] + peers[p][1]) for t in range(n) for p in range(3)]
    return sends, lands


def _share_halves(halves):
    n = len(halves)

    def body(*refs):
        ins, outs = refs[:n], refs[n:2 * n]
        send_sems, recv_sems = refs[2 * n:]
        x, y, c = _position()

        def copy(t, slab):
            return pltpu.make_async_remote_copy(
                src_ref=ins[t].at[slab], dst_ref=outs[t].at[slab], send_sem=send_sems.at[t], recv_sem=recv_sems.at[t],
                device_id=(x, y, 1 - c), device_id_type=MESH)

        for t in range(n):
            copy(t, c).start()
        for t in range(n):
            copy(t, 1 - c).wait_recv()
        for t in range(n):
            copy(t, c).wait_send()

    return pl.pallas_call(
        body, out_shape=[SDS(a.shape, a.dtype) for a in halves],
        in_specs=[HBM_SPEC] * n, out_specs=[HBM_SPEC] * n,
        scratch_shapes=[pltpu.SemaphoreType.DMA((n,)), pltpu.SemaphoreType.DMA((n,))],
        input_output_aliases={t: t for t in range(n)},
        compiler_params=pltpu.CompilerParams(has_side_effects=True), name="share_sibling")(*halves)


def _add_sibling_slab(grad_j, recv, core, j, sums):
    nch, r2, cols = recv.shape
    tr = _row_block(r2, cols)
    nb = r2 // tr
    fresh = sums is None

    def body(c_ref, g_ref, r_ref, *rest):
        del c_ref
        rest[-1][...] = (g_ref[...].astype(F32) + r_ref[...].astype(F32)).astype(BF16)

    return pl.pallas_call(
        body, out_shape=SDS(recv.shape, BF16),
        grid_spec=pltpu.PrefetchScalarGridSpec(
            num_scalar_prefetch=1, grid=(nb,),
            in_specs=[pl.BlockSpec((tr, cols), lambda i, c_ref: (c_ref[0] * nb + i, 0)),
                      pl.BlockSpec((None, tr, cols), lambda i, c_ref: (j, i, 0))] + ([] if fresh else [HBM_SPEC]),
            out_specs=pl.BlockSpec((None, tr, cols), lambda i, c_ref: (j, i, 0))),
        input_output_aliases={} if fresh else {3: 0},
        compiler_params=_params(("parallel",)), name="add_sibling_slab")(
            *((core, grad_j, recv) + (() if fresh else (sums,))))


def _add_sibling(grad, recv, core):
    if isinstance(grad, (list, tuple)):
        sums = None
        for j, g in enumerate(grad):
            sums = _add_sibling_slab(g, recv, core, j, sums)
        return sums
    nch, r2, cols = recv.shape
    tr = _row_block(r2, cols)
    nb = r2 // tr

    def body(c_ref, g_ref, r_ref, o_ref):
        del c_ref
        o_ref[...] = (g_ref[...].astype(F32) + r_ref[...].astype(F32)).astype(BF16)

    return pl.pallas_call(
        body, out_shape=SDS(recv.shape, BF16),
        grid_spec=pltpu.PrefetchScalarGridSpec(
            num_scalar_prefetch=1, grid=(nch, nb),
            in_specs=[pl.BlockSpec((None, tr, cols), lambda j, i, c_ref: (j, c_ref[0] * nb + i, 0)),
                      pl.BlockSpec((None, tr, cols), lambda j, i, c_ref: (j, i, 0))],
            out_specs=pl.BlockSpec((None, tr, cols), lambda j, i, c_ref: (j, i, 0))),
        compiler_params=_params(("parallel", "parallel")), name="add_sibling")(core, grad, recv)


def _add_chips(own, recv, chip_core):
    nch, r2, cols = recv.shape
    tr = _row_block(r2, cols)

    nsc = 2 + nch

    def body(*refs):
        me = refs[0][0]
        own_ref, p_refs, o_ref = refs[nsc], refs[nsc + 1:nsc + 1 + nch], refs[nsc + 1 + nch]
        acc = None
        for j in range(nch):
            term = jnp.where(me == j, own_ref[...], p_refs[j][...]).astype(F32)
            acc = term if acc is None else acc + term
        o_ref[...] = acc

    def slab(j):
        return pl.BlockSpec((None, tr, cols), lambda i, *sc: (sc[2 + j][0], i, 0))

    return pl.pallas_call(
        body, out_shape=SDS((2, r2, cols), F32),
        grid_spec=pltpu.PrefetchScalarGridSpec(
            num_scalar_prefetch=nsc, grid=(r2 // tr,),
            in_specs=[pl.BlockSpec((None, tr, cols), lambda i, *sc: (sc[0][0], i, 0))] + [slab(j) for j in range(nch)],
            out_specs=pl.BlockSpec((None, tr, cols), lambda i, *sc: (sc[1][0], i, 0))),
        compiler_params=_params(("parallel",)), name="add_chips")(*chip_core, own, *([recv] * nch))


def _allreduce_small(pack):
    rows = pack.shape[0]

    def body(p_ref, o_ref, buf, send_sems, recv_sems):
        x, y, c = _position()
        me = 4 * x + 2 * y + c
        buf[me] = p_ref[...]

        def copy(dst_dev, slot):
            return pltpu.make_async_remote_copy(
                src_ref=p_ref, dst_ref=buf.at[slot], send_sem=send_sems.at[dst_dev], recv_sem=recv_sems.at[slot],
                device_id=(dst_dev // 4, (dst_dev // 2) % 2, dst_dev % 2), device_id_type=MESH)

        for dev in range(N_DEV):
            @pl.when(dev != me)
            def _():
                copy(dev, me).start()
        for dev in range(N_DEV):
            @pl.when(dev != me)
            def _():
                copy(dev, dev).wait_recv()
        for dev in range(N_DEV):
            @pl.when(dev != me)
            def _():
                copy(dev, me).wait_send()
        acc = buf[0]
        for dev in range(1, N_DEV):
            acc = acc + buf[dev]
        o_ref[...] = acc

    return pl.pallas_call(
        body, out_shape=SDS(pack.shape, F32),
        in_specs=[pl.BlockSpec(memory_space=pltpu.VMEM)], out_specs=pl.BlockSpec(memory_space=pltpu.VMEM),
        scratch_shapes=[pltpu.VMEM((N_DEV, rows, LANES), F32), pltpu.SemaphoreType.DMA((N_DEV,)),
                        pltpu.SemaphoreType.DMA((N_DEV,))],
        compiler_params=pltpu.CompilerParams(has_side_effects=True), name="allreduce_small")(pack)


ATTN_TQ = 256


def _local_step(cfg, x, target, w, to_chips=None, late=None):
    d = cfg.D
    hn = _rmsnorm_fwd(x, w["norm_w"])
    proj = _mm(hn, w["w_main"], "nn", BF16, "proj_main", carry=late[0] if late else None)
    if late:
        proj, arrived = proj
        w = {**w, **late[1](arrived)}
    dt_raw = _mm(hn, w["w_dt"], "nn", F32, "proj_dt")
    slopes = _slopes(cfg.H)
    near = _Pass(ATTN_TQ, DILATED_PATTERNS[:-1], 1, cfg.S)
    far = _Pass(LANES, DILATED_PATTERNS[-1:], DEINT, cfg.S // DEINT)
    tab_near, tab_far = _attn_tables(near), _attn_tables(far)
    cols_near, cols_far = (cfg.OQ, cfg.OK, cfg.OV), (0, d, 2 * d)
    qkv_far = _deinterleave(proj, 0, 3 * d, "attn_deinterleave")
    o_1, lse_1 = _attn_fwd(cfg, near, proj, cols_near, tab_near, slopes, "attn_fwd_near")
    o_2, lse_2 = _attn_fwd(cfg, far, qkv_far, cols_far, tab_far, slopes, "attn_fwd_far")
    o_a, oag, lse = _attn_merge(cfg, proj, o_1, lse_1, o_2, lse_2)
    xact = _conv_fwd(cfg, proj, w["conv_w"], w["conv_b"])
    e = _expansion_matrix(cfg)
    y, y2, states = _ssd_fwd(cfg, xact, dt_raw, proj, w["dt_bias"], w["a_log"], w["d_skip"], w["ssm_norm_w"], e)
    a_br = _mm(oag, w["w_attn"], "nn", BF16, "branch_attn")
    s_br = _mm(y2, w["w_ssm"], "nn", BF16, "branch_ssm")
    merged = _merge_fwd(cfg, proj, a_br, s_br)
    dout_f, dout_b, loss_row, g_fnw = _outproj_loss(merged, w["w_out"], x, target, w["final_norm_w"])

    dmerged = _mm(dout_b, w["w_out"], "nt", BF16, "d_merged")
    g_w_out = _mm(merged, dout_b, "tn", BF16, "g_w_out")
    dproj, da_br = _merge_bwd(cfg, proj, a_br, dmerged, cfg.OGA, None, "merge_bwd_attn")
    dproj, ds_br = _merge_bwd(cfg, proj, s_br, dmerged, cfg.OGS, dproj, "merge_bwd_ssm")
    doag = _mm(da_br, w["w_attn"], "nt", BF16, "d_oag")
    g_w_attn = _mm(oag, da_br, "tn", BF16, "g_w_attn")
    dy2 = _mm(ds_br, w["w_ssm"], "nt", BF16, "d_y2")
    g_w_ssm = _mm(y2, ds_br, "tn", BF16, "g_w_ssm")
    dproj, dxact, ddt, g_snw, g_dtb, g_alog, g_dsk = _ssd_bwd(
        cfg, xact, dt_raw, proj, y, dy2, states, w["dt_bias"], w["a_log"], w["d_skip"], w["ssm_norm_w"], e, dproj)
    dproj, g_cw, g_cb = _conv_bwd(cfg, proj, dxact, w["conv_w"], w["conv_b"], dproj)
    dproj, do, do_far, dl, dl_far, lse_far = _attn_bwd_prep(cfg, proj, o_a, doag, lse, dproj)
    g_near = _attn_bwd(cfg, near, proj, cols_near, do, lse, dl, tab_near, slopes, "attn_bwd_near")
    g_far = _attn_bwd(cfg, far, qkv_far, cols_far, do_far, lse_far, dl_far, tab_far, slopes, "attn_bwd_far")
    for g_1, g_2, col0, nm in zip(g_near, g_far, cols_near, ("attn_dq", "attn_dk", "attn_dv")):
        dproj = _attn_grad_sum(cfg, g_1, g_2, col0, dproj, nm)
    ddt_b = ddt.astype(BF16)
    g_w_main = _mm(hn, dproj, "tn", BF16, "g_w_main")
    g_w_dt = _mm(hn, ddt_b, "tn", BF16, "g_w_dt")
    grads = dict(w_main=g_w_main, w_dt=g_w_dt, conv_w=g_cw, conv_b=g_cb, dt_bias=g_dtb, a_log=g_alog,
                 d_skip=g_dsk, ssm_norm_w=g_snw, w_attn=g_w_attn, w_ssm=g_w_ssm, w_out=g_w_out, final_norm_w=g_fnw)
    sent = to_chips(grads) if to_chips is not None else ()
    dhn = _mm(dproj, w["w_main"], "nt", F32, "d_hn", tk=1024, carry=_scatter_carry(sent) if sent else None)
    landed = ()
    if sent:
        dhn, landed = dhn
    dhn_dt = _mm(ddt_b, w["w_dt"], "nt", F32, "d_hn_dt")
    grad_x, grads["norm_w"] = _rmsnorm_bwd(x, w["norm_w"], dhn, dhn_dt, dout_f)
    return loss_row, grad_x, grads, sent, landed


def _pad_lanes(v):
    return jnp.pad(v, ((0, 0), (0, LANES - v.shape[1])))


def _cut(lo, hi, a, b):
    a, b = max(lo, a), min(hi, b)
    return (a, b) if a < b else None


def _main_from_shards(cfg, shards):
    per = cfg.N_IN // len(shards)
    main, dt = [], []
    for j, sh in enumerate(shards):
        lo, hi = j * per, (j + 1) * per
        for dst, rng in ((main, (0, cfg.OGA)), (dt, (cfg.OGA, cfg.OGA + cfg.NH)), (main, (cfg.OGA + cfg.NH, cfg.N_IN))):
            c = _cut(lo, hi, *rng)
            if c is not None:
                dst.append(sh[:, c[0] - lo:c[1] - lo])
    return jnp.concatenate(main, axis=1), _pad_lanes(jnp.concatenate(dt, axis=1))


def _shards_from_main(cfg, g_main, g_dt, n):
    per = cfg.N_IN // n
    out = []
    for j in range(n):
        lo, hi = j * per, (j + 1) * per
        parts = []
        for src, off, rng in ((g_main, 0, (0, cfg.OGA)), (g_dt, cfg.OGA, (cfg.OGA, cfg.OGA + cfg.NH)),
                              (g_main, cfg.NH, (cfg.OGA + cfg.NH, cfg.N_IN))):
            c = _cut(lo, hi, *rng)
            if c is not None:
                parts.append(src[:, c[0] - off:c[1] - off])
        out.append(jnp.concatenate(parts, axis=1) if len(parts) > 1 else parts[0])
    return out


def _full_weights(cfg, norm_w, w_in_shards, conv_w, conv_b, dt_bias, a_log, d_skip, ssm_norm_w, w_attn, w_ssm, w_out, fnw):
    w_main, w_dt = _main_from_shards(cfg, w_in_shards)
    return dict(norm_w=norm_w, w_main=w_main.astype(BF16), w_dt=w_dt.astype(BF16), conv_w=conv_w, conv_b=conv_b,
                dt_bias=_pad_lanes(dt_bias), a_log=_pad_lanes(a_log), d_skip=_pad_lanes(d_skip), ssm_norm_w=ssm_norm_w,
                final_norm_w=fnw, **{k: v.astype(BF16) for k, v in (("w_attn", w_attn), ("w_ssm", w_ssm), ("w_out", w_out))
                                     if v is not None})


def _grad_w_in(cfg, grads):
    return _shards_from_main(cfg, grads["w_main"], grads["w_dt"], 1)[0]


def kernel(x, norm_w, w_in, conv_w, conv_b, dt_bias, a_log, d_skip, ssm_norm_w, w_attn_branch, w_ssm_branch, w_out, final_norm_w, loss_target, m_norm_w, m_w_in, m_conv_w, m_conv_b, m_dt_bias, m_a_log, m_d_skip, m_ssm_norm_w, m_w_attn_branch, m_w_ssm_branch, m_w_out, m_final_norm_w, v_norm_w, v_w_in, v_conv_w, v_conv_b, v_dt_bias, v_a_log, v_d_skip, v_ssm_norm_w, v_w_attn_branch, v_w_ssm_branch, v_w_out, v_final_norm_w):
    cfg = _Cfg(x.shape[1], x.shape[2])
    d, si, cd, nh = cfg.D, cfg.SI, cfg.CD, cfg.NH
    chip = 2 * lax.axis_index("x") + lax.axis_index("y")
    core = lax.axis_index("c").astype(jnp.int32).reshape(1)
    chip = chip.astype(jnp.int32)
    chip_core = [chip.reshape(1), core] + [jnp.where(chip == j, (j + 1) % N_CHIPS, j).astype(jnp.int32).reshape(1)
                                           for j in range(N_CHIPS)]

    own = [w_in[0].astype(BF16), conv_w[0].reshape(4 * CONV_K, -1)]
    a_in, a_cw = [_with_own(g, o, chip) for g, o in zip(_run_carry(_gather_carry(own), "gather_weights"), own)]
    conv_w_full = a_cw.reshape(N_CHIPS, CONV_K, cd // N_CHIPS).transpose(1, 0, 2).reshape(CONV_K, cd)
    w = _full_weights(cfg, norm_w, [a_in[j] for j in range(N_CHIPS)], conv_w_full, conv_b, dt_bias, a_log, d_skip,
                      ssm_norm_w, None, None, None, final_norm_w.reshape(1, d))
    own_late = [w_attn_branch[0].astype(BF16), w_ssm_branch[0].astype(BF16), w_out[0].astype(BF16)]

    def late_weights(arrived):
        a_attn, a_ssm, a_out = [_with_own(g, o, chip) for g, o in zip(arrived, own_late)]
        return dict(w_attn=a_attn.reshape(d, d), w_ssm=a_ssm.reshape(si, d), w_out=a_out.reshape(d, d))

    def to_chips(grads):
        by_chip = [_shards_from_main(cfg, grads["w_main"], grads["w_dt"], N_CHIPS),
                   grads["w_attn"].reshape(N_CHIPS, d // N_CHIPS, d),
                   grads["w_ssm"].reshape(N_CHIPS, si // N_CHIPS, d),
                   grads["w_out"].reshape(N_CHIPS, d // N_CHIPS, d)]
        from_sibling = _exchange_halves(by_chip)
        return [_add_sibling(g, r, core) for g, r in zip(by_chip, from_sibling)]

    loss_row, grad_x, grads, chip_sums, from_chips = _local_step(
        cfg, x[0], loss_target[0], w, to_chips, (_gather_carry(own_late), late_weights))
    halves = [_add_chips(o, p, chip_core) for o, p in zip(chip_sums, from_chips)]
    g_in, g_attn, g_ssm, g_out = [h.reshape(2 * h.shape[1], h.shape[2]) for h in _share_halves(halves)]

    small = [loss_row, grads["norm_w"], grads["conv_b"], grads["dt_bias"], grads["a_log"], grads["d_skip"],
             grads["ssm_norm_w"], grads["final_norm_w"], grads["conv_w"].reshape(1, CONV_K * cd)]
    sizes = [a.shape[1] for a in small]
    total = sum(sizes)
    rows = -(-total // (8 * LANES)) * 8
    flat = jnp.pad(jnp.concatenate(small, axis=1), ((0, 0), (0, rows * LANES - total)))
    red = _allreduce_small(flat.reshape(rows, LANES)).reshape(1, rows * LANES)
    offs = [sum(sizes[:i]) for i in range(len(sizes))]
    loss_r, g_nw, g_cb, g_dtb, g_alog, g_dsk, g_snw, g_fnw, g_cw_flat = [
        red[:, o:o + n] for o, n in zip(offs, sizes)]
    loss = loss_r[0, 0]
    g_dtb, g_alog, g_dsk = g_dtb[:, :nh], g_alog[:, :nh], g_dsk[:, :nh]
    cshard = cd // N_CHIPS
    g_cw = lax.dynamic_slice_in_dim(g_cw_flat.reshape(CONV_K, cd), chip * cshard, cshard, axis=1)

    upd = {}
    for name, wv, gv, mv, vv in [("w_in", w_in[0], g_in, m_w_in[0], v_w_in[0]),
                                 ("w_attn", w_attn_branch[0], g_attn, m_w_attn_branch[0], v_w_attn_branch[0]),
                                 ("w_ssm", w_ssm_branch[0], g_ssm, m_w_ssm_branch[0], v_w_ssm_branch[0]),
                                 ("w_out", w_out[0], g_out, m_w_out[0], v_w_out[0])]:
        upd[name] = _adamw(wv, gv, mv, vv, "adamw_" + name)
    names = ["norm_w", "conv_w", "conv_b", "dt_bias", "a_log", "d_skip", "ssm_norm_w", "final_norm_w"]
    ws = [norm_w, conv_w[0].reshape(1, -1), conv_b, dt_bias, a_log, d_skip, ssm_norm_w, final_norm_w.reshape(1, d)]
    gs = [g_nw, g_cw.reshape(1, -1), g_cb, g_dtb, g_alog, g_dsk, g_snw, g_fnw]
    ms = [m_norm_w, m_conv_w[0].reshape(1, -1), m_conv_b, m_dt_bias, m_a_log, m_d_skip, m_ssm_norm_w,
          m_final_norm_w.reshape(1, d)]
    vs = [v_norm_w, v_conv_w[0].reshape(1, -1), v_conv_b, v_dt_bias, v_a_log, v_d_skip, v_ssm_norm_w,
          v_final_norm_w.reshape(1, d)]
    ssz = [a.shape[1] for a in ws]
    stot = sum(ssz)
    srows = -(-stot // (8 * LANES)) * 8

    def pack(parts):
        return jnp.pad(jnp.concatenate(parts, axis=1), ((0, 0), (0, srows * LANES - stot))).reshape(srows, LANES)

    packed = _adamw(pack(ws), pack(gs), pack(ms), pack(vs), "adamw_small")
    soffs = [sum(ssz[:i]) for i in range(len(ssz))]
    for k, nm in enumerate(names):
        upd[nm] = tuple(p.reshape(1, srows * LANES)[:, soffs[k]:soffs[k] + ssz[k]] for p in packed)

    shapes = dict(norm_w=norm_w.shape, w_in=w_in.shape, conv_w=conv_w.shape, conv_b=conv_b.shape, dt_bias=dt_bias.shape,
                  a_log=a_log.shape, d_skip=d_skip.shape, ssm_norm_w=ssm_norm_w.shape, w_attn=w_attn_branch.shape,
                  w_ssm=w_ssm_branch.shape, w_out=w_out.shape, final_norm_w=final_norm_w.shape)
    order = ["norm_w", "w_in", "conv_w", "conv_b", "dt_bias", "a_log", "d_skip", "ssm_norm_w", "w_attn", "w_ssm",
             "w_out", "final_norm_w"]
    gradv = dict(norm_w=g_nw, w_in=g_in, conv_w=g_cw, conv_b=g_cb, dt_bias=g_dtb, a_log=g_alog, d_skip=g_dsk,
                 ssm_norm_w=g_snw, w_attn=g_attn, w_ssm=g_ssm, w_out=g_out, final_norm_w=g_fnw)
    outs = [loss, grad_x[None]]
    outs += [gradv[n].reshape(shapes[n]) for n in order]
    for k in range(3):
        outs += [upd[n][k].reshape(shapes[n]) for n in order]
    return tuple(outs)
```

```python
import jax
import jax.numpy as jnp
from jax import lax
from jax.experimental import pallas as pl
from jax.experimental.pallas import tpu as pltpu

F32 = jnp.float32
BF16 = jnp.bfloat16
SDS = jax.ShapeDtypeStruct

RMS_EPS = 1e-6
LANES = 128
CHUNK = 128
SSM_HEAD_DIM = 64
SSM_GROUPS = 8
SSM_STATE = 128
CONV_K = 4
ATTN_HEAD_DIM = 128
DILATED_PATTERNS = ((128, 1), (512, 4), (2048, 16))
NEG = -1e30
VMEM_LIMIT = 56 * 1024 * 1024
ADAM_LR, ADAM_B1, ADAM_B2, ADAM_EPS, ADAM_WD, ADAM_STEP = 0.001, 0.9, 0.999, 1e-08, 0.01, 10
MESH = pl.DeviceIdType.MESH
N_CHIPS = 4
N_DEV = 8


class _Cfg:
    def __init__(self, s, d):
        self.S, self.D = s, d
        self.H = d // ATTN_HEAD_DIM
        self.SI = 2 * d
        self.NH = self.SI // SSM_HEAD_DIM
        self.HPG = self.NH // SSM_GROUPS
        self.GW = self.HPG * SSM_HEAD_DIM
        self.BC = SSM_GROUPS * SSM_STATE
        self.CD = self.SI + 2 * self.BC
        self.OQ, self.OK, self.OV, self.OZA = 0, d, 2 * d, 3 * d
        self.OZS = 4 * d
        self.OXBC = self.OZS + self.SI
        self.OGA = self.OXBC + self.CD
        self.OGS = self.OGA + d
        self.NM = self.OGS + d
        self.N_IN = self.NM + self.NH
        assert self.GW % LANES == 0 and self.NH <= LANES and s % 512 == 0 and d % 512 == 0


def _params(sem=None):
    return pltpu.CompilerParams(dimension_semantics=sem, vmem_limit_bytes=VMEM_LIMIT)


def _sigmoid(x):
    return 1.0 / (1.0 + jnp.exp(-x))


def _softplus(x):
    u = jnp.exp(-jnp.abs(x))
    l1p = jnp.where(u < 1e-3, u * (1.0 - u * (0.5 - u * (1.0 / 3.0))), jnp.log(1.0 + u))
    return jnp.maximum(x, 0.0) + l1p


def _nt(a, b):
    return lax.dot_general(a, b, (((1,), (1,)), ((), ())), preferred_element_type=F32)


def _tn(a, b):
    return lax.dot_general(a, b, (((0,), (0,)), ((), ())), preferred_element_type=F32)


def _nn(a, b):
    return jnp.dot(a, b, preferred_element_type=F32)


def _tile(n, target):
    if n <= target:
        return n
    best = None
    for t in range(LANES, target + 1, LANES):
        if n % t == 0:
            best = t
    assert best is not None, (n, target)
    return best


MM_TK = {"nn": 2048, "nt": 2048, "tn": 1024}


def _mm(a, b, dims, out_dtype, name, tm=1024, tn=2048, tk=None, init=None, carry=None):
    tk = MM_TK[dims] if tk is None else tk
    if dims == "nn":
        (m, k), (k2, n) = a.shape, b.shape
    elif dims == "nt":
        (m, k), (n, k2) = a.shape, b.shape
    else:
        (k, m), (k2, n) = a.shape, b.shape
    assert k == k2
    tm, tn, tk = _tile(m, tm), _tile(n, tn), _tile(k, tk)
    nk = k // tk
    if dims == "tn":
        a_spec = pl.BlockSpec((tk, tm), lambda i, j, kk: (kk, i))
    else:
        a_spec = pl.BlockSpec((tm, tk), lambda i, j, kk: (i, kk))
    if dims == "nt":
        b_spec = pl.BlockSpec((tn, tk), lambda i, j, kk: (j, kk))
    else:
        b_spec = pl.BlockSpec((tk, tn), lambda i, j, kk: (kk, j))
    o_spec = pl.BlockSpec((tm, tn), lambda i, j, kk: (i, j))
    op = {"nn": _nn, "nt": _nt, "tn": _tn}[dims]
    has_init = init is not None
    nx = len(carry.arrays) if carry is not None else 0
    ni, nj = m // tm, n // tn

    def body(*refs):
        a_ref, b_ref = refs[0], refs[1]
        i_ref = refs[2] if has_init else None
        x_in = refs[2 + has_init:2 + has_init + nx]
        o_ref = refs[2 + has_init + nx]
        x_out = refs[3 + has_init + nx:3 + has_init + 2 * nx]
        acc = refs[3 + has_init + 2 * nx]
        x_sems = refs[4 + has_init + 2 * nx:]
        i, j, kk = pl.program_id(0), pl.program_id(1), pl.program_id(2)

        if nx:
            @pl.when((i == 0) & (j == 0) & (kk == 0))
            def _():
                carry.start(x_in, x_out, x_sems)

        prod = lambda: op(a_ref[...], b_ref[...])
        with_init = (lambda p: p + i_ref[...].astype(F32)) if has_init else (lambda p: p)
        if nk == 1:
            o_ref[...] = with_init(prod()).astype(out_dtype)
        else:
            @pl.when(kk == 0)
            def _():
                acc[...] = with_init(prod())

            @pl.when((kk > 0) & (kk < nk - 1))
            def _():
                acc[...] += prod()

            @pl.when(kk == nk - 1)
            def _():
                o_ref[...] = (acc[...] + prod()).astype(out_dtype)

        if nx:
            @pl.when((i == ni - 1) & (j == nj - 1) & (kk == nk - 1))
            def _():
                carry.finish(x_in, x_out, x_sems)

    in_specs = [a_spec, b_spec] + ([o_spec] if has_init else []) + [HBM_SPEC] * nx
    args = (a, b) + ((init,) if has_init else ()) + (tuple(carry.arrays) if nx else ())
    sems = carry.sem_shapes() if nx else []
    outs = pl.pallas_call(
        body, out_shape=[SDS((m, n), out_dtype)] + (carry.out_shapes if nx else []), grid=(ni, nj, nk),
        in_specs=in_specs, out_specs=[o_spec] + [HBM_SPEC] * nx,
        scratch_shapes=[pltpu.VMEM((tm, tn) if nk > 1 else (8, LANES), F32)] + sems,
        compiler_params=_params(("arbitrary",) * 3 if nx else ("parallel", "parallel", "arbitrary")), name=name)(*args)
    return (outs[0], outs[1:]) if nx else outs[0]


def _rmsnorm_fwd(x, w):
    s, d = x.shape
    tr = 256

    def body(x_ref, w_ref, o_ref):
        xv = x_ref[...]
        r = lax.rsqrt(jnp.mean(xv * xv, axis=-1, keepdims=True) + RMS_EPS)
        o_ref[...] = (xv * r * w_ref[...]).astype(BF16)

    return pl.pallas_call(
        body, out_shape=SDS((s, d), BF16), grid=(s // tr,),
        in_specs=[pl.BlockSpec((tr, d), lambda i: (i, 0)), pl.BlockSpec((1, d), lambda i: (0, 0))],
        out_specs=pl.BlockSpec((tr, d), lambda i: (i, 0)),
        compiler_params=_params(("parallel",)), name="rmsnorm_fwd")(x, w)


def _rmsnorm_bwd(x, w, dhn_a, dhn_b, dout):
    s, d = x.shape
    tr = 256

    def body(x_ref, w_ref, dh_ref, dh2_ref, do_ref, gx_ref, gw_ref):
        xv = x_ref[...]
        r = lax.rsqrt(jnp.mean(xv * xv, axis=-1, keepdims=True) + RMS_EPS)
        nrm = xv * r
        dh = dh_ref[...] + dh2_ref[...]
        gy = dh * w_ref[...]
        gx_ref[...] = do_ref[...] + r * (gy - nrm * jnp.mean(gy * nrm, axis=-1, keepdims=True))

        @pl.when(pl.program_id(0) == 0)
        def _():
            gw_ref[...] = jnp.zeros_like(gw_ref)

        gw_ref[...] += jnp.sum(dh * nrm, axis=0, keepdims=True)

    blk = pl.BlockSpec((tr, d), lambda i: (i, 0))
    row = pl.BlockSpec((1, d), lambda i: (0, 0))
    return pl.pallas_call(
        body, out_shape=(SDS((s, d), F32), SDS((1, d), F32)), grid=(s // tr,),
        in_specs=[blk, row, blk, blk, blk], out_specs=(blk, row),
        compiler_params=_params(("arbitrary",)), name="rmsnorm_bwd")(x, w, dhn_a, dhn_b, dout)


DEINT = DILATED_PATTERNS[-1][1]
DEINT_ROWS = DEINT * LANES


class _Pass:
    def __init__(self, tq, patterns, unit, seg_len):
        self.tq, self.patterns, self.unit, self.seg_len = tq, patterns, unit, seg_len
        self.win = max(w for w, _ in patterns) // unit
        self.w = self.win + tq
        assert self.win % tq == 0


def _attn_tables(ps):
    i = jnp.arange(ps.tq, dtype=jnp.int32)[:, None]
    j = jnp.arange(ps.w, dtype=jnp.int32)[None, :]
    delta = (i + ps.win - j) * ps.unit
    n = jnp.zeros((ps.tq, ps.w), F32)
    for window, dil in ps.patterns:
        n = n + ((delta >= 0) & (delta <= window) & (delta % dil == 0)).astype(F32)
    logn = jnp.where(n > 0, jnp.log(jnp.maximum(n, 1.0)), NEG)
    return logn, jnp.maximum(delta, 0).astype(F32)


def _slopes(h):
    s = jnp.asarray([2.0 ** (-8.0 * (i + 1) / h) for i in range(h)], F32)
    return jnp.broadcast_to(s[:, None, None], (h, 1, LANES))


def _masked_logn(ps, logn_ref, start):
    col = lax.broadcasted_iota(jnp.int32, (ps.tq, ps.w), 1)
    return jnp.where(col >= ps.win - lax.rem(start, ps.seg_len), logn_ref[...], NEG)


def _head_cols(hh):
    return slice(hh * ATTN_HEAD_DIM, (hh + 1) * ATTN_HEAD_DIM)


def _head_window(refs, cs):
    return jnp.concatenate([r[:, cs] for r in refs], axis=0)


def _head_scores(q_ref, kw, cs, base, dist_ref, slope_ref, hh):
    return _nt(q_ref[:, cs], kw) * (ATTN_HEAD_DIM ** -0.5) + (base - slope_ref[hh][0:1, 0:1] * dist_ref[...])


def _lane_of(stat, hh):
    lane = lax.broadcasted_iota(jnp.int32, stat.shape, 1)
    return jnp.sum(jnp.where(lane == hh, stat, 0.0), axis=1, keepdims=True)


def _window_specs(ps, d, col, nb):
    nprev = ps.win // ps.tq
    return [pl.BlockSpec((ps.tq, d), lambda i, b=b: (jnp.maximum(jnp.minimum(i, nb - 1) - (nprev - b), 0), col))
            for b in range(nprev + 1)]


def _attn_fwd(cfg, ps, qkv, cols, tables, slopes, name):
    s, h, d = cfg.S, cfg.H, cfg.D
    tq, nw = ps.tq, ps.win // ps.tq + 1
    nb = s // tq
    logn, dist = tables
    qc, kc, vc = [c // d for c in cols]

    def body(*refs):
        q_ref, k_refs, v_refs = refs[0], refs[1:1 + nw], refs[1 + nw:1 + 2 * nw]
        logn_ref, dist_ref, slope_ref, o_ref, lse_ref = refs[1 + 2 * nw:]
        base = _masked_logn(ps, logn_ref, pl.program_id(0) * tq)
        lane = lax.broadcasted_iota(jnp.int32, (tq, LANES), 1)

        lse = jnp.zeros((tq, LANES), F32)
        for hh in range(h):
            cs = _head_cols(hh)
            sc = _head_scores(q_ref, _head_window(k_refs, cs), cs, base, dist_ref, slope_ref, hh)
            m = jnp.max(sc, axis=1, keepdims=True)
            p = jnp.exp(sc - m)
            l = jnp.sum(p, axis=1, keepdims=True)
            o_ref[:, cs] = (_nn(p.astype(BF16), _head_window(v_refs, cs)) / l).astype(BF16)
            lse = jnp.where(lane == hh, m + jnp.log(l), lse)
        lse_ref[...] = lse

    tab = pl.BlockSpec((tq, ps.w), lambda i: (0, 0))
    return pl.pallas_call(
        body, out_shape=(SDS((s, d), BF16), SDS((s, LANES), F32)), grid=(nb,),
        in_specs=[pl.BlockSpec((tq, d), lambda i: (i, qc))] + _window_specs(ps, d, kc, nb) + _window_specs(ps, d, vc, nb)
        + [tab, tab, pl.BlockSpec((h, 1, LANES), lambda i: (0, 0, 0))],
        out_specs=(pl.BlockSpec((tq, d), lambda i: (i, 0)), pl.BlockSpec((tq, LANES), lambda i: (i, 0))),
        compiler_params=_params(("parallel",)), name=name)(*([qkv] * (1 + 2 * nw)), logn, dist, slopes)


def _attn_bwd(cfg, ps, qkv, cols, do, lse, delta, tables, slopes, name):
    s, h, d = cfg.S, cfg.H, cfg.D
    tq, nprev = ps.tq, ps.win // ps.tq
    nw = nprev + 1
    nb = s // tq
    logn, dist = tables
    qc, kc, vc = [c // d for c in cols]
    scale = ATTN_HEAD_DIM ** -0.5

    def body(*refs):
        q_ref, k_refs, v_refs = refs[0], refs[1:1 + nw], refs[1 + nw:1 + 2 * nw]
        do_ref, lse_ref, dl_ref, logn_ref, dist_ref, slope_ref, dq_ref, dk_ref, dv_ref, ck, cv = refs[1 + 2 * nw:]
        i = pl.program_id(0)
        slot = lambda b: lax.rem(i + b, nprev)

        @pl.when(i == 0)
        def _():
            ck[...] = jnp.zeros_like(ck)
            cv[...] = jnp.zeros_like(cv)

        @pl.when(i < nb)
        def _():
            base = _masked_logn(ps, logn_ref, i * tq)
            lse_all, dl_all = lse_ref[...], dl_ref[...]

            for hh in range(h):
                cs = _head_cols(hh)
                kw, vw = _head_window(k_refs, cs), _head_window(v_refs, cs)
                sc = _head_scores(q_ref, kw, cs, base, dist_ref, slope_ref, hh)
                p = jnp.exp(sc - lse_all[:, hh:hh + 1])
                dob = do_ref[:, cs]
                ds = (p * (_nt(dob, vw) - dl_all[:, hh:hh + 1]) * scale).astype(BF16)
                dq_ref[:, cs] = _nn(ds, kw).astype(BF16)
                dkw = _tn(ds, q_ref[:, cs])
                dvw = _tn(p.astype(BF16), dob)
                dk_ref[:, cs] = ck[slot(0), :, cs] + dkw[0:tq]
                dv_ref[:, cs] = cv[slot(0), :, cs] + dvw[0:tq]
                for b in range(1, nprev):
                    ck[slot(b), :, cs] += dkw[b * tq:(b + 1) * tq]
                    cv[slot(b), :, cs] += dvw[b * tq:(b + 1) * tq]
                ck[slot(0), :, cs] = dkw[nprev * tq:]
                cv[slot(0), :, cs] = dvw[nprev * tq:]

        @pl.when(i >= nb)
        def _():
            dk_ref[...] = ck[slot(0)]
            dv_ref[...] = cv[slot(0)]

    here = lambda i: jnp.minimum(i, nb - 1)
    blk = pl.BlockSpec((tq, d), lambda i: (here(i), 0))
    stat = pl.BlockSpec((tq, LANES), lambda i: (here(i), 0))
    late = pl.BlockSpec((tq, d), lambda i: (jnp.maximum(i - nprev, 0), 0))
    tab = pl.BlockSpec((tq, ps.w), lambda i: (0, 0))
    return pl.pallas_call(
        body, out_shape=(SDS((s, d), BF16), SDS((s, d), F32), SDS((s, d), F32)), grid=(nb + nprev,),
        in_specs=[pl.BlockSpec((tq, d), lambda i: (here(i), qc))] + _window_specs(ps, d, kc, nb)
        + _window_specs(ps, d, vc, nb) + [blk, stat, stat, tab, tab, pl.BlockSpec((h, 1, LANES), lambda i: (0, 0, 0))],
        out_specs=(blk, late, late),
        scratch_shapes=[pltpu.VMEM((nprev, tq, d), F32), pltpu.VMEM((nprev, tq, d), F32)],
        compiler_params=_params(("arbitrary",)), name=name)(
            *([qkv] * (1 + 2 * nw)), do, lse, delta, logn, dist, slopes)


def _by_residue(a):
    return a.reshape(DEINT, a.shape[0] // DEINT, a.shape[1])


def _deint_spec(colblock):
    return pl.BlockSpec((DEINT, LANES, LANES), lambda b, j: (0, b, colblock(j)))


def _deint_rows(scr, out_ref, dtype):
    for r in range(DEINT):
        out_ref[r] = scr[pl.ds(r, LANES, stride=DEINT), :].astype(dtype)


def _int_rows(in_ref, scr):
    for r in range(DEINT):
        scr[pl.ds(r, LANES, stride=DEINT), :] = in_ref[r].astype(F32)


WIDE = 4 * LANES


def _wide_spec():
    return pl.BlockSpec((DEINT, LANES, WIDE), lambda b, j: (0, b, j))


def _deinterleave(x, col0, ncols, name):
    s = x.shape[0]
    c0 = col0 // WIDE

    def body(x_ref, o_ref, scr):
        for t in range(WIDE // LANES):
            cs = slice(t * LANES, (t + 1) * LANES)
            scr[t] = x_ref[:, cs].astype(F32)
            for r in range(DEINT):
                o_ref[r, :, cs] = scr.at[t][pl.ds(r, LANES, stride=DEINT), :].astype(x.dtype)

    out = pl.pallas_call(
        body, out_shape=SDS((DEINT, s // DEINT, ncols), x.dtype), grid=(s // DEINT_ROWS, ncols // WIDE),
        in_specs=[pl.BlockSpec((DEINT_ROWS, WIDE), lambda b, j: (b, c0 + j))],
        out_specs=_wide_spec(),
        scratch_shapes=[pltpu.VMEM((WIDE // LANES, DEINT_ROWS, LANES), F32)],
        compiler_params=_params(("parallel", "parallel")), name=name)(x)
    return out.reshape(s, ncols)


def _attn_merge(cfg, proj, o_1, lse_1, o_2, lse_2):
    s, h = cfg.S, cfg.H
    zb = cfg.OZA // WIDE
    rows = DEINT_ROWS
    hps = WIDE // LANES

    def body(o1_ref, l1_ref, o2_ref, l2_ref, z_ref, o_ref, og_ref, lse_ref, so, sl):
        j = pl.program_id(1)

        @pl.when(j == 0)
        def _():
            _int_rows(l2_ref, sl)
            lse_ref[...] = jnp.zeros_like(lse_ref)

        l1_all, l2_all = l1_ref[...], sl[...]
        lane = lax.broadcasted_iota(jnp.int32, (rows, LANES), 1)
        lse = lse_ref[...]
        for t in range(hps):
            hh = j * hps + t
            cs = slice(t * LANES, (t + 1) * LANES)
            for r in range(DEINT):
                so.at[t][pl.ds(r, LANES, stride=DEINT), :] = o2_ref[r, :, cs].astype(F32)
            l1, l2 = _lane_of(l1_all, hh), _lane_of(l2_all, hh)
            mx = jnp.maximum(l1, l2)
            w1, w2 = jnp.exp(l1 - mx), jnp.exp(l2 - mx)
            den = w1 + w2
            o = (w1 * o1_ref[:, cs].astype(F32) + w2 * so[t]) / den
            z = z_ref[:, cs].astype(F32)
            o_ref[:, cs] = o.astype(BF16)
            og_ref[:, cs] = (o * (z * _sigmoid(z))).astype(BF16)
            lse = jnp.where(lane == hh, mx + jnp.log(den), lse)
        lse_ref[...] = lse

    blk = pl.BlockSpec((rows, WIDE), lambda b, j: (b, j))
    stat = pl.BlockSpec((rows, LANES), lambda b, j: (b, 0))
    return pl.pallas_call(
        body, out_shape=(SDS((s, cfg.D), BF16), SDS((s, cfg.D), BF16), SDS((s, LANES), F32)),
        grid=(s // rows, h // hps),
        in_specs=[blk, stat, _wide_spec(), _deint_spec(lambda j: 0), pl.BlockSpec((rows, WIDE), lambda b, j: (b, zb + j))],
        out_specs=(blk, blk, stat),
        scratch_shapes=[pltpu.VMEM((hps, rows, LANES), F32), pltpu.VMEM((rows, LANES), F32)],
        compiler_params=_params(("parallel", "arbitrary")), name="attn_merge")(
            o_1, lse_1, _by_residue(o_2), _by_residue(lse_2), proj)


def _attn_bwd_prep(cfg, proj, o_a, doag, lse, dproj):
    s, h = cfg.S, cfg.H
    zb = cfg.OZA // WIDE
    rows = DEINT_ROWS
    hps = WIDE // LANES

    def body(o_ref, dg_ref, z_ref, lse_ref, dp_in, dz_ref, do_ref, do2_ref, dl_ref, dl2_ref, lse2_ref, scr):
        del dp_in
        j = pl.program_id(1)

        @pl.when(j == 0)
        def _():
            dl_ref[...] = jnp.zeros_like(dl_ref)

        lane = lax.broadcasted_iota(jnp.int32, (rows, LANES), 1)
        dl = dl_ref[...]
        for t in range(hps):
            cs = slice(t * LANES, (t + 1) * LANES)
            z = z_ref[:, cs].astype(F32)
            sg = _sigmoid(z)
            o = o_ref[:, cs].astype(F32)
            dg = dg_ref[:, cs].astype(F32)
            do = dg * (z * sg)
            dz_ref[:, cs] = (dg * o * (sg * (1.0 + z * (1.0 - sg)))).astype(BF16)
            do_ref[:, cs] = do.astype(BF16)
            scr[...] = do
            for r in range(DEINT):
                do2_ref[r, :, cs] = scr[pl.ds(r, LANES, stride=DEINT), :].astype(BF16)
            dl = jnp.where(lane == j * hps + t, jnp.sum(do * o, axis=1, keepdims=True), dl)
        dl_ref[...] = dl

        @pl.when(j == h // hps - 1)
        def _():
            scr[...] = dl
            _deint_rows(scr, dl2_ref, F32)
            scr[...] = lse_ref[...]
            _deint_rows(scr, lse2_ref, F32)

    blk = pl.BlockSpec((rows, WIDE), lambda b, j: (b, j))
    stat = pl.BlockSpec((rows, LANES), lambda b, j: (b, 0))
    stat2 = _deint_spec(lambda j: 0)
    outs = pl.pallas_call(
        body,
        out_shape=(SDS(dproj.shape, BF16), SDS((s, cfg.D), BF16), SDS((DEINT, s // DEINT, cfg.D), BF16),
                   SDS((s, LANES), F32), SDS((DEINT, s // DEINT, LANES), F32), SDS((DEINT, s // DEINT, LANES), F32)),
        grid=(s // rows, h // hps),
        in_specs=[blk, blk, pl.BlockSpec((rows, WIDE), lambda b, j: (b, zb + j)), stat, HBM_SPEC],
        out_specs=(pl.BlockSpec((rows, WIDE), lambda b, j: (b, zb + j)), blk, _wide_spec(), stat, stat2, stat2),
        scratch_shapes=[pltpu.VMEM((rows, LANES), F32)],
        input_output_aliases={4: 0},
        compiler_params=_params(("parallel", "arbitrary")), name="attn_bwd_prep")(o_a, doag, proj, lse, dproj)
    dproj, do, do2, dl, dl2, lse2 = outs
    return dproj, do, do2.reshape(s, cfg.D), dl, dl2.reshape(s, LANES), lse2.reshape(s, LANES)


def _attn_grad_sum(cfg, g_1, g_2, col0, dproj, name):
    s = cfg.S
    c0 = col0 // WIDE
    rows = DEINT_ROWS

    def body(g1_ref, g2_ref, dp_in, o_ref, scr):
        del dp_in
        for t in range(WIDE // LANES):
            cs = slice(t * LANES, (t + 1) * LANES)
            for r in range(DEINT):
                scr.at[t][pl.ds(r, LANES, stride=DEINT), :] = g2_ref[r, :, cs].astype(F32)
            o_ref[:, cs] = (g1_ref[:, cs].astype(F32) + scr[t]).astype(BF16)

    return pl.pallas_call(
        body, out_shape=SDS(dproj.shape, BF16), grid=(s // rows, cfg.D // WIDE),
        in_specs=[pl.BlockSpec((rows, WIDE), lambda b, j: (b, j)), _wide_spec(), HBM_SPEC],
        out_specs=pl.BlockSpec((rows, WIDE), lambda b, j: (b, c0 + j)),
        scratch_shapes=[pltpu.VMEM((WIDE // LANES, rows, LANES), F32)],
        input_output_aliases={2: 0},
        compiler_params=_params(("parallel", "parallel")), name=name)(g_1, _by_residue(g_2), dproj)


CONV_HALO = 16
CONV_TR = 512
CONV_CW = 1024


def _rows_back(a, n):
    return a if n == 0 else pltpu.roll(a, n % a.shape[0], axis=0)


def _conv_fwd(cfg, proj, conv_w, conv_b):
    s, cd = cfg.S, cfg.CD
    tr, cw, hl = CONV_TR, CONV_CW, CONV_HALO
    cb0 = cfg.OXBC // cw

    def body(x_ref, h_ref, w_ref, b_ref, o_ref):
        i = pl.program_id(0)
        halo = jnp.where(i > 0, h_ref[...].astype(F32), 0.0)
        ext = jnp.concatenate([halo, x_ref[...].astype(F32)], axis=0)
        pre = b_ref[...] + jnp.zeros((tr, cw), F32)
        for k in range(CONV_K):
            pre = pre + w_ref[k:k + 1, :] * _rows_back(ext, CONV_K - 1 - k)[hl:]
        o_ref[...] = (pre * _sigmoid(pre)).astype(BF16)

    return pl.pallas_call(
        body, out_shape=SDS((s, cd), BF16), grid=(s // tr, cd // cw),
        in_specs=[pl.BlockSpec((tr, cw), lambda i, j: (i, cb0 + j)),
                  pl.BlockSpec((hl, cw), lambda i, j: (jnp.maximum(i * (tr // hl) - 1, 0), cb0 + j)),
                  pl.BlockSpec((CONV_K, cw), lambda i, j: (0, j)),
                  pl.BlockSpec((1, cw), lambda i, j: (0, j))],
        out_specs=pl.BlockSpec((tr, cw), lambda i, j: (i, j)),
        compiler_params=_params(("parallel", "parallel")), name="conv_fwd")(proj, proj, conv_w, conv_b)


def _conv_bwd(cfg, proj, dact, conv_w, conv_b, dproj):
    s, cd = cfg.S, cfg.CD
    tr, cw, hl = CONV_TR, CONV_CW, CONV_HALO
    cb0 = cfg.OXBC // cw
    nr = s // tr
    last_h = s // hl - 1

    def body(x_ref, hp_ref, hn_ref, d_ref, dn_ref, w_ref, b_ref, dp_in, dx_ref, gw_ref, gb_ref):
        del dp_in
        i = pl.program_id(1)
        ext = jnp.concatenate([jnp.where(i > 0, hp_ref[...].astype(F32), 0.0), x_ref[...].astype(F32),
                               hn_ref[...].astype(F32)], axis=0)
        shifted = [_rows_back(ext, CONV_K - 1 - k)[hl:] for k in range(CONV_K)]
        pre = b_ref[...] + jnp.zeros((tr + hl, cw), F32)
        for k in range(CONV_K):
            pre = pre + w_ref[k:k + 1, :] * shifted[k]
        sg = _sigmoid(pre)
        dact = jnp.concatenate([d_ref[...].astype(F32), jnp.where(i < nr - 1, dn_ref[...].astype(F32), 0.0)], axis=0)
        dpre = dact * (sg * (1.0 + pre * (1.0 - sg)))
        dx = jnp.zeros((tr, cw), F32)
        for k in range(CONV_K):
            dx = dx + w_ref[k:k + 1, :] * _rows_back(dpre, -(CONV_K - 1 - k))[0:tr]
        dx_ref[...] = dx.astype(BF16)

        @pl.when(i == 0)
        def _():
            gw_ref[...] = jnp.zeros_like(gw_ref)
            gb_ref[...] = jnp.zeros_like(gb_ref)

        dcur = dpre[0:tr]
        gb_ref[...] += jnp.sum(dcur, axis=0, keepdims=True)
        for k in range(CONV_K):
            gw_ref[k:k + 1, :] += jnp.sum(dcur * shifted[k][0:tr], axis=0, keepdims=True)

    return pl.pallas_call(
        body, out_shape=(SDS(dproj.shape, BF16), SDS((CONV_K, cd), F32), SDS((1, cd), F32)), grid=(cd // cw, nr),
        in_specs=[pl.BlockSpec((tr, cw), lambda j, i: (i, cb0 + j)),
                  pl.BlockSpec((hl, cw), lambda j, i: (jnp.maximum(i * (tr // hl) - 1, 0), cb0 + j)),
                  pl.BlockSpec((hl, cw), lambda j, i: (jnp.minimum((i + 1) * (tr // hl), last_h), cb0 + j)),
                  pl.BlockSpec((tr, cw), lambda j, i: (i, j)),
                  pl.BlockSpec((hl, cw), lambda j, i: (jnp.minimum((i + 1) * (tr // hl), last_h), j)),
                  pl.BlockSpec((CONV_K, cw), lambda j, i: (0, j)),
                  pl.BlockSpec((1, cw), lambda j, i: (0, j)),
                  pl.BlockSpec(memory_space=pl.ANY)],
        out_specs=(pl.BlockSpec((tr, cw), lambda j, i: (i, cb0 + j)),
                   pl.BlockSpec((CONV_K, cw), lambda j, i: (0, j)),
                   pl.BlockSpec((1, cw), lambda j, i: (0, j))),
        input_output_aliases={7: 0},
        compiler_params=_params(("parallel", "arbitrary")), name="conv_bwd")(
            proj, proj, proj, dact, dact, conv_w, conv_b, dproj)


def _expand(v, e, terms):
    out, rem = None, v
    for _ in range(terms):
        hi = rem.astype(BF16)
        t = _nn(hi, e)
        out = t if out is None else out + t
        rem = rem - hi.astype(F32)
    return out


def _segsum(v, e, terms):
    out, rem = None, v
    for _ in range(terms):
        hi = rem.astype(BF16)
        t = _nt(hi, e)
        out = t if out is None else out + t
        rem = rem - hi.astype(F32)
    return out


def _expand_row(row, e, terms):
    return _expand(jnp.broadcast_to(row, (8, LANES)), e, terms)[0:1]


def _segsum_row(row, e, terms):
    return _segsum(jnp.broadcast_to(row, (8, row.shape[1])), e, terms)[0:1]


def _expansion_matrix(cfg):
    hh = jnp.arange(LANES, dtype=jnp.int32)[:, None]
    cc = jnp.arange(cfg.SI, dtype=jnp.int32)[None, :]
    return (cc // SSM_HEAD_DIM == hh).astype(BF16)


def _tri(lower):
    r = lax.broadcasted_iota(jnp.int32, (CHUNK, CHUNK), 0)
    c = lax.broadcasted_iota(jnp.int32, (CHUNK, CHUNK), 1)
    return (c <= r) if lower else (c >= r)


def _ssd_prep(dtr_ref, db_ref, al_ref, e):
    dtr = dtr_ref[...] + db_ref[...]
    dt = _softplus(dtr)
    a = -jnp.exp(al_ref[...])
    acum = jnp.dot(_tri(True).astype(F32), dt * a, precision=lax.Precision.HIGHEST, preferred_element_type=F32)
    return dtr, dt, a, _expand(dt, e, 2), _expand(acum, e, 3)


def _ssd_fwd(cfg, xact, dt_raw, proj, dt_bias, a_log, d_skip, norm_w, e):
    s, si, cd, gw, bc = cfg.S, cfg.SI, cfg.CD, cfg.GW, cfg.BC
    nc = s // CHUNK
    zb = cfg.OZS // si
    tiles = gw // LANES

    def body(xa_ref, dtr_ref, z_ref, db_ref, al_ref, dsk_ref, nw_ref, e_ref, y_ref, y2_ref, st_ref,
             state, ybuf, x_s, xw_s, ae_s, ea_s, lam_s):
        @pl.when(pl.program_id(0) == 0)
        def _():
            state[...] = jnp.zeros_like(state)

        st_ref[...] = state[...]
        ev = e_ref[...]
        _, _, _, dt_e, a_e = _ssd_prep(dtr_ref, db_ref, al_ref, ev)
        xs = xa_ref[:, 0:si].astype(F32)
        x = xs * dt_e
        lam_e = a_e[CHUNK - 1:CHUNK, :]
        x_s[...] = x.astype(BF16)
        xw_s[...] = (x * jnp.exp(lam_e - a_e)).astype(BF16)
        ae_s[...] = a_e
        ea_s[...] = jnp.exp(a_e)
        ybuf[...] = _expand_row(dsk_ref[...], ev, 3) * xs
        lam_s[...] = jnp.broadcast_to(jnp.exp(lam_e), (8, si))
        tril = _tri(True)
        lane = lax.broadcasted_iota(jnp.int32, (CHUNK, LANES), 1)

        def group(g, carry):
            co = pl.multiple_of(g * gw, LANES)
            bg = xa_ref[:, pl.ds(pl.multiple_of(si + g * SSM_STATE, LANES), SSM_STATE)]
            cg = xa_ref[:, pl.ds(pl.multiple_of(si + bc + g * SSM_STATE, LANES), SSM_STATE)]
            cbm = _nt(cg, bg)
            st = state[:, pl.ds(co, gw)]
            yoff = _nn(cg, st.astype(BF16)) * ea_s[:, pl.ds(co, gw)]
            for k in range(tiles):
                tc = pl.multiple_of(co + k * LANES, LANES)
                at = ae_s[:, pl.ds(tc, LANES)]
                att = at.T
                xt = x_s[:, pl.ds(tc, LANES)]
                acc = yoff[:, k * LANES:(k + 1) * LANES]
                for half in range(2):
                    lo = half * SSM_HEAD_DIM
                    seg = at[:, lo:lo + 1] - att[lo:lo + 1, :]
                    dec = jnp.exp(jnp.where(tril, seg, NEG))
                    xh = jnp.where((lane >= lo) & (lane < lo + SSM_HEAD_DIM), xt, jnp.zeros_like(xt))
                    acc = acc + _nn((cbm * dec).astype(BF16), xh)
                ybuf[:, pl.ds(tc, LANES)] += acc
            state[:, pl.ds(co, gw)] = st * lam_s[0:1, pl.ds(co, gw)] + _tn(bg, xw_s[:, pl.ds(co, gw)])
            return carry

        lax.fori_loop(0, SSM_GROUPS, group, 0)
        y = ybuf[...]
        y_ref[...] = y.astype(BF16)
        z = z_ref[...].astype(F32)
        u = y * (z * _sigmoid(z))
        r = lax.rsqrt(jnp.mean(u * u, axis=-1, keepdims=True) + RMS_EPS)
        y2_ref[...] = (u * r * nw_ref[...]).astype(BF16)

    row = lambda n: pl.BlockSpec((1, n), lambda c: (0, 0))
    return pl.pallas_call(
        body,
        out_shape=(SDS((s, si), BF16), SDS((s, si), BF16), SDS((nc, SSM_STATE, si), F32)),
        grid=(nc,),
        in_specs=[pl.BlockSpec((CHUNK, cd), lambda c: (c, 0)),
                  pl.BlockSpec((CHUNK, LANES), lambda c: (c, 0)),
                  pl.BlockSpec((CHUNK, si), lambda c: (c, zb)),
                  row(LANES), row(LANES), row(LANES), row(si),
                  pl.BlockSpec((LANES, si), lambda c: (0, 0))],
        out_specs=(pl.BlockSpec((CHUNK, si), lambda c: (c, 0)),
                   pl.BlockSpec((CHUNK, si), lambda c: (c, 0)),
                   pl.BlockSpec((None, SSM_STATE, si), lambda c: (c, 0, 0))),
        scratch_shapes=[pltpu.VMEM((SSM_STATE, si), F32), pltpu.VMEM((CHUNK, si), F32),
                        pltpu.VMEM((CHUNK, si), BF16), pltpu.VMEM((CHUNK, si), BF16),
                        pltpu.VMEM((CHUNK, si), F32), pltpu.VMEM((CHUNK, si), F32),
                        pltpu.VMEM((8, si), F32)],
        compiler_params=_params(("arbitrary",)), name="ssd_fwd")(
            xact, dt_raw, proj, dt_bias, a_log, d_skip, norm_w, e)


def _ssd_bwd(cfg, xact, dt_raw, proj, y, dy2, states, dt_bias, a_log, d_skip, norm_w, e, dproj):
    s, si, cd, gw, bc, hpg = cfg.S, cfg.SI, cfg.CD, cfg.GW, cfg.BC, cfg.HPG
    nc = s // CHUNK
    zb = cfg.OZS // si
    tiles = gw // LANES

    def body(xa_ref, dtr_ref, z_ref, y_ref, d2_ref, st_ref, db_ref, al_ref, dsk_ref, nw_ref, e_ref, dp_in,
             dz_ref, dxa_ref, ddt_ref, gnw_ref, gdb_ref, gal_ref, gds_ref,
             dh, dhn, xs_s, x_s, w_s, ae_s, ea_s, g_s, dx_s, dae_s, r_s, lam_s, dle_s):
        del dp_in

        @pl.when(pl.program_id(0) == 0)
        def _():
            dh[...] = jnp.zeros_like(dh)
            gnw_ref[...] = jnp.zeros_like(gnw_ref)
            gdb_ref[...] = jnp.zeros_like(gdb_ref)
            gal_ref[...] = jnp.zeros_like(gal_ref)
            gds_ref[...] = jnp.zeros_like(gds_ref)

        ev = e_ref[...]
        yv = y_ref[...].astype(F32)
        z = z_ref[...].astype(F32)
        sg = _sigmoid(z)
        sz = z * sg
        u = yv * sz
        r = lax.rsqrt(jnp.mean(u * u, axis=-1, keepdims=True) + RMS_EPS)
        nrm = u * r
        d2 = d2_ref[...].astype(F32)
        gnw_ref[...] += jnp.sum(d2 * nrm, axis=0, keepdims=True)
        gn = d2 * nw_ref[...]
        du = r * (gn - nrm * jnp.mean(gn * nrm, axis=-1, keepdims=True))
        gv = du * sz
        dz_ref[...] = (du * yv * (sg * (1.0 + z * (1.0 - sg)))).astype(BF16)
        g_s[...] = gv

        dtr, dt, a, dt_e, a_e = _ssd_prep(dtr_ref, db_ref, al_ref, ev)
        xs = xa_ref[:, 0:si].astype(F32)
        x = xs * dt_e
        lam_e = a_e[CHUNK - 1:CHUNK, :]
        xs_s[...] = xs
        x_s[...] = x
        w_s[...] = jnp.exp(lam_e - a_e)
        ae_s[...] = a_e
        ea_s[...] = jnp.exp(a_e)
        lam_s[...] = jnp.broadcast_to(jnp.exp(lam_e), (8, si))
        gds_ref[...] += _segsum_row(jnp.sum(gv * xs, axis=0, keepdims=True), ev, 2)
        r_s[...] = jnp.zeros_like(r_s)
        tril = _tri(True)
        lane = lax.broadcasted_iota(jnp.int32, (CHUNK, LANES), 1)
        sub = lax.broadcasted_iota(jnp.int32, (CHUNK, LANES), 0)

        def group(g, carry):
            co = pl.multiple_of(g * gw, LANES)
            bo = pl.multiple_of(si + g * SSM_STATE, LANES)
            cof = pl.multiple_of(si + bc + g * SSM_STATE, LANES)
            cols = pl.ds(co, gw)
            bg = xa_ref[:, pl.ds(bo, SSM_STATE)]
            cg = xa_ref[:, pl.ds(cof, SSM_STATE)]
            cbm = _nt(cg, bg)
            st = st_ref[:, cols]
            stb = st.astype(BF16)
            dho = dh[:, cols]
            dhob = dho.astype(BF16)
            ea = ea_s[:, cols]
            gg = g_s[:, cols]
            xg = x_s[:, cols]
            wg = w_s[:, cols]
            explam = lam_s[0:1, cols]
            yoff = _nn(cg, stb) * ea
            ga = (gg * ea).astype(BF16)
            dc = _nt(ga, stb)
            dhn[:, cols] = dho * explam + _tn(cg, ga)
            bdh = _nn(bg, dhob)
            db = _nt((xg * wg).astype(BF16), dhob)
            t = xg * bdh * wg
            dle_s[0:1, cols] = jnp.sum(t, axis=0, keepdims=True) + explam * jnp.sum(dho * st, axis=0, keepdims=True)
            dae_base = gg * yoff - t
            dxw = wg * bdh
            dcb = jnp.zeros((CHUNK, CHUNK), F32)
            for k in range(tiles):
                tc = pl.multiple_of(co + k * LANES, LANES)
                ksl = slice(k * LANES, (k + 1) * LANES)
                at = ae_s[:, pl.ds(tc, LANES)]
                att = at.T
                xt = xg[:, ksl].astype(BF16)
                gt = gg[:, ksl].astype(BF16)
                dxt = dxw[:, ksl]
                place = jnp.zeros((CHUNK, LANES), F32)
                for half in range(2):
                    lo = half * SSM_HEAD_DIM
                    seg = at[:, lo:lo + 1] - att[lo:lo + 1, :]
                    dec = jnp.exp(jnp.where(tril, seg, NEG))
                    mh = cbm * dec
                    gh = jnp.where((lane >= lo) & (lane < lo + SSM_HEAD_DIM), gt, jnp.zeros_like(gt))
                    dm = _nt(gh, xt)
                    dxt = dxt + _tn(mh.astype(BF16), gh)
                    dcb = dcb + dm * dec
                    dseg = dm * mh
                    place = place + jnp.where(lane == lo, jnp.sum(dseg, axis=1, keepdims=True), 0.0)
                    hidx = g * hpg + 2 * k + half
                    r_s[...] += jnp.where(sub == hidx, jnp.sum(dseg, axis=0, keepdims=True), 0.0)
                dx_s[:, pl.ds(tc, LANES)] = dxt
                dae_s[:, pl.ds(tc, LANES)] = dae_base[:, ksl] + place
            dcbb = dcb.astype(BF16)
            dxa_ref[:, pl.ds(bo, SSM_STATE)] = (db + _tn(dcbb, cg)).astype(BF16)
            dxa_ref[:, pl.ds(cof, SSM_STATE)] = (dc + _nn(dcbb, bg)).astype(BF16)
            return carry

        lax.fori_loop(0, SSM_GROUPS, group, 0)
        dlam = _segsum_row(dle_s[0:1, :], ev, 2)
        da_ = _segsum(dae_s[...], ev, 2) - r_s[...].T
        da_ = da_ + jnp.where(sub == CHUNK - 1, dlam, 0.0)
        dda = jnp.dot(_tri(False).astype(F32), da_, precision=lax.Precision.HIGHEST, preferred_element_type=F32)
        dxv = dx_s[...]
        xs = xs_s[...]
        ddt = dda * a + _segsum(dxv * xs, ev, 2)
        gal_ref[...] += jnp.sum(dda * dt, axis=0, keepdims=True) * a
        ddtr = ddt * _sigmoid(dtr)
        gdb_ref[...] += jnp.sum(ddtr, axis=0, keepdims=True)
        ddt_ref[...] = ddtr
        dxa_ref[:, 0:si] = (dxv * dt_e + g_s[...] * _expand_row(dsk_ref[...], ev, 3)).astype(BF16)
        dh[...] = dhn[...]

    rev = lambda c: nc - 1 - c
    row = lambda n: pl.BlockSpec((1, n), lambda c: (0, 0))
    big = lambda: pltpu.VMEM((CHUNK, si), F32)
    return pl.pallas_call(
        body,
        out_shape=(SDS(dproj.shape, BF16), SDS((s, cd), BF16), SDS((s, LANES), F32),
                   SDS((1, si), F32), SDS((1, LANES), F32), SDS((1, LANES), F32), SDS((1, LANES), F32)),
        grid=(nc,),
        in_specs=[pl.BlockSpec((CHUNK, cd), lambda c: (rev(c), 0)),
                  pl.BlockSpec((CHUNK, LANES), lambda c: (rev(c), 0)),
                  pl.BlockSpec((CHUNK, si), lambda c: (rev(c), zb)),
                  pl.BlockSpec((CHUNK, si), lambda c: (rev(c), 0)),
                  pl.BlockSpec((CHUNK, si), lambda c: (rev(c), 0)),
                  pl.BlockSpec((None, SSM_STATE, si), lambda c: (rev(c), 0, 0)),
                  row(LANES), row(LANES), row(LANES), row(si),
                  pl.BlockSpec((LANES, si), lambda c: (0, 0)),
                  pl.BlockSpec(memory_space=pl.ANY)],
        out_specs=(pl.BlockSpec((CHUNK, si), lambda c: (rev(c), zb)),
                   pl.BlockSpec((CHUNK, cd), lambda c: (rev(c), 0)),
                   pl.BlockSpec((CHUNK, LANES), lambda c: (rev(c), 0)),
                   row(si), row(LANES), row(LANES), row(LANES)),
        scratch_shapes=[pltpu.VMEM((SSM_STATE, si), F32), pltpu.VMEM((SSM_STATE, si), F32),
                        big(), big(), big(), big(), big(), big(), big(), big(),
                        pltpu.VMEM((CHUNK, LANES), F32), pltpu.VMEM((8, si), F32), pltpu.VMEM((8, si), F32)],
        input_output_aliases={11: 0},
        compiler_params=_params(("arbitrary",)), name="ssd_bwd")(
            xact, dt_raw, proj, y, dy2, states, dt_bias, a_log, d_skip, norm_w, e, dproj)


MERGE_TR = 512
MERGE_CW = 2048


def _merge_fwd(cfg, proj, a_br, s_br):
    s, d = cfg.S, cfg.D
    tr, cw = MERGE_TR, min(MERGE_CW, d)
    ga0, gs0 = cfg.OGA // cw, cfg.OGS // cw

    def body(ga_ref, gs_ref, a_ref, s_ref, o_ref):
        o_ref[...] = (_sigmoid(ga_ref[...].astype(F32)) * a_ref[...].astype(F32)
                      + _sigmoid(gs_ref[...].astype(F32)) * s_ref[...].astype(F32)).astype(BF16)

    blk = pl.BlockSpec((tr, cw), lambda i, j: (i, j))
    return pl.pallas_call(
        body, out_shape=SDS((s, d), BF16), grid=(s // tr, d // cw),
        in_specs=[pl.BlockSpec((tr, cw), lambda i, j: (i, ga0 + j)),
                  pl.BlockSpec((tr, cw), lambda i, j: (i, gs0 + j)), blk, blk],
        out_specs=blk, compiler_params=_params(("parallel", "parallel")), name="merge_fwd")(proj, proj, a_br, s_br)


def _merge_bwd(cfg, proj, branch, dmerged, gate_off, dproj, name):
    s, d = cfg.S, cfg.D
    tr, cw = MERGE_TR, min(MERGE_CW, d)
    g0 = gate_off // cw
    fresh = dproj is None

    def body(*refs):
        g_ref, b_ref, dm_ref = refs[:3]
        dg_ref, db_ref = refs[-2:]
        dm = dm_ref[...].astype(F32)
        sg = _sigmoid(g_ref[...].astype(F32))
        db_ref[...] = (dm * sg).astype(BF16)
        dg_ref[...] = (dm * b_ref[...].astype(F32) * sg * (1.0 - sg)).astype(BF16)

    blk = pl.BlockSpec((tr, cw), lambda i, j: (i, j))
    gate = pl.BlockSpec((tr, cw), lambda i, j: (i, g0 + j))
    return pl.pallas_call(
        body, out_shape=(SDS((s, cfg.NM), BF16), SDS((s, d), BF16)), grid=(s // tr, d // cw),
        in_specs=[gate, blk, blk] + ([] if fresh else [HBM_SPEC]),
        out_specs=(gate, blk),
        input_output_aliases={} if fresh else {3: 0},
        compiler_params=_params(("parallel", "parallel")), name=name)(
            *((proj, branch, dmerged) + (() if fresh else (dproj,))))


def _outproj_loss(merged, w_out, x, target, fnw):
    s, d = x.shape
    tr = 256

    def body(m_ref, w_ref, x_ref, t_ref, fw_ref, dof_ref, dob_ref, loss_ref, g_ref):
        out = x_ref[...] + _nn(m_ref[...], w_ref[...])
        r = lax.rsqrt(jnp.mean(out * out, axis=-1, keepdims=True) + RMS_EPS)
        nrm = out * r
        fw = fw_ref[...]
        err = nrm * fw - t_ref[...]
        dy = err * (1.0 / d)
        gy = dy * fw
        dout = r * (gy - nrm * jnp.mean(gy * nrm, axis=-1, keepdims=True))
        dof_ref[...] = dout
        dob_ref[...] = dout.astype(BF16)

        @pl.when(pl.program_id(0) == 0)
        def _():
            loss_ref[...] = jnp.zeros_like(loss_ref)
            g_ref[...] = jnp.zeros_like(g_ref)

        loss_ref[...] += jnp.sum(jnp.sum(err * err, axis=1, keepdims=True), axis=0, keepdims=True) * (0.5 / d)
        g_ref[...] += jnp.sum(dy * nrm, axis=0, keepdims=True)

    blk = pl.BlockSpec((tr, d), lambda i: (i, 0))
    return pl.pallas_call(
        body, out_shape=(SDS((s, d), F32), SDS((s, d), BF16), SDS((1, LANES), F32), SDS((1, d), F32)), grid=(s // tr,),
        in_specs=[blk, pl.BlockSpec((d, d), lambda i: (0, 0)), blk, blk, pl.BlockSpec((1, d), lambda i: (0, 0))],
        out_specs=(blk, blk, pl.BlockSpec((1, LANES), lambda i: (0, 0)), pl.BlockSpec((1, d), lambda i: (0, 0))),
        compiler_params=_params(("arbitrary",)), name="outproj_loss")(merged, w_out, x, target, fnw)


ELEMWISE_BLOCK_BYTES = 1 << 20


def _row_block(rows, cols, itemsize=4):
    best = None
    for tr in range(16, rows + 1, 16):
        if rows % tr == 0 and tr * cols * itemsize <= ELEMWISE_BLOCK_BYTES:
            best = tr
    return best if best is not None else rows


def _adamw(w, g, m, v, name):
    rows, cols = w.shape
    tr = _row_block(rows, cols)

    def body(w_ref, g_ref, m_ref, v_ref, d_ref, nm_ref, nv_ref):
        gv = g_ref[...]
        nm = ADAM_B1 * m_ref[...] + (1.0 - ADAM_B1) * gv
        nv = ADAM_B2 * v_ref[...] + (1.0 - ADAM_B2) * jnp.square(gv)
        m_hat = nm / (1.0 - ADAM_B1 ** ADAM_STEP)
        v_hat = nv / (1.0 - ADAM_B2 ** ADAM_STEP)
        d_ref[...] = -ADAM_LR * (m_hat / (jnp.sqrt(v_hat) + ADAM_EPS) + ADAM_WD * w_ref[...])
        nm_ref[...] = nm
        nv_ref[...] = nv

    blk = pl.BlockSpec((tr, cols), lambda i: (i, 0))
    out = SDS((rows, cols), F32)
    return pl.pallas_call(
        body, out_shape=(out, out, out), grid=(rows // tr,), in_specs=[blk] * 4, out_specs=(blk,) * 3,
        compiler_params=_params(("parallel",)), name=name)(w, g, m, v)


HBM_SPEC = pl.BlockSpec(memory_space=pl.ANY)


def _position():
    return lax.axis_index("x"), lax.axis_index("y"), lax.axis_index("c")


class _Carry:
    def __init__(self, arrays, out_shapes, sems, start, finish):
        self.arrays, self.out_shapes, self.sems, self.start, self.finish = list(arrays), out_shapes, sems, start, finish

    def sem_shapes(self):
        return [pltpu.SemaphoreType.DMA((k,)) for k in self.sems]


def _run_carry(carry, name):
    n = len(carry.arrays)

    def body(*refs):
        carry.start(refs[:n], refs[n:2 * n], refs[2 * n:])
        carry.finish(refs[:n], refs[n:2 * n], refs[2 * n:])

    return pl.pallas_call(
        body, out_shape=carry.out_shapes, in_specs=[HBM_SPEC] * n, out_specs=[HBM_SPEC] * n,
        scratch_shapes=carry.sem_shapes(),
        compiler_params=pltpu.CompilerParams(has_side_effects=True), name=name)(*carry.arrays)


def _gather_carry(shards):
    n = len(shards)

    def copies(ins, outs, sems):
        send_sems, recv_sems, fsend_sems, frecv_sems = sems
        x, y, c = _position()
        me = 2 * x + y
        peers = [(1 - x, y), (x, 1 - y), (1 - x, 1 - y)]

        def over_ici(t, p, chip):
            px, py = peers[p]
            r2 = ins[t].shape[0] // 2
            return pltpu.make_async_remote_copy(
                src_ref=ins[t].at[pl.ds(c * r2, r2), :], dst_ref=outs[t].at[chip, c], send_sem=send_sems.at[3 * t + p],
                recv_sem=recv_sems.at[3 * t + p], device_id=(px, py, c), device_id_type=MESH)

        def to_sibling(t, p, half):
            px, py = peers[p]
            slab = outs[t].at[2 * px + py, half]
            return pltpu.make_async_remote_copy(
                src_ref=slab, dst_ref=slab, send_sem=fsend_sems.at[3 * t + p], recv_sem=frecv_sems.at[3 * t + p],
                device_id=(x, y, 1 - c), device_id_type=MESH)

        pairs = [(t, p) for t in range(n) for p in range(3)]
        sends = [over_ici(t, p, me) for t, p in pairs]
        lands = [over_ici(t, p, 2 * peers[p][0] + peers[p][1]) for t, p in pairs]
        passed = [to_sibling(t, p, c) for t, p in pairs]
        from_sibling = [to_sibling(t, p, 1 - c) for t, p in pairs]
        return sends, lands, passed, from_sibling

    def start(ins, outs, sems):
        for cp in copies(ins, outs, sems)[0]:
            cp.start()

    def finish(ins, outs, sems):
        sends, lands, passed, from_sibling = copies(ins, outs, sems)
        for land, fwd in zip(lands, passed):
            land.wait_recv()
            fwd.start()
        for cp in from_sibling:
            cp.wait_recv()
        for cp in sends + passed:
            cp.wait_send()

    return _Carry(shards, [SDS((N_CHIPS, 2, a.shape[0] // 2, a.shape[1]), a.dtype) for a in shards], [3 * n] * 4,
                  start, finish)


def _scatter_carry(parts):
    def start(ins, outs, sems):
        for cp in _scatter_copies(ins, outs, *sems)[0]:
            cp.start()

    def finish(ins, outs, sems):
        sends, lands = _scatter_copies(ins, outs, *sems)
        for cp in lands:
            cp.wait_recv()
        for cp in sends:
            cp.wait_send()

    return _Carry(parts, [SDS(a.shape, a.dtype) for a in parts], [3 * len(parts)] * 2, start, finish)


def _with_own(gathered, own, chip):
    full = gathered.reshape((N_CHIPS,) + own.shape)
    return lax.dynamic_update_index_in_dim(full, own, chip, 0)


def _exchange_halves(grads):
    n = len(grads)
    slabs = [list(g) if isinstance(g, (list, tuple)) else [g] for g in grads]
    flat = [a for s in slabs for a in s]
    ncp = len(flat)

    def body(*refs):
        ins, outs = refs[:ncp], refs[ncp:ncp + n]
        send_sems, recv_sems = refs[ncp + n:]
        x, y, c = _position()
        cps, k = [], 0
        for t in range(n):
            for j in range(len(slabs[t])):
                if len(slabs[t]) == 1:
                    r2 = ins[k].shape[1] // 2
                    src, dst = ins[k].at[:, pl.ds((1 - c) * r2, r2), :], outs[t]
                else:
                    r2 = ins[k].shape[0] // 2
                    src, dst = ins[k].at[pl.ds((1 - c) * r2, r2), :], outs[t].at[j]
                cps.append(pltpu.make_async_remote_copy(
                    src_ref=src, dst_ref=dst, send_sem=send_sems.at[k], recv_sem=recv_sems.at[k],
                    device_id=(x, y, 1 - c), device_id_type=MESH))
                k += 1
        for cp in cps:
            cp.start()
        for cp in cps:
            cp.wait()

    def landing(s):
        a = s[0]
        return SDS((N_CHIPS, a.shape[-2] // 2, a.shape[-1]), a.dtype)

    return pl.pallas_call(
        body, out_shape=[landing(s) for s in slabs],
        in_specs=[HBM_SPEC] * ncp, out_specs=[HBM_SPEC] * n,
        scratch_shapes=[pltpu.SemaphoreType.DMA((ncp,)), pltpu.SemaphoreType.DMA((ncp,))],
        compiler_params=pltpu.CompilerParams(has_side_effects=True), name="reduce_sibling")(*flat)


def _scatter_copies(ins, outs, send_sems, recv_sems):
    x, y, c = _position()
    me = 2 * x + y
    peers = [(1 - x, y), (x, 1 - y), (1 - x, 1 - y)]

    def remote(t, p, src_slab, dst_slab):
        px, py = peers[p]
        return pltpu.make_async_remote_copy(
            src_ref=ins[t].at[src_slab], dst_ref=outs[t].at[dst_slab], send_sem=send_sems.at[3 * t + p],
            recv_sem=recv_sems.at[3 * t + p], device_id=(px, py, c), device_id_type=MESH)

    n = len(ins)
    sends = [remote(t, p, 2 * peers[p][0] + peers[p][1], me) for t in range(n) for p in range(3)]
    lands = [remote(t, p, me, 2 * peers[p][0] + peers[p][1]) for t in range(n) for p in range(3)]
    return sends, lands


def _share_halves(halves):
    n = len(halves)

    def body(*refs):
        ins, outs = refs[:n], refs[n:2 * n]
        send_sems, recv_sems = refs[2 * n:]
        x, y, c = _position()

        def copy(t, slab):
            return pltpu.make_async_remote_copy(
                src_ref=ins[t].at[slab], dst_ref=outs[t].at[slab], send_sem=send_sems.at[t], recv_sem=recv_sems.at[t],
                device_id=(x, y, 1 - c), device_id_type=MESH)

        for t in range(n):
            copy(t, c).start()
        for t in range(n):
            copy(t, 1 - c).wait_recv()
        for t in range(n):
            copy(t, c).wait_send()

    return pl.pallas_call(
        body, out_shape=[SDS(a.shape, a.dtype) for a in halves],
        in_specs=[HBM_SPEC] * n, out_specs=[HBM_SPEC] * n,
        scratch_shapes=[pltpu.SemaphoreType.DMA((n,)), pltpu.SemaphoreType.DMA((n,))],
        input_output_aliases={t: t for t in range(n)},
        compiler_params=pltpu.CompilerParams(has_side_effects=True), name="share_sibling")(*halves)


def _add_sibling_slab(grad_j, recv, core, j, sums):
    nch, r2, cols = recv.shape
    tr = _row_block(r2, cols)
    nb = r2 // tr
    fresh = sums is None

    def body(c_ref, g_ref, r_ref, *rest):
        del c_ref
        rest[-1][...] = (g_ref[...].astype(F32) + r_ref[...].astype(F32)).astype(BF16)

    return pl.pallas_call(
        body, out_shape=SDS(recv.shape, BF16),
        grid_spec=pltpu.PrefetchScalarGridSpec(
            num_scalar_prefetch=1, grid=(nb,),
            in_specs=[pl.BlockSpec((tr, cols), lambda i, c_ref: (c_ref[0] * nb + i, 0)),
                      pl.BlockSpec((None, tr, cols), lambda i, c_ref: (j, i, 0))] + ([] if fresh else [HBM_SPEC]),
            out_specs=pl.BlockSpec((None, tr, cols), lambda i, c_ref: (j, i, 0))),
        input_output_aliases={} if fresh else {3: 0},
        compiler_params=_params(("parallel",)), name="add_sibling_slab")(
            *((core, grad_j, recv) + (() if fresh else (sums,))))


def _add_sibling(grad, recv, core):
    if isinstance(grad, (list, tuple)):
        sums = None
        for j, g in enumerate(grad):
            sums = _add_sibling_slab(g, recv, core, j, sums)
        return sums
    nch, r2, cols = recv.shape
    tr = _row_block(r2, cols)
    nb = r2 // tr

    def body(c_ref, g_ref, r_ref, o_ref):
        del c_ref
        o_ref[...] = (g_ref[...].astype(F32) + r_ref[...].astype(F32)).astype(BF16)

    return pl.pallas_call(
        body, out_shape=SDS(recv.shape, BF16),
        grid_spec=pltpu.PrefetchScalarGridSpec(
            num_scalar_prefetch=1, grid=(nch, nb),
            in_specs=[pl.BlockSpec((None, tr, cols), lambda j, i, c_ref: (j, c_ref[0] * nb + i, 0)),
                      pl.BlockSpec((None, tr, cols), lambda j, i, c_ref: (j, i, 0))],
            out_specs=pl.BlockSpec((None, tr, cols), lambda j, i, c_ref: (j, i, 0))),
        compiler_params=_params(("parallel", "parallel")), name="add_sibling")(core, grad, recv)


def _add_chips(own, recv, chip_core):
    nch, r2, cols = recv.shape
    tr = _row_block(r2, cols)

    nsc = 2 + nch

    def body(*refs):
        me = refs[0][0]
        own_ref, p_refs, o_ref = refs[nsc], refs[nsc + 1:nsc + 1 + nch], refs[nsc + 1 + nch]
        acc = None
        for j in range(nch):
            term = jnp.where(me == j, own_ref[...], p_refs[j][...]).astype(F32)
            acc = term if acc is None else acc + term
        o_ref[...] = acc

    def slab(j):
        return pl.BlockSpec((None, tr, cols), lambda i, *sc: (sc[2 + j][0], i, 0))

    return pl.pallas_call(
        body, out_shape=SDS((2, r2, cols), F32),
        grid_spec=pltpu.PrefetchScalarGridSpec(
            num_scalar_prefetch=nsc, grid=(r2 // tr,),
            in_specs=[pl.BlockSpec((None, tr, cols), lambda i, *sc: (sc[0][0], i, 0))] + [slab(j) for j in range(nch)],
            out_specs=pl.BlockSpec((None, tr, cols), lambda i, *sc: (sc[1][0], i, 0))),
        compiler_params=_params(("parallel",)), name="add_chips")(*chip_core, own, *([recv] * nch))


def _allreduce_small(pack):
    rows = pack.shape[0]

    def body(p_ref, o_ref, buf, send_sems, recv_sems):
        x, y, c = _position()
        me = 4 * x + 2 * y + c
        buf[me] = p_ref[...]

        def copy(dst_dev, slot):
            return pltpu.make_async_remote_copy(
                src_ref=p_ref, dst_ref=buf.at[slot], send_sem=send_sems.at[dst_dev], recv_sem=recv_sems.at[slot],
                device_id=(dst_dev // 4, (dst_dev // 2) % 2, dst_dev % 2), device_id_type=MESH)

        for dev in range(N_DEV):
            @pl.when(dev != me)
            def _():
                copy(dev, me).start()
        for dev in range(N_DEV):
            @pl.when(dev != me)
            def _():
                copy(dev, dev).wait_recv()
        for dev in range(N_DEV):
            @pl.when(dev != me)
            def _():
                copy(dev, me).wait_send()
        acc = buf[0]
        for dev in range(1, N_DEV):
            acc = acc + buf[dev]
        o_ref[...] = acc

    return pl.pallas_call(
        body, out_shape=SDS(pack.shape, F32),
        in_specs=[pl.BlockSpec(memory_space=pltpu.VMEM)], out_specs=pl.BlockSpec(memory_space=pltpu.VMEM),
        scratch_shapes=[pltpu.VMEM((N_DEV, rows, LANES), F32), pltpu.SemaphoreType.DMA((N_DEV,)),
                        pltpu.SemaphoreType.DMA((N_DEV,))],
        compiler_params=pltpu.CompilerParams(has_side_effects=True), name="allreduce_small")(pack)


ATTN_TQ = 256


def _local_step(cfg, x, target, w, to_chips=None, late=None):
    d = cfg.D
    hn = _rmsnorm_fwd(x, w["norm_w"])
    proj = _mm(hn, w["w_main"], "nn", BF16, "proj_main", carry=late[0] if late else None)
    if late:
        proj, arrived = proj
        w = {**w, **late[1](arrived)}
    dt_raw = _mm(hn, w["w_dt"], "nn", F32, "proj_dt")
    slopes = _slopes(cfg.H)
    near = _Pass(ATTN_TQ, DILATED_PATTERNS[:-1], 1, cfg.S)
    far = _Pass(LANES, DILATED_PATTERNS[-1:], DEINT, cfg.S // DEINT)
    tab_near, tab_far = _attn_tables(near), _attn_tables(far)
    cols_near, cols_far = (cfg.OQ, cfg.OK, cfg.OV), (0, d, 2 * d)
    qkv_far = _deinterleave(proj, 0, 3 * d, "attn_deinterleave")
    o_1, lse_1 = _attn_fwd(cfg, near, proj, cols_near, tab_near, slopes, "attn_fwd_near")
    o_2, lse_2 = _attn_fwd(cfg, far, qkv_far, cols_far, tab_far, slopes, "attn_fwd_far")
    o_a, oag, lse = _attn_merge(cfg, proj, o_1, lse_1, o_2, lse_2)
    xact = _conv_fwd(cfg, proj, w["conv_w"], w["conv_b"])
    e = _expansion_matrix(cfg)
    y, y2, states = _ssd_fwd(cfg, xact, dt_raw, proj, w["dt_bias"], w["a_log"], w["d_skip"], w["ssm_norm_w"], e)
    a_br = _mm(oag, w["w_attn"], "nn", BF16, "branch_attn")
    s_br = _mm(y2, w["w_ssm"], "nn", BF16, "branch_ssm")
    merged = _merge_fwd(cfg, proj, a_br, s_br)
    dout_f, dout_b, loss_row, g_fnw = _outproj_loss(merged, w["w_out"], x, target, w["final_norm_w"])

    dmerged = _mm(dout_b, w["w_out"], "nt", BF16, "d_merged")
    g_w_out = _mm(merged, dout_b, "tn", BF16, "g_w_out")
    dproj, da_br = _merge_bwd(cfg, proj, a_br, dmerged, cfg.OGA, None, "merge_bwd_attn")
    dproj, ds_br = _merge_bwd(cfg, proj, s_br, dmerged, cfg.OGS, dproj, "merge_bwd_ssm")
    doag = _mm(da_br, w["w_attn"], "nt", BF16, "d_oag")
    g_w_attn = _mm(oag, da_br, "tn", BF16, "g_w_attn")
    dy2 = _mm(ds_br, w["w_ssm"], "nt", BF16, "d_y2")
    g_w_ssm = _mm(y2, ds_br, "tn", BF16, "g_w_ssm")
    dproj, dxact, ddt, g_snw, g_dtb, g_alog, g_dsk = _ssd_bwd(
        cfg, xact, dt_raw, proj, y, dy2, states, w["dt_bias"], w["a_log"], w["d_skip"], w["ssm_norm_w"], e, dproj)
    dproj, g_cw, g_cb = _conv_bwd(cfg, proj, dxact, w["conv_w"], w["conv_b"], dproj)
    dproj, do, do_far, dl, dl_far, lse_far = _attn_bwd_prep(cfg, proj, o_a, doag, lse, dproj)
    g_near = _attn_bwd(cfg, near, proj, cols_near, do, lse, dl, tab_near, slopes, "attn_bwd_near")
    g_far = _attn_bwd(cfg, far, qkv_far, cols_far, do_far, lse_far, dl_far, tab_far, slopes, "attn_bwd_far")
    for g_1, g_2, col0, nm in zip(g_near, g_far, cols_near, ("attn_dq", "attn_dk", "attn_dv")):
        dproj = _attn_grad_sum(cfg, g_1, g_2, col0, dproj, nm)
    ddt_b = ddt.astype(BF16)
    g_w_main = _mm(hn, dproj, "tn", BF16, "g_w_main")
    g_w_dt = _mm(hn, ddt_b, "tn", BF16, "g_w_dt")
    grads = dict(w_main=g_w_main, w_dt=g_w_dt, conv_w=g_cw, conv_b=g_cb, dt_bias=g_dtb, a_log=g_alog,
                 d_skip=g_dsk, ssm_norm_w=g_snw, w_attn=g_w_attn, w_ssm=g_w_ssm, w_out=g_w_out, final_norm_w=g_fnw)
    sent = to_chips(grads) if to_chips is not None else ()
    dhn = _mm(dproj, w["w_main"], "nt", F32, "d_hn", tk=1024, carry=_scatter_carry(sent) if sent else None)
    landed = ()
    if sent:
        dhn, landed = dhn
    dhn_dt = _mm(ddt_b, w["w_dt"], "nt", F32, "d_hn_dt")
    grad_x, grads["norm_w"] = _rmsnorm_bwd(x, w["norm_w"], dhn, dhn_dt, dout_f)
    return loss_row, grad_x, grads, sent, landed


def _pad_lanes(v):
    return jnp.pad(v, ((0, 0), (0, LANES - v.shape[1])))


def _cut(lo, hi, a, b):
    a, b = max(lo, a), min(hi, b)
    return (a, b) if a < b else None


def _main_from_shards(cfg, shards):
    per = cfg.N_IN // len(shards)
    main, dt = [], []
    for j, sh in enumerate(shards):
        lo, hi = j * per, (j + 1) * per
        for dst, rng in ((main, (0, cfg.OGA)), (dt, (cfg.OGA, cfg.OGA + cfg.NH)), (main, (cfg.OGA + cfg.NH, cfg.N_IN))):
            c = _cut(lo, hi, *rng)
            if c is not None:
                dst.append(sh[:, c[0] - lo:c[1] - lo])
    return jnp.concatenate(main, axis=1), _pad_lanes(jnp.concatenate(dt, axis=1))


def _shards_from_main(cfg, g_main, g_dt, n):
    per = cfg.N_IN // n
    out = []
    for j in range(n):
        lo, hi = j * per, (j + 1) * per
        parts = []
        for src, off, rng in ((g_main, 0, (0, cfg.OGA)), (g_dt, cfg.OGA, (cfg.OGA, cfg.OGA + cfg.NH)),
                              (g_main, cfg.NH, (cfg.OGA + cfg.NH, cfg.N_IN))):
            c = _cut(lo, hi, *rng)
            if c is not None:
                parts.append(src[:, c[0] - off:c[1] - off])
        out.append(jnp.concatenate(parts, axis=1) if len(parts) > 1 else parts[0])
    return out


def _full_weights(cfg, norm_w, w_in_shards, conv_w, conv_b, dt_bias, a_log, d_skip, ssm_norm_w, w_attn, w_ssm, w_out, fnw):
    w_main, w_dt = _main_from_shards(cfg, w_in_shards)
    return dict(norm_w=norm_w, w_main=w_main.astype(BF16), w_dt=w_dt.astype(BF16), conv_w=conv_w, conv_b=conv_b,
                dt_bias=_pad_lanes(dt_bias), a_log=_pad_lanes(a_log), d_skip=_pad_lanes(d_skip), ssm_norm_w=ssm_norm_w,
                final_norm_w=fnw, **{k: v.astype(BF16) for k, v in (("w_attn", w_attn), ("w_ssm", w_ssm), ("w_out", w_out))
                                     if v is not None})


def kernel(x, norm_w, w_in, conv_w, conv_b, dt_bias, a_log, d_skip, ssm_norm_w, w_attn_branch, w_ssm_branch, w_out, final_norm_w, loss_target, m_norm_w, m_w_in, m_conv_w, m_conv_b, m_dt_bias, m_a_log, m_d_skip, m_ssm_norm_w, m_w_attn_branch, m_w_ssm_branch, m_w_out, m_final_norm_w, v_norm_w, v_w_in, v_conv_w, v_conv_b, v_dt_bias, v_a_log, v_d_skip, v_ssm_norm_w, v_w_attn_branch, v_w_ssm_branch, v_w_out, v_final_norm_w):
    cfg = _Cfg(x.shape[1], x.shape[2])
    d, si, cd, nh = cfg.D, cfg.SI, cfg.CD, cfg.NH
    chip = 2 * lax.axis_index("x") + lax.axis_index("y")
    core = lax.axis_index("c").astype(jnp.int32).reshape(1)
    chip = chip.astype(jnp.int32)
    chip_core = [chip.reshape(1), core] + [jnp.where(chip == j, (j + 1) % N_CHIPS, j).astype(jnp.int32).reshape(1)
                                           for j in range(N_CHIPS)]

    own = [w_in[0].astype(BF16), conv_w[0].reshape(4 * CONV_K, -1)]
    a_in, a_cw = [_with_own(g, o, chip) for g, o in zip(_run_carry(_gather_carry(own), "gather_weights"), own)]
    conv_w_full = a_cw.reshape(N_CHIPS, CONV_K, cd // N_CHIPS).transpose(1, 0, 2).reshape(CONV_K, cd)
    w = _full_weights(cfg, norm_w, [a_in[j] for j in range(N_CHIPS)], conv_w_full, conv_b, dt_bias, a_log, d_skip,
                      ssm_norm_w, None, None, None, final_norm_w.reshape(1, d))
    own_late = [w_attn_branch[0].astype(BF16), w_ssm_branch[0].astype(BF16), w_out[0].astype(BF16)]

    def late_weights(arrived):
        a_attn, a_ssm, a_out = [_with_own(g, o, chip) for g, o in zip(arrived, own_late)]
        return dict(w_attn=a_attn.reshape(d, d), w_ssm=a_ssm.reshape(si, d), w_out=a_out.reshape(d, d))

    def to_chips(grads):
        by_chip = [_shards_from_main(cfg, grads["w_main"], grads["w_dt"], N_CHIPS),
                   grads["w_attn"].reshape(N_CHIPS, d // N_CHIPS, d),
                   grads["w_ssm"].reshape(N_CHIPS, si // N_CHIPS, d),
                   grads["w_out"].reshape(N_CHIPS, d // N_CHIPS, d)]
        from_sibling = _exchange_halves(by_chip)
        return [_add_sibling(g, r, core) for g, r in zip(by_chip, from_sibling)]

    loss_row, grad_x, grads, chip_sums, from_chips = _local_step(
        cfg, x[0], loss_target[0], w, to_chips, (_gather_carry(own_late), late_weights))
    halves = [_add_chips(o, p, chip_core) for o, p in zip(chip_sums, from_chips)]
    g_in, g_attn, g_ssm, g_out = [h.reshape(2 * h.shape[1], h.shape[2]) for h in _share_halves(halves)]

    small = [loss_row, grads["norm_w"], grads["conv_b"], grads["dt_bias"], grads["a_log"], grads["d_skip"],
             grads["ssm_norm_w"], grads["final_norm_w"], grads["conv_w"].reshape(1, CONV_K * cd)]
    sizes = [a.shape[1] for a in small]
    total = sum(sizes)
    rows = -(-total // (8 * LANES)) * 8
    flat = jnp.pad(jnp.concatenate(small, axis=1), ((0, 0), (0, rows * LANES - total)))
    red = _allreduce_small(flat.reshape(rows, LANES)).reshape(1, rows * LANES)
    offs = [sum(sizes[:i]) for i in range(len(sizes))]
    loss_r, g_nw, g_cb, g_dtb, g_alog, g_dsk, g_snw, g_fnw, g_cw_flat = [
        red[:, o:o + n] for o, n in zip(offs, sizes)]
    loss = loss_r[0, 0]
    g_dtb, g_alog, g_dsk = g_dtb[:, :nh], g_alog[:, :nh], g_dsk[:, :nh]
    cshard = cd // N_CHIPS
    g_cw = lax.dynamic_slice_in_dim(g_cw_flat.reshape(CONV_K, cd), chip * cshard, cshard, axis=1)

    upd = {}
    for name, wv, gv, mv, vv in [("w_in", w_in[0], g_in, m_w_in[0], v_w_in[0]),
                                 ("w_attn", w_attn_branch[0], g_attn, m_w_attn_branch[0], v_w_attn_branch[0]),
                                 ("w_ssm", w_ssm_branch[0], g_ssm, m_w_ssm_branch[0], v_w_ssm_branch[0]),
                                 ("w_out", w_out[0], g_out, m_w_out[0], v_w_out[0])]:
        upd[name] = _adamw(wv, gv, mv, vv, "adamw_" + name)
    names = ["norm_w", "conv_w", "conv_b", "dt_bias", "a_log", "d_skip", "ssm_norm_w", "final_norm_w"]
    ws = [norm_w, conv_w[0].reshape(1, -1), conv_b, dt_bias, a_log, d_skip, ssm_norm_w, final_norm_w.reshape(1, d)]
    gs = [g_nw, g_cw.reshape(1, -1), g_cb, g_dtb, g_alog, g_dsk, g_snw, g_fnw]
    ms = [m_norm_w, m_conv_w[0].reshape(1, -1), m_conv_b, m_dt_bias, m_a_log, m_d_skip, m_ssm_norm_w,
          m_final_norm_w.reshape(1, d)]
    vs = [v_norm_w, v_conv_w[0].reshape(1, -1), v_conv_b, v_dt_bias, v_a_log, v_d_skip, v_ssm_norm_w,
          v_final_norm_w.reshape(1, d)]
    ssz = [a.shape[1] for a in ws]
    stot = sum(ssz)
    srows = -(-stot // (8 * LANES)) * 8

    def pack(parts):
        return jnp.pad(jnp.concatenate(parts, axis=1), ((0, 0), (0, srows * LANES - stot))).reshape(srows, LANES)

    packed = _adamw(pack(ws), pack(gs), pack(ms), pack(vs), "adamw_small")
    soffs = [sum(ssz[:i]) for i in range(len(ssz))]
    for k, nm in enumerate(names):
        upd[nm] = tuple(p.reshape(1, srows * LANES)[:, soffs[k]:soffs[k] + ssz[k]] for p in packed)

    shapes = dict(norm_w=norm_w.shape, w_in=w_in.shape, conv_w=conv_w.shape, conv_b=conv_b.shape, dt_bias=dt_bias.shape,
                  a_log=a_log.shape, d_skip=d_skip.shape, ssm_norm_w=ssm_norm_w.shape, w_attn=w_attn_branch.shape,
                  w_ssm=w_ssm_branch.shape, w_out=w_out.shape, final_norm_w=final_norm_w.shape)
    order = ["norm_w", "w_in", "conv_w", "conv_b", "dt_bias", "a_log", "d_skip", "ssm_norm_w", "w_attn", "w_ssm",
             "w_out", "final_norm_w"]
    gradv = dict(norm_w=g_nw, w_in=g_in, conv_w=g_cw, conv_b=g_cb, dt_bias=g_dtb, a_log=g_alog, d_skip=g_dsk,
                 ssm_norm_w=g_snw, w_attn=g_attn, w_ssm=g_ssm, w_out=g_out, final_norm_w=g_fnw)
    outs = [loss, grad_x[None]]
    outs += [gradv[n].reshape(shapes[n]) for n in order]
    for k in range(3):
        outs += [upd[n][k].reshape(shapes[n]) for n in order]
    return tuple(outs)
```

```python
import jax
import jax.numpy as jnp
from jax import lax
from jax.experimental import pallas as pl
from jax.experimental.pallas import tpu as pltpu

F32 = jnp.float32
BF16 = jnp.bfloat16
SDS = jax.ShapeDtypeStruct

RMS_EPS = 1e-6
LANES = 128
CHUNK = 128
SSM_HEAD_DIM = 64
SSM_GROUPS = 8
SSM_STATE = 128
CONV_K = 4
ATTN_HEAD_DIM = 128
DILATED_PATTERNS = ((128, 1), (512, 4), (2048, 16))
NEG = -1e30
VMEM_LIMIT = 56 * 1024 * 1024
ADAM_LR, ADAM_B1, ADAM_B2, ADAM_EPS, ADAM_WD, ADAM_STEP = 0.001, 0.9, 0.999, 1e-08, 0.01, 10
MESH = pl.DeviceIdType.MESH
N_CHIPS = 4
N_DEV = 8


class _Cfg:
    def __init__(self, s, d):
        self.S, self.D = s, d
        self.H = d // ATTN_HEAD_DIM
        self.SI = 2 * d
        self.NH = self.SI // SSM_HEAD_DIM
        self.HPG = self.NH // SSM_GROUPS
        self.GW = self.HPG * SSM_HEAD_DIM
        self.BC = SSM_GROUPS * SSM_STATE
        self.CD = self.SI + 2 * self.BC
        self.OQ, self.OK, self.OV, self.OZA = 0, d, 2 * d, 3 * d
        self.OZS = 4 * d
        self.OXBC = self.OZS + self.SI
        self.OGA = self.OXBC + self.CD
        self.OGS = self.OGA + d
        self.NM = self.OGS + d
        self.N_IN = self.NM + self.NH
        assert self.GW % LANES == 0 and self.NH <= LANES and s % 512 == 0 and d % 512 == 0


def _params(sem=None):
    return pltpu.CompilerParams(dimension_semantics=sem, vmem_limit_bytes=VMEM_LIMIT)


def _sigmoid(x):
    return 0.5 * jnp.tanh(0.5 * x) + 0.5


def _softplus(x):
    u = jnp.exp(-jnp.abs(x))
    l1p = jnp.where(u < 1e-3, u * (1.0 - u * (0.5 - u * (1.0 / 3.0))), jnp.log(1.0 + u))
    return jnp.maximum(x, 0.0) + l1p


def _nt(a, b):
    return lax.dot_general(a, b, (((1,), (1,)), ((), ())), preferred_element_type=F32)


def _tn(a, b):
    return lax.dot_general(a, b, (((0,), (0,)), ((), ())), preferred_element_type=F32)


def _nn(a, b):
    return jnp.dot(a, b, preferred_element_type=F32)


def _tile(n, target):
    if n <= target:
        return n
    best = None
    for t in range(LANES, target + 1, LANES):
        if n % t == 0:
            best = t
    assert best is not None, (n, target)
    return best


MM_TK = {"nn": 2048, "nt": 2048, "tn": 1024}


def _mm(a, b, dims, out_dtype, name, tm=1024, tn=2048, tk=None, init=None, carry=None):
    tk = MM_TK[dims] if tk is None else tk
    if dims == "nn":
        (m, k), (k2, n) = a.shape, b.shape
    elif dims == "nt":
        (m, k), (n, k2) = a.shape, b.shape
    else:
        (k, m), (k2, n) = a.shape, b.shape
    assert k == k2
    tm, tn, tk = _tile(m, tm), _tile(n, tn), _tile(k, tk)
    nk = k // tk
    if dims == "tn":
        a_spec = pl.BlockSpec((tk, tm), lambda i, j, kk: (kk, i))
    else:
        a_spec = pl.BlockSpec((tm, tk), lambda i, j, kk: (i, kk))
    if dims == "nt":
        b_spec = pl.BlockSpec((tn, tk), lambda i, j, kk: (j, kk))
    else:
        b_spec = pl.BlockSpec((tk, tn), lambda i, j, kk: (kk, j))
    o_spec = pl.BlockSpec((tm, tn), lambda i, j, kk: (i, j))
    op = {"nn": _nn, "nt": _nt, "tn": _tn}[dims]
    has_init = init is not None
    nx = len(carry.arrays) if carry is not None else 0
    ni, nj = m // tm, n // tn

    def body(*refs):
        a_ref, b_ref = refs[0], refs[1]
        i_ref = refs[2] if has_init else None
        x_in = refs[2 + has_init:2 + has_init + nx]
        o_ref = refs[2 + has_init + nx]
        x_out = refs[3 + has_init + nx:3 + has_init + 2 * nx]
        acc = refs[3 + has_init + 2 * nx]
        x_sems = refs[4 + has_init + 2 * nx:]
        i, j, kk = pl.program_id(0), pl.program_id(1), pl.program_id(2)

        if nx:
            @pl.when((i == 0) & (j == 0) & (kk == 0))
            def _():
                carry.start(x_in, x_out, x_sems)

        prod = lambda: op(a_ref[...], b_ref[...])
        with_init = (lambda p: p + i_ref[...].astype(F32)) if has_init else (lambda p: p)
        if nk == 1:
            o_ref[...] = with_init(prod()).astype(out_dtype)
        else:
            @pl.when(kk == 0)
            def _():
                acc[...] = with_init(prod())

            @pl.when((kk > 0) & (kk < nk - 1))
            def _():
                acc[...] += prod()

            @pl.when(kk == nk - 1)
            def _():
                o_ref[...] = (acc[...] + prod()).astype(out_dtype)

        if nx:
            @pl.when((i == ni - 1) & (j == nj - 1) & (kk == nk - 1))
            def _():
                carry.finish(x_in, x_out, x_sems)

    in_specs = [a_spec, b_spec] + ([o_spec] if has_init else []) + [HBM_SPEC] * nx
    args = (a, b) + ((init,) if has_init else ()) + (tuple(carry.arrays) if nx else ())
    sems = carry.sem_shapes() if nx else []
    outs = pl.pallas_call(
        body, out_shape=[SDS((m, n), out_dtype)] + (carry.out_shapes if nx else []), grid=(ni, nj, nk),
        in_specs=in_specs, out_specs=[o_spec] + [HBM_SPEC] * nx,
        scratch_shapes=[pltpu.VMEM((tm, tn) if nk > 1 else (8, LANES), F32)] + sems,
        compiler_params=_params(("arbitrary",) * 3 if nx else ("parallel", "parallel", "arbitrary")), name=name)(*args)
    return (outs[0], outs[1:]) if nx else outs[0]


def _rmsnorm_fwd(x, w, carry=None):
    s, d = x.shape
    tr = 256
    nsteps = s // tr
    nx = len(carry.arrays) if carry is not None else 0

    def body(*refs):
        x_ref, w_ref, x_in = refs[0], refs[1], refs[2:2 + nx]
        o_ref, x_out, x_sems = refs[2 + nx], refs[3 + nx:3 + 2 * nx], refs[3 + 2 * nx:]
        if nx:
            @pl.when(pl.program_id(0) == 0)
            def _():
                carry.start(x_in, x_out, x_sems)

        xv = x_ref[...]
        r = lax.rsqrt(jnp.mean(xv * xv, axis=-1, keepdims=True) + RMS_EPS)
        o_ref[...] = (xv * r * w_ref[...]).astype(BF16)

        if nx:
            @pl.when(pl.program_id(0) == nsteps - 1)
            def _():
                carry.finish(x_in, x_out, x_sems)

    outs = pl.pallas_call(
        body, out_shape=[SDS((s, d), BF16)] + (carry.out_shapes if nx else []), grid=(nsteps,),
        in_specs=[pl.BlockSpec((tr, d), lambda i: (i, 0)), pl.BlockSpec((1, d), lambda i: (0, 0))] + [HBM_SPEC] * nx,
        out_specs=[pl.BlockSpec((tr, d), lambda i: (i, 0))] + [HBM_SPEC] * nx,
        scratch_shapes=carry.sem_shapes() if nx else [],
        compiler_params=_params(("arbitrary",) if nx else ("parallel",)), name="rmsnorm_fwd")(
            x, w, *(carry.arrays if nx else []))
    return (outs[0], outs[1:]) if nx else outs[0]


def _rmsnorm_bwd(x, w, dhn_a, dhn_b, dout):
    s, d = x.shape
    tr = 256

    def body(x_ref, w_ref, dh_ref, dh2_ref, do_ref, gx_ref, gw_ref):
        xv = x_ref[...]
        r = lax.rsqrt(jnp.mean(xv * xv, axis=-1, keepdims=True) + RMS_EPS)
        nrm = xv * r
        dh = dh_ref[...] + dh2_ref[...]
        gy = dh * w_ref[...]
        gx_ref[...] = do_ref[...] + r * (gy - nrm * jnp.mean(gy * nrm, axis=-1, keepdims=True))

        @pl.when(pl.program_id(0) == 0)
        def _():
            gw_ref[...] = jnp.zeros_like(gw_ref)

        gw_ref[...] += jnp.sum(dh * nrm, axis=0, keepdims=True)

    blk = pl.BlockSpec((tr, d), lambda i: (i, 0))
    row = pl.BlockSpec((1, d), lambda i: (0, 0))
    return pl.pallas_call(
        body, out_shape=(SDS((s, d), F32), SDS((1, d), F32)), grid=(s // tr,),
        in_specs=[blk, row, blk, blk, blk], out_specs=(blk, row),
        compiler_params=_params(("arbitrary",)), name="rmsnorm_bwd")(x, w, dhn_a, dhn_b, dout)


DEINT = DILATED_PATTERNS[-1][1]
DEINT_ROWS = DEINT * LANES


class _Pass:
    def __init__(self, tq, patterns, unit, seg_len):
        self.tq, self.patterns, self.unit, self.seg_len = tq, patterns, unit, seg_len
        self.win = max(w for w, _ in patterns) // unit
        self.w = self.win + tq
        assert self.win % tq == 0


def _attn_tables(ps):
    i = jnp.arange(ps.tq, dtype=jnp.int32)[:, None]
    j = jnp.arange(ps.w, dtype=jnp.int32)[None, :]
    delta = (i + ps.win - j) * ps.unit
    n = jnp.zeros((ps.tq, ps.w), F32)
    for window, dil in ps.patterns:
        n = n + ((delta >= 0) & (delta <= window) & (delta % dil == 0)).astype(F32)
    logn = jnp.where(n > 0, jnp.log(jnp.maximum(n, 1.0)), NEG)
    return logn, jnp.maximum(delta, 0).astype(F32)


def _slopes(h):
    s = jnp.asarray([2.0 ** (-8.0 * (i + 1) / h) for i in range(h)], F32)
    return jnp.broadcast_to(s[:, None, None], (h, 1, LANES))


def _masked_logn(ps, logn_ref, start):
    col = lax.broadcasted_iota(jnp.int32, (ps.tq, ps.w), 1)
    return jnp.where(col >= ps.win - lax.rem(start, ps.seg_len), logn_ref[...], NEG)


def _head_cols(hh):
    return slice(hh * ATTN_HEAD_DIM, (hh + 1) * ATTN_HEAD_DIM)


def _head_window(refs, cs):
    return jnp.concatenate([r[:, cs] for r in refs], axis=0)


def _head_scores(q_ref, kw, cs, base, dist_ref, slope_ref, hh):
    return _nt(q_ref[:, cs], kw) * (ATTN_HEAD_DIM ** -0.5) + (base - slope_ref[hh][0:1, 0:1] * dist_ref[...])


def _lane_of(stat, hh):
    lane = lax.broadcasted_iota(jnp.int32, stat.shape, 1)
    return jnp.sum(jnp.where(lane == hh, stat, 0.0), axis=1, keepdims=True)


def _window_specs(ps, d, col, nb):
    nprev = ps.win // ps.tq
    return [pl.BlockSpec((ps.tq, d), lambda i, b=b: (jnp.maximum(jnp.minimum(i, nb - 1) - (nprev - b), 0), col))
            for b in range(nprev + 1)]


def _attn_fwd(cfg, ps, qkv, cols, tables, slopes, name):
    s, h, d = cfg.S, cfg.H, cfg.D
    tq, nw = ps.tq, ps.win // ps.tq + 1
    nb = s // tq
    logn, dist = tables
    qc, kc, vc = [c // d for c in cols]

    def body(*refs):
        q_ref, k_refs, v_refs = refs[0], refs[1:1 + nw], refs[1 + nw:1 + 2 * nw]
        logn_ref, dist_ref, slope_ref, o_ref, lse_ref = refs[1 + 2 * nw:]
        base = _masked_logn(ps, logn_ref, pl.program_id(0) * tq)
        lane = lax.broadcasted_iota(jnp.int32, (tq, LANES), 1)

        lse = jnp.zeros((tq, LANES), F32)
        for hh in range(h):
            cs = _head_cols(hh)
            sc = _head_scores(q_ref, _head_window(k_refs, cs), cs, base, dist_ref, slope_ref, hh)
            m = jnp.max(sc, axis=1, keepdims=True)
            p = jnp.exp(sc - m)
            l = jnp.sum(p, axis=1, keepdims=True)
            o_ref[:, cs] = (_nn(p.astype(BF16), _head_window(v_refs, cs)) / l).astype(BF16)
            lse = jnp.where(lane == hh, m + jnp.log(l), lse)
        lse_ref[...] = lse

    tab = pl.BlockSpec((tq, ps.w), lambda i: (0, 0))
    return pl.pallas_call(
        body, out_shape=(SDS((s, d), BF16), SDS((s, LANES), F32)), grid=(nb,),
        in_specs=[pl.BlockSpec((tq, d), lambda i: (i, qc))] + _window_specs(ps, d, kc, nb) + _window_specs(ps, d, vc, nb)
        + [tab, tab, pl.BlockSpec((h, 1, LANES), lambda i: (0, 0, 0))],
        out_specs=(pl.BlockSpec((tq, d), lambda i: (i, 0)), pl.BlockSpec((tq, LANES), lambda i: (i, 0))),
        compiler_params=_params(("parallel",)), name=name)(*([qkv] * (1 + 2 * nw)), logn, dist, slopes)


def _attn_bwd(cfg, ps, qkv, cols, do, lse, delta, tables, slopes, name):
    s, h, d = cfg.S, cfg.H, cfg.D
    tq, nprev = ps.tq, ps.win // ps.tq
    nw = nprev + 1
    nb = s // tq
    logn, dist = tables
    qc, kc, vc = [c // d for c in cols]
    scale = ATTN_HEAD_DIM ** -0.5

    def body(*refs):
        q_ref, k_refs, v_refs = refs[0], refs[1:1 + nw], refs[1 + nw:1 + 2 * nw]
        do_ref, lse_ref, dl_ref, logn_ref, dist_ref, slope_ref, dq_ref, dk_ref, dv_ref, ck, cv = refs[1 + 2 * nw:]
        i = pl.program_id(0)
        slot = lambda b: lax.rem(i + b, nprev)

        @pl.when(i == 0)
        def _():
            ck[...] = jnp.zeros_like(ck)
            cv[...] = jnp.zeros_like(cv)

        @pl.when(i < nb)
        def _():
            base = _masked_logn(ps, logn_ref, i * tq)
            lse_all, dl_all = lse_ref[...], dl_ref[...]

            for hh in range(h):
                cs = _head_cols(hh)
                kw, vw = _head_window(k_refs, cs), _head_window(v_refs, cs)
                sc = _head_scores(q_ref, kw, cs, base, dist_ref, slope_ref, hh)
                p = jnp.exp(sc - lse_all[:, hh:hh + 1])
                dob = do_ref[:, cs]
                ds = (p * (_nt(dob, vw) - dl_all[:, hh:hh + 1]) * scale).astype(BF16)
                dq_ref[:, cs] = _nn(ds, kw).astype(BF16)
                dkw = _tn(ds, q_ref[:, cs])
                dvw = _tn(p.astype(BF16), dob)
                dk_ref[:, cs] = ck[slot(0), :, cs] + dkw[0:tq]
                dv_ref[:, cs] = cv[slot(0), :, cs] + dvw[0:tq]
                for b in range(1, nprev):
                    ck[slot(b), :, cs] += dkw[b * tq:(b + 1) * tq]
                    cv[slot(b), :, cs] += dvw[b * tq:(b + 1) * tq]
                ck[slot(0), :, cs] = dkw[nprev * tq:]
                cv[slot(0), :, cs] = dvw[nprev * tq:]

        @pl.when(i >= nb)
        def _():
            dk_ref[...] = ck[slot(0)]
            dv_ref[...] = cv[slot(0)]

    here = lambda i: jnp.minimum(i, nb - 1)
    blk = pl.BlockSpec((tq, d), lambda i: (here(i), 0))
    stat = pl.BlockSpec((tq, LANES), lambda i: (here(i), 0))
    late = pl.BlockSpec((tq, d), lambda i: (jnp.maximum(i - nprev, 0), 0))
    tab = pl.BlockSpec((tq, ps.w), lambda i: (0, 0))
    return pl.pallas_call(
        body, out_shape=(SDS((s, d), BF16), SDS((s, d), F32), SDS((s, d), F32)), grid=(nb + nprev,),
        in_specs=[pl.BlockSpec((tq, d), lambda i: (here(i), qc))] + _window_specs(ps, d, kc, nb)
        + _window_specs(ps, d, vc, nb) + [blk, stat, stat, tab, tab, pl.BlockSpec((h, 1, LANES), lambda i: (0, 0, 0))],
        out_specs=(blk, late, late),
        scratch_shapes=[pltpu.VMEM((nprev, tq, d), F32), pltpu.VMEM((nprev, tq, d), F32)],
        compiler_params=_params(("arbitrary",)), name=name)(
            *([qkv] * (1 + 2 * nw)), do, lse, delta, logn, dist, slopes)


def _by_residue(a):
    return a.reshape(DEINT, a.shape[0] // DEINT, a.shape[1])


def _deint_spec(colblock):
    return pl.BlockSpec((DEINT, LANES, LANES), lambda b, j: (0, b, colblock(j)))


def _deint_rows(scr, out_ref, dtype):
    for r in range(DEINT):
        out_ref[r] = scr[pl.ds(r, LANES, stride=DEINT), :].astype(dtype)


def _int_rows(in_ref, scr):
    for r in range(DEINT):
        scr[pl.ds(r, LANES, stride=DEINT), :] = in_ref[r].astype(F32)


WIDE = 4 * LANES


def _wide_spec():
    return pl.BlockSpec((DEINT, LANES, WIDE), lambda b, j: (0, b, j))


def _deinterleave(x, col0, ncols, name):
    s = x.shape[0]
    c0 = col0 // WIDE

    def body(x_ref, o_ref, scr):
        for t in range(WIDE // LANES):
            cs = slice(t * LANES, (t + 1) * LANES)
            scr[t] = x_ref[:, cs].astype(F32)
            for r in range(DEINT):
                o_ref[r, :, cs] = scr.at[t][pl.ds(r, LANES, stride=DEINT), :].astype(x.dtype)

    out = pl.pallas_call(
        body, out_shape=SDS((DEINT, s // DEINT, ncols), x.dtype), grid=(s // DEINT_ROWS, ncols // WIDE),
        in_specs=[pl.BlockSpec((DEINT_ROWS, WIDE), lambda b, j: (b, c0 + j))],
        out_specs=_wide_spec(),
        scratch_shapes=[pltpu.VMEM((WIDE // LANES, DEINT_ROWS, LANES), F32)],
        compiler_params=_params(("parallel", "parallel")), name=name)(x)
    return out.reshape(s, ncols)


def _attn_merge(cfg, proj, o_1, lse_1, o_2, lse_2):
    s, h = cfg.S, cfg.H
    zb = cfg.OZA // WIDE
    rows = DEINT_ROWS
    hps = WIDE // LANES

    def body(o1_ref, l1_ref, o2_ref, l2_ref, z_ref, o_ref, og_ref, lse_ref, so, sl):
        j = pl.program_id(1)

        @pl.when(j == 0)
        def _():
            _int_rows(l2_ref, sl)
            lse_ref[...] = jnp.zeros_like(lse_ref)

        l1_all, l2_all = l1_ref[...], sl[...]
        lane = lax.broadcasted_iota(jnp.int32, (rows, LANES), 1)
        lse = lse_ref[...]
        for t in range(hps):
            hh = j * hps + t
            cs = slice(t * LANES, (t + 1) * LANES)
            for r in range(DEINT):
                so.at[t][pl.ds(r, LANES, stride=DEINT), :] = o2_ref[r, :, cs].astype(F32)
            l1, l2 = _lane_of(l1_all, hh), _lane_of(l2_all, hh)
            mx = jnp.maximum(l1, l2)
            w1, w2 = jnp.exp(l1 - mx), jnp.exp(l2 - mx)
            den = w1 + w2
            o = (w1 * o1_ref[:, cs].astype(F32) + w2 * so[t]) / den
            z = z_ref[:, cs].astype(F32)
            o_ref[:, cs] = o.astype(BF16)
            og_ref[:, cs] = (o * (z * _sigmoid(z))).astype(BF16)
            lse = jnp.where(lane == hh, mx + jnp.log(den), lse)
        lse_ref[...] = lse

    blk = pl.BlockSpec((rows, WIDE), lambda b, j: (b, j))
    stat = pl.BlockSpec((rows, LANES), lambda b, j: (b, 0))
    return pl.pallas_call(
        body, out_shape=(SDS((s, cfg.D), BF16), SDS((s, cfg.D), BF16), SDS((s, LANES), F32)),
        grid=(s // rows, h // hps),
        in_specs=[blk, stat, _wide_spec(), _deint_spec(lambda j: 0), pl.BlockSpec((rows, WIDE), lambda b, j: (b, zb + j))],
        out_specs=(blk, blk, stat),
        scratch_shapes=[pltpu.VMEM((hps, rows, LANES), F32), pltpu.VMEM((rows, LANES), F32)],
        compiler_params=_params(("parallel", "arbitrary")), name="attn_merge")(
            o_1, lse_1, _by_residue(o_2), _by_residue(lse_2), proj)


def _attn_bwd_prep(cfg, proj, o_a, doag, lse, dproj):
    s, h = cfg.S, cfg.H
    zb = cfg.OZA // WIDE
    rows = DEINT_ROWS
    hps = WIDE // LANES

    def body(o_ref, dg_ref, z_ref, lse_ref, dp_in, dz_ref, do_ref, do2_ref, dl_ref, dl2_ref, lse2_ref, scr):
        del dp_in
        j = pl.program_id(1)

        @pl.when(j == 0)
        def _():
            dl_ref[...] = jnp.zeros_like(dl_ref)

        lane = lax.broadcasted_iota(jnp.int32, (rows, LANES), 1)
        dl = dl_ref[...]
        for t in range(hps):
            cs = slice(t * LANES, (t + 1) * LANES)
            z = z_ref[:, cs].astype(F32)
            sg = _sigmoid(z)
            o = o_ref[:, cs].astype(F32)
            dg = dg_ref[:, cs].astype(F32)
            do = dg * (z * sg)
            dz_ref[:, cs] = (dg * o * (sg * (1.0 + z * (1.0 - sg)))).astype(BF16)
            do_ref[:, cs] = do.astype(BF16)
            scr[...] = do
            for r in range(DEINT):
                do2_ref[r, :, cs] = scr[pl.ds(r, LANES, stride=DEINT), :].astype(BF16)
            dl = jnp.where(lane == j * hps + t, jnp.sum(do * o, axis=1, keepdims=True), dl)
        dl_ref[...] = dl

        @pl.when(j == h // hps - 1)
        def _():
            scr[...] = dl
            _deint_rows(scr, dl2_ref, F32)
            scr[...] = lse_ref[...]
            _deint_rows(scr, lse2_ref, F32)

    blk = pl.BlockSpec((rows, WIDE), lambda b, j: (b, j))
    stat = pl.BlockSpec((rows, LANES), lambda b, j: (b, 0))
    stat2 = _deint_spec(lambda j: 0)
    outs = pl.pallas_call(
        body,
        out_shape=(SDS(dproj.shape, BF16), SDS((s, cfg.D), BF16), SDS((DEINT, s // DEINT, cfg.D), BF16),
                   SDS((s, LANES), F32), SDS((DEINT, s // DEINT, LANES), F32), SDS((DEINT, s // DEINT, LANES), F32)),
        grid=(s // rows, h // hps),
        in_specs=[blk, blk, pl.BlockSpec((rows, WIDE), lambda b, j: (b, zb + j)), stat, HBM_SPEC],
        out_specs=(pl.BlockSpec((rows, WIDE), lambda b, j: (b, zb + j)), blk, _wide_spec(), stat, stat2, stat2),
        scratch_shapes=[pltpu.VMEM((rows, LANES), F32)],
        input_output_aliases={4: 0},
        compiler_params=_params(("parallel", "arbitrary")), name="attn_bwd_prep")(o_a, doag, proj, lse, dproj)
    dproj, do, do2, dl, dl2, lse2 = outs
    return dproj, do, do2.reshape(s, cfg.D), dl, dl2.reshape(s, LANES), lse2.reshape(s, LANES)


def _attn_grad_sum(cfg, g_1, g_2, col0, dproj, name):
    s = cfg.S
    c0 = col0 // WIDE
    rows = DEINT_ROWS

    def body(g1_ref, g2_ref, dp_in, o_ref, scr):
        del dp_in
        for t in range(WIDE // LANES):
            cs = slice(t * LANES, (t + 1) * LANES)
            for r in range(DEINT):
                scr.at[t][pl.ds(r, LANES, stride=DEINT), :] = g2_ref[r, :, cs].astype(F32)
            o_ref[:, cs] = (g1_ref[:, cs].astype(F32) + scr[t]).astype(BF16)

    return pl.pallas_call(
        body, out_shape=SDS(dproj.shape, BF16), grid=(s // rows, cfg.D // WIDE),
        in_specs=[pl.BlockSpec((rows, WIDE), lambda b, j: (b, j)), _wide_spec(), HBM_SPEC],
        out_specs=pl.BlockSpec((rows, WIDE), lambda b, j: (b, c0 + j)),
        scratch_shapes=[pltpu.VMEM((WIDE // LANES, rows, LANES), F32)],
        input_output_aliases={2: 0},
        compiler_params=_params(("parallel", "parallel")), name=name)(g_1, _by_residue(g_2), dproj)


CONV_HALO = 16
CONV_TR = 512
CONV_CW = 1024


def _rows_back(a, n):
    return a if n == 0 else pltpu.roll(a, n % a.shape[0], axis=0)


def _conv_fwd(cfg, proj, conv_w, conv_b):
    s, cd = cfg.S, cfg.CD
    tr, cw, hl = CONV_TR, CONV_CW, CONV_HALO
    cb0 = cfg.OXBC // cw

    def body(x_ref, h_ref, w_ref, b_ref, o_ref):
        i = pl.program_id(0)
        halo = jnp.where(i > 0, h_ref[...].astype(F32), 0.0)
        ext = jnp.concatenate([halo, x_ref[...].astype(F32)], axis=0)
        pre = b_ref[...] + jnp.zeros((tr, cw), F32)
        for k in range(CONV_K):
            pre = pre + w_ref[k:k + 1, :] * _rows_back(ext, CONV_K - 1 - k)[hl:]
        o_ref[...] = (pre * _sigmoid(pre)).astype(BF16)

    return pl.pallas_call(
        body, out_shape=SDS((s, cd), BF16), grid=(s // tr, cd // cw),
        in_specs=[pl.BlockSpec((tr, cw), lambda i, j: (i, cb0 + j)),
                  pl.BlockSpec((hl, cw), lambda i, j: (jnp.maximum(i * (tr // hl) - 1, 0), cb0 + j)),
                  pl.BlockSpec((CONV_K, cw), lambda i, j: (0, j)),
                  pl.BlockSpec((1, cw), lambda i, j: (0, j))],
        out_specs=pl.BlockSpec((tr, cw), lambda i, j: (i, j)),
        compiler_params=_params(("parallel", "parallel")), name="conv_fwd")(proj, proj, conv_w, conv_b)


def _conv_bwd(cfg, proj, dact, conv_w, conv_b, dproj):
    s, cd = cfg.S, cfg.CD
    tr, cw, hl = CONV_TR, CONV_CW, CONV_HALO
    cb0 = cfg.OXBC // cw
    nr = s // tr
    last_h = s // hl - 1

    def body(x_ref, hp_ref, hn_ref, d_ref, dn_ref, w_ref, b_ref, dp_in, dx_ref, gw_ref, gb_ref):
        del dp_in
        i = pl.program_id(1)
        ext = jnp.concatenate([jnp.where(i > 0, hp_ref[...].astype(F32), 0.0), x_ref[...].astype(F32),
                               hn_ref[...].astype(F32)], axis=0)
        shifted = [_rows_back(ext, CONV_K - 1 - k)[hl:] for k in range(CONV_K)]
        pre = b_ref[...] + jnp.zeros((tr + hl, cw), F32)
        for k in range(CONV_K):
            pre = pre + w_ref[k:k + 1, :] * shifted[k]
        sg = _sigmoid(pre)
        dact = jnp.concatenate([d_ref[...].astype(F32), jnp.where(i < nr - 1, dn_ref[...].astype(F32), 0.0)], axis=0)
        dpre = dact * (sg * (1.0 + pre * (1.0 - sg)))
        dx = jnp.zeros((tr, cw), F32)
        for k in range(CONV_K):
            dx = dx + w_ref[k:k + 1, :] * _rows_back(dpre, -(CONV_K - 1 - k))[0:tr]
        dx_ref[...] = dx.astype(BF16)

        @pl.when(i == 0)
        def _():
            gw_ref[...] = jnp.zeros_like(gw_ref)
            gb_ref[...] = jnp.zeros_like(gb_ref)

        dcur = dpre[0:tr]
        gb_ref[...] += jnp.sum(dcur, axis=0, keepdims=True)
        for k in range(CONV_K):
            gw_ref[k:k + 1, :] += jnp.sum(dcur * shifted[k][0:tr], axis=0, keepdims=True)

    return pl.pallas_call(
        body, out_shape=(SDS(dproj.shape, BF16), SDS((CONV_K, cd), F32), SDS((1, cd), F32)), grid=(cd // cw, nr),
        in_specs=[pl.BlockSpec((tr, cw), lambda j, i: (i, cb0 + j)),
                  pl.BlockSpec((hl, cw), lambda j, i: (jnp.maximum(i * (tr // hl) - 1, 0), cb0 + j)),
                  pl.BlockSpec((hl, cw), lambda j, i: (jnp.minimum((i + 1) * (tr // hl), last_h), cb0 + j)),
                  pl.BlockSpec((tr, cw), lambda j, i: (i, j)),
                  pl.BlockSpec((hl, cw), lambda j, i: (jnp.minimum((i + 1) * (tr // hl), last_h), j)),
                  pl.BlockSpec((CONV_K, cw), lambda j, i: (0, j)),
                  pl.BlockSpec((1, cw), lambda j, i: (0, j)),
                  pl.BlockSpec(memory_space=pl.ANY)],
        out_specs=(pl.BlockSpec((tr, cw), lambda j, i: (i, cb0 + j)),
                   pl.BlockSpec((CONV_K, cw), lambda j, i: (0, j)),
                   pl.BlockSpec((1, cw), lambda j, i: (0, j))),
        input_output_aliases={7: 0},
        compiler_params=_params(("parallel", "arbitrary")), name="conv_bwd")(
            proj, proj, proj, dact, dact, conv_w, conv_b, dproj)


def _expand(v, e, terms):
    out, rem = None, v
    for _ in range(terms):
        hi = rem.astype(BF16)
        t = _nn(hi, e)
        out = t if out is None else out + t
        rem = rem - hi.astype(F32)
    return out


def _segsum(v, e, terms):
    out, rem = None, v
    for _ in range(terms):
        hi = rem.astype(BF16)
        t = _nt(hi, e)
        out = t if out is None else out + t
        rem = rem - hi.astype(F32)
    return out


def _expand_row(row, e, terms):
    return _expand(jnp.broadcast_to(row, (8, LANES)), e, terms)[0:1]


def _segsum_row(row, e, terms):
    return _segsum(jnp.broadcast_to(row, (8, row.shape[1])), e, terms)[0:1]


def _expansion_matrix(cfg):
    hh = jnp.arange(LANES, dtype=jnp.int32)[:, None]
    cc = jnp.arange(cfg.SI, dtype=jnp.int32)[None, :]
    return (cc // SSM_HEAD_DIM == hh).astype(BF16)


def _tri(lower):
    r = lax.broadcasted_iota(jnp.int32, (CHUNK, CHUNK), 0)
    c = lax.broadcasted_iota(jnp.int32, (CHUNK, CHUNK), 1)
    return (c <= r) if lower else (c >= r)


def _ssd_prep(dtr_ref, db_ref, al_ref, e):
    dtr = dtr_ref[...] + db_ref[...]
    dt = _softplus(dtr)
    a = -jnp.exp(al_ref[...])
    acum = jnp.dot(_tri(True).astype(F32), dt * a, precision=lax.Precision.HIGHEST, preferred_element_type=F32)
    return dtr, dt, a, _expand(dt, e, 2), _expand(acum, e, 3)


def _ssd_fwd(cfg, xact, dt_raw, proj, dt_bias, a_log, d_skip, norm_w, e):
    s, si, cd, gw, bc = cfg.S, cfg.SI, cfg.CD, cfg.GW, cfg.BC
    nc = s // CHUNK
    zb = cfg.OZS // si
    tiles = gw // LANES

    def body(xa_ref, dtr_ref, z_ref, db_ref, al_ref, dsk_ref, nw_ref, e_ref, y_ref, y2_ref, st_ref,
             state, ybuf, x_s, xw_s, ae_s, ea_s, lam_s):
        @pl.when(pl.program_id(0) == 0)
        def _():
            state[...] = jnp.zeros_like(state)

        st_ref[...] = state[...]
        ev = e_ref[...]
        _, _, _, dt_e, a_e = _ssd_prep(dtr_ref, db_ref, al_ref, ev)
        xs = xa_ref[:, 0:si].astype(F32)
        x = xs * dt_e
        lam_e = a_e[CHUNK - 1:CHUNK, :]
        x_s[...] = x.astype(BF16)
        xw_s[...] = (x * jnp.exp(lam_e - a_e)).astype(BF16)
        ae_s[...] = a_e
        ea_s[...] = jnp.exp(a_e)
        ybuf[...] = _expand_row(dsk_ref[...], ev, 3) * xs
        lam_s[...] = jnp.broadcast_to(jnp.exp(lam_e), (8, si))
        tril = _tri(True)
        lane = lax.broadcasted_iota(jnp.int32, (CHUNK, LANES), 1)

        def group(g, carry):
            co = pl.multiple_of(g * gw, LANES)
            bg = xa_ref[:, pl.ds(pl.multiple_of(si + g * SSM_STATE, LANES), SSM_STATE)]
            cg = xa_ref[:, pl.ds(pl.multiple_of(si + bc + g * SSM_STATE, LANES), SSM_STATE)]
            cbm = _nt(cg, bg)
            st = state[:, pl.ds(co, gw)]
            yoff = _nn(cg, st.astype(BF16)) * ea_s[:, pl.ds(co, gw)]
            for k in range(tiles):
                tc = pl.multiple_of(co + k * LANES, LANES)
                at = ae_s[:, pl.ds(tc, LANES)]
                att = at.T
                xt = x_s[:, pl.ds(tc, LANES)]
                acc = yoff[:, k * LANES:(k + 1) * LANES]
                for half in range(2):
                    lo = half * SSM_HEAD_DIM
                    seg = at[:, lo:lo + 1] - att[lo:lo + 1, :]
                    dec = jnp.exp(jnp.where(tril, seg, NEG))
                    xh = jnp.where((lane >= lo) & (lane < lo + SSM_HEAD_DIM), xt, jnp.zeros_like(xt))
                    acc = acc + _nn((cbm * dec).astype(BF16), xh)
                ybuf[:, pl.ds(tc, LANES)] += acc
            state[:, pl.ds(co, gw)] = st * lam_s[0:1, pl.ds(co, gw)] + _tn(bg, xw_s[:, pl.ds(co, gw)])
            return carry

        lax.fori_loop(0, SSM_GROUPS, group, 0)
        y = ybuf[...]
        y_ref[...] = y.astype(BF16)
        z = z_ref[...].astype(F32)
        u = y * (z * _sigmoid(z))
        r = lax.rsqrt(jnp.mean(u * u, axis=-1, keepdims=True) + RMS_EPS)
        y2_ref[...] = (u * r * nw_ref[...]).astype(BF16)

    row = lambda n: pl.BlockSpec((1, n), lambda c: (0, 0))
    return pl.pallas_call(
        body,
        out_shape=(SDS((s, si), BF16), SDS((s, si), BF16), SDS((nc, SSM_STATE, si), F32)),
        grid=(nc,),
        in_specs=[pl.BlockSpec((CHUNK, cd), lambda c: (c, 0)),
                  pl.BlockSpec((CHUNK, LANES), lambda c: (c, 0)),
                  pl.BlockSpec((CHUNK, si), lambda c: (c, zb)),
                  row(LANES), row(LANES), row(LANES), row(si),
                  pl.BlockSpec((LANES, si), lambda c: (0, 0))],
        out_specs=(pl.BlockSpec((CHUNK, si), lambda c: (c, 0)),
                   pl.BlockSpec((CHUNK, si), lambda c: (c, 0)),
                   pl.BlockSpec((None, SSM_STATE, si), lambda c: (c, 0, 0))),
        scratch_shapes=[pltpu.VMEM((SSM_STATE, si), F32), pltpu.VMEM((CHUNK, si), F32),
                        pltpu.VMEM((CHUNK, si), BF16), pltpu.VMEM((CHUNK, si), BF16),
                        pltpu.VMEM((CHUNK, si), F32), pltpu.VMEM((CHUNK, si), F32),
                        pltpu.VMEM((8, si), F32)],
        compiler_params=_params(("arbitrary",)), name="ssd_fwd")(
            xact, dt_raw, proj, dt_bias, a_log, d_skip, norm_w, e)


def _ssd_bwd(cfg, xact, dt_raw, proj, y, dy2, states, dt_bias, a_log, d_skip, norm_w, e, dproj):
    s, si, cd, gw, bc, hpg = cfg.S, cfg.SI, cfg.CD, cfg.GW, cfg.BC, cfg.HPG
    nc = s // CHUNK
    zb = cfg.OZS // si
    tiles = gw // LANES

    def body(xa_ref, dtr_ref, z_ref, y_ref, d2_ref, st_ref, db_ref, al_ref, dsk_ref, nw_ref, e_ref, dp_in,
             dz_ref, dxa_ref, ddt_ref, gnw_ref, gdb_ref, gal_ref, gds_ref,
             dh, dhn, xs_s, x_s, w_s, ae_s, ea_s, g_s, dx_s, dae_s, r_s, lam_s, dle_s):
        del dp_in

        @pl.when(pl.program_id(0) == 0)
        def _():
            dh[...] = jnp.zeros_like(dh)
            gnw_ref[...] = jnp.zeros_like(gnw_ref)
            gdb_ref[...] = jnp.zeros_like(gdb_ref)
            gal_ref[...] = jnp.zeros_like(gal_ref)
            gds_ref[...] = jnp.zeros_like(gds_ref)

        ev = e_ref[...]
        yv = y_ref[...].astype(F32)
        z = z_ref[...].astype(F32)
        sg = _sigmoid(z)
        sz = z * sg
        u = yv * sz
        r = lax.rsqrt(jnp.mean(u * u, axis=-1, keepdims=True) + RMS_EPS)
        nrm = u * r
        d2 = d2_ref[...].astype(F32)
        gnw_ref[...] += jnp.sum(d2 * nrm, axis=0, keepdims=True)
        gn = d2 * nw_ref[...]
        du = r * (gn - nrm * jnp.mean(gn * nrm, axis=-1, keepdims=True))
        gv = du * sz
        dz_ref[...] = (du * yv * (sg * (1.0 + z * (1.0 - sg)))).astype(BF16)
        g_s[...] = gv

        dtr, dt, a, dt_e, a_e = _ssd_prep(dtr_ref, db_ref, al_ref, ev)
        xs = xa_ref[:, 0:si].astype(F32)
        x = xs * dt_e
        lam_e = a_e[CHUNK - 1:CHUNK, :]
        xs_s[...] = xs
        x_s[...] = x
        w_s[...] = jnp.exp(lam_e - a_e)
        ae_s[...] = a_e
        ea_s[...] = jnp.exp(a_e)
        lam_s[...] = jnp.broadcast_to(jnp.exp(lam_e), (8, si))
        gds_ref[...] += _segsum_row(jnp.sum(gv * xs, axis=0, keepdims=True), ev, 2)
        r_s[...] = jnp.zeros_like(r_s)
        tril = _tri(True)
        lane = lax.broadcasted_iota(jnp.int32, (CHUNK, LANES), 1)
        sub = lax.broadcasted_iota(jnp.int32, (CHUNK, LANES), 0)

        def group(g, carry):
            co = pl.multiple_of(g * gw, LANES)
            bo = pl.multiple_of(si + g * SSM_STATE, LANES)
            cof = pl.multiple_of(si + bc + g * SSM_STATE, LANES)
            cols = pl.ds(co, gw)
            bg = xa_ref[:, pl.ds(bo, SSM_STATE)]
            cg = xa_ref[:, pl.ds(cof, SSM_STATE)]
            cbm = _nt(cg, bg)
            st = st_ref[:, cols]
            stb = st.astype(BF16)
            dho = dh[:, cols]
            dhob = dho.astype(BF16)
            ea = ea_s[:, cols]
            gg = g_s[:, cols]
            xg = x_s[:, cols]
            wg = w_s[:, cols]
            explam = lam_s[0:1, cols]
            yoff = _nn(cg, stb) * ea
            ga = (gg * ea).astype(BF16)
            dc = _nt(ga, stb)
            dhn[:, cols] = dho * explam + _tn(cg, ga)
            bdh = _nn(bg, dhob)
            db = _nt((xg * wg).astype(BF16), dhob)
            t = xg * bdh * wg
            dle_s[0:1, cols] = jnp.sum(t, axis=0, keepdims=True) + explam * jnp.sum(dho * st, axis=0, keepdims=True)
            dae_base = gg * yoff - t
            dxw = wg * bdh
            dcb = jnp.zeros((CHUNK, CHUNK), F32)
            for k in range(tiles):
                tc = pl.multiple_of(co + k * LANES, LANES)
                ksl = slice(k * LANES, (k + 1) * LANES)
                at = ae_s[:, pl.ds(tc, LANES)]
                att = at.T
                xt = xg[:, ksl].astype(BF16)
                gt = gg[:, ksl].astype(BF16)
                dxt = dxw[:, ksl]
                place = jnp.zeros((CHUNK, LANES), F32)
                for half in range(2):
                    lo = half * SSM_HEAD_DIM
                    seg = at[:, lo:lo + 1] - att[lo:lo + 1, :]
                    dec = jnp.exp(jnp.where(tril, seg, NEG))
                    mh = cbm * dec
                    gh = jnp.where((lane >= lo) & (lane < lo + SSM_HEAD_DIM), gt, jnp.zeros_like(gt))
                    dm = _nt(gh, xt)
                    dxt = dxt + _tn(mh.astype(BF16), gh)
                    dcb = dcb + dm * dec
                    dseg = dm * mh
                    place = place + jnp.where(lane == lo, jnp.sum(dseg, axis=1, keepdims=True), 0.0)
                    hidx = g * hpg + 2 * k + half
                    r_s[...] += jnp.where(sub == hidx, jnp.sum(dseg, axis=0, keepdims=True), 0.0)
                dx_s[:, pl.ds(tc, LANES)] = dxt
                dae_s[:, pl.ds(tc, LANES)] = dae_base[:, ksl] + place
            dcbb = dcb.astype(BF16)
            dxa_ref[:, pl.ds(bo, SSM_STATE)] = (db + _tn(dcbb, cg)).astype(BF16)
            dxa_ref[:, pl.ds(cof, SSM_STATE)] = (dc + _nn(dcbb, bg)).astype(BF16)
            return carry

        lax.fori_loop(0, SSM_GROUPS, group, 0)
        dlam = _segsum_row(dle_s[0:1, :], ev, 2)
        da_ = _segsum(dae_s[...], ev, 2) - r_s[...].T
        da_ = da_ + jnp.where(sub == CHUNK - 1, dlam, 0.0)
        dda = jnp.dot(_tri(False).astype(F32), da_, precision=lax.Precision.HIGHEST, preferred_element_type=F32)
        dxv = dx_s[...]
        xs = xs_s[...]
        ddt = dda * a + _segsum(dxv * xs, ev, 2)
        gal_ref[...] += jnp.sum(dda * dt, axis=0, keepdims=True) * a
        ddtr = ddt * _sigmoid(dtr)
        gdb_ref[...] += jnp.sum(ddtr, axis=0, keepdims=True)
        ddt_ref[...] = ddtr
        dxa_ref[:, 0:si] = (dxv * dt_e + g_s[...] * _expand_row(dsk_ref[...], ev, 3)).astype(BF16)
        dh[...] = dhn[...]

    rev = lambda c: nc - 1 - c
    row = lambda n: pl.BlockSpec((1, n), lambda c: (0, 0))
    big = lambda: pltpu.VMEM((CHUNK, si), F32)
    return pl.pallas_call(
        body,
        out_shape=(SDS(dproj.shape, BF16), SDS((s, cd), BF16), SDS((s, LANES), F32),
                   SDS((1, si), F32), SDS((1, LANES), F32), SDS((1, LANES), F32), SDS((1, LANES), F32)),
        grid=(nc,),
        in_specs=[pl.BlockSpec((CHUNK, cd), lambda c: (rev(c), 0)),
                  pl.BlockSpec((CHUNK, LANES), lambda c: (rev(c), 0)),
                  pl.BlockSpec((CHUNK, si), lambda c: (rev(c), zb)),
                  pl.BlockSpec((CHUNK, si), lambda c: (rev(c), 0)),
                  pl.BlockSpec((CHUNK, si), lambda c: (rev(c), 0)),
                  pl.BlockSpec((None, SSM_STATE, si), lambda c: (rev(c), 0, 0)),
                  row(LANES), row(LANES), row(LANES), row(si),
                  pl.BlockSpec((LANES, si), lambda c: (0, 0)),
                  pl.BlockSpec(memory_space=pl.ANY)],
        out_specs=(pl.BlockSpec((CHUNK, si), lambda c: (rev(c), zb)),
                   pl.BlockSpec((CHUNK, cd), lambda c: (rev(c), 0)),
                   pl.BlockSpec((CHUNK, LANES), lambda c: (rev(c), 0)),
                   row(si), row(LANES), row(LANES), row(LANES)),
        scratch_shapes=[pltpu.VMEM((SSM_STATE, si), F32), pltpu.VMEM((SSM_STATE, si), F32),
                        big(), big(), big(), big(), big(), big(), big(), big(),
                        pltpu.VMEM((CHUNK, LANES), F32), pltpu.VMEM((8, si), F32), pltpu.VMEM((8, si), F32)],
        input_output_aliases={11: 0},
        compiler_params=_params(("arbitrary",)), name="ssd_bwd")(
            xact, dt_raw, proj, y, dy2, states, dt_bias, a_log, d_skip, norm_w, e, dproj)


MERGE_TR = 512
MERGE_CW = 2048


def _merge_fwd(cfg, proj, a_br, s_br):
    s, d = cfg.S, cfg.D
    tr, cw = MERGE_TR, min(MERGE_CW, d)
    ga0, gs0 = cfg.OGA // cw, cfg.OGS // cw

    def body(ga_ref, gs_ref, a_ref, s_ref, o_ref):
        o_ref[...] = (_sigmoid(ga_ref[...].astype(F32)) * a_ref[...].astype(F32)
                      + _sigmoid(gs_ref[...].astype(F32)) * s_ref[...].astype(F32)).astype(BF16)

    blk = pl.BlockSpec((tr, cw), lambda i, j: (i, j))
    return pl.pallas_call(
        body, out_shape=SDS((s, d), BF16), grid=(s // tr, d // cw),
        in_specs=[pl.BlockSpec((tr, cw), lambda i, j: (i, ga0 + j)),
                  pl.BlockSpec((tr, cw), lambda i, j: (i, gs0 + j)), blk, blk],
        out_specs=blk, compiler_params=_params(("parallel", "parallel")), name="merge_fwd")(proj, proj, a_br, s_br)


def _merge_bwd(cfg, proj, branch, dmerged, gate_off, dproj, name):
    s, d = cfg.S, cfg.D
    tr, cw = MERGE_TR, min(MERGE_CW, d)
    g0 = gate_off // cw
    fresh = dproj is None

    def body(*refs):
        g_ref, b_ref, dm_ref = refs[:3]
        dg_ref, db_ref = refs[-2:]
        dm = dm_ref[...].astype(F32)
        sg = _sigmoid(g_ref[...].astype(F32))
        db_ref[...] = (dm * sg).astype(BF16)
        dg_ref[...] = (dm * b_ref[...].astype(F32) * sg * (1.0 - sg)).astype(BF16)

    blk = pl.BlockSpec((tr, cw), lambda i, j: (i, j))
    gate = pl.BlockSpec((tr, cw), lambda i, j: (i, g0 + j))
    return pl.pallas_call(
        body, out_shape=(SDS((s, cfg.NM), BF16), SDS((s, d), BF16)), grid=(s // tr, d // cw),
        in_specs=[gate, blk, blk] + ([] if fresh else [HBM_SPEC]),
        out_specs=(gate, blk),
        input_output_aliases={} if fresh else {3: 0},
        compiler_params=_params(("parallel", "parallel")), name=name)(
            *((proj, branch, dmerged) + (() if fresh else (dproj,))))


def _outproj_loss(merged, w_out, x, target, fnw):
    s, d = x.shape
    tr = 256

    def body(m_ref, w_ref, x_ref, t_ref, fw_ref, dof_ref, dob_ref, loss_ref, g_ref):
        out = x_ref[...] + _nn(m_ref[...], w_ref[...])
        r = lax.rsqrt(jnp.mean(out * out, axis=-1, keepdims=True) + RMS_EPS)
        nrm = out * r
        fw = fw_ref[...]
        err = nrm * fw - t_ref[...]
        dy = err * (1.0 / d)
        gy = dy * fw
        dout = r * (gy - nrm * jnp.mean(gy * nrm, axis=-1, keepdims=True))
        dof_ref[...] = dout
        dob_ref[...] = dout.astype(BF16)

        @pl.when(pl.program_id(0) == 0)
        def _():
            loss_ref[...] = jnp.zeros_like(loss_ref)
            g_ref[...] = jnp.zeros_like(g_ref)

        loss_ref[...] += jnp.sum(jnp.sum(err * err, axis=1, keepdims=True), axis=0, keepdims=True) * (0.5 / d)
        g_ref[...] += jnp.sum(dy * nrm, axis=0, keepdims=True)

    blk = pl.BlockSpec((tr, d), lambda i: (i, 0))
    return pl.pallas_call(
        body, out_shape=(SDS((s, d), F32), SDS((s, d), BF16), SDS((1, LANES), F32), SDS((1, d), F32)), grid=(s // tr,),
        in_specs=[blk, pl.BlockSpec((d, d), lambda i: (0, 0)), blk, blk, pl.BlockSpec((1, d), lambda i: (0, 0))],
        out_specs=(blk, blk, pl.BlockSpec((1, LANES), lambda i: (0, 0)), pl.BlockSpec((1, d), lambda i: (0, 0))),
        compiler_params=_params(("arbitrary",)), name="outproj_loss")(merged, w_out, x, target, fnw)


ELEMWISE_BLOCK_BYTES = 1 << 20


def _row_block(rows, cols, itemsize=4):
    best = None
    for tr in range(16, rows + 1, 16):
        if rows % tr == 0 and tr * cols * itemsize <= ELEMWISE_BLOCK_BYTES:
            best = tr
    return best if best is not None else rows


def _adamw(w, g, m, v, name):
    rows, cols = w.shape
    tr = _row_block(rows, cols)

    def body(w_ref, g_ref, m_ref, v_ref, d_ref, nm_ref, nv_ref):
        gv = g_ref[...]
        nm = ADAM_B1 * m_ref[...] + (1.0 - ADAM_B1) * gv
        nv = ADAM_B2 * v_ref[...] + (1.0 - ADAM_B2) * jnp.square(gv)
        m_hat = nm / (1.0 - ADAM_B1 ** ADAM_STEP)
        v_hat = nv / (1.0 - ADAM_B2 ** ADAM_STEP)
        d_ref[...] = -ADAM_LR * (m_hat / (jnp.sqrt(v_hat) + ADAM_EPS) + ADAM_WD * w_ref[...])
        nm_ref[...] = nm
        nv_ref[...] = nv

    blk = pl.BlockSpec((tr, cols), lambda i: (i, 0))
    out = SDS((rows, cols), F32)
    return pl.pallas_call(
        body, out_shape=(out, out, out), grid=(rows // tr,), in_specs=[blk] * 4, out_specs=(blk,) * 3,
        compiler_params=_params(("parallel",)), name=name)(w, g, m, v)


HBM_SPEC = pl.BlockSpec(memory_space=pl.ANY)


def _position():
    return lax.axis_index("x"), lax.axis_index("y"), lax.axis_index("c")


class _Carry:
    def __init__(self, arrays, out_shapes, sems, start, finish):
        self.arrays, self.out_shapes, self.sems, self.start, self.finish = list(arrays), out_shapes, sems, start, finish

    def sem_shapes(self):
        return [pltpu.SemaphoreType.DMA((k,)) for k in self.sems]


def _gather_carry(shards):
    n = len(shards)

    def copies(ins, outs, sems):
        send_sems, recv_sems, fsend_sems, frecv_sems = sems
        x, y, c = _position()
        me = 2 * x + y
        peers = [(1 - x, y), (x, 1 - y), (1 - x, 1 - y)]

        def over_ici(t, p, chip):
            px, py = peers[p]
            r2 = ins[t].shape[0] // 2
            return pltpu.make_async_remote_copy(
                src_ref=ins[t].at[pl.ds(c * r2, r2), :], dst_ref=outs[t].at[chip, c], send_sem=send_sems.at[3 * t + p],
                recv_sem=recv_sems.at[3 * t + p], device_id=(px, py, c), device_id_type=MESH)

        def to_sibling(t, p, half):
            px, py = peers[p]
            slab = outs[t].at[2 * px + py, half]
            return pltpu.make_async_remote_copy(
                src_ref=slab, dst_ref=slab, send_sem=fsend_sems.at[3 * t + p], recv_sem=frecv_sems.at[3 * t + p],
                device_id=(x, y, 1 - c), device_id_type=MESH)

        pairs = [(t, p) for t in range(n) for p in range(3)]
        sends = [over_ici(t, p, me) for t, p in pairs]
        lands = [over_ici(t, p, 2 * peers[p][0] + peers[p][1]) for t, p in pairs]
        passed = [to_sibling(t, p, c) for t, p in pairs]
        from_sibling = [to_sibling(t, p, 1 - c) for t, p in pairs]
        return sends, lands, passed, from_sibling

    def start(ins, outs, sems):
        for cp in copies(ins, outs, sems)[0]:
            cp.start()

    def finish(ins, outs, sems):
        sends, lands, passed, from_sibling = copies(ins, outs, sems)
        for land, fwd in zip(lands, passed):
            land.wait_recv()
            fwd.start()
        for cp in from_sibling:
            cp.wait_recv()
        for cp in sends + passed:
            cp.wait_send()

    return _Carry(shards, [SDS((N_CHIPS, 2, a.shape[0] // 2, a.shape[1]), a.dtype) for a in shards], [3 * n] * 4,
                  start, finish)


def _scatter_carry(parts):
    def start(ins, outs, sems):
        for cp in _scatter_copies(ins, outs, *sems)[0]:
            cp.start()

    def finish(ins, outs, sems):
        sends, lands = _scatter_copies(ins, outs, *sems)
        for cp in lands:
            cp.wait_recv()
        for cp in sends:
            cp.wait_send()

    return _Carry(parts, [SDS(a.shape, a.dtype) for a in parts], [3 * len(parts)] * 2, start, finish)


def _with_own(gathered, own, chip):
    full = gathered.reshape((N_CHIPS,) + own.shape)
    return lax.dynamic_update_index_in_dim(full, own, chip, 0)


def _exchange_halves(grads):
    n = len(grads)
    slabs = [list(g) if isinstance(g, (list, tuple)) else [g] for g in grads]
    flat = [a for s in slabs for a in s]
    ncp = len(flat)

    def body(*refs):
        ins, outs = refs[:ncp], refs[ncp:ncp + n]
        send_sems, recv_sems = refs[ncp + n:]
        x, y, c = _position()
        cps, k = [], 0
        for t in range(n):
            for j in range(len(slabs[t])):
                if len(slabs[t]) == 1:
                    r2 = ins[k].shape[1] // 2
                    src, dst = ins[k].at[:, pl.ds((1 - c) * r2, r2), :], outs[t]
                else:
                    r2 = ins[k].shape[0] // 2
                    src, dst = ins[k].at[pl.ds((1 - c) * r2, r2), :], outs[t].at[j]
                cps.append(pltpu.make_async_remote_copy(
                    src_ref=src, dst_ref=dst, send_sem=send_sems.at[k], recv_sem=recv_sems.at[k],
                    device_id=(x, y, 1 - c), device_id_type=MESH))
                k += 1
        for cp in cps:
            cp.start()
        for cp in cps:
            cp.wait()

    def landing(s):
        a = s[0]
        return SDS((N_CHIPS, a.shape[-2] // 2, a.shape[-1]), a.dtype)

    return pl.pallas_call(
        body, out_shape=[landing(s) for s in slabs],
        in_specs=[HBM_SPEC] * ncp, out_specs=[HBM_SPEC] * n,
        scratch_shapes=[pltpu.SemaphoreType.DMA((ncp,)), pltpu.SemaphoreType.DMA((ncp,))],
        compiler_params=pltpu.CompilerParams(has_side_effects=True), name="reduce_sibling")(*flat)


def _scatter_copies(ins, outs, send_sems, recv_sems):
    x, y, c = _position()
    me = 2 * x + y
    peers = [(1 - x, y), (x, 1 - y), (1 - x, 1 - y)]

    def remote(t, p, src_slab, dst_slab):
        px, py = peers[p]
        return pltpu.make_async_remote_copy(
            src_ref=ins[t].at[src_slab], dst_ref=outs[t].at[dst_slab], send_sem=send_sems.at[3 * t + p],
            recv_sem=recv_sems.at[3 * t + p], device_id=(px, py, c), device_id_type=MESH)

    n = len(ins)
    sends = [remote(t, p, 2 * peers[p][0] + peers[p][1], me) for t in range(n) for p in range(3)]
    lands = [remote(t, p, me, 2 * peers[p][0] + peers[p][1]) for t in range(n) for p in range(3)]
    return sends, lands


def _share_halves(halves):
    n = len(halves)

    def body(*refs):
        ins, outs = refs[:n], refs[n:2 * n]
        send_sems, recv_sems = refs[2 * n:]
        x, y, c = _position()

        def copy(t, slab):
            return pltpu.make_async_remote_copy(
                src_ref=ins[t].at[slab], dst_ref=outs[t].at[slab], send_sem=send_sems.at[t], recv_sem=recv_sems.at[t],
                device_id=(x, y, 1 - c), device_id_type=MESH)

        for t in range(n):
            copy(t, c).start()
        for t in range(n):
            copy(t, 1 - c).wait_recv()
        for t in range(n):
            copy(t, c).wait_send()

    return pl.pallas_call(
        body, out_shape=[SDS(a.shape, a.dtype) for a in halves],
        in_specs=[HBM_SPEC] * n, out_specs=[HBM_SPEC] * n,
        scratch_shapes=[pltpu.SemaphoreType.DMA((n,)), pltpu.SemaphoreType.DMA((n,))],
        input_output_aliases={t: t for t in range(n)},
        compiler_params=pltpu.CompilerParams(has_side_effects=True), name="share_sibling")(*halves)


def _add_sibling_slab(grad_j, recv, core, j, sums):
    nch, r2, cols = recv.shape
    tr = _row_block(r2, cols)
    nb = r2 // tr
    fresh = sums is None

    def body(c_ref, g_ref, r_ref, *rest):
        del c_ref
        rest[-1][...] = (g_ref[...].astype(F32) + r_ref[...].astype(F32)).astype(BF16)

    return pl.pallas_call(
        body, out_shape=SDS(recv.shape, BF16),
        grid_spec=pltpu.PrefetchScalarGridSpec(
            num_scalar_prefetch=1, grid=(nb,),
            in_specs=[pl.BlockSpec((tr, cols), lambda i, c_ref: (c_ref[0] * nb + i, 0)),
                      pl.BlockSpec((None, tr, cols), lambda i, c_ref: (j, i, 0))] + ([] if fresh else [HBM_SPEC]),
            out_specs=pl.BlockSpec((None, tr, cols), lambda i, c_ref: (j, i, 0))),
        input_output_aliases={} if fresh else {3: 0},
        compiler_params=_params(("parallel",)), name="add_sibling_slab")(
            *((core, grad_j, recv) + (() if fresh else (sums,))))


def _add_sibling(grad, recv, core):
    if isinstance(grad, (list, tuple)):
        sums = None
        for j, g in enumerate(grad):
            sums = _add_sibling_slab(g, recv, core, j, sums)
        return sums
    nch, r2, cols = recv.shape
    tr = _row_block(r2, cols)
    nb = r2 // tr

    def body(c_ref, g_ref, r_ref, o_ref):
        del c_ref
        o_ref[...] = (g_ref[...].astype(F32) + r_ref[...].astype(F32)).astype(BF16)

    return pl.pallas_call(
        body, out_shape=SDS(recv.shape, BF16),
        grid_spec=pltpu.PrefetchScalarGridSpec(
            num_scalar_prefetch=1, grid=(nch, nb),
            in_specs=[pl.BlockSpec((None, tr, cols), lambda j, i, c_ref: (j, c_ref[0] * nb + i, 0)),
                      pl.BlockSpec((None, tr, cols), lambda j, i, c_ref: (j, i, 0))],
            out_specs=pl.BlockSpec((None, tr, cols), lambda j, i, c_ref: (j, i, 0))),
        compiler_params=_params(("parallel", "parallel")), name="add_sibling")(core, grad, recv)


def _add_chips(own, recv, chip_core):
    nch, r2, cols = recv.shape
    tr = _row_block(r2, cols)

    nsc = 2 + nch

    def body(*refs):
        me = refs[0][0]
        own_ref, p_refs, o_ref = refs[nsc], refs[nsc + 1:nsc + 1 + nch], refs[nsc + 1 + nch]
        acc = None
        for j in range(nch):
            term = jnp.where(me == j, own_ref[...], p_refs[j][...]).astype(F32)
            acc = term if acc is None else acc + term
        o_ref[...] = acc

    def slab(j):
        return pl.BlockSpec((None, tr, cols), lambda i, *sc: (sc[2 + j][0], i, 0))

    return pl.pallas_call(
        body, out_shape=SDS((2, r2, cols), F32),
        grid_spec=pltpu.PrefetchScalarGridSpec(
            num_scalar_prefetch=nsc, grid=(r2 // tr,),
            in_specs=[pl.BlockSpec((None, tr, cols), lambda i, *sc: (sc[0][0], i, 0))] + [slab(j) for j in range(nch)],
            out_specs=pl.BlockSpec((None, tr, cols), lambda i, *sc: (sc[1][0], i, 0))),
        compiler_params=_params(("parallel",)), name="add_chips")(*chip_core, own, *([recv] * nch))


def _allreduce_small(pack):
    rows = pack.shape[0]

    def body(p_ref, o_ref, buf, send_sems, recv_sems):
        x, y, c = _position()
        me = 4 * x + 2 * y + c
        buf[me] = p_ref[...]

        def copy(dst_dev, slot):
            return pltpu.make_async_remote_copy(
                src_ref=p_ref, dst_ref=buf.at[slot], send_sem=send_sems.at[dst_dev], recv_sem=recv_sems.at[slot],
                device_id=(dst_dev // 4, (dst_dev // 2) % 2, dst_dev % 2), device_id_type=MESH)

        for dev in range(N_DEV):
            @pl.when(dev != me)
            def _():
                copy(dev, me).start()
        for dev in range(N_DEV):
            @pl.when(dev != me)
            def _():
                copy(dev, dev).wait_recv()
        for dev in range(N_DEV):
            @pl.when(dev != me)
            def _():
                copy(dev, me).wait_send()
        acc = buf[0]
        for dev in range(1, N_DEV):
            acc = acc + buf[dev]
        o_ref[...] = acc

    return pl.pallas_call(
        body, out_shape=SDS(pack.shape, F32),
        in_specs=[pl.BlockSpec(memory_space=pltpu.VMEM)], out_specs=pl.BlockSpec(memory_space=pltpu.VMEM),
        scratch_shapes=[pltpu.VMEM((N_DEV, rows, LANES), F32), pltpu.SemaphoreType.DMA((N_DEV,)),
                        pltpu.SemaphoreType.DMA((N_DEV,))],
        compiler_params=pltpu.CompilerParams(has_side_effects=True), name="allreduce_small")(pack)


ATTN_TQ = 256


def _local_step(cfg, x, target, w, to_chips=None, late=None, hn=None):
    d = cfg.D
    if hn is None:
        hn = _rmsnorm_fwd(x, w["norm_w"])
    proj = _mm(hn, w["w_main"], "nn", BF16, "proj_main", carry=late[0] if late else None)
    if late:
        proj, arrived = proj
        w = {**w, **late[1](arrived)}
    dt_raw = _mm(hn, w["w_dt"], "nn", F32, "proj_dt")
    slopes = _slopes(cfg.H)
    near = _Pass(ATTN_TQ, DILATED_PATTERNS[:-1], 1, cfg.S)
    far = _Pass(LANES, DILATED_PATTERNS[-1:], DEINT, cfg.S // DEINT)
    tab_near, tab_far = _attn_tables(near), _attn_tables(far)
    cols_near, cols_far = (cfg.OQ, cfg.OK, cfg.OV), (0, d, 2 * d)
    qkv_far = _deinterleave(proj, 0, 3 * d, "attn_deinterleave")
    o_1, lse_1 = _attn_fwd(cfg, near, proj, cols_near, tab_near, slopes, "attn_fwd_near")
    o_2, lse_2 = _attn_fwd(cfg, far, qkv_far, cols_far, tab_far, slopes, "attn_fwd_far")
    o_a, oag, lse = _attn_merge(cfg, proj, o_1, lse_1, o_2, lse_2)
    xact = _conv_fwd(cfg, proj, w["conv_w"], w["conv_b"])
    e = _expansion_matrix(cfg)
    y, y2, states = _ssd_fwd(cfg, xact, dt_raw, proj, w["dt_bias"], w["a_log"], w["d_skip"], w["ssm_norm_w"], e)
    a_br = _mm(oag, w["w_attn"], "nn", BF16, "branch_attn")
    s_br = _mm(y2, w["w_ssm"], "nn", BF16, "branch_ssm")
    merged = _merge_fwd(cfg, proj, a_br, s_br)
    dout_f, dout_b, loss_row, g_fnw = _outproj_loss(merged, w["w_out"], x, target, w["final_norm_w"])

    dmerged = _mm(dout_b, w["w_out"], "nt", BF16, "d_merged")
    g_w_out = _mm(merged, dout_b, "tn", BF16, "g_w_out")
    dproj, da_br = _merge_bwd(cfg, proj, a_br, dmerged, cfg.OGA, None, "merge_bwd_attn")
    dproj, ds_br = _merge_bwd(cfg, proj, s_br, dmerged, cfg.OGS, dproj, "merge_bwd_ssm")
    doag = _mm(da_br, w["w_attn"], "nt", BF16, "d_oag")
    g_w_attn = _mm(oag, da_br, "tn", BF16, "g_w_attn")
    dy2 = _mm(ds_br, w["w_ssm"], "nt", BF16, "d_y2")
    g_w_ssm = _mm(y2, ds_br, "tn", BF16, "g_w_ssm")
    dproj, dxact, ddt, g_snw, g_dtb, g_alog, g_dsk = _ssd_bwd(
        cfg, xact, dt_raw, proj, y, dy2, states, w["dt_bias"], w["a_log"], w["d_skip"], w["ssm_norm_w"], e, dproj)
    dproj, g_cw, g_cb = _conv_bwd(cfg, proj, dxact, w["conv_w"], w["conv_b"], dproj)
    dproj, do, do_far, dl, dl_far, lse_far = _attn_bwd_prep(cfg, proj, o_a, doag, lse, dproj)
    g_near = _attn_bwd(cfg, near, proj, cols_near, do, lse, dl, tab_near, slopes, "attn_bwd_near")
    g_far = _attn_bwd(cfg, far, qkv_far, cols_far, do_far, lse_far, dl_far, tab_far, slopes, "attn_bwd_far")
    for g_1, g_2, col0, nm in zip(g_near, g_far, cols_near, ("attn_dq", "attn_dk", "attn_dv")):
        dproj = _attn_grad_sum(cfg, g_1, g_2, col0, dproj, nm)
    ddt_b = ddt.astype(BF16)
    g_w_main = _mm(hn, dproj, "tn", BF16, "g_w_main")
    g_w_dt = _mm(hn, ddt_b, "tn", BF16, "g_w_dt")
    grads = dict(w_main=g_w_main, w_dt=g_w_dt, conv_w=g_cw, conv_b=g_cb, dt_bias=g_dtb, a_log=g_alog,
                 d_skip=g_dsk, ssm_norm_w=g_snw, w_attn=g_w_attn, w_ssm=g_w_ssm, w_out=g_w_out, final_norm_w=g_fnw)
    sent = to_chips(grads) if to_chips is not None else ()
    dhn = _mm(dproj, w["w_main"], "nt", F32, "d_hn", tk=1024, carry=_scatter_carry(sent) if sent else None)
    landed = ()
    if sent:
        dhn, landed = dhn
    dhn_dt = _mm(ddt_b, w["w_dt"], "nt", F32, "d_hn_dt")
    grad_x, grads["norm_w"] = _rmsnorm_bwd(x, w["norm_w"], dhn, dhn_dt, dout_f)
    return loss_row, grad_x, grads, sent, landed


def _pad_lanes(v):
    return jnp.pad(v, ((0, 0), (0, LANES - v.shape[1])))


def _cut(lo, hi, a, b):
    a, b = max(lo, a), min(hi, b)
    return (a, b) if a < b else None


def _main_from_shards(cfg, shards):
    per = cfg.N_IN // len(shards)
    main, dt = [], []
    for j, sh in enumerate(shards):
        lo, hi = j * per, (j + 1) * per
        for dst, rng in ((main, (0, cfg.OGA)), (dt, (cfg.OGA, cfg.OGA + cfg.NH)), (main, (cfg.OGA + cfg.NH, cfg.N_IN))):
            c = _cut(lo, hi, *rng)
            if c is not None:
                dst.append(sh[:, c[0] - lo:c[1] - lo])
    return jnp.concatenate(main, axis=1), _pad_lanes(jnp.concatenate(dt, axis=1))


def _shards_from_main(cfg, g_main, g_dt, n):
    per = cfg.N_IN // n
    out = []
    for j in range(n):
        lo, hi = j * per, (j + 1) * per
        parts = []
        for src, off, rng in ((g_main, 0, (0, cfg.OGA)), (g_dt, cfg.OGA, (cfg.OGA, cfg.OGA + cfg.NH)),
                              (g_main, cfg.NH, (cfg.OGA + cfg.NH, cfg.N_IN))):
            c = _cut(lo, hi, *rng)
            if c is not None:
                parts.append(src[:, c[0] - off:c[1] - off])
        out.append(jnp.concatenate(parts, axis=1) if len(parts) > 1 else parts[0])
    return out


def _full_weights(cfg, norm_w, w_in_shards, conv_w, conv_b, dt_bias, a_log, d_skip, ssm_norm_w, w_attn, w_ssm, w_out, fnw):
    w_main, w_dt = _main_from_shards(cfg, w_in_shards)
    return dict(norm_w=norm_w, w_main=w_main.astype(BF16), w_dt=w_dt.astype(BF16), conv_w=conv_w, conv_b=conv_b,
                dt_bias=_pad_lanes(dt_bias), a_log=_pad_lanes(a_log), d_skip=_pad_lanes(d_skip), ssm_norm_w=ssm_norm_w,
                final_norm_w=fnw, **{k: v.astype(BF16) for k, v in (("w_attn", w_attn), ("w_ssm", w_ssm), ("w_out", w_out))
                                     if v is not None})


def kernel(x, norm_w, w_in, conv_w, conv_b, dt_bias, a_log, d_skip, ssm_norm_w, w_attn_branch, w_ssm_branch, w_out, final_norm_w, loss_target, m_norm_w, m_w_in, m_conv_w, m_conv_b, m_dt_bias, m_a_log, m_d_skip, m_ssm_norm_w, m_w_attn_branch, m_w_ssm_branch, m_w_out, m_final_norm_w, v_norm_w, v_w_in, v_conv_w, v_conv_b, v_dt_bias, v_a_log, v_d_skip, v_ssm_norm_w, v_w_attn_branch, v_w_ssm_branch, v_w_out, v_final_norm_w):
    cfg = _Cfg(x.shape[1], x.shape[2])
    d, si, cd, nh = cfg.D, cfg.SI, cfg.CD, cfg.NH
    chip = 2 * lax.axis_index("x") + lax.axis_index("y")
    core = lax.axis_index("c").astype(jnp.int32).reshape(1)
    chip = chip.astype(jnp.int32)
    chip_core = [chip.reshape(1), core] + [jnp.where(chip == j, (j + 1) % N_CHIPS, j).astype(jnp.int32).reshape(1)
                                           for j in range(N_CHIPS)]

    own = [w_in[0].astype(BF16), conv_w[0].reshape(4 * CONV_K, -1)]
    hn, gathered = _rmsnorm_fwd(x[0], norm_w, carry=_gather_carry(own))
    a_in, a_cw = [_with_own(g, o, chip) for g, o in zip(gathered, own)]
    conv_w_full = a_cw.reshape(N_CHIPS, CONV_K, cd // N_CHIPS).transpose(1, 0, 2).reshape(CONV_K, cd)
    w = _full_weights(cfg, norm_w, [a_in[j] for j in range(N_CHIPS)], conv_w_full, conv_b, dt_bias, a_log, d_skip,
                      ssm_norm_w, None, None, None, final_norm_w.reshape(1, d))
    own_late = [w_attn_branch[0].astype(BF16), w_ssm_branch[0].astype(BF16), w_out[0].astype(BF16)]

    def late_weights(arrived):
        a_attn, a_ssm, a_out = [_with_own(g, o, chip) for g, o in zip(arrived, own_late)]
        return dict(w_attn=a_attn.reshape(d, d), w_ssm=a_ssm.reshape(si, d), w_out=a_out.reshape(d, d))

    def to_chips(grads):
        by_chip = [_shards_from_main(cfg, grads["w_main"], grads["w_dt"], N_CHIPS),
                   grads["w_attn"].reshape(N_CHIPS, d // N_CHIPS, d),
                   grads["w_ssm"].reshape(N_CHIPS, si // N_CHIPS, d),
                   grads["w_out"].reshape(N_CHIPS, d // N_CHIPS, d)]
        from_sibling = _exchange_halves(by_chip)
        return [_add_sibling(g, r, core) for g, r in zip(by_chip, from_sibling)]

    loss_row, grad_x, grads, chip_sums, from_chips = _local_step(
        cfg, x[0], loss_target[0], w, to_chips, (_gather_carry(own_late), late_weights), hn)
    halves = [_add_chips(o, p, chip_core) for o, p in zip(chip_sums, from_chips)]
    g_in, g_attn, g_ssm, g_out = [h.reshape(2 * h.shape[1], h.shape[2]) for h in _share_halves(halves)]

    small = [loss_row, grads["norm_w"], grads["conv_b"], grads["dt_bias"], grads["a_log"], grads["d_skip"],
             grads["ssm_norm_w"], grads["final_norm_w"], grads["conv_w"].reshape(1, CONV_K * cd)]
    sizes = [a.shape[1] for a in small]
    total = sum(sizes)
    rows = -(-total // (8 * LANES)) * 8
    flat = jnp.pad(jnp.concatenate(small, axis=1), ((0, 0), (0, rows * LANES - total)))
    red = _allreduce_small(flat.reshape(rows, LANES)).reshape(1, rows * LANES)
    offs = [sum(sizes[:i]) for i in range(len(sizes))]
    loss_r, g_nw, g_cb, g_dtb, g_alog, g_dsk, g_snw, g_fnw, g_cw_flat = [
        red[:, o:o + n] for o, n in zip(offs, sizes)]
    loss = loss_r[0, 0]
    g_dtb, g_alog, g_dsk = g_dtb[:, :nh], g_alog[:, :nh], g_dsk[:, :nh]
    cshard = cd // N_CHIPS
    g_cw = lax.dynamic_slice_in_dim(g_cw_flat.reshape(CONV_K, cd), chip * cshard, cshard, axis=1)

    upd = {}
    for name, wv, gv, mv, vv in [("w_in", w_in[0], g_in, m_w_in[0], v_w_in[0]),
                                 ("w_attn", w_attn_branch[0], g_attn, m_w_attn_branch[0], v_w_attn_branch[0]),
                                 ("w_ssm", w_ssm_branch[0], g_ssm, m_w_ssm_branch[0], v_w_ssm_branch[0]),
                                 ("w_out", w_out[0], g_out, m_w_out[0], v_w_out[0])]:
        upd[name] = _adamw(wv, gv, mv, vv, "adamw_" + name)
    names = ["norm_w", "conv_w", "conv_b", "dt_bias", "a_log", "d_skip", "ssm_norm_w", "final_norm_w"]
    ws = [norm_w, conv_w[0].reshape(1, -1), conv_b, dt_bias, a_log, d_skip, ssm_norm_w, final_norm_w.reshape(1, d)]
    gs = [g_nw, g_cw.reshape(1, -1), g_cb, g_dtb, g_alog, g_dsk, g_snw, g_fnw]
    ms = [m_norm_w, m_conv_w[0].reshape(1, -1), m_conv_b, m_dt_bias, m_a_log, m_d_skip, m_ssm_norm_w,
          m_final_norm_w.reshape(1, d)]
    vs = [v_norm_w, v_conv_w[0].reshape(1, -1), v_conv_b, v_dt_bias, v_a_log, v_d_skip, v_ssm_norm_w,
          v_final_norm_w.reshape(1, d)]
    ssz = [a.shape[1] for a in ws]
    stot = sum(ssz)
    srows = -(-stot // (8 * LANES)) * 8

    def pack(parts):
        return jnp.pad(jnp.concatenate(parts, axis=1), ((0, 0), (0, srows * LANES - stot))).reshape(srows, LANES)

    packed = _adamw(pack(ws), pack(gs), pack(ms), pack(vs), "adamw_small")
    soffs = [sum(ssz[:i]) for i in range(len(ssz))]
    for k, nm in enumerate(names):
        upd[nm] = tuple(p.reshape(1, srows * LANES)[:, soffs[k]:soffs[k] + ssz[k]] for p in packed)

    shapes = dict(norm_w=norm_w.shape, w_in=w_in.shape, conv_w=conv_w.shape, conv_b=conv_b.shape, dt_bias=dt_bias.shape,
                  a_log=a_log.shape, d_skip=d_skip.shape, ssm_norm_w=ssm_norm_w.shape, w_attn=w_attn_branch.shape,
                  w_ssm=w_ssm_branch.shape, w_out=w_out.shape, final_norm_w=final_norm_w.shape)
    order = ["norm_w", "w_in", "conv_w", "conv_b", "dt_bias", "a_log", "d_skip", "ssm_norm_w", "w_attn", "w_ssm",
             "w_out", "final_norm_w"]
    gradv = dict(norm_w=g_nw, w_in=g_in, conv_w=g_cw, conv_b=g_cb, dt_bias=g_dtb, a_log=g_alog, d_skip=g_dsk,
                 ssm_norm_w=g_snw, w_attn=g_attn, w_ssm=g_ssm, w_out=g_out, final_norm_w=g_fnw)
    outs = [loss, grad_x[None]]
    outs += [gradv[n].reshape(shapes[n]) for n in order]
    for k in range(3):
        outs += [upd[n][k].reshape(shapes[n]) for n in order]
    return tuple(outs)
```

```python
import jax
import jax.numpy as jnp
from jax import lax
from jax.experimental import pallas as pl
from jax.experimental.pallas import tpu as pltpu

F32 = jnp.float32
BF16 = jnp.bfloat16
SDS = jax.ShapeDtypeStruct

RMS_EPS = 1e-6
LANES = 128
CHUNK = 128
SSM_HEAD_DIM = 64
SSM_GROUPS = 8
SSM_STATE = 128
CONV_K = 4
ATTN_HEAD_DIM = 128
DILATED_PATTERNS = ((128, 1), (512, 4), (2048, 16))
NEG = -1e30
VMEM_LIMIT = 56 * 1024 * 1024
ADAM_LR, ADAM_B1, ADAM_B2, ADAM_EPS, ADAM_WD, ADAM_STEP = 0.001, 0.9, 0.999, 1e-08, 0.01, 10
MESH = pl.DeviceIdType.MESH
N_CHIPS = 4
N_DEV = 8


class _Cfg:
    def __init__(self, s, d):
        self.S, self.D = s, d
        self.H = d // ATTN_HEAD_DIM
        self.SI = 2 * d
        self.NH = self.SI // SSM_HEAD_DIM
        self.HPG = self.NH // SSM_GROUPS
        self.GW = self.HPG * SSM_HEAD_DIM
        self.BC = SSM_GROUPS * SSM_STATE
        self.CD = self.SI + 2 * self.BC
        self.OQ, self.OK, self.OV, self.OZA = 0, d, 2 * d, 3 * d
        self.OZS = 4 * d
        self.OXBC = self.OZS + self.SI
        self.OGA = self.OXBC + self.CD
        self.OGS = self.OGA + d
        self.NM = self.OGS + d
        self.N_IN = self.NM + self.NH
        assert self.GW % LANES == 0 and self.NH <= LANES and s % 512 == 0 and d % 512 == 0


def _params(sem=None):
    return pltpu.CompilerParams(dimension_semantics=sem, vmem_limit_bytes=VMEM_LIMIT)


def _sigmoid(x):
    return 0.5 * jnp.tanh(0.5 * x) + 0.5


def _softplus(x):
    u = jnp.exp(-jnp.abs(x))
    l1p = jnp.where(u < 1e-3, u * (1.0 - u * (0.5 - u * (1.0 / 3.0))), jnp.log(1.0 + u))
    return jnp.maximum(x, 0.0) + l1p


def _nt(a, b):
    return lax.dot_general(a, b, (((1,), (1,)), ((), ())), preferred_element_type=F32)


def _tn(a, b):
    return lax.dot_general(a, b, (((0,), (0,)), ((), ())), preferred_element_type=F32)


def _nn(a, b):
    return jnp.dot(a, b, preferred_element_type=F32)


def _tile(n, target):
    if n <= target:
        return n
    best = None
    for t in range(LANES, target + 1, LANES):
        if n % t == 0:
            best = t
    assert best is not None, (n, target)
    return best


MM_TK = {"nn": 2048, "nt": 2048, "tn": 1024}


def _mm(a, b, dims, out_dtype, name, tm=1024, tn=2048, tk=None, init=None, carry=None):
    tk = MM_TK[dims] if tk is None else tk
    if dims == "nn":
        (m, k), (k2, n) = a.shape, b.shape
    elif dims == "nt":
        (m, k), (n, k2) = a.shape, b.shape
    else:
        (k, m), (k2, n) = a.shape, b.shape
    assert k == k2
    tm, tn, tk = _tile(m, tm), _tile(n, tn), _tile(k, tk)
    nk = k // tk
    if dims == "tn":
        a_spec = pl.BlockSpec((tk, tm), lambda i, j, kk: (kk, i))
    else:
        a_spec = pl.BlockSpec((tm, tk), lambda i, j, kk: (i, kk))
    if dims == "nt":
        b_spec = pl.BlockSpec((tn, tk), lambda i, j, kk: (j, kk))
    else:
        b_spec = pl.BlockSpec((tk, tn), lambda i, j, kk: (kk, j))
    o_spec = pl.BlockSpec((tm, tn), lambda i, j, kk: (i, j))
    op = {"nn": _nn, "nt": _nt, "tn": _tn}[dims]
    has_init = init is not None
    nx = len(carry.arrays) if carry is not None else 0
    ni, nj = m // tm, n // tn

    def body(*refs):
        a_ref, b_ref = refs[0], refs[1]
        i_ref = refs[2] if has_init else None
        x_in = refs[2 + has_init:2 + has_init + nx]
        o_ref = refs[2 + has_init + nx]
        x_out = refs[3 + has_init + nx:3 + has_init + 2 * nx]
        acc = refs[3 + has_init + 2 * nx]
        x_sems = refs[4 + has_init + 2 * nx:]
        i, j, kk = pl.program_id(0), pl.program_id(1), pl.program_id(2)

        if nx:
            @pl.when((i == 0) & (j == 0) & (kk == 0))
            def _():
                carry.start(x_in, x_out, x_sems)

        prod = lambda: op(a_ref[...], b_ref[...])
        with_init = (lambda p: p + i_ref[...].astype(F32)) if has_init else (lambda p: p)
        if nk == 1:
            o_ref[...] = with_init(prod()).astype(out_dtype)
        else:
            @pl.when(kk == 0)
            def _():
                acc[...] = with_init(prod())

            @pl.when((kk > 0) & (kk < nk - 1))
            def _():
                acc[...] += prod()

            @pl.when(kk == nk - 1)
            def _():
                o_ref[...] = (acc[...] + prod()).astype(out_dtype)

        if nx:
            @pl.when((i == ni - 1) & (j == nj - 1) & (kk == nk - 1))
            def _():
                carry.finish(x_in, x_out, x_sems)

    in_specs = [a_spec, b_spec] + ([o_spec] if has_init else []) + [HBM_SPEC] * nx
    args = (a, b) + ((init,) if has_init else ()) + (tuple(carry.arrays) if nx else ())
    sems = carry.sem_shapes() if nx else []
    outs = pl.pallas_call(
        body, out_shape=[SDS((m, n), out_dtype)] + (carry.out_shapes if nx else []), grid=(ni, nj, nk),
        in_specs=in_specs, out_specs=[o_spec] + [HBM_SPEC] * nx,
        scratch_shapes=[pltpu.VMEM((tm, tn) if nk > 1 else (8, LANES), F32)] + sems,
        compiler_params=_params(("arbitrary",) * 3 if nx else ("parallel", "parallel", "arbitrary")), name=name)(*args)
    return (outs[0], outs[1:]) if nx else outs[0]


def _rmsnorm_fwd(x, w, carry=None):
    s, d = x.shape
    tr = 256
    nsteps = s // tr
    nx = len(carry.arrays) if carry is not None else 0

    def body(*refs):
        x_ref, w_ref, x_in = refs[0], refs[1], refs[2:2 + nx]
        o_ref, x_out, x_sems = refs[2 + nx], refs[3 + nx:3 + 2 * nx], refs[3 + 2 * nx:]
        if nx:
            @pl.when(pl.program_id(0) == 0)
            def _():
                carry.start(x_in, x_out, x_sems)

        xv = x_ref[...]
        r = lax.rsqrt(jnp.mean(xv * xv, axis=-1, keepdims=True) + RMS_EPS)
        o_ref[...] = (xv * r * w_ref[...]).astype(BF16)

        if nx:
            @pl.when(pl.program_id(0) == nsteps - 1)
            def _():
                carry.finish(x_in, x_out, x_sems)

    outs = pl.pallas_call(
        body, out_shape=[SDS((s, d), BF16)] + (carry.out_shapes if nx else []), grid=(nsteps,),
        in_specs=[pl.BlockSpec((tr, d), lambda i: (i, 0)), pl.BlockSpec((1, d), lambda i: (0, 0))] + [HBM_SPEC] * nx,
        out_specs=[pl.BlockSpec((tr, d), lambda i: (i, 0))] + [HBM_SPEC] * nx,
        scratch_shapes=carry.sem_shapes() if nx else [],
        compiler_params=_params(("arbitrary",) if nx else ("parallel",)), name="rmsnorm_fwd")(
            x, w, *(carry.arrays if nx else []))
    return (outs[0], outs[1:]) if nx else outs[0]


def _rmsnorm_bwd(x, w, dhn_a, dhn_b, dout):
    s, d = x.shape
    tr = 256

    def body(x_ref, w_ref, dh_ref, dh2_ref, do_ref, gx_ref, gw_ref):
        xv = x_ref[...]
        r = lax.rsqrt(jnp.mean(xv * xv, axis=-1, keepdims=True) + RMS_EPS)
        nrm = xv * r
        dh = dh_ref[...] + dh2_ref[...]
        gy = dh * w_ref[...]
        gx_ref[...] = do_ref[...] + r * (gy - nrm * jnp.mean(gy * nrm, axis=-1, keepdims=True))

        @pl.when(pl.program_id(0) == 0)
        def _():
            gw_ref[...] = jnp.zeros_like(gw_ref)

        gw_ref[...] += jnp.sum(dh * nrm, axis=0, keepdims=True)

    blk = pl.BlockSpec((tr, d), lambda i: (i, 0))
    row = pl.BlockSpec((1, d), lambda i: (0, 0))
    return pl.pallas_call(
        body, out_shape=(SDS((s, d), F32), SDS((1, d), F32)), grid=(s // tr,),
        in_specs=[blk, row, blk, blk, blk], out_specs=(blk, row),
        compiler_params=_params(("arbitrary",)), name="rmsnorm_bwd")(x, w, dhn_a, dhn_b, dout)


DEINT = DILATED_PATTERNS[-1][1]
DEINT_ROWS = DEINT * LANES


class _Pass:
    def __init__(self, tq, patterns, unit, seg_len):
        self.tq, self.patterns, self.unit, self.seg_len = tq, patterns, unit, seg_len
        self.win = max(w for w, _ in patterns) // unit
        self.w = self.win + tq
        assert self.win % tq == 0


def _attn_tables(ps):
    i = jnp.arange(ps.tq, dtype=jnp.int32)[:, None]
    j = jnp.arange(ps.w, dtype=jnp.int32)[None, :]
    delta = (i + ps.win - j) * ps.unit
    n = jnp.zeros((ps.tq, ps.w), F32)
    for window, dil in ps.patterns:
        n = n + ((delta >= 0) & (delta <= window) & (delta % dil == 0)).astype(F32)
    logn = jnp.where(n > 0, jnp.log(jnp.maximum(n, 1.0)), NEG)
    return logn, jnp.maximum(delta, 0).astype(F32)


def _slopes(h):
    s = jnp.asarray([2.0 ** (-8.0 * (i + 1) / h) for i in range(h)], F32)
    return jnp.broadcast_to(s[:, None, None], (h, 1, LANES))


def _masked_logn(ps, logn_ref, start):
    col = lax.broadcasted_iota(jnp.int32, (ps.tq, ps.w), 1)
    return jnp.where(col >= ps.win - lax.rem(start, ps.seg_len), logn_ref[...], NEG)


def _head_cols(hh):
    return slice(hh * ATTN_HEAD_DIM, (hh + 1) * ATTN_HEAD_DIM)


def _head_window(refs, cs):
    return jnp.concatenate([r[:, cs] for r in refs], axis=0)


def _head_scores(q_ref, kw, cs, base, dist_ref, slope_ref, hh):
    return _nt(q_ref[:, cs], kw) * (ATTN_HEAD_DIM ** -0.5) + (base - slope_ref[hh][0:1, 0:1] * dist_ref[...])


def _lane_of(stat, hh):
    lane = lax.broadcasted_iota(jnp.int32, stat.shape, 1)
    return jnp.sum(jnp.where(lane == hh, stat, 0.0), axis=1, keepdims=True)


def _window_specs(ps, d, col, nb):
    nprev = ps.win // ps.tq
    return [pl.BlockSpec((ps.tq, d), lambda i, b=b: (jnp.maximum(jnp.minimum(i, nb - 1) - (nprev - b), 0), col))
            for b in range(nprev + 1)]


def _attn_fwd(cfg, ps, qkv, cols, tables, slopes, name):
    s, h, d = cfg.S, cfg.H, cfg.D
    tq, nw = ps.tq, ps.win // ps.tq + 1
    nb = s // tq
    logn, dist = tables
    qc, kc, vc = [c // d for c in cols]

    def body(*refs):
        q_ref, k_refs, v_refs = refs[0], refs[1:1 + nw], refs[1 + nw:1 + 2 * nw]
        logn_ref, dist_ref, slope_ref, o_ref, lse_ref = refs[1 + 2 * nw:]
        base = _masked_logn(ps, logn_ref, pl.program_id(0) * tq)
        lane = lax.broadcasted_iota(jnp.int32, (tq, LANES), 1)

        lse = jnp.zeros((tq, LANES), F32)
        for hh in range(h):
            cs = _head_cols(hh)
            sc = _head_scores(q_ref, _head_window(k_refs, cs), cs, base, dist_ref, slope_ref, hh)
            m = jnp.max(sc, axis=1, keepdims=True)
            p = jnp.exp(sc - m)
            l = jnp.sum(p, axis=1, keepdims=True)
            o_ref[:, cs] = (_nn(p.astype(BF16), _head_window(v_refs, cs)) / l).astype(BF16)
            lse = jnp.where(lane == hh, m + jnp.log(l), lse)
        lse_ref[...] = lse

    tab = pl.BlockSpec((tq, ps.w), lambda i: (0, 0))
    return pl.pallas_call(
        body, out_shape=(SDS((s, d), BF16), SDS((s, LANES), F32)), grid=(nb,),
        in_specs=[pl.BlockSpec((tq, d), lambda i: (i, qc))] + _window_specs(ps, d, kc, nb) + _window_specs(ps, d, vc, nb)
        + [tab, tab, pl.BlockSpec((h, 1, LANES), lambda i: (0, 0, 0))],
        out_specs=(pl.BlockSpec((tq, d), lambda i: (i, 0)), pl.BlockSpec((tq, LANES), lambda i: (i, 0))),
        compiler_params=_params(("parallel",)), name=name)(*([qkv] * (1 + 2 * nw)), logn, dist, slopes)


def _attn_bwd(cfg, ps, qkv, cols, do, lse, delta, tables, slopes, name):
    s, h, d = cfg.S, cfg.H, cfg.D
    tq, nprev = ps.tq, ps.win // ps.tq
    nw = nprev + 1
    nb = s // tq
    logn, dist = tables
    qc, kc, vc = [c // d for c in cols]
    scale = ATTN_HEAD_DIM ** -0.5

    def body(*refs):
        q_ref, k_refs, v_refs = refs[0], refs[1:1 + nw], refs[1 + nw:1 + 2 * nw]
        do_ref, lse_ref, dl_ref, logn_ref, dist_ref, slope_ref, dq_ref, dk_ref, dv_ref, ck, cv = refs[1 + 2 * nw:]
        i = pl.program_id(0)
        slot = lambda b: lax.rem(i + b, nprev)

        @pl.when(i == 0)
        def _():
            ck[...] = jnp.zeros_like(ck)
            cv[...] = jnp.zeros_like(cv)

        @pl.when(i < nb)
        def _():
            base = _masked_logn(ps, logn_ref, i * tq)
            lse_all, dl_all = lse_ref[...], dl_ref[...]

            for hh in range(h):
                cs = _head_cols(hh)
                kw, vw = _head_window(k_refs, cs), _head_window(v_refs, cs)
                sc = _head_scores(q_ref, kw, cs, base, dist_ref, slope_ref, hh)
                p = jnp.exp(sc - lse_all[:, hh:hh + 1])
                dob = do_ref[:, cs]
                ds = (p * (_nt(dob, vw) - dl_all[:, hh:hh + 1]) * scale).astype(BF16)
                dq_ref[:, cs] = _nn(ds, kw).astype(BF16)
                dkw = _tn(ds, q_ref[:, cs])
                dvw = _tn(p.astype(BF16), dob)
                dk_ref[:, cs] = ck[slot(0), :, cs] + dkw[0:tq]
                dv_ref[:, cs] = cv[slot(0), :, cs] + dvw[0:tq]
                for b in range(1, nprev):
                    ck[slot(b), :, cs] += dkw[b * tq:(b + 1) * tq]
                    cv[slot(b), :, cs] += dvw[b * tq:(b + 1) * tq]
                ck[slot(0), :, cs] = dkw[nprev * tq:]
                cv[slot(0), :, cs] = dvw[nprev * tq:]

        @pl.when(i >= nb)
        def _():
            dk_ref[...] = ck[slot(0)]
            dv_ref[...] = cv[slot(0)]

    here = lambda i: jnp.minimum(i, nb - 1)
    blk = pl.BlockSpec((tq, d), lambda i: (here(i), 0))
    stat = pl.BlockSpec((tq, LANES), lambda i: (here(i), 0))
    late = pl.BlockSpec((tq, d), lambda i: (jnp.maximum(i - nprev, 0), 0))
    tab = pl.BlockSpec((tq, ps.w), lambda i: (0, 0))
    return pl.pallas_call(
        body, out_shape=(SDS((s, d), BF16), SDS((s, d), F32), SDS((s, d), F32)), grid=(nb + nprev,),
        in_specs=[pl.BlockSpec((tq, d), lambda i: (here(i), qc))] + _window_specs(ps, d, kc, nb)
        + _window_specs(ps, d, vc, nb) + [blk, stat, stat, tab, tab, pl.BlockSpec((h, 1, LANES), lambda i: (0, 0, 0))],
        out_specs=(blk, late, late),
        scratch_shapes=[pltpu.VMEM((nprev, tq, d), F32), pltpu.VMEM((nprev, tq, d), F32)],
        compiler_params=_params(("arbitrary",)), name=name)(
            *([qkv] * (1 + 2 * nw)), do, lse, delta, logn, dist, slopes)


def _by_residue(a):
    return a.reshape(DEINT, a.shape[0] // DEINT, a.shape[1])


def _deint_spec(colblock):
    return pl.BlockSpec((DEINT, LANES, LANES), lambda b, j: (0, b, colblock(j)))


def _deint_rows(scr, out_ref, dtype):
    for r in range(DEINT):
        out_ref[r] = scr[pl.ds(r, LANES, stride=DEINT), :].astype(dtype)


def _int_rows(in_ref, scr):
    for r in range(DEINT):
        scr[pl.ds(r, LANES, stride=DEINT), :] = in_ref[r].astype(F32)


WIDE = 4 * LANES


def _wide_spec():
    return pl.BlockSpec((DEINT, LANES, WIDE), lambda b, j: (0, b, j))


def _deinterleave(x, col0, ncols, name):
    s = x.shape[0]
    c0 = col0 // WIDE

    def body(x_ref, o_ref, scr):
        for t in range(WIDE // LANES):
            cs = slice(t * LANES, (t + 1) * LANES)
            scr[t] = x_ref[:, cs].astype(F32)
            for r in range(DEINT):
                o_ref[r, :, cs] = scr.at[t][pl.ds(r, LANES, stride=DEINT), :].astype(x.dtype)

    out = pl.pallas_call(
        body, out_shape=SDS((DEINT, s // DEINT, ncols), x.dtype), grid=(s // DEINT_ROWS, ncols // WIDE),
        in_specs=[pl.BlockSpec((DEINT_ROWS, WIDE), lambda b, j: (b, c0 + j))],
        out_specs=_wide_spec(),
        scratch_shapes=[pltpu.VMEM((WIDE // LANES, DEINT_ROWS, LANES), F32)],
        compiler_params=_params(("parallel", "parallel")), name=name)(x)
    return out.reshape(s, ncols)


def _attn_merge(cfg, proj, o_1, lse_1, o_2, lse_2):
    s, h = cfg.S, cfg.H
    zb = cfg.OZA // WIDE
    rows = DEINT_ROWS
    hps = WIDE // LANES

    def body(o1_ref, l1_ref, o2_ref, l2_ref, z_ref, o_ref, og_ref, lse_ref, so, sl):
        j = pl.program_id(1)

        @pl.when(j == 0)
        def _():
            _int_rows(l2_ref, sl)
            lse_ref[...] = jnp.zeros_like(lse_ref)

        l1_all, l2_all = l1_ref[...], sl[...]
        lane = lax.broadcasted_iota(jnp.int32, (rows, LANES), 1)
        lse = lse_ref[...]
        for t in range(hps):
            hh = j * hps + t
            cs = slice(t * LANES, (t + 1) * LANES)
            for r in range(DEINT):
                so.at[t][pl.ds(r, LANES, stride=DEINT), :] = o2_ref[r, :, cs].astype(F32)
            l1, l2 = _lane_of(l1_all, hh), _lane_of(l2_all, hh)
            mx = jnp.maximum(l1, l2)
            w1, w2 = jnp.exp(l1 - mx), jnp.exp(l2 - mx)
            den = w1 + w2
            o = (w1 * o1_ref[:, cs].astype(F32) + w2 * so[t]) / den
            z = z_ref[:, cs].astype(F32)
            o_ref[:, cs] = o.astype(BF16)
            og_ref[:, cs] = (o * (z * _sigmoid(z))).astype(BF16)
            lse = jnp.where(lane == hh, mx + jnp.log(den), lse)
        lse_ref[...] = lse

    blk = pl.BlockSpec((rows, WIDE), lambda b, j: (b, j))
    stat = pl.BlockSpec((rows, LANES), lambda b, j: (b, 0))
    return pl.pallas_call(
        body, out_shape=(SDS((s, cfg.D), BF16), SDS((s, cfg.D), BF16), SDS((s, LANES), F32)),
        grid=(s // rows, h // hps),
        in_specs=[blk, stat, _wide_spec(), _deint_spec(lambda j: 0), pl.BlockSpec((rows, WIDE), lambda b, j: (b, zb + j))],
        out_specs=(blk, blk, stat),
        scratch_shapes=[pltpu.VMEM((hps, rows, LANES), F32), pltpu.VMEM((rows, LANES), F32)],
        compiler_params=_params(("parallel", "arbitrary")), name="attn_merge")(
            o_1, lse_1, _by_residue(o_2), _by_residue(lse_2), proj)


def _attn_bwd_prep(cfg, proj, o_a, doag, lse, dproj):
    s, h = cfg.S, cfg.H
    zb = cfg.OZA // WIDE
    rows = DEINT_ROWS
    hps = WIDE // LANES

    def body(o_ref, dg_ref, z_ref, lse_ref, dp_in, dz_ref, do_ref, do2_ref, dl_ref, dl2_ref, lse2_ref, scr):
        del dp_in
        j = pl.program_id(1)

        @pl.when(j == 0)
        def _():
            dl_ref[...] = jnp.zeros_like(dl_ref)

        lane = lax.broadcasted_iota(jnp.int32, (rows, LANES), 1)
        dl = dl_ref[...]
        for t in range(hps):
            cs = slice(t * LANES, (t + 1) * LANES)
            z = z_ref[:, cs].astype(F32)
            sg = _sigmoid(z)
            o = o_ref[:, cs].astype(F32)
            dg = dg_ref[:, cs].astype(F32)
            do = dg * (z * sg)
            dz_ref[:, cs] = (dg * o * (sg * (1.0 + z * (1.0 - sg)))).astype(BF16)
            do_ref[:, cs] = do.astype(BF16)
            scr[...] = do
            for r in range(DEINT):
                do2_ref[r, :, cs] = scr[pl.ds(r, LANES, stride=DEINT), :].astype(BF16)
            dl = jnp.where(lane == j * hps + t, jnp.sum(do * o, axis=1, keepdims=True), dl)
        dl_ref[...] = dl

        @pl.when(j == h // hps - 1)
        def _():
            scr[...] = dl
            _deint_rows(scr, dl2_ref, F32)
            scr[...] = lse_ref[...]
            _deint_rows(scr, lse2_ref, F32)

    blk = pl.BlockSpec((rows, WIDE), lambda b, j: (b, j))
    stat = pl.BlockSpec((rows, LANES), lambda b, j: (b, 0))
    stat2 = _deint_spec(lambda j: 0)
    outs = pl.pallas_call(
        body,
        out_shape=(SDS(dproj.shape, BF16), SDS((s, cfg.D), BF16), SDS((DEINT, s // DEINT, cfg.D), BF16),
                   SDS((s, LANES), F32), SDS((DEINT, s // DEINT, LANES), F32), SDS((DEINT, s // DEINT, LANES), F32)),
        grid=(s // rows, h // hps),
        in_specs=[blk, blk, pl.BlockSpec((rows, WIDE), lambda b, j: (b, zb + j)), stat, HBM_SPEC],
        out_specs=(pl.BlockSpec((rows, WIDE), lambda b, j: (b, zb + j)), blk, _wide_spec(), stat, stat2, stat2),
        scratch_shapes=[pltpu.VMEM((rows, LANES), F32)],
        input_output_aliases={4: 0},
        compiler_params=_params(("parallel", "arbitrary")), name="attn_bwd_prep")(o_a, doag, proj, lse, dproj)
    dproj, do, do2, dl, dl2, lse2 = outs
    return dproj, do, do2.reshape(s, cfg.D), dl, dl2.reshape(s, LANES), lse2.reshape(s, LANES)


def _attn_grad_sum(cfg, g_1, g_2, col0, dproj, name):
    s = cfg.S
    c0 = col0 // WIDE
    rows = DEINT_ROWS

    def body(g1_ref, g2_ref, dp_in, o_ref, scr):
        del dp_in
        for t in range(WIDE // LANES):
            cs = slice(t * LANES, (t + 1) * LANES)
            for r in range(DEINT):
                scr.at[t][pl.ds(r, LANES, stride=DEINT), :] = g2_ref[r, :, cs].astype(F32)
            o_ref[:, cs] = (g1_ref[:, cs].astype(F32) + scr[t]).astype(BF16)

    return pl.pallas_call(
        body, out_shape=SDS(dproj.shape, BF16), grid=(s // rows, cfg.D // WIDE),
        in_specs=[pl.BlockSpec((rows, WIDE), lambda b, j: (b, j)), _wide_spec(), HBM_SPEC],
        out_specs=pl.BlockSpec((rows, WIDE), lambda b, j: (b, c0 + j)),
        scratch_shapes=[pltpu.VMEM((WIDE // LANES, rows, LANES), F32)],
        input_output_aliases={2: 0},
        compiler_params=_params(("parallel", "parallel")), name=name)(g_1, _by_residue(g_2), dproj)


CONV_HALO = 16
CONV_TR = 512
CONV_CW = 1024


def _rows_back(a, n):
    return a if n == 0 else pltpu.roll(a, n % a.shape[0], axis=0)


def _conv_fwd(cfg, proj, conv_w, conv_b):
    s, cd = cfg.S, cfg.CD
    tr, cw, hl = CONV_TR, CONV_CW, CONV_HALO
    cb0 = cfg.OXBC // cw

    def body(x_ref, h_ref, w_ref, b_ref, o_ref):
        i = pl.program_id(0)
        halo = jnp.where(i > 0, h_ref[...].astype(F32), 0.0)
        ext = jnp.concatenate([halo, x_ref[...].astype(F32)], axis=0)
        pre = b_ref[...] + jnp.zeros((tr, cw), F32)
        for k in range(CONV_K):
            pre = pre + w_ref[k:k + 1, :] * _rows_back(ext, CONV_K - 1 - k)[hl:]
        o_ref[...] = (pre * _sigmoid(pre)).astype(BF16)

    return pl.pallas_call(
        body, out_shape=SDS((s, cd), BF16), grid=(s // tr, cd // cw),
        in_specs=[pl.BlockSpec((tr, cw), lambda i, j: (i, cb0 + j)),
                  pl.BlockSpec((hl, cw), lambda i, j: (jnp.maximum(i * (tr // hl) - 1, 0), cb0 + j)),
                  pl.BlockSpec((CONV_K, cw), lambda i, j: (0, j)),
                  pl.BlockSpec((1, cw), lambda i, j: (0, j))],
        out_specs=pl.BlockSpec((tr, cw), lambda i, j: (i, j)),
        compiler_params=_params(("parallel", "parallel")), name="conv_fwd")(proj, proj, conv_w, conv_b)


def _conv_bwd(cfg, proj, dact, conv_w, conv_b, dproj):
    s, cd = cfg.S, cfg.CD
    tr, cw, hl = CONV_TR, CONV_CW, CONV_HALO
    cb0 = cfg.OXBC // cw
    nr = s // tr
    last_h = s // hl - 1

    def body(x_ref, hp_ref, hn_ref, d_ref, dn_ref, w_ref, b_ref, dp_in, dx_ref, gw_ref, gb_ref):
        del dp_in
        i = pl.program_id(1)
        ext = jnp.concatenate([jnp.where(i > 0, hp_ref[...].astype(F32), 0.0), x_ref[...].astype(F32),
                               hn_ref[...].astype(F32)], axis=0)
        shifted = [_rows_back(ext, CONV_K - 1 - k)[hl:] for k in range(CONV_K)]
        pre = b_ref[...] + jnp.zeros((tr + hl, cw), F32)
        for k in range(CONV_K):
            pre = pre + w_ref[k:k + 1, :] * shifted[k]
        sg = _sigmoid(pre)
        dact = jnp.concatenate([d_ref[...].astype(F32), jnp.where(i < nr - 1, dn_ref[...].astype(F32), 0.0)], axis=0)
        dpre = dact * (sg * (1.0 + pre * (1.0 - sg)))
        dx = jnp.zeros((tr, cw), F32)
        for k in range(CONV_K):
            dx = dx + w_ref[k:k + 1, :] * _rows_back(dpre, -(CONV_K - 1 - k))[0:tr]
        dx_ref[...] = dx.astype(BF16)

        @pl.when(i == 0)
        def _():
            gw_ref[...] = jnp.zeros_like(gw_ref)
            gb_ref[...] = jnp.zeros_like(gb_ref)

        dcur = dpre[0:tr]
        gb_ref[...] += jnp.sum(dcur, axis=0, keepdims=True)
        for k in range(CONV_K):
            gw_ref[k:k + 1, :] += jnp.sum(dcur * shifted[k][0:tr], axis=0, keepdims=True)

    return pl.pallas_call(
        body, out_shape=(SDS(dproj.shape, BF16), SDS((CONV_K, cd), F32), SDS((1, cd), F32)), grid=(cd // cw, nr),
        in_specs=[pl.BlockSpec((tr, cw), lambda j, i: (i, cb0 + j)),
                  pl.BlockSpec((hl, cw), lambda j, i: (jnp.maximum(i * (tr // hl) - 1, 0), cb0 + j)),
                  pl.BlockSpec((hl, cw), lambda j, i: (jnp.minimum((i + 1) * (tr // hl), last_h), cb0 + j)),
                  pl.BlockSpec((tr, cw), lambda j, i: (i, j)),
                  pl.BlockSpec((hl, cw), lambda j, i: (jnp.minimum((i + 1) * (tr // hl), last_h), j)),
                  pl.BlockSpec((CONV_K, cw), lambda j, i: (0, j)),
                  pl.BlockSpec((1, cw), lambda j, i: (0, j)),
                  pl.BlockSpec(memory_space=pl.ANY)],
        out_specs=(pl.BlockSpec((tr, cw), lambda j, i: (i, cb0 + j)),
                   pl.BlockSpec((CONV_K, cw), lambda j, i: (0, j)),
                   pl.BlockSpec((1, cw), lambda j, i: (0, j))),
        input_output_aliases={7: 0},
        compiler_params=_params(("parallel", "arbitrary")), name="conv_bwd")(
            proj, proj, proj, dact, dact, conv_w, conv_b, dproj)


def _expand(v, e, terms):
    out, rem = None, v
    for _ in range(terms):
        hi = rem.astype(BF16)
        t = _nn(hi, e)
        out = t if out is None else out + t
        rem = rem - hi.astype(F32)
    return out


def _segsum(v, e, terms):
    out, rem = None, v
    for _ in range(terms):
        hi = rem.astype(BF16)
        t = _nt(hi, e)
        out = t if out is None else out + t
        rem = rem - hi.astype(F32)
    return out


def _expand_row(row, e, terms):
    return _expand(jnp.broadcast_to(row, (8, LANES)), e, terms)[0:1]


def _segsum_row(row, e, terms):
    return _segsum(jnp.broadcast_to(row, (8, row.shape[1])), e, terms)[0:1]


def _expansion_matrix(cfg):
    hh = jnp.arange(LANES, dtype=jnp.int32)[:, None]
    cc = jnp.arange(cfg.SI, dtype=jnp.int32)[None, :]
    return (cc // SSM_HEAD_DIM == hh).astype(BF16)


def _tri(lower):
    r = lax.broadcasted_iota(jnp.int32, (CHUNK, CHUNK), 0)
    c = lax.broadcasted_iota(jnp.int32, (CHUNK, CHUNK), 1)
    return (c <= r) if lower else (c >= r)


def _ssd_prep(dtr_ref, db_ref, al_ref, e):
    dtr = dtr_ref[...] + db_ref[...]
    dt = _softplus(dtr)
    a = -jnp.exp(al_ref[...])
    acum = jnp.dot(_tri(True).astype(F32), dt * a, precision=lax.Precision.HIGHEST, preferred_element_type=F32)
    return dtr, dt, a, _expand(dt, e, 2), _expand(acum, e, 3)


def _ssd_fwd(cfg, xact, dt_raw, proj, dt_bias, a_log, d_skip, norm_w, e):
    s, si, cd, gw, bc = cfg.S, cfg.SI, cfg.CD, cfg.GW, cfg.BC
    nc = s // CHUNK
    zb = cfg.OZS // si
    tiles = gw // LANES

    def body(xa_ref, dtr_ref, z_ref, db_ref, al_ref, dsk_ref, nw_ref, e_ref, y_ref, y2_ref, st_ref,
             state, ybuf, x_s, xw_s, ae_s, ea_s, lam_s):
        @pl.when(pl.program_id(0) == 0)
        def _():
            state[...] = jnp.zeros_like(state)

        st_ref[...] = state[...]
        ev = e_ref[...]
        _, _, _, dt_e, a_e = _ssd_prep(dtr_ref, db_ref, al_ref, ev)
        xs = xa_ref[:, 0:si].astype(F32)
        x = xs * dt_e
        lam_e = a_e[CHUNK - 1:CHUNK, :]
        x_s[...] = x.astype(BF16)
        xw_s[...] = (x * jnp.exp(lam_e - a_e)).astype(BF16)
        ae_s[...] = a_e
        ea_s[...] = jnp.exp(a_e)
        ybuf[...] = _expand_row(dsk_ref[...], ev, 3) * xs
        lam_s[...] = jnp.broadcast_to(jnp.exp(lam_e), (8, si))
        tril = _tri(True)
        lane = lax.broadcasted_iota(jnp.int32, (CHUNK, LANES), 1)

        def group(g, carry):
            co = pl.multiple_of(g * gw, LANES)
            bg = xa_ref[:, pl.ds(pl.multiple_of(si + g * SSM_STATE, LANES), SSM_STATE)]
            cg = xa_ref[:, pl.ds(pl.multiple_of(si + bc + g * SSM_STATE, LANES), SSM_STATE)]
            cbm = _nt(cg, bg)
            st = state[:, pl.ds(co, gw)]
            yoff = _nn(cg, st.astype(BF16)) * ea_s[:, pl.ds(co, gw)]
            for k in range(tiles):
                tc = pl.multiple_of(co + k * LANES, LANES)
                at = ae_s[:, pl.ds(tc, LANES)]
                att = at.T
                xt = x_s[:, pl.ds(tc, LANES)]
                acc = yoff[:, k * LANES:(k + 1) * LANES]
                for half in range(2):
                    lo = half * SSM_HEAD_DIM
                    seg = at[:, lo:lo + 1] - att[lo:lo + 1, :]
                    dec = jnp.exp(jnp.where(tril, seg, NEG))
                    xh = jnp.where((lane >= lo) & (lane < lo + SSM_HEAD_DIM), xt, jnp.zeros_like(xt))
                    acc = acc + _nn((cbm * dec).astype(BF16), xh)
                ybuf[:, pl.ds(tc, LANES)] += acc
            state[:, pl.ds(co, gw)] = st * lam_s[0:1, pl.ds(co, gw)] + _tn(bg, xw_s[:, pl.ds(co, gw)])
            return carry

        lax.fori_loop(0, SSM_GROUPS, group, 0)
        y = ybuf[...]
        y_ref[...] = y.astype(BF16)
        z = z_ref[...].astype(F32)
        u = y * (z * _sigmoid(z))
        r = lax.rsqrt(jnp.mean(u * u, axis=-1, keepdims=True) + RMS_EPS)
        y2_ref[...] = (u * r * nw_ref[...]).astype(BF16)

    row = lambda n: pl.BlockSpec((1, n), lambda c: (0, 0))
    return pl.pallas_call(
        body,
        out_shape=(SDS((s, si), BF16), SDS((s, si), BF16), SDS((nc, SSM_STATE, si), F32)),
        grid=(nc,),
        in_specs=[pl.BlockSpec((CHUNK, cd), lambda c: (c, 0)),
                  pl.BlockSpec((CHUNK, LANES), lambda c: (c, 0)),
                  pl.BlockSpec((CHUNK, si), lambda c: (c, zb)),
                  row(LANES), row(LANES), row(LANES), row(si),
                  pl.BlockSpec((LANES, si), lambda c: (0, 0))],
        out_specs=(pl.BlockSpec((CHUNK, si), lambda c: (c, 0)),
                   pl.BlockSpec((CHUNK, si), lambda c: (c, 0)),
                   pl.BlockSpec((None, SSM_STATE, si), lambda c: (c, 0, 0))),
        scratch_shapes=[pltpu.VMEM((SSM_STATE, si), F32), pltpu.VMEM((CHUNK, si), F32),
                        pltpu.VMEM((CHUNK, si), BF16), pltpu.VMEM((CHUNK, si), BF16),
                        pltpu.VMEM((CHUNK, si), F32), pltpu.VMEM((CHUNK, si), F32),
                        pltpu.VMEM((8, si), F32)],
        compiler_params=_params(("arbitrary",)), name="ssd_fwd")(
            xact, dt_raw, proj, dt_bias, a_log, d_skip, norm_w, e)


def _ssd_bwd(cfg, xact, dt_raw, proj, y, dy2, states, dt_bias, a_log, d_skip, norm_w, e, dproj):
    s, si, cd, gw, bc, hpg = cfg.S, cfg.SI, cfg.CD, cfg.GW, cfg.BC, cfg.HPG
    nc = s // CHUNK
    zb = cfg.OZS // si
    tiles = gw // LANES

    def body(xa_ref, dtr_ref, z_ref, y_ref, d2_ref, st_ref, db_ref, al_ref, dsk_ref, nw_ref, e_ref, dp_in,
             dz_ref, dxa_ref, ddt_ref, gnw_ref, gdb_ref, gal_ref, gds_ref,
             dh, dhn, xs_s, x_s, w_s, ae_s, ea_s, g_s, dx_s, dae_s, r_s, lam_s, dle_s):
        del dp_in

        @pl.when(pl.program_id(0) == 0)
        def _():
            dh[...] = jnp.zeros_like(dh)
            gnw_ref[...] = jnp.zeros_like(gnw_ref)
            gdb_ref[...] = jnp.zeros_like(gdb_ref)
            gal_ref[...] = jnp.zeros_like(gal_ref)
            gds_ref[...] = jnp.zeros_like(gds_ref)

        ev = e_ref[...]
        yv = y_ref[...].astype(F32)
        z = z_ref[...].astype(F32)
        sg = _sigmoid(z)
        sz = z * sg
        u = yv * sz
        r = lax.rsqrt(jnp.mean(u * u, axis=-1, keepdims=True) + RMS_EPS)
        nrm = u * r
        d2 = d2_ref[...].astype(F32)
        gnw_ref[...] += jnp.sum(d2 * nrm, axis=0, keepdims=True)
        gn = d2 * nw_ref[...]
        du = r * (gn - nrm * jnp.mean(gn * nrm, axis=-1, keepdims=True))
        gv = du * sz
        dz_ref[...] = (du * yv * (sg * (1.0 + z * (1.0 - sg)))).astype(BF16)
        g_s[...] = gv

        dtr, dt, a, dt_e, a_e = _ssd_prep(dtr_ref, db_ref, al_ref, ev)
        xs = xa_ref[:, 0:si].astype(F32)
        x = xs * dt_e
        lam_e = a_e[CHUNK - 1:CHUNK, :]
        xs_s[...] = xs
        x_s[...] = x
        w_s[...] = jnp.exp(lam_e - a_e)
        ae_s[...] = a_e
        ea_s[...] = jnp.exp(a_e)
        lam_s[...] = jnp.broadcast_to(jnp.exp(lam_e), (8, si))
        gds_ref[...] += _segsum_row(jnp.sum(gv * xs, axis=0, keepdims=True), ev, 2)
        r_s[...] = jnp.zeros_like(r_s)
        tril = _tri(True)
        lane = lax.broadcasted_iota(jnp.int32, (CHUNK, LANES), 1)
        sub = lax.broadcasted_iota(jnp.int32, (CHUNK, LANES), 0)

        def group(g, carry):
            co = pl.multiple_of(g * gw, LANES)
            bo = pl.multiple_of(si + g * SSM_STATE, LANES)
            cof = pl.multiple_of(si + bc + g * SSM_STATE, LANES)
            cols = pl.ds(co, gw)
            bg = xa_ref[:, pl.ds(bo, SSM_STATE)]
            cg = xa_ref[:, pl.ds(cof, SSM_STATE)]
            cbm = _nt(cg, bg)
            st = st_ref[:, cols]
            stb = st.astype(BF16)
            dho = dh[:, cols]
            dhob = dho.astype(BF16)
            ea = ea_s[:, cols]
            gg = g_s[:, cols]
            xg = x_s[:, cols]
            wg = w_s[:, cols]
            explam = lam_s[0:1, cols]
            yoff = _nn(cg, stb) * ea
            ga = (gg * ea).astype(BF16)
            dc = _nt(ga, stb)
            dhn[:, cols] = dho * explam + _tn(cg, ga)
            bdh = _nn(bg, dhob)
            db = _nt((xg * wg).astype(BF16), dhob)
            t = xg * bdh * wg
            dle_s[0:1, cols] = jnp.sum(t, axis=0, keepdims=True) + explam * jnp.sum(dho * st, axis=0, keepdims=True)
            dae_base = gg * yoff - t
            dxw = wg * bdh
            dcb = jnp.zeros((CHUNK, CHUNK), F32)
            for k in range(tiles):
                tc = pl.multiple_of(co + k * LANES, LANES)
                ksl = slice(k * LANES, (k + 1) * LANES)
                at = ae_s[:, pl.ds(tc, LANES)]
                att = at.T
                xt = xg[:, ksl].astype(BF16)
                gt = gg[:, ksl].astype(BF16)
                dxt = dxw[:, ksl]
                place = jnp.zeros((CHUNK, LANES), F32)
                for half in range(2):
                    lo = half * SSM_HEAD_DIM
                    seg = at[:, lo:lo + 1] - att[lo:lo + 1, :]
                    dec = jnp.exp(jnp.where(tril, seg, NEG))
                    mh = cbm * dec
                    gh = jnp.where((lane >= lo) & (lane < lo + SSM_HEAD_DIM), gt, jnp.zeros_like(gt))
                    dm = _nt(gh, xt)
                    dxt = dxt + _tn(mh.astype(BF16), gh)
                    dcb = dcb + dm * dec
                    dseg = dm * mh
                    place = place + jnp.where(lane == lo, jnp.sum(dseg, axis=1, keepdims=True), 0.0)
                    hidx = g * hpg + 2 * k + half
                    r_s[...] += jnp.where(sub == hidx, jnp.sum(dseg, axis=0, keepdims=True), 0.0)
                dx_s[:, pl.ds(tc, LANES)] = dxt
                dae_s[:, pl.ds(tc, LANES)] = dae_base[:, ksl] + place
            dcbb = dcb.astype(BF16)
            dxa_ref[:, pl.ds(bo, SSM_STATE)] = (db + _tn(dcbb, cg)).astype(BF16)
            dxa_ref[:, pl.ds(cof, SSM_STATE)] = (dc + _nn(dcbb, bg)).astype(BF16)
            return carry

        lax.fori_loop(0, SSM_GROUPS, group, 0)
        dlam = _segsum_row(dle_s[0:1, :], ev, 2)
        da_ = _segsum(dae_s[...], ev, 2) - r_s[...].T
        da_ = da_ + jnp.where(sub == CHUNK - 1, dlam, 0.0)
        dda = jnp.dot(_tri(False).astype(F32), da_, precision=lax.Precision.HIGHEST, preferred_element_type=F32)
        dxv = dx_s[...]
        xs = xs_s[...]
        ddt = dda * a + _segsum(dxv * xs, ev, 2)
        gal_ref[...] += jnp.sum(dda * dt, axis=0, keepdims=True) * a
        ddtr = ddt * _sigmoid(dtr)
        gdb_ref[...] += jnp.sum(ddtr, axis=0, keepdims=True)
        ddt_ref[...] = ddtr
        dxa_ref[:, 0:si] = (dxv * dt_e + g_s[...] * _expand_row(dsk_ref[...], ev, 3)).astype(BF16)
        dh[...] = dhn[...]

    rev = lambda c: nc - 1 - c
    row = lambda n: pl.BlockSpec((1, n), lambda c: (0, 0))
    big = lambda: pltpu.VMEM((CHUNK, si), F32)
    return pl.pallas_call(
        body,
        out_shape=(SDS(dproj.shape, BF16), SDS((s, cd), BF16), SDS((s, LANES), F32),
                   SDS((1, si), F32), SDS((1, LANES), F32), SDS((1, LANES), F32), SDS((1, LANES), F32)),
        grid=(nc,),
        in_specs=[pl.BlockSpec((CHUNK, cd), lambda c: (rev(c), 0)),
                  pl.BlockSpec((CHUNK, LANES), lambda c: (rev(c), 0)),
                  pl.BlockSpec((CHUNK, si), lambda c: (rev(c), zb)),
                  pl.BlockSpec((CHUNK, si), lambda c: (rev(c), 0)),
                  pl.BlockSpec((CHUNK, si), lambda c: (rev(c), 0)),
                  pl.BlockSpec((None, SSM_STATE, si), lambda c: (rev(c), 0, 0)),
                  row(LANES), row(LANES), row(LANES), row(si),
                  pl.BlockSpec((LANES, si), lambda c: (0, 0)),
                  pl.BlockSpec(memory_space=pl.ANY)],
        out_specs=(pl.BlockSpec((CHUNK, si), lambda c: (rev(c), zb)),
                   pl.BlockSpec((CHUNK, cd), lambda c: (rev(c), 0)),
                   pl.BlockSpec((CHUNK, LANES), lambda c: (rev(c), 0)),
                   row(si), row(LANES), row(LANES), row(LANES)),
        scratch_shapes=[pltpu.VMEM((SSM_STATE, si), F32), pltpu.VMEM((SSM_STATE, si), F32),
                        big(), big(), big(), big(), big(), big(), big(), big(),
                        pltpu.VMEM((CHUNK, LANES), F32), pltpu.VMEM((8, si), F32), pltpu.VMEM((8, si), F32)],
        input_output_aliases={11: 0},
        compiler_params=_params(("arbitrary",)), name="ssd_bwd")(
            xact, dt_raw, proj, y, dy2, states, dt_bias, a_log, d_skip, norm_w, e, dproj)


MERGE_TR = 512
MERGE_CW = 2048


def _merge_fwd(cfg, proj, a_br, s_br):
    s, d = cfg.S, cfg.D
    tr, cw = MERGE_TR, min(MERGE_CW, d)
    ga0, gs0 = cfg.OGA // cw, cfg.OGS // cw

    def body(ga_ref, gs_ref, a_ref, s_ref, o_ref):
        o_ref[...] = (_sigmoid(ga_ref[...].astype(F32)) * a_ref[...].astype(F32)
                      + _sigmoid(gs_ref[...].astype(F32)) * s_ref[...].astype(F32)).astype(BF16)

    blk = pl.BlockSpec((tr, cw), lambda i, j: (i, j))
    return pl.pallas_call(
        body, out_shape=SDS((s, d), BF16), grid=(s // tr, d // cw),
        in_specs=[pl.BlockSpec((tr, cw), lambda i, j: (i, ga0 + j)),
                  pl.BlockSpec((tr, cw), lambda i, j: (i, gs0 + j)), blk, blk],
        out_specs=blk, compiler_params=_params(("parallel", "parallel")), name="merge_fwd")(proj, proj, a_br, s_br)


def _merge_bwd(cfg, proj, branch, dmerged, gate_off, dproj, name):
    s, d = cfg.S, cfg.D
    tr, cw = MERGE_TR, min(MERGE_CW, d)
    g0 = gate_off // cw
    fresh = dproj is None

    def body(*refs):
        g_ref, b_ref, dm_ref = refs[:3]
        dg_ref, db_ref = refs[-2:]
        dm = dm_ref[...].astype(F32)
        sg = _sigmoid(g_ref[...].astype(F32))
        db_ref[...] = (dm * sg).astype(BF16)
        dg_ref[...] = (dm * b_ref[...].astype(F32) * sg * (1.0 - sg)).astype(BF16)

    blk = pl.BlockSpec((tr, cw), lambda i, j: (i, j))
    gate = pl.BlockSpec((tr, cw), lambda i, j: (i, g0 + j))
    return pl.pallas_call(
        body, out_shape=(SDS((s, cfg.NM), BF16), SDS((s, d), BF16)), grid=(s // tr, d // cw),
        in_specs=[gate, blk, blk] + ([] if fresh else [HBM_SPEC]),
        out_specs=(gate, blk),
        input_output_aliases={} if fresh else {3: 0},
        compiler_params=_params(("parallel", "parallel")), name=name)(
            *((proj, branch, dmerged) + (() if fresh else (dproj,))))


def _outproj_loss(merged, w_out, x, target, fnw):
    s, d = x.shape
    tr = 256

    def body(m_ref, w_ref, x_ref, t_ref, fw_ref, dof_ref, dob_ref, loss_ref, g_ref):
        out = x_ref[...] + _nn(m_ref[...], w_ref[...])
        r = lax.rsqrt(jnp.mean(out * out, axis=-1, keepdims=True) + RMS_EPS)
        nrm = out * r
        fw = fw_ref[...]
        err = nrm * fw - t_ref[...]
        dy = err * (1.0 / d)
        gy = dy * fw
        dout = r * (gy - nrm * jnp.mean(gy * nrm, axis=-1, keepdims=True))
        dof_ref[...] = dout
        dob_ref[...] = dout.astype(BF16)

        @pl.when(pl.program_id(0) == 0)
        def _():
            loss_ref[...] = jnp.zeros_like(loss_ref)
            g_ref[...] = jnp.zeros_like(g_ref)

        loss_ref[...] += jnp.sum(jnp.sum(err * err, axis=1, keepdims=True), axis=0, keepdims=True) * (0.5 / d)
        g_ref[...] += jnp.sum(dy * nrm, axis=0, keepdims=True)

    blk = pl.BlockSpec((tr, d), lambda i: (i, 0))
    return pl.pallas_call(
        body, out_shape=(SDS((s, d), F32), SDS((s, d), BF16), SDS((1, LANES), F32), SDS((1, d), F32)), grid=(s // tr,),
        in_specs=[blk, pl.BlockSpec((d, d), lambda i: (0, 0)), blk, blk, pl.BlockSpec((1, d), lambda i: (0, 0))],
        out_specs=(blk, blk, pl.BlockSpec((1, LANES), lambda i: (0, 0)), pl.BlockSpec((1, d), lambda i: (0, 0))),
        compiler_params=_params(("arbitrary",)), name="outproj_loss")(merged, w_out, x, target, fnw)


ELEMWISE_BLOCK_BYTES = 1 << 20


def _row_block(rows, cols, itemsize=4):
    best = None
    for tr in range(16, rows + 1, 16):
        if rows % tr == 0 and tr * cols * itemsize <= ELEMWISE_BLOCK_BYTES:
            best = tr
    return best if best is not None else rows


def _adamw(w, g, m, v, name):
    if w.ndim == 3:
        _, rows, cols = w.shape
        r2 = g.shape[1]
        tr = _row_block(r2, cols)
        nb = r2 // tr
        lead = pl.BlockSpec((None, tr, cols), lambda i: (0, i, 0))
        halves = pl.BlockSpec((None, tr, cols), lambda i: (i // nb, i % nb, 0))
        out = SDS(w.shape, F32)
        return pl.pallas_call(
            _adamw_body(), out_shape=(out, out, out), grid=(rows // tr,), in_specs=[lead, halves, lead, lead],
            out_specs=(lead,) * 3, compiler_params=_params(("parallel",)), name=name)(w, g, m, v)
    rows, cols = w.shape
    tr = _row_block(rows, cols)
    blk = pl.BlockSpec((tr, cols), lambda i: (i, 0))
    out = SDS((rows, cols), F32)
    return pl.pallas_call(
        _adamw_body(), out_shape=(out, out, out), grid=(rows // tr,), in_specs=[blk] * 4, out_specs=(blk,) * 3,
        compiler_params=_params(("parallel",)), name=name)(w, g, m, v)


def _adamw_body():
    def body(w_ref, g_ref, m_ref, v_ref, d_ref, nm_ref, nv_ref):
        gv = g_ref[...]
        nm = ADAM_B1 * m_ref[...] + (1.0 - ADAM_B1) * gv
        nv = ADAM_B2 * v_ref[...] + (1.0 - ADAM_B2) * jnp.square(gv)
        m_hat = nm / (1.0 - ADAM_B1 ** ADAM_STEP)
        v_hat = nv / (1.0 - ADAM_B2 ** ADAM_STEP)
        d_ref[...] = -ADAM_LR * (m_hat / (jnp.sqrt(v_hat) + ADAM_EPS) + ADAM_WD * w_ref[...])
        nm_ref[...] = nm
        nv_ref[...] = nv

    return body


HBM_SPEC = pl.BlockSpec(memory_space=pl.ANY)


def _position():
    return lax.axis_index("x"), lax.axis_index("y"), lax.axis_index("c")


class _Carry:
    def __init__(self, arrays, out_shapes, sems, start, finish):
        self.arrays, self.out_shapes, self.sems, self.start, self.finish = list(arrays), out_shapes, sems, start, finish

    def sem_shapes(self):
        return [pltpu.SemaphoreType.DMA((k,)) for k in self.sems]


def _gather_carry(shards):
    n = len(shards)

    def copies(ins, outs, sems):
        send_sems, recv_sems, fsend_sems, frecv_sems = sems
        x, y, c = _position()
        me = 2 * x + y
        peers = [(1 - x, y), (x, 1 - y), (1 - x, 1 - y)]

        def over_ici(t, p, chip):
            px, py = peers[p]
            r2 = ins[t].shape[0] // 2
            return pltpu.make_async_remote_copy(
                src_ref=ins[t].at[pl.ds(c * r2, r2), :], dst_ref=outs[t].at[chip, c], send_sem=send_sems.at[3 * t + p],
                recv_sem=recv_sems.at[3 * t + p], device_id=(px, py, c), device_id_type=MESH)

        def to_sibling(t, p, half):
            px, py = peers[p]
            slab = outs[t].at[2 * px + py, half]
            return pltpu.make_async_remote_copy(
                src_ref=slab, dst_ref=slab, send_sem=fsend_sems.at[3 * t + p], recv_sem=frecv_sems.at[3 * t + p],
                device_id=(x, y, 1 - c), device_id_type=MESH)

        pairs = [(t, p) for t in range(n) for p in range(3)]
        sends = [over_ici(t, p, me) for t, p in pairs]
        lands = [over_ici(t, p, 2 * peers[p][0] + peers[p][1]) for t, p in pairs]
        passed = [to_sibling(t, p, c) for t, p in pairs]
        from_sibling = [to_sibling(t, p, 1 - c) for t, p in pairs]
        return sends, lands, passed, from_sibling

    def start(ins, outs, sems):
        for cp in copies(ins, outs, sems)[0]:
            cp.start()

    def finish(ins, outs, sems):
        sends, lands, passed, from_sibling = copies(ins, outs, sems)
        for land, fwd in zip(lands, passed):
            land.wait_recv()
            fwd.start()
        for cp in from_sibling:
            cp.wait_recv()
        for cp in sends + passed:
            cp.wait_send()

    return _Carry(shards, [SDS((N_CHIPS, 2, a.shape[0] // 2, a.shape[1]), a.dtype) for a in shards], [3 * n] * 4,
                  start, finish)


def _scatter_carry(parts):
    def start(ins, outs, sems):
        for cp in _scatter_copies(ins, outs, *sems)[0]:
            cp.start()

    def finish(ins, outs, sems):
        sends, lands = _scatter_copies(ins, outs, *sems)
        for cp in lands:
            cp.wait_recv()
        for cp in sends:
            cp.wait_send()

    return _Carry(parts, [SDS(a.shape, a.dtype) for a in parts], [3 * len(parts)] * 2, start, finish)


def _with_own(gathered, own, chip):
    full = gathered.reshape((N_CHIPS,) + own.shape)
    return lax.dynamic_update_index_in_dim(full, own, chip, 0)


def _exchange_halves(grads):
    n = len(grads)
    slabs = [list(g) if isinstance(g, (list, tuple)) else [g] for g in grads]
    flat = [a for s in slabs for a in s]
    ncp = len(flat)

    def body(*refs):
        ins, outs = refs[:ncp], refs[ncp:ncp + n]
        send_sems, recv_sems = refs[ncp + n:]
        x, y, c = _position()
        cps, k = [], 0
        for t in range(n):
            for j in range(len(slabs[t])):
                if len(slabs[t]) == 1:
                    r2 = ins[k].shape[1] // 2
                    src, dst = ins[k].at[:, pl.ds((1 - c) * r2, r2), :], outs[t]
                else:
                    r2 = ins[k].shape[0] // 2
                    src, dst = ins[k].at[pl.ds((1 - c) * r2, r2), :], outs[t].at[j]
                cps.append(pltpu.make_async_remote_copy(
                    src_ref=src, dst_ref=dst, send_sem=send_sems.at[k], recv_sem=recv_sems.at[k],
                    device_id=(x, y, 1 - c), device_id_type=MESH))
                k += 1
        for cp in cps:
            cp.start()
        for cp in cps:
            cp.wait()

    def landing(s):
        a = s[0]
        return SDS((N_CHIPS, a.shape[-2] // 2, a.shape[-1]), a.dtype)

    return pl.pallas_call(
        body, out_shape=[landing(s) for s in slabs],
        in_specs=[HBM_SPEC] * ncp, out_specs=[HBM_SPEC] * n,
        scratch_shapes=[pltpu.SemaphoreType.DMA((ncp,)), pltpu.SemaphoreType.DMA((ncp,))],
        compiler_params=pltpu.CompilerParams(has_side_effects=True), name="reduce_sibling")(*flat)


def _scatter_copies(ins, outs, send_sems, recv_sems):
    x, y, c = _position()
    me = 2 * x + y
    peers = [(1 - x, y), (x, 1 - y), (1 - x, 1 - y)]

    def remote(t, p, src_slab, dst_slab):
        px, py = peers[p]
        return pltpu.make_async_remote_copy(
            src_ref=ins[t].at[src_slab], dst_ref=outs[t].at[dst_slab], send_sem=send_sems.at[3 * t + p],
            recv_sem=recv_sems.at[3 * t + p], device_id=(px, py, c), device_id_type=MESH)

    n = len(ins)
    sends = [remote(t, p, 2 * peers[p][0] + peers[p][1], me) for t in range(n) for p in range(3)]
    lands = [remote(t, p, me, 2 * peers[p][0] + peers[p][1]) for t in range(n) for p in range(3)]
    return sends, lands


def _share_halves(halves):
    n = len(halves)

    def body(*refs):
        ins, outs = refs[:n], refs[n:2 * n]
        send_sems, recv_sems = refs[2 * n:]
        x, y, c = _position()

        def copy(t, slab):
            return pltpu.make_async_remote_copy(
                src_ref=ins[t].at[slab], dst_ref=outs[t].at[slab], send_sem=send_sems.at[t], recv_sem=recv_sems.at[t],
                device_id=(x, y, 1 - c), device_id_type=MESH)

        for t in range(n):
            copy(t, c).start()
        for t in range(n):
            copy(t, 1 - c).wait_recv()
        for t in range(n):
            copy(t, c).wait_send()

    return pl.pallas_call(
        body, out_shape=[SDS(a.shape, a.dtype) for a in halves],
        in_specs=[HBM_SPEC] * n, out_specs=[HBM_SPEC] * n,
        scratch_shapes=[pltpu.SemaphoreType.DMA((n,)), pltpu.SemaphoreType.DMA((n,))],
        input_output_aliases={t: t for t in range(n)},
        compiler_params=pltpu.CompilerParams(has_side_effects=True), name="share_sibling")(*halves)


def _add_sibling_slab(grad_j, recv, core, j, sums):
    nch, r2, cols = recv.shape
    tr = _row_block(r2, cols)
    nb = r2 // tr
    fresh = sums is None

    def body(c_ref, g_ref, r_ref, *rest):
        del c_ref
        rest[-1][...] = (g_ref[...].astype(F32) + r_ref[...].astype(F32)).astype(BF16)

    return pl.pallas_call(
        body, out_shape=SDS(recv.shape, BF16),
        grid_spec=pltpu.PrefetchScalarGridSpec(
            num_scalar_prefetch=1, grid=(nb,),
            in_specs=[pl.BlockSpec((tr, cols), lambda i, c_ref: (c_ref[0] * nb + i, 0)),
                      pl.BlockSpec((None, tr, cols), lambda i, c_ref: (j, i, 0))] + ([] if fresh else [HBM_SPEC]),
            out_specs=pl.BlockSpec((None, tr, cols), lambda i, c_ref: (j, i, 0))),
        input_output_aliases={} if fresh else {3: 0},
        compiler_params=_params(("parallel",)), name="add_sibling_slab")(
            *((core, grad_j, recv) + (() if fresh else (sums,))))


def _add_sibling(grad, recv, core):
    if isinstance(grad, (list, tuple)):
        sums = None
        for j, g in enumerate(grad):
            sums = _add_sibling_slab(g, recv, core, j, sums)
        return sums
    nch, r2, cols = recv.shape
    tr = _row_block(r2, cols)
    nb = r2 // tr

    def body(c_ref, g_ref, r_ref, o_ref):
        del c_ref
        o_ref[...] = (g_ref[...].astype(F32) + r_ref[...].astype(F32)).astype(BF16)

    return pl.pallas_call(
        body, out_shape=SDS(recv.shape, BF16),
        grid_spec=pltpu.PrefetchScalarGridSpec(
            num_scalar_prefetch=1, grid=(nch, nb),
            in_specs=[pl.BlockSpec((None, tr, cols), lambda j, i, c_ref: (j, c_ref[0] * nb + i, 0)),
                      pl.BlockSpec((None, tr, cols), lambda j, i, c_ref: (j, i, 0))],
            out_specs=pl.BlockSpec((None, tr, cols), lambda j, i, c_ref: (j, i, 0))),
        compiler_params=_params(("parallel", "parallel")), name="add_sibling")(core, grad, recv)


def _add_chips(own, recv, chip_core):
    nch, r2, cols = recv.shape
    tr = _row_block(r2, cols)

    nsc = 2 + nch

    def body(*refs):
        me = refs[0][0]
        own_ref, p_refs, o_ref = refs[nsc], refs[nsc + 1:nsc + 1 + nch], refs[nsc + 1 + nch]
        acc = None
        for j in range(nch):
            term = jnp.where(me == j, own_ref[...], p_refs[j][...]).astype(F32)
            acc = term if acc is None else acc + term
        o_ref[...] = acc

    def slab(j):
        return pl.BlockSpec((None, tr, cols), lambda i, *sc: (sc[2 + j][0], i, 0))

    return pl.pallas_call(
        body, out_shape=SDS((2, r2, cols), F32),
        grid_spec=pltpu.PrefetchScalarGridSpec(
            num_scalar_prefetch=nsc, grid=(r2 // tr,),
            in_specs=[pl.BlockSpec((None, tr, cols), lambda i, *sc: (sc[0][0], i, 0))] + [slab(j) for j in range(nch)],
            out_specs=pl.BlockSpec((None, tr, cols), lambda i, *sc: (sc[1][0], i, 0))),
        compiler_params=_params(("parallel",)), name="add_chips")(*chip_core, own, *([recv] * nch))


def _allreduce_small(pack):
    rows = pack.shape[0]

    def body(p_ref, o_ref, buf, send_sems, recv_sems):
        x, y, c = _position()
        me = 4 * x + 2 * y + c
        buf[me] = p_ref[...]

        def copy(dst_dev, slot):
            return pltpu.make_async_remote_copy(
                src_ref=p_ref, dst_ref=buf.at[slot], send_sem=send_sems.at[dst_dev], recv_sem=recv_sems.at[slot],
                device_id=(dst_dev // 4, (dst_dev // 2) % 2, dst_dev % 2), device_id_type=MESH)

        for dev in range(N_DEV):
            @pl.when(dev != me)
            def _():
                copy(dev, me).start()
        for dev in range(N_DEV):
            @pl.when(dev != me)
            def _():
                copy(dev, dev).wait_recv()
        for dev in range(N_DEV):
            @pl.when(dev != me)
            def _():
                copy(dev, me).wait_send()
        acc = buf[0]
        for dev in range(1, N_DEV):
            acc = acc + buf[dev]
        o_ref[...] = acc

    return pl.pallas_call(
        body, out_shape=SDS(pack.shape, F32),
        in_specs=[pl.BlockSpec(memory_space=pltpu.VMEM)], out_specs=pl.BlockSpec(memory_space=pltpu.VMEM),
        scratch_shapes=[pltpu.VMEM((N_DEV, rows, LANES), F32), pltpu.SemaphoreType.DMA((N_DEV,)),
                        pltpu.SemaphoreType.DMA((N_DEV,))],
        compiler_params=pltpu.CompilerParams(has_side_effects=True), name="allreduce_small")(pack)


ATTN_TQ = 256


def _local_step(cfg, x, target, w, to_chips=None, late=None, hn=None):
    d = cfg.D
    if hn is None:
        hn = _rmsnorm_fwd(x, w["norm_w"])
    proj = _mm(hn, w["w_main"], "nn", BF16, "proj_main", carry=late[0] if late else None)
    if late:
        proj, arrived = proj
        w = {**w, **late[1](arrived)}
    dt_raw = _mm(hn, w["w_dt"], "nn", F32, "proj_dt")
    slopes = _slopes(cfg.H)
    near = _Pass(ATTN_TQ, DILATED_PATTERNS[:-1], 1, cfg.S)
    far = _Pass(LANES, DILATED_PATTERNS[-1:], DEINT, cfg.S // DEINT)
    tab_near, tab_far = _attn_tables(near), _attn_tables(far)
    cols_near, cols_far = (cfg.OQ, cfg.OK, cfg.OV), (0, d, 2 * d)
    qkv_far = _deinterleave(proj, 0, 3 * d, "attn_deinterleave")
    o_1, lse_1 = _attn_fwd(cfg, near, proj, cols_near, tab_near, slopes, "attn_fwd_near")
    o_2, lse_2 = _attn_fwd(cfg, far, qkv_far, cols_far, tab_far, slopes, "attn_fwd_far")
    o_a, oag, lse = _attn_merge(cfg, proj, o_1, lse_1, o_2, lse_2)
    xact = _conv_fwd(cfg, proj, w["conv_w"], w["conv_b"])
    e = _expansion_matrix(cfg)
    y, y2, states = _ssd_fwd(cfg, xact, dt_raw, proj, w["dt_bias"], w["a_log"], w["d_skip"], w["ssm_norm_w"], e)
    a_br = _mm(oag, w["w_attn"], "nn", BF16, "branch_attn")
    s_br = _mm(y2, w["w_ssm"], "nn", BF16, "branch_ssm")
    merged = _merge_fwd(cfg, proj, a_br, s_br)
    dout_f, dout_b, loss_row, g_fnw = _outproj_loss(merged, w["w_out"], x, target, w["final_norm_w"])

    dmerged = _mm(dout_b, w["w_out"], "nt", BF16, "d_merged")
    g_w_out = _mm(merged, dout_b, "tn", BF16, "g_w_out")
    dproj, da_br = _merge_bwd(cfg, proj, a_br, dmerged, cfg.OGA, None, "merge_bwd_attn")
    dproj, ds_br = _merge_bwd(cfg, proj, s_br, dmerged, cfg.OGS, dproj, "merge_bwd_ssm")
    doag = _mm(da_br, w["w_attn"], "nt", BF16, "d_oag")
    g_w_attn = _mm(oag, da_br, "tn", BF16, "g_w_attn")
    dy2 = _mm(ds_br, w["w_ssm"], "nt", BF16, "d_y2")
    g_w_ssm = _mm(y2, ds_br, "tn", BF16, "g_w_ssm")
    dproj, dxact, ddt, g_snw, g_dtb, g_alog, g_dsk = _ssd_bwd(
        cfg, xact, dt_raw, proj, y, dy2, states, w["dt_bias"], w["a_log"], w["d_skip"], w["ssm_norm_w"], e, dproj)
    dproj, g_cw, g_cb = _conv_bwd(cfg, proj, dxact, w["conv_w"], w["conv_b"], dproj)
    dproj, do, do_far, dl, dl_far, lse_far = _attn_bwd_prep(cfg, proj, o_a, doag, lse, dproj)
    g_near = _attn_bwd(cfg, near, proj, cols_near, do, lse, dl, tab_near, slopes, "attn_bwd_near")
    g_far = _attn_bwd(cfg, far, qkv_far, cols_far, do_far, lse_far, dl_far, tab_far, slopes, "attn_bwd_far")
    for g_1, g_2, col0, nm in zip(g_near, g_far, cols_near, ("attn_dq", "attn_dk", "attn_dv")):
        dproj = _attn_grad_sum(cfg, g_1, g_2, col0, dproj, nm)
    ddt_b = ddt.astype(BF16)
    g_w_main = _mm(hn, dproj, "tn", BF16, "g_w_main")
    g_w_dt = _mm(hn, ddt_b, "tn", BF16, "g_w_dt")
    grads = dict(w_main=g_w_main, w_dt=g_w_dt, conv_w=g_cw, conv_b=g_cb, dt_bias=g_dtb, a_log=g_alog,
                 d_skip=g_dsk, ssm_norm_w=g_snw, w_attn=g_w_attn, w_ssm=g_w_ssm, w_out=g_w_out, final_norm_w=g_fnw)
    sent = to_chips(grads) if to_chips is not None else ()
    dhn = _mm(dproj, w["w_main"], "nt", F32, "d_hn", tk=1024, carry=_scatter_carry(sent) if sent else None)
    landed = ()
    if sent:
        dhn, landed = dhn
    dhn_dt = _mm(ddt_b, w["w_dt"], "nt", F32, "d_hn_dt")
    grad_x, grads["norm_w"] = _rmsnorm_bwd(x, w["norm_w"], dhn, dhn_dt, dout_f)
    return loss_row, grad_x, grads, sent, landed


def _pad_lanes(v):
    return jnp.pad(v, ((0, 0), (0, LANES - v.shape[1])))


def _cut(lo, hi, a, b):
    a, b = max(lo, a), min(hi, b)
    return (a, b) if a < b else None


def _main_from_shards(cfg, shards):
    per = cfg.N_IN // len(shards)
    main, dt = [], []
    for j, sh in enumerate(shards):
        lo, hi = j * per, (j + 1) * per
        for dst, rng in ((main, (0, cfg.OGA)), (dt, (cfg.OGA, cfg.OGA + cfg.NH)), (main, (cfg.OGA + cfg.NH, cfg.N_IN))):
            c = _cut(lo, hi, *rng)
            if c is not None:
                dst.append(sh[:, c[0] - lo:c[1] - lo])
    return jnp.concatenate(main, axis=1), _pad_lanes(jnp.concatenate(dt, axis=1))


def _shards_from_main(cfg, g_main, g_dt, n):
    per = cfg.N_IN // n
    out = []
    for j in range(n):
        lo, hi = j * per, (j + 1) * per
        parts = []
        for src, off, rng in ((g_main, 0, (0, cfg.OGA)), (g_dt, cfg.OGA, (cfg.OGA, cfg.OGA + cfg.NH)),
                              (g_main, cfg.NH, (cfg.OGA + cfg.NH, cfg.N_IN))):
            c = _cut(lo, hi, *rng)
            if c is not None:
                parts.append(src[:, c[0] - off:c[1] - off])
        out.append(jnp.concatenate(parts, axis=1) if len(parts) > 1 else parts[0])
    return out


def _full_weights(cfg, norm_w, w_in_shards, conv_w, conv_b, dt_bias, a_log, d_skip, ssm_norm_w, w_attn, w_ssm, w_out, fnw):
    w_main, w_dt = _main_from_shards(cfg, w_in_shards)
    return dict(norm_w=norm_w, w_main=w_main.astype(BF16), w_dt=w_dt.astype(BF16), conv_w=conv_w, conv_b=conv_b,
                dt_bias=_pad_lanes(dt_bias), a_log=_pad_lanes(a_log), d_skip=_pad_lanes(d_skip), ssm_norm_w=ssm_norm_w,
                final_norm_w=fnw, **{k: v.astype(BF16) for k, v in (("w_attn", w_attn), ("w_ssm", w_ssm), ("w_out", w_out))
                                     if v is not None})


def kernel(x, norm_w, w_in, conv_w, conv_b, dt_bias, a_log, d_skip, ssm_norm_w, w_attn_branch, w_ssm_branch, w_out, final_norm_w, loss_target, m_norm_w, m_w_in, m_conv_w, m_conv_b, m_dt_bias, m_a_log, m_d_skip, m_ssm_norm_w, m_w_attn_branch, m_w_ssm_branch, m_w_out, m_final_norm_w, v_norm_w, v_w_in, v_conv_w, v_conv_b, v_dt_bias, v_a_log, v_d_skip, v_ssm_norm_w, v_w_attn_branch, v_w_ssm_branch, v_w_out, v_final_norm_w):
    cfg = _Cfg(x.shape[1], x.shape[2])
    d, si, cd, nh = cfg.D, cfg.SI, cfg.CD, cfg.NH
    chip = 2 * lax.axis_index("x") + lax.axis_index("y")
    core = lax.axis_index("c").astype(jnp.int32).reshape(1)
    chip = chip.astype(jnp.int32)
    chip_core = [chip.reshape(1), core] + [jnp.where(chip == j, (j + 1) % N_CHIPS, j).astype(jnp.int32).reshape(1)
                                           for j in range(N_CHIPS)]

    own = [w_in[0].astype(BF16), conv_w[0].reshape(4 * CONV_K, -1)]
    hn, gathered = _rmsnorm_fwd(x[0], norm_w, carry=_gather_carry(own))
    a_in, a_cw = [_with_own(g, o, chip) for g, o in zip(gathered, own)]
    conv_w_full = a_cw.reshape(N_CHIPS, CONV_K, cd // N_CHIPS).transpose(1, 0, 2).reshape(CONV_K, cd)
    w = _full_weights(cfg, norm_w, [a_in[j] for j in range(N_CHIPS)], conv_w_full, conv_b, dt_bias, a_log, d_skip,
                      ssm_norm_w, None, None, None, final_norm_w.reshape(1, d))
    own_late = [w_attn_branch[0].astype(BF16), w_ssm_branch[0].astype(BF16), w_out[0].astype(BF16)]

    def late_weights(arrived):
        a_attn, a_ssm, a_out = [_with_own(g, o, chip) for g, o in zip(arrived, own_late)]
        return dict(w_attn=a_attn.reshape(d, d), w_ssm=a_ssm.reshape(si, d), w_out=a_out.reshape(d, d))

    def to_chips(grads):
        by_chip = [_shards_from_main(cfg, grads["w_main"], grads["w_dt"], N_CHIPS),
                   grads["w_attn"].reshape(N_CHIPS, d // N_CHIPS, d),
                   grads["w_ssm"].reshape(N_CHIPS, si // N_CHIPS, d),
                   grads["w_out"].reshape(N_CHIPS, d // N_CHIPS, d)]
        from_sibling = _exchange_halves(by_chip)
        return [_add_sibling(g, r, core) for g, r in zip(by_chip, from_sibling)]

    loss_row, grad_x, grads, chip_sums, from_chips = _local_step(
        cfg, x[0], loss_target[0], w, to_chips, (_gather_carry(own_late), late_weights), hn)
    halves = [_add_chips(o, p, chip_core) for o, p in zip(chip_sums, from_chips)]
    r_in, r_attn, r_ssm, r_out = _share_halves(halves)
    g_in, g_attn, g_ssm, g_out = [h.reshape(2 * h.shape[1], h.shape[2]) for h in (r_in, r_attn, r_ssm, r_out)]

    small = [loss_row, grads["norm_w"], grads["conv_b"], grads["dt_bias"], grads["a_log"], grads["d_skip"],
             grads["ssm_norm_w"], grads["final_norm_w"], grads["conv_w"].reshape(1, CONV_K * cd)]
    sizes = [a.shape[1] for a in small]
    total = sum(sizes)
    rows = -(-total // (8 * LANES)) * 8
    flat = jnp.pad(jnp.concatenate(small, axis=1), ((0, 0), (0, rows * LANES - total)))
    red = _allreduce_small(flat.reshape(rows, LANES)).reshape(1, rows * LANES)
    offs = [sum(sizes[:i]) for i in range(len(sizes))]
    loss_r, g_nw, g_cb, g_dtb, g_alog, g_dsk, g_snw, g_fnw, g_cw_flat = [
        red[:, o:o + n] for o, n in zip(offs, sizes)]
    loss = loss_r[0, 0]
    g_dtb, g_alog, g_dsk = g_dtb[:, :nh], g_alog[:, :nh], g_dsk[:, :nh]
    cshard = cd // N_CHIPS
    g_cw = lax.dynamic_slice_in_dim(g_cw_flat.reshape(CONV_K, cd), chip * cshard, cshard, axis=1)

    upd = {}
    for name, wv, gv, mv, vv in [("w_in", w_in, r_in, m_w_in, v_w_in),
                                 ("w_attn", w_attn_branch, r_attn, m_w_attn_branch, v_w_attn_branch),
                                 ("w_ssm", w_ssm_branch, r_ssm, m_w_ssm_branch, v_w_ssm_branch),
                                 ("w_out", w_out, r_out, m_w_out, v_w_out)]:
        upd[name] = _adamw(wv, gv, mv, vv, "adamw_" + name)
    names = ["norm_w", "conv_w", "conv_b", "dt_bias", "a_log", "d_skip", "ssm_norm_w", "final_norm_w"]
    ws = [norm_w, conv_w[0].reshape(1, -1), conv_b, dt_bias, a_log, d_skip, ssm_norm_w, final_norm_w.reshape(1, d)]
    gs = [g_nw, g_cw.reshape(1, -1), g_cb, g_dtb, g_alog, g_dsk, g_snw, g_fnw]
    ms = [m_norm_w, m_conv_w[0].reshape(1, -1), m_conv_b, m_dt_bias, m_a_log, m_d_skip, m_ssm_norm_w,
          m_final_norm_w.reshape(1, d)]
    vs = [v_norm_w, v_conv_w[0].reshape(1, -1), v_conv_b, v_dt_bias, v_a_log, v_d_skip, v_ssm_norm_w,
          v_final_norm_w.reshape(1, d)]
    ssz = [a.shape[1] for a in ws]
    stot = sum(ssz)
    srows = -(-stot // (8 * LANES)) * 8

    def pack(parts):
        return jnp.pad(jnp.concatenate(parts, axis=1), ((0, 0), (0, srows * LANES - stot))).reshape(srows, LANES)

    packed = _adamw(pack(ws), pack(gs), pack(ms), pack(vs), "adamw_small")
    soffs = [sum(ssz[:i]) for i in range(len(ssz))]
    for k, nm in enumerate(names):
        upd[nm] = tuple(p.reshape(1, srows * LANES)[:, soffs[k]:soffs[k] + ssz[k]] for p in packed)

    shapes = dict(norm_w=norm_w.shape, w_in=w_in.shape, conv_w=conv_w.shape, conv_b=conv_b.shape, dt_bias=dt_bias.shape,
                  a_log=a_log.shape, d_skip=d_skip.shape, ssm_norm_w=ssm_norm_w.shape, w_attn=w_attn_branch.shape,
                  w_ssm=w_ssm_branch.shape, w_out=w_out.shape, final_norm_w=final_norm_w.shape)
    order = ["norm_w", "w_in", "conv_w", "conv_b", "dt_bias", "a_log", "d_skip", "ssm_norm_w", "w_attn", "w_ssm",
             "w_out", "final_norm_w"]
    gradv = dict(norm_w=g_nw, w_in=g_in, conv_w=g_cw, conv_b=g_cb, dt_bias=g_dtb, a_log=g_alog, d_skip=g_dsk,
                 ssm_norm_w=g_snw, w_attn=g_attn, w_ssm=g_ssm, w_out=g_out, final_norm_w=g_fnw)
    outs = [loss, grad_x[None]]
    outs += [gradv[n].reshape(shapes[n]) for n in order]
    for k in range(3):
        outs += [upd[n][k].reshape(shapes[n]) for n in order]
    return tuple(outs)
```

```python
import jax
import jax.numpy as jnp
from jax import lax
from jax.experimental import pallas as pl
from jax.experimental.pallas import tpu as pltpu

F32 = jnp.float32
BF16 = jnp.bfloat16
SDS = jax.ShapeDtypeStruct

RMS_EPS = 1e-6
LANES = 128
CHUNK = 128
SSM_HEAD_DIM = 64
SSM_GROUPS = 8
SSM_STATE = 128
CONV_K = 4
ATTN_HEAD_DIM = 128
DILATED_PATTERNS = ((128, 1), (512, 4), (2048, 16))
NEG = -1e30
VMEM_LIMIT = 56 * 1024 * 1024
ADAM_LR, ADAM_B1, ADAM_B2, ADAM_EPS, ADAM_WD, ADAM_STEP = 0.001, 0.9, 0.999, 1e-08, 0.01, 10
MESH = pl.DeviceIdType.MESH
N_CHIPS = 4
N_DEV = 8


class _Cfg:
    def __init__(self, s, d):
        self.S, self.D = s, d
        self.H = d // ATTN_HEAD_DIM
        self.SI = 2 * d
        self.NH = self.SI // SSM_HEAD_DIM
        self.HPG = self.NH // SSM_GROUPS
        self.GW = self.HPG * SSM_HEAD_DIM
        self.BC = SSM_GROUPS * SSM_STATE
        self.CD = self.SI + 2 * self.BC
        self.OQ, self.OK, self.OV, self.OZA = 0, d, 2 * d, 3 * d
        self.OZS = 4 * d
        self.OXBC = self.OZS + self.SI
        self.OGA = self.OXBC + self.CD
        self.OGS = self.OGA + d
        self.NM = self.OGS + d
        self.N_IN = self.NM + self.NH
        assert self.GW % LANES == 0 and self.NH <= LANES and s % 512 == 0 and d % 512 == 0


def _params(sem=None):
    return pltpu.CompilerParams(dimension_semantics=sem, vmem_limit_bytes=VMEM_LIMIT)


def _sigmoid(x):
    return 0.5 * jnp.tanh(0.5 * x) + 0.5


def _softplus(x):
    u = jnp.exp(-jnp.abs(x))
    l1p = jnp.where(u < 1e-3, u * (1.0 - u * (0.5 - u * (1.0 / 3.0))), jnp.log(1.0 + u))
    return jnp.maximum(x, 0.0) + l1p


def _nt(a, b):
    return lax.dot_general(a, b, (((1,), (1,)), ((), ())), preferred_element_type=F32)


def _tn(a, b):
    return lax.dot_general(a, b, (((0,), (0,)), ((), ())), preferred_element_type=F32)


def _nn(a, b):
    return jnp.dot(a, b, preferred_element_type=F32)


def _tile(n, target):
    if n <= target:
        return n
    best = None
    for t in range(LANES, target + 1, LANES):
        if n % t == 0:
            best = t
    assert best is not None, (n, target)
    return best


MM_TK = {"nn": 2048, "nt": 2048, "tn": 1024}


def _mm(a, b, dims, out_dtype, name, tm=1024, tn=2048, tk=None, init=None, carry=None):
    tk = MM_TK[dims] if tk is None else tk
    if dims == "nn":
        (m, k), (k2, n) = a.shape, b.shape
    elif dims == "nt":
        (m, k), (n, k2) = a.shape, b.shape
    else:
        (k, m), (k2, n) = a.shape, b.shape
    assert k == k2
    tm, tn, tk = _tile(m, tm), _tile(n, tn), _tile(k, tk)
    nk = k // tk
    if dims == "tn":
        a_spec = pl.BlockSpec((tk, tm), lambda i, j, kk: (kk, i))
    else:
        a_spec = pl.BlockSpec((tm, tk), lambda i, j, kk: (i, kk))
    if dims == "nt":
        b_spec = pl.BlockSpec((tn, tk), lambda i, j, kk: (j, kk))
    else:
        b_spec = pl.BlockSpec((tk, tn), lambda i, j, kk: (kk, j))
    o_spec = pl.BlockSpec((tm, tn), lambda i, j, kk: (i, j))
    op = {"nn": _nn, "nt": _nt, "tn": _tn}[dims]
    has_init = init is not None
    nx = len(carry.arrays) if carry is not None else 0
    ni, nj = m // tm, n // tn

    def body(*refs):
        a_ref, b_ref = refs[0], refs[1]
        i_ref = refs[2] if has_init else None
        x_in = refs[2 + has_init:2 + has_init + nx]
        o_ref = refs[2 + has_init + nx]
        x_out = refs[3 + has_init + nx:3 + has_init + 2 * nx]
        acc = refs[3 + has_init + 2 * nx]
        x_sems = refs[4 + has_init + 2 * nx:]
        i, j, kk = pl.program_id(0), pl.program_id(1), pl.program_id(2)

        if nx:
            @pl.when((i == 0) & (j == 0) & (kk == 0))
            def _():
                carry.start(x_in, x_out, x_sems)

        prod = lambda: op(a_ref[...], b_ref[...])
        with_init = (lambda p: p + i_ref[...].astype(F32)) if has_init else (lambda p: p)
        if nk == 1:
            o_ref[...] = with_init(prod()).astype(out_dtype)
        else:
            @pl.when(kk == 0)
            def _():
                acc[...] = with_init(prod())

            @pl.when((kk > 0) & (kk < nk - 1))
            def _():
                acc[...] += prod()

            @pl.when(kk == nk - 1)
            def _():
                o_ref[...] = (acc[...] + prod()).astype(out_dtype)

        if nx:
            @pl.when((i == ni - 1) & (j == nj - 1) & (kk == nk - 1))
            def _():
                carry.finish(x_in, x_out, x_sems)

    in_specs = [a_spec, b_spec] + ([o_spec] if has_init else []) + [HBM_SPEC] * nx
    args = (a, b) + ((init,) if has_init else ()) + (tuple(carry.arrays) if nx else ())
    sems = carry.sem_shapes() if nx else []
    outs = pl.pallas_call(
        body, out_shape=[SDS((m, n), out_dtype)] + (carry.out_shapes if nx else []), grid=(ni, nj, nk),
        in_specs=in_specs, out_specs=[o_spec] + [HBM_SPEC] * nx,
        scratch_shapes=[pltpu.VMEM((tm, tn) if nk > 1 else (8, LANES), F32)] + sems,
        compiler_params=_params(("arbitrary",) * 3 if nx else ("parallel", "parallel", "arbitrary")), name=name)(*args)
    return (outs[0], outs[1:]) if nx else outs[0]


def _rmsnorm_fwd(x, w, carry=None):
    s, d = x.shape
    tr = 256
    nsteps = s // tr
    nx = len(carry.arrays) if carry is not None else 0

    def body(*refs):
        x_ref, w_ref, x_in = refs[0], refs[1], refs[2:2 + nx]
        o_ref, x_out, x_sems = refs[2 + nx], refs[3 + nx:3 + 2 * nx], refs[3 + 2 * nx:]
        if nx:
            @pl.when(pl.program_id(0) == 0)
            def _():
                carry.start(x_in, x_out, x_sems)

        xv = x_ref[...]
        r = lax.rsqrt(jnp.mean(xv * xv, axis=-1, keepdims=True) + RMS_EPS)
        o_ref[...] = (xv * r * w_ref[...]).astype(BF16)

        if nx:
            @pl.when(pl.program_id(0) == nsteps - 1)
            def _():
                carry.finish(x_in, x_out, x_sems)

    outs = pl.pallas_call(
        body, out_shape=[SDS((s, d), BF16)] + (carry.out_shapes if nx else []), grid=(nsteps,),
        in_specs=[pl.BlockSpec((tr, d), lambda i: (i, 0)), pl.BlockSpec((1, d), lambda i: (0, 0))] + [HBM_SPEC] * nx,
        out_specs=[pl.BlockSpec((tr, d), lambda i: (i, 0))] + [HBM_SPEC] * nx,
        scratch_shapes=carry.sem_shapes() if nx else [],
        compiler_params=_params(("arbitrary",) if nx else ("parallel",)), name="rmsnorm_fwd")(
            x, w, *(carry.arrays if nx else []))
    return (outs[0], outs[1:]) if nx else outs[0]


def _rmsnorm_bwd(x, w, dhn_a, dhn_b, dout):
    s, d = x.shape
    tr = 256

    def body(x_ref, w_ref, dh_ref, dh2_ref, do_ref, gx_ref, gw_ref):
        xv = x_ref[...]
        r = lax.rsqrt(jnp.mean(xv * xv, axis=-1, keepdims=True) + RMS_EPS)
        nrm = xv * r
        dh = dh_ref[...] + dh2_ref[...]
        gy = dh * w_ref[...]
        gx_ref[...] = do_ref[...] + r * (gy - nrm * jnp.mean(gy * nrm, axis=-1, keepdims=True))

        @pl.when(pl.program_id(0) == 0)
        def _():
            gw_ref[...] = jnp.zeros_like(gw_ref)

        gw_ref[...] += jnp.sum(dh * nrm, axis=0, keepdims=True)

    blk = pl.BlockSpec((tr, d), lambda i: (i, 0))
    row = pl.BlockSpec((1, d), lambda i: (0, 0))
    return pl.pallas_call(
        body, out_shape=(SDS((s, d), F32), SDS((1, d), F32)), grid=(s // tr,),
        in_specs=[blk, row, blk, blk, blk], out_specs=(blk, row),
        compiler_params=_params(("arbitrary",)), name="rmsnorm_bwd")(x, w, dhn_a, dhn_b, dout)


DEINT = DILATED_PATTERNS[-1][1]
DEINT_ROWS = DEINT * LANES


class _Pass:
    def __init__(self, tq, patterns, unit, seg_len):
        self.tq, self.patterns, self.unit, self.seg_len = tq, patterns, unit, seg_len
        self.win = max(w for w, _ in patterns) // unit
        self.w = self.win + tq
        assert self.win % tq == 0


def _attn_tables(ps):
    i = jnp.arange(ps.tq, dtype=jnp.int32)[:, None]
    j = jnp.arange(ps.w, dtype=jnp.int32)[None, :]
    delta = (i + ps.win - j) * ps.unit
    n = jnp.zeros((ps.tq, ps.w), F32)
    for window, dil in ps.patterns:
        n = n + ((delta >= 0) & (delta <= window) & (delta % dil == 0)).astype(F32)
    logn = jnp.where(n > 0, jnp.log(jnp.maximum(n, 1.0)), NEG)
    return logn, jnp.maximum(delta, 0).astype(F32)


def _slopes(h):
    s = jnp.asarray([2.0 ** (-8.0 * (i + 1) / h) for i in range(h)], F32)
    return jnp.broadcast_to(s[:, None, None], (h, 1, LANES))


def _masked_logn(ps, logn_ref, start):
    col = lax.broadcasted_iota(jnp.int32, (ps.tq, ps.w), 1)
    return jnp.where(col >= ps.win - lax.rem(start, ps.seg_len), logn_ref[...], NEG)


def _head_cols(hh):
    return slice(hh * ATTN_HEAD_DIM, (hh + 1) * ATTN_HEAD_DIM)


def _head_window(refs, cs):
    return jnp.concatenate([r[:, cs] for r in refs], axis=0)


def _head_scores(q_ref, kw, cs, base, dist_ref, slope_ref, hh):
    return _nt(q_ref[:, cs], kw) * (ATTN_HEAD_DIM ** -0.5) + (base - slope_ref[hh][0:1, 0:1] * dist_ref[...])


def _lane_of(stat, hh):
    lane = lax.broadcasted_iota(jnp.int32, stat.shape, 1)
    return jnp.sum(jnp.where(lane == hh, stat, 0.0), axis=1, keepdims=True)


def _window_specs(ps, d, col, nb):
    nprev = ps.win // ps.tq
    return [pl.BlockSpec((ps.tq, d), lambda i, b=b: (jnp.maximum(jnp.minimum(i, nb - 1) - (nprev - b), 0), col))
            for b in range(nprev + 1)]


def _attn_fwd(cfg, ps, qkv, cols, tables, slopes, name):
    s, h, d = cfg.S, cfg.H, cfg.D
    tq, nw = ps.tq, ps.win // ps.tq + 1
    nb = s // tq
    logn, dist = tables
    qc, kc, vc = [c // d for c in cols]

    def body(*refs):
        q_ref, k_refs, v_refs = refs[0], refs[1:1 + nw], refs[1 + nw:1 + 2 * nw]
        logn_ref, dist_ref, slope_ref, o_ref, lse_ref = refs[1 + 2 * nw:]
        base = _masked_logn(ps, logn_ref, pl.program_id(0) * tq)
        lane = lax.broadcasted_iota(jnp.int32, (tq, LANES), 1)

        lse = jnp.zeros((tq, LANES), F32)
        for hh in range(h):
            cs = _head_cols(hh)
            sc = _head_scores(q_ref, _head_window(k_refs, cs), cs, base, dist_ref, slope_ref, hh)
            m = jnp.max(sc, axis=1, keepdims=True)
            p = jnp.exp(sc - m)
            l = jnp.sum(p, axis=1, keepdims=True)
            o_ref[:, cs] = (_nn(p.astype(BF16), _head_window(v_refs, cs)) / l).astype(BF16)
            lse = jnp.where(lane == hh, m + jnp.log(l), lse)
        lse_ref[...] = lse

    tab = pl.BlockSpec((tq, ps.w), lambda i: (0, 0))
    return pl.pallas_call(
        body, out_shape=(SDS((s, d), BF16), SDS((s, LANES), F32)), grid=(nb,),
        in_specs=[pl.BlockSpec((tq, d), lambda i: (i, qc))] + _window_specs(ps, d, kc, nb) + _window_specs(ps, d, vc, nb)
        + [tab, tab, pl.BlockSpec((h, 1, LANES), lambda i: (0, 0, 0))],
        out_specs=(pl.BlockSpec((tq, d), lambda i: (i, 0)), pl.BlockSpec((tq, LANES), lambda i: (i, 0))),
        compiler_params=_params(("parallel",)), name=name)(*([qkv] * (1 + 2 * nw)), logn, dist, slopes)


def _attn_bwd(cfg, ps, qkv, cols, do, lse, delta, tables, slopes, name):
    s, h, d = cfg.S, cfg.H, cfg.D
    tq, nprev = ps.tq, ps.win // ps.tq
    nw = nprev + 1
    nb = s // tq
    logn, dist = tables
    qc, kc, vc = [c // d for c in cols]
    scale = ATTN_HEAD_DIM ** -0.5

    def body(*refs):
        q_ref, k_refs, v_refs = refs[0], refs[1:1 + nw], refs[1 + nw:1 + 2 * nw]
        do_ref, lse_ref, dl_ref, logn_ref, dist_ref, slope_ref, dq_ref, dk_ref, dv_ref, ck, cv = refs[1 + 2 * nw:]
        i = pl.program_id(0)
        slot = lambda b: lax.rem(i + b, nprev)

        @pl.when(i == 0)
        def _():
            ck[...] = jnp.zeros_like(ck)
            cv[...] = jnp.zeros_like(cv)

        @pl.when(i < nb)
        def _():
            base = _masked_logn(ps, logn_ref, i * tq)
            lse_all, dl_all = lse_ref[...], dl_ref[...]

            for hh in range(h):
                cs = _head_cols(hh)
                kw, vw = _head_window(k_refs, cs), _head_window(v_refs, cs)
                sc = _head_scores(q_ref, kw, cs, base, dist_ref, slope_ref, hh)
                p = jnp.exp(sc - lse_all[:, hh:hh + 1])
                dob = do_ref[:, cs]
                ds = (p * (_nt(dob, vw) - dl_all[:, hh:hh + 1]) * scale).astype(BF16)
                dq_ref[:, cs] = _nn(ds, kw).astype(BF16)
                dkw = _tn(ds, q_ref[:, cs])
                dvw = _tn(p.astype(BF16), dob)
                dk_ref[:, cs] = ck[slot(0), :, cs] + dkw[0:tq]
                dv_ref[:, cs] = cv[slot(0), :, cs] + dvw[0:tq]
                for b in range(1, nprev):
                    ck[slot(b), :, cs] += dkw[b * tq:(b + 1) * tq]
                    cv[slot(b), :, cs] += dvw[b * tq:(b + 1) * tq]
                ck[slot(0), :, cs] = dkw[nprev * tq:]
                cv[slot(0), :, cs] = dvw[nprev * tq:]

        @pl.when(i >= nb)
        def _():
            dk_ref[...] = ck[slot(0)]
            dv_ref[...] = cv[slot(0)]

    here = lambda i: jnp.minimum(i, nb - 1)
    blk = pl.BlockSpec((tq, d), lambda i: (here(i), 0))
    stat = pl.BlockSpec((tq, LANES), lambda i: (here(i), 0))
    late = pl.BlockSpec((tq, d), lambda i: (jnp.maximum(i - nprev, 0), 0))
    tab = pl.BlockSpec((tq, ps.w), lambda i: (0, 0))
    return pl.pallas_call(
        body, out_shape=(SDS((s, d), BF16), SDS((s, d), F32), SDS((s, d), F32)), grid=(nb + nprev,),
        in_specs=[pl.BlockSpec((tq, d), lambda i: (here(i), qc))] + _window_specs(ps, d, kc, nb)
        + _window_specs(ps, d, vc, nb) + [blk, stat, stat, tab, tab, pl.BlockSpec((h, 1, LANES), lambda i: (0, 0, 0))],
        out_specs=(blk, late, late),
        scratch_shapes=[pltpu.VMEM((nprev, tq, d), F32), pltpu.VMEM((nprev, tq, d), F32)],
        compiler_params=_params(("arbitrary",)), name=name)(
            *([qkv] * (1 + 2 * nw)), do, lse, delta, logn, dist, slopes)


def _by_residue(a):
    return a.reshape(DEINT, a.shape[0] // DEINT, a.shape[1])


def _deint_spec(colblock):
    return pl.BlockSpec((DEINT, LANES, LANES), lambda b, j: (0, b, colblock(j)))


def _deint_rows(scr, out_ref, dtype):
    for r in range(DEINT):
        out_ref[r] = scr[pl.ds(r, LANES, stride=DEINT), :].astype(dtype)


def _int_rows(in_ref, scr):
    for r in range(DEINT):
        scr[pl.ds(r, LANES, stride=DEINT), :] = in_ref[r].astype(F32)


WIDE = 4 * LANES


def _wide_spec():
    return pl.BlockSpec((DEINT, LANES, WIDE), lambda b, j: (0, b, j))


def _deinterleave(x, col0, ncols, name):
    s = x.shape[0]
    c0 = col0 // WIDE

    def body(x_ref, o_ref, scr):
        for t in range(WIDE // LANES):
            cs = slice(t * LANES, (t + 1) * LANES)
            scr[t] = x_ref[:, cs].astype(F32)
            for r in range(DEINT):
                o_ref[r, :, cs] = scr.at[t][pl.ds(r, LANES, stride=DEINT), :].astype(x.dtype)

    out = pl.pallas_call(
        body, out_shape=SDS((DEINT, s // DEINT, ncols), x.dtype), grid=(s // DEINT_ROWS, ncols // WIDE),
        in_specs=[pl.BlockSpec((DEINT_ROWS, WIDE), lambda b, j: (b, c0 + j))],
        out_specs=_wide_spec(),
        scratch_shapes=[pltpu.VMEM((WIDE // LANES, DEINT_ROWS, LANES), F32)],
        compiler_params=_params(("parallel", "parallel")), name=name)(x)
    return out.reshape(s, ncols)


def _attn_merge(cfg, proj, o_1, lse_1, o_2, lse_2):
    s, h = cfg.S, cfg.H
    zb = cfg.OZA // WIDE
    rows = DEINT_ROWS
    hps = WIDE // LANES

    def body(o1_ref, l1_ref, o2_ref, l2_ref, z_ref, o_ref, og_ref, lse_ref, so, sl):
        j = pl.program_id(1)

        @pl.when(j == 0)
        def _():
            _int_rows(l2_ref, sl)
            lse_ref[...] = jnp.zeros_like(lse_ref)

        l1_all, l2_all = l1_ref[...], sl[...]
        lane = lax.broadcasted_iota(jnp.int32, (rows, LANES), 1)
        lse = lse_ref[...]
        for t in range(hps):
            hh = j * hps + t
            cs = slice(t * LANES, (t + 1) * LANES)
            for r in range(DEINT):
                so.at[t][pl.ds(r, LANES, stride=DEINT), :] = o2_ref[r, :, cs].astype(F32)
            l1, l2 = _lane_of(l1_all, hh), _lane_of(l2_all, hh)
            mx = jnp.maximum(l1, l2)
            w1, w2 = jnp.exp(l1 - mx), jnp.exp(l2 - mx)
            den = w1 + w2
            o = (w1 * o1_ref[:, cs].astype(F32) + w2 * so[t]) / den
            z = z_ref[:, cs].astype(F32)
            o_ref[:, cs] = o.astype(BF16)
            og_ref[:, cs] = (o * (z * _sigmoid(z))).astype(BF16)
            lse = jnp.where(lane == hh, mx + jnp.log(den), lse)
        lse_ref[...] = lse

    blk = pl.BlockSpec((rows, WIDE), lambda b, j: (b, j))
    stat = pl.BlockSpec((rows, LANES), lambda b, j: (b, 0))
    return pl.pallas_call(
        body, out_shape=(SDS((s, cfg.D), BF16), SDS((s, cfg.D), BF16), SDS((s, LANES), F32)),
        grid=(s // rows, h // hps),
        in_specs=[blk, stat, _wide_spec(), _deint_spec(lambda j: 0), pl.BlockSpec((rows, WIDE), lambda b, j: (b, zb + j))],
        out_specs=(blk, blk, stat),
        scratch_shapes=[pltpu.VMEM((hps, rows, LANES), F32), pltpu.VMEM((rows, LANES), F32)],
        compiler_params=_params(("parallel", "arbitrary")), name="attn_merge")(
            o_1, lse_1, _by_residue(o_2), _by_residue(lse_2), proj)


def _attn_bwd_prep(cfg, proj, o_a, doag, lse, dproj):
    s, h = cfg.S, cfg.H
    zb = cfg.OZA // WIDE
    rows = DEINT_ROWS
    hps = WIDE // LANES

    def body(o_ref, dg_ref, z_ref, lse_ref, dp_in, dz_ref, do_ref, do2_ref, dl_ref, dl2_ref, lse2_ref, scr):
        del dp_in
        j = pl.program_id(1)

        @pl.when(j == 0)
        def _():
            dl_ref[...] = jnp.zeros_like(dl_ref)

        lane = lax.broadcasted_iota(jnp.int32, (rows, LANES), 1)
        dl = dl_ref[...]
        for t in range(hps):
            cs = slice(t * LANES, (t + 1) * LANES)
            z = z_ref[:, cs].astype(F32)
            sg = _sigmoid(z)
            o = o_ref[:, cs].astype(F32)
            dg = dg_ref[:, cs].astype(F32)
            do = dg * (z * sg)
            dz_ref[:, cs] = (dg * o * (sg * (1.0 + z * (1.0 - sg)))).astype(BF16)
            do_ref[:, cs] = do.astype(BF16)
            scr[...] = do
            for r in range(DEINT):
                do2_ref[r, :, cs] = scr[pl.ds(r, LANES, stride=DEINT), :].astype(BF16)
            dl = jnp.where(lane == j * hps + t, jnp.sum(do * o, axis=1, keepdims=True), dl)
        dl_ref[...] = dl

        @pl.when(j == h // hps - 1)
        def _():
            scr[...] = dl
            _deint_rows(scr, dl2_ref, F32)
            scr[...] = lse_ref[...]
            _deint_rows(scr, lse2_ref, F32)

    blk = pl.BlockSpec((rows, WIDE), lambda b, j: (b, j))
    stat = pl.BlockSpec((rows, LANES), lambda b, j: (b, 0))
    stat2 = _deint_spec(lambda j: 0)
    outs = pl.pallas_call(
        body,
        out_shape=(SDS(dproj.shape, BF16), SDS((s, cfg.D), BF16), SDS((DEINT, s // DEINT, cfg.D), BF16),
                   SDS((s, LANES), F32), SDS((DEINT, s // DEINT, LANES), F32), SDS((DEINT, s // DEINT, LANES), F32)),
        grid=(s // rows, h // hps),
        in_specs=[blk, blk, pl.BlockSpec((rows, WIDE), lambda b, j: (b, zb + j)), stat, HBM_SPEC],
        out_specs=(pl.BlockSpec((rows, WIDE), lambda b, j: (b, zb + j)), blk, _wide_spec(), stat, stat2, stat2),
        scratch_shapes=[pltpu.VMEM((rows, LANES), F32)],
        input_output_aliases={4: 0},
        compiler_params=_params(("parallel", "arbitrary")), name="attn_bwd_prep")(o_a, doag, proj, lse, dproj)
    dproj, do, do2, dl, dl2, lse2 = outs
    return dproj, do, do2.reshape(s, cfg.D), dl, dl2.reshape(s, LANES), lse2.reshape(s, LANES)


def _attn_grad_sum(cfg, g_1, g_2, col0, dproj, name):
    s = cfg.S
    c0 = col0 // WIDE
    rows = DEINT_ROWS

    def body(g1_ref, g2_ref, dp_in, o_ref, scr):
        del dp_in
        for t in range(WIDE // LANES):
            cs = slice(t * LANES, (t + 1) * LANES)
            for r in range(DEINT):
                scr.at[t][pl.ds(r, LANES, stride=DEINT), :] = g2_ref[r, :, cs].astype(F32)
            o_ref[:, cs] = (g1_ref[:, cs].astype(F32) + scr[t]).astype(BF16)

    return pl.pallas_call(
        body, out_shape=SDS(dproj.shape, BF16), grid=(s // rows, cfg.D // WIDE),
        in_specs=[pl.BlockSpec((rows, WIDE), lambda b, j: (b, j)), _wide_spec(), HBM_SPEC],
        out_specs=pl.BlockSpec((rows, WIDE), lambda b, j: (b, c0 + j)),
        scratch_shapes=[pltpu.VMEM((WIDE // LANES, rows, LANES), F32)],
        input_output_aliases={2: 0},
        compiler_params=_params(("parallel", "parallel")), name=name)(g_1, _by_residue(g_2), dproj)


CONV_HALO = 16
CONV_TR = 512
CONV_CW = 1024


def _rows_back(a, n):
    return a if n == 0 else pltpu.roll(a, n % a.shape[0], axis=0)


def _conv_fwd(cfg, proj, conv_w, conv_b):
    s, cd = cfg.S, cfg.CD
    tr, cw, hl = CONV_TR, CONV_CW, CONV_HALO
    cb0 = cfg.OXBC // cw

    def body(x_ref, h_ref, w_ref, b_ref, o_ref):
        i = pl.program_id(0)
        halo = jnp.where(i > 0, h_ref[...].astype(F32), 0.0)
        ext = jnp.concatenate([halo, x_ref[...].astype(F32)], axis=0)
        pre = b_ref[...] + jnp.zeros((tr, cw), F32)
        for k in range(CONV_K):
            pre = pre + w_ref[k:k + 1, :] * _rows_back(ext, CONV_K - 1 - k)[hl:]
        o_ref[...] = (pre * _sigmoid(pre)).astype(BF16)

    return pl.pallas_call(
        body, out_shape=SDS((s, cd), BF16), grid=(s // tr, cd // cw),
        in_specs=[pl.BlockSpec((tr, cw), lambda i, j: (i, cb0 + j)),
                  pl.BlockSpec((hl, cw), lambda i, j: (jnp.maximum(i * (tr // hl) - 1, 0), cb0 + j)),
                  pl.BlockSpec((CONV_K, cw), lambda i, j: (0, j)),
                  pl.BlockSpec((1, cw), lambda i, j: (0, j))],
        out_specs=pl.BlockSpec((tr, cw), lambda i, j: (i, j)),
        compiler_params=_params(("parallel", "parallel")), name="conv_fwd")(proj, proj, conv_w, conv_b)


def _conv_bwd(cfg, proj, dact, conv_w, conv_b, dproj):
    s, cd = cfg.S, cfg.CD
    tr, cw, hl = CONV_TR, CONV_CW, CONV_HALO
    cb0 = cfg.OXBC // cw
    nr = s // tr
    last_h = s // hl - 1

    def body(x_ref, hp_ref, hn_ref, d_ref, dn_ref, w_ref, b_ref, dp_in, dx_ref, gw_ref, gb_ref):
        del dp_in
        i = pl.program_id(1)
        ext = jnp.concatenate([jnp.where(i > 0, hp_ref[...].astype(F32), 0.0), x_ref[...].astype(F32),
                               hn_ref[...].astype(F32)], axis=0)
        shifted = [_rows_back(ext, CONV_K - 1 - k)[hl:] for k in range(CONV_K)]
        pre = b_ref[...] + jnp.zeros((tr + hl, cw), F32)
        for k in range(CONV_K):
            pre = pre + w_ref[k:k + 1, :] * shifted[k]
        sg = _sigmoid(pre)
        dact = jnp.concatenate([d_ref[...].astype(F32), jnp.where(i < nr - 1, dn_ref[...].astype(F32), 0.0)], axis=0)
        dpre = dact * (sg * (1.0 + pre * (1.0 - sg)))
        dx = jnp.zeros((tr, cw), F32)
        for k in range(CONV_K):
            dx = dx + w_ref[k:k + 1, :] * _rows_back(dpre, -(CONV_K - 1 - k))[0:tr]
        dx_ref[...] = dx.astype(BF16)

        @pl.when(i == 0)
        def _():
            gw_ref[...] = jnp.zeros_like(gw_ref)
            gb_ref[...] = jnp.zeros_like(gb_ref)

        dcur = dpre[0:tr]
        gb_ref[...] += jnp.sum(dcur, axis=0, keepdims=True)
        for k in range(CONV_K):
            gw_ref[k:k + 1, :] += jnp.sum(dcur * shifted[k][0:tr], axis=0, keepdims=True)

    return pl.pallas_call(
        body, out_shape=(SDS(dproj.shape, BF16), SDS((CONV_K, cd), F32), SDS((1, cd), F32)), grid=(cd // cw, nr),
        in_specs=[pl.BlockSpec((tr, cw), lambda j, i: (i, cb0 + j)),
                  pl.BlockSpec((hl, cw), lambda j, i: (jnp.maximum(i * (tr // hl) - 1, 0), cb0 + j)),
                  pl.BlockSpec((hl, cw), lambda j, i: (jnp.minimum((i + 1) * (tr // hl), last_h), cb0 + j)),
                  pl.BlockSpec((tr, cw), lambda j, i: (i, j)),
                  pl.BlockSpec((hl, cw), lambda j, i: (jnp.minimum((i + 1) * (tr // hl), last_h), j)),
                  pl.BlockSpec((CONV_K, cw), lambda j, i: (0, j)),
                  pl.BlockSpec((1, cw), lambda j, i: (0, j)),
                  pl.BlockSpec(memory_space=pl.ANY)],
        out_specs=(pl.BlockSpec((tr, cw), lambda j, i: (i, cb0 + j)),
                   pl.BlockSpec((CONV_K, cw), lambda j, i: (0, j)),
                   pl.BlockSpec((1, cw), lambda j, i: (0, j))),
        input_output_aliases={7: 0},
        compiler_params=_params(("parallel", "arbitrary")), name="conv_bwd")(
            proj, proj, proj, dact, dact, conv_w, conv_b, dproj)


def _expand(v, e, terms):
    out, rem = None, v
    for _ in range(terms):
        hi = rem.astype(BF16)
        t = _nn(hi, e)
        out = t if out is None else out + t
        rem = rem - hi.astype(F32)
    return out


def _segsum(v, e, terms):
    out, rem = None, v
    for _ in range(terms):
        hi = rem.astype(BF16)
        t = _nt(hi, e)
        out = t if out is None else out + t
        rem = rem - hi.astype(F32)
    return out


def _expand_row(row, e, terms):
    return _expand(jnp.broadcast_to(row, (8, LANES)), e, terms)[0:1]


def _segsum_row(row, e, terms):
    return _segsum(jnp.broadcast_to(row, (8, row.shape[1])), e, terms)[0:1]


def _expansion_matrix(cfg):
    hh = jnp.arange(LANES, dtype=jnp.int32)[:, None]
    cc = jnp.arange(cfg.SI, dtype=jnp.int32)[None, :]
    return (cc // SSM_HEAD_DIM == hh).astype(BF16)


def _tri(lower):
    r = lax.broadcasted_iota(jnp.int32, (CHUNK, CHUNK), 0)
    c = lax.broadcasted_iota(jnp.int32, (CHUNK, CHUNK), 1)
    return (c <= r) if lower else (c >= r)


def _ssd_prep(dtr_ref, db_ref, al_ref, e):
    dtr = dtr_ref[...] + db_ref[...]
    dt = _softplus(dtr)
    a = -jnp.exp(al_ref[...])
    acum = jnp.dot(_tri(True).astype(F32), dt * a, precision=lax.Precision.HIGHEST, preferred_element_type=F32)
    return dtr, dt, a, _expand(dt, e, 2), _expand(acum, e, 3)


def _ssd_fwd(cfg, xact, dt_raw, proj, dt_bias, a_log, d_skip, norm_w, e):
    s, si, cd, gw, bc = cfg.S, cfg.SI, cfg.CD, cfg.GW, cfg.BC
    nc = s // CHUNK
    zb = cfg.OZS // si
    tiles = gw // LANES

    def body(xa_ref, dtr_ref, z_ref, db_ref, al_ref, dsk_ref, nw_ref, e_ref, y_ref, y2_ref, st_ref,
             state, ybuf, x_s, xw_s, ae_s, ea_s, lam_s):
        @pl.when(pl.program_id(0) == 0)
        def _():
            state[...] = jnp.zeros_like(state)

        st_ref[...] = state[...]
        ev = e_ref[...]
        _, _, _, dt_e, a_e = _ssd_prep(dtr_ref, db_ref, al_ref, ev)
        xs = xa_ref[:, 0:si].astype(F32)
        x = xs * dt_e
        lam_e = a_e[CHUNK - 1:CHUNK, :]
        x_s[...] = x.astype(BF16)
        xw_s[...] = (x * jnp.exp(lam_e - a_e)).astype(BF16)
        ae_s[...] = a_e
        ea_s[...] = jnp.exp(a_e)
        ybuf[...] = _expand_row(dsk_ref[...], ev, 3) * xs
        lam_s[...] = jnp.broadcast_to(jnp.exp(lam_e), (8, si))
        tril = _tri(True)
        lane = lax.broadcasted_iota(jnp.int32, (CHUNK, LANES), 1)

        def group(g, carry):
            co = pl.multiple_of(g * gw, LANES)
            bg = xa_ref[:, pl.ds(pl.multiple_of(si + g * SSM_STATE, LANES), SSM_STATE)]
            cg = xa_ref[:, pl.ds(pl.multiple_of(si + bc + g * SSM_STATE, LANES), SSM_STATE)]
            cbm = _nt(cg, bg)
            st = state[:, pl.ds(co, gw)]
            yoff = _nn(cg, st.astype(BF16)) * ea_s[:, pl.ds(co, gw)]
            for k in range(tiles):
                tc = pl.multiple_of(co + k * LANES, LANES)
                at = ae_s[:, pl.ds(tc, LANES)]
                att = at.T
                xt = x_s[:, pl.ds(tc, LANES)]
                acc = yoff[:, k * LANES:(k + 1) * LANES]
                for half in range(2):
                    lo = half * SSM_HEAD_DIM
                    seg = at[:, lo:lo + 1] - att[lo:lo + 1, :]
                    dec = jnp.exp(jnp.where(tril, seg, NEG))
                    xh = jnp.where((lane >= lo) & (lane < lo + SSM_HEAD_DIM), xt, jnp.zeros_like(xt))
                    acc = acc + _nn((cbm * dec).astype(BF16), xh)
                ybuf[:, pl.ds(tc, LANES)] += acc
            state[:, pl.ds(co, gw)] = st * lam_s[0:1, pl.ds(co, gw)] + _tn(bg, xw_s[:, pl.ds(co, gw)])
            return carry

        lax.fori_loop(0, SSM_GROUPS, group, 0)
        y = ybuf[...]
        y_ref[...] = y.astype(BF16)
        z = z_ref[...].astype(F32)
        u = y * (z * _sigmoid(z))
        r = lax.rsqrt(jnp.mean(u * u, axis=-1, keepdims=True) + RMS_EPS)
        y2_ref[...] = (u * r * nw_ref[...]).astype(BF16)

    row = lambda n: pl.BlockSpec((1, n), lambda c: (0, 0))
    return pl.pallas_call(
        body,
        out_shape=(SDS((s, si), BF16), SDS((s, si), BF16), SDS((nc, SSM_STATE, si), F32)),
        grid=(nc,),
        in_specs=[pl.BlockSpec((CHUNK, cd), lambda c: (c, 0)),
                  pl.BlockSpec((CHUNK, LANES), lambda c: (c, 0)),
                  pl.BlockSpec((CHUNK, si), lambda c: (c, zb)),
                  row(LANES), row(LANES), row(LANES), row(si),
                  pl.BlockSpec((LANES, si), lambda c: (0, 0))],
        out_specs=(pl.BlockSpec((CHUNK, si), lambda c: (c, 0)),
                   pl.BlockSpec((CHUNK, si), lambda c: (c, 0)),
                   pl.BlockSpec((None, SSM_STATE, si), lambda c: (c, 0, 0))),
        scratch_shapes=[pltpu.VMEM((SSM_STATE, si), F32), pltpu.VMEM((CHUNK, si), F32),
                        pltpu.VMEM((CHUNK, si), BF16), pltpu.VMEM((CHUNK, si), BF16),
                        pltpu.VMEM((CHUNK, si), F32), pltpu.VMEM((CHUNK, si), F32),
                        pltpu.VMEM((8, si), F32)],
        compiler_params=_params(("arbitrary",)), name="ssd_fwd")(
            xact, dt_raw, proj, dt_bias, a_log, d_skip, norm_w, e)


def _ssd_bwd(cfg, xact, dt_raw, proj, y, dy2, states, dt_bias, a_log, d_skip, norm_w, e, dproj):
    s, si, cd, gw, bc, hpg = cfg.S, cfg.SI, cfg.CD, cfg.GW, cfg.BC, cfg.HPG
    nc = s // CHUNK
    zb = cfg.OZS // si
    tiles = gw // LANES

    def body(xa_ref, dtr_ref, z_ref, y_ref, d2_ref, st_ref, db_ref, al_ref, dsk_ref, nw_ref, e_ref, dp_in,
             dz_ref, dxa_ref, ddt_ref, gnw_ref, gdb_ref, gal_ref, gds_ref,
             dh, dhn, xs_s, x_s, w_s, ae_s, ea_s, g_s, dx_s, dae_s, r_s, lam_s, dle_s):
        del dp_in

        @pl.when(pl.program_id(0) == 0)
        def _():
            dh[...] = jnp.zeros_like(dh)
            gnw_ref[...] = jnp.zeros_like(gnw_ref)
            gdb_ref[...] = jnp.zeros_like(gdb_ref)
            gal_ref[...] = jnp.zeros_like(gal_ref)
            gds_ref[...] = jnp.zeros_like(gds_ref)

        ev = e_ref[...]
        yv = y_ref[...].astype(F32)
        z = z_ref[...].astype(F32)
        sg = _sigmoid(z)
        sz = z * sg
        u = yv * sz
        r = lax.rsqrt(jnp.mean(u * u, axis=-1, keepdims=True) + RMS_EPS)
        nrm = u * r
        d2 = d2_ref[...].astype(F32)
        gnw_ref[...] += jnp.sum(d2 * nrm, axis=0, keepdims=True)
        gn = d2 * nw_ref[...]
        du = r * (gn - nrm * jnp.mean(gn * nrm, axis=-1, keepdims=True))
        gv = du * sz
        dz_ref[...] = (du * yv * (sg * (1.0 + z * (1.0 - sg)))).astype(BF16)
        g_s[...] = gv

        dtr, dt, a, dt_e, a_e = _ssd_prep(dtr_ref, db_ref, al_ref, ev)
        xs = xa_ref[:, 0:si].astype(F32)
        x = xs * dt_e
        lam_e = a_e[CHUNK - 1:CHUNK, :]
        xs_s[...] = xs
        x_s[...] = x
        w_s[...] = jnp.exp(lam_e - a_e)
        ae_s[...] = a_e
        ea_s[...] = jnp.exp(a_e)
        lam_s[...] = jnp.broadcast_to(jnp.exp(lam_e), (8, si))
        gds_ref[...] += _segsum_row(jnp.sum(gv * xs, axis=0, keepdims=True), ev, 2)
        r_s[...] = jnp.zeros_like(r_s)
        tril = _tri(True)
        lane = lax.broadcasted_iota(jnp.int32, (CHUNK, LANES), 1)
        sub = lax.broadcasted_iota(jnp.int32, (CHUNK, LANES), 0)

        def group(g, carry):
            co = pl.multiple_of(g * gw, LANES)
            bo = pl.multiple_of(si + g * SSM_STATE, LANES)
            cof = pl.multiple_of(si + bc + g * SSM_STATE, LANES)
            cols = pl.ds(co, gw)
            bg = xa_ref[:, pl.ds(bo, SSM_STATE)]
            cg = xa_ref[:, pl.ds(cof, SSM_STATE)]
            cbm = _nt(cg, bg)
            st = st_ref[:, cols]
            stb = st.astype(BF16)
            dho = dh[:, cols]
            dhob = dho.astype(BF16)
            ea = ea_s[:, cols]
            gg = g_s[:, cols]
            xg = x_s[:, cols]
            wg = w_s[:, cols]
            explam = lam_s[0:1, cols]
            yoff = _nn(cg, stb) * ea
            ga = (gg * ea).astype(BF16)
            dc = _nt(ga, stb)
            dhn[:, cols] = dho * explam + _tn(cg, ga)
            bdh = _nn(bg, dhob)
            db = _nt((xg * wg).astype(BF16), dhob)
            t = xg * bdh * wg
            dle_s[0:1, cols] = jnp.sum(t, axis=0, keepdims=True) + explam * jnp.sum(dho * st, axis=0, keepdims=True)
            dae_base = gg * yoff - t
            dxw = wg * bdh
            dcb = jnp.zeros((CHUNK, CHUNK), F32)
            for k in range(tiles):
                tc = pl.multiple_of(co + k * LANES, LANES)
                ksl = slice(k * LANES, (k + 1) * LANES)
                at = ae_s[:, pl.ds(tc, LANES)]
                att = at.T
                xt = xg[:, ksl].astype(BF16)
                gt = gg[:, ksl].astype(BF16)
                dxt = dxw[:, ksl]
                place = jnp.zeros((CHUNK, LANES), F32)
                for half in range(2):
                    lo = half * SSM_HEAD_DIM
                    seg = at[:, lo:lo + 1] - att[lo:lo + 1, :]
                    dec = jnp.exp(jnp.where(tril, seg, NEG))
                    mh = cbm * dec
                    gh = jnp.where((lane >= lo) & (lane < lo + SSM_HEAD_DIM), gt, jnp.zeros_like(gt))
                    dm = _nt(gh, xt)
                    dxt = dxt + _tn(mh.astype(BF16), gh)
                    dcb = dcb + dm * dec
                    dseg = dm * mh
                    place = place + jnp.where(lane == lo, jnp.sum(dseg, axis=1, keepdims=True), 0.0)
                    hidx = g * hpg + 2 * k + half
                    r_s[...] += jnp.where(sub == hidx, jnp.sum(dseg, axis=0, keepdims=True), 0.0)
                dx_s[:, pl.ds(tc, LANES)] = dxt
                dae_s[:, pl.ds(tc, LANES)] = dae_base[:, ksl] + place
            dcbb = dcb.astype(BF16)
            dxa_ref[:, pl.ds(bo, SSM_STATE)] = (db + _tn(dcbb, cg)).astype(BF16)
            dxa_ref[:, pl.ds(cof, SSM_STATE)] = (dc + _nn(dcbb, bg)).astype(BF16)
            return carry

        lax.fori_loop(0, SSM_GROUPS, group, 0)
        dlam = _segsum_row(dle_s[0:1, :], ev, 2)
        da_ = _segsum(dae_s[...], ev, 2) - r_s[...].T
        da_ = da_ + jnp.where(sub == CHUNK - 1, dlam, 0.0)
        dda = jnp.dot(_tri(False).astype(F32), da_, precision=lax.Precision.HIGHEST, preferred_element_type=F32)
        dxv = dx_s[...]
        xs = xs_s[...]
        ddt = dda * a + _segsum(dxv * xs, ev, 2)
        gal_ref[...] += jnp.sum(dda * dt, axis=0, keepdims=True) * a
        ddtr = ddt * _sigmoid(dtr)
        gdb_ref[...] += jnp.sum(ddtr, axis=0, keepdims=True)
        ddt_ref[...] = ddtr
        dxa_ref[:, 0:si] = (dxv * dt_e + g_s[...] * _expand_row(dsk_ref[...], ev, 3)).astype(BF16)
        dh[...] = dhn[...]

    rev = lambda c: nc - 1 - c
    row = lambda n: pl.BlockSpec((1, n), lambda c: (0, 0))
    big = lambda: pltpu.VMEM((CHUNK, si), F32)
    return pl.pallas_call(
        body,
        out_shape=(SDS(dproj.shape, BF16), SDS((s, cd), BF16), SDS((s, LANES), F32),
                   SDS((1, si), F32), SDS((1, LANES), F32), SDS((1, LANES), F32), SDS((1, LANES), F32)),
        grid=(nc,),
        in_specs=[pl.BlockSpec((CHUNK, cd), lambda c: (rev(c), 0)),
                  pl.BlockSpec((CHUNK, LANES), lambda c: (rev(c), 0)),
                  pl.BlockSpec((CHUNK, si), lambda c: (rev(c), zb)),
                  pl.BlockSpec((CHUNK, si), lambda c: (rev(c), 0)),
                  pl.BlockSpec((CHUNK, si), lambda c: (rev(c), 0)),
                  pl.BlockSpec((None, SSM_STATE, si), lambda c: (rev(c), 0, 0)),
                  row(LANES), row(LANES), row(LANES), row(si),
                  pl.BlockSpec((LANES, si), lambda c: (0, 0)),
                  pl.BlockSpec(memory_space=pl.ANY)],
        out_specs=(pl.BlockSpec((CHUNK, si), lambda c: (rev(c), zb)),
                   pl.BlockSpec((CHUNK, cd), lambda c: (rev(c), 0)),
                   pl.BlockSpec((CHUNK, LANES), lambda c: (rev(c), 0)),
                   row(si), row(LANES), row(LANES), row(LANES)),
        scratch_shapes=[pltpu.VMEM((SSM_STATE, si), F32), pltpu.VMEM((SSM_STATE, si), F32),
                        big(), big(), big(), big(), big(), big(), big(), big(),
                        pltpu.VMEM((CHUNK, LANES), F32), pltpu.VMEM((8, si), F32), pltpu.VMEM((8, si), F32)],
        input_output_aliases={11: 0},
        compiler_params=_params(("arbitrary",)), name="ssd_bwd")(
            xact, dt_raw, proj, y, dy2, states, dt_bias, a_log, d_skip, norm_w, e, dproj)


MERGE_TR = 512
MERGE_CW = 2048


def _merge_fwd(cfg, proj, a_br, s_br):
    s, d = cfg.S, cfg.D
    tr, cw = MERGE_TR, min(MERGE_CW, d)
    ga0, gs0 = cfg.OGA // cw, cfg.OGS // cw

    def body(ga_ref, gs_ref, a_ref, s_ref, o_ref):
        o_ref[...] = (_sigmoid(ga_ref[...].astype(F32)) * a_ref[...].astype(F32)
                      + _sigmoid(gs_ref[...].astype(F32)) * s_ref[...].astype(F32)).astype(BF16)

    blk = pl.BlockSpec((tr, cw), lambda i, j: (i, j))
    return pl.pallas_call(
        body, out_shape=SDS((s, d), BF16), grid=(s // tr, d // cw),
        in_specs=[pl.BlockSpec((tr, cw), lambda i, j: (i, ga0 + j)),
                  pl.BlockSpec((tr, cw), lambda i, j: (i, gs0 + j)), blk, blk],
        out_specs=blk, compiler_params=_params(("parallel", "parallel")), name="merge_fwd")(proj, proj, a_br, s_br)


def _merge_bwd(cfg, proj, branch, dmerged, gate_off, dproj, name):
    s, d = cfg.S, cfg.D
    tr, cw = MERGE_TR, min(MERGE_CW, d)
    g0 = gate_off // cw
    fresh = dproj is None

    def body(*refs):
        g_ref, b_ref, dm_ref = refs[:3]
        dg_ref, db_ref = refs[-2:]
        dm = dm_ref[...].astype(F32)
        sg = _sigmoid(g_ref[...].astype(F32))
        db_ref[...] = (dm * sg).astype(BF16)
        dg_ref[...] = (dm * b_ref[...].astype(F32) * sg * (1.0 - sg)).astype(BF16)

    blk = pl.BlockSpec((tr, cw), lambda i, j: (i, j))
    gate = pl.BlockSpec((tr, cw), lambda i, j: (i, g0 + j))
    return pl.pallas_call(
        body, out_shape=(SDS((s, cfg.NM), BF16), SDS((s, d), BF16)), grid=(s // tr, d // cw),
        in_specs=[gate, blk, blk] + ([] if fresh else [HBM_SPEC]),
        out_specs=(gate, blk),
        input_output_aliases={} if fresh else {3: 0},
        compiler_params=_params(("parallel", "parallel")), name=name)(
            *((proj, branch, dmerged) + (() if fresh else (dproj,))))


def _outproj_loss(merged, w_out, x, target, fnw):
    s, d = x.shape
    tr = 256

    def body(m_ref, w_ref, x_ref, t_ref, fw_ref, dof_ref, dob_ref, loss_ref, g_ref):
        out = x_ref[...] + _nn(m_ref[...], w_ref[...])
        r = lax.rsqrt(jnp.mean(out * out, axis=-1, keepdims=True) + RMS_EPS)
        nrm = out * r
        fw = fw_ref[...]
        err = nrm * fw - t_ref[...]
        dy = err * (1.0 / d)
        gy = dy * fw
        dout = r * (gy - nrm * jnp.mean(gy * nrm, axis=-1, keepdims=True))
        dof_ref[...] = dout
        dob_ref[...] = dout.astype(BF16)

        @pl.when(pl.program_id(0) == 0)
        def _():
            loss_ref[...] = jnp.zeros_like(loss_ref)
            g_ref[...] = jnp.zeros_like(g_ref)

        loss_ref[...] += jnp.sum(jnp.sum(err * err, axis=1, keepdims=True), axis=0, keepdims=True) * (0.5 / d)
        g_ref[...] += jnp.sum(dy * nrm, axis=0, keepdims=True)

    blk = pl.BlockSpec((tr, d), lambda i: (i, 0))
    return pl.pallas_call(
        body, out_shape=(SDS((s, d), F32), SDS((s, d), BF16), SDS((1, LANES), F32), SDS((1, d), F32)), grid=(s // tr,),
        in_specs=[blk, pl.BlockSpec((d, d), lambda i: (0, 0)), blk, blk, pl.BlockSpec((1, d), lambda i: (0, 0))],
        out_specs=(blk, blk, pl.BlockSpec((1, LANES), lambda i: (0, 0)), pl.BlockSpec((1, d), lambda i: (0, 0))),
        compiler_params=_params(("arbitrary",)), name="outproj_loss")(merged, w_out, x, target, fnw)


ELEMWISE_BLOCK_BYTES = 1 << 20


def _row_block(rows, cols, itemsize=4):
    best = None
    for tr in range(16, rows + 1, 16):
        if rows % tr == 0 and tr * cols * itemsize <= ELEMWISE_BLOCK_BYTES:
            best = tr
    return best if best is not None else rows


def _adamw(w, g, m, v, name):
    rows, cols = w.shape
    tr = _row_block(rows, cols)
    if rows // tr > 64 and cols % LANES == 0:
        blk, grid = pl.BlockSpec((rows, LANES), lambda i: (0, i)), (cols // LANES,)
    else:
        blk, grid = pl.BlockSpec((tr, cols), lambda i: (i, 0)), (rows // tr,)
    out = SDS((rows, cols), F32)
    return pl.pallas_call(
        _adamw_body(), out_shape=(out, out, out), grid=grid, in_specs=[blk] * 4, out_specs=(blk,) * 3,
        compiler_params=_params(("parallel",)), name=name)(w, g, m, v)


def _adamw_body():
    def body(w_ref, g_ref, m_ref, v_ref, d_ref, nm_ref, nv_ref):
        gv = g_ref[...]
        nm = ADAM_B1 * m_ref[...] + (1.0 - ADAM_B1) * gv
        nv = ADAM_B2 * v_ref[...] + (1.0 - ADAM_B2) * jnp.square(gv)
        m_hat = nm / (1.0 - ADAM_B1 ** ADAM_STEP)
        v_hat = nv / (1.0 - ADAM_B2 ** ADAM_STEP)
        d_ref[...] = -ADAM_LR * (m_hat / (jnp.sqrt(v_hat) + ADAM_EPS) + ADAM_WD * w_ref[...])
        nm_ref[...] = nm
        nv_ref[...] = nv

    return body


HBM_SPEC = pl.BlockSpec(memory_space=pl.ANY)


def _position():
    return lax.axis_index("x"), lax.axis_index("y"), lax.axis_index("c")


class _Carry:
    def __init__(self, arrays, out_shapes, sems, start, finish):
        self.arrays, self.out_shapes, self.sems, self.start, self.finish = list(arrays), out_shapes, sems, start, finish

    def sem_shapes(self):
        return [pltpu.SemaphoreType.DMA((k,)) for k in self.sems]


def _gather_carry(shards):
    n = len(shards)

    def copies(ins, outs, sems):
        send_sems, recv_sems, fsend_sems, frecv_sems = sems
        x, y, c = _position()
        me = 2 * x + y
        peers = [(1 - x, y), (x, 1 - y), (1 - x, 1 - y)]

        def over_ici(t, p, chip):
            px, py = peers[p]
            r2 = ins[t].shape[0] // 2
            return pltpu.make_async_remote_copy(
                src_ref=ins[t].at[pl.ds(c * r2, r2), :], dst_ref=outs[t].at[chip, c], send_sem=send_sems.at[3 * t + p],
                recv_sem=recv_sems.at[3 * t + p], device_id=(px, py, c), device_id_type=MESH)

        def to_sibling(t, p, half):
            px, py = peers[p]
            slab = outs[t].at[2 * px + py, half]
            return pltpu.make_async_remote_copy(
                src_ref=slab, dst_ref=slab, send_sem=fsend_sems.at[3 * t + p], recv_sem=frecv_sems.at[3 * t + p],
                device_id=(x, y, 1 - c), device_id_type=MESH)

        pairs = [(t, p) for t in range(n) for p in range(3)]
        sends = [over_ici(t, p, me) for t, p in pairs]
        lands = [over_ici(t, p, 2 * peers[p][0] + peers[p][1]) for t, p in pairs]
        passed = [to_sibling(t, p, c) for t, p in pairs]
        from_sibling = [to_sibling(t, p, 1 - c) for t, p in pairs]
        return sends, lands, passed, from_sibling

    def start(ins, outs, sems):
        for cp in copies(ins, outs, sems)[0]:
            cp.start()

    def finish(ins, outs, sems):
        sends, lands, passed, from_sibling = copies(ins, outs, sems)
        for land, fwd in zip(lands, passed):
            land.wait_recv()
            fwd.start()
        for cp in from_sibling:
            cp.wait_recv()
        for cp in sends + passed:
            cp.wait_send()

    return _Carry(shards, [SDS((N_CHIPS, 2, a.shape[0] // 2, a.shape[1]), a.dtype) for a in shards], [3 * n] * 4,
                  start, finish)


def _scatter_carry(parts):
    def start(ins, outs, sems):
        for cp in _scatter_copies(ins, outs, *sems)[0]:
            cp.start()

    def finish(ins, outs, sems):
        sends, lands = _scatter_copies(ins, outs, *sems)
        for cp in lands:
            cp.wait_recv()
        for cp in sends:
            cp.wait_send()

    return _Carry(parts, [SDS(a.shape, a.dtype) for a in parts], [3 * len(parts)] * 2, start, finish)


def _with_own(gathered, own, chip):
    full = gathered.reshape((N_CHIPS,) + own.shape)
    return lax.dynamic_update_index_in_dim(full, own, chip, 0)


def _exchange_halves(grads):
    n = len(grads)
    slabs = [list(g) if isinstance(g, (list, tuple)) else [g] for g in grads]
    flat = [a for s in slabs for a in s]
    ncp = len(flat)

    def body(*refs):
        ins, outs = refs[:ncp], refs[ncp:ncp + n]
        send_sems, recv_sems = refs[ncp + n:]
        x, y, c = _position()
        cps, k = [], 0
        for t in range(n):
            for j in range(len(slabs[t])):
                if len(slabs[t]) == 1:
                    r2 = ins[k].shape[1] // 2
                    src, dst = ins[k].at[:, pl.ds((1 - c) * r2, r2), :], outs[t]
                else:
                    r2 = ins[k].shape[0] // 2
                    src, dst = ins[k].at[pl.ds((1 - c) * r2, r2), :], outs[t].at[j]
                cps.append(pltpu.make_async_remote_copy(
                    src_ref=src, dst_ref=dst, send_sem=send_sems.at[k], recv_sem=recv_sems.at[k],
                    device_id=(x, y, 1 - c), device_id_type=MESH))
                k += 1
        for cp in cps:
            cp.start()
        for cp in cps:
            cp.wait()

    def landing(s):
        a = s[0]
        return SDS((N_CHIPS, a.shape[-2] // 2, a.shape[-1]), a.dtype)

    return pl.pallas_call(
        body, out_shape=[landing(s) for s in slabs],
        in_specs=[HBM_SPEC] * ncp, out_specs=[HBM_SPEC] * n,
        scratch_shapes=[pltpu.SemaphoreType.DMA((ncp,)), pltpu.SemaphoreType.DMA((ncp,))],
        compiler_params=pltpu.CompilerParams(has_side_effects=True), name="reduce_sibling")(*flat)


def _scatter_copies(ins, outs, send_sems, recv_sems):
    x, y, c = _position()
    me = 2 * x + y
    peers = [(1 - x, y), (x, 1 - y), (1 - x, 1 - y)]

    def remote(t, p, src_slab, dst_slab):
        px, py = peers[p]
        return pltpu.make_async_remote_copy(
            src_ref=ins[t].at[src_slab], dst_ref=outs[t].at[dst_slab], send_sem=send_sems.at[3 * t + p],
            recv_sem=recv_sems.at[3 * t + p], device_id=(px, py, c), device_id_type=MESH)

    n = len(ins)
    sends = [remote(t, p, 2 * peers[p][0] + peers[p][1], me) for t in range(n) for p in range(3)]
    lands = [remote(t, p, me, 2 * peers[p][0] + peers[p][1]) for t in range(n) for p in range(3)]
    return sends, lands


def _share_halves(halves):
    n = len(halves)

    def body(*refs):
        ins, outs = refs[:n], refs[n:2 * n]
        send_sems, recv_sems = refs[2 * n:]
        x, y, c = _position()

        def copy(t, slab):
            return pltpu.make_async_remote_copy(
                src_ref=ins[t].at[slab], dst_ref=outs[t].at[slab], send_sem=send_sems.at[t], recv_sem=recv_sems.at[t],
                device_id=(x, y, 1 - c), device_id_type=MESH)

        for t in range(n):
            copy(t, c).start()
        for t in range(n):
            copy(t, 1 - c).wait_recv()
        for t in range(n):
            copy(t, c).wait_send()

    return pl.pallas_call(
        body, out_shape=[SDS(a.shape, a.dtype) for a in halves],
        in_specs=[HBM_SPEC] * n, out_specs=[HBM_SPEC] * n,
        scratch_shapes=[pltpu.SemaphoreType.DMA((n,)), pltpu.SemaphoreType.DMA((n,))],
        input_output_aliases={t: t for t in range(n)},
        compiler_params=pltpu.CompilerParams(has_side_effects=True), name="share_sibling")(*halves)


def _add_sibling_slab(grad_j, recv, core, j, sums):
    nch, r2, cols = recv.shape
    tr = _row_block(r2, cols)
    nb = r2 // tr
    fresh = sums is None

    def body(c_ref, g_ref, r_ref, *rest):
        del c_ref
        rest[-1][...] = (g_ref[...].astype(F32) + r_ref[...].astype(F32)).astype(BF16)

    return pl.pallas_call(
        body, out_shape=SDS(recv.shape, BF16),
        grid_spec=pltpu.PrefetchScalarGridSpec(
            num_scalar_prefetch=1, grid=(nb,),
            in_specs=[pl.BlockSpec((tr, cols), lambda i, c_ref: (c_ref[0] * nb + i, 0)),
                      pl.BlockSpec((None, tr, cols), lambda i, c_ref: (j, i, 0))] + ([] if fresh else [HBM_SPEC]),
            out_specs=pl.BlockSpec((None, tr, cols), lambda i, c_ref: (j, i, 0))),
        input_output_aliases={} if fresh else {3: 0},
        compiler_params=_params(("parallel",)), name="add_sibling_slab")(
            *((core, grad_j, recv) + (() if fresh else (sums,))))


def _add_sibling(grad, recv, core):
    if isinstance(grad, (list, tuple)):
        sums = None
        for j, g in enumerate(grad):
            sums = _add_sibling_slab(g, recv, core, j, sums)
        return sums
    nch, r2, cols = recv.shape
    tr = _row_block(r2, cols)
    nb = r2 // tr

    def body(c_ref, g_ref, r_ref, o_ref):
        del c_ref
        o_ref[...] = (g_ref[...].astype(F32) + r_ref[...].astype(F32)).astype(BF16)

    return pl.pallas_call(
        body, out_shape=SDS(recv.shape, BF16),
        grid_spec=pltpu.PrefetchScalarGridSpec(
            num_scalar_prefetch=1, grid=(nch, nb),
            in_specs=[pl.BlockSpec((None, tr, cols), lambda j, i, c_ref: (j, c_ref[0] * nb + i, 0)),
                      pl.BlockSpec((None, tr, cols), lambda j, i, c_ref: (j, i, 0))],
            out_specs=pl.BlockSpec((None, tr, cols), lambda j, i, c_ref: (j, i, 0))),
        compiler_params=_params(("parallel", "parallel")), name="add_sibling")(core, grad, recv)


def _add_chips(own, recv, chip_core):
    nch, r2, cols = recv.shape
    tr = _row_block(r2, cols)

    nsc = 2 + nch

    def body(*refs):
        me = refs[0][0]
        own_ref, p_refs, o_ref = refs[nsc], refs[nsc + 1:nsc + 1 + nch], refs[nsc + 1 + nch]
        acc = None
        for j in range(nch):
            term = jnp.where(me == j, own_ref[...], p_refs[j][...]).astype(F32)
            acc = term if acc is None else acc + term
        o_ref[...] = acc

    def slab(j):
        return pl.BlockSpec((None, tr, cols), lambda i, *sc: (sc[2 + j][0], i, 0))

    return pl.pallas_call(
        body, out_shape=SDS((2, r2, cols), F32),
        grid_spec=pltpu.PrefetchScalarGridSpec(
            num_scalar_prefetch=nsc, grid=(r2 // tr,),
            in_specs=[pl.BlockSpec((None, tr, cols), lambda i, *sc: (sc[0][0], i, 0))] + [slab(j) for j in range(nch)],
            out_specs=pl.BlockSpec((None, tr, cols), lambda i, *sc: (sc[1][0], i, 0))),
        compiler_params=_params(("parallel",)), name="add_chips")(*chip_core, own, *([recv] * nch))


def _allreduce_small(pack):
    rows = pack.shape[0]

    def body(p_ref, o_ref, buf, send_sems, recv_sems):
        x, y, c = _position()
        me = 4 * x + 2 * y + c
        buf[me] = p_ref[...]

        def copy(dst_dev, slot):
            return pltpu.make_async_remote_copy(
                src_ref=p_ref, dst_ref=buf.at[slot], send_sem=send_sems.at[dst_dev], recv_sem=recv_sems.at[slot],
                device_id=(dst_dev // 4, (dst_dev // 2) % 2, dst_dev % 2), device_id_type=MESH)

        for dev in range(N_DEV):
            @pl.when(dev != me)
            def _():
                copy(dev, me).start()
        for dev in range(N_DEV):
            @pl.when(dev != me)
            def _():
                copy(dev, dev).wait_recv()
        for dev in range(N_DEV):
            @pl.when(dev != me)
            def _():
                copy(dev, me).wait_send()
        acc = buf[0]
        for dev in range(1, N_DEV):
            acc = acc + buf[dev]
        o_ref[...] = acc

    return pl.pallas_call(
        body, out_shape=SDS(pack.shape, F32),
        in_specs=[pl.BlockSpec(memory_space=pltpu.VMEM)], out_specs=pl.BlockSpec(memory_space=pltpu.VMEM),
        scratch_shapes=[pltpu.VMEM((N_DEV, rows, LANES), F32), pltpu.SemaphoreType.DMA((N_DEV,)),
                        pltpu.SemaphoreType.DMA((N_DEV,))],
        compiler_params=pltpu.CompilerParams(has_side_effects=True), name="allreduce_small")(pack)


ATTN_TQ = 256


def _local_step(cfg, x, target, w, to_chips=None, late=None, hn=None):
    d = cfg.D
    if hn is None:
        hn = _rmsnorm_fwd(x, w["norm_w"])
    proj = _mm(hn, w["w_main"], "nn", BF16, "proj_main", carry=late[0] if late else None)
    if late:
        proj, arrived = proj
        w = {**w, **late[1](arrived)}
    dt_raw = _mm(hn, w["w_dt"], "nn", F32, "proj_dt")
    slopes = _slopes(cfg.H)
    near = _Pass(ATTN_TQ, DILATED_PATTERNS[:-1], 1, cfg.S)
    far = _Pass(LANES, DILATED_PATTERNS[-1:], DEINT, cfg.S // DEINT)
    tab_near, tab_far = _attn_tables(near), _attn_tables(far)
    cols_near, cols_far = (cfg.OQ, cfg.OK, cfg.OV), (0, d, 2 * d)
    qkv_far = _deinterleave(proj, 0, 3 * d, "attn_deinterleave")
    o_1, lse_1 = _attn_fwd(cfg, near, proj, cols_near, tab_near, slopes, "attn_fwd_near")
    o_2, lse_2 = _attn_fwd(cfg, far, qkv_far, cols_far, tab_far, slopes, "attn_fwd_far")
    o_a, oag, lse = _attn_merge(cfg, proj, o_1, lse_1, o_2, lse_2)
    xact = _conv_fwd(cfg, proj, w["conv_w"], w["conv_b"])
    e = _expansion_matrix(cfg)
    y, y2, states = _ssd_fwd(cfg, xact, dt_raw, proj, w["dt_bias"], w["a_log"], w["d_skip"], w["ssm_norm_w"], e)
    a_br = _mm(oag, w["w_attn"], "nn", BF16, "branch_attn")
    s_br = _mm(y2, w["w_ssm"], "nn", BF16, "branch_ssm")
    merged = _merge_fwd(cfg, proj, a_br, s_br)
    dout_f, dout_b, loss_row, g_fnw = _outproj_loss(merged, w["w_out"], x, target, w["final_norm_w"])

    dmerged = _mm(dout_b, w["w_out"], "nt", BF16, "d_merged")
    g_w_out = _mm(merged, dout_b, "tn", BF16, "g_w_out")
    dproj, da_br = _merge_bwd(cfg, proj, a_br, dmerged, cfg.OGA, None, "merge_bwd_attn")
    dproj, ds_br = _merge_bwd(cfg, proj, s_br, dmerged, cfg.OGS, dproj, "merge_bwd_ssm")
    doag = _mm(da_br, w["w_attn"], "nt", BF16, "d_oag")
    g_w_attn = _mm(oag, da_br, "tn", BF16, "g_w_attn")
    dy2 = _mm(ds_br, w["w_ssm"], "nt", BF16, "d_y2")
    g_w_ssm = _mm(y2, ds_br, "tn", BF16, "g_w_ssm")
    dproj, dxact, ddt, g_snw, g_dtb, g_alog, g_dsk = _ssd_bwd(
        cfg, xact, dt_raw, proj, y, dy2, states, w["dt_bias"], w["a_log"], w["d_skip"], w["ssm_norm_w"], e, dproj)
    dproj, g_cw, g_cb = _conv_bwd(cfg, proj, dxact, w["conv_w"], w["conv_b"], dproj)
    dproj, do, do_far, dl, dl_far, lse_far = _attn_bwd_prep(cfg, proj, o_a, doag, lse, dproj)
    g_near = _attn_bwd(cfg, near, proj, cols_near, do, lse, dl, tab_near, slopes, "attn_bwd_near")
    g_far = _attn_bwd(cfg, far, qkv_far, cols_far, do_far, lse_far, dl_far, tab_far, slopes, "attn_bwd_far")
    for g_1, g_2, col0, nm in zip(g_near, g_far, cols_near, ("attn_dq", "attn_dk", "attn_dv")):
        dproj = _attn_grad_sum(cfg, g_1, g_2, col0, dproj, nm)
    ddt_b = ddt.astype(BF16)
    g_w_main = _mm(hn, dproj, "tn", BF16, "g_w_main")
    g_w_dt = _mm(hn, ddt_b, "tn", BF16, "g_w_dt")
    grads = dict(w_main=g_w_main, w_dt=g_w_dt, conv_w=g_cw, conv_b=g_cb, dt_bias=g_dtb, a_log=g_alog,
                 d_skip=g_dsk, ssm_norm_w=g_snw, w_attn=g_w_attn, w_ssm=g_w_ssm, w_out=g_w_out, final_norm_w=g_fnw)
    sent = to_chips(grads) if to_chips is not None else ()
    dhn = _mm(dproj, w["w_main"], "nt", F32, "d_hn", tk=1024, carry=_scatter_carry(sent) if sent else None)
    landed = ()
    if sent:
        dhn, landed = dhn
    dhn_dt = _mm(ddt_b, w["w_dt"], "nt", F32, "d_hn_dt")
    grad_x, grads["norm_w"] = _rmsnorm_bwd(x, w["norm_w"], dhn, dhn_dt, dout_f)
    return loss_row, grad_x, grads, sent, landed


def _pad_lanes(v):
    return jnp.pad(v, ((0, 0), (0, LANES - v.shape[1])))


def _cut(lo, hi, a, b):
    a, b = max(lo, a), min(hi, b)
    return (a, b) if a < b else None


def _main_from_shards(cfg, shards):
    per = cfg.N_IN // len(shards)
    main, dt = [], []
    for j, sh in enumerate(shards):
        lo, hi = j * per, (j + 1) * per
        for dst, rng in ((main, (0, cfg.OGA)), (dt, (cfg.OGA, cfg.OGA + cfg.NH)), (main, (cfg.OGA + cfg.NH, cfg.N_IN))):
            c = _cut(lo, hi, *rng)
            if c is not None:
                dst.append(sh[:, c[0] - lo:c[1] - lo])
    return jnp.concatenate(main, axis=1), _pad_lanes(jnp.concatenate(dt, axis=1))


def _shards_from_main(cfg, g_main, g_dt, n):
    per = cfg.N_IN // n
    out = []
    for j in range(n):
        lo, hi = j * per, (j + 1) * per
        parts = []
        for src, off, rng in ((g_main, 0, (0, cfg.OGA)), (g_dt, cfg.OGA, (cfg.OGA, cfg.OGA + cfg.NH)),
                              (g_main, cfg.NH, (cfg.OGA + cfg.NH, cfg.N_IN))):
            c = _cut(lo, hi, *rng)
            if c is not None:
                parts.append(src[:, c[0] - off:c[1] - off])
        out.append(jnp.concatenate(parts, axis=1) if len(parts) > 1 else parts[0])
    return out


def _full_weights(cfg, norm_w, w_in_shards, conv_w, conv_b, dt_bias, a_log, d_skip, ssm_norm_w, w_attn, w_ssm, w_out, fnw):
    w_main, w_dt = _main_from_shards(cfg, w_in_shards)
    return dict(norm_w=norm_w, w_main=w_main.astype(BF16), w_dt=w_dt.astype(BF16), conv_w=conv_w, conv_b=conv_b,
                dt_bias=_pad_lanes(dt_bias), a_log=_pad_lanes(a_log), d_skip=_pad_lanes(d_skip), ssm_norm_w=ssm_norm_w,
                final_norm_w=fnw, **{k: v.astype(BF16) for k, v in (("w_attn", w_attn), ("w_ssm", w_ssm), ("w_out", w_out))
                                     if v is not None})


def kernel(x, norm_w, w_in, conv_w, conv_b, dt_bias, a_log, d_skip, ssm_norm_w, w_attn_branch, w_ssm_branch, w_out, final_norm_w, loss_target, m_norm_w, m_w_in, m_conv_w, m_conv_b, m_dt_bias, m_a_log, m_d_skip, m_ssm_norm_w, m_w_attn_branch, m_w_ssm_branch, m_w_out, m_final_norm_w, v_norm_w, v_w_in, v_conv_w, v_conv_b, v_dt_bias, v_a_log, v_d_skip, v_ssm_norm_w, v_w_attn_branch, v_w_ssm_branch, v_w_out, v_final_norm_w):
    cfg = _Cfg(x.shape[1], x.shape[2])
    d, si, cd, nh = cfg.D, cfg.SI, cfg.CD, cfg.NH
    chip = 2 * lax.axis_index("x") + lax.axis_index("y")
    core = lax.axis_index("c").astype(jnp.int32).reshape(1)
    chip = chip.astype(jnp.int32)
    chip_core = [chip.reshape(1), core] + [jnp.where(chip == j, (j + 1) % N_CHIPS, j).astype(jnp.int32).reshape(1)
                                           for j in range(N_CHIPS)]

    own = [w_in[0].astype(BF16), conv_w[0].reshape(4 * CONV_K, -1)]
    hn, gathered = _rmsnorm_fwd(x[0], norm_w, carry=_gather_carry(own))
    a_in, a_cw = [_with_own(g, o, chip) for g, o in zip(gathered, own)]
    conv_w_full = a_cw.reshape(N_CHIPS, CONV_K, cd // N_CHIPS).transpose(1, 0, 2).reshape(CONV_K, cd)
    w = _full_weights(cfg, norm_w, [a_in[j] for j in range(N_CHIPS)], conv_w_full, conv_b, dt_bias, a_log, d_skip,
                      ssm_norm_w, None, None, None, final_norm_w.reshape(1, d))
    own_late = [w_attn_branch[0].astype(BF16), w_ssm_branch[0].astype(BF16), w_out[0].astype(BF16)]

    def late_weights(arrived):
        a_attn, a_ssm, a_out = [_with_own(g, o, chip) for g, o in zip(arrived, own_late)]
        return dict(w_attn=a_attn.reshape(d, d), w_ssm=a_ssm.reshape(si, d), w_out=a_out.reshape(d, d))

    def to_chips(grads):
        by_chip = [_shards_from_main(cfg, grads["w_main"], grads["w_dt"], N_CHIPS),
                   grads["w_attn"].reshape(N_CHIPS, d // N_CHIPS, d),
                   grads["w_ssm"].reshape(N_CHIPS, si // N_CHIPS, d),
                   grads["w_out"].reshape(N_CHIPS, d // N_CHIPS, d)]
        from_sibling = _exchange_halves(by_chip)
        return [_add_sibling(g, r, core) for g, r in zip(by_chip, from_sibling)]

    loss_row, grad_x, grads, chip_sums, from_chips = _local_step(
        cfg, x[0], loss_target[0], w, to_chips, (_gather_carry(own_late), late_weights), hn)
    halves = [_add_chips(o, p, chip_core) for o, p in zip(chip_sums, from_chips)]
    g_in, g_attn, g_ssm, g_out = [h.reshape(2 * h.shape[1], h.shape[2]) for h in _share_halves(halves)]

    small = [loss_row, grads["norm_w"], grads["conv_b"], grads["dt_bias"], grads["a_log"], grads["d_skip"],
             grads["ssm_norm_w"], grads["final_norm_w"], grads["conv_w"].reshape(1, CONV_K * cd)]
    sizes = [a.shape[1] for a in small]
    total = sum(sizes)
    rows = -(-total // (8 * LANES)) * 8
    flat = jnp.pad(jnp.concatenate(small, axis=1), ((0, 0), (0, rows * LANES - total)))
    red = _allreduce_small(flat.reshape(rows, LANES)).reshape(1, rows * LANES)
    offs = [sum(sizes[:i]) for i in range(len(sizes))]
    loss_r, g_nw, g_cb, g_dtb, g_alog, g_dsk, g_snw, g_fnw, g_cw_flat = [
        red[:, o:o + n] for o, n in zip(offs, sizes)]
    loss = loss_r[0, 0]
    g_dtb, g_alog, g_dsk = g_dtb[:, :nh], g_alog[:, :nh], g_dsk[:, :nh]
    cshard = cd // N_CHIPS
    g_cw = lax.dynamic_slice_in_dim(g_cw_flat.reshape(CONV_K, cd), chip * cshard, cshard, axis=1)

    upd = {}
    g_in_t = jnp.transpose(g_in)
    upd["w_in"] = tuple(jnp.transpose(u) for u in _adamw(
        jnp.transpose(w_in[0]), g_in_t, jnp.transpose(m_w_in[0]), jnp.transpose(v_w_in[0]), "adamw_w_in"))
    g_in = jnp.transpose(g_in_t)
    for name, wv, gv, mv, vv in [("w_attn", w_attn_branch[0], g_attn, m_w_attn_branch[0], v_w_attn_branch[0]),
                                 ("w_ssm", w_ssm_branch[0], g_ssm, m_w_ssm_branch[0], v_w_ssm_branch[0]),
                                 ("w_out", w_out[0], g_out, m_w_out[0], v_w_out[0])]:
        upd[name] = _adamw(wv, gv, mv, vv, "adamw_" + name)
    names = ["norm_w", "conv_w", "conv_b", "dt_bias", "a_log", "d_skip", "ssm_norm_w", "final_norm_w"]
    ws = [norm_w, conv_w[0].reshape(1, -1), conv_b, dt_bias, a_log, d_skip, ssm_norm_w, final_norm_w.reshape(1, d)]
    gs = [g_nw, g_cw.reshape(1, -1), g_cb, g_dtb, g_alog, g_dsk, g_snw, g_fnw]
    ms = [m_norm_w, m_conv_w[0].reshape(1, -1), m_conv_b, m_dt_bias, m_a_log, m_d_skip, m_ssm_norm_w,
          m_final_norm_w.reshape(1, d)]
    vs = [v_norm_w, v_conv_w[0].reshape(1, -1), v_conv_b, v_dt_bias, v_a_log, v_d_skip, v_ssm_norm_w,
          v_final_norm_w.reshape(1, d)]
    ssz = [a.shape[1] for a in ws]
    stot = sum(ssz)
    srows = -(-stot // (8 * LANES)) * 8

    def pack(parts):
        return jnp.pad(jnp.concatenate(parts, axis=1), ((0, 0), (0, srows * LANES - stot))).reshape(srows, LANES)

    packed = _adamw(pack(ws), pack(gs), pack(ms), pack(vs), "adamw_small")
    soffs = [sum(ssz[:i]) for i in range(len(ssz))]
    for k, nm in enumerate(names):
        upd[nm] = tuple(p.reshape(1, srows * LANES)[:, soffs[k]:soffs[k] + ssz[k]] for p in packed)

    shapes = dict(norm_w=norm_w.shape, w_in=w_in.shape, conv_w=conv_w.shape, conv_b=conv_b.shape, dt_bias=dt_bias.shape,
                  a_log=a_log.shape, d_skip=d_skip.shape, ssm_norm_w=ssm_norm_w.shape, w_attn=w_attn_branch.shape,
                  w_ssm=w_ssm_branch.shape, w_out=w_out.shape, final_norm_w=final_norm_w.shape)
    order = ["norm_w", "w_in", "conv_w", "conv_b", "dt_bias", "a_log", "d_skip", "ssm_norm_w", "w_attn", "w_ssm",
             "w_out", "final_norm_w"]
    gradv = dict(norm_w=g_nw, w_in=g_in, conv_w=g_cw, conv_b=g_cb, dt_bias=g_dtb, a_log=g_alog, d_skip=g_dsk,
                 ssm_norm_w=g_snw, w_attn=g_attn, w_ssm=g_ssm, w_out=g_out, final_norm_w=g_fnw)
    outs = [loss, grad_x[None]]
    outs += [gradv[n].reshape(shapes[n]) for n in order]
    for k in range(3):
        outs += [upd[n][k].reshape(shapes[n]) for n in order]
    return tuple(outs)
```

```python
import jax
import jax.numpy as jnp
from jax import lax
from jax.experimental import pallas as pl
from jax.experimental.pallas import tpu as pltpu

F32 = jnp.float32
BF16 = jnp.bfloat16
SDS = jax.ShapeDtypeStruct

RMS_EPS = 1e-6
LANES = 128
CHUNK = 128
SSM_HEAD_DIM = 64
SSM_GROUPS = 8
SSM_STATE = 128
CONV_K = 4
ATTN_HEAD_DIM = 128
DILATED_PATTERNS = ((128, 1), (512, 4), (2048, 16))
NEG = -1e30
VMEM_LIMIT = 56 * 1024 * 1024
ADAM_LR, ADAM_B1, ADAM_B2, ADAM_EPS, ADAM_WD, ADAM_STEP = 0.001, 0.9, 0.999, 1e-08, 0.01, 10
MESH = pl.DeviceIdType.MESH
N_CHIPS = 4
N_DEV = 8


class _Cfg:
    def __init__(self, s, d):
        self.S, self.D = s, d
        self.H = d // ATTN_HEAD_DIM
        self.SI = 2 * d
        self.NH = self.SI // SSM_HEAD_DIM
        self.HPG = self.NH // SSM_GROUPS
        self.GW = self.HPG * SSM_HEAD_DIM
        self.BC = SSM_GROUPS * SSM_STATE
        self.CD = self.SI + 2 * self.BC
        self.OQ, self.OK, self.OV, self.OZA = 0, d, 2 * d, 3 * d
        self.OZS = 4 * d
        self.OXBC = self.OZS + self.SI
        self.OGA = self.OXBC + self.CD
        self.OGS = self.OGA + d
        self.NM = self.OGS + d
        self.N_IN = self.NM + self.NH
        assert self.GW % LANES == 0 and self.NH <= LANES and s % 512 == 0 and d % 512 == 0


def _params(sem=None):
    return pltpu.CompilerParams(dimension_semantics=sem, vmem_limit_bytes=VMEM_LIMIT)


def _sigmoid(x):
    return 0.5 * jnp.tanh(0.5 * x) + 0.5


def _softplus(x):
    u = jnp.exp(-jnp.abs(x))
    l1p = jnp.where(u < 1e-3, u * (1.0 - u * (0.5 - u * (1.0 / 3.0))), jnp.log(1.0 + u))
    return jnp.maximum(x, 0.0) + l1p


def _nt(a, b):
    return lax.dot_general(a, b, (((1,), (1,)), ((), ())), preferred_element_type=F32)


def _tn(a, b):
    return lax.dot_general(a, b, (((0,), (0,)), ((), ())), preferred_element_type=F32)


def _nn(a, b):
    return jnp.dot(a, b, preferred_element_type=F32)


def _tile(n, target):
    if n <= target:
        return n
    best = None
    for t in range(LANES, target + 1, LANES):
        if n % t == 0:
            best = t
    assert best is not None, (n, target)
    return best


MM_TK = {"nn": 2048, "nt": 2048, "tn": 1024}


def _mm(a, b, dims, out_dtype, name, tm=1024, tn=2048, tk=None, init=None, carry=None):
    tk = MM_TK[dims] if tk is None else tk
    if dims == "nn":
        (m, k), (k2, n) = a.shape, b.shape
    elif dims == "nt":
        (m, k), (n, k2) = a.shape, b.shape
    else:
        (k, m), (k2, n) = a.shape, b.shape
    assert k == k2
    tm, tn, tk = _tile(m, tm), _tile(n, tn), _tile(k, tk)
    nk = k // tk
    if dims == "tn":
        a_spec = pl.BlockSpec((tk, tm), lambda i, j, kk: (kk, i))
    else:
        a_spec = pl.BlockSpec((tm, tk), lambda i, j, kk: (i, kk))
    if dims == "nt":
        b_spec = pl.BlockSpec((tn, tk), lambda i, j, kk: (j, kk))
    else:
        b_spec = pl.BlockSpec((tk, tn), lambda i, j, kk: (kk, j))
    o_spec = pl.BlockSpec((tm, tn), lambda i, j, kk: (i, j))
    op = {"nn": _nn, "nt": _nt, "tn": _tn}[dims]
    has_init = init is not None
    nx = len(carry.arrays) if carry is not None else 0
    ni, nj = m // tm, n // tn

    def body(*refs):
        a_ref, b_ref = refs[0], refs[1]
        i_ref = refs[2] if has_init else None
        x_in = refs[2 + has_init:2 + has_init + nx]
        o_ref = refs[2 + has_init + nx]
        x_out = refs[3 + has_init + nx:3 + has_init + 2 * nx]
        acc = refs[3 + has_init + 2 * nx]
        x_sems = refs[4 + has_init + 2 * nx:]
        i, j, kk = pl.program_id(0), pl.program_id(1), pl.program_id(2)

        if nx:
            @pl.when((i == 0) & (j == 0) & (kk == 0))
            def _():
                carry.start(x_in, x_out, x_sems)

        prod = lambda: op(a_ref[...], b_ref[...])
        with_init = (lambda p: p + i_ref[...].astype(F32)) if has_init else (lambda p: p)
        if nk == 1:
            o_ref[...] = with_init(prod()).astype(out_dtype)
        else:
            @pl.when(kk == 0)
            def _():
                acc[...] = with_init(prod())

            @pl.when((kk > 0) & (kk < nk - 1))
            def _():
                acc[...] += prod()

            @pl.when(kk == nk - 1)
            def _():
                o_ref[...] = (acc[...] + prod()).astype(out_dtype)

        if nx:
            @pl.when((i == ni - 1) & (j == nj - 1) & (kk == nk - 1))
            def _():
                carry.finish(x_in, x_out, x_sems)

    in_specs = [a_spec, b_spec] + ([o_spec] if has_init else []) + [HBM_SPEC] * nx
    args = (a, b) + ((init,) if has_init else ()) + (tuple(carry.arrays) if nx else ())
    sems = carry.sem_shapes() if nx else []
    outs = pl.pallas_call(
        body, out_shape=[SDS((m, n), out_dtype)] + (carry.out_shapes if nx else []), grid=(ni, nj, nk),
        in_specs=in_specs, out_specs=[o_spec] + [HBM_SPEC] * nx,
        scratch_shapes=[pltpu.VMEM((tm, tn) if nk > 1 else (8, LANES), F32)] + sems,
        compiler_params=_params(("arbitrary",) * 3 if nx else ("parallel", "parallel", "arbitrary")), name=name)(*args)
    return (outs[0], outs[1:]) if nx else outs[0]


def _rmsnorm_fwd(x, w, carry=None):
    s, d = x.shape
    tr = 256
    nsteps = s // tr
    nx = len(carry.arrays) if carry is not None else 0

    def body(*refs):
        x_ref, w_ref, x_in = refs[0], refs[1], refs[2:2 + nx]
        o_ref, x_out, x_sems = refs[2 + nx], refs[3 + nx:3 + 2 * nx], refs[3 + 2 * nx:]
        if nx:
            @pl.when(pl.program_id(0) == 0)
            def _():
                carry.start(x_in, x_out, x_sems)

        xv = x_ref[...]
        r = lax.rsqrt(jnp.mean(xv * xv, axis=-1, keepdims=True) + RMS_EPS)
        o_ref[...] = (xv * r * w_ref[...]).astype(BF16)

        if nx:
            @pl.when(pl.program_id(0) == nsteps - 1)
            def _():
                carry.finish(x_in, x_out, x_sems)

    outs = pl.pallas_call(
        body, out_shape=[SDS((s, d), BF16)] + (carry.out_shapes if nx else []), grid=(nsteps,),
        in_specs=[pl.BlockSpec((tr, d), lambda i: (i, 0)), pl.BlockSpec((1, d), lambda i: (0, 0))] + [HBM_SPEC] * nx,
        out_specs=[pl.BlockSpec((tr, d), lambda i: (i, 0))] + [HBM_SPEC] * nx,
        scratch_shapes=carry.sem_shapes() if nx else [],
        compiler_params=_params(("arbitrary",) if nx else ("parallel",)), name="rmsnorm_fwd")(
            x, w, *(carry.arrays if nx else []))
    return (outs[0], outs[1:]) if nx else outs[0]


def _rmsnorm_bwd(x, w, dhn_a, dhn_b, dout):
    s, d = x.shape
    tr = 256

    def body(x_ref, w_ref, dh_ref, dh2_ref, do_ref, gx_ref, gw_ref):
        xv = x_ref[...]
        r = lax.rsqrt(jnp.mean(xv * xv, axis=-1, keepdims=True) + RMS_EPS)
        nrm = xv * r
        dh = dh_ref[...] + dh2_ref[...]
        gy = dh * w_ref[...]
        gx_ref[...] = do_ref[...] + r * (gy - nrm * jnp.mean(gy * nrm, axis=-1, keepdims=True))

        @pl.when(pl.program_id(0) == 0)
        def _():
            gw_ref[...] = jnp.zeros_like(gw_ref)

        gw_ref[...] += jnp.sum(dh * nrm, axis=0, keepdims=True)

    blk = pl.BlockSpec((tr, d), lambda i: (i, 0))
    row = pl.BlockSpec((1, d), lambda i: (0, 0))
    return pl.pallas_call(
        body, out_shape=(SDS((s, d), F32), SDS((1, d), F32)), grid=(s // tr,),
        in_specs=[blk, row, blk, blk, blk], out_specs=(blk, row),
        compiler_params=_params(("arbitrary",)), name="rmsnorm_bwd")(x, w, dhn_a, dhn_b, dout)


DEINT = DILATED_PATTERNS[-1][1]
DEINT_ROWS = DEINT * LANES


class _Pass:
    def __init__(self, tq, patterns, unit, seg_len):
        self.tq, self.patterns, self.unit, self.seg_len = tq, patterns, unit, seg_len
        self.win = max(w for w, _ in patterns) // unit
        self.w = self.win + tq
        assert self.win % tq == 0


def _attn_tables(ps):
    i = jnp.arange(ps.tq, dtype=jnp.int32)[:, None]
    j = jnp.arange(ps.w, dtype=jnp.int32)[None, :]
    delta = (i + ps.win - j) * ps.unit
    n = jnp.zeros((ps.tq, ps.w), F32)
    for window, dil in ps.patterns:
        n = n + ((delta >= 0) & (delta <= window) & (delta % dil == 0)).astype(F32)
    logn = jnp.where(n > 0, jnp.log(jnp.maximum(n, 1.0)), NEG)
    return logn, jnp.maximum(delta, 0).astype(F32)


def _slopes(h):
    s = jnp.asarray([2.0 ** (-8.0 * (i + 1) / h) for i in range(h)], F32)
    return jnp.broadcast_to(s[:, None, None], (h, 1, LANES))


def _masked_logn(ps, logn_ref, start):
    col = lax.broadcasted_iota(jnp.int32, (ps.tq, ps.w), 1)
    return jnp.where(col >= ps.win - lax.rem(start, ps.seg_len), logn_ref[...], NEG)


def _head_cols(hh):
    return slice(hh * ATTN_HEAD_DIM, (hh + 1) * ATTN_HEAD_DIM)


def _head_window(refs, cs):
    return jnp.concatenate([r[:, cs] for r in refs], axis=0)


def _head_scores(q_ref, kw, cs, base, dist_ref, slope_ref, hh):
    return _nt(q_ref[:, cs], kw) * (ATTN_HEAD_DIM ** -0.5) + (base - slope_ref[hh][0:1, 0:1] * dist_ref[...])


def _lane_of(stat, hh):
    lane = lax.broadcasted_iota(jnp.int32, stat.shape, 1)
    return jnp.sum(jnp.where(lane == hh, stat, 0.0), axis=1, keepdims=True)


def _window_specs(ps, d, col, nb):
    nprev = ps.win // ps.tq
    return [pl.BlockSpec((ps.tq, d), lambda i, b=b: (jnp.maximum(jnp.minimum(i, nb - 1) - (nprev - b), 0), col))
            for b in range(nprev + 1)]


def _attn_fwd(cfg, ps, qkv, cols, tables, slopes, name):
    s, h, d = cfg.S, cfg.H, cfg.D
    tq, nw = ps.tq, ps.win // ps.tq + 1
    nb = s // tq
    logn, dist = tables
    qc, kc, vc = [c // d for c in cols]

    def body(*refs):
        q_ref, k_refs, v_refs = refs[0], refs[1:1 + nw], refs[1 + nw:1 + 2 * nw]
        logn_ref, dist_ref, slope_ref, o_ref, lse_ref = refs[1 + 2 * nw:]
        base = _masked_logn(ps, logn_ref, pl.program_id(0) * tq)
        lane = lax.broadcasted_iota(jnp.int32, (tq, LANES), 1)

        lse = jnp.zeros((tq, LANES), F32)
        for hh in range(h):
            cs = _head_cols(hh)
            sc = _head_scores(q_ref, _head_window(k_refs, cs), cs, base, dist_ref, slope_ref, hh)
            m = jnp.max(sc, axis=1, keepdims=True)
            p = jnp.exp(sc - m)
            l = jnp.sum(p, axis=1, keepdims=True)
            o_ref[:, cs] = (_nn(p.astype(BF16), _head_window(v_refs, cs)) / l).astype(BF16)
            lse = jnp.where(lane == hh, m + jnp.log(l), lse)
        lse_ref[...] = lse

    tab = pl.BlockSpec((tq, ps.w), lambda i: (0, 0))
    return pl.pallas_call(
        body, out_shape=(SDS((s, d), BF16), SDS((s, LANES), F32)), grid=(nb,),
        in_specs=[pl.BlockSpec((tq, d), lambda i: (i, qc))] + _window_specs(ps, d, kc, nb) + _window_specs(ps, d, vc, nb)
        + [tab, tab, pl.BlockSpec((h, 1, LANES), lambda i: (0, 0, 0))],
        out_specs=(pl.BlockSpec((tq, d), lambda i: (i, 0)), pl.BlockSpec((tq, LANES), lambda i: (i, 0))),
        compiler_params=_params(("parallel",)), name=name)(*([qkv] * (1 + 2 * nw)), logn, dist, slopes)


def _attn_bwd(cfg, ps, qkv, cols, do, lse, delta, tables, slopes, name):
    s, h, d = cfg.S, cfg.H, cfg.D
    tq, nprev = ps.tq, ps.win // ps.tq
    nw = nprev + 1
    nb = s // tq
    logn, dist = tables
    qc, kc, vc = [c // d for c in cols]
    scale = ATTN_HEAD_DIM ** -0.5

    def body(*refs):
        q_ref, k_refs, v_refs = refs[0], refs[1:1 + nw], refs[1 + nw:1 + 2 * nw]
        do_ref, lse_ref, dl_ref, logn_ref, dist_ref, slope_ref, dq_ref, dk_ref, dv_ref, ck, cv = refs[1 + 2 * nw:]
        i = pl.program_id(0)
        slot = lambda b: lax.rem(i + b, nprev)

        @pl.when(i == 0)
        def _():
            ck[...] = jnp.zeros_like(ck)
            cv[...] = jnp.zeros_like(cv)

        @pl.when(i < nb)
        def _():
            base = _masked_logn(ps, logn_ref, i * tq)
            lse_all, dl_all = lse_ref[...], dl_ref[...]

            for hh in range(h):
                cs = _head_cols(hh)
                kw, vw = _head_window(k_refs, cs), _head_window(v_refs, cs)
                sc = _head_scores(q_ref, kw, cs, base, dist_ref, slope_ref, hh)
                p = jnp.exp(sc - lse_all[:, hh:hh + 1])
                dob = do_ref[:, cs]
                ds = (p * (_nt(dob, vw) - dl_all[:, hh:hh + 1]) * scale).astype(BF16)
                dq_ref[:, cs] = _nn(ds, kw).astype(BF16)
                dkw = _tn(ds, q_ref[:, cs])
                dvw = _tn(p.astype(BF16), dob)
                dk_ref[:, cs] = ck[slot(0), :, cs] + dkw[0:tq]
                dv_ref[:, cs] = cv[slot(0), :, cs] + dvw[0:tq]
                for b in range(1, nprev):
                    ck[slot(b), :, cs] += dkw[b * tq:(b + 1) * tq]
                    cv[slot(b), :, cs] += dvw[b * tq:(b + 1) * tq]
                ck[slot(0), :, cs] = dkw[nprev * tq:]
                cv[slot(0), :, cs] = dvw[nprev * tq:]

        @pl.when(i >= nb)
        def _():
            dk_ref[...] = ck[slot(0)]
            dv_ref[...] = cv[slot(0)]

    here = lambda i: jnp.minimum(i, nb - 1)
    blk = pl.BlockSpec((tq, d), lambda i: (here(i), 0))
    stat = pl.BlockSpec((tq, LANES), lambda i: (here(i), 0))
    late = pl.BlockSpec((tq, d), lambda i: (jnp.maximum(i - nprev, 0), 0))
    tab = pl.BlockSpec((tq, ps.w), lambda i: (0, 0))
    return pl.pallas_call(
        body, out_shape=(SDS((s, d), BF16), SDS((s, d), F32), SDS((s, d), F32)), grid=(nb + nprev,),
        in_specs=[pl.BlockSpec((tq, d), lambda i: (here(i), qc))] + _window_specs(ps, d, kc, nb)
        + _window_specs(ps, d, vc, nb) + [blk, stat, stat, tab, tab, pl.BlockSpec((h, 1, LANES), lambda i: (0, 0, 0))],
        out_specs=(blk, late, late),
        scratch_shapes=[pltpu.VMEM((nprev, tq, d), F32), pltpu.VMEM((nprev, tq, d), F32)],
        compiler_params=_params(("arbitrary",)), name=name)(
            *([qkv] * (1 + 2 * nw)), do, lse, delta, logn, dist, slopes)


def _by_residue(a):
    return a.reshape(DEINT, a.shape[0] // DEINT, a.shape[1])


def _deint_spec(colblock):
    return pl.BlockSpec((DEINT, LANES, LANES), lambda b, j: (0, b, colblock(j)))


def _deint_rows(scr, out_ref, dtype):
    for r in range(DEINT):
        out_ref[r] = scr[pl.ds(r, LANES, stride=DEINT), :].astype(dtype)


def _int_rows(in_ref, scr):
    for r in range(DEINT):
        scr[pl.ds(r, LANES, stride=DEINT), :] = in_ref[r].astype(F32)


WIDE = 4 * LANES


def _wide_spec():
    return pl.BlockSpec((DEINT, LANES, WIDE), lambda b, j: (0, b, j))


def _deinterleave(x, col0, ncols, name):
    s = x.shape[0]
    c0 = col0 // WIDE

    def body(x_ref, o_ref, scr):
        for t in range(WIDE // LANES):
            cs = slice(t * LANES, (t + 1) * LANES)
            scr[t] = x_ref[:, cs].astype(F32)
            for r in range(DEINT):
                o_ref[r, :, cs] = scr.at[t][pl.ds(r, LANES, stride=DEINT), :].astype(x.dtype)

    out = pl.pallas_call(
        body, out_shape=SDS((DEINT, s // DEINT, ncols), x.dtype), grid=(s // DEINT_ROWS, ncols // WIDE),
        in_specs=[pl.BlockSpec((DEINT_ROWS, WIDE), lambda b, j: (b, c0 + j))],
        out_specs=_wide_spec(),
        scratch_shapes=[pltpu.VMEM((WIDE // LANES, DEINT_ROWS, LANES), F32)],
        compiler_params=_params(("parallel", "parallel")), name=name)(x)
    return out.reshape(s, ncols)


def _attn_merge(cfg, proj, o_1, lse_1, o_2, lse_2):
    s, h = cfg.S, cfg.H
    zb = cfg.OZA // WIDE
    rows = DEINT_ROWS
    hps = WIDE // LANES

    def body(o1_ref, l1_ref, o2_ref, l2_ref, z_ref, o_ref, og_ref, lse_ref, so, sl):
        j = pl.program_id(1)

        @pl.when(j == 0)
        def _():
            _int_rows(l2_ref, sl)
            lse_ref[...] = jnp.zeros_like(lse_ref)

        l1_all, l2_all = l1_ref[...], sl[...]
        lane = lax.broadcasted_iota(jnp.int32, (rows, LANES), 1)
        lse = lse_ref[...]
        for t in range(hps):
            hh = j * hps + t
            cs = slice(t * LANES, (t + 1) * LANES)
            for r in range(DEINT):
                so.at[t][pl.ds(r, LANES, stride=DEINT), :] = o2_ref[r, :, cs].astype(F32)
            l1, l2 = _lane_of(l1_all, hh), _lane_of(l2_all, hh)
            mx = jnp.maximum(l1, l2)
            w1, w2 = jnp.exp(l1 - mx), jnp.exp(l2 - mx)
            den = w1 + w2
            o = (w1 * o1_ref[:, cs].astype(F32) + w2 * so[t]) / den
            z = z_ref[:, cs].astype(F32)
            o_ref[:, cs] = o.astype(BF16)
            og_ref[:, cs] = (o * (z * _sigmoid(z))).astype(BF16)
            lse = jnp.where(lane == hh, mx + jnp.log(den), lse)
        lse_ref[...] = lse

    blk = pl.BlockSpec((rows, WIDE), lambda b, j: (b, j))
    stat = pl.BlockSpec((rows, LANES), lambda b, j: (b, 0))
    return pl.pallas_call(
        body, out_shape=(SDS((s, cfg.D), BF16), SDS((s, cfg.D), BF16), SDS((s, LANES), F32)),
        grid=(s // rows, h // hps),
        in_specs=[blk, stat, _wide_spec(), _deint_spec(lambda j: 0), pl.BlockSpec((rows, WIDE), lambda b, j: (b, zb + j))],
        out_specs=(blk, blk, stat),
        scratch_shapes=[pltpu.VMEM((hps, rows, LANES), F32), pltpu.VMEM((rows, LANES), F32)],
        compiler_params=_params(("parallel", "arbitrary")), name="attn_merge")(
            o_1, lse_1, _by_residue(o_2), _by_residue(lse_2), proj)


def _attn_bwd_prep(cfg, proj, o_a, doag, lse, dproj):
    s, h = cfg.S, cfg.H
    zb = cfg.OZA // WIDE
    rows = DEINT_ROWS
    hps = WIDE // LANES

    def body(o_ref, dg_ref, z_ref, lse_ref, dp_in, dz_ref, do_ref, do2_ref, dl_ref, dl2_ref, lse2_ref, scr):
        del dp_in
        j = pl.program_id(1)

        @pl.when(j == 0)
        def _():
            dl_ref[...] = jnp.zeros_like(dl_ref)

        lane = lax.broadcasted_iota(jnp.int32, (rows, LANES), 1)
        dl = dl_ref[...]
        for t in range(hps):
            cs = slice(t * LANES, (t + 1) * LANES)
            z = z_ref[:, cs].astype(F32)
            sg = _sigmoid(z)
            o = o_ref[:, cs].astype(F32)
            dg = dg_ref[:, cs].astype(F32)
            do = dg * (z * sg)
            dz_ref[:, cs] = (dg * o * (sg * (1.0 + z * (1.0 - sg)))).astype(BF16)
            do_ref[:, cs] = do.astype(BF16)
            scr[...] = do
            for r in range(DEINT):
                do2_ref[r, :, cs] = scr[pl.ds(r, LANES, stride=DEINT), :].astype(BF16)
            dl = jnp.where(lane == j * hps + t, jnp.sum(do * o, axis=1, keepdims=True), dl)
        dl_ref[...] = dl

        @pl.when(j == h // hps - 1)
        def _():
            scr[...] = dl
            _deint_rows(scr, dl2_ref, F32)
            scr[...] = lse_ref[...]
            _deint_rows(scr, lse2_ref, F32)

    blk = pl.BlockSpec((rows, WIDE), lambda b, j: (b, j))
    stat = pl.BlockSpec((rows, LANES), lambda b, j: (b, 0))
    stat2 = _deint_spec(lambda j: 0)
    outs = pl.pallas_call(
        body,
        out_shape=(SDS(dproj.shape, BF16), SDS((s, cfg.D), BF16), SDS((DEINT, s // DEINT, cfg.D), BF16),
                   SDS((s, LANES), F32), SDS((DEINT, s // DEINT, LANES), F32), SDS((DEINT, s // DEINT, LANES), F32)),
        grid=(s // rows, h // hps),
        in_specs=[blk, blk, pl.BlockSpec((rows, WIDE), lambda b, j: (b, zb + j)), stat, HBM_SPEC],
        out_specs=(pl.BlockSpec((rows, WIDE), lambda b, j: (b, zb + j)), blk, _wide_spec(), stat, stat2, stat2),
        scratch_shapes=[pltpu.VMEM((rows, LANES), F32)],
        input_output_aliases={4: 0},
        compiler_params=_params(("parallel", "arbitrary")), name="attn_bwd_prep")(o_a, doag, proj, lse, dproj)
    dproj, do, do2, dl, dl2, lse2 = outs
    return dproj, do, do2.reshape(s, cfg.D), dl, dl2.reshape(s, LANES), lse2.reshape(s, LANES)


def _attn_grad_sum(cfg, g_1, g_2, col0, dproj, name):
    s = cfg.S
    c0 = col0 // WIDE
    rows = DEINT_ROWS

    def body(g1_ref, g2_ref, dp_in, o_ref, scr):
        del dp_in
        for t in range(WIDE // LANES):
            cs = slice(t * LANES, (t + 1) * LANES)
            for r in range(DEINT):
                scr.at[t][pl.ds(r, LANES, stride=DEINT), :] = g2_ref[r, :, cs].astype(F32)
            o_ref[:, cs] = (g1_ref[:, cs].astype(F32) + scr[t]).astype(BF16)

    return pl.pallas_call(
        body, out_shape=SDS(dproj.shape, BF16), grid=(s // rows, cfg.D // WIDE),
        in_specs=[pl.BlockSpec((rows, WIDE), lambda b, j: (b, j)), _wide_spec(), HBM_SPEC],
        out_specs=pl.BlockSpec((rows, WIDE), lambda b, j: (b, c0 + j)),
        scratch_shapes=[pltpu.VMEM((WIDE // LANES, rows, LANES), F32)],
        input_output_aliases={2: 0},
        compiler_params=_params(("parallel", "parallel")), name=name)(g_1, _by_residue(g_2), dproj)


CONV_HALO = 16
CONV_TR = 512
CONV_CW = 1024


def _rows_back(a, n):
    return a if n == 0 else pltpu.roll(a, n % a.shape[0], axis=0)


def _conv_fwd(cfg, proj, conv_w, conv_b):
    s, cd = cfg.S, cfg.CD
    tr, cw, hl = CONV_TR, CONV_CW, CONV_HALO
    cb0 = cfg.OXBC // cw

    def body(x_ref, h_ref, w_ref, b_ref, o_ref):
        i = pl.program_id(0)
        halo = jnp.where(i > 0, h_ref[...].astype(F32), 0.0)
        ext = jnp.concatenate([halo, x_ref[...].astype(F32)], axis=0)
        pre = b_ref[...] + jnp.zeros((tr, cw), F32)
        for k in range(CONV_K):
            pre = pre + w_ref[k:k + 1, :] * _rows_back(ext, CONV_K - 1 - k)[hl:]
        o_ref[...] = (pre * _sigmoid(pre)).astype(BF16)

    return pl.pallas_call(
        body, out_shape=SDS((s, cd), BF16), grid=(s // tr, cd // cw),
        in_specs=[pl.BlockSpec((tr, cw), lambda i, j: (i, cb0 + j)),
                  pl.BlockSpec((hl, cw), lambda i, j: (jnp.maximum(i * (tr // hl) - 1, 0), cb0 + j)),
                  pl.BlockSpec((CONV_K, cw), lambda i, j: (0, j)),
                  pl.BlockSpec((1, cw), lambda i, j: (0, j))],
        out_specs=pl.BlockSpec((tr, cw), lambda i, j: (i, j)),
        compiler_params=_params(("parallel", "parallel")), name="conv_fwd")(proj, proj, conv_w, conv_b)


def _conv_bwd(cfg, proj, dact, conv_w, conv_b, dproj):
    s, cd = cfg.S, cfg.CD
    tr, cw, hl = CONV_TR, CONV_CW, CONV_HALO
    cb0 = cfg.OXBC // cw
    nr = s // tr
    last_h = s // hl - 1

    def body(x_ref, hp_ref, hn_ref, d_ref, dn_ref, w_ref, b_ref, dp_in, dx_ref, gw_ref, gb_ref):
        del dp_in
        i = pl.program_id(1)
        ext = jnp.concatenate([jnp.where(i > 0, hp_ref[...].astype(F32), 0.0), x_ref[...].astype(F32),
                               hn_ref[...].astype(F32)], axis=0)
        shifted = [_rows_back(ext, CONV_K - 1 - k)[hl:] for k in range(CONV_K)]
        pre = b_ref[...] + jnp.zeros((tr + hl, cw), F32)
        for k in range(CONV_K):
            pre = pre + w_ref[k:k + 1, :] * shifted[k]
        sg = _sigmoid(pre)
        dact = jnp.concatenate([d_ref[...].astype(F32), jnp.where(i < nr - 1, dn_ref[...].astype(F32), 0.0)], axis=0)
        dpre = dact * (sg * (1.0 + pre * (1.0 - sg)))
        dx = jnp.zeros((tr, cw), F32)
        for k in range(CONV_K):
            dx = dx + w_ref[k:k + 1, :] * _rows_back(dpre, -(CONV_K - 1 - k))[0:tr]
        dx_ref[...] = dx.astype(BF16)

        @pl.when(i == 0)
        def _():
            gw_ref[...] = jnp.zeros_like(gw_ref)
            gb_ref[...] = jnp.zeros_like(gb_ref)

        dcur = dpre[0:tr]
        gb_ref[...] += jnp.sum(dcur, axis=0, keepdims=True)
        for k in range(CONV_K):
            gw_ref[k:k + 1, :] += jnp.sum(dcur * shifted[k][0:tr], axis=0, keepdims=True)

    return pl.pallas_call(
        body, out_shape=(SDS(dproj.shape, BF16), SDS((CONV_K, cd), F32), SDS((1, cd), F32)), grid=(cd // cw, nr),
        in_specs=[pl.BlockSpec((tr, cw), lambda j, i: (i, cb0 + j)),
                  pl.BlockSpec((hl, cw), lambda j, i: (jnp.maximum(i * (tr // hl) - 1, 0), cb0 + j)),
                  pl.BlockSpec((hl, cw), lambda j, i: (jnp.minimum((i + 1) * (tr // hl), last_h), cb0 + j)),
                  pl.BlockSpec((tr, cw), lambda j, i: (i, j)),
                  pl.BlockSpec((hl, cw), lambda j, i: (jnp.minimum((i + 1) * (tr // hl), last_h), j)),
                  pl.BlockSpec((CONV_K, cw), lambda j, i: (0, j)),
                  pl.BlockSpec((1, cw), lambda j, i: (0, j)),
                  pl.BlockSpec(memory_space=pl.ANY)],
        out_specs=(pl.BlockSpec((tr, cw), lambda j, i: (i, cb0 + j)),
                   pl.BlockSpec((CONV_K, cw), lambda j, i: (0, j)),
                   pl.BlockSpec((1, cw), lambda j, i: (0, j))),
        input_output_aliases={7: 0},
        compiler_params=_params(("parallel", "arbitrary")), name="conv_bwd")(
            proj, proj, proj, dact, dact, conv_w, conv_b, dproj)


def _expand(v, e, terms):
    out, rem = None, v
    for _ in range(terms):
        hi = rem.astype(BF16)
        t = _nn(hi, e)
        out = t if out is None else out + t
        rem = rem - hi.astype(F32)
    return out


def _segsum(v, e, terms):
    out, rem = None, v
    for _ in range(terms):
        hi = rem.astype(BF16)
        t = _nt(hi, e)
        out = t if out is None else out + t
        rem = rem - hi.astype(F32)
    return out


def _expand_row(row, e, terms):
    return _expand(jnp.broadcast_to(row, (8, LANES)), e, terms)[0:1]


def _segsum_row(row, e, terms):
    return _segsum(jnp.broadcast_to(row, (8, row.shape[1])), e, terms)[0:1]


def _expansion_matrix(cfg):
    hh = jnp.arange(LANES, dtype=jnp.int32)[:, None]
    cc = jnp.arange(cfg.SI, dtype=jnp.int32)[None, :]
    return (cc // SSM_HEAD_DIM == hh).astype(BF16)


def _tri(lower):
    r = lax.broadcasted_iota(jnp.int32, (CHUNK, CHUNK), 0)
    c = lax.broadcasted_iota(jnp.int32, (CHUNK, CHUNK), 1)
    return (c <= r) if lower else (c >= r)


def _ssd_prep(dtr_ref, db_ref, al_ref, e):
    dtr = dtr_ref[...] + db_ref[...]
    dt = _softplus(dtr)
    a = -jnp.exp(al_ref[...])
    acum = jnp.dot(_tri(True).astype(F32), dt * a, precision=lax.Precision.HIGHEST, preferred_element_type=F32)
    return dtr, dt, a, _expand(dt, e, 2), _expand(acum, e, 3)


def _ssd_fwd(cfg, xact, dt_raw, proj, dt_bias, a_log, d_skip, norm_w, e):
    s, si, cd, gw, bc = cfg.S, cfg.SI, cfg.CD, cfg.GW, cfg.BC
    nc = s // CHUNK
    zb = cfg.OZS // si
    tiles = gw // LANES

    def body(xa_ref, dtr_ref, z_ref, db_ref, al_ref, dsk_ref, nw_ref, e_ref, y_ref, y2_ref, st_ref,
             state, ybuf, x_s, xw_s, ae_s, ea_s, lam_s):
        @pl.when(pl.program_id(0) == 0)
        def _():
            state[...] = jnp.zeros_like(state)

        st_ref[...] = state[...]
        ev = e_ref[...]
        _, _, _, dt_e, a_e = _ssd_prep(dtr_ref, db_ref, al_ref, ev)
        xs = xa_ref[:, 0:si].astype(F32)
        x = xs * dt_e
        lam_e = a_e[CHUNK - 1:CHUNK, :]
        x_s[...] = x.astype(BF16)
        xw_s[...] = (x * jnp.exp(lam_e - a_e)).astype(BF16)
        ae_s[...] = a_e
        ea_s[...] = jnp.exp(a_e)
        ybuf[...] = _expand_row(dsk_ref[...], ev, 3) * xs
        lam_s[...] = jnp.broadcast_to(jnp.exp(lam_e), (8, si))
        tril = _tri(True)
        lane = lax.broadcasted_iota(jnp.int32, (CHUNK, LANES), 1)

        def group(g, carry):
            co = pl.multiple_of(g * gw, LANES)
            bg = xa_ref[:, pl.ds(pl.multiple_of(si + g * SSM_STATE, LANES), SSM_STATE)]
            cg = xa_ref[:, pl.ds(pl.multiple_of(si + bc + g * SSM_STATE, LANES), SSM_STATE)]
            cbm = _nt(cg, bg)
            st = state[:, pl.ds(co, gw)]
            yoff = _nn(cg, st.astype(BF16)) * ea_s[:, pl.ds(co, gw)]
            for k in range(tiles):
                tc = pl.multiple_of(co + k * LANES, LANES)
                at = ae_s[:, pl.ds(tc, LANES)]
                att = at.T
                xt = x_s[:, pl.ds(tc, LANES)]
                acc = yoff[:, k * LANES:(k + 1) * LANES]
                for half in range(2):
                    lo = half * SSM_HEAD_DIM
                    seg = at[:, lo:lo + 1] - att[lo:lo + 1, :]
                    dec = jnp.exp(jnp.where(tril, seg, NEG))
                    xh = jnp.where((lane >= lo) & (lane < lo + SSM_HEAD_DIM), xt, jnp.zeros_like(xt))
                    acc = acc + _nn((cbm * dec).astype(BF16), xh)
                ybuf[:, pl.ds(tc, LANES)] += acc
            state[:, pl.ds(co, gw)] = st * lam_s[0:1, pl.ds(co, gw)] + _tn(bg, xw_s[:, pl.ds(co, gw)])
            return carry

        lax.fori_loop(0, SSM_GROUPS, group, 0)
        y = ybuf[...]
        y_ref[...] = y.astype(BF16)
        z = z_ref[...].astype(F32)
        u = y * (z * _sigmoid(z))
        r = lax.rsqrt(jnp.mean(u * u, axis=-1, keepdims=True) + RMS_EPS)
        y2_ref[...] = (u * r * nw_ref[...]).astype(BF16)

    row = lambda n: pl.BlockSpec((1, n), lambda c: (0, 0))
    return pl.pallas_call(
        body,
        out_shape=(SDS((s, si), BF16), SDS((s, si), BF16), SDS((nc, SSM_STATE, si), F32)),
        grid=(nc,),
        in_specs=[pl.BlockSpec((CHUNK, cd), lambda c: (c, 0)),
                  pl.BlockSpec((CHUNK, LANES), lambda c: (c, 0)),
                  pl.BlockSpec((CHUNK, si), lambda c: (c, zb)),
                  row(LANES), row(LANES), row(LANES), row(si),
                  pl.BlockSpec((LANES, si), lambda c: (0, 0))],
        out_specs=(pl.BlockSpec((CHUNK, si), lambda c: (c, 0)),
                   pl.BlockSpec((CHUNK, si), lambda c: (c, 0)),
                   pl.BlockSpec((None, SSM_STATE, si), lambda c: (c, 0, 0))),
        scratch_shapes=[pltpu.VMEM((SSM_STATE, si), F32), pltpu.VMEM((CHUNK, si), F32),
                        pltpu.VMEM((CHUNK, si), BF16), pltpu.VMEM((CHUNK, si), BF16),
                        pltpu.VMEM((CHUNK, si), F32), pltpu.VMEM((CHUNK, si), F32),
                        pltpu.VMEM((8, si), F32)],
        compiler_params=_params(("arbitrary",)), name="ssd_fwd")(
            xact, dt_raw, proj, dt_bias, a_log, d_skip, norm_w, e)


def _ssd_bwd(cfg, xact, dt_raw, proj, y, dy2, states, dt_bias, a_log, d_skip, norm_w, e, dproj):
    s, si, cd, gw, bc, hpg = cfg.S, cfg.SI, cfg.CD, cfg.GW, cfg.BC, cfg.HPG
    nc = s // CHUNK
    zb = cfg.OZS // si
    tiles = gw // LANES

    def body(xa_ref, dtr_ref, z_ref, y_ref, d2_ref, st_ref, db_ref, al_ref, dsk_ref, nw_ref, e_ref, dp_in,
             dz_ref, dxa_ref, ddt_ref, gnw_ref, gdb_ref, gal_ref, gds_ref,
             dh, dhn, xs_s, x_s, w_s, ae_s, ea_s, g_s, dx_s, dae_s, r_s, lam_s, dle_s):
        del dp_in

        @pl.when(pl.program_id(0) == 0)
        def _():
            dh[...] = jnp.zeros_like(dh)
            gnw_ref[...] = jnp.zeros_like(gnw_ref)
            gdb_ref[...] = jnp.zeros_like(gdb_ref)
            gal_ref[...] = jnp.zeros_like(gal_ref)
            gds_ref[...] = jnp.zeros_like(gds_ref)

        ev = e_ref[...]
        yv = y_ref[...].astype(F32)
        z = z_ref[...].astype(F32)
        sg = _sigmoid(z)
        sz = z * sg
        u = yv * sz
        r = lax.rsqrt(jnp.mean(u * u, axis=-1, keepdims=True) + RMS_EPS)
        nrm = u * r
        d2 = d2_ref[...].astype(F32)
        gnw_ref[...] += jnp.sum(d2 * nrm, axis=0, keepdims=True)
        gn = d2 * nw_ref[...]
        du = r * (gn - nrm * jnp.mean(gn * nrm, axis=-1, keepdims=True))
        gv = du * sz
        dz_ref[...] = (du * yv * (sg * (1.0 + z * (1.0 - sg)))).astype(BF16)
        g_s[...] = gv

        dtr, dt, a, dt_e, a_e = _ssd_prep(dtr_ref, db_ref, al_ref, ev)
        xs = xa_ref[:, 0:si].astype(F32)
        x = xs * dt_e
        lam_e = a_e[CHUNK - 1:CHUNK, :]
        xs_s[...] = xs
        x_s[...] = x
        w_s[...] = jnp.exp(lam_e - a_e)
        ae_s[...] = a_e
        ea_s[...] = jnp.exp(a_e)
        lam_s[...] = jnp.broadcast_to(jnp.exp(lam_e), (8, si))
        gds_ref[...] += _segsum_row(jnp.sum(gv * xs, axis=0, keepdims=True), ev, 2)
        r_s[...] = jnp.zeros_like(r_s)
        tril = _tri(True)
        lane = lax.broadcasted_iota(jnp.int32, (CHUNK, LANES), 1)
        sub = lax.broadcasted_iota(jnp.int32, (CHUNK, LANES), 0)

        def group(g, carry):
            co = pl.multiple_of(g * gw, LANES)
            bo = pl.multiple_of(si + g * SSM_STATE, LANES)
            cof = pl.multiple_of(si + bc + g * SSM_STATE, LANES)
            cols = pl.ds(co, gw)
            bg = xa_ref[:, pl.ds(bo, SSM_STATE)]
            cg = xa_ref[:, pl.ds(cof, SSM_STATE)]
            cbm = _nt(cg, bg)
            st = st_ref[:, cols]
            stb = st.astype(BF16)
            dho = dh[:, cols]
            dhob = dho.astype(BF16)
            ea = ea_s[:, cols]
            gg = g_s[:, cols]
            xg = x_s[:, cols]
            wg = w_s[:, cols]
            explam = lam_s[0:1, cols]
            yoff = _nn(cg, stb) * ea
            ga = (gg * ea).astype(BF16)
            dc = _nt(ga, stb)
            dhn[:, cols] = dho * explam + _tn(cg, ga)
            bdh = _nn(bg, dhob)
            db = _nt((xg * wg).astype(BF16), dhob)
            t = xg * bdh * wg
            dle_s[0:1, cols] = jnp.sum(t, axis=0, keepdims=True) + explam * jnp.sum(dho * st, axis=0, keepdims=True)
            dae_base = gg * yoff - t
            dxw = wg * bdh
            dcb = jnp.zeros((CHUNK, CHUNK), F32)
            for k in range(tiles):
                tc = pl.multiple_of(co + k * LANES, LANES)
                ksl = slice(k * LANES, (k + 1) * LANES)
                at = ae_s[:, pl.ds(tc, LANES)]
                att = at.T
                xt = xg[:, ksl].astype(BF16)
                gt = gg[:, ksl].astype(BF16)
                dxt = dxw[:, ksl]
                place = jnp.zeros((CHUNK, LANES), F32)
                for half in range(2):
                    lo = half * SSM_HEAD_DIM
                    seg = at[:, lo:lo + 1] - att[lo:lo + 1, :]
                    dec = jnp.exp(jnp.where(tril, seg, NEG))
                    mh = cbm * dec
                    gh = jnp.where((lane >= lo) & (lane < lo + SSM_HEAD_DIM), gt, jnp.zeros_like(gt))
                    dm = _nt(gh, xt)
                    dxt = dxt + _tn(mh.astype(BF16), gh)
                    dcb = dcb + dm * dec
                    dseg = dm * mh
                    place = place + jnp.where(lane == lo, jnp.sum(dseg, axis=1, keepdims=True), 0.0)
                    hidx = g * hpg + 2 * k + half
                    r_s[...] += jnp.where(sub == hidx, jnp.sum(dseg, axis=0, keepdims=True), 0.0)
                dx_s[:, pl.ds(tc, LANES)] = dxt
                dae_s[:, pl.ds(tc, LANES)] = dae_base[:, ksl] + place
            dcbb = dcb.astype(BF16)
            dxa_ref[:, pl.ds(bo, SSM_STATE)] = (db + _tn(dcbb, cg)).astype(BF16)
            dxa_ref[:, pl.ds(cof, SSM_STATE)] = (dc + _nn(dcbb, bg)).astype(BF16)
            return carry

        lax.fori_loop(0, SSM_GROUPS, group, 0)
        dlam = _segsum_row(dle_s[0:1, :], ev, 2)
        da_ = _segsum(dae_s[...], ev, 2) - r_s[...].T
        da_ = da_ + jnp.where(sub == CHUNK - 1, dlam, 0.0)
        dda = jnp.dot(_tri(False).astype(F32), da_, precision=lax.Precision.HIGHEST, preferred_element_type=F32)
        dxv = dx_s[...]
        xs = xs_s[...]
        ddt = dda * a + _segsum(dxv * xs, ev, 2)
        gal_ref[...] += jnp.sum(dda * dt, axis=0, keepdims=True) * a
        ddtr = ddt * _sigmoid(dtr)
        gdb_ref[...] += jnp.sum(ddtr, axis=0, keepdims=True)
        ddt_ref[...] = ddtr
        dxa_ref[:, 0:si] = (dxv * dt_e + g_s[...] * _expand_row(dsk_ref[...], ev, 3)).astype(BF16)
        dh[...] = dhn[...]

    rev = lambda c: nc - 1 - c
    row = lambda n: pl.BlockSpec((1, n), lambda c: (0, 0))
    big = lambda: pltpu.VMEM((CHUNK, si), F32)
    return pl.pallas_call(
        body,
        out_shape=(SDS(dproj.shape, BF16), SDS((s, cd), BF16), SDS((s, LANES), F32),
                   SDS((1, si), F32), SDS((1, LANES), F32), SDS((1, LANES), F32), SDS((1, LANES), F32)),
        grid=(nc,),
        in_specs=[pl.BlockSpec((CHUNK, cd), lambda c: (rev(c), 0)),
                  pl.BlockSpec((CHUNK, LANES), lambda c: (rev(c), 0)),
                  pl.BlockSpec((CHUNK, si), lambda c: (rev(c), zb)),
                  pl.BlockSpec((CHUNK, si), lambda c: (rev(c), 0)),
                  pl.BlockSpec((CHUNK, si), lambda c: (rev(c), 0)),
                  pl.BlockSpec((None, SSM_STATE, si), lambda c: (rev(c), 0, 0)),
                  row(LANES), row(LANES), row(LANES), row(si),
                  pl.BlockSpec((LANES, si), lambda c: (0, 0)),
                  pl.BlockSpec(memory_space=pl.ANY)],
        out_specs=(pl.BlockSpec((CHUNK, si), lambda c: (rev(c), zb)),
                   pl.BlockSpec((CHUNK, cd), lambda c: (rev(c), 0)),
                   pl.BlockSpec((CHUNK, LANES), lambda c: (rev(c), 0)),
                   row(si), row(LANES), row(LANES), row(LANES)),
        scratch_shapes=[pltpu.VMEM((SSM_STATE, si), F32), pltpu.VMEM((SSM_STATE, si), F32),
                        big(), big(), big(), big(), big(), big(), big(), big(),
                        pltpu.VMEM((CHUNK, LANES), F32), pltpu.VMEM((8, si), F32), pltpu.VMEM((8, si), F32)],
        input_output_aliases={11: 0},
        compiler_params=_params(("arbitrary",)), name="ssd_bwd")(
            xact, dt_raw, proj, y, dy2, states, dt_bias, a_log, d_skip, norm_w, e, dproj)


MERGE_TR = 512
MERGE_CW = 2048


def _merge_fwd(cfg, proj, a_br, s_br):
    s, d = cfg.S, cfg.D
    tr, cw = MERGE_TR, min(MERGE_CW, d)
    ga0, gs0 = cfg.OGA // cw, cfg.OGS // cw

    def body(ga_ref, gs_ref, a_ref, s_ref, o_ref):
        o_ref[...] = (_sigmoid(ga_ref[...].astype(F32)) * a_ref[...].astype(F32)
                      + _sigmoid(gs_ref[...].astype(F32)) * s_ref[...].astype(F32)).astype(BF16)

    blk = pl.BlockSpec((tr, cw), lambda i, j: (i, j))
    return pl.pallas_call(
        body, out_shape=SDS((s, d), BF16), grid=(s // tr, d // cw),
        in_specs=[pl.BlockSpec((tr, cw), lambda i, j: (i, ga0 + j)),
                  pl.BlockSpec((tr, cw), lambda i, j: (i, gs0 + j)), blk, blk],
        out_specs=blk, compiler_params=_params(("parallel", "parallel")), name="merge_fwd")(proj, proj, a_br, s_br)


def _merge_bwd(cfg, proj, branch, dmerged, gate_off, dproj, name):
    s, d = cfg.S, cfg.D
    tr, cw = MERGE_TR, min(MERGE_CW, d)
    g0 = gate_off // cw
    fresh = dproj is None

    def body(*refs):
        g_ref, b_ref, dm_ref = refs[:3]
        dg_ref, db_ref = refs[-2:]
        dm = dm_ref[...].astype(F32)
        sg = _sigmoid(g_ref[...].astype(F32))
        db_ref[...] = (dm * sg).astype(BF16)
        dg_ref[...] = (dm * b_ref[...].astype(F32) * sg * (1.0 - sg)).astype(BF16)

    blk = pl.BlockSpec((tr, cw), lambda i, j: (i, j))
    gate = pl.BlockSpec((tr, cw), lambda i, j: (i, g0 + j))
    return pl.pallas_call(
        body, out_shape=(SDS((s, cfg.NM), BF16), SDS((s, d), BF16)), grid=(s // tr, d // cw),
        in_specs=[gate, blk, blk] + ([] if fresh else [HBM_SPEC]),
        out_specs=(gate, blk),
        input_output_aliases={} if fresh else {3: 0},
        compiler_params=_params(("parallel", "parallel")), name=name)(
            *((proj, branch, dmerged) + (() if fresh else (dproj,))))


def _outproj_loss(merged, w_out, x, target, fnw):
    s, d = x.shape
    tr = 256

    def body(m_ref, w_ref, x_ref, t_ref, fw_ref, dof_ref, dob_ref, loss_ref, g_ref):
        out = x_ref[...] + _nn(m_ref[...], w_ref[...])
        r = lax.rsqrt(jnp.mean(out * out, axis=-1, keepdims=True) + RMS_EPS)
        nrm = out * r
        fw = fw_ref[...]
        err = nrm * fw - t_ref[...]
        dy = err * (1.0 / d)
        gy = dy * fw
        dout = r * (gy - nrm * jnp.mean(gy * nrm, axis=-1, keepdims=True))
        dof_ref[...] = dout
        dob_ref[...] = dout.astype(BF16)

        @pl.when(pl.program_id(0) == 0)
        def _():
            loss_ref[...] = jnp.zeros_like(loss_ref)
            g_ref[...] = jnp.zeros_like(g_ref)

        loss_ref[...] += jnp.sum(jnp.sum(err * err, axis=1, keepdims=True), axis=0, keepdims=True) * (0.5 / d)
        g_ref[...] += jnp.sum(dy * nrm, axis=0, keepdims=True)

    blk = pl.BlockSpec((tr, d), lambda i: (i, 0))
    return pl.pallas_call(
        body, out_shape=(SDS((s, d), F32), SDS((s, d), BF16), SDS((1, LANES), F32), SDS((1, d), F32)), grid=(s // tr,),
        in_specs=[blk, pl.BlockSpec((d, d), lambda i: (0, 0)), blk, blk, pl.BlockSpec((1, d), lambda i: (0, 0))],
        out_specs=(blk, blk, pl.BlockSpec((1, LANES), lambda i: (0, 0)), pl.BlockSpec((1, d), lambda i: (0, 0))),
        compiler_params=_params(("arbitrary",)), name="outproj_loss")(merged, w_out, x, target, fnw)


ELEMWISE_BLOCK_BYTES = 1 << 20


def _row_block(rows, cols, itemsize=4):
    best = None
    for tr in range(16, rows + 1, 16):
        if rows % tr == 0 and tr * cols * itemsize <= ELEMWISE_BLOCK_BYTES:
            best = tr
    return best if best is not None else rows


def _adamw(w, g, m, v, name):
    rows, cols = w.shape
    tr = _row_block(rows, cols)
    if rows // tr > 64 and cols % LANES == 0:
        blk, grid = pl.BlockSpec((rows, LANES), lambda i: (0, i)), (cols // LANES,)
    else:
        blk, grid = pl.BlockSpec((tr, cols), lambda i: (i, 0)), (rows // tr,)
    out = SDS((rows, cols), F32)
    return pl.pallas_call(
        _adamw_body(), out_shape=(out, out, out), grid=grid, in_specs=[blk] * 4, out_specs=(blk,) * 3,
        compiler_params=_params(("parallel",)), name=name)(w, g, m, v)


def _adamw_body():
    def body(w_ref, g_ref, m_ref, v_ref, d_ref, nm_ref, nv_ref):
        gv = g_ref[...]
        nm = ADAM_B1 * m_ref[...] + (1.0 - ADAM_B1) * gv
        nv = ADAM_B2 * v_ref[...] + (1.0 - ADAM_B2) * jnp.square(gv)
        m_hat = nm / (1.0 - ADAM_B1 ** ADAM_STEP)
        v_hat = nv / (1.0 - ADAM_B2 ** ADAM_STEP)
        d_ref[...] = -ADAM_LR * (m_hat / (jnp.sqrt(v_hat) + ADAM_EPS) + ADAM_WD * w_ref[...])
        nm_ref[...] = nm
        nv_ref[...] = nv

    return body


HBM_SPEC = pl.BlockSpec(memory_space=pl.ANY)


def _position():
    return lax.axis_index("x"), lax.axis_index("y"), lax.axis_index("c")


class _Carry:
    def __init__(self, arrays, out_shapes, sems, start, finish):
        self.arrays, self.out_shapes, self.sems, self.start, self.finish = list(arrays), out_shapes, sems, start, finish

    def sem_shapes(self):
        return [pltpu.SemaphoreType.DMA((k,)) for k in self.sems]


def _gather_carry(shards, by_cols=()):
    n = len(shards)

    def copies(ins, outs, sems):
        send_sems, recv_sems, fsend_sems, frecv_sems = sems
        x, y, c = _position()
        me = 2 * x + y
        peers = [(1 - x, y), (x, 1 - y), (1 - x, 1 - y)]

        def half_of(t, chip, half):
            if t in by_cols:
                c2 = ins[t].shape[1] // 2
                return outs[t].at[chip, :, pl.ds(half * c2, c2)]
            return outs[t].at[chip, half]

        def over_ici(t, p, chip):
            px, py = peers[p]
            if t in by_cols:
                c2 = ins[t].shape[1] // 2
                src = ins[t].at[:, pl.ds(c * c2, c2)]
            else:
                r2 = ins[t].shape[0] // 2
                src = ins[t].at[pl.ds(c * r2, r2), :]
            return pltpu.make_async_remote_copy(
                src_ref=src, dst_ref=half_of(t, chip, c), send_sem=send_sems.at[3 * t + p],
                recv_sem=recv_sems.at[3 * t + p], device_id=(px, py, c), device_id_type=MESH)

        def to_sibling(t, p, half):
            px, py = peers[p]
            slab = half_of(t, 2 * px + py, half)
            return pltpu.make_async_remote_copy(
                src_ref=slab, dst_ref=slab, send_sem=fsend_sems.at[3 * t + p], recv_sem=frecv_sems.at[3 * t + p],
                device_id=(x, y, 1 - c), device_id_type=MESH)

        pairs = [(t, p) for t in range(n) for p in range(3)]
        sends = [over_ici(t, p, me) for t, p in pairs]
        lands = [over_ici(t, p, 2 * peers[p][0] + peers[p][1]) for t, p in pairs]
        passed = [to_sibling(t, p, c) for t, p in pairs]
        from_sibling = [to_sibling(t, p, 1 - c) for t, p in pairs]
        return sends, lands, passed, from_sibling

    def start(ins, outs, sems):
        for cp in copies(ins, outs, sems)[0]:
            cp.start()

    def finish(ins, outs, sems):
        sends, lands, passed, from_sibling = copies(ins, outs, sems)
        for land, fwd in zip(lands, passed):
            land.wait_recv()
            fwd.start()
        for cp in from_sibling:
            cp.wait_recv()
        for cp in sends + passed:
            cp.wait_send()

    shapes = [SDS((N_CHIPS,) + a.shape if t in by_cols else (N_CHIPS, 2, a.shape[0] // 2, a.shape[1]), a.dtype)
              for t, a in enumerate(shards)]
    return _Carry(shards, shapes, [3 * n] * 4, start, finish)


def _scatter_carry(parts):
    def start(ins, outs, sems):
        for cp in _scatter_copies(ins, outs, *sems)[0]:
            cp.start()

    def finish(ins, outs, sems):
        sends, lands = _scatter_copies(ins, outs, *sems)
        for cp in lands:
            cp.wait_recv()
        for cp in sends:
            cp.wait_send()

    return _Carry(parts, [SDS(a.shape, a.dtype) for a in parts], [3 * len(parts)] * 2, start, finish)


def _with_own(gathered, own, chip):
    full = gathered.reshape((N_CHIPS,) + own.shape)
    return lax.dynamic_update_index_in_dim(full, own, chip, 0)


def _exchange_halves(grads):
    n = len(grads)
    slabs = [list(g) if isinstance(g, (list, tuple)) else [g] for g in grads]
    flat = [a for s in slabs for a in s]
    ncp = len(flat)

    def body(*refs):
        ins, outs = refs[:ncp], refs[ncp:ncp + n]
        send_sems, recv_sems = refs[ncp + n:]
        x, y, c = _position()
        cps, k = [], 0
        for t in range(n):
            for j in range(len(slabs[t])):
                if len(slabs[t]) == 1:
                    r2 = ins[k].shape[1] // 2
                    src, dst = ins[k].at[:, pl.ds((1 - c) * r2, r2), :], outs[t]
                else:
                    r2 = ins[k].shape[0] // 2
                    src, dst = ins[k].at[pl.ds((1 - c) * r2, r2), :], outs[t].at[j]
                cps.append(pltpu.make_async_remote_copy(
                    src_ref=src, dst_ref=dst, send_sem=send_sems.at[k], recv_sem=recv_sems.at[k],
                    device_id=(x, y, 1 - c), device_id_type=MESH))
                k += 1
        for cp in cps:
            cp.start()
        for cp in cps:
            cp.wait()

    def landing(s):
        a = s[0]
        return SDS((N_CHIPS, a.shape[-2] // 2, a.shape[-1]), a.dtype)

    return pl.pallas_call(
        body, out_shape=[landing(s) for s in slabs],
        in_specs=[HBM_SPEC] * ncp, out_specs=[HBM_SPEC] * n,
        scratch_shapes=[pltpu.SemaphoreType.DMA((ncp,)), pltpu.SemaphoreType.DMA((ncp,))],
        compiler_params=pltpu.CompilerParams(has_side_effects=True), name="reduce_sibling")(*flat)


def _scatter_copies(ins, outs, send_sems, recv_sems):
    x, y, c = _position()
    me = 2 * x + y
    peers = [(1 - x, y), (x, 1 - y), (1 - x, 1 - y)]

    def remote(t, p, src_slab, dst_slab):
        px, py = peers[p]
        return pltpu.make_async_remote_copy(
            src_ref=ins[t].at[src_slab], dst_ref=outs[t].at[dst_slab], send_sem=send_sems.at[3 * t + p],
            recv_sem=recv_sems.at[3 * t + p], device_id=(px, py, c), device_id_type=MESH)

    n = len(ins)
    sends = [remote(t, p, 2 * peers[p][0] + peers[p][1], me) for t in range(n) for p in range(3)]
    lands = [remote(t, p, me, 2 * peers[p][0] + peers[p][1]) for t in range(n) for p in range(3)]
    return sends, lands


def _share_halves(halves):
    n = len(halves)

    def body(*refs):
        ins, outs = refs[:n], refs[n:2 * n]
        send_sems, recv_sems = refs[2 * n:]
        x, y, c = _position()

        def copy(t, slab):
            return pltpu.make_async_remote_copy(
                src_ref=ins[t].at[slab], dst_ref=outs[t].at[slab], send_sem=send_sems.at[t], recv_sem=recv_sems.at[t],
                device_id=(x, y, 1 - c), device_id_type=MESH)

        for t in range(n):
            copy(t, c).start()
        for t in range(n):
            copy(t, 1 - c).wait_recv()
        for t in range(n):
            copy(t, c).wait_send()

    return pl.pallas_call(
        body, out_shape=[SDS(a.shape, a.dtype) for a in halves],
        in_specs=[HBM_SPEC] * n, out_specs=[HBM_SPEC] * n,
        scratch_shapes=[pltpu.SemaphoreType.DMA((n,)), pltpu.SemaphoreType.DMA((n,))],
        input_output_aliases={t: t for t in range(n)},
        compiler_params=pltpu.CompilerParams(has_side_effects=True), name="share_sibling")(*halves)


def _add_sibling_slab(grad_j, recv, core, j, sums):
    nch, r2, cols = recv.shape
    tr = _row_block(r2, cols)
    nb = r2 // tr
    fresh = sums is None

    def body(c_ref, g_ref, r_ref, *rest):
        del c_ref
        rest[-1][...] = (g_ref[...].astype(F32) + r_ref[...].astype(F32)).astype(BF16)

    return pl.pallas_call(
        body, out_shape=SDS(recv.shape, BF16),
        grid_spec=pltpu.PrefetchScalarGridSpec(
            num_scalar_prefetch=1, grid=(nb,),
            in_specs=[pl.BlockSpec((tr, cols), lambda i, c_ref: (c_ref[0] * nb + i, 0)),
                      pl.BlockSpec((None, tr, cols), lambda i, c_ref: (j, i, 0))] + ([] if fresh else [HBM_SPEC]),
            out_specs=pl.BlockSpec((None, tr, cols), lambda i, c_ref: (j, i, 0))),
        input_output_aliases={} if fresh else {3: 0},
        compiler_params=_params(("parallel",)), name="add_sibling_slab")(
            *((core, grad_j, recv) + (() if fresh else (sums,))))


def _add_sibling(grad, recv, core):
    if isinstance(grad, (list, tuple)):
        sums = None
        for j, g in enumerate(grad):
            sums = _add_sibling_slab(g, recv, core, j, sums)
        return sums
    nch, r2, cols = recv.shape
    tr = _row_block(r2, cols)
    nb = r2 // tr

    def body(c_ref, g_ref, r_ref, o_ref):
        del c_ref
        o_ref[...] = (g_ref[...].astype(F32) + r_ref[...].astype(F32)).astype(BF16)

    return pl.pallas_call(
        body, out_shape=SDS(recv.shape, BF16),
        grid_spec=pltpu.PrefetchScalarGridSpec(
            num_scalar_prefetch=1, grid=(nch, nb),
            in_specs=[pl.BlockSpec((None, tr, cols), lambda j, i, c_ref: (j, c_ref[0] * nb + i, 0)),
                      pl.BlockSpec((None, tr, cols), lambda j, i, c_ref: (j, i, 0))],
            out_specs=pl.BlockSpec((None, tr, cols), lambda j, i, c_ref: (j, i, 0))),
        compiler_params=_params(("parallel", "parallel")), name="add_sibling")(core, grad, recv)


def _add_chips(own, recv, chip_core):
    nch, r2, cols = recv.shape
    tr = _row_block(r2, cols)

    nsc = 2 + nch

    def body(*refs):
        me = refs[0][0]
        own_ref, p_refs, o_ref = refs[nsc], refs[nsc + 1:nsc + 1 + nch], refs[nsc + 1 + nch]
        acc = None
        for j in range(nch):
            term = jnp.where(me == j, own_ref[...], p_refs[j][...]).astype(F32)
            acc = term if acc is None else acc + term
        o_ref[...] = acc

    def slab(j):
        return pl.BlockSpec((None, tr, cols), lambda i, *sc: (sc[2 + j][0], i, 0))

    return pl.pallas_call(
        body, out_shape=SDS((2, r2, cols), F32),
        grid_spec=pltpu.PrefetchScalarGridSpec(
            num_scalar_prefetch=nsc, grid=(r2 // tr,),
            in_specs=[pl.BlockSpec((None, tr, cols), lambda i, *sc: (sc[0][0], i, 0))] + [slab(j) for j in range(nch)],
            out_specs=pl.BlockSpec((None, tr, cols), lambda i, *sc: (sc[1][0], i, 0))),
        compiler_params=_params(("parallel",)), name="add_chips")(*chip_core, own, *([recv] * nch))


def _allreduce_small(pack):
    rows = pack.shape[0]

    def body(p_ref, o_ref, buf, send_sems, recv_sems):
        x, y, c = _position()
        me = 4 * x + 2 * y + c
        buf[me] = p_ref[...]

        def copy(dst_dev, slot):
            return pltpu.make_async_remote_copy(
                src_ref=p_ref, dst_ref=buf.at[slot], send_sem=send_sems.at[dst_dev], recv_sem=recv_sems.at[slot],
                device_id=(dst_dev // 4, (dst_dev // 2) % 2, dst_dev % 2), device_id_type=MESH)

        for dev in range(N_DEV):
            @pl.when(dev != me)
            def _():
                copy(dev, me).start()
        for dev in range(N_DEV):
            @pl.when(dev != me)
            def _():
                copy(dev, dev).wait_recv()
        for dev in range(N_DEV):
            @pl.when(dev != me)
            def _():
                copy(dev, me).wait_send()
        acc = buf[0]
        for dev in range(1, N_DEV):
            acc = acc + buf[dev]
        o_ref[...] = acc

    return pl.pallas_call(
        body, out_shape=SDS(pack.shape, F32),
        in_specs=[pl.BlockSpec(memory_space=pltpu.VMEM)], out_specs=pl.BlockSpec(memory_space=pltpu.VMEM),
        scratch_shapes=[pltpu.VMEM((N_DEV, rows, LANES), F32), pltpu.SemaphoreType.DMA((N_DEV,)),
                        pltpu.SemaphoreType.DMA((N_DEV,))],
        compiler_params=pltpu.CompilerParams(has_side_effects=True), name="allreduce_small")(pack)


ATTN_TQ = 256


def _local_step(cfg, x, target, w, to_chips=None, late=None, hn=None):
    d = cfg.D
    if hn is None:
        hn = _rmsnorm_fwd(x, w["norm_w"])
    proj = _mm(hn, w["w_main_t"], "nt", BF16, "proj_main", carry=late[0] if late else None)
    if late:
        proj, arrived = proj
        w = {**w, **late[1](arrived)}
    dt_raw = _mm(hn, w["w_dt_t"], "nt", F32, "proj_dt")
    slopes = _slopes(cfg.H)
    near = _Pass(ATTN_TQ, DILATED_PATTERNS[:-1], 1, cfg.S)
    far = _Pass(LANES, DILATED_PATTERNS[-1:], DEINT, cfg.S // DEINT)
    tab_near, tab_far = _attn_tables(near), _attn_tables(far)
    cols_near, cols_far = (cfg.OQ, cfg.OK, cfg.OV), (0, d, 2 * d)
    qkv_far = _deinterleave(proj, 0, 3 * d, "attn_deinterleave")
    o_1, lse_1 = _attn_fwd(cfg, near, proj, cols_near, tab_near, slopes, "attn_fwd_near")
    o_2, lse_2 = _attn_fwd(cfg, far, qkv_far, cols_far, tab_far, slopes, "attn_fwd_far")
    o_a, oag, lse = _attn_merge(cfg, proj, o_1, lse_1, o_2, lse_2)
    xact = _conv_fwd(cfg, proj, w["conv_w"], w["conv_b"])
    e = _expansion_matrix(cfg)
    y, y2, states = _ssd_fwd(cfg, xact, dt_raw, proj, w["dt_bias"], w["a_log"], w["d_skip"], w["ssm_norm_w"], e)
    a_br = _mm(oag, w["w_attn"], "nn", BF16, "branch_attn")
    s_br = _mm(y2, w["w_ssm"], "nn", BF16, "branch_ssm")
    merged = _merge_fwd(cfg, proj, a_br, s_br)
    dout_f, dout_b, loss_row, g_fnw = _outproj_loss(merged, w["w_out"], x, target, w["final_norm_w"])

    dmerged = _mm(dout_b, w["w_out"], "nt", BF16, "d_merged")
    g_w_out = _mm(merged, dout_b, "tn", BF16, "g_w_out")
    dproj, da_br = _merge_bwd(cfg, proj, a_br, dmerged, cfg.OGA, None, "merge_bwd_attn")
    dproj, ds_br = _merge_bwd(cfg, proj, s_br, dmerged, cfg.OGS, dproj, "merge_bwd_ssm")
    doag = _mm(da_br, w["w_attn"], "nt", BF16, "d_oag")
    g_w_attn = _mm(oag, da_br, "tn", BF16, "g_w_attn")
    dy2 = _mm(ds_br, w["w_ssm"], "nt", BF16, "d_y2")
    g_w_ssm = _mm(y2, ds_br, "tn", BF16, "g_w_ssm")
    dproj, dxact, ddt, g_snw, g_dtb, g_alog, g_dsk = _ssd_bwd(
        cfg, xact, dt_raw, proj, y, dy2, states, w["dt_bias"], w["a_log"], w["d_skip"], w["ssm_norm_w"], e, dproj)
    dproj, g_cw, g_cb = _conv_bwd(cfg, proj, dxact, w["conv_w"], w["conv_b"], dproj)
    dproj, do, do_far, dl, dl_far, lse_far = _attn_bwd_prep(cfg, proj, o_a, doag, lse, dproj)
    g_near = _attn_bwd(cfg, near, proj, cols_near, do, lse, dl, tab_near, slopes, "attn_bwd_near")
    g_far = _attn_bwd(cfg, far, qkv_far, cols_far, do_far, lse_far, dl_far, tab_far, slopes, "attn_bwd_far")
    for g_1, g_2, col0, nm in zip(g_near, g_far, cols_near, ("attn_dq", "attn_dk", "attn_dv")):
        dproj = _attn_grad_sum(cfg, g_1, g_2, col0, dproj, nm)
    ddt_b = ddt.astype(BF16)
    g_w_main = _mm(hn, dproj, "tn", BF16, "g_w_main")
    g_w_dt = _mm(hn, ddt_b, "tn", BF16, "g_w_dt")
    grads = dict(w_main=g_w_main, w_dt=g_w_dt, conv_w=g_cw, conv_b=g_cb, dt_bias=g_dtb, a_log=g_alog,
                 d_skip=g_dsk, ssm_norm_w=g_snw, w_attn=g_w_attn, w_ssm=g_w_ssm, w_out=g_w_out, final_norm_w=g_fnw)
    sent = to_chips(grads) if to_chips is not None else ()
    dhn = _mm(dproj, w["w_main_t"], "nn", F32, "d_hn", tk=1024, carry=_scatter_carry(sent) if sent else None)
    landed = ()
    if sent:
        dhn, landed = dhn
    dhn_dt = _mm(ddt_b, w["w_dt_t"], "nn", F32, "d_hn_dt")
    grad_x, grads["norm_w"] = _rmsnorm_bwd(x, w["norm_w"], dhn, dhn_dt, dout_f)
    return loss_row, grad_x, grads, sent, landed


def _pad_lanes(v):
    return jnp.pad(v, ((0, 0), (0, LANES - v.shape[1])))


def _cut(lo, hi, a, b):
    a, b = max(lo, a), min(hi, b)
    return (a, b) if a < b else None


def _main_from_rows(cfg, w_in_t):
    lo, hi = cfg.OGA, cfg.OGA + cfg.NH
    main = jnp.concatenate([w_in_t[:lo], w_in_t[hi:]], axis=0)
    return main, jnp.pad(w_in_t[lo:hi], ((0, LANES - cfg.NH), (0, 0)))


def _shards_from_main(cfg, g_main, g_dt, n):
    per = cfg.N_IN // n
    out = []
    for j in range(n):
        lo, hi = j * per, (j + 1) * per
        parts = []
        for src, off, rng in ((g_main, 0, (0, cfg.OGA)), (g_dt, cfg.OGA, (cfg.OGA, cfg.OGA + cfg.NH)),
                              (g_main, cfg.NH, (cfg.OGA + cfg.NH, cfg.N_IN))):
            c = _cut(lo, hi, *rng)
            if c is not None:
                parts.append(src[:, c[0] - off:c[1] - off])
        out.append(jnp.concatenate(parts, axis=1) if len(parts) > 1 else parts[0])
    return out


def _full_weights(cfg, norm_w, w_in_t, conv_w, conv_b, dt_bias, a_log, d_skip, ssm_norm_w, w_attn, w_ssm, w_out, fnw):
    w_main, w_dt = _main_from_rows(cfg, w_in_t)
    return dict(norm_w=norm_w, w_main_t=w_main.astype(BF16), w_dt_t=w_dt.astype(BF16), conv_w=conv_w, conv_b=conv_b,
                dt_bias=_pad_lanes(dt_bias), a_log=_pad_lanes(a_log), d_skip=_pad_lanes(d_skip), ssm_norm_w=ssm_norm_w,
                final_norm_w=fnw, **{k: v.astype(BF16) for k, v in (("w_attn", w_attn), ("w_ssm", w_ssm), ("w_out", w_out))
                                     if v is not None})


def kernel(x, norm_w, w_in, conv_w, conv_b, dt_bias, a_log, d_skip, ssm_norm_w, w_attn_branch, w_ssm_branch, w_out, final_norm_w, loss_target, m_norm_w, m_w_in, m_conv_w, m_conv_b, m_dt_bias, m_a_log, m_d_skip, m_ssm_norm_w, m_w_attn_branch, m_w_ssm_branch, m_w_out, m_final_norm_w, v_norm_w, v_w_in, v_conv_w, v_conv_b, v_dt_bias, v_a_log, v_d_skip, v_ssm_norm_w, v_w_attn_branch, v_w_ssm_branch, v_w_out, v_final_norm_w):
    cfg = _Cfg(x.shape[1], x.shape[2])
    d, si, cd, nh = cfg.D, cfg.SI, cfg.CD, cfg.NH
    chip = 2 * lax.axis_index("x") + lax.axis_index("y")
    core = lax.axis_index("c").astype(jnp.int32).reshape(1)
    chip = chip.astype(jnp.int32)
    chip_core = [chip.reshape(1), core] + [jnp.where(chip == j, (j + 1) % N_CHIPS, j).astype(jnp.int32).reshape(1)
                                           for j in range(N_CHIPS)]

    own = [jnp.transpose(w_in[0]).astype(BF16), conv_w[0].reshape(4 * CONV_K, -1)]
    hn, gathered = _rmsnorm_fwd(x[0], norm_w, carry=_gather_carry(own, by_cols=(0,)))
    a_in, a_cw = [_with_own(g, o, chip) for g, o in zip(gathered, own)]
    conv_w_full = a_cw.reshape(N_CHIPS, CONV_K, cd // N_CHIPS).transpose(1, 0, 2).reshape(CONV_K, cd)
    w = _full_weights(cfg, norm_w, a_in.reshape(cfg.N_IN, d), conv_w_full, conv_b, dt_bias, a_log, d_skip,
                      ssm_norm_w, None, None, None, final_norm_w.reshape(1, d))
    own_late = [w_attn_branch[0].astype(BF16), w_ssm_branch[0].astype(BF16), w_out[0].astype(BF16)]

    def late_weights(arrived):
        a_attn, a_ssm, a_out = [_with_own(g, o, chip) for g, o in zip(arrived, own_late)]
        return dict(w_attn=a_attn.reshape(d, d), w_ssm=a_ssm.reshape(si, d), w_out=a_out.reshape(d, d))

    def to_chips(grads):
        by_chip = [_shards_from_main(cfg, grads["w_main"], grads["w_dt"], N_CHIPS),
                   grads["w_attn"].reshape(N_CHIPS, d // N_CHIPS, d),
                   grads["w_ssm"].reshape(N_CHIPS, si // N_CHIPS, d),
                   grads["w_out"].reshape(N_CHIPS, d // N_CHIPS, d)]
        from_sibling = _exchange_halves(by_chip)
        return [_add_sibling(g, r, core) for g, r in zip(by_chip, from_sibling)]

    loss_row, grad_x, grads, chip_sums, from_chips = _local_step(
        cfg, x[0], loss_target[0], w, to_chips, (_gather_carry(own_late), late_weights), hn)
    halves = [_add_chips(o, p, chip_core) for o, p in zip(chip_sums, from_chips)]
    g_in, g_attn, g_ssm, g_out = [h.reshape(2 * h.shape[1], h.shape[2]) for h in _share_halves(halves)]

    small = [loss_row, grads["norm_w"], grads["conv_b"], grads["dt_bias"], grads["a_log"], grads["d_skip"],
             grads["ssm_norm_w"], grads["final_norm_w"], grads["conv_w"].reshape(1, CONV_K * cd)]
    sizes = [a.shape[1] for a in small]
    total = sum(sizes)
    rows = -(-total // (8 * LANES)) * 8
    flat = jnp.pad(jnp.concatenate(small, axis=1), ((0, 0), (0, rows * LANES - total)))
    red = _allreduce_small(flat.reshape(rows, LANES)).reshape(1, rows * LANES)
    offs = [sum(sizes[:i]) for i in range(len(sizes))]
    loss_r, g_nw, g_cb, g_dtb, g_alog, g_dsk, g_snw, g_fnw, g_cw_flat = [
        red[:, o:o + n] for o, n in zip(offs, sizes)]
    loss = loss_r[0, 0]
    g_dtb, g_alog, g_dsk = g_dtb[:, :nh], g_alog[:, :nh], g_dsk[:, :nh]
    cshard = cd // N_CHIPS
    g_cw = lax.dynamic_slice_in_dim(g_cw_flat.reshape(CONV_K, cd), chip * cshard, cshard, axis=1)

    upd = {}
    g_in_t = jnp.transpose(g_in)
    upd["w_in"] = tuple(jnp.transpose(u) for u in _adamw(
        jnp.transpose(w_in[0]), g_in_t, jnp.transpose(m_w_in[0]), jnp.transpose(v_w_in[0]), "adamw_w_in"))
    g_in = jnp.transpose(g_in_t)
    for name, wv, gv, mv, vv in [("w_attn", w_attn_branch[0], g_attn, m_w_attn_branch[0], v_w_attn_branch[0]),
                                 ("w_ssm", w_ssm_branch[0], g_ssm, m_w_ssm_branch[0], v_w_ssm_branch[0]),
                                 ("w_out", w_out[0], g_out, m_w_out[0], v_w_out[0])]:
        upd[name] = _adamw(wv, gv, mv, vv, "adamw_" + name)
    names = ["norm_w", "conv_w", "conv_b", "dt_bias", "a_log", "d_skip", "ssm_norm_w", "final_norm_w"]
    ws = [norm_w, conv_w[0].reshape(1, -1), conv_b, dt_bias, a_log, d_skip, ssm_norm_w, final_norm_w.reshape(1, d)]
    gs = [g_nw, g_cw.reshape(1, -1), g_cb, g_dtb, g_alog, g_dsk, g_snw, g_fnw]
    ms = [m_norm_w, m_conv_w[0].reshape(1, -1), m_conv_b, m_dt_bias, m_a_log, m_d_skip, m_ssm_norm_w,
          m_final_norm_w.reshape(1, d)]
    vs = [v_norm_w, v_conv_w[0].reshape(1, -1), v_conv_b, v_dt_bias, v_a_log, v_d_skip, v_ssm_norm_w,
          v_final_norm_w.reshape(1, d)]
    ssz = [a.shape[1] for a in ws]
    stot = sum(ssz)
    srows = -(-stot // (8 * LANES)) * 8

    def pack(parts):
        return jnp.pad(jnp.concatenate(parts, axis=1), ((0, 0), (0, srows * LANES - stot))).reshape(srows, LANES)

    packed = _adamw(pack(ws), pack(gs), pack(ms), pack(vs), "adamw_small")
    soffs = [sum(ssz[:i]) for i in range(len(ssz))]
    for k, nm in enumerate(names):
        upd[nm] = tuple(p.reshape(1, srows * LANES)[:, soffs[k]:soffs[k] + ssz[k]] for p in packed)

    shapes = dict(norm_w=norm_w.shape, w_in=w_in.shape, conv_w=conv_w.shape, conv_b=conv_b.shape, dt_bias=dt_bias.shape,
                  a_log=a_log.shape, d_skip=d_skip.shape, ssm_norm_w=ssm_norm_w.shape, w_attn=w_attn_branch.shape,
                  w_ssm=w_ssm_branch.shape, w_out=w_out.shape, final_norm_w=final_norm_w.shape)
    order = ["norm_w", "w_in", "conv_w", "conv_b", "dt_bias", "a_log", "d_skip", "ssm_norm_w", "w_attn", "w_ssm",
             "w_out", "final_norm_w"]
    gradv = dict(norm_w=g_nw, w_in=g_in, conv_w=g_cw, conv_b=g_cb, dt_bias=g_dtb, a_log=g_alog, d_skip=g_dsk,
                 ssm_norm_w=g_snw, w_attn=g_attn, w_ssm=g_ssm, w_out=g_out, final_norm_w=g_fnw)
    outs = [loss, grad_x[None]]
    outs += [gradv[n].reshape(shapes[n]) for n in order]
    for k in range(3):
        outs += [upd[n][k].reshape(shapes[n]) for n in order]
    return tuple(outs)
```

```python
import jax
import jax.numpy as jnp
from jax import lax
from jax.experimental import pallas as pl
from jax.experimental.pallas import tpu as pltpu

F32 = jnp.float32
BF16 = jnp.bfloat16
SDS = jax.ShapeDtypeStruct

RMS_EPS = 1e-6
LANES = 128
CHUNK = 128
SSM_HEAD_DIM = 64
SSM_GROUPS = 8
SSM_STATE = 128
CONV_K = 4
ATTN_HEAD_DIM = 128
DILATED_PATTERNS = ((128, 1), (512, 4), (2048, 16))
NEG = -1e30
VMEM_LIMIT = 56 * 1024 * 1024
ADAM_LR, ADAM_B1, ADAM_B2, ADAM_EPS, ADAM_WD, ADAM_STEP = 0.001, 0.9, 0.999, 1e-08, 0.01, 10
MESH = pl.DeviceIdType.MESH
N_CHIPS = 4
N_DEV = 8


class _Cfg:
    def __init__(self, s, d):
        self.S, self.D = s, d
        self.H = d // ATTN_HEAD_DIM
        self.SI = 2 * d
        self.NH = self.SI // SSM_HEAD_DIM
        self.HPG = self.NH // SSM_GROUPS
        self.GW = self.HPG * SSM_HEAD_DIM
        self.BC = SSM_GROUPS * SSM_STATE
        self.CD = self.SI + 2 * self.BC
        self.OQ, self.OK, self.OV, self.OZA = 0, d, 2 * d, 3 * d
        self.OZS = 4 * d
        self.OXBC = self.OZS + self.SI
        self.OGA = self.OXBC + self.CD
        self.OGS = self.OGA + d
        self.NM = self.OGS + d
        self.N_IN = self.NM + self.NH
        assert self.GW % LANES == 0 and self.NH <= LANES and s % 512 == 0 and d % 512 == 0


def _params(sem=None):
    return pltpu.CompilerParams(dimension_semantics=sem, vmem_limit_bytes=VMEM_LIMIT)


def _sigmoid(x):
    return 0.5 * jnp.tanh(0.5 * x) + 0.5


def _softplus(x):
    u = jnp.exp(-jnp.abs(x))
    l1p = jnp.where(u < 1e-3, u * (1.0 - u * (0.5 - u * (1.0 / 3.0))), jnp.log(1.0 + u))
    return jnp.maximum(x, 0.0) + l1p


def _nt(a, b):
    return lax.dot_general(a, b, (((1,), (1,)), ((), ())), preferred_element_type=F32)


def _tn(a, b):
    return lax.dot_general(a, b, (((0,), (0,)), ((), ())), preferred_element_type=F32)


def _nn(a, b):
    return jnp.dot(a, b, preferred_element_type=F32)


def _tile(n, target):
    if n <= target:
        return n
    best = None
    for t in range(LANES, target + 1, LANES):
        if n % t == 0:
            best = t
    assert best is not None, (n, target)
    return best


MM_TK = {"nn": 2048, "nt": 2048, "tn": 1024}


def _mm(a, b, dims, out_dtype, name, tm=1024, tn=2048, tk=None, init=None, carry=None):
    tk = MM_TK[dims] if tk is None else tk
    if dims == "nn":
        (m, k), (k2, n) = a.shape, b.shape
    elif dims == "nt":
        (m, k), (n, k2) = a.shape, b.shape
    else:
        (k, m), (k2, n) = a.shape, b.shape
    assert k == k2
    tm, tn, tk = _tile(m, tm), _tile(n, tn), _tile(k, tk)
    nk = k // tk
    if dims == "tn":
        a_spec = pl.BlockSpec((tk, tm), lambda i, j, kk: (kk, i))
    else:
        a_spec = pl.BlockSpec((tm, tk), lambda i, j, kk: (i, kk))
    if dims == "nt":
        b_spec = pl.BlockSpec((tn, tk), lambda i, j, kk: (j, kk))
    else:
        b_spec = pl.BlockSpec((tk, tn), lambda i, j, kk: (kk, j))
    o_spec = pl.BlockSpec((tm, tn), lambda i, j, kk: (i, j))
    op = {"nn": _nn, "nt": _nt, "tn": _tn}[dims]
    has_init = init is not None
    nx = len(carry.arrays) if carry is not None else 0
    ni, nj = m // tm, n // tn

    def body(*refs):
        a_ref, b_ref = refs[0], refs[1]
        i_ref = refs[2] if has_init else None
        x_in = refs[2 + has_init:2 + has_init + nx]
        o_ref = refs[2 + has_init + nx]
        x_out = refs[3 + has_init + nx:3 + has_init + 2 * nx]
        acc = refs[3 + has_init + 2 * nx]
        x_sems = refs[4 + has_init + 2 * nx:]
        i, j, kk = pl.program_id(0), pl.program_id(1), pl.program_id(2)

        if nx:
            @pl.when((i == 0) & (j == 0) & (kk == 0))
            def _():
                carry.start(x_in, x_out, x_sems)

        prod = lambda: op(a_ref[...], b_ref[...])
        with_init = (lambda p: p + i_ref[...].astype(F32)) if has_init else (lambda p: p)
        if nk == 1:
            o_ref[...] = with_init(prod()).astype(out_dtype)
        else:
            @pl.when(kk == 0)
            def _():
                acc[...] = with_init(prod())

            @pl.when((kk > 0) & (kk < nk - 1))
            def _():
                acc[...] += prod()

            @pl.when(kk == nk - 1)
            def _():
                o_ref[...] = (acc[...] + prod()).astype(out_dtype)

        if nx:
            @pl.when((i == ni - 1) & (j == nj - 1) & (kk == nk - 1))
            def _():
                carry.finish(x_in, x_out, x_sems)

    in_specs = [a_spec, b_spec] + ([o_spec] if has_init else []) + [HBM_SPEC] * nx
    args = (a, b) + ((init,) if has_init else ()) + (tuple(carry.arrays) if nx else ())
    sems = carry.sem_shapes() if nx else []
    outs = pl.pallas_call(
        body, out_shape=[SDS((m, n), out_dtype)] + (carry.out_shapes if nx else []), grid=(ni, nj, nk),
        in_specs=in_specs, out_specs=[o_spec] + [HBM_SPEC] * nx,
        scratch_shapes=[pltpu.VMEM((tm, tn) if nk > 1 else (8, LANES), F32)] + sems,
        compiler_params=_params(("arbitrary",) * 3 if nx else ("parallel", "parallel", "arbitrary")), name=name)(*args)
    return (outs[0], outs[1:]) if nx else outs[0]


def _rmsnorm_fwd(x, w, carry=None):
    s, d = x.shape
    tr = 256
    nsteps = s // tr
    nx = len(carry.arrays) if carry is not None else 0

    def body(*refs):
        x_ref, w_ref, x_in = refs[0], refs[1], refs[2:2 + nx]
        o_ref, x_out, x_sems = refs[2 + nx], refs[3 + nx:3 + 2 * nx], refs[3 + 2 * nx:]
        if nx:
            @pl.when(pl.program_id(0) == 0)
            def _():
                carry.start(x_in, x_out, x_sems)

        xv = x_ref[...]
        r = lax.rsqrt(jnp.mean(xv * xv, axis=-1, keepdims=True) + RMS_EPS)
        o_ref[...] = (xv * r * w_ref[...]).astype(BF16)

        if nx:
            @pl.when(pl.program_id(0) == nsteps - 1)
            def _():
                carry.finish(x_in, x_out, x_sems)

    outs = pl.pallas_call(
        body, out_shape=[SDS((s, d), BF16)] + (carry.out_shapes if nx else []), grid=(nsteps,),
        in_specs=[pl.BlockSpec((tr, d), lambda i: (i, 0)), pl.BlockSpec((1, d), lambda i: (0, 0))] + [HBM_SPEC] * nx,
        out_specs=[pl.BlockSpec((tr, d), lambda i: (i, 0))] + [HBM_SPEC] * nx,
        scratch_shapes=carry.sem_shapes() if nx else [],
        compiler_params=_params(("arbitrary",) if nx else ("parallel",)), name="rmsnorm_fwd")(
            x, w, *(carry.arrays if nx else []))
    return (outs[0], outs[1:]) if nx else outs[0]


def _rmsnorm_bwd(x, w, dhn_a, dhn_b, dout):
    s, d = x.shape
    tr = 256

    def body(x_ref, w_ref, dh_ref, dh2_ref, do_ref, gx_ref, gw_ref):
        xv = x_ref[...]
        r = lax.rsqrt(jnp.mean(xv * xv, axis=-1, keepdims=True) + RMS_EPS)
        nrm = xv * r
        dh = dh_ref[...] + dh2_ref[...]
        gy = dh * w_ref[...]
        gx_ref[...] = do_ref[...] + r * (gy - nrm * jnp.mean(gy * nrm, axis=-1, keepdims=True))

        @pl.when(pl.program_id(0) == 0)
        def _():
            gw_ref[...] = jnp.zeros_like(gw_ref)

        gw_ref[...] += jnp.sum(dh * nrm, axis=0, keepdims=True)

    blk = pl.BlockSpec((tr, d), lambda i: (i, 0))
    row = pl.BlockSpec((1, d), lambda i: (0, 0))
    return pl.pallas_call(
        body, out_shape=(SDS((s, d), F32), SDS((1, d), F32)), grid=(s // tr,),
        in_specs=[blk, row, blk, blk, blk], out_specs=(blk, row),
        compiler_params=_params(("arbitrary",)), name="rmsnorm_bwd")(x, w, dhn_a, dhn_b, dout)


DEINT = DILATED_PATTERNS[-1][1]
DEINT_ROWS = DEINT * LANES


class _Pass:
    def __init__(self, tq, patterns, unit, seg_len):
        self.tq, self.patterns, self.unit, self.seg_len = tq, patterns, unit, seg_len
        self.win = max(w for w, _ in patterns) // unit
        self.w = self.win + tq
        assert self.win % tq == 0


def _attn_tables(ps):
    i = jnp.arange(ps.tq, dtype=jnp.int32)[:, None]
    j = jnp.arange(ps.w, dtype=jnp.int32)[None, :]
    delta = (i + ps.win - j) * ps.unit
    n = jnp.zeros((ps.tq, ps.w), F32)
    for window, dil in ps.patterns:
        n = n + ((delta >= 0) & (delta <= window) & (delta % dil == 0)).astype(F32)
    logn = jnp.where(n > 0, jnp.log(jnp.maximum(n, 1.0)), NEG)
    return logn, jnp.maximum(delta, 0).astype(F32)


def _slopes(h):
    s = jnp.asarray([2.0 ** (-8.0 * (i + 1) / h) for i in range(h)], F32)
    return jnp.broadcast_to(s[:, None, None], (h, 1, LANES))


def _masked_logn(ps, logn_ref, start):
    col = lax.broadcasted_iota(jnp.int32, (ps.tq, ps.w), 1)
    return jnp.where(col >= ps.win - lax.rem(start, ps.seg_len), logn_ref[...], NEG)


def _head_cols(hh):
    return slice(hh * ATTN_HEAD_DIM, (hh + 1) * ATTN_HEAD_DIM)


def _head_window(refs, cs):
    return jnp.concatenate([r[:, cs] for r in refs], axis=0)


def _head_scores(q_ref, kw, cs, base, dist_ref, slope_ref, hh):
    return _nt(q_ref[:, cs], kw) * (ATTN_HEAD_DIM ** -0.5) + (base - slope_ref[hh][0:1, 0:1] * dist_ref[...])


def _lane_of(stat, hh):
    lane = lax.broadcasted_iota(jnp.int32, stat.shape, 1)
    return jnp.sum(jnp.where(lane == hh, stat, 0.0), axis=1, keepdims=True)


def _window_specs(ps, d, col, nb):
    nprev = ps.win // ps.tq
    return [pl.BlockSpec((ps.tq, d), lambda i, b=b: (jnp.maximum(jnp.minimum(i, nb - 1) - (nprev - b), 0), col))
            for b in range(nprev + 1)]


def _attn_fwd(cfg, ps, qkv, cols, tables, slopes, name):
    s, h, d = cfg.S, cfg.H, cfg.D
    tq, nw = ps.tq, ps.win // ps.tq + 1
    nb = s // tq
    logn, dist = tables
    qc, kc, vc = [c // d for c in cols]

    def body(*refs):
        q_ref, k_refs, v_refs = refs[0], refs[1:1 + nw], refs[1 + nw:1 + 2 * nw]
        logn_ref, dist_ref, slope_ref, o_ref, lse_ref = refs[1 + 2 * nw:]
        base = _masked_logn(ps, logn_ref, pl.program_id(0) * tq)
        lane = lax.broadcasted_iota(jnp.int32, (tq, LANES), 1)

        lse = jnp.zeros((tq, LANES), F32)
        for hh in range(h):
            cs = _head_cols(hh)
            sc = _head_scores(q_ref, _head_window(k_refs, cs), cs, base, dist_ref, slope_ref, hh)
            m = jnp.max(sc, axis=1, keepdims=True)
            p = jnp.exp(sc - m)
            l = jnp.sum(p, axis=1, keepdims=True)
            o_ref[:, cs] = (_nn(p.astype(BF16), _head_window(v_refs, cs)) / l).astype(BF16)
            lse = jnp.where(lane == hh, m + jnp.log(l), lse)
        lse_ref[...] = lse

    tab = pl.BlockSpec((tq, ps.w), lambda i: (0, 0))
    return pl.pallas_call(
        body, out_shape=(SDS((s, d), BF16), SDS((s, LANES), F32)), grid=(nb,),
        in_specs=[pl.BlockSpec((tq, d), lambda i: (i, qc))] + _window_specs(ps, d, kc, nb) + _window_specs(ps, d, vc, nb)
        + [tab, tab, pl.BlockSpec((h, 1, LANES), lambda i: (0, 0, 0))],
        out_specs=(pl.BlockSpec((tq, d), lambda i: (i, 0)), pl.BlockSpec((tq, LANES), lambda i: (i, 0))),
        compiler_params=_params(("parallel",)), name=name)(*([qkv] * (1 + 2 * nw)), logn, dist, slopes)


def _attn_bwd(cfg, ps, qkv, cols, do, lse, delta, tables, slopes, name):
    s, h, d = cfg.S, cfg.H, cfg.D
    tq, nprev = ps.tq, ps.win // ps.tq
    nw = nprev + 1
    nb = s // tq
    logn, dist = tables
    qc, kc, vc = [c // d for c in cols]
    scale = ATTN_HEAD_DIM ** -0.5

    def body(*refs):
        q_ref, k_refs, v_refs = refs[0], refs[1:1 + nw], refs[1 + nw:1 + 2 * nw]
        do_ref, lse_ref, dl_ref, logn_ref, dist_ref, slope_ref, dq_ref, dk_ref, dv_ref, ck, cv = refs[1 + 2 * nw:]
        i = pl.program_id(0)
        slot = lambda b: lax.rem(i + b, nprev)

        @pl.when(i == 0)
        def _():
            ck[...] = jnp.zeros_like(ck)
            cv[...] = jnp.zeros_like(cv)

        @pl.when(i < nb)
        def _():
            base = _masked_logn(ps, logn_ref, i * tq)
            lse_all, dl_all = lse_ref[...], dl_ref[...]

            for hh in range(h):
                cs = _head_cols(hh)
                kw, vw = _head_window(k_refs, cs), _head_window(v_refs, cs)
                sc = _head_scores(q_ref, kw, cs, base, dist_ref, slope_ref, hh)
                p = jnp.exp(sc - lse_all[:, hh:hh + 1])
                dob = do_ref[:, cs]
                ds = (p * (_nt(dob, vw) - dl_all[:, hh:hh + 1]) * scale).astype(BF16)
                dq_ref[:, cs] = _nn(ds, kw).astype(BF16)
                dkw = _tn(ds, q_ref[:, cs])
                dvw = _tn(p.astype(BF16), dob)
                dk_ref[:, cs] = ck[slot(0), :, cs] + dkw[0:tq]
                dv_ref[:, cs] = cv[slot(0), :, cs] + dvw[0:tq]
                for b in range(1, nprev):
                    ck[slot(b), :, cs] += dkw[b * tq:(b + 1) * tq]
                    cv[slot(b), :, cs] += dvw[b * tq:(b + 1) * tq]
                ck[slot(0), :, cs] = dkw[nprev * tq:]
                cv[slot(0), :, cs] = dvw[nprev * tq:]

        @pl.when(i >= nb)
        def _():
            dk_ref[...] = ck[slot(0)]
            dv_ref[...] = cv[slot(0)]

    here = lambda i: jnp.minimum(i, nb - 1)
    blk = pl.BlockSpec((tq, d), lambda i: (here(i), 0))
    stat = pl.BlockSpec((tq, LANES), lambda i: (here(i), 0))
    late = pl.BlockSpec((tq, d), lambda i: (jnp.maximum(i - nprev, 0), 0))
    tab = pl.BlockSpec((tq, ps.w), lambda i: (0, 0))
    return pl.pallas_call(
        body, out_shape=(SDS((s, d), BF16), SDS((s, d), F32), SDS((s, d), F32)), grid=(nb + nprev,),
        in_specs=[pl.BlockSpec((tq, d), lambda i: (here(i), qc))] + _window_specs(ps, d, kc, nb)
        + _window_specs(ps, d, vc, nb) + [blk, stat, stat, tab, tab, pl.BlockSpec((h, 1, LANES), lambda i: (0, 0, 0))],
        out_specs=(blk, late, late),
        scratch_shapes=[pltpu.VMEM((nprev, tq, d), F32), pltpu.VMEM((nprev, tq, d), F32)],
        compiler_params=_params(("arbitrary",)), name=name)(
            *([qkv] * (1 + 2 * nw)), do, lse, delta, logn, dist, slopes)


def _by_residue(a):
    return a.reshape(DEINT, a.shape[0] // DEINT, a.shape[1])


def _deint_spec(colblock):
    return pl.BlockSpec((DEINT, LANES, LANES), lambda b, j: (0, b, colblock(j)))


def _deint_rows(scr, out_ref, dtype):
    for r in range(DEINT):
        out_ref[r] = scr[pl.ds(r, LANES, stride=DEINT), :].astype(dtype)


def _int_rows(in_ref, scr):
    for r in range(DEINT):
        scr[pl.ds(r, LANES, stride=DEINT), :] = in_ref[r].astype(F32)


WIDE = 4 * LANES


def _wide_spec():
    return pl.BlockSpec((DEINT, LANES, WIDE), lambda b, j: (0, b, j))


def _deinterleave(x, col0, ncols, name):
    s = x.shape[0]
    c0 = col0 // WIDE

    def body(x_ref, o_ref, scr):
        for t in range(WIDE // LANES):
            cs = slice(t * LANES, (t + 1) * LANES)
            scr[t] = x_ref[:, cs].astype(F32)
            for r in range(DEINT):
                o_ref[r, :, cs] = scr.at[t][pl.ds(r, LANES, stride=DEINT), :].astype(x.dtype)

    out = pl.pallas_call(
        body, out_shape=SDS((DEINT, s // DEINT, ncols), x.dtype), grid=(s // DEINT_ROWS, ncols // WIDE),
        in_specs=[pl.BlockSpec((DEINT_ROWS, WIDE), lambda b, j: (b, c0 + j))],
        out_specs=_wide_spec(),
        scratch_shapes=[pltpu.VMEM((WIDE // LANES, DEINT_ROWS, LANES), F32)],
        compiler_params=_params(("parallel", "parallel")), name=name)(x)
    return out.reshape(s, ncols)


def _attn_merge(cfg, proj, o_1, lse_1, o_2, lse_2):
    s, h = cfg.S, cfg.H
    zb = cfg.OZA // WIDE
    rows = DEINT_ROWS
    hps = WIDE // LANES

    def body(o1_ref, l1_ref, o2_ref, l2_ref, z_ref, o_ref, og_ref, lse_ref, so, sl):
        j = pl.program_id(1)

        @pl.when(j == 0)
        def _():
            _int_rows(l2_ref, sl)
            lse_ref[...] = jnp.zeros_like(lse_ref)

        l1_all, l2_all = l1_ref[...], sl[...]
        lane = lax.broadcasted_iota(jnp.int32, (rows, LANES), 1)
        lse = lse_ref[...]
        for t in range(hps):
            hh = j * hps + t
            cs = slice(t * LANES, (t + 1) * LANES)
            for r in range(DEINT):
                so.at[t][pl.ds(r, LANES, stride=DEINT), :] = o2_ref[r, :, cs].astype(F32)
            l1, l2 = _lane_of(l1_all, hh), _lane_of(l2_all, hh)
            mx = jnp.maximum(l1, l2)
            w1, w2 = jnp.exp(l1 - mx), jnp.exp(l2 - mx)
            den = w1 + w2
            o = (w1 * o1_ref[:, cs].astype(F32) + w2 * so[t]) / den
            z = z_ref[:, cs].astype(F32)
            o_ref[:, cs] = o.astype(BF16)
            og_ref[:, cs] = (o * (z * _sigmoid(z))).astype(BF16)
            lse = jnp.where(lane == hh, mx + jnp.log(den), lse)
        lse_ref[...] = lse

    blk = pl.BlockSpec((rows, WIDE), lambda b, j: (b, j))
    stat = pl.BlockSpec((rows, LANES), lambda b, j: (b, 0))
    return pl.pallas_call(
        body, out_shape=(SDS((s, cfg.D), BF16), SDS((s, cfg.D), BF16), SDS((s, LANES), F32)),
        grid=(s // rows, h // hps),
        in_specs=[blk, stat, _wide_spec(), _deint_spec(lambda j: 0), pl.BlockSpec((rows, WIDE), lambda b, j: (b, zb + j))],
        out_specs=(blk, blk, stat),
        scratch_shapes=[pltpu.VMEM((hps, rows, LANES), F32), pltpu.VMEM((rows, LANES), F32)],
        compiler_params=_params(("parallel", "arbitrary")), name="attn_merge")(
            o_1, lse_1, _by_residue(o_2), _by_residue(lse_2), proj)


def _attn_bwd_prep(cfg, proj, o_a, doag, lse, dproj):
    s, h = cfg.S, cfg.H
    zb = cfg.OZA // WIDE
    rows = DEINT_ROWS
    hps = WIDE // LANES

    def body(o_ref, dg_ref, z_ref, lse_ref, dp_in, dz_ref, do_ref, do2_ref, dl_ref, dl2_ref, lse2_ref, scr):
        del dp_in
        j = pl.program_id(1)

        @pl.when(j == 0)
        def _():
            dl_ref[...] = jnp.zeros_like(dl_ref)

        lane = lax.broadcasted_iota(jnp.int32, (rows, LANES), 1)
        dl = dl_ref[...]
        for t in range(hps):
            cs = slice(t * LANES, (t + 1) * LANES)
            z = z_ref[:, cs].astype(F32)
            sg = _sigmoid(z)
            o = o_ref[:, cs].astype(F32)
            dg = dg_ref[:, cs].astype(F32)
            do = dg * (z * sg)
            dz_ref[:, cs] = (dg * o * (sg * (1.0 + z * (1.0 - sg)))).astype(BF16)
            do_ref[:, cs] = do.astype(BF16)
            scr[...] = do
            for r in range(DEINT):
                do2_ref[r, :, cs] = scr[pl.ds(r, LANES, stride=DEINT), :].astype(BF16)
            dl = jnp.where(lane == j * hps + t, jnp.sum(do * o, axis=1, keepdims=True), dl)
        dl_ref[...] = dl

        @pl.when(j == h // hps - 1)
        def _():
            scr[...] = dl
            _deint_rows(scr, dl2_ref, F32)
            scr[...] = lse_ref[...]
            _deint_rows(scr, lse2_ref, F32)

    blk = pl.BlockSpec((rows, WIDE), lambda b, j: (b, j))
    stat = pl.BlockSpec((rows, LANES), lambda b, j: (b, 0))
    stat2 = _deint_spec(lambda j: 0)
    outs = pl.pallas_call(
        body,
        out_shape=(SDS(dproj.shape, BF16), SDS((s, cfg.D), BF16), SDS((DEINT, s // DEINT, cfg.D), BF16),
                   SDS((s, LANES), F32), SDS((DEINT, s // DEINT, LANES), F32), SDS((DEINT, s // DEINT, LANES), F32)),
        grid=(s // rows, h // hps),
        in_specs=[blk, blk, pl.BlockSpec((rows, WIDE), lambda b, j: (b, zb + j)), stat, HBM_SPEC],
        out_specs=(pl.BlockSpec((rows, WIDE), lambda b, j: (b, zb + j)), blk, _wide_spec(), stat, stat2, stat2),
        scratch_shapes=[pltpu.VMEM((rows, LANES), F32)],
        input_output_aliases={4: 0},
        compiler_params=_params(("parallel", "arbitrary")), name="attn_bwd_prep")(o_a, doag, proj, lse, dproj)
    dproj, do, do2, dl, dl2, lse2 = outs
    return dproj, do, do2.reshape(s, cfg.D), dl, dl2.reshape(s, LANES), lse2.reshape(s, LANES)


def _attn_grad_sum(cfg, g_1, g_2, col0, dproj, name):
    s = cfg.S
    c0 = col0 // WIDE
    rows = DEINT_ROWS

    def body(g1_ref, g2_ref, dp_in, o_ref, scr):
        del dp_in
        for t in range(WIDE // LANES):
            cs = slice(t * LANES, (t + 1) * LANES)
            for r in range(DEINT):
                scr.at[t][pl.ds(r, LANES, stride=DEINT), :] = g2_ref[r, :, cs].astype(F32)
            o_ref[:, cs] = (g1_ref[:, cs].astype(F32) + scr[t]).astype(BF16)

    return pl.pallas_call(
        body, out_shape=SDS(dproj.shape, BF16), grid=(s // rows, cfg.D // WIDE),
        in_specs=[pl.BlockSpec((rows, WIDE), lambda b, j: (b, j)), _wide_spec(), HBM_SPEC],
        out_specs=pl.BlockSpec((rows, WIDE), lambda b, j: (b, c0 + j)),
        scratch_shapes=[pltpu.VMEM((WIDE // LANES, rows, LANES), F32)],
        input_output_aliases={2: 0},
        compiler_params=_params(("parallel", "parallel")), name=name)(g_1, _by_residue(g_2), dproj)


CONV_HALO = 16
CONV_TR = 512
CONV_CW = 1024


def _rows_back(a, n):
    return a if n == 0 else pltpu.roll(a, n % a.shape[0], axis=0)


def _conv_fwd(cfg, proj, conv_w, conv_b):
    s, cd = cfg.S, cfg.CD
    tr, cw, hl = CONV_TR, CONV_CW, CONV_HALO
    cb0 = cfg.OXBC // cw

    def body(x_ref, h_ref, w_ref, b_ref, o_ref):
        i = pl.program_id(0)
        halo = jnp.where(i > 0, h_ref[...].astype(F32), 0.0)
        ext = jnp.concatenate([halo, x_ref[...].astype(F32)], axis=0)
        pre = b_ref[...] + jnp.zeros((tr, cw), F32)
        for k in range(CONV_K):
            pre = pre + w_ref[k:k + 1, :] * _rows_back(ext, CONV_K - 1 - k)[hl:]
        o_ref[...] = (pre * _sigmoid(pre)).astype(BF16)

    return pl.pallas_call(
        body, out_shape=SDS((s, cd), BF16), grid=(s // tr, cd // cw),
        in_specs=[pl.BlockSpec((tr, cw), lambda i, j: (i, cb0 + j)),
                  pl.BlockSpec((hl, cw), lambda i, j: (jnp.maximum(i * (tr // hl) - 1, 0), cb0 + j)),
                  pl.BlockSpec((CONV_K, cw), lambda i, j: (0, j)),
                  pl.BlockSpec((1, cw), lambda i, j: (0, j))],
        out_specs=pl.BlockSpec((tr, cw), lambda i, j: (i, j)),
        compiler_params=_params(("parallel", "parallel")), name="conv_fwd")(proj, proj, conv_w, conv_b)


def _conv_bwd(cfg, proj, dact, conv_w, conv_b, dproj):
    s, cd = cfg.S, cfg.CD
    tr, cw, hl = CONV_TR, CONV_CW, CONV_HALO
    cb0 = cfg.OXBC // cw
    nr = s // tr
    last_h = s // hl - 1

    def body(x_ref, hp_ref, hn_ref, d_ref, dn_ref, w_ref, b_ref, dp_in, dx_ref, gw_ref, gb_ref):
        del dp_in
        i = pl.program_id(1)
        ext = jnp.concatenate([jnp.where(i > 0, hp_ref[...].astype(F32), 0.0), x_ref[...].astype(F32),
                               hn_ref[...].astype(F32)], axis=0)
        shifted = [_rows_back(ext, CONV_K - 1 - k)[hl:] for k in range(CONV_K)]
        pre = b_ref[...] + jnp.zeros((tr + hl, cw), F32)
        for k in range(CONV_K):
            pre = pre + w_ref[k:k + 1, :] * shifted[k]
        sg = _sigmoid(pre)
        dact = jnp.concatenate([d_ref[...].astype(F32), jnp.where(i < nr - 1, dn_ref[...].astype(F32), 0.0)], axis=0)
        dpre = dact * (sg * (1.0 + pre * (1.0 - sg)))
        dx = jnp.zeros((tr, cw), F32)
        for k in range(CONV_K):
            dx = dx + w_ref[k:k + 1, :] * _rows_back(dpre, -(CONV_K - 1 - k))[0:tr]
        dx_ref[...] = dx.astype(BF16)

        @pl.when(i == 0)
        def _():
            gw_ref[...] = jnp.zeros_like(gw_ref)
            gb_ref[...] = jnp.zeros_like(gb_ref)

        dcur = dpre[0:tr]
        gb_ref[...] += jnp.sum(dcur, axis=0, keepdims=True)
        for k in range(CONV_K):
            gw_ref[k:k + 1, :] += jnp.sum(dcur * shifted[k][0:tr], axis=0, keepdims=True)

    return pl.pallas_call(
        body, out_shape=(SDS(dproj.shape, BF16), SDS((CONV_K, cd), F32), SDS((1, cd), F32)), grid=(cd // cw, nr),
        in_specs=[pl.BlockSpec((tr, cw), lambda j, i: (i, cb0 + j)),
                  pl.BlockSpec((hl, cw), lambda j, i: (jnp.maximum(i * (tr // hl) - 1, 0), cb0 + j)),
                  pl.BlockSpec((hl, cw), lambda j, i: (jnp.minimum((i + 1) * (tr // hl), last_h), cb0 + j)),
                  pl.BlockSpec((tr, cw), lambda j, i: (i, j)),
                  pl.BlockSpec((hl, cw), lambda j, i: (jnp.minimum((i + 1) * (tr // hl), last_h), j)),
                  pl.BlockSpec((CONV_K, cw), lambda j, i: (0, j)),
                  pl.BlockSpec((1, cw), lambda j, i: (0, j)),
                  pl.BlockSpec(memory_space=pl.ANY)],
        out_specs=(pl.BlockSpec((tr, cw), lambda j, i: (i, cb0 + j)),
                   pl.BlockSpec((CONV_K, cw), lambda j, i: (0, j)),
                   pl.BlockSpec((1, cw), lambda j, i: (0, j))),
        input_output_aliases={7: 0},
        compiler_params=_params(("parallel", "arbitrary")), name="conv_bwd")(
            proj, proj, proj, dact, dact, conv_w, conv_b, dproj)


def _expand(v, e, terms):
    out, rem = None, v
    for _ in range(terms):
        hi = rem.astype(BF16)
        t = _nn(hi, e)
        out = t if out is None else out + t
        rem = rem - hi.astype(F32)
    return out


def _segsum(v, e, terms):
    out, rem = None, v
    for _ in range(terms):
        hi = rem.astype(BF16)
        t = _nt(hi, e)
        out = t if out is None else out + t
        rem = rem - hi.astype(F32)
    return out


def _expand_row(row, e, terms):
    return _expand(jnp.broadcast_to(row, (8, LANES)), e, terms)[0:1]


def _segsum_row(row, e, terms):
    return _segsum(jnp.broadcast_to(row, (8, row.shape[1])), e, terms)[0:1]


def _expansion_matrix(cfg):
    hh = jnp.arange(LANES, dtype=jnp.int32)[:, None]
    cc = jnp.arange(cfg.SI, dtype=jnp.int32)[None, :]
    return (cc // SSM_HEAD_DIM == hh).astype(BF16)


def _tri(lower):
    r = lax.broadcasted_iota(jnp.int32, (CHUNK, CHUNK), 0)
    c = lax.broadcasted_iota(jnp.int32, (CHUNK, CHUNK), 1)
    return (c <= r) if lower else (c >= r)


def _ssd_prep(dtr_ref, db_ref, al_ref, e):
    dtr = dtr_ref[...] + db_ref[...]
    dt = _softplus(dtr)
    a = -jnp.exp(al_ref[...])
    acum = jnp.dot(_tri(True).astype(F32), dt * a, precision=lax.Precision.HIGHEST, preferred_element_type=F32)
    return dtr, dt, a, _expand(dt, e, 2), _expand(acum, e, 3)


def _ssd_fwd(cfg, xact, dt_raw, proj, dt_bias, a_log, d_skip, norm_w, e):
    s, si, cd, gw, bc = cfg.S, cfg.SI, cfg.CD, cfg.GW, cfg.BC
    nc = s // CHUNK
    zb = cfg.OZS // si
    tiles = gw // LANES

    def body(xa_ref, dtr_ref, z_ref, db_ref, al_ref, dsk_ref, nw_ref, e_ref, y_ref, y2_ref, st_ref,
             state, ybuf, x_s, xw_s, ae_s, ea_s, lam_s):
        @pl.when(pl.program_id(0) == 0)
        def _():
            state[...] = jnp.zeros_like(state)

        st_ref[...] = state[...]
        ev = e_ref[...]
        _, _, _, dt_e, a_e = _ssd_prep(dtr_ref, db_ref, al_ref, ev)
        xs = xa_ref[:, 0:si].astype(F32)
        x = xs * dt_e
        lam_e = a_e[CHUNK - 1:CHUNK, :]
        x_s[...] = x.astype(BF16)
        xw_s[...] = (x * jnp.exp(lam_e - a_e)).astype(BF16)
        ae_s[...] = a_e
        ea_s[...] = jnp.exp(a_e)
        ybuf[...] = _expand_row(dsk_ref[...], ev, 3) * xs
        lam_s[...] = jnp.broadcast_to(jnp.exp(lam_e), (8, si))
        tril = _tri(True)
        lane = lax.broadcasted_iota(jnp.int32, (CHUNK, LANES), 1)

        def group(g, carry):
            co = pl.multiple_of(g * gw, LANES)
            bg = xa_ref[:, pl.ds(pl.multiple_of(si + g * SSM_STATE, LANES), SSM_STATE)]
            cg = xa_ref[:, pl.ds(pl.multiple_of(si + bc + g * SSM_STATE, LANES), SSM_STATE)]
            cbm = _nt(cg, bg)
            st = state[:, pl.ds(co, gw)]
            yoff = _nn(cg, st.astype(BF16)) * ea_s[:, pl.ds(co, gw)]
            for k in range(tiles):
                tc = pl.multiple_of(co + k * LANES, LANES)
                at = ae_s[:, pl.ds(tc, LANES)]
                att = at.T
                xt = x_s[:, pl.ds(tc, LANES)]
                acc = yoff[:, k * LANES:(k + 1) * LANES]
                for half in range(2):
                    lo = half * SSM_HEAD_DIM
                    seg = at[:, lo:lo + 1] - att[lo:lo + 1, :]
                    dec = jnp.exp(jnp.where(tril, seg, NEG))
                    xh = jnp.where((lane >= lo) & (lane < lo + SSM_HEAD_DIM), xt, jnp.zeros_like(xt))
                    acc = acc + _nn((cbm * dec).astype(BF16), xh)
                ybuf[:, pl.ds(tc, LANES)] += acc
            state[:, pl.ds(co, gw)] = st * lam_s[0:1, pl.ds(co, gw)] + _tn(bg, xw_s[:, pl.ds(co, gw)])
            return carry

        lax.fori_loop(0, SSM_GROUPS, group, 0)
        y = ybuf[...]
        y_ref[...] = y.astype(BF16)
        z = z_ref[...].astype(F32)
        u = y * (z * _sigmoid(z))
        r = lax.rsqrt(jnp.mean(u * u, axis=-1, keepdims=True) + RMS_EPS)
        y2_ref[...] = (u * r * nw_ref[...]).astype(BF16)

    row = lambda n: pl.BlockSpec((1, n), lambda c: (0, 0))
    return pl.pallas_call(
        body,
        out_shape=(SDS((s, si), BF16), SDS((s, si), BF16), SDS((nc, SSM_STATE, si), F32)),
        grid=(nc,),
        in_specs=[pl.BlockSpec((CHUNK, cd), lambda c: (c, 0)),
                  pl.BlockSpec((CHUNK, LANES), lambda c: (c, 0)),
                  pl.BlockSpec((CHUNK, si), lambda c: (c, zb)),
                  row(LANES), row(LANES), row(LANES), row(si),
                  pl.BlockSpec((LANES, si), lambda c: (0, 0))],
        out_specs=(pl.BlockSpec((CHUNK, si), lambda c: (c, 0)),
                   pl.BlockSpec((CHUNK, si), lambda c: (c, 0)),
                   pl.BlockSpec((None, SSM_STATE, si), lambda c: (c, 0, 0))),
        scratch_shapes=[pltpu.VMEM((SSM_STATE, si), F32), pltpu.VMEM((CHUNK, si), F32),
                        pltpu.VMEM((CHUNK, si), BF16), pltpu.VMEM((CHUNK, si), BF16),
                        pltpu.VMEM((CHUNK, si), F32), pltpu.VMEM((CHUNK, si), F32),
                        pltpu.VMEM((8, si), F32)],
        compiler_params=_params(("arbitrary",)), name="ssd_fwd")(
            xact, dt_raw, proj, dt_bias, a_log, d_skip, norm_w, e)


def _ssd_bwd(cfg, xact, dt_raw, proj, y, dy2, states, dt_bias, a_log, d_skip, norm_w, e, dproj):
    s, si, cd, gw, bc, hpg = cfg.S, cfg.SI, cfg.CD, cfg.GW, cfg.BC, cfg.HPG
    nc = s // CHUNK
    zb = cfg.OZS // si
    tiles = gw // LANES

    def body(xa_ref, dtr_ref, z_ref, y_ref, d2_ref, st_ref, db_ref, al_ref, dsk_ref, nw_ref, e_ref, dp_in,
             dz_ref, dxa_ref, ddt_ref, gnw_ref, gdb_ref, gal_ref, gds_ref,
             dh, dhn, xs_s, x_s, w_s, ae_s, ea_s, g_s, dx_s, dae_s, r_s, lam_s, dle_s):
        del dp_in

        @pl.when(pl.program_id(0) == 0)
        def _():
            dh[...] = jnp.zeros_like(dh)
            gnw_ref[...] = jnp.zeros_like(gnw_ref)
            gdb_ref[...] = jnp.zeros_like(gdb_ref)
            gal_ref[...] = jnp.zeros_like(gal_ref)
            gds_ref[...] = jnp.zeros_like(gds_ref)

        ev = e_ref[...]
        yv = y_ref[...].astype(F32)
        z = z_ref[...].astype(F32)
        sg = _sigmoid(z)
        sz = z * sg
        u = yv * sz
        r = lax.rsqrt(jnp.mean(u * u, axis=-1, keepdims=True) + RMS_EPS)
        nrm = u * r
        d2 = d2_ref[...].astype(F32)
        gnw_ref[...] += jnp.sum(d2 * nrm, axis=0, keepdims=True)
        gn = d2 * nw_ref[...]
        du = r * (gn - nrm * jnp.mean(gn * nrm, axis=-1, keepdims=True))
        gv = du * sz
        dz_ref[...] = (du * yv * (sg * (1.0 + z * (1.0 - sg)))).astype(BF16)
        g_s[...] = gv

        dtr, dt, a, dt_e, a_e = _ssd_prep(dtr_ref, db_ref, al_ref, ev)
        xs = xa_ref[:, 0:si].astype(F32)
        x = xs * dt_e
        lam_e = a_e[CHUNK - 1:CHUNK, :]
        xs_s[...] = xs
        x_s[...] = x
        w_s[...] = jnp.exp(lam_e - a_e)
        ae_s[...] = a_e
        ea_s[...] = jnp.exp(a_e)
        lam_s[...] = jnp.broadcast_to(jnp.exp(lam_e), (8, si))
        gds_ref[...] += _segsum_row(jnp.sum(gv * xs, axis=0, keepdims=True), ev, 2)
        r_s[...] = jnp.zeros_like(r_s)
        tril = _tri(True)
        lane = lax.broadcasted_iota(jnp.int32, (CHUNK, LANES), 1)
        sub = lax.broadcasted_iota(jnp.int32, (CHUNK, LANES), 0)

        def group(g, carry):
            co = pl.multiple_of(g * gw, LANES)
            bo = pl.multiple_of(si + g * SSM_STATE, LANES)
            cof = pl.multiple_of(si + bc + g * SSM_STATE, LANES)
            cols = pl.ds(co, gw)
            bg = xa_ref[:, pl.ds(bo, SSM_STATE)]
            cg = xa_ref[:, pl.ds(cof, SSM_STATE)]
            cbm = _nt(cg, bg)
            st = st_ref[:, cols]
            stb = st.astype(BF16)
            dho = dh[:, cols]
            dhob = dho.astype(BF16)
            ea = ea_s[:, cols]
            gg = g_s[:, cols]
            xg = x_s[:, cols]
            wg = w_s[:, cols]
            explam = lam_s[0:1, cols]
            yoff = _nn(cg, stb) * ea
            ga = (gg * ea).astype(BF16)
            dc = _nt(ga, stb)
            dhn[:, cols] = dho * explam + _tn(cg, ga)
            bdh = _nn(bg, dhob)
            db = _nt((xg * wg).astype(BF16), dhob)
            t = xg * bdh * wg
            dle_s[0:1, cols] = jnp.sum(t, axis=0, keepdims=True) + explam * jnp.sum(dho * st, axis=0, keepdims=True)
            dae_base = gg * yoff - t
            dxw = wg * bdh
            dcb = jnp.zeros((CHUNK, CHUNK), F32)
            for k in range(tiles):
                tc = pl.multiple_of(co + k * LANES, LANES)
                ksl = slice(k * LANES, (k + 1) * LANES)
                at = ae_s[:, pl.ds(tc, LANES)]
                att = at.T
                xt = xg[:, ksl].astype(BF16)
                gt = gg[:, ksl].astype(BF16)
                dxt = dxw[:, ksl]
                place = jnp.zeros((CHUNK, LANES), F32)
                for half in range(2):
                    lo = half * SSM_HEAD_DIM
                    seg = at[:, lo:lo + 1] - att[lo:lo + 1, :]
                    dec = jnp.exp(jnp.where(tril, seg, NEG))
                    mh = cbm * dec
                    gh = jnp.where((lane >= lo) & (lane < lo + SSM_HEAD_DIM), gt, jnp.zeros_like(gt))
                    dm = _nt(gh, xt)
                    dxt = dxt + _tn(mh.astype(BF16), gh)
                    dcb = dcb + dm * dec
                    dseg = dm * mh
                    place = place + jnp.where(lane == lo, jnp.sum(dseg, axis=1, keepdims=True), 0.0)
                    hidx = g * hpg + 2 * k + half
                    r_s[...] += jnp.where(sub == hidx, jnp.sum(dseg, axis=0, keepdims=True), 0.0)
                dx_s[:, pl.ds(tc, LANES)] = dxt
                dae_s[:, pl.ds(tc, LANES)] = dae_base[:, ksl] + place
            dcbb = dcb.astype(BF16)
            dxa_ref[:, pl.ds(bo, SSM_STATE)] = (db + _tn(dcbb, cg)).astype(BF16)
            dxa_ref[:, pl.ds(cof, SSM_STATE)] = (dc + _nn(dcbb, bg)).astype(BF16)
            return carry

        lax.fori_loop(0, SSM_GROUPS, group, 0)
        dlam = _segsum_row(dle_s[0:1, :], ev, 2)
        da_ = _segsum(dae_s[...], ev, 2) - r_s[...].T
        da_ = da_ + jnp.where(sub == CHUNK - 1, dlam, 0.0)
        dda = jnp.dot(_tri(False).astype(F32), da_, precision=lax.Precision.HIGHEST, preferred_element_type=F32)
        dxv = dx_s[...]
        xs = xs_s[...]
        ddt = dda * a + _segsum(dxv * xs, ev, 2)
        gal_ref[...] += jnp.sum(dda * dt, axis=0, keepdims=True) * a
        ddtr = ddt * _sigmoid(dtr)
        gdb_ref[...] += jnp.sum(ddtr, axis=0, keepdims=True)
        ddt_ref[...] = ddtr
        dxa_ref[:, 0:si] = (dxv * dt_e + g_s[...] * _expand_row(dsk_ref[...], ev, 3)).astype(BF16)
        dh[...] = dhn[...]

    rev = lambda c: nc - 1 - c
    row = lambda n: pl.BlockSpec((1, n), lambda c: (0, 0))
    big = lambda: pltpu.VMEM((CHUNK, si), F32)
    return pl.pallas_call(
        body,
        out_shape=(SDS(dproj.shape, BF16), SDS((s, cd), BF16), SDS((s, LANES), F32),
                   SDS((1, si), F32), SDS((1, LANES), F32), SDS((1, LANES), F32), SDS((1, LANES), F32)),
        grid=(nc,),
        in_specs=[pl.BlockSpec((CHUNK, cd), lambda c: (rev(c), 0)),
                  pl.BlockSpec((CHUNK, LANES), lambda c: (rev(c), 0)),
                  pl.BlockSpec((CHUNK, si), lambda c: (rev(c), zb)),
                  pl.BlockSpec((CHUNK, si), lambda c: (rev(c), 0)),
                  pl.BlockSpec((CHUNK, si), lambda c: (rev(c), 0)),
                  pl.BlockSpec((None, SSM_STATE, si), lambda c: (rev(c), 0, 0)),
                  row(LANES), row(LANES), row(LANES), row(si),
                  pl.BlockSpec((LANES, si), lambda c: (0, 0)),
                  pl.BlockSpec(memory_space=pl.ANY)],
        out_specs=(pl.BlockSpec((CHUNK, si), lambda c: (rev(c), zb)),
                   pl.BlockSpec((CHUNK, cd), lambda c: (rev(c), 0)),
                   pl.BlockSpec((CHUNK, LANES), lambda c: (rev(c), 0)),
                   row(si), row(LANES), row(LANES), row(LANES)),
        scratch_shapes=[pltpu.VMEM((SSM_STATE, si), F32), pltpu.VMEM((SSM_STATE, si), F32),
                        big(), big(), big(), big(), big(), big(), big(), big(),
                        pltpu.VMEM((CHUNK, LANES), F32), pltpu.VMEM((8, si), F32), pltpu.VMEM((8, si), F32)],
        input_output_aliases={11: 0},
        compiler_params=_params(("arbitrary",)), name="ssd_bwd")(
            xact, dt_raw, proj, y, dy2, states, dt_bias, a_log, d_skip, norm_w, e, dproj)


MERGE_TR = 512
MERGE_CW = 2048


def _merge_fwd(cfg, proj, a_br, s_br):
    s, d = cfg.S, cfg.D
    tr, cw = MERGE_TR, min(MERGE_CW, d)
    ga0, gs0 = cfg.OGA // cw, cfg.OGS // cw

    def body(ga_ref, gs_ref, a_ref, s_ref, o_ref):
        o_ref[...] = (_sigmoid(ga_ref[...].astype(F32)) * a_ref[...].astype(F32)
                      + _sigmoid(gs_ref[...].astype(F32)) * s_ref[...].astype(F32)).astype(BF16)

    blk = pl.BlockSpec((tr, cw), lambda i, j: (i, j))
    return pl.pallas_call(
        body, out_shape=SDS((s, d), BF16), grid=(s // tr, d // cw),
        in_specs=[pl.BlockSpec((tr, cw), lambda i, j: (i, ga0 + j)),
                  pl.BlockSpec((tr, cw), lambda i, j: (i, gs0 + j)), blk, blk],
        out_specs=blk, compiler_params=_params(("parallel", "parallel")), name="merge_fwd")(proj, proj, a_br, s_br)


def _merge_bwd(cfg, proj, branch, dmerged, gate_off, dproj, name):
    s, d = cfg.S, cfg.D
    tr, cw = MERGE_TR, min(MERGE_CW, d)
    g0 = gate_off // cw
    fresh = dproj is None

    def body(*refs):
        g_ref, b_ref, dm_ref = refs[:3]
        dg_ref, db_ref = refs[-2:]
        dm = dm_ref[...].astype(F32)
        sg = _sigmoid(g_ref[...].astype(F32))
        db_ref[...] = (dm * sg).astype(BF16)
        dg_ref[...] = (dm * b_ref[...].astype(F32) * sg * (1.0 - sg)).astype(BF16)

    blk = pl.BlockSpec((tr, cw), lambda i, j: (i, j))
    gate = pl.BlockSpec((tr, cw), lambda i, j: (i, g0 + j))
    return pl.pallas_call(
        body, out_shape=(SDS((s, cfg.NM), BF16), SDS((s, d), BF16)), grid=(s // tr, d // cw),
        in_specs=[gate, blk, blk] + ([] if fresh else [HBM_SPEC]),
        out_specs=(gate, blk),
        input_output_aliases={} if fresh else {3: 0},
        compiler_params=_params(("parallel", "parallel")), name=name)(
            *((proj, branch, dmerged) + (() if fresh else (dproj,))))


def _outproj_loss(merged, w_out, x, target, fnw):
    s, d = x.shape
    tr = 256

    def body(m_ref, w_ref, x_ref, t_ref, fw_ref, dof_ref, dob_ref, loss_ref, g_ref):
        out = x_ref[...] + _nn(m_ref[...], w_ref[...])
        r = lax.rsqrt(jnp.mean(out * out, axis=-1, keepdims=True) + RMS_EPS)
        nrm = out * r
        fw = fw_ref[...]
        err = nrm * fw - t_ref[...]
        dy = err * (1.0 / d)
        gy = dy * fw
        dout = r * (gy - nrm * jnp.mean(gy * nrm, axis=-1, keepdims=True))
        dof_ref[...] = dout
        dob_ref[...] = dout.astype(BF16)

        @pl.when(pl.program_id(0) == 0)
        def _():
            loss_ref[...] = jnp.zeros_like(loss_ref)
            g_ref[...] = jnp.zeros_like(g_ref)

        loss_ref[...] += jnp.sum(jnp.sum(err * err, axis=1, keepdims=True), axis=0, keepdims=True) * (0.5 / d)
        g_ref[...] += jnp.sum(dy * nrm, axis=0, keepdims=True)

    blk = pl.BlockSpec((tr, d), lambda i: (i, 0))
    return pl.pallas_call(
        body, out_shape=(SDS((s, d), F32), SDS((s, d), BF16), SDS((1, LANES), F32), SDS((1, d), F32)), grid=(s // tr,),
        in_specs=[blk, pl.BlockSpec((d, d), lambda i: (0, 0)), blk, blk, pl.BlockSpec((1, d), lambda i: (0, 0))],
        out_specs=(blk, blk, pl.BlockSpec((1, LANES), lambda i: (0, 0)), pl.BlockSpec((1, d), lambda i: (0, 0))),
        compiler_params=_params(("arbitrary",)), name="outproj_loss")(merged, w_out, x, target, fnw)


ELEMWISE_BLOCK_BYTES = 1 << 20


def _row_block(rows, cols, itemsize=4):
    best = None
    for tr in range(16, rows + 1, 16):
        if rows % tr == 0 and tr * cols * itemsize <= ELEMWISE_BLOCK_BYTES:
            best = tr
    return best if best is not None else rows


def _adamw(w, g, m, v, name):
    rows, cols = w.shape
    tr = _row_block(rows, cols)
    if rows // tr > 64 and cols % LANES == 0:
        blk, grid = pl.BlockSpec((rows, LANES), lambda i: (0, i)), (cols // LANES,)
    else:
        blk, grid = pl.BlockSpec((tr, cols), lambda i: (i, 0)), (rows // tr,)
    out = SDS((rows, cols), F32)
    return pl.pallas_call(
        _adamw_body(), out_shape=(out, out, out), grid=grid, in_specs=[blk] * 4, out_specs=(blk,) * 3,
        compiler_params=_params(("parallel",)), name=name)(w, g, m, v)


def _adamw_body():
    def body(w_ref, g_ref, m_ref, v_ref, d_ref, nm_ref, nv_ref):
        gv = g_ref[...]
        nm = ADAM_B1 * m_ref[...] + (1.0 - ADAM_B1) * gv
        nv = ADAM_B2 * v_ref[...] + (1.0 - ADAM_B2) * jnp.square(gv)
        m_hat = nm / (1.0 - ADAM_B1 ** ADAM_STEP)
        v_hat = nv / (1.0 - ADAM_B2 ** ADAM_STEP)
        d_ref[...] = -ADAM_LR * (m_hat / (jnp.sqrt(v_hat) + ADAM_EPS) + ADAM_WD * w_ref[...])
        nm_ref[...] = nm
        nv_ref[...] = nv

    return body


HBM_SPEC = pl.BlockSpec(memory_space=pl.ANY)


def _position():
    return lax.axis_index("x"), lax.axis_index("y"), lax.axis_index("c")


class _Carry:
    def __init__(self, arrays, out_shapes, sems, start, finish):
        self.arrays, self.out_shapes, self.sems, self.start, self.finish = list(arrays), out_shapes, sems, start, finish

    def sem_shapes(self):
        return [pltpu.SemaphoreType.DMA((k,)) for k in self.sems]


def _gather_carry(shards, by_cols=()):
    n = len(shards)

    def copies(ins, outs, sems):
        send_sems, recv_sems, fsend_sems, frecv_sems = sems
        x, y, c = _position()
        me = 2 * x + y
        peers = [(1 - x, y), (x, 1 - y), (1 - x, 1 - y)]

        def half_of(t, chip, half):
            if t in by_cols:
                c2 = ins[t].shape[1] // 2
                return outs[t].at[chip, :, pl.ds(half * c2, c2)]
            return outs[t].at[chip, half]

        def over_ici(t, p, chip):
            px, py = peers[p]
            if t in by_cols:
                c2 = ins[t].shape[1] // 2
                src = ins[t].at[:, pl.ds(c * c2, c2)]
            else:
                r2 = ins[t].shape[0] // 2
                src = ins[t].at[pl.ds(c * r2, r2), :]
            return pltpu.make_async_remote_copy(
                src_ref=src, dst_ref=half_of(t, chip, c), send_sem=send_sems.at[3 * t + p],
                recv_sem=recv_sems.at[3 * t + p], device_id=(px, py, c), device_id_type=MESH)

        def to_sibling(t, p, half):
            px, py = peers[p]
            slab = half_of(t, 2 * px + py, half)
            return pltpu.make_async_remote_copy(
                src_ref=slab, dst_ref=slab, send_sem=fsend_sems.at[3 * t + p], recv_sem=frecv_sems.at[3 * t + p],
                device_id=(x, y, 1 - c), device_id_type=MESH)

        pairs = [(t, p) for t in range(n) for p in range(3)]
        sends = [over_ici(t, p, me) for t, p in pairs]
        lands = [over_ici(t, p, 2 * peers[p][0] + peers[p][1]) for t, p in pairs]
        passed = [to_sibling(t, p, c) for t, p in pairs]
        from_sibling = [to_sibling(t, p, 1 - c) for t, p in pairs]
        return sends, lands, passed, from_sibling

    def start(ins, outs, sems):
        for cp in copies(ins, outs, sems)[0]:
            cp.start()

    def finish(ins, outs, sems):
        sends, lands, passed, from_sibling = copies(ins, outs, sems)
        for land, fwd in zip(lands, passed):
            land.wait_recv()
            fwd.start()
        for cp in from_sibling:
            cp.wait_recv()
        for cp in sends + passed:
            cp.wait_send()

    shapes = [SDS((N_CHIPS,) + a.shape if t in by_cols else (N_CHIPS, 2, a.shape[0] // 2, a.shape[1]), a.dtype)
              for t, a in enumerate(shards)]
    return _Carry(shards, shapes, [3 * n] * 4, start, finish)


def _scatter_carry(parts):
    def start(ins, outs, sems):
        for cp in _scatter_copies(ins, outs, *sems)[0]:
            cp.start()

    def finish(ins, outs, sems):
        sends, lands = _scatter_copies(ins, outs, *sems)
        for cp in lands:
            cp.wait_recv()
        for cp in sends:
            cp.wait_send()

    return _Carry(parts, [SDS(a.shape, a.dtype) for a in parts], [3 * len(parts)] * 2, start, finish)


def _with_own(gathered, own, chip):
    full = gathered.reshape((N_CHIPS,) + own.shape)
    return lax.dynamic_update_index_in_dim(full, own, chip, 0)


def _exchange_halves(grads):
    n = len(grads)
    slabs = [list(g) if isinstance(g, (list, tuple)) else [g] for g in grads]
    flat = [a for s in slabs for a in s]
    ncp = len(flat)

    def body(*refs):
        ins, outs = refs[:ncp], refs[ncp:ncp + n]
        send_sems, recv_sems = refs[ncp + n:]
        x, y, c = _position()
        cps, k = [], 0
        for t in range(n):
            for j in range(len(slabs[t])):
                if len(slabs[t]) == 1:
                    r2 = ins[k].shape[1] // 2
                    src, dst = ins[k].at[:, pl.ds((1 - c) * r2, r2), :], outs[t]
                else:
                    r2 = ins[k].shape[0] // 2
                    src, dst = ins[k].at[pl.ds((1 - c) * r2, r2), :], outs[t].at[j]
                cps.append(pltpu.make_async_remote_copy(
                    src_ref=src, dst_ref=dst, send_sem=send_sems.at[k], recv_sem=recv_sems.at[k],
                    device_id=(x, y, 1 - c), device_id_type=MESH))
                k += 1
        for cp in cps:
            cp.start()
        for cp in cps:
            cp.wait()

    def landing(s):
        a = s[0]
        return SDS((N_CHIPS, a.shape[-2] // 2, a.shape[-1]), a.dtype)

    return pl.pallas_call(
        body, out_shape=[landing(s) for s in slabs],
        in_specs=[HBM_SPEC] * ncp, out_specs=[HBM_SPEC] * n,
        scratch_shapes=[pltpu.SemaphoreType.DMA((ncp,)), pltpu.SemaphoreType.DMA((ncp,))],
        compiler_params=pltpu.CompilerParams(has_side_effects=True), name="reduce_sibling")(*flat)


def _scatter_copies(ins, outs, send_sems, recv_sems):
    x, y, c = _position()
    me = 2 * x + y
    peers = [(1 - x, y), (x, 1 - y), (1 - x, 1 - y)]

    def remote(t, p, src_slab, dst_slab):
        px, py = peers[p]
        return pltpu.make_async_remote_copy(
            src_ref=ins[t].at[src_slab], dst_ref=outs[t].at[dst_slab], send_sem=send_sems.at[3 * t + p],
            recv_sem=recv_sems.at[3 * t + p], device_id=(px, py, c), device_id_type=MESH)

    n = len(ins)
    sends = [remote(t, p, 2 * peers[p][0] + peers[p][1], me) for t in range(n) for p in range(3)]
    lands = [remote(t, p, me, 2 * peers[p][0] + peers[p][1]) for t in range(n) for p in range(3)]
    return sends, lands


def _share_halves(halves):
    n = len(halves)

    def body(*refs):
        ins, outs = refs[:n], refs[n:2 * n]
        send_sems, recv_sems = refs[2 * n:]
        x, y, c = _position()

        def copy(t, slab):
            return pltpu.make_async_remote_copy(
                src_ref=ins[t].at[slab], dst_ref=outs[t].at[slab], send_sem=send_sems.at[t], recv_sem=recv_sems.at[t],
                device_id=(x, y, 1 - c), device_id_type=MESH)

        for t in range(n):
            copy(t, c).start()
        for t in range(n):
            copy(t, 1 - c).wait_recv()
        for t in range(n):
            copy(t, c).wait_send()

    return pl.pallas_call(
        body, out_shape=[SDS(a.shape, a.dtype) for a in halves],
        in_specs=[HBM_SPEC] * n, out_specs=[HBM_SPEC] * n,
        scratch_shapes=[pltpu.SemaphoreType.DMA((n,)), pltpu.SemaphoreType.DMA((n,))],
        input_output_aliases={t: t for t in range(n)},
        compiler_params=pltpu.CompilerParams(has_side_effects=True), name="share_sibling")(*halves)


def _add_sibling(grad, recv, core):
    nch, r2, cols = recv.shape
    tr = _row_block(r2, cols)
    nb = r2 // tr

    def body(c_ref, g_ref, r_ref, o_ref):
        del c_ref
        o_ref[...] = (g_ref[...].astype(F32) + r_ref[...].astype(F32)).astype(BF16)

    return pl.pallas_call(
        body, out_shape=SDS(recv.shape, BF16),
        grid_spec=pltpu.PrefetchScalarGridSpec(
            num_scalar_prefetch=1, grid=(nch, nb),
            in_specs=[pl.BlockSpec((None, tr, cols), lambda j, i, c_ref: (j, c_ref[0] * nb + i, 0)),
                      pl.BlockSpec((None, tr, cols), lambda j, i, c_ref: (j, i, 0))],
            out_specs=pl.BlockSpec((None, tr, cols), lambda j, i, c_ref: (j, i, 0))),
        compiler_params=_params(("parallel", "parallel")), name="add_sibling")(core, grad, recv)


def _add_chips(own, recv, chip_core):
    nch, r2, cols = recv.shape
    tr = _row_block(r2, cols)

    nsc = 2 + nch

    def body(*refs):
        me = refs[0][0]
        own_ref, p_refs, o_ref = refs[nsc], refs[nsc + 1:nsc + 1 + nch], refs[nsc + 1 + nch]
        acc = None
        for j in range(nch):
            term = jnp.where(me == j, own_ref[...], p_refs[j][...]).astype(F32)
            acc = term if acc is None else acc + term
        o_ref[...] = acc

    def slab(j):
        return pl.BlockSpec((None, tr, cols), lambda i, *sc: (sc[2 + j][0], i, 0))

    return pl.pallas_call(
        body, out_shape=SDS((2, r2, cols), F32),
        grid_spec=pltpu.PrefetchScalarGridSpec(
            num_scalar_prefetch=nsc, grid=(r2 // tr,),
            in_specs=[pl.BlockSpec((None, tr, cols), lambda i, *sc: (sc[0][0], i, 0))] + [slab(j) for j in range(nch)],
            out_specs=pl.BlockSpec((None, tr, cols), lambda i, *sc: (sc[1][0], i, 0))),
        compiler_params=_params(("parallel",)), name="add_chips")(*chip_core, own, *([recv] * nch))


def _exchange_col_halves(grad):
    nch, r, cols = grad.shape
    c2 = cols // 2

    def body(in_ref, out_ref, send_sem, recv_sem):
        x, y, c = _position()
        cp = pltpu.make_async_remote_copy(
            src_ref=in_ref.at[:, :, pl.ds((1 - c) * c2, c2)], dst_ref=out_ref, send_sem=send_sem.at[0],
            recv_sem=recv_sem.at[0], device_id=(x, y, 1 - c), device_id_type=MESH)
        cp.start()
        cp.wait()

    return pl.pallas_call(
        body, out_shape=SDS((nch, r, c2), grad.dtype), in_specs=[HBM_SPEC], out_specs=HBM_SPEC,
        scratch_shapes=[pltpu.SemaphoreType.DMA((1,)), pltpu.SemaphoreType.DMA((1,))],
        compiler_params=pltpu.CompilerParams(has_side_effects=True), name="reduce_sibling_cols")(grad)


def _add_sibling_cols(grad, recv, core):
    nch, r, c2 = recv.shape
    nb = c2 // LANES

    def body(c_ref, g_ref, r_ref, o_ref):
        del c_ref
        o_ref[...] = (g_ref[...].astype(F32) + r_ref[...].astype(F32)).astype(BF16)

    blk = pl.BlockSpec((None, r, LANES), lambda j, i, c_ref: (j, 0, i))
    return pl.pallas_call(
        body, out_shape=SDS(recv.shape, BF16),
        grid_spec=pltpu.PrefetchScalarGridSpec(
            num_scalar_prefetch=1, grid=(nch, nb),
            in_specs=[pl.BlockSpec((None, r, LANES), lambda j, i, c_ref: (j, 0, c_ref[0] * nb + i)), blk],
            out_specs=blk),
        compiler_params=_params(("parallel", "parallel")), name="add_sibling_cols")(core, grad, recv)


def _add_chips_cols(own, recv, chip_core):
    nch, r, c2 = recv.shape
    nb = c2 // LANES
    nsc = 2 + nch

    def body(*refs):
        me = refs[0][0]
        own_ref, p_refs, o_ref = refs[nsc], refs[nsc + 1:nsc + 1 + nch], refs[nsc + 1 + nch]
        acc = None
        for j in range(nch):
            term = jnp.where(me == j, own_ref[...], p_refs[j][...]).astype(F32)
            acc = term if acc is None else acc + term
        o_ref[...] = acc

    def slab(j):
        return pl.BlockSpec((None, r, LANES), lambda i, *sc: (sc[2 + j][0], 0, i))

    return pl.pallas_call(
        body, out_shape=SDS((r, 2 * c2), F32),
        grid_spec=pltpu.PrefetchScalarGridSpec(
            num_scalar_prefetch=nsc, grid=(nb,),
            in_specs=[pl.BlockSpec((None, r, LANES), lambda i, *sc: (sc[0][0], 0, i))] + [slab(j) for j in range(nch)],
            out_specs=pl.BlockSpec((r, LANES), lambda i, *sc: (0, sc[1][0] * nb + i))),
        compiler_params=_params(("parallel",)), name="add_chips_cols")(*chip_core, own, *([recv] * nch))


def _share_col_halves(full):
    r, cols = full.shape
    c2 = cols // 2

    def body(in_ref, out_ref, send_sem, recv_sem):
        x, y, c = _position()

        def copy(half):
            return pltpu.make_async_remote_copy(
                src_ref=in_ref.at[:, pl.ds(half * c2, c2)], dst_ref=out_ref.at[:, pl.ds(half * c2, c2)],
                send_sem=send_sem.at[0], recv_sem=recv_sem.at[0], device_id=(x, y, 1 - c), device_id_type=MESH)

        copy(c).start()
        copy(1 - c).wait_recv()
        copy(c).wait_send()

    return pl.pallas_call(
        body, out_shape=SDS(full.shape, full.dtype), in_specs=[HBM_SPEC], out_specs=HBM_SPEC,
        scratch_shapes=[pltpu.SemaphoreType.DMA((1,)), pltpu.SemaphoreType.DMA((1,))],
        input_output_aliases={0: 0},
        compiler_params=pltpu.CompilerParams(has_side_effects=True), name="share_sibling_cols")(full)


def _allreduce_small(pack):
    rows = pack.shape[0]

    def body(p_ref, o_ref, buf, send_sems, recv_sems):
        x, y, c = _position()
        me = 4 * x + 2 * y + c
        buf[me] = p_ref[...]

        def copy(dst_dev, slot):
            return pltpu.make_async_remote_copy(
                src_ref=p_ref, dst_ref=buf.at[slot], send_sem=send_sems.at[dst_dev], recv_sem=recv_sems.at[slot],
                device_id=(dst_dev // 4, (dst_dev // 2) % 2, dst_dev % 2), device_id_type=MESH)

        for dev in range(N_DEV):
            @pl.when(dev != me)
            def _():
                copy(dev, me).start()
        for dev in range(N_DEV):
            @pl.when(dev != me)
            def _():
                copy(dev, dev).wait_recv()
        for dev in range(N_DEV):
            @pl.when(dev != me)
            def _():
                copy(dev, me).wait_send()
        acc = buf[0]
        for dev in range(1, N_DEV):
            acc = acc + buf[dev]
        o_ref[...] = acc

    return pl.pallas_call(
        body, out_shape=SDS(pack.shape, F32),
        in_specs=[pl.BlockSpec(memory_space=pltpu.VMEM)], out_specs=pl.BlockSpec(memory_space=pltpu.VMEM),
        scratch_shapes=[pltpu.VMEM((N_DEV, rows, LANES), F32), pltpu.SemaphoreType.DMA((N_DEV,)),
                        pltpu.SemaphoreType.DMA((N_DEV,))],
        compiler_params=pltpu.CompilerParams(has_side_effects=True), name="allreduce_small")(pack)


ATTN_TQ = 256


def _local_step(cfg, x, target, w, to_chips=None, late=None, hn=None):
    d = cfg.D
    if hn is None:
        hn = _rmsnorm_fwd(x, w["norm_w"])
    proj = _mm(hn, w["w_main_t"], "nt", BF16, "proj_main", carry=late[0] if late else None)
    if late:
        proj, arrived = proj
        w = {**w, **late[1](arrived)}
    dt_raw = _mm(hn, w["w_dt_t"], "nt", F32, "proj_dt")
    slopes = _slopes(cfg.H)
    near = _Pass(ATTN_TQ, DILATED_PATTERNS[:-1], 1, cfg.S)
    far = _Pass(LANES, DILATED_PATTERNS[-1:], DEINT, cfg.S // DEINT)
    tab_near, tab_far = _attn_tables(near), _attn_tables(far)
    cols_near, cols_far = (cfg.OQ, cfg.OK, cfg.OV), (0, d, 2 * d)
    qkv_far = _deinterleave(proj, 0, 3 * d, "attn_deinterleave")
    o_1, lse_1 = _attn_fwd(cfg, near, proj, cols_near, tab_near, slopes, "attn_fwd_near")
    o_2, lse_2 = _attn_fwd(cfg, far, qkv_far, cols_far, tab_far, slopes, "attn_fwd_far")
    o_a, oag, lse = _attn_merge(cfg, proj, o_1, lse_1, o_2, lse_2)
    xact = _conv_fwd(cfg, proj, w["conv_w"], w["conv_b"])
    e = _expansion_matrix(cfg)
    y, y2, states = _ssd_fwd(cfg, xact, dt_raw, proj, w["dt_bias"], w["a_log"], w["d_skip"], w["ssm_norm_w"], e)
    a_br = _mm(oag, w["w_attn"], "nn", BF16, "branch_attn")
    s_br = _mm(y2, w["w_ssm"], "nn", BF16, "branch_ssm")
    merged = _merge_fwd(cfg, proj, a_br, s_br)
    dout_f, dout_b, loss_row, g_fnw = _outproj_loss(merged, w["w_out"], x, target, w["final_norm_w"])

    dmerged = _mm(dout_b, w["w_out"], "nt", BF16, "d_merged")
    g_w_out = _mm(merged, dout_b, "tn", BF16, "g_w_out")
    dproj, da_br = _merge_bwd(cfg, proj, a_br, dmerged, cfg.OGA, None, "merge_bwd_attn")
    dproj, ds_br = _merge_bwd(cfg, proj, s_br, dmerged, cfg.OGS, dproj, "merge_bwd_ssm")
    doag = _mm(da_br, w["w_attn"], "nt", BF16, "d_oag")
    g_w_attn = _mm(oag, da_br, "tn", BF16, "g_w_attn")
    dy2 = _mm(ds_br, w["w_ssm"], "nt", BF16, "d_y2")
    g_w_ssm = _mm(y2, ds_br, "tn", BF16, "g_w_ssm")
    dproj, dxact, ddt, g_snw, g_dtb, g_alog, g_dsk = _ssd_bwd(
        cfg, xact, dt_raw, proj, y, dy2, states, w["dt_bias"], w["a_log"], w["d_skip"], w["ssm_norm_w"], e, dproj)
    dproj, g_cw, g_cb = _conv_bwd(cfg, proj, dxact, w["conv_w"], w["conv_b"], dproj)
    dproj, do, do_far, dl, dl_far, lse_far = _attn_bwd_prep(cfg, proj, o_a, doag, lse, dproj)
    g_near = _attn_bwd(cfg, near, proj, cols_near, do, lse, dl, tab_near, slopes, "attn_bwd_near")
    g_far = _attn_bwd(cfg, far, qkv_far, cols_far, do_far, lse_far, dl_far, tab_far, slopes, "attn_bwd_far")
    for g_1, g_2, col0, nm in zip(g_near, g_far, cols_near, ("attn_dq", "attn_dk", "attn_dv")):
        dproj = _attn_grad_sum(cfg, g_1, g_2, col0, dproj, nm)
    ddt_b = ddt.astype(BF16)
    g_w_main = _mm(dproj, hn, "tn", BF16, "g_w_main")
    g_w_dt = _mm(ddt_b, hn, "tn", BF16, "g_w_dt")
    grads = dict(w_main_t=g_w_main, w_dt_t=g_w_dt, conv_w=g_cw, conv_b=g_cb, dt_bias=g_dtb, a_log=g_alog,
                 d_skip=g_dsk, ssm_norm_w=g_snw, w_attn=g_w_attn, w_ssm=g_w_ssm, w_out=g_w_out, final_norm_w=g_fnw)
    sent = to_chips(grads) if to_chips is not None else ()
    dhn = _mm(dproj, w["w_main_t"], "nn", F32, "d_hn", tk=1024, carry=_scatter_carry(sent) if sent else None)
    landed = ()
    if sent:
        dhn, landed = dhn
    dhn_dt = _mm(ddt_b, w["w_dt_t"], "nn", F32, "d_hn_dt")
    grad_x, grads["norm_w"] = _rmsnorm_bwd(x, w["norm_w"], dhn, dhn_dt, dout_f)
    return loss_row, grad_x, grads, sent, landed


def _pad_lanes(v):
    return jnp.pad(v, ((0, 0), (0, LANES - v.shape[1])))


def _main_from_rows(cfg, w_in_t):
    lo, hi = cfg.OGA, cfg.OGA + cfg.NH
    main = jnp.concatenate([w_in_t[:lo], w_in_t[hi:]], axis=0)
    return main, jnp.pad(w_in_t[lo:hi], ((0, LANES - cfg.NH), (0, 0)))


def _rows_from_main(cfg, g_main_t, g_dt_t):
    return jnp.concatenate([g_main_t[:cfg.OGA], g_dt_t[:cfg.NH], g_main_t[cfg.OGA:]], axis=0)


def _full_weights(cfg, norm_w, w_in_t, conv_w, conv_b, dt_bias, a_log, d_skip, ssm_norm_w, w_attn, w_ssm, w_out, fnw):
    w_main, w_dt = _main_from_rows(cfg, w_in_t)
    return dict(norm_w=norm_w, w_main_t=w_main.astype(BF16), w_dt_t=w_dt.astype(BF16), conv_w=conv_w, conv_b=conv_b,
                dt_bias=_pad_lanes(dt_bias), a_log=_pad_lanes(a_log), d_skip=_pad_lanes(d_skip), ssm_norm_w=ssm_norm_w,
                final_norm_w=fnw, **{k: v.astype(BF16) for k, v in (("w_attn", w_attn), ("w_ssm", w_ssm), ("w_out", w_out))
                                     if v is not None})


def kernel(x, norm_w, w_in, conv_w, conv_b, dt_bias, a_log, d_skip, ssm_norm_w, w_attn_branch, w_ssm_branch, w_out, final_norm_w, loss_target, m_norm_w, m_w_in, m_conv_w, m_conv_b, m_dt_bias, m_a_log, m_d_skip, m_ssm_norm_w, m_w_attn_branch, m_w_ssm_branch, m_w_out, m_final_norm_w, v_norm_w, v_w_in, v_conv_w, v_conv_b, v_dt_bias, v_a_log, v_d_skip, v_ssm_norm_w, v_w_attn_branch, v_w_ssm_branch, v_w_out, v_final_norm_w):
    cfg = _Cfg(x.shape[1], x.shape[2])
    d, si, cd, nh = cfg.D, cfg.SI, cfg.CD, cfg.NH
    chip = 2 * lax.axis_index("x") + lax.axis_index("y")
    core = lax.axis_index("c").astype(jnp.int32).reshape(1)
    chip = chip.astype(jnp.int32)
    chip_core = [chip.reshape(1), core] + [jnp.where(chip == j, (j + 1) % N_CHIPS, j).astype(jnp.int32).reshape(1)
                                           for j in range(N_CHIPS)]

    own = [jnp.transpose(w_in[0]).astype(BF16), conv_w[0].reshape(4 * CONV_K, -1)]
    hn, gathered = _rmsnorm_fwd(x[0], norm_w, carry=_gather_carry(own, by_cols=(0,)))
    a_in, a_cw = [_with_own(g, o, chip) for g, o in zip(gathered, own)]
    conv_w_full = a_cw.reshape(N_CHIPS, CONV_K, cd // N_CHIPS).transpose(1, 0, 2).reshape(CONV_K, cd)
    w = _full_weights(cfg, norm_w, a_in.reshape(cfg.N_IN, d), conv_w_full, conv_b, dt_bias, a_log, d_skip,
                      ssm_norm_w, None, None, None, final_norm_w.reshape(1, d))
    own_late = [w_attn_branch[0].astype(BF16), w_ssm_branch[0].astype(BF16), w_out[0].astype(BF16)]

    def late_weights(arrived):
        a_attn, a_ssm, a_out = [_with_own(g, o, chip) for g, o in zip(arrived, own_late)]
        return dict(w_attn=a_attn.reshape(d, d), w_ssm=a_ssm.reshape(si, d), w_out=a_out.reshape(d, d))

    def to_chips(grads):
        g_in_t = _rows_from_main(cfg, grads["w_main_t"], grads["w_dt_t"]).reshape(N_CHIPS, cfg.N_IN // N_CHIPS, d)
        by_chip = [grads["w_attn"].reshape(N_CHIPS, d // N_CHIPS, d),
                   grads["w_ssm"].reshape(N_CHIPS, si // N_CHIPS, d),
                   grads["w_out"].reshape(N_CHIPS, d // N_CHIPS, d)]
        from_sibling = _exchange_halves(by_chip)
        return ([_add_sibling_cols(g_in_t, _exchange_col_halves(g_in_t), core)]
                + [_add_sibling(g, r, core) for g, r in zip(by_chip, from_sibling)])

    loss_row, grad_x, grads, chip_sums, from_chips = _local_step(
        cfg, x[0], loss_target[0], w, to_chips, (_gather_carry(own_late), late_weights), hn)
    g_in_t = _share_col_halves(_add_chips_cols(chip_sums[0], from_chips[0], chip_core))
    halves = [_add_chips(o, p, chip_core) for o, p in zip(chip_sums[1:], from_chips[1:])]
    g_attn, g_ssm, g_out = [h.reshape(2 * h.shape[1], h.shape[2]) for h in _share_halves(halves)]
    g_in = jnp.transpose(g_in_t)

    small = [loss_row, grads["norm_w"], grads["conv_b"], grads["dt_bias"], grads["a_log"], grads["d_skip"],
             grads["ssm_norm_w"], grads["final_norm_w"], grads["conv_w"].reshape(1, CONV_K * cd)]
    sizes = [a.shape[1] for a in small]
    total = sum(sizes)
    rows = -(-total // (8 * LANES)) * 8
    flat = jnp.pad(jnp.concatenate(small, axis=1), ((0, 0), (0, rows * LANES - total)))
    red = _allreduce_small(flat.reshape(rows, LANES)).reshape(1, rows * LANES)
    offs = [sum(sizes[:i]) for i in range(len(sizes))]
    loss_r, g_nw, g_cb, g_dtb, g_alog, g_dsk, g_snw, g_fnw, g_cw_flat = [
        red[:, o:o + n] for o, n in zip(offs, sizes)]
    loss = loss_r[0, 0]
    g_dtb, g_alog, g_dsk = g_dtb[:, :nh], g_alog[:, :nh], g_dsk[:, :nh]
    cshard = cd // N_CHIPS
    g_cw = lax.dynamic_slice_in_dim(g_cw_flat.reshape(CONV_K, cd), chip * cshard, cshard, axis=1)

    upd = {}
    upd["w_in"] = tuple(jnp.transpose(u) for u in _adamw(
        jnp.transpose(w_in[0]), g_in_t, jnp.transpose(m_w_in[0]), jnp.transpose(v_w_in[0]), "adamw_w_in"))
    for name, wv, gv, mv, vv in [("w_attn", w_attn_branch[0], g_attn, m_w_attn_branch[0], v_w_attn_branch[0]),
                                 ("w_ssm", w_ssm_branch[0], g_ssm, m_w_ssm_branch[0], v_w_ssm_branch[0]),
                                 ("w_out", w_out[0], g_out, m_w_out[0], v_w_out[0])]:
        upd[name] = _adamw(wv, gv, mv, vv, "adamw_" + name)
    names = ["norm_w", "conv_w", "conv_b", "dt_bias", "a_log", "d_skip", "ssm_norm_w", "final_norm_w"]
    ws = [norm_w, conv_w[0].reshape(1, -1), conv_b, dt_bias, a_log, d_skip, ssm_norm_w, final_norm_w.reshape(1, d)]
    gs = [g_nw, g_cw.reshape(1, -1), g_cb, g_dtb, g_alog, g_dsk, g_snw, g_fnw]
    ms = [m_norm_w, m_conv_w[0].reshape(1, -1), m_conv_b, m_dt_bias, m_a_log, m_d_skip, m_ssm_norm_w,
          m_final_norm_w.reshape(1, d)]
    vs = [v_norm_w, v_conv_w[0].reshape(1, -1), v_conv_b, v_dt_bias, v_a_log, v_d_skip, v_ssm_norm_w,
          v_final_norm_w.reshape(1, d)]
    ssz = [a.shape[1] for a in ws]
    stot = sum(ssz)
    srows = -(-stot // (8 * LANES)) * 8

    def pack(parts):
        return jnp.pad(jnp.concatenate(parts, axis=1), ((0, 0), (0, srows * LANES - stot))).reshape(srows, LANES)

    packed = _adamw(pack(ws), pack(gs), pack(ms), pack(vs), "adamw_small")
    soffs = [sum(ssz[:i]) for i in range(len(ssz))]
    for k, nm in enumerate(names):
        upd[nm] = tuple(p.reshape(1, srows * LANES)[:, soffs[k]:soffs[k] + ssz[k]] for p in packed)

    shapes = dict(norm_w=norm_w.shape, w_in=w_in.shape, conv_w=conv_w.shape, conv_b=conv_b.shape, dt_bias=dt_bias.shape,
                  a_log=a_log.shape, d_skip=d_skip.shape, ssm_norm_w=ssm_norm_w.shape, w_attn=w_attn_branch.shape,
                  w_ssm=w_ssm_branch.shape, w_out=w_out.shape, final_norm_w=final_norm_w.shape)
    order = ["norm_w", "w_in", "conv_w", "conv_b", "dt_bias", "a_log", "d_skip", "ssm_norm_w", "w_attn", "w_ssm",
             "w_out", "final_norm_w"]
    gradv = dict(norm_w=g_nw, w_in=g_in, conv_w=g_cw, conv_b=g_cb, dt_bias=g_dtb, a_log=g_alog, d_skip=g_dsk,
                 ssm_norm_w=g_snw, w_attn=g_attn, w_ssm=g_ssm, w_out=g_out, final_norm_w=g_fnw)
    outs = [loss, grad_x[None]]
    outs += [gradv[n].reshape(shapes[n]) for n in order]
    for k in range(3):
        outs += [upd[n][k].reshape(shapes[n]) for n in order]
    return tuple(outs)
```

```python
import jax
import jax.numpy as jnp
from jax import lax
from jax.experimental import pallas as pl
from jax.experimental.pallas import tpu as pltpu

F32 = jnp.float32
BF16 = jnp.bfloat16
SDS = jax.ShapeDtypeStruct

RMS_EPS = 1e-6
LANES = 128
CHUNK = 128
SSM_HEAD_DIM = 64
SSM_GROUPS = 8
SSM_STATE = 128
CONV_K = 4
ATTN_HEAD_DIM = 128
DILATED_PATTERNS = ((128, 1), (512, 4), (2048, 16))
NEG = -1e30
VMEM_LIMIT = 56 * 1024 * 1024
ADAM_LR, ADAM_B1, ADAM_B2, ADAM_EPS, ADAM_WD, ADAM_STEP = 0.001, 0.9, 0.999, 1e-08, 0.01, 10
MESH = pl.DeviceIdType.MESH
N_CHIPS = 4
N_DEV = 8


class _Cfg:
    def __init__(self, s, d):
        self.S, self.D = s, d
        self.H = d // ATTN_HEAD_DIM
        self.SI = 2 * d
        self.NH = self.SI // SSM_HEAD_DIM
        self.HPG = self.NH // SSM_GROUPS
        self.GW = self.HPG * SSM_HEAD_DIM
        self.BC = SSM_GROUPS * SSM_STATE
        self.CD = self.SI + 2 * self.BC
        self.OQ, self.OK, self.OV, self.OZA = 0, d, 2 * d, 3 * d
        self.OZS = 4 * d
        self.OXBC = self.OZS + self.SI
        self.OGA = self.OXBC + self.CD
        self.OGS = self.OGA + d
        self.NM = self.OGS + d
        self.N_IN = self.NM + self.NH
        assert self.GW % LANES == 0 and self.NH <= LANES and s % 512 == 0 and d % 512 == 0


def _params(sem=None):
    return pltpu.CompilerParams(dimension_semantics=sem, vmem_limit_bytes=VMEM_LIMIT)


def _sigmoid(x):
    return 0.5 * jnp.tanh(0.5 * x) + 0.5


def _softplus(x):
    u = jnp.exp(-jnp.abs(x))
    l1p = jnp.where(u < 1e-3, u * (1.0 - u * (0.5 - u * (1.0 / 3.0))), jnp.log(1.0 + u))
    return jnp.maximum(x, 0.0) + l1p


def _nt(a, b):
    return lax.dot_general(a, b, (((1,), (1,)), ((), ())), preferred_element_type=F32)


def _tn(a, b):
    return lax.dot_general(a, b, (((0,), (0,)), ((), ())), preferred_element_type=F32)


def _nn(a, b):
    return jnp.dot(a, b, preferred_element_type=F32)


def _tile(n, target):
    if n <= target:
        return n
    best = None
    for t in range(LANES, target + 1, LANES):
        if n % t == 0:
            best = t
    assert best is not None, (n, target)
    return best


MM_TK = {"nn": 2048, "nt": 2048, "tn": 1024}


def _mm(a, b, dims, out_dtype, name, tm=1024, tn=2048, tk=None, init=None, carry=None, b_rows=None, out_rows=None):
    tk = MM_TK[dims] if tk is None else tk
    if dims == "nn":
        (m, k), (k2, n) = a.shape, b.shape
        k2 = k2 if b_rows is None else b_rows
    elif dims == "nt":
        (m, k), (n, k2) = a.shape, b.shape
        n = n if b_rows is None else b_rows
    else:
        (k, m), (k2, n) = a.shape, b.shape
    assert k == k2
    tm, tn, tk = _tile(m, tm), _tile(n, tn), _tile(k, tk)
    nk = k // tk
    if dims == "tn":
        a_spec = pl.BlockSpec((tk, tm), lambda i, j, kk: (kk, i))
    else:
        a_spec = pl.BlockSpec((tm, tk), lambda i, j, kk: (i, kk))
    if dims == "nt":
        b_spec = pl.BlockSpec((tn, tk), lambda i, j, kk: (j, kk))
    else:
        b_spec = pl.BlockSpec((tk, tn), lambda i, j, kk: (kk, j))
    o_spec = pl.BlockSpec((tm, tn), lambda i, j, kk: (i, j))
    op = {"nn": _nn, "nt": _nt, "tn": _tn}[dims]
    has_init = init is not None
    nx = len(carry.arrays) if carry is not None else 0
    ni, nj = m // tm, n // tn

    def body(*refs):
        a_ref, b_ref = refs[0], refs[1]
        i_ref = refs[2] if has_init else None
        x_in = refs[2 + has_init:2 + has_init + nx]
        o_ref = refs[2 + has_init + nx]
        x_out = refs[3 + has_init + nx:3 + has_init + 2 * nx]
        acc = refs[3 + has_init + 2 * nx]
        x_sems = refs[4 + has_init + 2 * nx:]
        i, j, kk = pl.program_id(0), pl.program_id(1), pl.program_id(2)

        if nx:
            @pl.when((i == 0) & (j == 0) & (kk == 0))
            def _():
                carry.start(x_in, x_out, x_sems)

        prod = lambda: op(a_ref[...], b_ref[...])
        with_init = (lambda p: p + i_ref[...].astype(F32)) if has_init else (lambda p: p)
        if nk == 1:
            o_ref[...] = with_init(prod()).astype(out_dtype)
        else:
            @pl.when(kk == 0)
            def _():
                acc[...] = with_init(prod())

            @pl.when((kk > 0) & (kk < nk - 1))
            def _():
                acc[...] += prod()

            @pl.when(kk == nk - 1)
            def _():
                o_ref[...] = (acc[...] + prod()).astype(out_dtype)

        if nx:
            @pl.when((i == ni - 1) & (j == nj - 1) & (kk == nk - 1))
            def _():
                carry.finish(x_in, x_out, x_sems)

    in_specs = [a_spec, b_spec] + ([o_spec] if has_init else []) + [HBM_SPEC] * nx
    args = (a, b) + ((init,) if has_init else ()) + (tuple(carry.arrays) if nx else ())
    sems = carry.sem_shapes() if nx else []
    outs = pl.pallas_call(
        body, out_shape=[SDS((m if out_rows is None else out_rows, n), out_dtype)] + (carry.out_shapes if nx else []),
        grid=(ni, nj, nk),
        in_specs=in_specs, out_specs=[o_spec] + [HBM_SPEC] * nx,
        scratch_shapes=[pltpu.VMEM((tm, tn) if nk > 1 else (8, LANES), F32)] + sems,
        compiler_params=_params(("arbitrary",) * 3 if nx else ("parallel", "parallel", "arbitrary")), name=name)(*args)
    return (outs[0], outs[1:]) if nx else outs[0]


def _rmsnorm_fwd(x, w, carry=None):
    s, d = x.shape
    tr = 256
    nsteps = s // tr
    nx = len(carry.arrays) if carry is not None else 0

    def body(*refs):
        x_ref, w_ref, x_in = refs[0], refs[1], refs[2:2 + nx]
        o_ref, x_out, x_sems = refs[2 + nx], refs[3 + nx:3 + 2 * nx], refs[3 + 2 * nx:]
        if nx:
            @pl.when(pl.program_id(0) == 0)
            def _():
                carry.start(x_in, x_out, x_sems)

        xv = x_ref[...]
        r = lax.rsqrt(jnp.mean(xv * xv, axis=-1, keepdims=True) + RMS_EPS)
        o_ref[...] = (xv * r * w_ref[...]).astype(BF16)

        if nx:
            @pl.when(pl.program_id(0) == nsteps - 1)
            def _():
                carry.finish(x_in, x_out, x_sems)

    outs = pl.pallas_call(
        body, out_shape=[SDS((s, d), BF16)] + (carry.out_shapes if nx else []), grid=(nsteps,),
        in_specs=[pl.BlockSpec((tr, d), lambda i: (i, 0)), pl.BlockSpec((1, d), lambda i: (0, 0))] + [HBM_SPEC] * nx,
        out_specs=[pl.BlockSpec((tr, d), lambda i: (i, 0))] + [HBM_SPEC] * nx,
        scratch_shapes=carry.sem_shapes() if nx else [],
        compiler_params=_params(("arbitrary",) if nx else ("parallel",)), name="rmsnorm_fwd")(
            x, w, *(carry.arrays if nx else []))
    return (outs[0], outs[1:]) if nx else outs[0]


def _rmsnorm_bwd(x, w, dhn_a, dhn_b, dout):
    s, d = x.shape
    tr = 256

    def body(x_ref, w_ref, dh_ref, dh2_ref, do_ref, gx_ref, gw_ref):
        xv = x_ref[...]
        r = lax.rsqrt(jnp.mean(xv * xv, axis=-1, keepdims=True) + RMS_EPS)
        nrm = xv * r
        dh = dh_ref[...] + dh2_ref[...]
        gy = dh * w_ref[...]
        gx_ref[...] = do_ref[...] + r * (gy - nrm * jnp.mean(gy * nrm, axis=-1, keepdims=True))

        @pl.when(pl.program_id(0) == 0)
        def _():
            gw_ref[...] = jnp.zeros_like(gw_ref)

        gw_ref[...] += jnp.sum(dh * nrm, axis=0, keepdims=True)

    blk = pl.BlockSpec((tr, d), lambda i: (i, 0))
    row = pl.BlockSpec((1, d), lambda i: (0, 0))
    return pl.pallas_call(
        body, out_shape=(SDS((s, d), F32), SDS((1, d), F32)), grid=(s // tr,),
        in_specs=[blk, row, blk, blk, blk], out_specs=(blk, row),
        compiler_params=_params(("arbitrary",)), name="rmsnorm_bwd")(x, w, dhn_a, dhn_b, dout)


DEINT = DILATED_PATTERNS[-1][1]
DEINT_ROWS = DEINT * LANES


class _Pass:
    def __init__(self, tq, patterns, unit, seg_len):
        self.tq, self.patterns, self.unit, self.seg_len = tq, patterns, unit, seg_len
        self.win = max(w for w, _ in patterns) // unit
        self.w = self.win + tq
        assert self.win % tq == 0


def _attn_tables(ps):
    i = jnp.arange(ps.tq, dtype=jnp.int32)[:, None]
    j = jnp.arange(ps.w, dtype=jnp.int32)[None, :]
    delta = (i + ps.win - j) * ps.unit
    n = jnp.zeros((ps.tq, ps.w), F32)
    for window, dil in ps.patterns:
        n = n + ((delta >= 0) & (delta <= window) & (delta % dil == 0)).astype(F32)
    logn = jnp.where(n > 0, jnp.log(jnp.maximum(n, 1.0)), NEG)
    return logn, jnp.maximum(delta, 0).astype(F32)


def _slopes(h):
    s = jnp.asarray([2.0 ** (-8.0 * (i + 1) / h) for i in range(h)], F32)
    return jnp.broadcast_to(s[:, None, None], (h, 1, LANES))


def _masked_logn(ps, logn_ref, start):
    col = lax.broadcasted_iota(jnp.int32, (ps.tq, ps.w), 1)
    return jnp.where(col >= ps.win - lax.rem(start, ps.seg_len), logn_ref[...], NEG)


def _head_cols(hh):
    return slice(hh * ATTN_HEAD_DIM, (hh + 1) * ATTN_HEAD_DIM)


def _head_window(refs, cs):
    return jnp.concatenate([r[:, cs] for r in refs], axis=0)


def _head_scores(q_ref, kw, cs, base, dist_ref, slope_ref, hh):
    return _nt(q_ref[:, cs], kw) * (ATTN_HEAD_DIM ** -0.5) + (base - slope_ref[hh][0:1, 0:1] * dist_ref[...])


def _lane_of(stat, hh):
    lane = lax.broadcasted_iota(jnp.int32, stat.shape, 1)
    return jnp.sum(jnp.where(lane == hh, stat, 0.0), axis=1, keepdims=True)


def _window_specs(ps, d, col, nb):
    nprev = ps.win // ps.tq
    return [pl.BlockSpec((ps.tq, d), lambda i, b=b: (jnp.maximum(jnp.minimum(i, nb - 1) - (nprev - b), 0), col))
            for b in range(nprev + 1)]


def _attn_fwd(cfg, ps, qkv, cols, tables, slopes, name):
    s, h, d = cfg.S, cfg.H, cfg.D
    tq, nw = ps.tq, ps.win // ps.tq + 1
    nb = s // tq
    logn, dist = tables
    qc, kc, vc = [c // d for c in cols]

    def body(*refs):
        q_ref, k_refs, v_refs = refs[0], refs[1:1 + nw], refs[1 + nw:1 + 2 * nw]
        logn_ref, dist_ref, slope_ref, o_ref, lse_ref = refs[1 + 2 * nw:]
        base = _masked_logn(ps, logn_ref, pl.program_id(0) * tq)
        lane = lax.broadcasted_iota(jnp.int32, (tq, LANES), 1)

        lse = jnp.zeros((tq, LANES), F32)
        for hh in range(h):
            cs = _head_cols(hh)
            sc = _head_scores(q_ref, _head_window(k_refs, cs), cs, base, dist_ref, slope_ref, hh)
            m = jnp.max(sc, axis=1, keepdims=True)
            p = jnp.exp(sc - m)
            l = jnp.sum(p, axis=1, keepdims=True)
            o_ref[:, cs] = (_nn(p.astype(BF16), _head_window(v_refs, cs)) / l).astype(BF16)
            lse = jnp.where(lane == hh, m + jnp.log(l), lse)
        lse_ref[...] = lse

    tab = pl.BlockSpec((tq, ps.w), lambda i: (0, 0))
    return pl.pallas_call(
        body, out_shape=(SDS((s, d), BF16), SDS((s, LANES), F32)), grid=(nb,),
        in_specs=[pl.BlockSpec((tq, d), lambda i: (i, qc))] + _window_specs(ps, d, kc, nb) + _window_specs(ps, d, vc, nb)
        + [tab, tab, pl.BlockSpec((h, 1, LANES), lambda i: (0, 0, 0))],
        out_specs=(pl.BlockSpec((tq, d), lambda i: (i, 0)), pl.BlockSpec((tq, LANES), lambda i: (i, 0))),
        compiler_params=_params(("parallel",)), name=name)(*([qkv] * (1 + 2 * nw)), logn, dist, slopes)


def _attn_bwd(cfg, ps, qkv, cols, do, lse, delta, tables, slopes, name):
    s, h, d = cfg.S, cfg.H, cfg.D
    tq, nprev = ps.tq, ps.win // ps.tq
    nw = nprev + 1
    nb = s // tq
    logn, dist = tables
    qc, kc, vc = [c // d for c in cols]
    scale = ATTN_HEAD_DIM ** -0.5

    def body(*refs):
        q_ref, k_refs, v_refs = refs[0], refs[1:1 + nw], refs[1 + nw:1 + 2 * nw]
        do_ref, lse_ref, dl_ref, logn_ref, dist_ref, slope_ref, dq_ref, dk_ref, dv_ref, ck, cv = refs[1 + 2 * nw:]
        i = pl.program_id(0)
        slot = lambda b: lax.rem(i + b, nprev)

        @pl.when(i == 0)
        def _():
            ck[...] = jnp.zeros_like(ck)
            cv[...] = jnp.zeros_like(cv)

        @pl.when(i < nb)
        def _():
            base = _masked_logn(ps, logn_ref, i * tq)
            lse_all, dl_all = lse_ref[...], dl_ref[...]

            for hh in range(h):
                cs = _head_cols(hh)
                kw, vw = _head_window(k_refs, cs), _head_window(v_refs, cs)
                sc = _head_scores(q_ref, kw, cs, base, dist_ref, slope_ref, hh)
                p = jnp.exp(sc - lse_all[:, hh:hh + 1])
                dob = do_ref[:, cs]
                ds = (p * (_nt(dob, vw) - dl_all[:, hh:hh + 1]) * scale).astype(BF16)
                dq_ref[:, cs] = _nn(ds, kw).astype(BF16)
                dkw = _tn(ds, q_ref[:, cs])
                dvw = _tn(p.astype(BF16), dob)
                dk_ref[:, cs] = ck[slot(0), :, cs] + dkw[0:tq]
                dv_ref[:, cs] = cv[slot(0), :, cs] + dvw[0:tq]
                for b in range(1, nprev):
                    ck[slot(b), :, cs] += dkw[b * tq:(b + 1) * tq]
                    cv[slot(b), :, cs] += dvw[b * tq:(b + 1) * tq]
                ck[slot(0), :, cs] = dkw[nprev * tq:]
                cv[slot(0), :, cs] = dvw[nprev * tq:]

        @pl.when(i >= nb)
        def _():
            dk_ref[...] = ck[slot(0)]
            dv_ref[...] = cv[slot(0)]

    here = lambda i: jnp.minimum(i, nb - 1)
    blk = pl.BlockSpec((tq, d), lambda i: (here(i), 0))
    stat = pl.BlockSpec((tq, LANES), lambda i: (here(i), 0))
    late = pl.BlockSpec((tq, d), lambda i: (jnp.maximum(i - nprev, 0), 0))
    tab = pl.BlockSpec((tq, ps.w), lambda i: (0, 0))
    return pl.pallas_call(
        body, out_shape=(SDS((s, d), BF16), SDS((s, d), F32), SDS((s, d), F32)), grid=(nb + nprev,),
        in_specs=[pl.BlockSpec((tq, d), lambda i: (here(i), qc))] + _window_specs(ps, d, kc, nb)
        + _window_specs(ps, d, vc, nb) + [blk, stat, stat, tab, tab, pl.BlockSpec((h, 1, LANES), lambda i: (0, 0, 0))],
        out_specs=(blk, late, late),
        scratch_shapes=[pltpu.VMEM((nprev, tq, d), F32), pltpu.VMEM((nprev, tq, d), F32)],
        compiler_params=_params(("arbitrary",)), name=name)(
            *([qkv] * (1 + 2 * nw)), do, lse, delta, logn, dist, slopes)


def _by_residue(a):
    return a.reshape(DEINT, a.shape[0] // DEINT, a.shape[1])


def _deint_spec(colblock):
    return pl.BlockSpec((DEINT, LANES, LANES), lambda b, j: (0, b, colblock(j)))


def _deint_rows(scr, out_ref, dtype):
    for r in range(DEINT):
        out_ref[r] = scr[pl.ds(r, LANES, stride=DEINT), :].astype(dtype)


def _int_rows(in_ref, scr):
    for r in range(DEINT):
        scr[pl.ds(r, LANES, stride=DEINT), :] = in_ref[r].astype(F32)


WIDE = 4 * LANES


def _wide_spec():
    return pl.BlockSpec((DEINT, LANES, WIDE), lambda b, j: (0, b, j))


def _deinterleave(x, col0, ncols, name):
    s = x.shape[0]
    c0 = col0 // WIDE

    def body(x_ref, o_ref, scr):
        for t in range(WIDE // LANES):
            cs = slice(t * LANES, (t + 1) * LANES)
            scr[t] = x_ref[:, cs].astype(F32)
            for r in range(DEINT):
                o_ref[r, :, cs] = scr.at[t][pl.ds(r, LANES, stride=DEINT), :].astype(x.dtype)

    out = pl.pallas_call(
        body, out_shape=SDS((DEINT, s // DEINT, ncols), x.dtype), grid=(s // DEINT_ROWS, ncols // WIDE),
        in_specs=[pl.BlockSpec((DEINT_ROWS, WIDE), lambda b, j: (b, c0 + j))],
        out_specs=_wide_spec(),
        scratch_shapes=[pltpu.VMEM((WIDE // LANES, DEINT_ROWS, LANES), F32)],
        compiler_params=_params(("parallel", "parallel")), name=name)(x)
    return out.reshape(s, ncols)


def _attn_merge(cfg, proj, o_1, lse_1, o_2, lse_2):
    s, h = cfg.S, cfg.H
    zb = cfg.OZA // WIDE
    rows = DEINT_ROWS
    hps = WIDE // LANES

    def body(o1_ref, l1_ref, o2_ref, l2_ref, z_ref, o_ref, og_ref, lse_ref, so, sl):
        j = pl.program_id(1)

        @pl.when(j == 0)
        def _():
            _int_rows(l2_ref, sl)
            lse_ref[...] = jnp.zeros_like(lse_ref)

        l1_all, l2_all = l1_ref[...], sl[...]
        lane = lax.broadcasted_iota(jnp.int32, (rows, LANES), 1)
        lse = lse_ref[...]
        for t in range(hps):
            hh = j * hps + t
            cs = slice(t * LANES, (t + 1) * LANES)
            for r in range(DEINT):
                so.at[t][pl.ds(r, LANES, stride=DEINT), :] = o2_ref[r, :, cs].astype(F32)
            l1, l2 = _lane_of(l1_all, hh), _lane_of(l2_all, hh)
            mx = jnp.maximum(l1, l2)
            w1, w2 = jnp.exp(l1 - mx), jnp.exp(l2 - mx)
            den = w1 + w2
            o = (w1 * o1_ref[:, cs].astype(F32) + w2 * so[t]) / den
            z = z_ref[:, cs].astype(F32)
            o_ref[:, cs] = o.astype(BF16)
            og_ref[:, cs] = (o * (z * _sigmoid(z))).astype(BF16)
            lse = jnp.where(lane == hh, mx + jnp.log(den), lse)
        lse_ref[...] = lse

    blk = pl.BlockSpec((rows, WIDE), lambda b, j: (b, j))
    stat = pl.BlockSpec((rows, LANES), lambda b, j: (b, 0))
    return pl.pallas_call(
        body, out_shape=(SDS((s, cfg.D), BF16), SDS((s, cfg.D), BF16), SDS((s, LANES), F32)),
        grid=(s // rows, h // hps),
        in_specs=[blk, stat, _wide_spec(), _deint_spec(lambda j: 0), pl.BlockSpec((rows, WIDE), lambda b, j: (b, zb + j))],
        out_specs=(blk, blk, stat),
        scratch_shapes=[pltpu.VMEM((hps, rows, LANES), F32), pltpu.VMEM((rows, LANES), F32)],
        compiler_params=_params(("parallel", "arbitrary")), name="attn_merge")(
            o_1, lse_1, _by_residue(o_2), _by_residue(lse_2), proj)


def _attn_bwd_prep(cfg, proj, o_a, doag, lse, dproj):
    s, h = cfg.S, cfg.H
    zb = cfg.OZA // WIDE
    rows = DEINT_ROWS
    hps = WIDE // LANES

    def body(o_ref, dg_ref, z_ref, lse_ref, dp_in, dz_ref, do_ref, do2_ref, dl_ref, dl2_ref, lse2_ref, scr):
        del dp_in
        j = pl.program_id(1)

        @pl.when(j == 0)
        def _():
            dl_ref[...] = jnp.zeros_like(dl_ref)

        lane = lax.broadcasted_iota(jnp.int32, (rows, LANES), 1)
        dl = dl_ref[...]
        for t in range(hps):
            cs = slice(t * LANES, (t + 1) * LANES)
            z = z_ref[:, cs].astype(F32)
            sg = _sigmoid(z)
            o = o_ref[:, cs].astype(F32)
            dg = dg_ref[:, cs].astype(F32)
            do = dg * (z * sg)
            dz_ref[:, cs] = (dg * o * (sg * (1.0 + z * (1.0 - sg)))).astype(BF16)
            do_ref[:, cs] = do.astype(BF16)
            scr[...] = do
            for r in range(DEINT):
                do2_ref[r, :, cs] = scr[pl.ds(r, LANES, stride=DEINT), :].astype(BF16)
            dl = jnp.where(lane == j * hps + t, jnp.sum(do * o, axis=1, keepdims=True), dl)
        dl_ref[...] = dl

        @pl.when(j == h // hps - 1)
        def _():
            scr[...] = dl
            _deint_rows(scr, dl2_ref, F32)
            scr[...] = lse_ref[...]
            _deint_rows(scr, lse2_ref, F32)

    blk = pl.BlockSpec((rows, WIDE), lambda b, j: (b, j))
    stat = pl.BlockSpec((rows, LANES), lambda b, j: (b, 0))
    stat2 = _deint_spec(lambda j: 0)
    outs = pl.pallas_call(
        body,
        out_shape=(SDS(dproj.shape, BF16), SDS((s, cfg.D), BF16), SDS((DEINT, s // DEINT, cfg.D), BF16),
                   SDS((s, LANES), F32), SDS((DEINT, s // DEINT, LANES), F32), SDS((DEINT, s // DEINT, LANES), F32)),
        grid=(s // rows, h // hps),
        in_specs=[blk, blk, pl.BlockSpec((rows, WIDE), lambda b, j: (b, zb + j)), stat, HBM_SPEC],
        out_specs=(pl.BlockSpec((rows, WIDE), lambda b, j: (b, zb + j)), blk, _wide_spec(), stat, stat2, stat2),
        scratch_shapes=[pltpu.VMEM((rows, LANES), F32)],
        input_output_aliases={4: 0},
        compiler_params=_params(("parallel", "arbitrary")), name="attn_bwd_prep")(o_a, doag, proj, lse, dproj)
    dproj, do, do2, dl, dl2, lse2 = outs
    return dproj, do, do2.reshape(s, cfg.D), dl, dl2.reshape(s, LANES), lse2.reshape(s, LANES)


def _attn_grad_sum(cfg, g_1, g_2, col0, dproj, name):
    s = cfg.S
    c0 = col0 // WIDE
    rows = DEINT_ROWS

    def body(g1_ref, g2_ref, dp_in, o_ref, scr):
        del dp_in
        for t in range(WIDE // LANES):
            cs = slice(t * LANES, (t + 1) * LANES)
            for r in range(DEINT):
                scr.at[t][pl.ds(r, LANES, stride=DEINT), :] = g2_ref[r, :, cs].astype(F32)
            o_ref[:, cs] = (g1_ref[:, cs].astype(F32) + scr[t]).astype(BF16)

    return pl.pallas_call(
        body, out_shape=SDS(dproj.shape, BF16), grid=(s // rows, cfg.D // WIDE),
        in_specs=[pl.BlockSpec((rows, WIDE), lambda b, j: (b, j)), _wide_spec(), HBM_SPEC],
        out_specs=pl.BlockSpec((rows, WIDE), lambda b, j: (b, c0 + j)),
        scratch_shapes=[pltpu.VMEM((WIDE // LANES, rows, LANES), F32)],
        input_output_aliases={2: 0},
        compiler_params=_params(("parallel", "parallel")), name=name)(g_1, _by_residue(g_2), dproj)


CONV_HALO = 16
CONV_TR = 512
CONV_CW = 1024


def _rows_back(a, n):
    return a if n == 0 else pltpu.roll(a, n % a.shape[0], axis=0)


def _conv_fwd(cfg, proj, conv_w, conv_b):
    s, cd = cfg.S, cfg.CD
    tr, cw, hl = CONV_TR, CONV_CW, CONV_HALO
    cb0 = cfg.OXBC // cw

    def body(x_ref, h_ref, w_ref, b_ref, o_ref):
        i = pl.program_id(0)
        halo = jnp.where(i > 0, h_ref[...].astype(F32), 0.0)
        ext = jnp.concatenate([halo, x_ref[...].astype(F32)], axis=0)
        pre = b_ref[...] + jnp.zeros((tr, cw), F32)
        for k in range(CONV_K):
            pre = pre + w_ref[k:k + 1, :] * _rows_back(ext, CONV_K - 1 - k)[hl:]
        o_ref[...] = (pre * _sigmoid(pre)).astype(BF16)

    return pl.pallas_call(
        body, out_shape=SDS((s, cd), BF16), grid=(s // tr, cd // cw),
        in_specs=[pl.BlockSpec((tr, cw), lambda i, j: (i, cb0 + j)),
                  pl.BlockSpec((hl, cw), lambda i, j: (jnp.maximum(i * (tr // hl) - 1, 0), cb0 + j)),
                  pl.BlockSpec((CONV_K, cw), lambda i, j: (0, j)),
                  pl.BlockSpec((1, cw), lambda i, j: (0, j))],
        out_specs=pl.BlockSpec((tr, cw), lambda i, j: (i, j)),
        compiler_params=_params(("parallel", "parallel")), name="conv_fwd")(proj, proj, conv_w, conv_b)


def _conv_bwd(cfg, proj, dact, conv_w, conv_b, dproj):
    s, cd = cfg.S, cfg.CD
    tr, cw, hl = CONV_TR, CONV_CW, CONV_HALO
    cb0 = cfg.OXBC // cw
    nr = s // tr
    last_h = s // hl - 1

    def body(x_ref, hp_ref, hn_ref, d_ref, dn_ref, w_ref, b_ref, dp_in, dx_ref, gw_ref, gb_ref):
        del dp_in
        i = pl.program_id(1)
        ext = jnp.concatenate([jnp.where(i > 0, hp_ref[...].astype(F32), 0.0), x_ref[...].astype(F32),
                               hn_ref[...].astype(F32)], axis=0)
        shifted = [_rows_back(ext, CONV_K - 1 - k)[hl:] for k in range(CONV_K)]
        pre = b_ref[...] + jnp.zeros((tr + hl, cw), F32)
        for k in range(CONV_K):
            pre = pre + w_ref[k:k + 1, :] * shifted[k]
        sg = _sigmoid(pre)
        dact = jnp.concatenate([d_ref[...].astype(F32), jnp.where(i < nr - 1, dn_ref[...].astype(F32), 0.0)], axis=0)
        dpre = dact * (sg * (1.0 + pre * (1.0 - sg)))
        dx = jnp.zeros((tr, cw), F32)
        for k in range(CONV_K):
            dx = dx + w_ref[k:k + 1, :] * _rows_back(dpre, -(CONV_K - 1 - k))[0:tr]
        dx_ref[...] = dx.astype(BF16)

        @pl.when(i == 0)
        def _():
            gw_ref[...] = jnp.zeros_like(gw_ref)
            gb_ref[...] = jnp.zeros_like(gb_ref)

        dcur = dpre[0:tr]
        gb_ref[...] += jnp.sum(dcur, axis=0, keepdims=True)
        for k in range(CONV_K):
            gw_ref[k:k + 1, :] += jnp.sum(dcur * shifted[k][0:tr], axis=0, keepdims=True)

    return pl.pallas_call(
        body, out_shape=(SDS(dproj.shape, BF16), SDS((CONV_K, cd), F32), SDS((1, cd), F32)), grid=(cd // cw, nr),
        in_specs=[pl.BlockSpec((tr, cw), lambda j, i: (i, cb0 + j)),
                  pl.BlockSpec((hl, cw), lambda j, i: (jnp.maximum(i * (tr // hl) - 1, 0), cb0 + j)),
                  pl.BlockSpec((hl, cw), lambda j, i: (jnp.minimum((i + 1) * (tr // hl), last_h), cb0 + j)),
                  pl.BlockSpec((tr, cw), lambda j, i: (i, j)),
                  pl.BlockSpec((hl, cw), lambda j, i: (jnp.minimum((i + 1) * (tr // hl), last_h), j)),
                  pl.BlockSpec((CONV_K, cw), lambda j, i: (0, j)),
                  pl.BlockSpec((1, cw), lambda j, i: (0, j)),
                  pl.BlockSpec(memory_space=pl.ANY)],
        out_specs=(pl.BlockSpec((tr, cw), lambda j, i: (i, cb0 + j)),
                   pl.BlockSpec((CONV_K, cw), lambda j, i: (0, j)),
                   pl.BlockSpec((1, cw), lambda j, i: (0, j))),
        input_output_aliases={7: 0},
        compiler_params=_params(("parallel", "arbitrary")), name="conv_bwd")(
            proj, proj, proj, dact, dact, conv_w, conv_b, dproj)


def _expand(v, e, terms):
    out, rem = None, v
    for _ in range(terms):
        hi = rem.astype(BF16)
        t = _nn(hi, e)
        out = t if out is None else out + t
        rem = rem - hi.astype(F32)
    return out


def _segsum(v, e, terms):
    out, rem = None, v
    for _ in range(terms):
        hi = rem.astype(BF16)
        t = _nt(hi, e)
        out = t if out is None else out + t
        rem = rem - hi.astype(F32)
    return out


def _expand_row(row, e, terms):
    return _expand(jnp.broadcast_to(row, (8, LANES)), e, terms)[0:1]


def _segsum_row(row, e, terms):
    return _segsum(jnp.broadcast_to(row, (8, row.shape[1])), e, terms)[0:1]


def _expansion_matrix(cfg):
    hh = jnp.arange(LANES, dtype=jnp.int32)[:, None]
    cc = jnp.arange(cfg.SI, dtype=jnp.int32)[None, :]
    return (cc // SSM_HEAD_DIM == hh).astype(BF16)


def _tri(lower):
    r = lax.broadcasted_iota(jnp.int32, (CHUNK, CHUNK), 0)
    c = lax.broadcasted_iota(jnp.int32, (CHUNK, CHUNK), 1)
    return (c <= r) if lower else (c >= r)


def _ssd_prep(dtr_ref, db_ref, al_ref, e):
    dtr = dtr_ref[...] + db_ref[...]
    dt = _softplus(dtr)
    a = -jnp.exp(al_ref[...])
    acum = jnp.dot(_tri(True).astype(F32), dt * a, precision=lax.Precision.HIGHEST, preferred_element_type=F32)
    return dtr, dt, a, _expand(dt, e, 2), _expand(acum, e, 3)


def _ssd_fwd(cfg, xact, dt_raw, proj, dt_bias, a_log, d_skip, norm_w, e):
    s, si, cd, gw, bc = cfg.S, cfg.SI, cfg.CD, cfg.GW, cfg.BC
    nc = s // CHUNK
    zb = cfg.OZS // si
    tiles = gw // LANES

    def body(xa_ref, dtr_ref, z_ref, db_ref, al_ref, dsk_ref, nw_ref, e_ref, y_ref, y2_ref, st_ref,
             state, ybuf, x_s, xw_s, ae_s, ea_s, lam_s):
        @pl.when(pl.program_id(0) == 0)
        def _():
            state[...] = jnp.zeros_like(state)

        st_ref[...] = state[...]
        ev = e_ref[...]
        _, _, _, dt_e, a_e = _ssd_prep(dtr_ref, db_ref, al_ref, ev)
        xs = xa_ref[:, 0:si].astype(F32)
        x = xs * dt_e
        lam_e = a_e[CHUNK - 1:CHUNK, :]
        x_s[...] = x.astype(BF16)
        xw_s[...] = (x * jnp.exp(lam_e - a_e)).astype(BF16)
        ae_s[...] = a_e
        ea_s[...] = jnp.exp(a_e)
        ybuf[...] = _expand_row(dsk_ref[...], ev, 3) * xs
        lam_s[...] = jnp.broadcast_to(jnp.exp(lam_e), (8, si))
        tril = _tri(True)
        lane = lax.broadcasted_iota(jnp.int32, (CHUNK, LANES), 1)

        def group(g, carry):
            co = pl.multiple_of(g * gw, LANES)
            bg = xa_ref[:, pl.ds(pl.multiple_of(si + g * SSM_STATE, LANES), SSM_STATE)]
            cg = xa_ref[:, pl.ds(pl.multiple_of(si + bc + g * SSM_STATE, LANES), SSM_STATE)]
            cbm = _nt(cg, bg)
            st = state[:, pl.ds(co, gw)]
            yoff = _nn(cg, st.astype(BF16)) * ea_s[:, pl.ds(co, gw)]
            for k in range(tiles):
                tc = pl.multiple_of(co + k * LANES, LANES)
                at = ae_s[:, pl.ds(tc, LANES)]
                att = at.T
                xt = x_s[:, pl.ds(tc, LANES)]
                acc = yoff[:, k * LANES:(k + 1) * LANES]
                for half in range(2):
                    lo = half * SSM_HEAD_DIM
                    seg = at[:, lo:lo + 1] - att[lo:lo + 1, :]
                    dec = jnp.exp(jnp.where(tril, seg, NEG))
                    xh = jnp.where((lane >= lo) & (lane < lo + SSM_HEAD_DIM), xt, jnp.zeros_like(xt))
                    acc = acc + _nn((cbm * dec).astype(BF16), xh)
                ybuf[:, pl.ds(tc, LANES)] += acc
            state[:, pl.ds(co, gw)] = st * lam_s[0:1, pl.ds(co, gw)] + _tn(bg, xw_s[:, pl.ds(co, gw)])
            return carry

        lax.fori_loop(0, SSM_GROUPS, group, 0)
        y = ybuf[...]
        y_ref[...] = y.astype(BF16)
        z = z_ref[...].astype(F32)
        u = y * (z * _sigmoid(z))
        r = lax.rsqrt(jnp.mean(u * u, axis=-1, keepdims=True) + RMS_EPS)
        y2_ref[...] = (u * r * nw_ref[...]).astype(BF16)

    row = lambda n: pl.BlockSpec((1, n), lambda c: (0, 0))
    return pl.pallas_call(
        body,
        out_shape=(SDS((s, si), BF16), SDS((s, si), BF16), SDS((nc, SSM_STATE, si), F32)),
        grid=(nc,),
        in_specs=[pl.BlockSpec((CHUNK, cd), lambda c: (c, 0)),
                  pl.BlockSpec((CHUNK, LANES), lambda c: (c, 0)),
                  pl.BlockSpec((CHUNK, si), lambda c: (c, zb)),
                  row(LANES), row(LANES), row(LANES), row(si),
                  pl.BlockSpec((LANES, si), lambda c: (0, 0))],
        out_specs=(pl.BlockSpec((CHUNK, si), lambda c: (c, 0)),
                   pl.BlockSpec((CHUNK, si), lambda c: (c, 0)),
                   pl.BlockSpec((None, SSM_STATE, si), lambda c: (c, 0, 0))),
        scratch_shapes=[pltpu.VMEM((SSM_STATE, si), F32), pltpu.VMEM((CHUNK, si), F32),
                        pltpu.VMEM((CHUNK, si), BF16), pltpu.VMEM((CHUNK, si), BF16),
                        pltpu.VMEM((CHUNK, si), F32), pltpu.VMEM((CHUNK, si), F32),
                        pltpu.VMEM((8, si), F32)],
        compiler_params=_params(("arbitrary",)), name="ssd_fwd")(
            xact, dt_raw, proj, dt_bias, a_log, d_skip, norm_w, e)


def _ssd_bwd(cfg, xact, dt_raw, proj, y, dy2, states, dt_bias, a_log, d_skip, norm_w, e, dproj):
    s, si, cd, gw, bc, hpg = cfg.S, cfg.SI, cfg.CD, cfg.GW, cfg.BC, cfg.HPG
    nc = s // CHUNK
    zb = cfg.OZS // si
    tiles = gw // LANES

    def body(xa_ref, dtr_ref, z_ref, y_ref, d2_ref, st_ref, db_ref, al_ref, dsk_ref, nw_ref, e_ref, dp_in,
             dz_ref, dxa_ref, ddt_ref, gnw_ref, gdb_ref, gal_ref, gds_ref,
             dh, dhn, xs_s, x_s, w_s, ae_s, ea_s, g_s, dx_s, dae_s, r_s, lam_s, dle_s):
        del dp_in

        @pl.when(pl.program_id(0) == 0)
        def _():
            dh[...] = jnp.zeros_like(dh)
            gnw_ref[...] = jnp.zeros_like(gnw_ref)
            gdb_ref[...] = jnp.zeros_like(gdb_ref)
            gal_ref[...] = jnp.zeros_like(gal_ref)
            gds_ref[...] = jnp.zeros_like(gds_ref)

        ev = e_ref[...]
        yv = y_ref[...].astype(F32)
        z = z_ref[...].astype(F32)
        sg = _sigmoid(z)
        sz = z * sg
        u = yv * sz
        r = lax.rsqrt(jnp.mean(u * u, axis=-1, keepdims=True) + RMS_EPS)
        nrm = u * r
        d2 = d2_ref[...].astype(F32)
        gnw_ref[...] += jnp.sum(d2 * nrm, axis=0, keepdims=True)
        gn = d2 * nw_ref[...]
        du = r * (gn - nrm * jnp.mean(gn * nrm, axis=-1, keepdims=True))
        gv = du * sz
        dz_ref[...] = (du * yv * (sg * (1.0 + z * (1.0 - sg)))).astype(BF16)
        g_s[...] = gv

        dtr, dt, a, dt_e, a_e = _ssd_prep(dtr_ref, db_ref, al_ref, ev)
        xs = xa_ref[:, 0:si].astype(F32)
        x = xs * dt_e
        lam_e = a_e[CHUNK - 1:CHUNK, :]
        xs_s[...] = xs
        x_s[...] = x
        w_s[...] = jnp.exp(lam_e - a_e)
        ae_s[...] = a_e
        ea_s[...] = jnp.exp(a_e)
        lam_s[...] = jnp.broadcast_to(jnp.exp(lam_e), (8, si))
        gds_ref[...] += _segsum_row(jnp.sum(gv * xs, axis=0, keepdims=True), ev, 2)
        r_s[...] = jnp.zeros_like(r_s)
        tril = _tri(True)
        lane = lax.broadcasted_iota(jnp.int32, (CHUNK, LANES), 1)
        sub = lax.broadcasted_iota(jnp.int32, (CHUNK, LANES), 0)

        def group(g, carry):
            co = pl.multiple_of(g * gw, LANES)
            bo = pl.multiple_of(si + g * SSM_STATE, LANES)
            cof = pl.multiple_of(si + bc + g * SSM_STATE, LANES)
            cols = pl.ds(co, gw)
            bg = xa_ref[:, pl.ds(bo, SSM_STATE)]
            cg = xa_ref[:, pl.ds(cof, SSM_STATE)]
            cbm = _nt(cg, bg)
            st = st_ref[:, cols]
            stb = st.astype(BF16)
            dho = dh[:, cols]
            dhob = dho.astype(BF16)
            ea = ea_s[:, cols]
            gg = g_s[:, cols]
            xg = x_s[:, cols]
            wg = w_s[:, cols]
            explam = lam_s[0:1, cols]
            yoff = _nn(cg, stb) * ea
            ga = (gg * ea).astype(BF16)
            dc = _nt(ga, stb)
            dhn[:, cols] = dho * explam + _tn(cg, ga)
            bdh = _nn(bg, dhob)
            db = _nt((xg * wg).astype(BF16), dhob)
            t = xg * bdh * wg
            dle_s[0:1, cols] = jnp.sum(t, axis=0, keepdims=True) + explam * jnp.sum(dho * st, axis=0, keepdims=True)
            dae_base = gg * yoff - t
            dxw = wg * bdh
            dcb = jnp.zeros((CHUNK, CHUNK), F32)
            for k in range(tiles):
                tc = pl.multiple_of(co + k * LANES, LANES)
                ksl = slice(k * LANES, (k + 1) * LANES)
                at = ae_s[:, pl.ds(tc, LANES)]
                att = at.T
                xt = xg[:, ksl].astype(BF16)
                gt = gg[:, ksl].astype(BF16)
                dxt = dxw[:, ksl]
                place = jnp.zeros((CHUNK, LANES), F32)
                for half in range(2):
                    lo = half * SSM_HEAD_DIM
                    seg = at[:, lo:lo + 1] - att[lo:lo + 1, :]
                    dec = jnp.exp(jnp.where(tril, seg, NEG))
                    mh = cbm * dec
                    gh = jnp.where((lane >= lo) & (lane < lo + SSM_HEAD_DIM), gt, jnp.zeros_like(gt))
                    dm = _nt(gh, xt)
                    dxt = dxt + _tn(mh.astype(BF16), gh)
                    dcb = dcb + dm * dec
                    dseg = dm * mh
                    place = place + jnp.where(lane == lo, jnp.sum(dseg, axis=1, keepdims=True), 0.0)
                    hidx = g * hpg + 2 * k + half
                    r_s[...] += jnp.where(sub == hidx, jnp.sum(dseg, axis=0, keepdims=True), 0.0)
                dx_s[:, pl.ds(tc, LANES)] = dxt
                dae_s[:, pl.ds(tc, LANES)] = dae_base[:, ksl] + place
            dcbb = dcb.astype(BF16)
            dxa_ref[:, pl.ds(bo, SSM_STATE)] = (db + _tn(dcbb, cg)).astype(BF16)
            dxa_ref[:, pl.ds(cof, SSM_STATE)] = (dc + _nn(dcbb, bg)).astype(BF16)
            return carry

        lax.fori_loop(0, SSM_GROUPS, group, 0)
        dlam = _segsum_row(dle_s[0:1, :], ev, 2)
        da_ = _segsum(dae_s[...], ev, 2) - r_s[...].T
        da_ = da_ + jnp.where(sub == CHUNK - 1, dlam, 0.0)
        dda = jnp.dot(_tri(False).astype(F32), da_, precision=lax.Precision.HIGHEST, preferred_element_type=F32)
        dxv = dx_s[...]
        xs = xs_s[...]
        ddt = dda * a + _segsum(dxv * xs, ev, 2)
        gal_ref[...] += jnp.sum(dda * dt, axis=0, keepdims=True) * a
        ddtr = ddt * _sigmoid(dtr)
        gdb_ref[...] += jnp.sum(ddtr, axis=0, keepdims=True)
        ddt_ref[...] = ddtr
        dxa_ref[:, 0:si] = (dxv * dt_e + g_s[...] * _expand_row(dsk_ref[...], ev, 3)).astype(BF16)
        dh[...] = dhn[...]

    rev = lambda c: nc - 1 - c
    row = lambda n: pl.BlockSpec((1, n), lambda c: (0, 0))
    big = lambda: pltpu.VMEM((CHUNK, si), F32)
    return pl.pallas_call(
        body,
        out_shape=(SDS(dproj.shape, BF16), SDS((s, cd), BF16), SDS((s, LANES), F32),
                   SDS((1, si), F32), SDS((1, LANES), F32), SDS((1, LANES), F32), SDS((1, LANES), F32)),
        grid=(nc,),
        in_specs=[pl.BlockSpec((CHUNK, cd), lambda c: (rev(c), 0)),
                  pl.BlockSpec((CHUNK, LANES), lambda c: (rev(c), 0)),
                  pl.BlockSpec((CHUNK, si), lambda c: (rev(c), zb)),
                  pl.BlockSpec((CHUNK, si), lambda c: (rev(c), 0)),
                  pl.BlockSpec((CHUNK, si), lambda c: (rev(c), 0)),
                  pl.BlockSpec((None, SSM_STATE, si), lambda c: (rev(c), 0, 0)),
                  row(LANES), row(LANES), row(LANES), row(si),
                  pl.BlockSpec((LANES, si), lambda c: (0, 0)),
                  pl.BlockSpec(memory_space=pl.ANY)],
        out_specs=(pl.BlockSpec((CHUNK, si), lambda c: (rev(c), zb)),
                   pl.BlockSpec((CHUNK, cd), lambda c: (rev(c), 0)),
                   pl.BlockSpec((CHUNK, LANES), lambda c: (rev(c), 0)),
                   row(si), row(LANES), row(LANES), row(LANES)),
        scratch_shapes=[pltpu.VMEM((SSM_STATE, si), F32), pltpu.VMEM((SSM_STATE, si), F32),
                        big(), big(), big(), big(), big(), big(), big(), big(),
                        pltpu.VMEM((CHUNK, LANES), F32), pltpu.VMEM((8, si), F32), pltpu.VMEM((8, si), F32)],
        input_output_aliases={11: 0},
        compiler_params=_params(("arbitrary",)), name="ssd_bwd")(
            xact, dt_raw, proj, y, dy2, states, dt_bias, a_log, d_skip, norm_w, e, dproj)


MERGE_TR = 512
MERGE_CW = 2048


def _merge_fwd(cfg, proj, a_br, s_br):
    s, d = cfg.S, cfg.D
    tr, cw = MERGE_TR, min(MERGE_CW, d)
    ga0, gs0 = cfg.OGA // cw, cfg.OGS // cw

    def body(ga_ref, gs_ref, a_ref, s_ref, o_ref):
        o_ref[...] = (_sigmoid(ga_ref[...].astype(F32)) * a_ref[...].astype(F32)
                      + _sigmoid(gs_ref[...].astype(F32)) * s_ref[...].astype(F32)).astype(BF16)

    blk = pl.BlockSpec((tr, cw), lambda i, j: (i, j))
    return pl.pallas_call(
        body, out_shape=SDS((s, d), BF16), grid=(s // tr, d // cw),
        in_specs=[pl.BlockSpec((tr, cw), lambda i, j: (i, ga0 + j)),
                  pl.BlockSpec((tr, cw), lambda i, j: (i, gs0 + j)), blk, blk],
        out_specs=blk, compiler_params=_params(("parallel", "parallel")), name="merge_fwd")(proj, proj, a_br, s_br)


def _merge_bwd(cfg, proj, branch, dmerged, gate_off, dproj, name):
    s, d = cfg.S, cfg.D
    tr, cw = MERGE_TR, min(MERGE_CW, d)
    g0 = gate_off // cw
    fresh = dproj is None

    def body(*refs):
        g_ref, b_ref, dm_ref = refs[:3]
        dg_ref, db_ref = refs[-2:]
        dm = dm_ref[...].astype(F32)
        sg = _sigmoid(g_ref[...].astype(F32))
        db_ref[...] = (dm * sg).astype(BF16)
        dg_ref[...] = (dm * b_ref[...].astype(F32) * sg * (1.0 - sg)).astype(BF16)

    blk = pl.BlockSpec((tr, cw), lambda i, j: (i, j))
    gate = pl.BlockSpec((tr, cw), lambda i, j: (i, g0 + j))
    return pl.pallas_call(
        body, out_shape=(SDS((s, cfg.NM), BF16), SDS((s, d), BF16)), grid=(s // tr, d // cw),
        in_specs=[gate, blk, blk] + ([] if fresh else [HBM_SPEC]),
        out_specs=(gate, blk),
        input_output_aliases={} if fresh else {3: 0},
        compiler_params=_params(("parallel", "parallel")), name=name)(
            *((proj, branch, dmerged) + (() if fresh else (dproj,))))


def _outproj_loss(merged, w_out, x, target, fnw):
    s, d = x.shape
    tr = 256

    def body(m_ref, w_ref, x_ref, t_ref, fw_ref, dof_ref, dob_ref, loss_ref, g_ref):
        out = x_ref[...] + _nn(m_ref[...], w_ref[...])
        r = lax.rsqrt(jnp.mean(out * out, axis=-1, keepdims=True) + RMS_EPS)
        nrm = out * r
        fw = fw_ref[...]
        err = nrm * fw - t_ref[...]
        dy = err * (1.0 / d)
        gy = dy * fw
        dout = r * (gy - nrm * jnp.mean(gy * nrm, axis=-1, keepdims=True))
        dof_ref[...] = dout
        dob_ref[...] = dout.astype(BF16)

        @pl.when(pl.program_id(0) == 0)
        def _():
            loss_ref[...] = jnp.zeros_like(loss_ref)
            g_ref[...] = jnp.zeros_like(g_ref)

        loss_ref[...] += jnp.sum(jnp.sum(err * err, axis=1, keepdims=True), axis=0, keepdims=True) * (0.5 / d)
        g_ref[...] += jnp.sum(dy * nrm, axis=0, keepdims=True)

    blk = pl.BlockSpec((tr, d), lambda i: (i, 0))
    return pl.pallas_call(
        body, out_shape=(SDS((s, d), F32), SDS((s, d), BF16), SDS((1, LANES), F32), SDS((1, d), F32)), grid=(s // tr,),
        in_specs=[blk, pl.BlockSpec((d, d), lambda i: (0, 0)), blk, blk, pl.BlockSpec((1, d), lambda i: (0, 0))],
        out_specs=(blk, blk, pl.BlockSpec((1, LANES), lambda i: (0, 0)), pl.BlockSpec((1, d), lambda i: (0, 0))),
        compiler_params=_params(("arbitrary",)), name="outproj_loss")(merged, w_out, x, target, fnw)


ELEMWISE_BLOCK_BYTES = 1 << 20


def _row_block(rows, cols, itemsize=4):
    best = None
    for tr in range(16, rows + 1, 16):
        if rows % tr == 0 and tr * cols * itemsize <= ELEMWISE_BLOCK_BYTES:
            best = tr
    return best if best is not None else rows


def _adamw(w, g, m, v, name):
    rows, cols = w.shape
    tr = _row_block(rows, cols)
    if rows // tr > 64 and cols % LANES == 0:
        blk, grid = pl.BlockSpec((rows, LANES), lambda i: (0, i)), (cols // LANES,)
    else:
        blk, grid = pl.BlockSpec((tr, cols), lambda i: (i, 0)), (rows // tr,)
    out = SDS((rows, cols), F32)
    return pl.pallas_call(
        _adamw_body(), out_shape=(out, out, out), grid=grid, in_specs=[blk] * 4, out_specs=(blk,) * 3,
        compiler_params=_params(("parallel",)), name=name)(w, g, m, v)


def _adamw_body():
    def body(w_ref, g_ref, m_ref, v_ref, d_ref, nm_ref, nv_ref):
        gv = g_ref[...]
        nm = ADAM_B1 * m_ref[...] + (1.0 - ADAM_B1) * gv
        nv = ADAM_B2 * v_ref[...] + (1.0 - ADAM_B2) * jnp.square(gv)
        m_hat = nm / (1.0 - ADAM_B1 ** ADAM_STEP)
        v_hat = nv / (1.0 - ADAM_B2 ** ADAM_STEP)
        d_ref[...] = -ADAM_LR * (m_hat / (jnp.sqrt(v_hat) + ADAM_EPS) + ADAM_WD * w_ref[...])
        nm_ref[...] = nm
        nv_ref[...] = nv

    return body


HBM_SPEC = pl.BlockSpec(memory_space=pl.ANY)


def _position():
    return lax.axis_index("x"), lax.axis_index("y"), lax.axis_index("c")


class _Carry:
    def __init__(self, arrays, out_shapes, sems, start, finish):
        self.arrays, self.out_shapes, self.sems, self.start, self.finish = list(arrays), out_shapes, sems, start, finish

    def sem_shapes(self):
        return [pltpu.SemaphoreType.DMA((k,)) for k in self.sems]


def _gather_carry(shards, by_cols=()):
    n = len(shards)

    def copies(ins, outs, sems):
        send_sems, recv_sems, fsend_sems, frecv_sems = sems
        x, y, c = _position()
        me = 2 * x + y
        peers = [(1 - x, y), (x, 1 - y), (1 - x, 1 - y)]

        def half_of(t, chip, half):
            if t in by_cols:
                c2 = ins[t].shape[1] // 2
                return outs[t].at[chip, :, pl.ds(half * c2, c2)]
            return outs[t].at[chip, half]

        def over_ici(t, p, chip):
            px, py = peers[p]
            if t in by_cols:
                c2 = ins[t].shape[1] // 2
                src = ins[t].at[:, pl.ds(c * c2, c2)]
            else:
                r2 = ins[t].shape[0] // 2
                src = ins[t].at[pl.ds(c * r2, r2), :]
            return pltpu.make_async_remote_copy(
                src_ref=src, dst_ref=half_of(t, chip, c), send_sem=send_sems.at[3 * t + p],
                recv_sem=recv_sems.at[3 * t + p], device_id=(px, py, c), device_id_type=MESH)

        def to_sibling(t, p, half):
            px, py = peers[p]
            slab = half_of(t, 2 * px + py, half)
            return pltpu.make_async_remote_copy(
                src_ref=slab, dst_ref=slab, send_sem=fsend_sems.at[3 * t + p], recv_sem=frecv_sems.at[3 * t + p],
                device_id=(x, y, 1 - c), device_id_type=MESH)

        pairs = [(t, p) for t in range(n) for p in range(3)]
        sends = [over_ici(t, p, me) for t, p in pairs]
        lands = [over_ici(t, p, 2 * peers[p][0] + peers[p][1]) for t, p in pairs]
        passed = [to_sibling(t, p, c) for t, p in pairs]
        from_sibling = [to_sibling(t, p, 1 - c) for t, p in pairs]
        return sends, lands, passed, from_sibling

    def start(ins, outs, sems):
        for cp in copies(ins, outs, sems)[0]:
            cp.start()

    def finish(ins, outs, sems):
        sends, lands, passed, from_sibling = copies(ins, outs, sems)
        for land, fwd in zip(lands, passed):
            land.wait_recv()
            fwd.start()
        for cp in from_sibling:
            cp.wait_recv()
        for cp in sends + passed:
            cp.wait_send()

    shapes = [SDS((N_CHIPS,) + a.shape if t in by_cols else (N_CHIPS, 2, a.shape[0] // 2, a.shape[1]), a.dtype)
              for t, a in enumerate(shards)]
    return _Carry(shards, shapes, [3 * n] * 4, start, finish)


def _scatter_carry(parts):
    def start(ins, outs, sems):
        for cp in _scatter_copies(ins, outs, *sems)[0]:
            cp.start()

    def finish(ins, outs, sems):
        sends, lands = _scatter_copies(ins, outs, *sems)
        for cp in lands:
            cp.wait_recv()
        for cp in sends:
            cp.wait_send()

    return _Carry(parts, [SDS(a.shape, a.dtype) for a in parts], [3 * len(parts)] * 2, start, finish)


def _with_own(gathered, own, chip):
    full = gathered.reshape((N_CHIPS,) + own.shape)
    return lax.dynamic_update_index_in_dim(full, own, chip, 0)


def _exchange_halves(grads):
    n = len(grads)
    slabs = [list(g) if isinstance(g, (list, tuple)) else [g] for g in grads]
    flat = [a for s in slabs for a in s]
    ncp = len(flat)

    def body(*refs):
        ins, outs = refs[:ncp], refs[ncp:ncp + n]
        send_sems, recv_sems = refs[ncp + n:]
        x, y, c = _position()
        cps, k = [], 0
        for t in range(n):
            for j in range(len(slabs[t])):
                if len(slabs[t]) == 1:
                    r2 = ins[k].shape[1] // 2
                    src, dst = ins[k].at[:, pl.ds((1 - c) * r2, r2), :], outs[t]
                else:
                    r2 = ins[k].shape[0] // 2
                    src, dst = ins[k].at[pl.ds((1 - c) * r2, r2), :], outs[t].at[j]
                cps.append(pltpu.make_async_remote_copy(
                    src_ref=src, dst_ref=dst, send_sem=send_sems.at[k], recv_sem=recv_sems.at[k],
                    device_id=(x, y, 1 - c), device_id_type=MESH))
                k += 1
        for cp in cps:
            cp.start()
        for cp in cps:
            cp.wait()

    def landing(s):
        a = s[0]
        return SDS((N_CHIPS, a.shape[-2] // 2, a.shape[-1]), a.dtype)

    return pl.pallas_call(
        body, out_shape=[landing(s) for s in slabs],
        in_specs=[HBM_SPEC] * ncp, out_specs=[HBM_SPEC] * n,
        scratch_shapes=[pltpu.SemaphoreType.DMA((ncp,)), pltpu.SemaphoreType.DMA((ncp,))],
        compiler_params=pltpu.CompilerParams(has_side_effects=True), name="reduce_sibling")(*flat)


def _scatter_copies(ins, outs, send_sems, recv_sems):
    x, y, c = _position()
    me = 2 * x + y
    peers = [(1 - x, y), (x, 1 - y), (1 - x, 1 - y)]

    def remote(t, p, src_slab, dst_slab):
        px, py = peers[p]
        return pltpu.make_async_remote_copy(
            src_ref=ins[t].at[src_slab], dst_ref=outs[t].at[dst_slab], send_sem=send_sems.at[3 * t + p],
            recv_sem=recv_sems.at[3 * t + p], device_id=(px, py, c), device_id_type=MESH)

    n = len(ins)
    sends = [remote(t, p, 2 * peers[p][0] + peers[p][1], me) for t in range(n) for p in range(3)]
    lands = [remote(t, p, me, 2 * peers[p][0] + peers[p][1]) for t in range(n) for p in range(3)]
    return sends, lands


def _share_halves(halves):
    n = len(halves)

    def body(*refs):
        ins, outs = refs[:n], refs[n:2 * n]
        send_sems, recv_sems = refs[2 * n:]
        x, y, c = _position()

        def copy(t, slab):
            return pltpu.make_async_remote_copy(
                src_ref=ins[t].at[slab], dst_ref=outs[t].at[slab], send_sem=send_sems.at[t], recv_sem=recv_sems.at[t],
                device_id=(x, y, 1 - c), device_id_type=MESH)

        for t in range(n):
            copy(t, c).start()
        for t in range(n):
            copy(t, 1 - c).wait_recv()
        for t in range(n):
            copy(t, c).wait_send()

    return pl.pallas_call(
        body, out_shape=[SDS(a.shape, a.dtype) for a in halves],
        in_specs=[HBM_SPEC] * n, out_specs=[HBM_SPEC] * n,
        scratch_shapes=[pltpu.SemaphoreType.DMA((n,)), pltpu.SemaphoreType.DMA((n,))],
        input_output_aliases={t: t for t in range(n)},
        compiler_params=pltpu.CompilerParams(has_side_effects=True), name="share_sibling")(*halves)


def _add_sibling(grad, recv, core):
    nch, r2, cols = recv.shape
    tr = _row_block(r2, cols)
    nb = r2 // tr

    def body(c_ref, g_ref, r_ref, o_ref):
        del c_ref
        o_ref[...] = (g_ref[...].astype(F32) + r_ref[...].astype(F32)).astype(BF16)

    return pl.pallas_call(
        body, out_shape=SDS(recv.shape, BF16),
        grid_spec=pltpu.PrefetchScalarGridSpec(
            num_scalar_prefetch=1, grid=(nch, nb),
            in_specs=[pl.BlockSpec((None, tr, cols), lambda j, i, c_ref: (j, c_ref[0] * nb + i, 0)),
                      pl.BlockSpec((None, tr, cols), lambda j, i, c_ref: (j, i, 0))],
            out_specs=pl.BlockSpec((None, tr, cols), lambda j, i, c_ref: (j, i, 0))),
        compiler_params=_params(("parallel", "parallel")), name="add_sibling")(core, grad, recv)


def _add_chips(own, recv, chip_core):
    nch, r2, cols = recv.shape
    tr = _row_block(r2, cols)

    nsc = 2 + nch

    def body(*refs):
        me = refs[0][0]
        own_ref, p_refs, o_ref = refs[nsc], refs[nsc + 1:nsc + 1 + nch], refs[nsc + 1 + nch]
        acc = None
        for j in range(nch):
            term = jnp.where(me == j, own_ref[...], p_refs[j][...]).astype(F32)
            acc = term if acc is None else acc + term
        o_ref[...] = acc

    def slab(j):
        return pl.BlockSpec((None, tr, cols), lambda i, *sc: (sc[2 + j][0], i, 0))

    return pl.pallas_call(
        body, out_shape=SDS((2, r2, cols), F32),
        grid_spec=pltpu.PrefetchScalarGridSpec(
            num_scalar_prefetch=nsc, grid=(r2 // tr,),
            in_specs=[pl.BlockSpec((None, tr, cols), lambda i, *sc: (sc[0][0], i, 0))] + [slab(j) for j in range(nch)],
            out_specs=pl.BlockSpec((None, tr, cols), lambda i, *sc: (sc[1][0], i, 0))),
        compiler_params=_params(("parallel",)), name="add_chips")(*chip_core, own, *([recv] * nch))


def _exchange_col_halves(grad):
    nch, r, cols = grad.shape
    c2 = cols // 2

    def body(in_ref, out_ref, send_sem, recv_sem):
        x, y, c = _position()
        cp = pltpu.make_async_remote_copy(
            src_ref=in_ref.at[:, :, pl.ds((1 - c) * c2, c2)], dst_ref=out_ref, send_sem=send_sem.at[0],
            recv_sem=recv_sem.at[0], device_id=(x, y, 1 - c), device_id_type=MESH)
        cp.start()
        cp.wait()

    return pl.pallas_call(
        body, out_shape=SDS((nch, r, c2), grad.dtype), in_specs=[HBM_SPEC], out_specs=HBM_SPEC,
        scratch_shapes=[pltpu.SemaphoreType.DMA((1,)), pltpu.SemaphoreType.DMA((1,))],
        compiler_params=pltpu.CompilerParams(has_side_effects=True), name="reduce_sibling_cols")(grad)


def _add_sibling_cols(grad, recv, core):
    nch, r, c2 = recv.shape
    nb = c2 // LANES

    def body(c_ref, g_ref, r_ref, o_ref):
        del c_ref
        o_ref[...] = (g_ref[...].astype(F32) + r_ref[...].astype(F32)).astype(BF16)

    blk = pl.BlockSpec((None, r, LANES), lambda j, i, c_ref: (j, 0, i))
    return pl.pallas_call(
        body, out_shape=SDS(recv.shape, BF16),
        grid_spec=pltpu.PrefetchScalarGridSpec(
            num_scalar_prefetch=1, grid=(nch, nb),
            in_specs=[pl.BlockSpec((None, r, LANES), lambda j, i, c_ref: (j, 0, c_ref[0] * nb + i)), blk],
            out_specs=blk),
        compiler_params=_params(("parallel", "parallel")), name="add_sibling_cols")(core, grad, recv)


def _add_chips_cols(own, recv, chip_core):
    nch, r, c2 = recv.shape
    nb = c2 // LANES
    nsc = 2 + nch

    def body(*refs):
        me = refs[0][0]
        own_ref, p_refs, o_ref = refs[nsc], refs[nsc + 1:nsc + 1 + nch], refs[nsc + 1 + nch]
        acc = None
        for j in range(nch):
            term = jnp.where(me == j, own_ref[...], p_refs[j][...]).astype(F32)
            acc = term if acc is None else acc + term
        o_ref[...] = acc

    def slab(j):
        return pl.BlockSpec((None, r, LANES), lambda i, *sc: (sc[2 + j][0], 0, i))

    return pl.pallas_call(
        body, out_shape=SDS((r, 2 * c2), F32),
        grid_spec=pltpu.PrefetchScalarGridSpec(
            num_scalar_prefetch=nsc, grid=(nb,),
            in_specs=[pl.BlockSpec((None, r, LANES), lambda i, *sc: (sc[0][0], 0, i))] + [slab(j) for j in range(nch)],
            out_specs=pl.BlockSpec((r, LANES), lambda i, *sc: (0, sc[1][0] * nb + i))),
        compiler_params=_params(("parallel",)), name="add_chips_cols")(*chip_core, own, *([recv] * nch))


def _share_col_halves(full):
    r, cols = full.shape
    c2 = cols // 2

    def body(in_ref, out_ref, send_sem, recv_sem):
        x, y, c = _position()

        def copy(half):
            return pltpu.make_async_remote_copy(
                src_ref=in_ref.at[:, pl.ds(half * c2, c2)], dst_ref=out_ref.at[:, pl.ds(half * c2, c2)],
                send_sem=send_sem.at[0], recv_sem=recv_sem.at[0], device_id=(x, y, 1 - c), device_id_type=MESH)

        copy(c).start()
        copy(1 - c).wait_recv()
        copy(c).wait_send()

    return pl.pallas_call(
        body, out_shape=SDS(full.shape, full.dtype), in_specs=[HBM_SPEC], out_specs=HBM_SPEC,
        scratch_shapes=[pltpu.SemaphoreType.DMA((1,)), pltpu.SemaphoreType.DMA((1,))],
        input_output_aliases={0: 0},
        compiler_params=pltpu.CompilerParams(has_side_effects=True), name="share_sibling_cols")(full)


def _allreduce_small(pack):
    rows = pack.shape[0]

    def body(p_ref, o_ref, buf, send_sems, recv_sems):
        x, y, c = _position()
        me = 4 * x + 2 * y + c
        buf[me] = p_ref[...]

        def copy(dst_dev, slot):
            return pltpu.make_async_remote_copy(
                src_ref=p_ref, dst_ref=buf.at[slot], send_sem=send_sems.at[dst_dev], recv_sem=recv_sems.at[slot],
                device_id=(dst_dev // 4, (dst_dev // 2) % 2, dst_dev % 2), device_id_type=MESH)

        for dev in range(N_DEV):
            @pl.when(dev != me)
            def _():
                copy(dev, me).start()
        for dev in range(N_DEV):
            @pl.when(dev != me)
            def _():
                copy(dev, dev).wait_recv()
        for dev in range(N_DEV):
            @pl.when(dev != me)
            def _():
                copy(dev, me).wait_send()
        acc = buf[0]
        for dev in range(1, N_DEV):
            acc = acc + buf[dev]
        o_ref[...] = acc

    return pl.pallas_call(
        body, out_shape=SDS(pack.shape, F32),
        in_specs=[pl.BlockSpec(memory_space=pltpu.VMEM)], out_specs=pl.BlockSpec(memory_space=pltpu.VMEM),
        scratch_shapes=[pltpu.VMEM((N_DEV, rows, LANES), F32), pltpu.SemaphoreType.DMA((N_DEV,)),
                        pltpu.SemaphoreType.DMA((N_DEV,))],
        compiler_params=pltpu.CompilerParams(has_side_effects=True), name="allreduce_small")(pack)


ATTN_TQ = 256


def _local_step(cfg, x, target, w, to_chips=None, late=None, hn=None):
    d = cfg.D
    if hn is None:
        hn = _rmsnorm_fwd(x, w["norm_w"])
    proj = _mm(hn, w["w_main_t"], "nt", BF16, "proj_main", carry=late[0] if late else None, b_rows=cfg.NM)
    if late:
        proj, arrived = proj
        w = {**w, **late[1](arrived)}
    dt_raw = _mm(hn, w["w_dt_t"], "nt", F32, "proj_dt")
    slopes = _slopes(cfg.H)
    near = _Pass(ATTN_TQ, DILATED_PATTERNS[:-1], 1, cfg.S)
    far = _Pass(LANES, DILATED_PATTERNS[-1:], DEINT, cfg.S // DEINT)
    tab_near, tab_far = _attn_tables(near), _attn_tables(far)
    cols_near, cols_far = (cfg.OQ, cfg.OK, cfg.OV), (0, d, 2 * d)
    qkv_far = _deinterleave(proj, 0, 3 * d, "attn_deinterleave")
    o_1, lse_1 = _attn_fwd(cfg, near, proj, cols_near, tab_near, slopes, "attn_fwd_near")
    o_2, lse_2 = _attn_fwd(cfg, far, qkv_far, cols_far, tab_far, slopes, "attn_fwd_far")
    o_a, oag, lse = _attn_merge(cfg, proj, o_1, lse_1, o_2, lse_2)
    xact = _conv_fwd(cfg, proj, w["conv_w"], w["conv_b"])
    e = _expansion_matrix(cfg)
    y, y2, states = _ssd_fwd(cfg, xact, dt_raw, proj, w["dt_bias"], w["a_log"], w["d_skip"], w["ssm_norm_w"], e)
    a_br = _mm(oag, w["w_attn"], "nn", BF16, "branch_attn")
    s_br = _mm(y2, w["w_ssm"], "nn", BF16, "branch_ssm")
    merged = _merge_fwd(cfg, proj, a_br, s_br)
    dout_f, dout_b, loss_row, g_fnw = _outproj_loss(merged, w["w_out"], x, target, w["final_norm_w"])

    dmerged = _mm(dout_b, w["w_out"], "nt", BF16, "d_merged")
    g_w_out = _mm(merged, dout_b, "tn", BF16, "g_w_out")
    dproj, da_br = _merge_bwd(cfg, proj, a_br, dmerged, cfg.OGA, None, "merge_bwd_attn")
    dproj, ds_br = _merge_bwd(cfg, proj, s_br, dmerged, cfg.OGS, dproj, "merge_bwd_ssm")
    doag = _mm(da_br, w["w_attn"], "nt", BF16, "d_oag")
    g_w_attn = _mm(oag, da_br, "tn", BF16, "g_w_attn")
    dy2 = _mm(ds_br, w["w_ssm"], "nt", BF16, "d_y2")
    g_w_ssm = _mm(y2, ds_br, "tn", BF16, "g_w_ssm")
    dproj, dxact, ddt, g_snw, g_dtb, g_alog, g_dsk = _ssd_bwd(
        cfg, xact, dt_raw, proj, y, dy2, states, w["dt_bias"], w["a_log"], w["d_skip"], w["ssm_norm_w"], e, dproj)
    dproj, g_cw, g_cb = _conv_bwd(cfg, proj, dxact, w["conv_w"], w["conv_b"], dproj)
    dproj, do, do_far, dl, dl_far, lse_far = _attn_bwd_prep(cfg, proj, o_a, doag, lse, dproj)
    g_near = _attn_bwd(cfg, near, proj, cols_near, do, lse, dl, tab_near, slopes, "attn_bwd_near")
    g_far = _attn_bwd(cfg, far, qkv_far, cols_far, do_far, lse_far, dl_far, tab_far, slopes, "attn_bwd_far")
    for g_1, g_2, col0, nm in zip(g_near, g_far, cols_near, ("attn_dq", "attn_dk", "attn_dv")):
        dproj = _attn_grad_sum(cfg, g_1, g_2, col0, dproj, nm)
    ddt_b = ddt.astype(BF16)
    g_w_main = _mm(dproj, hn, "tn", BF16, "g_w_main", out_rows=cfg.N_IN)
    g_w_dt = _mm(ddt_b, hn, "tn", BF16, "g_w_dt")
    grads = dict(w_main_t=g_w_main, w_dt_t=g_w_dt, conv_w=g_cw, conv_b=g_cb, dt_bias=g_dtb, a_log=g_alog,
                 d_skip=g_dsk, ssm_norm_w=g_snw, w_attn=g_w_attn, w_ssm=g_w_ssm, w_out=g_w_out, final_norm_w=g_fnw)
    sent = to_chips(grads) if to_chips is not None else ()
    dhn = _mm(dproj, w["w_main_t"], "nn", F32, "d_hn", tk=1024, carry=_scatter_carry(sent) if sent else None,
              b_rows=cfg.NM)
    landed = ()
    if sent:
        dhn, landed = dhn
    dhn_dt = _mm(ddt_b, w["w_dt_t"], "nn", F32, "d_hn_dt")
    grad_x, grads["norm_w"] = _rmsnorm_bwd(x, w["norm_w"], dhn, dhn_dt, dout_f)
    return loss_row, grad_x, grads, sent, landed


def _pad_lanes(v):
    return jnp.pad(v, ((0, 0), (0, LANES - v.shape[1])))


def _main_from_rows(cfg, w_in_t):
    lo, hi = cfg.OGA, cfg.OGA + cfg.NH
    dt = jnp.pad(w_in_t[lo:hi], ((0, LANES - cfg.NH), (0, 0)))
    return lax.dynamic_update_slice(w_in_t, w_in_t[hi:], (lo, 0)), dt


def _rows_from_main(cfg, g_main_t, g_dt_t):
    lo, hi = cfg.OGA, cfg.OGA + cfg.NH
    g = lax.dynamic_update_slice(g_main_t, g_main_t[lo:cfg.NM], (hi, 0))
    return lax.dynamic_update_slice(g, g_dt_t[:cfg.NH], (lo, 0))


def _full_weights(cfg, norm_w, w_in_t, conv_w, conv_b, dt_bias, a_log, d_skip, ssm_norm_w, w_attn, w_ssm, w_out, fnw):
    w_main, w_dt = _main_from_rows(cfg, w_in_t)
    return dict(norm_w=norm_w, w_main_t=w_main.astype(BF16), w_dt_t=w_dt.astype(BF16), conv_w=conv_w, conv_b=conv_b,
                dt_bias=_pad_lanes(dt_bias), a_log=_pad_lanes(a_log), d_skip=_pad_lanes(d_skip), ssm_norm_w=ssm_norm_w,
                final_norm_w=fnw, **{k: v.astype(BF16) for k, v in (("w_attn", w_attn), ("w_ssm", w_ssm), ("w_out", w_out))
                                     if v is not None})


def kernel(x, norm_w, w_in, conv_w, conv_b, dt_bias, a_log, d_skip, ssm_norm_w, w_attn_branch, w_ssm_branch, w_out, final_norm_w, loss_target, m_norm_w, m_w_in, m_conv_w, m_conv_b, m_dt_bias, m_a_log, m_d_skip, m_ssm_norm_w, m_w_attn_branch, m_w_ssm_branch, m_w_out, m_final_norm_w, v_norm_w, v_w_in, v_conv_w, v_conv_b, v_dt_bias, v_a_log, v_d_skip, v_ssm_norm_w, v_w_attn_branch, v_w_ssm_branch, v_w_out, v_final_norm_w):
    cfg = _Cfg(x.shape[1], x.shape[2])
    d, si, cd, nh = cfg.D, cfg.SI, cfg.CD, cfg.NH
    chip = 2 * lax.axis_index("x") + lax.axis_index("y")
    core = lax.axis_index("c").astype(jnp.int32).reshape(1)
    chip = chip.astype(jnp.int32)
    chip_core = [chip.reshape(1), core] + [jnp.where(chip == j, (j + 1) % N_CHIPS, j).astype(jnp.int32).reshape(1)
                                           for j in range(N_CHIPS)]

    own = [jnp.transpose(w_in[0]).astype(BF16), conv_w[0].reshape(4 * CONV_K, -1)]
    hn, gathered = _rmsnorm_fwd(x[0], norm_w, carry=_gather_carry(own, by_cols=(0,)))
    a_in, a_cw = [_with_own(g, o, chip) for g, o in zip(gathered, own)]
    conv_w_full = a_cw.reshape(N_CHIPS, CONV_K, cd // N_CHIPS).transpose(1, 0, 2).reshape(CONV_K, cd)
    w = _full_weights(cfg, norm_w, a_in.reshape(cfg.N_IN, d), conv_w_full, conv_b, dt_bias, a_log, d_skip,
                      ssm_norm_w, None, None, None, final_norm_w.reshape(1, d))
    own_late = [w_attn_branch[0].astype(BF16), w_ssm_branch[0].astype(BF16), w_out[0].astype(BF16)]

    def late_weights(arrived):
        a_attn, a_ssm, a_out = [_with_own(g, o, chip) for g, o in zip(arrived, own_late)]
        return dict(w_attn=a_attn.reshape(d, d), w_ssm=a_ssm.reshape(si, d), w_out=a_out.reshape(d, d))

    def to_chips(grads):
        g_in_t = _rows_from_main(cfg, grads["w_main_t"], grads["w_dt_t"]).reshape(N_CHIPS, cfg.N_IN // N_CHIPS, d)
        by_chip = [grads["w_attn"].reshape(N_CHIPS, d // N_CHIPS, d),
                   grads["w_ssm"].reshape(N_CHIPS, si // N_CHIPS, d),
                   grads["w_out"].reshape(N_CHIPS, d // N_CHIPS, d)]
        from_sibling = _exchange_halves(by_chip)
        return ([_add_sibling_cols(g_in_t, _exchange_col_halves(g_in_t), core)]
                + [_add_sibling(g, r, core) for g, r in zip(by_chip, from_sibling)])

    loss_row, grad_x, grads, chip_sums, from_chips = _local_step(
        cfg, x[0], loss_target[0], w, to_chips, (_gather_carry(own_late), late_weights), hn)
    g_in_t = _share_col_halves(_add_chips_cols(chip_sums[0], from_chips[0], chip_core))
    halves = [_add_chips(o, p, chip_core) for o, p in zip(chip_sums[1:], from_chips[1:])]
    g_attn, g_ssm, g_out = [h.reshape(2 * h.shape[1], h.shape[2]) for h in _share_halves(halves)]
    g_in = jnp.transpose(g_in_t)

    small = [loss_row, grads["norm_w"], grads["conv_b"], grads["dt_bias"], grads["a_log"], grads["d_skip"],
             grads["ssm_norm_w"], grads["final_norm_w"], grads["conv_w"].reshape(1, CONV_K * cd)]
    sizes = [a.shape[1] for a in small]
    total = sum(sizes)
    rows = -(-total // (8 * LANES)) * 8
    flat = jnp.pad(jnp.concatenate(small, axis=1), ((0, 0), (0, rows * LANES - total)))
    red = _allreduce_small(flat.reshape(rows, LANES)).reshape(1, rows * LANES)
    offs = [sum(sizes[:i]) for i in range(len(sizes))]
    loss_r, g_nw, g_cb, g_dtb, g_alog, g_dsk, g_snw, g_fnw, g_cw_flat = [
        red[:, o:o + n] for o, n in zip(offs, sizes)]
    loss = loss_r[0, 0]
    g_dtb, g_alog, g_dsk = g_dtb[:, :nh], g_alog[:, :nh], g_dsk[:, :nh]
    cshard = cd // N_CHIPS
    g_cw = lax.dynamic_slice_in_dim(g_cw_flat.reshape(CONV_K, cd), chip * cshard, cshard, axis=1)

    upd = {}
    upd["w_in"] = tuple(jnp.transpose(u) for u in _adamw(
        jnp.transpose(w_in[0]), g_in_t, jnp.transpose(m_w_in[0]), jnp.transpose(v_w_in[0]), "adamw_w_in"))
    for name, wv, gv, mv, vv in [("w_attn", w_attn_branch[0], g_attn, m_w_attn_branch[0], v_w_attn_branch[0]),
                                 ("w_ssm", w_ssm_branch[0], g_ssm, m_w_ssm_branch[0], v_w_ssm_branch[0]),
                                 ("w_out", w_out[0], g_out, m_w_out[0], v_w_out[0])]:
        upd[name] = _adamw(wv, gv, mv, vv, "adamw_" + name)
    names = ["norm_w", "conv_w", "conv_b", "dt_bias", "a_log", "d_skip", "ssm_norm_w", "final_norm_w"]
    ws = [norm_w, conv_w[0].reshape(1, -1), conv_b, dt_bias, a_log, d_skip, ssm_norm_w, final_norm_w.reshape(1, d)]
    gs = [g_nw, g_cw.reshape(1, -1), g_cb, g_dtb, g_alog, g_dsk, g_snw, g_fnw]
    ms = [m_norm_w, m_conv_w[0].reshape(1, -1), m_conv_b, m_dt_bias, m_a_log, m_d_skip, m_ssm_norm_w,
          m_final_norm_w.reshape(1, d)]
    vs = [v_norm_w, v_conv_w[0].reshape(1, -1), v_conv_b, v_dt_bias, v_a_log, v_d_skip, v_ssm_norm_w,
          v_final_norm_w.reshape(1, d)]
    ssz = [a.shape[1] for a in ws]
    stot = sum(ssz)
    srows = -(-stot // (8 * LANES)) * 8

    def pack(parts):
        return jnp.pad(jnp.concatenate(parts, axis=1), ((0, 0), (0, srows * LANES - stot))).reshape(srows, LANES)

    packed = _adamw(pack(ws), pack(gs), pack(ms), pack(vs), "adamw_small")
    soffs = [sum(ssz[:i]) for i in range(len(ssz))]
    for k, nm in enumerate(names):
        upd[nm] = tuple(p.reshape(1, srows * LANES)[:, soffs[k]:soffs[k] + ssz[k]] for p in packed)

    shapes = dict(norm_w=norm_w.shape, w_in=w_in.shape, conv_w=conv_w.shape, conv_b=conv_b.shape, dt_bias=dt_bias.shape,
                  a_log=a_log.shape, d_skip=d_skip.shape, ssm_norm_w=ssm_norm_w.shape, w_attn=w_attn_branch.shape,
                  w_ssm=w_ssm_branch.shape, w_out=w_out.shape, final_norm_w=final_norm_w.shape)
    order = ["norm_w", "w_in", "conv_w", "conv_b", "dt_bias", "a_log", "d_skip", "ssm_norm_w", "w_attn", "w_ssm",
             "w_out", "final_norm_w"]
    gradv = dict(norm_w=g_nw, w_in=g_in, conv_w=g_cw, conv_b=g_cb, dt_bias=g_dtb, a_log=g_alog, d_skip=g_dsk,
                 ssm_norm_w=g_snw, w_attn=g_attn, w_ssm=g_ssm, w_out=g_out, final_norm_w=g_fnw)
    outs = [loss, grad_x[None]]
    outs += [gradv[n].reshape(shapes[n]) for n in order]
    for k in range(3):
        outs += [upd[n][k].reshape(shapes[n]) for n in order]
    return tuple(outs)
```

```python
import jax
import jax.numpy as jnp
from jax import lax
from jax.experimental import pallas as pl
from jax.experimental.pallas import tpu as pltpu

F32 = jnp.float32
BF16 = jnp.bfloat16
SDS = jax.ShapeDtypeStruct

RMS_EPS = 1e-6
LANES = 128
CHUNK = 128
SSM_HEAD_DIM = 64
SSM_GROUPS = 8
SSM_STATE = 128
CONV_K = 4
ATTN_HEAD_DIM = 128
DILATED_PATTERNS = ((128, 1), (512, 4), (2048, 16))
NEG = -1e30
VMEM_LIMIT = 56 * 1024 * 1024
ADAM_LR, ADAM_B1, ADAM_B2, ADAM_EPS, ADAM_WD, ADAM_STEP = 0.001, 0.9, 0.999, 1e-08, 0.01, 10
MESH = pl.DeviceIdType.MESH
N_CHIPS = 4
N_DEV = 8


class _Cfg:
    def __init__(self, s, d):
        self.S, self.D = s, d
        self.H = d // ATTN_HEAD_DIM
        self.SI = 2 * d
        self.NH = self.SI // SSM_HEAD_DIM
        self.HPG = self.NH // SSM_GROUPS
        self.GW = self.HPG * SSM_HEAD_DIM
        self.BC = SSM_GROUPS * SSM_STATE
        self.CD = self.SI + 2 * self.BC
        self.OQ, self.OK, self.OV, self.OZA = 0, d, 2 * d, 3 * d
        self.OZS = 4 * d
        self.OXBC = self.OZS + self.SI
        self.OGA = self.OXBC + self.CD
        self.OGS = self.OGA + d
        self.NM = self.OGS + d
        self.N_IN = self.NM + self.NH
        assert self.GW % LANES == 0 and self.NH <= LANES and s % 512 == 0 and d % 512 == 0


def _params(sem=None):
    return pltpu.CompilerParams(dimension_semantics=sem, vmem_limit_bytes=VMEM_LIMIT)


def _sigmoid(x):
    return 0.5 * jnp.tanh(0.5 * x) + 0.5


def _softplus(x):
    u = jnp.exp(-jnp.abs(x))
    l1p = jnp.where(u < 1e-3, u * (1.0 - u * (0.5 - u * (1.0 / 3.0))), jnp.log(1.0 + u))
    return jnp.maximum(x, 0.0) + l1p


def _nt(a, b):
    return lax.dot_general(a, b, (((1,), (1,)), ((), ())), preferred_element_type=F32)


def _tn(a, b):
    return lax.dot_general(a, b, (((0,), (0,)), ((), ())), preferred_element_type=F32)


def _nn(a, b):
    return jnp.dot(a, b, preferred_element_type=F32)


def _tile(n, target):
    if n <= target:
        return n
    best = None
    for t in range(LANES, target + 1, LANES):
        if n % t == 0:
            best = t
    assert best is not None, (n, target)
    return best


MM_TK = {"nn": 2048, "nt": 2048, "tn": 1024}


def _mm(a, b, dims, out_dtype, name, tm=1024, tn=2048, tk=None, init=None, carry=None, b_rows=None, out_rows=None):
    tk = MM_TK[dims] if tk is None else tk
    if dims == "nn":
        (m, k), (k2, n) = a.shape, b.shape
        k2 = k2 if b_rows is None else b_rows
    elif dims == "nt":
        (m, k), (n, k2) = a.shape, b.shape
        n = n if b_rows is None else b_rows
    else:
        (k, m), (k2, n) = a.shape, b.shape
    assert k == k2
    tm, tn, tk = _tile(m, tm), _tile(n, tn), _tile(k, tk)
    nk = k // tk
    if dims == "tn":
        a_spec = pl.BlockSpec((tk, tm), lambda i, j, kk: (kk, i))
    else:
        a_spec = pl.BlockSpec((tm, tk), lambda i, j, kk: (i, kk))
    if dims == "nt":
        b_spec = pl.BlockSpec((tn, tk), lambda i, j, kk: (j, kk))
    else:
        b_spec = pl.BlockSpec((tk, tn), lambda i, j, kk: (kk, j))
    o_spec = pl.BlockSpec((tm, tn), lambda i, j, kk: (i, j))
    op = {"nn": _nn, "nt": _nt, "tn": _tn}[dims]
    has_init = init is not None
    nx = len(carry.arrays) if carry is not None else 0
    ni, nj = m // tm, n // tn

    def body(*refs):
        a_ref, b_ref = refs[0], refs[1]
        i_ref = refs[2] if has_init else None
        x_in = refs[2 + has_init:2 + has_init + nx]
        o_ref = refs[2 + has_init + nx]
        x_out = refs[3 + has_init + nx:3 + has_init + 2 * nx]
        acc = refs[3 + has_init + 2 * nx]
        x_sems = refs[4 + has_init + 2 * nx:]
        i, j, kk = pl.program_id(0), pl.program_id(1), pl.program_id(2)

        if nx:
            @pl.when((i == 0) & (j == 0) & (kk == 0))
            def _():
                carry.start(x_in, x_out, x_sems)

        prod = lambda: op(a_ref[...], b_ref[...])
        with_init = (lambda p: p + i_ref[...].astype(F32)) if has_init else (lambda p: p)
        if nk == 1:
            o_ref[...] = with_init(prod()).astype(out_dtype)
        else:
            @pl.when(kk == 0)
            def _():
                acc[...] = with_init(prod())

            @pl.when((kk > 0) & (kk < nk - 1))
            def _():
                acc[...] += prod()

            @pl.when(kk == nk - 1)
            def _():
                o_ref[...] = (acc[...] + prod()).astype(out_dtype)

        if nx:
            @pl.when((i == ni - 1) & (j == nj - 1) & (kk == nk - 1))
            def _():
                carry.finish(x_in, x_out, x_sems)

    in_specs = [a_spec, b_spec] + ([o_spec] if has_init else []) + [HBM_SPEC] * nx
    args = (a, b) + ((init,) if has_init else ()) + (tuple(carry.arrays) if nx else ())
    sems = carry.sem_shapes() if nx else []
    outs = pl.pallas_call(
        body, out_shape=[SDS((m if out_rows is None else out_rows, n), out_dtype)] + (carry.out_shapes if nx else []),
        grid=(ni, nj, nk),
        in_specs=in_specs, out_specs=[o_spec] + [HBM_SPEC] * nx,
        scratch_shapes=[pltpu.VMEM((tm, tn) if nk > 1 else (8, LANES), F32)] + sems,
        compiler_params=_params(("arbitrary",) * 3 if nx else ("parallel", "parallel", "arbitrary")), name=name)(*args)
    return (outs[0], outs[1:]) if nx else outs[0]


def _rmsnorm_fwd(x, w, carry=None):
    s, d = x.shape
    tr = 256
    nsteps = s // tr
    nx = len(carry.arrays) if carry is not None else 0

    def body(*refs):
        x_ref, w_ref, x_in = refs[0], refs[1], refs[2:2 + nx]
        o_ref, x_out, x_sems = refs[2 + nx], refs[3 + nx:3 + 2 * nx], refs[3 + 2 * nx:]
        if nx:
            @pl.when(pl.program_id(0) == 0)
            def _():
                carry.start(x_in, x_out, x_sems)

        xv = x_ref[...]
        r = lax.rsqrt(jnp.mean(xv * xv, axis=-1, keepdims=True) + RMS_EPS)
        o_ref[...] = (xv * r * w_ref[...]).astype(BF16)

        if nx:
            @pl.when(pl.program_id(0) == nsteps - 1)
            def _():
                carry.finish(x_in, x_out, x_sems)

    outs = pl.pallas_call(
        body, out_shape=[SDS((s, d), BF16)] + (carry.out_shapes if nx else []), grid=(nsteps,),
        in_specs=[pl.BlockSpec((tr, d), lambda i: (i, 0)), pl.BlockSpec((1, d), lambda i: (0, 0))] + [HBM_SPEC] * nx,
        out_specs=[pl.BlockSpec((tr, d), lambda i: (i, 0))] + [HBM_SPEC] * nx,
        scratch_shapes=carry.sem_shapes() if nx else [],
        compiler_params=_params(("arbitrary",) if nx else ("parallel",)), name="rmsnorm_fwd")(
            x, w, *(carry.arrays if nx else []))
    return (outs[0], outs[1:]) if nx else outs[0]


def _rmsnorm_bwd(x, w, dhn_a, dhn_b, dout):
    s, d = x.shape
    tr = 256

    def body(x_ref, w_ref, dh_ref, dh2_ref, do_ref, gx_ref, gw_ref):
        xv = x_ref[...]
        r = lax.rsqrt(jnp.mean(xv * xv, axis=-1, keepdims=True) + RMS_EPS)
        nrm = xv * r
        dh = dh_ref[...] + dh2_ref[...]
        gy = dh * w_ref[...]
        gx_ref[...] = do_ref[...] + r * (gy - nrm * jnp.mean(gy * nrm, axis=-1, keepdims=True))

        @pl.when(pl.program_id(0) == 0)
        def _():
            gw_ref[...] = jnp.zeros_like(gw_ref)

        gw_ref[...] += jnp.sum(dh * nrm, axis=0, keepdims=True)

    blk = pl.BlockSpec((tr, d), lambda i: (i, 0))
    row = pl.BlockSpec((1, d), lambda i: (0, 0))
    return pl.pallas_call(
        body, out_shape=(SDS((s, d), F32), SDS((1, d), F32)), grid=(s // tr,),
        in_specs=[blk, row, blk, blk, blk], out_specs=(blk, row),
        compiler_params=_params(("arbitrary",)), name="rmsnorm_bwd")(x, w, dhn_a, dhn_b, dout)


DEINT = DILATED_PATTERNS[-1][1]
DEINT_ROWS = DEINT * LANES


class _Pass:
    def __init__(self, tq, patterns, unit, seg_len):
        self.tq, self.patterns, self.unit, self.seg_len = tq, patterns, unit, seg_len
        self.win = max(w for w, _ in patterns) // unit
        self.w = self.win + tq
        assert self.win % tq == 0


def _attn_tables(ps):
    i = jnp.arange(ps.tq, dtype=jnp.int32)[:, None]
    j = jnp.arange(ps.w, dtype=jnp.int32)[None, :]
    delta = (i + ps.win - j) * ps.unit
    n = jnp.zeros((ps.tq, ps.w), F32)
    for window, dil in ps.patterns:
        n = n + ((delta >= 0) & (delta <= window) & (delta % dil == 0)).astype(F32)
    logn = jnp.where(n > 0, jnp.log(jnp.maximum(n, 1.0)), NEG)
    return logn, jnp.maximum(delta, 0).astype(F32)


def _slopes(h):
    s = jnp.asarray([2.0 ** (-8.0 * (i + 1) / h) for i in range(h)], F32)
    return jnp.broadcast_to(s[:, None, None], (h, 1, LANES))


def _masked_logn(ps, logn_ref, start):
    col = lax.broadcasted_iota(jnp.int32, (ps.tq, ps.w), 1)
    return jnp.where(col >= ps.win - lax.rem(start, ps.seg_len), logn_ref[...], NEG)


def _head_cols(hh):
    return slice(hh * ATTN_HEAD_DIM, (hh + 1) * ATTN_HEAD_DIM)


def _head_window(refs, cs):
    return jnp.concatenate([r[:, cs] for r in refs], axis=0)


def _head_scores(q_ref, kw, cs, base, dist_ref, slope_ref, hh):
    return _nt(q_ref[:, cs], kw) * (ATTN_HEAD_DIM ** -0.5) + (base - slope_ref[hh][0:1, 0:1] * dist_ref[...])


def _lane_of(stat, hh):
    lane = lax.broadcasted_iota(jnp.int32, stat.shape, 1)
    return jnp.sum(jnp.where(lane == hh, stat, 0.0), axis=1, keepdims=True)


def _window_specs(ps, d, col, nb):
    nprev = ps.win // ps.tq
    return [pl.BlockSpec((ps.tq, d), lambda i, b=b: (jnp.maximum(jnp.minimum(i, nb - 1) - (nprev - b), 0), col))
            for b in range(nprev + 1)]


def _attn_fwd(cfg, ps, qkv, cols, tables, slopes, name):
    s, h, d = cfg.S, cfg.H, cfg.D
    tq, nw = ps.tq, ps.win // ps.tq + 1
    nb = s // tq
    logn, dist = tables
    qc, kc, vc = [c // d for c in cols]

    def body(*refs):
        q_ref, k_refs, v_refs = refs[0], refs[1:1 + nw], refs[1 + nw:1 + 2 * nw]
        logn_ref, dist_ref, slope_ref, o_ref, lse_ref = refs[1 + 2 * nw:]
        base = _masked_logn(ps, logn_ref, pl.program_id(0) * tq)
        lane = lax.broadcasted_iota(jnp.int32, (tq, LANES), 1)

        lse = jnp.zeros((tq, LANES), F32)
        for hh in range(h):
            cs = _head_cols(hh)
            sc = _head_scores(q_ref, _head_window(k_refs, cs), cs, base, dist_ref, slope_ref, hh)
            m = jnp.max(sc, axis=1, keepdims=True)
            p = jnp.exp(sc - m)
            l = jnp.sum(p, axis=1, keepdims=True)
            o_ref[:, cs] = (_nn(p.astype(BF16), _head_window(v_refs, cs)) / l).astype(BF16)
            lse = jnp.where(lane == hh, m + jnp.log(l), lse)
        lse_ref[...] = lse

    tab = pl.BlockSpec((tq, ps.w), lambda i: (0, 0))
    return pl.pallas_call(
        body, out_shape=(SDS((s, d), BF16), SDS((s, LANES), F32)), grid=(nb,),
        in_specs=[pl.BlockSpec((tq, d), lambda i: (i, qc))] + _window_specs(ps, d, kc, nb) + _window_specs(ps, d, vc, nb)
        + [tab, tab, pl.BlockSpec((h, 1, LANES), lambda i: (0, 0, 0))],
        out_specs=(pl.BlockSpec((tq, d), lambda i: (i, 0)), pl.BlockSpec((tq, LANES), lambda i: (i, 0))),
        compiler_params=_params(("parallel",)), name=name)(*([qkv] * (1 + 2 * nw)), logn, dist, slopes)


def _attn_bwd(cfg, ps, qkv, cols, do, lse, delta, tables, slopes, name):
    s, h, d = cfg.S, cfg.H, cfg.D
    tq, nprev = ps.tq, ps.win // ps.tq
    nw = nprev + 1
    nb = s // tq
    logn, dist = tables
    qc, kc, vc = [c // d for c in cols]
    scale = ATTN_HEAD_DIM ** -0.5

    def body(*refs):
        q_ref, k_refs, v_refs = refs[0], refs[1:1 + nw], refs[1 + nw:1 + 2 * nw]
        do_ref, lse_ref, dl_ref, logn_ref, dist_ref, slope_ref, dq_ref, dk_ref, dv_ref, ck, cv = refs[1 + 2 * nw:]
        i = pl.program_id(0)
        slot = lambda b: lax.rem(i + b, nprev)

        @pl.when(i == 0)
        def _():
            ck[...] = jnp.zeros_like(ck)
            cv[...] = jnp.zeros_like(cv)

        @pl.when(i < nb)
        def _():
            base = _masked_logn(ps, logn_ref, i * tq)
            lse_all, dl_all = lse_ref[...], dl_ref[...]

            for hh in range(h):
                cs = _head_cols(hh)
                kw, vw = _head_window(k_refs, cs), _head_window(v_refs, cs)
                sc = _head_scores(q_ref, kw, cs, base, dist_ref, slope_ref, hh)
                p = jnp.exp(sc - lse_all[:, hh:hh + 1])
                dob = do_ref[:, cs]
                ds = (p * (_nt(dob, vw) - dl_all[:, hh:hh + 1]) * scale).astype(BF16)
                dq_ref[:, cs] = _nn(ds, kw).astype(BF16)
                dkw = _tn(ds, q_ref[:, cs])
                dvw = _tn(p.astype(BF16), dob)
                dk_ref[:, cs] = ck[slot(0), :, cs] + dkw[0:tq]
                dv_ref[:, cs] = cv[slot(0), :, cs] + dvw[0:tq]
                for b in range(1, nprev):
                    ck[slot(b), :, cs] += dkw[b * tq:(b + 1) * tq]
                    cv[slot(b), :, cs] += dvw[b * tq:(b + 1) * tq]
                ck[slot(0), :, cs] = dkw[nprev * tq:]
                cv[slot(0), :, cs] = dvw[nprev * tq:]

        @pl.when(i >= nb)
        def _():
            dk_ref[...] = ck[slot(0)]
            dv_ref[...] = cv[slot(0)]

    here = lambda i: jnp.minimum(i, nb - 1)
    blk = pl.BlockSpec((tq, d), lambda i: (here(i), 0))
    stat = pl.BlockSpec((tq, LANES), lambda i: (here(i), 0))
    late = pl.BlockSpec((tq, d), lambda i: (jnp.maximum(i - nprev, 0), 0))
    tab = pl.BlockSpec((tq, ps.w), lambda i: (0, 0))
    return pl.pallas_call(
        body, out_shape=(SDS((s, d), BF16), SDS((s, d), F32), SDS((s, d), F32)), grid=(nb + nprev,),
        in_specs=[pl.BlockSpec((tq, d), lambda i: (here(i), qc))] + _window_specs(ps, d, kc, nb)
        + _window_specs(ps, d, vc, nb) + [blk, stat, stat, tab, tab, pl.BlockSpec((h, 1, LANES), lambda i: (0, 0, 0))],
        out_specs=(blk, late, late),
        scratch_shapes=[pltpu.VMEM((nprev, tq, d), F32), pltpu.VMEM((nprev, tq, d), F32)],
        compiler_params=_params(("arbitrary",)), name=name)(
            *([qkv] * (1 + 2 * nw)), do, lse, delta, logn, dist, slopes)


def _by_residue(a):
    return a.reshape(DEINT, a.shape[0] // DEINT, a.shape[1])


def _deint_spec(colblock):
    return pl.BlockSpec((DEINT, LANES, LANES), lambda b, j: (0, b, colblock(j)))


def _deint_rows(scr, out_ref, dtype):
    for r in range(DEINT):
        out_ref[r] = scr[pl.ds(r, LANES, stride=DEINT), :].astype(dtype)


def _int_rows(in_ref, scr):
    for r in range(DEINT):
        scr[pl.ds(r, LANES, stride=DEINT), :] = in_ref[r].astype(F32)


WIDE = 4 * LANES


def _wide_spec():
    return pl.BlockSpec((DEINT, LANES, WIDE), lambda b, j: (0, b, j))


def _deinterleave(x, col0, ncols, name):
    s = x.shape[0]
    c0 = col0 // WIDE

    def body(x_ref, o_ref, scr):
        for t in range(WIDE // LANES):
            cs = slice(t * LANES, (t + 1) * LANES)
            scr[t] = x_ref[:, cs].astype(F32)
            for r in range(DEINT):
                o_ref[r, :, cs] = scr.at[t][pl.ds(r, LANES, stride=DEINT), :].astype(x.dtype)

    out = pl.pallas_call(
        body, out_shape=SDS((DEINT, s // DEINT, ncols), x.dtype), grid=(s // DEINT_ROWS, ncols // WIDE),
        in_specs=[pl.BlockSpec((DEINT_ROWS, WIDE), lambda b, j: (b, c0 + j))],
        out_specs=_wide_spec(),
        scratch_shapes=[pltpu.VMEM((WIDE // LANES, DEINT_ROWS, LANES), F32)],
        compiler_params=_params(("parallel", "parallel")), name=name)(x)
    return out.reshape(s, ncols)


def _attn_merge(cfg, proj, o_1, lse_1, o_2, lse_2):
    s, h = cfg.S, cfg.H
    zb = cfg.OZA // WIDE
    rows = DEINT_ROWS
    hps = WIDE // LANES

    def body(o1_ref, l1_ref, o2_ref, l2_ref, z_ref, o_ref, og_ref, lse_ref, so, sl):
        j = pl.program_id(1)

        @pl.when(j == 0)
        def _():
            _int_rows(l2_ref, sl)
            lse_ref[...] = jnp.zeros_like(lse_ref)

        l1_all, l2_all = l1_ref[...], sl[...]
        lane = lax.broadcasted_iota(jnp.int32, (rows, LANES), 1)
        lse = lse_ref[...]
        for t in range(hps):
            hh = j * hps + t
            cs = slice(t * LANES, (t + 1) * LANES)
            for r in range(DEINT):
                so.at[t][pl.ds(r, LANES, stride=DEINT), :] = o2_ref[r, :, cs].astype(F32)
            l1, l2 = _lane_of(l1_all, hh), _lane_of(l2_all, hh)
            mx = jnp.maximum(l1, l2)
            w1, w2 = jnp.exp(l1 - mx), jnp.exp(l2 - mx)
            den = w1 + w2
            o = (w1 * o1_ref[:, cs].astype(F32) + w2 * so[t]) / den
            z = z_ref[:, cs].astype(F32)
            o_ref[:, cs] = o.astype(BF16)
            og_ref[:, cs] = (o * (z * _sigmoid(z))).astype(BF16)
            lse = jnp.where(lane == hh, mx + jnp.log(den), lse)
        lse_ref[...] = lse

    blk = pl.BlockSpec((rows, WIDE), lambda b, j: (b, j))
    stat = pl.BlockSpec((rows, LANES), lambda b, j: (b, 0))
    return pl.pallas_call(
        body, out_shape=(SDS((s, cfg.D), BF16), SDS((s, cfg.D), BF16), SDS((s, LANES), F32)),
        grid=(s // rows, h // hps),
        in_specs=[blk, stat, _wide_spec(), _deint_spec(lambda j: 0), pl.BlockSpec((rows, WIDE), lambda b, j: (b, zb + j))],
        out_specs=(blk, blk, stat),
        scratch_shapes=[pltpu.VMEM((hps, rows, LANES), F32), pltpu.VMEM((rows, LANES), F32)],
        compiler_params=_params(("parallel", "arbitrary")), name="attn_merge")(
            o_1, lse_1, _by_residue(o_2), _by_residue(lse_2), proj)


def _attn_bwd_prep(cfg, proj, o_a, doag, lse, dproj):
    s, h = cfg.S, cfg.H
    zb = cfg.OZA // WIDE
    rows = DEINT_ROWS
    hps = WIDE // LANES

    def body(o_ref, dg_ref, z_ref, lse_ref, dp_in, dz_ref, do_ref, do2_ref, dl_ref, dl2_ref, lse2_ref, scr):
        del dp_in
        j = pl.program_id(1)

        @pl.when(j == 0)
        def _():
            dl_ref[...] = jnp.zeros_like(dl_ref)

        lane = lax.broadcasted_iota(jnp.int32, (rows, LANES), 1)
        dl = dl_ref[...]
        for t in range(hps):
            cs = slice(t * LANES, (t + 1) * LANES)
            z = z_ref[:, cs].astype(F32)
            sg = _sigmoid(z)
            o = o_ref[:, cs].astype(F32)
            dg = dg_ref[:, cs].astype(F32)
            do = dg * (z * sg)
            dz_ref[:, cs] = (dg * o * (sg * (1.0 + z * (1.0 - sg)))).astype(BF16)
            do_ref[:, cs] = do.astype(BF16)
            scr[...] = do
            for r in range(DEINT):
                do2_ref[r, :, cs] = scr[pl.ds(r, LANES, stride=DEINT), :].astype(BF16)
            dl = jnp.where(lane == j * hps + t, jnp.sum(do * o, axis=1, keepdims=True), dl)
        dl_ref[...] = dl

        @pl.when(j == h // hps - 1)
        def _():
            scr[...] = dl
            _deint_rows(scr, dl2_ref, F32)
            scr[...] = lse_ref[...]
            _deint_rows(scr, lse2_ref, F32)

    blk = pl.BlockSpec((rows, WIDE), lambda b, j: (b, j))
    stat = pl.BlockSpec((rows, LANES), lambda b, j: (b, 0))
    stat2 = _deint_spec(lambda j: 0)
    outs = pl.pallas_call(
        body,
        out_shape=(SDS(dproj.shape, BF16), SDS((s, cfg.D), BF16), SDS((DEINT, s // DEINT, cfg.D), BF16),
                   SDS((s, LANES), F32), SDS((DEINT, s // DEINT, LANES), F32), SDS((DEINT, s // DEINT, LANES), F32)),
        grid=(s // rows, h // hps),
        in_specs=[blk, blk, pl.BlockSpec((rows, WIDE), lambda b, j: (b, zb + j)), stat, HBM_SPEC],
        out_specs=(pl.BlockSpec((rows, WIDE), lambda b, j: (b, zb + j)), blk, _wide_spec(), stat, stat2, stat2),
        scratch_shapes=[pltpu.VMEM((rows, LANES), F32)],
        input_output_aliases={4: 0},
        compiler_params=_params(("parallel", "arbitrary")), name="attn_bwd_prep")(o_a, doag, proj, lse, dproj)
    dproj, do, do2, dl, dl2, lse2 = outs
    return dproj, do, do2.reshape(s, cfg.D), dl, dl2.reshape(s, LANES), lse2.reshape(s, LANES)


def _attn_grad_sum(cfg, g_1, g_2, col0, dproj, name):
    s = cfg.S
    c0 = col0 // WIDE
    rows = DEINT_ROWS

    def body(g1_ref, g2_ref, dp_in, o_ref, scr):
        del dp_in
        for t in range(WIDE // LANES):
            cs = slice(t * LANES, (t + 1) * LANES)
            for r in range(DEINT):
                scr.at[t][pl.ds(r, LANES, stride=DEINT), :] = g2_ref[r, :, cs].astype(F32)
            o_ref[:, cs] = (g1_ref[:, cs].astype(F32) + scr[t]).astype(BF16)

    return pl.pallas_call(
        body, out_shape=SDS(dproj.shape, BF16), grid=(s // rows, cfg.D // WIDE),
        in_specs=[pl.BlockSpec((rows, WIDE), lambda b, j: (b, j)), _wide_spec(), HBM_SPEC],
        out_specs=pl.BlockSpec((rows, WIDE), lambda b, j: (b, c0 + j)),
        scratch_shapes=[pltpu.VMEM((WIDE // LANES, rows, LANES), F32)],
        input_output_aliases={2: 0},
        compiler_params=_params(("parallel", "parallel")), name=name)(g_1, _by_residue(g_2), dproj)


CONV_HALO = 16
CONV_TR = 512
CONV_CW = 1024


def _rows_back(a, n):
    return a if n == 0 else pltpu.roll(a, n % a.shape[0], axis=0)


def _conv_fwd(cfg, proj, conv_w, conv_b):
    s, cd = cfg.S, cfg.CD
    tr, cw, hl = CONV_TR, CONV_CW, CONV_HALO
    cb0 = cfg.OXBC // cw

    def body(x_ref, h_ref, w_ref, b_ref, o_ref):
        i = pl.program_id(0)
        halo = jnp.where(i > 0, h_ref[...].astype(F32), 0.0)
        ext = jnp.concatenate([halo, x_ref[...].astype(F32)], axis=0)
        pre = b_ref[...] + jnp.zeros((tr, cw), F32)
        for k in range(CONV_K):
            pre = pre + w_ref[k:k + 1, :] * _rows_back(ext, CONV_K - 1 - k)[hl:]
        o_ref[...] = (pre * _sigmoid(pre)).astype(BF16)

    return pl.pallas_call(
        body, out_shape=SDS((s, cd), BF16), grid=(s // tr, cd // cw),
        in_specs=[pl.BlockSpec((tr, cw), lambda i, j: (i, cb0 + j)),
                  pl.BlockSpec((hl, cw), lambda i, j: (jnp.maximum(i * (tr // hl) - 1, 0), cb0 + j)),
                  pl.BlockSpec((CONV_K, cw), lambda i, j: (0, j)),
                  pl.BlockSpec((1, cw), lambda i, j: (0, j))],
        out_specs=pl.BlockSpec((tr, cw), lambda i, j: (i, j)),
        compiler_params=_params(("parallel", "parallel")), name="conv_fwd")(proj, proj, conv_w, conv_b)


def _conv_bwd(cfg, proj, dact, conv_w, conv_b, dproj):
    s, cd = cfg.S, cfg.CD
    tr, cw, hl = CONV_TR, CONV_CW, CONV_HALO
    cb0 = cfg.OXBC // cw
    nr = s // tr
    last_h = s // hl - 1

    def body(x_ref, hp_ref, hn_ref, d_ref, dn_ref, w_ref, b_ref, dp_in, dx_ref, gw_ref, gb_ref):
        del dp_in
        i = pl.program_id(1)
        ext = jnp.concatenate([jnp.where(i > 0, hp_ref[...].astype(F32), 0.0), x_ref[...].astype(F32),
                               hn_ref[...].astype(F32)], axis=0)
        shifted = [_rows_back(ext, CONV_K - 1 - k)[hl:] for k in range(CONV_K)]
        pre = b_ref[...] + jnp.zeros((tr + hl, cw), F32)
        for k in range(CONV_K):
            pre = pre + w_ref[k:k + 1, :] * shifted[k]
        sg = _sigmoid(pre)
        dact = jnp.concatenate([d_ref[...].astype(F32), jnp.where(i < nr - 1, dn_ref[...].astype(F32), 0.0)], axis=0)
        dpre = dact * (sg * (1.0 + pre * (1.0 - sg)))
        dx = jnp.zeros((tr, cw), F32)
        for k in range(CONV_K):
            dx = dx + w_ref[k:k + 1, :] * _rows_back(dpre, -(CONV_K - 1 - k))[0:tr]
        dx_ref[...] = dx.astype(BF16)

        @pl.when(i == 0)
        def _():
            gw_ref[...] = jnp.zeros_like(gw_ref)
            gb_ref[...] = jnp.zeros_like(gb_ref)

        dcur = dpre[0:tr]
        gb_ref[...] += jnp.sum(dcur, axis=0, keepdims=True)
        for k in range(CONV_K):
            gw_ref[k:k + 1, :] += jnp.sum(dcur * shifted[k][0:tr], axis=0, keepdims=True)

    return pl.pallas_call(
        body, out_shape=(SDS(dproj.shape, BF16), SDS((CONV_K, cd), F32), SDS((1, cd), F32)), grid=(cd // cw, nr),
        in_specs=[pl.BlockSpec((tr, cw), lambda j, i: (i, cb0 + j)),
                  pl.BlockSpec((hl, cw), lambda j, i: (jnp.maximum(i * (tr // hl) - 1, 0), cb0 + j)),
                  pl.BlockSpec((hl, cw), lambda j, i: (jnp.minimum((i + 1) * (tr // hl), last_h), cb0 + j)),
                  pl.BlockSpec((tr, cw), lambda j, i: (i, j)),
                  pl.BlockSpec((hl, cw), lambda j, i: (jnp.minimum((i + 1) * (tr // hl), last_h), j)),
                  pl.BlockSpec((CONV_K, cw), lambda j, i: (0, j)),
                  pl.BlockSpec((1, cw), lambda j, i: (0, j)),
                  pl.BlockSpec(memory_space=pl.ANY)],
        out_specs=(pl.BlockSpec((tr, cw), lambda j, i: (i, cb0 + j)),
                   pl.BlockSpec((CONV_K, cw), lambda j, i: (0, j)),
                   pl.BlockSpec((1, cw), lambda j, i: (0, j))),
        input_output_aliases={7: 0},
        compiler_params=_params(("parallel", "arbitrary")), name="conv_bwd")(
            proj, proj, proj, dact, dact, conv_w, conv_b, dproj)


def _expand(v, e, terms):
    out, rem = None, v
    for _ in range(terms):
        hi = rem.astype(BF16)
        t = _nn(hi, e)
        out = t if out is None else out + t
        rem = rem - hi.astype(F32)
    return out


def _segsum(v, e, terms):
    out, rem = None, v
    for _ in range(terms):
        hi = rem.astype(BF16)
        t = _nt(hi, e)
        out = t if out is None else out + t
        rem = rem - hi.astype(F32)
    return out


def _expand_row(row, e, terms):
    return _expand(jnp.broadcast_to(row, (8, LANES)), e, terms)[0:1]


def _segsum_row(row, e, terms):
    return _segsum(jnp.broadcast_to(row, (8, row.shape[1])), e, terms)[0:1]


def _expansion_matrix(cfg):
    hh = jnp.arange(LANES, dtype=jnp.int32)[:, None]
    cc = jnp.arange(cfg.SI, dtype=jnp.int32)[None, :]
    return (cc // SSM_HEAD_DIM == hh).astype(BF16)


def _tri(lower):
    r = lax.broadcasted_iota(jnp.int32, (CHUNK, CHUNK), 0)
    c = lax.broadcasted_iota(jnp.int32, (CHUNK, CHUNK), 1)
    return (c <= r) if lower else (c >= r)


def _ssd_prep(dtr_ref, db_ref, al_ref, e):
    dtr = dtr_ref[...] + db_ref[...]
    dt = _softplus(dtr)
    a = -jnp.exp(al_ref[...])
    acum = jnp.dot(_tri(True).astype(F32), dt * a, precision=lax.Precision.HIGHEST, preferred_element_type=F32)
    return dtr, dt, a, _expand(dt, e, 2), _expand(acum, e, 3)


def _ssd_fwd(cfg, xact, dt_raw, proj, dt_bias, a_log, d_skip, norm_w, e):
    s, si, cd, gw, bc = cfg.S, cfg.SI, cfg.CD, cfg.GW, cfg.BC
    nc = s // CHUNK
    zb = cfg.OZS // si
    tiles = gw // LANES

    def body(xa_ref, dtr_ref, z_ref, db_ref, al_ref, dsk_ref, nw_ref, e_ref, y_ref, y2_ref, st_ref,
             state, ybuf, x_s, xw_s, ae_s, ea_s, lam_s):
        @pl.when(pl.program_id(0) == 0)
        def _():
            state[...] = jnp.zeros_like(state)

        st_ref[...] = state[...]
        ev = e_ref[...]
        _, _, _, dt_e, a_e = _ssd_prep(dtr_ref, db_ref, al_ref, ev)
        xs = xa_ref[:, 0:si].astype(F32)
        x = xs * dt_e
        lam_e = a_e[CHUNK - 1:CHUNK, :]
        x_s[...] = x.astype(BF16)
        xw_s[...] = (x * jnp.exp(lam_e - a_e)).astype(BF16)
        ae_s[...] = a_e
        ea_s[...] = jnp.exp(a_e)
        ybuf[...] = _expand_row(dsk_ref[...], ev, 3) * xs
        lam_s[...] = jnp.broadcast_to(jnp.exp(lam_e), (8, si))
        tril = _tri(True)
        lane = lax.broadcasted_iota(jnp.int32, (CHUNK, LANES), 1)

        def group(g, carry):
            co = g * gw
            bg = xa_ref[:, pl.ds(si + g * SSM_STATE, SSM_STATE)]
            cg = xa_ref[:, pl.ds(si + bc + g * SSM_STATE, SSM_STATE)]
            cbm = _nt(cg, bg)
            st = state[:, pl.ds(co, gw)]
            yoff = _nn(cg, st.astype(BF16)) * ea_s[:, pl.ds(co, gw)]
            for k in range(tiles):
                tc = co + k * LANES
                at = ae_s[:, pl.ds(tc, LANES)]
                att = at.T
                xt = x_s[:, pl.ds(tc, LANES)]
                acc = yoff[:, k * LANES:(k + 1) * LANES]
                for half in range(2):
                    lo = half * SSM_HEAD_DIM
                    seg = at[:, lo:lo + 1] - att[lo:lo + 1, :]
                    dec = jnp.exp(jnp.where(tril, seg, NEG))
                    xh = jnp.where((lane >= lo) & (lane < lo + SSM_HEAD_DIM), xt, jnp.zeros_like(xt))
                    acc = acc + _nn((cbm * dec).astype(BF16), xh)
                ybuf[:, pl.ds(tc, LANES)] += acc
            state[:, pl.ds(co, gw)] = st * lam_s[0:1, pl.ds(co, gw)] + _tn(bg, xw_s[:, pl.ds(co, gw)])
            return carry

        for g in range(SSM_GROUPS):
            group(g, 0)
        y = ybuf[...]
        y_ref[...] = y.astype(BF16)
        z = z_ref[...].astype(F32)
        u = y * (z * _sigmoid(z))
        r = lax.rsqrt(jnp.mean(u * u, axis=-1, keepdims=True) + RMS_EPS)
        y2_ref[...] = (u * r * nw_ref[...]).astype(BF16)

    row = lambda n: pl.BlockSpec((1, n), lambda c: (0, 0))
    return pl.pallas_call(
        body,
        out_shape=(SDS((s, si), BF16), SDS((s, si), BF16), SDS((nc, SSM_STATE, si), F32)),
        grid=(nc,),
        in_specs=[pl.BlockSpec((CHUNK, cd), lambda c: (c, 0)),
                  pl.BlockSpec((CHUNK, LANES), lambda c: (c, 0)),
                  pl.BlockSpec((CHUNK, si), lambda c: (c, zb)),
                  row(LANES), row(LANES), row(LANES), row(si),
                  pl.BlockSpec((LANES, si), lambda c: (0, 0))],
        out_specs=(pl.BlockSpec((CHUNK, si), lambda c: (c, 0)),
                   pl.BlockSpec((CHUNK, si), lambda c: (c, 0)),
                   pl.BlockSpec((None, SSM_STATE, si), lambda c: (c, 0, 0))),
        scratch_shapes=[pltpu.VMEM((SSM_STATE, si), F32), pltpu.VMEM((CHUNK, si), F32),
                        pltpu.VMEM((CHUNK, si), BF16), pltpu.VMEM((CHUNK, si), BF16),
                        pltpu.VMEM((CHUNK, si), F32), pltpu.VMEM((CHUNK, si), F32),
                        pltpu.VMEM((8, si), F32)],
        compiler_params=_params(("arbitrary",)), name="ssd_fwd")(
            xact, dt_raw, proj, dt_bias, a_log, d_skip, norm_w, e)


def _ssd_bwd(cfg, xact, dt_raw, proj, y, dy2, states, dt_bias, a_log, d_skip, norm_w, e, dproj):
    s, si, cd, gw, bc, hpg = cfg.S, cfg.SI, cfg.CD, cfg.GW, cfg.BC, cfg.HPG
    nc = s // CHUNK
    zb = cfg.OZS // si
    tiles = gw // LANES

    def body(xa_ref, dtr_ref, z_ref, y_ref, d2_ref, st_ref, db_ref, al_ref, dsk_ref, nw_ref, e_ref, dp_in,
             dz_ref, dxa_ref, ddt_ref, gnw_ref, gdb_ref, gal_ref, gds_ref,
             dh, dhn, xs_s, x_s, w_s, ae_s, ea_s, g_s, dx_s, dae_s, r_s, lam_s, dle_s):
        del dp_in

        @pl.when(pl.program_id(0) == 0)
        def _():
            dh[...] = jnp.zeros_like(dh)
            gnw_ref[...] = jnp.zeros_like(gnw_ref)
            gdb_ref[...] = jnp.zeros_like(gdb_ref)
            gal_ref[...] = jnp.zeros_like(gal_ref)
            gds_ref[...] = jnp.zeros_like(gds_ref)

        ev = e_ref[...]
        yv = y_ref[...].astype(F32)
        z = z_ref[...].astype(F32)
        sg = _sigmoid(z)
        sz = z * sg
        u = yv * sz
        r = lax.rsqrt(jnp.mean(u * u, axis=-1, keepdims=True) + RMS_EPS)
        nrm = u * r
        d2 = d2_ref[...].astype(F32)
        gnw_ref[...] += jnp.sum(d2 * nrm, axis=0, keepdims=True)
        gn = d2 * nw_ref[...]
        du = r * (gn - nrm * jnp.mean(gn * nrm, axis=-1, keepdims=True))
        gv = du * sz
        dz_ref[...] = (du * yv * (sg * (1.0 + z * (1.0 - sg)))).astype(BF16)
        g_s[...] = gv

        dtr, dt, a, dt_e, a_e = _ssd_prep(dtr_ref, db_ref, al_ref, ev)
        xs = xa_ref[:, 0:si].astype(F32)
        x = xs * dt_e
        lam_e = a_e[CHUNK - 1:CHUNK, :]
        xs_s[...] = xs
        x_s[...] = x
        w_s[...] = jnp.exp(lam_e - a_e)
        ae_s[...] = a_e
        ea_s[...] = jnp.exp(a_e)
        lam_s[...] = jnp.broadcast_to(jnp.exp(lam_e), (8, si))
        gds_ref[...] += _segsum_row(jnp.sum(gv * xs, axis=0, keepdims=True), ev, 2)
        r_s[...] = jnp.zeros_like(r_s)
        tril = _tri(True)
        lane = lax.broadcasted_iota(jnp.int32, (CHUNK, LANES), 1)
        sub = lax.broadcasted_iota(jnp.int32, (CHUNK, LANES), 0)

        def group(g, carry):
            co = g * gw
            bo = si + g * SSM_STATE
            cof = si + bc + g * SSM_STATE
            cols = pl.ds(co, gw)
            bg = xa_ref[:, pl.ds(bo, SSM_STATE)]
            cg = xa_ref[:, pl.ds(cof, SSM_STATE)]
            cbm = _nt(cg, bg)
            st = st_ref[:, cols]
            stb = st.astype(BF16)
            dho = dh[:, cols]
            dhob = dho.astype(BF16)
            ea = ea_s[:, cols]
            gg = g_s[:, cols]
            xg = x_s[:, cols]
            wg = w_s[:, cols]
            explam = lam_s[0:1, cols]
            yoff = _nn(cg, stb) * ea
            ga = (gg * ea).astype(BF16)
            dc = _nt(ga, stb)
            dhn[:, cols] = dho * explam + _tn(cg, ga)
            bdh = _nn(bg, dhob)
            db = _nt((xg * wg).astype(BF16), dhob)
            t = xg * bdh * wg
            dle_s[0:1, cols] = jnp.sum(t, axis=0, keepdims=True) + explam * jnp.sum(dho * st, axis=0, keepdims=True)
            dae_base = gg * yoff - t
            dxw = wg * bdh
            dcb = jnp.zeros((CHUNK, CHUNK), F32)
            for k in range(tiles):
                tc = co + k * LANES
                ksl = slice(k * LANES, (k + 1) * LANES)
                at = ae_s[:, pl.ds(tc, LANES)]
                att = at.T
                xt = xg[:, ksl].astype(BF16)
                gt = gg[:, ksl].astype(BF16)
                dxt = dxw[:, ksl]
                place = jnp.zeros((CHUNK, LANES), F32)
                for half in range(2):
                    lo = half * SSM_HEAD_DIM
                    seg = at[:, lo:lo + 1] - att[lo:lo + 1, :]
                    dec = jnp.exp(jnp.where(tril, seg, NEG))
                    mh = cbm * dec
                    gh = jnp.where((lane >= lo) & (lane < lo + SSM_HEAD_DIM), gt, jnp.zeros_like(gt))
                    dm = _nt(gh, xt)
                    dxt = dxt + _tn(mh.astype(BF16), gh)
                    dcb = dcb + dm * dec
                    dseg = dm * mh
                    place = place + jnp.where(lane == lo, jnp.sum(dseg, axis=1, keepdims=True), 0.0)
                    hidx = g * hpg + 2 * k + half
                    r_s[...] += jnp.where(sub == hidx, jnp.sum(dseg, axis=0, keepdims=True), 0.0)
                dx_s[:, pl.ds(tc, LANES)] = dxt
                dae_s[:, pl.ds(tc, LANES)] = dae_base[:, ksl] + place
            dcbb = dcb.astype(BF16)
            dxa_ref[:, pl.ds(bo, SSM_STATE)] = (db + _tn(dcbb, cg)).astype(BF16)
            dxa_ref[:, pl.ds(cof, SSM_STATE)] = (dc + _nn(dcbb, bg)).astype(BF16)
            return carry

        for g in range(SSM_GROUPS):
            group(g, 0)
        dlam = _segsum_row(dle_s[0:1, :], ev, 2)
        da_ = _segsum(dae_s[...], ev, 2) - r_s[...].T
        da_ = da_ + jnp.where(sub == CHUNK - 1, dlam, 0.0)
        dda = jnp.dot(_tri(False).astype(F32), da_, precision=lax.Precision.HIGHEST, preferred_element_type=F32)
        dxv = dx_s[...]
        xs = xs_s[...]
        ddt = dda * a + _segsum(dxv * xs, ev, 2)
        gal_ref[...] += jnp.sum(dda * dt, axis=0, keepdims=True) * a
        ddtr = ddt * _sigmoid(dtr)
        gdb_ref[...] += jnp.sum(ddtr, axis=0, keepdims=True)
        ddt_ref[...] = ddtr
        dxa_ref[:, 0:si] = (dxv * dt_e + g_s[...] * _expand_row(dsk_ref[...], ev, 3)).astype(BF16)
        dh[...] = dhn[...]

    rev = lambda c: nc - 1 - c
    row = lambda n: pl.BlockSpec((1, n), lambda c: (0, 0))
    big = lambda: pltpu.VMEM((CHUNK, si), F32)
    return pl.pallas_call(
        body,
        out_shape=(SDS(dproj.shape, BF16), SDS((s, cd), BF16), SDS((s, LANES), F32),
                   SDS((1, si), F32), SDS((1, LANES), F32), SDS((1, LANES), F32), SDS((1, LANES), F32)),
        grid=(nc,),
        in_specs=[pl.BlockSpec((CHUNK, cd), lambda c: (rev(c), 0)),
                  pl.BlockSpec((CHUNK, LANES), lambda c: (rev(c), 0)),
                  pl.BlockSpec((CHUNK, si), lambda c: (rev(c), zb)),
                  pl.BlockSpec((CHUNK, si), lambda c: (rev(c), 0)),
                  pl.BlockSpec((CHUNK, si), lambda c: (rev(c), 0)),
                  pl.BlockSpec((None, SSM_STATE, si), lambda c: (rev(c), 0, 0)),
                  row(LANES), row(LANES), row(LANES), row(si),
                  pl.BlockSpec((LANES, si), lambda c: (0, 0)),
                  pl.BlockSpec(memory_space=pl.ANY)],
        out_specs=(pl.BlockSpec((CHUNK, si), lambda c: (rev(c), zb)),
                   pl.BlockSpec((CHUNK, cd), lambda c: (rev(c), 0)),
                   pl.BlockSpec((CHUNK, LANES), lambda c: (rev(c), 0)),
                   row(si), row(LANES), row(LANES), row(LANES)),
        scratch_shapes=[pltpu.VMEM((SSM_STATE, si), F32), pltpu.VMEM((SSM_STATE, si), F32),
                        big(), big(), big(), big(), big(), big(), big(), big(),
                        pltpu.VMEM((CHUNK, LANES), F32), pltpu.VMEM((8, si), F32), pltpu.VMEM((8, si), F32)],
        input_output_aliases={11: 0},
        compiler_params=_params(("arbitrary",)), name="ssd_bwd")(
            xact, dt_raw, proj, y, dy2, states, dt_bias, a_log, d_skip, norm_w, e, dproj)


MERGE_TR = 512
MERGE_CW = 2048


def _merge_fwd(cfg, proj, a_br, s_br):
    s, d = cfg.S, cfg.D
    tr, cw = MERGE_TR, min(MERGE_CW, d)
    ga0, gs0 = cfg.OGA // cw, cfg.OGS // cw

    def body(ga_ref, gs_ref, a_ref, s_ref, o_ref):
        o_ref[...] = (_sigmoid(ga_ref[...].astype(F32)) * a_ref[...].astype(F32)
                      + _sigmoid(gs_ref[...].astype(F32)) * s_ref[...].astype(F32)).astype(BF16)

    blk = pl.BlockSpec((tr, cw), lambda i, j: (i, j))
    return pl.pallas_call(
        body, out_shape=SDS((s, d), BF16), grid=(s // tr, d // cw),
        in_specs=[pl.BlockSpec((tr, cw), lambda i, j: (i, ga0 + j)),
                  pl.BlockSpec((tr, cw), lambda i, j: (i, gs0 + j)), blk, blk],
        out_specs=blk, compiler_params=_params(("parallel", "parallel")), name="merge_fwd")(proj, proj, a_br, s_br)


def _merge_bwd(cfg, proj, branch, dmerged, gate_off, dproj, name):
    s, d = cfg.S, cfg.D
    tr, cw = MERGE_TR, min(MERGE_CW, d)
    g0 = gate_off // cw
    fresh = dproj is None

    def body(*refs):
        g_ref, b_ref, dm_ref = refs[:3]
        dg_ref, db_ref = refs[-2:]
        dm = dm_ref[...].astype(F32)
        sg = _sigmoid(g_ref[...].astype(F32))
        db_ref[...] = (dm * sg).astype(BF16)
        dg_ref[...] = (dm * b_ref[...].astype(F32) * sg * (1.0 - sg)).astype(BF16)

    blk = pl.BlockSpec((tr, cw), lambda i, j: (i, j))
    gate = pl.BlockSpec((tr, cw), lambda i, j: (i, g0 + j))
    return pl.pallas_call(
        body, out_shape=(SDS((s, cfg.NM), BF16), SDS((s, d), BF16)), grid=(s // tr, d // cw),
        in_specs=[gate, blk, blk] + ([] if fresh else [HBM_SPEC]),
        out_specs=(gate, blk),
        input_output_aliases={} if fresh else {3: 0},
        compiler_params=_params(("parallel", "parallel")), name=name)(
            *((proj, branch, dmerged) + (() if fresh else (dproj,))))


def _outproj_loss(merged, w_out, x, target, fnw):
    s, d = x.shape
    tr = 256

    def body(m_ref, w_ref, x_ref, t_ref, fw_ref, dof_ref, dob_ref, loss_ref, g_ref):
        out = x_ref[...] + _nn(m_ref[...], w_ref[...])
        r = lax.rsqrt(jnp.mean(out * out, axis=-1, keepdims=True) + RMS_EPS)
        nrm = out * r
        fw = fw_ref[...]
        err = nrm * fw - t_ref[...]
        dy = err * (1.0 / d)
        gy = dy * fw
        dout = r * (gy - nrm * jnp.mean(gy * nrm, axis=-1, keepdims=True))
        dof_ref[...] = dout
        dob_ref[...] = dout.astype(BF16)

        @pl.when(pl.program_id(0) == 0)
        def _():
            loss_ref[...] = jnp.zeros_like(loss_ref)
            g_ref[...] = jnp.zeros_like(g_ref)

        loss_ref[...] += jnp.sum(jnp.sum(err * err, axis=1, keepdims=True), axis=0, keepdims=True) * (0.5 / d)
        g_ref[...] += jnp.sum(dy * nrm, axis=0, keepdims=True)

    blk = pl.BlockSpec((tr, d), lambda i: (i, 0))
    return pl.pallas_call(
        body, out_shape=(SDS((s, d), F32), SDS((s, d), BF16), SDS((1, LANES), F32), SDS((1, d), F32)), grid=(s // tr,),
        in_specs=[blk, pl.BlockSpec((d, d), lambda i: (0, 0)), blk, blk, pl.BlockSpec((1, d), lambda i: (0, 0))],
        out_specs=(blk, blk, pl.BlockSpec((1, LANES), lambda i: (0, 0)), pl.BlockSpec((1, d), lambda i: (0, 0))),
        compiler_params=_params(("arbitrary",)), name="outproj_loss")(merged, w_out, x, target, fnw)


ELEMWISE_BLOCK_BYTES = 1 << 20


def _row_block(rows, cols, itemsize=4):
    best = None
    for tr in range(16, rows + 1, 16):
        if rows % tr == 0 and tr * cols * itemsize <= ELEMWISE_BLOCK_BYTES:
            best = tr
    return best if best is not None else rows


def _adamw(w, g, m, v, name):
    rows, cols = w.shape
    tr = _row_block(rows, cols)
    if rows // tr > 64 and cols % LANES == 0:
        blk, grid = pl.BlockSpec((rows, LANES), lambda i: (0, i)), (cols // LANES,)
    else:
        blk, grid = pl.BlockSpec((tr, cols), lambda i: (i, 0)), (rows // tr,)
    out = SDS((rows, cols), F32)
    return pl.pallas_call(
        _adamw_body(), out_shape=(out, out, out), grid=grid, in_specs=[blk] * 4, out_specs=(blk,) * 3,
        compiler_params=_params(("parallel",)), name=name)(w, g, m, v)


def _adamw_body():
    def body(w_ref, g_ref, m_ref, v_ref, d_ref, nm_ref, nv_ref):
        gv = g_ref[...]
        nm = ADAM_B1 * m_ref[...] + (1.0 - ADAM_B1) * gv
        nv = ADAM_B2 * v_ref[...] + (1.0 - ADAM_B2) * jnp.square(gv)
        m_hat = nm / (1.0 - ADAM_B1 ** ADAM_STEP)
        v_hat = nv / (1.0 - ADAM_B2 ** ADAM_STEP)
        d_ref[...] = -ADAM_LR * (m_hat / (jnp.sqrt(v_hat) + ADAM_EPS) + ADAM_WD * w_ref[...])
        nm_ref[...] = nm
        nv_ref[...] = nv

    return body


HBM_SPEC = pl.BlockSpec(memory_space=pl.ANY)


def _position():
    return lax.axis_index("x"), lax.axis_index("y"), lax.axis_index("c")


class _Carry:
    def __init__(self, arrays, out_shapes, sems, start, finish):
        self.arrays, self.out_shapes, self.sems, self.start, self.finish = list(arrays), out_shapes, sems, start, finish

    def sem_shapes(self):
        return [pltpu.SemaphoreType.DMA((k,)) for k in self.sems]


def _gather_carry(shards, by_cols=()):
    n = len(shards)

    def copies(ins, outs, sems):
        send_sems, recv_sems, fsend_sems, frecv_sems = sems
        x, y, c = _position()
        me = 2 * x + y
        peers = [(1 - x, y), (x, 1 - y), (1 - x, 1 - y)]

        def half_of(t, chip, half):
            if t in by_cols:
                c2 = ins[t].shape[1] // 2
                return outs[t].at[chip, :, pl.ds(half * c2, c2)]
            return outs[t].at[chip, half]

        def over_ici(t, p, chip):
            px, py = peers[p]
            if t in by_cols:
                c2 = ins[t].shape[1] // 2
                src = ins[t].at[:, pl.ds(c * c2, c2)]
            else:
                r2 = ins[t].shape[0] // 2
                src = ins[t].at[pl.ds(c * r2, r2), :]
            return pltpu.make_async_remote_copy(
                src_ref=src, dst_ref=half_of(t, chip, c), send_sem=send_sems.at[3 * t + p],
                recv_sem=recv_sems.at[3 * t + p], device_id=(px, py, c), device_id_type=MESH)

        def to_sibling(t, p, half):
            px, py = peers[p]
            slab = half_of(t, 2 * px + py, half)
            return pltpu.make_async_remote_copy(
                src_ref=slab, dst_ref=slab, send_sem=fsend_sems.at[3 * t + p], recv_sem=frecv_sems.at[3 * t + p],
                device_id=(x, y, 1 - c), device_id_type=MESH)

        pairs = [(t, p) for t in range(n) for p in range(3)]
        sends = [over_ici(t, p, me) for t, p in pairs]
        lands = [over_ici(t, p, 2 * peers[p][0] + peers[p][1]) for t, p in pairs]
        passed = [to_sibling(t, p, c) for t, p in pairs]
        from_sibling = [to_sibling(t, p, 1 - c) for t, p in pairs]
        return sends, lands, passed, from_sibling

    def start(ins, outs, sems):
        for cp in copies(ins, outs, sems)[0]:
            cp.start()

    def finish(ins, outs, sems):
        sends, lands, passed, from_sibling = copies(ins, outs, sems)
        for land, fwd in zip(lands, passed):
            land.wait_recv()
            fwd.start()
        for cp in from_sibling:
            cp.wait_recv()
        for cp in sends + passed:
            cp.wait_send()

    shapes = [SDS((N_CHIPS,) + a.shape if t in by_cols else (N_CHIPS, 2, a.shape[0] // 2, a.shape[1]), a.dtype)
              for t, a in enumerate(shards)]
    return _Carry(shards, shapes, [3 * n] * 4, start, finish)


def _scatter_carry(parts):
    def start(ins, outs, sems):
        for cp in _scatter_copies(ins, outs, *sems)[0]:
            cp.start()

    def finish(ins, outs, sems):
        sends, lands = _scatter_copies(ins, outs, *sems)
        for cp in lands:
            cp.wait_recv()
        for cp in sends:
            cp.wait_send()

    return _Carry(parts, [SDS(a.shape, a.dtype) for a in parts], [3 * len(parts)] * 2, start, finish)


def _with_own(gathered, own, chip):
    full = gathered.reshape((N_CHIPS,) + own.shape)
    return lax.dynamic_update_index_in_dim(full, own, chip, 0)


def _exchange_halves(grads):
    n = len(grads)
    slabs = [list(g) if isinstance(g, (list, tuple)) else [g] for g in grads]
    flat = [a for s in slabs for a in s]
    ncp = len(flat)

    def body(*refs):
        ins, outs = refs[:ncp], refs[ncp:ncp + n]
        send_sems, recv_sems = refs[ncp + n:]
        x, y, c = _position()
        cps, k = [], 0
        for t in range(n):
            for j in range(len(slabs[t])):
                if len(slabs[t]) == 1:
                    r2 = ins[k].shape[1] // 2
                    src, dst = ins[k].at[:, pl.ds((1 - c) * r2, r2), :], outs[t]
                else:
                    r2 = ins[k].shape[0] // 2
                    src, dst = ins[k].at[pl.ds((1 - c) * r2, r2), :], outs[t].at[j]
                cps.append(pltpu.make_async_remote_copy(
                    src_ref=src, dst_ref=dst, send_sem=send_sems.at[k], recv_sem=recv_sems.at[k],
                    device_id=(x, y, 1 - c), device_id_type=MESH))
                k += 1
        for cp in cps:
            cp.start()
        for cp in cps:
            cp.wait()

    def landing(s):
        a = s[0]
        return SDS((N_CHIPS, a.shape[-2] // 2, a.shape[-1]), a.dtype)

    return pl.pallas_call(
        body, out_shape=[landing(s) for s in slabs],
        in_specs=[HBM_SPEC] * ncp, out_specs=[HBM_SPEC] * n,
        scratch_shapes=[pltpu.SemaphoreType.DMA((ncp,)), pltpu.SemaphoreType.DMA((ncp,))],
        compiler_params=pltpu.CompilerParams(has_side_effects=True), name="reduce_sibling")(*flat)


def _scatter_copies(ins, outs, send_sems, recv_sems):
    x, y, c = _position()
    me = 2 * x + y
    peers = [(1 - x, y), (x, 1 - y), (1 - x, 1 - y)]

    def remote(t, p, src_slab, dst_slab):
        px, py = peers[p]
        return pltpu.make_async_remote_copy(
            src_ref=ins[t].at[src_slab], dst_ref=outs[t].at[dst_slab], send_sem=send_sems.at[3 * t + p],
            recv_sem=recv_sems.at[3 * t + p], device_id=(px, py, c), device_id_type=MESH)

    n = len(ins)
    sends = [remote(t, p, 2 * peers[p][0] + peers[p][1], me) for t in range(n) for p in range(3)]
    lands = [remote(t, p, me, 2 * peers[p][0] + peers[p][1]) for t in range(n) for p in range(3)]
    return sends, lands


def _share_halves(halves):
    n = len(halves)

    def body(*refs):
        ins, outs = refs[:n], refs[n:2 * n]
        send_sems, recv_sems = refs[2 * n:]
        x, y, c = _position()

        def copy(t, slab):
            return pltpu.make_async_remote_copy(
                src_ref=ins[t].at[slab], dst_ref=outs[t].at[slab], send_sem=send_sems.at[t], recv_sem=recv_sems.at[t],
                device_id=(x, y, 1 - c), device_id_type=MESH)

        for t in range(n):
            copy(t, c).start()
        for t in range(n):
            copy(t, 1 - c).wait_recv()
        for t in range(n):
            copy(t, c).wait_send()

    return pl.pallas_call(
        body, out_shape=[SDS(a.shape, a.dtype) for a in halves],
        in_specs=[HBM_SPEC] * n, out_specs=[HBM_SPEC] * n,
        scratch_shapes=[pltpu.SemaphoreType.DMA((n,)), pltpu.SemaphoreType.DMA((n,))],
        input_output_aliases={t: t for t in range(n)},
        compiler_params=pltpu.CompilerParams(has_side_effects=True), name="share_sibling")(*halves)


def _add_sibling(grad, recv, core):
    nch, r2, cols = recv.shape
    tr = _row_block(r2, cols)
    nb = r2 // tr

    def body(c_ref, g_ref, r_ref, o_ref):
        del c_ref
        o_ref[...] = (g_ref[...].astype(F32) + r_ref[...].astype(F32)).astype(BF16)

    return pl.pallas_call(
        body, out_shape=SDS(recv.shape, BF16),
        grid_spec=pltpu.PrefetchScalarGridSpec(
            num_scalar_prefetch=1, grid=(nch, nb),
            in_specs=[pl.BlockSpec((None, tr, cols), lambda j, i, c_ref: (j, c_ref[0] * nb + i, 0)),
                      pl.BlockSpec((None, tr, cols), lambda j, i, c_ref: (j, i, 0))],
            out_specs=pl.BlockSpec((None, tr, cols), lambda j, i, c_ref: (j, i, 0))),
        compiler_params=_params(("parallel", "parallel")), name="add_sibling")(core, grad, recv)


def _add_chips(own, recv, chip_core):
    nch, r2, cols = recv.shape
    tr = _row_block(r2, cols)

    nsc = 2 + nch

    def body(*refs):
        me = refs[0][0]
        own_ref, p_refs, o_ref = refs[nsc], refs[nsc + 1:nsc + 1 + nch], refs[nsc + 1 + nch]
        acc = None
        for j in range(nch):
            term = jnp.where(me == j, own_ref[...], p_refs[j][...]).astype(F32)
            acc = term if acc is None else acc + term
        o_ref[...] = acc

    def slab(j):
        return pl.BlockSpec((None, tr, cols), lambda i, *sc: (sc[2 + j][0], i, 0))

    return pl.pallas_call(
        body, out_shape=SDS((2, r2, cols), F32),
        grid_spec=pltpu.PrefetchScalarGridSpec(
            num_scalar_prefetch=nsc, grid=(r2 // tr,),
            in_specs=[pl.BlockSpec((None, tr, cols), lambda i, *sc: (sc[0][0], i, 0))] + [slab(j) for j in range(nch)],
            out_specs=pl.BlockSpec((None, tr, cols), lambda i, *sc: (sc[1][0], i, 0))),
        compiler_params=_params(("parallel",)), name="add_chips")(*chip_core, own, *([recv] * nch))


def _exchange_col_halves(grad):
    nch, r, cols = grad.shape
    c2 = cols // 2

    def body(in_ref, out_ref, send_sem, recv_sem):
        x, y, c = _position()
        cp = pltpu.make_async_remote_copy(
            src_ref=in_ref.at[:, :, pl.ds((1 - c) * c2, c2)], dst_ref=out_ref, send_sem=send_sem.at[0],
            recv_sem=recv_sem.at[0], device_id=(x, y, 1 - c), device_id_type=MESH)
        cp.start()
        cp.wait()

    return pl.pallas_call(
        body, out_shape=SDS((nch, r, c2), grad.dtype), in_specs=[HBM_SPEC], out_specs=HBM_SPEC,
        scratch_shapes=[pltpu.SemaphoreType.DMA((1,)), pltpu.SemaphoreType.DMA((1,))],
        compiler_params=pltpu.CompilerParams(has_side_effects=True), name="reduce_sibling_cols")(grad)


def _add_sibling_cols(grad, recv, core):
    nch, r, c2 = recv.shape
    nb = c2 // LANES

    def body(c_ref, g_ref, r_ref, o_ref):
        del c_ref
        o_ref[...] = (g_ref[...].astype(F32) + r_ref[...].astype(F32)).astype(BF16)

    blk = pl.BlockSpec((None, r, LANES), lambda j, i, c_ref: (j, 0, i))
    return pl.pallas_call(
        body, out_shape=SDS(recv.shape, BF16),
        grid_spec=pltpu.PrefetchScalarGridSpec(
            num_scalar_prefetch=1, grid=(nch, nb),
            in_specs=[pl.BlockSpec((None, r, LANES), lambda j, i, c_ref: (j, 0, c_ref[0] * nb + i)), blk],
            out_specs=blk),
        compiler_params=_params(("parallel", "parallel")), name="add_sibling_cols")(core, grad, recv)


def _add_chips_cols(own, recv, chip_core):
    nch, r, c2 = recv.shape
    nb = c2 // LANES
    nsc = 2 + nch

    def body(*refs):
        me = refs[0][0]
        own_ref, p_refs, o_ref = refs[nsc], refs[nsc + 1:nsc + 1 + nch], refs[nsc + 1 + nch]
        acc = None
        for j in range(nch):
            term = jnp.where(me == j, own_ref[...], p_refs[j][...]).astype(F32)
            acc = term if acc is None else acc + term
        o_ref[...] = acc

    def slab(j):
        return pl.BlockSpec((None, r, LANES), lambda i, *sc: (sc[2 + j][0], 0, i))

    return pl.pallas_call(
        body, out_shape=SDS((r, 2 * c2), F32),
        grid_spec=pltpu.PrefetchScalarGridSpec(
            num_scalar_prefetch=nsc, grid=(nb,),
            in_specs=[pl.BlockSpec((None, r, LANES), lambda i, *sc: (sc[0][0], 0, i))] + [slab(j) for j in range(nch)],
            out_specs=pl.BlockSpec((r, LANES), lambda i, *sc: (0, sc[1][0] * nb + i))),
        compiler_params=_params(("parallel",)), name="add_chips_cols")(*chip_core, own, *([recv] * nch))


def _share_col_halves(full):
    r, cols = full.shape
    c2 = cols // 2

    def body(in_ref, out_ref, send_sem, recv_sem):
        x, y, c = _position()

        def copy(half):
            return pltpu.make_async_remote_copy(
                src_ref=in_ref.at[:, pl.ds(half * c2, c2)], dst_ref=out_ref.at[:, pl.ds(half * c2, c2)],
                send_sem=send_sem.at[0], recv_sem=recv_sem.at[0], device_id=(x, y, 1 - c), device_id_type=MESH)

        copy(c).start()
        copy(1 - c).wait_recv()
        copy(c).wait_send()

    return pl.pallas_call(
        body, out_shape=SDS(full.shape, full.dtype), in_specs=[HBM_SPEC], out_specs=HBM_SPEC,
        scratch_shapes=[pltpu.SemaphoreType.DMA((1,)), pltpu.SemaphoreType.DMA((1,))],
        input_output_aliases={0: 0},
        compiler_params=pltpu.CompilerParams(has_side_effects=True), name="share_sibling_cols")(full)


def _allreduce_small(pack):
    rows = pack.shape[0]

    def body(p_ref, o_ref, buf, send_sems, recv_sems):
        x, y, c = _position()
        me = 4 * x + 2 * y + c
        buf[me] = p_ref[...]

        def copy(dst_dev, slot):
            return pltpu.make_async_remote_copy(
                src_ref=p_ref, dst_ref=buf.at[slot], send_sem=send_sems.at[dst_dev], recv_sem=recv_sems.at[slot],
                device_id=(dst_dev // 4, (dst_dev // 2) % 2, dst_dev % 2), device_id_type=MESH)

        for dev in range(N_DEV):
            @pl.when(dev != me)
            def _():
                copy(dev, me).start()
        for dev in range(N_DEV):
            @pl.when(dev != me)
            def _():
                copy(dev, dev).wait_recv()
        for dev in range(N_DEV):
            @pl.when(dev != me)
            def _():
                copy(dev, me).wait_send()
        acc = buf[0]
        for dev in range(1, N_DEV):
            acc = acc + buf[dev]
        o_ref[...] = acc

    return pl.pallas_call(
        body, out_shape=SDS(pack.shape, F32),
        in_specs=[pl.BlockSpec(memory_space=pltpu.VMEM)], out_specs=pl.BlockSpec(memory_space=pltpu.VMEM),
        scratch_shapes=[pltpu.VMEM((N_DEV, rows, LANES), F32), pltpu.SemaphoreType.DMA((N_DEV,)),
                        pltpu.SemaphoreType.DMA((N_DEV,))],
        compiler_params=pltpu.CompilerParams(has_side_effects=True), name="allreduce_small")(pack)


ATTN_TQ = 256


def _local_step(cfg, x, target, w, to_chips=None, late=None, hn=None):
    d = cfg.D
    if hn is None:
        hn = _rmsnorm_fwd(x, w["norm_w"])
    proj = _mm(hn, w["w_main_t"], "nt", BF16, "proj_main", carry=late[0] if late else None, b_rows=cfg.NM)
    if late:
        proj, arrived = proj
        w = {**w, **late[1](arrived)}
    dt_raw = _mm(hn, w["w_dt_t"], "nt", F32, "proj_dt")
    slopes = _slopes(cfg.H)
    near = _Pass(ATTN_TQ, DILATED_PATTERNS[:-1], 1, cfg.S)
    far = _Pass(LANES, DILATED_PATTERNS[-1:], DEINT, cfg.S // DEINT)
    tab_near, tab_far = _attn_tables(near), _attn_tables(far)
    cols_near, cols_far = (cfg.OQ, cfg.OK, cfg.OV), (0, d, 2 * d)
    qkv_far = _deinterleave(proj, 0, 3 * d, "attn_deinterleave")
    o_1, lse_1 = _attn_fwd(cfg, near, proj, cols_near, tab_near, slopes, "attn_fwd_near")
    o_2, lse_2 = _attn_fwd(cfg, far, qkv_far, cols_far, tab_far, slopes, "attn_fwd_far")
    o_a, oag, lse = _attn_merge(cfg, proj, o_1, lse_1, o_2, lse_2)
    xact = _conv_fwd(cfg, proj, w["conv_w"], w["conv_b"])
    e = _expansion_matrix(cfg)
    y, y2, states = _ssd_fwd(cfg, xact, dt_raw, proj, w["dt_bias"], w["a_log"], w["d_skip"], w["ssm_norm_w"], e)
    a_br = _mm(oag, w["w_attn"], "nn", BF16, "branch_attn")
    s_br = _mm(y2, w["w_ssm"], "nn", BF16, "branch_ssm")
    merged = _merge_fwd(cfg, proj, a_br, s_br)
    dout_f, dout_b, loss_row, g_fnw = _outproj_loss(merged, w["w_out"], x, target, w["final_norm_w"])

    dmerged = _mm(dout_b, w["w_out"], "nt", BF16, "d_merged")
    g_w_out = _mm(merged, dout_b, "tn", BF16, "g_w_out")
    dproj, da_br = _merge_bwd(cfg, proj, a_br, dmerged, cfg.OGA, None, "merge_bwd_attn")
    dproj, ds_br = _merge_bwd(cfg, proj, s_br, dmerged, cfg.OGS, dproj, "merge_bwd_ssm")
    doag = _mm(da_br, w["w_attn"], "nt", BF16, "d_oag")
    g_w_attn = _mm(oag, da_br, "tn", BF16, "g_w_attn")
    dy2 = _mm(ds_br, w["w_ssm"], "nt", BF16, "d_y2")
    g_w_ssm = _mm(y2, ds_br, "tn", BF16, "g_w_ssm")
    dproj, dxact, ddt, g_snw, g_dtb, g_alog, g_dsk = _ssd_bwd(
        cfg, xact, dt_raw, proj, y, dy2, states, w["dt_bias"], w["a_log"], w["d_skip"], w["ssm_norm_w"], e, dproj)
    dproj, g_cw, g_cb = _conv_bwd(cfg, proj, dxact, w["conv_w"], w["conv_b"], dproj)
    dproj, do, do_far, dl, dl_far, lse_far = _attn_bwd_prep(cfg, proj, o_a, doag, lse, dproj)
    g_near = _attn_bwd(cfg, near, proj, cols_near, do, lse, dl, tab_near, slopes, "attn_bwd_near")
    g_far = _attn_bwd(cfg, far, qkv_far, cols_far, do_far, lse_far, dl_far, tab_far, slopes, "attn_bwd_far")
    for g_1, g_2, col0, nm in zip(g_near, g_far, cols_near, ("attn_dq", "attn_dk", "attn_dv")):
        dproj = _attn_grad_sum(cfg, g_1, g_2, col0, dproj, nm)
    ddt_b = ddt.astype(BF16)
    g_w_main = _mm(dproj, hn, "tn", BF16, "g_w_main", out_rows=cfg.N_IN)
    g_w_dt = _mm(ddt_b, hn, "tn", BF16, "g_w_dt")
    grads = dict(w_main_t=g_w_main, w_dt_t=g_w_dt, conv_w=g_cw, conv_b=g_cb, dt_bias=g_dtb, a_log=g_alog,
                 d_skip=g_dsk, ssm_norm_w=g_snw, w_attn=g_w_attn, w_ssm=g_w_ssm, w_out=g_w_out, final_norm_w=g_fnw)
    sent = to_chips(grads) if to_chips is not None else ()
    dhn = _mm(dproj, w["w_main_t"], "nn", F32, "d_hn", tk=1024, carry=_scatter_carry(sent) if sent else None,
              b_rows=cfg.NM)
    landed = ()
    if sent:
        dhn, landed = dhn
    dhn_dt = _mm(ddt_b, w["w_dt_t"], "nn", F32, "d_hn_dt")
    grad_x, grads["norm_w"] = _rmsnorm_bwd(x, w["norm_w"], dhn, dhn_dt, dout_f)
    return loss_row, grad_x, grads, sent, landed


def _pad_lanes(v):
    return jnp.pad(v, ((0, 0), (0, LANES - v.shape[1])))


def _main_from_rows(cfg, w_in_t):
    lo, hi = cfg.OGA, cfg.OGA + cfg.NH
    dt = jnp.pad(w_in_t[lo:hi], ((0, LANES - cfg.NH), (0, 0)))
    return lax.dynamic_update_slice(w_in_t, w_in_t[hi:], (lo, 0)), dt


def _rows_from_main(cfg, g_main_t, g_dt_t):
    lo, hi = cfg.OGA, cfg.OGA + cfg.NH
    g = lax.dynamic_update_slice(g_main_t, g_main_t[lo:cfg.NM], (hi, 0))
    return lax.dynamic_update_slice(g, g_dt_t[:cfg.NH], (lo, 0))


def _full_weights(cfg, norm_w, w_in_t, conv_w, conv_b, dt_bias, a_log, d_skip, ssm_norm_w, w_attn, w_ssm, w_out, fnw):
    w_main, w_dt = _main_from_rows(cfg, w_in_t)
    return dict(norm_w=norm_w, w_main_t=w_main.astype(BF16), w_dt_t=w_dt.astype(BF16), conv_w=conv_w, conv_b=conv_b,
                dt_bias=_pad_lanes(dt_bias), a_log=_pad_lanes(a_log), d_skip=_pad_lanes(d_skip), ssm_norm_w=ssm_norm_w,
                final_norm_w=fnw, **{k: v.astype(BF16) for k, v in (("w_attn", w_attn), ("w_ssm", w_ssm), ("w_out", w_out))
                                     if v is not None})


def kernel(x, norm_w, w_in, conv_w, conv_b, dt_bias, a_log, d_skip, ssm_norm_w, w_attn_branch, w_ssm_branch, w_out, final_norm_w, loss_target, m_norm_w, m_w_in, m_conv_w, m_conv_b, m_dt_bias, m_a_log, m_d_skip, m_ssm_norm_w, m_w_attn_branch, m_w_ssm_branch, m_w_out, m_final_norm_w, v_norm_w, v_w_in, v_conv_w, v_conv_b, v_dt_bias, v_a_log, v_d_skip, v_ssm_norm_w, v_w_attn_branch, v_w_ssm_branch, v_w_out, v_final_norm_w):
    cfg = _Cfg(x.shape[1], x.shape[2])
    d, si, cd, nh = cfg.D, cfg.SI, cfg.CD, cfg.NH
    chip = 2 * lax.axis_index("x") + lax.axis_index("y")
    core = lax.axis_index("c").astype(jnp.int32).reshape(1)
    chip = chip.astype(jnp.int32)
    chip_core = [chip.reshape(1), core] + [jnp.where(chip == j, (j + 1) % N_CHIPS, j).astype(jnp.int32).reshape(1)
                                           for j in range(N_CHIPS)]

    own = [jnp.transpose(w_in[0]).astype(BF16), conv_w[0].reshape(4 * CONV_K, -1)]
    hn, gathered = _rmsnorm_fwd(x[0], norm_w, carry=_gather_carry(own, by_cols=(0,)))
    a_in, a_cw = [_with_own(g, o, chip) for g, o in zip(gathered, own)]
    conv_w_full = a_cw.reshape(N_CHIPS, CONV_K, cd // N_CHIPS).transpose(1, 0, 2).reshape(CONV_K, cd)
    w = _full_weights(cfg, norm_w, a_in.reshape(cfg.N_IN, d), conv_w_full, conv_b, dt_bias, a_log, d_skip,
                      ssm_norm_w, None, None, None, final_norm_w.reshape(1, d))
    own_late = [w_attn_branch[0].astype(BF16), w_ssm_branch[0].astype(BF16), w_out[0].astype(BF16)]

    def late_weights(arrived):
        a_attn, a_ssm, a_out = [_with_own(g, o, chip) for g, o in zip(arrived, own_late)]
        return dict(w_attn=a_attn.reshape(d, d), w_ssm=a_ssm.reshape(si, d), w_out=a_out.reshape(d, d))

    def to_chips(grads):
        g_in_t = _rows_from_main(cfg, grads["w_main_t"], grads["w_dt_t"]).reshape(N_CHIPS, cfg.N_IN // N_CHIPS, d)
        by_chip = [grads["w_attn"].reshape(N_CHIPS, d // N_CHIPS, d),
                   grads["w_ssm"].reshape(N_CHIPS, si // N_CHIPS, d),
                   grads["w_out"].reshape(N_CHIPS, d // N_CHIPS, d)]
        from_sibling = _exchange_halves(by_chip)
        return ([_add_sibling_cols(g_in_t, _exchange_col_halves(g_in_t), core)]
                + [_add_sibling(g, r, core) for g, r in zip(by_chip, from_sibling)])

    loss_row, grad_x, grads, chip_sums, from_chips = _local_step(
        cfg, x[0], loss_target[0], w, to_chips, (_gather_carry(own_late), late_weights), hn)
    g_in_t = _share_col_halves(_add_chips_cols(chip_sums[0], from_chips[0], chip_core))
    halves = [_add_chips(o, p, chip_core) for o, p in zip(chip_sums[1:], from_chips[1:])]
    g_attn, g_ssm, g_out = [h.reshape(2 * h.shape[1], h.shape[2]) for h in _share_halves(halves)]
    g_in = jnp.transpose(g_in_t)

    small = [loss_row, grads["norm_w"], grads["conv_b"], grads["dt_bias"], grads["a_log"], grads["d_skip"],
             grads["ssm_norm_w"], grads["final_norm_w"], grads["conv_w"].reshape(1, CONV_K * cd)]
    sizes = [a.shape[1] for a in small]
    total = sum(sizes)
    rows = -(-total // (8 * LANES)) * 8
    flat = jnp.pad(jnp.concatenate(small, axis=1), ((0, 0), (0, rows * LANES - total)))
    red = _allreduce_small(flat.reshape(rows, LANES)).reshape(1, rows * LANES)
    offs = [sum(sizes[:i]) for i in range(len(sizes))]
    loss_r, g_nw, g_cb, g_dtb, g_alog, g_dsk, g_snw, g_fnw, g_cw_flat = [
        red[:, o:o + n] for o, n in zip(offs, sizes)]
    loss = loss_r[0, 0]
    g_dtb, g_alog, g_dsk = g_dtb[:, :nh], g_alog[:, :nh], g_dsk[:, :nh]
    cshard = cd // N_CHIPS
    g_cw = lax.dynamic_slice_in_dim(g_cw_flat.reshape(CONV_K, cd), chip * cshard, cshard, axis=1)

    upd = {}
    upd["w_in"] = tuple(jnp.transpose(u) for u in _adamw(
        jnp.transpose(w_in[0]), g_in_t, jnp.transpose(m_w_in[0]), jnp.transpose(v_w_in[0]), "adamw_w_in"))
    for name, wv, gv, mv, vv in [("w_attn", w_attn_branch[0], g_attn, m_w_attn_branch[0], v_w_attn_branch[0]),
                                 ("w_ssm", w_ssm_branch[0], g_ssm, m_w_ssm_branch[0], v_w_ssm_branch[0]),
                                 ("w_out", w_out[0], g_out, m_w_out[0], v_w_out[0])]:
        upd[name] = _adamw(wv, gv, mv, vv, "adamw_" + name)
    names = ["norm_w", "conv_w", "conv_b", "dt_bias", "a_log", "d_skip", "ssm_norm_w", "final_norm_w"]
    ws = [norm_w, conv_w[0].reshape(1, -1), conv_b, dt_bias, a_log, d_skip, ssm_norm_w, final_norm_w.reshape(1, d)]
    gs = [g_nw, g_cw.reshape(1, -1), g_cb, g_dtb, g_alog, g_dsk, g_snw, g_fnw]
    ms = [m_norm_w, m_conv_w[0].reshape(1, -1), m_conv_b, m_dt_bias, m_a_log, m_d_skip, m_ssm_norm_w,
          m_final_norm_w.reshape(1, d)]
    vs = [v_norm_w, v_conv_w[0].reshape(1, -1), v_conv_b, v_dt_bias, v_a_log, v_d_skip, v_ssm_norm_w,
          v_final_norm_w.reshape(1, d)]
    ssz = [a.shape[1] for a in ws]
    stot = sum(ssz)
    srows = -(-stot // (8 * LANES)) * 8

    def pack(parts):
        return jnp.pad(jnp.concatenate(parts, axis=1), ((0, 0), (0, srows * LANES - stot))).reshape(srows, LANES)

    packed = _adamw(pack(ws), pack(gs), pack(ms), pack(vs), "adamw_small")
    soffs = [sum(ssz[:i]) for i in range(len(ssz))]
    for k, nm in enumerate(names):
        upd[nm] = tuple(p.reshape(1, srows * LANES)[:, soffs[k]:soffs[k] + ssz[k]] for p in packed)

    shapes = dict(norm_w=norm_w.shape, w_in=w_in.shape, conv_w=conv_w.shape, conv_b=conv_b.shape, dt_bias=dt_bias.shape,
                  a_log=a_log.shape, d_skip=d_skip.shape, ssm_norm_w=ssm_norm_w.shape, w_attn=w_attn_branch.shape,
                  w_ssm=w_ssm_branch.shape, w_out=w_out.shape, final_norm_w=final_norm_w.shape)
    order = ["norm_w", "w_in", "conv_w", "conv_b", "dt_bias", "a_log", "d_skip", "ssm_norm_w", "w_attn", "w_ssm",
             "w_out", "final_norm_w"]
    gradv = dict(norm_w=g_nw, w_in=g_in, conv_w=g_cw, conv_b=g_cb, dt_bias=g_dtb, a_log=g_alog, d_skip=g_dsk,
                 ssm_norm_w=g_snw, w_attn=g_attn, w_ssm=g_ssm, w_out=g_out, final_norm_w=g_fnw)
    outs = [loss, grad_x[None]]
    outs += [gradv[n].reshape(shapes[n]) for n in order]
    for k in range(3):
        outs += [upd[n][k].reshape(shapes[n]) for n in order]
    return tuple(outs)
```

```python
import jax
import jax.numpy as jnp
from jax import lax
from jax.experimental import pallas as pl
from jax.experimental.pallas import tpu as pltpu

F32 = jnp.float32
BF16 = jnp.bfloat16
SDS = jax.ShapeDtypeStruct

RMS_EPS = 1e-6
LANES = 128
CHUNK = 128
SSM_HEAD_DIM = 64
SSM_GROUPS = 8
SSM_STATE = 128
CONV_K = 4
ATTN_HEAD_DIM = 128
DILATED_PATTERNS = ((128, 1), (512, 4), (2048, 16))
NEG = -1e30
VMEM_LIMIT = 56 * 1024 * 1024
ADAM_LR, ADAM_B1, ADAM_B2, ADAM_EPS, ADAM_WD, ADAM_STEP = 0.001, 0.9, 0.999, 1e-08, 0.01, 10
MESH = pl.DeviceIdType.MESH
N_CHIPS = 4
N_DEV = 8


class _Cfg:
    def __init__(self, s, d):
        self.S, self.D = s, d
        self.H = d // ATTN_HEAD_DIM
        self.SI = 2 * d
        self.NH = self.SI // SSM_HEAD_DIM
        self.HPG = self.NH // SSM_GROUPS
        self.GW = self.HPG * SSM_HEAD_DIM
        self.BC = SSM_GROUPS * SSM_STATE
        self.CD = self.SI + 2 * self.BC
        self.OQ, self.OK, self.OV, self.OZA = 0, d, 2 * d, 3 * d
        self.OZS = 4 * d
        self.OXBC = self.OZS + self.SI
        self.OGA = self.OXBC + self.CD
        self.OGS = self.OGA + d
        self.NM = self.OGS + d
        self.N_IN = self.NM + self.NH
        assert self.GW % LANES == 0 and self.NH <= LANES and s % 512 == 0 and d % 512 == 0


def _params(sem=None):
    return pltpu.CompilerParams(dimension_semantics=sem, vmem_limit_bytes=VMEM_LIMIT)


def _sigmoid(x):
    return 0.5 * jnp.tanh(0.5 * x) + 0.5


def _softplus(x):
    u = jnp.exp(-jnp.abs(x))
    l1p = jnp.where(u < 1e-3, u * (1.0 - u * (0.5 - u * (1.0 / 3.0))), jnp.log(1.0 + u))
    return jnp.maximum(x, 0.0) + l1p


def _nt(a, b):
    return lax.dot_general(a, b, (((1,), (1,)), ((), ())), preferred_element_type=F32)


def _tn(a, b):
    return lax.dot_general(a, b, (((0,), (0,)), ((), ())), preferred_element_type=F32)


def _nn(a, b):
    return jnp.dot(a, b, preferred_element_type=F32)


def _tile(n, target):
    if n <= target:
        return n
    best = None
    for t in range(LANES, target + 1, LANES):
        if n % t == 0:
            best = t
    assert best is not None, (n, target)
    return best


MM_TK = {"nn": 2048, "nt": 2048, "tn": 1024}


def _mm(a, b, dims, out_dtype, name, tm=1024, tn=2048, tk=None, init=None, carry=None, b_rows=None, out_rows=None):
    tk = MM_TK[dims] if tk is None else tk
    if dims == "nn":
        (m, k), (k2, n) = a.shape, b.shape
        k2 = k2 if b_rows is None else b_rows
    elif dims == "nt":
        (m, k), (n, k2) = a.shape, b.shape
        n = n if b_rows is None else b_rows
    else:
        (k, m), (k2, n) = a.shape, b.shape
    assert k == k2
    tm, tn, tk = _tile(m, tm), _tile(n, tn), _tile(k, tk)
    nk = k // tk
    if dims == "tn":
        a_spec = pl.BlockSpec((tk, tm), lambda i, j, kk: (kk, i))
    else:
        a_spec = pl.BlockSpec((tm, tk), lambda i, j, kk: (i, kk))
    if dims == "nt":
        b_spec = pl.BlockSpec((tn, tk), lambda i, j, kk: (j, kk))
    else:
        b_spec = pl.BlockSpec((tk, tn), lambda i, j, kk: (kk, j))
    o_spec = pl.BlockSpec((tm, tn), lambda i, j, kk: (i, j))
    op = {"nn": _nn, "nt": _nt, "tn": _tn}[dims]
    has_init = init is not None
    nx = len(carry.arrays) if carry is not None else 0
    ni, nj = m // tm, n // tn

    def body(*refs):
        a_ref, b_ref = refs[0], refs[1]
        i_ref = refs[2] if has_init else None
        x_in = refs[2 + has_init:2 + has_init + nx]
        o_ref = refs[2 + has_init + nx]
        x_out = refs[3 + has_init + nx:3 + has_init + 2 * nx]
        acc = refs[3 + has_init + 2 * nx]
        x_sems = refs[4 + has_init + 2 * nx:]
        i, j, kk = pl.program_id(0), pl.program_id(1), pl.program_id(2)

        if nx:
            @pl.when((i == 0) & (j == 0) & (kk == 0))
            def _():
                carry.start(x_in, x_out, x_sems)

        prod = lambda: op(a_ref[...], b_ref[...])
        with_init = (lambda p: p + i_ref[...].astype(F32)) if has_init else (lambda p: p)
        if nk == 1:
            o_ref[...] = with_init(prod()).astype(out_dtype)
        else:
            @pl.when(kk == 0)
            def _():
                acc[...] = with_init(prod())

            @pl.when((kk > 0) & (kk < nk - 1))
            def _():
                acc[...] += prod()

            @pl.when(kk == nk - 1)
            def _():
                o_ref[...] = (acc[...] + prod()).astype(out_dtype)

        if nx:
            @pl.when((i == ni - 1) & (j == nj - 1) & (kk == nk - 1))
            def _():
                carry.finish(x_in, x_out, x_sems)

    in_specs = [a_spec, b_spec] + ([o_spec] if has_init else []) + [HBM_SPEC] * nx
    args = (a, b) + ((init,) if has_init else ()) + (tuple(carry.arrays) if nx else ())
    sems = carry.sem_shapes() if nx else []
    outs = pl.pallas_call(
        body, out_shape=[SDS((m if out_rows is None else out_rows, n), out_dtype)] + (carry.out_shapes if nx else []),
        grid=(ni, nj, nk),
        in_specs=in_specs, out_specs=[o_spec] + [HBM_SPEC] * nx,
        scratch_shapes=[pltpu.VMEM((tm, tn) if nk > 1 else (8, LANES), F32)] + sems,
        compiler_params=_params(("arbitrary",) * 3 if nx else ("parallel", "parallel", "arbitrary")), name=name)(*args)
    return (outs[0], outs[1:]) if nx else outs[0]


def _rmsnorm_fwd(x, w, carry=None):
    s, d = x.shape
    tr = 256
    nsteps = s // tr
    nx = len(carry.arrays) if carry is not None else 0

    def body(*refs):
        x_ref, w_ref, x_in = refs[0], refs[1], refs[2:2 + nx]
        o_ref, x_out, x_sems = refs[2 + nx], refs[3 + nx:3 + 2 * nx], refs[3 + 2 * nx:]
        if nx:
            @pl.when(pl.program_id(0) == 0)
            def _():
                carry.start(x_in, x_out, x_sems)

        xv = x_ref[...]
        r = lax.rsqrt(jnp.mean(xv * xv, axis=-1, keepdims=True) + RMS_EPS)
        o_ref[...] = (xv * r * w_ref[...]).astype(BF16)

        if nx:
            @pl.when(pl.program_id(0) == nsteps - 1)
            def _():
                carry.finish(x_in, x_out, x_sems)

    outs = pl.pallas_call(
        body, out_shape=[SDS((s, d), BF16)] + (carry.out_shapes if nx else []), grid=(nsteps,),
        in_specs=[pl.BlockSpec((tr, d), lambda i: (i, 0)), pl.BlockSpec((1, d), lambda i: (0, 0))] + [HBM_SPEC] * nx,
        out_specs=[pl.BlockSpec((tr, d), lambda i: (i, 0))] + [HBM_SPEC] * nx,
        scratch_shapes=carry.sem_shapes() if nx else [],
        compiler_params=_params(("arbitrary",) if nx else ("parallel",)), name="rmsnorm_fwd")(
            x, w, *(carry.arrays if nx else []))
    return (outs[0], outs[1:]) if nx else outs[0]


def _rmsnorm_bwd(x, w, dhn_a, dhn_b, dout):
    s, d = x.shape
    tr = 256

    def body(x_ref, w_ref, dh_ref, dh2_ref, do_ref, gx_ref, gw_ref):
        xv = x_ref[...]
        r = lax.rsqrt(jnp.mean(xv * xv, axis=-1, keepdims=True) + RMS_EPS)
        nrm = xv * r
        dh = dh_ref[...] + dh2_ref[...]
        gy = dh * w_ref[...]
        gx_ref[...] = do_ref[...] + r * (gy - nrm * jnp.mean(gy * nrm, axis=-1, keepdims=True))

        @pl.when(pl.program_id(0) == 0)
        def _():
            gw_ref[...] = jnp.zeros_like(gw_ref)

        gw_ref[...] += jnp.sum(dh * nrm, axis=0, keepdims=True)

    blk = pl.BlockSpec((tr, d), lambda i: (i, 0))
    row = pl.BlockSpec((1, d), lambda i: (0, 0))
    return pl.pallas_call(
        body, out_shape=(SDS((s, d), F32), SDS((1, d), F32)), grid=(s // tr,),
        in_specs=[blk, row, blk, blk, blk], out_specs=(blk, row),
        compiler_params=_params(("arbitrary",)), name="rmsnorm_bwd")(x, w, dhn_a, dhn_b, dout)


DEINT = DILATED_PATTERNS[-1][1]
DEINT_ROWS = DEINT * LANES


class _Pass:
    def __init__(self, tq, patterns, unit, seg_len):
        self.tq, self.patterns, self.unit, self.seg_len = tq, patterns, unit, seg_len
        self.win = max(w for w, _ in patterns) // unit
        self.w = self.win + tq
        assert self.win % tq == 0


def _attn_tables(ps):
    i = jnp.arange(ps.tq, dtype=jnp.int32)[:, None]
    j = jnp.arange(ps.w, dtype=jnp.int32)[None, :]
    delta = (i + ps.win - j) * ps.unit
    n = jnp.zeros((ps.tq, ps.w), F32)
    for window, dil in ps.patterns:
        n = n + ((delta >= 0) & (delta <= window) & (delta % dil == 0)).astype(F32)
    logn = jnp.where(n > 0, jnp.log(jnp.maximum(n, 1.0)), NEG)
    return logn, jnp.maximum(delta, 0).astype(F32)


def _slopes(h):
    s = jnp.asarray([2.0 ** (-8.0 * (i + 1) / h) for i in range(h)], F32)
    return jnp.broadcast_to(s[:, None, None], (h, 1, LANES))


def _masked_logn(ps, logn_ref, start):
    col = lax.broadcasted_iota(jnp.int32, (ps.tq, ps.w), 1)
    return jnp.where(col >= ps.win - lax.rem(start, ps.seg_len), logn_ref[...], NEG)


def _head_cols(hh):
    return slice(hh * ATTN_HEAD_DIM, (hh + 1) * ATTN_HEAD_DIM)


def _head_window(refs, cs):
    return jnp.concatenate([r[:, cs] for r in refs], axis=0)


def _head_scores(q_ref, kw, cs, base, dist_ref, slope_ref, hh):
    return _nt(q_ref[:, cs], kw) * (ATTN_HEAD_DIM ** -0.5) + (base - slope_ref[hh][0:1, 0:1] * dist_ref[...])


def _lane_of(stat, hh):
    lane = lax.broadcasted_iota(jnp.int32, stat.shape, 1)
    return jnp.sum(jnp.where(lane == hh, stat, 0.0), axis=1, keepdims=True)


def _window_specs(ps, d, col, nb):
    nprev = ps.win // ps.tq
    return [pl.BlockSpec((ps.tq, d), lambda i, b=b: (jnp.maximum(jnp.minimum(i, nb - 1) - (nprev - b), 0), col))
            for b in range(nprev + 1)]


def _attn_fwd(cfg, ps, qkv, cols, tables, slopes, name):
    s, h, d = cfg.S, cfg.H, cfg.D
    tq, nw = ps.tq, ps.win // ps.tq + 1
    nb = s // tq
    logn, dist = tables
    qc, kc, vc = [c // d for c in cols]

    def body(*refs):
        q_ref, k_refs, v_refs = refs[0], refs[1:1 + nw], refs[1 + nw:1 + 2 * nw]
        logn_ref, dist_ref, slope_ref, o_ref, lse_ref = refs[1 + 2 * nw:]
        base = _masked_logn(ps, logn_ref, pl.program_id(0) * tq)
        lane = lax.broadcasted_iota(jnp.int32, (tq, LANES), 1)

        lse = jnp.zeros((tq, LANES), F32)
        for hh in range(h):
            cs = _head_cols(hh)
            sc = _head_scores(q_ref, _head_window(k_refs, cs), cs, base, dist_ref, slope_ref, hh)
            m = jnp.max(sc, axis=1, keepdims=True)
            p = jnp.exp(sc - m)
            l = jnp.sum(p, axis=1, keepdims=True)
            o_ref[:, cs] = (_nn(p.astype(BF16), _head_window(v_refs, cs)) / l).astype(BF16)
            lse = jnp.where(lane == hh, m + jnp.log(l), lse)
        lse_ref[...] = lse

    tab = pl.BlockSpec((tq, ps.w), lambda i: (0, 0))
    return pl.pallas_call(
        body, out_shape=(SDS((s, d), BF16), SDS((s, LANES), F32)), grid=(nb,),
        in_specs=[pl.BlockSpec((tq, d), lambda i: (i, qc))] + _window_specs(ps, d, kc, nb) + _window_specs(ps, d, vc, nb)
        + [tab, tab, pl.BlockSpec((h, 1, LANES), lambda i: (0, 0, 0))],
        out_specs=(pl.BlockSpec((tq, d), lambda i: (i, 0)), pl.BlockSpec((tq, LANES), lambda i: (i, 0))),
        compiler_params=_params(("parallel",)), name=name)(*([qkv] * (1 + 2 * nw)), logn, dist, slopes)


def _attn_bwd(cfg, ps, qkv, cols, do, lse, delta, tables, slopes, name):
    s, h, d = cfg.S, cfg.H, cfg.D
    tq, nprev = ps.tq, ps.win // ps.tq
    nw = nprev + 1
    nb = s // tq
    logn, dist = tables
    qc, kc, vc = [c // d for c in cols]
    scale = ATTN_HEAD_DIM ** -0.5

    def body(*refs):
        q_ref, k_refs, v_refs = refs[0], refs[1:1 + nw], refs[1 + nw:1 + 2 * nw]
        do_ref, lse_ref, dl_ref, logn_ref, dist_ref, slope_ref, dq_ref, dk_ref, dv_ref, ck, cv = refs[1 + 2 * nw:]
        i = pl.program_id(0)
        slot = (lambda b: 0) if nprev == 1 else (lambda b: lax.rem(i + b, nprev))

        @pl.when(i == 0)
        def _():
            ck[...] = jnp.zeros_like(ck)
            cv[...] = jnp.zeros_like(cv)

        def active(at):
            base = _masked_logn(ps, logn_ref, i * tq)
            lse_all, dl_all = lse_ref[...], dl_ref[...]

            for hh in range(h):
                cs = _head_cols(hh)
                kw, vw = _head_window(k_refs, cs), _head_window(v_refs, cs)
                sc = _head_scores(q_ref, kw, cs, base, dist_ref, slope_ref, hh)
                p = jnp.exp(sc - lse_all[:, hh:hh + 1])
                dob = do_ref[:, cs]
                ds = (p * (_nt(dob, vw) - dl_all[:, hh:hh + 1]) * scale).astype(BF16)
                dq_ref[:, cs] = _nn(ds, kw).astype(BF16)
                dkw = _tn(ds, q_ref[:, cs])
                dvw = _tn(p.astype(BF16), dob)
                dk_ref[:, cs] = ck[at(0), :, cs] + dkw[0:tq]
                dv_ref[:, cs] = cv[at(0), :, cs] + dvw[0:tq]
                for b in range(1, nprev):
                    ck[at(b), :, cs] += dkw[b * tq:(b + 1) * tq]
                    cv[at(b), :, cs] += dvw[b * tq:(b + 1) * tq]
                ck[at(0), :, cs] = dkw[nprev * tq:]
                cv[at(0), :, cs] = dvw[nprev * tq:]

        for first in range(nprev):
            @pl.when((i < nb) & (lax.rem(i, nprev) == first))
            def _():
                active(lambda b: (first + b) % nprev)

        @pl.when(i >= nb)
        def _():
            dk_ref[...] = ck[slot(0)]
            dv_ref[...] = cv[slot(0)]

    here = lambda i: jnp.minimum(i, nb - 1)
    blk = pl.BlockSpec((tq, d), lambda i: (here(i), 0))
    stat = pl.BlockSpec((tq, LANES), lambda i: (here(i), 0))
    late = pl.BlockSpec((tq, d), lambda i: (jnp.maximum(i - nprev, 0), 0))
    tab = pl.BlockSpec((tq, ps.w), lambda i: (0, 0))
    return pl.pallas_call(
        body, out_shape=(SDS((s, d), BF16), SDS((s, d), F32), SDS((s, d), F32)), grid=(nb + nprev,),
        in_specs=[pl.BlockSpec((tq, d), lambda i: (here(i), qc))] + _window_specs(ps, d, kc, nb)
        + _window_specs(ps, d, vc, nb) + [blk, stat, stat, tab, tab, pl.BlockSpec((h, 1, LANES), lambda i: (0, 0, 0))],
        out_specs=(blk, late, late),
        scratch_shapes=[pltpu.VMEM((nprev, tq, d), F32), pltpu.VMEM((nprev, tq, d), F32)],
        compiler_params=_params(("arbitrary",)), name=name)(
            *([qkv] * (1 + 2 * nw)), do, lse, delta, logn, dist, slopes)


def _by_residue(a):
    return a.reshape(DEINT, a.shape[0] // DEINT, a.shape[1])


def _deint_spec(colblock):
    return pl.BlockSpec((DEINT, LANES, LANES), lambda b, j: (0, b, colblock(j)))


def _deint_rows(scr, out_ref, dtype):
    for r in range(DEINT):
        out_ref[r] = scr[pl.ds(r, LANES, stride=DEINT), :].astype(dtype)


def _int_rows(in_ref, scr):
    for r in range(DEINT):
        scr[pl.ds(r, LANES, stride=DEINT), :] = in_ref[r].astype(F32)


WIDE = 4 * LANES


def _wide_spec():
    return pl.BlockSpec((DEINT, LANES, WIDE), lambda b, j: (0, b, j))


def _deinterleave(x, col0, ncols, name):
    s = x.shape[0]
    c0 = col0 // WIDE

    def body(x_ref, o_ref, scr):
        for t in range(WIDE // LANES):
            cs = slice(t * LANES, (t + 1) * LANES)
            scr[t] = x_ref[:, cs].astype(F32)
            for r in range(DEINT):
                o_ref[r, :, cs] = scr.at[t][pl.ds(r, LANES, stride=DEINT), :].astype(x.dtype)

    out = pl.pallas_call(
        body, out_shape=SDS((DEINT, s // DEINT, ncols), x.dtype), grid=(s // DEINT_ROWS, ncols // WIDE),
        in_specs=[pl.BlockSpec((DEINT_ROWS, WIDE), lambda b, j: (b, c0 + j))],
        out_specs=_wide_spec(),
        scratch_shapes=[pltpu.VMEM((WIDE // LANES, DEINT_ROWS, LANES), F32)],
        compiler_params=_params(("parallel", "parallel")), name=name)(x)
    return out.reshape(s, ncols)


def _attn_merge(cfg, proj, o_1, lse_1, o_2, lse_2):
    s, h = cfg.S, cfg.H
    zb = cfg.OZA // WIDE
    rows = DEINT_ROWS
    hps = WIDE // LANES

    def body(o1_ref, l1_ref, o2_ref, l2_ref, z_ref, o_ref, og_ref, lse_ref, so, sl):
        j = pl.program_id(1)

        @pl.when(j == 0)
        def _():
            _int_rows(l2_ref, sl)
            lse_ref[...] = jnp.zeros_like(lse_ref)

        l1_all, l2_all = l1_ref[...], sl[...]
        lane = lax.broadcasted_iota(jnp.int32, (rows, LANES), 1)
        lse = lse_ref[...]
        for t in range(hps):
            hh = j * hps + t
            cs = slice(t * LANES, (t + 1) * LANES)
            for r in range(DEINT):
                so.at[t][pl.ds(r, LANES, stride=DEINT), :] = o2_ref[r, :, cs].astype(F32)
            l1, l2 = _lane_of(l1_all, hh), _lane_of(l2_all, hh)
            mx = jnp.maximum(l1, l2)
            w1, w2 = jnp.exp(l1 - mx), jnp.exp(l2 - mx)
            den = w1 + w2
            o = (w1 * o1_ref[:, cs].astype(F32) + w2 * so[t]) / den
            z = z_ref[:, cs].astype(F32)
            o_ref[:, cs] = o.astype(BF16)
            og_ref[:, cs] = (o * (z * _sigmoid(z))).astype(BF16)
            lse = jnp.where(lane == hh, mx + jnp.log(den), lse)
        lse_ref[...] = lse

    blk = pl.BlockSpec((rows, WIDE), lambda b, j: (b, j))
    stat = pl.BlockSpec((rows, LANES), lambda b, j: (b, 0))
    return pl.pallas_call(
        body, out_shape=(SDS((s, cfg.D), BF16), SDS((s, cfg.D), BF16), SDS((s, LANES), F32)),
        grid=(s // rows, h // hps),
        in_specs=[blk, stat, _wide_spec(), _deint_spec(lambda j: 0), pl.BlockSpec((rows, WIDE), lambda b, j: (b, zb + j))],
        out_specs=(blk, blk, stat),
        scratch_shapes=[pltpu.VMEM((hps, rows, LANES), F32), pltpu.VMEM((rows, LANES), F32)],
        compiler_params=_params(("parallel", "arbitrary")), name="attn_merge")(
            o_1, lse_1, _by_residue(o_2), _by_residue(lse_2), proj)


def _attn_bwd_prep(cfg, proj, o_a, doag, lse, dproj):
    s, h = cfg.S, cfg.H
    zb = cfg.OZA // WIDE
    rows = DEINT_ROWS
    hps = WIDE // LANES

    def body(o_ref, dg_ref, z_ref, lse_ref, dp_in, dz_ref, do_ref, do2_ref, dl_ref, dl2_ref, lse2_ref, scr):
        del dp_in
        j = pl.program_id(1)

        @pl.when(j == 0)
        def _():
            dl_ref[...] = jnp.zeros_like(dl_ref)

        lane = lax.broadcasted_iota(jnp.int32, (rows, LANES), 1)
        dl = dl_ref[...]
        for t in range(hps):
            cs = slice(t * LANES, (t + 1) * LANES)
            z = z_ref[:, cs].astype(F32)
            sg = _sigmoid(z)
            o = o_ref[:, cs].astype(F32)
            dg = dg_ref[:, cs].astype(F32)
            do = dg * (z * sg)
            dz_ref[:, cs] = (dg * o * (sg * (1.0 + z * (1.0 - sg)))).astype(BF16)
            do_ref[:, cs] = do.astype(BF16)
            scr[...] = do
            for r in range(DEINT):
                do2_ref[r, :, cs] = scr[pl.ds(r, LANES, stride=DEINT), :].astype(BF16)
            dl = jnp.where(lane == j * hps + t, jnp.sum(do * o, axis=1, keepdims=True), dl)
        dl_ref[...] = dl

        @pl.when(j == h // hps - 1)
        def _():
            scr[...] = dl
            _deint_rows(scr, dl2_ref, F32)
            scr[...] = lse_ref[...]
            _deint_rows(scr, lse2_ref, F32)

    blk = pl.BlockSpec((rows, WIDE), lambda b, j: (b, j))
    stat = pl.BlockSpec((rows, LANES), lambda b, j: (b, 0))
    stat2 = _deint_spec(lambda j: 0)
    outs = pl.pallas_call(
        body,
        out_shape=(SDS(dproj.shape, BF16), SDS((s, cfg.D), BF16), SDS((DEINT, s // DEINT, cfg.D), BF16),
                   SDS((s, LANES), F32), SDS((DEINT, s // DEINT, LANES), F32), SDS((DEINT, s // DEINT, LANES), F32)),
        grid=(s // rows, h // hps),
        in_specs=[blk, blk, pl.BlockSpec((rows, WIDE), lambda b, j: (b, zb + j)), stat, HBM_SPEC],
        out_specs=(pl.BlockSpec((rows, WIDE), lambda b, j: (b, zb + j)), blk, _wide_spec(), stat, stat2, stat2),
        scratch_shapes=[pltpu.VMEM((rows, LANES), F32)],
        input_output_aliases={4: 0},
        compiler_params=_params(("parallel", "arbitrary")), name="attn_bwd_prep")(o_a, doag, proj, lse, dproj)
    dproj, do, do2, dl, dl2, lse2 = outs
    return dproj, do, do2.reshape(s, cfg.D), dl, dl2.reshape(s, LANES), lse2.reshape(s, LANES)


def _attn_grad_sum(cfg, g_1, g_2, col0, dproj, name):
    s = cfg.S
    c0 = col0 // WIDE
    rows = DEINT_ROWS

    def body(g1_ref, g2_ref, dp_in, o_ref, scr):
        del dp_in
        for t in range(WIDE // LANES):
            cs = slice(t * LANES, (t + 1) * LANES)
            for r in range(DEINT):
                scr.at[t][pl.ds(r, LANES, stride=DEINT), :] = g2_ref[r, :, cs].astype(F32)
            o_ref[:, cs] = (g1_ref[:, cs].astype(F32) + scr[t]).astype(BF16)

    return pl.pallas_call(
        body, out_shape=SDS(dproj.shape, BF16), grid=(s // rows, cfg.D // WIDE),
        in_specs=[pl.BlockSpec((rows, WIDE), lambda b, j: (b, j)), _wide_spec(), HBM_SPEC],
        out_specs=pl.BlockSpec((rows, WIDE), lambda b, j: (b, c0 + j)),
        scratch_shapes=[pltpu.VMEM((WIDE // LANES, rows, LANES), F32)],
        input_output_aliases={2: 0},
        compiler_params=_params(("parallel", "parallel")), name=name)(g_1, _by_residue(g_2), dproj)


CONV_HALO = 16
CONV_TR = 512
CONV_CW = 1024


def _rows_back(a, n):
    return a if n == 0 else pltpu.roll(a, n % a.shape[0], axis=0)


def _conv_fwd(cfg, proj, conv_w, conv_b):
    s, cd = cfg.S, cfg.CD
    tr, cw, hl = CONV_TR, CONV_CW, CONV_HALO
    cb0 = cfg.OXBC // cw

    def body(x_ref, h_ref, w_ref, b_ref, o_ref):
        i = pl.program_id(0)
        halo = jnp.where(i > 0, h_ref[...].astype(F32), 0.0)
        ext = jnp.concatenate([halo, x_ref[...].astype(F32)], axis=0)
        pre = b_ref[...] + jnp.zeros((tr, cw), F32)
        for k in range(CONV_K):
            pre = pre + w_ref[k:k + 1, :] * _rows_back(ext, CONV_K - 1 - k)[hl:]
        o_ref[...] = (pre * _sigmoid(pre)).astype(BF16)

    return pl.pallas_call(
        body, out_shape=SDS((s, cd), BF16), grid=(s // tr, cd // cw),
        in_specs=[pl.BlockSpec((tr, cw), lambda i, j: (i, cb0 + j)),
                  pl.BlockSpec((hl, cw), lambda i, j: (jnp.maximum(i * (tr // hl) - 1, 0), cb0 + j)),
                  pl.BlockSpec((CONV_K, cw), lambda i, j: (0, j)),
                  pl.BlockSpec((1, cw), lambda i, j: (0, j))],
        out_specs=pl.BlockSpec((tr, cw), lambda i, j: (i, j)),
        compiler_params=_params(("parallel", "parallel")), name="conv_fwd")(proj, proj, conv_w, conv_b)


def _conv_bwd(cfg, proj, dact, conv_w, conv_b, dproj):
    s, cd = cfg.S, cfg.CD
    tr, cw, hl = CONV_TR, CONV_CW, CONV_HALO
    cb0 = cfg.OXBC // cw
    nr = s // tr
    last_h = s // hl - 1

    def body(x_ref, hp_ref, hn_ref, d_ref, dn_ref, w_ref, b_ref, dp_in, dx_ref, gw_ref, gb_ref):
        del dp_in
        i = pl.program_id(1)
        ext = jnp.concatenate([jnp.where(i > 0, hp_ref[...].astype(F32), 0.0), x_ref[...].astype(F32),
                               hn_ref[...].astype(F32)], axis=0)
        shifted = [_rows_back(ext, CONV_K - 1 - k)[hl:] for k in range(CONV_K)]
        pre = b_ref[...] + jnp.zeros((tr + hl, cw), F32)
        for k in range(CONV_K):
            pre = pre + w_ref[k:k + 1, :] * shifted[k]
        sg = _sigmoid(pre)
        dact = jnp.concatenate([d_ref[...].astype(F32), jnp.where(i < nr - 1, dn_ref[...].astype(F32), 0.0)], axis=0)
        dpre = dact * (sg * (1.0 + pre * (1.0 - sg)))
        dx = jnp.zeros((tr, cw), F32)
        for k in range(CONV_K):
            dx = dx + w_ref[k:k + 1, :] * _rows_back(dpre, -(CONV_K - 1 - k))[0:tr]
        dx_ref[...] = dx.astype(BF16)

        @pl.when(i == 0)
        def _():
            gw_ref[...] = jnp.zeros_like(gw_ref)
            gb_ref[...] = jnp.zeros_like(gb_ref)

        dcur = dpre[0:tr]
        gb_ref[...] += jnp.sum(dcur, axis=0, keepdims=True)
        for k in range(CONV_K):
            gw_ref[k:k + 1, :] += jnp.sum(dcur * shifted[k][0:tr], axis=0, keepdims=True)

    return pl.pallas_call(
        body, out_shape=(SDS(dproj.shape, BF16), SDS((CONV_K, cd), F32), SDS((1, cd), F32)), grid=(cd // cw, nr),
        in_specs=[pl.BlockSpec((tr, cw), lambda j, i: (i, cb0 + j)),
                  pl.BlockSpec((hl, cw), lambda j, i: (jnp.maximum(i * (tr // hl) - 1, 0), cb0 + j)),
                  pl.BlockSpec((hl, cw), lambda j, i: (jnp.minimum((i + 1) * (tr // hl), last_h), cb0 + j)),
                  pl.BlockSpec((tr, cw), lambda j, i: (i, j)),
                  pl.BlockSpec((hl, cw), lambda j, i: (jnp.minimum((i + 1) * (tr // hl), last_h), j)),
                  pl.BlockSpec((CONV_K, cw), lambda j, i: (0, j)),
                  pl.BlockSpec((1, cw), lambda j, i: (0, j)),
                  pl.BlockSpec(memory_space=pl.ANY)],
        out_specs=(pl.BlockSpec((tr, cw), lambda j, i: (i, cb0 + j)),
                   pl.BlockSpec((CONV_K, cw), lambda j, i: (0, j)),
                   pl.BlockSpec((1, cw), lambda j, i: (0, j))),
        input_output_aliases={7: 0},
        compiler_params=_params(("parallel", "arbitrary")), name="conv_bwd")(
            proj, proj, proj, dact, dact, conv_w, conv_b, dproj)


def _expand(v, e, terms):
    out, rem = None, v
    for _ in range(terms):
        hi = rem.astype(BF16)
        t = _nn(hi, e)
        out = t if out is None else out + t
        rem = rem - hi.astype(F32)
    return out


def _segsum(v, e, terms):
    out, rem = None, v
    for _ in range(terms):
        hi = rem.astype(BF16)
        t = _nt(hi, e)
        out = t if out is None else out + t
        rem = rem - hi.astype(F32)
    return out


def _expand_row(row, e, terms):
    return _expand(jnp.broadcast_to(row, (8, LANES)), e, terms)[0:1]


def _segsum_row(row, e, terms):
    return _segsum(jnp.broadcast_to(row, (8, row.shape[1])), e, terms)[0:1]


def _expansion_matrix(cfg):
    hh = jnp.arange(LANES, dtype=jnp.int32)[:, None]
    cc = jnp.arange(cfg.SI, dtype=jnp.int32)[None, :]
    return (cc // SSM_HEAD_DIM == hh).astype(BF16)


def _tri(lower):
    r = lax.broadcasted_iota(jnp.int32, (CHUNK, CHUNK), 0)
    c = lax.broadcasted_iota(jnp.int32, (CHUNK, CHUNK), 1)
    return (c <= r) if lower else (c >= r)


def _ssd_prep(dtr_ref, db_ref, al_ref, e):
    dtr = dtr_ref[...] + db_ref[...]
    dt = _softplus(dtr)
    a = -jnp.exp(al_ref[...])
    acum = jnp.dot(_tri(True).astype(F32), dt * a, precision=lax.Precision.HIGHEST, preferred_element_type=F32)
    return dtr, dt, a, _expand(dt, e, 2), _expand(acum, e, 3)


def _ssd_fwd(cfg, xact, dt_raw, proj, dt_bias, a_log, d_skip, norm_w, e):
    s, si, cd, gw, bc = cfg.S, cfg.SI, cfg.CD, cfg.GW, cfg.BC
    nc = s // CHUNK
    zb = cfg.OZS // si
    tiles = gw // LANES

    def body(xa_ref, dtr_ref, z_ref, db_ref, al_ref, dsk_ref, nw_ref, e_ref, y_ref, y2_ref, st_ref,
             state, ybuf, x_s, xw_s, ae_s, ea_s, lam_s):
        @pl.when(pl.program_id(0) == 0)
        def _():
            state[...] = jnp.zeros_like(state)

        st_ref[...] = state[...]
        ev = e_ref[...]
        _, _, _, dt_e, a_e = _ssd_prep(dtr_ref, db_ref, al_ref, ev)
        xs = xa_ref[:, 0:si].astype(F32)
        x = xs * dt_e
        lam_e = a_e[CHUNK - 1:CHUNK, :]
        x_s[...] = x.astype(BF16)
        xw_s[...] = (x * jnp.exp(lam_e - a_e)).astype(BF16)
        ae_s[...] = a_e
        ea_s[...] = jnp.exp(a_e)
        ybuf[...] = _expand_row(dsk_ref[...], ev, 3) * xs
        lam_s[...] = jnp.broadcast_to(jnp.exp(lam_e), (8, si))
        tril = _tri(True)
        lane = lax.broadcasted_iota(jnp.int32, (CHUNK, LANES), 1)

        def group(g, carry):
            co = g * gw
            bg = xa_ref[:, pl.ds(si + g * SSM_STATE, SSM_STATE)]
            cg = xa_ref[:, pl.ds(si + bc + g * SSM_STATE, SSM_STATE)]
            cbm = _nt(cg, bg)
            st = state[:, pl.ds(co, gw)]
            yoff = _nn(cg, st.astype(BF16)) * ea_s[:, pl.ds(co, gw)]
            for k in range(tiles):
                tc = co + k * LANES
                at = ae_s[:, pl.ds(tc, LANES)]
                att = at.T
                xt = x_s[:, pl.ds(tc, LANES)]
                acc = yoff[:, k * LANES:(k + 1) * LANES]
                for half in range(2):
                    lo = half * SSM_HEAD_DIM
                    seg = at[:, lo:lo + 1] - att[lo:lo + 1, :]
                    dec = jnp.exp(jnp.where(tril, seg, NEG))
                    xh = jnp.where((lane >= lo) & (lane < lo + SSM_HEAD_DIM), xt, jnp.zeros_like(xt))
                    acc = acc + _nn((cbm * dec).astype(BF16), xh)
                ybuf[:, pl.ds(tc, LANES)] += acc
            state[:, pl.ds(co, gw)] = st * lam_s[0:1, pl.ds(co, gw)] + _tn(bg, xw_s[:, pl.ds(co, gw)])
            return carry

        for g in range(SSM_GROUPS):
            group(g, 0)
        y = ybuf[...]
        y_ref[...] = y.astype(BF16)
        z = z_ref[...].astype(F32)
        u = y * (z * _sigmoid(z))
        r = lax.rsqrt(jnp.mean(u * u, axis=-1, keepdims=True) + RMS_EPS)
        y2_ref[...] = (u * r * nw_ref[...]).astype(BF16)

    row = lambda n: pl.BlockSpec((1, n), lambda c: (0, 0))
    return pl.pallas_call(
        body,
        out_shape=(SDS((s, si), BF16), SDS((s, si), BF16), SDS((nc, SSM_STATE, si), F32)),
        grid=(nc,),
        in_specs=[pl.BlockSpec((CHUNK, cd), lambda c: (c, 0)),
                  pl.BlockSpec((CHUNK, LANES), lambda c: (c, 0)),
                  pl.BlockSpec((CHUNK, si), lambda c: (c, zb)),
                  row(LANES), row(LANES), row(LANES), row(si),
                  pl.BlockSpec((LANES, si), lambda c: (0, 0))],
        out_specs=(pl.BlockSpec((CHUNK, si), lambda c: (c, 0)),
                   pl.BlockSpec((CHUNK, si), lambda c: (c, 0)),
                   pl.BlockSpec((None, SSM_STATE, si), lambda c: (c, 0, 0))),
        scratch_shapes=[pltpu.VMEM((SSM_STATE, si), F32), pltpu.VMEM((CHUNK, si), F32),
                        pltpu.VMEM((CHUNK, si), BF16), pltpu.VMEM((CHUNK, si), BF16),
                        pltpu.VMEM((CHUNK, si), F32), pltpu.VMEM((CHUNK, si), F32),
                        pltpu.VMEM((8, si), F32)],
        compiler_params=_params(("arbitrary",)), name="ssd_fwd")(
            xact, dt_raw, proj, dt_bias, a_log, d_skip, norm_w, e)


def _ssd_bwd(cfg, xact, dt_raw, proj, y, dy2, states, dt_bias, a_log, d_skip, norm_w, e, dproj):
    s, si, cd, gw, bc, hpg = cfg.S, cfg.SI, cfg.CD, cfg.GW, cfg.BC, cfg.HPG
    nc = s // CHUNK
    zb = cfg.OZS // si
    tiles = gw // LANES

    def body(xa_ref, dtr_ref, z_ref, y_ref, d2_ref, st_ref, db_ref, al_ref, dsk_ref, nw_ref, e_ref, dp_in,
             dz_ref, dxa_ref, ddt_ref, gnw_ref, gdb_ref, gal_ref, gds_ref,
             dh, dhn, xs_s, x_s, w_s, ae_s, ea_s, g_s, dx_s, dae_s, r_s, lam_s, dle_s):
        del dp_in

        @pl.when(pl.program_id(0) == 0)
        def _():
            dh[...] = jnp.zeros_like(dh)
            gnw_ref[...] = jnp.zeros_like(gnw_ref)
            gdb_ref[...] = jnp.zeros_like(gdb_ref)
            gal_ref[...] = jnp.zeros_like(gal_ref)
            gds_ref[...] = jnp.zeros_like(gds_ref)

        ev = e_ref[...]
        yv = y_ref[...].astype(F32)
        z = z_ref[...].astype(F32)
        sg = _sigmoid(z)
        sz = z * sg
        u = yv * sz
        r = lax.rsqrt(jnp.mean(u * u, axis=-1, keepdims=True) + RMS_EPS)
        nrm = u * r
        d2 = d2_ref[...].astype(F32)
        gnw_ref[...] += jnp.sum(d2 * nrm, axis=0, keepdims=True)
        gn = d2 * nw_ref[...]
        du = r * (gn - nrm * jnp.mean(gn * nrm, axis=-1, keepdims=True))
        gv = du * sz
        dz_ref[...] = (du * yv * (sg * (1.0 + z * (1.0 - sg)))).astype(BF16)
        g_s[...] = gv

        dtr, dt, a, dt_e, a_e = _ssd_prep(dtr_ref, db_ref, al_ref, ev)
        xs = xa_ref[:, 0:si].astype(F32)
        x = xs * dt_e
        lam_e = a_e[CHUNK - 1:CHUNK, :]
        xs_s[...] = xs
        x_s[...] = x
        w_s[...] = jnp.exp(lam_e - a_e)
        ae_s[...] = a_e
        ea_s[...] = jnp.exp(a_e)
        lam_s[...] = jnp.broadcast_to(jnp.exp(lam_e), (8, si))
        gds_ref[...] += _segsum_row(jnp.sum(gv * xs, axis=0, keepdims=True), ev, 2)
        r_s[...] = jnp.zeros_like(r_s)
        tril = _tri(True)
        lane = lax.broadcasted_iota(jnp.int32, (CHUNK, LANES), 1)
        sub = lax.broadcasted_iota(jnp.int32, (CHUNK, LANES), 0)

        def group(g, carry):
            co = g * gw
            bo = si + g * SSM_STATE
            cof = si + bc + g * SSM_STATE
            cols = pl.ds(co, gw)
            bg = xa_ref[:, pl.ds(bo, SSM_STATE)]
            cg = xa_ref[:, pl.ds(cof, SSM_STATE)]
            cbm = _nt(cg, bg)
            st = st_ref[:, cols]
            stb = st.astype(BF16)
            dho = dh[:, cols]
            dhob = dho.astype(BF16)
            ea = ea_s[:, cols]
            gg = g_s[:, cols]
            xg = x_s[:, cols]
            wg = w_s[:, cols]
            explam = lam_s[0:1, cols]
            yoff = _nn(cg, stb) * ea
            ga = (gg * ea).astype(BF16)
            dc = _nt(ga, stb)
            dhn[:, cols] = dho * explam + _tn(cg, ga)
            bdh = _nn(bg, dhob)
            db = _nt((xg * wg).astype(BF16), dhob)
            t = xg * bdh * wg
            dle_s[0:1, cols] = jnp.sum(t, axis=0, keepdims=True) + explam * jnp.sum(dho * st, axis=0, keepdims=True)
            dae_base = gg * yoff - t
            dxw = wg * bdh
            dcb = jnp.zeros((CHUNK, CHUNK), F32)
            for k in range(tiles):
                tc = co + k * LANES
                ksl = slice(k * LANES, (k + 1) * LANES)
                at = ae_s[:, pl.ds(tc, LANES)]
                att = at.T
                xt = xg[:, ksl].astype(BF16)
                gt = gg[:, ksl].astype(BF16)
                dxt = dxw[:, ksl]
                place = jnp.zeros((CHUNK, LANES), F32)
                for half in range(2):
                    lo = half * SSM_HEAD_DIM
                    seg = at[:, lo:lo + 1] - att[lo:lo + 1, :]
                    dec = jnp.exp(jnp.where(tril, seg, NEG))
                    mh = cbm * dec
                    gh = jnp.where((lane >= lo) & (lane < lo + SSM_HEAD_DIM), gt, jnp.zeros_like(gt))
                    dm = _nt(gh, xt)
                    dxt = dxt + _tn(mh.astype(BF16), gh)
                    dcb = dcb + dm * dec
                    dseg = dm * mh
                    place = place + jnp.where(lane == lo, jnp.sum(dseg, axis=1, keepdims=True), 0.0)
                    hidx = g * hpg + 2 * k + half
                    r_s[...] += jnp.where(sub == hidx, jnp.sum(dseg, axis=0, keepdims=True), 0.0)
                dx_s[:, pl.ds(tc, LANES)] = dxt
                dae_s[:, pl.ds(tc, LANES)] = dae_base[:, ksl] + place
            dcbb = dcb.astype(BF16)
            dxa_ref[:, pl.ds(bo, SSM_STATE)] = (db + _tn(dcbb, cg)).astype(BF16)
            dxa_ref[:, pl.ds(cof, SSM_STATE)] = (dc + _nn(dcbb, bg)).astype(BF16)
            return carry

        for g in range(SSM_GROUPS):
            group(g, 0)
        dlam = _segsum_row(dle_s[0:1, :], ev, 2)
        da_ = _segsum(dae_s[...], ev, 2) - r_s[...].T
        da_ = da_ + jnp.where(sub == CHUNK - 1, dlam, 0.0)
        dda = jnp.dot(_tri(False).astype(F32), da_, precision=lax.Precision.HIGHEST, preferred_element_type=F32)
        dxv = dx_s[...]
        xs = xs_s[...]
        ddt = dda * a + _segsum(dxv * xs, ev, 2)
        gal_ref[...] += jnp.sum(dda * dt, axis=0, keepdims=True) * a
        ddtr = ddt * _sigmoid(dtr)
        gdb_ref[...] += jnp.sum(ddtr, axis=0, keepdims=True)
        ddt_ref[...] = ddtr
        dxa_ref[:, 0:si] = (dxv * dt_e + g_s[...] * _expand_row(dsk_ref[...], ev, 3)).astype(BF16)
        dh[...] = dhn[...]

    rev = lambda c: nc - 1 - c
    row = lambda n: pl.BlockSpec((1, n), lambda c: (0, 0))
    big = lambda: pltpu.VMEM((CHUNK, si), F32)
    return pl.pallas_call(
        body,
        out_shape=(SDS(dproj.shape, BF16), SDS((s, cd), BF16), SDS((s, LANES), F32),
                   SDS((1, si), F32), SDS((1, LANES), F32), SDS((1, LANES), F32), SDS((1, LANES), F32)),
        grid=(nc,),
        in_specs=[pl.BlockSpec((CHUNK, cd), lambda c: (rev(c), 0)),
                  pl.BlockSpec((CHUNK, LANES), lambda c: (rev(c), 0)),
                  pl.BlockSpec((CHUNK, si), lambda c: (rev(c), zb)),
                  pl.BlockSpec((CHUNK, si), lambda c: (rev(c), 0)),
                  pl.BlockSpec((CHUNK, si), lambda c: (rev(c), 0)),
                  pl.BlockSpec((None, SSM_STATE, si), lambda c: (rev(c), 0, 0)),
                  row(LANES), row(LANES), row(LANES), row(si),
                  pl.BlockSpec((LANES, si), lambda c: (0, 0)),
                  pl.BlockSpec(memory_space=pl.ANY)],
        out_specs=(pl.BlockSpec((CHUNK, si), lambda c: (rev(c), zb)),
                   pl.BlockSpec((CHUNK, cd), lambda c: (rev(c), 0)),
                   pl.BlockSpec((CHUNK, LANES), lambda c: (rev(c), 0)),
                   row(si), row(LANES), row(LANES), row(LANES)),
        scratch_shapes=[pltpu.VMEM((SSM_STATE, si), F32), pltpu.VMEM((SSM_STATE, si), F32),
                        big(), big(), big(), big(), big(), big(), big(), big(),
                        pltpu.VMEM((CHUNK, LANES), F32), pltpu.VMEM((8, si), F32), pltpu.VMEM((8, si), F32)],
        input_output_aliases={11: 0},
        compiler_params=_params(("arbitrary",)), name="ssd_bwd")(
            xact, dt_raw, proj, y, dy2, states, dt_bias, a_log, d_skip, norm_w, e, dproj)


MERGE_TR = 512
MERGE_CW = 2048


def _merge_fwd(cfg, proj, a_br, s_br):
    s, d = cfg.S, cfg.D
    tr, cw = MERGE_TR, min(MERGE_CW, d)
    ga0, gs0 = cfg.OGA // cw, cfg.OGS // cw

    def body(ga_ref, gs_ref, a_ref, s_ref, o_ref):
        o_ref[...] = (_sigmoid(ga_ref[...].astype(F32)) * a_ref[...].astype(F32)
                      + _sigmoid(gs_ref[...].astype(F32)) * s_ref[...].astype(F32)).astype(BF16)

    blk = pl.BlockSpec((tr, cw), lambda i, j: (i, j))
    return pl.pallas_call(
        body, out_shape=SDS((s, d), BF16), grid=(s // tr, d // cw),
        in_specs=[pl.BlockSpec((tr, cw), lambda i, j: (i, ga0 + j)),
                  pl.BlockSpec((tr, cw), lambda i, j: (i, gs0 + j)), blk, blk],
        out_specs=blk, compiler_params=_params(("parallel", "parallel")), name="merge_fwd")(proj, proj, a_br, s_br)


def _merge_bwd(cfg, proj, branch, dmerged, gate_off, dproj, name):
    s, d = cfg.S, cfg.D
    tr, cw = MERGE_TR, min(MERGE_CW, d)
    g0 = gate_off // cw
    fresh = dproj is None

    def body(*refs):
        g_ref, b_ref, dm_ref = refs[:3]
        dg_ref, db_ref = refs[-2:]
        dm = dm_ref[...].astype(F32)
        sg = _sigmoid(g_ref[...].astype(F32))
        db_ref[...] = (dm * sg).astype(BF16)
        dg_ref[...] = (dm * b_ref[...].astype(F32) * sg * (1.0 - sg)).astype(BF16)

    blk = pl.BlockSpec((tr, cw), lambda i, j: (i, j))
    gate = pl.BlockSpec((tr, cw), lambda i, j: (i, g0 + j))
    return pl.pallas_call(
        body, out_shape=(SDS((s, cfg.NM), BF16), SDS((s, d), BF16)), grid=(s // tr, d // cw),
        in_specs=[gate, blk, blk] + ([] if fresh else [HBM_SPEC]),
        out_specs=(gate, blk),
        input_output_aliases={} if fresh else {3: 0},
        compiler_params=_params(("parallel", "parallel")), name=name)(
            *((proj, branch, dmerged) + (() if fresh else (dproj,))))


def _outproj_loss(merged, w_out, x, target, fnw):
    s, d = x.shape
    tr = 256

    def body(m_ref, w_ref, x_ref, t_ref, fw_ref, dof_ref, dob_ref, loss_ref, g_ref):
        out = x_ref[...] + _nn(m_ref[...], w_ref[...])
        r = lax.rsqrt(jnp.mean(out * out, axis=-1, keepdims=True) + RMS_EPS)
        nrm = out * r
        fw = fw_ref[...]
        err = nrm * fw - t_ref[...]
        dy = err * (1.0 / d)
        gy = dy * fw
        dout = r * (gy - nrm * jnp.mean(gy * nrm, axis=-1, keepdims=True))
        dof_ref[...] = dout
        dob_ref[...] = dout.astype(BF16)

        @pl.when(pl.program_id(0) == 0)
        def _():
            loss_ref[...] = jnp.zeros_like(loss_ref)
            g_ref[...] = jnp.zeros_like(g_ref)

        loss_ref[...] += jnp.sum(jnp.sum(err * err, axis=1, keepdims=True), axis=0, keepdims=True) * (0.5 / d)
        g_ref[...] += jnp.sum(dy * nrm, axis=0, keepdims=True)

    blk = pl.BlockSpec((tr, d), lambda i: (i, 0))
    return pl.pallas_call(
        body, out_shape=(SDS((s, d), F32), SDS((s, d), BF16), SDS((1, LANES), F32), SDS((1, d), F32)), grid=(s // tr,),
        in_specs=[blk, pl.BlockSpec((d, d), lambda i: (0, 0)), blk, blk, pl.BlockSpec((1, d), lambda i: (0, 0))],
        out_specs=(blk, blk, pl.BlockSpec((1, LANES), lambda i: (0, 0)), pl.BlockSpec((1, d), lambda i: (0, 0))),
        compiler_params=_params(("arbitrary",)), name="outproj_loss")(merged, w_out, x, target, fnw)


ELEMWISE_BLOCK_BYTES = 1 << 20


def _row_block(rows, cols, itemsize=4):
    best = None
    for tr in range(16, rows + 1, 16):
        if rows % tr == 0 and tr * cols * itemsize <= ELEMWISE_BLOCK_BYTES:
            best = tr
    return best if best is not None else rows


def _adamw(w, g, m, v, name):
    rows, cols = w.shape
    tr = _row_block(rows, cols)
    if rows // tr > 64 and cols % LANES == 0:
        blk, grid = pl.BlockSpec((rows, LANES), lambda i: (0, i)), (cols // LANES,)
    else:
        blk, grid = pl.BlockSpec((tr, cols), lambda i: (i, 0)), (rows // tr,)
    out = SDS((rows, cols), F32)
    return pl.pallas_call(
        _adamw_body(), out_shape=(out, out, out), grid=grid, in_specs=[blk] * 4, out_specs=(blk,) * 3,
        compiler_params=_params(("parallel",)), name=name)(w, g, m, v)


def _adamw_body():
    def body(w_ref, g_ref, m_ref, v_ref, d_ref, nm_ref, nv_ref):
        gv = g_ref[...]
        nm = ADAM_B1 * m_ref[...] + (1.0 - ADAM_B1) * gv
        nv = ADAM_B2 * v_ref[...] + (1.0 - ADAM_B2) * jnp.square(gv)
        m_hat = nm / (1.0 - ADAM_B1 ** ADAM_STEP)
        v_hat = nv / (1.0 - ADAM_B2 ** ADAM_STEP)
        d_ref[...] = -ADAM_LR * (m_hat / (jnp.sqrt(v_hat) + ADAM_EPS) + ADAM_WD * w_ref[...])
        nm_ref[...] = nm
        nv_ref[...] = nv

    return body


HBM_SPEC = pl.BlockSpec(memory_space=pl.ANY)


def _position():
    return lax.axis_index("x"), lax.axis_index("y"), lax.axis_index("c")


class _Carry:
    def __init__(self, arrays, out_shapes, sems, start, finish):
        self.arrays, self.out_shapes, self.sems, self.start, self.finish = list(arrays), out_shapes, sems, start, finish

    def sem_shapes(self):
        return [pltpu.SemaphoreType.DMA((k,)) for k in self.sems]


def _gather_carry(shards, by_cols=()):
    n = len(shards)

    def copies(ins, outs, sems):
        send_sems, recv_sems, fsend_sems, frecv_sems = sems
        x, y, c = _position()
        me = 2 * x + y
        peers = [(1 - x, y), (x, 1 - y), (1 - x, 1 - y)]

        def half_of(t, chip, half):
            if t in by_cols:
                c2 = ins[t].shape[1] // 2
                return outs[t].at[chip, :, pl.ds(half * c2, c2)]
            return outs[t].at[chip, half]

        def over_ici(t, p, chip):
            px, py = peers[p]
            if t in by_cols:
                c2 = ins[t].shape[1] // 2
                src = ins[t].at[:, pl.ds(c * c2, c2)]
            else:
                r2 = ins[t].shape[0] // 2
                src = ins[t].at[pl.ds(c * r2, r2), :]
            return pltpu.make_async_remote_copy(
                src_ref=src, dst_ref=half_of(t, chip, c), send_sem=send_sems.at[3 * t + p],
                recv_sem=recv_sems.at[3 * t + p], device_id=(px, py, c), device_id_type=MESH)

        def to_sibling(t, p, half):
            px, py = peers[p]
            slab = half_of(t, 2 * px + py, half)
            return pltpu.make_async_remote_copy(
                src_ref=slab, dst_ref=slab, send_sem=fsend_sems.at[3 * t + p], recv_sem=frecv_sems.at[3 * t + p],
                device_id=(x, y, 1 - c), device_id_type=MESH)

        pairs = [(t, p) for t in range(n) for p in range(3)]
        sends = [over_ici(t, p, me) for t, p in pairs]
        lands = [over_ici(t, p, 2 * peers[p][0] + peers[p][1]) for t, p in pairs]
        passed = [to_sibling(t, p, c) for t, p in pairs]
        from_sibling = [to_sibling(t, p, 1 - c) for t, p in pairs]
        return sends, lands, passed, from_sibling

    def start(ins, outs, sems):
        for cp in copies(ins, outs, sems)[0]:
            cp.start()

    def finish(ins, outs, sems):
        sends, lands, passed, from_sibling = copies(ins, outs, sems)
        for land, fwd in zip(lands, passed):
            land.wait_recv()
            fwd.start()
        for cp in from_sibling:
            cp.wait_recv()
        for cp in sends + passed:
            cp.wait_send()

    shapes = [SDS((N_CHIPS,) + a.shape if t in by_cols else (N_CHIPS, 2, a.shape[0] // 2, a.shape[1]), a.dtype)
              for t, a in enumerate(shards)]
    return _Carry(shards, shapes, [3 * n] * 4, start, finish)


def _scatter_carry(parts):
    def start(ins, outs, sems):
        for cp in _scatter_copies(ins, outs, *sems)[0]:
            cp.start()

    def finish(ins, outs, sems):
        sends, lands = _scatter_copies(ins, outs, *sems)
        for cp in lands:
            cp.wait_recv()
        for cp in sends:
            cp.wait_send()

    return _Carry(parts, [SDS(a.shape, a.dtype) for a in parts], [3 * len(parts)] * 2, start, finish)


def _with_own(gathered, own, chip):
    full = gathered.reshape((N_CHIPS,) + own.shape)
    return lax.dynamic_update_index_in_dim(full, own, chip, 0)


def _exchange_halves(grads):
    n = len(grads)
    slabs = [list(g) if isinstance(g, (list, tuple)) else [g] for g in grads]
    flat = [a for s in slabs for a in s]
    ncp = len(flat)

    def body(*refs):
        ins, outs = refs[:ncp], refs[ncp:ncp + n]
        send_sems, recv_sems = refs[ncp + n:]
        x, y, c = _position()
        cps, k = [], 0
        for t in range(n):
            for j in range(len(slabs[t])):
                if len(slabs[t]) == 1:
                    r2 = ins[k].shape[1] // 2
                    src, dst = ins[k].at[:, pl.ds((1 - c) * r2, r2), :], outs[t]
                else:
                    r2 = ins[k].shape[0] // 2
                    src, dst = ins[k].at[pl.ds((1 - c) * r2, r2), :], outs[t].at[j]
                cps.append(pltpu.make_async_remote_copy(
                    src_ref=src, dst_ref=dst, send_sem=send_sems.at[k], recv_sem=recv_sems.at[k],
                    device_id=(x, y, 1 - c), device_id_type=MESH))
                k += 1
        for cp in cps:
            cp.start()
        for cp in cps:
            cp.wait()

    def landing(s):
        a = s[0]
        return SDS((N_CHIPS, a.shape[-2] // 2, a.shape[-1]), a.dtype)

    return pl.pallas_call(
        body, out_shape=[landing(s) for s in slabs],
        in_specs=[HBM_SPEC] * ncp, out_specs=[HBM_SPEC] * n,
        scratch_shapes=[pltpu.SemaphoreType.DMA((ncp,)), pltpu.SemaphoreType.DMA((ncp,))],
        compiler_params=pltpu.CompilerParams(has_side_effects=True), name="reduce_sibling")(*flat)


def _scatter_copies(ins, outs, send_sems, recv_sems):
    x, y, c = _position()
    me = 2 * x + y
    peers = [(1 - x, y), (x, 1 - y), (1 - x, 1 - y)]

    def remote(t, p, src_slab, dst_slab):
        px, py = peers[p]
        return pltpu.make_async_remote_copy(
            src_ref=ins[t].at[src_slab], dst_ref=outs[t].at[dst_slab], send_sem=send_sems.at[3 * t + p],
            recv_sem=recv_sems.at[3 * t + p], device_id=(px, py, c), device_id_type=MESH)

    n = len(ins)
    sends = [remote(t, p, 2 * peers[p][0] + peers[p][1], me) for t in range(n) for p in range(3)]
    lands = [remote(t, p, me, 2 * peers[p][0] + peers[p][1]) for t in range(n) for p in range(3)]
    return sends, lands


def _share_halves(halves):
    n = len(halves)

    def body(*refs):
        ins, outs = refs[:n], refs[n:2 * n]
        send_sems, recv_sems = refs[2 * n:]
        x, y, c = _position()

        def copy(t, slab):
            return pltpu.make_async_remote_copy(
                src_ref=ins[t].at[slab], dst_ref=outs[t].at[slab], send_sem=send_sems.at[t], recv_sem=recv_sems.at[t],
                device_id=(x, y, 1 - c), device_id_type=MESH)

        for t in range(n):
            copy(t, c).start()
        for t in range(n):
            copy(t, 1 - c).wait_recv()
        for t in range(n):
            copy(t, c).wait_send()

    return pl.pallas_call(
        body, out_shape=[SDS(a.shape, a.dtype) for a in halves],
        in_specs=[HBM_SPEC] * n, out_specs=[HBM_SPEC] * n,
        scratch_shapes=[pltpu.SemaphoreType.DMA((n,)), pltpu.SemaphoreType.DMA((n,))],
        input_output_aliases={t: t for t in range(n)},
        compiler_params=pltpu.CompilerParams(has_side_effects=True), name="share_sibling")(*halves)


def _add_sibling(grad, recv, core):
    nch, r2, cols = recv.shape
    tr = _row_block(r2, cols)
    nb = r2 // tr

    def body(c_ref, g_ref, r_ref, o_ref):
        del c_ref
        o_ref[...] = (g_ref[...].astype(F32) + r_ref[...].astype(F32)).astype(BF16)

    return pl.pallas_call(
        body, out_shape=SDS(recv.shape, BF16),
        grid_spec=pltpu.PrefetchScalarGridSpec(
            num_scalar_prefetch=1, grid=(nch, nb),
            in_specs=[pl.BlockSpec((None, tr, cols), lambda j, i, c_ref: (j, c_ref[0] * nb + i, 0)),
                      pl.BlockSpec((None, tr, cols), lambda j, i, c_ref: (j, i, 0))],
            out_specs=pl.BlockSpec((None, tr, cols), lambda j, i, c_ref: (j, i, 0))),
        compiler_params=_params(("parallel", "parallel")), name="add_sibling")(core, grad, recv)


def _add_chips(own, recv, chip_core):
    nch, r2, cols = recv.shape
    tr = _row_block(r2, cols)

    nsc = 2 + nch

    def body(*refs):
        me = refs[0][0]
        own_ref, p_refs, o_ref = refs[nsc], refs[nsc + 1:nsc + 1 + nch], refs[nsc + 1 + nch]
        acc = None
        for j in range(nch):
            term = jnp.where(me == j, own_ref[...], p_refs[j][...]).astype(F32)
            acc = term if acc is None else acc + term
        o_ref[...] = acc

    def slab(j):
        return pl.BlockSpec((None, tr, cols), lambda i, *sc: (sc[2 + j][0], i, 0))

    return pl.pallas_call(
        body, out_shape=SDS((2, r2, cols), F32),
        grid_spec=pltpu.PrefetchScalarGridSpec(
            num_scalar_prefetch=nsc, grid=(r2 // tr,),
            in_specs=[pl.BlockSpec((None, tr, cols), lambda i, *sc: (sc[0][0], i, 0))] + [slab(j) for j in range(nch)],
            out_specs=pl.BlockSpec((None, tr, cols), lambda i, *sc: (sc[1][0], i, 0))),
        compiler_params=_params(("parallel",)), name="add_chips")(*chip_core, own, *([recv] * nch))


def _exchange_col_halves(grad):
    nch, r, cols = grad.shape
    c2 = cols // 2

    def body(in_ref, out_ref, send_sem, recv_sem):
        x, y, c = _position()
        cp = pltpu.make_async_remote_copy(
            src_ref=in_ref.at[:, :, pl.ds((1 - c) * c2, c2)], dst_ref=out_ref, send_sem=send_sem.at[0],
            recv_sem=recv_sem.at[0], device_id=(x, y, 1 - c), device_id_type=MESH)
        cp.start()
        cp.wait()

    return pl.pallas_call(
        body, out_shape=SDS((nch, r, c2), grad.dtype), in_specs=[HBM_SPEC], out_specs=HBM_SPEC,
        scratch_shapes=[pltpu.SemaphoreType.DMA((1,)), pltpu.SemaphoreType.DMA((1,))],
        compiler_params=pltpu.CompilerParams(has_side_effects=True), name="reduce_sibling_cols")(grad)


def _add_sibling_cols(grad, recv, core):
    nch, r, c2 = recv.shape
    nb = c2 // LANES

    def body(c_ref, g_ref, r_ref, o_ref):
        del c_ref
        o_ref[...] = (g_ref[...].astype(F32) + r_ref[...].astype(F32)).astype(BF16)

    blk = pl.BlockSpec((None, r, LANES), lambda j, i, c_ref: (j, 0, i))
    return pl.pallas_call(
        body, out_shape=SDS(recv.shape, BF16),
        grid_spec=pltpu.PrefetchScalarGridSpec(
            num_scalar_prefetch=1, grid=(nch, nb),
            in_specs=[pl.BlockSpec((None, r, LANES), lambda j, i, c_ref: (j, 0, c_ref[0] * nb + i)), blk],
            out_specs=blk),
        compiler_params=_params(("parallel", "parallel")), name="add_sibling_cols")(core, grad, recv)


def _add_chips_cols(own, recv, chip_core):
    nch, r, c2 = recv.shape
    nb = c2 // LANES
    nsc = 2 + nch

    def body(*refs):
        me = refs[0][0]
        own_ref, p_refs, o_ref = refs[nsc], refs[nsc + 1:nsc + 1 + nch], refs[nsc + 1 + nch]
        acc = None
        for j in range(nch):
            term = jnp.where(me == j, own_ref[...], p_refs[j][...]).astype(F32)
            acc = term if acc is None else acc + term
        o_ref[...] = acc

    def slab(j):
        return pl.BlockSpec((None, r, LANES), lambda i, *sc: (sc[2 + j][0], 0, i))

    return pl.pallas_call(
        body, out_shape=SDS((r, 2 * c2), F32),
        grid_spec=pltpu.PrefetchScalarGridSpec(
            num_scalar_prefetch=nsc, grid=(nb,),
            in_specs=[pl.BlockSpec((None, r, LANES), lambda i, *sc: (sc[0][0], 0, i))] + [slab(j) for j in range(nch)],
            out_specs=pl.BlockSpec((r, LANES), lambda i, *sc: (0, sc[1][0] * nb + i))),
        compiler_params=_params(("parallel",)), name="add_chips_cols")(*chip_core, own, *([recv] * nch))


def _share_col_halves(full):
    r, cols = full.shape
    c2 = cols // 2

    def body(in_ref, out_ref, send_sem, recv_sem):
        x, y, c = _position()

        def copy(half):
            return pltpu.make_async_remote_copy(
                src_ref=in_ref.at[:, pl.ds(half * c2, c2)], dst_ref=out_ref.at[:, pl.ds(half * c2, c2)],
                send_sem=send_sem.at[0], recv_sem=recv_sem.at[0], device_id=(x, y, 1 - c), device_id_type=MESH)

        copy(c).start()
        copy(1 - c).wait_recv()
        copy(c).wait_send()

    return pl.pallas_call(
        body, out_shape=SDS(full.shape, full.dtype), in_specs=[HBM_SPEC], out_specs=HBM_SPEC,
        scratch_shapes=[pltpu.SemaphoreType.DMA((1,)), pltpu.SemaphoreType.DMA((1,))],
        input_output_aliases={0: 0},
        compiler_params=pltpu.CompilerParams(has_side_effects=True), name="share_sibling_cols")(full)


def _allreduce_small(pack):
    rows = pack.shape[0]

    def body(p_ref, o_ref, buf, send_sems, recv_sems):
        x, y, c = _position()
        me = 4 * x + 2 * y + c
        buf[me] = p_ref[...]

        def copy(dst_dev, slot):
            return pltpu.make_async_remote_copy(
                src_ref=p_ref, dst_ref=buf.at[slot], send_sem=send_sems.at[dst_dev], recv_sem=recv_sems.at[slot],
                device_id=(dst_dev // 4, (dst_dev // 2) % 2, dst_dev % 2), device_id_type=MESH)

        for dev in range(N_DEV):
            @pl.when(dev != me)
            def _():
                copy(dev, me).start()
        for dev in range(N_DEV):
            @pl.when(dev != me)
            def _():
                copy(dev, dev).wait_recv()
        for dev in range(N_DEV):
            @pl.when(dev != me)
            def _():
                copy(dev, me).wait_send()
        acc = buf[0]
        for dev in range(1, N_DEV):
            acc = acc + buf[dev]
        o_ref[...] = acc

    return pl.pallas_call(
        body, out_shape=SDS(pack.shape, F32),
        in_specs=[pl.BlockSpec(memory_space=pltpu.VMEM)], out_specs=pl.BlockSpec(memory_space=pltpu.VMEM),
        scratch_shapes=[pltpu.VMEM((N_DEV, rows, LANES), F32), pltpu.SemaphoreType.DMA((N_DEV,)),
                        pltpu.SemaphoreType.DMA((N_DEV,))],
        compiler_params=pltpu.CompilerParams(has_side_effects=True), name="allreduce_small")(pack)


ATTN_TQ = 256


def _local_step(cfg, x, target, w, to_chips=None, late=None, hn=None):
    d = cfg.D
    if hn is None:
        hn = _rmsnorm_fwd(x, w["norm_w"])
    proj = _mm(hn, w["w_main_t"], "nt", BF16, "proj_main", carry=late[0] if late else None, b_rows=cfg.NM)
    if late:
        proj, arrived = proj
        w = {**w, **late[1](arrived)}
    dt_raw = _mm(hn, w["w_dt_t"], "nt", F32, "proj_dt")
    slopes = _slopes(cfg.H)
    near = _Pass(ATTN_TQ, DILATED_PATTERNS[:-1], 1, cfg.S)
    far = _Pass(LANES, DILATED_PATTERNS[-1:], DEINT, cfg.S // DEINT)
    tab_near, tab_far = _attn_tables(near), _attn_tables(far)
    cols_near, cols_far = (cfg.OQ, cfg.OK, cfg.OV), (0, d, 2 * d)
    qkv_far = _deinterleave(proj, 0, 3 * d, "attn_deinterleave")
    o_1, lse_1 = _attn_fwd(cfg, near, proj, cols_near, tab_near, slopes, "attn_fwd_near")
    o_2, lse_2 = _attn_fwd(cfg, far, qkv_far, cols_far, tab_far, slopes, "attn_fwd_far")
    o_a, oag, lse = _attn_merge(cfg, proj, o_1, lse_1, o_2, lse_2)
    xact = _conv_fwd(cfg, proj, w["conv_w"], w["conv_b"])
    e = _expansion_matrix(cfg)
    y, y2, states = _ssd_fwd(cfg, xact, dt_raw, proj, w["dt_bias"], w["a_log"], w["d_skip"], w["ssm_norm_w"], e)
    a_br = _mm(oag, w["w_attn"], "nn", BF16, "branch_attn")
    s_br = _mm(y2, w["w_ssm"], "nn", BF16, "branch_ssm")
    merged = _merge_fwd(cfg, proj, a_br, s_br)
    dout_f, dout_b, loss_row, g_fnw = _outproj_loss(merged, w["w_out"], x, target, w["final_norm_w"])

    dmerged = _mm(dout_b, w["w_out"], "nt", BF16, "d_merged")
    g_w_out = _mm(merged, dout_b, "tn", BF16, "g_w_out")
    dproj, da_br = _merge_bwd(cfg, proj, a_br, dmerged, cfg.OGA, None, "merge_bwd_attn")
    dproj, ds_br = _merge_bwd(cfg, proj, s_br, dmerged, cfg.OGS, dproj, "merge_bwd_ssm")
    doag = _mm(da_br, w["w_attn"], "nt", BF16, "d_oag")
    g_w_attn = _mm(oag, da_br, "tn", BF16, "g_w_attn")
    dy2 = _mm(ds_br, w["w_ssm"], "nt", BF16, "d_y2")
    g_w_ssm = _mm(y2, ds_br, "tn", BF16, "g_w_ssm")
    dproj, dxact, ddt, g_snw, g_dtb, g_alog, g_dsk = _ssd_bwd(
        cfg, xact, dt_raw, proj, y, dy2, states, w["dt_bias"], w["a_log"], w["d_skip"], w["ssm_norm_w"], e, dproj)
    dproj, g_cw, g_cb = _conv_bwd(cfg, proj, dxact, w["conv_w"], w["conv_b"], dproj)
    dproj, do, do_far, dl, dl_far, lse_far = _attn_bwd_prep(cfg, proj, o_a, doag, lse, dproj)
    g_near = _attn_bwd(cfg, near, proj, cols_near, do, lse, dl, tab_near, slopes, "attn_bwd_near")
    g_far = _attn_bwd(cfg, far, qkv_far, cols_far, do_far, lse_far, dl_far, tab_far, slopes, "attn_bwd_far")
    for g_1, g_2, col0, nm in zip(g_near, g_far, cols_near, ("attn_dq", "attn_dk", "attn_dv")):
        dproj = _attn_grad_sum(cfg, g_1, g_2, col0, dproj, nm)
    ddt_b = ddt.astype(BF16)
    g_w_main = _mm(dproj, hn, "tn", BF16, "g_w_main", out_rows=cfg.N_IN)
    g_w_dt = _mm(ddt_b, hn, "tn", BF16, "g_w_dt")
    grads = dict(w_main_t=g_w_main, w_dt_t=g_w_dt, conv_w=g_cw, conv_b=g_cb, dt_bias=g_dtb, a_log=g_alog,
                 d_skip=g_dsk, ssm_norm_w=g_snw, w_attn=g_w_attn, w_ssm=g_w_ssm, w_out=g_w_out, final_norm_w=g_fnw)
    sent = to_chips(grads) if to_chips is not None else ()
    dhn = _mm(dproj, w["w_main_t"], "nn", F32, "d_hn", tk=1024, carry=_scatter_carry(sent) if sent else None,
              b_rows=cfg.NM)
    landed = ()
    if sent:
        dhn, landed = dhn
    dhn_dt = _mm(ddt_b, w["w_dt_t"], "nn", F32, "d_hn_dt")
    grad_x, grads["norm_w"] = _rmsnorm_bwd(x, w["norm_w"], dhn, dhn_dt, dout_f)
    return loss_row, grad_x, grads, sent, landed


def _pad_lanes(v):
    return jnp.pad(v, ((0, 0), (0, LANES - v.shape[1])))


def _main_from_rows(cfg, w_in_t):
    lo, hi = cfg.OGA, cfg.OGA + cfg.NH
    dt = jnp.pad(w_in_t[lo:hi], ((0, LANES - cfg.NH), (0, 0)))
    return lax.dynamic_update_slice(w_in_t, w_in_t[hi:], (lo, 0)), dt


def _rows_from_main(cfg, g_main_t, g_dt_t):
    lo, hi = cfg.OGA, cfg.OGA + cfg.NH
    g = lax.dynamic_update_slice(g_main_t, g_main_t[lo:cfg.NM], (hi, 0))
    return lax.dynamic_update_slice(g, g_dt_t[:cfg.NH], (lo, 0))


def _full_weights(cfg, norm_w, w_in_t, conv_w, conv_b, dt_bias, a_log, d_skip, ssm_norm_w, w_attn, w_ssm, w_out, fnw):
    w_main, w_dt = _main_from_rows(cfg, w_in_t)
    return dict(norm_w=norm_w, w_main_t=w_main.astype(BF16), w_dt_t=w_dt.astype(BF16), conv_w=conv_w, conv_b=conv_b,
                dt_bias=_pad_lanes(dt_bias), a_log=_pad_lanes(a_log), d_skip=_pad_lanes(d_skip), ssm_norm_w=ssm_norm_w,
                final_norm_w=fnw, **{k: v.astype(BF16) for k, v in (("w_attn", w_attn), ("w_ssm", w_ssm), ("w_out", w_out))
                                     if v is not None})


def kernel(x, norm_w, w_in, conv_w, conv_b, dt_bias, a_log, d_skip, ssm_norm_w, w_attn_branch, w_ssm_branch, w_out, final_norm_w, loss_target, m_norm_w, m_w_in, m_conv_w, m_conv_b, m_dt_bias, m_a_log, m_d_skip, m_ssm_norm_w, m_w_attn_branch, m_w_ssm_branch, m_w_out, m_final_norm_w, v_norm_w, v_w_in, v_conv_w, v_conv_b, v_dt_bias, v_a_log, v_d_skip, v_ssm_norm_w, v_w_attn_branch, v_w_ssm_branch, v_w_out, v_final_norm_w):
    cfg = _Cfg(x.shape[1], x.shape[2])
    d, si, cd, nh = cfg.D, cfg.SI, cfg.CD, cfg.NH
    chip = 2 * lax.axis_index("x") + lax.axis_index("y")
    core = lax.axis_index("c").astype(jnp.int32).reshape(1)
    chip = chip.astype(jnp.int32)
    chip_core = [chip.reshape(1), core] + [jnp.where(chip == j, (j + 1) % N_CHIPS, j).astype(jnp.int32).reshape(1)
                                           for j in range(N_CHIPS)]

    own = [jnp.transpose(w_in[0]).astype(BF16), conv_w[0].reshape(4 * CONV_K, -1)]
    hn, gathered = _rmsnorm_fwd(x[0], norm_w, carry=_gather_carry(own, by_cols=(0,)))
    a_in, a_cw = [_with_own(g, o, chip) for g, o in zip(gathered, own)]
    conv_w_full = a_cw.reshape(N_CHIPS, CONV_K, cd // N_CHIPS).transpose(1, 0, 2).reshape(CONV_K, cd)
    w = _full_weights(cfg, norm_w, a_in.reshape(cfg.N_IN, d), conv_w_full, conv_b, dt_bias, a_log, d_skip,
                      ssm_norm_w, None, None, None, final_norm_w.reshape(1, d))
    own_late = [w_attn_branch[0].astype(BF16), w_ssm_branch[0].astype(BF16), w_out[0].astype(BF16)]

    def late_weights(arrived):
        a_attn, a_ssm, a_out = [_with_own(g, o, chip) for g, o in zip(arrived, own_late)]
        return dict(w_attn=a_attn.reshape(d, d), w_ssm=a_ssm.reshape(si, d), w_out=a_out.reshape(d, d))

    def to_chips(grads):
        g_in_t = _rows_from_main(cfg, grads["w_main_t"], grads["w_dt_t"]).reshape(N_CHIPS, cfg.N_IN // N_CHIPS, d)
        by_chip = [grads["w_attn"].reshape(N_CHIPS, d // N_CHIPS, d),
                   grads["w_ssm"].reshape(N_CHIPS, si // N_CHIPS, d),
                   grads["w_out"].reshape(N_CHIPS, d // N_CHIPS, d)]
        from_sibling = _exchange_halves(by_chip)
        return ([_add_sibling_cols(g_in_t, _exchange_col_halves(g_in_t), core)]
                + [_add_sibling(g, r, core) for g, r in zip(by_chip, from_sibling)])

    loss_row, grad_x, grads, chip_sums, from_chips = _local_step(
        cfg, x[0], loss_target[0], w, to_chips, (_gather_carry(own_late), late_weights), hn)
    g_in_t = _share_col_halves(_add_chips_cols(chip_sums[0], from_chips[0], chip_core))
    halves = [_add_chips(o, p, chip_core) for o, p in zip(chip_sums[1:], from_chips[1:])]
    g_attn, g_ssm, g_out = [h.reshape(2 * h.shape[1], h.shape[2]) for h in _share_halves(halves)]
    g_in = jnp.transpose(g_in_t)

    small = [loss_row, grads["norm_w"], grads["conv_b"], grads["dt_bias"], grads["a_log"], grads["d_skip"],
             grads["ssm_norm_w"], grads["final_norm_w"], grads["conv_w"].reshape(1, CONV_K * cd)]
    sizes = [a.shape[1] for a in small]
    total = sum(sizes)
    rows = -(-total // (8 * LANES)) * 8
    flat = jnp.pad(jnp.concatenate(small, axis=1), ((0, 0), (0, rows * LANES - total)))
    red = _allreduce_small(flat.reshape(rows, LANES)).reshape(1, rows * LANES)
    offs = [sum(sizes[:i]) for i in range(len(sizes))]
    loss_r, g_nw, g_cb, g_dtb, g_alog, g_dsk, g_snw, g_fnw, g_cw_flat = [
        red[:, o:o + n] for o, n in zip(offs, sizes)]
    loss = loss_r[0, 0]
    g_dtb, g_alog, g_dsk = g_dtb[:, :nh], g_alog[:, :nh], g_dsk[:, :nh]
    cshard = cd // N_CHIPS
    g_cw = lax.dynamic_slice_in_dim(g_cw_flat.reshape(CONV_K, cd), chip * cshard, cshard, axis=1)

    upd = {}
    upd["w_in"] = tuple(jnp.transpose(u) for u in _adamw(
        jnp.transpose(w_in[0]), g_in_t, jnp.transpose(m_w_in[0]), jnp.transpose(v_w_in[0]), "adamw_w_in"))
    for name, wv, gv, mv, vv in [("w_attn", w_attn_branch[0], g_attn, m_w_attn_branch[0], v_w_attn_branch[0]),
                                 ("w_ssm", w_ssm_branch[0], g_ssm, m_w_ssm_branch[0], v_w_ssm_branch[0]),
                                 ("w_out", w_out[0], g_out, m_w_out[0], v_w_out[0])]:
        upd[name] = _adamw(wv, gv, mv, vv, "adamw_" + name)
    names = ["norm_w", "conv_w", "conv_b", "dt_bias", "a_log", "d_skip", "ssm_norm_w", "final_norm_w"]
    ws = [norm_w, conv_w[0].reshape(1, -1), conv_b, dt_bias, a_log, d_skip, ssm_norm_w, final_norm_w.reshape(1, d)]
    gs = [g_nw, g_cw.reshape(1, -1), g_cb, g_dtb, g_alog, g_dsk, g_snw, g_fnw]
    ms = [m_norm_w, m_conv_w[0].reshape(1, -1), m_conv_b, m_dt_bias, m_a_log, m_d_skip, m_ssm_norm_w,
          m_final_norm_w.reshape(1, d)]
    vs = [v_norm_w, v_conv_w[0].reshape(1, -1), v_conv_b, v_dt_bias, v_a_log, v_d_skip, v_ssm_norm_w,
          v_final_norm_w.reshape(1, d)]
    ssz = [a.shape[1] for a in ws]
    stot = sum(ssz)
    srows = -(-stot // (8 * LANES)) * 8

    def pack(parts):
        return jnp.pad(jnp.concatenate(parts, axis=1), ((0, 0), (0, srows * LANES - stot))).reshape(srows, LANES)

    packed = _adamw(pack(ws), pack(gs), pack(ms), pack(vs), "adamw_small")
    soffs = [sum(ssz[:i]) for i in range(len(ssz))]
    for k, nm in enumerate(names):
        upd[nm] = tuple(p.reshape(1, srows * LANES)[:, soffs[k]:soffs[k] + ssz[k]] for p in packed)

    shapes = dict(norm_w=norm_w.shape, w_in=w_in.shape, conv_w=conv_w.shape, conv_b=conv_b.shape, dt_bias=dt_bias.shape,
                  a_log=a_log.shape, d_skip=d_skip.shape, ssm_norm_w=ssm_norm_w.shape, w_attn=w_attn_branch.shape,
                  w_ssm=w_ssm_branch.shape, w_out=w_out.shape, final_norm_w=final_norm_w.shape)
    order = ["norm_w", "w_in", "conv_w", "conv_b", "dt_bias", "a_log", "d_skip", "ssm_norm_w", "w_attn", "w_ssm",
             "w_out", "final_norm_w"]
    gradv = dict(norm_w=g_nw, w_in=g_in, conv_w=g_cw, conv_b=g_cb, dt_bias=g_dtb, a_log=g_alog, d_skip=g_dsk,
                 ssm_norm_w=g_snw, w_attn=g_attn, w_ssm=g_ssm, w_out=g_out, final_norm_w=g_fnw)
    outs = [loss, grad_x[None]]
    outs += [gradv[n].reshape(shapes[n]) for n in order]
    for k in range(3):
        outs += [upd[n][k].reshape(shapes[n]) for n in order]
    return tuple(outs)
```

```python
import jax
import jax.numpy as jnp
from jax import lax
from jax.experimental import pallas as pl
from jax.experimental.pallas import tpu as pltpu

F32 = jnp.float32
BF16 = jnp.bfloat16
SDS = jax.ShapeDtypeStruct

RMS_EPS = 1e-6
LANES = 128
CHUNK = 128
SSM_HEAD_DIM = 64
SSM_GROUPS = 8
SSM_STATE = 128
CONV_K = 4
ATTN_HEAD_DIM = 128
DILATED_PATTERNS = ((128, 1), (512, 4), (2048, 16))
NEG = -1e30
VMEM_LIMIT = 56 * 1024 * 1024
ADAM_LR, ADAM_B1, ADAM_B2, ADAM_EPS, ADAM_WD, ADAM_STEP = 0.001, 0.9, 0.999, 1e-08, 0.01, 10
MESH = pl.DeviceIdType.MESH
N_CHIPS = 4
N_DEV = 8


class _Cfg:
    def __init__(self, s, d):
        self.S, self.D = s, d
        self.H = d // ATTN_HEAD_DIM
        self.SI = 2 * d
        self.NH = self.SI // SSM_HEAD_DIM
        self.HPG = self.NH // SSM_GROUPS
        self.GW = self.HPG * SSM_HEAD_DIM
        self.BC = SSM_GROUPS * SSM_STATE
        self.CD = self.SI + 2 * self.BC
        self.OQ, self.OK, self.OV, self.OZA = 0, d, 2 * d, 3 * d
        self.OZS = 4 * d
        self.OXBC = self.OZS + self.SI
        self.OGA = self.OXBC + self.CD
        self.OGS = self.OGA + d
        self.NM = self.OGS + d
        self.N_IN = self.NM + self.NH
        assert self.GW % LANES == 0 and self.NH <= LANES and s % 512 == 0 and d % 512 == 0


def _params(sem=None):
    return pltpu.CompilerParams(dimension_semantics=sem, vmem_limit_bytes=VMEM_LIMIT)


def _sigmoid(x):
    return 0.5 * jnp.tanh(0.5 * x) + 0.5


def _softplus(x):
    u = jnp.exp(-jnp.abs(x))
    l1p = jnp.where(u < 1e-3, u * (1.0 - u * (0.5 - u * (1.0 / 3.0))), jnp.log(1.0 + u))
    return jnp.maximum(x, 0.0) + l1p


def _nt(a, b):
    return lax.dot_general(a, b, (((1,), (1,)), ((), ())), preferred_element_type=F32)


def _tn(a, b):
    return lax.dot_general(a, b, (((0,), (0,)), ((), ())), preferred_element_type=F32)


def _nn(a, b):
    return jnp.dot(a, b, preferred_element_type=F32)


def _tile(n, target):
    if n <= target:
        return n
    best = None
    for t in range(LANES, target + 1, LANES):
        if n % t == 0:
            best = t
    assert best is not None, (n, target)
    return best


MM_TK = {"nn": 2048, "nt": 2048, "tn": 1024}


def _mm(a, b, dims, out_dtype, name, tm=1024, tn=2048, tk=None, init=None, carry=None, b_rows=None, out_rows=None):
    tk = MM_TK[dims] if tk is None else tk
    if dims == "nn":
        (m, k), (k2, n) = a.shape, b.shape
        k2 = k2 if b_rows is None else b_rows
    elif dims == "nt":
        (m, k), (n, k2) = a.shape, b.shape
        n = n if b_rows is None else b_rows
    else:
        (k, m), (k2, n) = a.shape, b.shape
    assert k == k2
    tm, tn, tk = _tile(m, tm), _tile(n, tn), _tile(k, tk)
    nk = k // tk
    if dims == "tn":
        a_spec = pl.BlockSpec((tk, tm), lambda i, j, kk: (kk, i))
    else:
        a_spec = pl.BlockSpec((tm, tk), lambda i, j, kk: (i, kk))
    if dims == "nt":
        b_spec = pl.BlockSpec((tn, tk), lambda i, j, kk: (j, kk))
    else:
        b_spec = pl.BlockSpec((tk, tn), lambda i, j, kk: (kk, j))
    o_spec = pl.BlockSpec((tm, tn), lambda i, j, kk: (i, j))
    op = {"nn": _nn, "nt": _nt, "tn": _tn}[dims]
    has_init = init is not None
    nx = len(carry.arrays) if carry is not None else 0
    ni, nj = m // tm, n // tn

    def body(*refs):
        a_ref, b_ref = refs[0], refs[1]
        i_ref = refs[2] if has_init else None
        x_in = refs[2 + has_init:2 + has_init + nx]
        o_ref = refs[2 + has_init + nx]
        x_out = refs[3 + has_init + nx:3 + has_init + 2 * nx]
        acc = refs[3 + has_init + 2 * nx]
        x_sems = refs[4 + has_init + 2 * nx:]
        i, j, kk = pl.program_id(0), pl.program_id(1), pl.program_id(2)

        if nx:
            @pl.when((i == 0) & (j == 0) & (kk == 0))
            def _():
                carry.start(x_in, x_out, x_sems)

        prod = lambda: op(a_ref[...], b_ref[...])
        with_init = (lambda p: p + i_ref[...].astype(F32)) if has_init else (lambda p: p)
        if nk == 1:
            o_ref[...] = with_init(prod()).astype(out_dtype)
        else:
            @pl.when(kk == 0)
            def _():
                acc[...] = with_init(prod())

            @pl.when((kk > 0) & (kk < nk - 1))
            def _():
                acc[...] += prod()

            @pl.when(kk == nk - 1)
            def _():
                o_ref[...] = (acc[...] + prod()).astype(out_dtype)

        if nx:
            @pl.when((i == ni - 1) & (j == nj - 1) & (kk == nk - 1))
            def _():
                carry.finish(x_in, x_out, x_sems)

    in_specs = [a_spec, b_spec] + ([o_spec] if has_init else []) + [HBM_SPEC] * nx
    args = (a, b) + ((init,) if has_init else ()) + (tuple(carry.arrays) if nx else ())
    sems = carry.sem_shapes() if nx else []
    outs = pl.pallas_call(
        body, out_shape=[SDS((m if out_rows is None else out_rows, n), out_dtype)] + (carry.out_shapes if nx else []),
        grid=(ni, nj, nk),
        in_specs=in_specs, out_specs=[o_spec] + [HBM_SPEC] * nx,
        scratch_shapes=[pltpu.VMEM((tm, tn) if nk > 1 else (8, LANES), F32)] + sems,
        compiler_params=_params(("arbitrary",) * 3 if nx else ("parallel", "parallel", "arbitrary")), name=name)(*args)
    return (outs[0], outs[1:]) if nx else outs[0]


def _rmsnorm_fwd(x, w, carry=None):
    s, d = x.shape
    tr = 256
    nsteps = s // tr
    nx = len(carry.arrays) if carry is not None else 0

    def body(*refs):
        x_ref, w_ref, x_in = refs[0], refs[1], refs[2:2 + nx]
        o_ref, x_out, x_sems = refs[2 + nx], refs[3 + nx:3 + 2 * nx], refs[3 + 2 * nx:]
        if nx:
            @pl.when(pl.program_id(0) == 0)
            def _():
                carry.start(x_in, x_out, x_sems)

        xv = x_ref[...]
        r = lax.rsqrt(jnp.mean(xv * xv, axis=-1, keepdims=True) + RMS_EPS)
        o_ref[...] = (xv * r * w_ref[...]).astype(BF16)

        if nx:
            @pl.when(pl.program_id(0) == nsteps - 1)
            def _():
                carry.finish(x_in, x_out, x_sems)

    outs = pl.pallas_call(
        body, out_shape=[SDS((s, d), BF16)] + (carry.out_shapes if nx else []), grid=(nsteps,),
        in_specs=[pl.BlockSpec((tr, d), lambda i: (i, 0)), pl.BlockSpec((1, d), lambda i: (0, 0))] + [HBM_SPEC] * nx,
        out_specs=[pl.BlockSpec((tr, d), lambda i: (i, 0))] + [HBM_SPEC] * nx,
        scratch_shapes=carry.sem_shapes() if nx else [],
        compiler_params=_params(("arbitrary",) if nx else ("parallel",)), name="rmsnorm_fwd")(
            x, w, *(carry.arrays if nx else []))
    return (outs[0], outs[1:]) if nx else outs[0]


def _rmsnorm_bwd(x, w, dhn_a, dhn_b, dout):
    s, d = x.shape
    tr = 256

    def body(x_ref, w_ref, dh_ref, dh2_ref, do_ref, gx_ref, gw_ref):
        xv = x_ref[...]
        r = lax.rsqrt(jnp.mean(xv * xv, axis=-1, keepdims=True) + RMS_EPS)
        nrm = xv * r
        dh = dh_ref[...] + dh2_ref[...]
        gy = dh * w_ref[...]
        gx_ref[...] = do_ref[...] + r * (gy - nrm * jnp.mean(gy * nrm, axis=-1, keepdims=True))

        @pl.when(pl.program_id(0) == 0)
        def _():
            gw_ref[...] = jnp.zeros_like(gw_ref)

        gw_ref[...] += jnp.sum(dh * nrm, axis=0, keepdims=True)

    blk = pl.BlockSpec((tr, d), lambda i: (i, 0))
    row = pl.BlockSpec((1, d), lambda i: (0, 0))
    return pl.pallas_call(
        body, out_shape=(SDS((s, d), F32), SDS((1, d), F32)), grid=(s // tr,),
        in_specs=[blk, row, blk, blk, blk], out_specs=(blk, row),
        compiler_params=_params(("arbitrary",)), name="rmsnorm_bwd")(x, w, dhn_a, dhn_b, dout)


DEINT = DILATED_PATTERNS[-1][1]
DEINT_ROWS = DEINT * LANES


class _Pass:
    def __init__(self, tq, patterns, unit, seg_len):
        self.tq, self.patterns, self.unit, self.seg_len = tq, patterns, unit, seg_len
        self.win = max(w for w, _ in patterns) // unit
        self.w = self.win + tq
        assert self.win % tq == 0


def _attn_tables(ps):
    i = jnp.arange(ps.tq, dtype=jnp.int32)[:, None]
    j = jnp.arange(ps.w, dtype=jnp.int32)[None, :]
    delta = (i + ps.win - j) * ps.unit
    n = jnp.zeros((ps.tq, ps.w), F32)
    for window, dil in ps.patterns:
        n = n + ((delta >= 0) & (delta <= window) & (delta % dil == 0)).astype(F32)
    logn = jnp.where(n > 0, jnp.log(jnp.maximum(n, 1.0)), NEG)
    return logn, jnp.maximum(delta, 0).astype(F32)


def _slopes(h):
    s = jnp.asarray([2.0 ** (-8.0 * (i + 1) / h) for i in range(h)], F32)
    return jnp.broadcast_to(s[:, None, None], (h, 1, LANES))


def _masked_logn(ps, logn_ref, start):
    col = lax.broadcasted_iota(jnp.int32, (ps.tq, ps.w), 1)
    return jnp.where(col >= ps.win - lax.rem(start, ps.seg_len), logn_ref[...], NEG)


def _head_cols(hh):
    return slice(hh * ATTN_HEAD_DIM, (hh + 1) * ATTN_HEAD_DIM)


def _head_window(refs, cs):
    return jnp.concatenate([r[:, cs] for r in refs], axis=0)


def _head_scores(q_ref, kw, cs, base, dist_ref, slope_ref, hh):
    return _nt(q_ref[:, cs], kw) * (ATTN_HEAD_DIM ** -0.5) + (base - slope_ref[hh][0:1, 0:1] * dist_ref[...])


def _lane_of(stat, hh):
    lane = lax.broadcasted_iota(jnp.int32, stat.shape, 1)
    return jnp.sum(jnp.where(lane == hh, stat, 0.0), axis=1, keepdims=True)


def _window_specs(ps, d, col, nb):
    nprev = ps.win // ps.tq
    return [pl.BlockSpec((ps.tq, d), lambda i, b=b: (jnp.maximum(jnp.minimum(i, nb - 1) - (nprev - b), 0), col))
            for b in range(nprev + 1)]


def _attn_fwd(cfg, ps, qkv, cols, tables, slopes, name):
    s, h, d = cfg.S, cfg.H, cfg.D
    tq, nw = ps.tq, ps.win // ps.tq + 1
    nb = s // tq
    logn, dist = tables
    qc, kc, vc = [c // d for c in cols]

    def body(*refs):
        q_ref, k_refs, v_refs = refs[0], refs[1:1 + nw], refs[1 + nw:1 + 2 * nw]
        logn_ref, dist_ref, slope_ref, o_ref, lse_ref = refs[1 + 2 * nw:]
        base = _masked_logn(ps, logn_ref, pl.program_id(0) * tq)
        lane = lax.broadcasted_iota(jnp.int32, (tq, LANES), 1)

        lse = jnp.zeros((tq, LANES), F32)
        for hh in range(h):
            cs = _head_cols(hh)
            sc = _head_scores(q_ref, _head_window(k_refs, cs), cs, base, dist_ref, slope_ref, hh)
            m = jnp.max(sc, axis=1, keepdims=True)
            p = jnp.exp(sc - m)
            l = jnp.sum(p, axis=1, keepdims=True)
            o_ref[:, cs] = (_nn(p.astype(BF16), _head_window(v_refs, cs)) / l).astype(BF16)
            lse = jnp.where(lane == hh, m + jnp.log(l), lse)
        lse_ref[...] = lse

    tab = pl.BlockSpec((tq, ps.w), lambda i: (0, 0))
    return pl.pallas_call(
        body, out_shape=(SDS((s, d), BF16), SDS((s, LANES), F32)), grid=(nb,),
        in_specs=[pl.BlockSpec((tq, d), lambda i: (i, qc))] + _window_specs(ps, d, kc, nb) + _window_specs(ps, d, vc, nb)
        + [tab, tab, pl.BlockSpec((h, 1, LANES), lambda i: (0, 0, 0))],
        out_specs=(pl.BlockSpec((tq, d), lambda i: (i, 0)), pl.BlockSpec((tq, LANES), lambda i: (i, 0))),
        compiler_params=_params(("parallel",)), name=name)(*([qkv] * (1 + 2 * nw)), logn, dist, slopes)


def _attn_bwd(cfg, ps, qkv, cols, do, lse, delta, tables, slopes, name):
    s, h, d = cfg.S, cfg.H, cfg.D
    tq, nprev = ps.tq, ps.win // ps.tq
    nw = nprev + 1
    nb = s // tq
    logn, dist = tables
    qc, kc, vc = [c // d for c in cols]
    scale = ATTN_HEAD_DIM ** -0.5

    def body(*refs):
        q_ref, k_refs, v_refs = refs[0], refs[1:1 + nw], refs[1 + nw:1 + 2 * nw]
        do_ref, lse_ref, dl_ref, logn_ref, dist_ref, slope_ref, dq_ref, dk_ref, dv_ref, ck, cv = refs[1 + 2 * nw:]
        i = pl.program_id(0)
        slot = lambda b: lax.rem(i + b, nprev)

        @pl.when(i == 0)
        def _():
            ck[...] = jnp.zeros_like(ck)
            cv[...] = jnp.zeros_like(cv)

        @pl.when(i < nb)
        def _():
            base = _masked_logn(ps, logn_ref, i * tq)
            lse_all, dl_all = lse_ref[...], dl_ref[...]

            for hh in range(h):
                cs = _head_cols(hh)
                kw, vw = _head_window(k_refs, cs), _head_window(v_refs, cs)
                sc = _head_scores(q_ref, kw, cs, base, dist_ref, slope_ref, hh)
                p = jnp.exp(sc - lse_all[:, hh:hh + 1])
                dob = do_ref[:, cs]
                ds = (p * (_nt(dob, vw) - dl_all[:, hh:hh + 1]) * scale).astype(BF16)
                dq_ref[:, cs] = _nn(ds, kw).astype(BF16)
                dkw = _tn(ds, q_ref[:, cs])
                dvw = _tn(p.astype(BF16), dob)
                dk_ref[:, cs] = ck[slot(0), :, cs] + dkw[0:tq]
                dv_ref[:, cs] = cv[slot(0), :, cs] + dvw[0:tq]
                for b in range(1, nprev):
                    ck[slot(b), :, cs] += dkw[b * tq:(b + 1) * tq]
                    cv[slot(b), :, cs] += dvw[b * tq:(b + 1) * tq]
                ck[slot(0), :, cs] = dkw[nprev * tq:]
                cv[slot(0), :, cs] = dvw[nprev * tq:]

        @pl.when(i >= nb)
        def _():
            dk_ref[...] = ck[slot(0)]
            dv_ref[...] = cv[slot(0)]

    here = lambda i: jnp.minimum(i, nb - 1)
    blk = pl.BlockSpec((tq, d), lambda i: (here(i), 0))
    stat = pl.BlockSpec((tq, LANES), lambda i: (here(i), 0))
    late = pl.BlockSpec((tq, d), lambda i: (jnp.maximum(i - nprev, 0), 0))
    tab = pl.BlockSpec((tq, ps.w), lambda i: (0, 0))
    return pl.pallas_call(
        body, out_shape=(SDS((s, d), BF16), SDS((s, d), F32), SDS((s, d), F32)), grid=(nb + nprev,),
        in_specs=[pl.BlockSpec((tq, d), lambda i: (here(i), qc))] + _window_specs(ps, d, kc, nb)
        + _window_specs(ps, d, vc, nb) + [blk, stat, stat, tab, tab, pl.BlockSpec((h, 1, LANES), lambda i: (0, 0, 0))],
        out_specs=(blk, late, late),
        scratch_shapes=[pltpu.VMEM((nprev, tq, d), F32), pltpu.VMEM((nprev, tq, d), F32)],
        compiler_params=_params(("arbitrary",)), name=name)(
            *([qkv] * (1 + 2 * nw)), do, lse, delta, logn, dist, slopes)


def _by_residue(a):
    return a.reshape(DEINT, a.shape[0] // DEINT, a.shape[1])


def _deint_spec(colblock):
    return pl.BlockSpec((DEINT, LANES, LANES), lambda b, j: (0, b, colblock(j)))


def _deint_rows(scr, out_ref, dtype):
    for r in range(DEINT):
        out_ref[r] = scr[pl.ds(r, LANES, stride=DEINT), :].astype(dtype)


def _int_rows(in_ref, scr):
    for r in range(DEINT):
        scr[pl.ds(r, LANES, stride=DEINT), :] = in_ref[r].astype(F32)


WIDE = 4 * LANES


def _wide_spec():
    return pl.BlockSpec((DEINT, LANES, WIDE), lambda b, j: (0, b, j))


def _deinterleave(x, col0, ncols, name):
    s = x.shape[0]
    c0 = col0 // WIDE

    def body(x_ref, o_ref, scr):
        for t in range(WIDE // LANES):
            cs = slice(t * LANES, (t + 1) * LANES)
            scr[t] = x_ref[:, cs].astype(F32)
            for r in range(DEINT):
                o_ref[r, :, cs] = scr.at[t][pl.ds(r, LANES, stride=DEINT), :].astype(x.dtype)

    out = pl.pallas_call(
        body, out_shape=SDS((DEINT, s // DEINT, ncols), x.dtype), grid=(s // DEINT_ROWS, ncols // WIDE),
        in_specs=[pl.BlockSpec((DEINT_ROWS, WIDE), lambda b, j: (b, c0 + j))],
        out_specs=_wide_spec(),
        scratch_shapes=[pltpu.VMEM((WIDE // LANES, DEINT_ROWS, LANES), F32)],
        compiler_params=_params(("parallel", "parallel")), name=name)(x)
    return out.reshape(s, ncols)


def _attn_merge(cfg, proj, o_1, lse_1, o_2, lse_2):
    s, h = cfg.S, cfg.H
    zb = cfg.OZA // WIDE
    rows = DEINT_ROWS
    hps = WIDE // LANES

    def body(o1_ref, l1_ref, o2_ref, l2_ref, z_ref, o_ref, og_ref, lse_ref, so, sl):
        j = pl.program_id(1)

        @pl.when(j == 0)
        def _():
            _int_rows(l2_ref, sl)
            lse_ref[...] = jnp.zeros_like(lse_ref)

        l1_all, l2_all = l1_ref[...], sl[...]
        lane = lax.broadcasted_iota(jnp.int32, (rows, LANES), 1)
        lse = lse_ref[...]
        for t in range(hps):
            hh = j * hps + t
            cs = slice(t * LANES, (t + 1) * LANES)
            for r in range(DEINT):
                so.at[t][pl.ds(r, LANES, stride=DEINT), :] = o2_ref[r, :, cs].astype(F32)
            l1, l2 = _lane_of(l1_all, hh), _lane_of(l2_all, hh)
            mx = jnp.maximum(l1, l2)
            w1, w2 = jnp.exp(l1 - mx), jnp.exp(l2 - mx)
            den = w1 + w2
            o = (w1 * o1_ref[:, cs].astype(F32) + w2 * so[t]) / den
            z = z_ref[:, cs].astype(F32)
            o_ref[:, cs] = o.astype(BF16)
            og_ref[:, cs] = (o * (z * _sigmoid(z))).astype(BF16)
            lse = jnp.where(lane == hh, mx + jnp.log(den), lse)
        lse_ref[...] = lse

    blk = pl.BlockSpec((rows, WIDE), lambda b, j: (b, j))
    stat = pl.BlockSpec((rows, LANES), lambda b, j: (b, 0))
    return pl.pallas_call(
        body, out_shape=(SDS((s, cfg.D), BF16), SDS((s, cfg.D), BF16), SDS((s, LANES), F32)),
        grid=(s // rows, h // hps),
        in_specs=[blk, stat, _wide_spec(), _deint_spec(lambda j: 0), pl.BlockSpec((rows, WIDE), lambda b, j: (b, zb + j))],
        out_specs=(blk, blk, stat),
        scratch_shapes=[pltpu.VMEM((hps, rows, LANES), F32), pltpu.VMEM((rows, LANES), F32)],
        compiler_params=_params(("parallel", "arbitrary")), name="attn_merge")(
            o_1, lse_1, _by_residue(o_2), _by_residue(lse_2), proj)


def _attn_bwd_prep(cfg, proj, o_a, doag, lse, dproj):
    s, h = cfg.S, cfg.H
    zb = cfg.OZA // WIDE
    rows = DEINT_ROWS
    hps = WIDE // LANES

    def body(o_ref, dg_ref, z_ref, lse_ref, dp_in, dz_ref, do_ref, do2_ref, dl_ref, dl2_ref, lse2_ref, scr):
        del dp_in
        j = pl.program_id(1)

        @pl.when(j == 0)
        def _():
            dl_ref[...] = jnp.zeros_like(dl_ref)

        lane = lax.broadcasted_iota(jnp.int32, (rows, LANES), 1)
        dl = dl_ref[...]
        for t in range(hps):
            cs = slice(t * LANES, (t + 1) * LANES)
            z = z_ref[:, cs].astype(F32)
            sg = _sigmoid(z)
            o = o_ref[:, cs].astype(F32)
            dg = dg_ref[:, cs].astype(F32)
            do = dg * (z * sg)
            dz_ref[:, cs] = (dg * o * (sg * (1.0 + z * (1.0 - sg)))).astype(BF16)
            do_ref[:, cs] = do.astype(BF16)
            scr[...] = do
            for r in range(DEINT):
                do2_ref[r, :, cs] = scr[pl.ds(r, LANES, stride=DEINT), :].astype(BF16)
            dl = jnp.where(lane == j * hps + t, jnp.sum(do * o, axis=1, keepdims=True), dl)
        dl_ref[...] = dl

        @pl.when(j == h // hps - 1)
        def _():
            scr[...] = dl
            _deint_rows(scr, dl2_ref, F32)
            scr[...] = lse_ref[...]
            _deint_rows(scr, lse2_ref, F32)

    blk = pl.BlockSpec((rows, WIDE), lambda b, j: (b, j))
    stat = pl.BlockSpec((rows, LANES), lambda b, j: (b, 0))
    stat2 = _deint_spec(lambda j: 0)
    outs = pl.pallas_call(
        body,
        out_shape=(SDS(dproj.shape, BF16), SDS((s, cfg.D), BF16), SDS((DEINT, s // DEINT, cfg.D), BF16),
                   SDS((s, LANES), F32), SDS((DEINT, s // DEINT, LANES), F32), SDS((DEINT, s // DEINT, LANES), F32)),
        grid=(s // rows, h // hps),
        in_specs=[blk, blk, pl.BlockSpec((rows, WIDE), lambda b, j: (b, zb + j)), stat, HBM_SPEC],
        out_specs=(pl.BlockSpec((rows, WIDE), lambda b, j: (b, zb + j)), blk, _wide_spec(), stat, stat2, stat2),
        scratch_shapes=[pltpu.VMEM((rows, LANES), F32)],
        input_output_aliases={4: 0},
        compiler_params=_params(("parallel", "arbitrary")), name="attn_bwd_prep")(o_a, doag, proj, lse, dproj)
    dproj, do, do2, dl, dl2, lse2 = outs
    return dproj, do, do2.reshape(s, cfg.D), dl, dl2.reshape(s, LANES), lse2.reshape(s, LANES)


def _attn_grad_sum(cfg, g_1, g_2, col0, dproj, name):
    s = cfg.S
    c0 = col0 // WIDE
    rows = DEINT_ROWS

    def body(g1_ref, g2_ref, dp_in, o_ref, scr):
        del dp_in
        for t in range(WIDE // LANES):
            cs = slice(t * LANES, (t + 1) * LANES)
            for r in range(DEINT):
                scr.at[t][pl.ds(r, LANES, stride=DEINT), :] = g2_ref[r, :, cs].astype(F32)
            o_ref[:, cs] = (g1_ref[:, cs].astype(F32) + scr[t]).astype(BF16)

    return pl.pallas_call(
        body, out_shape=SDS(dproj.shape, BF16), grid=(s // rows, cfg.D // WIDE),
        in_specs=[pl.BlockSpec((rows, WIDE), lambda b, j: (b, j)), _wide_spec(), HBM_SPEC],
        out_specs=pl.BlockSpec((rows, WIDE), lambda b, j: (b, c0 + j)),
        scratch_shapes=[pltpu.VMEM((WIDE // LANES, rows, LANES), F32)],
        input_output_aliases={2: 0},
        compiler_params=_params(("parallel", "parallel")), name=name)(g_1, _by_residue(g_2), dproj)


CONV_HALO = 16
CONV_TR = 512
CONV_CW = 1024


def _rows_back(a, n):
    return a if n == 0 else pltpu.roll(a, n % a.shape[0], axis=0)


def _conv_fwd(cfg, proj, conv_w, conv_b):
    s, cd = cfg.S, cfg.CD
    tr, cw, hl = CONV_TR, CONV_CW, CONV_HALO
    cb0 = cfg.OXBC // cw

    def body(x_ref, h_ref, w_ref, b_ref, o_ref):
        i = pl.program_id(0)
        halo = jnp.where(i > 0, h_ref[...].astype(F32), 0.0)
        ext = jnp.concatenate([halo, x_ref[...].astype(F32)], axis=0)
        pre = b_ref[...] + jnp.zeros((tr, cw), F32)
        for k in range(CONV_K):
            pre = pre + w_ref[k:k + 1, :] * _rows_back(ext, CONV_K - 1 - k)[hl:]
        o_ref[...] = (pre * _sigmoid(pre)).astype(BF16)

    return pl.pallas_call(
        body, out_shape=SDS((s, cd), BF16), grid=(s // tr, cd // cw),
        in_specs=[pl.BlockSpec((tr, cw), lambda i, j: (i, cb0 + j)),
                  pl.BlockSpec((hl, cw), lambda i, j: (jnp.maximum(i * (tr // hl) - 1, 0), cb0 + j)),
                  pl.BlockSpec((CONV_K, cw), lambda i, j: (0, j)),
                  pl.BlockSpec((1, cw), lambda i, j: (0, j))],
        out_specs=pl.BlockSpec((tr, cw), lambda i, j: (i, j)),
        compiler_params=_params(("parallel", "parallel")), name="conv_fwd")(proj, proj, conv_w, conv_b)


def _conv_bwd(cfg, proj, dact, conv_w, conv_b, dproj):
    s, cd = cfg.S, cfg.CD
    tr, cw, hl = CONV_TR, CONV_CW, CONV_HALO
    cb0 = cfg.OXBC // cw
    nr = s // tr
    last_h = s // hl - 1

    def body(x_ref, hp_ref, hn_ref, d_ref, dn_ref, w_ref, b_ref, dp_in, dx_ref, gw_ref, gb_ref):
        del dp_in
        i = pl.program_id(1)
        ext = jnp.concatenate([jnp.where(i > 0, hp_ref[...].astype(F32), 0.0), x_ref[...].astype(F32),
                               hn_ref[...].astype(F32)], axis=0)
        shifted = [_rows_back(ext, CONV_K - 1 - k)[hl:] for k in range(CONV_K)]
        pre = b_ref[...] + jnp.zeros((tr + hl, cw), F32)
        for k in range(CONV_K):
            pre = pre + w_ref[k:k + 1, :] * shifted[k]
        sg = _sigmoid(pre)
        dact = jnp.concatenate([d_ref[...].astype(F32), jnp.where(i < nr - 1, dn_ref[...].astype(F32), 0.0)], axis=0)
        dpre = dact * (sg * (1.0 + pre * (1.0 - sg)))
        dx = jnp.zeros((tr, cw), F32)
        for k in range(CONV_K):
            dx = dx + w_ref[k:k + 1, :] * _rows_back(dpre, -(CONV_K - 1 - k))[0:tr]
        dx_ref[...] = dx.astype(BF16)

        @pl.when(i == 0)
        def _():
            gw_ref[...] = jnp.zeros_like(gw_ref)
            gb_ref[...] = jnp.zeros_like(gb_ref)

        dcur = dpre[0:tr]
        gb_ref[...] += jnp.sum(dcur, axis=0, keepdims=True)
        for k in range(CONV_K):
            gw_ref[k:k + 1, :] += jnp.sum(dcur * shifted[k][0:tr], axis=0, keepdims=True)

    return pl.pallas_call(
        body, out_shape=(SDS(dproj.shape, BF16), SDS((CONV_K, cd), F32), SDS((1, cd), F32)), grid=(cd // cw, nr),
        in_specs=[pl.BlockSpec((tr, cw), lambda j, i: (i, cb0 + j)),
                  pl.BlockSpec((hl, cw), lambda j, i: (jnp.maximum(i * (tr // hl) - 1, 0), cb0 + j)),
                  pl.BlockSpec((hl, cw), lambda j, i: (jnp.minimum((i + 1) * (tr // hl), last_h), cb0 + j)),
                  pl.BlockSpec((tr, cw), lambda j, i: (i, j)),
                  pl.BlockSpec((hl, cw), lambda j, i: (jnp.minimum((i + 1) * (tr // hl), last_h), j)),
                  pl.BlockSpec((CONV_K, cw), lambda j, i: (0, j)),
                  pl.BlockSpec((1, cw), lambda j, i: (0, j)),
                  pl.BlockSpec(memory_space=pl.ANY)],
        out_specs=(pl.BlockSpec((tr, cw), lambda j, i: (i, cb0 + j)),
                   pl.BlockSpec((CONV_K, cw), lambda j, i: (0, j)),
                   pl.BlockSpec((1, cw), lambda j, i: (0, j))),
        input_output_aliases={7: 0},
        compiler_params=_params(("parallel", "arbitrary")), name="conv_bwd")(
            proj, proj, proj, dact, dact, conv_w, conv_b, dproj)


def _expand(v, e, terms):
    out, rem = None, v
    for _ in range(terms):
        hi = rem.astype(BF16)
        t = _nn(hi, e)
        out = t if out is None else out + t
        rem = rem - hi.astype(F32)
    return out


def _segsum(v, e, terms):
    out, rem = None, v
    for _ in range(terms):
        hi = rem.astype(BF16)
        t = _nt(hi, e)
        out = t if out is None else out + t
        rem = rem - hi.astype(F32)
    return out


def _expand_row(row, e, terms):
    return _expand(jnp.broadcast_to(row, (8, LANES)), e, terms)[0:1]


def _segsum_row(row, e, terms):
    return _segsum(jnp.broadcast_to(row, (8, row.shape[1])), e, terms)[0:1]


def _expansion_matrix(cfg):
    hh = jnp.arange(LANES, dtype=jnp.int32)[:, None]
    cc = jnp.arange(cfg.SI, dtype=jnp.int32)[None, :]
    return (cc // SSM_HEAD_DIM == hh).astype(BF16)


def _tri(lower):
    r = lax.broadcasted_iota(jnp.int32, (CHUNK, CHUNK), 0)
    c = lax.broadcasted_iota(jnp.int32, (CHUNK, CHUNK), 1)
    return (c <= r) if lower else (c >= r)


def _ssd_prep(dtr_ref, db_ref, al_ref, e):
    dtr = dtr_ref[...] + db_ref[...]
    dt = _softplus(dtr)
    a = -jnp.exp(al_ref[...])
    acum = jnp.dot(_tri(True).astype(F32), dt * a, precision=lax.Precision.HIGHEST, preferred_element_type=F32)
    return dtr, dt, a, _expand(dt, e, 2), _expand(acum, e, 3)


def _ssd_fwd(cfg, xact, dt_raw, proj, dt_bias, a_log, d_skip, norm_w, e):
    s, si, cd, gw, bc = cfg.S, cfg.SI, cfg.CD, cfg.GW, cfg.BC
    nc = s // CHUNK
    zb = cfg.OZS // si
    tiles = gw // LANES

    def body(xa_ref, dtr_ref, z_ref, db_ref, al_ref, dsk_ref, nw_ref, e_ref, y_ref, y2_ref, st_ref,
             state, ybuf, x_s, xw_s, ae_s, ea_s, lam_s):
        @pl.when(pl.program_id(0) == 0)
        def _():
            state[...] = jnp.zeros_like(state)

        st_ref[...] = state[...]
        ev = e_ref[...]
        _, _, _, dt_e, a_e = _ssd_prep(dtr_ref, db_ref, al_ref, ev)
        xs = xa_ref[:, 0:si].astype(F32)
        x = xs * dt_e
        lam_e = a_e[CHUNK - 1:CHUNK, :]
        x_s[...] = x.astype(BF16)
        xw_s[...] = (x * jnp.exp(lam_e - a_e)).astype(BF16)
        ae_s[...] = a_e
        ea_s[...] = jnp.exp(a_e)
        ybuf[...] = _expand_row(dsk_ref[...], ev, 3) * xs
        lam_s[...] = jnp.broadcast_to(jnp.exp(lam_e), (8, si))
        tril = _tri(True)
        lane = lax.broadcasted_iota(jnp.int32, (CHUNK, LANES), 1)

        def group(g, carry):
            co = g * gw
            bg = xa_ref[:, pl.ds(si + g * SSM_STATE, SSM_STATE)]
            cg = xa_ref[:, pl.ds(si + bc + g * SSM_STATE, SSM_STATE)]
            cbm = _nt(cg, bg)
            st = state[:, pl.ds(co, gw)]
            yoff = _nn(cg, st.astype(BF16)) * ea_s[:, pl.ds(co, gw)]
            for k in range(tiles):
                tc = co + k * LANES
                at = ae_s[:, pl.ds(tc, LANES)]
                att = at.T
                xt = x_s[:, pl.ds(tc, LANES)]
                acc = yoff[:, k * LANES:(k + 1) * LANES]
                for half in range(2):
                    lo = half * SSM_HEAD_DIM
                    seg = at[:, lo:lo + 1] - att[lo:lo + 1, :]
                    dec = jnp.exp(jnp.where(tril, seg, NEG))
                    xh = jnp.where((lane >= lo) & (lane < lo + SSM_HEAD_DIM), xt, jnp.zeros_like(xt))
                    acc = acc + _nn((cbm * dec).astype(BF16), xh)
                ybuf[:, pl.ds(tc, LANES)] += acc
            state[:, pl.ds(co, gw)] = st * lam_s[0:1, pl.ds(co, gw)] + _tn(bg, xw_s[:, pl.ds(co, gw)])
            return carry

        for g in range(SSM_GROUPS):
            group(g, 0)
        y = ybuf[...]
        y_ref[...] = y.astype(BF16)
        z = z_ref[...].astype(F32)
        u = y * (z * _sigmoid(z))
        r = lax.rsqrt(jnp.mean(u * u, axis=-1, keepdims=True) + RMS_EPS)
        y2_ref[...] = (u * r * nw_ref[...]).astype(BF16)

    row = lambda n: pl.BlockSpec((1, n), lambda c: (0, 0))
    return pl.pallas_call(
        body,
        out_shape=(SDS((s, si), BF16), SDS((s, si), BF16), SDS((nc, SSM_STATE, si), F32)),
        grid=(nc,),
        in_specs=[pl.BlockSpec((CHUNK, cd), lambda c: (c, 0)),
                  pl.BlockSpec((CHUNK, LANES), lambda c: (c, 0)),
                  pl.BlockSpec((CHUNK, si), lambda c: (c, zb)),
                  row(LANES), row(LANES), row(LANES), row(si),
                  pl.BlockSpec((LANES, si), lambda c: (0, 0))],
        out_specs=(pl.BlockSpec((CHUNK, si), lambda c: (c, 0)),
                   pl.BlockSpec((CHUNK, si), lambda c: (c, 0)),
                   pl.BlockSpec((None, SSM_STATE, si), lambda c: (c, 0, 0))),
        scratch_shapes=[pltpu.VMEM((SSM_STATE, si), F32), pltpu.VMEM((CHUNK, si), F32),
                        pltpu.VMEM((CHUNK, si), BF16), pltpu.VMEM((CHUNK, si), BF16),
                        pltpu.VMEM((CHUNK, si), F32), pltpu.VMEM((CHUNK, si), F32),
                        pltpu.VMEM((8, si), F32)],
        compiler_params=_params(("arbitrary",)), name="ssd_fwd")(
            xact, dt_raw, proj, dt_bias, a_log, d_skip, norm_w, e)


def _ssd_bwd(cfg, xact, dt_raw, proj, y, dy2, states, dt_bias, a_log, d_skip, norm_w, e, dproj):
    s, si, cd, gw, bc, hpg = cfg.S, cfg.SI, cfg.CD, cfg.GW, cfg.BC, cfg.HPG
    nc = s // CHUNK
    zb = cfg.OZS // si
    tiles = gw // LANES

    def body(xa_ref, dtr_ref, z_ref, y_ref, d2_ref, st_ref, db_ref, al_ref, dsk_ref, nw_ref, e_ref, dp_in,
             dz_ref, dxa_ref, ddt_ref, gnw_ref, gdb_ref, gal_ref, gds_ref,
             dh, dhn, xs_s, x_s, w_s, ae_s, ea_s, g_s, dx_s, dae_s, r_s, lam_s, dle_s):
        del dp_in

        @pl.when(pl.program_id(0) == 0)
        def _():
            dh[...] = jnp.zeros_like(dh)
            gnw_ref[...] = jnp.zeros_like(gnw_ref)
            gdb_ref[...] = jnp.zeros_like(gdb_ref)
            gal_ref[...] = jnp.zeros_like(gal_ref)
            gds_ref[...] = jnp.zeros_like(gds_ref)

        ev = e_ref[...]
        yv = y_ref[...].astype(F32)
        z = z_ref[...].astype(F32)
        sg = _sigmoid(z)
        sz = z * sg
        u = yv * sz
        r = lax.rsqrt(jnp.mean(u * u, axis=-1, keepdims=True) + RMS_EPS)
        nrm = u * r
        d2 = d2_ref[...].astype(F32)
        gnw_ref[...] += jnp.sum(d2 * nrm, axis=0, keepdims=True)
        gn = d2 * nw_ref[...]
        du = r * (gn - nrm * jnp.mean(gn * nrm, axis=-1, keepdims=True))
        gv = du * sz
        dz_ref[...] = (du * yv * (sg * (1.0 + z * (1.0 - sg)))).astype(BF16)
        g_s[...] = gv

        dtr, dt, a, dt_e, a_e = _ssd_prep(dtr_ref, db_ref, al_ref, ev)
        xs = xa_ref[:, 0:si].astype(F32)
        x = xs * dt_e
        lam_e = a_e[CHUNK - 1:CHUNK, :]
        xs_s[...] = xs
        x_s[...] = x
        w_s[...] = jnp.exp(lam_e - a_e)
        ae_s[...] = a_e
        ea_s[...] = jnp.exp(a_e)
        lam_s[...] = jnp.broadcast_to(jnp.exp(lam_e), (8, si))
        gds_ref[...] += _segsum_row(jnp.sum(gv * xs, axis=0, keepdims=True), ev, 2)
        r_s[...] = jnp.zeros_like(r_s)
        tril = _tri(True)
        lane = lax.broadcasted_iota(jnp.int32, (CHUNK, LANES), 1)
        sub = lax.broadcasted_iota(jnp.int32, (CHUNK, LANES), 0)

        def group(g, carry):
            co = g * gw
            bo = si + g * SSM_STATE
            cof = si + bc + g * SSM_STATE
            cols = pl.ds(co, gw)
            bg = xa_ref[:, pl.ds(bo, SSM_STATE)]
            cg = xa_ref[:, pl.ds(cof, SSM_STATE)]
            cbm = _nt(cg, bg)
            st = st_ref[:, cols]
            stb = st.astype(BF16)
            dho = dh[:, cols]
            dhob = dho.astype(BF16)
            ea = ea_s[:, cols]
            gg = g_s[:, cols]
            xg = x_s[:, cols]
            wg = w_s[:, cols]
            explam = lam_s[0:1, cols]
            yoff = _nn(cg, stb) * ea
            ga = (gg * ea).astype(BF16)
            dc = _nt(ga, stb)
            dhn[:, cols] = dho * explam + _tn(cg, ga)
            bdh = _nn(bg, dhob)
            db = _nt((xg * wg).astype(BF16), dhob)
            t = xg * bdh * wg
            dle_s[0:1, cols] = jnp.sum(t, axis=0, keepdims=True) + explam * jnp.sum(dho * st, axis=0, keepdims=True)
            dae_base = gg * yoff - t
            dxw = wg * bdh
            dcb = jnp.zeros((CHUNK, CHUNK), F32)
            for k in range(tiles):
                tc = co + k * LANES
                ksl = slice(k * LANES, (k + 1) * LANES)
                at = ae_s[:, pl.ds(tc, LANES)]
                att = at.T
                xt = xg[:, ksl].astype(BF16)
                gt = gg[:, ksl].astype(BF16)
                dxt = dxw[:, ksl]
                place = jnp.zeros((CHUNK, LANES), F32)
                for half in range(2):
                    lo = half * SSM_HEAD_DIM
                    seg = at[:, lo:lo + 1] - att[lo:lo + 1, :]
                    dec = jnp.exp(jnp.where(tril, seg, NEG))
                    mh = cbm * dec
                    gh = jnp.where((lane >= lo) & (lane < lo + SSM_HEAD_DIM), gt, jnp.zeros_like(gt))
                    dm = _nt(gh, xt)
                    dxt = dxt + _tn(mh.astype(BF16), gh)
                    dcb = dcb + dm * dec
                    dseg = dm * mh
                    place = place + jnp.where(lane == lo, jnp.sum(dseg, axis=1, keepdims=True), 0.0)
                    hidx = g * hpg + 2 * k + half
                    r_s[...] += jnp.where(sub == hidx, jnp.sum(dseg, axis=0, keepdims=True), 0.0)
                dx_s[:, pl.ds(tc, LANES)] = dxt
                dae_s[:, pl.ds(tc, LANES)] = dae_base[:, ksl] + place
            dcbb = dcb.astype(BF16)
            dxa_ref[:, pl.ds(bo, SSM_STATE)] = (db + _tn(dcbb, cg)).astype(BF16)
            dxa_ref[:, pl.ds(cof, SSM_STATE)] = (dc + _nn(dcbb, bg)).astype(BF16)
            return carry

        for g in range(SSM_GROUPS):
            group(g, 0)
        dlam = _segsum_row(dle_s[0:1, :], ev, 2)
        da_ = _segsum(dae_s[...], ev, 2) - r_s[...].T
        da_ = da_ + jnp.where(sub == CHUNK - 1, dlam, 0.0)
        dda = jnp.dot(_tri(False).astype(F32), da_, precision=lax.Precision.HIGHEST, preferred_element_type=F32)
        dxv = dx_s[...]
        xs = xs_s[...]
        ddt = dda * a + _segsum(dxv * xs, ev, 2)
        gal_ref[...] += jnp.sum(dda * dt, axis=0, keepdims=True) * a
        ddtr = ddt * _sigmoid(dtr)
        gdb_ref[...] += jnp.sum(ddtr, axis=0, keepdims=True)
        ddt_ref[...] = ddtr
        dxa_ref[:, 0:si] = (dxv * dt_e + g_s[...] * _expand_row(dsk_ref[...], ev, 3)).astype(BF16)
        dh[...] = dhn[...]

    rev = lambda c: nc - 1 - c
    row = lambda n: pl.BlockSpec((1, n), lambda c: (0, 0))
    big = lambda: pltpu.VMEM((CHUNK, si), F32)
    return pl.pallas_call(
        body,
        out_shape=(SDS(dproj.shape, BF16), SDS((s, cd), BF16), SDS((s, LANES), F32),
                   SDS((1, si), F32), SDS((1, LANES), F32), SDS((1, LANES), F32), SDS((1, LANES), F32)),
        grid=(nc,),
        in_specs=[pl.BlockSpec((CHUNK, cd), lambda c: (rev(c), 0)),
                  pl.BlockSpec((CHUNK, LANES), lambda c: (rev(c), 0)),
                  pl.BlockSpec((CHUNK, si), lambda c: (rev(c), zb)),
                  pl.BlockSpec((CHUNK, si), lambda c: (rev(c), 0)),
                  pl.BlockSpec((CHUNK, si), lambda c: (rev(c), 0)),
                  pl.BlockSpec((None, SSM_STATE, si), lambda c: (rev(c), 0, 0)),
                  row(LANES), row(LANES), row(LANES), row(si),
                  pl.BlockSpec((LANES, si), lambda c: (0, 0)),
                  pl.BlockSpec(memory_space=pl.ANY)],
        out_specs=(pl.BlockSpec((CHUNK, si), lambda c: (rev(c), zb)),
                   pl.BlockSpec((CHUNK, cd), lambda c: (rev(c), 0)),
                   pl.BlockSpec((CHUNK, LANES), lambda c: (rev(c), 0)),
                   row(si), row(LANES), row(LANES), row(LANES)),
        scratch_shapes=[pltpu.VMEM((SSM_STATE, si), F32), pltpu.VMEM((SSM_STATE, si), F32),
                        big(), big(), big(), big(), big(), big(), big(), big(),
                        pltpu.VMEM((CHUNK, LANES), F32), pltpu.VMEM((8, si), F32), pltpu.VMEM((8, si), F32)],
        input_output_aliases={11: 0},
        compiler_params=_params(("arbitrary",)), name="ssd_bwd")(
            xact, dt_raw, proj, y, dy2, states, dt_bias, a_log, d_skip, norm_w, e, dproj)


MERGE_TR = 512
MERGE_CW = 2048


def _merge_fwd(cfg, proj, a_br, s_br):
    s, d = cfg.S, cfg.D
    tr, cw = MERGE_TR, min(MERGE_CW, d)
    ga0, gs0 = cfg.OGA // cw, cfg.OGS // cw

    def body(ga_ref, gs_ref, a_ref, s_ref, o_ref):
        o_ref[...] = (_sigmoid(ga_ref[...].astype(F32)) * a_ref[...].astype(F32)
                      + _sigmoid(gs_ref[...].astype(F32)) * s_ref[...].astype(F32)).astype(BF16)

    blk = pl.BlockSpec((tr, cw), lambda i, j: (i, j))
    return pl.pallas_call(
        body, out_shape=SDS((s, d), BF16), grid=(s // tr, d // cw),
        in_specs=[pl.BlockSpec((tr, cw), lambda i, j: (i, ga0 + j)),
                  pl.BlockSpec((tr, cw), lambda i, j: (i, gs0 + j)), blk, blk],
        out_specs=blk, compiler_params=_params(("parallel", "parallel")), name="merge_fwd")(proj, proj, a_br, s_br)


def _merge_bwd(cfg, proj, branch, dmerged, gate_off, dproj, name):
    s, d = cfg.S, cfg.D
    tr, cw = MERGE_TR, min(MERGE_CW, d)
    g0 = gate_off // cw
    fresh = dproj is None

    def body(*refs):
        g_ref, b_ref, dm_ref = refs[:3]
        dg_ref, db_ref = refs[-2:]
        dm = dm_ref[...].astype(F32)
        sg = _sigmoid(g_ref[...].astype(F32))
        db_ref[...] = (dm * sg).astype(BF16)
        dg_ref[...] = (dm * b_ref[...].astype(F32) * sg * (1.0 - sg)).astype(BF16)

    blk = pl.BlockSpec((tr, cw), lambda i, j: (i, j))
    gate = pl.BlockSpec((tr, cw), lambda i, j: (i, g0 + j))
    return pl.pallas_call(
        body, out_shape=(SDS((s, cfg.NM), BF16), SDS((s, d), BF16)), grid=(s // tr, d // cw),
        in_specs=[gate, blk, blk] + ([] if fresh else [HBM_SPEC]),
        out_specs=(gate, blk),
        input_output_aliases={} if fresh else {3: 0},
        compiler_params=_params(("parallel", "parallel")), name=name)(
            *((proj, branch, dmerged) + (() if fresh else (dproj,))))


def _outproj_loss(merged, w_out, x, target, fnw):
    s, d = x.shape
    tr = 256

    def body(m_ref, w_ref, x_ref, t_ref, fw_ref, dof_ref, dob_ref, loss_ref, g_ref):
        out = x_ref[...] + _nn(m_ref[...], w_ref[...])
        r = lax.rsqrt(jnp.mean(out * out, axis=-1, keepdims=True) + RMS_EPS)
        nrm = out * r
        fw = fw_ref[...]
        err = nrm * fw - t_ref[...]
        dy = err * (1.0 / d)
        gy = dy * fw
        dout = r * (gy - nrm * jnp.mean(gy * nrm, axis=-1, keepdims=True))
        dof_ref[...] = dout
        dob_ref[...] = dout.astype(BF16)

        @pl.when(pl.program_id(0) == 0)
        def _():
            loss_ref[...] = jnp.zeros_like(loss_ref)
            g_ref[...] = jnp.zeros_like(g_ref)

        loss_ref[...] += jnp.sum(jnp.sum(err * err, axis=1, keepdims=True), axis=0, keepdims=True) * (0.5 / d)
        g_ref[...] += jnp.sum(dy * nrm, axis=0, keepdims=True)

    blk = pl.BlockSpec((tr, d), lambda i: (i, 0))
    return pl.pallas_call(
        body, out_shape=(SDS((s, d), F32), SDS((s, d), BF16), SDS((1, LANES), F32), SDS((1, d), F32)), grid=(s // tr,),
        in_specs=[blk, pl.BlockSpec((d, d), lambda i: (0, 0)), blk, blk, pl.BlockSpec((1, d), lambda i: (0, 0))],
        out_specs=(blk, blk, pl.BlockSpec((1, LANES), lambda i: (0, 0)), pl.BlockSpec((1, d), lambda i: (0, 0))),
        compiler_params=_params(("arbitrary",)), name="outproj_loss")(merged, w_out, x, target, fnw)


ELEMWISE_BLOCK_BYTES = 1 << 20


def _row_block(rows, cols, itemsize=4):
    best = None
    for tr in range(16, rows + 1, 16):
        if rows % tr == 0 and tr * cols * itemsize <= ELEMWISE_BLOCK_BYTES:
            best = tr
    return best if best is not None else rows


def _adamw(w, g, m, v, name):
    rows, cols = w.shape
    tr = _row_block(rows, cols)
    if rows // tr > 64 and cols % LANES == 0:
        blk, grid = pl.BlockSpec((rows, LANES), lambda i: (0, i)), (cols // LANES,)
    else:
        blk, grid = pl.BlockSpec((tr, cols), lambda i: (i, 0)), (rows // tr,)
    out = SDS((rows, cols), F32)
    return pl.pallas_call(
        _adamw_body(), out_shape=(out, out, out), grid=grid, in_specs=[blk] * 4, out_specs=(blk,) * 3,
        compiler_params=_params(("parallel",)), name=name)(w, g, m, v)


def _adamw_body():
    def body(w_ref, g_ref, m_ref, v_ref, d_ref, nm_ref, nv_ref):
        gv = g_ref[...]
        nm = ADAM_B1 * m_ref[...] + (1.0 - ADAM_B1) * gv
        nv = ADAM_B2 * v_ref[...] + (1.0 - ADAM_B2) * jnp.square(gv)
        m_hat = nm / (1.0 - ADAM_B1 ** ADAM_STEP)
        v_hat = nv / (1.0 - ADAM_B2 ** ADAM_STEP)
        d_ref[...] = -ADAM_LR * (m_hat / (jnp.sqrt(v_hat) + ADAM_EPS) + ADAM_WD * w_ref[...])
        nm_ref[...] = nm
        nv_ref[...] = nv

    return body


HBM_SPEC = pl.BlockSpec(memory_space=pl.ANY)


def _position():
    return lax.axis_index("x"), lax.axis_index("y"), lax.axis_index("c")


class _Carry:
    def __init__(self, arrays, out_shapes, sems, start, finish):
        self.arrays, self.out_shapes, self.sems, self.start, self.finish = list(arrays), out_shapes, sems, start, finish

    def sem_shapes(self):
        return [pltpu.SemaphoreType.DMA((k,)) for k in self.sems]


def _gather_carry(shards, by_cols=()):
    n = len(shards)

    def copies(ins, outs, sems):
        send_sems, recv_sems, fsend_sems, frecv_sems = sems
        x, y, c = _position()
        me = 2 * x + y
        peers = [(1 - x, y), (x, 1 - y), (1 - x, 1 - y)]

        def half_of(t, chip, half):
            if t in by_cols:
                c2 = ins[t].shape[1] // 2
                return outs[t].at[chip, :, pl.ds(half * c2, c2)]
            return outs[t].at[chip, half]

        def over_ici(t, p, chip):
            px, py = peers[p]
            if t in by_cols:
                c2 = ins[t].shape[1] // 2
                src = ins[t].at[:, pl.ds(c * c2, c2)]
            else:
                r2 = ins[t].shape[0] // 2
                src = ins[t].at[pl.ds(c * r2, r2), :]
            return pltpu.make_async_remote_copy(
                src_ref=src, dst_ref=half_of(t, chip, c), send_sem=send_sems.at[3 * t + p],
                recv_sem=recv_sems.at[3 * t + p], device_id=(px, py, c), device_id_type=MESH)

        def to_sibling(t, p, half):
            px, py = peers[p]
            slab = half_of(t, 2 * px + py, half)
            return pltpu.make_async_remote_copy(
                src_ref=slab, dst_ref=slab, send_sem=fsend_sems.at[3 * t + p], recv_sem=frecv_sems.at[3 * t + p],
                device_id=(x, y, 1 - c), device_id_type=MESH)

        pairs = [(t, p) for t in range(n) for p in range(3)]
        sends = [over_ici(t, p, me) for t, p in pairs]
        lands = [over_ici(t, p, 2 * peers[p][0] + peers[p][1]) for t, p in pairs]
        passed = [to_sibling(t, p, c) for t, p in pairs]
        from_sibling = [to_sibling(t, p, 1 - c) for t, p in pairs]
        return sends, lands, passed, from_sibling

    def start(ins, outs, sems):
        for cp in copies(ins, outs, sems)[0]:
            cp.start()

    def finish(ins, outs, sems):
        sends, lands, passed, from_sibling = copies(ins, outs, sems)
        for land, fwd in zip(lands, passed):
            land.wait_recv()
            fwd.start()
        for cp in from_sibling:
            cp.wait_recv()
        for cp in sends + passed:
            cp.wait_send()

    shapes = [SDS((N_CHIPS,) + a.shape if t in by_cols else (N_CHIPS, 2, a.shape[0] // 2, a.shape[1]), a.dtype)
              for t, a in enumerate(shards)]
    return _Carry(shards, shapes, [3 * n] * 4, start, finish)


def _scatter_carry(parts):
    def start(ins, outs, sems):
        for cp in _scatter_copies(ins, outs, *sems)[0]:
            cp.start()

    def finish(ins, outs, sems):
        sends, lands = _scatter_copies(ins, outs, *sems)
        for cp in lands:
            cp.wait_recv()
        for cp in sends:
            cp.wait_send()

    return _Carry(parts, [SDS(a.shape, a.dtype) for a in parts], [3 * len(parts)] * 2, start, finish)


def _with_own(gathered, own, chip):
    full = gathered.reshape((N_CHIPS,) + own.shape)
    return lax.dynamic_update_index_in_dim(full, own, chip, 0)


def _exchange_halves(grads):
    n = len(grads)
    slabs = [list(g) if isinstance(g, (list, tuple)) else [g] for g in grads]
    flat = [a for s in slabs for a in s]
    ncp = len(flat)

    def body(*refs):
        ins, outs = refs[:ncp], refs[ncp:ncp + n]
        send_sems, recv_sems = refs[ncp + n:]
        x, y, c = _position()
        cps, k = [], 0
        for t in range(n):
            for j in range(len(slabs[t])):
                if len(slabs[t]) == 1:
                    r2 = ins[k].shape[1] // 2
                    src, dst = ins[k].at[:, pl.ds((1 - c) * r2, r2), :], outs[t]
                else:
                    r2 = ins[k].shape[0] // 2
                    src, dst = ins[k].at[pl.ds((1 - c) * r2, r2), :], outs[t].at[j]
                cps.append(pltpu.make_async_remote_copy(
                    src_ref=src, dst_ref=dst, send_sem=send_sems.at[k], recv_sem=recv_sems.at[k],
                    device_id=(x, y, 1 - c), device_id_type=MESH))
                k += 1
        for cp in cps:
            cp.start()
        for cp in cps:
            cp.wait()

    def landing(s):
        a = s[0]
        return SDS((N_CHIPS, a.shape[-2] // 2, a.shape[-1]), a.dtype)

    return pl.pallas_call(
        body, out_shape=[landing(s) for s in slabs],
        in_specs=[HBM_SPEC] * ncp, out_specs=[HBM_SPEC] * n,
        scratch_shapes=[pltpu.SemaphoreType.DMA((ncp,)), pltpu.SemaphoreType.DMA((ncp,))],
        compiler_params=pltpu.CompilerParams(has_side_effects=True), name="reduce_sibling")(*flat)


def _scatter_copies(ins, outs, send_sems, recv_sems):
    x, y, c = _position()
    me = 2 * x + y
    peers = [(1 - x, y), (x, 1 - y), (1 - x, 1 - y)]

    def remote(t, p, src_slab, dst_slab):
        px, py = peers[p]
        return pltpu.make_async_remote_copy(
            src_ref=ins[t].at[src_slab], dst_ref=outs[t].at[dst_slab], send_sem=send_sems.at[3 * t + p],
            recv_sem=recv_sems.at[3 * t + p], device_id=(px, py, c), device_id_type=MESH)

    n = len(ins)
    sends = [remote(t, p, 2 * peers[p][0] + peers[p][1], me) for t in range(n) for p in range(3)]
    lands = [remote(t, p, me, 2 * peers[p][0] + peers[p][1]) for t in range(n) for p in range(3)]
    return sends, lands


def _share_halves(halves):
    n = len(halves)

    def body(*refs):
        ins, outs = refs[:n], refs[n:2 * n]
        send_sems, recv_sems = refs[2 * n:]
        x, y, c = _position()

        def copy(t, slab):
            return pltpu.make_async_remote_copy(
                src_ref=ins[t].at[slab], dst_ref=outs[t].at[slab], send_sem=send_sems.at[t], recv_sem=recv_sems.at[t],
                device_id=(x, y, 1 - c), device_id_type=MESH)

        for t in range(n):
            copy(t, c).start()
        for t in range(n):
            copy(t, 1 - c).wait_recv()
        for t in range(n):
            copy(t, c).wait_send()

    return pl.pallas_call(
        body, out_shape=[SDS(a.shape, a.dtype) for a in halves],
        in_specs=[HBM_SPEC] * n, out_specs=[HBM_SPEC] * n,
        scratch_shapes=[pltpu.SemaphoreType.DMA((n,)), pltpu.SemaphoreType.DMA((n,))],
        input_output_aliases={t: t for t in range(n)},
        compiler_params=pltpu.CompilerParams(has_side_effects=True), name="share_sibling")(*halves)


def _add_sibling(grad, recv, core):
    nch, r2, cols = recv.shape
    tr = _row_block(r2, cols)
    nb = r2 // tr

    def body(c_ref, g_ref, r_ref, o_ref):
        del c_ref
        o_ref[...] = (g_ref[...].astype(F32) + r_ref[...].astype(F32)).astype(BF16)

    return pl.pallas_call(
        body, out_shape=SDS(recv.shape, BF16),
        grid_spec=pltpu.PrefetchScalarGridSpec(
            num_scalar_prefetch=1, grid=(nch, nb),
            in_specs=[pl.BlockSpec((None, tr, cols), lambda j, i, c_ref: (j, c_ref[0] * nb + i, 0)),
                      pl.BlockSpec((None, tr, cols), lambda j, i, c_ref: (j, i, 0))],
            out_specs=pl.BlockSpec((None, tr, cols), lambda j, i, c_ref: (j, i, 0))),
        compiler_params=_params(("parallel", "parallel")), name="add_sibling")(core, grad, recv)


def _add_chips(own, recv, chip_core):
    nch, r2, cols = recv.shape
    tr = _row_block(r2, cols)

    nsc = 2 + nch

    def body(*refs):
        me = refs[0][0]
        own_ref, p_refs, o_ref = refs[nsc], refs[nsc + 1:nsc + 1 + nch], refs[nsc + 1 + nch]
        acc = None
        for j in range(nch):
            term = jnp.where(me == j, own_ref[...], p_refs[j][...]).astype(F32)
            acc = term if acc is None else acc + term
        o_ref[...] = acc

    def slab(j):
        return pl.BlockSpec((None, tr, cols), lambda i, *sc: (sc[2 + j][0], i, 0))

    return pl.pallas_call(
        body, out_shape=SDS((2, r2, cols), F32),
        grid_spec=pltpu.PrefetchScalarGridSpec(
            num_scalar_prefetch=nsc, grid=(r2 // tr,),
            in_specs=[pl.BlockSpec((None, tr, cols), lambda i, *sc: (sc[0][0], i, 0))] + [slab(j) for j in range(nch)],
            out_specs=pl.BlockSpec((None, tr, cols), lambda i, *sc: (sc[1][0], i, 0))),
        compiler_params=_params(("parallel",)), name="add_chips")(*chip_core, own, *([recv] * nch))


def _exchange_col_halves(grad):
    nch, r, cols = grad.shape
    c2 = cols // 2

    def body(in_ref, out_ref, send_sem, recv_sem):
        x, y, c = _position()
        cp = pltpu.make_async_remote_copy(
            src_ref=in_ref.at[:, :, pl.ds((1 - c) * c2, c2)], dst_ref=out_ref, send_sem=send_sem.at[0],
            recv_sem=recv_sem.at[0], device_id=(x, y, 1 - c), device_id_type=MESH)
        cp.start()
        cp.wait()

    return pl.pallas_call(
        body, out_shape=SDS((nch, r, c2), grad.dtype), in_specs=[HBM_SPEC], out_specs=HBM_SPEC,
        scratch_shapes=[pltpu.SemaphoreType.DMA((1,)), pltpu.SemaphoreType.DMA((1,))],
        compiler_params=pltpu.CompilerParams(has_side_effects=True), name="reduce_sibling_cols")(grad)


def _add_sibling_cols(grad, recv, core):
    nch, r, c2 = recv.shape
    nb = c2 // LANES

    def body(c_ref, g_ref, r_ref, o_ref):
        del c_ref
        o_ref[...] = (g_ref[...].astype(F32) + r_ref[...].astype(F32)).astype(BF16)

    blk = pl.BlockSpec((None, r, LANES), lambda j, i, c_ref: (j, 0, i))
    return pl.pallas_call(
        body, out_shape=SDS(recv.shape, BF16),
        grid_spec=pltpu.PrefetchScalarGridSpec(
            num_scalar_prefetch=1, grid=(nch, nb),
            in_specs=[pl.BlockSpec((None, r, LANES), lambda j, i, c_ref: (j, 0, c_ref[0] * nb + i)), blk],
            out_specs=blk),
        compiler_params=_params(("parallel", "parallel")), name="add_sibling_cols")(core, grad, recv)


def _add_chips_cols(own, recv, chip_core):
    nch, r, c2 = recv.shape
    nb = c2 // LANES
    nsc = 2 + nch

    def body(*refs):
        me = refs[0][0]
        own_ref, p_refs, o_ref = refs[nsc], refs[nsc + 1:nsc + 1 + nch], refs[nsc + 1 + nch]
        acc = None
        for j in range(nch):
            term = jnp.where(me == j, own_ref[...], p_refs[j][...]).astype(F32)
            acc = term if acc is None else acc + term
        o_ref[...] = acc

    def slab(j):
        return pl.BlockSpec((None, r, LANES), lambda i, *sc: (sc[2 + j][0], 0, i))

    return pl.pallas_call(
        body, out_shape=SDS((r, 2 * c2), F32),
        grid_spec=pltpu.PrefetchScalarGridSpec(
            num_scalar_prefetch=nsc, grid=(nb,),
            in_specs=[pl.BlockSpec((None, r, LANES), lambda i, *sc: (sc[0][0], 0, i))] + [slab(j) for j in range(nch)],
            out_specs=pl.BlockSpec((r, LANES), lambda i, *sc: (0, sc[1][0] * nb + i))),
        compiler_params=_params(("parallel",)), name="add_chips_cols")(*chip_core, own, *([recv] * nch))


def _share_col_halves(full):
    r, cols = full.shape
    c2 = cols // 2

    def body(in_ref, out_ref, send_sem, recv_sem):
        x, y, c = _position()

        def copy(half):
            return pltpu.make_async_remote_copy(
                src_ref=in_ref.at[:, pl.ds(half * c2, c2)], dst_ref=out_ref.at[:, pl.ds(half * c2, c2)],
                send_sem=send_sem.at[0], recv_sem=recv_sem.at[0], device_id=(x, y, 1 - c), device_id_type=MESH)

        copy(c).start()
        copy(1 - c).wait_recv()
        copy(c).wait_send()

    return pl.pallas_call(
        body, out_shape=SDS(full.shape, full.dtype), in_specs=[HBM_SPEC], out_specs=HBM_SPEC,
        scratch_shapes=[pltpu.SemaphoreType.DMA((1,)), pltpu.SemaphoreType.DMA((1,))],
        input_output_aliases={0: 0},
        compiler_params=pltpu.CompilerParams(has_side_effects=True), name="share_sibling_cols")(full)


def _allreduce_small(pack):
    rows = pack.shape[0]

    def body(p_ref, o_ref, buf, send_sems, recv_sems):
        x, y, c = _position()
        me = 4 * x + 2 * y + c
        buf[me] = p_ref[...]

        def copy(dst_dev, slot):
            return pltpu.make_async_remote_copy(
                src_ref=p_ref, dst_ref=buf.at[slot], send_sem=send_sems.at[dst_dev], recv_sem=recv_sems.at[slot],
                device_id=(dst_dev // 4, (dst_dev // 2) % 2, dst_dev % 2), device_id_type=MESH)

        for dev in range(N_DEV):
            @pl.when(dev != me)
            def _():
                copy(dev, me).start()
        for dev in range(N_DEV):
            @pl.when(dev != me)
            def _():
                copy(dev, dev).wait_recv()
        for dev in range(N_DEV):
            @pl.when(dev != me)
            def _():
                copy(dev, me).wait_send()
        acc = buf[0]
        for dev in range(1, N_DEV):
            acc = acc + buf[dev]
        o_ref[...] = acc

    return pl.pallas_call(
        body, out_shape=SDS(pack.shape, F32),
        in_specs=[pl.BlockSpec(memory_space=pltpu.VMEM)], out_specs=pl.BlockSpec(memory_space=pltpu.VMEM),
        scratch_shapes=[pltpu.VMEM((N_DEV, rows, LANES), F32), pltpu.SemaphoreType.DMA((N_DEV,)),
                        pltpu.SemaphoreType.DMA((N_DEV,))],
        compiler_params=pltpu.CompilerParams(has_side_effects=True), name="allreduce_small")(pack)


ATTN_TQ = 128


def _local_step(cfg, x, target, w, to_chips=None, late=None, hn=None):
    d = cfg.D
    if hn is None:
        hn = _rmsnorm_fwd(x, w["norm_w"])
    proj = _mm(hn, w["w_main_t"], "nt", BF16, "proj_main", carry=late[0] if late else None, b_rows=cfg.NM)
    if late:
        proj, arrived = proj
        w = {**w, **late[1](arrived)}
    dt_raw = _mm(hn, w["w_dt_t"], "nt", F32, "proj_dt")
    slopes = _slopes(cfg.H)
    near = _Pass(ATTN_TQ, DILATED_PATTERNS[:-1], 1, cfg.S)
    far = _Pass(LANES, DILATED_PATTERNS[-1:], DEINT, cfg.S // DEINT)
    tab_near, tab_far = _attn_tables(near), _attn_tables(far)
    cols_near, cols_far = (cfg.OQ, cfg.OK, cfg.OV), (0, d, 2 * d)
    qkv_far = _deinterleave(proj, 0, 3 * d, "attn_deinterleave")
    o_1, lse_1 = _attn_fwd(cfg, near, proj, cols_near, tab_near, slopes, "attn_fwd_near")
    o_2, lse_2 = _attn_fwd(cfg, far, qkv_far, cols_far, tab_far, slopes, "attn_fwd_far")
    o_a, oag, lse = _attn_merge(cfg, proj, o_1, lse_1, o_2, lse_2)
    xact = _conv_fwd(cfg, proj, w["conv_w"], w["conv_b"])
    e = _expansion_matrix(cfg)
    y, y2, states = _ssd_fwd(cfg, xact, dt_raw, proj, w["dt_bias"], w["a_log"], w["d_skip"], w["ssm_norm_w"], e)
    a_br = _mm(oag, w["w_attn"], "nn", BF16, "branch_attn")
    s_br = _mm(y2, w["w_ssm"], "nn", BF16, "branch_ssm")
    merged = _merge_fwd(cfg, proj, a_br, s_br)
    dout_f, dout_b, loss_row, g_fnw = _outproj_loss(merged, w["w_out"], x, target, w["final_norm_w"])

    dmerged = _mm(dout_b, w["w_out"], "nt", BF16, "d_merged")
    g_w_out = _mm(merged, dout_b, "tn", BF16, "g_w_out")
    dproj, da_br = _merge_bwd(cfg, proj, a_br, dmerged, cfg.OGA, None, "merge_bwd_attn")
    dproj, ds_br = _merge_bwd(cfg, proj, s_br, dmerged, cfg.OGS, dproj, "merge_bwd_ssm")
    doag = _mm(da_br, w["w_attn"], "nt", BF16, "d_oag")
    g_w_attn = _mm(oag, da_br, "tn", BF16, "g_w_attn")
    dy2 = _mm(ds_br, w["w_ssm"], "nt", BF16, "d_y2")
    g_w_ssm = _mm(y2, ds_br, "tn", BF16, "g_w_ssm")
    dproj, dxact, ddt, g_snw, g_dtb, g_alog, g_dsk = _ssd_bwd(
        cfg, xact, dt_raw, proj, y, dy2, states, w["dt_bias"], w["a_log"], w["d_skip"], w["ssm_norm_w"], e, dproj)
    dproj, g_cw, g_cb = _conv_bwd(cfg, proj, dxact, w["conv_w"], w["conv_b"], dproj)
    dproj, do, do_far, dl, dl_far, lse_far = _attn_bwd_prep(cfg, proj, o_a, doag, lse, dproj)
    g_near = _attn_bwd(cfg, near, proj, cols_near, do, lse, dl, tab_near, slopes, "attn_bwd_near")
    g_far = _attn_bwd(cfg, far, qkv_far, cols_far, do_far, lse_far, dl_far, tab_far, slopes, "attn_bwd_far")
    for g_1, g_2, col0, nm in zip(g_near, g_far, cols_near, ("attn_dq", "attn_dk", "attn_dv")):
        dproj = _attn_grad_sum(cfg, g_1, g_2, col0, dproj, nm)
    ddt_b = ddt.astype(BF16)
    g_w_main = _mm(dproj, hn, "tn", BF16, "g_w_main", out_rows=cfg.N_IN)
    g_w_dt = _mm(ddt_b, hn, "tn", BF16, "g_w_dt")
    grads = dict(w_main_t=g_w_main, w_dt_t=g_w_dt, conv_w=g_cw, conv_b=g_cb, dt_bias=g_dtb, a_log=g_alog,
                 d_skip=g_dsk, ssm_norm_w=g_snw, w_attn=g_w_attn, w_ssm=g_w_ssm, w_out=g_w_out, final_norm_w=g_fnw)
    sent = to_chips(grads) if to_chips is not None else ()
    dhn = _mm(dproj, w["w_main_t"], "nn", F32, "d_hn", tk=1024, carry=_scatter_carry(sent) if sent else None,
              b_rows=cfg.NM)
    landed = ()
    if sent:
        dhn, landed = dhn
    dhn_dt = _mm(ddt_b, w["w_dt_t"], "nn", F32, "d_hn_dt")
    grad_x, grads["norm_w"] = _rmsnorm_bwd(x, w["norm_w"], dhn, dhn_dt, dout_f)
    return loss_row, grad_x, grads, sent, landed


def _pad_lanes(v):
    return jnp.pad(v, ((0, 0), (0, LANES - v.shape[1])))


def _main_from_rows(cfg, w_in_t):
    lo, hi = cfg.OGA, cfg.OGA + cfg.NH
    dt = jnp.pad(w_in_t[lo:hi], ((0, LANES - cfg.NH), (0, 0)))
    return lax.dynamic_update_slice(w_in_t, w_in_t[hi:], (lo, 0)), dt


def _rows_from_main(cfg, g_main_t, g_dt_t):
    lo, hi = cfg.OGA, cfg.OGA + cfg.NH
    g = lax.dynamic_update_slice(g_main_t, g_main_t[lo:cfg.NM], (hi, 0))
    return lax.dynamic_update_slice(g, g_dt_t[:cfg.NH], (lo, 0))


def _full_weights(cfg, norm_w, w_in_t, conv_w, conv_b, dt_bias, a_log, d_skip, ssm_norm_w, w_attn, w_ssm, w_out, fnw):
    w_main, w_dt = _main_from_rows(cfg, w_in_t)
    return dict(norm_w=norm_w, w_main_t=w_main.astype(BF16), w_dt_t=w_dt.astype(BF16), conv_w=conv_w, conv_b=conv_b,
                dt_bias=_pad_lanes(dt_bias), a_log=_pad_lanes(a_log), d_skip=_pad_lanes(d_skip), ssm_norm_w=ssm_norm_w,
                final_norm_w=fnw, **{k: v.astype(BF16) for k, v in (("w_attn", w_attn), ("w_ssm", w_ssm), ("w_out", w_out))
                                     if v is not None})


def kernel(x, norm_w, w_in, conv_w, conv_b, dt_bias, a_log, d_skip, ssm_norm_w, w_attn_branch, w_ssm_branch, w_out, final_norm_w, loss_target, m_norm_w, m_w_in, m_conv_w, m_conv_b, m_dt_bias, m_a_log, m_d_skip, m_ssm_norm_w, m_w_attn_branch, m_w_ssm_branch, m_w_out, m_final_norm_w, v_norm_w, v_w_in, v_conv_w, v_conv_b, v_dt_bias, v_a_log, v_d_skip, v_ssm_norm_w, v_w_attn_branch, v_w_ssm_branch, v_w_out, v_final_norm_w):
    cfg = _Cfg(x.shape[1], x.shape[2])
    d, si, cd, nh = cfg.D, cfg.SI, cfg.CD, cfg.NH
    chip = 2 * lax.axis_index("x") + lax.axis_index("y")
    core = lax.axis_index("c").astype(jnp.int32).reshape(1)
    chip = chip.astype(jnp.int32)
    chip_core = [chip.reshape(1), core] + [jnp.where(chip == j, (j + 1) % N_CHIPS, j).astype(jnp.int32).reshape(1)
                                           for j in range(N_CHIPS)]

    own = [jnp.transpose(w_in[0]).astype(BF16), conv_w[0].reshape(4 * CONV_K, -1)]
    hn, gathered = _rmsnorm_fwd(x[0], norm_w, carry=_gather_carry(own, by_cols=(0,)))
    a_in, a_cw = [_with_own(g, o, chip) for g, o in zip(gathered, own)]
    conv_w_full = a_cw.reshape(N_CHIPS, CONV_K, cd // N_CHIPS).transpose(1, 0, 2).reshape(CONV_K, cd)
    w = _full_weights(cfg, norm_w, a_in.reshape(cfg.N_IN, d), conv_w_full, conv_b, dt_bias, a_log, d_skip,
                      ssm_norm_w, None, None, None, final_norm_w.reshape(1, d))
    own_late = [w_attn_branch[0].astype(BF16), w_ssm_branch[0].astype(BF16), w_out[0].astype(BF16)]

    def late_weights(arrived):
        a_attn, a_ssm, a_out = [_with_own(g, o, chip) for g, o in zip(arrived, own_late)]
        return dict(w_attn=a_attn.reshape(d, d), w_ssm=a_ssm.reshape(si, d), w_out=a_out.reshape(d, d))

    def to_chips(grads):
        g_in_t = _rows_from_main(cfg, grads["w_main_t"], grads["w_dt_t"]).reshape(N_CHIPS, cfg.N_IN // N_CHIPS, d)
        by_chip = [grads["w_attn"].reshape(N_CHIPS, d // N_CHIPS, d),
                   grads["w_ssm"].reshape(N_CHIPS, si // N_CHIPS, d),
                   grads["w_out"].reshape(N_CHIPS, d // N_CHIPS, d)]
        from_sibling = _exchange_halves(by_chip)
        return ([_add_sibling_cols(g_in_t, _exchange_col_halves(g_in_t), core)]
                + [_add_sibling(g, r, core) for g, r in zip(by_chip, from_sibling)])

    loss_row, grad_x, grads, chip_sums, from_chips = _local_step(
        cfg, x[0], loss_target[0], w, to_chips, (_gather_carry(own_late), late_weights), hn)
    g_in_t = _share_col_halves(_add_chips_cols(chip_sums[0], from_chips[0], chip_core))
    halves = [_add_chips(o, p, chip_core) for o, p in zip(chip_sums[1:], from_chips[1:])]
    g_attn, g_ssm, g_out = [h.reshape(2 * h.shape[1], h.shape[2]) for h in _share_halves(halves)]
    g_in = jnp.transpose(g_in_t)

    small = [loss_row, grads["norm_w"], grads["conv_b"], grads["dt_bias"], grads["a_log"], grads["d_skip"],
             grads["ssm_norm_w"], grads["final_norm_w"], grads["conv_w"].reshape(1, CONV_K * cd)]
    sizes = [a.shape[1] for a in small]
    total = sum(sizes)
    rows = -(-total // (8 * LANES)) * 8
    flat = jnp.pad(jnp.concatenate(small, axis=1), ((0, 0), (0, rows * LANES - total)))
    red = _allreduce_small(flat.reshape(rows, LANES)).reshape(1, rows * LANES)
    offs = [sum(sizes[:i]) for i in range(len(sizes))]
    loss_r, g_nw, g_cb, g_dtb, g_alog, g_dsk, g_snw, g_fnw, g_cw_flat = [
        red[:, o:o + n] for o, n in zip(offs, sizes)]
    loss = loss_r[0, 0]
    g_dtb, g_alog, g_dsk = g_dtb[:, :nh], g_alog[:, :nh], g_dsk[:, :nh]
    cshard = cd // N_CHIPS
    g_cw = lax.dynamic_slice_in_dim(g_cw_flat.reshape(CONV_K, cd), chip * cshard, cshard, axis=1)

    upd = {}
    upd["w_in"] = tuple(jnp.transpose(u) for u in _adamw(
        jnp.transpose(w_in[0]), g_in_t, jnp.transpose(m_w_in[0]), jnp.transpose(v_w_in[0]), "adamw_w_in"))
    for name, wv, gv, mv, vv in [("w_attn", w_attn_branch[0], g_attn, m_w_attn_branch[0], v_w_attn_branch[0]),
                                 ("w_ssm", w_ssm_branch[0], g_ssm, m_w_ssm_branch[0], v_w_ssm_branch[0]),
                                 ("w_out", w_out[0], g_out, m_w_out[0], v_w_out[0])]:
        upd[name] = _adamw(wv, gv, mv, vv, "adamw_" + name)
    names = ["norm_w", "conv_w", "conv_b", "dt_bias", "a_log", "d_skip", "ssm_norm_w", "final_norm_w"]
    ws = [norm_w, conv_w[0].reshape(1, -1), conv_b, dt_bias, a_log, d_skip, ssm_norm_w, final_norm_w.reshape(1, d)]
    gs = [g_nw, g_cw.reshape(1, -1), g_cb, g_dtb, g_alog, g_dsk, g_snw, g_fnw]
    ms = [m_norm_w, m_conv_w[0].reshape(1, -1), m_conv_b, m_dt_bias, m_a_log, m_d_skip, m_ssm_norm_w,
          m_final_norm_w.reshape(1, d)]
    vs = [v_norm_w, v_conv_w[0].reshape(1, -1), v_conv_b, v_dt_bias, v_a_log, v_d_skip, v_ssm_norm_w,
          v_final_norm_w.reshape(1, d)]
    ssz = [a.shape[1] for a in ws]
    stot = sum(ssz)
    srows = -(-stot // (8 * LANES)) * 8

    def pack(parts):
        return jnp.pad(jnp.concatenate(parts, axis=1), ((0, 0), (0, srows * LANES - stot))).reshape(srows, LANES)

    packed = _adamw(pack(ws), pack(gs), pack(ms), pack(vs), "adamw_small")
    soffs = [sum(ssz[:i]) for i in range(len(ssz))]
    for k, nm in enumerate(names):
        upd[nm] = tuple(p.reshape(1, srows * LANES)[:, soffs[k]:soffs[k] + ssz[k]] for p in packed)

    shapes = dict(norm_w=norm_w.shape, w_in=w_in.shape, conv_w=conv_w.shape, conv_b=conv_b.shape, dt_bias=dt_bias.shape,
                  a_log=a_log.shape, d_skip=d_skip.shape, ssm_norm_w=ssm_norm_w.shape, w_attn=w_attn_branch.shape,
                  w_ssm=w_ssm_branch.shape, w_out=w_out.shape, final_norm_w=final_norm_w.shape)
    order = ["norm_w", "w_in", "conv_w", "conv_b", "dt_bias", "a_log", "d_skip", "ssm_norm_w", "w_attn", "w_ssm",
             "w_out", "final_norm_w"]
    gradv = dict(norm_w=g_nw, w_in=g_in, conv_w=g_cw, conv_b=g_cb, dt_bias=g_dtb, a_log=g_alog, d_skip=g_dsk,
                 ssm_norm_w=g_snw, w_attn=g_attn, w_ssm=g_ssm, w_out=g_out, final_norm_w=g_fnw)
    outs = [loss, grad_x[None]]
    outs += [gradv[n].reshape(shapes[n]) for n in order]
    for k in range(3):
        outs += [upd[n][k].reshape(shapes[n]) for n in order]
    return tuple(outs)
```

```python
import jax
import jax.numpy as jnp
from jax import lax
from jax.experimental import pallas as pl
from jax.experimental.pallas import tpu as pltpu

F32 = jnp.float32
BF16 = jnp.bfloat16
SDS = jax.ShapeDtypeStruct

RMS_EPS = 1e-6
LANES = 128
CHUNK = 128
SSM_HEAD_DIM = 64
SSM_GROUPS = 8
SSM_STATE = 128
CONV_K = 4
ATTN_HEAD_DIM = 128
DILATED_PATTERNS = ((128, 1), (512, 4), (2048, 16))
NEG = -1e30
VMEM_LIMIT = 56 * 1024 * 1024
ADAM_LR, ADAM_B1, ADAM_B2, ADAM_EPS, ADAM_WD, ADAM_STEP = 0.001, 0.9, 0.999, 1e-08, 0.01, 10
MESH = pl.DeviceIdType.MESH
N_CHIPS = 4
N_DEV = 8


class _Cfg:
    def __init__(self, s, d):
        self.S, self.D = s, d
        self.H = d // ATTN_HEAD_DIM
        self.SI = 2 * d
        self.NH = self.SI // SSM_HEAD_DIM
        self.HPG = self.NH // SSM_GROUPS
        self.GW = self.HPG * SSM_HEAD_DIM
        self.BC = SSM_GROUPS * SSM_STATE
        self.CD = self.SI + 2 * self.BC
        self.OQ, self.OK, self.OV, self.OZA = 0, d, 2 * d, 3 * d
        self.OZS = 4 * d
        self.OXBC = self.OZS + self.SI
        self.OGA = self.OXBC + self.CD
        self.OGS = self.OGA + d
        self.NM = self.OGS + d
        self.N_IN = self.NM + self.NH
        assert self.GW % LANES == 0 and self.NH <= LANES and s % 512 == 0 and d % 512 == 0


def _params(sem=None):
    return pltpu.CompilerParams(dimension_semantics=sem, vmem_limit_bytes=VMEM_LIMIT)


def _sigmoid(x):
    return 0.5 * jnp.tanh(0.5 * x) + 0.5


def _softplus(x):
    u = jnp.exp(-jnp.abs(x))
    l1p = jnp.where(u < 1e-3, u * (1.0 - u * (0.5 - u * (1.0 / 3.0))), jnp.log(1.0 + u))
    return jnp.maximum(x, 0.0) + l1p


def _nt(a, b):
    return lax.dot_general(a, b, (((1,), (1,)), ((), ())), preferred_element_type=F32)


def _tn(a, b):
    return lax.dot_general(a, b, (((0,), (0,)), ((), ())), preferred_element_type=F32)


def _nn(a, b):
    return jnp.dot(a, b, preferred_element_type=F32)


def _tile(n, target):
    if n <= target:
        return n
    best = None
    for t in range(LANES, target + 1, LANES):
        if n % t == 0:
            best = t
    assert best is not None, (n, target)
    return best


MM_TK = {"nn": 2048, "nt": 2048, "tn": 1024}


def _mm(a, b, dims, out_dtype, name, tm=1024, tn=2048, tk=None, init=None, carry=None, b_rows=None, out_rows=None):
    tk = MM_TK[dims] if tk is None else tk
    if dims == "nn":
        (m, k), (k2, n) = a.shape, b.shape
        k2 = k2 if b_rows is None else b_rows
    elif dims == "nt":
        (m, k), (n, k2) = a.shape, b.shape
        n = n if b_rows is None else b_rows
    else:
        (k, m), (k2, n) = a.shape, b.shape
    assert k == k2
    tm, tn, tk = _tile(m, tm), _tile(n, tn), _tile(k, tk)
    nk = k // tk
    if dims == "tn":
        a_spec = pl.BlockSpec((tk, tm), lambda i, j, kk: (kk, i))
    else:
        a_spec = pl.BlockSpec((tm, tk), lambda i, j, kk: (i, kk))
    if dims == "nt":
        b_spec = pl.BlockSpec((tn, tk), lambda i, j, kk: (j, kk))
    else:
        b_spec = pl.BlockSpec((tk, tn), lambda i, j, kk: (kk, j))
    o_spec = pl.BlockSpec((tm, tn), lambda i, j, kk: (i, j))
    op = {"nn": _nn, "nt": _nt, "tn": _tn}[dims]
    has_init = init is not None
    nx = len(carry.arrays) if carry is not None else 0
    ni, nj = m // tm, n // tn

    def body(*refs):
        a_ref, b_ref = refs[0], refs[1]
        i_ref = refs[2] if has_init else None
        x_in = refs[2 + has_init:2 + has_init + nx]
        o_ref = refs[2 + has_init + nx]
        x_out = refs[3 + has_init + nx:3 + has_init + 2 * nx]
        acc = refs[3 + has_init + 2 * nx]
        x_sems = refs[4 + has_init + 2 * nx:]
        i, j, kk = pl.program_id(0), pl.program_id(1), pl.program_id(2)

        if nx:
            @pl.when((i == 0) & (j == 0) & (kk == 0))
            def _():
                carry.start(x_in, x_out, x_sems)

        prod = lambda: op(a_ref[...], b_ref[...])
        with_init = (lambda p: p + i_ref[...].astype(F32)) if has_init else (lambda p: p)
        if nk == 1:
            o_ref[...] = with_init(prod()).astype(out_dtype)
        else:
            @pl.when(kk == 0)
            def _():
                acc[...] = with_init(prod())

            @pl.when((kk > 0) & (kk < nk - 1))
            def _():
                acc[...] += prod()

            @pl.when(kk == nk - 1)
            def _():
                o_ref[...] = (acc[...] + prod()).astype(out_dtype)

        if nx:
            @pl.when((i == ni - 1) & (j == nj - 1) & (kk == nk - 1))
            def _():
                carry.finish(x_in, x_out, x_sems)

    in_specs = [a_spec, b_spec] + ([o_spec] if has_init else []) + [HBM_SPEC] * nx
    args = (a, b) + ((init,) if has_init else ()) + (tuple(carry.arrays) if nx else ())
    sems = carry.sem_shapes() if nx else []
    outs = pl.pallas_call(
        body, out_shape=[SDS((m if out_rows is None else out_rows, n), out_dtype)] + (carry.out_shapes if nx else []),
        grid=(ni, nj, nk),
        in_specs=in_specs, out_specs=[o_spec] + [HBM_SPEC] * nx,
        scratch_shapes=[pltpu.VMEM((tm, tn) if nk > 1 else (8, LANES), F32)] + sems,
        compiler_params=_params(("arbitrary",) * 3 if nx else ("parallel", "parallel", "arbitrary")), name=name)(*args)
    return (outs[0], outs[1:]) if nx else outs[0]


def _rmsnorm_fwd(x, w, carry=None):
    s, d = x.shape
    tr = 256
    nsteps = s // tr
    nx = len(carry.arrays) if carry is not None else 0

    def body(*refs):
        x_ref, w_ref, x_in = refs[0], refs[1], refs[2:2 + nx]
        o_ref, x_out, x_sems = refs[2 + nx], refs[3 + nx:3 + 2 * nx], refs[3 + 2 * nx:]
        if nx:
            @pl.when(pl.program_id(0) == 0)
            def _():
                carry.start(x_in, x_out, x_sems)

        xv = x_ref[...]
        r = lax.rsqrt(jnp.mean(xv * xv, axis=-1, keepdims=True) + RMS_EPS)
        o_ref[...] = (xv * r * w_ref[...]).astype(BF16)

        if nx:
            @pl.when(pl.program_id(0) == nsteps - 1)
            def _():
                carry.finish(x_in, x_out, x_sems)

    outs = pl.pallas_call(
        body, out_shape=[SDS((s, d), BF16)] + (carry.out_shapes if nx else []), grid=(nsteps,),
        in_specs=[pl.BlockSpec((tr, d), lambda i: (i, 0)), pl.BlockSpec((1, d), lambda i: (0, 0))] + [HBM_SPEC] * nx,
        out_specs=[pl.BlockSpec((tr, d), lambda i: (i, 0))] + [HBM_SPEC] * nx,
        scratch_shapes=carry.sem_shapes() if nx else [],
        compiler_params=_params(("arbitrary",) if nx else ("parallel",)), name="rmsnorm_fwd")(
            x, w, *(carry.arrays if nx else []))
    return (outs[0], outs[1:]) if nx else outs[0]


def _rmsnorm_bwd(x, w, dhn_a, dhn_b, dout):
    s, d = x.shape
    tr = 256

    def body(x_ref, w_ref, dh_ref, dh2_ref, do_ref, gx_ref, gw_ref):
        xv = x_ref[...]
        r = lax.rsqrt(jnp.mean(xv * xv, axis=-1, keepdims=True) + RMS_EPS)
        nrm = xv * r
        dh = dh_ref[...] + dh2_ref[...]
        gy = dh * w_ref[...]
        gx_ref[...] = do_ref[...] + r * (gy - nrm * jnp.mean(gy * nrm, axis=-1, keepdims=True))

        @pl.when(pl.program_id(0) == 0)
        def _():
            gw_ref[...] = jnp.zeros_like(gw_ref)

        gw_ref[...] += jnp.sum(dh * nrm, axis=0, keepdims=True)

    blk = pl.BlockSpec((tr, d), lambda i: (i, 0))
    row = pl.BlockSpec((1, d), lambda i: (0, 0))
    return pl.pallas_call(
        body, out_shape=(SDS((s, d), F32), SDS((1, d), F32)), grid=(s // tr,),
        in_specs=[blk, row, blk, blk, blk], out_specs=(blk, row),
        compiler_params=_params(("arbitrary",)), name="rmsnorm_bwd")(x, w, dhn_a, dhn_b, dout)


DEINT = DILATED_PATTERNS[-1][1]
DEINT_ROWS = DEINT * LANES


class _Pass:
    def __init__(self, tq, patterns, unit, seg_len):
        self.tq, self.patterns, self.unit, self.seg_len = tq, patterns, unit, seg_len
        self.win = max(w for w, _ in patterns) // unit
        self.w = self.win + tq
        assert self.win % tq == 0


def _attn_tables(ps):
    i = jnp.arange(ps.tq, dtype=jnp.int32)[:, None]
    j = jnp.arange(ps.w, dtype=jnp.int32)[None, :]
    delta = (i + ps.win - j) * ps.unit
    n = jnp.zeros((ps.tq, ps.w), F32)
    for window, dil in ps.patterns:
        n = n + ((delta >= 0) & (delta <= window) & (delta % dil == 0)).astype(F32)
    logn = jnp.where(n > 0, jnp.log(jnp.maximum(n, 1.0)), NEG)
    return logn, jnp.maximum(delta, 0).astype(F32)


def _slopes(h):
    s = jnp.asarray([2.0 ** (-8.0 * (i + 1) / h) for i in range(h)], F32)
    return jnp.broadcast_to(s[:, None, None], (h, 1, LANES))


def _masked_logn(ps, logn_ref, start):
    col = lax.broadcasted_iota(jnp.int32, (ps.tq, ps.w), 1)
    return jnp.where(col >= ps.win - lax.rem(start, ps.seg_len), logn_ref[...], NEG)


def _head_cols(hh):
    return slice(hh * ATTN_HEAD_DIM, (hh + 1) * ATTN_HEAD_DIM)


def _head_window(refs, cs):
    return jnp.concatenate([r[:, cs] for r in refs], axis=0)


def _head_scores(q_ref, kw, cs, base, dist_ref, slope_ref, hh):
    return _nt(q_ref[:, cs], kw) * (ATTN_HEAD_DIM ** -0.5) + (base - slope_ref[hh][0:1, 0:1] * dist_ref[...])


def _lane_of(stat, hh):
    lane = lax.broadcasted_iota(jnp.int32, stat.shape, 1)
    return jnp.sum(jnp.where(lane == hh, stat, 0.0), axis=1, keepdims=True)


def _window_specs(ps, d, col, nb):
    nprev = ps.win // ps.tq
    return [pl.BlockSpec((ps.tq, d), lambda i, b=b: (jnp.maximum(jnp.minimum(i, nb - 1) - (nprev - b), 0), col))
            for b in range(nprev + 1)]


def _attn_fwd(cfg, ps, qkv, cols, tables, slopes, name):
    s, h, d = cfg.S, cfg.H, cfg.D
    tq, nw = ps.tq, ps.win // ps.tq + 1
    nb = s // tq
    logn, dist = tables
    qc, kc, vc = [c // d for c in cols]

    def body(*refs):
        q_ref, k_refs, v_refs = refs[0], refs[1:1 + nw], refs[1 + nw:1 + 2 * nw]
        logn_ref, dist_ref, slope_ref, o_ref, lse_ref = refs[1 + 2 * nw:]
        base = _masked_logn(ps, logn_ref, pl.program_id(0) * tq)
        lane = lax.broadcasted_iota(jnp.int32, (tq, LANES), 1)

        lse = jnp.zeros((tq, LANES), F32)
        for hh in range(h):
            cs = _head_cols(hh)
            sc = _head_scores(q_ref, _head_window(k_refs, cs), cs, base, dist_ref, slope_ref, hh)
            m = jnp.max(sc, axis=1, keepdims=True)
            p = jnp.exp(sc - m)
            l = jnp.sum(p, axis=1, keepdims=True)
            o_ref[:, cs] = (_nn(p.astype(BF16), _head_window(v_refs, cs)) / l).astype(BF16)
            lse = jnp.where(lane == hh, m + jnp.log(l), lse)
        lse_ref[...] = lse

    tab = pl.BlockSpec((tq, ps.w), lambda i: (0, 0))
    return pl.pallas_call(
        body, out_shape=(SDS((s, d), BF16), SDS((s, LANES), F32)), grid=(nb,),
        in_specs=[pl.BlockSpec((tq, d), lambda i: (i, qc))] + _window_specs(ps, d, kc, nb) + _window_specs(ps, d, vc, nb)
        + [tab, tab, pl.BlockSpec((h, 1, LANES), lambda i: (0, 0, 0))],
        out_specs=(pl.BlockSpec((tq, d), lambda i: (i, 0)), pl.BlockSpec((tq, LANES), lambda i: (i, 0))),
        compiler_params=_params(("parallel",)), name=name)(*([qkv] * (1 + 2 * nw)), logn, dist, slopes)


def _attn_bwd(cfg, ps, qkv, cols, do, lse, delta, tables, slopes, name):
    s, h, d = cfg.S, cfg.H, cfg.D
    tq, nprev = ps.tq, ps.win // ps.tq
    nw = nprev + 1
    nb = s // tq
    logn, dist = tables
    qc, kc, vc = [c // d for c in cols]
    scale = ATTN_HEAD_DIM ** -0.5

    def body(*refs):
        q_ref, k_refs, v_refs = refs[0], refs[1:1 + nw], refs[1 + nw:1 + 2 * nw]
        do_ref, lse_ref, dl_ref, logn_ref, dist_ref, slope_ref, dq_ref, dk_ref, dv_ref, ck, cv = refs[1 + 2 * nw:]
        i = pl.program_id(0)
        slot = lambda b: lax.rem(i + b, nprev)

        @pl.when(i == 0)
        def _():
            ck[...] = jnp.zeros_like(ck)
            cv[...] = jnp.zeros_like(cv)

        @pl.when(i < nb)
        def _():
            base = _masked_logn(ps, logn_ref, i * tq)
            lse_all, dl_all = lse_ref[...], dl_ref[...]

            for hh in range(h):
                cs = _head_cols(hh)
                kw, vw = _head_window(k_refs, cs), _head_window(v_refs, cs)
                sc = _head_scores(q_ref, kw, cs, base, dist_ref, slope_ref, hh)
                p = jnp.exp(sc - lse_all[:, hh:hh + 1])
                dob = do_ref[:, cs]
                ds = (p * (_nt(dob, vw) - dl_all[:, hh:hh + 1]) * scale).astype(BF16)
                dq_ref[:, cs] = _nn(ds, kw).astype(BF16)
                dkw = _tn(ds, q_ref[:, cs])
                dvw = _tn(p.astype(BF16), dob)
                dk_ref[:, cs] = ck[slot(0), :, cs] + dkw[0:tq]
                dv_ref[:, cs] = cv[slot(0), :, cs] + dvw[0:tq]
                for b in range(1, nprev):
                    ck[slot(b), :, cs] += dkw[b * tq:(b + 1) * tq]
                    cv[slot(b), :, cs] += dvw[b * tq:(b + 1) * tq]
                ck[slot(0), :, cs] = dkw[nprev * tq:]
                cv[slot(0), :, cs] = dvw[nprev * tq:]

        @pl.when(i >= nb)
        def _():
            dk_ref[...] = ck[slot(0)]
            dv_ref[...] = cv[slot(0)]

    here = lambda i: jnp.minimum(i, nb - 1)
    blk = pl.BlockSpec((tq, d), lambda i: (here(i), 0))
    stat = pl.BlockSpec((tq, LANES), lambda i: (here(i), 0))
    late = pl.BlockSpec((tq, d), lambda i: (jnp.maximum(i - nprev, 0), 0))
    tab = pl.BlockSpec((tq, ps.w), lambda i: (0, 0))
    return pl.pallas_call(
        body, out_shape=(SDS((s, d), BF16), SDS((s, d), F32), SDS((s, d), F32)), grid=(nb + nprev,),
        in_specs=[pl.BlockSpec((tq, d), lambda i: (here(i), qc))] + _window_specs(ps, d, kc, nb)
        + _window_specs(ps, d, vc, nb) + [blk, stat, stat, tab, tab, pl.BlockSpec((h, 1, LANES), lambda i: (0, 0, 0))],
        out_specs=(blk, late, late),
        scratch_shapes=[pltpu.VMEM((nprev, tq, d), F32), pltpu.VMEM((nprev, tq, d), F32)],
        compiler_params=_params(("arbitrary",)), name=name)(
            *([qkv] * (1 + 2 * nw)), do, lse, delta, logn, dist, slopes)


def _by_residue(a):
    return a.reshape(DEINT, a.shape[0] // DEINT, a.shape[1])


def _deint_spec(colblock):
    return pl.BlockSpec((DEINT, LANES, LANES), lambda b, j: (0, b, colblock(j)))


def _deint_rows(scr, out_ref, dtype):
    for r in range(DEINT):
        out_ref[r] = scr[pl.ds(r, LANES, stride=DEINT), :].astype(dtype)


def _int_rows(in_ref, scr):
    for r in range(DEINT):
        scr[pl.ds(r, LANES, stride=DEINT), :] = in_ref[r].astype(F32)


WIDE = 4 * LANES


def _wide_spec():
    return pl.BlockSpec((DEINT, LANES, WIDE), lambda b, j: (0, b, j))


def _deinterleave(x, col0, ncols, name):
    s = x.shape[0]
    c0 = col0 // WIDE

    def body(x_ref, o_ref, scr):
        for t in range(WIDE // LANES):
            cs = slice(t * LANES, (t + 1) * LANES)
            scr[t] = x_ref[:, cs].astype(F32)
            for r in range(DEINT):
                o_ref[r, :, cs] = scr.at[t][pl.ds(r, LANES, stride=DEINT), :].astype(x.dtype)

    out = pl.pallas_call(
        body, out_shape=SDS((DEINT, s // DEINT, ncols), x.dtype), grid=(s // DEINT_ROWS, ncols // WIDE),
        in_specs=[pl.BlockSpec((DEINT_ROWS, WIDE), lambda b, j: (b, c0 + j))],
        out_specs=_wide_spec(),
        scratch_shapes=[pltpu.VMEM((WIDE // LANES, DEINT_ROWS, LANES), F32)],
        compiler_params=_params(("parallel", "parallel")), name=name)(x)
    return out.reshape(s, ncols)


def _attn_merge(cfg, proj, o_1, lse_1, o_2, lse_2):
    s, h = cfg.S, cfg.H
    zb = cfg.OZA // WIDE
    rows = DEINT_ROWS
    hps = WIDE // LANES

    def body(o1_ref, l1_ref, o2_ref, l2_ref, z_ref, o_ref, og_ref, lse_ref, so, sl):
        j = pl.program_id(1)

        @pl.when(j == 0)
        def _():
            _int_rows(l2_ref, sl)
            lse_ref[...] = jnp.zeros_like(lse_ref)

        l1_all, l2_all = l1_ref[...], sl[...]
        lane = lax.broadcasted_iota(jnp.int32, (rows, LANES), 1)
        lse = lse_ref[...]
        for t in range(hps):
            hh = j * hps + t
            cs = slice(t * LANES, (t + 1) * LANES)
            for r in range(DEINT):
                so.at[t][pl.ds(r, LANES, stride=DEINT), :] = o2_ref[r, :, cs].astype(F32)
            l1, l2 = _lane_of(l1_all, hh), _lane_of(l2_all, hh)
            mx = jnp.maximum(l1, l2)
            w1, w2 = jnp.exp(l1 - mx), jnp.exp(l2 - mx)
            den = w1 + w2
            o = (w1 * o1_ref[:, cs].astype(F32) + w2 * so[t]) / den
            z = z_ref[:, cs].astype(F32)
            o_ref[:, cs] = o.astype(BF16)
            og_ref[:, cs] = (o * (z * _sigmoid(z))).astype(BF16)
            lse = jnp.where(lane == hh, mx + jnp.log(den), lse)
        lse_ref[...] = lse

    blk = pl.BlockSpec((rows, WIDE), lambda b, j: (b, j))
    stat = pl.BlockSpec((rows, LANES), lambda b, j: (b, 0))
    return pl.pallas_call(
        body, out_shape=(SDS((s, cfg.D), BF16), SDS((s, cfg.D), BF16), SDS((s, LANES), F32)),
        grid=(s // rows, h // hps),
        in_specs=[blk, stat, _wide_spec(), _deint_spec(lambda j: 0), pl.BlockSpec((rows, WIDE), lambda b, j: (b, zb + j))],
        out_specs=(blk, blk, stat),
        scratch_shapes=[pltpu.VMEM((hps, rows, LANES), F32), pltpu.VMEM((rows, LANES), F32)],
        compiler_params=_params(("parallel", "arbitrary")), name="attn_merge")(
            o_1, lse_1, _by_residue(o_2), _by_residue(lse_2), proj)


def _attn_bwd_prep(cfg, proj, o_a, doag, lse, dproj):
    s, h = cfg.S, cfg.H
    zb = cfg.OZA // WIDE
    rows = DEINT_ROWS
    hps = WIDE // LANES

    def body(o_ref, dg_ref, z_ref, lse_ref, dp_in, dz_ref, do_ref, do2_ref, dl_ref, dl2_ref, lse2_ref, scr):
        del dp_in
        j = pl.program_id(1)

        @pl.when(j == 0)
        def _():
            dl_ref[...] = jnp.zeros_like(dl_ref)

        lane = lax.broadcasted_iota(jnp.int32, (rows, LANES), 1)
        dl = dl_ref[...]
        for t in range(hps):
            cs = slice(t * LANES, (t + 1) * LANES)
            z = z_ref[:, cs].astype(F32)
            sg = _sigmoid(z)
            o = o_ref[:, cs].astype(F32)
            dg = dg_ref[:, cs].astype(F32)
            do = dg * (z * sg)
            dz_ref[:, cs] = (dg * o * (sg * (1.0 + z * (1.0 - sg)))).astype(BF16)
            do_ref[:, cs] = do.astype(BF16)
            scr[...] = do
            for r in range(DEINT):
                do2_ref[r, :, cs] = scr[pl.ds(r, LANES, stride=DEINT), :].astype(BF16)
            dl = jnp.where(lane == j * hps + t, jnp.sum(do * o, axis=1, keepdims=True), dl)
        dl_ref[...] = dl

        @pl.when(j == h // hps - 1)
        def _():
            scr[...] = dl
            _deint_rows(scr, dl2_ref, F32)
            scr[...] = lse_ref[...]
            _deint_rows(scr, lse2_ref, F32)

    blk = pl.BlockSpec((rows, WIDE), lambda b, j: (b, j))
    stat = pl.BlockSpec((rows, LANES), lambda b, j: (b, 0))
    stat2 = _deint_spec(lambda j: 0)
    outs = pl.pallas_call(
        body,
        out_shape=(SDS(dproj.shape, BF16), SDS((s, cfg.D), BF16), SDS((DEINT, s // DEINT, cfg.D), BF16),
                   SDS((s, LANES), F32), SDS((DEINT, s // DEINT, LANES), F32), SDS((DEINT, s // DEINT, LANES), F32)),
        grid=(s // rows, h // hps),
        in_specs=[blk, blk, pl.BlockSpec((rows, WIDE), lambda b, j: (b, zb + j)), stat, HBM_SPEC],
        out_specs=(pl.BlockSpec((rows, WIDE), lambda b, j: (b, zb + j)), blk, _wide_spec(), stat, stat2, stat2),
        scratch_shapes=[pltpu.VMEM((rows, LANES), F32)],
        input_output_aliases={4: 0},
        compiler_params=_params(("parallel", "arbitrary")), name="attn_bwd_prep")(o_a, doag, proj, lse, dproj)
    dproj, do, do2, dl, dl2, lse2 = outs
    return dproj, do, do2.reshape(s, cfg.D), dl, dl2.reshape(s, LANES), lse2.reshape(s, LANES)


def _attn_grad_sum(cfg, g_1, g_2, col0, dproj, name):
    s = cfg.S
    c0 = col0 // WIDE
    rows = DEINT_ROWS

    def body(g1_ref, g2_ref, dp_in, o_ref, scr):
        del dp_in
        for t in range(WIDE // LANES):
            cs = slice(t * LANES, (t + 1) * LANES)
            for r in range(DEINT):
                scr.at[t][pl.ds(r, LANES, stride=DEINT), :] = g2_ref[r, :, cs].astype(F32)
            o_ref[:, cs] = (g1_ref[:, cs].astype(F32) + scr[t]).astype(BF16)

    return pl.pallas_call(
        body, out_shape=SDS(dproj.shape, BF16), grid=(s // rows, cfg.D // WIDE),
        in_specs=[pl.BlockSpec((rows, WIDE), lambda b, j: (b, j)), _wide_spec(), HBM_SPEC],
        out_specs=pl.BlockSpec((rows, WIDE), lambda b, j: (b, c0 + j)),
        scratch_shapes=[pltpu.VMEM((WIDE // LANES, rows, LANES), F32)],
        input_output_aliases={2: 0},
        compiler_params=_params(("parallel", "parallel")), name=name)(g_1, _by_residue(g_2), dproj)


CONV_HALO = 16
CONV_TR = 512
CONV_CW = 1024


def _rows_back(a, n):
    return a if n == 0 else pltpu.roll(a, n % a.shape[0], axis=0)


def _conv_fwd(cfg, proj, conv_w, conv_b):
    s, cd = cfg.S, cfg.CD
    tr, cw, hl = CONV_TR, CONV_CW, CONV_HALO
    cb0 = cfg.OXBC // cw

    def body(x_ref, h_ref, w_ref, b_ref, o_ref):
        i = pl.program_id(0)
        halo = jnp.where(i > 0, h_ref[...].astype(F32), 0.0)
        ext = jnp.concatenate([halo, x_ref[...].astype(F32)], axis=0)
        pre = b_ref[...] + jnp.zeros((tr, cw), F32)
        for k in range(CONV_K):
            pre = pre + w_ref[k:k + 1, :] * _rows_back(ext, CONV_K - 1 - k)[hl:]
        o_ref[...] = (pre * _sigmoid(pre)).astype(BF16)

    return pl.pallas_call(
        body, out_shape=SDS((s, cd), BF16), grid=(s // tr, cd // cw),
        in_specs=[pl.BlockSpec((tr, cw), lambda i, j: (i, cb0 + j)),
                  pl.BlockSpec((hl, cw), lambda i, j: (jnp.maximum(i * (tr // hl) - 1, 0), cb0 + j)),
                  pl.BlockSpec((CONV_K, cw), lambda i, j: (0, j)),
                  pl.BlockSpec((1, cw), lambda i, j: (0, j))],
        out_specs=pl.BlockSpec((tr, cw), lambda i, j: (i, j)),
        compiler_params=_params(("parallel", "parallel")), name="conv_fwd")(proj, proj, conv_w, conv_b)


def _conv_bwd(cfg, proj, dact, conv_w, conv_b, dproj):
    s, cd = cfg.S, cfg.CD
    tr, cw, hl = CONV_TR, CONV_CW, CONV_HALO
    cb0 = cfg.OXBC // cw
    nr = s // tr
    last_h = s // hl - 1

    def body(x_ref, hp_ref, hn_ref, d_ref, dn_ref, w_ref, b_ref, dp_in, dx_ref, gw_ref, gb_ref):
        del dp_in
        i = pl.program_id(1)
        ext = jnp.concatenate([jnp.where(i > 0, hp_ref[...].astype(F32), 0.0), x_ref[...].astype(F32),
                               hn_ref[...].astype(F32)], axis=0)
        shifted = [_rows_back(ext, CONV_K - 1 - k)[hl:] for k in range(CONV_K)]
        pre = b_ref[...] + jnp.zeros((tr + hl, cw), F32)
        for k in range(CONV_K):
            pre = pre + w_ref[k:k + 1, :] * shifted[k]
        sg = _sigmoid(pre)
        dact = jnp.concatenate([d_ref[...].astype(F32), jnp.where(i < nr - 1, dn_ref[...].astype(F32), 0.0)], axis=0)
        dpre = dact * (sg * (1.0 + pre * (1.0 - sg)))
        dx = jnp.zeros((tr, cw), F32)
        for k in range(CONV_K):
            dx = dx + w_ref[k:k + 1, :] * _rows_back(dpre, -(CONV_K - 1 - k))[0:tr]
        dx_ref[...] = dx.astype(BF16)

        @pl.when(i == 0)
        def _():
            gw_ref[...] = jnp.zeros_like(gw_ref)
            gb_ref[...] = jnp.zeros_like(gb_ref)

        dcur = dpre[0:tr]
        gb_ref[...] += jnp.sum(dcur, axis=0, keepdims=True)
        for k in range(CONV_K):
            gw_ref[k:k + 1, :] += jnp.sum(dcur * shifted[k][0:tr], axis=0, keepdims=True)

    return pl.pallas_call(
        body, out_shape=(SDS(dproj.shape, BF16), SDS((CONV_K, cd), F32), SDS((1, cd), F32)), grid=(cd // cw, nr),
        in_specs=[pl.BlockSpec((tr, cw), lambda j, i: (i, cb0 + j)),
                  pl.BlockSpec((hl, cw), lambda j, i: (jnp.maximum(i * (tr // hl) - 1, 0), cb0 + j)),
                  pl.BlockSpec((hl, cw), lambda j, i: (jnp.minimum((i + 1) * (tr // hl), last_h), cb0 + j)),
                  pl.BlockSpec((tr, cw), lambda j, i: (i, j)),
                  pl.BlockSpec((hl, cw), lambda j, i: (jnp.minimum((i + 1) * (tr // hl), last_h), j)),
                  pl.BlockSpec((CONV_K, cw), lambda j, i: (0, j)),
                  pl.BlockSpec((1, cw), lambda j, i: (0, j)),
                  pl.BlockSpec(memory_space=pl.ANY)],
        out_specs=(pl.BlockSpec((tr, cw), lambda j, i: (i, cb0 + j)),
                   pl.BlockSpec((CONV_K, cw), lambda j, i: (0, j)),
                   pl.BlockSpec((1, cw), lambda j, i: (0, j))),
        input_output_aliases={7: 0},
        compiler_params=_params(("parallel", "arbitrary")), name="conv_bwd")(
            proj, proj, proj, dact, dact, conv_w, conv_b, dproj)


def _expand(v, e, terms):
    out, rem = None, v
    for _ in range(terms):
        hi = rem.astype(BF16)
        t = _nn(hi, e)
        out = t if out is None else out + t
        rem = rem - hi.astype(F32)
    return out


def _segsum(v, e, terms):
    out, rem = None, v
    for _ in range(terms):
        hi = rem.astype(BF16)
        t = _nt(hi, e)
        out = t if out is None else out + t
        rem = rem - hi.astype(F32)
    return out


def _expand_row(row, e, terms):
    return _expand(jnp.broadcast_to(row, (8, LANES)), e, terms)[0:1]


def _segsum_row(row, e, terms):
    return _segsum(jnp.broadcast_to(row, (8, row.shape[1])), e, terms)[0:1]


def _expansion_matrix(cfg):
    hh = jnp.arange(LANES, dtype=jnp.int32)[:, None]
    cc = jnp.arange(cfg.SI, dtype=jnp.int32)[None, :]
    return (cc // SSM_HEAD_DIM == hh).astype(BF16)


def _tri(lower):
    r = lax.broadcasted_iota(jnp.int32, (CHUNK, CHUNK), 0)
    c = lax.broadcasted_iota(jnp.int32, (CHUNK, CHUNK), 1)
    return (c <= r) if lower else (c >= r)


def _ssd_prep(dtr_ref, db_ref, al_ref, e):
    dtr = dtr_ref[...] + db_ref[...]
    dt = _softplus(dtr)
    a = -jnp.exp(al_ref[...])
    acum = jnp.dot(_tri(True).astype(F32), dt * a, precision=lax.Precision.HIGHEST, preferred_element_type=F32)
    return dtr, dt, a, _expand(dt, e, 2), _expand(acum, e, 3)


def _ssd_fwd(cfg, xact, dt_raw, proj, dt_bias, a_log, d_skip, norm_w, e):
    s, si, cd, gw, bc = cfg.S, cfg.SI, cfg.CD, cfg.GW, cfg.BC
    nc = s // CHUNK
    zb = cfg.OZS // si
    tiles = gw // LANES

    def body(xa_ref, dtr_ref, z_ref, db_ref, al_ref, dsk_ref, nw_ref, e_ref, y_ref, y2_ref, st_ref,
             state, ybuf, x_s, xw_s, ae_s, ea_s, lam_s):
        @pl.when(pl.program_id(0) == 0)
        def _():
            state[...] = jnp.zeros_like(state)

        st_ref[...] = state[...]
        ev = e_ref[...]
        _, _, _, dt_e, a_e = _ssd_prep(dtr_ref, db_ref, al_ref, ev)
        xs = xa_ref[:, 0:si].astype(F32)
        x = xs * dt_e
        lam_e = a_e[CHUNK - 1:CHUNK, :]
        x_s[...] = x.astype(BF16)
        xw_s[...] = (x * jnp.exp(lam_e - a_e)).astype(BF16)
        ae_s[...] = a_e
        ea_s[...] = jnp.exp(a_e)
        ybuf[...] = _expand_row(dsk_ref[...], ev, 3) * xs
        lam_s[...] = jnp.broadcast_to(jnp.exp(lam_e), (8, si))
        tril = _tri(True)
        lane = lax.broadcasted_iota(jnp.int32, (CHUNK, LANES), 1)

        def group(g, carry):
            co = g * gw
            bg = xa_ref[:, pl.ds(si + g * SSM_STATE, SSM_STATE)]
            cg = xa_ref[:, pl.ds(si + bc + g * SSM_STATE, SSM_STATE)]
            cbm = _nt(cg, bg)
            st = state[:, pl.ds(co, gw)]
            yoff = _nn(cg, st.astype(BF16)) * ea_s[:, pl.ds(co, gw)]
            for k in range(tiles):
                tc = co + k * LANES
                at = ae_s[:, pl.ds(tc, LANES)]
                att = at.T
                xt = x_s[:, pl.ds(tc, LANES)]
                acc = yoff[:, k * LANES:(k + 1) * LANES]
                for half in range(2):
                    lo = half * SSM_HEAD_DIM
                    seg = at[:, lo:lo + 1] - att[lo:lo + 1, :]
                    dec = jnp.exp(jnp.where(tril, seg, NEG))
                    xh = jnp.where((lane >= lo) & (lane < lo + SSM_HEAD_DIM), xt, jnp.zeros_like(xt))
                    acc = acc + _nn((cbm * dec).astype(BF16), xh)
                ybuf[:, pl.ds(tc, LANES)] += acc
            state[:, pl.ds(co, gw)] = st * lam_s[0:1, pl.ds(co, gw)] + _tn(bg, xw_s[:, pl.ds(co, gw)])
            return carry

        for g in range(SSM_GROUPS):
            group(g, 0)
        y = ybuf[...]
        y_ref[...] = y.astype(BF16)
        z = z_ref[...].astype(F32)
        u = y * (z * _sigmoid(z))
        r = lax.rsqrt(jnp.mean(u * u, axis=-1, keepdims=True) + RMS_EPS)
        y2_ref[...] = (u * r * nw_ref[...]).astype(BF16)

    row = lambda n: pl.BlockSpec((1, n), lambda c: (0, 0))
    return pl.pallas_call(
        body,
        out_shape=(SDS((s, si), BF16), SDS((s, si), BF16), SDS((nc, SSM_STATE, si), F32)),
        grid=(nc,),
        in_specs=[pl.BlockSpec((CHUNK, cd), lambda c: (c, 0)),
                  pl.BlockSpec((CHUNK, LANES), lambda c: (c, 0)),
                  pl.BlockSpec((CHUNK, si), lambda c: (c, zb)),
                  row(LANES), row(LANES), row(LANES), row(si),
                  pl.BlockSpec((LANES, si), lambda c: (0, 0))],
        out_specs=(pl.BlockSpec((CHUNK, si), lambda c: (c, 0)),
                   pl.BlockSpec((CHUNK, si), lambda c: (c, 0)),
                   pl.BlockSpec((None, SSM_STATE, si), lambda c: (c, 0, 0))),
        scratch_shapes=[pltpu.VMEM((SSM_STATE, si), F32), pltpu.VMEM((CHUNK, si), F32),
                        pltpu.VMEM((CHUNK, si), BF16), pltpu.VMEM((CHUNK, si), BF16),
                        pltpu.VMEM((CHUNK, si), F32), pltpu.VMEM((CHUNK, si), F32),
                        pltpu.VMEM((8, si), F32)],
        compiler_params=_params(("arbitrary",)), name="ssd_fwd")(
            xact, dt_raw, proj, dt_bias, a_log, d_skip, norm_w, e)


def _ssd_bwd(cfg, xact, dt_raw, proj, y, dy2, states, dt_bias, a_log, d_skip, norm_w, e, dproj):
    s, si, cd, gw, bc, hpg = cfg.S, cfg.SI, cfg.CD, cfg.GW, cfg.BC, cfg.HPG
    nc = s // CHUNK
    zb = cfg.OZS // si
    tiles = gw // LANES

    def body(xa_ref, dtr_ref, z_ref, y_ref, d2_ref, st_ref, db_ref, al_ref, dsk_ref, nw_ref, e_ref, dp_in,
             dz_ref, dxa_ref, ddt_ref, gnw_ref, gdb_ref, gal_ref, gds_ref,
             dh, dhn, xs_s, x_s, w_s, ae_s, ea_s, g_s, dx_s, dae_s, r_s, lam_s, dle_s):
        del dp_in

        @pl.when(pl.program_id(0) == 0)
        def _():
            dh[...] = jnp.zeros_like(dh)
            gnw_ref[...] = jnp.zeros_like(gnw_ref)
            gdb_ref[...] = jnp.zeros_like(gdb_ref)
            gal_ref[...] = jnp.zeros_like(gal_ref)
            gds_ref[...] = jnp.zeros_like(gds_ref)

        ev = e_ref[...]
        yv = y_ref[...].astype(F32)
        z = z_ref[...].astype(F32)
        sg = _sigmoid(z)
        sz = z * sg
        u = yv * sz
        r = lax.rsqrt(jnp.mean(u * u, axis=-1, keepdims=True) + RMS_EPS)
        nrm = u * r
        d2 = d2_ref[...].astype(F32)
        gnw_ref[...] += jnp.sum(d2 * nrm, axis=0, keepdims=True)
        gn = d2 * nw_ref[...]
        du = r * (gn - nrm * jnp.mean(gn * nrm, axis=-1, keepdims=True))
        gv = du * sz
        dz_ref[...] = (du * yv * (sg * (1.0 + z * (1.0 - sg)))).astype(BF16)
        g_s[...] = gv

        dtr, dt, a, dt_e, a_e = _ssd_prep(dtr_ref, db_ref, al_ref, ev)
        xs = xa_ref[:, 0:si].astype(F32)
        x = xs * dt_e
        lam_e = a_e[CHUNK - 1:CHUNK, :]
        xs_s[...] = xs
        x_s[...] = x
        w_s[...] = jnp.exp(lam_e - a_e)
        ae_s[...] = a_e
        ea_s[...] = jnp.exp(a_e)
        lam_s[...] = jnp.broadcast_to(jnp.exp(lam_e), (8, si))
        gds_ref[...] += _segsum_row(jnp.sum(gv * xs, axis=0, keepdims=True), ev, 2)
        r_s[...] = jnp.zeros_like(r_s)
        tril = _tri(True)
        lane = lax.broadcasted_iota(jnp.int32, (CHUNK, LANES), 1)
        sub = lax.broadcasted_iota(jnp.int32, (CHUNK, LANES), 0)

        def group(g, carry):
            co = g * gw
            bo = si + g * SSM_STATE
            cof = si + bc + g * SSM_STATE
            cols = pl.ds(co, gw)
            bg = xa_ref[:, pl.ds(bo, SSM_STATE)]
            cg = xa_ref[:, pl.ds(cof, SSM_STATE)]
            cbm = _nt(cg, bg)
            st = st_ref[:, cols]
            stb = st.astype(BF16)
            dho = dh[:, cols]
            dhob = dho.astype(BF16)
            ea = ea_s[:, cols]
            gg = g_s[:, cols]
            xg = x_s[:, cols]
            wg = w_s[:, cols]
            explam = lam_s[0:1, cols]
            yoff = _nn(cg, stb) * ea
            ga = (gg * ea).astype(BF16)
            dc = _nt(ga, stb)
            dhn[:, cols] = dho * explam + _tn(cg, ga)
            bdh = _nn(bg, dhob)
            db = _nt((xg * wg).astype(BF16), dhob)
            t = xg * bdh * wg
            dle_s[0:1, cols] = jnp.sum(t, axis=0, keepdims=True) + explam * jnp.sum(dho * st, axis=0, keepdims=True)
            dae_base = gg * yoff - t
            dxw = wg * bdh
            dcb = jnp.zeros((CHUNK, CHUNK), F32)
            for k in range(tiles):
                tc = co + k * LANES
                ksl = slice(k * LANES, (k + 1) * LANES)
                at = ae_s[:, pl.ds(tc, LANES)]
                att = at.T
                xt = xg[:, ksl].astype(BF16)
                gt = gg[:, ksl].astype(BF16)
                dxt = dxw[:, ksl]
                place = jnp.zeros((CHUNK, LANES), F32)
                for half in range(2):
                    lo = half * SSM_HEAD_DIM
                    seg = at[:, lo:lo + 1] - att[lo:lo + 1, :]
                    dec = jnp.exp(jnp.where(tril, seg, NEG))
                    mh = cbm * dec
                    gh = jnp.where((lane >= lo) & (lane < lo + SSM_HEAD_DIM), gt, jnp.zeros_like(gt))
                    dm = _nt(gh, xt)
                    dxt = dxt + _tn(mh.astype(BF16), gh)
                    dcb = dcb + dm * dec
                    dseg = dm * mh
                    place = place + jnp.where(lane == lo, jnp.sum(dseg, axis=1, keepdims=True), 0.0)
                    hidx = g * hpg + 2 * k + half
                    r_s[...] += jnp.where(sub == hidx, jnp.sum(dseg, axis=0, keepdims=True), 0.0)
                dx_s[:, pl.ds(tc, LANES)] = dxt
                dae_s[:, pl.ds(tc, LANES)] = dae_base[:, ksl] + place
            dcbb = dcb.astype(BF16)
            dxa_ref[:, pl.ds(bo, SSM_STATE)] = (db + _tn(dcbb, cg)).astype(BF16)
            dxa_ref[:, pl.ds(cof, SSM_STATE)] = (dc + _nn(dcbb, bg)).astype(BF16)
            return carry

        for g in range(SSM_GROUPS):
            group(g, 0)
        dlam = _segsum_row(dle_s[0:1, :], ev, 2)
        da_ = _segsum(dae_s[...], ev, 2) - r_s[...].T
        da_ = da_ + jnp.where(sub == CHUNK - 1, dlam, 0.0)
        dda = jnp.dot(_tri(False).astype(F32), da_, precision=lax.Precision.HIGHEST, preferred_element_type=F32)
        dxv = dx_s[...]
        xs = xs_s[...]
        ddt = dda * a + _segsum(dxv * xs, ev, 2)
        gal_ref[...] += jnp.sum(dda * dt, axis=0, keepdims=True) * a
        ddtr = ddt * _sigmoid(dtr)
        gdb_ref[...] += jnp.sum(ddtr, axis=0, keepdims=True)
        ddt_ref[...] = ddtr
        dxa_ref[:, 0:si] = (dxv * dt_e + g_s[...] * _expand_row(dsk_ref[...], ev, 3)).astype(BF16)
        dh[...] = dhn[...]

    rev = lambda c: nc - 1 - c
    row = lambda n: pl.BlockSpec((1, n), lambda c: (0, 0))
    big = lambda: pltpu.VMEM((CHUNK, si), F32)
    return pl.pallas_call(
        body,
        out_shape=(SDS(dproj.shape, BF16), SDS((s, cd), BF16), SDS((s, LANES), F32),
                   SDS((1, si), F32), SDS((1, LANES), F32), SDS((1, LANES), F32), SDS((1, LANES), F32)),
        grid=(nc,),
        in_specs=[pl.BlockSpec((CHUNK, cd), lambda c: (rev(c), 0)),
                  pl.BlockSpec((CHUNK, LANES), lambda c: (rev(c), 0)),
                  pl.BlockSpec((CHUNK, si), lambda c: (rev(c), zb)),
                  pl.BlockSpec((CHUNK, si), lambda c: (rev(c), 0)),
                  pl.BlockSpec((CHUNK, si), lambda c: (rev(c), 0)),
                  pl.BlockSpec((None, SSM_STATE, si), lambda c: (rev(c), 0, 0)),
                  row(LANES), row(LANES), row(LANES), row(si),
                  pl.BlockSpec((LANES, si), lambda c: (0, 0)),
                  pl.BlockSpec(memory_space=pl.ANY)],
        out_specs=(pl.BlockSpec((CHUNK, si), lambda c: (rev(c), zb)),
                   pl.BlockSpec((CHUNK, cd), lambda c: (rev(c), 0)),
                   pl.BlockSpec((CHUNK, LANES), lambda c: (rev(c), 0)),
                   row(si), row(LANES), row(LANES), row(LANES)),
        scratch_shapes=[pltpu.VMEM((SSM_STATE, si), F32), pltpu.VMEM((SSM_STATE, si), F32),
                        big(), big(), big(), big(), big(), big(), big(), big(),
                        pltpu.VMEM((CHUNK, LANES), F32), pltpu.VMEM((8, si), F32), pltpu.VMEM((8, si), F32)],
        input_output_aliases={11: 0},
        compiler_params=_params(("arbitrary",)), name="ssd_bwd")(
            xact, dt_raw, proj, y, dy2, states, dt_bias, a_log, d_skip, norm_w, e, dproj)


MERGE_TR = 512
MERGE_CW = 2048


def _merge_fwd(cfg, proj, a_br, s_br):
    s, d = cfg.S, cfg.D
    tr, cw = MERGE_TR, min(MERGE_CW, d)
    ga0, gs0 = cfg.OGA // cw, cfg.OGS // cw

    def body(ga_ref, gs_ref, a_ref, s_ref, o_ref):
        o_ref[...] = (_sigmoid(ga_ref[...].astype(F32)) * a_ref[...].astype(F32)
                      + _sigmoid(gs_ref[...].astype(F32)) * s_ref[...].astype(F32)).astype(BF16)

    blk = pl.BlockSpec((tr, cw), lambda i, j: (i, j))
    return pl.pallas_call(
        body, out_shape=SDS((s, d), BF16), grid=(s // tr, d // cw),
        in_specs=[pl.BlockSpec((tr, cw), lambda i, j: (i, ga0 + j)),
                  pl.BlockSpec((tr, cw), lambda i, j: (i, gs0 + j)), blk, blk],
        out_specs=blk, compiler_params=_params(("parallel", "parallel")), name="merge_fwd")(proj, proj, a_br, s_br)


def _merge_bwd(cfg, proj, branch, dmerged, gate_off, dproj, name):
    s, d = cfg.S, cfg.D
    tr, cw = MERGE_TR, min(MERGE_CW, d)
    g0 = gate_off // cw
    fresh = dproj is None

    def body(*refs):
        g_ref, b_ref, dm_ref = refs[:3]
        dg_ref, db_ref = refs[-2:]
        dm = dm_ref[...].astype(F32)
        sg = _sigmoid(g_ref[...].astype(F32))
        db_ref[...] = (dm * sg).astype(BF16)
        dg_ref[...] = (dm * b_ref[...].astype(F32) * sg * (1.0 - sg)).astype(BF16)

    blk = pl.BlockSpec((tr, cw), lambda i, j: (i, j))
    gate = pl.BlockSpec((tr, cw), lambda i, j: (i, g0 + j))
    return pl.pallas_call(
        body, out_shape=(SDS((s, cfg.NM), BF16), SDS((s, d), BF16)), grid=(s // tr, d // cw),
        in_specs=[gate, blk, blk] + ([] if fresh else [HBM_SPEC]),
        out_specs=(gate, blk),
        input_output_aliases={} if fresh else {3: 0},
        compiler_params=_params(("parallel", "parallel")), name=name)(
            *((proj, branch, dmerged) + (() if fresh else (dproj,))))


def _outproj_loss(merged, w_out, x, target, fnw):
    s, d = x.shape
    tr = 256

    def body(m_ref, w_ref, x_ref, t_ref, fw_ref, dof_ref, dob_ref, loss_ref, g_ref):
        out = x_ref[...] + _nn(m_ref[...], w_ref[...])
        r = lax.rsqrt(jnp.mean(out * out, axis=-1, keepdims=True) + RMS_EPS)
        nrm = out * r
        fw = fw_ref[...]
        err = nrm * fw - t_ref[...]
        dy = err * (1.0 / d)
        gy = dy * fw
        dout = r * (gy - nrm * jnp.mean(gy * nrm, axis=-1, keepdims=True))
        dof_ref[...] = dout
        dob_ref[...] = dout.astype(BF16)

        @pl.when(pl.program_id(0) == 0)
        def _():
            loss_ref[...] = jnp.zeros_like(loss_ref)
            g_ref[...] = jnp.zeros_like(g_ref)

        loss_ref[...] += jnp.sum(jnp.sum(err * err, axis=1, keepdims=True), axis=0, keepdims=True) * (0.5 / d)
        g_ref[...] += jnp.sum(dy * nrm, axis=0, keepdims=True)

    blk = pl.BlockSpec((tr, d), lambda i: (i, 0))
    return pl.pallas_call(
        body, out_shape=(SDS((s, d), F32), SDS((s, d), BF16), SDS((1, LANES), F32), SDS((1, d), F32)), grid=(s // tr,),
        in_specs=[blk, pl.BlockSpec((d, d), lambda i: (0, 0)), blk, blk, pl.BlockSpec((1, d), lambda i: (0, 0))],
        out_specs=(blk, blk, pl.BlockSpec((1, LANES), lambda i: (0, 0)), pl.BlockSpec((1, d), lambda i: (0, 0))),
        compiler_params=_params(("arbitrary",)), name="outproj_loss")(merged, w_out, x, target, fnw)


ELEMWISE_BLOCK_BYTES = 1 << 20


def _row_block(rows, cols, itemsize=4):
    best = None
    for tr in range(16, rows + 1, 16):
        if rows % tr == 0 and tr * cols * itemsize <= ELEMWISE_BLOCK_BYTES:
            best = tr
    return best if best is not None else rows


def _adamw(w, g, m, v, name):
    rows, cols = w.shape
    tr = _row_block(rows, cols)
    if rows // tr > 64 and cols % LANES == 0:
        blk, grid = pl.BlockSpec((rows, LANES), lambda i: (0, i)), (cols // LANES,)
    else:
        blk, grid = pl.BlockSpec((tr, cols), lambda i: (i, 0)), (rows // tr,)
    out = SDS((rows, cols), F32)
    return pl.pallas_call(
        _adamw_body(), out_shape=(out, out, out), grid=grid, in_specs=[blk] * 4, out_specs=(blk,) * 3,
        compiler_params=_params(("parallel",)), name=name)(w, g, m, v)


def _adamw_body():
    def body(w_ref, g_ref, m_ref, v_ref, d_ref, nm_ref, nv_ref):
        gv = g_ref[...]
        nm = ADAM_B1 * m_ref[...] + (1.0 - ADAM_B1) * gv
        nv = ADAM_B2 * v_ref[...] + (1.0 - ADAM_B2) * jnp.square(gv)
        m_hat = nm / (1.0 - ADAM_B1 ** ADAM_STEP)
        v_hat = nv / (1.0 - ADAM_B2 ** ADAM_STEP)
        d_ref[...] = -ADAM_LR * (m_hat / (jnp.sqrt(v_hat) + ADAM_EPS) + ADAM_WD * w_ref[...])
        nm_ref[...] = nm
        nv_ref[...] = nv

    return body


HBM_SPEC = pl.BlockSpec(memory_space=pl.ANY)


def _position():
    return lax.axis_index("x"), lax.axis_index("y"), lax.axis_index("c")


class _Carry:
    def __init__(self, arrays, out_shapes, sems, start, finish):
        self.arrays, self.out_shapes, self.sems, self.start, self.finish = list(arrays), out_shapes, sems, start, finish

    def sem_shapes(self):
        return [pltpu.SemaphoreType.DMA((k,)) for k in self.sems]


def _gather_carry(shards, by_cols=()):
    n = len(shards)

    def copies(ins, outs, sems):
        send_sems, recv_sems, fsend_sems, frecv_sems = sems
        x, y, c = _position()
        me = 2 * x + y
        peers = [(1 - x, y), (x, 1 - y), (1 - x, 1 - y)]

        def half_of(t, chip, half):
            if t in by_cols:
                c2 = ins[t].shape[1] // 2
                return outs[t].at[chip, :, pl.ds(half * c2, c2)]
            return outs[t].at[chip, half]

        def over_ici(t, p, chip):
            px, py = peers[p]
            if t in by_cols:
                c2 = ins[t].shape[1] // 2
                src = ins[t].at[:, pl.ds(c * c2, c2)]
            else:
                r2 = ins[t].shape[0] // 2
                src = ins[t].at[pl.ds(c * r2, r2), :]
            return pltpu.make_async_remote_copy(
                src_ref=src, dst_ref=half_of(t, chip, c), send_sem=send_sems.at[3 * t + p],
                recv_sem=recv_sems.at[3 * t + p], device_id=(px, py, c), device_id_type=MESH)

        def to_sibling(t, p, half):
            px, py = peers[p]
            slab = half_of(t, 2 * px + py, half)
            return pltpu.make_async_remote_copy(
                src_ref=slab, dst_ref=slab, send_sem=fsend_sems.at[3 * t + p], recv_sem=frecv_sems.at[3 * t + p],
                device_id=(x, y, 1 - c), device_id_type=MESH)

        pairs = [(t, p) for t in range(n) for p in range(3)]
        sends = [over_ici(t, p, me) for t, p in pairs]
        lands = [over_ici(t, p, 2 * peers[p][0] + peers[p][1]) for t, p in pairs]
        passed = [to_sibling(t, p, c) for t, p in pairs]
        from_sibling = [to_sibling(t, p, 1 - c) for t, p in pairs]
        return sends, lands, passed, from_sibling

    def start(ins, outs, sems):
        for cp in copies(ins, outs, sems)[0]:
            cp.start()

    def finish(ins, outs, sems):
        sends, lands, passed, from_sibling = copies(ins, outs, sems)
        for land, fwd in zip(lands, passed):
            land.wait_recv()
            fwd.start()
        for cp in from_sibling:
            cp.wait_recv()
        for cp in sends + passed:
            cp.wait_send()

    shapes = [SDS((N_CHIPS,) + a.shape if t in by_cols else (N_CHIPS, 2, a.shape[0] // 2, a.shape[1]), a.dtype)
              for t, a in enumerate(shards)]
    return _Carry(shards, shapes, [3 * n] * 4, start, finish)


def _scatter_carry(parts):
    def start(ins, outs, sems):
        for cp in _scatter_copies(ins, outs, *sems)[0]:
            cp.start()

    def finish(ins, outs, sems):
        sends, lands = _scatter_copies(ins, outs, *sems)
        for cp in lands:
            cp.wait_recv()
        for cp in sends:
            cp.wait_send()

    return _Carry(parts, [SDS(a.shape, a.dtype) for a in parts], [3 * len(parts)] * 2, start, finish)


def _with_own(gathered, own, chip):
    full = gathered.reshape((N_CHIPS,) + own.shape)
    return lax.dynamic_update_index_in_dim(full, own, chip, 0)


def _exchange_halves(grads):
    n = len(grads)
    slabs = [list(g) if isinstance(g, (list, tuple)) else [g] for g in grads]
    flat = [a for s in slabs for a in s]
    ncp = len(flat)

    def body(*refs):
        ins, outs = refs[:ncp], refs[ncp:ncp + n]
        send_sems, recv_sems = refs[ncp + n:]
        x, y, c = _position()
        cps, k = [], 0
        for t in range(n):
            for j in range(len(slabs[t])):
                if len(slabs[t]) == 1:
                    r2 = ins[k].shape[1] // 2
                    src, dst = ins[k].at[:, pl.ds((1 - c) * r2, r2), :], outs[t]
                else:
                    r2 = ins[k].shape[0] // 2
                    src, dst = ins[k].at[pl.ds((1 - c) * r2, r2), :], outs[t].at[j]
                cps.append(pltpu.make_async_remote_copy(
                    src_ref=src, dst_ref=dst, send_sem=send_sems.at[k], recv_sem=recv_sems.at[k],
                    device_id=(x, y, 1 - c), device_id_type=MESH))
                k += 1
        for cp in cps:
            cp.start()
        for cp in cps:
            cp.wait()

    def landing(s):
        a = s[0]
        return SDS((N_CHIPS, a.shape[-2] // 2, a.shape[-1]), a.dtype)

    return pl.pallas_call(
        body, out_shape=[landing(s) for s in slabs],
        in_specs=[HBM_SPEC] * ncp, out_specs=[HBM_SPEC] * n,
        scratch_shapes=[pltpu.SemaphoreType.DMA((ncp,)), pltpu.SemaphoreType.DMA((ncp,))],
        compiler_params=pltpu.CompilerParams(has_side_effects=True), name="reduce_sibling")(*flat)


def _scatter_copies(ins, outs, send_sems, recv_sems):
    x, y, c = _position()
    me = 2 * x + y
    peers = [(1 - x, y), (x, 1 - y), (1 - x, 1 - y)]

    def remote(t, p, src_slab, dst_slab):
        px, py = peers[p]
        return pltpu.make_async_remote_copy(
            src_ref=ins[t].at[src_slab], dst_ref=outs[t].at[dst_slab], send_sem=send_sems.at[3 * t + p],
            recv_sem=recv_sems.at[3 * t + p], device_id=(px, py, c), device_id_type=MESH)

    n = len(ins)
    sends = [remote(t, p, 2 * peers[p][0] + peers[p][1], me) for t in range(n) for p in range(3)]
    lands = [remote(t, p, me, 2 * peers[p][0] + peers[p][1]) for t in range(n) for p in range(3)]
    return sends, lands


def _share_halves(halves):
    n = len(halves)

    def body(*refs):
        ins, outs = refs[:n], refs[n:2 * n]
        send_sems, recv_sems = refs[2 * n:]
        x, y, c = _position()

        def copy(t, slab):
            return pltpu.make_async_remote_copy(
                src_ref=ins[t].at[slab], dst_ref=outs[t].at[slab], send_sem=send_sems.at[t], recv_sem=recv_sems.at[t],
                device_id=(x, y, 1 - c), device_id_type=MESH)

        for t in range(n):
            copy(t, c).start()
        for t in range(n):
            copy(t, 1 - c).wait_recv()
        for t in range(n):
            copy(t, c).wait_send()

    return pl.pallas_call(
        body, out_shape=[SDS(a.shape, a.dtype) for a in halves],
        in_specs=[HBM_SPEC] * n, out_specs=[HBM_SPEC] * n,
        scratch_shapes=[pltpu.SemaphoreType.DMA((n,)), pltpu.SemaphoreType.DMA((n,))],
        input_output_aliases={t: t for t in range(n)},
        compiler_params=pltpu.CompilerParams(has_side_effects=True), name="share_sibling")(*halves)


def _add_sibling(grad, recv, core):
    nch, r2, cols = recv.shape
    tr = _row_block(r2, cols)
    nb = r2 // tr

    def body(c_ref, g_ref, r_ref, o_ref):
        del c_ref
        o_ref[...] = (g_ref[...].astype(F32) + r_ref[...].astype(F32)).astype(BF16)

    return pl.pallas_call(
        body, out_shape=SDS(recv.shape, BF16),
        grid_spec=pltpu.PrefetchScalarGridSpec(
            num_scalar_prefetch=1, grid=(nch, nb),
            in_specs=[pl.BlockSpec((None, tr, cols), lambda j, i, c_ref: (j, c_ref[0] * nb + i, 0)),
                      pl.BlockSpec((None, tr, cols), lambda j, i, c_ref: (j, i, 0))],
            out_specs=pl.BlockSpec((None, tr, cols), lambda j, i, c_ref: (j, i, 0))),
        compiler_params=_params(("parallel", "parallel")), name="add_sibling")(core, grad, recv)


def _add_chips(own, recv, chip_core):
    nch, r2, cols = recv.shape
    tr = _row_block(r2, cols)

    nsc = 2 + nch

    def body(*refs):
        me = refs[0][0]
        own_ref, p_refs, o_ref = refs[nsc], refs[nsc + 1:nsc + 1 + nch], refs[nsc + 1 + nch]
        acc = None
        for j in range(nch):
            term = jnp.where(me == j, own_ref[...], p_refs[j][...]).astype(F32)
            acc = term if acc is None else acc + term
        o_ref[...] = acc

    def slab(j):
        return pl.BlockSpec((None, tr, cols), lambda i, *sc: (sc[2 + j][0], i, 0))

    return pl.pallas_call(
        body, out_shape=SDS((2, r2, cols), F32),
        grid_spec=pltpu.PrefetchScalarGridSpec(
            num_scalar_prefetch=nsc, grid=(r2 // tr,),
            in_specs=[pl.BlockSpec((None, tr, cols), lambda i, *sc: (sc[0][0], i, 0))] + [slab(j) for j in range(nch)],
            out_specs=pl.BlockSpec((None, tr, cols), lambda i, *sc: (sc[1][0], i, 0))),
        compiler_params=_params(("parallel",)), name="add_chips")(*chip_core, own, *([recv] * nch))


def _col_halves_carry(grad):
    nch, r, cols = grad.shape
    c2 = cols // 2

    def copy(ins, outs, sems):
        x, y, c = _position()
        return pltpu.make_async_remote_copy(
            src_ref=ins[0].at[:, :, pl.ds((1 - c) * c2, c2)], dst_ref=outs[0], send_sem=sems[0].at[0],
            recv_sem=sems[1].at[0], device_id=(x, y, 1 - c), device_id_type=MESH)

    return _Carry([grad], [SDS((nch, r, c2), grad.dtype)], [1, 1],
                  lambda ins, outs, sems: copy(ins, outs, sems).start(),
                  lambda ins, outs, sems: copy(ins, outs, sems).wait())


def _add_sibling_cols(grad, recv, core):
    nch, r, c2 = recv.shape
    nb = c2 // LANES

    def body(c_ref, g_ref, r_ref, o_ref):
        del c_ref
        o_ref[...] = (g_ref[...].astype(F32) + r_ref[...].astype(F32)).astype(BF16)

    blk = pl.BlockSpec((None, r, LANES), lambda j, i, c_ref: (j, 0, i))
    return pl.pallas_call(
        body, out_shape=SDS(recv.shape, BF16),
        grid_spec=pltpu.PrefetchScalarGridSpec(
            num_scalar_prefetch=1, grid=(nch, nb),
            in_specs=[pl.BlockSpec((None, r, LANES), lambda j, i, c_ref: (j, 0, c_ref[0] * nb + i)), blk],
            out_specs=blk),
        compiler_params=_params(("parallel", "parallel")), name="add_sibling_cols")(core, grad, recv)


def _add_chips_cols(own, recv, chip_core):
    nch, r, c2 = recv.shape
    nb = c2 // LANES
    nsc = 2 + nch

    def body(*refs):
        me = refs[0][0]
        own_ref, p_refs, o_ref = refs[nsc], refs[nsc + 1:nsc + 1 + nch], refs[nsc + 1 + nch]
        acc = None
        for j in range(nch):
            term = jnp.where(me == j, own_ref[...], p_refs[j][...]).astype(F32)
            acc = term if acc is None else acc + term
        o_ref[...] = acc

    def slab(j):
        return pl.BlockSpec((None, r, LANES), lambda i, *sc: (sc[2 + j][0], 0, i))

    return pl.pallas_call(
        body, out_shape=SDS((r, 2 * c2), F32),
        grid_spec=pltpu.PrefetchScalarGridSpec(
            num_scalar_prefetch=nsc, grid=(nb,),
            in_specs=[pl.BlockSpec((None, r, LANES), lambda i, *sc: (sc[0][0], 0, i))] + [slab(j) for j in range(nch)],
            out_specs=pl.BlockSpec((r, LANES), lambda i, *sc: (0, sc[1][0] * nb + i))),
        compiler_params=_params(("parallel",)), name="add_chips_cols")(*chip_core, own, *([recv] * nch))


def _share_col_halves(full):
    r, cols = full.shape
    c2 = cols // 2

    def body(in_ref, out_ref, send_sem, recv_sem):
        x, y, c = _position()

        def copy(half):
            return pltpu.make_async_remote_copy(
                src_ref=in_ref.at[:, pl.ds(half * c2, c2)], dst_ref=out_ref.at[:, pl.ds(half * c2, c2)],
                send_sem=send_sem.at[0], recv_sem=recv_sem.at[0], device_id=(x, y, 1 - c), device_id_type=MESH)

        copy(c).start()
        copy(1 - c).wait_recv()
        copy(c).wait_send()

    return pl.pallas_call(
        body, out_shape=SDS(full.shape, full.dtype), in_specs=[HBM_SPEC], out_specs=HBM_SPEC,
        scratch_shapes=[pltpu.SemaphoreType.DMA((1,)), pltpu.SemaphoreType.DMA((1,))],
        input_output_aliases={0: 0},
        compiler_params=pltpu.CompilerParams(has_side_effects=True), name="share_sibling_cols")(full)


def _allreduce_small(pack):
    rows = pack.shape[0]

    def body(p_ref, o_ref, buf, send_sems, recv_sems):
        x, y, c = _position()
        me = 4 * x + 2 * y + c
        buf[me] = p_ref[...]

        def copy(dst_dev, slot):
            return pltpu.make_async_remote_copy(
                src_ref=p_ref, dst_ref=buf.at[slot], send_sem=send_sems.at[dst_dev], recv_sem=recv_sems.at[slot],
                device_id=(dst_dev // 4, (dst_dev // 2) % 2, dst_dev % 2), device_id_type=MESH)

        for dev in range(N_DEV):
            @pl.when(dev != me)
            def _():
                copy(dev, me).start()
        for dev in range(N_DEV):
            @pl.when(dev != me)
            def _():
                copy(dev, dev).wait_recv()
        for dev in range(N_DEV):
            @pl.when(dev != me)
            def _():
                copy(dev, me).wait_send()
        acc = buf[0]
        for dev in range(1, N_DEV):
            acc = acc + buf[dev]
        o_ref[...] = acc

    return pl.pallas_call(
        body, out_shape=SDS(pack.shape, F32),
        in_specs=[pl.BlockSpec(memory_space=pltpu.VMEM)], out_specs=pl.BlockSpec(memory_space=pltpu.VMEM),
        scratch_shapes=[pltpu.VMEM((N_DEV, rows, LANES), F32), pltpu.SemaphoreType.DMA((N_DEV,)),
                        pltpu.SemaphoreType.DMA((N_DEV,))],
        compiler_params=pltpu.CompilerParams(has_side_effects=True), name="allreduce_small")(pack)


ATTN_TQ = 256


def _local_step(cfg, x, target, w, to_chips=None, late=None, hn=None):
    d = cfg.D
    if hn is None:
        hn = _rmsnorm_fwd(x, w["norm_w"])
    proj = _mm(hn, w["w_main_t"], "nt", BF16, "proj_main", carry=late[0] if late else None, b_rows=cfg.NM)
    if late:
        proj, arrived = proj
        w = {**w, **late[1](arrived)}
    dt_raw = _mm(hn, w["w_dt_t"], "nt", F32, "proj_dt")
    slopes = _slopes(cfg.H)
    near = _Pass(ATTN_TQ, DILATED_PATTERNS[:-1], 1, cfg.S)
    far = _Pass(LANES, DILATED_PATTERNS[-1:], DEINT, cfg.S // DEINT)
    tab_near, tab_far = _attn_tables(near), _attn_tables(far)
    cols_near, cols_far = (cfg.OQ, cfg.OK, cfg.OV), (0, d, 2 * d)
    qkv_far = _deinterleave(proj, 0, 3 * d, "attn_deinterleave")
    o_1, lse_1 = _attn_fwd(cfg, near, proj, cols_near, tab_near, slopes, "attn_fwd_near")
    o_2, lse_2 = _attn_fwd(cfg, far, qkv_far, cols_far, tab_far, slopes, "attn_fwd_far")
    o_a, oag, lse = _attn_merge(cfg, proj, o_1, lse_1, o_2, lse_2)
    xact = _conv_fwd(cfg, proj, w["conv_w"], w["conv_b"])
    e = _expansion_matrix(cfg)
    y, y2, states = _ssd_fwd(cfg, xact, dt_raw, proj, w["dt_bias"], w["a_log"], w["d_skip"], w["ssm_norm_w"], e)
    a_br = _mm(oag, w["w_attn"], "nn", BF16, "branch_attn")
    s_br = _mm(y2, w["w_ssm"], "nn", BF16, "branch_ssm")
    merged = _merge_fwd(cfg, proj, a_br, s_br)
    dout_f, dout_b, loss_row, g_fnw = _outproj_loss(merged, w["w_out"], x, target, w["final_norm_w"])

    dmerged = _mm(dout_b, w["w_out"], "nt", BF16, "d_merged")
    dproj, da_br = _merge_bwd(cfg, proj, a_br, dmerged, cfg.OGA, None, "merge_bwd_attn")
    dproj, ds_br = _merge_bwd(cfg, proj, s_br, dmerged, cfg.OGS, dproj, "merge_bwd_ssm")
    doag = _mm(da_br, w["w_attn"], "nt", BF16, "d_oag")
    dy2 = _mm(ds_br, w["w_ssm"], "nt", BF16, "d_y2")
    dproj, dxact, ddt, g_snw, g_dtb, g_alog, g_dsk = _ssd_bwd(
        cfg, xact, dt_raw, proj, y, dy2, states, w["dt_bias"], w["a_log"], w["d_skip"], w["ssm_norm_w"], e, dproj)
    dproj, g_cw, g_cb = _conv_bwd(cfg, proj, dxact, w["conv_w"], w["conv_b"], dproj)
    dproj, do, do_far, dl, dl_far, lse_far = _attn_bwd_prep(cfg, proj, o_a, doag, lse, dproj)
    g_near = _attn_bwd(cfg, near, proj, cols_near, do, lse, dl, tab_near, slopes, "attn_bwd_near")
    g_far = _attn_bwd(cfg, far, qkv_far, cols_far, do_far, lse_far, dl_far, tab_far, slopes, "attn_bwd_far")
    for g_1, g_2, col0, nm in zip(g_near, g_far, cols_near, ("attn_dq", "attn_dk", "attn_dv")):
        dproj = _attn_grad_sum(cfg, g_1, g_2, col0, dproj, nm)
    ddt_b = ddt.astype(BF16)
    g_w_main = _mm(dproj, hn, "tn", BF16, "g_w_main", out_rows=cfg.N_IN)
    g_w_dt = _mm(ddt_b, hn, "tn", BF16, "g_w_dt")
    grads = dict(w_main_t=g_w_main, w_dt_t=g_w_dt, conv_w=g_cw, conv_b=g_cb, dt_bias=g_dtb, a_log=g_alog,
                 d_skip=g_dsk, ssm_norm_w=g_snw, final_norm_w=g_fnw)
    riding = to_chips[0](grads) if to_chips is not None else None
    g_w_ssm = _mm(y2, ds_br, "tn", BF16, "g_w_ssm", carry=riding[1] if riding else None)
    if riding:
        g_w_ssm, from_sibling_in = g_w_ssm
    grads.update(w_ssm=g_w_ssm, w_out=_mm(merged, dout_b, "tn", BF16, "g_w_out"),
                 w_attn=_mm(oag, da_br, "tn", BF16, "g_w_attn"))
    sent = to_chips[1](grads, riding[0], from_sibling_in[0]) if to_chips is not None else ()
    dhn = _mm(dproj, w["w_main_t"], "nn", F32, "d_hn", tk=1024, carry=_scatter_carry(sent) if sent else None,
              b_rows=cfg.NM)
    landed = ()
    if sent:
        dhn, landed = dhn
    dhn_dt = _mm(ddt_b, w["w_dt_t"], "nn", F32, "d_hn_dt")
    grad_x, grads["norm_w"] = _rmsnorm_bwd(x, w["norm_w"], dhn, dhn_dt, dout_f)
    return loss_row, grad_x, grads, sent, landed


def _pad_lanes(v):
    return jnp.pad(v, ((0, 0), (0, LANES - v.shape[1])))


def _main_from_rows(cfg, w_in_t):
    lo, hi = cfg.OGA, cfg.OGA + cfg.NH
    dt = jnp.pad(w_in_t[lo:hi], ((0, LANES - cfg.NH), (0, 0)))
    return lax.dynamic_update_slice(w_in_t, w_in_t[hi:], (lo, 0)), dt


def _rows_from_main(cfg, g_main_t, g_dt_t):
    lo, hi = cfg.OGA, cfg.OGA + cfg.NH
    g = lax.dynamic_update_slice(g_main_t, g_main_t[lo:cfg.NM], (hi, 0))
    return lax.dynamic_update_slice(g, g_dt_t[:cfg.NH], (lo, 0))


def _full_weights(cfg, norm_w, w_in_t, conv_w, conv_b, dt_bias, a_log, d_skip, ssm_norm_w, w_attn, w_ssm, w_out, fnw):
    w_main, w_dt = _main_from_rows(cfg, w_in_t)
    return dict(norm_w=norm_w, w_main_t=w_main.astype(BF16), w_dt_t=w_dt.astype(BF16), conv_w=conv_w, conv_b=conv_b,
                dt_bias=_pad_lanes(dt_bias), a_log=_pad_lanes(a_log), d_skip=_pad_lanes(d_skip), ssm_norm_w=ssm_norm_w,
                final_norm_w=fnw, **{k: v.astype(BF16) for k, v in (("w_attn", w_attn), ("w_ssm", w_ssm), ("w_out", w_out))
                                     if v is not None})


def kernel(x, norm_w, w_in, conv_w, conv_b, dt_bias, a_log, d_skip, ssm_norm_w, w_attn_branch, w_ssm_branch, w_out, final_norm_w, loss_target, m_norm_w, m_w_in, m_conv_w, m_conv_b, m_dt_bias, m_a_log, m_d_skip, m_ssm_norm_w, m_w_attn_branch, m_w_ssm_branch, m_w_out, m_final_norm_w, v_norm_w, v_w_in, v_conv_w, v_conv_b, v_dt_bias, v_a_log, v_d_skip, v_ssm_norm_w, v_w_attn_branch, v_w_ssm_branch, v_w_out, v_final_norm_w):
    cfg = _Cfg(x.shape[1], x.shape[2])
    d, si, cd, nh = cfg.D, cfg.SI, cfg.CD, cfg.NH
    chip = 2 * lax.axis_index("x") + lax.axis_index("y")
    core = lax.axis_index("c").astype(jnp.int32).reshape(1)
    chip = chip.astype(jnp.int32)
    chip_core = [chip.reshape(1), core] + [jnp.where(chip == j, (j + 1) % N_CHIPS, j).astype(jnp.int32).reshape(1)
                                           for j in range(N_CHIPS)]

    own = [jnp.transpose(w_in[0]).astype(BF16), conv_w[0].reshape(4 * CONV_K, -1)]
    hn, gathered = _rmsnorm_fwd(x[0], norm_w, carry=_gather_carry(own, by_cols=(0,)))
    a_in, a_cw = [_with_own(g, o, chip) for g, o in zip(gathered, own)]
    conv_w_full = a_cw.reshape(N_CHIPS, CONV_K, cd // N_CHIPS).transpose(1, 0, 2).reshape(CONV_K, cd)
    w = _full_weights(cfg, norm_w, a_in.reshape(cfg.N_IN, d), conv_w_full, conv_b, dt_bias, a_log, d_skip,
                      ssm_norm_w, None, None, None, final_norm_w.reshape(1, d))
    own_late = [w_attn_branch[0].astype(BF16), w_ssm_branch[0].astype(BF16), w_out[0].astype(BF16)]

    def late_weights(arrived):
        a_attn, a_ssm, a_out = [_with_own(g, o, chip) for g, o in zip(arrived, own_late)]
        return dict(w_attn=a_attn.reshape(d, d), w_ssm=a_ssm.reshape(si, d), w_out=a_out.reshape(d, d))

    def w_in_to_sibling(grads):
        g_in_t = _rows_from_main(cfg, grads["w_main_t"], grads["w_dt_t"]).reshape(N_CHIPS, cfg.N_IN // N_CHIPS, d)
        return g_in_t, _col_halves_carry(g_in_t)

    def to_chips(grads, g_in_t, from_sibling_in):
        by_chip = [grads["w_attn"].reshape(N_CHIPS, d // N_CHIPS, d),
                   grads["w_ssm"].reshape(N_CHIPS, si // N_CHIPS, d),
                   grads["w_out"].reshape(N_CHIPS, d // N_CHIPS, d)]
        from_sibling = _exchange_halves(by_chip)
        return ([_add_sibling_cols(g_in_t, from_sibling_in, core)]
                + [_add_sibling(g, r, core) for g, r in zip(by_chip, from_sibling)])

    loss_row, grad_x, grads, chip_sums, from_chips = _local_step(
        cfg, x[0], loss_target[0], w, (w_in_to_sibling, to_chips), (_gather_carry(own_late), late_weights), hn)
    g_in_t = _share_col_halves(_add_chips_cols(chip_sums[0], from_chips[0], chip_core))
    halves = [_add_chips(o, p, chip_core) for o, p in zip(chip_sums[1:], from_chips[1:])]
    g_attn, g_ssm, g_out = [h.reshape(2 * h.shape[1], h.shape[2]) for h in _share_halves(halves)]
    g_in = jnp.transpose(g_in_t)

    small = [loss_row, grads["norm_w"], grads["conv_b"], grads["dt_bias"], grads["a_log"], grads["d_skip"],
             grads["ssm_norm_w"], grads["final_norm_w"], grads["conv_w"].reshape(1, CONV_K * cd)]
    sizes = [a.shape[1] for a in small]
    total = sum(sizes)
    rows = -(-total // (8 * LANES)) * 8
    flat = jnp.pad(jnp.concatenate(small, axis=1), ((0, 0), (0, rows * LANES - total)))
    red = _allreduce_small(flat.reshape(rows, LANES)).reshape(1, rows * LANES)
    offs = [sum(sizes[:i]) for i in range(len(sizes))]
    loss_r, g_nw, g_cb, g_dtb, g_alog, g_dsk, g_snw, g_fnw, g_cw_flat = [
        red[:, o:o + n] for o, n in zip(offs, sizes)]
    loss = loss_r[0, 0]
    g_dtb, g_alog, g_dsk = g_dtb[:, :nh], g_alog[:, :nh], g_dsk[:, :nh]
    cshard = cd // N_CHIPS
    g_cw = lax.dynamic_slice_in_dim(g_cw_flat.reshape(CONV_K, cd), chip * cshard, cshard, axis=1)

    upd = {}
    upd["w_in"] = tuple(jnp.transpose(u) for u in _adamw(
        jnp.transpose(w_in[0]), g_in_t, jnp.transpose(m_w_in[0]), jnp.transpose(v_w_in[0]), "adamw_w_in"))
    for name, wv, gv, mv, vv in [("w_attn", w_attn_branch[0], g_attn, m_w_attn_branch[0], v_w_attn_branch[0]),
                                 ("w_ssm", w_ssm_branch[0], g_ssm, m_w_ssm_branch[0], v_w_ssm_branch[0]),
                                 ("w_out", w_out[0], g_out, m_w_out[0], v_w_out[0])]:
        upd[name] = _adamw(wv, gv, mv, vv, "adamw_" + name)
    names = ["norm_w", "conv_w", "conv_b", "dt_bias", "a_log", "d_skip", "ssm_norm_w", "final_norm_w"]
    ws = [norm_w, conv_w[0].reshape(1, -1), conv_b, dt_bias, a_log, d_skip, ssm_norm_w, final_norm_w.reshape(1, d)]
    gs = [g_nw, g_cw.reshape(1, -1), g_cb, g_dtb, g_alog, g_dsk, g_snw, g_fnw]
    ms = [m_norm_w, m_conv_w[0].reshape(1, -1), m_conv_b, m_dt_bias, m_a_log, m_d_skip, m_ssm_norm_w,
          m_final_norm_w.reshape(1, d)]
    vs = [v_norm_w, v_conv_w[0].reshape(1, -1), v_conv_b, v_dt_bias, v_a_log, v_d_skip, v_ssm_norm_w,
          v_final_norm_w.reshape(1, d)]
    ssz = [a.shape[1] for a in ws]
    stot = sum(ssz)
    srows = -(-stot // (8 * LANES)) * 8

    def pack(parts):
        return jnp.pad(jnp.concatenate(parts, axis=1), ((0, 0), (0, srows * LANES - stot))).reshape(srows, LANES)

    packed = _adamw(pack(ws), pack(gs), pack(ms), pack(vs), "adamw_small")
    soffs = [sum(ssz[:i]) for i in range(len(ssz))]
    for k, nm in enumerate(names):
        upd[nm] = tuple(p.reshape(1, srows * LANES)[:, soffs[k]:soffs[k] + ssz[k]] for p in packed)

    shapes = dict(norm_w=norm_w.shape, w_in=w_in.shape, conv_w=conv_w.shape, conv_b=conv_b.shape, dt_bias=dt_bias.shape,
                  a_log=a_log.shape, d_skip=d_skip.shape, ssm_norm_w=ssm_norm_w.shape, w_attn=w_attn_branch.shape,
                  w_ssm=w_ssm_branch.shape, w_out=w_out.shape, final_norm_w=final_norm_w.shape)
    order = ["norm_w", "w_in", "conv_w", "conv_b", "dt_bias", "a_log", "d_skip", "ssm_norm_w", "w_attn", "w_ssm",
             "w_out", "final_norm_w"]
    gradv = dict(norm_w=g_nw, w_in=g_in, conv_w=g_cw, conv_b=g_cb, dt_bias=g_dtb, a_log=g_alog, d_skip=g_dsk,
                 ssm_norm_w=g_snw, w_attn=g_attn, w_ssm=g_ssm, w_out=g_out, final_norm_w=g_fnw)
    outs = [loss, grad_x[None]]
    outs += [gradv[n].reshape(shapes[n]) for n in order]
    for k in range(3):
        outs += [upd[n][k].reshape(shapes[n]) for n in order]
    return tuple(outs)
```

```python
import jax
import jax.numpy as jnp
from jax import lax
from jax.experimental import pallas as pl
from jax.experimental.pallas import tpu as pltpu

F32 = jnp.float32
BF16 = jnp.bfloat16
SDS = jax.ShapeDtypeStruct

RMS_EPS = 1e-6
LANES = 128
CHUNK = 128
SSM_HEAD_DIM = 64
SSM_GROUPS = 8
SSM_STATE = 128
CONV_K = 4
ATTN_HEAD_DIM = 128
DILATED_PATTERNS = ((128, 1), (512, 4), (2048, 16))
NEG = -1e30
VMEM_LIMIT = 56 * 1024 * 1024
ADAM_LR, ADAM_B1, ADAM_B2, ADAM_EPS, ADAM_WD, ADAM_STEP = 0.001, 0.9, 0.999, 1e-08, 0.01, 10
MESH = pl.DeviceIdType.MESH
N_CHIPS = 4
N_DEV = 8


class _Cfg:
    def __init__(self, s, d):
        self.S, self.D = s, d
        self.H = d // ATTN_HEAD_DIM
        self.SI = 2 * d
        self.NH = self.SI // SSM_HEAD_DIM
        self.HPG = self.NH // SSM_GROUPS
        self.GW = self.HPG * SSM_HEAD_DIM
        self.BC = SSM_GROUPS * SSM_STATE
        self.CD = self.SI + 2 * self.BC
        self.OQ, self.OK, self.OV, self.OZA = 0, d, 2 * d, 3 * d
        self.OZS = 4 * d
        self.OXBC = self.OZS + self.SI
        self.OGA = self.OXBC + self.CD
        self.OGS = self.OGA + d
        self.NM = self.OGS + d
        self.N_IN = self.NM + self.NH
        assert self.GW % LANES == 0 and self.NH <= LANES and s % 512 == 0 and d % 512 == 0


def _params(sem=None):
    return pltpu.CompilerParams(dimension_semantics=sem, vmem_limit_bytes=VMEM_LIMIT)


def _sigmoid(x):
    return 0.5 * jnp.tanh(0.5 * x) + 0.5


def _softplus(x):
    u = jnp.exp(-jnp.abs(x))
    l1p = jnp.where(u < 1e-3, u * (1.0 - u * (0.5 - u * (1.0 / 3.0))), jnp.log(1.0 + u))
    return jnp.maximum(x, 0.0) + l1p


def _nt(a, b):
    return lax.dot_general(a, b, (((1,), (1,)), ((), ())), preferred_element_type=F32)


def _tn(a, b):
    return lax.dot_general(a, b, (((0,), (0,)), ((), ())), preferred_element_type=F32)


def _nn(a, b):
    return jnp.dot(a, b, preferred_element_type=F32)


def _tile(n, target):
    if n <= target:
        return n
    best = None
    for t in range(LANES, target + 1, LANES):
        if n % t == 0:
            best = t
    assert best is not None, (n, target)
    return best


MM_TK = {"nn": 2048, "nt": 2048, "tn": 1024}


def _mm(a, b, dims, out_dtype, name, tm=1024, tn=2048, tk=None, init=None, carry=None, b_rows=None, out_rows=None):
    tk = MM_TK[dims] if tk is None else tk
    if dims == "nn":
        (m, k), (k2, n) = a.shape, b.shape
        k2 = k2 if b_rows is None else b_rows
    elif dims == "nt":
        (m, k), (n, k2) = a.shape, b.shape
        n = n if b_rows is None else b_rows
    else:
        (k, m), (k2, n) = a.shape, b.shape
    assert k == k2
    tm, tn, tk = _tile(m, tm), _tile(n, tn), _tile(k, tk)
    nk = k // tk
    if dims == "tn":
        a_spec = pl.BlockSpec((tk, tm), lambda i, j, kk: (kk, i))
    else:
        a_spec = pl.BlockSpec((tm, tk), lambda i, j, kk: (i, kk))
    if dims == "nt":
        b_spec = pl.BlockSpec((tn, tk), lambda i, j, kk: (j, kk))
    else:
        b_spec = pl.BlockSpec((tk, tn), lambda i, j, kk: (kk, j))
    o_spec = pl.BlockSpec((tm, tn), lambda i, j, kk: (i, j))
    op = {"nn": _nn, "nt": _nt, "tn": _tn}[dims]
    has_init = init is not None
    nx = len(carry.arrays) if carry is not None else 0
    ni, nj = m // tm, n // tn

    def body(*refs):
        a_ref, b_ref = refs[0], refs[1]
        i_ref = refs[2] if has_init else None
        x_in = refs[2 + has_init:2 + has_init + nx]
        o_ref = refs[2 + has_init + nx]
        x_out = refs[3 + has_init + nx:3 + has_init + 2 * nx]
        acc = refs[3 + has_init + 2 * nx]
        x_sems = refs[4 + has_init + 2 * nx:]
        i, j, kk = pl.program_id(0), pl.program_id(1), pl.program_id(2)

        if nx:
            @pl.when((i == 0) & (j == 0) & (kk == 0))
            def _():
                carry.start(x_in, x_out, x_sems)

        prod = lambda: op(a_ref[...], b_ref[...])
        with_init = (lambda p: p + i_ref[...].astype(F32)) if has_init else (lambda p: p)
        if nk == 1:
            o_ref[...] = with_init(prod()).astype(out_dtype)
        else:
            @pl.when(kk == 0)
            def _():
                acc[...] = with_init(prod())

            @pl.when((kk > 0) & (kk < nk - 1))
            def _():
                acc[...] += prod()

            @pl.when(kk == nk - 1)
            def _():
                o_ref[...] = (acc[...] + prod()).astype(out_dtype)

        if nx:
            @pl.when((i == ni - 1) & (j == nj - 1) & (kk == nk - 1))
            def _():
                carry.finish(x_in, x_out, x_sems)

    in_specs = [a_spec, b_spec] + ([o_spec] if has_init else []) + [HBM_SPEC] * nx
    args = (a, b) + ((init,) if has_init else ()) + (tuple(carry.arrays) if nx else ())
    sems = carry.sem_shapes() if nx else []
    outs = pl.pallas_call(
        body, out_shape=[SDS((m if out_rows is None else out_rows, n), out_dtype)] + (carry.out_shapes if nx else []),
        grid=(ni, nj, nk),
        in_specs=in_specs, out_specs=[o_spec] + [HBM_SPEC] * nx,
        scratch_shapes=[pltpu.VMEM((tm, tn) if nk > 1 else (8, LANES), F32)] + sems,
        compiler_params=_params(("arbitrary",) * 3 if nx else ("parallel", "parallel", "arbitrary")), name=name)(*args)
    return (outs[0], outs[1:]) if nx else outs[0]


def _rmsnorm_fwd(x, w, carry=None):
    s, d = x.shape
    tr = 256
    nsteps = s // tr
    nx = len(carry.arrays) if carry is not None else 0

    def body(*refs):
        x_ref, w_ref, x_in = refs[0], refs[1], refs[2:2 + nx]
        o_ref, x_out, x_sems = refs[2 + nx], refs[3 + nx:3 + 2 * nx], refs[3 + 2 * nx:]
        if nx:
            @pl.when(pl.program_id(0) == 0)
            def _():
                carry.start(x_in, x_out, x_sems)

        xv = x_ref[...]
        r = lax.rsqrt(jnp.mean(xv * xv, axis=-1, keepdims=True) + RMS_EPS)
        o_ref[...] = (xv * r * w_ref[...]).astype(BF16)

        if nx:
            @pl.when(pl.program_id(0) == nsteps - 1)
            def _():
                carry.finish(x_in, x_out, x_sems)

    outs = pl.pallas_call(
        body, out_shape=[SDS((s, d), BF16)] + (carry.out_shapes if nx else []), grid=(nsteps,),
        in_specs=[pl.BlockSpec((tr, d), lambda i: (i, 0)), pl.BlockSpec((1, d), lambda i: (0, 0))] + [HBM_SPEC] * nx,
        out_specs=[pl.BlockSpec((tr, d), lambda i: (i, 0))] + [HBM_SPEC] * nx,
        scratch_shapes=carry.sem_shapes() if nx else [],
        compiler_params=_params(("arbitrary",) if nx else ("parallel",)), name="rmsnorm_fwd")(
            x, w, *(carry.arrays if nx else []))
    return (outs[0], outs[1:]) if nx else outs[0]


def _rmsnorm_bwd(x, w, dhn_a, dhn_b, dout):
    s, d = x.shape
    tr = 256

    def body(x_ref, w_ref, dh_ref, dh2_ref, do_ref, gx_ref, gw_ref):
        xv = x_ref[...]
        r = lax.rsqrt(jnp.mean(xv * xv, axis=-1, keepdims=True) + RMS_EPS)
        nrm = xv * r
        dh = dh_ref[...] + dh2_ref[...]
        gy = dh * w_ref[...]
        gx_ref[...] = do_ref[...] + r * (gy - nrm * jnp.mean(gy * nrm, axis=-1, keepdims=True))

        @pl.when(pl.program_id(0) == 0)
        def _():
            gw_ref[...] = jnp.zeros_like(gw_ref)

        gw_ref[...] += jnp.sum(dh * nrm, axis=0, keepdims=True)

    blk = pl.BlockSpec((tr, d), lambda i: (i, 0))
    row = pl.BlockSpec((1, d), lambda i: (0, 0))
    return pl.pallas_call(
        body, out_shape=(SDS((s, d), F32), SDS((1, d), F32)), grid=(s // tr,),
        in_specs=[blk, row, blk, blk, blk], out_specs=(blk, row),
        compiler_params=_params(("arbitrary",)), name="rmsnorm_bwd")(x, w, dhn_a, dhn_b, dout)


DEINT = DILATED_PATTERNS[-1][1]
DEINT_ROWS = DEINT * LANES


class _Pass:
    def __init__(self, tq, patterns, unit, seg_len):
        self.tq, self.patterns, self.unit, self.seg_len = tq, patterns, unit, seg_len
        self.win = max(w for w, _ in patterns) // unit
        self.w = self.win + tq
        assert self.win % tq == 0


def _attn_tables(ps):
    i = jnp.arange(ps.tq, dtype=jnp.int32)[:, None]
    j = jnp.arange(ps.w, dtype=jnp.int32)[None, :]
    delta = (i + ps.win - j) * ps.unit
    n = jnp.zeros((ps.tq, ps.w), F32)
    for window, dil in ps.patterns:
        n = n + ((delta >= 0) & (delta <= window) & (delta % dil == 0)).astype(F32)
    logn = jnp.where(n > 0, jnp.log(jnp.maximum(n, 1.0)), NEG)
    return logn, jnp.maximum(delta, 0).astype(F32)


def _slopes(h):
    s = jnp.asarray([2.0 ** (-8.0 * (i + 1) / h) for i in range(h)], F32)
    return jnp.broadcast_to(s[:, None, None], (h, 1, LANES))


def _masked_logn(ps, logn_ref, start):
    col = lax.broadcasted_iota(jnp.int32, (ps.tq, ps.w), 1)
    return jnp.where(col >= ps.win - lax.rem(start, ps.seg_len), logn_ref[...], NEG)


def _head_cols(hh):
    return slice(hh * ATTN_HEAD_DIM, (hh + 1) * ATTN_HEAD_DIM)


def _head_window(refs, cs):
    return jnp.concatenate([r[:, cs] for r in refs], axis=0)


def _head_scores(q_ref, kw, cs, base, dist_ref, slope_ref, hh):
    return _nt(q_ref[:, cs], kw) * (ATTN_HEAD_DIM ** -0.5) + (base - slope_ref[hh][0:1, 0:1] * dist_ref[...])


def _lane_of(stat, hh):
    lane = lax.broadcasted_iota(jnp.int32, stat.shape, 1)
    return jnp.sum(jnp.where(lane == hh, stat, 0.0), axis=1, keepdims=True)


def _window_specs(ps, d, col, nb):
    nprev = ps.win // ps.tq
    return [pl.BlockSpec((ps.tq, d), lambda i, b=b: (jnp.maximum(jnp.minimum(i, nb - 1) - (nprev - b), 0), col))
            for b in range(nprev + 1)]


def _attn_fwd(cfg, ps, qkv, cols, tables, slopes, name):
    s, h, d = cfg.S, cfg.H, cfg.D
    tq, nw = ps.tq, ps.win // ps.tq + 1
    nb = s // tq
    logn, dist = tables
    qc, kc, vc = [c // d for c in cols]

    def body(*refs):
        q_ref, k_refs, v_refs = refs[0], refs[1:1 + nw], refs[1 + nw:1 + 2 * nw]
        logn_ref, dist_ref, slope_ref, o_ref, lse_ref = refs[1 + 2 * nw:]
        base = _masked_logn(ps, logn_ref, pl.program_id(0) * tq)
        lane = lax.broadcasted_iota(jnp.int32, (tq, LANES), 1)

        lse = jnp.zeros((tq, LANES), F32)
        for hh in range(h):
            cs = _head_cols(hh)
            sc = _head_scores(q_ref, _head_window(k_refs, cs), cs, base, dist_ref, slope_ref, hh)
            m = jnp.max(sc, axis=1, keepdims=True)
            p = jnp.exp(sc - m)
            l = jnp.sum(p, axis=1, keepdims=True)
            o_ref[:, cs] = (_nn(p.astype(BF16), _head_window(v_refs, cs)) / l).astype(BF16)
            lse = jnp.where(lane == hh, m + jnp.log(l), lse)
        lse_ref[...] = lse

    tab = pl.BlockSpec((tq, ps.w), lambda i: (0, 0))
    return pl.pallas_call(
        body, out_shape=(SDS((s, d), BF16), SDS((s, LANES), F32)), grid=(nb,),
        in_specs=[pl.BlockSpec((tq, d), lambda i: (i, qc))] + _window_specs(ps, d, kc, nb) + _window_specs(ps, d, vc, nb)
        + [tab, tab, pl.BlockSpec((h, 1, LANES), lambda i: (0, 0, 0))],
        out_specs=(pl.BlockSpec((tq, d), lambda i: (i, 0)), pl.BlockSpec((tq, LANES), lambda i: (i, 0))),
        compiler_params=_params(("parallel",)), name=name)(*([qkv] * (1 + 2 * nw)), logn, dist, slopes)


def _attn_bwd(cfg, ps, qkv, cols, do, lse, delta, tables, slopes, name):
    s, h, d = cfg.S, cfg.H, cfg.D
    tq, nprev = ps.tq, ps.win // ps.tq
    nw = nprev + 1
    nb = s // tq
    logn, dist = tables
    qc, kc, vc = [c // d for c in cols]
    scale = ATTN_HEAD_DIM ** -0.5

    def body(*refs):
        q_ref, k_refs, v_refs = refs[0], refs[1:1 + nw], refs[1 + nw:1 + 2 * nw]
        do_ref, lse_ref, dl_ref, logn_ref, dist_ref, slope_ref, dq_ref, dk_ref, dv_ref, ck, cv = refs[1 + 2 * nw:]
        i = pl.program_id(0)
        slot = lambda b: lax.rem(i + b, nprev)

        @pl.when(i == 0)
        def _():
            ck[...] = jnp.zeros_like(ck)
            cv[...] = jnp.zeros_like(cv)

        @pl.when(i < nb)
        def _():
            base = _masked_logn(ps, logn_ref, i * tq)
            lse_all, dl_all = lse_ref[...], dl_ref[...]

            for hh in range(h):
                cs = _head_cols(hh)
                kw, vw = _head_window(k_refs, cs), _head_window(v_refs, cs)
                sc = _head_scores(q_ref, kw, cs, base, dist_ref, slope_ref, hh)
                p = jnp.exp(sc - lse_all[:, hh:hh + 1])
                dob = do_ref[:, cs]
                ds = (p * (_nt(dob, vw) - dl_all[:, hh:hh + 1]) * scale).astype(BF16)
                dq_ref[:, cs] = _nn(ds, kw).astype(BF16)
                dkw = _tn(ds, q_ref[:, cs])
                dvw = _tn(p.astype(BF16), dob)
                dk_ref[:, cs] = ck[slot(0), :, cs] + dkw[0:tq]
                dv_ref[:, cs] = cv[slot(0), :, cs] + dvw[0:tq]
                for b in range(1, nprev):
                    ck[slot(b), :, cs] += dkw[b * tq:(b + 1) * tq]
                    cv[slot(b), :, cs] += dvw[b * tq:(b + 1) * tq]
                ck[slot(0), :, cs] = dkw[nprev * tq:]
                cv[slot(0), :, cs] = dvw[nprev * tq:]

        @pl.when(i >= nb)
        def _():
            dk_ref[...] = ck[slot(0)]
            dv_ref[...] = cv[slot(0)]

    here = lambda i: jnp.minimum(i, nb - 1)
    blk = pl.BlockSpec((tq, d), lambda i: (here(i), 0))
    stat = pl.BlockSpec((tq, LANES), lambda i: (here(i), 0))
    late = pl.BlockSpec((tq, d), lambda i: (jnp.maximum(i - nprev, 0), 0))
    tab = pl.BlockSpec((tq, ps.w), lambda i: (0, 0))
    return pl.pallas_call(
        body, out_shape=(SDS((s, d), BF16), SDS((s, d), F32), SDS((s, d), F32)), grid=(nb + nprev,),
        in_specs=[pl.BlockSpec((tq, d), lambda i: (here(i), qc))] + _window_specs(ps, d, kc, nb)
        + _window_specs(ps, d, vc, nb) + [blk, stat, stat, tab, tab, pl.BlockSpec((h, 1, LANES), lambda i: (0, 0, 0))],
        out_specs=(blk, late, late),
        scratch_shapes=[pltpu.VMEM((nprev, tq, d), F32), pltpu.VMEM((nprev, tq, d), F32)],
        compiler_params=_params(("arbitrary",)), name=name)(
            *([qkv] * (1 + 2 * nw)), do, lse, delta, logn, dist, slopes)


def _by_residue(a):
    return a.reshape(DEINT, a.shape[0] // DEINT, a.shape[1])


def _deint_spec(colblock):
    return pl.BlockSpec((DEINT, LANES, LANES), lambda b, j: (0, b, colblock(j)))


def _deint_rows(scr, out_ref, dtype):
    for r in range(DEINT):
        out_ref[r] = scr[pl.ds(r, LANES, stride=DEINT), :].astype(dtype)


def _int_rows(in_ref, scr):
    for r in range(DEINT):
        scr[pl.ds(r, LANES, stride=DEINT), :] = in_ref[r].astype(F32)


WIDE = 4 * LANES


def _wide_spec():
    return pl.BlockSpec((DEINT, LANES, WIDE), lambda b, j: (0, b, j))


def _deinterleave(x, col0, ncols, name):
    s = x.shape[0]
    c0 = col0 // WIDE

    def body(x_ref, o_ref, scr):
        for t in range(WIDE // LANES):
            cs = slice(t * LANES, (t + 1) * LANES)
            scr[t] = x_ref[:, cs].astype(F32)
            for r in range(DEINT):
                o_ref[r, :, cs] = scr.at[t][pl.ds(r, LANES, stride=DEINT), :].astype(x.dtype)

    out = pl.pallas_call(
        body, out_shape=SDS((DEINT, s // DEINT, ncols), x.dtype), grid=(s // DEINT_ROWS, ncols // WIDE),
        in_specs=[pl.BlockSpec((DEINT_ROWS, WIDE), lambda b, j: (b, c0 + j))],
        out_specs=_wide_spec(),
        scratch_shapes=[pltpu.VMEM((WIDE // LANES, DEINT_ROWS, LANES), F32)],
        compiler_params=_params(("parallel", "parallel")), name=name)(x)
    return out.reshape(s, ncols)


def _attn_merge(cfg, proj, o_1, lse_1, o_2, lse_2):
    s, h = cfg.S, cfg.H
    zb = cfg.OZA // WIDE
    rows = DEINT_ROWS
    hps = WIDE // LANES

    def body(o1_ref, l1_ref, o2_ref, l2_ref, z_ref, o_ref, og_ref, lse_ref, so, sl):
        j = pl.program_id(1)

        @pl.when(j == 0)
        def _():
            _int_rows(l2_ref, sl)
            lse_ref[...] = jnp.zeros_like(lse_ref)

        l1_all, l2_all = l1_ref[...], sl[...]
        lane = lax.broadcasted_iota(jnp.int32, (rows, LANES), 1)
        lse = lse_ref[...]
        for t in range(hps):
            hh = j * hps + t
            cs = slice(t * LANES, (t + 1) * LANES)
            for r in range(DEINT):
                so.at[t][pl.ds(r, LANES, stride=DEINT), :] = o2_ref[r, :, cs].astype(F32)
            l1, l2 = _lane_of(l1_all, hh), _lane_of(l2_all, hh)
            mx = jnp.maximum(l1, l2)
            w1, w2 = jnp.exp(l1 - mx), jnp.exp(l2 - mx)
            den = w1 + w2
            o = (w1 * o1_ref[:, cs].astype(F32) + w2 * so[t]) / den
            z = z_ref[:, cs].astype(F32)
            o_ref[:, cs] = o.astype(BF16)
            og_ref[:, cs] = (o * (z * _sigmoid(z))).astype(BF16)
            lse = jnp.where(lane == hh, mx + jnp.log(den), lse)
        lse_ref[...] = lse

    blk = pl.BlockSpec((rows, WIDE), lambda b, j: (b, j))
    stat = pl.BlockSpec((rows, LANES), lambda b, j: (b, 0))
    return pl.pallas_call(
        body, out_shape=(SDS((s, cfg.D), BF16), SDS((s, cfg.D), BF16), SDS((s, LANES), F32)),
        grid=(s // rows, h // hps),
        in_specs=[blk, stat, _wide_spec(), _deint_spec(lambda j: 0), pl.BlockSpec((rows, WIDE), lambda b, j: (b, zb + j))],
        out_specs=(blk, blk, stat),
        scratch_shapes=[pltpu.VMEM((hps, rows, LANES), F32), pltpu.VMEM((rows, LANES), F32)],
        compiler_params=_params(("parallel", "arbitrary")), name="attn_merge")(
            o_1, lse_1, _by_residue(o_2), _by_residue(lse_2), proj)


def _attn_bwd_prep(cfg, proj, o_a, doag, lse, dproj):
    s, h = cfg.S, cfg.H
    zb = cfg.OZA // WIDE
    rows = DEINT_ROWS
    hps = WIDE // LANES

    def body(o_ref, dg_ref, z_ref, lse_ref, dp_in, dz_ref, do_ref, do2_ref, dl_ref, dl2_ref, lse2_ref, scr):
        del dp_in
        j = pl.program_id(1)

        @pl.when(j == 0)
        def _():
            dl_ref[...] = jnp.zeros_like(dl_ref)

        lane = lax.broadcasted_iota(jnp.int32, (rows, LANES), 1)
        dl = dl_ref[...]
        for t in range(hps):
            cs = slice(t * LANES, (t + 1) * LANES)
            z = z_ref[:, cs].astype(F32)
            sg = _sigmoid(z)
            o = o_ref[:, cs].astype(F32)
            dg = dg_ref[:, cs].astype(F32)
            do = dg * (z * sg)
            dz_ref[:, cs] = (dg * o * (sg * (1.0 + z * (1.0 - sg)))).astype(BF16)
            do_ref[:, cs] = do.astype(BF16)
            scr[...] = do
            for r in range(DEINT):
                do2_ref[r, :, cs] = scr[pl.ds(r, LANES, stride=DEINT), :].astype(BF16)
            dl = jnp.where(lane == j * hps + t, jnp.sum(do * o, axis=1, keepdims=True), dl)
        dl_ref[...] = dl

        @pl.when(j == h // hps - 1)
        def _():
            scr[...] = dl
            _deint_rows(scr, dl2_ref, F32)
            scr[...] = lse_ref[...]
            _deint_rows(scr, lse2_ref, F32)

    blk = pl.BlockSpec((rows, WIDE), lambda b, j: (b, j))
    stat = pl.BlockSpec((rows, LANES), lambda b, j: (b, 0))
    stat2 = _deint_spec(lambda j: 0)
    outs = pl.pallas_call(
        body,
        out_shape=(SDS(dproj.shape, BF16), SDS((s, cfg.D), BF16), SDS((DEINT, s // DEINT, cfg.D), BF16),
                   SDS((s, LANES), F32), SDS((DEINT, s // DEINT, LANES), F32), SDS((DEINT, s // DEINT, LANES), F32)),
        grid=(s // rows, h // hps),
        in_specs=[blk, blk, pl.BlockSpec((rows, WIDE), lambda b, j: (b, zb + j)), stat, HBM_SPEC],
        out_specs=(pl.BlockSpec((rows, WIDE), lambda b, j: (b, zb + j)), blk, _wide_spec(), stat, stat2, stat2),
        scratch_shapes=[pltpu.VMEM((rows, LANES), F32)],
        input_output_aliases={4: 0},
        compiler_params=_params(("parallel", "arbitrary")), name="attn_bwd_prep")(o_a, doag, proj, lse, dproj)
    dproj, do, do2, dl, dl2, lse2 = outs
    return dproj, do, do2.reshape(s, cfg.D), dl, dl2.reshape(s, LANES), lse2.reshape(s, LANES)


def _attn_grad_sum(cfg, g_1, g_2, col0, dproj, name):
    s = cfg.S
    c0 = col0 // WIDE
    rows = DEINT_ROWS

    def body(g1_ref, g2_ref, dp_in, o_ref, scr):
        del dp_in
        for t in range(WIDE // LANES):
            cs = slice(t * LANES, (t + 1) * LANES)
            for r in range(DEINT):
                scr.at[t][pl.ds(r, LANES, stride=DEINT), :] = g2_ref[r, :, cs].astype(F32)
            o_ref[:, cs] = (g1_ref[:, cs].astype(F32) + scr[t]).astype(BF16)

    return pl.pallas_call(
        body, out_shape=SDS(dproj.shape, BF16), grid=(s // rows, cfg.D // WIDE),
        in_specs=[pl.BlockSpec((rows, WIDE), lambda b, j: (b, j)), _wide_spec(), HBM_SPEC],
        out_specs=pl.BlockSpec((rows, WIDE), lambda b, j: (b, c0 + j)),
        scratch_shapes=[pltpu.VMEM((WIDE // LANES, rows, LANES), F32)],
        input_output_aliases={2: 0},
        compiler_params=_params(("parallel", "parallel")), name=name)(g_1, _by_residue(g_2), dproj)


CONV_HALO = 16
CONV_TR = 512
CONV_CW = 1024


def _rows_back(a, n):
    return a if n == 0 else pltpu.roll(a, n % a.shape[0], axis=0)


def _conv_fwd(cfg, proj, conv_w, conv_b):
    s, cd = cfg.S, cfg.CD
    tr, cw, hl = CONV_TR, CONV_CW, CONV_HALO
    cb0 = cfg.OXBC // cw

    def body(x_ref, h_ref, w_ref, b_ref, o_ref):
        i = pl.program_id(0)
        halo = jnp.where(i > 0, h_ref[...].astype(F32), 0.0)
        ext = jnp.concatenate([halo, x_ref[...].astype(F32)], axis=0)
        pre = b_ref[...] + jnp.zeros((tr, cw), F32)
        for k in range(CONV_K):
            pre = pre + w_ref[k:k + 1, :] * _rows_back(ext, CONV_K - 1 - k)[hl:]
        o_ref[...] = (pre * _sigmoid(pre)).astype(BF16)

    return pl.pallas_call(
        body, out_shape=SDS((s, cd), BF16), grid=(s // tr, cd // cw),
        in_specs=[pl.BlockSpec((tr, cw), lambda i, j: (i, cb0 + j)),
                  pl.BlockSpec((hl, cw), lambda i, j: (jnp.maximum(i * (tr // hl) - 1, 0), cb0 + j)),
                  pl.BlockSpec((CONV_K, cw), lambda i, j: (0, j)),
                  pl.BlockSpec((1, cw), lambda i, j: (0, j))],
        out_specs=pl.BlockSpec((tr, cw), lambda i, j: (i, j)),
        compiler_params=_params(("parallel", "parallel")), name="conv_fwd")(proj, proj, conv_w, conv_b)


def _conv_bwd(cfg, proj, dact, conv_w, conv_b, dproj):
    s, cd = cfg.S, cfg.CD
    tr, cw, hl = CONV_TR, CONV_CW, CONV_HALO
    cb0 = cfg.OXBC // cw
    nr = s // tr
    last_h = s // hl - 1

    def body(x_ref, hp_ref, hn_ref, d_ref, dn_ref, w_ref, b_ref, dp_in, dx_ref, gw_ref, gb_ref):
        del dp_in
        i = pl.program_id(1)
        ext = jnp.concatenate([jnp.where(i > 0, hp_ref[...].astype(F32), 0.0), x_ref[...].astype(F32),
                               hn_ref[...].astype(F32)], axis=0)
        shifted = [_rows_back(ext, CONV_K - 1 - k)[hl:] for k in range(CONV_K)]
        pre = b_ref[...] + jnp.zeros((tr + hl, cw), F32)
        for k in range(CONV_K):
            pre = pre + w_ref[k:k + 1, :] * shifted[k]
        sg = _sigmoid(pre)
        dact = jnp.concatenate([d_ref[...].astype(F32), jnp.where(i < nr - 1, dn_ref[...].astype(F32), 0.0)], axis=0)
        dpre = dact * (sg * (1.0 + pre * (1.0 - sg)))
        dx = jnp.zeros((tr, cw), F32)
        for k in range(CONV_K):
            dx = dx + w_ref[k:k + 1, :] * _rows_back(dpre, -(CONV_K - 1 - k))[0:tr]
        dx_ref[...] = dx.astype(BF16)

        @pl.when(i == 0)
        def _():
            gw_ref[...] = jnp.zeros_like(gw_ref)
            gb_ref[...] = jnp.zeros_like(gb_ref)

        dcur = dpre[0:tr]
        gb_ref[...] += jnp.sum(dcur, axis=0, keepdims=True)
        for k in range(CONV_K):
            gw_ref[k:k + 1, :] += jnp.sum(dcur * shifted[k][0:tr], axis=0, keepdims=True)

    return pl.pallas_call(
        body, out_shape=(SDS(dproj.shape, BF16), SDS((CONV_K, cd), F32), SDS((1, cd), F32)), grid=(cd // cw, nr),
        in_specs=[pl.BlockSpec((tr, cw), lambda j, i: (i, cb0 + j)),
                  pl.BlockSpec((hl, cw), lambda j, i: (jnp.maximum(i * (tr // hl) - 1, 0), cb0 + j)),
                  pl.BlockSpec((hl, cw), lambda j, i: (jnp.minimum((i + 1) * (tr // hl), last_h), cb0 + j)),
                  pl.BlockSpec((tr, cw), lambda j, i: (i, j)),
                  pl.BlockSpec((hl, cw), lambda j, i: (jnp.minimum((i + 1) * (tr // hl), last_h), j)),
                  pl.BlockSpec((CONV_K, cw), lambda j, i: (0, j)),
                  pl.BlockSpec((1, cw), lambda j, i: (0, j)),
                  pl.BlockSpec(memory_space=pl.ANY)],
        out_specs=(pl.BlockSpec((tr, cw), lambda j, i: (i, cb0 + j)),
                   pl.BlockSpec((CONV_K, cw), lambda j, i: (0, j)),
                   pl.BlockSpec((1, cw), lambda j, i: (0, j))),
        input_output_aliases={7: 0},
        compiler_params=_params(("parallel", "arbitrary")), name="conv_bwd")(
            proj, proj, proj, dact, dact, conv_w, conv_b, dproj)


def _expand(v, e, terms):
    out, rem = None, v
    for _ in range(terms):
        hi = rem.astype(BF16)
        t = _nn(hi, e)
        out = t if out is None else out + t
        rem = rem - hi.astype(F32)
    return out


def _segsum(v, e, terms):
    out, rem = None, v
    for _ in range(terms):
        hi = rem.astype(BF16)
        t = _nt(hi, e)
        out = t if out is None else out + t
        rem = rem - hi.astype(F32)
    return out


def _expand_row(row, e, terms):
    return _expand(jnp.broadcast_to(row, (8, LANES)), e, terms)[0:1]


def _segsum_row(row, e, terms):
    return _segsum(jnp.broadcast_to(row, (8, row.shape[1])), e, terms)[0:1]


def _expansion_matrix(cfg):
    hh = jnp.arange(LANES, dtype=jnp.int32)[:, None]
    cc = jnp.arange(cfg.SI, dtype=jnp.int32)[None, :]
    return (cc // SSM_HEAD_DIM == hh).astype(BF16)


def _tri(lower):
    r = lax.broadcasted_iota(jnp.int32, (CHUNK, CHUNK), 0)
    c = lax.broadcasted_iota(jnp.int32, (CHUNK, CHUNK), 1)
    return (c <= r) if lower else (c >= r)


def _ssd_prep(dtr_ref, db_ref, al_ref, e):
    dtr = dtr_ref[...] + db_ref[...]
    dt = _softplus(dtr)
    a = -jnp.exp(al_ref[...])
    acum = jnp.dot(_tri(True).astype(F32), dt * a, precision=lax.Precision.HIGHEST, preferred_element_type=F32)
    return dtr, dt, a, _expand(dt, e, 2), _expand(acum, e, 3)


def _ssd_fwd(cfg, xact, dt_raw, proj, dt_bias, a_log, d_skip, norm_w, e):
    s, si, cd, gw, bc = cfg.S, cfg.SI, cfg.CD, cfg.GW, cfg.BC
    nc = s // CHUNK
    zb = cfg.OZS // si
    tiles = gw // LANES

    def body(xa_ref, dtr_ref, z_ref, db_ref, al_ref, dsk_ref, nw_ref, e_ref, y_ref, y2_ref, st_ref,
             state, ybuf, x_s, xw_s, ae_s, ea_s, lam_s):
        @pl.when(pl.program_id(0) == 0)
        def _():
            state[...] = jnp.zeros_like(state)

        st_ref[...] = state[...]
        ev = e_ref[...]
        _, _, _, dt_e, a_e = _ssd_prep(dtr_ref, db_ref, al_ref, ev)
        xs = xa_ref[:, 0:si].astype(F32)
        x = xs * dt_e
        lam_e = a_e[CHUNK - 1:CHUNK, :]
        x_s[...] = x.astype(BF16)
        xw_s[...] = (x * jnp.exp(lam_e - a_e)).astype(BF16)
        ae_s[...] = a_e
        ea_s[...] = jnp.exp(a_e)
        skip = _expand_row(dsk_ref[...], ev, 3) * xs
        lam_s[...] = jnp.broadcast_to(jnp.exp(lam_e), (8, si))
        tril = _tri(True)
        lane = lax.broadcasted_iota(jnp.int32, (CHUNK, LANES), 1)

        def group(g, carry):
            co = g * gw
            bg = xa_ref[:, pl.ds(si + g * SSM_STATE, SSM_STATE)]
            cg = xa_ref[:, pl.ds(si + bc + g * SSM_STATE, SSM_STATE)]
            cbm = _nt(cg, bg)
            st = state[:, pl.ds(co, gw)]
            yoff = _nn(cg, st.astype(BF16)) * ea_s[:, pl.ds(co, gw)]
            for k in range(tiles):
                tc = co + k * LANES
                at = ae_s[:, pl.ds(tc, LANES)]
                att = at.T
                xt = x_s[:, pl.ds(tc, LANES)]
                acc = yoff[:, k * LANES:(k + 1) * LANES]
                for half in range(2):
                    lo = half * SSM_HEAD_DIM
                    seg = at[:, lo:lo + 1] - att[lo:lo + 1, :]
                    dec = jnp.exp(jnp.where(tril, seg, NEG))
                    xh = jnp.where((lane >= lo) & (lane < lo + SSM_HEAD_DIM), xt, jnp.zeros_like(xt))
                    acc = acc + _nn((cbm * dec).astype(BF16), xh)
                ybuf[:, pl.ds(tc, LANES)] = acc + skip[:, tc:tc + LANES]
            state[:, pl.ds(co, gw)] = st * lam_s[0:1, pl.ds(co, gw)] + _tn(bg, xw_s[:, pl.ds(co, gw)])
            return carry

        for g in range(SSM_GROUPS):
            group(g, 0)
        y = ybuf[...]
        y_ref[...] = y.astype(BF16)
        z = z_ref[...].astype(F32)
        u = y * (z * _sigmoid(z))
        r = lax.rsqrt(jnp.mean(u * u, axis=-1, keepdims=True) + RMS_EPS)
        y2_ref[...] = (u * r * nw_ref[...]).astype(BF16)

    row = lambda n: pl.BlockSpec((1, n), lambda c: (0, 0))
    return pl.pallas_call(
        body,
        out_shape=(SDS((s, si), BF16), SDS((s, si), BF16), SDS((nc, SSM_STATE, si), F32)),
        grid=(nc,),
        in_specs=[pl.BlockSpec((CHUNK, cd), lambda c: (c, 0)),
                  pl.BlockSpec((CHUNK, LANES), lambda c: (c, 0)),
                  pl.BlockSpec((CHUNK, si), lambda c: (c, zb)),
                  row(LANES), row(LANES), row(LANES), row(si),
                  pl.BlockSpec((LANES, si), lambda c: (0, 0))],
        out_specs=(pl.BlockSpec((CHUNK, si), lambda c: (c, 0)),
                   pl.BlockSpec((CHUNK, si), lambda c: (c, 0)),
                   pl.BlockSpec((None, SSM_STATE, si), lambda c: (c, 0, 0))),
        scratch_shapes=[pltpu.VMEM((SSM_STATE, si), F32), pltpu.VMEM((CHUNK, si), F32),
                        pltpu.VMEM((CHUNK, si), BF16), pltpu.VMEM((CHUNK, si), BF16),
                        pltpu.VMEM((CHUNK, si), F32), pltpu.VMEM((CHUNK, si), F32),
                        pltpu.VMEM((8, si), F32)],
        compiler_params=_params(("arbitrary",)), name="ssd_fwd")(
            xact, dt_raw, proj, dt_bias, a_log, d_skip, norm_w, e)


def _ssd_bwd(cfg, xact, dt_raw, proj, y, dy2, states, dt_bias, a_log, d_skip, norm_w, e, dproj):
    s, si, cd, gw, bc, hpg = cfg.S, cfg.SI, cfg.CD, cfg.GW, cfg.BC, cfg.HPG
    nc = s // CHUNK
    zb = cfg.OZS // si
    tiles = gw // LANES

    def body(xa_ref, dtr_ref, z_ref, y_ref, d2_ref, st_ref, db_ref, al_ref, dsk_ref, nw_ref, e_ref, dp_in,
             dz_ref, dxa_ref, ddt_ref, gnw_ref, gdb_ref, gal_ref, gds_ref,
             dh, dhn, xs_s, x_s, w_s, ae_s, ea_s, g_s, dx_s, dae_s, lam_s, dle_s):
        del dp_in

        @pl.when(pl.program_id(0) == 0)
        def _():
            dh[...] = jnp.zeros_like(dh)
            gnw_ref[...] = jnp.zeros_like(gnw_ref)
            gdb_ref[...] = jnp.zeros_like(gdb_ref)
            gal_ref[...] = jnp.zeros_like(gal_ref)
            gds_ref[...] = jnp.zeros_like(gds_ref)

        ev = e_ref[...]
        yv = y_ref[...].astype(F32)
        z = z_ref[...].astype(F32)
        sg = _sigmoid(z)
        sz = z * sg
        u = yv * sz
        r = lax.rsqrt(jnp.mean(u * u, axis=-1, keepdims=True) + RMS_EPS)
        nrm = u * r
        d2 = d2_ref[...].astype(F32)
        gnw_ref[...] += jnp.sum(d2 * nrm, axis=0, keepdims=True)
        gn = d2 * nw_ref[...]
        du = r * (gn - nrm * jnp.mean(gn * nrm, axis=-1, keepdims=True))
        gv = du * sz
        dz_ref[...] = (du * yv * (sg * (1.0 + z * (1.0 - sg)))).astype(BF16)
        g_s[...] = gv

        dtr, dt, a, dt_e, a_e = _ssd_prep(dtr_ref, db_ref, al_ref, ev)
        xs = xa_ref[:, 0:si].astype(F32)
        x = xs * dt_e
        lam_e = a_e[CHUNK - 1:CHUNK, :]
        xs_s[...] = xs
        x_s[...] = x
        w_s[...] = jnp.exp(lam_e - a_e)
        ae_s[...] = a_e
        ea_s[...] = jnp.exp(a_e)
        lam_s[...] = jnp.broadcast_to(jnp.exp(lam_e), (8, si))
        gds_ref[...] += _segsum_row(jnp.sum(gv * xs, axis=0, keepdims=True), ev, 2)
        col_sums = [jnp.zeros((CHUNK, LANES), F32)]
        tril = _tri(True)
        lane = lax.broadcasted_iota(jnp.int32, (CHUNK, LANES), 1)
        sub = lax.broadcasted_iota(jnp.int32, (CHUNK, LANES), 0)

        def group(g, carry):
            co = g * gw
            bo = si + g * SSM_STATE
            cof = si + bc + g * SSM_STATE
            cols = pl.ds(co, gw)
            bg = xa_ref[:, pl.ds(bo, SSM_STATE)]
            cg = xa_ref[:, pl.ds(cof, SSM_STATE)]
            cbm = _nt(cg, bg)
            st = st_ref[:, cols]
            stb = st.astype(BF16)
            dho = dh[:, cols]
            dhob = dho.astype(BF16)
            ea = ea_s[:, cols]
            gg = g_s[:, cols]
            xg = x_s[:, cols]
            wg = w_s[:, cols]
            explam = lam_s[0:1, cols]
            yoff = _nn(cg, stb) * ea
            ga = (gg * ea).astype(BF16)
            dc = _nt(ga, stb)
            dhn[:, cols] = dho * explam + _tn(cg, ga)
            bdh = _nn(bg, dhob)
            db = _nt((xg * wg).astype(BF16), dhob)
            t = xg * bdh * wg
            dle_s[0:1, cols] = jnp.sum(t, axis=0, keepdims=True) + explam * jnp.sum(dho * st, axis=0, keepdims=True)
            dae_base = gg * yoff - t
            dxw = wg * bdh
            dcb = jnp.zeros((CHUNK, CHUNK), F32)
            for k in range(tiles):
                tc = co + k * LANES
                ksl = slice(k * LANES, (k + 1) * LANES)
                at = ae_s[:, pl.ds(tc, LANES)]
                att = at.T
                xt = xg[:, ksl].astype(BF16)
                gt = gg[:, ksl].astype(BF16)
                dxt = dxw[:, ksl]
                place = jnp.zeros((CHUNK, LANES), F32)
                for half in range(2):
                    lo = half * SSM_HEAD_DIM
                    seg = at[:, lo:lo + 1] - att[lo:lo + 1, :]
                    dec = jnp.exp(jnp.where(tril, seg, NEG))
                    mh = cbm * dec
                    gh = jnp.where((lane >= lo) & (lane < lo + SSM_HEAD_DIM), gt, jnp.zeros_like(gt))
                    dm = _nt(gh, xt)
                    dxt = dxt + _tn(mh.astype(BF16), gh)
                    dcb = dcb + dm * dec
                    dseg = dm * mh
                    place = place + jnp.where(lane == lo, jnp.sum(dseg, axis=1, keepdims=True), 0.0)
                    hidx = g * hpg + 2 * k + half
                    col_sums[0] = col_sums[0] + jnp.where(sub == hidx, jnp.sum(dseg, axis=0, keepdims=True), 0.0)
                dx_s[:, pl.ds(tc, LANES)] = dxt
                dae_s[:, pl.ds(tc, LANES)] = dae_base[:, ksl] + place
            dcbb = dcb.astype(BF16)
            dxa_ref[:, pl.ds(bo, SSM_STATE)] = (db + _tn(dcbb, cg)).astype(BF16)
            dxa_ref[:, pl.ds(cof, SSM_STATE)] = (dc + _nn(dcbb, bg)).astype(BF16)
            return carry

        for g in range(SSM_GROUPS):
            group(g, 0)
        dlam = _segsum_row(dle_s[0:1, :], ev, 2)
        da_ = _segsum(dae_s[...], ev, 2) - col_sums[0].T
        da_ = da_ + jnp.where(sub == CHUNK - 1, dlam, 0.0)
        dda = jnp.dot(_tri(False).astype(F32), da_, precision=lax.Precision.HIGHEST, preferred_element_type=F32)
        dxv = dx_s[...]
        xs = xs_s[...]
        ddt = dda * a + _segsum(dxv * xs, ev, 2)
        gal_ref[...] += jnp.sum(dda * dt, axis=0, keepdims=True) * a
        ddtr = ddt * _sigmoid(dtr)
        gdb_ref[...] += jnp.sum(ddtr, axis=0, keepdims=True)
        ddt_ref[...] = ddtr
        dxa_ref[:, 0:si] = (dxv * dt_e + g_s[...] * _expand_row(dsk_ref[...], ev, 3)).astype(BF16)
        dh[...] = dhn[...]

    rev = lambda c: nc - 1 - c
    row = lambda n: pl.BlockSpec((1, n), lambda c: (0, 0))
    big = lambda: pltpu.VMEM((CHUNK, si), F32)
    return pl.pallas_call(
        body,
        out_shape=(SDS(dproj.shape, BF16), SDS((s, cd), BF16), SDS((s, LANES), F32),
                   SDS((1, si), F32), SDS((1, LANES), F32), SDS((1, LANES), F32), SDS((1, LANES), F32)),
        grid=(nc,),
        in_specs=[pl.BlockSpec((CHUNK, cd), lambda c: (rev(c), 0)),
                  pl.BlockSpec((CHUNK, LANES), lambda c: (rev(c), 0)),
                  pl.BlockSpec((CHUNK, si), lambda c: (rev(c), zb)),
                  pl.BlockSpec((CHUNK, si), lambda c: (rev(c), 0)),
                  pl.BlockSpec((CHUNK, si), lambda c: (rev(c), 0)),
                  pl.BlockSpec((None, SSM_STATE, si), lambda c: (rev(c), 0, 0)),
                  row(LANES), row(LANES), row(LANES), row(si),
                  pl.BlockSpec((LANES, si), lambda c: (0, 0)),
                  pl.BlockSpec(memory_space=pl.ANY)],
        out_specs=(pl.BlockSpec((CHUNK, si), lambda c: (rev(c), zb)),
                   pl.BlockSpec((CHUNK, cd), lambda c: (rev(c), 0)),
                   pl.BlockSpec((CHUNK, LANES), lambda c: (rev(c), 0)),
                   row(si), row(LANES), row(LANES), row(LANES)),
        scratch_shapes=[pltpu.VMEM((SSM_STATE, si), F32), pltpu.VMEM((SSM_STATE, si), F32),
                        big(), big(), big(), big(), big(), big(), big(), big(),
                        pltpu.VMEM((8, si), F32), pltpu.VMEM((8, si), F32)],
        input_output_aliases={11: 0},
        compiler_params=_params(("arbitrary",)), name="ssd_bwd")(
            xact, dt_raw, proj, y, dy2, states, dt_bias, a_log, d_skip, norm_w, e, dproj)


MERGE_TR = 512
MERGE_CW = 2048


def _merge_fwd(cfg, proj, a_br, s_br):
    s, d = cfg.S, cfg.D
    tr, cw = MERGE_TR, min(MERGE_CW, d)
    ga0, gs0 = cfg.OGA // cw, cfg.OGS // cw

    def body(ga_ref, gs_ref, a_ref, s_ref, o_ref):
        o_ref[...] = (_sigmoid(ga_ref[...].astype(F32)) * a_ref[...].astype(F32)
                      + _sigmoid(gs_ref[...].astype(F32)) * s_ref[...].astype(F32)).astype(BF16)

    blk = pl.BlockSpec((tr, cw), lambda i, j: (i, j))
    return pl.pallas_call(
        body, out_shape=SDS((s, d), BF16), grid=(s // tr, d // cw),
        in_specs=[pl.BlockSpec((tr, cw), lambda i, j: (i, ga0 + j)),
                  pl.BlockSpec((tr, cw), lambda i, j: (i, gs0 + j)), blk, blk],
        out_specs=blk, compiler_params=_params(("parallel", "parallel")), name="merge_fwd")(proj, proj, a_br, s_br)


def _merge_bwd(cfg, proj, branch, dmerged, gate_off, dproj, name):
    s, d = cfg.S, cfg.D
    tr, cw = MERGE_TR, min(MERGE_CW, d)
    g0 = gate_off // cw
    fresh = dproj is None

    def body(*refs):
        g_ref, b_ref, dm_ref = refs[:3]
        dg_ref, db_ref = refs[-2:]
        dm = dm_ref[...].astype(F32)
        sg = _sigmoid(g_ref[...].astype(F32))
        db_ref[...] = (dm * sg).astype(BF16)
        dg_ref[...] = (dm * b_ref[...].astype(F32) * sg * (1.0 - sg)).astype(BF16)

    blk = pl.BlockSpec((tr, cw), lambda i, j: (i, j))
    gate = pl.BlockSpec((tr, cw), lambda i, j: (i, g0 + j))
    return pl.pallas_call(
        body, out_shape=(SDS((s, cfg.NM), BF16), SDS((s, d), BF16)), grid=(s // tr, d // cw),
        in_specs=[gate, blk, blk] + ([] if fresh else [HBM_SPEC]),
        out_specs=(gate, blk),
        input_output_aliases={} if fresh else {3: 0},
        compiler_params=_params(("parallel", "parallel")), name=name)(
            *((proj, branch, dmerged) + (() if fresh else (dproj,))))


def _outproj_loss(merged, w_out, x, target, fnw):
    s, d = x.shape
    tr = 256

    def body(m_ref, w_ref, x_ref, t_ref, fw_ref, dof_ref, dob_ref, loss_ref, g_ref):
        out = x_ref[...] + _nn(m_ref[...], w_ref[...])
        r = lax.rsqrt(jnp.mean(out * out, axis=-1, keepdims=True) + RMS_EPS)
        nrm = out * r
        fw = fw_ref[...]
        err = nrm * fw - t_ref[...]
        dy = err * (1.0 / d)
        gy = dy * fw
        dout = r * (gy - nrm * jnp.mean(gy * nrm, axis=-1, keepdims=True))
        dof_ref[...] = dout
        dob_ref[...] = dout.astype(BF16)

        @pl.when(pl.program_id(0) == 0)
        def _():
            loss_ref[...] = jnp.zeros_like(loss_ref)
            g_ref[...] = jnp.zeros_like(g_ref)

        loss_ref[...] += jnp.sum(jnp.sum(err * err, axis=1, keepdims=True), axis=0, keepdims=True) * (0.5 / d)
        g_ref[...] += jnp.sum(dy * nrm, axis=0, keepdims=True)

    blk = pl.BlockSpec((tr, d), lambda i: (i, 0))
    return pl.pallas_call(
        body, out_shape=(SDS((s, d), F32), SDS((s, d), BF16), SDS((1, LANES), F32), SDS((1, d), F32)), grid=(s // tr,),
        in_specs=[blk, pl.BlockSpec((d, d), lambda i: (0, 0)), blk, blk, pl.BlockSpec((1, d), lambda i: (0, 0))],
        out_specs=(blk, blk, pl.BlockSpec((1, LANES), lambda i: (0, 0)), pl.BlockSpec((1, d), lambda i: (0, 0))),
        compiler_params=_params(("arbitrary",)), name="outproj_loss")(merged, w_out, x, target, fnw)


ELEMWISE_BLOCK_BYTES = 1 << 20


def _row_block(rows, cols, itemsize=4):
    best = None
    for tr in range(16, rows + 1, 16):
        if rows % tr == 0 and tr * cols * itemsize <= ELEMWISE_BLOCK_BYTES:
            best = tr
    return best if best is not None else rows


def _adamw(w, g, m, v, name):
    rows, cols = w.shape
    tr = _row_block(rows, cols)
    if rows // tr > 64 and cols % LANES == 0:
        blk, grid = pl.BlockSpec((rows, LANES), lambda i: (0, i)), (cols // LANES,)
    else:
        blk, grid = pl.BlockSpec((tr, cols), lambda i: (i, 0)), (rows // tr,)
    out = SDS((rows, cols), F32)
    return pl.pallas_call(
        _adamw_body(), out_shape=(out, out, out), grid=grid, in_specs=[blk] * 4, out_specs=(blk,) * 3,
        compiler_params=_params(("parallel",)), name=name)(w, g, m, v)


def _adamw_body():
    def body(w_ref, g_ref, m_ref, v_ref, d_ref, nm_ref, nv_ref):
        gv = g_ref[...]
        nm = ADAM_B1 * m_ref[...] + (1.0 - ADAM_B1) * gv
        nv = ADAM_B2 * v_ref[...] + (1.0 - ADAM_B2) * jnp.square(gv)
        m_hat = nm / (1.0 - ADAM_B1 ** ADAM_STEP)
        v_hat = nv / (1.0 - ADAM_B2 ** ADAM_STEP)
        d_ref[...] = -ADAM_LR * (m_hat / (jnp.sqrt(v_hat) + ADAM_EPS) + ADAM_WD * w_ref[...])
        nm_ref[...] = nm
        nv_ref[...] = nv

    return body


HBM_SPEC = pl.BlockSpec(memory_space=pl.ANY)


def _position():
    return lax.axis_index("x"), lax.axis_index("y"), lax.axis_index("c")


class _Carry:
    def __init__(self, arrays, out_shapes, sems, start, finish):
        self.arrays, self.out_shapes, self.sems, self.start, self.finish = list(arrays), out_shapes, sems, start, finish

    def sem_shapes(self):
        return [pltpu.SemaphoreType.DMA((k,)) for k in self.sems]


def _gather_carry(shards, by_cols=()):
    n = len(shards)

    def copies(ins, outs, sems):
        send_sems, recv_sems, fsend_sems, frecv_sems = sems
        x, y, c = _position()
        me = 2 * x + y
        peers = [(1 - x, y), (x, 1 - y), (1 - x, 1 - y)]

        def half_of(t, chip, half):
            if t in by_cols:
                c2 = ins[t].shape[1] // 2
                return outs[t].at[chip, :, pl.ds(half * c2, c2)]
            return outs[t].at[chip, half]

        def over_ici(t, p, chip):
            px, py = peers[p]
            if t in by_cols:
                c2 = ins[t].shape[1] // 2
                src = ins[t].at[:, pl.ds(c * c2, c2)]
            else:
                r2 = ins[t].shape[0] // 2
                src = ins[t].at[pl.ds(c * r2, r2), :]
            return pltpu.make_async_remote_copy(
                src_ref=src, dst_ref=half_of(t, chip, c), send_sem=send_sems.at[3 * t + p],
                recv_sem=recv_sems.at[3 * t + p], device_id=(px, py, c), device_id_type=MESH)

        def to_sibling(t, p, half):
            px, py = peers[p]
            slab = half_of(t, 2 * px + py, half)
            return pltpu.make_async_remote_copy(
                src_ref=slab, dst_ref=slab, send_sem=fsend_sems.at[3 * t + p], recv_sem=frecv_sems.at[3 * t + p],
                device_id=(x, y, 1 - c), device_id_type=MESH)

        pairs = [(t, p) for t in range(n) for p in range(3)]
        sends = [over_ici(t, p, me) for t, p in pairs]
        lands = [over_ici(t, p, 2 * peers[p][0] + peers[p][1]) for t, p in pairs]
        passed = [to_sibling(t, p, c) for t, p in pairs]
        from_sibling = [to_sibling(t, p, 1 - c) for t, p in pairs]
        return sends, lands, passed, from_sibling

    def start(ins, outs, sems):
        for cp in copies(ins, outs, sems)[0]:
            cp.start()

    def finish(ins, outs, sems):
        sends, lands, passed, from_sibling = copies(ins, outs, sems)
        for land, fwd in zip(lands, passed):
            land.wait_recv()
            fwd.start()
        for cp in from_sibling:
            cp.wait_recv()
        for cp in sends + passed:
            cp.wait_send()

    shapes = [SDS((N_CHIPS,) + a.shape if t in by_cols else (N_CHIPS, 2, a.shape[0] // 2, a.shape[1]), a.dtype)
              for t, a in enumerate(shards)]
    return _Carry(shards, shapes, [3 * n] * 4, start, finish)


def _scatter_carry(parts):
    def start(ins, outs, sems):
        for cp in _scatter_copies(ins, outs, *sems)[0]:
            cp.start()

    def finish(ins, outs, sems):
        sends, lands = _scatter_copies(ins, outs, *sems)
        for cp in lands:
            cp.wait_recv()
        for cp in sends:
            cp.wait_send()

    return _Carry(parts, [SDS(a.shape, a.dtype) for a in parts], [3 * len(parts)] * 2, start, finish)


def _with_own(gathered, own, chip):
    full = gathered.reshape((N_CHIPS,) + own.shape)
    return lax.dynamic_update_index_in_dim(full, own, chip, 0)


def _exchange_halves(grads):
    n = len(grads)
    slabs = [list(g) if isinstance(g, (list, tuple)) else [g] for g in grads]
    flat = [a for s in slabs for a in s]
    ncp = len(flat)

    def body(*refs):
        ins, outs = refs[:ncp], refs[ncp:ncp + n]
        send_sems, recv_sems = refs[ncp + n:]
        x, y, c = _position()
        cps, k = [], 0
        for t in range(n):
            for j in range(len(slabs[t])):
                if len(slabs[t]) == 1:
                    r2 = ins[k].shape[1] // 2
                    src, dst = ins[k].at[:, pl.ds((1 - c) * r2, r2), :], outs[t]
                else:
                    r2 = ins[k].shape[0] // 2
                    src, dst = ins[k].at[pl.ds((1 - c) * r2, r2), :], outs[t].at[j]
                cps.append(pltpu.make_async_remote_copy(
                    src_ref=src, dst_ref=dst, send_sem=send_sems.at[k], recv_sem=recv_sems.at[k],
                    device_id=(x, y, 1 - c), device_id_type=MESH))
                k += 1
        for cp in cps:
            cp.start()
        for cp in cps:
            cp.wait()

    def landing(s):
        a = s[0]
        return SDS((N_CHIPS, a.shape[-2] // 2, a.shape[-1]), a.dtype)

    return pl.pallas_call(
        body, out_shape=[landing(s) for s in slabs],
        in_specs=[HBM_SPEC] * ncp, out_specs=[HBM_SPEC] * n,
        scratch_shapes=[pltpu.SemaphoreType.DMA((ncp,)), pltpu.SemaphoreType.DMA((ncp,))],
        compiler_params=pltpu.CompilerParams(has_side_effects=True), name="reduce_sibling")(*flat)


def _scatter_copies(ins, outs, send_sems, recv_sems):
    x, y, c = _position()
    me = 2 * x + y
    peers = [(1 - x, y), (x, 1 - y), (1 - x, 1 - y)]

    def remote(t, p, src_slab, dst_slab):
        px, py = peers[p]
        return pltpu.make_async_remote_copy(
            src_ref=ins[t].at[src_slab], dst_ref=outs[t].at[dst_slab], send_sem=send_sems.at[3 * t + p],
            recv_sem=recv_sems.at[3 * t + p], device_id=(px, py, c), device_id_type=MESH)

    n = len(ins)
    sends = [remote(t, p, 2 * peers[p][0] + peers[p][1], me) for t in range(n) for p in range(3)]
    lands = [remote(t, p, me, 2 * peers[p][0] + peers[p][1]) for t in range(n) for p in range(3)]
    return sends, lands


def _share_halves(halves):
    n = len(halves)

    def body(*refs):
        ins, outs = refs[:n], refs[n:2 * n]
        send_sems, recv_sems = refs[2 * n:]
        x, y, c = _position()

        def copy(t, slab):
            return pltpu.make_async_remote_copy(
                src_ref=ins[t].at[slab], dst_ref=outs[t].at[slab], send_sem=send_sems.at[t], recv_sem=recv_sems.at[t],
                device_id=(x, y, 1 - c), device_id_type=MESH)

        for t in range(n):
            copy(t, c).start()
        for t in range(n):
            copy(t, 1 - c).wait_recv()
        for t in range(n):
            copy(t, c).wait_send()

    return pl.pallas_call(
        body, out_shape=[SDS(a.shape, a.dtype) for a in halves],
        in_specs=[HBM_SPEC] * n, out_specs=[HBM_SPEC] * n,
        scratch_shapes=[pltpu.SemaphoreType.DMA((n,)), pltpu.SemaphoreType.DMA((n,))],
        input_output_aliases={t: t for t in range(n)},
        compiler_params=pltpu.CompilerParams(has_side_effects=True), name="share_sibling")(*halves)


def _add_sibling(grad, recv, core):
    nch, r2, cols = recv.shape
    tr = _row_block(r2, cols)
    nb = r2 // tr

    def body(c_ref, g_ref, r_ref, o_ref):
        del c_ref
        o_ref[...] = (g_ref[...].astype(F32) + r_ref[...].astype(F32)).astype(BF16)

    return pl.pallas_call(
        body, out_shape=SDS(recv.shape, BF16),
        grid_spec=pltpu.PrefetchScalarGridSpec(
            num_scalar_prefetch=1, grid=(nch, nb),
            in_specs=[pl.BlockSpec((None, tr, cols), lambda j, i, c_ref: (j, c_ref[0] * nb + i, 0)),
                      pl.BlockSpec((None, tr, cols), lambda j, i, c_ref: (j, i, 0))],
            out_specs=pl.BlockSpec((None, tr, cols), lambda j, i, c_ref: (j, i, 0))),
        compiler_params=_params(("parallel", "parallel")), name="add_sibling")(core, grad, recv)


def _add_chips(own, recv, chip_core):
    nch, r2, cols = recv.shape
    tr = _row_block(r2, cols)

    nsc = 2 + nch

    def body(*refs):
        me = refs[0][0]
        own_ref, p_refs, o_ref = refs[nsc], refs[nsc + 1:nsc + 1 + nch], refs[nsc + 1 + nch]
        acc = None
        for j in range(nch):
            term = jnp.where(me == j, own_ref[...], p_refs[j][...]).astype(F32)
            acc = term if acc is None else acc + term
        o_ref[...] = acc

    def slab(j):
        return pl.BlockSpec((None, tr, cols), lambda i, *sc: (sc[2 + j][0], i, 0))

    return pl.pallas_call(
        body, out_shape=SDS((2, r2, cols), F32),
        grid_spec=pltpu.PrefetchScalarGridSpec(
            num_scalar_prefetch=nsc, grid=(r2 // tr,),
            in_specs=[pl.BlockSpec((None, tr, cols), lambda i, *sc: (sc[0][0], i, 0))] + [slab(j) for j in range(nch)],
            out_specs=pl.BlockSpec((None, tr, cols), lambda i, *sc: (sc[1][0], i, 0))),
        compiler_params=_params(("parallel",)), name="add_chips")(*chip_core, own, *([recv] * nch))


def _col_halves_carry(grad):
    nch, r, cols = grad.shape
    c2 = cols // 2

    def copy(ins, outs, sems):
        x, y, c = _position()
        return pltpu.make_async_remote_copy(
            src_ref=ins[0].at[:, :, pl.ds((1 - c) * c2, c2)], dst_ref=outs[0], send_sem=sems[0].at[0],
            recv_sem=sems[1].at[0], device_id=(x, y, 1 - c), device_id_type=MESH)

    return _Carry([grad], [SDS((nch, r, c2), grad.dtype)], [1, 1],
                  lambda ins, outs, sems: copy(ins, outs, sems).start(),
                  lambda ins, outs, sems: copy(ins, outs, sems).wait())


def _add_sibling_cols(grad, recv, core):
    nch, r, c2 = recv.shape
    nb = c2 // LANES

    def body(c_ref, g_ref, r_ref, o_ref):
        del c_ref
        o_ref[...] = (g_ref[...].astype(F32) + r_ref[...].astype(F32)).astype(BF16)

    blk = pl.BlockSpec((None, r, LANES), lambda j, i, c_ref: (j, 0, i))
    return pl.pallas_call(
        body, out_shape=SDS(recv.shape, BF16),
        grid_spec=pltpu.PrefetchScalarGridSpec(
            num_scalar_prefetch=1, grid=(nch, nb),
            in_specs=[pl.BlockSpec((None, r, LANES), lambda j, i, c_ref: (j, 0, c_ref[0] * nb + i)), blk],
            out_specs=blk),
        compiler_params=_params(("parallel", "parallel")), name="add_sibling_cols")(core, grad, recv)


def _add_chips_cols(own, recv, chip_core):
    nch, r, c2 = recv.shape
    nb = c2 // LANES
    nsc = 2 + nch

    def body(*refs):
        me = refs[0][0]
        own_ref, p_refs, o_ref = refs[nsc], refs[nsc + 1:nsc + 1 + nch], refs[nsc + 1 + nch]
        acc = None
        for j in range(nch):
            term = jnp.where(me == j, own_ref[...], p_refs[j][...]).astype(F32)
            acc = term if acc is None else acc + term
        o_ref[...] = acc

    def slab(j):
        return pl.BlockSpec((None, r, LANES), lambda i, *sc: (sc[2 + j][0], 0, i))

    return pl.pallas_call(
        body, out_shape=SDS((r, 2 * c2), F32),
        grid_spec=pltpu.PrefetchScalarGridSpec(
            num_scalar_prefetch=nsc, grid=(nb,),
            in_specs=[pl.BlockSpec((None, r, LANES), lambda i, *sc: (sc[0][0], 0, i))] + [slab(j) for j in range(nch)],
            out_specs=pl.BlockSpec((r, LANES), lambda i, *sc: (0, sc[1][0] * nb + i))),
        compiler_params=_params(("parallel",)), name="add_chips_cols")(*chip_core, own, *([recv] * nch))


def _share_col_halves(full):
    r, cols = full.shape
    c2 = cols // 2

    def body(in_ref, out_ref, send_sem, recv_sem):
        x, y, c = _position()

        def copy(half):
            return pltpu.make_async_remote_copy(
                src_ref=in_ref.at[:, pl.ds(half * c2, c2)], dst_ref=out_ref.at[:, pl.ds(half * c2, c2)],
                send_sem=send_sem.at[0], recv_sem=recv_sem.at[0], device_id=(x, y, 1 - c), device_id_type=MESH)

        copy(c).start()
        copy(1 - c).wait_recv()
        copy(c).wait_send()

    return pl.pallas_call(
        body, out_shape=SDS(full.shape, full.dtype), in_specs=[HBM_SPEC], out_specs=HBM_SPEC,
        scratch_shapes=[pltpu.SemaphoreType.DMA((1,)), pltpu.SemaphoreType.DMA((1,))],
        input_output_aliases={0: 0},
        compiler_params=pltpu.CompilerParams(has_side_effects=True), name="share_sibling_cols")(full)


def _allreduce_small(pack):
    rows = pack.shape[0]

    def body(p_ref, o_ref, buf, send_sems, recv_sems):
        x, y, c = _position()
        me = 4 * x + 2 * y + c
        buf[me] = p_ref[...]

        def copy(dst_dev, slot):
            return pltpu.make_async_remote_copy(
                src_ref=p_ref, dst_ref=buf.at[slot], send_sem=send_sems.at[dst_dev], recv_sem=recv_sems.at[slot],
                device_id=(dst_dev // 4, (dst_dev // 2) % 2, dst_dev % 2), device_id_type=MESH)

        for dev in range(N_DEV):
            @pl.when(dev != me)
            def _():
                copy(dev, me).start()
        for dev in range(N_DEV):
            @pl.when(dev != me)
            def _():
                copy(dev, dev).wait_recv()
        for dev in range(N_DEV):
            @pl.when(dev != me)
            def _():
                copy(dev, me).wait_send()
        acc = buf[0]
        for dev in range(1, N_DEV):
            acc = acc + buf[dev]
        o_ref[...] = acc

    return pl.pallas_call(
        body, out_shape=SDS(pack.shape, F32),
        in_specs=[pl.BlockSpec(memory_space=pltpu.VMEM)], out_specs=pl.BlockSpec(memory_space=pltpu.VMEM),
        scratch_shapes=[pltpu.VMEM((N_DEV, rows, LANES), F32), pltpu.SemaphoreType.DMA((N_DEV,)),
                        pltpu.SemaphoreType.DMA((N_DEV,))],
        compiler_params=pltpu.CompilerParams(has_side_effects=True), name="allreduce_small")(pack)


ATTN_TQ = 256


def _local_step(cfg, x, target, w, to_chips=None, late=None, hn=None):
    d = cfg.D
    if hn is None:
        hn = _rmsnorm_fwd(x, w["norm_w"])
    proj = _mm(hn, w["w_main_t"], "nt", BF16, "proj_main", carry=late[0] if late else None, b_rows=cfg.NM)
    if late:
        proj, arrived = proj
        w = {**w, **late[1](arrived)}
    dt_raw = _mm(hn, w["w_dt_t"], "nt", F32, "proj_dt")
    slopes = _slopes(cfg.H)
    near = _Pass(ATTN_TQ, DILATED_PATTERNS[:-1], 1, cfg.S)
    far = _Pass(LANES, DILATED_PATTERNS[-1:], DEINT, cfg.S // DEINT)
    tab_near, tab_far = _attn_tables(near), _attn_tables(far)
    cols_near, cols_far = (cfg.OQ, cfg.OK, cfg.OV), (0, d, 2 * d)
    qkv_far = _deinterleave(proj, 0, 3 * d, "attn_deinterleave")
    o_1, lse_1 = _attn_fwd(cfg, near, proj, cols_near, tab_near, slopes, "attn_fwd_near")
    o_2, lse_2 = _attn_fwd(cfg, far, qkv_far, cols_far, tab_far, slopes, "attn_fwd_far")
    o_a, oag, lse = _attn_merge(cfg, proj, o_1, lse_1, o_2, lse_2)
    xact = _conv_fwd(cfg, proj, w["conv_w"], w["conv_b"])
    e = _expansion_matrix(cfg)
    y, y2, states = _ssd_fwd(cfg, xact, dt_raw, proj, w["dt_bias"], w["a_log"], w["d_skip"], w["ssm_norm_w"], e)
    a_br = _mm(oag, w["w_attn"], "nn", BF16, "branch_attn")
    s_br = _mm(y2, w["w_ssm"], "nn", BF16, "branch_ssm")
    merged = _merge_fwd(cfg, proj, a_br, s_br)
    dout_f, dout_b, loss_row, g_fnw = _outproj_loss(merged, w["w_out"], x, target, w["final_norm_w"])

    dmerged = _mm(dout_b, w["w_out"], "nt", BF16, "d_merged")
    dproj, da_br = _merge_bwd(cfg, proj, a_br, dmerged, cfg.OGA, None, "merge_bwd_attn")
    dproj, ds_br = _merge_bwd(cfg, proj, s_br, dmerged, cfg.OGS, dproj, "merge_bwd_ssm")
    doag = _mm(da_br, w["w_attn"], "nt", BF16, "d_oag")
    dy2 = _mm(ds_br, w["w_ssm"], "nt", BF16, "d_y2")
    dproj, dxact, ddt, g_snw, g_dtb, g_alog, g_dsk = _ssd_bwd(
        cfg, xact, dt_raw, proj, y, dy2, states, w["dt_bias"], w["a_log"], w["d_skip"], w["ssm_norm_w"], e, dproj)
    dproj, g_cw, g_cb = _conv_bwd(cfg, proj, dxact, w["conv_w"], w["conv_b"], dproj)
    dproj, do, do_far, dl, dl_far, lse_far = _attn_bwd_prep(cfg, proj, o_a, doag, lse, dproj)
    g_near = _attn_bwd(cfg, near, proj, cols_near, do, lse, dl, tab_near, slopes, "attn_bwd_near")
    g_far = _attn_bwd(cfg, far, qkv_far, cols_far, do_far, lse_far, dl_far, tab_far, slopes, "attn_bwd_far")
    for g_1, g_2, col0, nm in zip(g_near, g_far, cols_near, ("attn_dq", "attn_dk", "attn_dv")):
        dproj = _attn_grad_sum(cfg, g_1, g_2, col0, dproj, nm)
    ddt_b = ddt.astype(BF16)
    g_w_main = _mm(dproj, hn, "tn", BF16, "g_w_main", out_rows=cfg.N_IN)
    g_w_dt = _mm(ddt_b, hn, "tn", BF16, "g_w_dt")
    grads = dict(w_main_t=g_w_main, w_dt_t=g_w_dt, conv_w=g_cw, conv_b=g_cb, dt_bias=g_dtb, a_log=g_alog,
                 d_skip=g_dsk, ssm_norm_w=g_snw, final_norm_w=g_fnw)
    riding = to_chips[0](grads) if to_chips is not None else None
    g_w_ssm = _mm(y2, ds_br, "tn", BF16, "g_w_ssm", carry=riding[1] if riding else None)
    if riding:
        g_w_ssm, from_sibling_in = g_w_ssm
    grads.update(w_ssm=g_w_ssm, w_out=_mm(merged, dout_b, "tn", BF16, "g_w_out"),
                 w_attn=_mm(oag, da_br, "tn", BF16, "g_w_attn"))
    sent = to_chips[1](grads, riding[0], from_sibling_in[0]) if to_chips is not None else ()
    dhn = _mm(dproj, w["w_main_t"], "nn", F32, "d_hn", tk=1024, carry=_scatter_carry(sent) if sent else None,
              b_rows=cfg.NM)
    landed = ()
    if sent:
        dhn, landed = dhn
    dhn_dt = _mm(ddt_b, w["w_dt_t"], "nn", F32, "d_hn_dt")
    grad_x, grads["norm_w"] = _rmsnorm_bwd(x, w["norm_w"], dhn, dhn_dt, dout_f)
    return loss_row, grad_x, grads, sent, landed


def _pad_lanes(v):
    return jnp.pad(v, ((0, 0), (0, LANES - v.shape[1])))


def _main_from_rows(cfg, w_in_t):
    lo, hi = cfg.OGA, cfg.OGA + cfg.NH
    dt = jnp.pad(w_in_t[lo:hi], ((0, LANES - cfg.NH), (0, 0)))
    return lax.dynamic_update_slice(w_in_t, w_in_t[hi:], (lo, 0)), dt


def _rows_from_main(cfg, g_main_t, g_dt_t):
    lo, hi = cfg.OGA, cfg.OGA + cfg.NH
    g = lax.dynamic_update_slice(g_main_t, g_main_t[lo:cfg.NM], (hi, 0))
    return lax.dynamic_update_slice(g, g_dt_t[:cfg.NH], (lo, 0))


def _full_weights(cfg, norm_w, w_in_t, conv_w, conv_b, dt_bias, a_log, d_skip, ssm_norm_w, w_attn, w_ssm, w_out, fnw):
    w_main, w_dt = _main_from_rows(cfg, w_in_t)
    return dict(norm_w=norm_w, w_main_t=w_main.astype(BF16), w_dt_t=w_dt.astype(BF16), conv_w=conv_w, conv_b=conv_b,
                dt_bias=_pad_lanes(dt_bias), a_log=_pad_lanes(a_log), d_skip=_pad_lanes(d_skip), ssm_norm_w=ssm_norm_w,
                final_norm_w=fnw, **{k: v.astype(BF16) for k, v in (("w_attn", w_attn), ("w_ssm", w_ssm), ("w_out", w_out))
                                     if v is not None})


def kernel(x, norm_w, w_in, conv_w, conv_b, dt_bias, a_log, d_skip, ssm_norm_w, w_attn_branch, w_ssm_branch, w_out, final_norm_w, loss_target, m_norm_w, m_w_in, m_conv_w, m_conv_b, m_dt_bias, m_a_log, m_d_skip, m_ssm_norm_w, m_w_attn_branch, m_w_ssm_branch, m_w_out, m_final_norm_w, v_norm_w, v_w_in, v_conv_w, v_conv_b, v_dt_bias, v_a_log, v_d_skip, v_ssm_norm_w, v_w_attn_branch, v_w_ssm_branch, v_w_out, v_final_norm_w):
    cfg = _Cfg(x.shape[1], x.shape[2])
    d, si, cd, nh = cfg.D, cfg.SI, cfg.CD, cfg.NH
    chip = 2 * lax.axis_index("x") + lax.axis_index("y")
    core = lax.axis_index("c").astype(jnp.int32).reshape(1)
    chip = chip.astype(jnp.int32)
    chip_core = [chip.reshape(1), core] + [jnp.where(chip == j, (j + 1) % N_CHIPS, j).astype(jnp.int32).reshape(1)
                                           for j in range(N_CHIPS)]

    own = [jnp.transpose(w_in[0]).astype(BF16), conv_w[0].reshape(4 * CONV_K, -1)]
    hn, gathered = _rmsnorm_fwd(x[0], norm_w, carry=_gather_carry(own, by_cols=(0,)))
    a_in, a_cw = [_with_own(g, o, chip) for g, o in zip(gathered, own)]
    conv_w_full = a_cw.reshape(N_CHIPS, CONV_K, cd // N_CHIPS).transpose(1, 0, 2).reshape(CONV_K, cd)
    w = _full_weights(cfg, norm_w, a_in.reshape(cfg.N_IN, d), conv_w_full, conv_b, dt_bias, a_log, d_skip,
                      ssm_norm_w, None, None, None, final_norm_w.reshape(1, d))
    own_late = [w_attn_branch[0].astype(BF16), w_ssm_branch[0].astype(BF16), w_out[0].astype(BF16)]

    def late_weights(arrived):
        a_attn, a_ssm, a_out = [_with_own(g, o, chip) for g, o in zip(arrived, own_late)]
        return dict(w_attn=a_attn.reshape(d, d), w_ssm=a_ssm.reshape(si, d), w_out=a_out.reshape(d, d))

    def w_in_to_sibling(grads):
        g_in_t = _rows_from_main(cfg, grads["w_main_t"], grads["w_dt_t"]).reshape(N_CHIPS, cfg.N_IN // N_CHIPS, d)
        return g_in_t, _col_halves_carry(g_in_t)

    def to_chips(grads, g_in_t, from_sibling_in):
        by_chip = [grads["w_attn"].reshape(N_CHIPS, d // N_CHIPS, d),
                   grads["w_ssm"].reshape(N_CHIPS, si // N_CHIPS, d),
                   grads["w_out"].reshape(N_CHIPS, d // N_CHIPS, d)]
        from_sibling = _exchange_halves(by_chip)
        return ([_add_sibling_cols(g_in_t, from_sibling_in, core)]
                + [_add_sibling(g, r, core) for g, r in zip(by_chip, from_sibling)])

    loss_row, grad_x, grads, chip_sums, from_chips = _local_step(
        cfg, x[0], loss_target[0], w, (w_in_to_sibling, to_chips), (_gather_carry(own_late), late_weights), hn)
    g_in_t = _share_col_halves(_add_chips_cols(chip_sums[0], from_chips[0], chip_core))
    halves = [_add_chips(o, p, chip_core) for o, p in zip(chip_sums[1:], from_chips[1:])]
    g_attn, g_ssm, g_out = [h.reshape(2 * h.shape[1], h.shape[2]) for h in _share_halves(halves)]
    g_in = jnp.transpose(g_in_t)

    small = [loss_row, grads["norm_w"], grads["conv_b"], grads["dt_bias"], grads["a_log"], grads["d_skip"],
             grads["ssm_norm_w"], grads["final_norm_w"], grads["conv_w"].reshape(1, CONV_K * cd)]
    sizes = [a.shape[1] for a in small]
    total = sum(sizes)
    rows = -(-total // (8 * LANES)) * 8
    flat = jnp.pad(jnp.concatenate(small, axis=1), ((0, 0), (0, rows * LANES - total)))
    red = _allreduce_small(flat.reshape(rows, LANES)).reshape(1, rows * LANES)
    offs = [sum(sizes[:i]) for i in range(len(sizes))]
    loss_r, g_nw, g_cb, g_dtb, g_alog, g_dsk, g_snw, g_fnw, g_cw_flat = [
        red[:, o:o + n] for o, n in zip(offs, sizes)]
    loss = loss_r[0, 0]
    g_dtb, g_alog, g_dsk = g_dtb[:, :nh], g_alog[:, :nh], g_dsk[:, :nh]
    cshard = cd // N_CHIPS
    g_cw = lax.dynamic_slice_in_dim(g_cw_flat.reshape(CONV_K, cd), chip * cshard, cshard, axis=1)

    upd = {}
    upd["w_in"] = tuple(jnp.transpose(u) for u in _adamw(
        jnp.transpose(w_in[0]), g_in_t, jnp.transpose(m_w_in[0]), jnp.transpose(v_w_in[0]), "adamw_w_in"))
    for name, wv, gv, mv, vv in [("w_attn", w_attn_branch[0], g_attn, m_w_attn_branch[0], v_w_attn_branch[0]),
                                 ("w_ssm", w_ssm_branch[0], g_ssm, m_w_ssm_branch[0], v_w_ssm_branch[0]),
                                 ("w_out", w_out[0], g_out, m_w_out[0], v_w_out[0])]:
        upd[name] = _adamw(wv, gv, mv, vv, "adamw_" + name)
    names = ["norm_w", "conv_w", "conv_b", "dt_bias", "a_log", "d_skip", "ssm_norm_w", "final_norm_w"]
    ws = [norm_w, conv_w[0].reshape(1, -1), conv_b, dt_bias, a_log, d_skip, ssm_norm_w, final_norm_w.reshape(1, d)]
    gs = [g_nw, g_cw.reshape(1, -1), g_cb, g_dtb, g_alog, g_dsk, g_snw, g_fnw]
    ms = [m_norm_w, m_conv_w[0].reshape(1, -1), m_conv_b, m_dt_bias, m_a_log, m_d_skip, m_ssm_norm_w,
          m_final_norm_w.reshape(1, d)]
    vs = [v_norm_w, v_conv_w[0].reshape(1, -1), v_conv_b, v_dt_bias, v_a_log, v_d_skip, v_ssm_norm_w,
          v_final_norm_w.reshape(1, d)]
    ssz = [a.shape[1] for a in ws]
    stot = sum(ssz)
    srows = -(-stot // (8 * LANES)) * 8

    def pack(parts):
        return jnp.pad(jnp.concatenate(parts, axis=1), ((0, 0), (0, srows * LANES - stot))).reshape(srows, LANES)

    packed = _adamw(pack(ws), pack(gs), pack(ms), pack(vs), "adamw_small")
    soffs = [sum(ssz[:i]) for i in range(len(ssz))]
    for k, nm in enumerate(names):
        upd[nm] = tuple(p.reshape(1, srows * LANES)[:, soffs[k]:soffs[k] + ssz[k]] for p in packed)

    shapes = dict(norm_w=norm_w.shape, w_in=w_in.shape, conv_w=conv_w.shape, conv_b=conv_b.shape, dt_bias=dt_bias.shape,
                  a_log=a_log.shape, d_skip=d_skip.shape, ssm_norm_w=ssm_norm_w.shape, w_attn=w_attn_branch.shape,
                  w_ssm=w_ssm_branch.shape, w_out=w_out.shape, final_norm_w=final_norm_w.shape)
    order = ["norm_w", "w_in", "conv_w", "conv_b", "dt_bias", "a_log", "d_skip", "ssm_norm_w", "w_attn", "w_ssm",
             "w_out", "final_norm_w"]
    gradv = dict(norm_w=g_nw, w_in=g_in, conv_w=g_cw, conv_b=g_cb, dt_bias=g_dtb, a_log=g_alog, d_skip=g_dsk,
                 ssm_norm_w=g_snw, w_attn=g_attn, w_ssm=g_ssm, w_out=g_out, final_norm_w=g_fnw)
    outs = [loss, grad_x[None]]
    outs += [gradv[n].reshape(shapes[n]) for n in order]
    for k in range(3):
        outs += [upd[n][k].reshape(shapes[n]) for n in order]
    return tuple(outs)
```

```python
import jax
import jax.numpy as jnp
from jax import lax
from jax.experimental import pallas as pl
from jax.experimental.pallas import tpu as pltpu

F32 = jnp.float32
BF16 = jnp.bfloat16
SDS = jax.ShapeDtypeStruct

RMS_EPS = 1e-6
LANES = 128
CHUNK = 128
SSM_HEAD_DIM = 64
SSM_GROUPS = 8
SSM_STATE = 128
CONV_K = 4
ATTN_HEAD_DIM = 128
DILATED_PATTERNS = ((128, 1), (512, 4), (2048, 16))
NEG = -1e30
VMEM_LIMIT = 56 * 1024 * 1024
ADAM_LR, ADAM_B1, ADAM_B2, ADAM_EPS, ADAM_WD, ADAM_STEP = 0.001, 0.9, 0.999, 1e-08, 0.01, 10
MESH = pl.DeviceIdType.MESH
N_CHIPS = 4
N_DEV = 8


class _Cfg:
    def __init__(self, s, d):
        self.S, self.D = s, d
        self.H = d // ATTN_HEAD_DIM
        self.SI = 2 * d
        self.NH = self.SI // SSM_HEAD_DIM
        self.HPG = self.NH // SSM_GROUPS
        self.GW = self.HPG * SSM_HEAD_DIM
        self.BC = SSM_GROUPS * SSM_STATE
        self.CD = self.SI + 2 * self.BC
        self.OQ, self.OK, self.OV, self.OZA = 0, d, 2 * d, 3 * d
        self.OZS = 4 * d
        self.OXBC = self.OZS + self.SI
        self.OGA = self.OXBC + self.CD
        self.OGS = self.OGA + d
        self.NM = self.OGS + d
        self.N_IN = self.NM + self.NH
        assert self.GW % LANES == 0 and self.NH <= LANES and s % 512 == 0 and d % 512 == 0


def _params(sem=None):
    return pltpu.CompilerParams(dimension_semantics=sem, vmem_limit_bytes=VMEM_LIMIT)


def _sigmoid(x):
    return 0.5 * jnp.tanh(0.5 * x) + 0.5


def _softplus(x):
    u = jnp.exp(-jnp.abs(x))
    l1p = jnp.where(u < 1e-3, u * (1.0 - u * (0.5 - u * (1.0 / 3.0))), jnp.log(1.0 + u))
    return jnp.maximum(x, 0.0) + l1p


def _nt(a, b):
    return lax.dot_general(a, b, (((1,), (1,)), ((), ())), preferred_element_type=F32)


def _tn(a, b):
    return lax.dot_general(a, b, (((0,), (0,)), ((), ())), preferred_element_type=F32)


def _nn(a, b):
    return jnp.dot(a, b, preferred_element_type=F32)


def _tile(n, target):
    if n <= target:
        return n
    best = None
    for t in range(LANES, target + 1, LANES):
        if n % t == 0:
            best = t
    assert best is not None, (n, target)
    return best


MM_TK = {"nn": 2048, "nt": 2048, "tn": 1024}


def _mm(a, b, dims, out_dtype, name, tm=1024, tn=2048, tk=None, init=None, carry=None, b_rows=None, out_rows=None):
    tk = MM_TK[dims] if tk is None else tk
    if dims == "nn":
        (m, k), (k2, n) = a.shape, b.shape
        k2 = k2 if b_rows is None else b_rows
    elif dims == "nt":
        (m, k), (n, k2) = a.shape, b.shape
        n = n if b_rows is None else b_rows
    else:
        (k, m), (k2, n) = a.shape, b.shape
    assert k == k2
    tm, tn, tk = _tile(m, tm), _tile(n, tn), _tile(k, tk)
    nk = k // tk
    if dims == "tn":
        a_spec = pl.BlockSpec((tk, tm), lambda i, j, kk: (kk, i))
    else:
        a_spec = pl.BlockSpec((tm, tk), lambda i, j, kk: (i, kk))
    if dims == "nt":
        b_spec = pl.BlockSpec((tn, tk), lambda i, j, kk: (j, kk))
    else:
        b_spec = pl.BlockSpec((tk, tn), lambda i, j, kk: (kk, j))
    o_spec = pl.BlockSpec((tm, tn), lambda i, j, kk: (i, j))
    op = {"nn": _nn, "nt": _nt, "tn": _tn}[dims]
    has_init = init is not None
    nx = len(carry.arrays) if carry is not None else 0
    ni, nj = m // tm, n // tn

    def body(*refs):
        a_ref, b_ref = refs[0], refs[1]
        i_ref = refs[2] if has_init else None
        x_in = refs[2 + has_init:2 + has_init + nx]
        o_ref = refs[2 + has_init + nx]
        x_out = refs[3 + has_init + nx:3 + has_init + 2 * nx]
        acc = refs[3 + has_init + 2 * nx]
        x_sems = refs[4 + has_init + 2 * nx:]
        i, j, kk = pl.program_id(0), pl.program_id(1), pl.program_id(2)

        if nx:
            @pl.when((i == 0) & (j == 0) & (kk == 0))
            def _():
                carry.start(x_in, x_out, x_sems)

        prod = lambda: op(a_ref[...], b_ref[...])
        with_init = (lambda p: p + i_ref[...].astype(F32)) if has_init else (lambda p: p)
        if nk == 1:
            o_ref[...] = with_init(prod()).astype(out_dtype)
        else:
            @pl.when(kk == 0)
            def _():
                acc[...] = with_init(prod())

            @pl.when((kk > 0) & (kk < nk - 1))
            def _():
                acc[...] += prod()

            @pl.when(kk == nk - 1)
            def _():
                o_ref[...] = (acc[...] + prod()).astype(out_dtype)

        if nx:
            @pl.when((i == ni - 1) & (j == nj - 1) & (kk == nk - 1))
            def _():
                carry.finish(x_in, x_out, x_sems)

    in_specs = [a_spec, b_spec] + ([o_spec] if has_init else []) + [HBM_SPEC] * nx
    args = (a, b) + ((init,) if has_init else ()) + (tuple(carry.arrays) if nx else ())
    sems = carry.sem_shapes() if nx else []
    outs = pl.pallas_call(
        body, out_shape=[SDS((m if out_rows is None else out_rows, n), out_dtype)] + (carry.out_shapes if nx else []),
        grid=(ni, nj, nk),
        in_specs=in_specs, out_specs=[o_spec] + [HBM_SPEC] * nx,
        scratch_shapes=[pltpu.VMEM((tm, tn) if nk > 1 else (8, LANES), F32)] + sems,
        compiler_params=_params(("arbitrary",) * 3 if nx else ("parallel", "parallel", "arbitrary")), name=name)(*args)
    return (outs[0], outs[1:]) if nx else outs[0]


def _rmsnorm_fwd(x, w, carry=None):
    s, d = x.shape
    tr = 256
    nsteps = s // tr
    nx = len(carry.arrays) if carry is not None else 0

    def body(*refs):
        x_ref, w_ref, x_in = refs[0], refs[1], refs[2:2 + nx]
        o_ref, x_out, x_sems = refs[2 + nx], refs[3 + nx:3 + 2 * nx], refs[3 + 2 * nx:]
        if nx:
            @pl.when(pl.program_id(0) == 0)
            def _():
                carry.start(x_in, x_out, x_sems)

        xv = x_ref[...]
        r = lax.rsqrt(jnp.mean(xv * xv, axis=-1, keepdims=True) + RMS_EPS)
        o_ref[...] = (xv * r * w_ref[...]).astype(BF16)

        if nx:
            @pl.when(pl.program_id(0) == nsteps - 1)
            def _():
                carry.finish(x_in, x_out, x_sems)

    outs = pl.pallas_call(
        body, out_shape=[SDS((s, d), BF16)] + (carry.out_shapes if nx else []), grid=(nsteps,),
        in_specs=[pl.BlockSpec((tr, d), lambda i: (i, 0)), pl.BlockSpec((1, d), lambda i: (0, 0))] + [HBM_SPEC] * nx,
        out_specs=[pl.BlockSpec((tr, d), lambda i: (i, 0))] + [HBM_SPEC] * nx,
        scratch_shapes=carry.sem_shapes() if nx else [],
        compiler_params=_params(("arbitrary",) if nx else ("parallel",)), name="rmsnorm_fwd")(
            x, w, *(carry.arrays if nx else []))
    return (outs[0], outs[1:]) if nx else outs[0]


def _rmsnorm_bwd(x, w, dhn_a, dhn_b, dout):
    s, d = x.shape
    tr = 256

    def body(x_ref, w_ref, dh_ref, dh2_ref, do_ref, gx_ref, gw_ref):
        xv = x_ref[...]
        r = lax.rsqrt(jnp.mean(xv * xv, axis=-1, keepdims=True) + RMS_EPS)
        nrm = xv * r
        dh = dh_ref[...] + dh2_ref[...]
        gy = dh * w_ref[...]
        gx_ref[...] = do_ref[...] + r * (gy - nrm * jnp.mean(gy * nrm, axis=-1, keepdims=True))

        @pl.when(pl.program_id(0) == 0)
        def _():
            gw_ref[...] = jnp.zeros_like(gw_ref)

        gw_ref[...] += jnp.sum(dh * nrm, axis=0, keepdims=True)

    blk = pl.BlockSpec((tr, d), lambda i: (i, 0))
    row = pl.BlockSpec((1, d), lambda i: (0, 0))
    return pl.pallas_call(
        body, out_shape=(SDS((s, d), F32), SDS((1, d), F32)), grid=(s // tr,),
        in_specs=[blk, row, blk, blk, blk], out_specs=(blk, row),
        compiler_params=_params(("arbitrary",)), name="rmsnorm_bwd")(x, w, dhn_a, dhn_b, dout)


DEINT = DILATED_PATTERNS[-1][1]
DEINT_ROWS = DEINT * LANES


class _Pass:
    def __init__(self, tq, patterns, unit, seg_len):
        self.tq, self.patterns, self.unit, self.seg_len = tq, patterns, unit, seg_len
        self.win = max(w for w, _ in patterns) // unit
        self.w = self.win + tq
        assert self.win % tq == 0


def _attn_tables(ps):
    i = jnp.arange(ps.tq, dtype=jnp.int32)[:, None]
    j = jnp.arange(ps.w, dtype=jnp.int32)[None, :]
    delta = (i + ps.win - j) * ps.unit
    n = jnp.zeros((ps.tq, ps.w), F32)
    for window, dil in ps.patterns:
        n = n + ((delta >= 0) & (delta <= window) & (delta % dil == 0)).astype(F32)
    logn = jnp.where(n > 0, jnp.log(jnp.maximum(n, 1.0)), NEG)
    return logn, jnp.maximum(delta, 0).astype(F32)


def _slopes(h):
    s = jnp.asarray([2.0 ** (-8.0 * (i + 1) / h) for i in range(h)], F32)
    return jnp.broadcast_to(s[:, None, None], (h, 1, LANES))


def _masked_logn(ps, logn_ref, start):
    col = lax.broadcasted_iota(jnp.int32, (ps.tq, ps.w), 1)
    return jnp.where(col >= ps.win - lax.rem(start, ps.seg_len), logn_ref[...], NEG)


def _head_cols(hh):
    return slice(hh * ATTN_HEAD_DIM, (hh + 1) * ATTN_HEAD_DIM)


def _head_window(refs, cs):
    return jnp.concatenate([r[:, cs] for r in refs], axis=0)


def _head_scores(q_ref, kw, cs, base, dist_ref, slope_ref, hh):
    return _nt(q_ref[:, cs], kw) * (ATTN_HEAD_DIM ** -0.5) + (base - slope_ref[hh][0:1, 0:1] * dist_ref[...])


def _lane_of(stat, hh):
    lane = lax.broadcasted_iota(jnp.int32, stat.shape, 1)
    return jnp.sum(jnp.where(lane == hh, stat, 0.0), axis=1, keepdims=True)


def _window_specs(ps, d, col, nb):
    nprev = ps.win // ps.tq
    return [pl.BlockSpec((ps.tq, d), lambda i, b=b: (jnp.maximum(jnp.minimum(i, nb - 1) - (nprev - b), 0), col))
            for b in range(nprev + 1)]


def _attn_fwd(cfg, ps, qkv, cols, tables, slopes, name):
    s, h, d = cfg.S, cfg.H, cfg.D
    tq, nw = ps.tq, ps.win // ps.tq + 1
    nb = s // tq
    logn, dist = tables
    qc, kc, vc = [c // d for c in cols]

    def body(*refs):
        q_ref, k_refs, v_refs = refs[0], refs[1:1 + nw], refs[1 + nw:1 + 2 * nw]
        logn_ref, dist_ref, slope_ref, o_ref, lse_ref = refs[1 + 2 * nw:]
        base = _masked_logn(ps, logn_ref, pl.program_id(0) * tq)
        lane = lax.broadcasted_iota(jnp.int32, (tq, LANES), 1)

        lse = jnp.zeros((tq, LANES), F32)
        for hh in range(h):
            cs = _head_cols(hh)
            sc = _head_scores(q_ref, _head_window(k_refs, cs), cs, base, dist_ref, slope_ref, hh)
            m = jnp.max(sc, axis=1, keepdims=True)
            p = jnp.exp(sc - m)
            l = jnp.sum(p, axis=1, keepdims=True)
            o_ref[:, cs] = (_nn(p.astype(BF16), _head_window(v_refs, cs)) / l).astype(BF16)
            lse = jnp.where(lane == hh, m + jnp.log(l), lse)
        lse_ref[...] = lse

    tab = pl.BlockSpec((tq, ps.w), lambda i: (0, 0))
    return pl.pallas_call(
        body, out_shape=(SDS((s, d), BF16), SDS((s, LANES), F32)), grid=(nb,),
        in_specs=[pl.BlockSpec((tq, d), lambda i: (i, qc))] + _window_specs(ps, d, kc, nb) + _window_specs(ps, d, vc, nb)
        + [tab, tab, pl.BlockSpec((h, 1, LANES), lambda i: (0, 0, 0))],
        out_specs=(pl.BlockSpec((tq, d), lambda i: (i, 0)), pl.BlockSpec((tq, LANES), lambda i: (i, 0))),
        compiler_params=_params(("parallel",)), name=name)(*([qkv] * (1 + 2 * nw)), logn, dist, slopes)


def _attn_bwd(cfg, ps, qkv, cols, do, lse, delta, tables, slopes, name):
    s, h, d = cfg.S, cfg.H, cfg.D
    tq, nprev = ps.tq, ps.win // ps.tq
    nw = nprev + 1
    nb = s // tq
    logn, dist = tables
    qc, kc, vc = [c // d for c in cols]
    scale = ATTN_HEAD_DIM ** -0.5

    def body(*refs):
        q_ref, k_refs, v_refs = refs[0], refs[1:1 + nw], refs[1 + nw:1 + 2 * nw]
        do_ref, lse_ref, dl_ref, logn_ref, dist_ref, slope_ref, dq_ref, dk_ref, dv_ref, ck, cv = refs[1 + 2 * nw:]
        i = pl.program_id(0)
        slot = lambda b: lax.rem(i + b, nprev)

        @pl.when(i == 0)
        def _():
            ck[...] = jnp.zeros_like(ck)
            cv[...] = jnp.zeros_like(cv)

        @pl.when(i < nb)
        def _():
            base = _masked_logn(ps, logn_ref, i * tq)
            lse_all, dl_all = lse_ref[...], dl_ref[...]

            for hh in range(h):
                cs = _head_cols(hh)
                kw, vw = _head_window(k_refs, cs), _head_window(v_refs, cs)
                sc = _head_scores(q_ref, kw, cs, base, dist_ref, slope_ref, hh)
                p = jnp.exp(sc - lse_all[:, hh:hh + 1])
                dob = do_ref[:, cs]
                ds = (p * (_nt(dob, vw) - dl_all[:, hh:hh + 1]) * scale).astype(BF16)
                dq_ref[:, cs] = _nn(ds, kw).astype(BF16)
                dkw = _tn(ds, q_ref[:, cs])
                dvw = _tn(p.astype(BF16), dob)
                dk_ref[:, cs] = (ck[slot(0), :, cs] + dkw[0:tq]).astype(BF16)
                dv_ref[:, cs] = (cv[slot(0), :, cs] + dvw[0:tq]).astype(BF16)
                for b in range(1, nprev):
                    ck[slot(b), :, cs] += dkw[b * tq:(b + 1) * tq]
                    cv[slot(b), :, cs] += dvw[b * tq:(b + 1) * tq]
                ck[slot(0), :, cs] = dkw[nprev * tq:]
                cv[slot(0), :, cs] = dvw[nprev * tq:]

        @pl.when(i >= nb)
        def _():
            dk_ref[...] = ck[slot(0)].astype(BF16)
            dv_ref[...] = cv[slot(0)].astype(BF16)

    here = lambda i: jnp.minimum(i, nb - 1)
    blk = pl.BlockSpec((tq, d), lambda i: (here(i), 0))
    stat = pl.BlockSpec((tq, LANES), lambda i: (here(i), 0))
    late = pl.BlockSpec((tq, d), lambda i: (jnp.maximum(i - nprev, 0), 0))
    tab = pl.BlockSpec((tq, ps.w), lambda i: (0, 0))
    return pl.pallas_call(
        body, out_shape=(SDS((s, d), BF16), SDS((s, d), BF16), SDS((s, d), BF16)), grid=(nb + nprev,),
        in_specs=[pl.BlockSpec((tq, d), lambda i: (here(i), qc))] + _window_specs(ps, d, kc, nb)
        + _window_specs(ps, d, vc, nb) + [blk, stat, stat, tab, tab, pl.BlockSpec((h, 1, LANES), lambda i: (0, 0, 0))],
        out_specs=(blk, late, late),
        scratch_shapes=[pltpu.VMEM((nprev, tq, d), F32), pltpu.VMEM((nprev, tq, d), F32)],
        compiler_params=_params(("arbitrary",)), name=name)(
            *([qkv] * (1 + 2 * nw)), do, lse, delta, logn, dist, slopes)


def _by_residue(a):
    return a.reshape(DEINT, a.shape[0] // DEINT, a.shape[1])


def _deint_spec(colblock):
    return pl.BlockSpec((DEINT, LANES, LANES), lambda b, j: (0, b, colblock(j)))


def _deint_rows(scr, out_ref, dtype):
    for r in range(DEINT):
        out_ref[r] = scr[pl.ds(r, LANES, stride=DEINT), :].astype(dtype)


def _int_rows(in_ref, scr):
    for r in range(DEINT):
        scr[pl.ds(r, LANES, stride=DEINT), :] = in_ref[r].astype(F32)


WIDE = 4 * LANES


def _wide_spec():
    return pl.BlockSpec((DEINT, LANES, WIDE), lambda b, j: (0, b, j))


def _deinterleave(x, col0, ncols, name):
    s = x.shape[0]
    c0 = col0 // WIDE

    def body(x_ref, o_ref, scr):
        for t in range(WIDE // LANES):
            cs = slice(t * LANES, (t + 1) * LANES)
            scr[t] = x_ref[:, cs].astype(F32)
            for r in range(DEINT):
                o_ref[r, :, cs] = scr.at[t][pl.ds(r, LANES, stride=DEINT), :].astype(x.dtype)

    out = pl.pallas_call(
        body, out_shape=SDS((DEINT, s // DEINT, ncols), x.dtype), grid=(s // DEINT_ROWS, ncols // WIDE),
        in_specs=[pl.BlockSpec((DEINT_ROWS, WIDE), lambda b, j: (b, c0 + j))],
        out_specs=_wide_spec(),
        scratch_shapes=[pltpu.VMEM((WIDE // LANES, DEINT_ROWS, LANES), F32)],
        compiler_params=_params(("parallel", "parallel")), name=name)(x)
    return out.reshape(s, ncols)


def _attn_merge(cfg, proj, o_1, lse_1, o_2, lse_2):
    s, h = cfg.S, cfg.H
    zb = cfg.OZA // WIDE
    rows = DEINT_ROWS
    hps = WIDE // LANES

    def body(o1_ref, l1_ref, o2_ref, l2_ref, z_ref, o_ref, og_ref, lse_ref, so, sl):
        j = pl.program_id(1)

        @pl.when(j == 0)
        def _():
            _int_rows(l2_ref, sl)
            lse_ref[...] = jnp.zeros_like(lse_ref)

        l1_all, l2_all = l1_ref[...], sl[...]
        lane = lax.broadcasted_iota(jnp.int32, (rows, LANES), 1)
        lse = lse_ref[...]
        for t in range(hps):
            hh = j * hps + t
            cs = slice(t * LANES, (t + 1) * LANES)
            for r in range(DEINT):
                so.at[t][pl.ds(r, LANES, stride=DEINT), :] = o2_ref[r, :, cs].astype(F32)
            l1, l2 = _lane_of(l1_all, hh), _lane_of(l2_all, hh)
            mx = jnp.maximum(l1, l2)
            w1, w2 = jnp.exp(l1 - mx), jnp.exp(l2 - mx)
            den = w1 + w2
            o = (w1 * o1_ref[:, cs].astype(F32) + w2 * so[t]) / den
            z = z_ref[:, cs].astype(F32)
            o_ref[:, cs] = o.astype(BF16)
            og_ref[:, cs] = (o * (z * _sigmoid(z))).astype(BF16)
            lse = jnp.where(lane == hh, mx + jnp.log(den), lse)
        lse_ref[...] = lse

    blk = pl.BlockSpec((rows, WIDE), lambda b, j: (b, j))
    stat = pl.BlockSpec((rows, LANES), lambda b, j: (b, 0))
    return pl.pallas_call(
        body, out_shape=(SDS((s, cfg.D), BF16), SDS((s, cfg.D), BF16), SDS((s, LANES), F32)),
        grid=(s // rows, h // hps),
        in_specs=[blk, stat, _wide_spec(), _deint_spec(lambda j: 0), pl.BlockSpec((rows, WIDE), lambda b, j: (b, zb + j))],
        out_specs=(blk, blk, stat),
        scratch_shapes=[pltpu.VMEM((hps, rows, LANES), F32), pltpu.VMEM((rows, LANES), F32)],
        compiler_params=_params(("parallel", "arbitrary")), name="attn_merge")(
            o_1, lse_1, _by_residue(o_2), _by_residue(lse_2), proj)


def _attn_bwd_prep(cfg, proj, o_a, doag, lse, dproj):
    s, h = cfg.S, cfg.H
    zb = cfg.OZA // WIDE
    rows = DEINT_ROWS
    hps = WIDE // LANES

    def body(o_ref, dg_ref, z_ref, lse_ref, dp_in, dz_ref, do_ref, do2_ref, dl_ref, dl2_ref, lse2_ref, scr):
        del dp_in
        j = pl.program_id(1)

        @pl.when(j == 0)
        def _():
            dl_ref[...] = jnp.zeros_like(dl_ref)

        lane = lax.broadcasted_iota(jnp.int32, (rows, LANES), 1)
        dl = dl_ref[...]
        for t in range(hps):
            cs = slice(t * LANES, (t + 1) * LANES)
            z = z_ref[:, cs].astype(F32)
            sg = _sigmoid(z)
            o = o_ref[:, cs].astype(F32)
            dg = dg_ref[:, cs].astype(F32)
            do = dg * (z * sg)
            dz_ref[:, cs] = (dg * o * (sg * (1.0 + z * (1.0 - sg)))).astype(BF16)
            do_ref[:, cs] = do.astype(BF16)
            scr[...] = do
            for r in range(DEINT):
                do2_ref[r, :, cs] = scr[pl.ds(r, LANES, stride=DEINT), :].astype(BF16)
            dl = jnp.where(lane == j * hps + t, jnp.sum(do * o, axis=1, keepdims=True), dl)
        dl_ref[...] = dl

        @pl.when(j == h // hps - 1)
        def _():
            scr[...] = dl
            _deint_rows(scr, dl2_ref, F32)
            scr[...] = lse_ref[...]
            _deint_rows(scr, lse2_ref, F32)

    blk = pl.BlockSpec((rows, WIDE), lambda b, j: (b, j))
    stat = pl.BlockSpec((rows, LANES), lambda b, j: (b, 0))
    stat2 = _deint_spec(lambda j: 0)
    outs = pl.pallas_call(
        body,
        out_shape=(SDS(dproj.shape, BF16), SDS((s, cfg.D), BF16), SDS((DEINT, s // DEINT, cfg.D), BF16),
                   SDS((s, LANES), F32), SDS((DEINT, s // DEINT, LANES), F32), SDS((DEINT, s // DEINT, LANES), F32)),
        grid=(s // rows, h // hps),
        in_specs=[blk, blk, pl.BlockSpec((rows, WIDE), lambda b, j: (b, zb + j)), stat, HBM_SPEC],
        out_specs=(pl.BlockSpec((rows, WIDE), lambda b, j: (b, zb + j)), blk, _wide_spec(), stat, stat2, stat2),
        scratch_shapes=[pltpu.VMEM((rows, LANES), F32)],
        input_output_aliases={4: 0},
        compiler_params=_params(("parallel", "arbitrary")), name="attn_bwd_prep")(o_a, doag, proj, lse, dproj)
    dproj, do, do2, dl, dl2, lse2 = outs
    return dproj, do, do2.reshape(s, cfg.D), dl, dl2.reshape(s, LANES), lse2.reshape(s, LANES)


def _attn_grad_sum(cfg, g_1, g_2, col0, dproj, name):
    s = cfg.S
    c0 = col0 // WIDE
    rows = DEINT_ROWS

    def body(g1_ref, g2_ref, dp_in, o_ref, scr):
        del dp_in
        for t in range(WIDE // LANES):
            cs = slice(t * LANES, (t + 1) * LANES)
            for r in range(DEINT):
                scr.at[t][pl.ds(r, LANES, stride=DEINT), :] = g2_ref[r, :, cs].astype(F32)
            o_ref[:, cs] = (g1_ref[:, cs].astype(F32) + scr[t]).astype(BF16)

    return pl.pallas_call(
        body, out_shape=SDS(dproj.shape, BF16), grid=(s // rows, cfg.D // WIDE),
        in_specs=[pl.BlockSpec((rows, WIDE), lambda b, j: (b, j)), _wide_spec(), HBM_SPEC],
        out_specs=pl.BlockSpec((rows, WIDE), lambda b, j: (b, c0 + j)),
        scratch_shapes=[pltpu.VMEM((WIDE // LANES, rows, LANES), F32)],
        input_output_aliases={2: 0},
        compiler_params=_params(("parallel", "parallel")), name=name)(g_1, _by_residue(g_2), dproj)


CONV_HALO = 16
CONV_TR = 512
CONV_CW = 1024


def _rows_back(a, n):
    return a if n == 0 else pltpu.roll(a, n % a.shape[0], axis=0)


def _conv_fwd(cfg, proj, conv_w, conv_b):
    s, cd = cfg.S, cfg.CD
    tr, cw, hl = CONV_TR, CONV_CW, CONV_HALO
    cb0 = cfg.OXBC // cw

    def body(x_ref, h_ref, w_ref, b_ref, o_ref):
        i = pl.program_id(0)
        halo = jnp.where(i > 0, h_ref[...].astype(F32), 0.0)
        ext = jnp.concatenate([halo, x_ref[...].astype(F32)], axis=0)
        pre = b_ref[...] + jnp.zeros((tr, cw), F32)
        for k in range(CONV_K):
            pre = pre + w_ref[k:k + 1, :] * _rows_back(ext, CONV_K - 1 - k)[hl:]
        o_ref[...] = (pre * _sigmoid(pre)).astype(BF16)

    return pl.pallas_call(
        body, out_shape=SDS((s, cd), BF16), grid=(s // tr, cd // cw),
        in_specs=[pl.BlockSpec((tr, cw), lambda i, j: (i, cb0 + j)),
                  pl.BlockSpec((hl, cw), lambda i, j: (jnp.maximum(i * (tr // hl) - 1, 0), cb0 + j)),
                  pl.BlockSpec((CONV_K, cw), lambda i, j: (0, j)),
                  pl.BlockSpec((1, cw), lambda i, j: (0, j))],
        out_specs=pl.BlockSpec((tr, cw), lambda i, j: (i, j)),
        compiler_params=_params(("parallel", "parallel")), name="conv_fwd")(proj, proj, conv_w, conv_b)


def _conv_bwd(cfg, proj, dact, conv_w, conv_b, dproj):
    s, cd = cfg.S, cfg.CD
    tr, cw, hl = CONV_TR, CONV_CW, CONV_HALO
    cb0 = cfg.OXBC // cw
    nr = s // tr
    last_h = s // hl - 1

    def body(x_ref, hp_ref, hn_ref, d_ref, dn_ref, w_ref, b_ref, dp_in, dx_ref, gw_ref, gb_ref):
        del dp_in
        i = pl.program_id(1)
        ext = jnp.concatenate([jnp.where(i > 0, hp_ref[...].astype(F32), 0.0), x_ref[...].astype(F32),
                               hn_ref[...].astype(F32)], axis=0)
        shifted = [_rows_back(ext, CONV_K - 1 - k)[hl:] for k in range(CONV_K)]
        pre = b_ref[...] + jnp.zeros((tr + hl, cw), F32)
        for k in range(CONV_K):
            pre = pre + w_ref[k:k + 1, :] * shifted[k]
        sg = _sigmoid(pre)
        dact = jnp.concatenate([d_ref[...].astype(F32), jnp.where(i < nr - 1, dn_ref[...].astype(F32), 0.0)], axis=0)
        dpre = dact * (sg * (1.0 + pre * (1.0 - sg)))
        dx = jnp.zeros((tr, cw), F32)
        for k in range(CONV_K):
            dx = dx + w_ref[k:k + 1, :] * _rows_back(dpre, -(CONV_K - 1 - k))[0:tr]
        dx_ref[...] = dx.astype(BF16)

        @pl.when(i == 0)
        def _():
            gw_ref[...] = jnp.zeros_like(gw_ref)
            gb_ref[...] = jnp.zeros_like(gb_ref)

        dcur = dpre[0:tr]
        gb_ref[...] += jnp.sum(dcur, axis=0, keepdims=True)
        for k in range(CONV_K):
            gw_ref[k:k + 1, :] += jnp.sum(dcur * shifted[k][0:tr], axis=0, keepdims=True)

    return pl.pallas_call(
        body, out_shape=(SDS(dproj.shape, BF16), SDS((CONV_K, cd), F32), SDS((1, cd), F32)), grid=(cd // cw, nr),
        in_specs=[pl.BlockSpec((tr, cw), lambda j, i: (i, cb0 + j)),
                  pl.BlockSpec((hl, cw), lambda j, i: (jnp.maximum(i * (tr // hl) - 1, 0), cb0 + j)),
                  pl.BlockSpec((hl, cw), lambda j, i: (jnp.minimum((i + 1) * (tr // hl), last_h), cb0 + j)),
                  pl.BlockSpec((tr, cw), lambda j, i: (i, j)),
                  pl.BlockSpec((hl, cw), lambda j, i: (jnp.minimum((i + 1) * (tr // hl), last_h), j)),
                  pl.BlockSpec((CONV_K, cw), lambda j, i: (0, j)),
                  pl.BlockSpec((1, cw), lambda j, i: (0, j)),
                  pl.BlockSpec(memory_space=pl.ANY)],
        out_specs=(pl.BlockSpec((tr, cw), lambda j, i: (i, cb0 + j)),
                   pl.BlockSpec((CONV_K, cw), lambda j, i: (0, j)),
                   pl.BlockSpec((1, cw), lambda j, i: (0, j))),
        input_output_aliases={7: 0},
        compiler_params=_params(("parallel", "arbitrary")), name="conv_bwd")(
            proj, proj, proj, dact, dact, conv_w, conv_b, dproj)


def _expand(v, e, terms):
    out, rem = None, v
    for _ in range(terms):
        hi = rem.astype(BF16)
        t = _nn(hi, e)
        out = t if out is None else out + t
        rem = rem - hi.astype(F32)
    return out


def _segsum(v, e, terms):
    out, rem = None, v
    for _ in range(terms):
        hi = rem.astype(BF16)
        t = _nt(hi, e)
        out = t if out is None else out + t
        rem = rem - hi.astype(F32)
    return out


def _expand_row(row, e, terms):
    return _expand(jnp.broadcast_to(row, (8, LANES)), e, terms)[0:1]


def _segsum_row(row, e, terms):
    return _segsum(jnp.broadcast_to(row, (8, row.shape[1])), e, terms)[0:1]


def _expansion_matrix(cfg):
    hh = jnp.arange(LANES, dtype=jnp.int32)[:, None]
    cc = jnp.arange(cfg.SI, dtype=jnp.int32)[None, :]
    return (cc // SSM_HEAD_DIM == hh).astype(BF16)


def _tri(lower):
    r = lax.broadcasted_iota(jnp.int32, (CHUNK, CHUNK), 0)
    c = lax.broadcasted_iota(jnp.int32, (CHUNK, CHUNK), 1)
    return (c <= r) if lower else (c >= r)


def _ssd_prep(dtr_ref, db_ref, al_ref, e):
    dtr = dtr_ref[...] + db_ref[...]
    dt = _softplus(dtr)
    a = -jnp.exp(al_ref[...])
    acum = jnp.dot(_tri(True).astype(F32), dt * a, precision=lax.Precision.HIGHEST, preferred_element_type=F32)
    return dtr, dt, a, _expand(dt, e, 2), _expand(acum, e, 3)


def _ssd_fwd(cfg, xact, dt_raw, proj, dt_bias, a_log, d_skip, norm_w, e):
    s, si, cd, gw, bc = cfg.S, cfg.SI, cfg.CD, cfg.GW, cfg.BC
    nc = s // CHUNK
    zb = cfg.OZS // si
    tiles = gw // LANES

    def body(xa_ref, dtr_ref, z_ref, db_ref, al_ref, dsk_ref, nw_ref, e_ref, y_ref, y2_ref, st_ref,
             state, ybuf, x_s, xw_s, ae_s, ea_s, lam_s):
        @pl.when(pl.program_id(0) == 0)
        def _():
            state[...] = jnp.zeros_like(state)

        st_ref[...] = state[...]
        ev = e_ref[...]
        _, _, _, dt_e, a_e = _ssd_prep(dtr_ref, db_ref, al_ref, ev)
        xs = xa_ref[:, 0:si].astype(F32)
        x = xs * dt_e
        lam_e = a_e[CHUNK - 1:CHUNK, :]
        x_s[...] = x.astype(BF16)
        xw_s[...] = (x * jnp.exp(lam_e - a_e)).astype(BF16)
        ae_s[...] = a_e
        ea_s[...] = jnp.exp(a_e)
        skip = _expand_row(dsk_ref[...], ev, 3) * xs
        lam_s[...] = jnp.broadcast_to(jnp.exp(lam_e), (8, si))
        tril = _tri(True)
        lane = lax.broadcasted_iota(jnp.int32, (CHUNK, LANES), 1)

        def group(g, carry):
            co = g * gw
            bg = xa_ref[:, pl.ds(si + g * SSM_STATE, SSM_STATE)]
            cg = xa_ref[:, pl.ds(si + bc + g * SSM_STATE, SSM_STATE)]
            cbm = _nt(cg, bg)
            st = state[:, pl.ds(co, gw)]
            yoff = _nn(cg, st.astype(BF16)) * ea_s[:, pl.ds(co, gw)]
            for k in range(tiles):
                tc = co + k * LANES
                at = ae_s[:, pl.ds(tc, LANES)]
                att = at.T
                xt = x_s[:, pl.ds(tc, LANES)]
                acc = yoff[:, k * LANES:(k + 1) * LANES]
                for half in range(2):
                    lo = half * SSM_HEAD_DIM
                    seg = at[:, lo:lo + 1] - att[lo:lo + 1, :]
                    dec = jnp.exp(jnp.where(tril, seg, NEG))
                    xh = jnp.where((lane >= lo) & (lane < lo + SSM_HEAD_DIM), xt, jnp.zeros_like(xt))
                    acc = acc + _nn((cbm * dec).astype(BF16), xh)
                ybuf[:, pl.ds(tc, LANES)] = acc + skip[:, tc:tc + LANES]
            state[:, pl.ds(co, gw)] = st * lam_s[0:1, pl.ds(co, gw)] + _tn(bg, xw_s[:, pl.ds(co, gw)])
            return carry

        for g in range(SSM_GROUPS):
            group(g, 0)
        y = ybuf[...]
        y_ref[...] = y.astype(BF16)
        z = z_ref[...].astype(F32)
        u = y * (z * _sigmoid(z))
        r = lax.rsqrt(jnp.mean(u * u, axis=-1, keepdims=True) + RMS_EPS)
        y2_ref[...] = (u * r * nw_ref[...]).astype(BF16)

    row = lambda n: pl.BlockSpec((1, n), lambda c: (0, 0))
    return pl.pallas_call(
        body,
        out_shape=(SDS((s, si), BF16), SDS((s, si), BF16), SDS((nc, SSM_STATE, si), F32)),
        grid=(nc,),
        in_specs=[pl.BlockSpec((CHUNK, cd), lambda c: (c, 0)),
                  pl.BlockSpec((CHUNK, LANES), lambda c: (c, 0)),
                  pl.BlockSpec((CHUNK, si), lambda c: (c, zb)),
                  row(LANES), row(LANES), row(LANES), row(si),
                  pl.BlockSpec((LANES, si), lambda c: (0, 0))],
        out_specs=(pl.BlockSpec((CHUNK, si), lambda c: (c, 0)),
                   pl.BlockSpec((CHUNK, si), lambda c: (c, 0)),
                   pl.BlockSpec((None, SSM_STATE, si), lambda c: (c, 0, 0))),
        scratch_shapes=[pltpu.VMEM((SSM_STATE, si), F32), pltpu.VMEM((CHUNK, si), F32),
                        pltpu.VMEM((CHUNK, si), BF16), pltpu.VMEM((CHUNK, si), BF16),
                        pltpu.VMEM((CHUNK, si), F32), pltpu.VMEM((CHUNK, si), F32),
                        pltpu.VMEM((8, si), F32)],
        compiler_params=_params(("arbitrary",)), name="ssd_fwd")(
            xact, dt_raw, proj, dt_bias, a_log, d_skip, norm_w, e)


def _ssd_bwd(cfg, xact, dt_raw, proj, y, dy2, states, dt_bias, a_log, d_skip, norm_w, e, dproj):
    s, si, cd, gw, bc, hpg = cfg.S, cfg.SI, cfg.CD, cfg.GW, cfg.BC, cfg.HPG
    nc = s // CHUNK
    zb = cfg.OZS // si
    tiles = gw // LANES

    def body(xa_ref, dtr_ref, z_ref, y_ref, d2_ref, st_ref, db_ref, al_ref, dsk_ref, nw_ref, e_ref, dp_in,
             dz_ref, dxa_ref, ddt_ref, gnw_ref, gdb_ref, gal_ref, gds_ref,
             dh, dhn, xs_s, x_s, w_s, ae_s, ea_s, g_s, dx_s, dae_s, lam_s, dle_s):
        del dp_in

        @pl.when(pl.program_id(0) == 0)
        def _():
            dh[...] = jnp.zeros_like(dh)
            gnw_ref[...] = jnp.zeros_like(gnw_ref)
            gdb_ref[...] = jnp.zeros_like(gdb_ref)
            gal_ref[...] = jnp.zeros_like(gal_ref)
            gds_ref[...] = jnp.zeros_like(gds_ref)

        ev = e_ref[...]
        yv = y_ref[...].astype(F32)
        z = z_ref[...].astype(F32)
        sg = _sigmoid(z)
        sz = z * sg
        u = yv * sz
        r = lax.rsqrt(jnp.mean(u * u, axis=-1, keepdims=True) + RMS_EPS)
        nrm = u * r
        d2 = d2_ref[...].astype(F32)
        gnw_ref[...] += jnp.sum(d2 * nrm, axis=0, keepdims=True)
        gn = d2 * nw_ref[...]
        du = r * (gn - nrm * jnp.mean(gn * nrm, axis=-1, keepdims=True))
        gv = du * sz
        dz_ref[...] = (du * yv * (sg * (1.0 + z * (1.0 - sg)))).astype(BF16)
        g_s[...] = gv

        dtr, dt, a, dt_e, a_e = _ssd_prep(dtr_ref, db_ref, al_ref, ev)
        xs = xa_ref[:, 0:si].astype(F32)
        x = xs * dt_e
        lam_e = a_e[CHUNK - 1:CHUNK, :]
        xs_s[...] = xs
        x_s[...] = x
        w_s[...] = jnp.exp(lam_e - a_e)
        ae_s[...] = a_e
        ea_s[...] = jnp.exp(a_e)
        lam_s[...] = jnp.broadcast_to(jnp.exp(lam_e), (8, si))
        gds_ref[...] += _segsum_row(jnp.sum(gv * xs, axis=0, keepdims=True), ev, 2)
        col_sums = [jnp.zeros((CHUNK, LANES), F32)]
        tril = _tri(True)
        lane = lax.broadcasted_iota(jnp.int32, (CHUNK, LANES), 1)
        sub = lax.broadcasted_iota(jnp.int32, (CHUNK, LANES), 0)

        def group(g, carry):
            co = g * gw
            bo = si + g * SSM_STATE
            cof = si + bc + g * SSM_STATE
            cols = pl.ds(co, gw)
            bg = xa_ref[:, pl.ds(bo, SSM_STATE)]
            cg = xa_ref[:, pl.ds(cof, SSM_STATE)]
            cbm = _nt(cg, bg)
            st = st_ref[:, cols]
            stb = st.astype(BF16)
            dho = dh[:, cols]
            dhob = dho.astype(BF16)
            ea = ea_s[:, cols]
            gg = g_s[:, cols]
            xg = x_s[:, cols]
            wg = w_s[:, cols]
            explam = lam_s[0:1, cols]
            yoff = _nn(cg, stb) * ea
            ga = (gg * ea).astype(BF16)
            dc = _nt(ga, stb)
            dhn[:, cols] = dho * explam + _tn(cg, ga)
            bdh = _nn(bg, dhob)
            db = _nt((xg * wg).astype(BF16), dhob)
            t = xg * bdh * wg
            dle_s[0:1, cols] = jnp.sum(t, axis=0, keepdims=True) + explam * jnp.sum(dho * st, axis=0, keepdims=True)
            dae_base = gg * yoff - t
            dxw = wg * bdh
            dcb = jnp.zeros((CHUNK, CHUNK), F32)
            for k in range(tiles):
                tc = co + k * LANES
                ksl = slice(k * LANES, (k + 1) * LANES)
                at = ae_s[:, pl.ds(tc, LANES)]
                att = at.T
                xt = xg[:, ksl].astype(BF16)
                gt = gg[:, ksl].astype(BF16)
                dxt = dxw[:, ksl]
                place = jnp.zeros((CHUNK, LANES), F32)
                for half in range(2):
                    lo = half * SSM_HEAD_DIM
                    seg = at[:, lo:lo + 1] - att[lo:lo + 1, :]
                    dec = jnp.exp(jnp.where(tril, seg, NEG))
                    mh = cbm * dec
                    gh = jnp.where((lane >= lo) & (lane < lo + SSM_HEAD_DIM), gt, jnp.zeros_like(gt))
                    dm = _nt(gh, xt)
                    dxt = dxt + _tn(mh.astype(BF16), gh)
                    dcb = dcb + dm * dec
                    dseg = dm * mh
                    place = place + jnp.where(lane == lo, jnp.sum(dseg, axis=1, keepdims=True), 0.0)
                    hidx = g * hpg + 2 * k + half
                    col_sums[0] = col_sums[0] + jnp.where(sub == hidx, jnp.sum(dseg, axis=0, keepdims=True), 0.0)
                dx_s[:, pl.ds(tc, LANES)] = dxt
                dae_s[:, pl.ds(tc, LANES)] = dae_base[:, ksl] + place
            dcbb = dcb.astype(BF16)
            dxa_ref[:, pl.ds(bo, SSM_STATE)] = (db + _tn(dcbb, cg)).astype(BF16)
            dxa_ref[:, pl.ds(cof, SSM_STATE)] = (dc + _nn(dcbb, bg)).astype(BF16)
            return carry

        for g in range(SSM_GROUPS):
            group(g, 0)
        dlam = _segsum_row(dle_s[0:1, :], ev, 2)
        da_ = _segsum(dae_s[...], ev, 2) - col_sums[0].T
        da_ = da_ + jnp.where(sub == CHUNK - 1, dlam, 0.0)
        dda = jnp.dot(_tri(False).astype(F32), da_, precision=lax.Precision.HIGHEST, preferred_element_type=F32)
        dxv = dx_s[...]
        xs = xs_s[...]
        ddt = dda * a + _segsum(dxv * xs, ev, 2)
        gal_ref[...] += jnp.sum(dda * dt, axis=0, keepdims=True) * a
        ddtr = ddt * _sigmoid(dtr)
        gdb_ref[...] += jnp.sum(ddtr, axis=0, keepdims=True)
        ddt_ref[...] = ddtr
        dxa_ref[:, 0:si] = (dxv * dt_e + g_s[...] * _expand_row(dsk_ref[...], ev, 3)).astype(BF16)
        dh[...] = dhn[...]

    rev = lambda c: nc - 1 - c
    row = lambda n: pl.BlockSpec((1, n), lambda c: (0, 0))
    big = lambda: pltpu.VMEM((CHUNK, si), F32)
    return pl.pallas_call(
        body,
        out_shape=(SDS(dproj.shape, BF16), SDS((s, cd), BF16), SDS((s, LANES), F32),
                   SDS((1, si), F32), SDS((1, LANES), F32), SDS((1, LANES), F32), SDS((1, LANES), F32)),
        grid=(nc,),
        in_specs=[pl.BlockSpec((CHUNK, cd), lambda c: (rev(c), 0)),
                  pl.BlockSpec((CHUNK, LANES), lambda c: (rev(c), 0)),
                  pl.BlockSpec((CHUNK, si), lambda c: (rev(c), zb)),
                  pl.BlockSpec((CHUNK, si), lambda c: (rev(c), 0)),
                  pl.BlockSpec((CHUNK, si), lambda c: (rev(c), 0)),
                  pl.BlockSpec((None, SSM_STATE, si), lambda c: (rev(c), 0, 0)),
                  row(LANES), row(LANES), row(LANES), row(si),
                  pl.BlockSpec((LANES, si), lambda c: (0, 0)),
                  pl.BlockSpec(memory_space=pl.ANY)],
        out_specs=(pl.BlockSpec((CHUNK, si), lambda c: (rev(c), zb)),
                   pl.BlockSpec((CHUNK, cd), lambda c: (rev(c), 0)),
                   pl.BlockSpec((CHUNK, LANES), lambda c: (rev(c), 0)),
                   row(si), row(LANES), row(LANES), row(LANES)),
        scratch_shapes=[pltpu.VMEM((SSM_STATE, si), F32), pltpu.VMEM((SSM_STATE, si), F32),
                        big(), big(), big(), big(), big(), big(), big(), big(),
                        pltpu.VMEM((8, si), F32), pltpu.VMEM((8, si), F32)],
        input_output_aliases={11: 0},
        compiler_params=_params(("arbitrary",)), name="ssd_bwd")(
            xact, dt_raw, proj, y, dy2, states, dt_bias, a_log, d_skip, norm_w, e, dproj)


MERGE_TR = 512
MERGE_CW = 2048


def _merge_fwd(cfg, proj, a_br, s_br):
    s, d = cfg.S, cfg.D
    tr, cw = MERGE_TR, min(MERGE_CW, d)
    ga0, gs0 = cfg.OGA // cw, cfg.OGS // cw

    def body(ga_ref, gs_ref, a_ref, s_ref, o_ref):
        o_ref[...] = (_sigmoid(ga_ref[...].astype(F32)) * a_ref[...].astype(F32)
                      + _sigmoid(gs_ref[...].astype(F32)) * s_ref[...].astype(F32)).astype(BF16)

    blk = pl.BlockSpec((tr, cw), lambda i, j: (i, j))
    return pl.pallas_call(
        body, out_shape=SDS((s, d), BF16), grid=(s // tr, d // cw),
        in_specs=[pl.BlockSpec((tr, cw), lambda i, j: (i, ga0 + j)),
                  pl.BlockSpec((tr, cw), lambda i, j: (i, gs0 + j)), blk, blk],
        out_specs=blk, compiler_params=_params(("parallel", "parallel")), name="merge_fwd")(proj, proj, a_br, s_br)


def _merge_bwd(cfg, proj, branch, dmerged, gate_off, dproj, name):
    s, d = cfg.S, cfg.D
    tr, cw = MERGE_TR, min(MERGE_CW, d)
    g0 = gate_off // cw
    fresh = dproj is None

    def body(*refs):
        g_ref, b_ref, dm_ref = refs[:3]
        dg_ref, db_ref = refs[-2:]
        dm = dm_ref[...].astype(F32)
        sg = _sigmoid(g_ref[...].astype(F32))
        db_ref[...] = (dm * sg).astype(BF16)
        dg_ref[...] = (dm * b_ref[...].astype(F32) * sg * (1.0 - sg)).astype(BF16)

    blk = pl.BlockSpec((tr, cw), lambda i, j: (i, j))
    gate = pl.BlockSpec((tr, cw), lambda i, j: (i, g0 + j))
    return pl.pallas_call(
        body, out_shape=(SDS((s, cfg.NM), BF16), SDS((s, d), BF16)), grid=(s // tr, d // cw),
        in_specs=[gate, blk, blk] + ([] if fresh else [HBM_SPEC]),
        out_specs=(gate, blk),
        input_output_aliases={} if fresh else {3: 0},
        compiler_params=_params(("parallel", "parallel")), name=name)(
            *((proj, branch, dmerged) + (() if fresh else (dproj,))))


def _outproj_loss(merged, w_out, x, target, fnw):
    s, d = x.shape
    tr = 256

    def body(m_ref, w_ref, x_ref, t_ref, fw_ref, dof_ref, dob_ref, loss_ref, g_ref):
        out = x_ref[...] + _nn(m_ref[...], w_ref[...])
        r = lax.rsqrt(jnp.mean(out * out, axis=-1, keepdims=True) + RMS_EPS)
        nrm = out * r
        fw = fw_ref[...]
        err = nrm * fw - t_ref[...]
        dy = err * (1.0 / d)
        gy = dy * fw
        dout = r * (gy - nrm * jnp.mean(gy * nrm, axis=-1, keepdims=True))
        dof_ref[...] = dout
        dob_ref[...] = dout.astype(BF16)

        @pl.when(pl.program_id(0) == 0)
        def _():
            loss_ref[...] = jnp.zeros_like(loss_ref)
            g_ref[...] = jnp.zeros_like(g_ref)

        loss_ref[...] += jnp.sum(jnp.sum(err * err, axis=1, keepdims=True), axis=0, keepdims=True) * (0.5 / d)
        g_ref[...] += jnp.sum(dy * nrm, axis=0, keepdims=True)

    blk = pl.BlockSpec((tr, d), lambda i: (i, 0))
    return pl.pallas_call(
        body, out_shape=(SDS((s, d), F32), SDS((s, d), BF16), SDS((1, LANES), F32), SDS((1, d), F32)), grid=(s // tr,),
        in_specs=[blk, pl.BlockSpec((d, d), lambda i: (0, 0)), blk, blk, pl.BlockSpec((1, d), lambda i: (0, 0))],
        out_specs=(blk, blk, pl.BlockSpec((1, LANES), lambda i: (0, 0)), pl.BlockSpec((1, d), lambda i: (0, 0))),
        compiler_params=_params(("arbitrary",)), name="outproj_loss")(merged, w_out, x, target, fnw)


ELEMWISE_BLOCK_BYTES = 1 << 20


def _row_block(rows, cols, itemsize=4):
    best = None
    for tr in range(16, rows + 1, 16):
        if rows % tr == 0 and tr * cols * itemsize <= ELEMWISE_BLOCK_BYTES:
            best = tr
    return best if best is not None else rows


def _adamw(w, g, m, v, name):
    rows, cols = w.shape
    tr = _row_block(rows, cols)
    if rows // tr > 64 and cols % LANES == 0:
        blk, grid = pl.BlockSpec((rows, LANES), lambda i: (0, i)), (cols // LANES,)
    else:
        blk, grid = pl.BlockSpec((tr, cols), lambda i: (i, 0)), (rows // tr,)
    out = SDS((rows, cols), F32)
    return pl.pallas_call(
        _adamw_body(), out_shape=(out, out, out), grid=grid, in_specs=[blk] * 4, out_specs=(blk,) * 3,
        compiler_params=_params(("parallel",)), name=name)(w, g, m, v)


def _adamw_body():
    def body(w_ref, g_ref, m_ref, v_ref, d_ref, nm_ref, nv_ref):
        gv = g_ref[...]
        nm = ADAM_B1 * m_ref[...] + (1.0 - ADAM_B1) * gv
        nv = ADAM_B2 * v_ref[...] + (1.0 - ADAM_B2) * jnp.square(gv)
        m_hat = nm / (1.0 - ADAM_B1 ** ADAM_STEP)
        v_hat = nv / (1.0 - ADAM_B2 ** ADAM_STEP)
        d_ref[...] = -ADAM_LR * (m_hat / (jnp.sqrt(v_hat) + ADAM_EPS) + ADAM_WD * w_ref[...])
        nm_ref[...] = nm
        nv_ref[...] = nv

    return body


HBM_SPEC = pl.BlockSpec(memory_space=pl.ANY)


def _position():
    return lax.axis_index("x"), lax.axis_index("y"), lax.axis_index("c")


class _Carry:
    def __init__(self, arrays, out_shapes, sems, start, finish):
        self.arrays, self.out_shapes, self.sems, self.start, self.finish = list(arrays), out_shapes, sems, start, finish

    def sem_shapes(self):
        return [pltpu.SemaphoreType.DMA((k,)) for k in self.sems]


def _gather_carry(shards, by_cols=()):
    n = len(shards)

    def copies(ins, outs, sems):
        send_sems, recv_sems, fsend_sems, frecv_sems = sems
        x, y, c = _position()
        me = 2 * x + y
        peers = [(1 - x, y), (x, 1 - y), (1 - x, 1 - y)]

        def half_of(t, chip, half):
            if t in by_cols:
                c2 = ins[t].shape[1] // 2
                return outs[t].at[chip, :, pl.ds(half * c2, c2)]
            return outs[t].at[chip, half]

        def over_ici(t, p, chip):
            px, py = peers[p]
            if t in by_cols:
                c2 = ins[t].shape[1] // 2
                src = ins[t].at[:, pl.ds(c * c2, c2)]
            else:
                r2 = ins[t].shape[0] // 2
                src = ins[t].at[pl.ds(c * r2, r2), :]
            return pltpu.make_async_remote_copy(
                src_ref=src, dst_ref=half_of(t, chip, c), send_sem=send_sems.at[3 * t + p],
                recv_sem=recv_sems.at[3 * t + p], device_id=(px, py, c), device_id_type=MESH)

        def to_sibling(t, p, half):
            px, py = peers[p]
            slab = half_of(t, 2 * px + py, half)
            return pltpu.make_async_remote_copy(
                src_ref=slab, dst_ref=slab, send_sem=fsend_sems.at[3 * t + p], recv_sem=frecv_sems.at[3 * t + p],
                device_id=(x, y, 1 - c), device_id_type=MESH)

        pairs = [(t, p) for t in range(n) for p in range(3)]
        sends = [over_ici(t, p, me) for t, p in pairs]
        lands = [over_ici(t, p, 2 * peers[p][0] + peers[p][1]) for t, p in pairs]
        passed = [to_sibling(t, p, c) for t, p in pairs]
        from_sibling = [to_sibling(t, p, 1 - c) for t, p in pairs]
        return sends, lands, passed, from_sibling

    def start(ins, outs, sems):
        for cp in copies(ins, outs, sems)[0]:
            cp.start()

    def finish(ins, outs, sems):
        sends, lands, passed, from_sibling = copies(ins, outs, sems)
        for land, fwd in zip(lands, passed):
            land.wait_recv()
            fwd.start()
        for cp in from_sibling:
            cp.wait_recv()
        for cp in sends + passed:
            cp.wait_send()

    shapes = [SDS((N_CHIPS,) + a.shape if t in by_cols else (N_CHIPS, 2, a.shape[0] // 2, a.shape[1]), a.dtype)
              for t, a in enumerate(shards)]
    return _Carry(shards, shapes, [3 * n] * 4, start, finish)


def _scatter_carry(parts):
    def start(ins, outs, sems):
        for cp in _scatter_copies(ins, outs, *sems)[0]:
            cp.start()

    def finish(ins, outs, sems):
        sends, lands = _scatter_copies(ins, outs, *sems)
        for cp in lands:
            cp.wait_recv()
        for cp in sends:
            cp.wait_send()

    return _Carry(parts, [SDS(a.shape, a.dtype) for a in parts], [3 * len(parts)] * 2, start, finish)


def _with_own(gathered, own, chip):
    full = gathered.reshape((N_CHIPS,) + own.shape)
    return lax.dynamic_update_index_in_dim(full, own, chip, 0)


def _exchange_halves(grads):
    n = len(grads)
    slabs = [list(g) if isinstance(g, (list, tuple)) else [g] for g in grads]
    flat = [a for s in slabs for a in s]
    ncp = len(flat)

    def body(*refs):
        ins, outs = refs[:ncp], refs[ncp:ncp + n]
        send_sems, recv_sems = refs[ncp + n:]
        x, y, c = _position()
        cps, k = [], 0
        for t in range(n):
            for j in range(len(slabs[t])):
                if len(slabs[t]) == 1:
                    r2 = ins[k].shape[1] // 2
                    src, dst = ins[k].at[:, pl.ds((1 - c) * r2, r2), :], outs[t]
                else:
                    r2 = ins[k].shape[0] // 2
                    src, dst = ins[k].at[pl.ds((1 - c) * r2, r2), :], outs[t].at[j]
                cps.append(pltpu.make_async_remote_copy(
                    src_ref=src, dst_ref=dst, send_sem=send_sems.at[k], recv_sem=recv_sems.at[k],
                    device_id=(x, y, 1 - c), device_id_type=MESH))
                k += 1
        for cp in cps:
            cp.start()
        for cp in cps:
            cp.wait()

    def landing(s):
        a = s[0]
        return SDS((N_CHIPS, a.shape[-2] // 2, a.shape[-1]), a.dtype)

    return pl.pallas_call(
        body, out_shape=[landing(s) for s in slabs],
        in_specs=[HBM_SPEC] * ncp, out_specs=[HBM_SPEC] * n,
        scratch_shapes=[pltpu.SemaphoreType.DMA((ncp,)), pltpu.SemaphoreType.DMA((ncp,))],
        compiler_params=pltpu.CompilerParams(has_side_effects=True), name="reduce_sibling")(*flat)


def _scatter_copies(ins, outs, send_sems, recv_sems):
    x, y, c = _position()
    me = 2 * x + y
    peers = [(1 - x, y), (x, 1 - y), (1 - x, 1 - y)]

    def remote(t, p, src_slab, dst_slab):
        px, py = peers[p]
        return pltpu.make_async_remote_copy(
            src_ref=ins[t].at[src_slab], dst_ref=outs[t].at[dst_slab], send_sem=send_sems.at[3 * t + p],
            recv_sem=recv_sems.at[3 * t + p], device_id=(px, py, c), device_id_type=MESH)

    n = len(ins)
    sends = [remote(t, p, 2 * peers[p][0] + peers[p][1], me) for t in range(n) for p in range(3)]
    lands = [remote(t, p, me, 2 * peers[p][0] + peers[p][1]) for t in range(n) for p in range(3)]
    return sends, lands


def _share_halves(halves):
    n = len(halves)

    def body(*refs):
        ins, outs = refs[:n], refs[n:2 * n]
        send_sems, recv_sems = refs[2 * n:]
        x, y, c = _position()

        def copy(t, slab):
            return pltpu.make_async_remote_copy(
                src_ref=ins[t].at[slab], dst_ref=outs[t].at[slab], send_sem=send_sems.at[t], recv_sem=recv_sems.at[t],
                device_id=(x, y, 1 - c), device_id_type=MESH)

        for t in range(n):
            copy(t, c).start()
        for t in range(n):
            copy(t, 1 - c).wait_recv()
        for t in range(n):
            copy(t, c).wait_send()

    return pl.pallas_call(
        body, out_shape=[SDS(a.shape, a.dtype) for a in halves],
        in_specs=[HBM_SPEC] * n, out_specs=[HBM_SPEC] * n,
        scratch_shapes=[pltpu.SemaphoreType.DMA((n,)), pltpu.SemaphoreType.DMA((n,))],
        input_output_aliases={t: t for t in range(n)},
        compiler_params=pltpu.CompilerParams(has_side_effects=True), name="share_sibling")(*halves)


def _add_sibling(grad, recv, core):
    nch, r2, cols = recv.shape
    tr = _row_block(r2, cols)
    nb = r2 // tr

    def body(c_ref, g_ref, r_ref, o_ref):
        del c_ref
        o_ref[...] = (g_ref[...].astype(F32) + r_ref[...].astype(F32)).astype(BF16)

    return pl.pallas_call(
        body, out_shape=SDS(recv.shape, BF16),
        grid_spec=pltpu.PrefetchScalarGridSpec(
            num_scalar_prefetch=1, grid=(nch, nb),
            in_specs=[pl.BlockSpec((None, tr, cols), lambda j, i, c_ref: (j, c_ref[0] * nb + i, 0)),
                      pl.BlockSpec((None, tr, cols), lambda j, i, c_ref: (j, i, 0))],
            out_specs=pl.BlockSpec((None, tr, cols), lambda j, i, c_ref: (j, i, 0))),
        compiler_params=_params(("parallel", "parallel")), name="add_sibling")(core, grad, recv)


def _add_chips(own, recv, chip_core):
    nch, r2, cols = recv.shape
    tr = _row_block(r2, cols)

    nsc = 2 + nch

    def body(*refs):
        me = refs[0][0]
        own_ref, p_refs, o_ref = refs[nsc], refs[nsc + 1:nsc + 1 + nch], refs[nsc + 1 + nch]
        acc = None
        for j in range(nch):
            term = jnp.where(me == j, own_ref[...], p_refs[j][...]).astype(F32)
            acc = term if acc is None else acc + term
        o_ref[...] = acc

    def slab(j):
        return pl.BlockSpec((None, tr, cols), lambda i, *sc: (sc[2 + j][0], i, 0))

    return pl.pallas_call(
        body, out_shape=SDS((2, r2, cols), F32),
        grid_spec=pltpu.PrefetchScalarGridSpec(
            num_scalar_prefetch=nsc, grid=(r2 // tr,),
            in_specs=[pl.BlockSpec((None, tr, cols), lambda i, *sc: (sc[0][0], i, 0))] + [slab(j) for j in range(nch)],
            out_specs=pl.BlockSpec((None, tr, cols), lambda i, *sc: (sc[1][0], i, 0))),
        compiler_params=_params(("parallel",)), name="add_chips")(*chip_core, own, *([recv] * nch))


def _col_halves_carry(grad):
    nch, r, cols = grad.shape
    c2 = cols // 2

    def copy(ins, outs, sems):
        x, y, c = _position()
        return pltpu.make_async_remote_copy(
            src_ref=ins[0].at[:, :, pl.ds((1 - c) * c2, c2)], dst_ref=outs[0], send_sem=sems[0].at[0],
            recv_sem=sems[1].at[0], device_id=(x, y, 1 - c), device_id_type=MESH)

    return _Carry([grad], [SDS((nch, r, c2), grad.dtype)], [1, 1],
                  lambda ins, outs, sems: copy(ins, outs, sems).start(),
                  lambda ins, outs, sems: copy(ins, outs, sems).wait())


def _add_sibling_cols(grad, recv, core):
    nch, r, c2 = recv.shape
    nb = c2 // LANES

    def body(c_ref, g_ref, r_ref, o_ref):
        del c_ref
        o_ref[...] = (g_ref[...].astype(F32) + r_ref[...].astype(F32)).astype(BF16)

    blk = pl.BlockSpec((None, r, LANES), lambda j, i, c_ref: (j, 0, i))
    return pl.pallas_call(
        body, out_shape=SDS(recv.shape, BF16),
        grid_spec=pltpu.PrefetchScalarGridSpec(
            num_scalar_prefetch=1, grid=(nch, nb),
            in_specs=[pl.BlockSpec((None, r, LANES), lambda j, i, c_ref: (j, 0, c_ref[0] * nb + i)), blk],
            out_specs=blk),
        compiler_params=_params(("parallel", "parallel")), name="add_sibling_cols")(core, grad, recv)


def _add_chips_cols(own, recv, chip_core):
    nch, r, c2 = recv.shape
    nb = c2 // LANES
    nsc = 2 + nch

    def body(*refs):
        me = refs[0][0]
        own_ref, p_refs, o_ref = refs[nsc], refs[nsc + 1:nsc + 1 + nch], refs[nsc + 1 + nch]
        acc = None
        for j in range(nch):
            term = jnp.where(me == j, own_ref[...], p_refs[j][...]).astype(F32)
            acc = term if acc is None else acc + term
        o_ref[...] = acc

    def slab(j):
        return pl.BlockSpec((None, r, LANES), lambda i, *sc: (sc[2 + j][0], 0, i))

    return pl.pallas_call(
        body, out_shape=SDS((r, 2 * c2), F32),
        grid_spec=pltpu.PrefetchScalarGridSpec(
            num_scalar_prefetch=nsc, grid=(nb,),
            in_specs=[pl.BlockSpec((None, r, LANES), lambda i, *sc: (sc[0][0], 0, i))] + [slab(j) for j in range(nch)],
            out_specs=pl.BlockSpec((r, LANES), lambda i, *sc: (0, sc[1][0] * nb + i))),
        compiler_params=_params(("parallel",)), name="add_chips_cols")(*chip_core, own, *([recv] * nch))


def _share_col_halves(full):
    r, cols = full.shape
    c2 = cols // 2

    def body(in_ref, out_ref, send_sem, recv_sem):
        x, y, c = _position()

        def copy(half):
            return pltpu.make_async_remote_copy(
                src_ref=in_ref.at[:, pl.ds(half * c2, c2)], dst_ref=out_ref.at[:, pl.ds(half * c2, c2)],
                send_sem=send_sem.at[0], recv_sem=recv_sem.at[0], device_id=(x, y, 1 - c), device_id_type=MESH)

        copy(c).start()
        copy(1 - c).wait_recv()
        copy(c).wait_send()

    return pl.pallas_call(
        body, out_shape=SDS(full.shape, full.dtype), in_specs=[HBM_SPEC], out_specs=HBM_SPEC,
        scratch_shapes=[pltpu.SemaphoreType.DMA((1,)), pltpu.SemaphoreType.DMA((1,))],
        input_output_aliases={0: 0},
        compiler_params=pltpu.CompilerParams(has_side_effects=True), name="share_sibling_cols")(full)


def _allreduce_small(pack):
    rows = pack.shape[0]

    def body(p_ref, o_ref, buf, send_sems, recv_sems):
        x, y, c = _position()
        me = 4 * x + 2 * y + c
        buf[me] = p_ref[...]

        def copy(dst_dev, slot):
            return pltpu.make_async_remote_copy(
                src_ref=p_ref, dst_ref=buf.at[slot], send_sem=send_sems.at[dst_dev], recv_sem=recv_sems.at[slot],
                device_id=(dst_dev // 4, (dst_dev // 2) % 2, dst_dev % 2), device_id_type=MESH)

        for dev in range(N_DEV):
            @pl.when(dev != me)
            def _():
                copy(dev, me).start()
        for dev in range(N_DEV):
            @pl.when(dev != me)
            def _():
                copy(dev, dev).wait_recv()
        for dev in range(N_DEV):
            @pl.when(dev != me)
            def _():
                copy(dev, me).wait_send()
        acc = buf[0]
        for dev in range(1, N_DEV):
            acc = acc + buf[dev]
        o_ref[...] = acc

    return pl.pallas_call(
        body, out_shape=SDS(pack.shape, F32),
        in_specs=[pl.BlockSpec(memory_space=pltpu.VMEM)], out_specs=pl.BlockSpec(memory_space=pltpu.VMEM),
        scratch_shapes=[pltpu.VMEM((N_DEV, rows, LANES), F32), pltpu.SemaphoreType.DMA((N_DEV,)),
                        pltpu.SemaphoreType.DMA((N_DEV,))],
        compiler_params=pltpu.CompilerParams(has_side_effects=True), name="allreduce_small")(pack)


ATTN_TQ = 256


def _local_step(cfg, x, target, w, to_chips=None, late=None, hn=None):
    d = cfg.D
    if hn is None:
        hn = _rmsnorm_fwd(x, w["norm_w"])
    proj = _mm(hn, w["w_main_t"], "nt", BF16, "proj_main", carry=late[0] if late else None, b_rows=cfg.NM)
    if late:
        proj, arrived = proj
        w = {**w, **late[1](arrived)}
    dt_raw = _mm(hn, w["w_dt_t"], "nt", F32, "proj_dt")
    slopes = _slopes(cfg.H)
    near = _Pass(ATTN_TQ, DILATED_PATTERNS[:-1], 1, cfg.S)
    far = _Pass(LANES, DILATED_PATTERNS[-1:], DEINT, cfg.S // DEINT)
    tab_near, tab_far = _attn_tables(near), _attn_tables(far)
    cols_near, cols_far = (cfg.OQ, cfg.OK, cfg.OV), (0, d, 2 * d)
    qkv_far = _deinterleave(proj, 0, 3 * d, "attn_deinterleave")
    o_1, lse_1 = _attn_fwd(cfg, near, proj, cols_near, tab_near, slopes, "attn_fwd_near")
    o_2, lse_2 = _attn_fwd(cfg, far, qkv_far, cols_far, tab_far, slopes, "attn_fwd_far")
    o_a, oag, lse = _attn_merge(cfg, proj, o_1, lse_1, o_2, lse_2)
    xact = _conv_fwd(cfg, proj, w["conv_w"], w["conv_b"])
    e = _expansion_matrix(cfg)
    y, y2, states = _ssd_fwd(cfg, xact, dt_raw, proj, w["dt_bias"], w["a_log"], w["d_skip"], w["ssm_norm_w"], e)
    a_br = _mm(oag, w["w_attn"], "nn", BF16, "branch_attn")
    s_br = _mm(y2, w["w_ssm"], "nn", BF16, "branch_ssm")
    merged = _merge_fwd(cfg, proj, a_br, s_br)
    dout_f, dout_b, loss_row, g_fnw = _outproj_loss(merged, w["w_out"], x, target, w["final_norm_w"])

    dmerged = _mm(dout_b, w["w_out"], "nt", BF16, "d_merged")
    dproj, da_br = _merge_bwd(cfg, proj, a_br, dmerged, cfg.OGA, None, "merge_bwd_attn")
    dproj, ds_br = _merge_bwd(cfg, proj, s_br, dmerged, cfg.OGS, dproj, "merge_bwd_ssm")
    doag = _mm(da_br, w["w_attn"], "nt", BF16, "d_oag")
    dy2 = _mm(ds_br, w["w_ssm"], "nt", BF16, "d_y2")
    dproj, dxact, ddt, g_snw, g_dtb, g_alog, g_dsk = _ssd_bwd(
        cfg, xact, dt_raw, proj, y, dy2, states, w["dt_bias"], w["a_log"], w["d_skip"], w["ssm_norm_w"], e, dproj)
    dproj, g_cw, g_cb = _conv_bwd(cfg, proj, dxact, w["conv_w"], w["conv_b"], dproj)
    dproj, do, do_far, dl, dl_far, lse_far = _attn_bwd_prep(cfg, proj, o_a, doag, lse, dproj)
    g_near = _attn_bwd(cfg, near, proj, cols_near, do, lse, dl, tab_near, slopes, "attn_bwd_near")
    g_far = _attn_bwd(cfg, far, qkv_far, cols_far, do_far, lse_far, dl_far, tab_far, slopes, "attn_bwd_far")
    for g_1, g_2, col0, nm in zip(g_near, g_far, cols_near, ("attn_dq", "attn_dk", "attn_dv")):
        dproj = _attn_grad_sum(cfg, g_1, g_2, col0, dproj, nm)
    ddt_b = ddt.astype(BF16)
    g_w_main = _mm(dproj, hn, "tn", BF16, "g_w_main", out_rows=cfg.N_IN)
    g_w_dt = _mm(ddt_b, hn, "tn", BF16, "g_w_dt")
    grads = dict(w_main_t=g_w_main, w_dt_t=g_w_dt, conv_w=g_cw, conv_b=g_cb, dt_bias=g_dtb, a_log=g_alog,
                 d_skip=g_dsk, ssm_norm_w=g_snw, final_norm_w=g_fnw)
    riding = to_chips[0](grads) if to_chips is not None else None
    g_w_ssm = _mm(y2, ds_br, "tn", BF16, "g_w_ssm", carry=riding[1] if riding else None)
    if riding:
        g_w_ssm, from_sibling_in = g_w_ssm
    grads.update(w_ssm=g_w_ssm, w_out=_mm(merged, dout_b, "tn", BF16, "g_w_out"),
                 w_attn=_mm(oag, da_br, "tn", BF16, "g_w_attn"))
    sent = to_chips[1](grads, riding[0], from_sibling_in[0]) if to_chips is not None else ()
    dhn = _mm(dproj, w["w_main_t"], "nn", F32, "d_hn", tk=1024, carry=_scatter_carry(sent) if sent else None,
              b_rows=cfg.NM)
    landed = ()
    if sent:
        dhn, landed = dhn
    dhn_dt = _mm(ddt_b, w["w_dt_t"], "nn", F32, "d_hn_dt")
    grad_x, grads["norm_w"] = _rmsnorm_bwd(x, w["norm_w"], dhn, dhn_dt, dout_f)
    return loss_row, grad_x, grads, sent, landed


def _pad_lanes(v):
    return jnp.pad(v, ((0, 0), (0, LANES - v.shape[1])))


def _main_from_rows(cfg, w_in_t):
    lo, hi = cfg.OGA, cfg.OGA + cfg.NH
    dt = jnp.pad(w_in_t[lo:hi], ((0, LANES - cfg.NH), (0, 0)))
    return lax.dynamic_update_slice(w_in_t, w_in_t[hi:], (lo, 0)), dt


def _rows_from_main(cfg, g_main_t, g_dt_t):
    lo, hi = cfg.OGA, cfg.OGA + cfg.NH
    g = lax.dynamic_update_slice(g_main_t, g_main_t[lo:cfg.NM], (hi, 0))
    return lax.dynamic_update_slice(g, g_dt_t[:cfg.NH], (lo, 0))


def _full_weights(cfg, norm_w, w_in_t, conv_w, conv_b, dt_bias, a_log, d_skip, ssm_norm_w, w_attn, w_ssm, w_out, fnw):
    w_main, w_dt = _main_from_rows(cfg, w_in_t)
    return dict(norm_w=norm_w, w_main_t=w_main.astype(BF16), w_dt_t=w_dt.astype(BF16), conv_w=conv_w, conv_b=conv_b,
                dt_bias=_pad_lanes(dt_bias), a_log=_pad_lanes(a_log), d_skip=_pad_lanes(d_skip), ssm_norm_w=ssm_norm_w,
                final_norm_w=fnw, **{k: v.astype(BF16) for k, v in (("w_attn", w_attn), ("w_ssm", w_ssm), ("w_out", w_out))
                                     if v is not None})


def kernel(x, norm_w, w_in, conv_w, conv_b, dt_bias, a_log, d_skip, ssm_norm_w, w_attn_branch, w_ssm_branch, w_out, final_norm_w, loss_target, m_norm_w, m_w_in, m_conv_w, m_conv_b, m_dt_bias, m_a_log, m_d_skip, m_ssm_norm_w, m_w_attn_branch, m_w_ssm_branch, m_w_out, m_final_norm_w, v_norm_w, v_w_in, v_conv_w, v_conv_b, v_dt_bias, v_a_log, v_d_skip, v_ssm_norm_w, v_w_attn_branch, v_w_ssm_branch, v_w_out, v_final_norm_w):
    cfg = _Cfg(x.shape[1], x.shape[2])
    d, si, cd, nh = cfg.D, cfg.SI, cfg.CD, cfg.NH
    chip = 2 * lax.axis_index("x") + lax.axis_index("y")
    core = lax.axis_index("c").astype(jnp.int32).reshape(1)
    chip = chip.astype(jnp.int32)
    chip_core = [chip.reshape(1), core] + [jnp.where(chip == j, (j + 1) % N_CHIPS, j).astype(jnp.int32).reshape(1)
                                           for j in range(N_CHIPS)]

    own = [jnp.transpose(w_in[0]).astype(BF16), conv_w[0].reshape(4 * CONV_K, -1)]
    hn, gathered = _rmsnorm_fwd(x[0], norm_w, carry=_gather_carry(own, by_cols=(0,)))
    a_in, a_cw = [_with_own(g, o, chip) for g, o in zip(gathered, own)]
    conv_w_full = a_cw.reshape(N_CHIPS, CONV_K, cd // N_CHIPS).transpose(1, 0, 2).reshape(CONV_K, cd)
    w = _full_weights(cfg, norm_w, a_in.reshape(cfg.N_IN, d), conv_w_full, conv_b, dt_bias, a_log, d_skip,
                      ssm_norm_w, None, None, None, final_norm_w.reshape(1, d))
    own_late = [w_attn_branch[0].astype(BF16), w_ssm_branch[0].astype(BF16), w_out[0].astype(BF16)]

    def late_weights(arrived):
        a_attn, a_ssm, a_out = [_with_own(g, o, chip) for g, o in zip(arrived, own_late)]
        return dict(w_attn=a_attn.reshape(d, d), w_ssm=a_ssm.reshape(si, d), w_out=a_out.reshape(d, d))

    def w_in_to_sibling(grads):
        g_in_t = _rows_from_main(cfg, grads["w_main_t"], grads["w_dt_t"]).reshape(N_CHIPS, cfg.N_IN // N_CHIPS, d)
        return g_in_t, _col_halves_carry(g_in_t)

    def to_chips(grads, g_in_t, from_sibling_in):
        by_chip = [grads["w_attn"].reshape(N_CHIPS, d // N_CHIPS, d),
                   grads["w_ssm"].reshape(N_CHIPS, si // N_CHIPS, d),
                   grads["w_out"].reshape(N_CHIPS, d // N_CHIPS, d)]
        from_sibling = _exchange_halves(by_chip)
        return ([_add_sibling_cols(g_in_t, from_sibling_in, core)]
                + [_add_sibling(g, r, core) for g, r in zip(by_chip, from_sibling)])

    loss_row, grad_x, grads, chip_sums, from_chips = _local_step(
        cfg, x[0], loss_target[0], w, (w_in_to_sibling, to_chips), (_gather_carry(own_late), late_weights), hn)
    g_in_t = _share_col_halves(_add_chips_cols(chip_sums[0], from_chips[0], chip_core))
    halves = [_add_chips(o, p, chip_core) for o, p in zip(chip_sums[1:], from_chips[1:])]
    g_attn, g_ssm, g_out = [h.reshape(2 * h.shape[1], h.shape[2]) for h in _share_halves(halves)]
    g_in = jnp.transpose(g_in_t)

    small = [loss_row, grads["norm_w"], grads["conv_b"], grads["dt_bias"], grads["a_log"], grads["d_skip"],
             grads["ssm_norm_w"], grads["final_norm_w"], grads["conv_w"].reshape(1, CONV_K * cd)]
    sizes = [a.shape[1] for a in small]
    total = sum(sizes)
    rows = -(-total // (8 * LANES)) * 8
    flat = jnp.pad(jnp.concatenate(small, axis=1), ((0, 0), (0, rows * LANES - total)))
    red = _allreduce_small(flat.reshape(rows, LANES)).reshape(1, rows * LANES)
    offs = [sum(sizes[:i]) for i in range(len(sizes))]
    loss_r, g_nw, g_cb, g_dtb, g_alog, g_dsk, g_snw, g_fnw, g_cw_flat = [
        red[:, o:o + n] for o, n in zip(offs, sizes)]
    loss = loss_r[0, 0]
    g_dtb, g_alog, g_dsk = g_dtb[:, :nh], g_alog[:, :nh], g_dsk[:, :nh]
    cshard = cd // N_CHIPS
    g_cw = lax.dynamic_slice_in_dim(g_cw_flat.reshape(CONV_K, cd), chip * cshard, cshard, axis=1)

    upd = {}
    upd["w_in"] = tuple(jnp.transpose(u) for u in _adamw(
        jnp.transpose(w_in[0]), g_in_t, jnp.transpose(m_w_in[0]), jnp.transpose(v_w_in[0]), "adamw_w_in"))
    for name, wv, gv, mv, vv in [("w_attn", w_attn_branch[0], g_attn, m_w_attn_branch[0], v_w_attn_branch[0]),
                                 ("w_ssm", w_ssm_branch[0], g_ssm, m_w_ssm_branch[0], v_w_ssm_branch[0]),
                                 ("w_out", w_out[0], g_out, m_w_out[0], v_w_out[0])]:
        upd[name] = _adamw(wv, gv, mv, vv, "adamw_" + name)
    names = ["norm_w", "conv_w", "conv_b", "dt_bias", "a_log", "d_skip", "ssm_norm_w", "final_norm_w"]
    ws = [norm_w, conv_w[0].reshape(1, -1), conv_b, dt_bias, a_log, d_skip, ssm_norm_w, final_norm_w.reshape(1, d)]
    gs = [g_nw, g_cw.reshape(1, -1), g_cb, g_dtb, g_alog, g_dsk, g_snw, g_fnw]
    ms = [m_norm_w, m_conv_w[0].reshape(1, -1), m_conv_b, m_dt_bias, m_a_log, m_d_skip, m_ssm_norm_w,
          m_final_norm_w.reshape(1, d)]
    vs = [v_norm_w, v_conv_w[0].reshape(1, -1), v_conv_b, v_dt_bias, v_a_log, v_d_skip, v_ssm_norm_w,
          v_final_norm_w.reshape(1, d)]
    ssz = [a.shape[1] for a in ws]
    stot = sum(ssz)
    srows = -(-stot // (8 * LANES)) * 8

    def pack(parts):
        return jnp.pad(jnp.concatenate(parts, axis=1), ((0, 0), (0, srows * LANES - stot))).reshape(srows, LANES)

    packed = _adamw(pack(ws), pack(gs), pack(ms), pack(vs), "adamw_small")
    soffs = [sum(ssz[:i]) for i in range(len(ssz))]
    for k, nm in enumerate(names):
        upd[nm] = tuple(p.reshape(1, srows * LANES)[:, soffs[k]:soffs[k] + ssz[k]] for p in packed)

    shapes = dict(norm_w=norm_w.shape, w_in=w_in.shape, conv_w=conv_w.shape, conv_b=conv_b.shape, dt_bias=dt_bias.shape,
                  a_log=a_log.shape, d_skip=d_skip.shape, ssm_norm_w=ssm_norm_w.shape, w_attn=w_attn_branch.shape,
                  w_ssm=w_ssm_branch.shape, w_out=w_out.shape, final_norm_w=final_norm_w.shape)
    order = ["norm_w", "w_in", "conv_w", "conv_b", "dt_bias", "a_log", "d_skip", "ssm_norm_w", "w_attn", "w_ssm",
             "w_out", "final_norm_w"]
    gradv = dict(norm_w=g_nw, w_in=g_in, conv_w=g_cw, conv_b=g_cb, dt_bias=g_dtb, a_log=g_alog, d_skip=g_dsk,
                 ssm_norm_w=g_snw, w_attn=g_attn, w_ssm=g_ssm, w_out=g_out, final_norm_w=g_fnw)
    outs = [loss, grad_x[None]]
    outs += [gradv[n].reshape(shapes[n]) for n in order]
    for k in range(3):
        outs += [upd[n][k].reshape(shapes[n]) for n in order]
    return tuple(outs)
```
